```python
import jax, jax.numpy as jnp
from jax import lax
import numpy as np

D_MODEL = 2048
BATCH = 8
SEQ = 2048
DEPTH = 2

PLE_DIM = 256
MIX_WIDTH = D_MODEL
DN_WIDTH = MIX_WIDTH // 2
HG_WIDTH = MIX_WIDTH - DN_WIDTH
DN_HEAD_DIM = 128
DN_HEADS = DN_WIDTH // DN_HEAD_DIM
HG_HEAD_DIM = 128
HG_HEADS = HG_WIDTH // HG_HEAD_DIM
CONV_WIDTH = 4
DN_CHUNK = 64
HG_CHUNK = 64
NORM_EPS = 1e-6
L2_EPS = 1e-6
SPLITS = (3 * DN_WIDTH, DN_WIDTH, DN_HEADS, DN_HEADS, HG_WIDTH, HG_WIDTH, HG_WIDTH, HG_WIDTH)
IN_WIDTH = sum(SPLITS)

kernel_name = "hybrid_gdn_hgrn2_parallel_heads"


def rms_norm(x, w):
    xf = x.astype(jnp.float32)
    y = xf * lax.rsqrt(jnp.mean(xf * xf, axis=-1, keepdims=True) + NORM_EPS)
    return (y * w.astype(jnp.float32)).astype(x.dtype)


def l2_norm(x):
    return x * lax.rsqrt(jnp.sum(x * x, axis=-1, keepdims=True) + L2_EPS)


def masked_exp(mask, diff):
    return jnp.where(mask, jnp.exp(jnp.where(mask, diff, 0.0)), 0.0)


def to_heads(t, n_heads):
    B, S, W = t.shape
    return t.reshape(B, S, n_heads, W // n_heads).transpose(0, 2, 1, 3)


def gated_head_norm(o, z, w):
    B, H, S, d = o.shape
    o = o.transpose(0, 2, 1, 3)
    o = o * lax.rsqrt(jnp.mean(o * o, axis=-1, keepdims=True) + NORM_EPS) * w.astype(jnp.float32)
    o = o * jax.nn.silu(z.astype(jnp.float32).reshape(B, S, H, d))
    return o.reshape(B, S, H * d)


def causal_conv(x, w):
    C = x.shape[-1]
    return lax.conv_general_dilated(
        x, w[:, None, :].astype(x.dtype), window_strides=(1,),
        padding=[(CONV_WIDTH - 1, 0)], dimension_numbers=("NWC", "WIO", "NWC"),
        feature_group_count=C)


def chunk_gated_delta_rule(q, k, v, g, beta):
    B, H, S, dk = q.shape
    dv = v.shape[-1]
    C = DN_CHUNK
    N = S // C
    q, k, v = (t.reshape(B, H, N, C, t.shape[-1]) for t in (q, k, v))
    g = g.reshape(B, H, N, C)
    beta = beta.reshape(B, H, N, C)
    G = jnp.cumsum(g, axis=-1)
    causal = jnp.tril(jnp.ones((C, C), bool))
    strict = jnp.tril(jnp.ones((C, C), bool), -1)
    decay = masked_exp(causal, G[..., :, None] - G[..., None, :])
    k_beta = k * beta[..., None]
    A = jnp.where(strict, jnp.einsum("bhncd,bhnsd->bhncs", k_beta, k) * decay, 0.0)
    eye = jnp.eye(C, dtype=A.dtype)
    rhs = jnp.concatenate([v * beta[..., None], k_beta * jnp.exp(G)[..., None]], axis=-1)
    sol = lax.linalg.triangular_solve(A + eye, rhs, left_side=True, lower=True,
                                      unit_diagonal=True)
    u, w = sol[..., :dv], sol[..., dv:]
    qk = jnp.einsum("bhncd,bhnsd->bhncs", q, k) * decay
    q_decay = q * jnp.exp(G)[..., None]
    k_tail = k * jnp.exp(G[..., -1:] - G)[..., None]
    tail = jnp.exp(G[..., -1])

    def step(state, xs):
        u_c, w_c, qk_c, qd_c, kt_c, tail_c = xs
        v_new = u_c - jnp.einsum("bhcd,bhde->bhce", w_c, state)
        o = (jnp.einsum("bhcd,bhde->bhce", qd_c, state)
             + jnp.einsum("bhcs,bhse->bhce", qk_c, v_new))
        state = tail_c[..., None, None] * state + jnp.einsum("bhcd,bhce->bhde", kt_c, v_new)
        return state, o

    xs = tuple(jnp.moveaxis(t, 2, 0) for t in (u, w, qk, q_decay, k_tail, tail))
    _, o = lax.scan(step, jnp.zeros((B, H, dk, dv), q.dtype), xs)
    return jnp.moveaxis(o, 0, 2).reshape(B, H, S, dv)


def chunk_hgrn2(q, k, v, log_f):
    B, H, S, dk = q.shape
    dv = v.shape[-1]
    C = HG_CHUNK
    N = S // C
    q, k, v, log_f = (t.reshape(B, H, N, C, t.shape[-1]) for t in (q, k, v, log_f))
    G = jnp.cumsum(log_f, axis=3)
    q_decay = q * jnp.exp(G)
    k_tail = k * jnp.exp(G[:, :, :, -1:] - G)
    tail = jnp.exp(G[:, :, :, -1])
    causal = jnp.tril(jnp.ones((C, C), bool))[:, :, None]

    def step(state, xs):
        q_c, k_c, v_c, G_c, qd_c, kt_c, tail_c = xs
        rel = masked_exp(causal, G_c[:, :, :, None, :] - G_c[:, :, None, :, :])
        A = jnp.einsum("bhrd,bhsd,bhrsd->bhrs", q_c, k_c, rel)
        o = (jnp.einsum("bhrd,bhde->bhre", qd_c, state)
             + jnp.einsum("bhrs,bhse->bhre", A, v_c))
        state = tail_c[..., None] * state + jnp.einsum("bhsd,bhse->bhde", kt_c, v_c)
        return state, o

    xs = tuple(jnp.moveaxis(t, 2, 0) for t in (q, k, v, G, q_decay, k_tail, tail))
    _, o = lax.scan(step, jnp.zeros((B, H, dk, dv), q.dtype), xs)
    return jnp.moveaxis(o, 0, 2).reshape(B, H, S, dv)


def deltanet_branch(qkv, z, b, a, conv_w, A_log, dt_bias, norm_w):
    f32 = jnp.float32
    qkv = jax.nn.silu(causal_conv(qkv.astype(f32), conv_w.astype(f32)))
    q, k, v = jnp.split(qkv, 3, axis=-1)
    q = l2_norm(to_heads(q, DN_HEADS)) * (DN_HEAD_DIM ** -0.5)
    k = l2_norm(to_heads(k, DN_HEADS))
    v = to_heads(v, DN_HEADS)
    beta = jax.nn.sigmoid(b.astype(f32)).transpose(0, 2, 1)
    g = -(jnp.exp(A_log.astype(f32))
          * jax.nn.softplus(a.astype(f32) + dt_bias.astype(f32))).transpose(0, 2, 1)
    o = chunk_gated_delta_rule(q, k, v, g, beta)
    return gated_head_norm(o, z, norm_w)


def hgrn2_branch(q, f, i, z, lb, norm_w):
    f32 = jnp.float32
    q = to_heads(jax.nn.silu(q.astype(f32)), HG_HEADS)
    fh = to_heads(f.astype(f32), HG_HEADS)
    lbh = lb.astype(f32).reshape(HG_HEADS, 1, HG_HEAD_DIM)
    log_f = jnp.log(lbh + (1.0 - lbh) * jax.nn.sigmoid(fh))
    k = (1.0 - lbh) * jax.nn.sigmoid(-fh)
    v = to_heads(i.astype(f32), HG_HEADS)
    o = chunk_hgrn2(q, k, v, log_f)
    return gated_head_norm(o, z, norm_w)


def _fwd_setup_inputs(seed: int = 0) -> dict:
    key = jax.random.key(seed)
    ks = jax.random.split(key, 16)
    f32 = jnp.float32
    x = jax.random.normal(ks[0], (BATCH, SEQ, D_MODEL), f32)
    p = jax.random.normal(ks[1], (DEPTH, BATCH, SEQ, PLE_DIM), f32)
    norm_w = 1.0 + 0.02 * jax.random.normal(ks[2], (DEPTH, D_MODEL), f32)
    w_in = jax.random.normal(ks[3], (DEPTH, D_MODEL, IN_WIDTH), f32) * D_MODEL ** -0.5
    dn_conv_w = jax.random.normal(ks[4], (DEPTH, CONV_WIDTH, 3 * DN_WIDTH), f32) * CONV_WIDTH ** -0.5
    dn_A_log = jnp.log(jax.random.uniform(ks[5], (DEPTH, DN_HEADS), f32, 1.0, 16.0))
    dt = jnp.exp(jax.random.uniform(ks[6], (DEPTH, DN_HEADS), f32, np.log(1e-3), np.log(1e-1)))
    dn_dt_bias = dt + jnp.log(-jnp.expm1(-dt))
    dn_norm_w = 1.0 + 0.02 * jax.random.normal(ks[7], (DEPTH, DN_HEAD_DIM), f32)
    hg_lb_logits = 0.5 * jax.random.normal(ks[8], (DEPTH, HG_WIDTH), f32)
    hg_norm_w = 1.0 + 0.02 * jax.random.normal(ks[9], (DEPTH, HG_HEAD_DIM), f32)
    w_out = jax.random.normal(ks[10], (DEPTH, MIX_WIDTH, D_MODEL), f32) * MIX_WIDTH ** -0.5
    w_ple_up = jax.random.normal(ks[11], (DEPTH, PLE_DIM, D_MODEL), f32) * PLE_DIM ** -0.5
    w_ple_gate = jax.random.normal(ks[12], (DEPTH, D_MODEL, D_MODEL), f32) * D_MODEL ** -0.5
    final_norm_w = 1.0 + 0.02 * jax.random.normal(ks[13], (D_MODEL,), f32)
    return {"x": x, "p": p, "norm_w": norm_w, "w_in": w_in, "dn_conv_w": dn_conv_w,
            "dn_A_log": dn_A_log, "dn_dt_bias": dn_dt_bias, "dn_norm_w": dn_norm_w,
            "hg_lb_logits": hg_lb_logits, "hg_norm_w": hg_norm_w, "w_out": w_out,
            "w_ple_up": w_ple_up, "w_ple_gate": w_ple_gate, "final_norm_w": final_norm_w}


def _fwd_reference(x, p, norm_w, w_in, dn_conv_w, dn_A_log, dn_dt_bias, dn_norm_w,
              hg_lb_logits, hg_norm_w, w_out, w_ple_up, w_ple_gate, final_norm_w):
    split_points = [int(s) for s in np.cumsum(SPLITS)[:-1]]
    lb_probs = jax.nn.softmax(hg_lb_logits.astype(jnp.float32), axis=0)
    lower_bounds = jnp.cumsum(lb_probs, axis=0) - lb_probs[0]
    h = x
    for l in range(DEPTH):
        hn = rms_norm(h, norm_w[l])
        proj = hn @ w_in[l]
        dn_qkv, dn_z, dn_b, dn_a, hg_q, hg_f, hg_i, hg_z = jnp.split(proj, split_points, axis=-1)
        y_a = deltanet_branch(dn_qkv, dn_z, dn_b, dn_a, dn_conv_w[l], dn_A_log[l],
                              dn_dt_bias[l], dn_norm_w[l])
        y_b = hgrn2_branch(hg_q, hg_f, hg_i, hg_z, lower_bounds[l], hg_norm_w[l])
        y = jnp.concatenate([y_a, y_b], axis=-1).astype(h.dtype)
        h = h + y @ w_out[l]
        gate = jax.nn.sigmoid((h @ w_ple_gate[l]).astype(jnp.float32))
        h = h + ((p[l] @ w_ple_up[l]).astype(jnp.float32) * gate).astype(h.dtype)
    return rms_norm(h, final_norm_w)


import jax as _jax
import jax.numpy as _jnp

TWIN_FORMAT = 'train_step'
FWD_PARAMS = ['x', 'p', 'norm_w', 'w_in', 'dn_conv_w', 'dn_A_log', 'dn_dt_bias', 'dn_norm_w', 'hg_lb_logits', 'hg_norm_w', 'w_out', 'w_ple_up', 'w_ple_gate', 'final_norm_w']
TWIN_WEIGHTS = ['norm_w', 'w_in', 'dn_conv_w', 'dn_A_log', 'dn_dt_bias', 'dn_norm_w', 'hg_lb_logits', 'hg_norm_w', 'w_out', 'w_ple_up', 'w_ple_gate', 'final_norm_w']
TWIN_DIFF_INPUT = 'x'
TWIN_INPUTS = ['x', 'p', 'norm_w', 'w_in', 'dn_conv_w', 'dn_A_log', 'dn_dt_bias', 'dn_norm_w', 'hg_lb_logits', 'hg_norm_w', 'w_out', 'w_ple_up', 'w_ple_gate', 'final_norm_w', 'loss_target', 'm_norm_w', 'm_w_in', 'm_dn_conv_w', 'm_dn_A_log', 'm_dn_dt_bias', 'm_dn_norm_w', 'm_hg_lb_logits', 'm_hg_norm_w', 'm_w_out', 'm_w_ple_up', 'm_w_ple_gate', 'm_final_norm_w', 'v_norm_w', 'v_w_in', 'v_dn_conv_w', 'v_dn_A_log', 'v_dn_dt_bias', 'v_dn_norm_w', 'v_hg_lb_logits', 'v_hg_norm_w', 'v_w_out', 'v_w_ple_up', 'v_w_ple_gate', 'v_final_norm_w']
TWIN_OUTPUTS = ['loss', 'grad_x', 'grad_norm_w', 'grad_w_in', 'grad_dn_conv_w', 'grad_dn_A_log', 'grad_dn_dt_bias', 'grad_dn_norm_w', 'grad_hg_lb_logits', 'grad_hg_norm_w', 'grad_w_out', 'grad_w_ple_up', 'grad_w_ple_gate', 'grad_final_norm_w', 'delta_norm_w', 'delta_w_in', 'delta_dn_conv_w', 'delta_dn_A_log', 'delta_dn_dt_bias', 'delta_dn_norm_w', 'delta_hg_lb_logits', 'delta_hg_norm_w', 'delta_w_out', 'delta_w_ple_up', 'delta_w_ple_gate', 'delta_final_norm_w', 'new_m_norm_w', 'new_m_w_in', 'new_m_dn_conv_w', 'new_m_dn_A_log', 'new_m_dn_dt_bias', 'new_m_dn_norm_w', 'new_m_hg_lb_logits', 'new_m_hg_norm_w', 'new_m_w_out', 'new_m_w_ple_up', 'new_m_w_ple_gate', 'new_m_final_norm_w', 'new_v_norm_w', 'new_v_w_in', 'new_v_dn_conv_w', 'new_v_dn_A_log', 'new_v_dn_dt_bias', 'new_v_dn_norm_w', 'new_v_hg_lb_logits', 'new_v_hg_norm_w', 'new_v_w_out', 'new_v_w_ple_up', 'new_v_w_ple_gate', 'new_v_final_norm_w']
TWIN_LEAF_KINDS = {'loss': 'loss', 'grad_x': 'grad_x', 'grad_norm_w': 'grad_w', 'grad_w_in': 'grad_w', 'grad_dn_conv_w': 'grad_w', 'grad_dn_A_log': 'grad_w', 'grad_dn_dt_bias': 'grad_w', 'grad_dn_norm_w': 'grad_w', 'grad_hg_lb_logits': 'grad_w', 'grad_hg_norm_w': 'grad_w', 'grad_w_out': 'grad_w', 'grad_w_ple_up': 'grad_w', 'grad_w_ple_gate': 'grad_w', 'grad_final_norm_w': 'grad_w', 'delta_norm_w': 'delta_w', 'delta_w_in': 'delta_w', 'delta_dn_conv_w': 'delta_w', 'delta_dn_A_log': 'delta_w', 'delta_dn_dt_bias': 'delta_w', 'delta_dn_norm_w': 'delta_w', 'delta_hg_lb_logits': 'delta_w', 'delta_hg_norm_w': 'delta_w', 'delta_w_out': 'delta_w', 'delta_w_ple_up': 'delta_w', 'delta_w_ple_gate': 'delta_w', 'delta_final_norm_w': 'delta_w', 'new_m_norm_w': 'new_m', 'new_m_w_in': 'new_m', 'new_m_dn_conv_w': 'new_m', 'new_m_dn_A_log': 'new_m', 'new_m_dn_dt_bias': 'new_m', 'new_m_dn_norm_w': 'new_m', 'new_m_hg_lb_logits': 'new_m', 'new_m_hg_norm_w': 'new_m', 'new_m_w_out': 'new_m', 'new_m_w_ple_up': 'new_m', 'new_m_w_ple_gate': 'new_m', 'new_m_final_norm_w': 'new_m', 'new_v_norm_w': 'new_v', 'new_v_w_in': 'new_v', 'new_v_dn_conv_w': 'new_v', 'new_v_dn_A_log': 'new_v', 'new_v_dn_dt_bias': 'new_v', 'new_v_dn_norm_w': 'new_v', 'new_v_hg_lb_logits': 'new_v', 'new_v_hg_norm_w': 'new_v', 'new_v_w_out': 'new_v', 'new_v_w_ple_up': 'new_v', 'new_v_w_ple_gate': 'new_v', 'new_v_final_norm_w': 'new_v'}


def _forward(args):
    return _fwd_reference(*[args[k] for k in FWD_PARAMS])


def _output_shape():
    out = _jax.eval_shape(lambda: _forward(_fwd_setup_inputs(0)))
    return out.shape, out.dtype

N_MICROBATCH = 1
ADAM_LR = 0.001
ADAM_B1 = 0.9
ADAM_B2 = 0.999
ADAM_EPS = 1e-08
ADAM_WD = 0.01
ADAM_STEP = 10
PER_EXAMPLE_BATCH_AXIS = {'x': 0, 'p': 1, 'loss_target': 0}
SHARED_INPUTS = []
_WEIGHT_DTYPES = {'norm_w': _jnp.float32, 'w_in': _jnp.float32, 'dn_conv_w': _jnp.float32, 'dn_A_log': _jnp.float32, 'dn_dt_bias': _jnp.float32, 'dn_norm_w': _jnp.float32, 'hg_lb_logits': _jnp.float32, 'hg_norm_w': _jnp.float32, 'w_out': _jnp.float32, 'w_ple_up': _jnp.float32, 'w_ple_gate': _jnp.float32, 'final_norm_w': _jnp.float32}
MOMENT_SCALE = {'norm_w': 4.500029e-02, 'w_in': 2.203040e-02, 'dn_conv_w': 2.115208e-02, 'dn_A_log': 1.337305e-01, 'dn_dt_bias': 1.289831e-01, 'dn_norm_w': 8.137418e-02, 'hg_lb_logits': 2.208299e-03, 'hg_norm_w': 7.965146e-02, 'w_out': 2.799274e-02, 'w_ple_up': 2.598345e-02, 'w_ple_gate': 1.171733e-02, 'final_norm_w': 8.005306e+00}


def _to_microbatches(a, axis):
    t = _jnp.moveaxis(a, axis, 0)
    t = t.reshape((N_MICROBATCH, t.shape[0] // N_MICROBATCH) + t.shape[1:])
    return _jnp.moveaxis(t, 1, axis + 1)


def setup_inputs(seed: int = 0) -> dict:
    inp = _fwd_setup_inputs(seed)
    key = _jax.random.fold_in(_jax.random.key(seed), 7919)
    shape, _ = _output_shape()
    out = dict(inp)
    out["loss_target"] = _jax.random.normal(_jax.random.fold_in(key, 0), shape, _jnp.float32)
    for i, name in enumerate(TWIN_WEIGHTS):
        w = inp[name].astype(_jnp.float32)
        if MOMENT_SCALE is None:
            s = _jnp.sqrt(_jnp.mean(_jnp.square(w)) + 1e-30)
        else:
            s = MOMENT_SCALE[name]
        km, kv = _jax.random.split(_jax.random.fold_in(key, i + 1))
        out[name] = w
        out["m_" + name] = s * _jax.random.normal(km, w.shape, _jnp.float32)
        out["v_" + name] = (s * s) * _jax.random.uniform(kv, w.shape, _jnp.float32, 0.5, 1.5)
    if N_MICROBATCH > 1:
        for name, axis in PER_EXAMPLE_BATCH_AXIS.items():
            out[name] = _to_microbatches(out[name], axis)
    return {'x': out['x'], 'p': out['p'], 'norm_w': out['norm_w'], 'w_in': out['w_in'], 'dn_conv_w': out['dn_conv_w'], 'dn_A_log': out['dn_A_log'], 'dn_dt_bias': out['dn_dt_bias'], 'dn_norm_w': out['dn_norm_w'], 'hg_lb_logits': out['hg_lb_logits'], 'hg_norm_w': out['hg_norm_w'], 'w_out': out['w_out'], 'w_ple_up': out['w_ple_up'], 'w_ple_gate': out['w_ple_gate'], 'final_norm_w': out['final_norm_w'], 'loss_target': out['loss_target'], 'm_norm_w': out['m_norm_w'], 'm_w_in': out['m_w_in'], 'm_dn_conv_w': out['m_dn_conv_w'], 'm_dn_A_log': out['m_dn_A_log'], 'm_dn_dt_bias': out['m_dn_dt_bias'], 'm_dn_norm_w': out['m_dn_norm_w'], 'm_hg_lb_logits': out['m_hg_lb_logits'], 'm_hg_norm_w': out['m_hg_norm_w'], 'm_w_out': out['m_w_out'], 'm_w_ple_up': out['m_w_ple_up'], 'm_w_ple_gate': out['m_w_ple_gate'], 'm_final_norm_w': out['m_final_norm_w'], 'v_norm_w': out['v_norm_w'], 'v_w_in': out['v_w_in'], 'v_dn_conv_w': out['v_dn_conv_w'], 'v_dn_A_log': out['v_dn_A_log'], 'v_dn_dt_bias': out['v_dn_dt_bias'], 'v_dn_norm_w': out['v_dn_norm_w'], 'v_hg_lb_logits': out['v_hg_lb_logits'], 'v_hg_norm_w': out['v_hg_norm_w'], 'v_w_out': out['v_w_out'], 'v_w_ple_up': out['v_w_ple_up'], 'v_w_ple_gate': out['v_w_ple_gate'], 'v_final_norm_w': out['v_final_norm_w']}


def _loss(weights, diff, rest, loss_target):
    with _jax.named_scope("forward"):
        args = {**rest, TWIN_DIFF_INPUT: diff, **{k: w.astype(_WEIGHT_DTYPES[k]) for k, w in weights.items()}}
        y = _forward(args)
    with _jax.named_scope("loss_head"):
        err = _jnp.square(y.astype(_jnp.float32) - loss_target)
        return 0.5 * _jnp.sum(_jnp.mean(err, axis=-1)) if err.ndim else 0.5 * err


def _adamw(w, g, m, v):
    m = ADAM_B1 * m + (1.0 - ADAM_B1) * g
    v = ADAM_B2 * v + (1.0 - ADAM_B2) * _jnp.square(g)
    m_hat = m / (1.0 - ADAM_B1 ** ADAM_STEP)
    v_hat = v / (1.0 - ADAM_B2 ** ADAM_STEP)
    delta = -ADAM_LR * (m_hat / (_jnp.sqrt(v_hat) + ADAM_EPS) + ADAM_WD * w)
    return delta, m, v


def reference(x, p, norm_w, w_in, dn_conv_w, dn_A_log, dn_dt_bias, dn_norm_w, hg_lb_logits, hg_norm_w, w_out, w_ple_up, w_ple_gate, final_norm_w, loss_target, m_norm_w, m_w_in, m_dn_conv_w, m_dn_A_log, m_dn_dt_bias, m_dn_norm_w, m_hg_lb_logits, m_hg_norm_w, m_w_out, m_w_ple_up, m_w_ple_gate, m_final_norm_w, v_norm_w, v_w_in, v_dn_conv_w, v_dn_A_log, v_dn_dt_bias, v_dn_norm_w, v_hg_lb_logits, v_hg_norm_w, v_w_out, v_w_ple_up, v_w_ple_gate, v_final_norm_w):
    given = dict(x=x, p=p, norm_w=norm_w, w_in=w_in, dn_conv_w=dn_conv_w, dn_A_log=dn_A_log, dn_dt_bias=dn_dt_bias, dn_norm_w=dn_norm_w, hg_lb_logits=hg_lb_logits, hg_norm_w=hg_norm_w, w_out=w_out, w_ple_up=w_ple_up, w_ple_gate=w_ple_gate, final_norm_w=final_norm_w, loss_target=loss_target, m_norm_w=m_norm_w, m_w_in=m_w_in, m_dn_conv_w=m_dn_conv_w, m_dn_A_log=m_dn_A_log, m_dn_dt_bias=m_dn_dt_bias, m_dn_norm_w=m_dn_norm_w, m_hg_lb_logits=m_hg_lb_logits, m_hg_norm_w=m_hg_norm_w, m_w_out=m_w_out, m_w_ple_up=m_w_ple_up, m_w_ple_gate=m_w_ple_gate, m_final_norm_w=m_final_norm_w, v_norm_w=v_norm_w, v_w_in=v_w_in, v_dn_conv_w=v_dn_conv_w, v_dn_A_log=v_dn_A_log, v_dn_dt_bias=v_dn_dt_bias, v_dn_norm_w=v_dn_norm_w, v_hg_lb_logits=v_hg_lb_logits, v_hg_norm_w=v_hg_norm_w, v_w_out=v_w_out, v_w_ple_up=v_w_ple_up, v_w_ple_gate=v_w_ple_gate, v_final_norm_w=v_final_norm_w)
    weights = {n: given[n] for n in TWIN_WEIGHTS}
    shared = {n: given[n] for n in SHARED_INPUTS}
    per_example = {n: given[n] for n in ['x', 'p']}
    grad_fn = _jax.value_and_grad(_loss, argnums=(0, 1))

    def one_microbatch(ex, loss_target):
        ex = dict(ex)
        diff = ex.pop(TWIN_DIFF_INPUT)
        return grad_fn(weights, diff, {**shared, **ex}, loss_target)

    if N_MICROBATCH == 1:
        loss, (grad_w, grad_x) = one_microbatch(per_example, given["loss_target"])
    else:
        def body(carry, xs):
            loss_sum, grad_sum = carry
            l_k, (gw_k, gx_k) = one_microbatch(xs[0], xs[1])
            with _jax.named_scope("update"):
                return (loss_sum + l_k, _jax.tree.map(_jnp.add, grad_sum, gw_k)), gx_k

        init = (_jnp.zeros((), _jnp.float32), _jax.tree.map(_jnp.zeros_like, weights))
        (loss, grad_w), grad_x = _jax.lax.scan(body, init, (per_example, given["loss_target"]))
    with _jax.named_scope("update"):
        delta_w, new_m, new_v = {}, {}, {}
        for n in TWIN_WEIGHTS:
            delta_w[n], new_m[n], new_v[n] = _adamw(weights[n], grad_w[n], given["m_" + n], given["v_" + n])
    return (loss, grad_x, *[grad_w[n] for n in TWIN_WEIGHTS], *[delta_w[n] for n in TWIN_WEIGHTS],
            *[new_m[n] for n in TWIN_WEIGHTS], *[new_v[n] for n in TWIN_WEIGHTS])
```

```python
import functools

import jax
import jax.numpy as jnp
from jax import lax
from jax.experimental import pallas as pl
from jax.experimental.pallas import tpu as pltpu

F32 = jnp.float32
BF16 = jnp.bfloat16
HIGHEST = lax.Precision.HIGHEST

N_DEV = 8
D_MODEL = 2048
PLE_DIM = 256
HEAD_DIM = 128
N_HEADS = 8
BR_WIDTH = N_HEADS * HEAD_DIM
CHUNK = 64
SUB = 16
CONV_W = 4
NORM_EPS = 1e-6
L2_EPS = 1e-6
IN_WIDTH = 8208
SHARD_IN = IN_WIDTH // N_DEV
EXP_CLAMP = 80.0

C_QKV, C_Z, C_HQ, C_HF, C_HI, C_HZ, C_B, C_A, N_PROJ = 0, 3072, 4096, 5120, 6144, 7168, 8192, 8320, 8448

ADAM_LR, ADAM_B1, ADAM_B2, ADAM_EPS, ADAM_WD, ADAM_STEP = 0.001, 0.9, 0.999, 1e-08, 0.01, 10

VMEM_LIMIT = 48 * 1024 * 1024


def _cp(*sem):
    return pltpu.CompilerParams(dimension_semantics=sem, vmem_limit_bytes=VMEM_LIMIT)


def _dot(a, b, ca, cb):
    return lax.dot_general(a.astype(BF16), b.astype(BF16), (((ca,), (cb,)), ((), ())),
                           preferred_element_type=F32)


def _nn(a, b):
    return _dot(a, b, 1, 0)


def _nt(a, b):
    return _dot(a, b, 1, 1)


def _tn(a, b):
    return _dot(a, b, 0, 0)


def _split(a):
    hi = a.astype(BF16)
    return hi, (a - hi.astype(F32)).astype(BF16)


def _dot3(a, b, ca, cb):
    ah, al = _split(a)
    bh, bl = _split(b)
    return _dot(ah, bh, ca, cb) + (_dot(ah, bl, ca, cb) + _dot(al, bh, ca, cb))


def _nn_exact(a, b):
    return lax.dot_general(a, b, (((1,), (0,)), ((), ())), precision=HIGHEST, preferred_element_type=F32)


def _sigmoid(x):
    return jax.nn.sigmoid(x)


def _silu(x):
    return x * _sigmoid(x)


def _dsilu(x):
    s = _sigmoid(x)
    return s * (1.0 + x * (1.0 - s))


def _softplus(x):
    return jnp.maximum(x, 0.0) + jnp.log(1.0 + jnp.exp(-jnp.abs(x)))


def _iota2(n, m, axis):
    return lax.broadcasted_iota(jnp.int32, (n, m), axis)


def _col2row(col, eye):
    return jnp.sum(eye * col, axis=0, keepdims=True)


def _row2col(row, eye):
    return jnp.sum(eye * row, axis=1, keepdims=True)


def _pick_lane(block, lane_idx):
    lane = _iota2(block.shape[0], block.shape[1], 1)
    return jnp.sum(jnp.where(lane == lane_idx, block, 0.0), axis=1, keepdims=True)


def _mm(a, b, *, mode, out_dtype, res=None, tm=512, tn=512, tk=512, name):
    if mode == "nn":
        (m, kd), (_, n) = a.shape, b.shape
    elif mode == "nt":
        (m, kd), (n, _) = a.shape, b.shape
    else:
        (kd, m), (_, n) = a.shape, b.shape
    tm, tn, tk = min(tm, m), min(tn, n), min(tk, kd)
    assert m % tm == 0 and n % tn == 0 and kd % tk == 0, (m, n, kd, tm, tn, tk)
    nk = kd // tk
    ca, cb = {"nn": (1, 0), "nt": (1, 1), "tn": (0, 0)}[mode]

    def body(*refs):
        if res is None:
            a_ref, b_ref, o_ref, acc_ref = refs
            r_ref = None
        else:
            a_ref, b_ref, r_ref, o_ref, acc_ref = refs
        k = pl.program_id(2)

        @pl.when(k == 0)
        def _():
            acc_ref[...] = jnp.zeros_like(acc_ref)

        acc_ref[...] += _dot(a_ref[...], b_ref[...], ca, cb)

        @pl.when(k == nk - 1)
        def _():
            out = acc_ref[...]
            if r_ref is not None:
                out = out + r_ref[...].astype(F32)
            o_ref[...] = out.astype(o_ref.dtype)

    a_spec = pl.BlockSpec((tk, tm), lambda i, j, k: (k, i)) if mode == "tn" else pl.BlockSpec((tm, tk), lambda i, j, k: (i, k))
    b_spec = pl.BlockSpec((tn, tk), lambda i, j, k: (j, k)) if mode == "nt" else pl.BlockSpec((tk, tn), lambda i, j, k: (k, j))
    o_spec = pl.BlockSpec((tm, tn), lambda i, j, k: (i, j))
    in_specs = [a_spec, b_spec] + ([o_spec] if res is not None else [])
    args = (a, b) + ((res,) if res is not None else ())
    return pl.pallas_call(
        body, name=name, grid=(m // tm, n // tn, nk), in_specs=in_specs, out_specs=o_spec,
        out_shape=jax.ShapeDtypeStruct((m, n), out_dtype),
        scratch_shapes=[pltpu.VMEM((tm, tn), F32)],
        compiler_params=_cp("parallel", "parallel", "arbitrary"),
    )(*args)


ROW_TILE = 256


def _rms_fwd(h, w, *, name):
    s, d = h.shape
    tr = min(ROW_TILE, s)

    def body(h_ref, w_ref, o_ref):
        x = h_ref[...]
        r = lax.rsqrt(jnp.mean(x * x, axis=-1, keepdims=True) + NORM_EPS)
        o_ref[...] = (x * r * w_ref[...]).astype(o_ref.dtype)

    return pl.pallas_call(
        body, name=name, grid=(s // tr,),
        in_specs=[pl.BlockSpec((tr, d), lambda i: (i, 0)), pl.BlockSpec((1, d), lambda i: (0, 0))],
        out_specs=pl.BlockSpec((tr, d), lambda i: (i, 0)),
        out_shape=jax.ShapeDtypeStruct((s, d), BF16), compiler_params=_cp("parallel"),
    )(h, w.reshape(1, d))


def _rms_bwd_math(x, w, dy):
    d = x.shape[-1]
    r = lax.rsqrt(jnp.mean(x * x, axis=-1, keepdims=True) + NORM_EPS)
    gw = dy * w
    dx = r * gw - x * ((r * r * r) * (jnp.sum(gw * x, axis=-1, keepdims=True) / d))
    return dx, dy * x * r


def _rms_bwd(h, w, dhn, res, *, name):
    s, d = h.shape
    tr = min(ROW_TILE, s)

    def body(h_ref, w_ref, g_ref, r_ref, dh_ref, dw_ref):
        @pl.when(pl.program_id(0) == 0)
        def _():
            dw_ref[...] = jnp.zeros_like(dw_ref)

        dx, dwt = _rms_bwd_math(h_ref[...], w_ref[...], g_ref[...])
        dh_ref[...] = r_ref[...] + dx
        dw_ref[...] += jnp.sum(dwt, axis=0, keepdims=True)

    row = pl.BlockSpec((tr, d), lambda i: (i, 0))
    vec = pl.BlockSpec((1, d), lambda i: (0, 0))
    return pl.pallas_call(
        body, name=name, grid=(s // tr,), in_specs=[row, vec, row, row], out_specs=[row, vec],
        out_shape=[jax.ShapeDtypeStruct((s, d), F32), jax.ShapeDtypeStruct((1, d), F32)],
        compiler_params=_cp("arbitrary"),
    )(h, w.reshape(1, d), dhn, res)


def _final_fwd_bwd(h, w, tgt, *, name):
    s, d = h.shape
    tr = min(ROW_TILE, s)

    def body(h_ref, w_ref, t_ref, loss_ref, dh_ref, dw_ref):
        @pl.when(pl.program_id(0) == 0)
        def _():
            loss_ref[...] = jnp.zeros_like(loss_ref)
            dw_ref[...] = jnp.zeros_like(dw_ref)

        x = h_ref[...]
        wv = w_ref[...]
        r = lax.rsqrt(jnp.mean(x * x, axis=-1, keepdims=True) + NORM_EPS)
        err = x * r * wv - t_ref[...]
        row_loss = jnp.mean(err * err, axis=-1, keepdims=True)
        loss_ref[...] += 0.5 * jnp.sum(row_loss, axis=0, keepdims=True)
        dx, dwt = _rms_bwd_math(x, wv, err / d)
        dh_ref[...] = dx
        dw_ref[...] += jnp.sum(dwt, axis=0, keepdims=True)

    row = pl.BlockSpec((tr, d), lambda i: (i, 0))
    vec = pl.BlockSpec((1, d), lambda i: (0, 0))
    return pl.pallas_call(
        body, name=name, grid=(s // tr,), in_specs=[row, vec, row],
        out_specs=[pl.BlockSpec((1, 128), lambda i: (0, 0)), row, vec],
        out_shape=[jax.ShapeDtypeStruct((1, 128), F32), jax.ShapeDtypeStruct((s, d), F32),
                   jax.ShapeDtypeStruct((1, d), F32)],
        compiler_params=_cp("arbitrary"),
    )(h, w.reshape(1, d), tgt)


def _ple_fwd(h1, gate_pre, up, *, name):
    s, d = h1.shape
    tr = min(ROW_TILE, s)

    def body(h_ref, g_ref, u_ref, o_ref):
        o_ref[...] = h_ref[...] + u_ref[...] * _sigmoid(g_ref[...])

    row = pl.BlockSpec((tr, d), lambda i: (i, 0))
    return pl.pallas_call(body, name=name, grid=(s // tr,), in_specs=[row, row, row], out_specs=row,
                          out_shape=jax.ShapeDtypeStruct((s, d), F32), compiler_params=_cp("parallel"))(h1, gate_pre, up)


def _ple_bwd(dh2, gate_pre, up, *, name):
    s, d = dh2.shape
    tr = min(ROW_TILE, s)

    def body(d_ref, g_ref, u_ref, dup_ref, dgp_ref):
        dh = d_ref[...]
        gate = _sigmoid(g_ref[...])
        dup_ref[...] = (dh * gate).astype(BF16)
        dgp_ref[...] = (dh * u_ref[...] * gate * (1.0 - gate)).astype(BF16)

    row = pl.BlockSpec((tr, d), lambda i: (i, 0))
    return pl.pallas_call(body, name=name, grid=(s // tr,), in_specs=[row, row, row], out_specs=[row, row],
                          out_shape=[jax.ShapeDtypeStruct((s, d), BF16)] * 2, compiler_params=_cp("parallel"))(dh2, gate_pre, up)


HN_TILE = 512


def _hnorm_fwd(o, proj, z_col, w, *, name):
    s = o.shape[0]
    tr = min(HN_TILE, s)
    zb = z_col // HEAD_DIM

    def body(o_ref, z_ref, w_ref, y_ref):
        x = o_ref[...]
        r = lax.rsqrt(jnp.mean(x * x, axis=-1, keepdims=True) + NORM_EPS)
        y_ref[...] = (x * r * w_ref[...] * _silu(z_ref[...])).astype(BF16)

    blk = pl.BlockSpec((tr, HEAD_DIM), lambda i, h: (i, h))
    return pl.pallas_call(
        body, name=name, grid=(s // tr, N_HEADS),
        in_specs=[blk, pl.BlockSpec((tr, HEAD_DIM), lambda i, h: (i, zb + h)), pl.BlockSpec((1, HEAD_DIM), lambda i, h: (0, 0))],
        out_specs=blk, out_shape=jax.ShapeDtypeStruct((s, BR_WIDTH), BF16), compiler_params=_cp("parallel", "parallel"),
    )(o, proj, w.reshape(1, HEAD_DIM))


def _hnorm_bwd(o, proj, z_col, w, dy, dy_col, *, name):
    s = o.shape[0]
    tr = min(HN_TILE, s)
    zb, yb = z_col // HEAD_DIM, dy_col // HEAD_DIM

    def body(o_ref, z_ref, w_ref, dy_ref, do_ref, dz_ref, dw_ref):
        @pl.when((pl.program_id(0) == 0) & (pl.program_id(1) == 0))
        def _():
            dw_ref[...] = jnp.zeros_like(dw_ref)

        x, z, wv, g = o_ref[...], z_ref[...], w_ref[...], dy_ref[...]
        r = lax.rsqrt(jnp.mean(x * x, axis=-1, keepdims=True) + NORM_EPS)
        on = x * r * wv
        don = g * _silu(z)
        dz_ref[...] = (g * on * _dsilu(z)).astype(BF16)
        gw = don * wv
        do_ref[...] = r * gw - x * ((r * r * r) * (jnp.sum(gw * x, axis=-1, keepdims=True) / HEAD_DIM))
        dw_ref[...] += jnp.sum(don * x * r, axis=0, keepdims=True)

    blk = pl.BlockSpec((tr, HEAD_DIM), lambda i, h: (i, h))
    vec = pl.BlockSpec((1, HEAD_DIM), lambda i, h: (0, 0))
    return pl.pallas_call(
        body, name=name, grid=(s // tr, N_HEADS),
        in_specs=[blk, pl.BlockSpec((tr, HEAD_DIM), lambda i, h: (i, zb + h)), vec,
                  pl.BlockSpec((tr, HEAD_DIM), lambda i, h: (i, yb + h))],
        out_specs=[blk, blk, vec],
        out_shape=[jax.ShapeDtypeStruct((s, BR_WIDTH), F32), jax.ShapeDtypeStruct((s, BR_WIDTH), BF16),
                   jax.ShapeDtypeStruct((1, HEAD_DIM), F32)],
        compiler_params=_cp("arbitrary", "arbitrary"),
    )(o, proj, w.reshape(1, HEAD_DIM), dy)


def _conv_silu(x, w, s):
    row = _iota2(s, x.shape[1], 0)
    c = w[CONV_W - 1:CONV_W, :] * x
    for k in range(1, CONV_W):
        c = c + w[CONV_W - 1 - k:CONV_W - k, :] * jnp.where(row >= k, pltpu.roll(x, k, 0), 0.0)
    return c


def _dn_qkv_fwd(proj, conv_w, *, name):
    s = proj.shape[0]
    nb = 3 * N_HEADS

    def body(x_ref, w_ref, o_ref):
        j = pl.program_id(0)
        sv = _silu(_conv_silu(x_ref[...], w_ref[...], s))
        r = lax.rsqrt(jnp.sum(sv * sv, axis=-1, keepdims=True) + L2_EPS)
        scale = jnp.where(j < N_HEADS, HEAD_DIM ** -0.5, 1.0).astype(F32)
        o_ref[...] = jnp.where(j < 2 * N_HEADS, sv * r * scale, sv)

    return pl.pallas_call(
        body, name=name, grid=(nb,),
        in_specs=[pl.BlockSpec((s, HEAD_DIM), lambda j: (0, j)), pl.BlockSpec((CONV_W, HEAD_DIM), lambda j: (0, j))],
        out_specs=pl.BlockSpec((s, HEAD_DIM), lambda j: (0, j)),
        out_shape=jax.ShapeDtypeStruct((s, 3 * BR_WIDTH), F32), compiler_params=_cp("parallel"),
    )(proj, conv_w)


def _dn_qkv_bwd(proj, conv_w, dqkv, *, name):
    s = proj.shape[0]
    nb = 3 * N_HEADS

    def body(x_ref, w_ref, g_ref, dx_ref, dw_ref):
        j = pl.program_id(0)
        x, w, g = x_ref[...], w_ref[...], g_ref[...]
        c = _conv_silu(x, w, s)
        sv = _silu(c)
        r = lax.rsqrt(jnp.sum(sv * sv, axis=-1, keepdims=True) + L2_EPS)
        scale = jnp.where(j < N_HEADS, HEAD_DIM ** -0.5, 1.0).astype(F32)
        ds_n = scale * (r * g - sv * ((r * r * r) * jnp.sum(g * sv, axis=-1, keepdims=True)))
        dc = jnp.where(j < 2 * N_HEADS, ds_n, g) * _dsilu(c)
        row = _iota2(s, HEAD_DIM, 0)
        dx = w[CONV_W - 1:CONV_W, :] * dc
        dws = [jnp.sum(dc * x, axis=0, keepdims=True)]
        for k in range(1, CONV_W):
            dx = dx + w[CONV_W - 1 - k:CONV_W - k, :] * jnp.where(row < s - k, pltpu.roll(dc, s - k, 0), 0.0)
            dws.append(jnp.sum(dc * jnp.where(row >= k, pltpu.roll(x, k, 0), 0.0), axis=0, keepdims=True))
        dx_ref[...] = dx.astype(BF16)
        for k in range(CONV_W):
            dw_ref[CONV_W - 1 - k:CONV_W - k, :] = dws[k]

    blk = pl.BlockSpec((s, HEAD_DIM), lambda j: (0, j))
    wblk = pl.BlockSpec((CONV_W, HEAD_DIM), lambda j: (0, j))
    return pl.pallas_call(
        body, name=name, grid=(nb,), in_specs=[blk, wblk, blk], out_specs=[blk, wblk],
        out_shape=[jax.ShapeDtypeStruct((s, 3 * BR_WIDTH), BF16), jax.ShapeDtypeStruct((CONV_W, 3 * BR_WIDTH), F32)],
        compiler_params=_cp("parallel"),
    )(proj, conv_w, dqkv)


def _tri(n, kind):
    r, c = _iota2(n, n, 0), _iota2(n, n, 1)
    if kind == "lower":
        return (r >= c).astype(F32)
    if kind == "upper":
        return (r <= c).astype(F32)
    return (r == c).astype(F32)


def _dn_gate_fwd(proj, a_log, dt_bias, *, name):
    s = proj.shape[0]

    def body(b_ref, a_ref, al_ref, dt_ref, beta_ref, g_ref):
        beta_ref[...] = _sigmoid(b_ref[...])
        g = -jnp.exp(al_ref[...]) * _softplus(a_ref[...] + dt_ref[...])
        g_ref[...] = _nn_exact(_tri(CHUNK, "lower"), g)

    blk = lambda cb: pl.BlockSpec((CHUNK, HEAD_DIM), lambda i: (i, cb))
    vec = pl.BlockSpec((1, HEAD_DIM), lambda i: (0, 0))
    out = pl.BlockSpec((CHUNK, HEAD_DIM), lambda i: (i, 0))
    return pl.pallas_call(
        body, name=name, grid=(s // CHUNK,), in_specs=[blk(C_B // HEAD_DIM), blk(C_A // HEAD_DIM), vec, vec],
        out_specs=[out, out], out_shape=[jax.ShapeDtypeStruct((s, HEAD_DIM), F32)] * 2, compiler_params=_cp("parallel"),
    )(proj, proj, a_log, dt_bias)


def _dn_gate_bwd(proj, a_log, dt_bias, dbeta, d_g, *, name):
    s = proj.shape[0]

    def body(b_ref, a_ref, al_ref, dt_ref, dbeta_ref, dG_ref, db_ref, da_ref, dal_ref, ddt_ref):
        @pl.when(pl.program_id(0) == 0)
        def _():
            dal_ref[...] = jnp.zeros_like(dal_ref)
            ddt_ref[...] = jnp.zeros_like(ddt_ref)

        beta = _sigmoid(b_ref[...])
        db_ref[...] = (dbeta_ref[...] * beta * (1.0 - beta)).astype(BF16)
        pre = a_ref[...] + dt_ref[...]
        neg_ea = -jnp.exp(al_ref[...])
        dg = _nn_exact(_tri(CHUNK, "upper"), dG_ref[...])
        da = dg * neg_ea * _sigmoid(pre)
        da_ref[...] = da.astype(BF16)
        ddt_ref[...] += jnp.sum(da, axis=0, keepdims=True)
        dal_ref[...] += jnp.sum(dg * neg_ea * _softplus(pre), axis=0, keepdims=True)

    blk = lambda cb: pl.BlockSpec((CHUNK, HEAD_DIM), lambda i: (i, cb))
    vec = pl.BlockSpec((1, HEAD_DIM), lambda i: (0, 0))
    io = pl.BlockSpec((CHUNK, HEAD_DIM), lambda i: (i, 0))
    return pl.pallas_call(
        body, name=name, grid=(s // CHUNK,),
        in_specs=[blk(C_B // HEAD_DIM), blk(C_A // HEAD_DIM), vec, vec, io, io], out_specs=[io, io, vec, vec],
        out_shape=[jax.ShapeDtypeStruct((s, HEAD_DIM), BF16)] * 2 + [jax.ShapeDtypeStruct((1, HEAD_DIM), F32)] * 2,
        compiler_params=_cp("arbitrary"),
    )(proj, proj, a_log, dt_bias, dbeta, d_g)


def _unit_lower_inverse(a_strict, eye):
    x = -a_strict
    t = eye + x
    p = x
    n = 2
    while n < CHUNK:
        p = _nn(p, p)
        t = t + _nn(t, p)
        n *= 2
    return t


def _dn_chunk_common(q, k, v, gc, beta, st):
    c = CHUNK
    eye = _tri(c, "eye")
    low = _tri(c, "lower")
    strict = low - eye
    grow = _col2row(gc, eye)
    dec = low * jnp.exp(low * (gc - grow))
    kb = k * beta
    a_mat = strict * _nt(kb, k) * dec
    t_inv = _unit_lower_inverse(a_mat, eye)
    e_g = jnp.exp(gc)
    u = _nn(t_inv, v * beta)
    w = _nn(t_inv, kb * e_g)
    p_qk = _nt(q, k)
    qk = p_qk * dec
    qd = q * e_g
    last = (_iota2(c, 1, 0) == c - 1).astype(F32)
    g_last = jnp.sum(last * gc, axis=0, keepdims=True)
    e_t = jnp.exp(g_last - gc)
    kt = k * e_t
    tail = jnp.exp(g_last)
    vn = u - _nn(w, st)
    return dict(eye=eye, low=low, strict=strict, dec=dec, kb=kb, a_mat=a_mat, t_inv=t_inv, e_g=e_g, u=u, w=w,
                qk=qk, qd=qd, last=last, e_t=e_t, kt=kt, tail=tail, vn=vn)


def _dn_chunk_fwd_math(q, k, v, gc, beta, st):
    m = _dn_chunk_common(q, k, v, gc, beta, st)
    o = _nn(m["qd"], st) + _nn(m["qk"], m["vn"])
    st2 = m["tail"] * st + _tn(m["kt"], m["vn"])
    return o, st2


def _dn_chunk_bwd_math(q, k, v, gc, beta, st, do, dst2):
    m = _dn_chunk_common(q, k, v, gc, beta, st)
    eye, low, strict = m["eye"], m["low"], m["strict"]
    dvn = _tn(m["qk"], do) + _nn(m["kt"], dst2)
    dqk = low * _nt(do, m["vn"])
    dqd = _nt(do, st)
    dst = _tn(m["qd"], do) + m["tail"] * dst2 - _tn(m["w"], dvn)
    dkt = _nt(m["vn"], dst2)
    dtail = jnp.sum(jnp.sum(st * dst2, axis=1, keepdims=True), axis=0, keepdims=True)
    dw = -_nt(dvn, st)
    dvb = _tn(m["t_inv"], dvn)
    dkg = _tn(m["t_inv"], dw)
    d_a = -strict * (_nt(dvb, m["u"]) + _nt(dkg, m["w"]))
    dkk = d_a * m["dec"]
    dp = dqk * m["dec"]
    dq = _nn(dp, k) + dqd * m["e_g"]
    dkb = _nn(dkk, k) + dkg * m["e_g"]
    dk = _tn(dp, q) + _tn(dkk, m["kb"]) + dkb * beta + dkt * m["e_t"]
    dv = dvb * beta
    dbeta = jnp.sum(dvb * v + dkb * k, axis=1, keepdims=True)
    de_g = jnp.sum(dkg * m["kb"] + dqd * q, axis=1, keepdims=True)
    de_t = jnp.sum(dkt * k, axis=1, keepdims=True)
    mm = d_a * m["a_mat"] + dqk * m["qk"]
    dgc = (jnp.sum(mm, axis=1, keepdims=True) - _row2col(jnp.sum(mm, axis=0, keepdims=True), eye)
           + de_g * m["e_g"] - de_t * m["e_t"]
           + m["last"] * (jnp.sum(de_t * m["e_t"], axis=0, keepdims=True) + dtail * m["tail"]))
    return dq, dk, dv, dgc, dbeta, dst


def _dn_chunk_fwd(qkv, gcs, beta, *, name):
    s = qkv.shape[0]
    n = s // CHUNK

    def body(q_ref, k_ref, v_ref, g_ref, b_ref, o_ref, st_out_ref, st_ref):
        c, h = pl.program_id(0), pl.program_id(1)

        @pl.when(c == 0)
        def _():
            st_ref[h] = jnp.zeros((HEAD_DIM, HEAD_DIM), F32)

        st = st_ref[h]
        st_out_ref[0, 0] = st
        o, st2 = _dn_chunk_fwd_math(q_ref[...], k_ref[...], v_ref[...], _pick_lane(g_ref[...], h),
                                    _pick_lane(b_ref[...], h), st)
        o_ref[...] = o
        st_ref[h] = st2

    blk = lambda off: pl.BlockSpec((CHUNK, HEAD_DIM), lambda c, h: (c, off + h))
    sc = pl.BlockSpec((CHUNK, HEAD_DIM), lambda c, h: (c, 0))
    return pl.pallas_call(
        body, name=name, grid=(n, N_HEADS),
        in_specs=[blk(0), blk(N_HEADS), blk(2 * N_HEADS), sc, sc],
        out_specs=[blk(0), pl.BlockSpec((1, 1, HEAD_DIM, HEAD_DIM), lambda c, h: (c, h, 0, 0))],
        out_shape=[jax.ShapeDtypeStruct((s, BR_WIDTH), F32), jax.ShapeDtypeStruct((n, N_HEADS, HEAD_DIM, HEAD_DIM), F32)],
        scratch_shapes=[pltpu.VMEM((N_HEADS, HEAD_DIM, HEAD_DIM), F32)],
        compiler_params=_cp("arbitrary", "arbitrary"),
    )(qkv, qkv, qkv, gcs, beta)


def _dn_chunk_bwd(qkv, gcs, beta, states, do, *, name):
    s = qkv.shape[0]
    n = s // CHUNK

    def body(q_ref, k_ref, v_ref, g_ref, b_ref, st_in_ref, do_ref, dq_ref, dk_ref, dv_ref, dg_ref, dbeta_ref, dst_ref):
        c, h = pl.program_id(0), pl.program_id(1)

        @pl.when(c == 0)
        def _():
            dst_ref[h] = jnp.zeros((HEAD_DIM, HEAD_DIM), F32)

        @pl.when(h == 0)
        def _():
            dg_ref[...] = jnp.zeros_like(dg_ref)
            dbeta_ref[...] = jnp.zeros_like(dbeta_ref)

        dq, dk, dv, dgc, dbeta, dst = _dn_chunk_bwd_math(
            q_ref[...], k_ref[...], v_ref[...], _pick_lane(g_ref[...], h), _pick_lane(b_ref[...], h),
            st_in_ref[0, 0], do_ref[...], dst_ref[h])
        dq_ref[...] = dq
        dk_ref[...] = dk
        dv_ref[...] = dv
        sel = (_iota2(CHUNK, HEAD_DIM, 1) == h).astype(F32)
        dg_ref[...] += sel * dgc
        dbeta_ref[...] += sel * dbeta
        dst_ref[h] = dst

    blk = lambda off: pl.BlockSpec((CHUNK, HEAD_DIM), lambda c, h: (n - 1 - c, off + h))
    sc = pl.BlockSpec((CHUNK, HEAD_DIM), lambda c, h: (n - 1 - c, 0))
    outs = pl.pallas_call(
        body, name=name, grid=(n, N_HEADS),
        in_specs=[blk(0), blk(N_HEADS), blk(2 * N_HEADS), sc, sc,
                  pl.BlockSpec((1, 1, HEAD_DIM, HEAD_DIM), lambda c, h: (n - 1 - c, h, 0, 0)), blk(0)],
        out_specs=[blk(0), blk(0), blk(0), sc, sc],
        out_shape=[jax.ShapeDtypeStruct((s, BR_WIDTH), F32)] * 3 + [jax.ShapeDtypeStruct((s, HEAD_DIM), F32)] * 2,
        scratch_shapes=[pltpu.VMEM((N_HEADS, HEAD_DIM, HEAD_DIM), F32)],
        compiler_params=_cp("arbitrary", "arbitrary"),
    )(qkv, qkv, qkv, gcs, beta, states, do)
    return outs


def _hg_prep_fwd(proj, lb, *, name):
    s = proj.shape[0]
    tr = min(ROW_TILE, s)

    def body(q_ref, f_ref, lb_ref, qo_ref, ko_ref, lf_ref):
        f, lbv = f_ref[...], lb_ref[...]
        qo_ref[...] = _silu(q_ref[...])
        ko_ref[...] = (1.0 - lbv) * _sigmoid(-f)
        lf_ref[...] = jnp.log(lbv + (1.0 - lbv) * _sigmoid(f))

    blk = lambda cb: pl.BlockSpec((tr, BR_WIDTH), lambda i: (i, cb))
    out = pl.BlockSpec((tr, BR_WIDTH), lambda i: (i, 0))
    return pl.pallas_call(
        body, name=name, grid=(s // tr,),
        in_specs=[blk(C_HQ // BR_WIDTH), blk(C_HF // BR_WIDTH), pl.BlockSpec((1, BR_WIDTH), lambda i: (0, 0))],
        out_specs=[out, out, out], out_shape=[jax.ShapeDtypeStruct((s, BR_WIDTH), F32)] * 3, compiler_params=_cp("parallel"),
    )(proj, proj, lb)


def _hg_prep_bwd(proj, lb, dq, dk, dlf, *, name):
    s = proj.shape[0]
    tr = min(ROW_TILE, s)

    def body(q_ref, f_ref, lb_ref, dq_ref, dk_ref, dlf_ref, dhq_ref, dhf_ref, dlb_ref):
        @pl.when(pl.program_id(0) == 0)
        def _():
            dlb_ref[...] = jnp.zeros_like(dlb_ref)

        f, lbv = f_ref[...], lb_ref[...]
        dhq_ref[...] = (dq_ref[...] * _dsilu(q_ref[...])).astype(BF16)
        sp, sn = _sigmoid(f), _sigmoid(-f)
        inner = lbv + (1.0 - lbv) * sp
        dlf_over = dlf_ref[...] / inner
        dkv = dk_ref[...]
        dhf_ref[...] = (dlf_over * (1.0 - lbv) * sp * sn - dkv * (1.0 - lbv) * sn * (1.0 - sn)).astype(BF16)
        dlb_ref[...] += jnp.sum(dlf_over * (1.0 - sp) - dkv * sn, axis=0, keepdims=True)

    blk = lambda cb: pl.BlockSpec((tr, BR_WIDTH), lambda i: (i, cb))
    io = pl.BlockSpec((tr, BR_WIDTH), lambda i: (i, 0))
    vec = pl.BlockSpec((1, BR_WIDTH), lambda i: (0, 0))
    return pl.pallas_call(
        body, name=name, grid=(s // tr,),
        in_specs=[blk(C_HQ // BR_WIDTH), blk(C_HF // BR_WIDTH), vec, io, io, io], out_specs=[io, io, vec],
        out_shape=[jax.ShapeDtypeStruct((s, BR_WIDTH), BF16)] * 2 + [jax.ShapeDtypeStruct((1, BR_WIDTH), F32)],
        compiler_params=_cp("arbitrary"),
    )(proj, proj, lb, dq, dk, dlf)


def _hg_chunk_common(q, k, lf):
    c = CHUNK
    g = _nn_exact(_tri(c, "lower"), lf)
    e_g = jnp.exp(g)
    qd = q * e_g
    g_last = g[c - 1:c, :]
    e_t = jnp.exp(g_last - g)
    kt = k * e_t
    tail = jnp.exp(g_last)
    q_sc, k_sc, e_q, e_k = [], [], [], []
    for i in range(c // SUB):
        g_ref = g[i * SUB:i * SUB + 1, :]
        eq = jnp.exp(g[i * SUB:(i + 1) * SUB, :] - g_ref)
        ek = jnp.exp(jnp.minimum(g_ref - g, EXP_CLAMP))
        e_q.append(eq)
        e_k.append(ek)
        q_sc.append(q[i * SUB:(i + 1) * SUB, :] * eq)
        k_sc.append(k * ek)
    a_mat = _tri(c, "lower") * jnp.concatenate([_nt(qi, ki) for qi, ki in zip(q_sc, k_sc)], axis=0)
    return dict(e_g=e_g, qd=qd, e_t=e_t, kt=kt, tail=tail, q_sc=q_sc, k_sc=k_sc, e_q=e_q, e_k=e_k, a_mat=a_mat)


def _hg_chunk_fwd_math(q, k, v, lf, stt):
    m = _hg_chunk_common(q, k, lf)
    o = _nt(m["qd"], stt) + _nn(m["a_mat"], v)
    stt2 = stt * m["tail"] + _tn(v, m["kt"])
    return o, stt2


def _hg_chunk_bwd_math(q, k, v, lf, stt, do, dstt2):
    c = CHUNK
    m = _hg_chunk_common(q, k, lf)
    stt2 = stt * m["tail"] + _tn(v, m["kt"])
    later = jnp.sum(stt2 * dstt2, axis=0, keepdims=True)
    dqd = _dot3(do, stt, 1, 0)
    dstt = _tn(do, m["qd"]) + dstt2 * m["tail"]
    d_a = _tri(c, "lower") * _dot3(do, v, 1, 1)
    dv = _tn(m["a_mat"], do) + _nt(m["kt"], dstt2)
    dkt = _dot3(v, dstt2, 1, 0)
    dq_parts = []
    dk = dkt * m["e_t"]
    for i in range(c // SUB):
        d_ai = d_a[i * SUB:(i + 1) * SUB, :]
        dq_parts.append(_dot3(d_ai, m["k_sc"][i], 1, 0) * m["e_q"][i])
        dk = dk + _dot3(d_ai, m["q_sc"][i], 0, 0) * m["e_k"][i]
    dq = dqd * m["e_g"] + jnp.concatenate(dq_parts, axis=0)
    db = q * dq - k * dk
    dlf = _nn_exact(_tri(c, "upper"), db) + later
    return dq, dk, dv, dlf, dstt


def _hg_chunk_fwd(qh, kh, proj, lf, *, name):
    s = qh.shape[0]
    n = s // CHUNK
    vb = C_HI // HEAD_DIM

    def body(q_ref, k_ref, v_ref, lf_ref, o_ref, st_out_ref, st_ref):
        c, h = pl.program_id(0), pl.program_id(1)

        @pl.when(c == 0)
        def _():
            st_ref[h] = jnp.zeros((HEAD_DIM, HEAD_DIM), F32)

        st = st_ref[h]
        st_out_ref[0, 0] = st
        o, st2 = _hg_chunk_fwd_math(q_ref[...], k_ref[...], v_ref[...], lf_ref[...], st)
        o_ref[...] = o
        st_ref[h] = st2

    blk = lambda off: pl.BlockSpec((CHUNK, HEAD_DIM), lambda c, h: (c, off + h))
    return pl.pallas_call(
        body, name=name, grid=(n, N_HEADS), in_specs=[blk(0), blk(0), blk(vb), blk(0)],
        out_specs=[blk(0), pl.BlockSpec((1, 1, HEAD_DIM, HEAD_DIM), lambda c, h: (c, h, 0, 0))],
        out_shape=[jax.ShapeDtypeStruct((s, BR_WIDTH), F32), jax.ShapeDtypeStruct((n, N_HEADS, HEAD_DIM, HEAD_DIM), F32)],
        scratch_shapes=[pltpu.VMEM((N_HEADS, HEAD_DIM, HEAD_DIM), F32)],
        compiler_params=_cp("arbitrary", "arbitrary"),
    )(qh, kh, proj, lf)


def _hg_chunk_bwd(qh, kh, proj, lf, states, do, *, name):
    s = qh.shape[0]
    n = s // CHUNK
    vb = C_HI // HEAD_DIM

    def body(q_ref, k_ref, v_ref, lf_ref, st_in_ref, do_ref, dq_ref, dk_ref, dv_ref, dlf_ref, dst_ref):
        c, h = pl.program_id(0), pl.program_id(1)

        @pl.when(c == 0)
        def _():
            dst_ref[h] = jnp.zeros((HEAD_DIM, HEAD_DIM), F32)

        dq, dk, dv, dlf, dst = _hg_chunk_bwd_math(
            q_ref[...], k_ref[...], v_ref[...], lf_ref[...], st_in_ref[0, 0], do_ref[...], dst_ref[h])
        dq_ref[...] = dq
        dk_ref[...] = dk
        dv_ref[...] = dv.astype(BF16)
        dlf_ref[...] = dlf
        dst_ref[h] = dst

    blk = lambda off: pl.BlockSpec((CHUNK, HEAD_DIM), lambda c, h: (n - 1 - c, off + h))
    return pl.pallas_call(
        body, name=name, grid=(n, N_HEADS),
        in_specs=[blk(0), blk(0), blk(vb), blk(0),
                  pl.BlockSpec((1, 1, HEAD_DIM, HEAD_DIM), lambda c, h: (n - 1 - c, h, 0, 0)), blk(0)],
        out_specs=[blk(0), blk(0), blk(0), blk(0)],
        out_shape=[jax.ShapeDtypeStruct((s, BR_WIDTH), F32)] * 2 + [jax.ShapeDtypeStruct((s, BR_WIDTH), BF16),
                                                                    jax.ShapeDtypeStruct((s, BR_WIDTH), F32)],
        scratch_shapes=[pltpu.VMEM((N_HEADS, HEAD_DIM, HEAD_DIM), F32)],
        compiler_params=_cp("arbitrary", "arbitrary"),
    )(qh, kh, proj, lf, states, do)


_ANY = pl.BlockSpec(memory_space=pl.ANY)
_MESH = pl.DeviceIdType.MESH


def _all_gather(x_local, *, name):
    def body(x_ref, out_ref, send_sems, recv_sems, local_sem):
        x, y, c = lax.axis_index("x"), lax.axis_index("y"), lax.axis_index("c")
        me, sibling = (x, y, c), (x, y, 1 - c)
        chips = [(1 - x, y), (x, 1 - y), (1 - x, 1 - y)]

        def slot(px, py, pc):
            return out_ref.at[4 * px + 2 * py + pc]

        def copy(k, block, to, src=None):
            return pltpu.make_async_remote_copy(
                src_ref=slot(*block) if src is None else src, dst_ref=slot(*block),
                send_sem=send_sems.at[k], recv_sem=recv_sems.at[k], device_id=to, device_id_type=_MESH)

        mine = pltpu.make_async_copy(x_ref, slot(*me), local_sem)
        mine.start()
        first = [copy(0, me, sibling, src=x_ref)]
        first += [copy(1 + j, me, (*chip, c), src=x_ref) for j, chip in enumerate(chips)]
        for cp in first:
            cp.start()
        passed = [copy(4 + j, (*chip, c), sibling) for j, chip in enumerate(chips)]
        for j, chip in enumerate(chips):
            copy(1 + j, (*chip, c), me).wait_recv()
            passed[j].start()
        copy(0, sibling, me).wait_recv()
        for j, chip in enumerate(chips):
            copy(4 + j, (*chip, 1 - c), me).wait_recv()
        for cp in first + passed:
            cp.wait_send()
        mine.wait()

    return pl.pallas_call(
        body, name=name, out_shape=jax.ShapeDtypeStruct((N_DEV,) + x_local.shape, x_local.dtype),
        in_specs=[_ANY], out_specs=_ANY,
        scratch_shapes=[pltpu.SemaphoreType.DMA((7,)), pltpu.SemaphoreType.DMA((7,)), pltpu.SemaphoreType.DMA],
    )(x_local)


def _exchange(parts, *, name):
    def body(p_ref, out_ref, send_sems, recv_sems, local_sem):
        x, y, c = lax.axis_index("x"), lax.axis_index("y"), lax.axis_index("c")
        my = 4 * x + 2 * y + c
        mine = pltpu.make_async_copy(p_ref.at[my], out_ref.at[my], local_sem)
        mine.start()
        copies = []
        for k in range(1, N_DEV):
            px, py, pc = x ^ ((k >> 2) & 1), y ^ ((k >> 1) & 1), c ^ (k & 1)
            peer = 4 * px + 2 * py + pc
            send = pltpu.make_async_remote_copy(
                src_ref=p_ref.at[peer], dst_ref=out_ref.at[my], send_sem=send_sems.at[k - 1],
                recv_sem=recv_sems.at[k - 1], device_id=(px, py, pc), device_id_type=_MESH)
            send.start()
            recv = pltpu.make_async_remote_copy(
                src_ref=p_ref.at[peer], dst_ref=out_ref.at[peer], send_sem=send_sems.at[k - 1],
                recv_sem=recv_sems.at[k - 1], device_id=(px, py, pc), device_id_type=_MESH)
            copies.append((send, recv))
        for send, recv in copies:
            recv.wait_recv()
        for send, recv in copies:
            send.wait_send()
        mine.wait()

    return pl.pallas_call(
        body, name=name, out_shape=jax.ShapeDtypeStruct(parts.shape, parts.dtype), in_specs=[_ANY], out_specs=_ANY,
        scratch_shapes=[pltpu.SemaphoreType.DMA((7,)), pltpu.SemaphoreType.DMA((7,)), pltpu.SemaphoreType.DMA],
    )(parts)


def _adamw(parts, row_off, w, m, v, *, name, tr):
    r, c = w.shape
    np_ = parts.shape[0]
    tr = min(tr, r)
    assert r % tr == 0 and row_off % tr == 0
    ob = row_off // tr
    c1 = 1.0 - ADAM_B1 ** ADAM_STEP
    c2 = 1.0 - ADAM_B2 ** ADAM_STEP

    def body(p_ref, w_ref, m_ref, v_ref, g_ref, d_ref, nm_ref, nv_ref):
        g = p_ref[0].astype(F32)
        for s in range(1, np_):
            g = g + p_ref[s].astype(F32)
        wv = w_ref[...]
        m2 = ADAM_B1 * m_ref[...] + (1.0 - ADAM_B1) * g
        v2 = ADAM_B2 * v_ref[...] + (1.0 - ADAM_B2) * jnp.square(g)
        m_hat = m2 / c1
        v_hat = v2 / c2
        g_ref[...] = g
        d_ref[...] = -ADAM_LR * (m_hat / (jnp.sqrt(v_hat) + ADAM_EPS) + ADAM_WD * wv)
        nm_ref[...] = m2
        nv_ref[...] = v2

    blk = pl.BlockSpec((tr, c), lambda i: (i, 0))
    return pl.pallas_call(
        body, name=name, grid=(r // tr,),
        in_specs=[pl.BlockSpec((np_, tr, c), lambda i: (0, ob + i, 0)), blk, blk, blk], out_specs=[blk] * 4,
        out_shape=[jax.ShapeDtypeStruct((r, c), F32)] * 4, compiler_params=_cp("parallel"),
    )(parts, w, m, v)


def _sum_parts(parts, *, name):
    np_, r, c = parts.shape

    def body(p_ref, o_ref):
        g = p_ref[0]
        for s in range(1, np_):
            g = g + p_ref[s]
        o_ref[...] = g

    return pl.pallas_call(body, name=name, out_shape=jax.ShapeDtypeStruct((r, c), F32))(parts)


def _pack(arrs):
    rows = []
    for a in arrs:
        f = a.reshape(-1).astype(F32)
        pad = (-f.shape[0]) % 128
        rows.append(jnp.pad(f, (0, pad)).reshape(-1, 128))
    out = jnp.concatenate(rows, axis=0)
    return jnp.pad(out, ((0, (-out.shape[0]) % 8), (0, 0)))


def _unpack(packed, shapes):
    outs, r0 = [], 0
    for shp in shapes:
        n = 1
        for d in shp:
            n *= d
        nr = -(-n // 128)
        outs.append(packed[r0:r0 + nr].reshape(-1)[:n].reshape(shp))
        r0 += nr
    return outs


def _win_reorder(w):
    z = jnp.zeros((w.shape[0], HEAD_DIM - N_HEADS), w.dtype)
    return jnp.concatenate([w[:, 0:4096], w[:, 4112:8208], w[:, 4096:4104], z, w[:, 4104:4112], z], axis=1)


def _win_restore(g):
    return jnp.concatenate([g[:, 0:4096], g[:, C_B:C_B + N_HEADS], g[:, C_A:C_A + N_HEADS], g[:, 4096:8192]], axis=1)


def _lower_bounds(logits):
    probs = jax.nn.softmax(logits.astype(F32), axis=0)
    return jnp.cumsum(probs, axis=0) - probs[0]


def _pad_lanes(vec8):
    return jnp.pad(vec8.reshape(1, N_HEADS), ((0, 0), (0, HEAD_DIM - N_HEADS)))


def kernel(x, p, norm_w, w_in, dn_conv_w, dn_A_log, dn_dt_bias, dn_norm_w, hg_lb_logits, hg_norm_w, w_out, w_ple_up, w_ple_gate, final_norm_w, loss_target, m_norm_w, m_w_in, m_dn_conv_w, m_dn_A_log, m_dn_dt_bias, m_dn_norm_w, m_hg_lb_logits, m_hg_norm_w, m_w_out, m_w_ple_up, m_w_ple_gate, m_final_norm_w, v_norm_w, v_w_in, v_dn_conv_w, v_dn_A_log, v_dn_dt_bias, v_dn_norm_w, v_hg_lb_logits, v_hg_norm_w, v_w_out, v_w_ple_up, v_w_ple_gate, v_final_norm_w):
    depth = norm_w.shape[0]
    my = 4 * lax.axis_index("x") + 2 * lax.axis_index("y") + lax.axis_index("c")
    h = x[0]
    tgt = loss_target[0]
    rows_out = D_MODEL // N_DEV
    up_rows = PLE_DIM * (D_MODEL // N_DEV) // D_MODEL

    win_all = _all_gather(w_in.astype(BF16), name="gather_w_in")
    rest_local = jnp.concatenate([w_out.reshape(depth * rows_out, D_MODEL), w_ple_gate.reshape(depth * rows_out, D_MODEL),
                                  w_ple_up.reshape(depth * up_rows, D_MODEL)], axis=0).astype(BF16)
    rest_all = _all_gather(rest_local, name="gather_w_rest")
    g_off = depth * rows_out
    u_off = 2 * depth * rows_out
    conv_all = _all_gather(dn_conv_w, name="gather_conv_w")
    conv_full = conv_all.transpose(1, 2, 0, 3).reshape(depth, CONV_W, 3 * BR_WIDTH)
    lbs = _lower_bounds(hg_lb_logits)

    def layer_weights(l):
        wi = _win_reorder(win_all[:, l].transpose(1, 0, 2).reshape(D_MODEL, IN_WIDTH))
        wo = rest_all[:, l * rows_out:(l + 1) * rows_out].reshape(D_MODEL, D_MODEL)
        wg = rest_all[:, g_off + l * rows_out:g_off + (l + 1) * rows_out].reshape(D_MODEL, D_MODEL)
        wu = rest_all[:, u_off + l * up_rows:u_off + (l + 1) * up_rows].reshape(N_DEV, PLE_DIM, D_MODEL // N_DEV)
        wu = wu.transpose(1, 0, 2).reshape(PLE_DIM, D_MODEL)
        return wi, wo, wg, wu

    saved = []
    for l in range(depth):
        wi, wo, wg, wu = layer_weights(l)
        tag = f"l{l}"
        hn = _rms_fwd(h, norm_w[l], name=f"rms_fwd_{tag}")
        proj = _mm(hn, wi, mode="nn", out_dtype=F32, tn=768, name=f"mm_proj_{tag}")
        al, dt = _pad_lanes(dn_A_log[l]), _pad_lanes(dn_dt_bias[l])
        qkv = _dn_qkv_fwd(proj, conv_full[l], name=f"dn_qkv_fwd_{tag}")
        beta, gcs = _dn_gate_fwd(proj, al, dt, name=f"dn_gate_fwd_{tag}")
        o_dn, st_dn = _dn_chunk_fwd(qkv, gcs, beta, name=f"dn_chunk_fwd_{tag}")
        lb = lbs[l].reshape(1, BR_WIDTH)
        qh, kh, lf = _hg_prep_fwd(proj, lb, name=f"hg_prep_fwd_{tag}")
        o_hg, st_hg = _hg_chunk_fwd(qh, kh, proj, lf, name=f"hg_chunk_fwd_{tag}")
        y_dn = _hnorm_fwd(o_dn, proj, C_Z, dn_norm_w[l], name=f"hnorm_dn_fwd_{tag}")
        y_hg = _hnorm_fwd(o_hg, proj, C_HZ, hg_norm_w[l], name=f"hnorm_hg_fwd_{tag}")
        y = jnp.concatenate([y_dn, y_hg], axis=1)
        h1 = _mm(y, wo, mode="nn", out_dtype=F32, res=h, name=f"mm_out_{tag}")
        gp = _mm(h1, wg, mode="nn", out_dtype=F32, name=f"mm_gate_{tag}")
        up = _mm(p[l, 0], wu, mode="nn", out_dtype=F32, name=f"mm_up_{tag}")
        h2 = _ple_fwd(h1, gp, up, name=f"ple_fwd_{tag}")
        saved.append(dict(h=h, hn=hn, proj=proj, qkv=qkv, beta=beta, gcs=gcs, st_dn=st_dn, qh=qh, kh=kh, lf=lf,
                          st_hg=st_hg, o_dn=o_dn, o_hg=o_hg, y=y, h1=h1, gp=gp, up=up, al=al, dt=dt, lb=lb))
        h = h2

    loss_row, dh, d_final_w = _final_fwd_bwd(h, final_norm_w, tgt, name="final_norm_loss")

    d_norm_w, d_alog, d_dt, d_dn_nw, d_hg_nw, d_lb, d_conv = ([None] * depth for _ in range(7))
    d_win, d_wo, d_wg, d_wu = ([None] * depth for _ in range(4))
    for l in reversed(range(depth)):
        wi, wo, wg, wu = layer_weights(l)
        sv = saved[l]
        tag = f"l{l}"
        dup, dgp = _ple_bwd(dh, sv["gp"], sv["up"], name=f"ple_bwd_{tag}")
        d_wu[l] = _mm(p[l, 0], dup, mode="tn", out_dtype=BF16, name=f"mm_dwup_{tag}")
        d_wg[l] = _mm(sv["h1"], dgp, mode="tn", out_dtype=BF16, name=f"mm_dwgate_{tag}")
        dh1 = _mm(dgp, wg, mode="nt", out_dtype=F32, res=dh, name=f"mm_dh1_{tag}")
        d_wo[l] = _mm(sv["y"], dh1, mode="tn", out_dtype=BF16, name=f"mm_dwout_{tag}")
        dy = _mm(dh1, wo, mode="nt", out_dtype=F32, name=f"mm_dy_{tag}")
        do_dn, dz_dn, d_dn_nw[l] = _hnorm_bwd(sv["o_dn"], sv["proj"], C_Z, dn_norm_w[l], dy, 0, name=f"hnorm_dn_bwd_{tag}")
        do_hg, dz_hg, d_hg_nw[l] = _hnorm_bwd(sv["o_hg"], sv["proj"], C_HZ, hg_norm_w[l], dy, BR_WIDTH, name=f"hnorm_hg_bwd_{tag}")
        dq, dk, dv, d_gc, dbeta = _dn_chunk_bwd(sv["qkv"], sv["gcs"], sv["beta"], sv["st_dn"], do_dn, name=f"dn_chunk_bwd_{tag}")
        dqkv_pre, d_conv[l] = _dn_qkv_bwd(sv["proj"], conv_full[l], jnp.concatenate([dq, dk, dv], axis=1), name=f"dn_qkv_bwd_{tag}")
        db, da, d_alog[l], d_dt[l] = _dn_gate_bwd(sv["proj"], sv["al"], sv["dt"], dbeta, d_gc, name=f"dn_gate_bwd_{tag}")
        dqh, dkh, dhi, dlf = _hg_chunk_bwd(sv["qh"], sv["kh"], sv["proj"], sv["lf"], sv["st_hg"], do_hg, name=f"hg_chunk_bwd_{tag}")
        dhq, dhf, d_lb[l] = _hg_prep_bwd(sv["proj"], sv["lb"], dqh, dkh, dlf, name=f"hg_prep_bwd_{tag}")
        dproj = jnp.concatenate([dqkv_pre, dz_dn, dhq, dhf, dhi, dz_hg, db, da], axis=1)
        d_win[l] = _mm(sv["hn"], dproj, mode="tn", out_dtype=BF16, tn=768, name=f"mm_dwin_{tag}")
        dhn = _mm(dproj, wi, mode="nt", out_dtype=F32, tk=768, name=f"mm_dhn_{tag}")
        dh, d_norm_w[l] = _rms_bwd(sv["h"], norm_w[l], dhn, dh1, name=f"rms_bwd_{tag}")
    grad_x = dh[None]

    parts_in = jnp.stack([_win_restore(d_win[l]).reshape(D_MODEL, N_DEV, SHARD_IN).transpose(1, 0, 2) for l in range(depth)],
                         axis=1).reshape(N_DEV, depth * D_MODEL, SHARD_IN)
    land_in = _exchange(parts_in, name="exchange_dw_in")
    parts_rest = jnp.concatenate(
        [d_wo[l].reshape(N_DEV, rows_out, D_MODEL) for l in range(depth)]
        + [d_wg[l].reshape(N_DEV, rows_out, D_MODEL) for l in range(depth)]
        + [d_wu[l].reshape(PLE_DIM, N_DEV, D_MODEL // N_DEV).transpose(1, 0, 2).reshape(N_DEV, up_rows, D_MODEL) for l in range(depth)],
        axis=1)
    land_rest = _exchange(parts_rest, name="exchange_dw_rest")

    def big(land, off, w, m, v, name, tr):
        shp = w.shape
        c = land.shape[-1]
        outs = _adamw(land, off, w.reshape(-1, c), m.reshape(-1, c), v.reshape(-1, c), name=name, tr=tr)
        return [o.reshape(shp) for o in outs]

    r_win = big(land_in, 0, w_in, m_w_in, v_w_in, "adamw_w_in", 256)
    r_wo = big(land_rest, 0, w_out, m_w_out, v_w_out, "adamw_w_out", 256)
    r_wg = big(land_rest, g_off, w_ple_gate, m_w_ple_gate, v_w_ple_gate, "adamw_w_gate", 256)
    r_wu = big(land_rest, u_off, w_ple_up, m_w_ple_up, v_w_ple_up, "adamw_w_up", depth * up_rows)

    small_shapes = [(1, 128), norm_w.shape, final_norm_w.shape, dn_A_log.shape, dn_dt_bias.shape, dn_norm_w.shape,
                    hg_norm_w.shape, hg_lb_logits.shape, (depth, CONV_W, 3 * BR_WIDTH)]
    small = _pack([loss_row, jnp.concatenate(d_norm_w, axis=0), d_final_w,
                   jnp.stack([a[0, :N_HEADS] for a in d_alog]), jnp.stack([a[0, :N_HEADS] for a in d_dt]),
                   jnp.concatenate(d_dn_nw, axis=0), jnp.concatenate(d_hg_nw, axis=0), jnp.concatenate(d_lb, axis=0),
                   jnp.stack(d_conv)])
    tot = _unpack(_sum_parts(_all_gather(small, name="gather_small"), name="sum_small"), small_shapes)
    loss = tot[0][0, 0]
    g_lb = tot[7]
    g_logits = jax.vjp(_lower_bounds, hg_lb_logits)[1](g_lb)[0]
    g_conv = lax.dynamic_slice_in_dim(tot[8], my * (3 * BR_WIDTH // N_DEV), 3 * BR_WIDTH // N_DEV, axis=2)
    small_g = [tot[1], g_conv, tot[3], tot[4], tot[5], g_logits, tot[6], tot[2]]
    small_w = [norm_w, dn_conv_w, dn_A_log, dn_dt_bias, dn_norm_w, hg_lb_logits, hg_norm_w, final_norm_w]
    small_m = [m_norm_w, m_dn_conv_w, m_dn_A_log, m_dn_dt_bias, m_dn_norm_w, m_hg_lb_logits, m_hg_norm_w, m_final_norm_w]
    small_v = [v_norm_w, v_dn_conv_w, v_dn_A_log, v_dn_dt_bias, v_dn_norm_w, v_hg_lb_logits, v_hg_norm_w, v_final_norm_w]
    pk_w = _pack(small_w)
    res_small = _adamw(_pack(small_g)[None], 0, pk_w, _pack(small_m), _pack(small_v), name="adamw_small", tr=pk_w.shape[0])
    shapes_w = [a.shape for a in small_w]
    sg, sd, sm, sv_ = (_unpack(r, shapes_w) for r in res_small)

    def order(small_list, big_in, big_out, big_up, big_gate):
        nw, cw, al_, dt_, dnw, lbl, hnw, fw = small_list
        return [nw, big_in, cw, al_, dt_, dnw, lbl, hnw, big_out, big_up, big_gate, fw]

    outs = [loss, grad_x]
    for i, sl in enumerate((sg, sd, sm, sv_)):
        outs += order(sl, r_win[i], r_wo[i], r_wu[i], r_wg[i])
    return tuple(outs)
```

```python
import functools

import jax
import jax.numpy as jnp
from jax import lax
from jax.experimental import pallas as pl
from jax.experimental.pallas import tpu as pltpu

F32 = jnp.float32
BF16 = jnp.bfloat16
HIGHEST = lax.Precision.HIGHEST

N_DEV = 8
D_MODEL = 2048
PLE_DIM = 256
HEAD_DIM = 128
N_HEADS = 8
BR_WIDTH = N_HEADS * HEAD_DIM
CHUNK = 64
SUB = 16
CONV_W = 4
NORM_EPS = 1e-6
L2_EPS = 1e-6
IN_WIDTH = 8208
SHARD_IN = IN_WIDTH // N_DEV
EXP_CLAMP = 80.0

C_QKV, C_Z, C_HQ, C_HF, C_HI, C_HZ, C_B, C_A, N_PROJ = 0, 3072, 4096, 5120, 6144, 7168, 8192, 8320, 8448

ADAM_LR, ADAM_B1, ADAM_B2, ADAM_EPS, ADAM_WD, ADAM_STEP = 0.001, 0.9, 0.999, 1e-08, 0.01, 10

VMEM_LIMIT = 48 * 1024 * 1024


def _cp(*sem):
    return pltpu.CompilerParams(dimension_semantics=sem, vmem_limit_bytes=VMEM_LIMIT)


class _Heads:
    def __init__(self, vals):
        self.v = tuple(vals)

    def __add__(self, o):
        return _hmap(lambda a, b: a + b, self, o)

    def __radd__(self, o):
        return _hmap(lambda a, b: b + a, self, o)

    def __sub__(self, o):
        return _hmap(lambda a, b: a - b, self, o)

    def __rsub__(self, o):
        return _hmap(lambda a, b: b - a, self, o)

    def __mul__(self, o):
        return _hmap(lambda a, b: a * b, self, o)

    def __rmul__(self, o):
        return _hmap(lambda a, b: b * a, self, o)

    def __neg__(self):
        return _hmap(lambda a: -a, self)

    def __getitem__(self, idx):
        return _hmap(lambda a: a[idx], self)


def _hmap(fn, *args):
    n = next((len(a.v) for a in args if isinstance(a, _Heads)), None)
    if n is None:
        return fn(*args)
    return _Heads(fn(*[a.v[i] if isinstance(a, _Heads) else a for a in args]) for i in range(n))


def _dot(a, b, ca, cb):
    return _hmap(lambda x, y: lax.dot_general(x.astype(BF16), y.astype(BF16), (((ca,), (cb,)), ((), ())),
                                              preferred_element_type=F32), a, b)


def _nn(a, b):
    return _dot(a, b, 1, 0)


def _nt(a, b):
    return _dot(a, b, 1, 1)


def _tn(a, b):
    return _dot(a, b, 0, 0)


def _split(a):
    hi = _hmap(lambda x: x.astype(BF16), a)
    return hi, _hmap(lambda x, h: (x - h.astype(F32)).astype(BF16), a, hi)


def _dot3(a, b, ca, cb):
    ah, al = _split(a)
    bh, bl = _split(b)
    return _dot(ah, bh, ca, cb) + (_dot(ah, bl, ca, cb) + _dot(al, bh, ca, cb))


def _nn_exact(a, b):
    return _hmap(lambda y: lax.dot_general(a, y, (((1,), (0,)), ((), ())), precision=HIGHEST,
                                           preferred_element_type=F32), b)


def _exp(x):
    return _hmap(jnp.exp, x)


def _sum(x, axis):
    return _hmap(lambda a: jnp.sum(a, axis=axis, keepdims=True), x)


def _stack_rows(parts):
    return _hmap(lambda *xs: jnp.concatenate(xs, axis=0), *parts)


def _sigmoid(x):
    return jax.nn.sigmoid(x)


def _silu(x):
    return x * _sigmoid(x)


def _dsilu(x):
    s = _sigmoid(x)
    return s * (1.0 + x * (1.0 - s))


def _softplus(x):
    return jnp.maximum(x, 0.0) + jnp.log(1.0 + jnp.exp(-jnp.abs(x)))


def _iota2(n, m, axis):
    return lax.broadcasted_iota(jnp.int32, (n, m), axis)


def _col2row(col, eye):
    return _hmap(lambda c: jnp.sum(eye * c, axis=0, keepdims=True), col)


def _row2col(row, eye):
    return _hmap(lambda r: jnp.sum(eye * r, axis=1, keepdims=True), row)


def _pick_lane(block, lane_idx):
    lane = _iota2(block.shape[0], block.shape[1], 1)
    return jnp.sum(jnp.where(lane == lane_idx, block, 0.0), axis=1, keepdims=True)


MM_TILE_M, MM_TILE_N, MM_TILE_K = 1024, 1408, 2048


def _tile(dim, cap):
    if dim <= cap:
        return dim
    t = cap - cap % 128
    while dim % t:
        t -= 128
    return t


def _mm(a, b, *, mode, out_dtype, res=None, name):
    if mode == "nn":
        (m, kd), (_, n) = a.shape, b.shape
    elif mode == "nt":
        (m, kd), (n, _) = a.shape, b.shape
    else:
        (kd, m), (_, n) = a.shape, b.shape
    tm, tn, tk = _tile(m, MM_TILE_M), _tile(n, MM_TILE_N), _tile(kd, MM_TILE_K)
    assert m % tm == 0 and n % tn == 0 and kd % tk == 0, (m, n, kd, tm, tn, tk)
    nk = kd // tk
    ca, cb = {"nn": (1, 0), "nt": (1, 1), "tn": (0, 0)}[mode]

    def body(*refs):
        if res is None:
            a_ref, b_ref, o_ref, acc_ref = refs
            r_ref = None
        else:
            a_ref, b_ref, r_ref, o_ref, acc_ref = refs
        k = pl.program_id(2)

        @pl.when(k == 0)
        def _():
            acc_ref[...] = jnp.zeros_like(acc_ref)

        acc_ref[...] += _dot(a_ref[...], b_ref[...], ca, cb)

        @pl.when(k == nk - 1)
        def _():
            out = acc_ref[...]
            if r_ref is not None:
                out = out + r_ref[...].astype(F32)
            o_ref[...] = out.astype(o_ref.dtype)

    a_spec = pl.BlockSpec((tk, tm), lambda i, j, k: (k, i)) if mode == "tn" else pl.BlockSpec((tm, tk), lambda i, j, k: (i, k))
    b_spec = pl.BlockSpec((tn, tk), lambda i, j, k: (j, k)) if mode == "nt" else pl.BlockSpec((tk, tn), lambda i, j, k: (k, j))
    o_spec = pl.BlockSpec((tm, tn), lambda i, j, k: (i, j))
    in_specs = [a_spec, b_spec] + ([o_spec] if res is not None else [])
    args = (a, b) + ((res,) if res is not None else ())
    return pl.pallas_call(
        body, name=name, grid=(m // tm, n // tn, nk), in_specs=in_specs, out_specs=o_spec,
        out_shape=jax.ShapeDtypeStruct((m, n), out_dtype),
        scratch_shapes=[pltpu.VMEM((tm, tn), F32)],
        compiler_params=_cp("parallel", "parallel", "arbitrary"),
    )(*args)


ROW_TILE = 256


def _rms_fwd(h, w, *, name):
    s, d = h.shape
    tr = min(ROW_TILE, s)

    def body(h_ref, w_ref, o_ref):
        x = h_ref[...]
        r = lax.rsqrt(jnp.mean(x * x, axis=-1, keepdims=True) + NORM_EPS)
        o_ref[...] = (x * r * w_ref[...]).astype(o_ref.dtype)

    return pl.pallas_call(
        body, name=name, grid=(s // tr,),
        in_specs=[pl.BlockSpec((tr, d), lambda i: (i, 0)), pl.BlockSpec((1, d), lambda i: (0, 0))],
        out_specs=pl.BlockSpec((tr, d), lambda i: (i, 0)),
        out_shape=jax.ShapeDtypeStruct((s, d), BF16), compiler_params=_cp("parallel"),
    )(h, w.reshape(1, d))


def _rms_bwd_math(x, w, dy):
    d = x.shape[-1]
    r = lax.rsqrt(jnp.mean(x * x, axis=-1, keepdims=True) + NORM_EPS)
    gw = dy * w
    dx = r * gw - x * ((r * r * r) * (jnp.sum(gw * x, axis=-1, keepdims=True) / d))
    return dx, dy * x * r


def _rms_bwd(h, w, dhn, res, *, name):
    s, d = h.shape
    tr = min(ROW_TILE, s)

    def body(h_ref, w_ref, g_ref, r_ref, dh_ref, dw_ref):
        @pl.when(pl.program_id(0) == 0)
        def _():
            dw_ref[...] = jnp.zeros_like(dw_ref)

        dx, dwt = _rms_bwd_math(h_ref[...], w_ref[...], g_ref[...])
        dh_ref[...] = r_ref[...] + dx
        dw_ref[...] += jnp.sum(dwt, axis=0, keepdims=True)

    row = pl.BlockSpec((tr, d), lambda i: (i, 0))
    vec = pl.BlockSpec((1, d), lambda i: (0, 0))
    return pl.pallas_call(
        body, name=name, grid=(s // tr,), in_specs=[row, vec, row, row], out_specs=[row, vec],
        out_shape=[jax.ShapeDtypeStruct((s, d), F32), jax.ShapeDtypeStruct((1, d), F32)],
        compiler_params=_cp("arbitrary"),
    )(h, w.reshape(1, d), dhn, res)


def _final_fwd_bwd(h, w, tgt, *, name):
    s, d = h.shape
    tr = min(ROW_TILE, s)

    def body(h_ref, w_ref, t_ref, loss_ref, dh_ref, dw_ref):
        @pl.when(pl.program_id(0) == 0)
        def _():
            loss_ref[...] = jnp.zeros_like(loss_ref)
            dw_ref[...] = jnp.zeros_like(dw_ref)

        x = h_ref[...]
        wv = w_ref[...]
        r = lax.rsqrt(jnp.mean(x * x, axis=-1, keepdims=True) + NORM_EPS)
        err = x * r * wv - t_ref[...]
        row_loss = jnp.mean(err * err, axis=-1, keepdims=True)
        loss_ref[...] += 0.5 * jnp.sum(row_loss, axis=0, keepdims=True)
        dx, dwt = _rms_bwd_math(x, wv, err / d)
        dh_ref[...] = dx
        dw_ref[...] += jnp.sum(dwt, axis=0, keepdims=True)

    row = pl.BlockSpec((tr, d), lambda i: (i, 0))
    vec = pl.BlockSpec((1, d), lambda i: (0, 0))
    return pl.pallas_call(
        body, name=name, grid=(s // tr,), in_specs=[row, vec, row],
        out_specs=[pl.BlockSpec((1, 128), lambda i: (0, 0)), row, vec],
        out_shape=[jax.ShapeDtypeStruct((1, 128), F32), jax.ShapeDtypeStruct((s, d), F32),
                   jax.ShapeDtypeStruct((1, d), F32)],
        compiler_params=_cp("arbitrary"),
    )(h, w.reshape(1, d), tgt)


def _ple_fwd(h1, gate_pre, up, *, name):
    s, d = h1.shape
    tr = min(ROW_TILE, s)

    def body(h_ref, g_ref, u_ref, o_ref):
        o_ref[...] = h_ref[...] + u_ref[...] * _sigmoid(g_ref[...])

    row = pl.BlockSpec((tr, d), lambda i: (i, 0))
    return pl.pallas_call(body, name=name, grid=(s // tr,), in_specs=[row, row, row], out_specs=row,
                          out_shape=jax.ShapeDtypeStruct((s, d), F32), compiler_params=_cp("parallel"))(h1, gate_pre, up)


def _ple_bwd(dh2, gate_pre, up, *, name):
    s, d = dh2.shape
    tr = min(ROW_TILE, s)

    def body(d_ref, g_ref, u_ref, dup_ref, dgp_ref):
        dh = d_ref[...]
        gate = _sigmoid(g_ref[...])
        dup_ref[...] = (dh * gate).astype(BF16)
        dgp_ref[...] = (dh * u_ref[...] * gate * (1.0 - gate)).astype(BF16)

    row = pl.BlockSpec((tr, d), lambda i: (i, 0))
    return pl.pallas_call(body, name=name, grid=(s // tr,), in_specs=[row, row, row], out_specs=[row, row],
                          out_shape=[jax.ShapeDtypeStruct((s, d), BF16)] * 2, compiler_params=_cp("parallel"))(dh2, gate_pre, up)


HN_TILE = 512


def _hnorm_fwd(o, proj, z_col, w, *, name):
    s = o.shape[0]
    tr = min(HN_TILE, s)
    zb = z_col // HEAD_DIM

    def body(o_ref, z_ref, w_ref, y_ref):
        x = o_ref[...]
        r = lax.rsqrt(jnp.mean(x * x, axis=-1, keepdims=True) + NORM_EPS)
        y_ref[...] = (x * r * w_ref[...] * _silu(z_ref[...])).astype(BF16)

    blk = pl.BlockSpec((tr, HEAD_DIM), lambda i, h: (i, h))
    return pl.pallas_call(
        body, name=name, grid=(s // tr, N_HEADS),
        in_specs=[blk, pl.BlockSpec((tr, HEAD_DIM), lambda i, h: (i, zb + h)), pl.BlockSpec((1, HEAD_DIM), lambda i, h: (0, 0))],
        out_specs=blk, out_shape=jax.ShapeDtypeStruct((s, BR_WIDTH), BF16), compiler_params=_cp("parallel", "parallel"),
    )(o, proj, w.reshape(1, HEAD_DIM))


def _hnorm_bwd(o, proj, z_col, w, dy, dy_col, *, name):
    s = o.shape[0]
    tr = min(HN_TILE, s)
    zb, yb = z_col // HEAD_DIM, dy_col // HEAD_DIM

    def body(o_ref, z_ref, w_ref, dy_ref, do_ref, dz_ref, dw_ref):
        @pl.when((pl.program_id(0) == 0) & (pl.program_id(1) == 0))
        def _():
            dw_ref[...] = jnp.zeros_like(dw_ref)

        x, z, wv, g = o_ref[...], z_ref[...], w_ref[...], dy_ref[...]
        r = lax.rsqrt(jnp.mean(x * x, axis=-1, keepdims=True) + NORM_EPS)
        on = x * r * wv
        don = g * _silu(z)
        dz_ref[...] = (g * on * _dsilu(z)).astype(BF16)
        gw = don * wv
        do_ref[...] = r * gw - x * ((r * r * r) * (jnp.sum(gw * x, axis=-1, keepdims=True) / HEAD_DIM))
        dw_ref[...] += jnp.sum(don * x * r, axis=0, keepdims=True)

    blk = pl.BlockSpec((tr, HEAD_DIM), lambda i, h: (i, h))
    vec = pl.BlockSpec((1, HEAD_DIM), lambda i, h: (0, 0))
    return pl.pallas_call(
        body, name=name, grid=(s // tr, N_HEADS),
        in_specs=[blk, pl.BlockSpec((tr, HEAD_DIM), lambda i, h: (i, zb + h)), vec,
                  pl.BlockSpec((tr, HEAD_DIM), lambda i, h: (i, yb + h))],
        out_specs=[blk, blk, vec],
        out_shape=[jax.ShapeDtypeStruct((s, BR_WIDTH), F32), jax.ShapeDtypeStruct((s, BR_WIDTH), BF16),
                   jax.ShapeDtypeStruct((1, HEAD_DIM), F32)],
        compiler_params=_cp("arbitrary", "arbitrary"),
    )(o, proj, w.reshape(1, HEAD_DIM), dy)


def _conv_silu(x, w, s):
    row = _iota2(s, x.shape[1], 0)
    c = w[CONV_W - 1:CONV_W, :] * x
    for k in range(1, CONV_W):
        c = c + w[CONV_W - 1 - k:CONV_W - k, :] * jnp.where(row >= k, pltpu.roll(x, k, 0), 0.0)
    return c


def _dn_qkv_fwd(proj, conv_w, *, name):
    s = proj.shape[0]
    nb = 3 * N_HEADS

    def body(x_ref, w_ref, o_ref):
        j = pl.program_id(0)
        sv = _silu(_conv_silu(x_ref[...], w_ref[...], s))
        r = lax.rsqrt(jnp.sum(sv * sv, axis=-1, keepdims=True) + L2_EPS)
        scale = jnp.where(j < N_HEADS, HEAD_DIM ** -0.5, 1.0).astype(F32)
        o_ref[...] = jnp.where(j < 2 * N_HEADS, sv * r * scale, sv)

    return pl.pallas_call(
        body, name=name, grid=(nb,),
        in_specs=[pl.BlockSpec((s, HEAD_DIM), lambda j: (0, j)), pl.BlockSpec((CONV_W, HEAD_DIM), lambda j: (0, j))],
        out_specs=pl.BlockSpec((s, HEAD_DIM), lambda j: (0, j)),
        out_shape=jax.ShapeDtypeStruct((s, 3 * BR_WIDTH), F32), compiler_params=_cp("parallel"),
    )(proj, conv_w)


def _dn_qkv_bwd(proj, conv_w, dqkv, *, name):
    s = proj.shape[0]
    nb = 3 * N_HEADS

    def body(x_ref, w_ref, g_ref, dx_ref, dw_ref):
        j = pl.program_id(0)
        x, w, g = x_ref[...], w_ref[...], g_ref[...]
        c = _conv_silu(x, w, s)
        sv = _silu(c)
        r = lax.rsqrt(jnp.sum(sv * sv, axis=-1, keepdims=True) + L2_EPS)
        scale = jnp.where(j < N_HEADS, HEAD_DIM ** -0.5, 1.0).astype(F32)
        ds_n = scale * (r * g - sv * ((r * r * r) * jnp.sum(g * sv, axis=-1, keepdims=True)))
        dc = jnp.where(j < 2 * N_HEADS, ds_n, g) * _dsilu(c)
        row = _iota2(s, HEAD_DIM, 0)
        dx = w[CONV_W - 1:CONV_W, :] * dc
        dws = [jnp.sum(dc * x, axis=0, keepdims=True)]
        for k in range(1, CONV_W):
            dx = dx + w[CONV_W - 1 - k:CONV_W - k, :] * jnp.where(row < s - k, pltpu.roll(dc, s - k, 0), 0.0)
            dws.append(jnp.sum(dc * jnp.where(row >= k, pltpu.roll(x, k, 0), 0.0), axis=0, keepdims=True))
        dx_ref[...] = dx.astype(BF16)
        for k in range(CONV_W):
            dw_ref[CONV_W - 1 - k:CONV_W - k, :] = dws[k]

    blk = pl.BlockSpec((s, HEAD_DIM), lambda j: (0, j))
    wblk = pl.BlockSpec((CONV_W, HEAD_DIM), lambda j: (0, j))
    return pl.pallas_call(
        body, name=name, grid=(nb,), in_specs=[blk, wblk, blk], out_specs=[blk, wblk],
        out_shape=[jax.ShapeDtypeStruct((s, 3 * BR_WIDTH), BF16), jax.ShapeDtypeStruct((CONV_W, 3 * BR_WIDTH), F32)],
        compiler_params=_cp("parallel"),
    )(proj, conv_w, dqkv)


def _tri(n, kind):
    r, c = _iota2(n, n, 0), _iota2(n, n, 1)
    if kind == "lower":
        return (r >= c).astype(F32)
    if kind == "upper":
        return (r <= c).astype(F32)
    return (r == c).astype(F32)


def _dn_gate_fwd(proj, a_log, dt_bias, *, name):
    s = proj.shape[0]

    def body(b_ref, a_ref, al_ref, dt_ref, beta_ref, g_ref):
        beta_ref[...] = _sigmoid(b_ref[...])
        g = -jnp.exp(al_ref[...]) * _softplus(a_ref[...] + dt_ref[...])
        g_ref[...] = _nn_exact(_tri(CHUNK, "lower"), g)

    blk = lambda cb: pl.BlockSpec((CHUNK, HEAD_DIM), lambda i: (i, cb))
    vec = pl.BlockSpec((1, HEAD_DIM), lambda i: (0, 0))
    out = pl.BlockSpec((CHUNK, HEAD_DIM), lambda i: (i, 0))
    return pl.pallas_call(
        body, name=name, grid=(s // CHUNK,), in_specs=[blk(C_B // HEAD_DIM), blk(C_A // HEAD_DIM), vec, vec],
        out_specs=[out, out], out_shape=[jax.ShapeDtypeStruct((s, HEAD_DIM), F32)] * 2, compiler_params=_cp("parallel"),
    )(proj, proj, a_log, dt_bias)


def _dn_gate_bwd(proj, a_log, dt_bias, dbeta, d_g, *, name):
    s = proj.shape[0]

    def body(b_ref, a_ref, al_ref, dt_ref, dbeta_ref, dG_ref, db_ref, da_ref, dal_ref, ddt_ref):
        @pl.when(pl.program_id(0) == 0)
        def _():
            dal_ref[...] = jnp.zeros_like(dal_ref)
            ddt_ref[...] = jnp.zeros_like(ddt_ref)

        beta = _sigmoid(b_ref[...])
        db_ref[...] = (dbeta_ref[...] * beta * (1.0 - beta)).astype(BF16)
        pre = a_ref[...] + dt_ref[...]
        neg_ea = -jnp.exp(al_ref[...])
        dg = _nn_exact(_tri(CHUNK, "upper"), dG_ref[...])
        da = dg * neg_ea * _sigmoid(pre)
        da_ref[...] = da.astype(BF16)
        ddt_ref[...] += jnp.sum(da, axis=0, keepdims=True)
        dal_ref[...] += jnp.sum(dg * neg_ea * _softplus(pre), axis=0, keepdims=True)

    blk = lambda cb: pl.BlockSpec((CHUNK, HEAD_DIM), lambda i: (i, cb))
    vec = pl.BlockSpec((1, HEAD_DIM), lambda i: (0, 0))
    io = pl.BlockSpec((CHUNK, HEAD_DIM), lambda i: (i, 0))
    return pl.pallas_call(
        body, name=name, grid=(s // CHUNK,),
        in_specs=[blk(C_B // HEAD_DIM), blk(C_A // HEAD_DIM), vec, vec, io, io], out_specs=[io, io, vec, vec],
        out_shape=[jax.ShapeDtypeStruct((s, HEAD_DIM), BF16)] * 2 + [jax.ShapeDtypeStruct((1, HEAD_DIM), F32)] * 2,
        compiler_params=_cp("arbitrary"),
    )(proj, proj, a_log, dt_bias, dbeta, d_g)


def _unit_lower_inverse(a_strict, eye):
    x = -a_strict
    t = x + eye
    p = x
    n = 2
    while n < CHUNK:
        p = _nn(p, p)
        t = t + _nn(t, p)
        n *= 2
    return t


def _dn_chunk_common(q, k, v, gc, beta, st):
    c = CHUNK
    eye = _tri(c, "eye")
    low = _tri(c, "lower")
    strict = low - eye
    grow = _col2row(gc, eye)
    dec = _hmap(lambda g_, gr: low * jnp.exp(low * (g_ - gr)), gc, grow)
    kb = k * beta
    a_mat = _nt(kb, k) * dec * strict
    t_inv = _unit_lower_inverse(a_mat, eye)
    e_g = _exp(gc)
    u = _nn(t_inv, v * beta)
    w = _nn(t_inv, kb * e_g)
    p_qk = _nt(q, k)
    qk = p_qk * dec
    qd = q * e_g
    last = (_iota2(c, 1, 0) == c - 1).astype(F32)
    g_last = _sum(gc * last, 0)
    e_t = _exp(g_last - gc)
    kt = k * e_t
    tail = _exp(g_last)
    vn = u - _nn(w, st)
    return dict(eye=eye, low=low, strict=strict, dec=dec, kb=kb, a_mat=a_mat, t_inv=t_inv, e_g=e_g, u=u, w=w,
                qk=qk, qd=qd, last=last, e_t=e_t, kt=kt, tail=tail, vn=vn)


def _dn_chunk_fwd_math(q, k, v, gc, beta, st):
    m = _dn_chunk_common(q, k, v, gc, beta, st)
    o = _nn(m["qd"], st) + _nn(m["qk"], m["vn"])
    st2 = st * m["tail"] + _tn(m["kt"], m["vn"])
    return o, st2


def _dn_chunk_bwd_math(q, k, v, gc, beta, st, do, dst2):
    m = _dn_chunk_common(q, k, v, gc, beta, st)
    eye, low, strict = m["eye"], m["low"], m["strict"]
    dvn = _tn(m["qk"], do) + _nn(m["kt"], dst2)
    dqk = _nt(do, m["vn"]) * low
    dqd = _nt(do, st)
    dst = _tn(m["qd"], do) + dst2 * m["tail"] - _tn(m["w"], dvn)
    dkt = _nt(m["vn"], dst2)
    dtail = _sum(_sum(st * dst2, 1), 0)
    dw = -_nt(dvn, st)
    dvb = _tn(m["t_inv"], dvn)
    dkg = _tn(m["t_inv"], dw)
    d_a = (_nt(dvb, m["u"]) + _nt(dkg, m["w"])) * (-strict)
    dkk = d_a * m["dec"]
    dp = dqk * m["dec"]
    dq = _nn(dp, k) + dqd * m["e_g"]
    dkb = _nn(dkk, k) + dkg * m["e_g"]
    dk = _tn(dp, q) + _tn(dkk, m["kb"]) + dkb * beta + dkt * m["e_t"]
    dv = dvb * beta
    dbeta = _sum(dvb * v + dkb * k, 1)
    de_g = _sum(dkg * m["kb"] + dqd * q, 1)
    de_t = _sum(dkt * k, 1)
    mm = d_a * m["a_mat"] + dqk * m["qk"]
    dgc = (_sum(mm, 1) - _row2col(_sum(mm, 0), eye) + de_g * m["e_g"] - de_t * m["e_t"]
           + (_sum(de_t * m["e_t"], 0) + dtail * m["tail"]) * m["last"])
    return dq, dk, dv, dgc, dbeta, dst


def _heads_of(ref):
    return _Heads(ref[:, h * HEAD_DIM:(h + 1) * HEAD_DIM] for h in range(N_HEADS))


def _lanes_of(block):
    return _Heads(_pick_lane(block, h) for h in range(N_HEADS))


def _dn_chunk_fwd(qkv, gcs, beta, *, name):
    s = qkv.shape[0]
    n = s // CHUNK

    def body(q_ref, k_ref, v_ref, g_ref, b_ref, o_ref, st_out_ref, st_ref):
        @pl.when(pl.program_id(0) == 0)
        def _():
            st_ref[...] = jnp.zeros_like(st_ref)

        gblk, bblk = g_ref[...], b_ref[...]
        st = _Heads(st_ref[h] for h in range(N_HEADS))
        o, st2 = _dn_chunk_fwd_math(_heads_of(q_ref), _heads_of(k_ref), _heads_of(v_ref), _lanes_of(gblk),
                                    _lanes_of(bblk), st)
        for h in range(N_HEADS):
            st_out_ref[0, h] = st.v[h]
            o_ref[:, h * HEAD_DIM:(h + 1) * HEAD_DIM] = o.v[h]
            st_ref[h] = st2.v[h]

    blk = lambda off: pl.BlockSpec((CHUNK, BR_WIDTH), lambda c: (c, off))
    sc = pl.BlockSpec((CHUNK, HEAD_DIM), lambda c: (c, 0))
    return pl.pallas_call(
        body, name=name, grid=(n,),
        in_specs=[blk(0), blk(1), blk(2), sc, sc],
        out_specs=[blk(0), pl.BlockSpec((1, N_HEADS, HEAD_DIM, HEAD_DIM), lambda c: (c, 0, 0, 0))],
        out_shape=[jax.ShapeDtypeStruct((s, BR_WIDTH), F32), jax.ShapeDtypeStruct((n, N_HEADS, HEAD_DIM, HEAD_DIM), F32)],
        scratch_shapes=[pltpu.VMEM((N_HEADS, HEAD_DIM, HEAD_DIM), F32)],
        compiler_params=_cp("arbitrary"),
    )(qkv, qkv, qkv, gcs, beta)


def _dn_chunk_bwd(qkv, gcs, beta, states, do, *, name):
    s = qkv.shape[0]
    n = s // CHUNK

    def body(q_ref, k_ref, v_ref, g_ref, b_ref, st_in_ref, do_ref, dqkv_ref, dg_ref, dbeta_ref, dst_ref):
        @pl.when(pl.program_id(0) == 0)
        def _():
            dst_ref[...] = jnp.zeros_like(dst_ref)

        gblk, bblk = g_ref[...], b_ref[...]
        lane = _iota2(CHUNK, HEAD_DIM, 1)
        dg_all = jnp.zeros((CHUNK, HEAD_DIM), F32)
        dbeta_all = jnp.zeros((CHUNK, HEAD_DIM), F32)
        dq, dk, dv, dgc, dbeta, dst = _dn_chunk_bwd_math(
            _heads_of(q_ref), _heads_of(k_ref), _heads_of(v_ref), _lanes_of(gblk), _lanes_of(bblk),
            _Heads(st_in_ref[0, h] for h in range(N_HEADS)), _heads_of(do_ref),
            _Heads(dst_ref[h] for h in range(N_HEADS)))
        for h in range(N_HEADS):
            for part, val in enumerate((dq, dk, dv)):
                c0 = part * BR_WIDTH + h * HEAD_DIM
                dqkv_ref[:, c0:c0 + HEAD_DIM] = val.v[h]
            dg_all = jnp.where(lane == h, dgc.v[h], dg_all)
            dbeta_all = jnp.where(lane == h, dbeta.v[h], dbeta_all)
            dst_ref[h] = dst.v[h]
        dg_ref[...] = dg_all
        dbeta_ref[...] = dbeta_all

    blk = lambda off: pl.BlockSpec((CHUNK, BR_WIDTH), lambda c: (n - 1 - c, off))
    sc = pl.BlockSpec((CHUNK, HEAD_DIM), lambda c: (n - 1 - c, 0))
    outs = pl.pallas_call(
        body, name=name, grid=(n,),
        in_specs=[blk(0), blk(1), blk(2), sc, sc,
                  pl.BlockSpec((1, N_HEADS, HEAD_DIM, HEAD_DIM), lambda c: (n - 1 - c, 0, 0, 0)), blk(0)],
        out_specs=[pl.BlockSpec((CHUNK, 3 * BR_WIDTH), lambda c: (n - 1 - c, 0)), sc, sc],
        out_shape=[jax.ShapeDtypeStruct((s, 3 * BR_WIDTH), F32)] + [jax.ShapeDtypeStruct((s, HEAD_DIM), F32)] * 2,
        scratch_shapes=[pltpu.VMEM((N_HEADS, HEAD_DIM, HEAD_DIM), F32)],
        compiler_params=_cp("arbitrary"),
    )(qkv, qkv, qkv, gcs, beta, states, do)
    return outs


def _hg_prep_fwd(proj, lb, *, name):
    s = proj.shape[0]
    tr = min(ROW_TILE, s)

    def body(q_ref, f_ref, lb_ref, qo_ref, ko_ref, lf_ref):
        f, lbv = f_ref[...], lb_ref[...]
        qo_ref[...] = _silu(q_ref[...])
        ko_ref[...] = (1.0 - lbv) * _sigmoid(-f)
        lf_ref[...] = jnp.log(lbv + (1.0 - lbv) * _sigmoid(f))

    blk = lambda cb: pl.BlockSpec((tr, BR_WIDTH), lambda i: (i, cb))
    out = pl.BlockSpec((tr, BR_WIDTH), lambda i: (i, 0))
    return pl.pallas_call(
        body, name=name, grid=(s // tr,),
        in_specs=[blk(C_HQ // BR_WIDTH), blk(C_HF // BR_WIDTH), pl.BlockSpec((1, BR_WIDTH), lambda i: (0, 0))],
        out_specs=[out, out, out], out_shape=[jax.ShapeDtypeStruct((s, BR_WIDTH), F32)] * 3, compiler_params=_cp("parallel"),
    )(proj, proj, lb)


def _hg_prep_bwd(proj, lb, dq, dk, dlf, *, name):
    s = proj.shape[0]
    tr = min(ROW_TILE, s)

    def body(q_ref, f_ref, lb_ref, dq_ref, dk_ref, dlf_ref, dhq_ref, dhf_ref, dlb_ref):
        @pl.when(pl.program_id(0) == 0)
        def _():
            dlb_ref[...] = jnp.zeros_like(dlb_ref)

        f, lbv = f_ref[...], lb_ref[...]
        dhq_ref[...] = (dq_ref[...] * _dsilu(q_ref[...])).astype(BF16)
        sp, sn = _sigmoid(f), _sigmoid(-f)
        inner = lbv + (1.0 - lbv) * sp
        dlf_over = dlf_ref[...] / inner
        dkv = dk_ref[...]
        dhf_ref[...] = (dlf_over * (1.0 - lbv) * sp * sn - dkv * (1.0 - lbv) * sn * (1.0 - sn)).astype(BF16)
        dlb_ref[...] += jnp.sum(dlf_over * (1.0 - sp) - dkv * sn, axis=0, keepdims=True)

    blk = lambda cb: pl.BlockSpec((tr, BR_WIDTH), lambda i: (i, cb))
    io = pl.BlockSpec((tr, BR_WIDTH), lambda i: (i, 0))
    vec = pl.BlockSpec((1, BR_WIDTH), lambda i: (0, 0))
    return pl.pallas_call(
        body, name=name, grid=(s // tr,),
        in_specs=[blk(C_HQ // BR_WIDTH), blk(C_HF // BR_WIDTH), vec, io, io, io], out_specs=[io, io, vec],
        out_shape=[jax.ShapeDtypeStruct((s, BR_WIDTH), BF16)] * 2 + [jax.ShapeDtypeStruct((1, BR_WIDTH), F32)],
        compiler_params=_cp("arbitrary"),
    )(proj, proj, lb, dq, dk, dlf)


def _hg_chunk_common(q, k, lf):
    c = CHUNK
    g = _nn_exact(_tri(c, "lower"), lf)
    e_g = _exp(g)
    qd = q * e_g
    g_last = g[c - 1:c, :]
    e_t = _exp(g_last - g)
    kt = k * e_t
    tail = _exp(g_last)
    q_sc, k_sc, e_q, e_k = [], [], [], []
    for i in range(c // SUB):
        g_ref = g[i * SUB:i * SUB + 1, :]
        eq = _exp(g[i * SUB:(i + 1) * SUB, :] - g_ref)
        ek = _hmap(lambda gr, g_: jnp.exp(jnp.minimum(gr - g_, EXP_CLAMP)), g_ref, g)
        e_q.append(eq)
        e_k.append(ek)
        q_sc.append(q[i * SUB:(i + 1) * SUB, :] * eq)
        k_sc.append(k * ek)
    a_mat = _stack_rows([_nt(qi, ki) for qi, ki in zip(q_sc, k_sc)]) * _tri(c, "lower")
    return dict(e_g=e_g, qd=qd, e_t=e_t, kt=kt, tail=tail, q_sc=q_sc, k_sc=k_sc, e_q=e_q, e_k=e_k, a_mat=a_mat)


def _hg_chunk_fwd_math(q, k, v, lf, stt):
    m = _hg_chunk_common(q, k, lf)
    o = _nt(m["qd"], stt) + _nn(m["a_mat"], v)
    stt2 = stt * m["tail"] + _tn(v, m["kt"])
    return o, stt2


def _hg_chunk_bwd_math(q, k, v, lf, stt, do, dstt2):
    c = CHUNK
    m = _hg_chunk_common(q, k, lf)
    stt2 = stt * m["tail"] + _tn(v, m["kt"])
    later = _sum(stt2 * dstt2, 0)
    dqd = _dot3(do, stt, 1, 0)
    dstt = _tn(do, m["qd"]) + dstt2 * m["tail"]
    d_a = _dot3(do, v, 1, 1) * _tri(c, "lower")
    dv = _tn(m["a_mat"], do) + _nt(m["kt"], dstt2)
    dkt = _dot3(v, dstt2, 1, 0)
    dq_parts = []
    dk = dkt * m["e_t"]
    for i in range(c // SUB):
        d_ai = d_a[i * SUB:(i + 1) * SUB, :]
        dq_parts.append(_dot3(d_ai, m["k_sc"][i], 1, 0) * m["e_q"][i])
        dk = dk + _dot3(d_ai, m["q_sc"][i], 0, 0) * m["e_k"][i]
    dq = dqd * m["e_g"] + _stack_rows(dq_parts)
    db = q * dq - k * dk
    dlf = _nn_exact(_tri(c, "upper"), db) + later
    return dq, dk, dv, dlf, dstt


def _hg_chunk_fwd(qh, kh, proj, lf, *, name):
    s = qh.shape[0]
    n = s // CHUNK
    vb = C_HI // BR_WIDTH

    def body(q_ref, k_ref, v_ref, lf_ref, o_ref, st_out_ref, st_ref):
        @pl.when(pl.program_id(0) == 0)
        def _():
            st_ref[...] = jnp.zeros_like(st_ref)

        st = _Heads(st_ref[h] for h in range(N_HEADS))
        o, st2 = _hg_chunk_fwd_math(_heads_of(q_ref), _heads_of(k_ref), _heads_of(v_ref), _heads_of(lf_ref), st)
        for h in range(N_HEADS):
            st_out_ref[0, h] = st.v[h]
            o_ref[:, h * HEAD_DIM:(h + 1) * HEAD_DIM] = o.v[h]
            st_ref[h] = st2.v[h]

    blk = lambda off: pl.BlockSpec((CHUNK, BR_WIDTH), lambda c: (c, off))
    return pl.pallas_call(
        body, name=name, grid=(n,), in_specs=[blk(0), blk(0), blk(vb), blk(0)],
        out_specs=[blk(0), pl.BlockSpec((1, N_HEADS, HEAD_DIM, HEAD_DIM), lambda c: (c, 0, 0, 0))],
        out_shape=[jax.ShapeDtypeStruct((s, BR_WIDTH), F32), jax.ShapeDtypeStruct((n, N_HEADS, HEAD_DIM, HEAD_DIM), F32)],
        scratch_shapes=[pltpu.VMEM((N_HEADS, HEAD_DIM, HEAD_DIM), F32)],
        compiler_params=_cp("arbitrary"),
    )(qh, kh, proj, lf)


def _hg_chunk_bwd(qh, kh, proj, lf, states, do, *, name):
    s = qh.shape[0]
    n = s // CHUNK
    vb = C_HI // BR_WIDTH

    def body(q_ref, k_ref, v_ref, lf_ref, st_in_ref, do_ref, dq_ref, dk_ref, dv_ref, dlf_ref, dst_ref):
        @pl.when(pl.program_id(0) == 0)
        def _():
            dst_ref[...] = jnp.zeros_like(dst_ref)

        dq, dk, dv, dlf, dst = _hg_chunk_bwd_math(
            _heads_of(q_ref), _heads_of(k_ref), _heads_of(v_ref), _heads_of(lf_ref),
            _Heads(st_in_ref[0, h] for h in range(N_HEADS)), _heads_of(do_ref),
            _Heads(dst_ref[h] for h in range(N_HEADS)))
        for h in range(N_HEADS):
            cols = slice(h * HEAD_DIM, (h + 1) * HEAD_DIM)
            dq_ref[:, cols] = dq.v[h]
            dk_ref[:, cols] = dk.v[h]
            dv_ref[:, cols] = dv.v[h].astype(BF16)
            dlf_ref[:, cols] = dlf.v[h]
            dst_ref[h] = dst.v[h]

    blk = lambda off: pl.BlockSpec((CHUNK, BR_WIDTH), lambda c: (n - 1 - c, off))
    return pl.pallas_call(
        body, name=name, grid=(n,),
        in_specs=[blk(0), blk(0), blk(vb), blk(0),
                  pl.BlockSpec((1, N_HEADS, HEAD_DIM, HEAD_DIM), lambda c: (n - 1 - c, 0, 0, 0)), blk(0)],
        out_specs=[blk(0), blk(0), blk(0), blk(0)],
        out_shape=[jax.ShapeDtypeStruct((s, BR_WIDTH), F32)] * 2 + [jax.ShapeDtypeStruct((s, BR_WIDTH), BF16),
                                                                    jax.ShapeDtypeStruct((s, BR_WIDTH), F32)],
        scratch_shapes=[pltpu.VMEM((N_HEADS, HEAD_DIM, HEAD_DIM), F32)],
        compiler_params=_cp("arbitrary"),
    )(qh, kh, proj, lf, states, do)


_ANY = pl.BlockSpec(memory_space=pl.ANY)
_MESH = pl.DeviceIdType.MESH


def _all_gather(x_local, *, name):
    def body(x_ref, out_ref, send_sems, recv_sems, local_sem):
        x, y, c = lax.axis_index("x"), lax.axis_index("y"), lax.axis_index("c")
        me, sibling = (x, y, c), (x, y, 1 - c)
        chips = [(1 - x, y), (x, 1 - y), (1 - x, 1 - y)]

        def slot(px, py, pc):
            return out_ref.at[4 * px + 2 * py + pc]

        def copy(k, block, to, src=None):
            return pltpu.make_async_remote_copy(
                src_ref=slot(*block) if src is None else src, dst_ref=slot(*block),
                send_sem=send_sems.at[k], recv_sem=recv_sems.at[k], device_id=to, device_id_type=_MESH)

        mine = pltpu.make_async_copy(x_ref, slot(*me), local_sem)
        mine.start()
        first = [copy(0, me, sibling, src=x_ref)]
        first += [copy(1 + j, me, (*chip, c), src=x_ref) for j, chip in enumerate(chips)]
        for cp in first:
            cp.start()
        passed = [copy(4 + j, (*chip, c), sibling) for j, chip in enumerate(chips)]
        for j, chip in enumerate(chips):
            copy(1 + j, (*chip, c), me).wait_recv()
            passed[j].start()
        copy(0, sibling, me).wait_recv()
        for j, chip in enumerate(chips):
            copy(4 + j, (*chip, 1 - c), me).wait_recv()
        for cp in first + passed:
            cp.wait_send()
        mine.wait()

    return pl.pallas_call(
        body, name=name, out_shape=jax.ShapeDtypeStruct((N_DEV,) + x_local.shape, x_local.dtype),
        in_specs=[_ANY], out_specs=_ANY,
        scratch_shapes=[pltpu.SemaphoreType.DMA((7,)), pltpu.SemaphoreType.DMA((7,)), pltpu.SemaphoreType.DMA],
    )(x_local)


def _exchange(parts, *, name):
    def body(p_ref, out_ref, send_sems, recv_sems, local_sem):
        x, y, c = lax.axis_index("x"), lax.axis_index("y"), lax.axis_index("c")
        my = 4 * x + 2 * y + c
        mine = pltpu.make_async_copy(p_ref.at[my], out_ref.at[my], local_sem)
        mine.start()
        copies = []
        for k in range(1, N_DEV):
            px, py, pc = x ^ ((k >> 2) & 1), y ^ ((k >> 1) & 1), c ^ (k & 1)
            peer = 4 * px + 2 * py + pc
            send = pltpu.make_async_remote_copy(
                src_ref=p_ref.at[peer], dst_ref=out_ref.at[my], send_sem=send_sems.at[k - 1],
                recv_sem=recv_sems.at[k - 1], device_id=(px, py, pc), device_id_type=_MESH)
            send.start()
            recv = pltpu.make_async_remote_copy(
                src_ref=p_ref.at[peer], dst_ref=out_ref.at[peer], send_sem=send_sems.at[k - 1],
                recv_sem=recv_sems.at[k - 1], device_id=(px, py, pc), device_id_type=_MESH)
            copies.append((send, recv))
        for send, recv in copies:
            recv.wait_recv()
        for send, recv in copies:
            send.wait_send()
        mine.wait()

    return pl.pallas_call(
        body, name=name, out_shape=jax.ShapeDtypeStruct(parts.shape, parts.dtype), in_specs=[_ANY], out_specs=_ANY,
        scratch_shapes=[pltpu.SemaphoreType.DMA((7,)), pltpu.SemaphoreType.DMA((7,)), pltpu.SemaphoreType.DMA],
    )(parts)


def _adamw(parts, row_off, w, m, v, *, name, tr):
    r, c = w.shape
    np_ = parts.shape[0]
    tr = min(tr, r)
    assert r % tr == 0 and row_off % tr == 0
    ob = row_off // tr
    c1 = 1.0 - ADAM_B1 ** ADAM_STEP
    c2 = 1.0 - ADAM_B2 ** ADAM_STEP

    def body(p_ref, w_ref, m_ref, v_ref, g_ref, d_ref, nm_ref, nv_ref):
        g = p_ref[0].astype(F32)
        for s in range(1, np_):
            g = g + p_ref[s].astype(F32)
        wv = w_ref[...]
        m2 = ADAM_B1 * m_ref[...] + (1.0 - ADAM_B1) * g
        v2 = ADAM_B2 * v_ref[...] + (1.0 - ADAM_B2) * jnp.square(g)
        m_hat = m2 / c1
        v_hat = v2 / c2
        g_ref[...] = g
        d_ref[...] = -ADAM_LR * (m_hat / (jnp.sqrt(v_hat) + ADAM_EPS) + ADAM_WD * wv)
        nm_ref[...] = m2
        nv_ref[...] = v2

    blk = pl.BlockSpec((tr, c), lambda i: (i, 0))
    return pl.pallas_call(
        body, name=name, grid=(r // tr,),
        in_specs=[pl.BlockSpec((np_, tr, c), lambda i: (0, ob + i, 0)), blk, blk, blk], out_specs=[blk] * 4,
        out_shape=[jax.ShapeDtypeStruct((r, c), F32)] * 4, compiler_params=_cp("parallel"),
    )(parts, w, m, v)


def _sum_parts(parts, *, name):
    np_, r, c = parts.shape

    def body(p_ref, o_ref):
        g = p_ref[0]
        for s in range(1, np_):
            g = g + p_ref[s]
        o_ref[...] = g

    return pl.pallas_call(body, name=name, out_shape=jax.ShapeDtypeStruct((r, c), F32))(parts)


def _pack(arrs):
    rows = []
    for a in arrs:
        f = a.reshape(-1).astype(F32)
        pad = (-f.shape[0]) % 128
        rows.append(jnp.pad(f, (0, pad)).reshape(-1, 128))
    out = jnp.concatenate(rows, axis=0)
    return jnp.pad(out, ((0, (-out.shape[0]) % 8), (0, 0)))


def _unpack(packed, shapes):
    outs, r0 = [], 0
    for shp in shapes:
        n = 1
        for d in shp:
            n *= d
        nr = -(-n // 128)
        outs.append(packed[r0:r0 + nr].reshape(-1)[:n].reshape(shp))
        r0 += nr
    return outs


_WIN_PIECES = ((0, 4096, 0), (4112, 8208, 0), (4096, 4104, HEAD_DIM - N_HEADS), (4104, 4112, HEAD_DIM - N_HEADS))


def _win_from_shards(shards):
    cols = []
    for lo, hi, pad in _WIN_PIECES:
        for j in range(N_DEV):
            a, b = max(lo, j * SHARD_IN), min(hi, (j + 1) * SHARD_IN)
            if a < b:
                cols.append(shards[j][:, a - j * SHARD_IN:b - j * SHARD_IN])
        if pad:
            cols.append(jnp.zeros((shards[0].shape[0], pad), shards[0].dtype))
    return jnp.concatenate(cols, axis=1)


def _win_to_shards(g):
    starts, off = [], 0
    for lo, hi, pad in _WIN_PIECES:
        starts.append((lo, hi, off))
        off += hi - lo + pad
    shards = []
    for j in range(N_DEV):
        cols = []
        for lo, hi, off in sorted(starts):
            a, b = max(lo, j * SHARD_IN), min(hi, (j + 1) * SHARD_IN)
            if a < b:
                cols.append(g[:, off + a - lo:off + b - lo])
        shards.append(jnp.concatenate(cols, axis=1))
    return shards


def _lower_bounds(logits):
    probs = jax.nn.softmax(logits.astype(F32), axis=0)
    return jnp.cumsum(probs, axis=0) - probs[0]


def _pad_lanes(vec8):
    return jnp.pad(vec8.reshape(1, N_HEADS), ((0, 0), (0, HEAD_DIM - N_HEADS)))


def kernel(x, p, norm_w, w_in, dn_conv_w, dn_A_log, dn_dt_bias, dn_norm_w, hg_lb_logits, hg_norm_w, w_out, w_ple_up, w_ple_gate, final_norm_w, loss_target, m_norm_w, m_w_in, m_dn_conv_w, m_dn_A_log, m_dn_dt_bias, m_dn_norm_w, m_hg_lb_logits, m_hg_norm_w, m_w_out, m_w_ple_up, m_w_ple_gate, m_final_norm_w, v_norm_w, v_w_in, v_dn_conv_w, v_dn_A_log, v_dn_dt_bias, v_dn_norm_w, v_hg_lb_logits, v_hg_norm_w, v_w_out, v_w_ple_up, v_w_ple_gate, v_final_norm_w):
    depth = norm_w.shape[0]
    my = 4 * lax.axis_index("x") + 2 * lax.axis_index("y") + lax.axis_index("c")
    h = x[0]
    tgt = loss_target[0]
    rows_out = D_MODEL // N_DEV
    up_rows = PLE_DIM * (D_MODEL // N_DEV) // D_MODEL

    win_all = _all_gather(w_in.astype(BF16), name="gather_w_in")
    rest_local = jnp.concatenate([w_out.reshape(depth * rows_out, D_MODEL), w_ple_gate.reshape(depth * rows_out, D_MODEL),
                                  w_ple_up.reshape(depth * up_rows, D_MODEL)], axis=0).astype(BF16)
    rest_all = _all_gather(rest_local, name="gather_w_rest")
    g_off = depth * rows_out
    u_off = 2 * depth * rows_out
    conv_all = _all_gather(dn_conv_w, name="gather_conv_w")
    conv_full = conv_all.transpose(1, 2, 0, 3).reshape(depth, CONV_W, 3 * BR_WIDTH)
    lbs = _lower_bounds(hg_lb_logits)

    def layer_weights(l):
        wi = _win_from_shards([win_all[j, l] for j in range(N_DEV)])
        wo = rest_all[:, l * rows_out:(l + 1) * rows_out].reshape(D_MODEL, D_MODEL)
        wg = rest_all[:, g_off + l * rows_out:g_off + (l + 1) * rows_out].reshape(D_MODEL, D_MODEL)
        wu = rest_all[:, u_off + l * up_rows:u_off + (l + 1) * up_rows].reshape(N_DEV, PLE_DIM, D_MODEL // N_DEV)
        wu = wu.transpose(1, 0, 2).reshape(PLE_DIM, D_MODEL)
        return wi, wo, wg, wu

    saved = []
    for l in range(depth):
        wi, wo, wg, wu = layer_weights(l)
        tag = f"l{l}"
        hn = _rms_fwd(h, norm_w[l], name=f"rms_fwd_{tag}")
        proj = _mm(hn, wi, mode="nn", out_dtype=F32, name=f"mm_proj_{tag}")
        al, dt = _pad_lanes(dn_A_log[l]), _pad_lanes(dn_dt_bias[l])
        qkv = _dn_qkv_fwd(proj, conv_full[l], name=f"dn_qkv_fwd_{tag}")
        beta, gcs = _dn_gate_fwd(proj, al, dt, name=f"dn_gate_fwd_{tag}")
        o_dn, st_dn = _dn_chunk_fwd(qkv, gcs, beta, name=f"dn_chunk_fwd_{tag}")
        lb = lbs[l].reshape(1, BR_WIDTH)
        qh, kh, lf = _hg_prep_fwd(proj, lb, name=f"hg_prep_fwd_{tag}")
        o_hg, st_hg = _hg_chunk_fwd(qh, kh, proj, lf, name=f"hg_chunk_fwd_{tag}")
        y_dn = _hnorm_fwd(o_dn, proj, C_Z, dn_norm_w[l], name=f"hnorm_dn_fwd_{tag}")
        y_hg = _hnorm_fwd(o_hg, proj, C_HZ, hg_norm_w[l], name=f"hnorm_hg_fwd_{tag}")
        y = jnp.concatenate([y_dn, y_hg], axis=1)
        h1 = _mm(y, wo, mode="nn", out_dtype=F32, res=h, name=f"mm_out_{tag}")
        gp = _mm(h1, wg, mode="nn", out_dtype=F32, name=f"mm_gate_{tag}")
        up = _mm(p[l, 0], wu, mode="nn", out_dtype=F32, name=f"mm_up_{tag}")
        h2 = _ple_fwd(h1, gp, up, name=f"ple_fwd_{tag}")
        saved.append(dict(h=h, hn=hn, proj=proj, qkv=qkv, beta=beta, gcs=gcs, st_dn=st_dn, qh=qh, kh=kh, lf=lf,
                          st_hg=st_hg, o_dn=o_dn, o_hg=o_hg, y=y, h1=h1, gp=gp, up=up, al=al, dt=dt, lb=lb))
        h = h2

    loss_row, dh, d_final_w = _final_fwd_bwd(h, final_norm_w, tgt, name="final_norm_loss")

    d_norm_w, d_alog, d_dt, d_dn_nw, d_hg_nw, d_lb, d_conv = ([None] * depth for _ in range(7))
    d_win, d_wo, d_wg, d_wu = ([None] * depth for _ in range(4))
    for l in reversed(range(depth)):
        wi, wo, wg, wu = layer_weights(l)
        sv = saved[l]
        tag = f"l{l}"
        dup, dgp = _ple_bwd(dh, sv["gp"], sv["up"], name=f"ple_bwd_{tag}")
        d_wu[l] = _mm(p[l, 0], dup, mode="tn", out_dtype=BF16, name=f"mm_dwup_{tag}")
        d_wg[l] = _mm(sv["h1"], dgp, mode="tn", out_dtype=BF16, name=f"mm_dwgate_{tag}")
        dh1 = _mm(dgp, wg, mode="nt", out_dtype=F32, res=dh, name=f"mm_dh1_{tag}")
        d_wo[l] = _mm(sv["y"], dh1, mode="tn", out_dtype=BF16, name=f"mm_dwout_{tag}")
        dy = _mm(dh1, wo, mode="nt", out_dtype=F32, name=f"mm_dy_{tag}")
        do_dn, dz_dn, d_dn_nw[l] = _hnorm_bwd(sv["o_dn"], sv["proj"], C_Z, dn_norm_w[l], dy, 0, name=f"hnorm_dn_bwd_{tag}")
        do_hg, dz_hg, d_hg_nw[l] = _hnorm_bwd(sv["o_hg"], sv["proj"], C_HZ, hg_norm_w[l], dy, BR_WIDTH, name=f"hnorm_hg_bwd_{tag}")
        dqkv, d_gc, dbeta = _dn_chunk_bwd(sv["qkv"], sv["gcs"], sv["beta"], sv["st_dn"], do_dn, name=f"dn_chunk_bwd_{tag}")
        dqkv_pre, d_conv[l] = _dn_qkv_bwd(sv["proj"], conv_full[l], dqkv, name=f"dn_qkv_bwd_{tag}")
        db, da, d_alog[l], d_dt[l] = _dn_gate_bwd(sv["proj"], sv["al"], sv["dt"], dbeta, d_gc, name=f"dn_gate_bwd_{tag}")
        dqh, dkh, dhi, dlf = _hg_chunk_bwd(sv["qh"], sv["kh"], sv["proj"], sv["lf"], sv["st_hg"], do_hg, name=f"hg_chunk_bwd_{tag}")
        dhq, dhf, d_lb[l] = _hg_prep_bwd(sv["proj"], sv["lb"], dqh, dkh, dlf, name=f"hg_prep_bwd_{tag}")
        dproj = jnp.concatenate([dqkv_pre, dz_dn, dhq, dhf, dhi, dz_hg, db, da], axis=1)
        d_win[l] = _mm(sv["hn"], dproj, mode="tn", out_dtype=BF16, name=f"mm_dwin_{tag}")
        dhn = _mm(dproj, wi, mode="nt", out_dtype=F32, name=f"mm_dhn_{tag}")
        dh, d_norm_w[l] = _rms_bwd(sv["h"], norm_w[l], dhn, dh1, name=f"rms_bwd_{tag}")
    grad_x = dh[None]

    parts_in = jnp.stack([jnp.stack(_win_to_shards(d_win[l])) for l in range(depth)],
                         axis=1).reshape(N_DEV, depth * D_MODEL, SHARD_IN)
    land_in = _exchange(parts_in, name="exchange_dw_in")
    parts_rest = jnp.concatenate(
        [d_wo[l].reshape(N_DEV, rows_out, D_MODEL) for l in range(depth)]
        + [d_wg[l].reshape(N_DEV, rows_out, D_MODEL) for l in range(depth)]
        + [d_wu[l].reshape(PLE_DIM, N_DEV, D_MODEL // N_DEV).transpose(1, 0, 2).reshape(N_DEV, up_rows, D_MODEL) for l in range(depth)],
        axis=1)
    land_rest = _exchange(parts_rest, name="exchange_dw_rest")

    def big(land, off, w, m, v, name, tr):
        shp = w.shape
        c = land.shape[-1]
        outs = _adamw(land, off, w.reshape(-1, c), m.reshape(-1, c), v.reshape(-1, c), name=name, tr=tr)
        return [o.reshape(shp) for o in outs]

    r_win = big(land_in, 0, w_in, m_w_in, v_w_in, "adamw_w_in", 256)
    r_wo = big(land_rest, 0, w_out, m_w_out, v_w_out, "adamw_w_out", 256)
    r_wg = big(land_rest, g_off, w_ple_gate, m_w_ple_gate, v_w_ple_gate, "adamw_w_gate", 256)
    r_wu = big(land_rest, u_off, w_ple_up, m_w_ple_up, v_w_ple_up, "adamw_w_up", depth * up_rows)

    small_shapes = [(1, 128), norm_w.shape, final_norm_w.shape, dn_A_log.shape, dn_dt_bias.shape, dn_norm_w.shape,
                    hg_norm_w.shape, hg_lb_logits.shape, (depth, CONV_W, 3 * BR_WIDTH)]
    small = _pack([loss_row, jnp.concatenate(d_norm_w, axis=0), d_final_w,
                   jnp.stack([a[0, :N_HEADS] for a in d_alog]), jnp.stack([a[0, :N_HEADS] for a in d_dt]),
                   jnp.concatenate(d_dn_nw, axis=0), jnp.concatenate(d_hg_nw, axis=0), jnp.concatenate(d_lb, axis=0),
                   jnp.stack(d_conv)])
    tot = _unpack(_sum_parts(_all_gather(small, name="gather_small"), name="sum_small"), small_shapes)
    loss = tot[0][0, 0]
    g_lb = tot[7]
    g_logits = jax.vjp(_lower_bounds, hg_lb_logits)[1](g_lb)[0]
    g_conv = lax.dynamic_slice_in_dim(tot[8], my * (3 * BR_WIDTH // N_DEV), 3 * BR_WIDTH // N_DEV, axis=2)
    small_g = [tot[1], g_conv, tot[3], tot[4], tot[5], g_logits, tot[6], tot[2]]
    small_w = [norm_w, dn_conv_w, dn_A_log, dn_dt_bias, dn_norm_w, hg_lb_logits, hg_norm_w, final_norm_w]
    small_m = [m_norm_w, m_dn_conv_w, m_dn_A_log, m_dn_dt_bias, m_dn_norm_w, m_hg_lb_logits, m_hg_norm_w, m_final_norm_w]
    small_v = [v_norm_w, v_dn_conv_w, v_dn_A_log, v_dn_dt_bias, v_dn_norm_w, v_hg_lb_logits, v_hg_norm_w, v_final_norm_w]
    pk_w = _pack(small_w)
    res_small = _adamw(_pack(small_g)[None], 0, pk_w, _pack(small_m), _pack(small_v), name="adamw_small", tr=pk_w.shape[0])
    shapes_w = [a.shape for a in small_w]
    sg, sd, sm, sv_ = (_unpack(r, shapes_w) for r in res_small)

    def order(small_list, big_in, big_out, big_up, big_gate):
        nw, cw, al_, dt_, dnw, lbl, hnw, fw = small_list
        return [nw, big_in, cw, al_, dt_, dnw, lbl, hnw, big_out, big_up, big_gate, fw]

    outs = [loss, grad_x]
    for i, sl in enumerate((sg, sd, sm, sv_)):
        outs += order(sl, r_win[i], r_wo[i], r_wu[i], r_wg[i])
    return tuple(outs)
```

```python
import functools

import jax
import jax.numpy as jnp
from jax import lax
from jax.experimental import pallas as pl
from jax.experimental.pallas import tpu as pltpu

F32 = jnp.float32
BF16 = jnp.bfloat16
HIGHEST = lax.Precision.HIGHEST

N_DEV = 8
D_MODEL = 2048
PLE_DIM = 256
HEAD_DIM = 128
N_HEADS = 8
BR_WIDTH = N_HEADS * HEAD_DIM
CHUNK = 64
SUB = 16
CONV_W = 4
NORM_EPS = 1e-6
L2_EPS = 1e-6
IN_WIDTH = 8208
SHARD_IN = IN_WIDTH // N_DEV
EXP_CLAMP = 80.0

C_QKV, C_Z, C_HQ, C_HF, C_HI, C_HZ, C_B, C_A, N_PROJ = 0, 3072, 4096, 5120, 6144, 7168, 8192, 8320, 8448

ADAM_LR, ADAM_B1, ADAM_B2, ADAM_EPS, ADAM_WD, ADAM_STEP = 0.001, 0.9, 0.999, 1e-08, 0.01, 10

VMEM_LIMIT = 48 * 1024 * 1024


def _cp(*sem):
    return pltpu.CompilerParams(dimension_semantics=sem, vmem_limit_bytes=VMEM_LIMIT)


class _Heads:
    def __init__(self, vals):
        self.v = tuple(vals)

    def __add__(self, o):
        return _hmap(lambda a, b: a + b, self, o)

    def __radd__(self, o):
        return _hmap(lambda a, b: b + a, self, o)

    def __sub__(self, o):
        return _hmap(lambda a, b: a - b, self, o)

    def __rsub__(self, o):
        return _hmap(lambda a, b: b - a, self, o)

    def __mul__(self, o):
        return _hmap(lambda a, b: a * b, self, o)

    def __rmul__(self, o):
        return _hmap(lambda a, b: b * a, self, o)

    def __neg__(self):
        return _hmap(lambda a: -a, self)

    def __getitem__(self, idx):
        return _hmap(lambda a: a[idx], self)


def _hmap(fn, *args):
    n = next((len(a.v) for a in args if isinstance(a, _Heads)), None)
    if n is None:
        return fn(*args)
    return _Heads(fn(*[a.v[i] if isinstance(a, _Heads) else a for a in args]) for i in range(n))


def _dot(a, b, ca, cb):
    return _hmap(lambda x, y: lax.dot_general(x.astype(BF16), y.astype(BF16), (((ca,), (cb,)), ((), ())),
                                              preferred_element_type=F32), a, b)


def _nn(a, b):
    return _dot(a, b, 1, 0)


def _nt(a, b):
    return _dot(a, b, 1, 1)


def _tn(a, b):
    return _dot(a, b, 0, 0)


def _split(a):
    hi = _hmap(lambda x: x.astype(BF16), a)
    return hi, _hmap(lambda x, h: (x - h.astype(F32)).astype(BF16), a, hi)


def _dot3(a, b, ca, cb):
    ah, al = _split(a)
    bh, bl = _split(b)
    return _dot(ah, bh, ca, cb) + (_dot(ah, bl, ca, cb) + _dot(al, bh, ca, cb))


def _nn_exact(a, b):
    return _hmap(lambda y: lax.dot_general(a, y, (((1,), (0,)), ((), ())), precision=HIGHEST,
                                           preferred_element_type=F32), b)


def _exp(x):
    return _hmap(jnp.exp, x)


def _sum(x, axis):
    return _hmap(lambda a: jnp.sum(a, axis=axis, keepdims=True), x)


def _stack_rows(parts):
    return _hmap(lambda *xs: jnp.concatenate(xs, axis=0), *parts)


def _sigmoid(x):
    return jax.nn.sigmoid(x)


def _silu(x):
    return x * _sigmoid(x)


def _dsilu(x):
    s = _sigmoid(x)
    return s * (1.0 + x * (1.0 - s))


def _softplus(x):
    return jnp.maximum(x, 0.0) + jnp.log(1.0 + jnp.exp(-jnp.abs(x)))


def _iota2(n, m, axis):
    return lax.broadcasted_iota(jnp.int32, (n, m), axis)


def _col2row(col, eye):
    return _hmap(lambda c: jnp.sum(eye * c, axis=0, keepdims=True), col)


def _row2col(row, eye):
    return _hmap(lambda r: jnp.sum(eye * r, axis=1, keepdims=True), row)


def _pick_lane(block, lane_idx):
    lane = _iota2(block.shape[0], block.shape[1], 1)
    return jnp.sum(jnp.where(lane == lane_idx, block, 0.0), axis=1, keepdims=True)


MM_TILE_M, MM_TILE_N, MM_TILE_K = 1024, 1408, 2048


def _tile(dim, cap):
    if dim <= cap:
        return dim
    t = cap - cap % 128
    while dim % t:
        t -= 128
    return t


def _mm(a, b, *, mode, out_dtype, res=None, name):
    if mode == "nn":
        (m, kd), (_, n) = a.shape, b.shape
    elif mode == "nt":
        (m, kd), (n, _) = a.shape, b.shape
    else:
        (kd, m), (_, n) = a.shape, b.shape
    tm, tn, tk = _tile(m, MM_TILE_M), _tile(n, MM_TILE_N), _tile(kd, MM_TILE_K)
    assert m % tm == 0 and n % tn == 0 and kd % tk == 0, (m, n, kd, tm, tn, tk)
    nk = kd // tk
    ca, cb = {"nn": (1, 0), "nt": (1, 1), "tn": (0, 0)}[mode]

    def body(*refs):
        if res is None:
            a_ref, b_ref, o_ref, acc_ref = refs
            r_ref = None
        else:
            a_ref, b_ref, r_ref, o_ref, acc_ref = refs
        k = pl.program_id(2)

        @pl.when(k == 0)
        def _():
            acc_ref[...] = jnp.zeros_like(acc_ref)

        acc_ref[...] += _dot(a_ref[...], b_ref[...], ca, cb)

        @pl.when(k == nk - 1)
        def _():
            out = acc_ref[...]
            if r_ref is not None:
                out = out + r_ref[...].astype(F32)
            o_ref[...] = out.astype(o_ref.dtype)

    a_spec = pl.BlockSpec((tk, tm), lambda i, j, k: (k, i)) if mode == "tn" else pl.BlockSpec((tm, tk), lambda i, j, k: (i, k))
    b_spec = pl.BlockSpec((tn, tk), lambda i, j, k: (j, k)) if mode == "nt" else pl.BlockSpec((tk, tn), lambda i, j, k: (k, j))
    o_spec = pl.BlockSpec((tm, tn), lambda i, j, k: (i, j))
    in_specs = [a_spec, b_spec] + ([o_spec] if res is not None else [])
    args = (a, b) + ((res,) if res is not None else ())
    return pl.pallas_call(
        body, name=name, grid=(m // tm, n // tn, nk), in_specs=in_specs, out_specs=o_spec,
        out_shape=jax.ShapeDtypeStruct((m, n), out_dtype),
        scratch_shapes=[pltpu.VMEM((tm, tn), F32)],
        compiler_params=_cp("parallel", "parallel", "arbitrary"),
    )(*args)


ROW_TILE = 256


def _rms_fwd(h, w, *, name):
    s, d = h.shape
    tr = min(ROW_TILE, s)

    def body(h_ref, w_ref, o_ref):
        x = h_ref[...]
        r = lax.rsqrt(jnp.mean(x * x, axis=-1, keepdims=True) + NORM_EPS)
        o_ref[...] = (x * r * w_ref[...]).astype(o_ref.dtype)

    return pl.pallas_call(
        body, name=name, grid=(s // tr,),
        in_specs=[pl.BlockSpec((tr, d), lambda i: (i, 0)), pl.BlockSpec((1, d), lambda i: (0, 0))],
        out_specs=pl.BlockSpec((tr, d), lambda i: (i, 0)),
        out_shape=jax.ShapeDtypeStruct((s, d), BF16), compiler_params=_cp("parallel"),
    )(h, w.reshape(1, d))


def _rms_bwd_math(x, w, dy):
    d = x.shape[-1]
    r = lax.rsqrt(jnp.mean(x * x, axis=-1, keepdims=True) + NORM_EPS)
    gw = dy * w
    dx = r * gw - x * ((r * r * r) * (jnp.sum(gw * x, axis=-1, keepdims=True) / d))
    return dx, dy * x * r


def _rms_bwd(h, w, dhn, res, *, name):
    s, d = h.shape
    tr = min(ROW_TILE, s)

    def body(h_ref, w_ref, g_ref, r_ref, dh_ref, dw_ref):
        @pl.when(pl.program_id(0) == 0)
        def _():
            dw_ref[...] = jnp.zeros_like(dw_ref)

        dx, dwt = _rms_bwd_math(h_ref[...], w_ref[...], g_ref[...])
        dh_ref[...] = r_ref[...] + dx
        dw_ref[...] += jnp.sum(dwt, axis=0, keepdims=True)

    row = pl.BlockSpec((tr, d), lambda i: (i, 0))
    vec = pl.BlockSpec((1, d), lambda i: (0, 0))
    return pl.pallas_call(
        body, name=name, grid=(s // tr,), in_specs=[row, vec, row, row], out_specs=[row, vec],
        out_shape=[jax.ShapeDtypeStruct((s, d), F32), jax.ShapeDtypeStruct((1, d), F32)],
        compiler_params=_cp("arbitrary"),
    )(h, w.reshape(1, d), dhn, res)


def _final_fwd_bwd(h, w, tgt, *, name):
    s, d = h.shape
    tr = min(ROW_TILE, s)

    def body(h_ref, w_ref, t_ref, loss_ref, dh_ref, dw_ref):
        @pl.when(pl.program_id(0) == 0)
        def _():
            loss_ref[...] = jnp.zeros_like(loss_ref)
            dw_ref[...] = jnp.zeros_like(dw_ref)

        x = h_ref[...]
        wv = w_ref[...]
        r = lax.rsqrt(jnp.mean(x * x, axis=-1, keepdims=True) + NORM_EPS)
        err = x * r * wv - t_ref[...]
        row_loss = jnp.mean(err * err, axis=-1, keepdims=True)
        loss_ref[...] += 0.5 * jnp.sum(row_loss, axis=0, keepdims=True)
        dx, dwt = _rms_bwd_math(x, wv, err / d)
        dh_ref[...] = dx
        dw_ref[...] += jnp.sum(dwt, axis=0, keepdims=True)

    row = pl.BlockSpec((tr, d), lambda i: (i, 0))
    vec = pl.BlockSpec((1, d), lambda i: (0, 0))
    return pl.pallas_call(
        body, name=name, grid=(s // tr,), in_specs=[row, vec, row],
        out_specs=[pl.BlockSpec((1, 128), lambda i: (0, 0)), row, vec],
        out_shape=[jax.ShapeDtypeStruct((1, 128), F32), jax.ShapeDtypeStruct((s, d), F32),
                   jax.ShapeDtypeStruct((1, d), F32)],
        compiler_params=_cp("arbitrary"),
    )(h, w.reshape(1, d), tgt)


def _ple_fwd(h1, gate_pre, up, *, name):
    s, d = h1.shape
    tr = min(ROW_TILE, s)

    def body(h_ref, g_ref, u_ref, o_ref):
        o_ref[...] = h_ref[...] + u_ref[...] * _sigmoid(g_ref[...])

    row = pl.BlockSpec((tr, d), lambda i: (i, 0))
    return pl.pallas_call(body, name=name, grid=(s // tr,), in_specs=[row, row, row], out_specs=row,
                          out_shape=jax.ShapeDtypeStruct((s, d), F32), compiler_params=_cp("parallel"))(h1, gate_pre, up)


def _ple_bwd(dh2, gate_pre, up, *, name):
    s, d = dh2.shape
    tr = min(ROW_TILE, s)

    def body(d_ref, g_ref, u_ref, dup_ref, dgp_ref):
        dh = d_ref[...]
        gate = _sigmoid(g_ref[...])
        dup_ref[...] = (dh * gate).astype(BF16)
        dgp_ref[...] = (dh * u_ref[...] * gate * (1.0 - gate)).astype(BF16)

    row = pl.BlockSpec((tr, d), lambda i: (i, 0))
    return pl.pallas_call(body, name=name, grid=(s // tr,), in_specs=[row, row, row], out_specs=[row, row],
                          out_shape=[jax.ShapeDtypeStruct((s, d), BF16)] * 2, compiler_params=_cp("parallel"))(dh2, gate_pre, up)


HN_TILE = 512


def _hnorm_fwd(o, proj, z_col, w, *, name):
    s = o.shape[0]
    tr = min(HN_TILE, s)
    zb = z_col // HEAD_DIM

    def body(o_ref, z_ref, w_ref, y_ref):
        x = o_ref[...]
        r = lax.rsqrt(jnp.mean(x * x, axis=-1, keepdims=True) + NORM_EPS)
        y_ref[...] = (x * r * w_ref[...] * _silu(z_ref[...])).astype(BF16)

    blk = pl.BlockSpec((tr, HEAD_DIM), lambda i, h: (i, h))
    return pl.pallas_call(
        body, name=name, grid=(s // tr, N_HEADS),
        in_specs=[blk, pl.BlockSpec((tr, HEAD_DIM), lambda i, h: (i, zb + h)), pl.BlockSpec((1, HEAD_DIM), lambda i, h: (0, 0))],
        out_specs=blk, out_shape=jax.ShapeDtypeStruct((s, BR_WIDTH), BF16), compiler_params=_cp("parallel", "parallel"),
    )(o, proj, w.reshape(1, HEAD_DIM))


def _hnorm_bwd(o, proj, z_col, w, dy, dy_col, *, name):
    s = o.shape[0]
    tr = min(HN_TILE, s)
    zb, yb = z_col // HEAD_DIM, dy_col // HEAD_DIM

    def body(o_ref, z_ref, w_ref, dy_ref, do_ref, dz_ref, dw_ref):
        @pl.when((pl.program_id(0) == 0) & (pl.program_id(1) == 0))
        def _():
            dw_ref[...] = jnp.zeros_like(dw_ref)

        x, z, wv, g = o_ref[...], z_ref[...], w_ref[...], dy_ref[...]
        r = lax.rsqrt(jnp.mean(x * x, axis=-1, keepdims=True) + NORM_EPS)
        on = x * r * wv
        don = g * _silu(z)
        dz_ref[...] = (g * on * _dsilu(z)).astype(BF16)
        gw = don * wv
        do_ref[...] = r * gw - x * ((r * r * r) * (jnp.sum(gw * x, axis=-1, keepdims=True) / HEAD_DIM))
        dw_ref[...] += jnp.sum(don * x * r, axis=0, keepdims=True)

    blk = pl.BlockSpec((tr, HEAD_DIM), lambda i, h: (i, h))
    vec = pl.BlockSpec((1, HEAD_DIM), lambda i, h: (0, 0))
    return pl.pallas_call(
        body, name=name, grid=(s // tr, N_HEADS),
        in_specs=[blk, pl.BlockSpec((tr, HEAD_DIM), lambda i, h: (i, zb + h)), vec,
                  pl.BlockSpec((tr, HEAD_DIM), lambda i, h: (i, yb + h))],
        out_specs=[blk, blk, vec],
        out_shape=[jax.ShapeDtypeStruct((s, BR_WIDTH), F32), jax.ShapeDtypeStruct((s, BR_WIDTH), BF16),
                   jax.ShapeDtypeStruct((1, HEAD_DIM), F32)],
        compiler_params=_cp("arbitrary", "arbitrary"),
    )(o, proj, w.reshape(1, HEAD_DIM), dy)


def _conv_silu(x, w, s):
    row = _iota2(s, x.shape[1], 0)
    c = w[CONV_W - 1:CONV_W, :] * x
    for k in range(1, CONV_W):
        c = c + w[CONV_W - 1 - k:CONV_W - k, :] * jnp.where(row >= k, pltpu.roll(x, k, 0), 0.0)
    return c


def _dn_qkv_fwd(proj, conv_w, *, name):
    s = proj.shape[0]
    nb = 3 * N_HEADS

    def body(x_ref, w_ref, o_ref):
        j = pl.program_id(0)
        sv = _silu(_conv_silu(x_ref[...], w_ref[...], s))
        r = lax.rsqrt(jnp.sum(sv * sv, axis=-1, keepdims=True) + L2_EPS)
        scale = jnp.where(j < N_HEADS, HEAD_DIM ** -0.5, 1.0).astype(F32)
        o_ref[...] = jnp.where(j < 2 * N_HEADS, sv * r * scale, sv)

    return pl.pallas_call(
        body, name=name, grid=(nb,),
        in_specs=[pl.BlockSpec((s, HEAD_DIM), lambda j: (0, j)), pl.BlockSpec((CONV_W, HEAD_DIM), lambda j: (0, j))],
        out_specs=pl.BlockSpec((s, HEAD_DIM), lambda j: (0, j)),
        out_shape=jax.ShapeDtypeStruct((s, 3 * BR_WIDTH), F32), compiler_params=_cp("parallel"),
    )(proj, conv_w)


def _dn_qkv_bwd(proj, conv_w, dqkv, *, name):
    s = proj.shape[0]
    nb = 3 * N_HEADS

    def body(x_ref, w_ref, g_ref, dx_ref, dw_ref):
        j = pl.program_id(0)
        x, w, g = x_ref[...], w_ref[...], g_ref[...]
        c = _conv_silu(x, w, s)
        sv = _silu(c)
        r = lax.rsqrt(jnp.sum(sv * sv, axis=-1, keepdims=True) + L2_EPS)
        scale = jnp.where(j < N_HEADS, HEAD_DIM ** -0.5, 1.0).astype(F32)
        ds_n = scale * (r * g - sv * ((r * r * r) * jnp.sum(g * sv, axis=-1, keepdims=True)))
        dc = jnp.where(j < 2 * N_HEADS, ds_n, g) * _dsilu(c)
        row = _iota2(s, HEAD_DIM, 0)
        dx = w[CONV_W - 1:CONV_W, :] * dc
        dws = [jnp.sum(dc * x, axis=0, keepdims=True)]
        for k in range(1, CONV_W):
            dx = dx + w[CONV_W - 1 - k:CONV_W - k, :] * jnp.where(row < s - k, pltpu.roll(dc, s - k, 0), 0.0)
            dws.append(jnp.sum(dc * jnp.where(row >= k, pltpu.roll(x, k, 0), 0.0), axis=0, keepdims=True))
        dx_ref[...] = dx.astype(BF16)
        for k in range(CONV_W):
            dw_ref[CONV_W - 1 - k:CONV_W - k, :] = dws[k]

    blk = pl.BlockSpec((s, HEAD_DIM), lambda j: (0, j))
    wblk = pl.BlockSpec((CONV_W, HEAD_DIM), lambda j: (0, j))
    return pl.pallas_call(
        body, name=name, grid=(nb,), in_specs=[blk, wblk, blk], out_specs=[blk, wblk],
        out_shape=[jax.ShapeDtypeStruct((s, 3 * BR_WIDTH), BF16), jax.ShapeDtypeStruct((CONV_W, 3 * BR_WIDTH), F32)],
        compiler_params=_cp("parallel"),
    )(proj, conv_w, dqkv)


def _tri(n, kind):
    r, c = _iota2(n, n, 0), _iota2(n, n, 1)
    if kind == "lower":
        return (r >= c).astype(F32)
    if kind == "upper":
        return (r <= c).astype(F32)
    return (r == c).astype(F32)


def _dn_gate_fwd(proj, a_log, dt_bias, *, name):
    s = proj.shape[0]

    def body(b_ref, a_ref, al_ref, dt_ref, beta_ref, g_ref):
        beta_ref[...] = _sigmoid(b_ref[...])
        g = -jnp.exp(al_ref[...]) * _softplus(a_ref[...] + dt_ref[...])
        g_ref[...] = _nn_exact(_tri(CHUNK, "lower"), g)

    blk = lambda cb: pl.BlockSpec((CHUNK, HEAD_DIM), lambda i: (i, cb))
    vec = pl.BlockSpec((1, HEAD_DIM), lambda i: (0, 0))
    out = pl.BlockSpec((CHUNK, HEAD_DIM), lambda i: (i, 0))
    return pl.pallas_call(
        body, name=name, grid=(s // CHUNK,), in_specs=[blk(C_B // HEAD_DIM), blk(C_A // HEAD_DIM), vec, vec],
        out_specs=[out, out], out_shape=[jax.ShapeDtypeStruct((s, HEAD_DIM), F32)] * 2, compiler_params=_cp("parallel"),
    )(proj, proj, a_log, dt_bias)


def _dn_gate_bwd(proj, a_log, dt_bias, dbeta, d_g, *, name):
    s = proj.shape[0]

    def body(b_ref, a_ref, al_ref, dt_ref, dbeta_ref, dG_ref, db_ref, da_ref, dal_ref, ddt_ref):
        @pl.when(pl.program_id(0) == 0)
        def _():
            dal_ref[...] = jnp.zeros_like(dal_ref)
            ddt_ref[...] = jnp.zeros_like(ddt_ref)

        beta = _sigmoid(b_ref[...])
        db_ref[...] = (dbeta_ref[...] * beta * (1.0 - beta)).astype(BF16)
        pre = a_ref[...] + dt_ref[...]
        neg_ea = -jnp.exp(al_ref[...])
        dg = _nn_exact(_tri(CHUNK, "upper"), dG_ref[...])
        da = dg * neg_ea * _sigmoid(pre)
        da_ref[...] = da.astype(BF16)
        ddt_ref[...] += jnp.sum(da, axis=0, keepdims=True)
        dal_ref[...] += jnp.sum(dg * neg_ea * _softplus(pre), axis=0, keepdims=True)

    blk = lambda cb: pl.BlockSpec((CHUNK, HEAD_DIM), lambda i: (i, cb))
    vec = pl.BlockSpec((1, HEAD_DIM), lambda i: (0, 0))
    io = pl.BlockSpec((CHUNK, HEAD_DIM), lambda i: (i, 0))
    return pl.pallas_call(
        body, name=name, grid=(s // CHUNK,),
        in_specs=[blk(C_B // HEAD_DIM), blk(C_A // HEAD_DIM), vec, vec, io, io], out_specs=[io, io, vec, vec],
        out_shape=[jax.ShapeDtypeStruct((s, HEAD_DIM), BF16)] * 2 + [jax.ShapeDtypeStruct((1, HEAD_DIM), F32)] * 2,
        compiler_params=_cp("arbitrary"),
    )(proj, proj, a_log, dt_bias, dbeta, d_g)


def _unit_lower_inverse(a_strict, eye):
    x = -a_strict
    t = x + eye
    p = x
    n = 2
    while n < CHUNK:
        p = _nn(p, p)
        t = t + _nn(t, p)
        n *= 2
    return t


def _dn_chunk_common(q, k, v, gc, beta, st):
    c = CHUNK
    eye = _tri(c, "eye")
    low = _tri(c, "lower")
    strict = low - eye
    grow = _col2row(gc, eye)
    dec = _hmap(lambda g_, gr: low * jnp.exp(low * (g_ - gr)), gc, grow)
    kb = k * beta
    a_mat = _nt(kb, k) * dec * strict
    t_inv = _unit_lower_inverse(a_mat, eye)
    e_g = _exp(gc)
    u = _nn(t_inv, v * beta)
    w = _nn(t_inv, kb * e_g)
    p_qk = _nt(q, k)
    qk = p_qk * dec
    qd = q * e_g
    last = (_iota2(c, 1, 0) == c - 1).astype(F32)
    g_last = _sum(gc * last, 0)
    e_t = _exp(g_last - gc)
    kt = k * e_t
    tail = _exp(g_last)
    vn = u - _nn(w, st)
    return dict(eye=eye, low=low, strict=strict, dec=dec, kb=kb, a_mat=a_mat, t_inv=t_inv, e_g=e_g, u=u, w=w,
                qk=qk, qd=qd, last=last, e_t=e_t, kt=kt, tail=tail, vn=vn)


def _dn_chunk_fwd_math(q, k, v, gc, beta, st):
    m = _dn_chunk_common(q, k, v, gc, beta, st)
    o = _nn(m["qd"], st) + _nn(m["qk"], m["vn"])
    st2 = st * m["tail"] + _tn(m["kt"], m["vn"])
    return o, st2


def _dn_chunk_bwd_math(q, k, v, gc, beta, st, do, dst2):
    m = _dn_chunk_common(q, k, v, gc, beta, st)
    eye, low, strict = m["eye"], m["low"], m["strict"]
    dvn = _tn(m["qk"], do) + _nn(m["kt"], dst2)
    dqk = _nt(do, m["vn"]) * low
    dqd = _nt(do, st)
    dst = _tn(m["qd"], do) + dst2 * m["tail"] - _tn(m["w"], dvn)
    dkt = _nt(m["vn"], dst2)
    dtail = _sum(_sum(st * dst2, 1), 0)
    dw = -_nt(dvn, st)
    dvb = _tn(m["t_inv"], dvn)
    dkg = _tn(m["t_inv"], dw)
    d_a = (_nt(dvb, m["u"]) + _nt(dkg, m["w"])) * (-strict)
    dkk = d_a * m["dec"]
    dp = dqk * m["dec"]
    dq = _nn(dp, k) + dqd * m["e_g"]
    dkb = _nn(dkk, k) + dkg * m["e_g"]
    dk = _tn(dp, q) + _tn(dkk, m["kb"]) + dkb * beta + dkt * m["e_t"]
    dv = dvb * beta
    dbeta = _sum(dvb * v + dkb * k, 1)
    de_g = _sum(dkg * m["kb"] + dqd * q, 1)
    de_t = _sum(dkt * k, 1)
    mm = d_a * m["a_mat"] + dqk * m["qk"]
    dgc = (_sum(mm, 1) - _row2col(_sum(mm, 0), eye) + de_g * m["e_g"] - de_t * m["e_t"]
           + (_sum(de_t * m["e_t"], 0) + dtail * m["tail"]) * m["last"])
    return dq, dk, dv, dgc, dbeta, dst


def _heads_of(ref):
    return _Heads(ref[:, h * HEAD_DIM:(h + 1) * HEAD_DIM] for h in range(N_HEADS))


def _lanes_of(block):
    return _Heads(_pick_lane(block, h) for h in range(N_HEADS))


def _dn_chunk_fwd(qkv, gcs, beta, *, name):
    s = qkv.shape[0]
    n = s // CHUNK

    def body(q_ref, k_ref, v_ref, g_ref, b_ref, o_ref, st_out_ref, st_ref):
        @pl.when(pl.program_id(0) == 0)
        def _():
            st_ref[...] = jnp.zeros_like(st_ref)

        gblk, bblk = g_ref[...], b_ref[...]
        st = _Heads(st_ref[h] for h in range(N_HEADS))
        o, st2 = _dn_chunk_fwd_math(_heads_of(q_ref), _heads_of(k_ref), _heads_of(v_ref), _lanes_of(gblk),
                                    _lanes_of(bblk), st)
        for h in range(N_HEADS):
            st_out_ref[0, h] = st.v[h]
            o_ref[:, h * HEAD_DIM:(h + 1) * HEAD_DIM] = o.v[h]
            st_ref[h] = st2.v[h]

    blk = lambda off: pl.BlockSpec((CHUNK, BR_WIDTH), lambda c: (c, off))
    sc = pl.BlockSpec((CHUNK, HEAD_DIM), lambda c: (c, 0))
    return pl.pallas_call(
        body, name=name, grid=(n,),
        in_specs=[blk(0), blk(1), blk(2), sc, sc],
        out_specs=[blk(0), pl.BlockSpec((1, N_HEADS, HEAD_DIM, HEAD_DIM), lambda c: (c, 0, 0, 0))],
        out_shape=[jax.ShapeDtypeStruct((s, BR_WIDTH), F32), jax.ShapeDtypeStruct((n, N_HEADS, HEAD_DIM, HEAD_DIM), F32)],
        scratch_shapes=[pltpu.VMEM((N_HEADS, HEAD_DIM, HEAD_DIM), F32)],
        compiler_params=_cp("arbitrary"),
    )(qkv, qkv, qkv, gcs, beta)


def _dn_chunk_bwd(qkv, gcs, beta, states, do, *, name):
    s = qkv.shape[0]
    n = s // CHUNK

    def body(q_ref, k_ref, v_ref, g_ref, b_ref, st_in_ref, do_ref, dqkv_ref, dg_ref, dbeta_ref, dst_ref):
        @pl.when(pl.program_id(0) == 0)
        def _():
            dst_ref[...] = jnp.zeros_like(dst_ref)

        gblk, bblk = g_ref[...], b_ref[...]
        lane = _iota2(CHUNK, HEAD_DIM, 1)
        dg_all = jnp.zeros((CHUNK, HEAD_DIM), F32)
        dbeta_all = jnp.zeros((CHUNK, HEAD_DIM), F32)
        dq, dk, dv, dgc, dbeta, dst = _dn_chunk_bwd_math(
            _heads_of(q_ref), _heads_of(k_ref), _heads_of(v_ref), _lanes_of(gblk), _lanes_of(bblk),
            _Heads(st_in_ref[0, h] for h in range(N_HEADS)), _heads_of(do_ref),
            _Heads(dst_ref[h] for h in range(N_HEADS)))
        for h in range(N_HEADS):
            for part, val in enumerate((dq, dk, dv)):
                c0 = part * BR_WIDTH + h * HEAD_DIM
                dqkv_ref[:, c0:c0 + HEAD_DIM] = val.v[h]
            dg_all = jnp.where(lane == h, dgc.v[h], dg_all)
            dbeta_all = jnp.where(lane == h, dbeta.v[h], dbeta_all)
            dst_ref[h] = dst.v[h]
        dg_ref[...] = dg_all
        dbeta_ref[...] = dbeta_all

    blk = lambda off: pl.BlockSpec((CHUNK, BR_WIDTH), lambda c: (n - 1 - c, off))
    sc = pl.BlockSpec((CHUNK, HEAD_DIM), lambda c: (n - 1 - c, 0))
    outs = pl.pallas_call(
        body, name=name, grid=(n,),
        in_specs=[blk(0), blk(1), blk(2), sc, sc,
                  pl.BlockSpec((1, N_HEADS, HEAD_DIM, HEAD_DIM), lambda c: (n - 1 - c, 0, 0, 0)), blk(0)],
        out_specs=[pl.BlockSpec((CHUNK, 3 * BR_WIDTH), lambda c: (n - 1 - c, 0)), sc, sc],
        out_shape=[jax.ShapeDtypeStruct((s, 3 * BR_WIDTH), F32)] + [jax.ShapeDtypeStruct((s, HEAD_DIM), F32)] * 2,
        scratch_shapes=[pltpu.VMEM((N_HEADS, HEAD_DIM, HEAD_DIM), F32)],
        compiler_params=_cp("arbitrary"),
    )(qkv, qkv, qkv, gcs, beta, states, do)
    return outs


def _hg_prep_fwd(proj, lb, *, name):
    s = proj.shape[0]
    tr = min(ROW_TILE, s)

    def body(q_ref, f_ref, lb_ref, qo_ref, ko_ref, lf_ref):
        f, lbv = f_ref[...], lb_ref[...]
        qo_ref[...] = _silu(q_ref[...])
        ko_ref[...] = (1.0 - lbv) * _sigmoid(-f)
        lf_ref[...] = jnp.log(lbv + (1.0 - lbv) * _sigmoid(f))

    blk = lambda cb: pl.BlockSpec((tr, BR_WIDTH), lambda i: (i, cb))
    out = pl.BlockSpec((tr, BR_WIDTH), lambda i: (i, 0))
    return pl.pallas_call(
        body, name=name, grid=(s // tr,),
        in_specs=[blk(C_HQ // BR_WIDTH), blk(C_HF // BR_WIDTH), pl.BlockSpec((1, BR_WIDTH), lambda i: (0, 0))],
        out_specs=[out, out, out], out_shape=[jax.ShapeDtypeStruct((s, BR_WIDTH), F32)] * 3, compiler_params=_cp("parallel"),
    )(proj, proj, lb)


def _hg_prep_bwd(proj, lb, dq, dk, dlf, *, name):
    s = proj.shape[0]
    tr = min(ROW_TILE, s)

    def body(q_ref, f_ref, lb_ref, dq_ref, dk_ref, dlf_ref, dhq_ref, dhf_ref, dlb_ref):
        @pl.when(pl.program_id(0) == 0)
        def _():
            dlb_ref[...] = jnp.zeros_like(dlb_ref)

        f, lbv = f_ref[...], lb_ref[...]
        dhq_ref[...] = (dq_ref[...] * _dsilu(q_ref[...])).astype(BF16)
        sp, sn = _sigmoid(f), _sigmoid(-f)
        inner = lbv + (1.0 - lbv) * sp
        dlf_over = dlf_ref[...] / inner
        dkv = dk_ref[...]
        dhf_ref[...] = (dlf_over * (1.0 - lbv) * sp * sn - dkv * (1.0 - lbv) * sn * (1.0 - sn)).astype(BF16)
        dlb_ref[...] += jnp.sum(dlf_over * (1.0 - sp) - dkv * sn, axis=0, keepdims=True)

    blk = lambda cb: pl.BlockSpec((tr, BR_WIDTH), lambda i: (i, cb))
    io = pl.BlockSpec((tr, BR_WIDTH), lambda i: (i, 0))
    vec = pl.BlockSpec((1, BR_WIDTH), lambda i: (0, 0))
    return pl.pallas_call(
        body, name=name, grid=(s // tr,),
        in_specs=[blk(C_HQ // BR_WIDTH), blk(C_HF // BR_WIDTH), vec, io, io, io], out_specs=[io, io, vec],
        out_shape=[jax.ShapeDtypeStruct((s, BR_WIDTH), BF16)] * 2 + [jax.ShapeDtypeStruct((1, BR_WIDTH), F32)],
        compiler_params=_cp("arbitrary"),
    )(proj, proj, lb, dq, dk, dlf)


def _hg_chunk_common(q, k, lf):
    c = CHUNK
    g = _nn_exact(_tri(c, "lower"), lf)
    e_g = _exp(g)
    qd = q * e_g
    g_last = g[c - 1:c, :]
    e_t = _exp(g_last - g)
    kt = k * e_t
    tail = _exp(g_last)
    q_sc, k_sc, e_q, e_k = [], [], [], []
    for i in range(c // SUB):
        g_ref = g[i * SUB:i * SUB + 1, :]
        eq = _exp(g[i * SUB:(i + 1) * SUB, :] - g_ref)
        ek = _hmap(lambda gr, g_: jnp.exp(jnp.minimum(gr - g_, EXP_CLAMP)), g_ref, g)
        e_q.append(eq)
        e_k.append(ek)
        q_sc.append(q[i * SUB:(i + 1) * SUB, :] * eq)
        k_sc.append(k * ek)
    a_mat = _stack_rows([_nt(qi, ki) for qi, ki in zip(q_sc, k_sc)]) * _tri(c, "lower")
    return dict(e_g=e_g, qd=qd, e_t=e_t, kt=kt, tail=tail, q_sc=q_sc, k_sc=k_sc, e_q=e_q, e_k=e_k, a_mat=a_mat)


def _hg_chunk_fwd_math(q, k, v, lf, stt):
    m = _hg_chunk_common(q, k, lf)
    o = _nt(m["qd"], stt) + _nn(m["a_mat"], v)
    stt2 = stt * m["tail"] + _tn(v, m["kt"])
    return o, stt2


def _hg_chunk_bwd_math(q, k, v, lf, stt, do, dstt2):
    c = CHUNK
    m = _hg_chunk_common(q, k, lf)
    stt2 = stt * m["tail"] + _tn(v, m["kt"])
    later = _sum(stt2 * dstt2, 0)
    dqd = _dot3(do, stt, 1, 0)
    dstt = _tn(do, m["qd"]) + dstt2 * m["tail"]
    d_a = _dot3(do, v, 1, 1) * _tri(c, "lower")
    dv = _tn(m["a_mat"], do) + _nt(m["kt"], dstt2)
    dkt = _dot3(v, dstt2, 1, 0)
    dq_parts = []
    dk = dkt * m["e_t"]
    for i in range(c // SUB):
        d_ai = d_a[i * SUB:(i + 1) * SUB, :]
        dq_parts.append(_dot3(d_ai, m["k_sc"][i], 1, 0) * m["e_q"][i])
        dk = dk + _dot3(d_ai, m["q_sc"][i], 0, 0) * m["e_k"][i]
    dq = dqd * m["e_g"] + _stack_rows(dq_parts)
    db = q * dq - k * dk
    dlf = _nn_exact(_tri(c, "upper"), db) + later
    return dq, dk, dv, dlf, dstt


def _hg_chunk_fwd(qh, kh, proj, lf, *, name):
    s = qh.shape[0]
    n = s // CHUNK
    vb = C_HI // BR_WIDTH

    def body(q_ref, k_ref, v_ref, lf_ref, o_ref, st_out_ref, st_ref):
        @pl.when(pl.program_id(0) == 0)
        def _():
            st_ref[...] = jnp.zeros_like(st_ref)

        st = _Heads(st_ref[h] for h in range(N_HEADS))
        o, st2 = _hg_chunk_fwd_math(_heads_of(q_ref), _heads_of(k_ref), _heads_of(v_ref), _heads_of(lf_ref), st)
        for h in range(N_HEADS):
            st_out_ref[0, h] = st.v[h]
            o_ref[:, h * HEAD_DIM:(h + 1) * HEAD_DIM] = o.v[h]
            st_ref[h] = st2.v[h]

    blk = lambda off: pl.BlockSpec((CHUNK, BR_WIDTH), lambda c: (c, off))
    return pl.pallas_call(
        body, name=name, grid=(n,), in_specs=[blk(0), blk(0), blk(vb), blk(0)],
        out_specs=[blk(0), pl.BlockSpec((1, N_HEADS, HEAD_DIM, HEAD_DIM), lambda c: (c, 0, 0, 0))],
        out_shape=[jax.ShapeDtypeStruct((s, BR_WIDTH), F32), jax.ShapeDtypeStruct((n, N_HEADS, HEAD_DIM, HEAD_DIM), F32)],
        scratch_shapes=[pltpu.VMEM((N_HEADS, HEAD_DIM, HEAD_DIM), F32)],
        compiler_params=_cp("arbitrary"),
    )(qh, kh, proj, lf)


def _hg_chunk_bwd(qh, kh, proj, lf, states, do, *, name):
    s = qh.shape[0]
    n = s // CHUNK
    vb = C_HI // BR_WIDTH

    def body(q_ref, k_ref, v_ref, lf_ref, st_in_ref, do_ref, dq_ref, dk_ref, dv_ref, dlf_ref, dst_ref):
        @pl.when(pl.program_id(0) == 0)
        def _():
            dst_ref[...] = jnp.zeros_like(dst_ref)

        dq, dk, dv, dlf, dst = _hg_chunk_bwd_math(
            _heads_of(q_ref), _heads_of(k_ref), _heads_of(v_ref), _heads_of(lf_ref),
            _Heads(st_in_ref[0, h] for h in range(N_HEADS)), _heads_of(do_ref),
            _Heads(dst_ref[h] for h in range(N_HEADS)))
        for h in range(N_HEADS):
            cols = slice(h * HEAD_DIM, (h + 1) * HEAD_DIM)
            dq_ref[:, cols] = dq.v[h]
            dk_ref[:, cols] = dk.v[h]
            dv_ref[:, cols] = dv.v[h].astype(BF16)
            dlf_ref[:, cols] = dlf.v[h]
            dst_ref[h] = dst.v[h]

    blk = lambda off: pl.BlockSpec((CHUNK, BR_WIDTH), lambda c: (n - 1 - c, off))
    return pl.pallas_call(
        body, name=name, grid=(n,),
        in_specs=[blk(0), blk(0), blk(vb), blk(0),
                  pl.BlockSpec((1, N_HEADS, HEAD_DIM, HEAD_DIM), lambda c: (n - 1 - c, 0, 0, 0)), blk(0)],
        out_specs=[blk(0), blk(0), blk(0), blk(0)],
        out_shape=[jax.ShapeDtypeStruct((s, BR_WIDTH), F32)] * 2 + [jax.ShapeDtypeStruct((s, BR_WIDTH), BF16),
                                                                    jax.ShapeDtypeStruct((s, BR_WIDTH), F32)],
        scratch_shapes=[pltpu.VMEM((N_HEADS, HEAD_DIM, HEAD_DIM), F32)],
        compiler_params=_cp("arbitrary"),
    )(qh, kh, proj, lf, states, do)


_ANY = pl.BlockSpec(memory_space=pl.ANY)
_MESH = pl.DeviceIdType.MESH


def _all_gather(x_local, *, name):
    def body(x_ref, out_ref, send_sems, recv_sems, local_sem):
        x, y, c = lax.axis_index("x"), lax.axis_index("y"), lax.axis_index("c")
        me, sibling = (x, y, c), (x, y, 1 - c)
        chips = [(1 - x, y), (x, 1 - y), (1 - x, 1 - y)]

        def slot(px, py, pc):
            return out_ref.at[4 * px + 2 * py + pc]

        def copy(k, block, to, src=None):
            return pltpu.make_async_remote_copy(
                src_ref=slot(*block) if src is None else src, dst_ref=slot(*block),
                send_sem=send_sems.at[k], recv_sem=recv_sems.at[k], device_id=to, device_id_type=_MESH)

        mine = pltpu.make_async_copy(x_ref, slot(*me), local_sem)
        mine.start()
        first = [copy(0, me, sibling, src=x_ref)]
        first += [copy(1 + j, me, (*chip, c), src=x_ref) for j, chip in enumerate(chips)]
        for cp in first:
            cp.start()
        passed = [copy(4 + j, (*chip, c), sibling) for j, chip in enumerate(chips)]
        for j, chip in enumerate(chips):
            copy(1 + j, (*chip, c), me).wait_recv()
            passed[j].start()
        copy(0, sibling, me).wait_recv()
        for j, chip in enumerate(chips):
            copy(4 + j, (*chip, 1 - c), me).wait_recv()
        for cp in first + passed:
            cp.wait_send()
        mine.wait()

    return pl.pallas_call(
        body, name=name, out_shape=jax.ShapeDtypeStruct((N_DEV,) + x_local.shape, x_local.dtype),
        in_specs=[_ANY], out_specs=_ANY,
        scratch_shapes=[pltpu.SemaphoreType.DMA((7,)), pltpu.SemaphoreType.DMA((7,)), pltpu.SemaphoreType.DMA],
    )(x_local)


_HBM =pl.BlockSpec(memory_space=pltpu.HBM)
_SEM = pl.BlockSpec(memory_space=pltpu.SEMAPHORE)
_EFFECT = pltpu.SideEffectType.DATAFLOW_SIDE_EFFECTING


def _peers():
    x, y, c = lax.axis_index("x"), lax.axis_index("y"), lax.axis_index("c")
    out = []
    for k in range(1, N_DEV):
        px, py, pc = x ^ ((k >> 2) & 1), y ^ ((k >> 1) & 1), c ^ (k & 1)
        out.append(((px, py, pc), 4 * px + 2 * py + pc))
    return 4 * x + 2 * y + c, out


def _push_copies(src_ref, land_ref, send_sems, recv_sems, broadcast):
    my, peers = _peers()
    pairs = []
    for k, (pos, idx) in enumerate(peers):
        src = src_ref if broadcast else src_ref.at[idx]
        send = pltpu.make_async_remote_copy(src_ref=src, dst_ref=land_ref.at[my], send_sem=send_sems.at[k],
                                            recv_sem=recv_sems.at[k], device_id=pos, device_id_type=_MESH)
        recv = pltpu.make_async_remote_copy(src_ref=src, dst_ref=land_ref.at[idx], send_sem=send_sems.at[k],
                                            recv_sem=recv_sems.at[k], device_id=pos, device_id_type=_MESH)
        pairs.append((send, recv))
    return pairs


def _push_start(src, land, *, broadcast, name):
    def body(src_ref, land_ref, send_sems, recv_sems, src_thru, land_thru, token):
        for send, _ in _push_copies(src_ref, land_ref, send_sems, recv_sems, broadcast):
            send.start()
        token[...] = jnp.zeros_like(token)

    return pl.pallas_call(
        body, name=name,
        out_shape=(pltpu.SemaphoreType.DMA((N_DEV - 1,)), pltpu.SemaphoreType.DMA((N_DEV - 1,)),
                   pltpu.HBM(src.shape, src.dtype), pltpu.HBM(land.shape, land.dtype), jax.ShapeDtypeStruct((8, 128), F32)),
        in_specs=(_HBM, _HBM), out_specs=(_SEM, _SEM, _HBM, _HBM, pl.BlockSpec(memory_space=pltpu.VMEM)),
        input_output_aliases={0: 2, 1: 3}, compiler_params=pltpu.CompilerParams(has_side_effects=_EFFECT),
    )(pltpu.with_memory_space_constraint(src, pltpu.HBM), pltpu.with_memory_space_constraint(land, pltpu.HBM))


def _push_wait(handle, after, *, broadcast, name):
    send_sems, recv_sems, src_thru, land_thru, _ = handle

    def body(src_ref, land_ref, send_sems, recv_sems, *rest):
        for send, recv in _push_copies(src_ref, land_ref, send_sems, recv_sems, broadcast):
            send.wait_send()
            recv.wait_recv()

    return pl.pallas_call(
        body, name=name,
        out_shape=(pltpu.HBM(src_thru.shape, src_thru.dtype), pltpu.HBM(land_thru.shape, land_thru.dtype)),
        in_specs=(_HBM, _HBM, _SEM, _SEM) + (_ANY,) * len(after), out_specs=(_HBM, _HBM),
        input_output_aliases={0: 0, 1: 1}, compiler_params=pltpu.CompilerParams(has_side_effects=_EFFECT),
    )(src_thru, land_thru, send_sems, recv_sems, *after)[1]


def _adamw(parts, row_off, w, m, v, *, layer=0, prev=None, name, tr):
    nl, r, c = w.shape
    np_ = parts.shape[0]
    tr = min(tr, r)
    assert r % tr == 0 and row_off % tr == 0
    ob = row_off // tr
    c1 = 1.0 - ADAM_B1 ** ADAM_STEP
    c2 = 1.0 - ADAM_B2 ** ADAM_STEP
    n_prev = 0 if prev is None else 4

    def body(p_ref, w_ref, m_ref, v_ref, *refs):
        g_ref, d_ref, nm_ref, nv_ref = refs[n_prev:]
        g = p_ref[0].astype(F32)
        for s in range(1, np_):
            g = g + p_ref[s].astype(F32)
        wv = w_ref[0]
        m2 = ADAM_B1 * m_ref[0] + (1.0 - ADAM_B1) * g
        v2 = ADAM_B2 * v_ref[0] + (1.0 - ADAM_B2) * jnp.square(g)
        m_hat = m2 / c1
        v_hat = v2 / c2
        g_ref[0] = g
        d_ref[0] = -ADAM_LR * (m_hat / (jnp.sqrt(v_hat) + ADAM_EPS) + ADAM_WD * wv)
        nm_ref[0] = m2
        nv_ref[0] = v2

    blk = pl.BlockSpec((1, tr, c), lambda i: (layer, i, 0))
    return pl.pallas_call(
        body, name=name, grid=(r // tr,),
        in_specs=[pl.BlockSpec((np_, tr, c), lambda i: (0, ob + i, 0)), blk, blk, blk] + [_ANY] * n_prev,
        out_specs=[blk] * 4, out_shape=[jax.ShapeDtypeStruct((nl, r, c), F32)] * 4,
        input_output_aliases={4 + i: i for i in range(n_prev)}, compiler_params=_cp("parallel"),
    )(parts, w, m, v, *(prev or ()))


def _sum_parts(parts, *, name):
    np_, r, c = parts.shape

    def body(p_ref, o_ref):
        g = p_ref[0]
        for s in range(1, np_):
            g = g + p_ref[s]
        o_ref[...] = g

    return pl.pallas_call(body, name=name, out_shape=jax.ShapeDtypeStruct((r, c), F32))(parts)


def _pack(arrs):
    rows = []
    for a in arrs:
        f = a.reshape(-1).astype(F32)
        pad = (-f.shape[0]) % 128
        rows.append(jnp.pad(f, (0, pad)).reshape(-1, 128))
    out = jnp.concatenate(rows, axis=0)
    return jnp.pad(out, ((0, (-out.shape[0]) % 8), (0, 0)))


def _unpack(packed, shapes):
    outs, r0 = [], 0
    for shp in shapes:
        n = 1
        for d in shp:
            n *= d
        nr = -(-n // 128)
        outs.append(packed[r0:r0 + nr].reshape(-1)[:n].reshape(shp))
        r0 += nr
    return outs


_WIN_PIECES = ((0, 4096, 0), (4112, 8208, 0), (4096, 4104, HEAD_DIM - N_HEADS), (4104, 4112, HEAD_DIM - N_HEADS))


def _win_from_shards(shards):
    cols = []
    for lo, hi, pad in _WIN_PIECES:
        for j in range(N_DEV):
            a, b = max(lo, j * SHARD_IN), min(hi, (j + 1) * SHARD_IN)
            if a < b:
                cols.append(shards[j][:, a - j * SHARD_IN:b - j * SHARD_IN])
        if pad:
            cols.append(jnp.zeros((shards[0].shape[0], pad), shards[0].dtype))
    return jnp.concatenate(cols, axis=1)


def _win_to_shards(g):
    starts, off = [], 0
    for lo, hi, pad in _WIN_PIECES:
        starts.append((lo, hi, off))
        off += hi - lo + pad
    shards = []
    for j in range(N_DEV):
        cols = []
        for lo, hi, off in sorted(starts):
            a, b = max(lo, j * SHARD_IN), min(hi, (j + 1) * SHARD_IN)
            if a < b:
                cols.append(g[:, off + a - lo:off + b - lo])
        shards.append(jnp.concatenate(cols, axis=1))
    return shards


def _lower_bounds(logits):
    probs = jax.nn.softmax(logits.astype(F32), axis=0)
    return jnp.cumsum(probs, axis=0) - probs[0]


def _pad_lanes(vec8):
    return jnp.pad(vec8.reshape(1, N_HEADS), ((0, 0), (0, HEAD_DIM - N_HEADS)))


def kernel(x, p, norm_w, w_in, dn_conv_w, dn_A_log, dn_dt_bias, dn_norm_w, hg_lb_logits, hg_norm_w, w_out, w_ple_up, w_ple_gate, final_norm_w, loss_target, m_norm_w, m_w_in, m_dn_conv_w, m_dn_A_log, m_dn_dt_bias, m_dn_norm_w, m_hg_lb_logits, m_hg_norm_w, m_w_out, m_w_ple_up, m_w_ple_gate, m_final_norm_w, v_norm_w, v_w_in, v_dn_conv_w, v_dn_A_log, v_dn_dt_bias, v_dn_norm_w, v_hg_lb_logits, v_hg_norm_w, v_w_out, v_w_ple_up, v_w_ple_gate, v_final_norm_w):
    depth = norm_w.shape[0]
    my = 4 * lax.axis_index("x") + 2 * lax.axis_index("y") + lax.axis_index("c")
    h = x[0]
    tgt = loss_target[0]
    rows_out = D_MODEL // N_DEV
    up_rows = PLE_DIM * (D_MODEL // N_DEV) // D_MODEL
    g_off, u_off = rows_out, 2 * rows_out

    def own_slot(block):
        return lax.dynamic_update_index_in_dim(lax.empty((N_DEV,) + block.shape, block.dtype), block, my, 0)

    win_bf = w_in.astype(BF16)
    rest_bf = [jnp.concatenate([w_out[l], w_ple_gate[l], w_ple_up[l].reshape(up_rows, D_MODEL)], axis=0).astype(BF16)
               for l in range(depth)]
    win_all = {0: _all_gather(win_bf[0], name="gather_w_in_l0")}
    conv_all = _all_gather(dn_conv_w, name="gather_conv_w")
    conv_full = conv_all.transpose(1, 2, 0, 3).reshape(depth, CONV_W, 3 * BR_WIDTH)
    pending = {}
    order_tok = jnp.zeros((), F32)
    for l in range(depth):
        if l > 0:
            pending["win", l] = _push_start(win_bf[l], own_slot(win_bf[l]), broadcast=True, name=f"gather_w_in_l{l}_start")
            order_tok = order_tok + pending["win", l][4][0, 0]
        pending["rest", l] = _push_start(rest_bf[l], own_slot(rest_bf[l]), broadcast=True, name=f"gather_rest_l{l}_start")
        order_tok = order_tok + pending["rest", l][4][0, 0]
    lbs = _lower_bounds(hg_lb_logits)

    saved = []
    weights = []
    for l in range(depth):
        tag = f"l{l}"
        if l > 0:
            win_all[l] = _push_wait(pending["win", l], [h], broadcast=True, name=f"gather_w_in_{tag}_wait")
        wi = _win_from_shards([win_all[l][j] for j in range(N_DEV)])
        nw = norm_w[l] + order_tok if l == 0 else norm_w[l]
        hn = _rms_fwd(h, nw, name=f"rms_fwd_{tag}")
        proj = _mm(hn, wi, mode="nn", out_dtype=F32, name=f"mm_proj_{tag}")
        al, dt = _pad_lanes(dn_A_log[l]), _pad_lanes(dn_dt_bias[l])
        qkv = _dn_qkv_fwd(proj, conv_full[l], name=f"dn_qkv_fwd_{tag}")
        beta, gcs = _dn_gate_fwd(proj, al, dt, name=f"dn_gate_fwd_{tag}")
        o_dn, st_dn = _dn_chunk_fwd(qkv, gcs, beta, name=f"dn_chunk_fwd_{tag}")
        lb = lbs[l].reshape(1, BR_WIDTH)
        qh, kh, lf = _hg_prep_fwd(proj, lb, name=f"hg_prep_fwd_{tag}")
        o_hg, st_hg = _hg_chunk_fwd(qh, kh, proj, lf, name=f"hg_chunk_fwd_{tag}")
        y_dn = _hnorm_fwd(o_dn, proj, C_Z, dn_norm_w[l], name=f"hnorm_dn_fwd_{tag}")
        y_hg = _hnorm_fwd(o_hg, proj, C_HZ, hg_norm_w[l], name=f"hnorm_hg_fwd_{tag}")
        y = jnp.concatenate([y_dn, y_hg], axis=1)
        rest_all = _push_wait(pending["rest", l], [y], broadcast=True, name=f"gather_rest_{tag}_wait")
        wo = rest_all[:, 0:rows_out].reshape(D_MODEL, D_MODEL)
        wg = rest_all[:, g_off:g_off + rows_out].reshape(D_MODEL, D_MODEL)
        wu = rest_all[:, u_off:u_off + up_rows].reshape(N_DEV, PLE_DIM, D_MODEL // N_DEV).transpose(1, 0, 2).reshape(PLE_DIM, D_MODEL)
        weights.append((wi, wo, wg, wu))
        h1 = _mm(y, wo, mode="nn", out_dtype=F32, res=h, name=f"mm_out_{tag}")
        gp = _mm(h1, wg, mode="nn", out_dtype=F32, name=f"mm_gate_{tag}")
        up = _mm(p[l, 0], wu, mode="nn", out_dtype=F32, name=f"mm_up_{tag}")
        h2 = _ple_fwd(h1, gp, up, name=f"ple_fwd_{tag}")
        saved.append(dict(h=h, hn=hn, proj=proj, qkv=qkv, beta=beta, gcs=gcs, st_dn=st_dn, qh=qh, kh=kh, lf=lf,
                          st_hg=st_hg, o_dn=o_dn, o_hg=o_hg, y=y, h1=h1, gp=gp, up=up, al=al, dt=dt, lb=lb))
        h = h2

    loss_row, dh, d_final_w = _final_fwd_bwd(h, final_norm_w, tgt, name="final_norm_loss")

    d_norm_w, d_alog, d_dt, d_dn_nw, d_hg_nw, d_lb, d_conv = ([None] * depth for _ in range(7))
    sent = {}
    for l in reversed(range(depth)):
        wi, wo, wg, wu = weights[l]
        sv = saved[l]
        tag = f"l{l}"
        dup, dgp = _ple_bwd(dh, sv["gp"], sv["up"], name=f"ple_bwd_{tag}")
        d_wu = _mm(p[l, 0], dup, mode="tn", out_dtype=BF16, name=f"mm_dwup_{tag}")
        d_wg = _mm(sv["h1"], dgp, mode="tn", out_dtype=BF16, name=f"mm_dwgate_{tag}")
        dh1 = _mm(dgp, wg, mode="nt", out_dtype=F32, res=dh, name=f"mm_dh1_{tag}")
        d_wo = _mm(sv["y"], dh1, mode="tn", out_dtype=BF16, name=f"mm_dwout_{tag}")
        parts_rest = jnp.concatenate(
            [d_wo.reshape(N_DEV, rows_out, D_MODEL), d_wg.reshape(N_DEV, rows_out, D_MODEL),
             d_wu.reshape(PLE_DIM, N_DEV, D_MODEL // N_DEV).transpose(1, 0, 2).reshape(N_DEV, up_rows, D_MODEL)], axis=1)
        sent["rest", l] = _push_start(parts_rest, own_slot(parts_rest[my]), broadcast=False, name=f"exchange_rest_{tag}_start")
        dy = _mm(dh1, wo, mode="nt", out_dtype=F32, name=f"mm_dy_{tag}")
        dn_nw = dn_norm_w[l] + sent["rest", l][4][0, 0]
        do_dn, dz_dn, d_dn_nw[l] = _hnorm_bwd(sv["o_dn"], sv["proj"], C_Z, dn_nw, dy, 0, name=f"hnorm_dn_bwd_{tag}")
        do_hg, dz_hg, d_hg_nw[l] = _hnorm_bwd(sv["o_hg"], sv["proj"], C_HZ, hg_norm_w[l], dy, BR_WIDTH, name=f"hnorm_hg_bwd_{tag}")
        dqkv, d_gc, dbeta = _dn_chunk_bwd(sv["qkv"], sv["gcs"], sv["beta"], sv["st_dn"], do_dn, name=f"dn_chunk_bwd_{tag}")
        dqkv_pre, d_conv[l] = _dn_qkv_bwd(sv["proj"], conv_full[l], dqkv, name=f"dn_qkv_bwd_{tag}")
        db, da, d_alog[l], d_dt[l] = _dn_gate_bwd(sv["proj"], sv["al"], sv["dt"], dbeta, d_gc, name=f"dn_gate_bwd_{tag}")
        dqh, dkh, dhi, dlf = _hg_chunk_bwd(sv["qh"], sv["kh"], sv["proj"], sv["lf"], sv["st_hg"], do_hg, name=f"hg_chunk_bwd_{tag}")
        dhq, dhf, d_lb[l] = _hg_prep_bwd(sv["proj"], sv["lb"], dqh, dkh, dlf, name=f"hg_prep_bwd_{tag}")
        dproj = jnp.concatenate([dqkv_pre, dz_dn, dhq, dhf, dhi, dz_hg, db, da], axis=1)
        d_win = _mm(sv["hn"], dproj, mode="tn", out_dtype=BF16, name=f"mm_dwin_{tag}")
        parts_in = jnp.stack(_win_to_shards(d_win))
        sent["win", l] = _push_start(parts_in, own_slot(parts_in[my]), broadcast=False, name=f"exchange_w_in_{tag}_start")
        dhn = _mm(dproj, wi, mode="nt", out_dtype=F32, name=f"mm_dhn_{tag}")
        dh, d_norm_w[l] = _rms_bwd(sv["h"], norm_w[l] + sent["win", l][4][0, 0], dhn, dh1, name=f"rms_bwd_{tag}")
    grad_x = dh[None]

    r_win = r_wo = r_wg = r_wu = None
    done = [grad_x]
    up3 = (depth, up_rows, D_MODEL)
    for l in reversed(range(depth)):
        tag = f"l{l}"
        land_rest = _push_wait(sent["rest", l], done, broadcast=False, name=f"exchange_rest_{tag}_wait")
        r_wo = _adamw(land_rest, 0, w_out, m_w_out, v_w_out, layer=l, prev=r_wo, name=f"adamw_w_out_{tag}", tr=rows_out)
        r_wg = _adamw(land_rest, g_off, w_ple_gate, m_w_ple_gate, v_w_ple_gate, layer=l, prev=r_wg,
                      name=f"adamw_w_gate_{tag}", tr=rows_out)
        r_wu = _adamw(land_rest, u_off, w_ple_up.reshape(up3), m_w_ple_up.reshape(up3), v_w_ple_up.reshape(up3), layer=l,
                      prev=r_wu, name=f"adamw_w_up_{tag}", tr=up_rows)
        done = [r_wo[0], r_wg[0], r_wu[0]]
    for l in reversed(range(depth)):
        tag = f"l{l}"
        land_in = _push_wait(sent["win", l], done, broadcast=False, name=f"exchange_w_in_{tag}_wait")
        r_win = _adamw(land_in, 0, w_in, m_w_in, v_w_in, layer=l, prev=r_win, name=f"adamw_w_in_{tag}", tr=256)
        done = [r_win[0]]
    r_wu = [o.reshape(w_ple_up.shape) for o in r_wu]

    small_shapes = [(1, 128), norm_w.shape, final_norm_w.shape, dn_A_log.shape, dn_dt_bias.shape, dn_norm_w.shape,
                    hg_norm_w.shape, hg_lb_logits.shape, (depth, CONV_W, 3 * BR_WIDTH)]
    small = _pack([loss_row, jnp.concatenate(d_norm_w, axis=0), d_final_w,
                   jnp.stack([a[0, :N_HEADS] for a in d_alog]), jnp.stack([a[0, :N_HEADS] for a in d_dt]),
                   jnp.concatenate(d_dn_nw, axis=0), jnp.concatenate(d_hg_nw, axis=0), jnp.concatenate(d_lb, axis=0),
                   jnp.stack(d_conv)])
    tot = _unpack(_sum_parts(_all_gather(small, name="gather_small"), name="sum_small"), small_shapes)
    loss = tot[0][0, 0]
    g_lb = tot[7]
    g_logits = jax.vjp(_lower_bounds, hg_lb_logits)[1](g_lb)[0]
    g_conv = lax.dynamic_slice_in_dim(tot[8], my * (3 * BR_WIDTH // N_DEV), 3 * BR_WIDTH // N_DEV, axis=2)
    small_g = [tot[1], g_conv, tot[3], tot[4], tot[5], g_logits, tot[6], tot[2]]
    small_w = [norm_w, dn_conv_w, dn_A_log, dn_dt_bias, dn_norm_w, hg_lb_logits, hg_norm_w, final_norm_w]
    small_m = [m_norm_w, m_dn_conv_w, m_dn_A_log, m_dn_dt_bias, m_dn_norm_w, m_hg_lb_logits, m_hg_norm_w, m_final_norm_w]
    small_v = [v_norm_w, v_dn_conv_w, v_dn_A_log, v_dn_dt_bias, v_dn_norm_w, v_hg_lb_logits, v_hg_norm_w, v_final_norm_w]
    pk_w = _pack(small_w)
    res_small = _adamw(_pack(small_g)[None], 0, pk_w[None], _pack(small_m)[None], _pack(small_v)[None], name="adamw_small",
                       tr=pk_w.shape[0])
    shapes_w = [a.shape for a in small_w]
    sg, sd, sm, sv_ = (_unpack(r[0], shapes_w) for r in res_small)

    def order(small_list, big_in, big_out, big_up, big_gate):
        nw, cw, al_, dt_, dnw, lbl, hnw, fw = small_list
        return [nw, big_in, cw, al_, dt_, dnw, lbl, hnw, big_out, big_up, big_gate, fw]

    outs = [loss, grad_x]
    for i, sl in enumerate((sg, sd, sm, sv_)):
        outs += order(sl, r_win[i], r_wo[i], r_wu[i], r_wg[i])
    return tuple(outs)
```

```python
import functools

import jax
import jax.numpy as jnp
from jax import lax
from jax.experimental import pallas as pl
from jax.experimental.pallas import tpu as pltpu

F32 = jnp.float32
BF16 = jnp.bfloat16
HIGHEST = lax.Precision.HIGHEST

N_DEV = 8
D_MODEL = 2048
PLE_DIM = 256
HEAD_DIM = 128
N_HEADS = 8
BR_WIDTH = N_HEADS * HEAD_DIM
CHUNK = 64
SUB = 16
CONV_W = 4
NORM_EPS = 1e-6
L2_EPS = 1e-6
IN_WIDTH = 8208
SHARD_IN = IN_WIDTH // N_DEV
EXP_CLAMP = 80.0

C_QKV, C_Z, C_HQ, C_HF, C_HI, C_HZ, C_B, C_A, N_PROJ = 0, 3072, 4096, 5120, 6144, 7168, 8192, 8320, 8448

ADAM_LR, ADAM_B1, ADAM_B2, ADAM_EPS, ADAM_WD, ADAM_STEP = 0.001, 0.9, 0.999, 1e-08, 0.01, 10

VMEM_LIMIT = 48 * 1024 * 1024


def _cp(*sem):
    return pltpu.CompilerParams(dimension_semantics=sem, vmem_limit_bytes=VMEM_LIMIT)


class _Heads:
    def __init__(self, vals):
        self.v = tuple(vals)

    def __add__(self, o):
        return _hmap(lambda a, b: a + b, self, o)

    def __radd__(self, o):
        return _hmap(lambda a, b: b + a, self, o)

    def __sub__(self, o):
        return _hmap(lambda a, b: a - b, self, o)

    def __rsub__(self, o):
        return _hmap(lambda a, b: b - a, self, o)

    def __mul__(self, o):
        return _hmap(lambda a, b: a * b, self, o)

    def __rmul__(self, o):
        return _hmap(lambda a, b: b * a, self, o)

    def __neg__(self):
        return _hmap(lambda a: -a, self)

    def __getitem__(self, idx):
        return _hmap(lambda a: a[idx], self)


def _hmap(fn, *args):
    n = next((len(a.v) for a in args if isinstance(a, _Heads)), None)
    if n is None:
        return fn(*args)
    return _Heads(fn(*[a.v[i] if isinstance(a, _Heads) else a for a in args]) for i in range(n))


def _dot(a, b, ca, cb):
    return _hmap(lambda x, y: lax.dot_general(x.astype(BF16), y.astype(BF16), (((ca,), (cb,)), ((), ())),
                                              preferred_element_type=F32), a, b)


def _nn(a, b):
    return _dot(a, b, 1, 0)


def _nt(a, b):
    return _dot(a, b, 1, 1)


def _tn(a, b):
    return _dot(a, b, 0, 0)


def _split(a):
    hi = _hmap(lambda x: x.astype(BF16), a)
    return hi, _hmap(lambda x, h: (x - h.astype(F32)).astype(BF16), a, hi)


def _dot3(a, b, ca, cb):
    ah, al = _split(a)
    bh, bl = _split(b)
    return _dot(ah, bh, ca, cb) + (_dot(ah, bl, ca, cb) + _dot(al, bh, ca, cb))


def _nn_exact(a, b):
    return _hmap(lambda y: lax.dot_general(a, y, (((1,), (0,)), ((), ())), precision=HIGHEST,
                                           preferred_element_type=F32), b)


def _exp(x):
    return _hmap(jnp.exp, x)


def _sum(x, axis):
    return _hmap(lambda a: jnp.sum(a, axis=axis, keepdims=True), x)


def _stack_rows(parts):
    return _hmap(lambda *xs: jnp.concatenate(xs, axis=0), *parts)


def _sigmoid(x):
    return jax.nn.sigmoid(x)


def _silu(x):
    return x * _sigmoid(x)


def _dsilu(x):
    s = _sigmoid(x)
    return s * (1.0 + x * (1.0 - s))


def _softplus(x):
    return jnp.maximum(x, 0.0) + jnp.log(1.0 + jnp.exp(-jnp.abs(x)))


def _iota2(n, m, axis):
    return lax.broadcasted_iota(jnp.int32, (n, m), axis)


def _col2row(col, eye):
    return _hmap(lambda c: jnp.sum(eye * c, axis=0, keepdims=True), col)


def _row2col(row, eye):
    return _hmap(lambda r: jnp.sum(eye * r, axis=1, keepdims=True), row)


def _pick_lane(block, lane_idx):
    lane = _iota2(block.shape[0], block.shape[1], 1)
    return jnp.sum(jnp.where(lane == lane_idx, block, 0.0), axis=1, keepdims=True)


MM_TILE_M, MM_TILE_N, MM_TILE_K = 1024, 1408, 2048


def _tile(dim, cap):
    if dim <= cap:
        return dim
    t = cap - cap % 128
    while dim % t:
        t -= 128
    return t


def _mm(a, b, *, mode, out_dtype, res=None, after=(), name):
    if mode == "nn":
        (m, kd), (_, n) = a.shape, b.shape
    elif mode == "nt":
        (m, kd), (n, _) = a.shape, b.shape
    else:
        (kd, m), (_, n) = a.shape, b.shape
    tm, tn, tk = _tile(m, MM_TILE_M), _tile(n, MM_TILE_N), _tile(kd, MM_TILE_K)
    assert m % tm == 0 and n % tn == 0 and kd % tk == 0, (m, n, kd, tm, tn, tk)
    nk = kd // tk
    ca, cb = {"nn": (1, 0), "nt": (1, 1), "tn": (0, 0)}[mode]

    def body(*refs):
        a_ref, b_ref = refs[:2]
        r_ref = None if res is None else refs[2]
        o_ref, acc_ref = refs[-2:]
        k = pl.program_id(2)

        @pl.when(k == 0)
        def _():
            acc_ref[...] = jnp.zeros_like(acc_ref)

        acc_ref[...] += _dot(a_ref[...], b_ref[...], ca, cb)

        @pl.when(k == nk - 1)
        def _():
            out = acc_ref[...]
            if r_ref is not None:
                out = out + r_ref[...].astype(F32)
            o_ref[...] = out.astype(o_ref.dtype)

    a_spec = pl.BlockSpec((tk, tm), lambda i, j, k: (k, i)) if mode == "tn" else pl.BlockSpec((tm, tk), lambda i, j, k: (i, k))
    b_spec = pl.BlockSpec((tn, tk), lambda i, j, k: (j, k)) if mode == "nt" else pl.BlockSpec((tk, tn), lambda i, j, k: (k, j))
    o_spec = pl.BlockSpec((tm, tn), lambda i, j, k: (i, j))
    in_specs = [a_spec, b_spec] + ([o_spec] if res is not None else []) + [pl.BlockSpec(memory_space=pl.ANY)] * len(after)
    args = (a, b) + ((res,) if res is not None else ()) + tuple(after)
    return pl.pallas_call(
        body, name=name, grid=(m // tm, n // tn, nk), in_specs=in_specs, out_specs=o_spec,
        out_shape=jax.ShapeDtypeStruct((m, n), out_dtype),
        scratch_shapes=[pltpu.VMEM((tm, tn), F32)],
        compiler_params=_cp("parallel", "parallel", "arbitrary"),
    )(*args)


ROW_TILE = 256


def _rms_fwd(h, w, *, name):
    s, d = h.shape
    tr = min(ROW_TILE, s)

    def body(h_ref, w_ref, o_ref):
        x = h_ref[...]
        r = lax.rsqrt(jnp.mean(x * x, axis=-1, keepdims=True) + NORM_EPS)
        o_ref[...] = (x * r * w_ref[...]).astype(o_ref.dtype)

    return pl.pallas_call(
        body, name=name, grid=(s // tr,),
        in_specs=[pl.BlockSpec((tr, d), lambda i: (i, 0)), pl.BlockSpec((1, d), lambda i: (0, 0))],
        out_specs=pl.BlockSpec((tr, d), lambda i: (i, 0)),
        out_shape=jax.ShapeDtypeStruct((s, d), BF16), compiler_params=_cp("parallel"),
    )(h, w.reshape(1, d))


def _rms_bwd_math(x, w, dy):
    d = x.shape[-1]
    r = lax.rsqrt(jnp.mean(x * x, axis=-1, keepdims=True) + NORM_EPS)
    gw = dy * w
    dx = r * gw - x * ((r * r * r) * (jnp.sum(gw * x, axis=-1, keepdims=True) / d))
    return dx, dy * x * r


def _rms_bwd(h, w, dhn, res, *, name):
    s, d = h.shape
    tr = min(ROW_TILE, s)

    def body(h_ref, w_ref, g_ref, r_ref, dh_ref, dw_ref):
        @pl.when(pl.program_id(0) == 0)
        def _():
            dw_ref[...] = jnp.zeros_like(dw_ref)

        dx, dwt = _rms_bwd_math(h_ref[...], w_ref[...], g_ref[...])
        dh_ref[...] = r_ref[...] + dx
        dw_ref[...] += jnp.sum(dwt, axis=0, keepdims=True)

    row = pl.BlockSpec((tr, d), lambda i: (i, 0))
    vec = pl.BlockSpec((1, d), lambda i: (0, 0))
    return pl.pallas_call(
        body, name=name, grid=(s // tr,), in_specs=[row, vec, row, row], out_specs=[row, vec],
        out_shape=[jax.ShapeDtypeStruct((s, d), F32), jax.ShapeDtypeStruct((1, d), F32)],
        compiler_params=_cp("arbitrary"),
    )(h, w.reshape(1, d), dhn, res)


def _final_fwd_bwd(h, w, tgt, *, name):
    s, d = h.shape
    tr = min(ROW_TILE, s)

    def body(h_ref, w_ref, t_ref, loss_ref, dh_ref, dw_ref):
        @pl.when(pl.program_id(0) == 0)
        def _():
            loss_ref[...] = jnp.zeros_like(loss_ref)
            dw_ref[...] = jnp.zeros_like(dw_ref)

        x = h_ref[...]
        wv = w_ref[...]
        r = lax.rsqrt(jnp.mean(x * x, axis=-1, keepdims=True) + NORM_EPS)
        err = x * r * wv - t_ref[...]
        row_loss = jnp.mean(err * err, axis=-1, keepdims=True)
        loss_ref[...] += 0.5 * jnp.sum(row_loss, axis=0, keepdims=True)
        dx, dwt = _rms_bwd_math(x, wv, err / d)
        dh_ref[...] = dx
        dw_ref[...] += jnp.sum(dwt, axis=0, keepdims=True)

    row = pl.BlockSpec((tr, d), lambda i: (i, 0))
    vec = pl.BlockSpec((1, d), lambda i: (0, 0))
    return pl.pallas_call(
        body, name=name, grid=(s // tr,), in_specs=[row, vec, row],
        out_specs=[pl.BlockSpec((1, 128), lambda i: (0, 0)), row, vec],
        out_shape=[jax.ShapeDtypeStruct((1, 128), F32), jax.ShapeDtypeStruct((s, d), F32),
                   jax.ShapeDtypeStruct((1, d), F32)],
        compiler_params=_cp("arbitrary"),
    )(h, w.reshape(1, d), tgt)


def _ple_fwd(h1, gate_pre, up, *, name):
    s, d = h1.shape
    tr = min(ROW_TILE, s)

    def body(h_ref, g_ref, u_ref, o_ref):
        o_ref[...] = h_ref[...] + u_ref[...] * _sigmoid(g_ref[...])

    row = pl.BlockSpec((tr, d), lambda i: (i, 0))
    return pl.pallas_call(body, name=name, grid=(s // tr,), in_specs=[row, row, row], out_specs=row,
                          out_shape=jax.ShapeDtypeStruct((s, d), F32), compiler_params=_cp("parallel"))(h1, gate_pre, up)


def _ple_bwd(dh2, gate_pre, up, *, name):
    s, d = dh2.shape
    tr = min(ROW_TILE, s)

    def body(d_ref, g_ref, u_ref, dup_ref, dgp_ref):
        dh = d_ref[...]
        gate = _sigmoid(g_ref[...])
        dup_ref[...] = (dh * gate).astype(BF16)
        dgp_ref[...] = (dh * u_ref[...] * gate * (1.0 - gate)).astype(BF16)

    row = pl.BlockSpec((tr, d), lambda i: (i, 0))
    return pl.pallas_call(body, name=name, grid=(s // tr,), in_specs=[row, row, row], out_specs=[row, row],
                          out_shape=[jax.ShapeDtypeStruct((s, d), BF16)] * 2, compiler_params=_cp("parallel"))(dh2, gate_pre, up)


HN_TILE = 512


def _hnorm_fwd(o, proj, z_col, w, *, name):
    s = o.shape[0]
    tr = min(HN_TILE, s)
    zb = z_col // HEAD_DIM

    def body(o_ref, z_ref, w_ref, y_ref):
        x = o_ref[...]
        r = lax.rsqrt(jnp.mean(x * x, axis=-1, keepdims=True) + NORM_EPS)
        y_ref[...] = (x * r * w_ref[...] * _silu(z_ref[...])).astype(BF16)

    blk = pl.BlockSpec((tr, HEAD_DIM), lambda i, h: (i, h))
    return pl.pallas_call(
        body, name=name, grid=(s // tr, N_HEADS),
        in_specs=[blk, pl.BlockSpec((tr, HEAD_DIM), lambda i, h: (i, zb + h)), pl.BlockSpec((1, HEAD_DIM), lambda i, h: (0, 0))],
        out_specs=blk, out_shape=jax.ShapeDtypeStruct((s, BR_WIDTH), BF16), compiler_params=_cp("parallel", "parallel"),
    )(o, proj, w.reshape(1, HEAD_DIM))


def _hnorm_bwd(o, proj, z_col, w, dy, dy_col, *, name):
    s = o.shape[0]
    tr = min(HN_TILE, s)
    zb, yb = z_col // HEAD_DIM, dy_col // HEAD_DIM

    def body(o_ref, z_ref, w_ref, dy_ref, do_ref, dz_ref, dw_ref):
        @pl.when((pl.program_id(0) == 0) & (pl.program_id(1) == 0))
        def _():
            dw_ref[...] = jnp.zeros_like(dw_ref)

        x, z, wv, g = o_ref[...], z_ref[...], w_ref[...], dy_ref[...]
        r = lax.rsqrt(jnp.mean(x * x, axis=-1, keepdims=True) + NORM_EPS)
        on = x * r * wv
        don = g * _silu(z)
        dz_ref[...] = (g * on * _dsilu(z)).astype(BF16)
        gw = don * wv
        do_ref[...] = r * gw - x * ((r * r * r) * (jnp.sum(gw * x, axis=-1, keepdims=True) / HEAD_DIM))
        dw_ref[...] += jnp.sum(don * x * r, axis=0, keepdims=True)

    blk = pl.BlockSpec((tr, HEAD_DIM), lambda i, h: (i, h))
    vec = pl.BlockSpec((1, HEAD_DIM), lambda i, h: (0, 0))
    return pl.pallas_call(
        body, name=name, grid=(s // tr, N_HEADS),
        in_specs=[blk, pl.BlockSpec((tr, HEAD_DIM), lambda i, h: (i, zb + h)), vec,
                  pl.BlockSpec((tr, HEAD_DIM), lambda i, h: (i, yb + h))],
        out_specs=[blk, blk, vec],
        out_shape=[jax.ShapeDtypeStruct((s, BR_WIDTH), F32), jax.ShapeDtypeStruct((s, BR_WIDTH), BF16),
                   jax.ShapeDtypeStruct((1, HEAD_DIM), F32)],
        compiler_params=_cp("arbitrary", "arbitrary"),
    )(o, proj, w.reshape(1, HEAD_DIM), dy)


def _conv_silu(x, w, s):
    row = _iota2(s, x.shape[1], 0)
    c = w[CONV_W - 1:CONV_W, :] * x
    for k in range(1, CONV_W):
        c = c + w[CONV_W - 1 - k:CONV_W - k, :] * jnp.where(row >= k, pltpu.roll(x, k, 0), 0.0)
    return c


def _dn_qkv_fwd(proj, conv_w, *, name):
    s = proj.shape[0]
    nb = 3 * N_HEADS

    def body(x_ref, w_ref, o_ref):
        j = pl.program_id(0)
        sv = _silu(_conv_silu(x_ref[...], w_ref[...], s))
        r = lax.rsqrt(jnp.sum(sv * sv, axis=-1, keepdims=True) + L2_EPS)
        scale = jnp.where(j < N_HEADS, HEAD_DIM ** -0.5, 1.0).astype(F32)
        o_ref[...] = jnp.where(j < 2 * N_HEADS, sv * r * scale, sv)

    return pl.pallas_call(
        body, name=name, grid=(nb,),
        in_specs=[pl.BlockSpec((s, HEAD_DIM), lambda j: (0, j)), pl.BlockSpec((CONV_W, HEAD_DIM), lambda j: (0, j))],
        out_specs=pl.BlockSpec((s, HEAD_DIM), lambda j: (0, j)),
        out_shape=jax.ShapeDtypeStruct((s, 3 * BR_WIDTH), F32), compiler_params=_cp("parallel"),
    )(proj, conv_w)


def _dn_qkv_bwd(proj, conv_w, dqkv, *, name):
    s = proj.shape[0]
    nb = 3 * N_HEADS

    def body(x_ref, w_ref, g_ref, dx_ref, dw_ref):
        j = pl.program_id(0)
        x, w, g = x_ref[...], w_ref[...], g_ref[...]
        c = _conv_silu(x, w, s)
        sv = _silu(c)
        r = lax.rsqrt(jnp.sum(sv * sv, axis=-1, keepdims=True) + L2_EPS)
        scale = jnp.where(j < N_HEADS, HEAD_DIM ** -0.5, 1.0).astype(F32)
        ds_n = scale * (r * g - sv * ((r * r * r) * jnp.sum(g * sv, axis=-1, keepdims=True)))
        dc = jnp.where(j < 2 * N_HEADS, ds_n, g) * _dsilu(c)
        row = _iota2(s, HEAD_DIM, 0)
        dx = w[CONV_W - 1:CONV_W, :] * dc
        dws = [jnp.sum(dc * x, axis=0, keepdims=True)]
        for k in range(1, CONV_W):
            dx = dx + w[CONV_W - 1 - k:CONV_W - k, :] * jnp.where(row < s - k, pltpu.roll(dc, s - k, 0), 0.0)
            dws.append(jnp.sum(dc * jnp.where(row >= k, pltpu.roll(x, k, 0), 0.0), axis=0, keepdims=True))
        dx_ref[...] = dx.astype(BF16)
        for k in range(CONV_W):
            dw_ref[CONV_W - 1 - k:CONV_W - k, :] = dws[k]

    blk = pl.BlockSpec((s, HEAD_DIM), lambda j: (0, j))
    wblk = pl.BlockSpec((CONV_W, HEAD_DIM), lambda j: (0, j))
    return pl.pallas_call(
        body, name=name, grid=(nb,), in_specs=[blk, wblk, blk], out_specs=[blk, wblk],
        out_shape=[jax.ShapeDtypeStruct((s, 3 * BR_WIDTH), BF16), jax.ShapeDtypeStruct((CONV_W, 3 * BR_WIDTH), F32)],
        compiler_params=_cp("parallel"),
    )(proj, conv_w, dqkv)


def _tri(n, kind):
    r, c = _iota2(n, n, 0), _iota2(n, n, 1)
    if kind == "lower":
        return (r >= c).astype(F32)
    if kind == "upper":
        return (r <= c).astype(F32)
    return (r == c).astype(F32)


def _dn_gate_fwd(proj, a_log, dt_bias, *, name):
    s = proj.shape[0]

    def body(b_ref, a_ref, al_ref, dt_ref, beta_ref, g_ref):
        beta_ref[...] = _sigmoid(b_ref[...])
        g = -jnp.exp(al_ref[...]) * _softplus(a_ref[...] + dt_ref[...])
        g_ref[...] = _nn_exact(_tri(CHUNK, "lower"), g)

    blk = lambda cb: pl.BlockSpec((CHUNK, HEAD_DIM), lambda i: (i, cb))
    vec = pl.BlockSpec((1, HEAD_DIM), lambda i: (0, 0))
    out = pl.BlockSpec((CHUNK, HEAD_DIM), lambda i: (i, 0))
    return pl.pallas_call(
        body, name=name, grid=(s // CHUNK,), in_specs=[blk(C_B // HEAD_DIM), blk(C_A // HEAD_DIM), vec, vec],
        out_specs=[out, out], out_shape=[jax.ShapeDtypeStruct((s, HEAD_DIM), F32)] * 2, compiler_params=_cp("parallel"),
    )(proj, proj, a_log, dt_bias)


def _dn_gate_bwd(proj, a_log, dt_bias, dbeta, d_g, *, name):
    s = proj.shape[0]

    def body(b_ref, a_ref, al_ref, dt_ref, dbeta_ref, dG_ref, db_ref, da_ref, dal_ref, ddt_ref):
        @pl.when(pl.program_id(0) == 0)
        def _():
            dal_ref[...] = jnp.zeros_like(dal_ref)
            ddt_ref[...] = jnp.zeros_like(ddt_ref)

        beta = _sigmoid(b_ref[...])
        db_ref[...] = (dbeta_ref[...] * beta * (1.0 - beta)).astype(BF16)
        pre = a_ref[...] + dt_ref[...]
        neg_ea = -jnp.exp(al_ref[...])
        dg = _nn_exact(_tri(CHUNK, "upper"), dG_ref[...])
        da = dg * neg_ea * _sigmoid(pre)
        da_ref[...] = da.astype(BF16)
        ddt_ref[...] += jnp.sum(da, axis=0, keepdims=True)
        dal_ref[...] += jnp.sum(dg * neg_ea * _softplus(pre), axis=0, keepdims=True)

    blk = lambda cb: pl.BlockSpec((CHUNK, HEAD_DIM), lambda i: (i, cb))
    vec = pl.BlockSpec((1, HEAD_DIM), lambda i: (0, 0))
    io = pl.BlockSpec((CHUNK, HEAD_DIM), lambda i: (i, 0))
    return pl.pallas_call(
        body, name=name, grid=(s // CHUNK,),
        in_specs=[blk(C_B // HEAD_DIM), blk(C_A // HEAD_DIM), vec, vec, io, io], out_specs=[io, io, vec, vec],
        out_shape=[jax.ShapeDtypeStruct((s, HEAD_DIM), BF16)] * 2 + [jax.ShapeDtypeStruct((1, HEAD_DIM), F32)] * 2,
        compiler_params=_cp("arbitrary"),
    )(proj, proj, a_log, dt_bias, dbeta, d_g)


def _unit_lower_inverse(a_strict, eye):
    x = -a_strict
    t = x + eye
    p = x
    n = 2
    while n < CHUNK:
        p = _nn(p, p)
        t = t + _nn(t, p)
        n *= 2
    return t


def _dn_chunk_common(q, k, v, gc, beta, st):
    c = CHUNK
    eye = _tri(c, "eye")
    low = _tri(c, "lower")
    strict = low - eye
    grow = _col2row(gc, eye)
    dec = _hmap(lambda g_, gr: low * jnp.exp(low * (g_ - gr)), gc, grow)
    kb = k * beta
    a_mat = _nt(kb, k) * dec * strict
    t_inv = _unit_lower_inverse(a_mat, eye)
    e_g = _exp(gc)
    u = _nn(t_inv, v * beta)
    w = _nn(t_inv, kb * e_g)
    p_qk = _nt(q, k)
    qk = p_qk * dec
    qd = q * e_g
    last = (_iota2(c, 1, 0) == c - 1).astype(F32)
    g_last = _sum(gc * last, 0)
    e_t = _exp(g_last - gc)
    kt = k * e_t
    tail = _exp(g_last)
    vn = u - _nn(w, st)
    return dict(eye=eye, low=low, strict=strict, dec=dec, kb=kb, a_mat=a_mat, t_inv=t_inv, e_g=e_g, u=u, w=w,
                qk=qk, qd=qd, last=last, e_t=e_t, kt=kt, tail=tail, vn=vn)


def _dn_chunk_fwd_math(q, k, v, gc, beta, st):
    m = _dn_chunk_common(q, k, v, gc, beta, st)
    o = _nn(m["qd"], st) + _nn(m["qk"], m["vn"])
    st2 = st * m["tail"] + _tn(m["kt"], m["vn"])
    return o, st2


def _dn_chunk_bwd_math(q, k, v, gc, beta, st, do, dst2):
    m = _dn_chunk_common(q, k, v, gc, beta, st)
    eye, low, strict = m["eye"], m["low"], m["strict"]
    dvn = _tn(m["qk"], do) + _nn(m["kt"], dst2)
    dqk = _nt(do, m["vn"]) * low
    dqd = _nt(do, st)
    dst = _tn(m["qd"], do) + dst2 * m["tail"] - _tn(m["w"], dvn)
    dkt = _nt(m["vn"], dst2)
    dtail = _sum(_sum(st * dst2, 1), 0)
    dw = -_nt(dvn, st)
    dvb = _tn(m["t_inv"], dvn)
    dkg = _tn(m["t_inv"], dw)
    d_a = (_nt(dvb, m["u"]) + _nt(dkg, m["w"])) * (-strict)
    dkk = d_a * m["dec"]
    dp = dqk * m["dec"]
    dq = _nn(dp, k) + dqd * m["e_g"]
    dkb = _nn(dkk, k) + dkg * m["e_g"]
    dk = _tn(dp, q) + _tn(dkk, m["kb"]) + dkb * beta + dkt * m["e_t"]
    dv = dvb * beta
    dbeta = _sum(dvb * v + dkb * k, 1)
    de_g = _sum(dkg * m["kb"] + dqd * q, 1)
    de_t = _sum(dkt * k, 1)
    mm = d_a * m["a_mat"] + dqk * m["qk"]
    dgc = (_sum(mm, 1) - _row2col(_sum(mm, 0), eye) + de_g * m["e_g"] - de_t * m["e_t"]
           + (_sum(de_t * m["e_t"], 0) + dtail * m["tail"]) * m["last"])
    return dq, dk, dv, dgc, dbeta, dst


def _heads_of(ref):
    return _Heads(ref[:, h * HEAD_DIM:(h + 1) * HEAD_DIM] for h in range(N_HEADS))


def _lanes_of(block):
    return _Heads(_pick_lane(block, h) for h in range(N_HEADS))


def _dn_chunk_fwd(qkv, gcs, beta, *, name):
    s = qkv.shape[0]
    n = s // CHUNK

    def body(q_ref, k_ref, v_ref, g_ref, b_ref, o_ref, st_out_ref, st_ref):
        @pl.when(pl.program_id(0) == 0)
        def _():
            st_ref[...] = jnp.zeros_like(st_ref)

        gblk, bblk = g_ref[...], b_ref[...]
        st = _Heads(st_ref[h] for h in range(N_HEADS))
        o, st2 = _dn_chunk_fwd_math(_heads_of(q_ref), _heads_of(k_ref), _heads_of(v_ref), _lanes_of(gblk),
                                    _lanes_of(bblk), st)
        for h in range(N_HEADS):
            st_out_ref[0, h] = st.v[h]
            o_ref[:, h * HEAD_DIM:(h + 1) * HEAD_DIM] = o.v[h]
            st_ref[h] = st2.v[h]

    blk = lambda off: pl.BlockSpec((CHUNK, BR_WIDTH), lambda c: (c, off))
    sc = pl.BlockSpec((CHUNK, HEAD_DIM), lambda c: (c, 0))
    return pl.pallas_call(
        body, name=name, grid=(n,),
        in_specs=[blk(0), blk(1), blk(2), sc, sc],
        out_specs=[blk(0), pl.BlockSpec((1, N_HEADS, HEAD_DIM, HEAD_DIM), lambda c: (c, 0, 0, 0))],
        out_shape=[jax.ShapeDtypeStruct((s, BR_WIDTH), F32), jax.ShapeDtypeStruct((n, N_HEADS, HEAD_DIM, HEAD_DIM), F32)],
        scratch_shapes=[pltpu.VMEM((N_HEADS, HEAD_DIM, HEAD_DIM), F32)],
        compiler_params=_cp("arbitrary"),
    )(qkv, qkv, qkv, gcs, beta)


def _dn_chunk_bwd(qkv, gcs, beta, states, do, *, name):
    s = qkv.shape[0]
    n = s // CHUNK

    def body(q_ref, k_ref, v_ref, g_ref, b_ref, st_in_ref, do_ref, dqkv_ref, dg_ref, dbeta_ref, dst_ref):
        @pl.when(pl.program_id(0) == 0)
        def _():
            dst_ref[...] = jnp.zeros_like(dst_ref)

        gblk, bblk = g_ref[...], b_ref[...]
        lane = _iota2(CHUNK, HEAD_DIM, 1)
        dg_all = jnp.zeros((CHUNK, HEAD_DIM), F32)
        dbeta_all = jnp.zeros((CHUNK, HEAD_DIM), F32)
        dq, dk, dv, dgc, dbeta, dst = _dn_chunk_bwd_math(
            _heads_of(q_ref), _heads_of(k_ref), _heads_of(v_ref), _lanes_of(gblk), _lanes_of(bblk),
            _Heads(st_in_ref[0, h] for h in range(N_HEADS)), _heads_of(do_ref),
            _Heads(dst_ref[h] for h in range(N_HEADS)))
        for h in range(N_HEADS):
            for part, val in enumerate((dq, dk, dv)):
                c0 = part * BR_WIDTH + h * HEAD_DIM
                dqkv_ref[:, c0:c0 + HEAD_DIM] = val.v[h]
            dg_all = jnp.where(lane == h, dgc.v[h], dg_all)
            dbeta_all = jnp.where(lane == h, dbeta.v[h], dbeta_all)
            dst_ref[h] = dst.v[h]
        dg_ref[...] = dg_all
        dbeta_ref[...] = dbeta_all

    blk = lambda off: pl.BlockSpec((CHUNK, BR_WIDTH), lambda c: (n - 1 - c, off))
    sc = pl.BlockSpec((CHUNK, HEAD_DIM), lambda c: (n - 1 - c, 0))
    outs = pl.pallas_call(
        body, name=name, grid=(n,),
        in_specs=[blk(0), blk(1), blk(2), sc, sc,
                  pl.BlockSpec((1, N_HEADS, HEAD_DIM, HEAD_DIM), lambda c: (n - 1 - c, 0, 0, 0)), blk(0)],
        out_specs=[pl.BlockSpec((CHUNK, 3 * BR_WIDTH), lambda c: (n - 1 - c, 0)), sc, sc],
        out_shape=[jax.ShapeDtypeStruct((s, 3 * BR_WIDTH), F32)] + [jax.ShapeDtypeStruct((s, HEAD_DIM), F32)] * 2,
        scratch_shapes=[pltpu.VMEM((N_HEADS, HEAD_DIM, HEAD_DIM), F32)],
        compiler_params=_cp("arbitrary"),
    )(qkv, qkv, qkv, gcs, beta, states, do)
    return outs


def _hg_prep_fwd(proj, lb, *, name):
    s = proj.shape[0]
    tr = min(ROW_TILE, s)

    def body(q_ref, f_ref, lb_ref, qo_ref, ko_ref, lf_ref):
        f, lbv = f_ref[...], lb_ref[...]
        qo_ref[...] = _silu(q_ref[...])
        ko_ref[...] = (1.0 - lbv) * _sigmoid(-f)
        lf_ref[...] = jnp.log(lbv + (1.0 - lbv) * _sigmoid(f))

    blk = lambda cb: pl.BlockSpec((tr, BR_WIDTH), lambda i: (i, cb))
    out = pl.BlockSpec((tr, BR_WIDTH), lambda i: (i, 0))
    return pl.pallas_call(
        body, name=name, grid=(s // tr,),
        in_specs=[blk(C_HQ // BR_WIDTH), blk(C_HF // BR_WIDTH), pl.BlockSpec((1, BR_WIDTH), lambda i: (0, 0))],
        out_specs=[out, out, out], out_shape=[jax.ShapeDtypeStruct((s, BR_WIDTH), F32)] * 3, compiler_params=_cp("parallel"),
    )(proj, proj, lb)


def _hg_prep_bwd(proj, lb, dq, dk, dlf, *, name):
    s = proj.shape[0]
    tr = min(ROW_TILE, s)

    def body(q_ref, f_ref, lb_ref, dq_ref, dk_ref, dlf_ref, dhq_ref, dhf_ref, dlb_ref):
        @pl.when(pl.program_id(0) == 0)
        def _():
            dlb_ref[...] = jnp.zeros_like(dlb_ref)

        f, lbv = f_ref[...], lb_ref[...]
        dhq_ref[...] = (dq_ref[...] * _dsilu(q_ref[...])).astype(BF16)
        sp, sn = _sigmoid(f), _sigmoid(-f)
        inner = lbv + (1.0 - lbv) * sp
        dlf_over = dlf_ref[...] / inner
        dkv = dk_ref[...]
        dhf_ref[...] = (dlf_over * (1.0 - lbv) * sp * sn - dkv * (1.0 - lbv) * sn * (1.0 - sn)).astype(BF16)
        dlb_ref[...] += jnp.sum(dlf_over * (1.0 - sp) - dkv * sn, axis=0, keepdims=True)

    blk = lambda cb: pl.BlockSpec((tr, BR_WIDTH), lambda i: (i, cb))
    io = pl.BlockSpec((tr, BR_WIDTH), lambda i: (i, 0))
    vec = pl.BlockSpec((1, BR_WIDTH), lambda i: (0, 0))
    return pl.pallas_call(
        body, name=name, grid=(s // tr,),
        in_specs=[blk(C_HQ // BR_WIDTH), blk(C_HF // BR_WIDTH), vec, io, io, io], out_specs=[io, io, vec],
        out_shape=[jax.ShapeDtypeStruct((s, BR_WIDTH), BF16)] * 2 + [jax.ShapeDtypeStruct((1, BR_WIDTH), F32)],
        compiler_params=_cp("arbitrary"),
    )(proj, proj, lb, dq, dk, dlf)


def _hg_chunk_common(q, k, lf):
    c = CHUNK
    g = _nn_exact(_tri(c, "lower"), lf)
    e_g = _exp(g)
    qd = q * e_g
    g_last = g[c - 1:c, :]
    e_t = _exp(g_last - g)
    kt = k * e_t
    tail = _exp(g_last)
    q_sc, k_sc, e_q, e_k = [], [], [], []
    for i in range(c // SUB):
        g_ref = g[i * SUB:i * SUB + 1, :]
        eq = _exp(g[i * SUB:(i + 1) * SUB, :] - g_ref)
        ek = _hmap(lambda gr, g_: jnp.exp(jnp.minimum(gr - g_, EXP_CLAMP)), g_ref, g)
        e_q.append(eq)
        e_k.append(ek)
        q_sc.append(q[i * SUB:(i + 1) * SUB, :] * eq)
        k_sc.append(k * ek)
    a_mat = _stack_rows([_nt(qi, ki) for qi, ki in zip(q_sc, k_sc)]) * _tri(c, "lower")
    return dict(e_g=e_g, qd=qd, e_t=e_t, kt=kt, tail=tail, q_sc=q_sc, k_sc=k_sc, e_q=e_q, e_k=e_k, a_mat=a_mat)


def _hg_chunk_fwd_math(q, k, v, lf, stt):
    m = _hg_chunk_common(q, k, lf)
    o = _nt(m["qd"], stt) + _nn(m["a_mat"], v)
    stt2 = stt * m["tail"] + _tn(v, m["kt"])
    return o, stt2


def _hg_chunk_bwd_math(q, k, v, lf, stt, do, dstt2):
    c = CHUNK
    m = _hg_chunk_common(q, k, lf)
    stt2 = stt * m["tail"] + _tn(v, m["kt"])
    later = _sum(stt2 * dstt2, 0)
    dqd = _dot3(do, stt, 1, 0)
    dstt = _tn(do, m["qd"]) + dstt2 * m["tail"]
    d_a = _dot3(do, v, 1, 1) * _tri(c, "lower")
    dv = _tn(m["a_mat"], do) + _nt(m["kt"], dstt2)
    dkt = _dot3(v, dstt2, 1, 0)
    dq_parts = []
    dk = dkt * m["e_t"]
    for i in range(c // SUB):
        d_ai = d_a[i * SUB:(i + 1) * SUB, :]
        dq_parts.append(_dot3(d_ai, m["k_sc"][i], 1, 0) * m["e_q"][i])
        dk = dk + _dot3(d_ai, m["q_sc"][i], 0, 0) * m["e_k"][i]
    dq = dqd * m["e_g"] + _stack_rows(dq_parts)
    db = q * dq - k * dk
    dlf = _nn_exact(_tri(c, "upper"), db) + later
    return dq, dk, dv, dlf, dstt


def _hg_chunk_fwd(qh, kh, proj, lf, *, name):
    s = qh.shape[0]
    n = s // CHUNK
    vb = C_HI // BR_WIDTH

    def body(q_ref, k_ref, v_ref, lf_ref, o_ref, st_out_ref, st_ref):
        @pl.when(pl.program_id(0) == 0)
        def _():
            st_ref[...] = jnp.zeros_like(st_ref)

        st = _Heads(st_ref[h] for h in range(N_HEADS))
        o, st2 = _hg_chunk_fwd_math(_heads_of(q_ref), _heads_of(k_ref), _heads_of(v_ref), _heads_of(lf_ref), st)
        for h in range(N_HEADS):
            st_out_ref[0, h] = st.v[h]
            o_ref[:, h * HEAD_DIM:(h + 1) * HEAD_DIM] = o.v[h]
            st_ref[h] = st2.v[h]

    blk = lambda off: pl.BlockSpec((CHUNK, BR_WIDTH), lambda c: (c, off))
    return pl.pallas_call(
        body, name=name, grid=(n,), in_specs=[blk(0), blk(0), blk(vb), blk(0)],
        out_specs=[blk(0), pl.BlockSpec((1, N_HEADS, HEAD_DIM, HEAD_DIM), lambda c: (c, 0, 0, 0))],
        out_shape=[jax.ShapeDtypeStruct((s, BR_WIDTH), F32), jax.ShapeDtypeStruct((n, N_HEADS, HEAD_DIM, HEAD_DIM), F32)],
        scratch_shapes=[pltpu.VMEM((N_HEADS, HEAD_DIM, HEAD_DIM), F32)],
        compiler_params=_cp("arbitrary"),
    )(qh, kh, proj, lf)


def _hg_chunk_bwd(qh, kh, proj, lf, states, do, *, name):
    s = qh.shape[0]
    n = s // CHUNK
    vb = C_HI // BR_WIDTH

    def body(q_ref, k_ref, v_ref, lf_ref, st_in_ref, do_ref, dq_ref, dk_ref, dv_ref, dlf_ref, dst_ref):
        @pl.when(pl.program_id(0) == 0)
        def _():
            dst_ref[...] = jnp.zeros_like(dst_ref)

        dq, dk, dv, dlf, dst = _hg_chunk_bwd_math(
            _heads_of(q_ref), _heads_of(k_ref), _heads_of(v_ref), _heads_of(lf_ref),
            _Heads(st_in_ref[0, h] for h in range(N_HEADS)), _heads_of(do_ref),
            _Heads(dst_ref[h] for h in range(N_HEADS)))
        for h in range(N_HEADS):
            cols = slice(h * HEAD_DIM, (h + 1) * HEAD_DIM)
            dq_ref[:, cols] = dq.v[h]
            dk_ref[:, cols] = dk.v[h]
            dv_ref[:, cols] = dv.v[h].astype(BF16)
            dlf_ref[:, cols] = dlf.v[h]
            dst_ref[h] = dst.v[h]

    blk = lambda off: pl.BlockSpec((CHUNK, BR_WIDTH), lambda c: (n - 1 - c, off))
    return pl.pallas_call(
        body, name=name, grid=(n,),
        in_specs=[blk(0), blk(0), blk(vb), blk(0),
                  pl.BlockSpec((1, N_HEADS, HEAD_DIM, HEAD_DIM), lambda c: (n - 1 - c, 0, 0, 0)), blk(0)],
        out_specs=[blk(0), blk(0), blk(0), blk(0)],
        out_shape=[jax.ShapeDtypeStruct((s, BR_WIDTH), F32)] * 2 + [jax.ShapeDtypeStruct((s, BR_WIDTH), BF16),
                                                                    jax.ShapeDtypeStruct((s, BR_WIDTH), F32)],
        scratch_shapes=[pltpu.VMEM((N_HEADS, HEAD_DIM, HEAD_DIM), F32)],
        compiler_params=_cp("arbitrary"),
    )(qh, kh, proj, lf, states, do)


_ANY = pl.BlockSpec(memory_space=pl.ANY)
_MESH = pl.DeviceIdType.MESH


def _all_gather(x_local, *, name, after=()):
    n_after = len(after)

    def body(x_ref, *refs):
        out_ref, send_sems, recv_sems, local_sem = refs[n_after:]
        x, y, c = lax.axis_index("x"), lax.axis_index("y"), lax.axis_index("c")
        me, sibling = (x, y, c), (x, y, 1 - c)
        chips = [(1 - x, y), (x, 1 - y), (1 - x, 1 - y)]

        def slot(px, py, pc):
            return out_ref.at[4 * px + 2 * py + pc]

        def copy(k, block, to, src=None):
            return pltpu.make_async_remote_copy(
                src_ref=slot(*block) if src is None else src, dst_ref=slot(*block),
                send_sem=send_sems.at[k], recv_sem=recv_sems.at[k], device_id=to, device_id_type=_MESH)

        mine = pltpu.make_async_copy(x_ref, slot(*me), local_sem)
        mine.start()
        first = [copy(0, me, sibling, src=x_ref)]
        first += [copy(1 + j, me, (*chip, c), src=x_ref) for j, chip in enumerate(chips)]
        for cp in first:
            cp.start()
        passed = [copy(4 + j, (*chip, c), sibling) for j, chip in enumerate(chips)]
        for j, chip in enumerate(chips):
            copy(1 + j, (*chip, c), me).wait_recv()
            passed[j].start()
        copy(0, sibling, me).wait_recv()
        for j, chip in enumerate(chips):
            copy(4 + j, (*chip, 1 - c), me).wait_recv()
        for cp in first + passed:
            cp.wait_send()
        mine.wait()

    return pl.pallas_call(
        body, name=name, out_shape=jax.ShapeDtypeStruct((N_DEV,) + x_local.shape, x_local.dtype),
        in_specs=[_ANY] * (1 + n_after), out_specs=_ANY,
        scratch_shapes=[pltpu.SemaphoreType.DMA((7,)), pltpu.SemaphoreType.DMA((7,)), pltpu.SemaphoreType.DMA],
    )(x_local, *after)


_HBM = pl.BlockSpec(memory_space=pltpu.HBM)
_SEM = pl.BlockSpec(memory_space=pltpu.SEMAPHORE)
_EFFECT = pltpu.SideEffectType.DATAFLOW_SIDE_EFFECTING


def _peers():
    x, y, c = lax.axis_index("x"), lax.axis_index("y"), lax.axis_index("c")
    out = []
    for k in range(1, N_DEV):
        px, py, pc = x ^ ((k >> 2) & 1), y ^ ((k >> 1) & 1), c ^ (k & 1)
        out.append(((px, py, pc), 4 * px + 2 * py + pc))
    return 4 * x + 2 * y + c, out


def _push_copies(src_ref, land_ref, send_sems, recv_sems, broadcast):
    my, peers = _peers()
    pairs = []
    for k, (pos, idx) in enumerate(peers):
        src = src_ref if broadcast else src_ref.at[idx]
        send = pltpu.make_async_remote_copy(src_ref=src, dst_ref=land_ref.at[my], send_sem=send_sems.at[k],
                                            recv_sem=recv_sems.at[k], device_id=pos, device_id_type=_MESH)
        recv = pltpu.make_async_remote_copy(src_ref=src, dst_ref=land_ref.at[idx], send_sem=send_sems.at[k],
                                            recv_sem=recv_sems.at[k], device_id=pos, device_id_type=_MESH)
        pairs.append((send, recv))
    return pairs


def _push_start(src, land, *, broadcast, name, after=()):
    n_after = len(after)

    def body(src_ref, land_ref, *refs):
        send_sems, recv_sems, _, _, token = refs[n_after:]
        for send, _ in _push_copies(src_ref, land_ref, send_sems, recv_sems, broadcast):
            send.start()
        token[...] = jnp.zeros_like(token)

    return pl.pallas_call(
        body, name=name,
        out_shape=(pltpu.SemaphoreType.DMA((N_DEV - 1,)), pltpu.SemaphoreType.DMA((N_DEV - 1,)),
                   pltpu.HBM(src.shape, src.dtype), pltpu.HBM(land.shape, land.dtype), jax.ShapeDtypeStruct((8, 128), F32)),
        in_specs=(_HBM, _HBM) + (_ANY,) * n_after, out_specs=(_SEM, _SEM, _HBM, _HBM, pl.BlockSpec(memory_space=pltpu.VMEM)),
        input_output_aliases={0: 2, 1: 3}, compiler_params=pltpu.CompilerParams(has_side_effects=_EFFECT),
    )(pltpu.with_memory_space_constraint(src, pltpu.HBM), pltpu.with_memory_space_constraint(land, pltpu.HBM), *after)


def _push_wait(handle, after, *, broadcast, name):
    send_sems, recv_sems, src_thru, land_thru, _ = handle

    def body(src_ref, land_ref, send_sems, recv_sems, *rest):
        for send, recv in _push_copies(src_ref, land_ref, send_sems, recv_sems, broadcast):
            send.wait_send()
            recv.wait_recv()

    return pl.pallas_call(
        body, name=name,
        out_shape=(pltpu.HBM(src_thru.shape, src_thru.dtype), pltpu.HBM(land_thru.shape, land_thru.dtype)),
        in_specs=(_HBM, _HBM, _SEM, _SEM) + (_ANY,) * len(after), out_specs=(_HBM, _HBM),
        input_output_aliases={0: 0, 1: 1}, compiler_params=pltpu.CompilerParams(has_side_effects=_EFFECT),
    )(src_thru, land_thru, send_sems, recv_sems, *after)[1]


def _adamw(parts, row_off, w, m, v, *, layer=0, n_layers=1, prev=None, name, tr):
    rows, c = w.shape
    r = rows // n_layers
    np_ = parts.shape[0]
    tr = min(tr, r)
    assert r % tr == 0 and row_off % tr == 0
    ob, lb = row_off // tr, layer * (r // tr)
    c1 = 1.0 - ADAM_B1 ** ADAM_STEP
    c2 = 1.0 - ADAM_B2 ** ADAM_STEP
    n_prev = 0 if prev is None else 4

    def body(p_ref, w_ref, m_ref, v_ref, *refs):
        g_ref, d_ref, nm_ref, nv_ref = refs[n_prev:]
        g = p_ref[0].astype(F32)
        for s in range(1, np_):
            g = g + p_ref[s].astype(F32)
        wv = w_ref[...]
        m2 = ADAM_B1 * m_ref[...] + (1.0 - ADAM_B1) * g
        v2 = ADAM_B2 * v_ref[...] + (1.0 - ADAM_B2) * jnp.square(g)
        m_hat = m2 / c1
        v_hat = v2 / c2
        g_ref[...] = g
        d_ref[...] = -ADAM_LR * (m_hat / (jnp.sqrt(v_hat) + ADAM_EPS) + ADAM_WD * wv)
        nm_ref[...] = m2
        nv_ref[...] = v2

    blk = pl.BlockSpec((tr, c), lambda i: (lb + i, 0))
    return pl.pallas_call(
        body, name=name, grid=(r // tr,),
        in_specs=[pl.BlockSpec((np_, tr, c), lambda i: (0, ob + i, 0)), blk, blk, blk] + [_ANY] * n_prev,
        out_specs=[blk] * 4, out_shape=[jax.ShapeDtypeStruct((rows, c), F32)] * 4,
        input_output_aliases={4 + i: i for i in range(n_prev)}, compiler_params=_cp("parallel"),
    )(parts, w, m, v, *(prev or ()))


def _sum_parts(parts, *, name):
    np_, r, c = parts.shape

    def body(p_ref, o_ref):
        g = p_ref[0]
        for s in range(1, np_):
            g = g + p_ref[s]
        o_ref[...] = g

    return pl.pallas_call(body, name=name, out_shape=jax.ShapeDtypeStruct((r, c), F32))(parts)


def _pack(arrs):
    rows = []
    for a in arrs:
        f = a.reshape(-1).astype(F32)
        pad = (-f.shape[0]) % 128
        rows.append(jnp.pad(f, (0, pad)).reshape(-1, 128))
    out = jnp.concatenate(rows, axis=0)
    return jnp.pad(out, ((0, (-out.shape[0]) % 8), (0, 0)))


def _unpack(packed, shapes):
    outs, r0 = [], 0
    for shp in shapes:
        n = 1
        for d in shp:
            n *= d
        nr = -(-n // 128)
        outs.append(packed[r0:r0 + nr].reshape(-1)[:n].reshape(shp))
        r0 += nr
    return outs


_WIN_PIECES = ((0, 4096, 0), (4112, 8208, 0), (4096, 4104, HEAD_DIM - N_HEADS), (4104, 4112, HEAD_DIM - N_HEADS))


def _win_from_shards(shards):
    cols = []
    for lo, hi, pad in _WIN_PIECES:
        for j in range(N_DEV):
            a, b = max(lo, j * SHARD_IN), min(hi, (j + 1) * SHARD_IN)
            if a < b:
                cols.append(shards[j][:, a - j * SHARD_IN:b - j * SHARD_IN])
        if pad:
            cols.append(jnp.zeros((shards[0].shape[0], pad), shards[0].dtype))
    return jnp.concatenate(cols, axis=1)


def _win_to_shards(g):
    starts, off = [], 0
    for lo, hi, pad in _WIN_PIECES:
        starts.append((lo, hi, off))
        off += hi - lo + pad
    shards = []
    for j in range(N_DEV):
        cols = []
        for lo, hi, off in sorted(starts):
            a, b = max(lo, j * SHARD_IN), min(hi, (j + 1) * SHARD_IN)
            if a < b:
                cols.append(g[:, off + a - lo:off + b - lo])
        shards.append(jnp.concatenate(cols, axis=1))
    return shards


def _lower_bounds(logits):
    probs = jax.nn.softmax(logits.astype(F32), axis=0)
    return jnp.cumsum(probs, axis=0) - probs[0]


def _pad_lanes(vec8):
    return jnp.pad(vec8.reshape(1, N_HEADS), ((0, 0), (0, HEAD_DIM - N_HEADS)))


def kernel(x, p, norm_w, w_in, dn_conv_w, dn_A_log, dn_dt_bias, dn_norm_w, hg_lb_logits, hg_norm_w, w_out, w_ple_up, w_ple_gate, final_norm_w, loss_target, m_norm_w, m_w_in, m_dn_conv_w, m_dn_A_log, m_dn_dt_bias, m_dn_norm_w, m_hg_lb_logits, m_hg_norm_w, m_w_out, m_w_ple_up, m_w_ple_gate, m_final_norm_w, v_norm_w, v_w_in, v_dn_conv_w, v_dn_A_log, v_dn_dt_bias, v_dn_norm_w, v_hg_lb_logits, v_hg_norm_w, v_w_out, v_w_ple_up, v_w_ple_gate, v_final_norm_w):
    depth = norm_w.shape[0]
    my = 4 * lax.axis_index("x") + 2 * lax.axis_index("y") + lax.axis_index("c")
    h = x[0]
    tgt = loss_target[0]
    rows_out = D_MODEL // N_DEV
    up_rows = PLE_DIM * (D_MODEL // N_DEV) // D_MODEL
    g_off, u_off = rows_out, 2 * rows_out

    def own_slot(block):
        return lax.dynamic_update_index_in_dim(lax.empty((N_DEV,) + block.shape, block.dtype), block, my, 0)

    win_bf = w_in.astype(BF16)
    rest_bf = [jnp.concatenate([w_out[l], w_ple_gate[l], w_ple_up[l].reshape(up_rows, D_MODEL)], axis=0).astype(BF16)
               for l in range(depth)]
    conv_all = _all_gather(dn_conv_w, name="gather_conv_w")
    conv_full = conv_all.transpose(1, 2, 0, 3).reshape(depth, CONV_W, 3 * BR_WIDTH)
    win_all = {0: _all_gather(win_bf[0], name="gather_w_in_l0", after=[conv_all])}
    pending = {}
    last = win_all[0]
    for l in range(depth):
        if l > 0:
            pending["win", l] = _push_start(win_bf[l], own_slot(win_bf[l]), broadcast=True, after=[last],
                                            name=f"gather_w_in_l{l}_start")
            last = pending["win", l][4]
        pending["rest", l] = _push_start(rest_bf[l], own_slot(rest_bf[l]), broadcast=True, after=[last],
                                         name=f"gather_rest_l{l}_start")
        last = pending["rest", l][4]
    order_tok = last[0, 0]
    lbs = _lower_bounds(hg_lb_logits)

    saved = []
    weights = []
    for l in range(depth):
        tag = f"l{l}"
        if l > 0:
            win_all[l] = _push_wait(pending["win", l], [h], broadcast=True, name=f"gather_w_in_{tag}_wait")
        wi = _win_from_shards([win_all[l][j] for j in range(N_DEV)])
        nw = norm_w[l] + order_tok if l == 0 else norm_w[l]
        hn = _rms_fwd(h, nw, name=f"rms_fwd_{tag}")
        proj = _mm(hn, wi, mode="nn", out_dtype=F32, name=f"mm_proj_{tag}")
        al, dt = _pad_lanes(dn_A_log[l]), _pad_lanes(dn_dt_bias[l])
        qkv = _dn_qkv_fwd(proj, conv_full[l], name=f"dn_qkv_fwd_{tag}")
        beta, gcs = _dn_gate_fwd(proj, al, dt, name=f"dn_gate_fwd_{tag}")
        o_dn, st_dn = _dn_chunk_fwd(qkv, gcs, beta, name=f"dn_chunk_fwd_{tag}")
        lb = lbs[l].reshape(1, BR_WIDTH)
        qh, kh, lf = _hg_prep_fwd(proj, lb, name=f"hg_prep_fwd_{tag}")
        o_hg, st_hg = _hg_chunk_fwd(qh, kh, proj, lf, name=f"hg_chunk_fwd_{tag}")
        y_dn = _hnorm_fwd(o_dn, proj, C_Z, dn_norm_w[l], name=f"hnorm_dn_fwd_{tag}")
        y_hg = _hnorm_fwd(o_hg, proj, C_HZ, hg_norm_w[l], name=f"hnorm_hg_fwd_{tag}")
        y = jnp.concatenate([y_dn, y_hg], axis=1)
        rest_all = _push_wait(pending["rest", l], [y], broadcast=True, name=f"gather_rest_{tag}_wait")
        wo = rest_all[:, 0:rows_out].reshape(D_MODEL, D_MODEL)
        wg = rest_all[:, g_off:g_off + rows_out].reshape(D_MODEL, D_MODEL)
        wu = rest_all[:, u_off:u_off + up_rows].reshape(N_DEV, PLE_DIM, D_MODEL // N_DEV).transpose(1, 0, 2).reshape(PLE_DIM, D_MODEL)
        weights.append((wi, wo, wg, wu))
        h1 = _mm(y, wo, mode="nn", out_dtype=F32, res=h, name=f"mm_out_{tag}")
        gp = _mm(h1, wg, mode="nn", out_dtype=F32, name=f"mm_gate_{tag}")
        up = _mm(p[l, 0], wu, mode="nn", out_dtype=F32, name=f"mm_up_{tag}")
        h2 = _ple_fwd(h1, gp, up, name=f"ple_fwd_{tag}")
        saved.append(dict(h=h, hn=hn, proj=proj, qkv=qkv, beta=beta, gcs=gcs, st_dn=st_dn, qh=qh, kh=kh, lf=lf,
                          st_hg=st_hg, o_dn=o_dn, o_hg=o_hg, y=y, h1=h1, gp=gp, up=up, al=al, dt=dt, lb=lb))
        h = h2

    loss_row, dh, d_final_w = _final_fwd_bwd(h, final_norm_w, tgt, name="final_norm_loss")

    d_norm_w, d_alog, d_dt, d_dn_nw, d_hg_nw, d_lb, d_conv = ([None] * depth for _ in range(7))
    sent = {}
    for l in reversed(range(depth)):
        wi, wo, wg, wu = weights[l]
        sv = saved[l]
        tag = f"l{l}"
        dup, dgp = _ple_bwd(dh, sv["gp"], sv["up"], name=f"ple_bwd_{tag}")
        d_wu = _mm(p[l, 0], dup, mode="tn", out_dtype=BF16, name=f"mm_dwup_{tag}")
        d_wg = _mm(sv["h1"], dgp, mode="tn", out_dtype=BF16, name=f"mm_dwgate_{tag}")
        dh1 = _mm(dgp, wg, mode="nt", out_dtype=F32, res=dh, name=f"mm_dh1_{tag}")
        d_wo = _mm(sv["y"], dh1, mode="tn", out_dtype=BF16, name=f"mm_dwout_{tag}")
        parts_rest = jnp.concatenate(
            [d_wo.reshape(N_DEV, rows_out, D_MODEL), d_wg.reshape(N_DEV, rows_out, D_MODEL),
             d_wu.reshape(PLE_DIM, N_DEV, D_MODEL // N_DEV).transpose(1, 0, 2).reshape(N_DEV, up_rows, D_MODEL)], axis=1)
        sent["rest", l] = _push_start(parts_rest, own_slot(parts_rest[my]), broadcast=False, name=f"exchange_rest_{tag}_start")
        dy = _mm(dh1, wo, mode="nt", out_dtype=F32, name=f"mm_dy_{tag}")
        dn_nw = dn_norm_w[l] + sent["rest", l][4][0, 0]
        do_dn, dz_dn, d_dn_nw[l] = _hnorm_bwd(sv["o_dn"], sv["proj"], C_Z, dn_nw, dy, 0, name=f"hnorm_dn_bwd_{tag}")
        do_hg, dz_hg, d_hg_nw[l] = _hnorm_bwd(sv["o_hg"], sv["proj"], C_HZ, hg_norm_w[l], dy, BR_WIDTH, name=f"hnorm_hg_bwd_{tag}")
        dqkv, d_gc, dbeta = _dn_chunk_bwd(sv["qkv"], sv["gcs"], sv["beta"], sv["st_dn"], do_dn, name=f"dn_chunk_bwd_{tag}")
        dqkv_pre, d_conv[l] = _dn_qkv_bwd(sv["proj"], conv_full[l], dqkv, name=f"dn_qkv_bwd_{tag}")
        db, da, d_alog[l], d_dt[l] = _dn_gate_bwd(sv["proj"], sv["al"], sv["dt"], dbeta, d_gc, name=f"dn_gate_bwd_{tag}")
        dqh, dkh, dhi, dlf = _hg_chunk_bwd(sv["qh"], sv["kh"], sv["proj"], sv["lf"], sv["st_hg"], do_hg, name=f"hg_chunk_bwd_{tag}")
        dhq, dhf, d_lb[l] = _hg_prep_bwd(sv["proj"], sv["lb"], dqh, dkh, dlf, name=f"hg_prep_bwd_{tag}")
        dproj = jnp.concatenate([dqkv_pre, dz_dn, dhq, dhf, dhi, dz_hg, db, da], axis=1)
        dhn = _mm(dproj, wi, mode="nt", out_dtype=F32, name=f"mm_dhn_{tag}")
        dh, d_norm_w[l] = _rms_bwd(sv["h"], norm_w[l], dhn, dh1, name=f"rms_bwd_{tag}")
        before_dwin = [dh]
        if l == 0:
            small = _pack([loss_row, jnp.concatenate(d_norm_w, axis=0), d_final_w,
                           jnp.stack([a[0, :N_HEADS] for a in d_alog]), jnp.stack([a[0, :N_HEADS] for a in d_dt]),
                           jnp.concatenate(d_dn_nw, axis=0), jnp.concatenate(d_hg_nw, axis=0), jnp.concatenate(d_lb, axis=0),
                           jnp.stack(d_conv)])
            small_all = _all_gather(small, name="gather_small")
            before_dwin = [small_all]
        d_win = _mm(sv["hn"], dproj, mode="tn", out_dtype=BF16, after=before_dwin, name=f"mm_dwin_{tag}")
        parts_in = jnp.stack(_win_to_shards(d_win))
        sent["win", l] = _push_start(parts_in, own_slot(parts_in[my]), broadcast=False, name=f"exchange_w_in_{tag}_start")
    grad_x = dh[None]

    small_shapes = [(1, 128), norm_w.shape, final_norm_w.shape, dn_A_log.shape, dn_dt_bias.shape, dn_norm_w.shape,
                    hg_norm_w.shape, hg_lb_logits.shape, (depth, CONV_W, 3 * BR_WIDTH)]
    tot = _unpack(_sum_parts(small_all, name="sum_small"), small_shapes)
    loss = tot[0][0, 0]
    g_lb = tot[7]
    g_logits = jax.vjp(_lower_bounds, hg_lb_logits)[1](g_lb)[0]
    g_conv = lax.dynamic_slice_in_dim(tot[8], my * (3 * BR_WIDTH // N_DEV), 3 * BR_WIDTH // N_DEV, axis=2)
    small_g = [tot[1], g_conv, tot[3], tot[4], tot[5], g_logits, tot[6], tot[2]]
    small_w = [norm_w, dn_conv_w, dn_A_log, dn_dt_bias, dn_norm_w, hg_lb_logits, hg_norm_w, final_norm_w]
    small_m = [m_norm_w, m_dn_conv_w, m_dn_A_log, m_dn_dt_bias, m_dn_norm_w, m_hg_lb_logits, m_hg_norm_w, m_final_norm_w]
    small_v = [v_norm_w, v_dn_conv_w, v_dn_A_log, v_dn_dt_bias, v_dn_norm_w, v_hg_lb_logits, v_hg_norm_w, v_final_norm_w]
    pk_w = _pack(small_w)
    res_small = _adamw(_pack(small_g)[None], 0, pk_w, _pack(small_m), _pack(small_v), name="adamw_small", tr=pk_w.shape[0])
    shapes_w = [a.shape for a in small_w]
    sg, sd, sm, sv_ = (_unpack(r, shapes_w) for r in res_small)

    r_win = r_wo = r_wg = r_wu = None
    done = [grad_x, res_small[0]]

    def flat(a, cols):
        return a.reshape(-1, cols)

    for l in reversed(range(depth)):
        tag = f"l{l}"
        land_rest = _push_wait(sent["rest", l], done, broadcast=False, name=f"exchange_rest_{tag}_wait")
        r_wo = _adamw(land_rest, 0, flat(w_out, D_MODEL), flat(m_w_out, D_MODEL), flat(v_w_out, D_MODEL), layer=l,
                      n_layers=depth, prev=r_wo, name=f"adamw_w_out_{tag}", tr=rows_out)
        r_wg = _adamw(land_rest, g_off, flat(w_ple_gate, D_MODEL), flat(m_w_ple_gate, D_MODEL), flat(v_w_ple_gate, D_MODEL),
                      layer=l, n_layers=depth, prev=r_wg, name=f"adamw_w_gate_{tag}", tr=rows_out)
        r_wu = _adamw(land_rest, u_off, flat(w_ple_up, D_MODEL), flat(m_w_ple_up, D_MODEL), flat(v_w_ple_up, D_MODEL),
                      layer=l, n_layers=depth, prev=r_wu, name=f"adamw_w_up_{tag}", tr=up_rows)
        done = [r_wo[0], r_wg[0], r_wu[0]]
    for l in reversed(range(depth)):
        tag = f"l{l}"
        land_in = _push_wait(sent["win", l], done, broadcast=False, name=f"exchange_w_in_{tag}_wait")
        r_win = _adamw(land_in, 0, flat(w_in, SHARD_IN), flat(m_w_in, SHARD_IN), flat(v_w_in, SHARD_IN), layer=l,
                       n_layers=depth, prev=r_win, name=f"adamw_w_in_{tag}", tr=256)
        done = [r_win[0]]
    r_win = [o.reshape(w_in.shape) for o in r_win]
    r_wo = [o.reshape(w_out.shape) for o in r_wo]
    r_wg = [o.reshape(w_ple_gate.shape) for o in r_wg]
    r_wu = [o.reshape(w_ple_up.shape) for o in r_wu]

    def order(small_list, big_in, big_out, big_up, big_gate):
        nw, cw, al_, dt_, dnw, lbl, hnw, fw = small_list
        return [nw, big_in, cw, al_, dt_, dnw, lbl, hnw, big_out, big_up, big_gate, fw]

    outs = [loss, grad_x]
    for i, sl in enumerate((sg, sd, sm, sv_)):
        outs += order(sl, r_win[i], r_wo[i], r_wu[i], r_wg[i])
    return tuple(outs)
```

```python
import functools

import jax
import jax.numpy as jnp
from jax import lax
from jax.experimental import pallas as pl
from jax.experimental.pallas import tpu as pltpu

F32 = jnp.float32
BF16 = jnp.bfloat16
HIGHEST = lax.Precision.HIGHEST

N_DEV = 8
D_MODEL = 2048
PLE_DIM = 256
HEAD_DIM = 128
N_HEADS = 8
BR_WIDTH = N_HEADS * HEAD_DIM
CHUNK = 64
SUB = 16
CONV_W = 4
NORM_EPS = 1e-6
L2_EPS = 1e-6
IN_WIDTH = 8208
SHARD_IN = IN_WIDTH // N_DEV
EXP_CLAMP = 80.0

C_QKV, C_Z, C_HQ, C_HF, C_HI, C_HZ, C_B, C_A, N_PROJ = 0, 3072, 4096, 5120, 6144, 7168, 8192, 8320, 8448

ADAM_LR, ADAM_B1, ADAM_B2, ADAM_EPS, ADAM_WD, ADAM_STEP = 0.001, 0.9, 0.999, 1e-08, 0.01, 10

VMEM_LIMIT = 48 * 1024 * 1024


def _cp(*sem):
    return pltpu.CompilerParams(dimension_semantics=sem, vmem_limit_bytes=VMEM_LIMIT)


class _Heads:
    def __init__(self, vals):
        self.v = tuple(vals)

    def __add__(self, o):
        return _hmap(lambda a, b: a + b, self, o)

    def __radd__(self, o):
        return _hmap(lambda a, b: b + a, self, o)

    def __sub__(self, o):
        return _hmap(lambda a, b: a - b, self, o)

    def __rsub__(self, o):
        return _hmap(lambda a, b: b - a, self, o)

    def __mul__(self, o):
        return _hmap(lambda a, b: a * b, self, o)

    def __rmul__(self, o):
        return _hmap(lambda a, b: b * a, self, o)

    def __neg__(self):
        return _hmap(lambda a: -a, self)

    def __getitem__(self, idx):
        return _hmap(lambda a: a[idx], self)


def _hmap(fn, *args):
    n = next((len(a.v) for a in args if isinstance(a, _Heads)), None)
    if n is None:
        return fn(*args)
    return _Heads(fn(*[a.v[i] if isinstance(a, _Heads) else a for a in args]) for i in range(n))


def _dot(a, b, ca, cb):
    return _hmap(lambda x, y: lax.dot_general(x.astype(BF16), y.astype(BF16), (((ca,), (cb,)), ((), ())),
                                              preferred_element_type=F32), a, b)


def _nn(a, b):
    return _dot(a, b, 1, 0)


def _nt(a, b):
    return _dot(a, b, 1, 1)


def _tn(a, b):
    return _dot(a, b, 0, 0)


def _split(a):
    hi = _hmap(lambda x: x.astype(BF16), a)
    return hi, _hmap(lambda x, h: (x - h.astype(F32)).astype(BF16), a, hi)


def _dot3(a, b, ca, cb):
    ah, al = _split(a)
    bh, bl = _split(b)
    return _dot(ah, bh, ca, cb) + (_dot(ah, bl, ca, cb) + _dot(al, bh, ca, cb))


def _nn_exact(a, b):
    return _hmap(lambda y: lax.dot_general(a, y, (((1,), (0,)), ((), ())), precision=HIGHEST,
                                           preferred_element_type=F32), b)


def _exp(x):
    return _hmap(jnp.exp, x)


def _sum(x, axis):
    return _hmap(lambda a: jnp.sum(a, axis=axis, keepdims=True), x)


def _stack_rows(parts):
    return _hmap(lambda *xs: jnp.concatenate(xs, axis=0), *parts)


def _sigmoid(x):
    return jax.nn.sigmoid(x)


def _silu(x):
    return x * _sigmoid(x)


def _dsilu(x):
    s = _sigmoid(x)
    return s * (1.0 + x * (1.0 - s))


def _softplus(x):
    return jnp.maximum(x, 0.0) + jnp.log(1.0 + jnp.exp(-jnp.abs(x)))


def _iota2(n, m, axis):
    return lax.broadcasted_iota(jnp.int32, (n, m), axis)


def _col2row(col, eye):
    return _hmap(lambda c: jnp.sum(eye * c, axis=0, keepdims=True), col)


def _row2col(row, eye):
    return _hmap(lambda r: jnp.sum(eye * r, axis=1, keepdims=True), row)


def _pick_lane(block, lane_idx):
    lane = _iota2(block.shape[0], block.shape[1], 1)
    return jnp.sum(jnp.where(lane == lane_idx, block, 0.0), axis=1, keepdims=True)


MM_TILE_M, MM_TILE_N, MM_TILE_K = 1024, 1408, 2048


def _tile(dim, cap):
    if dim <= cap:
        return dim
    t = cap - cap % 128
    while dim % t:
        t -= 128
    return t


def _mm(a, b, *, mode, out_dtype, res=None, after=(), name):
    if mode == "nn":
        (m, kd), (_, n) = a.shape, b.shape
    elif mode == "nt":
        (m, kd), (n, _) = a.shape, b.shape
    else:
        (kd, m), (_, n) = a.shape, b.shape
    tm, tn, tk = _tile(m, MM_TILE_M), _tile(n, MM_TILE_N), _tile(kd, MM_TILE_K)
    assert m % tm == 0 and n % tn == 0 and kd % tk == 0, (m, n, kd, tm, tn, tk)
    nk = kd // tk
    ca, cb = {"nn": (1, 0), "nt": (1, 1), "tn": (0, 0)}[mode]

    def body(*refs):
        a_ref, b_ref = refs[:2]
        r_ref = None if res is None else refs[2]
        o_ref, acc_ref = refs[-2:]
        k = pl.program_id(2)

        @pl.when(k == 0)
        def _():
            acc_ref[...] = jnp.zeros_like(acc_ref)

        acc_ref[...] += _dot(a_ref[...], b_ref[...], ca, cb)

        @pl.when(k == nk - 1)
        def _():
            out = acc_ref[...]
            if r_ref is not None:
                out = out + r_ref[...].astype(F32)
            o_ref[...] = out.astype(o_ref.dtype)

    a_spec = pl.BlockSpec((tk, tm), lambda i, j, k: (k, i)) if mode == "tn" else pl.BlockSpec((tm, tk), lambda i, j, k: (i, k))
    b_spec = pl.BlockSpec((tn, tk), lambda i, j, k: (j, k)) if mode == "nt" else pl.BlockSpec((tk, tn), lambda i, j, k: (k, j))
    o_spec = pl.BlockSpec((tm, tn), lambda i, j, k: (i, j))
    in_specs = [a_spec, b_spec] + ([o_spec] if res is not None else []) + [pl.BlockSpec(memory_space=pl.ANY)] * len(after)
    args = (a, b) + ((res,) if res is not None else ()) + tuple(after)
    return pl.pallas_call(
        body, name=name, grid=(m // tm, n // tn, nk), in_specs=in_specs, out_specs=o_spec,
        out_shape=jax.ShapeDtypeStruct((m, n), out_dtype),
        scratch_shapes=[pltpu.VMEM((tm, tn), F32)],
        compiler_params=_cp("parallel", "parallel", "arbitrary"),
    )(*args)


ROW_TILE = 256


def _rms_fwd(h, w, *, name):
    s, d = h.shape
    tr = min(ROW_TILE, s)

    def body(h_ref, w_ref, o_ref):
        x = h_ref[...]
        r = lax.rsqrt(jnp.mean(x * x, axis=-1, keepdims=True) + NORM_EPS)
        o_ref[...] = (x * r * w_ref[...]).astype(o_ref.dtype)

    return pl.pallas_call(
        body, name=name, grid=(s // tr,),
        in_specs=[pl.BlockSpec((tr, d), lambda i: (i, 0)), pl.BlockSpec((1, d), lambda i: (0, 0))],
        out_specs=pl.BlockSpec((tr, d), lambda i: (i, 0)),
        out_shape=jax.ShapeDtypeStruct((s, d), BF16), compiler_params=_cp("parallel"),
    )(h, w.reshape(1, d))


def _rms_bwd_math(x, w, dy):
    d = x.shape[-1]
    r = lax.rsqrt(jnp.mean(x * x, axis=-1, keepdims=True) + NORM_EPS)
    gw = dy * w
    dx = r * gw - x * ((r * r * r) * (jnp.sum(gw * x, axis=-1, keepdims=True) / d))
    return dx, dy * x * r


def _rms_bwd(h, w, dhn, res, *, name):
    s, d = h.shape
    tr = min(ROW_TILE, s)

    def body(h_ref, w_ref, g_ref, r_ref, dh_ref, dw_ref):
        @pl.when(pl.program_id(0) == 0)
        def _():
            dw_ref[...] = jnp.zeros_like(dw_ref)

        dx, dwt = _rms_bwd_math(h_ref[...], w_ref[...], g_ref[...])
        dh_ref[...] = r_ref[...] + dx
        dw_ref[...] += jnp.sum(dwt, axis=0, keepdims=True)

    row = pl.BlockSpec((tr, d), lambda i: (i, 0))
    vec = pl.BlockSpec((1, d), lambda i: (0, 0))
    return pl.pallas_call(
        body, name=name, grid=(s // tr,), in_specs=[row, vec, row, row], out_specs=[row, vec],
        out_shape=[jax.ShapeDtypeStruct((s, d), F32), jax.ShapeDtypeStruct((1, d), F32)],
        compiler_params=_cp("arbitrary"),
    )(h, w.reshape(1, d), dhn, res)


def _final_fwd_bwd(h, w, tgt, *, name):
    s, d = h.shape
    tr = min(ROW_TILE, s)

    def body(h_ref, w_ref, t_ref, loss_ref, dh_ref, dw_ref):
        @pl.when(pl.program_id(0) == 0)
        def _():
            loss_ref[...] = jnp.zeros_like(loss_ref)
            dw_ref[...] = jnp.zeros_like(dw_ref)

        x = h_ref[...]
        wv = w_ref[...]
        r = lax.rsqrt(jnp.mean(x * x, axis=-1, keepdims=True) + NORM_EPS)
        err = x * r * wv - t_ref[...]
        row_loss = jnp.mean(err * err, axis=-1, keepdims=True)
        loss_ref[...] += 0.5 * jnp.sum(row_loss, axis=0, keepdims=True)
        dx, dwt = _rms_bwd_math(x, wv, err / d)
        dh_ref[...] = dx
        dw_ref[...] += jnp.sum(dwt, axis=0, keepdims=True)

    row = pl.BlockSpec((tr, d), lambda i: (i, 0))
    vec = pl.BlockSpec((1, d), lambda i: (0, 0))
    return pl.pallas_call(
        body, name=name, grid=(s // tr,), in_specs=[row, vec, row],
        out_specs=[pl.BlockSpec((1, 128), lambda i: (0, 0)), row, vec],
        out_shape=[jax.ShapeDtypeStruct((1, 128), F32), jax.ShapeDtypeStruct((s, d), F32),
                   jax.ShapeDtypeStruct((1, d), F32)],
        compiler_params=_cp("arbitrary"),
    )(h, w.reshape(1, d), tgt)


def _ple_fwd(h1, gate_pre, up, *, name):
    s, d = h1.shape
    tr = min(ROW_TILE, s)

    def body(h_ref, g_ref, u_ref, o_ref):
        o_ref[...] = h_ref[...] + u_ref[...] * _sigmoid(g_ref[...])

    row = pl.BlockSpec((tr, d), lambda i: (i, 0))
    return pl.pallas_call(body, name=name, grid=(s // tr,), in_specs=[row, row, row], out_specs=row,
                          out_shape=jax.ShapeDtypeStruct((s, d), F32), compiler_params=_cp("parallel"))(h1, gate_pre, up)


def _ple_bwd(dh2, gate_pre, up, *, name):
    s, d = dh2.shape
    tr = min(ROW_TILE, s)

    def body(d_ref, g_ref, u_ref, dup_ref, dgp_ref):
        dh = d_ref[...]
        gate = _sigmoid(g_ref[...])
        dup_ref[...] = (dh * gate).astype(BF16)
        dgp_ref[...] = (dh * u_ref[...] * gate * (1.0 - gate)).astype(BF16)

    row = pl.BlockSpec((tr, d), lambda i: (i, 0))
    return pl.pallas_call(body, name=name, grid=(s // tr,), in_specs=[row, row, row], out_specs=[row, row],
                          out_shape=[jax.ShapeDtypeStruct((s, d), BF16)] * 2, compiler_params=_cp("parallel"))(dh2, gate_pre, up)


HN_TILE = 512


def _hnorm_fwd(o, proj, z_col, w, *, name):
    s = o.shape[0]
    tr = min(HN_TILE, s)
    zb = z_col // HEAD_DIM

    def body(o_ref, z_ref, w_ref, y_ref):
        x = o_ref[...]
        r = lax.rsqrt(jnp.mean(x * x, axis=-1, keepdims=True) + NORM_EPS)
        y_ref[...] = (x * r * w_ref[...] * _silu(z_ref[...])).astype(BF16)

    blk = pl.BlockSpec((tr, HEAD_DIM), lambda i, h: (i, h))
    return pl.pallas_call(
        body, name=name, grid=(s // tr, N_HEADS),
        in_specs=[blk, pl.BlockSpec((tr, HEAD_DIM), lambda i, h: (i, zb + h)), pl.BlockSpec((1, HEAD_DIM), lambda i, h: (0, 0))],
        out_specs=blk, out_shape=jax.ShapeDtypeStruct((s, BR_WIDTH), BF16), compiler_params=_cp("parallel", "parallel"),
    )(o, proj, w.reshape(1, HEAD_DIM))


def _hnorm_bwd(o, proj, z_col, w, dy, dy_col, *, name):
    s = o.shape[0]
    tr = min(HN_TILE, s)
    zb, yb = z_col // HEAD_DIM, dy_col // HEAD_DIM

    def body(o_ref, z_ref, w_ref, dy_ref, do_ref, dz_ref, dw_ref):
        @pl.when((pl.program_id(0) == 0) & (pl.program_id(1) == 0))
        def _():
            dw_ref[...] = jnp.zeros_like(dw_ref)

        x, z, wv, g = o_ref[...], z_ref[...], w_ref[...], dy_ref[...]
        r = lax.rsqrt(jnp.mean(x * x, axis=-1, keepdims=True) + NORM_EPS)
        on = x * r * wv
        don = g * _silu(z)
        dz_ref[...] = (g * on * _dsilu(z)).astype(BF16)
        gw = don * wv
        do_ref[...] = r * gw - x * ((r * r * r) * (jnp.sum(gw * x, axis=-1, keepdims=True) / HEAD_DIM))
        dw_ref[...] += jnp.sum(don * x * r, axis=0, keepdims=True)

    blk = pl.BlockSpec((tr, HEAD_DIM), lambda i, h: (i, h))
    vec = pl.BlockSpec((1, HEAD_DIM), lambda i, h: (0, 0))
    return pl.pallas_call(
        body, name=name, grid=(s // tr, N_HEADS),
        in_specs=[blk, pl.BlockSpec((tr, HEAD_DIM), lambda i, h: (i, zb + h)), vec,
                  pl.BlockSpec((tr, HEAD_DIM), lambda i, h: (i, yb + h))],
        out_specs=[blk, blk, vec],
        out_shape=[jax.ShapeDtypeStruct((s, BR_WIDTH), F32), jax.ShapeDtypeStruct((s, BR_WIDTH), BF16),
                   jax.ShapeDtypeStruct((1, HEAD_DIM), F32)],
        compiler_params=_cp("arbitrary", "arbitrary"),
    )(o, proj, w.reshape(1, HEAD_DIM), dy)


def _conv_silu(x, w, s):
    row = _iota2(s, x.shape[1], 0)
    c = w[CONV_W - 1:CONV_W, :] * x
    for k in range(1, CONV_W):
        c = c + w[CONV_W - 1 - k:CONV_W - k, :] * jnp.where(row >= k, pltpu.roll(x, k, 0), 0.0)
    return c


def _dn_qkv_fwd(proj, conv_w, *, name):
    s = proj.shape[0]
    nb = 3 * N_HEADS

    def body(x_ref, w_ref, o_ref):
        j = pl.program_id(0)
        sv = _silu(_conv_silu(x_ref[...], w_ref[...], s))
        r = lax.rsqrt(jnp.sum(sv * sv, axis=-1, keepdims=True) + L2_EPS)
        scale = jnp.where(j < N_HEADS, HEAD_DIM ** -0.5, 1.0).astype(F32)
        o_ref[...] = jnp.where(j < 2 * N_HEADS, sv * r * scale, sv)

    return pl.pallas_call(
        body, name=name, grid=(nb,),
        in_specs=[pl.BlockSpec((s, HEAD_DIM), lambda j: (0, j)), pl.BlockSpec((CONV_W, HEAD_DIM), lambda j: (0, j))],
        out_specs=pl.BlockSpec((s, HEAD_DIM), lambda j: (0, j)),
        out_shape=jax.ShapeDtypeStruct((s, 3 * BR_WIDTH), F32), compiler_params=_cp("parallel"),
    )(proj, conv_w)


def _dn_qkv_bwd(proj, conv_w, dqkv, *, name):
    s = proj.shape[0]
    nb = 3 * N_HEADS

    def body(x_ref, w_ref, g_ref, dx_ref, dw_ref):
        j = pl.program_id(0)
        x, w, g = x_ref[...], w_ref[...], g_ref[...]
        c = _conv_silu(x, w, s)
        sv = _silu(c)
        r = lax.rsqrt(jnp.sum(sv * sv, axis=-1, keepdims=True) + L2_EPS)
        scale = jnp.where(j < N_HEADS, HEAD_DIM ** -0.5, 1.0).astype(F32)
        ds_n = scale * (r * g - sv * ((r * r * r) * jnp.sum(g * sv, axis=-1, keepdims=True)))
        dc = jnp.where(j < 2 * N_HEADS, ds_n, g) * _dsilu(c)
        row = _iota2(s, HEAD_DIM, 0)
        dx = w[CONV_W - 1:CONV_W, :] * dc
        dws = [jnp.sum(dc * x, axis=0, keepdims=True)]
        for k in range(1, CONV_W):
            dx = dx + w[CONV_W - 1 - k:CONV_W - k, :] * jnp.where(row < s - k, pltpu.roll(dc, s - k, 0), 0.0)
            dws.append(jnp.sum(dc * jnp.where(row >= k, pltpu.roll(x, k, 0), 0.0), axis=0, keepdims=True))
        dx_ref[...] = dx.astype(BF16)
        for k in range(CONV_W):
            dw_ref[CONV_W - 1 - k:CONV_W - k, :] = dws[k]

    blk = pl.BlockSpec((s, HEAD_DIM), lambda j: (0, j))
    wblk = pl.BlockSpec((CONV_W, HEAD_DIM), lambda j: (0, j))
    return pl.pallas_call(
        body, name=name, grid=(nb,), in_specs=[blk, wblk, blk], out_specs=[blk, wblk],
        out_shape=[jax.ShapeDtypeStruct((s, 3 * BR_WIDTH), BF16), jax.ShapeDtypeStruct((CONV_W, 3 * BR_WIDTH), F32)],
        compiler_params=_cp("parallel"),
    )(proj, conv_w, dqkv)


def _tri(n, kind):
    r, c = _iota2(n, n, 0), _iota2(n, n, 1)
    if kind == "lower":
        return (r >= c).astype(F32)
    if kind == "upper":
        return (r <= c).astype(F32)
    return (r == c).astype(F32)


def _dn_gate_fwd(proj, a_log, dt_bias, *, name):
    s = proj.shape[0]

    def body(b_ref, a_ref, al_ref, dt_ref, beta_ref, g_ref):
        beta_ref[...] = _sigmoid(b_ref[...])
        g = -jnp.exp(al_ref[...]) * _softplus(a_ref[...] + dt_ref[...])
        g_ref[...] = _nn_exact(_tri(CHUNK, "lower"), g)

    blk = lambda cb: pl.BlockSpec((CHUNK, HEAD_DIM), lambda i: (i, cb))
    vec = pl.BlockSpec((1, HEAD_DIM), lambda i: (0, 0))
    out = pl.BlockSpec((CHUNK, HEAD_DIM), lambda i: (i, 0))
    return pl.pallas_call(
        body, name=name, grid=(s // CHUNK,), in_specs=[blk(C_B // HEAD_DIM), blk(C_A // HEAD_DIM), vec, vec],
        out_specs=[out, out], out_shape=[jax.ShapeDtypeStruct((s, HEAD_DIM), F32)] * 2, compiler_params=_cp("parallel"),
    )(proj, proj, a_log, dt_bias)


def _dn_gate_bwd(proj, a_log, dt_bias, dbeta, d_g, *, name):
    s = proj.shape[0]

    def body(b_ref, a_ref, al_ref, dt_ref, dbeta_ref, dG_ref, db_ref, da_ref, dal_ref, ddt_ref):
        @pl.when(pl.program_id(0) == 0)
        def _():
            dal_ref[...] = jnp.zeros_like(dal_ref)
            ddt_ref[...] = jnp.zeros_like(ddt_ref)

        beta = _sigmoid(b_ref[...])
        db_ref[...] = (dbeta_ref[...] * beta * (1.0 - beta)).astype(BF16)
        pre = a_ref[...] + dt_ref[...]
        neg_ea = -jnp.exp(al_ref[...])
        dg = _nn_exact(_tri(CHUNK, "upper"), dG_ref[...])
        da = dg * neg_ea * _sigmoid(pre)
        da_ref[...] = da.astype(BF16)
        ddt_ref[...] += jnp.sum(da, axis=0, keepdims=True)
        dal_ref[...] += jnp.sum(dg * neg_ea * _softplus(pre), axis=0, keepdims=True)

    blk = lambda cb: pl.BlockSpec((CHUNK, HEAD_DIM), lambda i: (i, cb))
    vec = pl.BlockSpec((1, HEAD_DIM), lambda i: (0, 0))
    io = pl.BlockSpec((CHUNK, HEAD_DIM), lambda i: (i, 0))
    return pl.pallas_call(
        body, name=name, grid=(s // CHUNK,),
        in_specs=[blk(C_B // HEAD_DIM), blk(C_A // HEAD_DIM), vec, vec, io, io], out_specs=[io, io, vec, vec],
        out_shape=[jax.ShapeDtypeStruct((s, HEAD_DIM), BF16)] * 2 + [jax.ShapeDtypeStruct((1, HEAD_DIM), F32)] * 2,
        compiler_params=_cp("arbitrary"),
    )(proj, proj, a_log, dt_bias, dbeta, d_g)


def _unit_lower_inverse(a_strict, eye):
    x = -a_strict
    t = x + eye
    p = x
    n = 2
    while n < CHUNK:
        p = _nn(p, p)
        t = t + _nn(t, p)
        n *= 2
    return t


def _dn_chunk_common(q, k, v, gc, beta, st):
    c = CHUNK
    eye = _tri(c, "eye")
    low = _tri(c, "lower")
    strict = low - eye
    grow = _col2row(gc, eye)
    dec = _hmap(lambda g_, gr: low * jnp.exp(low * (g_ - gr)), gc, grow)
    kb = k * beta
    a_mat = _nt(kb, k) * dec * strict
    t_inv = _unit_lower_inverse(a_mat, eye)
    e_g = _exp(gc)
    u = _nn(t_inv, v * beta)
    w = _nn(t_inv, kb * e_g)
    p_qk = _nt(q, k)
    qk = p_qk * dec
    qd = q * e_g
    last = (_iota2(c, 1, 0) == c - 1).astype(F32)
    g_last = _sum(gc * last, 0)
    e_t = _exp(g_last - gc)
    kt = k * e_t
    tail = _exp(g_last)
    vn = u - _nn(w, st)
    return dict(eye=eye, low=low, strict=strict, dec=dec, kb=kb, a_mat=a_mat, t_inv=t_inv, e_g=e_g, u=u, w=w,
                qk=qk, qd=qd, last=last, e_t=e_t, kt=kt, tail=tail, vn=vn)


def _dn_chunk_fwd_math(q, k, v, gc, beta, st):
    m = _dn_chunk_common(q, k, v, gc, beta, st)
    o = _nn(m["qd"], st) + _nn(m["qk"], m["vn"])
    st2 = st * m["tail"] + _tn(m["kt"], m["vn"])
    return o, st2


def _dn_chunk_bwd_math(q, k, v, gc, beta, st, do, dst2):
    m = _dn_chunk_common(q, k, v, gc, beta, st)
    eye, low, strict = m["eye"], m["low"], m["strict"]
    dvn = _tn(m["qk"], do) + _nn(m["kt"], dst2)
    dqk = _nt(do, m["vn"]) * low
    dqd = _nt(do, st)
    dst = _tn(m["qd"], do) + dst2 * m["tail"] - _tn(m["w"], dvn)
    dkt = _nt(m["vn"], dst2)
    dtail = _sum(_sum(st * dst2, 1), 0)
    dw = -_nt(dvn, st)
    dvb = _tn(m["t_inv"], dvn)
    dkg = _tn(m["t_inv"], dw)
    d_a = (_nt(dvb, m["u"]) + _nt(dkg, m["w"])) * (-strict)
    dkk = d_a * m["dec"]
    dp = dqk * m["dec"]
    dq = _nn(dp, k) + dqd * m["e_g"]
    dkb = _nn(dkk, k) + dkg * m["e_g"]
    dk = _tn(dp, q) + _tn(dkk, m["kb"]) + dkb * beta + dkt * m["e_t"]
    dv = dvb * beta
    dbeta = _sum(dvb * v + dkb * k, 1)
    de_g = _sum(dkg * m["kb"] + dqd * q, 1)
    de_t = _sum(dkt * k, 1)
    mm = d_a * m["a_mat"] + dqk * m["qk"]
    dgc = (_sum(mm, 1) - _row2col(_sum(mm, 0), eye) + de_g * m["e_g"] - de_t * m["e_t"]
           + (_sum(de_t * m["e_t"], 0) + dtail * m["tail"]) * m["last"])
    return dq, dk, dv, dgc, dbeta, dst


def _heads_of(ref):
    return _Heads(ref[:, h * HEAD_DIM:(h + 1) * HEAD_DIM] for h in range(N_HEADS))


def _lanes_of(block):
    return _Heads(_pick_lane(block, h) for h in range(N_HEADS))


def _dn_chunk_fwd(qkv, gcs, beta, *, name):
    s = qkv.shape[0]
    n = s // CHUNK

    def body(q_ref, k_ref, v_ref, g_ref, b_ref, o_ref, st_out_ref, st_ref):
        @pl.when(pl.program_id(0) == 0)
        def _():
            st_ref[...] = jnp.zeros_like(st_ref)

        gblk, bblk = g_ref[...], b_ref[...]
        st = _Heads(st_ref[h] for h in range(N_HEADS))
        o, st2 = _dn_chunk_fwd_math(_heads_of(q_ref), _heads_of(k_ref), _heads_of(v_ref), _lanes_of(gblk),
                                    _lanes_of(bblk), st)
        for h in range(N_HEADS):
            st_out_ref[0, h] = st.v[h]
            o_ref[:, h * HEAD_DIM:(h + 1) * HEAD_DIM] = o.v[h]
            st_ref[h] = st2.v[h]

    blk = lambda off: pl.BlockSpec((CHUNK, BR_WIDTH), lambda c: (c, off))
    sc = pl.BlockSpec((CHUNK, HEAD_DIM), lambda c: (c, 0))
    return pl.pallas_call(
        body, name=name, grid=(n,),
        in_specs=[blk(0), blk(1), blk(2), sc, sc],
        out_specs=[blk(0), pl.BlockSpec((1, N_HEADS, HEAD_DIM, HEAD_DIM), lambda c: (c, 0, 0, 0))],
        out_shape=[jax.ShapeDtypeStruct((s, BR_WIDTH), F32), jax.ShapeDtypeStruct((n, N_HEADS, HEAD_DIM, HEAD_DIM), F32)],
        scratch_shapes=[pltpu.VMEM((N_HEADS, HEAD_DIM, HEAD_DIM), F32)],
        compiler_params=_cp("arbitrary"),
    )(qkv, qkv, qkv, gcs, beta)


def _dn_chunk_bwd(qkv, gcs, beta, states, do, *, name):
    s = qkv.shape[0]
    n = s // CHUNK

    def body(q_ref, k_ref, v_ref, g_ref, b_ref, st_in_ref, do_ref, dqkv_ref, dg_ref, dbeta_ref, dst_ref):
        @pl.when(pl.program_id(0) == 0)
        def _():
            dst_ref[...] = jnp.zeros_like(dst_ref)

        gblk, bblk = g_ref[...], b_ref[...]
        lane = _iota2(CHUNK, HEAD_DIM, 1)
        dg_all = jnp.zeros((CHUNK, HEAD_DIM), F32)
        dbeta_all = jnp.zeros((CHUNK, HEAD_DIM), F32)
        dq, dk, dv, dgc, dbeta, dst = _dn_chunk_bwd_math(
            _heads_of(q_ref), _heads_of(k_ref), _heads_of(v_ref), _lanes_of(gblk), _lanes_of(bblk),
            _Heads(st_in_ref[0, h] for h in range(N_HEADS)), _heads_of(do_ref),
            _Heads(dst_ref[h] for h in range(N_HEADS)))
        for h in range(N_HEADS):
            for part, val in enumerate((dq, dk, dv)):
                c0 = part * BR_WIDTH + h * HEAD_DIM
                dqkv_ref[:, c0:c0 + HEAD_DIM] = val.v[h]
            dg_all = jnp.where(lane == h, dgc.v[h], dg_all)
            dbeta_all = jnp.where(lane == h, dbeta.v[h], dbeta_all)
            dst_ref[h] = dst.v[h]
        dg_ref[...] = dg_all
        dbeta_ref[...] = dbeta_all

    blk = lambda off: pl.BlockSpec((CHUNK, BR_WIDTH), lambda c: (n - 1 - c, off))
    sc = pl.BlockSpec((CHUNK, HEAD_DIM), lambda c: (n - 1 - c, 0))
    outs = pl.pallas_call(
        body, name=name, grid=(n,),
        in_specs=[blk(0), blk(1), blk(2), sc, sc,
                  pl.BlockSpec((1, N_HEADS, HEAD_DIM, HEAD_DIM), lambda c: (n - 1 - c, 0, 0, 0)), blk(0)],
        out_specs=[pl.BlockSpec((CHUNK, 3 * BR_WIDTH), lambda c: (n - 1 - c, 0)), sc, sc],
        out_shape=[jax.ShapeDtypeStruct((s, 3 * BR_WIDTH), F32)] + [jax.ShapeDtypeStruct((s, HEAD_DIM), F32)] * 2,
        scratch_shapes=[pltpu.VMEM((N_HEADS, HEAD_DIM, HEAD_DIM), F32)],
        compiler_params=_cp("arbitrary"),
    )(qkv, qkv, qkv, gcs, beta, states, do)
    return outs


def _hg_prep_fwd(proj, lb, *, name):
    s = proj.shape[0]
    tr = min(ROW_TILE, s)

    def body(q_ref, f_ref, lb_ref, qo_ref, ko_ref, lf_ref):
        f, lbv = f_ref[...], lb_ref[...]
        qo_ref[...] = _silu(q_ref[...])
        ko_ref[...] = (1.0 - lbv) * _sigmoid(-f)
        lf_ref[...] = jnp.log(lbv + (1.0 - lbv) * _sigmoid(f))

    blk = lambda cb: pl.BlockSpec((tr, BR_WIDTH), lambda i: (i, cb))
    out = pl.BlockSpec((tr, BR_WIDTH), lambda i: (i, 0))
    return pl.pallas_call(
        body, name=name, grid=(s // tr,),
        in_specs=[blk(C_HQ // BR_WIDTH), blk(C_HF // BR_WIDTH), pl.BlockSpec((1, BR_WIDTH), lambda i: (0, 0))],
        out_specs=[out, out, out], out_shape=[jax.ShapeDtypeStruct((s, BR_WIDTH), F32)] * 3, compiler_params=_cp("parallel"),
    )(proj, proj, lb)


def _hg_prep_bwd(proj, lb, dq, dk, dlf, *, name):
    s = proj.shape[0]
    tr = min(ROW_TILE, s)

    def body(q_ref, f_ref, lb_ref, dq_ref, dk_ref, dlf_ref, dhq_ref, dhf_ref, dlb_ref):
        @pl.when(pl.program_id(0) == 0)
        def _():
            dlb_ref[...] = jnp.zeros_like(dlb_ref)

        f, lbv = f_ref[...], lb_ref[...]
        dhq_ref[...] = (dq_ref[...] * _dsilu(q_ref[...])).astype(BF16)
        sp, sn = _sigmoid(f), _sigmoid(-f)
        inner = lbv + (1.0 - lbv) * sp
        dlf_over = dlf_ref[...] / inner
        dkv = dk_ref[...]
        dhf_ref[...] = (dlf_over * (1.0 - lbv) * sp * sn - dkv * (1.0 - lbv) * sn * (1.0 - sn)).astype(BF16)
        dlb_ref[...] += jnp.sum(dlf_over * (1.0 - sp) - dkv * sn, axis=0, keepdims=True)

    blk = lambda cb: pl.BlockSpec((tr, BR_WIDTH), lambda i: (i, cb))
    io = pl.BlockSpec((tr, BR_WIDTH), lambda i: (i, 0))
    vec = pl.BlockSpec((1, BR_WIDTH), lambda i: (0, 0))
    return pl.pallas_call(
        body, name=name, grid=(s // tr,),
        in_specs=[blk(C_HQ // BR_WIDTH), blk(C_HF // BR_WIDTH), vec, io, io, io], out_specs=[io, io, vec],
        out_shape=[jax.ShapeDtypeStruct((s, BR_WIDTH), BF16)] * 2 + [jax.ShapeDtypeStruct((1, BR_WIDTH), F32)],
        compiler_params=_cp("arbitrary"),
    )(proj, proj, lb, dq, dk, dlf)


def _hg_chunk_common(q, k, lf):
    c = CHUNK
    g = _nn_exact(_tri(c, "lower"), lf)
    e_g = _exp(g)
    qd = q * e_g
    g_last = g[c - 1:c, :]
    e_t = _exp(g_last - g)
    kt = k * e_t
    tail = _exp(g_last)
    q_sc, k_sc, e_q, e_k = [], [], [], []
    for i in range(c // SUB):
        g_ref = g[i * SUB:i * SUB + 1, :]
        eq = _exp(g[i * SUB:(i + 1) * SUB, :] - g_ref)
        ek = _hmap(lambda gr, g_: jnp.exp(jnp.minimum(gr - g_, EXP_CLAMP)), g_ref, g)
        e_q.append(eq)
        e_k.append(ek)
        q_sc.append(q[i * SUB:(i + 1) * SUB, :] * eq)
        k_sc.append(k * ek)
    a_mat = _stack_rows([_nt(qi, ki) for qi, ki in zip(q_sc, k_sc)]) * _tri(c, "lower")
    return dict(e_g=e_g, qd=qd, e_t=e_t, kt=kt, tail=tail, q_sc=q_sc, k_sc=k_sc, e_q=e_q, e_k=e_k, a_mat=a_mat)


def _hg_chunk_fwd_math(q, k, v, lf, stt):
    m = _hg_chunk_common(q, k, lf)
    o = _nt(m["qd"], stt) + _nn(m["a_mat"], v)
    stt2 = stt * m["tail"] + _tn(v, m["kt"])
    return o, stt2


def _hg_chunk_bwd_math(q, k, v, lf, stt, do, dstt2):
    c = CHUNK
    m = _hg_chunk_common(q, k, lf)
    stt2 = stt * m["tail"] + _tn(v, m["kt"])
    later = _sum(stt2 * dstt2, 0)
    dqd = _dot3(do, stt, 1, 0)
    dstt = _tn(do, m["qd"]) + dstt2 * m["tail"]
    d_a = _dot3(do, v, 1, 1) * _tri(c, "lower")
    dv = _tn(m["a_mat"], do) + _nt(m["kt"], dstt2)
    dkt = _dot3(v, dstt2, 1, 0)
    dq_parts = []
    dk = dkt * m["e_t"]
    for i in range(c // SUB):
        d_ai = d_a[i * SUB:(i + 1) * SUB, :]
        dq_parts.append(_dot3(d_ai, m["k_sc"][i], 1, 0) * m["e_q"][i])
        dk = dk + _dot3(d_ai, m["q_sc"][i], 0, 0) * m["e_k"][i]
    dq = dqd * m["e_g"] + _stack_rows(dq_parts)
    db = q * dq - k * dk
    dlf = _nn_exact(_tri(c, "upper"), db) + later
    return dq, dk, dv, dlf, dstt


def _hg_chunk_fwd(qh, kh, proj, lf, *, name):
    s = qh.shape[0]
    n = s // CHUNK
    vb = C_HI // BR_WIDTH

    def body(q_ref, k_ref, v_ref, lf_ref, o_ref, st_out_ref, st_ref):
        @pl.when(pl.program_id(0) == 0)
        def _():
            st_ref[...] = jnp.zeros_like(st_ref)

        st = _Heads(st_ref[h] for h in range(N_HEADS))
        o, st2 = _hg_chunk_fwd_math(_heads_of(q_ref), _heads_of(k_ref), _heads_of(v_ref), _heads_of(lf_ref), st)
        for h in range(N_HEADS):
            st_out_ref[0, h] = st.v[h]
            o_ref[:, h * HEAD_DIM:(h + 1) * HEAD_DIM] = o.v[h]
            st_ref[h] = st2.v[h]

    blk = lambda off: pl.BlockSpec((CHUNK, BR_WIDTH), lambda c: (c, off))
    return pl.pallas_call(
        body, name=name, grid=(n,), in_specs=[blk(0), blk(0), blk(vb), blk(0)],
        out_specs=[blk(0), pl.BlockSpec((1, N_HEADS, HEAD_DIM, HEAD_DIM), lambda c: (c, 0, 0, 0))],
        out_shape=[jax.ShapeDtypeStruct((s, BR_WIDTH), F32), jax.ShapeDtypeStruct((n, N_HEADS, HEAD_DIM, HEAD_DIM), F32)],
        scratch_shapes=[pltpu.VMEM((N_HEADS, HEAD_DIM, HEAD_DIM), F32)],
        compiler_params=_cp("arbitrary"),
    )(qh, kh, proj, lf)


def _hg_chunk_bwd(qh, kh, proj, lf, states, do, *, name):
    s = qh.shape[0]
    n = s // CHUNK
    vb = C_HI // BR_WIDTH

    def body(q_ref, k_ref, v_ref, lf_ref, st_in_ref, do_ref, dq_ref, dk_ref, dv_ref, dlf_ref, dst_ref):
        @pl.when(pl.program_id(0) == 0)
        def _():
            dst_ref[...] = jnp.zeros_like(dst_ref)

        dq, dk, dv, dlf, dst = _hg_chunk_bwd_math(
            _heads_of(q_ref), _heads_of(k_ref), _heads_of(v_ref), _heads_of(lf_ref),
            _Heads(st_in_ref[0, h] for h in range(N_HEADS)), _heads_of(do_ref),
            _Heads(dst_ref[h] for h in range(N_HEADS)))
        for h in range(N_HEADS):
            cols = slice(h * HEAD_DIM, (h + 1) * HEAD_DIM)
            dq_ref[:, cols] = dq.v[h]
            dk_ref[:, cols] = dk.v[h]
            dv_ref[:, cols] = dv.v[h].astype(BF16)
            dlf_ref[:, cols] = dlf.v[h]
            dst_ref[h] = dst.v[h]

    blk = lambda off: pl.BlockSpec((CHUNK, BR_WIDTH), lambda c: (n - 1 - c, off))
    return pl.pallas_call(
        body, name=name, grid=(n,),
        in_specs=[blk(0), blk(0), blk(vb), blk(0),
                  pl.BlockSpec((1, N_HEADS, HEAD_DIM, HEAD_DIM), lambda c: (n - 1 - c, 0, 0, 0)), blk(0)],
        out_specs=[blk(0), blk(0), blk(0), blk(0)],
        out_shape=[jax.ShapeDtypeStruct((s, BR_WIDTH), F32)] * 2 + [jax.ShapeDtypeStruct((s, BR_WIDTH), BF16),
                                                                    jax.ShapeDtypeStruct((s, BR_WIDTH), F32)],
        scratch_shapes=[pltpu.VMEM((N_HEADS, HEAD_DIM, HEAD_DIM), F32)],
        compiler_params=_cp("arbitrary"),
    )(qh, kh, proj, lf, states, do)


_ANY = pl.BlockSpec(memory_space=pl.ANY)
_MESH = pl.DeviceIdType.MESH


def _all_gather(x_local, *, name, after=()):
    n_after = len(after)

    def body(x_ref, *refs):
        out_ref, send_sems, recv_sems, local_sem = refs[n_after:]
        x, y, c = lax.axis_index("x"), lax.axis_index("y"), lax.axis_index("c")
        me, sibling = (x, y, c), (x, y, 1 - c)
        chips = [(1 - x, y), (x, 1 - y), (1 - x, 1 - y)]

        def slot(px, py, pc):
            return out_ref.at[4 * px + 2 * py + pc]

        def copy(k, block, to, src=None):
            return pltpu.make_async_remote_copy(
                src_ref=slot(*block) if src is None else src, dst_ref=slot(*block),
                send_sem=send_sems.at[k], recv_sem=recv_sems.at[k], device_id=to, device_id_type=_MESH)

        mine = pltpu.make_async_copy(x_ref, slot(*me), local_sem)
        mine.start()
        first = [copy(0, me, sibling, src=x_ref)]
        first += [copy(1 + j, me, (*chip, c), src=x_ref) for j, chip in enumerate(chips)]
        for cp in first:
            cp.start()
        passed = [copy(4 + j, (*chip, c), sibling) for j, chip in enumerate(chips)]
        for j, chip in enumerate(chips):
            copy(1 + j, (*chip, c), me).wait_recv()
            passed[j].start()
        copy(0, sibling, me).wait_recv()
        for j, chip in enumerate(chips):
            copy(4 + j, (*chip, 1 - c), me).wait_recv()
        for cp in first + passed:
            cp.wait_send()
        mine.wait()

    return pl.pallas_call(
        body, name=name, out_shape=jax.ShapeDtypeStruct((N_DEV,) + x_local.shape, x_local.dtype),
        in_specs=[_ANY] * (1 + n_after), out_specs=_ANY,
        scratch_shapes=[pltpu.SemaphoreType.DMA((7,)), pltpu.SemaphoreType.DMA((7,)), pltpu.SemaphoreType.DMA],
    )(x_local, *after)


_HBM = pl.BlockSpec(memory_space=pltpu.HBM)
_SEM = pl.BlockSpec(memory_space=pltpu.SEMAPHORE)
_EFFECT = pltpu.SideEffectType.DATAFLOW_SIDE_EFFECTING


def _peers():
    x, y, c = lax.axis_index("x"), lax.axis_index("y"), lax.axis_index("c")
    out = []
    for k in range(1, N_DEV):
        px, py, pc = x ^ ((k >> 2) & 1), y ^ ((k >> 1) & 1), c ^ (k & 1)
        out.append(((px, py, pc), 4 * px + 2 * py + pc))
    return 4 * x + 2 * y + c, out


def _push_copies(src_ref, land_ref, send_sems, recv_sems, broadcast):
    my, peers = _peers()
    pairs = []
    for k, (pos, idx) in enumerate(peers):
        src = src_ref if broadcast else src_ref.at[idx]
        send = pltpu.make_async_remote_copy(src_ref=src, dst_ref=land_ref.at[my], send_sem=send_sems.at[k],
                                            recv_sem=recv_sems.at[k], device_id=pos, device_id_type=_MESH)
        recv = pltpu.make_async_remote_copy(src_ref=src, dst_ref=land_ref.at[idx], send_sem=send_sems.at[k],
                                            recv_sem=recv_sems.at[k], device_id=pos, device_id_type=_MESH)
        pairs.append((send, recv))
    return pairs


def _push_start(src, land, *, broadcast, name, after=()):
    n_after = len(after)

    def body(src_ref, land_ref, *refs):
        send_sems, recv_sems, _, _, token = refs[n_after:]
        for send, _ in _push_copies(src_ref, land_ref, send_sems, recv_sems, broadcast):
            send.start()
        token[...] = jnp.zeros_like(token)

    return pl.pallas_call(
        body, name=name,
        out_shape=(pltpu.SemaphoreType.DMA((N_DEV - 1,)), pltpu.SemaphoreType.DMA((N_DEV - 1,)),
                   pltpu.HBM(src.shape, src.dtype), pltpu.HBM(land.shape, land.dtype), jax.ShapeDtypeStruct((8, 128), F32)),
        in_specs=(_HBM, _HBM) + (_ANY,) * n_after, out_specs=(_SEM, _SEM, _HBM, _HBM, pl.BlockSpec(memory_space=pltpu.VMEM)),
        input_output_aliases={0: 2, 1: 3}, compiler_params=pltpu.CompilerParams(has_side_effects=_EFFECT),
    )(pltpu.with_memory_space_constraint(src, pltpu.HBM), pltpu.with_memory_space_constraint(land, pltpu.HBM), *after)


def _push_wait(handle, after, *, broadcast, name):
    send_sems, recv_sems, src_thru, land_thru, _ = handle

    def body(src_ref, land_ref, send_sems, recv_sems, *rest):
        for send, recv in _push_copies(src_ref, land_ref, send_sems, recv_sems, broadcast):
            send.wait_send()
            recv.wait_recv()

    return pl.pallas_call(
        body, name=name,
        out_shape=(pltpu.HBM(src_thru.shape, src_thru.dtype), pltpu.HBM(land_thru.shape, land_thru.dtype)),
        in_specs=(_HBM, _HBM, _SEM, _SEM) + (_ANY,) * len(after), out_specs=(_HBM, _HBM),
        input_output_aliases={0: 0, 1: 1}, compiler_params=pltpu.CompilerParams(has_side_effects=_EFFECT),
    )(src_thru, land_thru, send_sems, recv_sems, *after)[1]


def _adamw(parts, row_off, w, m, v, *, layer=0, n_layers=1, prev=None, name, tr):
    rows, c = w.shape
    r = rows // n_layers
    np_ = parts.shape[0]
    tr = min(tr, r)
    assert r % tr == 0 and row_off % tr == 0
    ob, lb = row_off // tr, layer * (r // tr)
    c1 = 1.0 - ADAM_B1 ** ADAM_STEP
    c2 = 1.0 - ADAM_B2 ** ADAM_STEP
    n_prev = 0 if prev is None else 4

    def body(p_ref, w_ref, m_ref, v_ref, *refs):
        g_ref, d_ref, nm_ref, nv_ref = refs[n_prev:]
        g = p_ref[0].astype(F32)
        for s in range(1, np_):
            g = g + p_ref[s].astype(F32)
        wv = w_ref[...]
        m2 = ADAM_B1 * m_ref[...] + (1.0 - ADAM_B1) * g
        v2 = ADAM_B2 * v_ref[...] + (1.0 - ADAM_B2) * jnp.square(g)
        m_hat = m2 / c1
        v_hat = v2 / c2
        g_ref[...] = g
        d_ref[...] = -ADAM_LR * (m_hat / (jnp.sqrt(v_hat) + ADAM_EPS) + ADAM_WD * wv)
        nm_ref[...] = m2
        nv_ref[...] = v2

    blk = pl.BlockSpec((tr, c), lambda i: (lb + i, 0))
    return pl.pallas_call(
        body, name=name, grid=(r // tr,),
        in_specs=[pl.BlockSpec((np_, tr, c), lambda i: (0, ob + i, 0)), blk, blk, blk] + [_ANY] * n_prev,
        out_specs=[blk] * 4, out_shape=[jax.ShapeDtypeStruct((rows, c), F32)] * 4,
        input_output_aliases={4 + i: i for i in range(n_prev)}, compiler_params=_cp("parallel"),
    )(parts, w, m, v, *(prev or ()))


def _sum_parts(parts, *, name, after=()):
    np_, r, c = parts.shape

    def body(p_ref, *refs):
        o_ref = refs[-1]
        g = p_ref[0]
        for s in range(1, np_):
            g = g + p_ref[s]
        o_ref[...] = g

    vmem = pl.BlockSpec(memory_space=pltpu.VMEM)
    return pl.pallas_call(body, name=name, in_specs=[vmem] + [_ANY] * len(after), out_specs=vmem,
                          out_shape=jax.ShapeDtypeStruct((r, c), F32))(parts, *after)


def _pack(arrs):
    rows = []
    for a in arrs:
        f = a.reshape(-1).astype(F32)
        pad = (-f.shape[0]) % 128
        rows.append(jnp.pad(f, (0, pad)).reshape(-1, 128))
    out = jnp.concatenate(rows, axis=0)
    return jnp.pad(out, ((0, (-out.shape[0]) % 8), (0, 0)))


def _unpack(packed, shapes):
    outs, r0 = [], 0
    for shp in shapes:
        n = 1
        for d in shp:
            n *= d
        nr = -(-n // 128)
        outs.append(packed[r0:r0 + nr].reshape(-1)[:n].reshape(shp))
        r0 += nr
    return outs


_WIN_PIECES = ((0, 4096, 0), (4112, 8208, 0), (4096, 4104, HEAD_DIM - N_HEADS), (4104, 4112, HEAD_DIM - N_HEADS))


def _win_from_shards(shards):
    cols = []
    for lo, hi, pad in _WIN_PIECES:
        for j in range(N_DEV):
            a, b = max(lo, j * SHARD_IN), min(hi, (j + 1) * SHARD_IN)
            if a < b:
                cols.append(shards[j][:, a - j * SHARD_IN:b - j * SHARD_IN])
        if pad:
            cols.append(jnp.zeros((shards[0].shape[0], pad), shards[0].dtype))
    return jnp.concatenate(cols, axis=1)


def _win_to_shards(g):
    starts, off = [], 0
    for lo, hi, pad in _WIN_PIECES:
        starts.append((lo, hi, off))
        off += hi - lo + pad
    shards = []
    for j in range(N_DEV):
        cols = []
        for lo, hi, off in sorted(starts):
            a, b = max(lo, j * SHARD_IN), min(hi, (j + 1) * SHARD_IN)
            if a < b:
                cols.append(g[:, off + a - lo:off + b - lo])
        shards.append(jnp.concatenate(cols, axis=1))
    return shards


def _lower_bounds(logits):
    probs = jax.nn.softmax(logits.astype(F32), axis=0)
    return jnp.cumsum(probs, axis=0) - probs[0]


def _pad_lanes(vec8):
    return jnp.pad(vec8.reshape(1, N_HEADS), ((0, 0), (0, HEAD_DIM - N_HEADS)))


def kernel(x, p, norm_w, w_in, dn_conv_w, dn_A_log, dn_dt_bias, dn_norm_w, hg_lb_logits, hg_norm_w, w_out, w_ple_up, w_ple_gate, final_norm_w, loss_target, m_norm_w, m_w_in, m_dn_conv_w, m_dn_A_log, m_dn_dt_bias, m_dn_norm_w, m_hg_lb_logits, m_hg_norm_w, m_w_out, m_w_ple_up, m_w_ple_gate, m_final_norm_w, v_norm_w, v_w_in, v_dn_conv_w, v_dn_A_log, v_dn_dt_bias, v_dn_norm_w, v_hg_lb_logits, v_hg_norm_w, v_w_out, v_w_ple_up, v_w_ple_gate, v_final_norm_w):
    depth = norm_w.shape[0]
    my = 4 * lax.axis_index("x") + 2 * lax.axis_index("y") + lax.axis_index("c")
    h = x[0]
    tgt = loss_target[0]
    rows_out = D_MODEL // N_DEV
    up_rows = PLE_DIM * (D_MODEL // N_DEV) // D_MODEL
    g_off, u_off = rows_out, 2 * rows_out

    def own_slot(block):
        return lax.dynamic_update_index_in_dim(lax.empty((N_DEV,) + block.shape, block.dtype), block, my, 0)

    win_bf = w_in.astype(BF16)
    rest_bf = [jnp.concatenate([w_out[l], w_ple_gate[l], w_ple_up[l].reshape(up_rows, D_MODEL)], axis=0).astype(BF16)
               for l in range(depth)]
    conv_all = _all_gather(dn_conv_w, name="gather_conv_w")
    conv_full = conv_all.transpose(1, 2, 0, 3).reshape(depth, CONV_W, 3 * BR_WIDTH)
    win_all = {0: _all_gather(win_bf[0], name="gather_w_in_l0", after=[conv_all])}
    pending = {}
    last = win_all[0]
    for l in range(depth):
        if l > 0:
            pending["win", l] = _push_start(win_bf[l], own_slot(win_bf[l]), broadcast=True, after=[last],
                                            name=f"gather_w_in_l{l}_start")
            last = pending["win", l][4]
        pending["rest", l] = _push_start(rest_bf[l], own_slot(rest_bf[l]), broadcast=True, after=[last],
                                         name=f"gather_rest_l{l}_start")
        last = pending["rest", l][4]
    order_tok = last[0, 0]
    lbs = _lower_bounds(hg_lb_logits)

    saved = []
    weights = []
    for l in range(depth):
        tag = f"l{l}"
        if l > 0:
            win_all[l] = _push_wait(pending["win", l], [h], broadcast=True, name=f"gather_w_in_{tag}_wait")
        wi = _win_from_shards([win_all[l][j] for j in range(N_DEV)])
        nw = norm_w[l] + order_tok if l == 0 else norm_w[l]
        hn = _rms_fwd(h, nw, name=f"rms_fwd_{tag}")
        proj = _mm(hn, wi, mode="nn", out_dtype=F32, name=f"mm_proj_{tag}")
        al, dt = _pad_lanes(dn_A_log[l]), _pad_lanes(dn_dt_bias[l])
        qkv = _dn_qkv_fwd(proj, conv_full[l], name=f"dn_qkv_fwd_{tag}")
        beta, gcs = _dn_gate_fwd(proj, al, dt, name=f"dn_gate_fwd_{tag}")
        o_dn, st_dn = _dn_chunk_fwd(qkv, gcs, beta, name=f"dn_chunk_fwd_{tag}")
        lb = lbs[l].reshape(1, BR_WIDTH)
        qh, kh, lf = _hg_prep_fwd(proj, lb, name=f"hg_prep_fwd_{tag}")
        o_hg, st_hg = _hg_chunk_fwd(qh, kh, proj, lf, name=f"hg_chunk_fwd_{tag}")
        y_dn = _hnorm_fwd(o_dn, proj, C_Z, dn_norm_w[l], name=f"hnorm_dn_fwd_{tag}")
        y_hg = _hnorm_fwd(o_hg, proj, C_HZ, hg_norm_w[l], name=f"hnorm_hg_fwd_{tag}")
        y = jnp.concatenate([y_dn, y_hg], axis=1)
        rest_all = _push_wait(pending["rest", l], [y], broadcast=True, name=f"gather_rest_{tag}_wait")
        wo = rest_all[:, 0:rows_out].reshape(D_MODEL, D_MODEL)
        wg = rest_all[:, g_off:g_off + rows_out].reshape(D_MODEL, D_MODEL)
        wu = rest_all[:, u_off:u_off + up_rows].reshape(N_DEV, PLE_DIM, D_MODEL // N_DEV).transpose(1, 0, 2).reshape(PLE_DIM, D_MODEL)
        weights.append((wi, wo, wg, wu))
        h1 = _mm(y, wo, mode="nn", out_dtype=F32, res=h, name=f"mm_out_{tag}")
        gp = _mm(h1, wg, mode="nn", out_dtype=F32, name=f"mm_gate_{tag}")
        up = _mm(p[l, 0], wu, mode="nn", out_dtype=F32, name=f"mm_up_{tag}")
        h2 = _ple_fwd(h1, gp, up, name=f"ple_fwd_{tag}")
        saved.append(dict(h=h, hn=hn, proj=proj, qkv=qkv, beta=beta, gcs=gcs, st_dn=st_dn, qh=qh, kh=kh, lf=lf,
                          st_hg=st_hg, o_dn=o_dn, o_hg=o_hg, y=y, h1=h1, gp=gp, up=up, al=al, dt=dt, lb=lb))
        h = h2

    loss_row, dh, d_final_w = _final_fwd_bwd(h, final_norm_w, tgt, name="final_norm_loss")

    d_norm_w, d_alog, d_dt, d_dn_nw, d_hg_nw, d_lb, d_conv = ([None] * depth for _ in range(7))
    sent = {}
    for l in reversed(range(depth)):
        wi, wo, wg, wu = weights[l]
        sv = saved[l]
        tag = f"l{l}"
        dup, dgp = _ple_bwd(dh, sv["gp"], sv["up"], name=f"ple_bwd_{tag}")
        pin = [sent["win", l + 1][4]] if l + 1 < depth else []
        d_wu = _mm(p[l, 0], dup, mode="tn", out_dtype=BF16, after=pin, name=f"mm_dwup_{tag}")
        d_wg = _mm(sv["h1"], dgp, mode="tn", out_dtype=BF16, name=f"mm_dwgate_{tag}")
        dh1 = _mm(dgp, wg, mode="nt", out_dtype=F32, res=dh, name=f"mm_dh1_{tag}")
        d_wo = _mm(sv["y"], dh1, mode="tn", out_dtype=BF16, name=f"mm_dwout_{tag}")
        parts_rest = jnp.concatenate(
            [d_wo.reshape(N_DEV, rows_out, D_MODEL), d_wg.reshape(N_DEV, rows_out, D_MODEL),
             d_wu.reshape(PLE_DIM, N_DEV, D_MODEL // N_DEV).transpose(1, 0, 2).reshape(N_DEV, up_rows, D_MODEL)], axis=1)
        sent["rest", l] = _push_start(parts_rest, own_slot(parts_rest[my]), broadcast=False, name=f"exchange_rest_{tag}_start")
        dy = _mm(dh1, wo, mode="nt", out_dtype=F32, name=f"mm_dy_{tag}")
        dn_nw = dn_norm_w[l] + sent["rest", l][4][0, 0]
        do_dn, dz_dn, d_dn_nw[l] = _hnorm_bwd(sv["o_dn"], sv["proj"], C_Z, dn_nw, dy, 0, name=f"hnorm_dn_bwd_{tag}")
        do_hg, dz_hg, d_hg_nw[l] = _hnorm_bwd(sv["o_hg"], sv["proj"], C_HZ, hg_norm_w[l], dy, BR_WIDTH, name=f"hnorm_hg_bwd_{tag}")
        dqkv, d_gc, dbeta = _dn_chunk_bwd(sv["qkv"], sv["gcs"], sv["beta"], sv["st_dn"], do_dn, name=f"dn_chunk_bwd_{tag}")
        dqkv_pre, d_conv[l] = _dn_qkv_bwd(sv["proj"], conv_full[l], dqkv, name=f"dn_qkv_bwd_{tag}")
        db, da, d_alog[l], d_dt[l] = _dn_gate_bwd(sv["proj"], sv["al"], sv["dt"], dbeta, d_gc, name=f"dn_gate_bwd_{tag}")
        dqh, dkh, dhi, dlf = _hg_chunk_bwd(sv["qh"], sv["kh"], sv["proj"], sv["lf"], sv["st_hg"], do_hg, name=f"hg_chunk_bwd_{tag}")
        dhq, dhf, d_lb[l] = _hg_prep_bwd(sv["proj"], sv["lb"], dqh, dkh, dlf, name=f"hg_prep_bwd_{tag}")
        dproj = jnp.concatenate([dqkv_pre, dz_dn, dhq, dhf, dhi, dz_hg, db, da], axis=1)
        dhn = _mm(dproj, wi, mode="nt", out_dtype=F32, name=f"mm_dhn_{tag}")
        dh, d_norm_w[l] = _rms_bwd(sv["h"], norm_w[l], dhn, dh1, name=f"rms_bwd_{tag}")
        before_dwin = [dh]
        if l == 0:
            small = _pack([loss_row, jnp.concatenate(d_norm_w, axis=0), d_final_w,
                           jnp.stack([a[0, :N_HEADS] for a in d_alog]), jnp.stack([a[0, :N_HEADS] for a in d_dt]),
                           jnp.concatenate(d_dn_nw, axis=0), jnp.concatenate(d_hg_nw, axis=0), jnp.concatenate(d_lb, axis=0),
                           jnp.stack(d_conv)])
            small_all = _all_gather(small, name="gather_small")
            before_dwin = [small_all]
        d_win = _mm(sv["hn"], dproj, mode="tn", out_dtype=BF16, after=before_dwin, name=f"mm_dwin_{tag}")
        parts_in = jnp.stack(_win_to_shards(d_win))
        sent["win", l] = _push_start(parts_in, own_slot(parts_in[my]), broadcast=False, name=f"exchange_w_in_{tag}_start")
    grad_x = dh[None]

    small_shapes = [(1, 128), norm_w.shape, final_norm_w.shape, dn_A_log.shape, dn_dt_bias.shape, dn_norm_w.shape,
                    hg_norm_w.shape, hg_lb_logits.shape, (depth, CONV_W, 3 * BR_WIDTH)]
    tot = _unpack(_sum_parts(small_all, after=[sent["win", 0][4]], name="sum_small"), small_shapes)
    loss = tot[0][0, 0]
    g_lb = tot[7]
    g_logits = jax.vjp(_lower_bounds, hg_lb_logits)[1](g_lb)[0]
    g_conv = lax.dynamic_slice_in_dim(tot[8], my * (3 * BR_WIDTH // N_DEV), 3 * BR_WIDTH // N_DEV, axis=2)
    small_g = [tot[1], g_conv, tot[3], tot[4], tot[5], g_logits, tot[6], tot[2]]
    small_w = [norm_w, dn_conv_w, dn_A_log, dn_dt_bias, dn_norm_w, hg_lb_logits, hg_norm_w, final_norm_w]
    small_m = [m_norm_w, m_dn_conv_w, m_dn_A_log, m_dn_dt_bias, m_dn_norm_w, m_hg_lb_logits, m_hg_norm_w, m_final_norm_w]
    small_v = [v_norm_w, v_dn_conv_w, v_dn_A_log, v_dn_dt_bias, v_dn_norm_w, v_hg_lb_logits, v_hg_norm_w, v_final_norm_w]
    pk_w = _pack(small_w)
    res_small = _adamw(_pack(small_g)[None], 0, pk_w, _pack(small_m), _pack(small_v), name="adamw_small", tr=pk_w.shape[0])
    shapes_w = [a.shape for a in small_w]
    sg, sd, sm, sv_ = (_unpack(r, shapes_w) for r in res_small)

    r_win = r_wo = r_wg = r_wu = None
    done = [grad_x, res_small[0]]

    def flat(a, cols):
        return a.reshape(-1, cols)

    for l in reversed(range(depth)):
        tag = f"l{l}"
        land_rest = _push_wait(sent["rest", l], done, broadcast=False, name=f"exchange_rest_{tag}_wait")
        r_wo = _adamw(land_rest, 0, flat(w_out, D_MODEL), flat(m_w_out, D_MODEL), flat(v_w_out, D_MODEL), layer=l,
                      n_layers=depth, prev=r_wo, name=f"adamw_w_out_{tag}", tr=rows_out)
        r_wg = _adamw(land_rest, g_off, flat(w_ple_gate, D_MODEL), flat(m_w_ple_gate, D_MODEL), flat(v_w_ple_gate, D_MODEL),
                      layer=l, n_layers=depth, prev=r_wg, name=f"adamw_w_gate_{tag}", tr=rows_out)
        r_wu = _adamw(land_rest, u_off, flat(w_ple_up, D_MODEL), flat(m_w_ple_up, D_MODEL), flat(v_w_ple_up, D_MODEL),
                      layer=l, n_layers=depth, prev=r_wu, name=f"adamw_w_up_{tag}", tr=up_rows)
        done = [r_wo[0], r_wg[0], r_wu[0]]
    for l in reversed(range(depth)):
        tag = f"l{l}"
        land_in = _push_wait(sent["win", l], done, broadcast=False, name=f"exchange_w_in_{tag}_wait")
        r_win = _adamw(land_in, 0, flat(w_in, SHARD_IN), flat(m_w_in, SHARD_IN), flat(v_w_in, SHARD_IN), layer=l,
                       n_layers=depth, prev=r_win, name=f"adamw_w_in_{tag}", tr=256)
        done = [r_win[0]]
    r_win = [o.reshape(w_in.shape) for o in r_win]
    r_wo = [o.reshape(w_out.shape) for o in r_wo]
    r_wg = [o.reshape(w_ple_gate.shape) for o in r_wg]
    r_wu = [o.reshape(w_ple_up.shape) for o in r_wu]

    def order(small_list, big_in, big_out, big_up, big_gate):
        nw, cw, al_, dt_, dnw, lbl, hnw, fw = small_list
        return [nw, big_in, cw, al_, dt_, dnw, lbl, hnw, big_out, big_up, big_gate, fw]

    outs = [loss, grad_x]
    for i, sl in enumerate((sg, sd, sm, sv_)):
        outs += order(sl, r_win[i], r_wo[i], r_wu[i], r_wg[i])
    return tuple(outs)
```

```python
import functools

import jax
import jax.numpy as jnp
from jax import lax
from jax.experimental import pallas as pl
from jax.experimental.pallas import tpu as pltpu

F32 = jnp.float32
BF16 = jnp.bfloat16
HIGHEST = lax.Precision.HIGHEST

N_DEV = 8
D_MODEL = 2048
PLE_DIM = 256
HEAD_DIM = 128
N_HEADS = 8
BR_WIDTH = N_HEADS * HEAD_DIM
CHUNK = 64
SUB = 16
CONV_W = 4
NORM_EPS = 1e-6
L2_EPS = 1e-6
IN_WIDTH = 8208
SHARD_IN = IN_WIDTH // N_DEV
EXP_CLAMP = 80.0

C_QKV, C_Z, C_HQ, C_HF, C_HI, C_HZ, C_B, C_A, N_PROJ = 0, 3072, 4096, 5120, 6144, 7168, 8192, 8320, 8448

ADAM_LR, ADAM_B1, ADAM_B2, ADAM_EPS, ADAM_WD, ADAM_STEP = 0.001, 0.9, 0.999, 1e-08, 0.01, 10

VMEM_LIMIT = 48 * 1024 * 1024


def _cp(*sem):
    return pltpu.CompilerParams(dimension_semantics=sem, vmem_limit_bytes=VMEM_LIMIT)


class _Heads:
    def __init__(self, vals):
        self.v = tuple(vals)

    def __add__(self, o):
        return _hmap(lambda a, b: a + b, self, o)

    def __radd__(self, o):
        return _hmap(lambda a, b: b + a, self, o)

    def __sub__(self, o):
        return _hmap(lambda a, b: a - b, self, o)

    def __rsub__(self, o):
        return _hmap(lambda a, b: b - a, self, o)

    def __mul__(self, o):
        return _hmap(lambda a, b: a * b, self, o)

    def __rmul__(self, o):
        return _hmap(lambda a, b: b * a, self, o)

    def __neg__(self):
        return _hmap(lambda a: -a, self)

    def __getitem__(self, idx):
        return _hmap(lambda a: a[idx], self)


def _hmap(fn, *args):
    n = next((len(a.v) for a in args if isinstance(a, _Heads)), None)
    if n is None:
        return fn(*args)
    return _Heads(fn(*[a.v[i] if isinstance(a, _Heads) else a for a in args]) for i in range(n))


def _dot(a, b, ca, cb):
    return _hmap(lambda x, y: lax.dot_general(x.astype(BF16), y.astype(BF16), (((ca,), (cb,)), ((), ())),
                                              preferred_element_type=F32), a, b)


def _nn(a, b):
    return _dot(a, b, 1, 0)


def _nt(a, b):
    return _dot(a, b, 1, 1)


def _tn(a, b):
    return _dot(a, b, 0, 0)


def _split(a):
    hi = _hmap(lambda x: x.astype(BF16), a)
    return hi, _hmap(lambda x, h: (x - h.astype(F32)).astype(BF16), a, hi)


def _dot3(a, b, ca, cb):
    ah, al = _split(a)
    bh, bl = _split(b)
    return _dot(ah, bh, ca, cb) + (_dot(ah, bl, ca, cb) + _dot(al, bh, ca, cb))


def _nn_exact(a, b):
    return _hmap(lambda y: lax.dot_general(a, y, (((1,), (0,)), ((), ())), precision=HIGHEST,
                                           preferred_element_type=F32), b)


def _exp(x):
    return _hmap(jnp.exp, x)


def _sum(x, axis):
    return _hmap(lambda a: jnp.sum(a, axis=axis, keepdims=True), x)


def _stack_rows(parts):
    return _hmap(lambda *xs: jnp.concatenate(xs, axis=0), *parts)


def _sigmoid(x):
    return jax.nn.sigmoid(x)


def _silu(x):
    return x * _sigmoid(x)


def _dsilu(x):
    s = _sigmoid(x)
    return s * (1.0 + x * (1.0 - s))


def _softplus(x):
    return jnp.maximum(x, 0.0) + jnp.log(1.0 + jnp.exp(-jnp.abs(x)))


def _iota2(n, m, axis):
    return lax.broadcasted_iota(jnp.int32, (n, m), axis)


def _col2row(col, eye):
    return _hmap(lambda c: jnp.sum(eye * c, axis=0, keepdims=True), col)


def _row2col(row, eye):
    return _hmap(lambda r: jnp.sum(eye * r, axis=1, keepdims=True), row)


def _pick_lane(block, lane_idx):
    lane = _iota2(block.shape[0], block.shape[1], 1)
    return jnp.sum(jnp.where(lane == lane_idx, block, 0.0), axis=1, keepdims=True)


MM_TILE_M, MM_TILE_N, MM_TILE_K = 1024, 1408, 2048


def _tile(dim, cap):
    if dim <= cap:
        return dim
    t = cap - cap % 128
    while dim % t:
        t -= 128
    return t


def _mm(a, b, *, mode, out_dtype, res=None, after=(), name):
    if mode == "nn":
        (m, kd), (_, n) = a.shape, b.shape
    elif mode == "nt":
        (m, kd), (n, _) = a.shape, b.shape
    else:
        (kd, m), (_, n) = a.shape, b.shape
    tm, tn, tk = _tile(m, MM_TILE_M), _tile(n, MM_TILE_N), _tile(kd, MM_TILE_K)
    assert m % tm == 0 and n % tn == 0 and kd % tk == 0, (m, n, kd, tm, tn, tk)
    nk = kd // tk
    ca, cb = {"nn": (1, 0), "nt": (1, 1), "tn": (0, 0)}[mode]

    def body(*refs):
        a_ref, b_ref = refs[:2]
        r_ref = None if res is None else refs[2]
        o_ref, acc_ref = refs[-2:]
        k = pl.program_id(2)

        @pl.when(k == 0)
        def _():
            acc_ref[...] = jnp.zeros_like(acc_ref)

        acc_ref[...] += _dot(a_ref[...], b_ref[...], ca, cb)

        @pl.when(k == nk - 1)
        def _():
            out = acc_ref[...]
            if r_ref is not None:
                out = out + r_ref[...].astype(F32)
            o_ref[...] = out.astype(o_ref.dtype)

    a_spec = pl.BlockSpec((tk, tm), lambda i, j, k: (k, i)) if mode == "tn" else pl.BlockSpec((tm, tk), lambda i, j, k: (i, k))
    b_spec = pl.BlockSpec((tn, tk), lambda i, j, k: (j, k)) if mode == "nt" else pl.BlockSpec((tk, tn), lambda i, j, k: (k, j))
    o_spec = pl.BlockSpec((tm, tn), lambda i, j, k: (i, j))
    in_specs = [a_spec, b_spec] + ([o_spec] if res is not None else []) + [pl.BlockSpec(memory_space=pl.ANY)] * len(after)
    args = (a, b) + ((res,) if res is not None else ()) + tuple(after)
    return pl.pallas_call(
        body, name=name, grid=(m // tm, n // tn, nk), in_specs=in_specs, out_specs=o_spec,
        out_shape=jax.ShapeDtypeStruct((m, n), out_dtype),
        scratch_shapes=[pltpu.VMEM((tm, tn), F32)],
        compiler_params=_cp("parallel", "parallel", "arbitrary"),
    )(*args)


ROW_TILE = 256


def _rms_fwd(h, w, *, name):
    s, d = h.shape
    tr = min(ROW_TILE, s)

    def body(h_ref, w_ref, o_ref):
        x = h_ref[...]
        r = lax.rsqrt(jnp.mean(x * x, axis=-1, keepdims=True) + NORM_EPS)
        o_ref[...] = (x * r * w_ref[...]).astype(o_ref.dtype)

    return pl.pallas_call(
        body, name=name, grid=(s // tr,),
        in_specs=[pl.BlockSpec((tr, d), lambda i: (i, 0)), pl.BlockSpec((1, d), lambda i: (0, 0))],
        out_specs=pl.BlockSpec((tr, d), lambda i: (i, 0)),
        out_shape=jax.ShapeDtypeStruct((s, d), BF16), compiler_params=_cp("parallel"),
    )(h, w.reshape(1, d))


def _rms_bwd_math(x, w, dy):
    d = x.shape[-1]
    r = lax.rsqrt(jnp.mean(x * x, axis=-1, keepdims=True) + NORM_EPS)
    gw = dy * w
    dx = r * gw - x * ((r * r * r) * (jnp.sum(gw * x, axis=-1, keepdims=True) / d))
    return dx, dy * x * r


def _rms_bwd(h, w, dhn, res, *, name):
    s, d = h.shape
    tr = min(ROW_TILE, s)

    def body(h_ref, w_ref, g_ref, r_ref, dh_ref, dw_ref):
        @pl.when(pl.program_id(0) == 0)
        def _():
            dw_ref[...] = jnp.zeros_like(dw_ref)

        dx, dwt = _rms_bwd_math(h_ref[...], w_ref[...], g_ref[...])
        dh_ref[...] = r_ref[...] + dx
        dw_ref[...] += jnp.sum(dwt, axis=0, keepdims=True)

    row = pl.BlockSpec((tr, d), lambda i: (i, 0))
    vec = pl.BlockSpec((1, d), lambda i: (0, 0))
    return pl.pallas_call(
        body, name=name, grid=(s // tr,), in_specs=[row, vec, row, row], out_specs=[row, vec],
        out_shape=[jax.ShapeDtypeStruct((s, d), F32), jax.ShapeDtypeStruct((1, d), F32)],
        compiler_params=_cp("arbitrary"),
    )(h, w.reshape(1, d), dhn, res)


def _final_fwd_bwd(h, w, tgt, *, name):
    s, d = h.shape
    tr = min(ROW_TILE, s)

    def body(h_ref, w_ref, t_ref, loss_ref, dh_ref, dw_ref):
        @pl.when(pl.program_id(0) == 0)
        def _():
            loss_ref[...] = jnp.zeros_like(loss_ref)
            dw_ref[...] = jnp.zeros_like(dw_ref)

        x = h_ref[...]
        wv = w_ref[...]
        r = lax.rsqrt(jnp.mean(x * x, axis=-1, keepdims=True) + NORM_EPS)
        err = x * r * wv - t_ref[...]
        row_loss = jnp.mean(err * err, axis=-1, keepdims=True)
        loss_ref[...] += 0.5 * jnp.sum(row_loss, axis=0, keepdims=True)
        dx, dwt = _rms_bwd_math(x, wv, err / d)
        dh_ref[...] = dx
        dw_ref[...] += jnp.sum(dwt, axis=0, keepdims=True)

    row = pl.BlockSpec((tr, d), lambda i: (i, 0))
    vec = pl.BlockSpec((1, d), lambda i: (0, 0))
    return pl.pallas_call(
        body, name=name, grid=(s // tr,), in_specs=[row, vec, row],
        out_specs=[pl.BlockSpec((1, 128), lambda i: (0, 0)), row, vec],
        out_shape=[jax.ShapeDtypeStruct((1, 128), F32), jax.ShapeDtypeStruct((s, d), F32),
                   jax.ShapeDtypeStruct((1, d), F32)],
        compiler_params=_cp("arbitrary"),
    )(h, w.reshape(1, d), tgt)


def _ple_fwd(h1, gate_pre, up, *, name):
    s, d = h1.shape
    tr = min(ROW_TILE, s)

    def body(h_ref, g_ref, u_ref, o_ref):
        o_ref[...] = h_ref[...] + u_ref[...] * _sigmoid(g_ref[...])

    row = pl.BlockSpec((tr, d), lambda i: (i, 0))
    return pl.pallas_call(body, name=name, grid=(s // tr,), in_specs=[row, row, row], out_specs=row,
                          out_shape=jax.ShapeDtypeStruct((s, d), F32), compiler_params=_cp("parallel"))(h1, gate_pre, up)


def _ple_bwd(dh2, gate_pre, up, *, name):
    s, d = dh2.shape
    tr = min(ROW_TILE, s)

    def body(d_ref, g_ref, u_ref, dup_ref, dgp_ref):
        dh = d_ref[...]
        gate = _sigmoid(g_ref[...])
        dup_ref[...] = (dh * gate).astype(BF16)
        dgp_ref[...] = (dh * u_ref[...] * gate * (1.0 - gate)).astype(BF16)

    row = pl.BlockSpec((tr, d), lambda i: (i, 0))
    return pl.pallas_call(body, name=name, grid=(s // tr,), in_specs=[row, row, row], out_specs=[row, row],
                          out_shape=[jax.ShapeDtypeStruct((s, d), BF16)] * 2, compiler_params=_cp("parallel"))(dh2, gate_pre, up)


HN_TILE = 512


def _hnorm_fwd(o, proj, z_col, w, *, name):
    s = o.shape[0]
    tr = min(HN_TILE, s)

    def body(o_ref, z_ref, w_ref, y_ref):
        wv = w_ref[...]
        for h in range(N_HEADS):
            cols = slice(h * HEAD_DIM, (h + 1) * HEAD_DIM)
            x = o_ref[:, cols]
            r = lax.rsqrt(jnp.mean(x * x, axis=-1, keepdims=True) + NORM_EPS)
            y_ref[:, cols] = (x * r * wv * _silu(z_ref[:, cols])).astype(BF16)

    blk = pl.BlockSpec((tr, BR_WIDTH), lambda i: (i, 0))
    return pl.pallas_call(
        body, name=name, grid=(s // tr,),
        in_specs=[blk, pl.BlockSpec((tr, BR_WIDTH), lambda i: (i, z_col // BR_WIDTH)), pl.BlockSpec((1, HEAD_DIM), lambda i: (0, 0))],
        out_specs=blk, out_shape=jax.ShapeDtypeStruct((s, BR_WIDTH), BF16), compiler_params=_cp("parallel"),
    )(o, proj, w.reshape(1, HEAD_DIM))


def _hnorm_bwd(o, proj, z_col, w, dy, dy_col, *, name):
    s = o.shape[0]
    tr = min(HN_TILE, s)

    def body(o_ref, z_ref, w_ref, dy_ref, do_ref, dz_ref, dw_ref):
        @pl.when(pl.program_id(0) == 0)
        def _():
            dw_ref[...] = jnp.zeros_like(dw_ref)

        wv = w_ref[...]
        dw = jnp.zeros((1, HEAD_DIM), F32)
        for h in range(N_HEADS):
            cols = slice(h * HEAD_DIM, (h + 1) * HEAD_DIM)
            x, z, g = o_ref[:, cols], z_ref[:, cols], dy_ref[:, cols]
            r = lax.rsqrt(jnp.mean(x * x, axis=-1, keepdims=True) + NORM_EPS)
            on = x * r * wv
            don = g * _silu(z)
            dz_ref[:, cols] = (g * on * _dsilu(z)).astype(BF16)
            gw = don * wv
            do_ref[:, cols] = r * gw - x * ((r * r * r) * (jnp.sum(gw * x, axis=-1, keepdims=True) / HEAD_DIM))
            dw = dw + jnp.sum(don * x * r, axis=0, keepdims=True)
        dw_ref[...] += dw

    blk = pl.BlockSpec((tr, BR_WIDTH), lambda i: (i, 0))
    vec = pl.BlockSpec((1, HEAD_DIM), lambda i: (0, 0))
    return pl.pallas_call(
        body, name=name, grid=(s // tr,),
        in_specs=[blk, pl.BlockSpec((tr, BR_WIDTH), lambda i: (i, z_col // BR_WIDTH)), vec,
                  pl.BlockSpec((tr, BR_WIDTH), lambda i: (i, dy_col // BR_WIDTH))],
        out_specs=[blk, blk, vec],
        out_shape=[jax.ShapeDtypeStruct((s, BR_WIDTH), F32), jax.ShapeDtypeStruct((s, BR_WIDTH), BF16),
                   jax.ShapeDtypeStruct((1, HEAD_DIM), F32)],
        compiler_params=_cp("arbitrary"),
    )(o, proj, w.reshape(1, HEAD_DIM), dy)


def _conv_silu(x, w, s):
    row = _iota2(s, x.shape[1], 0)
    c = w[CONV_W - 1:CONV_W, :] * x
    for k in range(1, CONV_W):
        c = c + w[CONV_W - 1 - k:CONV_W - k, :] * jnp.where(row >= k, pltpu.roll(x, k, 0), 0.0)
    return c


def _dn_qkv_fwd(proj, conv_w, *, name):
    s = proj.shape[0]
    nb = 3 * N_HEADS

    def body(x_ref, w_ref, o_ref):
        j = pl.program_id(0)
        sv = _silu(_conv_silu(x_ref[...], w_ref[...], s))
        r = lax.rsqrt(jnp.sum(sv * sv, axis=-1, keepdims=True) + L2_EPS)
        scale = jnp.where(j < N_HEADS, HEAD_DIM ** -0.5, 1.0).astype(F32)
        o_ref[...] = jnp.where(j < 2 * N_HEADS, sv * r * scale, sv)

    return pl.pallas_call(
        body, name=name, grid=(nb,),
        in_specs=[pl.BlockSpec((s, HEAD_DIM), lambda j: (0, j)), pl.BlockSpec((CONV_W, HEAD_DIM), lambda j: (0, j))],
        out_specs=pl.BlockSpec((s, HEAD_DIM), lambda j: (0, j)),
        out_shape=jax.ShapeDtypeStruct((s, 3 * BR_WIDTH), F32), compiler_params=_cp("parallel"),
    )(proj, conv_w)


def _dn_qkv_bwd(proj, conv_w, dqkv, *, name):
    s = proj.shape[0]
    nb = 3 * N_HEADS

    def body(x_ref, w_ref, g_ref, dx_ref, dw_ref):
        j = pl.program_id(0)
        x, w, g = x_ref[...], w_ref[...], g_ref[...]
        c = _conv_silu(x, w, s)
        sv = _silu(c)
        r = lax.rsqrt(jnp.sum(sv * sv, axis=-1, keepdims=True) + L2_EPS)
        scale = jnp.where(j < N_HEADS, HEAD_DIM ** -0.5, 1.0).astype(F32)
        ds_n = scale * (r * g - sv * ((r * r * r) * jnp.sum(g * sv, axis=-1, keepdims=True)))
        dc = jnp.where(j < 2 * N_HEADS, ds_n, g) * _dsilu(c)
        row = _iota2(s, HEAD_DIM, 0)
        dx = w[CONV_W - 1:CONV_W, :] * dc
        dws = [jnp.sum(dc * x, axis=0, keepdims=True)]
        for k in range(1, CONV_W):
            dx = dx + w[CONV_W - 1 - k:CONV_W - k, :] * jnp.where(row < s - k, pltpu.roll(dc, s - k, 0), 0.0)
            dws.append(jnp.sum(dc * jnp.where(row >= k, pltpu.roll(x, k, 0), 0.0), axis=0, keepdims=True))
        dx_ref[...] = dx.astype(BF16)
        for k in range(CONV_W):
            dw_ref[CONV_W - 1 - k:CONV_W - k, :] = dws[k]

    blk = pl.BlockSpec((s, HEAD_DIM), lambda j: (0, j))
    wblk = pl.BlockSpec((CONV_W, HEAD_DIM), lambda j: (0, j))
    return pl.pallas_call(
        body, name=name, grid=(nb,), in_specs=[blk, wblk, blk], out_specs=[blk, wblk],
        out_shape=[jax.ShapeDtypeStruct((s, 3 * BR_WIDTH), BF16), jax.ShapeDtypeStruct((CONV_W, 3 * BR_WIDTH), F32)],
        compiler_params=_cp("parallel"),
    )(proj, conv_w, dqkv)


def _tri(n, kind):
    r, c = _iota2(n, n, 0), _iota2(n, n, 1)
    if kind == "lower":
        return (r >= c).astype(F32)
    if kind == "upper":
        return (r <= c).astype(F32)
    return (r == c).astype(F32)


GATE_TILE = 512


def _dn_gate_fwd(proj, a_log, dt_bias, *, name):
    s = proj.shape[0]
    tr = min(GATE_TILE, s)

    def body(b_ref, a_ref, al_ref, dt_ref, beta_ref, g_ref):
        beta_ref[...] = _sigmoid(b_ref[...])
        g = -jnp.exp(al_ref[...]) * _softplus(a_ref[...] + dt_ref[...])
        low = _tri(CHUNK, "lower")
        for c in range(tr // CHUNK):
            rows = slice(c * CHUNK, (c + 1) * CHUNK)
            g_ref[rows, :] = _nn_exact(low, g[rows, :])

    blk = lambda cb: pl.BlockSpec((tr, HEAD_DIM), lambda i: (i, cb))
    vec = pl.BlockSpec((1, HEAD_DIM), lambda i: (0, 0))
    out = pl.BlockSpec((tr, HEAD_DIM), lambda i: (i, 0))
    return pl.pallas_call(
        body, name=name, grid=(s // tr,), in_specs=[blk(C_B // HEAD_DIM), blk(C_A // HEAD_DIM), vec, vec],
        out_specs=[out, out], out_shape=[jax.ShapeDtypeStruct((s, HEAD_DIM), F32)] * 2, compiler_params=_cp("parallel"),
    )(proj, proj, a_log, dt_bias)


def _dn_gate_bwd(proj, a_log, dt_bias, dbeta, d_g, *, name):
    s = proj.shape[0]
    tr = min(GATE_TILE, s)

    def body(b_ref, a_ref, al_ref, dt_ref, dbeta_ref, dG_ref, db_ref, da_ref, dal_ref, ddt_ref):
        @pl.when(pl.program_id(0) == 0)
        def _():
            dal_ref[...] = jnp.zeros_like(dal_ref)
            ddt_ref[...] = jnp.zeros_like(ddt_ref)

        beta = _sigmoid(b_ref[...])
        db_ref[...] = (dbeta_ref[...] * beta * (1.0 - beta)).astype(BF16)
        pre = a_ref[...] + dt_ref[...]
        neg_ea = -jnp.exp(al_ref[...])
        up = _tri(CHUNK, "upper")
        d_g = dG_ref[...]
        dg = jnp.concatenate([_nn_exact(up, d_g[c * CHUNK:(c + 1) * CHUNK, :]) for c in range(tr // CHUNK)], axis=0)
        da = dg * neg_ea * _sigmoid(pre)
        da_ref[...] = da.astype(BF16)
        ddt_ref[...] += jnp.sum(da, axis=0, keepdims=True)
        dal_ref[...] += jnp.sum(dg * neg_ea * _softplus(pre), axis=0, keepdims=True)

    blk = lambda cb: pl.BlockSpec((tr, HEAD_DIM), lambda i: (i, cb))
    vec = pl.BlockSpec((1, HEAD_DIM), lambda i: (0, 0))
    io = pl.BlockSpec((tr, HEAD_DIM), lambda i: (i, 0))
    return pl.pallas_call(
        body, name=name, grid=(s // tr,),
        in_specs=[blk(C_B // HEAD_DIM), blk(C_A // HEAD_DIM), vec, vec, io, io], out_specs=[io, io, vec, vec],
        out_shape=[jax.ShapeDtypeStruct((s, HEAD_DIM), BF16)] * 2 + [jax.ShapeDtypeStruct((1, HEAD_DIM), F32)] * 2,
        compiler_params=_cp("arbitrary"),
    )(proj, proj, a_log, dt_bias, dbeta, d_g)


def _unit_lower_inverse(a_strict, eye):
    x = -a_strict
    t = x + eye
    p = x
    n = 2
    while n < CHUNK:
        p = _nn(p, p)
        t = t + _nn(t, p)
        n *= 2
    return t


def _dn_chunk_common(q, k, v, gc, beta, st):
    c = CHUNK
    eye = _tri(c, "eye")
    low = _tri(c, "lower")
    strict = low - eye
    grow = _col2row(gc, eye)
    dec = _hmap(lambda g_, gr: low * jnp.exp(low * (g_ - gr)), gc, grow)
    kb = k * beta
    a_mat = _nt(kb, k) * dec * strict
    t_inv = _unit_lower_inverse(a_mat, eye)
    e_g = _exp(gc)
    u = _nn(t_inv, v * beta)
    w = _nn(t_inv, kb * e_g)
    p_qk = _nt(q, k)
    qk = p_qk * dec
    qd = q * e_g
    last = (_iota2(c, 1, 0) == c - 1).astype(F32)
    g_last = _sum(gc * last, 0)
    e_t = _exp(g_last - gc)
    kt = k * e_t
    tail = _exp(g_last)
    vn = u - _nn(w, st)
    return dict(eye=eye, low=low, strict=strict, dec=dec, kb=kb, a_mat=a_mat, t_inv=t_inv, e_g=e_g, u=u, w=w,
                qk=qk, qd=qd, last=last, e_t=e_t, kt=kt, tail=tail, vn=vn)


def _dn_chunk_fwd_math(q, k, v, gc, beta, st):
    m = _dn_chunk_common(q, k, v, gc, beta, st)
    o = _nn(m["qd"], st) + _nn(m["qk"], m["vn"])
    st2 = st * m["tail"] + _tn(m["kt"], m["vn"])
    return o, st2


def _dn_chunk_bwd_math(q, k, v, gc, beta, st, do, dst2):
    m = _dn_chunk_common(q, k, v, gc, beta, st)
    eye, low, strict = m["eye"], m["low"], m["strict"]
    dvn = _tn(m["qk"], do) + _nn(m["kt"], dst2)
    dqk = _nt(do, m["vn"]) * low
    dqd = _nt(do, st)
    dst = _tn(m["qd"], do) + dst2 * m["tail"] - _tn(m["w"], dvn)
    dkt = _nt(m["vn"], dst2)
    dtail = _sum(_sum(st * dst2, 1), 0)
    dw = -_nt(dvn, st)
    dvb = _tn(m["t_inv"], dvn)
    dkg = _tn(m["t_inv"], dw)
    d_a = (_nt(dvb, m["u"]) + _nt(dkg, m["w"])) * (-strict)
    dkk = d_a * m["dec"]
    dp = dqk * m["dec"]
    dq = _nn(dp, k) + dqd * m["e_g"]
    dkb = _nn(dkk, k) + dkg * m["e_g"]
    dk = _tn(dp, q) + _tn(dkk, m["kb"]) + dkb * beta + dkt * m["e_t"]
    dv = dvb * beta
    dbeta = _sum(dvb * v + dkb * k, 1)
    de_g = _sum(dkg * m["kb"] + dqd * q, 1)
    de_t = _sum(dkt * k, 1)
    mm = d_a * m["a_mat"] + dqk * m["qk"]
    dgc = (_sum(mm, 1) - _row2col(_sum(mm, 0), eye) + de_g * m["e_g"] - de_t * m["e_t"]
           + (_sum(de_t * m["e_t"], 0) + dtail * m["tail"]) * m["last"])
    return dq, dk, dv, dgc, dbeta, dst


def _heads_of(ref):
    return _Heads(ref[:, h * HEAD_DIM:(h + 1) * HEAD_DIM] for h in range(N_HEADS))


def _lanes_of(block):
    return _Heads(_pick_lane(block, h) for h in range(N_HEADS))


def _dn_chunk_fwd(qkv, gcs, beta, *, name):
    s = qkv.shape[0]
    n = s // CHUNK

    def body(q_ref, k_ref, v_ref, g_ref, b_ref, o_ref, st_out_ref, st_ref):
        @pl.when(pl.program_id(0) == 0)
        def _():
            st_ref[...] = jnp.zeros_like(st_ref)

        gblk, bblk = g_ref[...], b_ref[...]
        st = _Heads(st_ref[h] for h in range(N_HEADS))
        o, st2 = _dn_chunk_fwd_math(_heads_of(q_ref), _heads_of(k_ref), _heads_of(v_ref), _lanes_of(gblk),
                                    _lanes_of(bblk), st)
        for h in range(N_HEADS):
            st_out_ref[0, h] = st.v[h]
            o_ref[:, h * HEAD_DIM:(h + 1) * HEAD_DIM] = o.v[h]
            st_ref[h] = st2.v[h]

    blk = lambda off: pl.BlockSpec((CHUNK, BR_WIDTH), lambda c: (c, off))
    sc = pl.BlockSpec((CHUNK, HEAD_DIM), lambda c: (c, 0))
    return pl.pallas_call(
        body, name=name, grid=(n,),
        in_specs=[blk(0), blk(1), blk(2), sc, sc],
        out_specs=[blk(0), pl.BlockSpec((1, N_HEADS, HEAD_DIM, HEAD_DIM), lambda c: (c, 0, 0, 0))],
        out_shape=[jax.ShapeDtypeStruct((s, BR_WIDTH), F32), jax.ShapeDtypeStruct((n, N_HEADS, HEAD_DIM, HEAD_DIM), F32)],
        scratch_shapes=[pltpu.VMEM((N_HEADS, HEAD_DIM, HEAD_DIM), F32)],
        compiler_params=_cp("arbitrary"),
    )(qkv, qkv, qkv, gcs, beta)


def _dn_chunk_bwd(qkv, gcs, beta, states, do, *, name):
    s = qkv.shape[0]
    n = s // CHUNK

    def body(q_ref, k_ref, v_ref, g_ref, b_ref, st_in_ref, do_ref, dqkv_ref, dg_ref, dbeta_ref, dst_ref):
        @pl.when(pl.program_id(0) == 0)
        def _():
            dst_ref[...] = jnp.zeros_like(dst_ref)

        gblk, bblk = g_ref[...], b_ref[...]
        lane = _iota2(CHUNK, HEAD_DIM, 1)
        dg_all = jnp.zeros((CHUNK, HEAD_DIM), F32)
        dbeta_all = jnp.zeros((CHUNK, HEAD_DIM), F32)
        dq, dk, dv, dgc, dbeta, dst = _dn_chunk_bwd_math(
            _heads_of(q_ref), _heads_of(k_ref), _heads_of(v_ref), _lanes_of(gblk), _lanes_of(bblk),
            _Heads(st_in_ref[0, h] for h in range(N_HEADS)), _heads_of(do_ref),
            _Heads(dst_ref[h] for h in range(N_HEADS)))
        for h in range(N_HEADS):
            for part, val in enumerate((dq, dk, dv)):
                c0 = part * BR_WIDTH + h * HEAD_DIM
                dqkv_ref[:, c0:c0 + HEAD_DIM] = val.v[h]
            dg_all = jnp.where(lane == h, dgc.v[h], dg_all)
            dbeta_all = jnp.where(lane == h, dbeta.v[h], dbeta_all)
            dst_ref[h] = dst.v[h]
        dg_ref[...] = dg_all
        dbeta_ref[...] = dbeta_all

    blk = lambda off: pl.BlockSpec((CHUNK, BR_WIDTH), lambda c: (n - 1 - c, off))
    sc = pl.BlockSpec((CHUNK, HEAD_DIM), lambda c: (n - 1 - c, 0))
    outs = pl.pallas_call(
        body, name=name, grid=(n,),
        in_specs=[blk(0), blk(1), blk(2), sc, sc,
                  pl.BlockSpec((1, N_HEADS, HEAD_DIM, HEAD_DIM), lambda c: (n - 1 - c, 0, 0, 0)), blk(0)],
        out_specs=[pl.BlockSpec((CHUNK, 3 * BR_WIDTH), lambda c: (n - 1 - c, 0)), sc, sc],
        out_shape=[jax.ShapeDtypeStruct((s, 3 * BR_WIDTH), F32)] + [jax.ShapeDtypeStruct((s, HEAD_DIM), F32)] * 2,
        scratch_shapes=[pltpu.VMEM((N_HEADS, HEAD_DIM, HEAD_DIM), F32)],
        compiler_params=_cp("arbitrary"),
    )(qkv, qkv, qkv, gcs, beta, states, do)
    return outs


def _hg_prep_fwd(proj, lb, *, name):
    s = proj.shape[0]
    tr = min(ROW_TILE, s)

    def body(q_ref, f_ref, lb_ref, qo_ref, ko_ref, lf_ref):
        f, lbv = f_ref[...], lb_ref[...]
        qo_ref[...] = _silu(q_ref[...])
        ko_ref[...] = (1.0 - lbv) * _sigmoid(-f)
        lf_ref[...] = jnp.log(lbv + (1.0 - lbv) * _sigmoid(f))

    blk = lambda cb: pl.BlockSpec((tr, BR_WIDTH), lambda i: (i, cb))
    out = pl.BlockSpec((tr, BR_WIDTH), lambda i: (i, 0))
    return pl.pallas_call(
        body, name=name, grid=(s // tr,),
        in_specs=[blk(C_HQ // BR_WIDTH), blk(C_HF // BR_WIDTH), pl.BlockSpec((1, BR_WIDTH), lambda i: (0, 0))],
        out_specs=[out, out, out], out_shape=[jax.ShapeDtypeStruct((s, BR_WIDTH), F32)] * 3, compiler_params=_cp("parallel"),
    )(proj, proj, lb)


def _hg_prep_bwd(proj, lb, dq, dk, dlf, *, name):
    s = proj.shape[0]
    tr = min(ROW_TILE, s)

    def body(q_ref, f_ref, lb_ref, dq_ref, dk_ref, dlf_ref, dhq_ref, dhf_ref, dlb_ref):
        @pl.when(pl.program_id(0) == 0)
        def _():
            dlb_ref[...] = jnp.zeros_like(dlb_ref)

        f, lbv = f_ref[...], lb_ref[...]
        dhq_ref[...] = (dq_ref[...] * _dsilu(q_ref[...])).astype(BF16)
        sp, sn = _sigmoid(f), _sigmoid(-f)
        inner = lbv + (1.0 - lbv) * sp
        dlf_over = dlf_ref[...] / inner
        dkv = dk_ref[...]
        dhf_ref[...] = (dlf_over * (1.0 - lbv) * sp * sn - dkv * (1.0 - lbv) * sn * (1.0 - sn)).astype(BF16)
        dlb_ref[...] += jnp.sum(dlf_over * (1.0 - sp) - dkv * sn, axis=0, keepdims=True)

    blk = lambda cb: pl.BlockSpec((tr, BR_WIDTH), lambda i: (i, cb))
    io = pl.BlockSpec((tr, BR_WIDTH), lambda i: (i, 0))
    vec = pl.BlockSpec((1, BR_WIDTH), lambda i: (0, 0))
    return pl.pallas_call(
        body, name=name, grid=(s // tr,),
        in_specs=[blk(C_HQ // BR_WIDTH), blk(C_HF // BR_WIDTH), vec, io, io, io], out_specs=[io, io, vec],
        out_shape=[jax.ShapeDtypeStruct((s, BR_WIDTH), BF16)] * 2 + [jax.ShapeDtypeStruct((1, BR_WIDTH), F32)],
        compiler_params=_cp("arbitrary"),
    )(proj, proj, lb, dq, dk, dlf)


def _hg_chunk_common(q, k, lf):
    c = CHUNK
    g = _nn_exact(_tri(c, "lower"), lf)
    e_g = _exp(g)
    qd = q * e_g
    g_last = g[c - 1:c, :]
    e_t = _exp(g_last - g)
    kt = k * e_t
    tail = _exp(g_last)
    q_sc, k_sc, e_q, e_k = [], [], [], []
    for i in range(c // SUB):
        g_ref = g[i * SUB:i * SUB + 1, :]
        eq = _exp(g[i * SUB:(i + 1) * SUB, :] - g_ref)
        ek = _hmap(lambda gr, g_: jnp.exp(jnp.minimum(gr - g_, EXP_CLAMP)), g_ref, g)
        e_q.append(eq)
        e_k.append(ek)
        q_sc.append(q[i * SUB:(i + 1) * SUB, :] * eq)
        k_sc.append(k * ek)
    a_mat = _stack_rows([_nt(qi, ki) for qi, ki in zip(q_sc, k_sc)]) * _tri(c, "lower")
    return dict(e_g=e_g, qd=qd, e_t=e_t, kt=kt, tail=tail, q_sc=q_sc, k_sc=k_sc, e_q=e_q, e_k=e_k, a_mat=a_mat)


def _hg_chunk_fwd_math(q, k, v, lf, stt):
    m = _hg_chunk_common(q, k, lf)
    o = _nt(m["qd"], stt) + _nn(m["a_mat"], v)
    stt2 = stt * m["tail"] + _tn(v, m["kt"])
    return o, stt2


def _hg_chunk_bwd_math(q, k, v, lf, stt, do, dstt2):
    c = CHUNK
    m = _hg_chunk_common(q, k, lf)
    stt2 = stt * m["tail"] + _tn(v, m["kt"])
    later = _sum(stt2 * dstt2, 0)
    dqd = _dot3(do, stt, 1, 0)
    dstt = _tn(do, m["qd"]) + dstt2 * m["tail"]
    d_a = _dot3(do, v, 1, 1) * _tri(c, "lower")
    dv = _tn(m["a_mat"], do) + _nt(m["kt"], dstt2)
    dkt = _dot3(v, dstt2, 1, 0)
    dq_parts = []
    dk = dkt * m["e_t"]
    for i in range(c // SUB):
        d_ai = d_a[i * SUB:(i + 1) * SUB, :]
        dq_parts.append(_dot3(d_ai, m["k_sc"][i], 1, 0) * m["e_q"][i])
        dk = dk + _dot3(d_ai, m["q_sc"][i], 0, 0) * m["e_k"][i]
    dq = dqd * m["e_g"] + _stack_rows(dq_parts)
    db = q * dq - k * dk
    dlf = _nn_exact(_tri(c, "upper"), db) + later
    return dq, dk, dv, dlf, dstt


def _hg_chunk_fwd(qh, kh, proj, lf, *, name):
    s = qh.shape[0]
    n = s // CHUNK
    vb = C_HI // BR_WIDTH

    def body(q_ref, k_ref, v_ref, lf_ref, o_ref, st_out_ref, st_ref):
        @pl.when(pl.program_id(0) == 0)
        def _():
            st_ref[...] = jnp.zeros_like(st_ref)

        st = _Heads(st_ref[h] for h in range(N_HEADS))
        o, st2 = _hg_chunk_fwd_math(_heads_of(q_ref), _heads_of(k_ref), _heads_of(v_ref), _heads_of(lf_ref), st)
        for h in range(N_HEADS):
            st_out_ref[0, h] = st.v[h]
            o_ref[:, h * HEAD_DIM:(h + 1) * HEAD_DIM] = o.v[h]
            st_ref[h] = st2.v[h]

    blk = lambda off: pl.BlockSpec((CHUNK, BR_WIDTH), lambda c: (c, off))
    return pl.pallas_call(
        body, name=name, grid=(n,), in_specs=[blk(0), blk(0), blk(vb), blk(0)],
        out_specs=[blk(0), pl.BlockSpec((1, N_HEADS, HEAD_DIM, HEAD_DIM), lambda c: (c, 0, 0, 0))],
        out_shape=[jax.ShapeDtypeStruct((s, BR_WIDTH), F32), jax.ShapeDtypeStruct((n, N_HEADS, HEAD_DIM, HEAD_DIM), F32)],
        scratch_shapes=[pltpu.VMEM((N_HEADS, HEAD_DIM, HEAD_DIM), F32)],
        compiler_params=_cp("arbitrary"),
    )(qh, kh, proj, lf)


def _hg_chunk_bwd(qh, kh, proj, lf, states, do, *, name):
    s = qh.shape[0]
    n = s // CHUNK
    vb = C_HI // BR_WIDTH

    def body(q_ref, k_ref, v_ref, lf_ref, st_in_ref, do_ref, dq_ref, dk_ref, dv_ref, dlf_ref, dst_ref):
        @pl.when(pl.program_id(0) == 0)
        def _():
            dst_ref[...] = jnp.zeros_like(dst_ref)

        dq, dk, dv, dlf, dst = _hg_chunk_bwd_math(
            _heads_of(q_ref), _heads_of(k_ref), _heads_of(v_ref), _heads_of(lf_ref),
            _Heads(st_in_ref[0, h] for h in range(N_HEADS)), _heads_of(do_ref),
            _Heads(dst_ref[h] for h in range(N_HEADS)))
        for h in range(N_HEADS):
            cols = slice(h * HEAD_DIM, (h + 1) * HEAD_DIM)
            dq_ref[:, cols] = dq.v[h]
            dk_ref[:, cols] = dk.v[h]
            dv_ref[:, cols] = dv.v[h].astype(BF16)
            dlf_ref[:, cols] = dlf.v[h]
            dst_ref[h] = dst.v[h]

    blk = lambda off: pl.BlockSpec((CHUNK, BR_WIDTH), lambda c: (n - 1 - c, off))
    return pl.pallas_call(
        body, name=name, grid=(n,),
        in_specs=[blk(0), blk(0), blk(vb), blk(0),
                  pl.BlockSpec((1, N_HEADS, HEAD_DIM, HEAD_DIM), lambda c: (n - 1 - c, 0, 0, 0)), blk(0)],
        out_specs=[blk(0), blk(0), blk(0), blk(0)],
        out_shape=[jax.ShapeDtypeStruct((s, BR_WIDTH), F32)] * 2 + [jax.ShapeDtypeStruct((s, BR_WIDTH), BF16),
                                                                    jax.ShapeDtypeStruct((s, BR_WIDTH), F32)],
        scratch_shapes=[pltpu.VMEM((N_HEADS, HEAD_DIM, HEAD_DIM), F32)],
        compiler_params=_cp("arbitrary"),
    )(qh, kh, proj, lf, states, do)


_ANY = pl.BlockSpec(memory_space=pl.ANY)
_MESH = pl.DeviceIdType.MESH


def _all_gather(x_local, *, name, after=()):
    n_after = len(after)

    def body(x_ref, *refs):
        out_ref, send_sems, recv_sems, local_sem = refs[n_after:]
        x, y, c = lax.axis_index("x"), lax.axis_index("y"), lax.axis_index("c")
        me, sibling = (x, y, c), (x, y, 1 - c)
        chips = [(1 - x, y), (x, 1 - y), (1 - x, 1 - y)]

        def slot(px, py, pc):
            return out_ref.at[4 * px + 2 * py + pc]

        def copy(k, block, to, src=None):
            return pltpu.make_async_remote_copy(
                src_ref=slot(*block) if src is None else src, dst_ref=slot(*block),
                send_sem=send_sems.at[k], recv_sem=recv_sems.at[k], device_id=to, device_id_type=_MESH)

        mine = pltpu.make_async_copy(x_ref, slot(*me), local_sem)
        mine.start()
        first = [copy(0, me, sibling, src=x_ref)]
        first += [copy(1 + j, me, (*chip, c), src=x_ref) for j, chip in enumerate(chips)]
        for cp in first:
            cp.start()
        passed = [copy(4 + j, (*chip, c), sibling) for j, chip in enumerate(chips)]
        for j, chip in enumerate(chips):
            copy(1 + j, (*chip, c), me).wait_recv()
            passed[j].start()
        copy(0, sibling, me).wait_recv()
        for j, chip in enumerate(chips):
            copy(4 + j, (*chip, 1 - c), me).wait_recv()
        for cp in first + passed:
            cp.wait_send()
        mine.wait()

    return pl.pallas_call(
        body, name=name, out_shape=jax.ShapeDtypeStruct((N_DEV,) + x_local.shape, x_local.dtype),
        in_specs=[_ANY] * (1 + n_after), out_specs=_ANY,
        scratch_shapes=[pltpu.SemaphoreType.DMA((7,)), pltpu.SemaphoreType.DMA((7,)), pltpu.SemaphoreType.DMA],
    )(x_local, *after)


_HBM = pl.BlockSpec(memory_space=pltpu.HBM)
_SEM = pl.BlockSpec(memory_space=pltpu.SEMAPHORE)
_EFFECT = pltpu.SideEffectType.DATAFLOW_SIDE_EFFECTING


def _peers():
    x, y, c = lax.axis_index("x"), lax.axis_index("y"), lax.axis_index("c")
    out = []
    for k in range(1, N_DEV):
        px, py, pc = x ^ ((k >> 2) & 1), y ^ ((k >> 1) & 1), c ^ (k & 1)
        out.append(((px, py, pc), 4 * px + 2 * py + pc))
    return 4 * x + 2 * y + c, out


def _push_copies(src_ref, land_ref, send_sems, recv_sems, broadcast):
    my, peers = _peers()
    pairs = []
    for k, (pos, idx) in enumerate(peers):
        src = src_ref if broadcast else src_ref.at[idx]
        send = pltpu.make_async_remote_copy(src_ref=src, dst_ref=land_ref.at[my], send_sem=send_sems.at[k],
                                            recv_sem=recv_sems.at[k], device_id=pos, device_id_type=_MESH)
        recv = pltpu.make_async_remote_copy(src_ref=src, dst_ref=land_ref.at[idx], send_sem=send_sems.at[k],
                                            recv_sem=recv_sems.at[k], device_id=pos, device_id_type=_MESH)
        pairs.append((send, recv))
    return pairs


def _push_start(src, land, *, broadcast, name, after=()):
    n_after = len(after)

    def body(src_ref, land_ref, *refs):
        send_sems, recv_sems, _, _, token = refs[n_after:]
        for send, _ in _push_copies(src_ref, land_ref, send_sems, recv_sems, broadcast):
            send.start()
        token[...] = jnp.zeros_like(token)

    return pl.pallas_call(
        body, name=name,
        out_shape=(pltpu.SemaphoreType.DMA((N_DEV - 1,)), pltpu.SemaphoreType.DMA((N_DEV - 1,)),
                   pltpu.HBM(src.shape, src.dtype), pltpu.HBM(land.shape, land.dtype), jax.ShapeDtypeStruct((8, 128), F32)),
        in_specs=(_HBM, _HBM) + (_ANY,) * n_after, out_specs=(_SEM, _SEM, _HBM, _HBM, pl.BlockSpec(memory_space=pltpu.VMEM)),
        input_output_aliases={0: 2, 1: 3}, compiler_params=pltpu.CompilerParams(has_side_effects=_EFFECT),
    )(pltpu.with_memory_space_constraint(src, pltpu.HBM), pltpu.with_memory_space_constraint(land, pltpu.HBM), *after)


def _push_wait(handle, after, *, broadcast, name):
    send_sems, recv_sems, src_thru, land_thru, _ = handle

    def body(src_ref, land_ref, send_sems, recv_sems, *rest):
        for send, recv in _push_copies(src_ref, land_ref, send_sems, recv_sems, broadcast):
            send.wait_send()
            recv.wait_recv()

    return pl.pallas_call(
        body, name=name,
        out_shape=(pltpu.HBM(src_thru.shape, src_thru.dtype), pltpu.HBM(land_thru.shape, land_thru.dtype)),
        in_specs=(_HBM, _HBM, _SEM, _SEM) + (_ANY,) * len(after), out_specs=(_HBM, _HBM),
        input_output_aliases={0: 0, 1: 1}, compiler_params=pltpu.CompilerParams(has_side_effects=_EFFECT),
    )(src_thru, land_thru, send_sems, recv_sems, *after)[1]


def _adamw(parts, row_off, w, m, v, *, layer=0, n_layers=1, prev=None, name, tr):
    rows, c = w.shape
    r = rows // n_layers
    np_ = parts.shape[0]
    tr = min(tr, r)
    assert r % tr == 0 and row_off % tr == 0
    ob, lb = row_off // tr, layer * (r // tr)
    c1 = 1.0 - ADAM_B1 ** ADAM_STEP
    c2 = 1.0 - ADAM_B2 ** ADAM_STEP
    n_prev = 0 if prev is None else 4

    def body(p_ref, w_ref, m_ref, v_ref, *refs):
        g_ref, d_ref, nm_ref, nv_ref = refs[n_prev:]
        g = p_ref[0].astype(F32)
        for s in range(1, np_):
            g = g + p_ref[s].astype(F32)
        wv = w_ref[...]
        m2 = ADAM_B1 * m_ref[...] + (1.0 - ADAM_B1) * g
        v2 = ADAM_B2 * v_ref[...] + (1.0 - ADAM_B2) * jnp.square(g)
        m_hat = m2 / c1
        v_hat = v2 / c2
        g_ref[...] = g
        d_ref[...] = -ADAM_LR * (m_hat / (jnp.sqrt(v_hat) + ADAM_EPS) + ADAM_WD * wv)
        nm_ref[...] = m2
        nv_ref[...] = v2

    blk = pl.BlockSpec((tr, c), lambda i: (lb + i, 0))
    return pl.pallas_call(
        body, name=name, grid=(r // tr,),
        in_specs=[pl.BlockSpec((np_, tr, c), lambda i: (0, ob + i, 0)), blk, blk, blk] + [_ANY] * n_prev,
        out_specs=[blk] * 4, out_shape=[jax.ShapeDtypeStruct((rows, c), F32)] * 4,
        input_output_aliases={4 + i: i for i in range(n_prev)}, compiler_params=_cp("parallel"),
    )(parts, w, m, v, *(prev or ()))


def _sum_parts(parts, *, name, after=()):
    np_, r, c = parts.shape

    def body(p_ref, *refs):
        o_ref = refs[-1]
        g = p_ref[0]
        for s in range(1, np_):
            g = g + p_ref[s]
        o_ref[...] = g

    vmem = pl.BlockSpec(memory_space=pltpu.VMEM)
    return pl.pallas_call(body, name=name, in_specs=[vmem] + [_ANY] * len(after), out_specs=vmem,
                          out_shape=jax.ShapeDtypeStruct((r, c), F32))(parts, *after)


def _pack(arrs):
    rows = []
    for a in arrs:
        f = a.reshape(-1).astype(F32)
        pad = (-f.shape[0]) % 128
        rows.append(jnp.pad(f, (0, pad)).reshape(-1, 128))
    out = jnp.concatenate(rows, axis=0)
    return jnp.pad(out, ((0, (-out.shape[0]) % 8), (0, 0)))


def _unpack(packed, shapes):
    outs, r0 = [], 0
    for shp in shapes:
        n = 1
        for d in shp:
            n *= d
        nr = -(-n // 128)
        outs.append(packed[r0:r0 + nr].reshape(-1)[:n].reshape(shp))
        r0 += nr
    return outs


_WIN_PIECES = ((0, 4096, 0), (4112, 8208, 0), (4096, 4104, HEAD_DIM - N_HEADS), (4104, 4112, HEAD_DIM - N_HEADS))


def _win_from_shards(shards):
    cols = []
    for lo, hi, pad in _WIN_PIECES:
        for j in range(N_DEV):
            a, b = max(lo, j * SHARD_IN), min(hi, (j + 1) * SHARD_IN)
            if a < b:
                cols.append(shards[j][:, a - j * SHARD_IN:b - j * SHARD_IN])
        if pad:
            cols.append(jnp.zeros((shards[0].shape[0], pad), shards[0].dtype))
    return jnp.concatenate(cols, axis=1)


def _win_to_shards(g):
    starts, off = [], 0
    for lo, hi, pad in _WIN_PIECES:
        starts.append((lo, hi, off))
        off += hi - lo + pad
    shards = []
    for j in range(N_DEV):
        cols = []
        for lo, hi, off in sorted(starts):
            a, b = max(lo, j * SHARD_IN), min(hi, (j + 1) * SHARD_IN)
            if a < b:
                cols.append(g[:, off + a - lo:off + b - lo])
        shards.append(jnp.concatenate(cols, axis=1))
    return shards


def _lower_bounds(logits):
    probs = jax.nn.softmax(logits.astype(F32), axis=0)
    return jnp.cumsum(probs, axis=0) - probs[0]


def _pad_lanes(vec8):
    return jnp.pad(vec8.reshape(1, N_HEADS), ((0, 0), (0, HEAD_DIM - N_HEADS)))


def kernel(x, p, norm_w, w_in, dn_conv_w, dn_A_log, dn_dt_bias, dn_norm_w, hg_lb_logits, hg_norm_w, w_out, w_ple_up, w_ple_gate, final_norm_w, loss_target, m_norm_w, m_w_in, m_dn_conv_w, m_dn_A_log, m_dn_dt_bias, m_dn_norm_w, m_hg_lb_logits, m_hg_norm_w, m_w_out, m_w_ple_up, m_w_ple_gate, m_final_norm_w, v_norm_w, v_w_in, v_dn_conv_w, v_dn_A_log, v_dn_dt_bias, v_dn_norm_w, v_hg_lb_logits, v_hg_norm_w, v_w_out, v_w_ple_up, v_w_ple_gate, v_final_norm_w):
    depth = norm_w.shape[0]
    my = 4 * lax.axis_index("x") + 2 * lax.axis_index("y") + lax.axis_index("c")
    h = x[0]
    tgt = loss_target[0]
    rows_out = D_MODEL // N_DEV
    up_rows = PLE_DIM * (D_MODEL // N_DEV) // D_MODEL
    g_off, u_off = rows_out, 2 * rows_out

    def own_slot(block):
        return lax.dynamic_update_index_in_dim(lax.empty((N_DEV,) + block.shape, block.dtype), block, my, 0)

    win_bf = w_in.astype(BF16)
    rest_bf = [jnp.concatenate([w_out[l], w_ple_gate[l], w_ple_up[l].reshape(up_rows, D_MODEL)], axis=0).astype(BF16)
               for l in range(depth)]
    conv_all = _all_gather(dn_conv_w, name="gather_conv_w")
    conv_full = conv_all.transpose(1, 2, 0, 3).reshape(depth, CONV_W, 3 * BR_WIDTH)
    win_all = {0: _all_gather(win_bf[0], name="gather_w_in_l0", after=[conv_all])}
    pending = {}
    last = win_all[0]
    for l in range(depth):
        if l > 0:
            pending["win", l] = _push_start(win_bf[l], own_slot(win_bf[l]), broadcast=True, after=[last],
                                            name=f"gather_w_in_l{l}_start")
            last = pending["win", l][4]
        pending["rest", l] = _push_start(rest_bf[l], own_slot(rest_bf[l]), broadcast=True, after=[last],
                                         name=f"gather_rest_l{l}_start")
        last = pending["rest", l][4]
    order_tok = last[0, 0]
    lbs = _lower_bounds(hg_lb_logits)

    saved = []
    weights = []
    for l in range(depth):
        tag = f"l{l}"
        if l > 0:
            win_all[l] = _push_wait(pending["win", l], [h], broadcast=True, name=f"gather_w_in_{tag}_wait")
        wi = _win_from_shards([win_all[l][j] for j in range(N_DEV)])
        nw = norm_w[l] + order_tok if l == 0 else norm_w[l]
        hn = _rms_fwd(h, nw, name=f"rms_fwd_{tag}")
        proj = _mm(hn, wi, mode="nn", out_dtype=F32, name=f"mm_proj_{tag}")
        al, dt = _pad_lanes(dn_A_log[l]), _pad_lanes(dn_dt_bias[l])
        qkv = _dn_qkv_fwd(proj, conv_full[l], name=f"dn_qkv_fwd_{tag}")
        beta, gcs = _dn_gate_fwd(proj, al, dt, name=f"dn_gate_fwd_{tag}")
        o_dn, st_dn = _dn_chunk_fwd(qkv, gcs, beta, name=f"dn_chunk_fwd_{tag}")
        lb = lbs[l].reshape(1, BR_WIDTH)
        qh, kh, lf = _hg_prep_fwd(proj, lb, name=f"hg_prep_fwd_{tag}")
        o_hg, st_hg = _hg_chunk_fwd(qh, kh, proj, lf, name=f"hg_chunk_fwd_{tag}")
        y_dn = _hnorm_fwd(o_dn, proj, C_Z, dn_norm_w[l], name=f"hnorm_dn_fwd_{tag}")
        y_hg = _hnorm_fwd(o_hg, proj, C_HZ, hg_norm_w[l], name=f"hnorm_hg_fwd_{tag}")
        y = jnp.concatenate([y_dn, y_hg], axis=1)
        rest_all = _push_wait(pending["rest", l], [y], broadcast=True, name=f"gather_rest_{tag}_wait")
        wo = rest_all[:, 0:rows_out].reshape(D_MODEL, D_MODEL)
        wg = rest_all[:, g_off:g_off + rows_out].reshape(D_MODEL, D_MODEL)
        wu = rest_all[:, u_off:u_off + up_rows].reshape(N_DEV, PLE_DIM, D_MODEL // N_DEV).transpose(1, 0, 2).reshape(PLE_DIM, D_MODEL)
        weights.append((wi, wo, wg, wu))
        h1 = _mm(y, wo, mode="nn", out_dtype=F32, res=h, name=f"mm_out_{tag}")
        gp = _mm(h1, wg, mode="nn", out_dtype=F32, name=f"mm_gate_{tag}")
        up = _mm(p[l, 0], wu, mode="nn", out_dtype=F32, name=f"mm_up_{tag}")
        h2 = _ple_fwd(h1, gp, up, name=f"ple_fwd_{tag}")
        saved.append(dict(h=h, hn=hn, proj=proj, qkv=qkv, beta=beta, gcs=gcs, st_dn=st_dn, qh=qh, kh=kh, lf=lf,
                          st_hg=st_hg, o_dn=o_dn, o_hg=o_hg, y=y, h1=h1, gp=gp, up=up, al=al, dt=dt, lb=lb))
        h = h2

    loss_row, dh, d_final_w = _final_fwd_bwd(h, final_norm_w, tgt, name="final_norm_loss")

    d_norm_w, d_alog, d_dt, d_dn_nw, d_hg_nw, d_lb, d_conv = ([None] * depth for _ in range(7))
    sent = {}
    for l in reversed(range(depth)):
        wi, wo, wg, wu = weights[l]
        sv = saved[l]
        tag = f"l{l}"
        dup, dgp = _ple_bwd(dh, sv["gp"], sv["up"], name=f"ple_bwd_{tag}")
        pin = [sent["win", l + 1][4]] if l + 1 < depth else []
        d_wu = _mm(p[l, 0], dup, mode="tn", out_dtype=BF16, after=pin, name=f"mm_dwup_{tag}")
        d_wg = _mm(sv["h1"], dgp, mode="tn", out_dtype=BF16, name=f"mm_dwgate_{tag}")
        dh1 = _mm(dgp, wg, mode="nt", out_dtype=F32, res=dh, name=f"mm_dh1_{tag}")
        d_wo = _mm(sv["y"], dh1, mode="tn", out_dtype=BF16, name=f"mm_dwout_{tag}")
        parts_rest = jnp.concatenate(
            [d_wo.reshape(N_DEV, rows_out, D_MODEL), d_wg.reshape(N_DEV, rows_out, D_MODEL),
             d_wu.reshape(PLE_DIM, N_DEV, D_MODEL // N_DEV).transpose(1, 0, 2).reshape(N_DEV, up_rows, D_MODEL)], axis=1)
        sent["rest", l] = _push_start(parts_rest, own_slot(parts_rest[my]), broadcast=False, name=f"exchange_rest_{tag}_start")
        dy = _mm(dh1, wo, mode="nt", out_dtype=F32, name=f"mm_dy_{tag}")
        dn_nw = dn_norm_w[l] + sent["rest", l][4][0, 0]
        do_dn, dz_dn, d_dn_nw[l] = _hnorm_bwd(sv["o_dn"], sv["proj"], C_Z, dn_nw, dy, 0, name=f"hnorm_dn_bwd_{tag}")
        do_hg, dz_hg, d_hg_nw[l] = _hnorm_bwd(sv["o_hg"], sv["proj"], C_HZ, hg_norm_w[l], dy, BR_WIDTH, name=f"hnorm_hg_bwd_{tag}")
        dqkv, d_gc, dbeta = _dn_chunk_bwd(sv["qkv"], sv["gcs"], sv["beta"], sv["st_dn"], do_dn, name=f"dn_chunk_bwd_{tag}")
        dqkv_pre, d_conv[l] = _dn_qkv_bwd(sv["proj"], conv_full[l], dqkv, name=f"dn_qkv_bwd_{tag}")
        db, da, d_alog[l], d_dt[l] = _dn_gate_bwd(sv["proj"], sv["al"], sv["dt"], dbeta, d_gc, name=f"dn_gate_bwd_{tag}")
        dqh, dkh, dhi, dlf = _hg_chunk_bwd(sv["qh"], sv["kh"], sv["proj"], sv["lf"], sv["st_hg"], do_hg, name=f"hg_chunk_bwd_{tag}")
        dhq, dhf, d_lb[l] = _hg_prep_bwd(sv["proj"], sv["lb"], dqh, dkh, dlf, name=f"hg_prep_bwd_{tag}")
        dproj = jnp.concatenate([dqkv_pre, dz_dn, dhq, dhf, dhi, dz_hg, db, da], axis=1)
        dhn = _mm(dproj, wi, mode="nt", out_dtype=F32, name=f"mm_dhn_{tag}")
        dh, d_norm_w[l] = _rms_bwd(sv["h"], norm_w[l], dhn, dh1, name=f"rms_bwd_{tag}")
        before_dwin = [dh]
        if l == 0:
            small = _pack([loss_row, jnp.concatenate(d_norm_w, axis=0), d_final_w,
                           jnp.stack([a[0, :N_HEADS] for a in d_alog]), jnp.stack([a[0, :N_HEADS] for a in d_dt]),
                           jnp.concatenate(d_dn_nw, axis=0), jnp.concatenate(d_hg_nw, axis=0), jnp.concatenate(d_lb, axis=0),
                           jnp.stack(d_conv)])
            small_all = _all_gather(small, name="gather_small")
            before_dwin = [small_all]
        d_win = _mm(sv["hn"], dproj, mode="tn", out_dtype=BF16, after=before_dwin, name=f"mm_dwin_{tag}")
        parts_in = jnp.stack(_win_to_shards(d_win))
        sent["win", l] = _push_start(parts_in, own_slot(parts_in[my]), broadcast=False, name=f"exchange_w_in_{tag}_start")
    grad_x = dh[None]

    small_shapes = [(1, 128), norm_w.shape, final_norm_w.shape, dn_A_log.shape, dn_dt_bias.shape, dn_norm_w.shape,
                    hg_norm_w.shape, hg_lb_logits.shape, (depth, CONV_W, 3 * BR_WIDTH)]
    tot = _unpack(_sum_parts(small_all, after=[sent["win", 0][4]], name="sum_small"), small_shapes)
    loss = tot[0][0, 0]
    g_lb = tot[7]
    g_logits = jax.vjp(_lower_bounds, hg_lb_logits)[1](g_lb)[0]
    g_conv = lax.dynamic_slice_in_dim(tot[8], my * (3 * BR_WIDTH // N_DEV), 3 * BR_WIDTH // N_DEV, axis=2)
    small_g = [tot[1], g_conv, tot[3], tot[4], tot[5], g_logits, tot[6], tot[2]]
    small_w = [norm_w, dn_conv_w, dn_A_log, dn_dt_bias, dn_norm_w, hg_lb_logits, hg_norm_w, final_norm_w]
    small_m = [m_norm_w, m_dn_conv_w, m_dn_A_log, m_dn_dt_bias, m_dn_norm_w, m_hg_lb_logits, m_hg_norm_w, m_final_norm_w]
    small_v = [v_norm_w, v_dn_conv_w, v_dn_A_log, v_dn_dt_bias, v_dn_norm_w, v_hg_lb_logits, v_hg_norm_w, v_final_norm_w]
    pk_w = _pack(small_w)
    res_small = _adamw(_pack(small_g)[None], 0, pk_w, _pack(small_m), _pack(small_v), name="adamw_small", tr=pk_w.shape[0])
    shapes_w = [a.shape for a in small_w]
    sg, sd, sm, sv_ = (_unpack(r, shapes_w) for r in res_small)

    r_win = r_wo = r_wg = r_wu = None
    done = [grad_x, res_small[0]]

    def flat(a, cols):
        return a.reshape(-1, cols)

    for l in reversed(range(depth)):
        tag = f"l{l}"
        land_rest = _push_wait(sent["rest", l], done, broadcast=False, name=f"exchange_rest_{tag}_wait")
        r_wo = _adamw(land_rest, 0, flat(w_out, D_MODEL), flat(m_w_out, D_MODEL), flat(v_w_out, D_MODEL), layer=l,
                      n_layers=depth, prev=r_wo, name=f"adamw_w_out_{tag}", tr=rows_out)
        r_wg = _adamw(land_rest, g_off, flat(w_ple_gate, D_MODEL), flat(m_w_ple_gate, D_MODEL), flat(v_w_ple_gate, D_MODEL),
                      layer=l, n_layers=depth, prev=r_wg, name=f"adamw_w_gate_{tag}", tr=rows_out)
        r_wu = _adamw(land_rest, u_off, flat(w_ple_up, D_MODEL), flat(m_w_ple_up, D_MODEL), flat(v_w_ple_up, D_MODEL),
                      layer=l, n_layers=depth, prev=r_wu, name=f"adamw_w_up_{tag}", tr=up_rows)
        done = [r_wo[0], r_wg[0], r_wu[0]]
    for l in reversed(range(depth)):
        tag = f"l{l}"
        land_in = _push_wait(sent["win", l], done, broadcast=False, name=f"exchange_w_in_{tag}_wait")
        r_win = _adamw(land_in, 0, flat(w_in, SHARD_IN), flat(m_w_in, SHARD_IN), flat(v_w_in, SHARD_IN), layer=l,
                       n_layers=depth, prev=r_win, name=f"adamw_w_in_{tag}", tr=256)
        done = [r_win[0]]
    r_win = [o.reshape(w_in.shape) for o in r_win]
    r_wo = [o.reshape(w_out.shape) for o in r_wo]
    r_wg = [o.reshape(w_ple_gate.shape) for o in r_wg]
    r_wu = [o.reshape(w_ple_up.shape) for o in r_wu]

    def order(small_list, big_in, big_out, big_up, big_gate):
        nw, cw, al_, dt_, dnw, lbl, hnw, fw = small_list
        return [nw, big_in, cw, al_, dt_, dnw, lbl, hnw, big_out, big_up, big_gate, fw]

    outs = [loss, grad_x]
    for i, sl in enumerate((sg, sd, sm, sv_)):
        outs += order(sl, r_win[i], r_wo[i], r_wu[i], r_wg[i])
    return tuple(outs)
```

```python
import functools

import jax
import jax.numpy as jnp
from jax import lax
from jax.experimental import pallas as pl
from jax.experimental.pallas import tpu as pltpu

F32 = jnp.float32
BF16 = jnp.bfloat16
HIGHEST = lax.Precision.HIGHEST

N_DEV = 8
D_MODEL = 2048
PLE_DIM = 256
HEAD_DIM = 128
N_HEADS = 8
BR_WIDTH = N_HEADS * HEAD_DIM
CHUNK = 64
SUB = 16
CONV_W = 4
NORM_EPS = 1e-6
L2_EPS = 1e-6
IN_WIDTH = 8208
SHARD_IN = IN_WIDTH // N_DEV
EXP_CLAMP = 80.0

C_QKV, C_Z, C_HQ, C_HF, C_HI, C_HZ, C_B, C_A, N_PROJ = 0, 3072, 4096, 5120, 6144, 7168, 8192, 8320, 8448

ADAM_LR, ADAM_B1, ADAM_B2, ADAM_EPS, ADAM_WD, ADAM_STEP = 0.001, 0.9, 0.999, 1e-08, 0.01, 10

VMEM_LIMIT = 48 * 1024 * 1024


def _cp(*sem):
    return pltpu.CompilerParams(dimension_semantics=sem, vmem_limit_bytes=VMEM_LIMIT)


class _Heads:
    def __init__(self, vals):
        self.v = tuple(vals)

    def __add__(self, o):
        return _hmap(lambda a, b: a + b, self, o)

    def __radd__(self, o):
        return _hmap(lambda a, b: b + a, self, o)

    def __sub__(self, o):
        return _hmap(lambda a, b: a - b, self, o)

    def __rsub__(self, o):
        return _hmap(lambda a, b: b - a, self, o)

    def __mul__(self, o):
        return _hmap(lambda a, b: a * b, self, o)

    def __rmul__(self, o):
        return _hmap(lambda a, b: b * a, self, o)

    def __neg__(self):
        return _hmap(lambda a: -a, self)

    def __getitem__(self, idx):
        return _hmap(lambda a: a[idx], self)


def _hmap(fn, *args):
    n = next((len(a.v) for a in args if isinstance(a, _Heads)), None)
    if n is None:
        return fn(*args)
    return _Heads(fn(*[a.v[i] if isinstance(a, _Heads) else a for a in args]) for i in range(n))


def _dot(a, b, ca, cb):
    return _hmap(lambda x, y: lax.dot_general(x.astype(BF16), y.astype(BF16), (((ca,), (cb,)), ((), ())),
                                              preferred_element_type=F32), a, b)


def _nn(a, b):
    return _dot(a, b, 1, 0)


def _nt(a, b):
    return _dot(a, b, 1, 1)


def _tn(a, b):
    return _dot(a, b, 0, 0)


def _split(a):
    hi = _hmap(lambda x: x.astype(BF16), a)
    return hi, _hmap(lambda x, h: (x - h.astype(F32)).astype(BF16), a, hi)


def _dot3(a, b, ca, cb):
    ah, al = _split(a)
    bh, bl = _split(b)
    return _dot(ah, bh, ca, cb) + (_dot(ah, bl, ca, cb) + _dot(al, bh, ca, cb))


def _nn_exact(a, b):
    return _hmap(lambda y: lax.dot_general(a, y, (((1,), (0,)), ((), ())), precision=HIGHEST,
                                           preferred_element_type=F32), b)


def _exp(x):
    return _hmap(jnp.exp, x)


def _sum(x, axis):
    return _hmap(lambda a: jnp.sum(a, axis=axis, keepdims=True), x)


def _stack_rows(parts):
    return _hmap(lambda *xs: jnp.concatenate(xs, axis=0), *parts)


def _sigmoid(x):
    return jax.nn.sigmoid(x)


def _silu(x):
    return x * _sigmoid(x)


def _dsilu(x):
    s = _sigmoid(x)
    return s * (1.0 + x * (1.0 - s))


def _softplus(x):
    return jnp.maximum(x, 0.0) + jnp.log(1.0 + jnp.exp(-jnp.abs(x)))


def _iota2(n, m, axis):
    return lax.broadcasted_iota(jnp.int32, (n, m), axis)


def _col2row(col, eye):
    return _hmap(lambda c: jnp.sum(eye * c, axis=0, keepdims=True), col)


def _row2col(row, eye):
    return _hmap(lambda r: jnp.sum(eye * r, axis=1, keepdims=True), row)


def _pick_lane(block, lane_idx):
    lane = _iota2(block.shape[0], block.shape[1], 1)
    return jnp.sum(jnp.where(lane == lane_idx, block, 0.0), axis=1, keepdims=True)


MM_TILE_M, MM_TILE_N, MM_TILE_K = 1024, 1408, 2048


def _tile(dim, cap):
    if dim <= cap:
        return dim
    t = cap - cap % 128
    while dim % t:
        t -= 128
    return t


def _mm(a, b, *, mode, out_dtype, res=None, after=(), name):
    if mode == "nn":
        (m, kd), (_, n) = a.shape, b.shape
    elif mode == "nt":
        (m, kd), (n, _) = a.shape, b.shape
    else:
        (kd, m), (_, n) = a.shape, b.shape
    tm, tn, tk = _tile(m, MM_TILE_M), _tile(n, MM_TILE_N), _tile(kd, MM_TILE_K)
    assert m % tm == 0 and n % tn == 0 and kd % tk == 0, (m, n, kd, tm, tn, tk)
    nk = kd // tk
    ca, cb = {"nn": (1, 0), "nt": (1, 1), "tn": (0, 0)}[mode]

    def body(*refs):
        a_ref, b_ref = refs[:2]
        r_ref = None if res is None else refs[2]
        o_ref, acc_ref = refs[-2:]
        k = pl.program_id(2)

        @pl.when(k == 0)
        def _():
            acc_ref[...] = jnp.zeros_like(acc_ref)

        acc_ref[...] += _dot(a_ref[...], b_ref[...], ca, cb)

        @pl.when(k == nk - 1)
        def _():
            out = acc_ref[...]
            if r_ref is not None:
                out = out + r_ref[...].astype(F32)
            o_ref[...] = out.astype(o_ref.dtype)

    a_spec = pl.BlockSpec((tk, tm), lambda i, j, k: (k, i)) if mode == "tn" else pl.BlockSpec((tm, tk), lambda i, j, k: (i, k))
    b_spec = pl.BlockSpec((tn, tk), lambda i, j, k: (j, k)) if mode == "nt" else pl.BlockSpec((tk, tn), lambda i, j, k: (k, j))
    o_spec = pl.BlockSpec((tm, tn), lambda i, j, k: (i, j))
    in_specs = [a_spec, b_spec] + ([o_spec] if res is not None else []) + [pl.BlockSpec(memory_space=pl.ANY)] * len(after)
    args = (a, b) + ((res,) if res is not None else ()) + tuple(after)
    return pl.pallas_call(
        body, name=name, grid=(m // tm, n // tn, nk), in_specs=in_specs, out_specs=o_spec,
        out_shape=jax.ShapeDtypeStruct((m, n), out_dtype),
        scratch_shapes=[pltpu.VMEM((tm, tn), F32)],
        compiler_params=_cp("parallel", "parallel", "arbitrary"),
    )(*args)


ROW_TILE = 256


def _rms_fwd(h, w, *, name):
    s, d = h.shape
    tr = min(ROW_TILE, s)

    def body(h_ref, w_ref, o_ref):
        x = h_ref[...]
        r = lax.rsqrt(jnp.mean(x * x, axis=-1, keepdims=True) + NORM_EPS)
        o_ref[...] = (x * r * w_ref[...]).astype(o_ref.dtype)

    return pl.pallas_call(
        body, name=name, grid=(s // tr,),
        in_specs=[pl.BlockSpec((tr, d), lambda i: (i, 0)), pl.BlockSpec((1, d), lambda i: (0, 0))],
        out_specs=pl.BlockSpec((tr, d), lambda i: (i, 0)),
        out_shape=jax.ShapeDtypeStruct((s, d), BF16), compiler_params=_cp("parallel"),
    )(h, w.reshape(1, d))


def _rms_bwd_math(x, w, dy):
    d = x.shape[-1]
    r = lax.rsqrt(jnp.mean(x * x, axis=-1, keepdims=True) + NORM_EPS)
    gw = dy * w
    dx = r * gw - x * ((r * r * r) * (jnp.sum(gw * x, axis=-1, keepdims=True) / d))
    return dx, dy * x * r


def _rms_bwd(h, w, dhn, res, *, name):
    s, d = h.shape
    tr = min(ROW_TILE, s)

    def body(h_ref, w_ref, g_ref, r_ref, dh_ref, dw_ref):
        @pl.when(pl.program_id(0) == 0)
        def _():
            dw_ref[...] = jnp.zeros_like(dw_ref)

        dx, dwt = _rms_bwd_math(h_ref[...], w_ref[...], g_ref[...])
        dh_ref[...] = r_ref[...] + dx
        dw_ref[...] += jnp.sum(dwt, axis=0, keepdims=True)

    row = pl.BlockSpec((tr, d), lambda i: (i, 0))
    vec = pl.BlockSpec((1, d), lambda i: (0, 0))
    return pl.pallas_call(
        body, name=name, grid=(s // tr,), in_specs=[row, vec, row, row], out_specs=[row, vec],
        out_shape=[jax.ShapeDtypeStruct((s, d), F32), jax.ShapeDtypeStruct((1, d), F32)],
        compiler_params=_cp("arbitrary"),
    )(h, w.reshape(1, d), dhn, res)


def _final_fwd_bwd(h, w, tgt, *, name):
    s, d = h.shape
    tr = min(ROW_TILE, s)

    def body(h_ref, w_ref, t_ref, loss_ref, dh_ref, dw_ref):
        @pl.when(pl.program_id(0) == 0)
        def _():
            loss_ref[...] = jnp.zeros_like(loss_ref)
            dw_ref[...] = jnp.zeros_like(dw_ref)

        x = h_ref[...]
        wv = w_ref[...]
        r = lax.rsqrt(jnp.mean(x * x, axis=-1, keepdims=True) + NORM_EPS)
        err = x * r * wv - t_ref[...]
        row_loss = jnp.mean(err * err, axis=-1, keepdims=True)
        loss_ref[...] += 0.5 * jnp.sum(row_loss, axis=0, keepdims=True)
        dx, dwt = _rms_bwd_math(x, wv, err / d)
        dh_ref[...] = dx
        dw_ref[...] += jnp.sum(dwt, axis=0, keepdims=True)

    row = pl.BlockSpec((tr, d), lambda i: (i, 0))
    vec = pl.BlockSpec((1, d), lambda i: (0, 0))
    return pl.pallas_call(
        body, name=name, grid=(s // tr,), in_specs=[row, vec, row],
        out_specs=[pl.BlockSpec((1, 128), lambda i: (0, 0)), row, vec],
        out_shape=[jax.ShapeDtypeStruct((1, 128), F32), jax.ShapeDtypeStruct((s, d), F32),
                   jax.ShapeDtypeStruct((1, d), F32)],
        compiler_params=_cp("arbitrary"),
    )(h, w.reshape(1, d), tgt)


def _ple_fwd(h1, gate_pre, up, *, name):
    s, d = h1.shape
    tr = min(ROW_TILE, s)

    def body(h_ref, g_ref, u_ref, o_ref):
        o_ref[...] = h_ref[...] + u_ref[...] * _sigmoid(g_ref[...])

    row = pl.BlockSpec((tr, d), lambda i: (i, 0))
    return pl.pallas_call(body, name=name, grid=(s // tr,), in_specs=[row, row, row], out_specs=row,
                          out_shape=jax.ShapeDtypeStruct((s, d), F32), compiler_params=_cp("parallel"))(h1, gate_pre, up)


def _ple_bwd(dh2, gate_pre, up, *, name):
    s, d = dh2.shape
    tr = min(ROW_TILE, s)

    def body(d_ref, g_ref, u_ref, dup_ref, dgp_ref):
        dh = d_ref[...]
        gate = _sigmoid(g_ref[...])
        dup_ref[...] = (dh * gate).astype(BF16)
        dgp_ref[...] = (dh * u_ref[...] * gate * (1.0 - gate)).astype(BF16)

    row = pl.BlockSpec((tr, d), lambda i: (i, 0))
    return pl.pallas_call(body, name=name, grid=(s // tr,), in_specs=[row, row, row], out_specs=[row, row],
                          out_shape=[jax.ShapeDtypeStruct((s, d), BF16)] * 2, compiler_params=_cp("parallel"))(dh2, gate_pre, up)


HN_TILE = 512


def _hnorm_fwd(o, proj, z_col, w, *, name):
    s = o.shape[0]
    tr = min(HN_TILE, s)

    def body(o_ref, z_ref, w_ref, y_ref):
        wv = w_ref[...]
        for h in range(N_HEADS):
            cols = slice(h * HEAD_DIM, (h + 1) * HEAD_DIM)
            x = o_ref[:, cols]
            r = lax.rsqrt(jnp.mean(x * x, axis=-1, keepdims=True) + NORM_EPS)
            y_ref[:, cols] = (x * r * wv * _silu(z_ref[:, cols])).astype(BF16)

    blk = pl.BlockSpec((tr, BR_WIDTH), lambda i: (i, 0))
    return pl.pallas_call(
        body, name=name, grid=(s // tr,),
        in_specs=[blk, pl.BlockSpec((tr, BR_WIDTH), lambda i: (i, z_col // BR_WIDTH)), pl.BlockSpec((1, HEAD_DIM), lambda i: (0, 0))],
        out_specs=blk, out_shape=jax.ShapeDtypeStruct((s, BR_WIDTH), BF16), compiler_params=_cp("parallel"),
    )(o, proj, w.reshape(1, HEAD_DIM))


def _hnorm_bwd(o, proj, z_col, w, dy, dy_col, *, name):
    s = o.shape[0]
    tr = min(HN_TILE, s)

    def body(o_ref, z_ref, w_ref, dy_ref, do_ref, dz_ref, dw_ref):
        @pl.when(pl.program_id(0) == 0)
        def _():
            dw_ref[...] = jnp.zeros_like(dw_ref)

        wv = w_ref[...]
        dw = jnp.zeros((1, HEAD_DIM), F32)
        for h in range(N_HEADS):
            cols = slice(h * HEAD_DIM, (h + 1) * HEAD_DIM)
            x, z, g = o_ref[:, cols], z_ref[:, cols], dy_ref[:, cols]
            r = lax.rsqrt(jnp.mean(x * x, axis=-1, keepdims=True) + NORM_EPS)
            on = x * r * wv
            don = g * _silu(z)
            dz_ref[:, cols] = (g * on * _dsilu(z)).astype(BF16)
            gw = don * wv
            do_ref[:, cols] = r * gw - x * ((r * r * r) * (jnp.sum(gw * x, axis=-1, keepdims=True) / HEAD_DIM))
            dw = dw + jnp.sum(don * x * r, axis=0, keepdims=True)
        dw_ref[...] += dw

    blk = pl.BlockSpec((tr, BR_WIDTH), lambda i: (i, 0))
    vec = pl.BlockSpec((1, HEAD_DIM), lambda i: (0, 0))
    return pl.pallas_call(
        body, name=name, grid=(s // tr,),
        in_specs=[blk, pl.BlockSpec((tr, BR_WIDTH), lambda i: (i, z_col // BR_WIDTH)), vec,
                  pl.BlockSpec((tr, BR_WIDTH), lambda i: (i, dy_col // BR_WIDTH))],
        out_specs=[blk, blk, vec],
        out_shape=[jax.ShapeDtypeStruct((s, BR_WIDTH), F32), jax.ShapeDtypeStruct((s, BR_WIDTH), BF16),
                   jax.ShapeDtypeStruct((1, HEAD_DIM), F32)],
        compiler_params=_cp("arbitrary"),
    )(o, proj, w.reshape(1, HEAD_DIM), dy)


def _conv_silu(x, w, s):
    row = _iota2(s, x.shape[1], 0)
    c = w[CONV_W - 1:CONV_W, :] * x
    for k in range(1, CONV_W):
        c = c + w[CONV_W - 1 - k:CONV_W - k, :] * jnp.where(row >= k, pltpu.roll(x, k, 0), 0.0)
    return c


def _dn_qkv_fwd(proj, conv_w, *, name):
    s = proj.shape[0]
    nb = 3 * N_HEADS

    def body(x_ref, w_ref, o_ref):
        j = pl.program_id(0)
        sv = _silu(_conv_silu(x_ref[...], w_ref[...], s))
        r = lax.rsqrt(jnp.sum(sv * sv, axis=-1, keepdims=True) + L2_EPS)
        scale = jnp.where(j < N_HEADS, HEAD_DIM ** -0.5, 1.0).astype(F32)
        o_ref[...] = jnp.where(j < 2 * N_HEADS, sv * r * scale, sv)

    return pl.pallas_call(
        body, name=name, grid=(nb,),
        in_specs=[pl.BlockSpec((s, HEAD_DIM), lambda j: (0, j)), pl.BlockSpec((CONV_W, HEAD_DIM), lambda j: (0, j))],
        out_specs=pl.BlockSpec((s, HEAD_DIM), lambda j: (0, j)),
        out_shape=jax.ShapeDtypeStruct((s, 3 * BR_WIDTH), F32), compiler_params=_cp("parallel"),
    )(proj, conv_w)


def _dn_qkv_bwd(proj, conv_w, dqkv, *, name):
    s = proj.shape[0]
    nb = 3 * N_HEADS

    def body(x_ref, w_ref, g_ref, dx_ref, dw_ref):
        j = pl.program_id(0)
        x, w, g = x_ref[...], w_ref[...], g_ref[...]
        c = _conv_silu(x, w, s)
        sv = _silu(c)
        r = lax.rsqrt(jnp.sum(sv * sv, axis=-1, keepdims=True) + L2_EPS)
        scale = jnp.where(j < N_HEADS, HEAD_DIM ** -0.5, 1.0).astype(F32)
        ds_n = scale * (r * g - sv * ((r * r * r) * jnp.sum(g * sv, axis=-1, keepdims=True)))
        dc = jnp.where(j < 2 * N_HEADS, ds_n, g) * _dsilu(c)
        row = _iota2(s, HEAD_DIM, 0)
        dx = w[CONV_W - 1:CONV_W, :] * dc
        dws = [jnp.sum(dc * x, axis=0, keepdims=True)]
        for k in range(1, CONV_W):
            dx = dx + w[CONV_W - 1 - k:CONV_W - k, :] * jnp.where(row < s - k, pltpu.roll(dc, s - k, 0), 0.0)
            dws.append(jnp.sum(dc * jnp.where(row >= k, pltpu.roll(x, k, 0), 0.0), axis=0, keepdims=True))
        dx_ref[...] = dx.astype(BF16)
        for k in range(CONV_W):
            dw_ref[CONV_W - 1 - k:CONV_W - k, :] = dws[k]

    blk = pl.BlockSpec((s, HEAD_DIM), lambda j: (0, j))
    wblk = pl.BlockSpec((CONV_W, HEAD_DIM), lambda j: (0, j))
    return pl.pallas_call(
        body, name=name, grid=(nb,), in_specs=[blk, wblk, blk], out_specs=[blk, wblk],
        out_shape=[jax.ShapeDtypeStruct((s, 3 * BR_WIDTH), BF16), jax.ShapeDtypeStruct((CONV_W, 3 * BR_WIDTH), F32)],
        compiler_params=_cp("parallel"),
    )(proj, conv_w, dqkv)


def _tri(n, kind):
    r, c = _iota2(n, n, 0), _iota2(n, n, 1)
    if kind == "lower":
        return (r >= c).astype(F32)
    if kind == "upper":
        return (r <= c).astype(F32)
    return (r == c).astype(F32)


GATE_TILE = 512


def _dn_gate_fwd(proj, a_log, dt_bias, *, name):
    s = proj.shape[0]
    tr = min(GATE_TILE, s)

    def body(b_ref, a_ref, al_ref, dt_ref, beta_ref, g_ref):
        beta_ref[...] = _sigmoid(b_ref[...])
        g = -jnp.exp(al_ref[...]) * _softplus(a_ref[...] + dt_ref[...])
        low = _tri(CHUNK, "lower")
        for c in range(tr // CHUNK):
            rows = slice(c * CHUNK, (c + 1) * CHUNK)
            g_ref[rows, :] = _nn_exact(low, g[rows, :])

    blk = lambda cb: pl.BlockSpec((tr, HEAD_DIM), lambda i: (i, cb))
    vec = pl.BlockSpec((1, HEAD_DIM), lambda i: (0, 0))
    out = pl.BlockSpec((tr, HEAD_DIM), lambda i: (i, 0))
    return pl.pallas_call(
        body, name=name, grid=(s // tr,), in_specs=[blk(C_B // HEAD_DIM), blk(C_A // HEAD_DIM), vec, vec],
        out_specs=[out, out], out_shape=[jax.ShapeDtypeStruct((s, HEAD_DIM), F32)] * 2, compiler_params=_cp("parallel"),
    )(proj, proj, a_log, dt_bias)


def _dn_gate_bwd(proj, a_log, dt_bias, dbeta, d_g, *, name):
    s = proj.shape[0]
    tr = min(GATE_TILE, s)

    def body(b_ref, a_ref, al_ref, dt_ref, dbeta_ref, dG_ref, db_ref, da_ref, dal_ref, ddt_ref):
        @pl.when(pl.program_id(0) == 0)
        def _():
            dal_ref[...] = jnp.zeros_like(dal_ref)
            ddt_ref[...] = jnp.zeros_like(ddt_ref)

        beta = _sigmoid(b_ref[...])
        db_ref[...] = (dbeta_ref[...] * beta * (1.0 - beta)).astype(BF16)
        pre = a_ref[...] + dt_ref[...]
        neg_ea = -jnp.exp(al_ref[...])
        up = _tri(CHUNK, "upper")
        d_g = dG_ref[...]
        dg = jnp.concatenate([_nn_exact(up, d_g[c * CHUNK:(c + 1) * CHUNK, :]) for c in range(tr // CHUNK)], axis=0)
        da = dg * neg_ea * _sigmoid(pre)
        da_ref[...] = da.astype(BF16)
        ddt_ref[...] += jnp.sum(da, axis=0, keepdims=True)
        dal_ref[...] += jnp.sum(dg * neg_ea * _softplus(pre), axis=0, keepdims=True)

    blk = lambda cb: pl.BlockSpec((tr, HEAD_DIM), lambda i: (i, cb))
    vec = pl.BlockSpec((1, HEAD_DIM), lambda i: (0, 0))
    io = pl.BlockSpec((tr, HEAD_DIM), lambda i: (i, 0))
    return pl.pallas_call(
        body, name=name, grid=(s // tr,),
        in_specs=[blk(C_B // HEAD_DIM), blk(C_A // HEAD_DIM), vec, vec, io, io], out_specs=[io, io, vec, vec],
        out_shape=[jax.ShapeDtypeStruct((s, HEAD_DIM), BF16)] * 2 + [jax.ShapeDtypeStruct((1, HEAD_DIM), F32)] * 2,
        compiler_params=_cp("arbitrary"),
    )(proj, proj, a_log, dt_bias, dbeta, d_g)


def _unit_lower_inverse(a_strict, eye):
    x = -a_strict
    t = x + eye
    p = x
    n = 2
    while n < CHUNK:
        p = _nn(p, p)
        t = t + _nn(t, p)
        n *= 2
    return t


def _dn_chunk_common(q, k, v, gc, beta, st):
    c = CHUNK
    eye = _tri(c, "eye")
    low = _tri(c, "lower")
    strict = low - eye
    grow = _col2row(gc, eye)
    dec = _hmap(lambda g_, gr: low * jnp.exp(low * (g_ - gr)), gc, grow)
    kb = k * beta
    a_mat = _nt(kb, k) * dec * strict
    t_inv = _unit_lower_inverse(a_mat, eye)
    e_g = _exp(gc)
    u = _nn(t_inv, v * beta)
    w = _nn(t_inv, kb * e_g)
    p_qk = _nt(q, k)
    qk = p_qk * dec
    qd = q * e_g
    last = (_iota2(c, 1, 0) == c - 1).astype(F32)
    g_last = _sum(gc * last, 0)
    e_t = _exp(g_last - gc)
    kt = k * e_t
    tail = _exp(g_last)
    vn = u - _nn(w, st)
    return dict(eye=eye, low=low, strict=strict, dec=dec, kb=kb, a_mat=a_mat, t_inv=t_inv, e_g=e_g, u=u, w=w,
                qk=qk, qd=qd, last=last, e_t=e_t, kt=kt, tail=tail, vn=vn)


def _dn_chunk_fwd_math(q, k, v, gc, beta, st):
    m = _dn_chunk_common(q, k, v, gc, beta, st)
    o = _nn(m["qd"], st) + _nn(m["qk"], m["vn"])
    st2 = st * m["tail"] + _tn(m["kt"], m["vn"])
    return o, st2


def _dn_chunk_bwd_math(q, k, v, gc, beta, st, do, dst2):
    m = _dn_chunk_common(q, k, v, gc, beta, st)
    eye, low, strict = m["eye"], m["low"], m["strict"]
    dvn = _tn(m["qk"], do) + _nn(m["kt"], dst2)
    dqk = _nt(do, m["vn"]) * low
    dqd = _nt(do, st)
    dst = _tn(m["qd"], do) + dst2 * m["tail"] - _tn(m["w"], dvn)
    dkt = _nt(m["vn"], dst2)
    dtail = _sum(_sum(st * dst2, 1), 0)
    dw = -_nt(dvn, st)
    dvb = _tn(m["t_inv"], dvn)
    dkg = _tn(m["t_inv"], dw)
    d_a = (_nt(dvb, m["u"]) + _nt(dkg, m["w"])) * (-strict)
    dkk = d_a * m["dec"]
    dp = dqk * m["dec"]
    dq = _nn(dp, k) + dqd * m["e_g"]
    dkb = _nn(dkk, k) + dkg * m["e_g"]
    dk = _tn(dp, q) + _tn(dkk, m["kb"]) + dkb * beta + dkt * m["e_t"]
    dv = dvb * beta
    dbeta = _sum(dvb * v + dkb * k, 1)
    de_g = _sum(dkg * m["kb"] + dqd * q, 1)
    de_t = _sum(dkt * k, 1)
    mm = d_a * m["a_mat"] + dqk * m["qk"]
    dgc = (_sum(mm, 1) - _row2col(_sum(mm, 0), eye) + de_g * m["e_g"] - de_t * m["e_t"]
           + (_sum(de_t * m["e_t"], 0) + dtail * m["tail"]) * m["last"])
    return dq, dk, dv, dgc, dbeta, dst


def _heads_of(ref):
    return _Heads(ref[:, h * HEAD_DIM:(h + 1) * HEAD_DIM] for h in range(N_HEADS))


def _lanes_of(block):
    return _Heads(_pick_lane(block, h) for h in range(N_HEADS))


def _dn_chunk_fwd(qkv, gcs, beta, *, name):
    s = qkv.shape[0]
    n = s // CHUNK

    def body(q_ref, k_ref, v_ref, g_ref, b_ref, o_ref, st_out_ref, st_ref):
        @pl.when(pl.program_id(0) == 0)
        def _():
            st_ref[...] = jnp.zeros_like(st_ref)

        gblk, bblk = g_ref[...], b_ref[...]
        st = _Heads(st_ref[h] for h in range(N_HEADS))
        o, st2 = _dn_chunk_fwd_math(_heads_of(q_ref), _heads_of(k_ref), _heads_of(v_ref), _lanes_of(gblk),
                                    _lanes_of(bblk), st)
        for h in range(N_HEADS):
            st_out_ref[0, h] = st.v[h]
            o_ref[:, h * HEAD_DIM:(h + 1) * HEAD_DIM] = o.v[h]
            st_ref[h] = st2.v[h]

    blk = lambda off: pl.BlockSpec((CHUNK, BR_WIDTH), lambda c: (c, off))
    sc = pl.BlockSpec((CHUNK, HEAD_DIM), lambda c: (c, 0))
    return pl.pallas_call(
        body, name=name, grid=(n,),
        in_specs=[blk(0), blk(1), blk(2), sc, sc],
        out_specs=[blk(0), pl.BlockSpec((1, N_HEADS, HEAD_DIM, HEAD_DIM), lambda c: (c, 0, 0, 0))],
        out_shape=[jax.ShapeDtypeStruct((s, BR_WIDTH), F32), jax.ShapeDtypeStruct((n, N_HEADS, HEAD_DIM, HEAD_DIM), F32)],
        scratch_shapes=[pltpu.VMEM((N_HEADS, HEAD_DIM, HEAD_DIM), F32)],
        compiler_params=_cp("arbitrary"),
    )(qkv, qkv, qkv, gcs, beta)


def _dn_chunk_bwd(qkv, gcs, beta, states, do, *, name):
    s = qkv.shape[0]
    n = s // CHUNK

    def body(q_ref, k_ref, v_ref, g_ref, b_ref, st_in_ref, do_ref, dqkv_ref, dg_ref, dbeta_ref, dst_ref):
        @pl.when(pl.program_id(0) == 0)
        def _():
            dst_ref[...] = jnp.zeros_like(dst_ref)

        gblk, bblk = g_ref[...], b_ref[...]
        lane = _iota2(CHUNK, HEAD_DIM, 1)
        dg_all = jnp.zeros((CHUNK, HEAD_DIM), F32)
        dbeta_all = jnp.zeros((CHUNK, HEAD_DIM), F32)
        dq, dk, dv, dgc, dbeta, dst = _dn_chunk_bwd_math(
            _heads_of(q_ref), _heads_of(k_ref), _heads_of(v_ref), _lanes_of(gblk), _lanes_of(bblk),
            _Heads(st_in_ref[0, h] for h in range(N_HEADS)), _heads_of(do_ref),
            _Heads(dst_ref[h] for h in range(N_HEADS)))
        for h in range(N_HEADS):
            for part, val in enumerate((dq, dk, dv)):
                c0 = part * BR_WIDTH + h * HEAD_DIM
                dqkv_ref[:, c0:c0 + HEAD_DIM] = val.v[h]
            dg_all = jnp.where(lane == h, dgc.v[h], dg_all)
            dbeta_all = jnp.where(lane == h, dbeta.v[h], dbeta_all)
            dst_ref[h] = dst.v[h]
        dg_ref[...] = dg_all
        dbeta_ref[...] = dbeta_all

    blk = lambda off: pl.BlockSpec((CHUNK, BR_WIDTH), lambda c: (n - 1 - c, off))
    sc = pl.BlockSpec((CHUNK, HEAD_DIM), lambda c: (n - 1 - c, 0))
    outs = pl.pallas_call(
        body, name=name, grid=(n,),
        in_specs=[blk(0), blk(1), blk(2), sc, sc,
                  pl.BlockSpec((1, N_HEADS, HEAD_DIM, HEAD_DIM), lambda c: (n - 1 - c, 0, 0, 0)), blk(0)],
        out_specs=[pl.BlockSpec((CHUNK, 3 * BR_WIDTH), lambda c: (n - 1 - c, 0)), sc, sc],
        out_shape=[jax.ShapeDtypeStruct((s, 3 * BR_WIDTH), F32)] + [jax.ShapeDtypeStruct((s, HEAD_DIM), F32)] * 2,
        scratch_shapes=[pltpu.VMEM((N_HEADS, HEAD_DIM, HEAD_DIM), F32)],
        compiler_params=_cp("arbitrary"),
    )(qkv, qkv, qkv, gcs, beta, states, do)
    return outs


def _hg_prep_fwd(proj, lb, *, name):
    s = proj.shape[0]
    tr = min(ROW_TILE, s)

    def body(q_ref, f_ref, lb_ref, qo_ref, ko_ref, lf_ref):
        f, lbv = f_ref[...], lb_ref[...]
        qo_ref[...] = _silu(q_ref[...])
        ko_ref[...] = (1.0 - lbv) * _sigmoid(-f)
        lf_ref[...] = jnp.log(lbv + (1.0 - lbv) * _sigmoid(f))

    blk = lambda cb: pl.BlockSpec((tr, BR_WIDTH), lambda i: (i, cb))
    out = pl.BlockSpec((tr, BR_WIDTH), lambda i: (i, 0))
    return pl.pallas_call(
        body, name=name, grid=(s // tr,),
        in_specs=[blk(C_HQ // BR_WIDTH), blk(C_HF // BR_WIDTH), pl.BlockSpec((1, BR_WIDTH), lambda i: (0, 0))],
        out_specs=[out, out, out], out_shape=[jax.ShapeDtypeStruct((s, BR_WIDTH), F32)] * 3, compiler_params=_cp("parallel"),
    )(proj, proj, lb)


def _hg_prep_bwd(proj, lb, dq, dk, dlf, *, name):
    s = proj.shape[0]
    tr = min(ROW_TILE, s)

    def body(q_ref, f_ref, lb_ref, dq_ref, dk_ref, dlf_ref, dhq_ref, dhf_ref, dlb_ref):
        @pl.when(pl.program_id(0) == 0)
        def _():
            dlb_ref[...] = jnp.zeros_like(dlb_ref)

        f, lbv = f_ref[...], lb_ref[...]
        dhq_ref[...] = (dq_ref[...] * _dsilu(q_ref[...])).astype(BF16)
        sp, sn = _sigmoid(f), _sigmoid(-f)
        inner = lbv + (1.0 - lbv) * sp
        dlf_over = dlf_ref[...] / inner
        dkv = dk_ref[...]
        dhf_ref[...] = (dlf_over * (1.0 - lbv) * sp * sn - dkv * (1.0 - lbv) * sn * (1.0 - sn)).astype(BF16)
        dlb_ref[...] += jnp.sum(dlf_over * (1.0 - sp) - dkv * sn, axis=0, keepdims=True)

    blk = lambda cb: pl.BlockSpec((tr, BR_WIDTH), lambda i: (i, cb))
    io = pl.BlockSpec((tr, BR_WIDTH), lambda i: (i, 0))
    vec = pl.BlockSpec((1, BR_WIDTH), lambda i: (0, 0))
    return pl.pallas_call(
        body, name=name, grid=(s // tr,),
        in_specs=[blk(C_HQ // BR_WIDTH), blk(C_HF // BR_WIDTH), vec, io, io, io], out_specs=[io, io, vec],
        out_shape=[jax.ShapeDtypeStruct((s, BR_WIDTH), BF16)] * 2 + [jax.ShapeDtypeStruct((1, BR_WIDTH), F32)],
        compiler_params=_cp("arbitrary"),
    )(proj, proj, lb, dq, dk, dlf)


def _hg_chunk_common(q, k, lf):
    c = CHUNK
    g = _nn_exact(_tri(c, "lower"), lf)
    e_g = _exp(g)
    qd = q * e_g
    g_last = g[c - 1:c, :]
    e_t = _exp(g_last - g)
    kt = k * e_t
    tail = _exp(g_last)
    q_sc, k_sc, e_q, e_k = [], [], [], []
    for i in range(c // SUB):
        g_ref = g[i * SUB:i * SUB + 1, :]
        eq = _exp(g[i * SUB:(i + 1) * SUB, :] - g_ref)
        ek = _hmap(lambda gr, g_: jnp.exp(jnp.minimum(gr - g_, EXP_CLAMP)), g_ref, g)
        e_q.append(eq)
        e_k.append(ek)
        q_sc.append(q[i * SUB:(i + 1) * SUB, :] * eq)
        k_sc.append(k * ek)
    a_mat = _stack_rows([_nt(qi, ki) for qi, ki in zip(q_sc, k_sc)]) * _tri(c, "lower")
    return dict(e_g=e_g, qd=qd, e_t=e_t, kt=kt, tail=tail, q_sc=q_sc, k_sc=k_sc, e_q=e_q, e_k=e_k, a_mat=a_mat)


def _hg_chunk_fwd_math(q, k, v, lf, stt):
    m = _hg_chunk_common(q, k, lf)
    o = _nt(m["qd"], stt) + _nn(m["a_mat"], v)
    stt2 = stt * m["tail"] + _tn(v, m["kt"])
    return o, stt2


def _hg_chunk_bwd_math(q, k, v, lf, stt, do, dstt2):
    c = CHUNK
    m = _hg_chunk_common(q, k, lf)
    stt2 = stt * m["tail"] + _tn(v, m["kt"])
    later = _sum(stt2 * dstt2, 0)
    dqd = _dot3(do, stt, 1, 0)
    dstt = _tn(do, m["qd"]) + dstt2 * m["tail"]
    d_a = _dot3(do, v, 1, 1) * _tri(c, "lower")
    dv = _tn(m["a_mat"], do) + _nt(m["kt"], dstt2)
    dkt = _dot3(v, dstt2, 1, 0)
    dq_parts = []
    dk = dkt * m["e_t"]
    for i in range(c // SUB):
        d_ai = d_a[i * SUB:(i + 1) * SUB, :]
        dq_parts.append(_dot3(d_ai, m["k_sc"][i], 1, 0) * m["e_q"][i])
        dk = dk + _dot3(d_ai, m["q_sc"][i], 0, 0) * m["e_k"][i]
    dq = dqd * m["e_g"] + _stack_rows(dq_parts)
    db = q * dq - k * dk
    dlf = _nn_exact(_tri(c, "upper"), db) + later
    return dq, dk, dv, dlf, dstt


def _hg_chunk_fwd(qh, kh, proj, lf, *, name):
    s = qh.shape[0]
    n = s // CHUNK
    vb = C_HI // BR_WIDTH

    def body(q_ref, k_ref, v_ref, lf_ref, o_ref, st_out_ref, st_ref):
        @pl.when(pl.program_id(0) == 0)
        def _():
            st_ref[...] = jnp.zeros_like(st_ref)

        st = _Heads(st_ref[h] for h in range(N_HEADS))
        o, st2 = _hg_chunk_fwd_math(_heads_of(q_ref), _heads_of(k_ref), _heads_of(v_ref), _heads_of(lf_ref), st)
        for h in range(N_HEADS):
            st_out_ref[0, h] = st.v[h]
            o_ref[:, h * HEAD_DIM:(h + 1) * HEAD_DIM] = o.v[h]
            st_ref[h] = st2.v[h]

    blk = lambda off: pl.BlockSpec((CHUNK, BR_WIDTH), lambda c: (c, off))
    return pl.pallas_call(
        body, name=name, grid=(n,), in_specs=[blk(0), blk(0), blk(vb), blk(0)],
        out_specs=[blk(0), pl.BlockSpec((1, N_HEADS, HEAD_DIM, HEAD_DIM), lambda c: (c, 0, 0, 0))],
        out_shape=[jax.ShapeDtypeStruct((s, BR_WIDTH), F32), jax.ShapeDtypeStruct((n, N_HEADS, HEAD_DIM, HEAD_DIM), F32)],
        scratch_shapes=[pltpu.VMEM((N_HEADS, HEAD_DIM, HEAD_DIM), F32)],
        compiler_params=_cp("arbitrary"),
    )(qh, kh, proj, lf)


def _hg_chunk_bwd(qh, kh, proj, lf, states, do, *, name):
    s = qh.shape[0]
    n = s // CHUNK
    vb = C_HI // BR_WIDTH

    def body(q_ref, k_ref, v_ref, lf_ref, st_in_ref, do_ref, dq_ref, dk_ref, dv_ref, dlf_ref, dst_ref):
        @pl.when(pl.program_id(0) == 0)
        def _():
            dst_ref[...] = jnp.zeros_like(dst_ref)

        dq, dk, dv, dlf, dst = _hg_chunk_bwd_math(
            _heads_of(q_ref), _heads_of(k_ref), _heads_of(v_ref), _heads_of(lf_ref),
            _Heads(st_in_ref[0, h] for h in range(N_HEADS)), _heads_of(do_ref),
            _Heads(dst_ref[h] for h in range(N_HEADS)))
        for h in range(N_HEADS):
            cols = slice(h * HEAD_DIM, (h + 1) * HEAD_DIM)
            dq_ref[:, cols] = dq.v[h]
            dk_ref[:, cols] = dk.v[h]
            dv_ref[:, cols] = dv.v[h].astype(BF16)
            dlf_ref[:, cols] = dlf.v[h]
            dst_ref[h] = dst.v[h]

    blk = lambda off: pl.BlockSpec((CHUNK, BR_WIDTH), lambda c: (n - 1 - c, off))
    return pl.pallas_call(
        body, name=name, grid=(n,),
        in_specs=[blk(0), blk(0), blk(vb), blk(0),
                  pl.BlockSpec((1, N_HEADS, HEAD_DIM, HEAD_DIM), lambda c: (n - 1 - c, 0, 0, 0)), blk(0)],
        out_specs=[blk(0), blk(0), blk(0), blk(0)],
        out_shape=[jax.ShapeDtypeStruct((s, BR_WIDTH), F32)] * 2 + [jax.ShapeDtypeStruct((s, BR_WIDTH), BF16),
                                                                    jax.ShapeDtypeStruct((s, BR_WIDTH), F32)],
        scratch_shapes=[pltpu.VMEM((N_HEADS, HEAD_DIM, HEAD_DIM), F32)],
        compiler_params=_cp("arbitrary"),
    )(qh, kh, proj, lf, states, do)


_ANY = pl.BlockSpec(memory_space=pl.ANY)
_MESH = pl.DeviceIdType.MESH


def _all_gather(x_local, *, name, after=()):
    n_after = len(after)

    def body(x_ref, *refs):
        out_ref, send_sems, recv_sems, local_sem = refs[n_after:]
        x, y, c = lax.axis_index("x"), lax.axis_index("y"), lax.axis_index("c")
        me, sibling = (x, y, c), (x, y, 1 - c)
        chips = [(1 - x, y), (x, 1 - y), (1 - x, 1 - y)]

        def slot(px, py, pc):
            return out_ref.at[4 * px + 2 * py + pc]

        def copy(k, block, to, src=None):
            return pltpu.make_async_remote_copy(
                src_ref=slot(*block) if src is None else src, dst_ref=slot(*block),
                send_sem=send_sems.at[k], recv_sem=recv_sems.at[k], device_id=to, device_id_type=_MESH)

        mine = pltpu.make_async_copy(x_ref, slot(*me), local_sem)
        mine.start()
        first = [copy(0, me, sibling, src=x_ref)]
        first += [copy(1 + j, me, (*chip, c), src=x_ref) for j, chip in enumerate(chips)]
        for cp in first:
            cp.start()
        passed = [copy(4 + j, (*chip, c), sibling) for j, chip in enumerate(chips)]
        for j, chip in enumerate(chips):
            copy(1 + j, (*chip, c), me).wait_recv()
            passed[j].start()
        copy(0, sibling, me).wait_recv()
        for j, chip in enumerate(chips):
            copy(4 + j, (*chip, 1 - c), me).wait_recv()
        for cp in first + passed:
            cp.wait_send()
        mine.wait()

    return pl.pallas_call(
        body, name=name, out_shape=jax.ShapeDtypeStruct((N_DEV,) + x_local.shape, x_local.dtype),
        in_specs=[_ANY] * (1 + n_after), out_specs=_ANY,
        scratch_shapes=[pltpu.SemaphoreType.DMA((7,)), pltpu.SemaphoreType.DMA((7,)), pltpu.SemaphoreType.DMA],
    )(x_local, *after)


_HBM = pl.BlockSpec(memory_space=pltpu.HBM)
_SEM = pl.BlockSpec(memory_space=pltpu.SEMAPHORE)
_EFFECT = pltpu.SideEffectType.DATAFLOW_SIDE_EFFECTING


def _peers():
    x, y, c = lax.axis_index("x"), lax.axis_index("y"), lax.axis_index("c")
    out = []
    for k in range(1, N_DEV):
        px, py, pc = x ^ ((k >> 2) & 1), y ^ ((k >> 1) & 1), c ^ (k & 1)
        out.append(((px, py, pc), 4 * px + 2 * py + pc))
    return 4 * x + 2 * y + c, out


def _push_copies(src_ref, land_ref, send_sems, recv_sems, broadcast):
    my, peers = _peers()
    pairs = []
    for k, (pos, idx) in enumerate(peers):
        src = src_ref if broadcast else src_ref.at[idx]
        send = pltpu.make_async_remote_copy(src_ref=src, dst_ref=land_ref.at[my], send_sem=send_sems.at[k],
                                            recv_sem=recv_sems.at[k], device_id=pos, device_id_type=_MESH)
        recv = pltpu.make_async_remote_copy(src_ref=src, dst_ref=land_ref.at[idx], send_sem=send_sems.at[k],
                                            recv_sem=recv_sems.at[k], device_id=pos, device_id_type=_MESH)
        pairs.append((send, recv))
    return pairs


def _push_start(src, land, *, broadcast, name, after=()):
    n_after = len(after)

    def body(src_ref, land_ref, *refs):
        send_sems, recv_sems, _, _, token = refs[n_after:]
        for send, _ in _push_copies(src_ref, land_ref, send_sems, recv_sems, broadcast):
            send.start()
        token[...] = jnp.zeros_like(token)

    return pl.pallas_call(
        body, name=name,
        out_shape=(pltpu.SemaphoreType.DMA((N_DEV - 1,)), pltpu.SemaphoreType.DMA((N_DEV - 1,)),
                   pltpu.HBM(src.shape, src.dtype), pltpu.HBM(land.shape, land.dtype), jax.ShapeDtypeStruct((8, 128), F32)),
        in_specs=(_HBM, _HBM) + (_ANY,) * n_after, out_specs=(_SEM, _SEM, _HBM, _HBM, pl.BlockSpec(memory_space=pltpu.VMEM)),
        input_output_aliases={0: 2, 1: 3}, compiler_params=pltpu.CompilerParams(has_side_effects=_EFFECT),
    )(pltpu.with_memory_space_constraint(src, pltpu.HBM), pltpu.with_memory_space_constraint(land, pltpu.HBM), *after)


def _push_wait(handle, after, *, broadcast, name):
    send_sems, recv_sems, src_thru, land_thru, _ = handle

    def body(src_ref, land_ref, send_sems, recv_sems, *rest):
        for send, recv in _push_copies(src_ref, land_ref, send_sems, recv_sems, broadcast):
            send.wait_send()
            recv.wait_recv()

    return pl.pallas_call(
        body, name=name,
        out_shape=(pltpu.HBM(src_thru.shape, src_thru.dtype), pltpu.HBM(land_thru.shape, land_thru.dtype)),
        in_specs=(_HBM, _HBM, _SEM, _SEM) + (_ANY,) * len(after), out_specs=(_HBM, _HBM),
        input_output_aliases={0: 0, 1: 1}, compiler_params=pltpu.CompilerParams(has_side_effects=_EFFECT),
    )(src_thru, land_thru, send_sems, recv_sems, *after)[1]


def _relay_copies(src_ref, land_ref, sems_a, sems_b):
    x, y, c = lax.axis_index("x"), lax.axis_index("y"), lax.axis_index("c")
    slot = lambda px, py, pc: land_ref.at[4 * px + 2 * py + pc]
    chips = [(1 - x, y), (x, 1 - y), (1 - x, 1 - y)]
    (send_a, recv_a), (send_b, recv_b) = sems_a, sems_b

    def copy(sems, k, src, dst_slot, to):
        return pltpu.make_async_remote_copy(src_ref=src, dst_ref=dst_slot, send_sem=sems[0].at[k], recv_sem=sems[1].at[k],
                                            device_id=to, device_id_type=_MESH)

    first = [copy((send_a, recv_a), 0, src_ref, slot(x, y, c), (x, y, 1 - c))]
    first += [copy((send_a, recv_a), 1 + j, src_ref, slot(x, y, c), (*chip, c)) for j, chip in enumerate(chips)]
    first_in = [copy((send_a, recv_a), 0, src_ref, slot(x, y, 1 - c), (x, y, 1 - c))]
    first_in += [copy((send_a, recv_a), 1 + j, src_ref, slot(*chip, c), (*chip, c)) for j, chip in enumerate(chips)]
    relay = [copy((send_b, recv_b), j, slot(*chip, c), slot(*chip, c), (x, y, 1 - c)) for j, chip in enumerate(chips)]
    relay_in = [copy((send_b, recv_b), j, slot(*chip, 1 - c), slot(*chip, 1 - c), (x, y, 1 - c)) for j, chip in enumerate(chips)]
    return first, first_in, relay, relay_in


def _relay_start(src, land, *, name, after=()):
    n_after = len(after)

    def body(src_ref, land_ref, *refs):
        send_a, recv_a, _, _, token = refs[n_after:]
        for cp in _relay_copies(src_ref, land_ref, (send_a, recv_a), (send_a, recv_a))[0]:
            cp.start()
        token[...] = jnp.zeros_like(token)

    send_a, recv_a, src_thru, land_thru, token = pl.pallas_call(
        body, name=name,
        out_shape=(pltpu.SemaphoreType.DMA((4,)), pltpu.SemaphoreType.DMA((4,)), pltpu.HBM(src.shape, src.dtype),
                   pltpu.HBM(land.shape, land.dtype), jax.ShapeDtypeStruct((8, 128), F32)),
        in_specs=(_HBM, _HBM) + (_ANY,) * n_after, out_specs=(_SEM, _SEM, _HBM, _HBM, pl.BlockSpec(memory_space=pltpu.VMEM)),
        input_output_aliases={0: 2, 1: 3}, compiler_params=pltpu.CompilerParams(has_side_effects=_EFFECT),
    )(pltpu.with_memory_space_constraint(src, pltpu.HBM), pltpu.with_memory_space_constraint(land, pltpu.HBM), *after)
    return (send_a, recv_a), src_thru, land_thru, token


def _relay_mid(handle, after, *, name):
    sems_a, src_thru, land_thru, _ = handle
    n_after = len(after)

    def body(src_ref, land_ref, send_a, recv_a, *refs):
        send_b, recv_b, _, _, token = refs[n_after:]
        _, first_in, relay, _ = _relay_copies(src_ref, land_ref, (send_a, recv_a), (send_b, recv_b))
        for j in range(3):
            first_in[1 + j].wait_recv()
            relay[j].start()
        token[...] = jnp.zeros_like(token)

    send_b, recv_b, src2, land2, token = pl.pallas_call(
        body, name=name,
        out_shape=(pltpu.SemaphoreType.DMA((3,)), pltpu.SemaphoreType.DMA((3,)), pltpu.HBM(src_thru.shape, src_thru.dtype),
                   pltpu.HBM(land_thru.shape, land_thru.dtype), jax.ShapeDtypeStruct((8, 128), F32)),
        in_specs=(_HBM, _HBM, _SEM, _SEM) + (_ANY,) * n_after,
        out_specs=(_SEM, _SEM, _HBM, _HBM, pl.BlockSpec(memory_space=pltpu.VMEM)),
        input_output_aliases={0: 2, 1: 3}, compiler_params=pltpu.CompilerParams(has_side_effects=_EFFECT),
    )(src_thru, land_thru, *sems_a, *after)
    return sems_a, (send_b, recv_b), src2, land2, token


def _relay_wait(handle, after, *, name):
    sems_a, sems_b, src_thru, land_thru, _ = handle

    def body(src_ref, land_ref, send_a, recv_a, send_b, recv_b, *rest):
        first, first_in, relay, relay_in = _relay_copies(src_ref, land_ref, (send_a, recv_a), (send_b, recv_b))
        first_in[0].wait_recv()
        for cp in relay_in:
            cp.wait_recv()
        for cp in first + relay:
            cp.wait_send()

    return pl.pallas_call(
        body, name=name,
        out_shape=(pltpu.HBM(src_thru.shape, src_thru.dtype), pltpu.HBM(land_thru.shape, land_thru.dtype)),
        in_specs=(_HBM, _HBM, _SEM, _SEM, _SEM, _SEM) + (_ANY,) * len(after), out_specs=(_HBM, _HBM),
        input_output_aliases={0: 0, 1: 1}, compiler_params=pltpu.CompilerParams(has_side_effects=_EFFECT),
    )(src_thru, land_thru, *sems_a, *sems_b, *after)[1]


def _adamw(parts, row_off, w, m, v, *, layer=0, n_layers=1, prev=None, name, tr):
    rows, c = w.shape
    r = rows // n_layers
    np_ = parts.shape[0]
    tr = min(tr, r)
    assert r % tr == 0 and row_off % tr == 0
    ob, lb = row_off // tr, layer * (r // tr)
    c1 = 1.0 - ADAM_B1 ** ADAM_STEP
    c2 = 1.0 - ADAM_B2 ** ADAM_STEP
    n_prev = 0 if prev is None else 4

    def body(p_ref, w_ref, m_ref, v_ref, *refs):
        g_ref, d_ref, nm_ref, nv_ref = refs[n_prev:]
        g = p_ref[0].astype(F32)
        for s in range(1, np_):
            g = g + p_ref[s].astype(F32)
        wv = w_ref[...]
        m2 = ADAM_B1 * m_ref[...] + (1.0 - ADAM_B1) * g
        v2 = ADAM_B2 * v_ref[...] + (1.0 - ADAM_B2) * jnp.square(g)
        m_hat = m2 / c1
        v_hat = v2 / c2
        g_ref[...] = g
        d_ref[...] = -ADAM_LR * (m_hat / (jnp.sqrt(v_hat) + ADAM_EPS) + ADAM_WD * wv)
        nm_ref[...] = m2
        nv_ref[...] = v2

    blk = pl.BlockSpec((tr, c), lambda i: (lb + i, 0))
    return pl.pallas_call(
        body, name=name, grid=(r // tr,),
        in_specs=[pl.BlockSpec((np_, tr, c), lambda i: (0, ob + i, 0)), blk, blk, blk] + [_ANY] * n_prev,
        out_specs=[blk] * 4, out_shape=[jax.ShapeDtypeStruct((rows, c), F32)] * 4,
        input_output_aliases={4 + i: i for i in range(n_prev)}, compiler_params=_cp("parallel"),
    )(parts, w, m, v, *(prev or ()))


def _sum_parts(parts, *, name, after=()):
    np_, r, c = parts.shape

    def body(p_ref, *refs):
        o_ref = refs[-1]
        g = p_ref[0]
        for s in range(1, np_):
            g = g + p_ref[s]
        o_ref[...] = g

    vmem = pl.BlockSpec(memory_space=pltpu.VMEM)
    return pl.pallas_call(body, name=name, in_specs=[vmem] + [_ANY] * len(after), out_specs=vmem,
                          out_shape=jax.ShapeDtypeStruct((r, c), F32))(parts, *after)


def _pack(arrs):
    rows = []
    for a in arrs:
        f = a.reshape(-1).astype(F32)
        pad = (-f.shape[0]) % 128
        rows.append(jnp.pad(f, (0, pad)).reshape(-1, 128))
    out = jnp.concatenate(rows, axis=0)
    return jnp.pad(out, ((0, (-out.shape[0]) % 8), (0, 0)))


def _unpack(packed, shapes):
    outs, r0 = [], 0
    for shp in shapes:
        n = 1
        for d in shp:
            n *= d
        nr = -(-n // 128)
        outs.append(packed[r0:r0 + nr].reshape(-1)[:n].reshape(shp))
        r0 += nr
    return outs


_WIN_PIECES = ((0, 4096, 0), (4112, 8208, 0), (4096, 4104, HEAD_DIM - N_HEADS), (4104, 4112, HEAD_DIM - N_HEADS))


def _win_from_shards(shards):
    cols = []
    for lo, hi, pad in _WIN_PIECES:
        for j in range(N_DEV):
            a, b = max(lo, j * SHARD_IN), min(hi, (j + 1) * SHARD_IN)
            if a < b:
                cols.append(shards[j][:, a - j * SHARD_IN:b - j * SHARD_IN])
        if pad:
            cols.append(jnp.zeros((shards[0].shape[0], pad), shards[0].dtype))
    return jnp.concatenate(cols, axis=1)


def _win_to_shards(g):
    starts, off = [], 0
    for lo, hi, pad in _WIN_PIECES:
        starts.append((lo, hi, off))
        off += hi - lo + pad
    shards = []
    for j in range(N_DEV):
        cols = []
        for lo, hi, off in sorted(starts):
            a, b = max(lo, j * SHARD_IN), min(hi, (j + 1) * SHARD_IN)
            if a < b:
                cols.append(g[:, off + a - lo:off + b - lo])
        shards.append(jnp.concatenate(cols, axis=1))
    return shards


def _lower_bounds(logits):
    probs = jax.nn.softmax(logits.astype(F32), axis=0)
    return jnp.cumsum(probs, axis=0) - probs[0]


def _pad_lanes(vec8):
    return jnp.pad(vec8.reshape(1, N_HEADS), ((0, 0), (0, HEAD_DIM - N_HEADS)))


def kernel(x, p, norm_w, w_in, dn_conv_w, dn_A_log, dn_dt_bias, dn_norm_w, hg_lb_logits, hg_norm_w, w_out, w_ple_up, w_ple_gate, final_norm_w, loss_target, m_norm_w, m_w_in, m_dn_conv_w, m_dn_A_log, m_dn_dt_bias, m_dn_norm_w, m_hg_lb_logits, m_hg_norm_w, m_w_out, m_w_ple_up, m_w_ple_gate, m_final_norm_w, v_norm_w, v_w_in, v_dn_conv_w, v_dn_A_log, v_dn_dt_bias, v_dn_norm_w, v_hg_lb_logits, v_hg_norm_w, v_w_out, v_w_ple_up, v_w_ple_gate, v_final_norm_w):
    depth = norm_w.shape[0]
    my = 4 * lax.axis_index("x") + 2 * lax.axis_index("y") + lax.axis_index("c")
    h = x[0]
    tgt = loss_target[0]
    rows_out = D_MODEL // N_DEV
    up_rows = PLE_DIM * (D_MODEL // N_DEV) // D_MODEL
    g_off, u_off = rows_out, 2 * rows_out

    def own_slot(block):
        return lax.dynamic_update_index_in_dim(lax.empty((N_DEV,) + block.shape, block.dtype), block, my, 0)

    win_bf = w_in.astype(BF16)
    rest_bf = [jnp.concatenate([w_out[l], w_ple_gate[l], w_ple_up[l].reshape(up_rows, D_MODEL)], axis=0).astype(BF16)
               for l in range(depth)]
    conv_all = _all_gather(dn_conv_w, name="gather_conv_w")
    conv_full = conv_all.transpose(1, 2, 0, 3).reshape(depth, CONV_W, 3 * BR_WIDTH)
    win_all = {0: _all_gather(win_bf[0], name="gather_w_in_l0", after=[conv_all])}
    pending, relayed = {}, {}
    last = win_all[0]
    for l in range(depth):
        if l > 0:
            relayed["win", l] = _relay_start(win_bf[l], own_slot(win_bf[l]), after=[last], name=f"gather_w_in_l{l}_first")
            last = relayed["win", l][3]
        if l == 0:
            relayed["rest", l] = _relay_start(rest_bf[l], own_slot(rest_bf[l]), after=[last], name=f"gather_rest_l{l}_first")
            last = relayed["rest", l][3]
        else:
            pending["rest", l] = _push_start(rest_bf[l], own_slot(rest_bf[l]), broadcast=True, after=[last],
                                             name=f"gather_rest_l{l}_start")
            last = pending["rest", l][4]
    order_tok = last[0, 0]
    lbs = _lower_bounds(hg_lb_logits)

    saved = []
    weights = []
    for l in range(depth):
        tag = f"l{l}"
        if l > 0:
            win_all[l] = _relay_wait(relayed["win", l], [h], name=f"gather_w_in_{tag}_wait")
        wi = _win_from_shards([win_all[l][j] for j in range(N_DEV)])
        nw = norm_w[l] + order_tok if l == 0 else norm_w[l]
        hn = _rms_fwd(h, nw, name=f"rms_fwd_{tag}")
        proj = _mm(hn, wi, mode="nn", out_dtype=F32, name=f"mm_proj_{tag}")
        al, dt = _pad_lanes(dn_A_log[l]), _pad_lanes(dn_dt_bias[l])
        qkv = _dn_qkv_fwd(proj, conv_full[l], name=f"dn_qkv_fwd_{tag}")
        if ("rest", l) in relayed:
            relayed["rest", l] = _relay_mid(relayed["rest", l], [qkv], name=f"gather_rest_{tag}_relay")
            al = al + relayed["rest", l][4][0, 0]
        beta, gcs = _dn_gate_fwd(proj, al, dt, name=f"dn_gate_fwd_{tag}")
        o_dn, st_dn = _dn_chunk_fwd(qkv, gcs, beta, name=f"dn_chunk_fwd_{tag}")
        lb = lbs[l].reshape(1, BR_WIDTH)
        qh, kh, lf = _hg_prep_fwd(proj, lb, name=f"hg_prep_fwd_{tag}")
        o_hg, st_hg = _hg_chunk_fwd(qh, kh, proj, lf, name=f"hg_chunk_fwd_{tag}")
        y_dn = _hnorm_fwd(o_dn, proj, C_Z, dn_norm_w[l], name=f"hnorm_dn_fwd_{tag}")
        y_hg = _hnorm_fwd(o_hg, proj, C_HZ, hg_norm_w[l], name=f"hnorm_hg_fwd_{tag}")
        y = jnp.concatenate([y_dn, y_hg], axis=1)
        if ("rest", l) in relayed:
            rest_all = _relay_wait(relayed["rest", l], [y], name=f"gather_rest_{tag}_wait")
        else:
            rest_all = _push_wait(pending["rest", l], [y], broadcast=True, name=f"gather_rest_{tag}_wait")
        wo = rest_all[:, 0:rows_out].reshape(D_MODEL, D_MODEL)
        wg = rest_all[:, g_off:g_off + rows_out].reshape(D_MODEL, D_MODEL)
        wu = rest_all[:, u_off:u_off + up_rows].reshape(N_DEV, PLE_DIM, D_MODEL // N_DEV).transpose(1, 0, 2).reshape(PLE_DIM, D_MODEL)
        weights.append((wi, wo, wg, wu))
        h1 = _mm(y, wo, mode="nn", out_dtype=F32, res=h, name=f"mm_out_{tag}")
        pin = []
        if ("win", l + 1) in relayed:
            relayed["win", l + 1] = _relay_mid(relayed["win", l + 1], [h1], name=f"gather_w_in_l{l + 1}_relay")
            pin = [relayed["win", l + 1][4]]
        gp = _mm(h1, wg, mode="nn", out_dtype=F32, after=pin, name=f"mm_gate_{tag}")
        up = _mm(p[l, 0], wu, mode="nn", out_dtype=F32, name=f"mm_up_{tag}")
        h2 = _ple_fwd(h1, gp, up, name=f"ple_fwd_{tag}")
        saved.append(dict(h=h, hn=hn, proj=proj, qkv=qkv, beta=beta, gcs=gcs, st_dn=st_dn, qh=qh, kh=kh, lf=lf,
                          st_hg=st_hg, o_dn=o_dn, o_hg=o_hg, y=y, h1=h1, gp=gp, up=up, al=al, dt=dt, lb=lb))
        h = h2

    loss_row, dh, d_final_w = _final_fwd_bwd(h, final_norm_w, tgt, name="final_norm_loss")

    d_norm_w, d_alog, d_dt, d_dn_nw, d_hg_nw, d_lb, d_conv = ([None] * depth for _ in range(7))
    sent = {}
    for l in reversed(range(depth)):
        wi, wo, wg, wu = weights[l]
        sv = saved[l]
        tag = f"l{l}"
        dup, dgp = _ple_bwd(dh, sv["gp"], sv["up"], name=f"ple_bwd_{tag}")
        d_wu = _mm(p[l, 0], dup, mode="tn", out_dtype=BF16, name=f"mm_dwup_{tag}")
        d_wg = _mm(sv["h1"], dgp, mode="tn", out_dtype=BF16, name=f"mm_dwgate_{tag}")
        dh1 = _mm(dgp, wg, mode="nt", out_dtype=F32, res=dh, name=f"mm_dh1_{tag}")
        d_wo = _mm(sv["y"], dh1, mode="tn", out_dtype=BF16, name=f"mm_dwout_{tag}")
        parts_rest = jnp.concatenate(
            [d_wo.reshape(N_DEV, rows_out, D_MODEL), d_wg.reshape(N_DEV, rows_out, D_MODEL),
             d_wu.reshape(PLE_DIM, N_DEV, D_MODEL // N_DEV).transpose(1, 0, 2).reshape(N_DEV, up_rows, D_MODEL)], axis=1)
        sent["rest", l] = _push_start(parts_rest, own_slot(parts_rest[my]), broadcast=False, name=f"exchange_rest_{tag}_start")
        dy = _mm(dh1, wo, mode="nt", out_dtype=F32, name=f"mm_dy_{tag}")
        dn_nw = dn_norm_w[l] + sent["rest", l][4][0, 0]
        do_dn, dz_dn, d_dn_nw[l] = _hnorm_bwd(sv["o_dn"], sv["proj"], C_Z, dn_nw, dy, 0, name=f"hnorm_dn_bwd_{tag}")
        do_hg, dz_hg, d_hg_nw[l] = _hnorm_bwd(sv["o_hg"], sv["proj"], C_HZ, hg_norm_w[l], dy, BR_WIDTH, name=f"hnorm_hg_bwd_{tag}")
        dqkv, d_gc, dbeta = _dn_chunk_bwd(sv["qkv"], sv["gcs"], sv["beta"], sv["st_dn"], do_dn, name=f"dn_chunk_bwd_{tag}")
        dqkv_pre, d_conv[l] = _dn_qkv_bwd(sv["proj"], conv_full[l], dqkv, name=f"dn_qkv_bwd_{tag}")
        db, da, d_alog[l], d_dt[l] = _dn_gate_bwd(sv["proj"], sv["al"], sv["dt"], dbeta, d_gc, name=f"dn_gate_bwd_{tag}")
        dqh, dkh, dhi, dlf = _hg_chunk_bwd(sv["qh"], sv["kh"], sv["proj"], sv["lf"], sv["st_hg"], do_hg, name=f"hg_chunk_bwd_{tag}")
        dhq, dhf, d_lb[l] = _hg_prep_bwd(sv["proj"], sv["lb"], dqh, dkh, dlf, name=f"hg_prep_bwd_{tag}")
        dproj = jnp.concatenate([dqkv_pre, dz_dn, dhq, dhf, dhi, dz_hg, db, da], axis=1)
        def push_d_win(after):
            d_win = _mm(sv["hn"], dproj, mode="tn", out_dtype=BF16, after=after, name=f"mm_dwin_{tag}")
            parts_in = jnp.stack(_win_to_shards(d_win))
            return _push_start(parts_in, own_slot(parts_in[my]), broadcast=False, name=f"exchange_w_in_{tag}_start")

        if l > 0:
            sent["win", l] = push_d_win([])
            dhn = _mm(dproj, wi, mode="nt", out_dtype=F32, after=[sent["win", l][4]], name=f"mm_dhn_{tag}")
            dh, d_norm_w[l] = _rms_bwd(sv["h"], norm_w[l], dhn, dh1, name=f"rms_bwd_{tag}")
        else:
            dhn = _mm(dproj, wi, mode="nt", out_dtype=F32, name=f"mm_dhn_{tag}")
            dh, d_norm_w[l] = _rms_bwd(sv["h"], norm_w[l], dhn, dh1, name=f"rms_bwd_{tag}")
            small = _pack([loss_row, jnp.concatenate(d_norm_w, axis=0), d_final_w,
                           jnp.stack([a[0, :N_HEADS] for a in d_alog]), jnp.stack([a[0, :N_HEADS] for a in d_dt]),
                           jnp.concatenate(d_dn_nw, axis=0), jnp.concatenate(d_hg_nw, axis=0), jnp.concatenate(d_lb, axis=0),
                           jnp.stack(d_conv)])
            small_all = _all_gather(small, name="gather_small")
            sent["win", l] = push_d_win([small_all])
    grad_x = dh[None]

    small_shapes = [(1, 128), norm_w.shape, final_norm_w.shape, dn_A_log.shape, dn_dt_bias.shape, dn_norm_w.shape,
                    hg_norm_w.shape, hg_lb_logits.shape, (depth, CONV_W, 3 * BR_WIDTH)]
    tot = _unpack(_sum_parts(small_all, after=[sent["win", 0][4]], name="sum_small"), small_shapes)
    loss = tot[0][0, 0]
    g_lb = tot[7]
    g_logits = jax.vjp(_lower_bounds, hg_lb_logits)[1](g_lb)[0]
    g_conv = lax.dynamic_slice_in_dim(tot[8], my * (3 * BR_WIDTH // N_DEV), 3 * BR_WIDTH // N_DEV, axis=2)
    small_g = [tot[1], g_conv, tot[3], tot[4], tot[5], g_logits, tot[6], tot[2]]
    small_w = [norm_w, dn_conv_w, dn_A_log, dn_dt_bias, dn_norm_w, hg_lb_logits, hg_norm_w, final_norm_w]
    small_m = [m_norm_w, m_dn_conv_w, m_dn_A_log, m_dn_dt_bias, m_dn_norm_w, m_hg_lb_logits, m_hg_norm_w, m_final_norm_w]
    small_v = [v_norm_w, v_dn_conv_w, v_dn_A_log, v_dn_dt_bias, v_dn_norm_w, v_hg_lb_logits, v_hg_norm_w, v_final_norm_w]
    pk_w = _pack(small_w)
    res_small = _adamw(_pack(small_g)[None], 0, pk_w, _pack(small_m), _pack(small_v), name="adamw_small", tr=pk_w.shape[0])
    shapes_w = [a.shape for a in small_w]
    sg, sd, sm, sv_ = (_unpack(r, shapes_w) for r in res_small)

    r_win = r_wo = r_wg = r_wu = None
    done = [grad_x, res_small[0]]

    def flat(a, cols):
        return a.reshape(-1, cols)

    for l in reversed(range(depth)):
        tag = f"l{l}"
        land_rest = _push_wait(sent["rest", l], done, broadcast=False, name=f"exchange_rest_{tag}_wait")
        r_wo = _adamw(land_rest, 0, flat(w_out, D_MODEL), flat(m_w_out, D_MODEL), flat(v_w_out, D_MODEL), layer=l,
                      n_layers=depth, prev=r_wo, name=f"adamw_w_out_{tag}", tr=rows_out)
        r_wg = _adamw(land_rest, g_off, flat(w_ple_gate, D_MODEL), flat(m_w_ple_gate, D_MODEL), flat(v_w_ple_gate, D_MODEL),
                      layer=l, n_layers=depth, prev=r_wg, name=f"adamw_w_gate_{tag}", tr=rows_out)
        r_wu = _adamw(land_rest, u_off, flat(w_ple_up, D_MODEL), flat(m_w_ple_up, D_MODEL), flat(v_w_ple_up, D_MODEL),
                      layer=l, n_layers=depth, prev=r_wu, name=f"adamw_w_up_{tag}", tr=up_rows)
        done = [r_wo[0], r_wg[0], r_wu[0]]
    for l in reversed(range(depth)):
        tag = f"l{l}"
        land_in = _push_wait(sent["win", l], done, broadcast=False, name=f"exchange_w_in_{tag}_wait")
        r_win = _adamw(land_in, 0, flat(w_in, SHARD_IN), flat(m_w_in, SHARD_IN), flat(v_w_in, SHARD_IN), layer=l,
                       n_layers=depth, prev=r_win, name=f"adamw_w_in_{tag}", tr=256)
        done = [r_win[0]]
    r_win = [o.reshape(w_in.shape) for o in r_win]
    r_wo = [o.reshape(w_out.shape) for o in r_wo]
    r_wg = [o.reshape(w_ple_gate.shape) for o in r_wg]
    r_wu = [o.reshape(w_ple_up.shape) for o in r_wu]

    def order(small_list, big_in, big_out, big_up, big_gate):
        nw, cw, al_, dt_, dnw, lbl, hnw, fw = small_list
        return [nw, big_in, cw, al_, dt_, dnw, lbl, hnw, big_out, big_up, big_gate, fw]

    outs = [loss, grad_x]
    for i, sl in enumerate((sg, sd, sm, sv_)):
        outs += order(sl, r_win[i], r_wo[i], r_wu[i], r_wg[i])
    return tuple(outs)
```

```python
import functools

import jax
import jax.numpy as jnp
from jax import lax
from jax.experimental import pallas as pl
from jax.experimental.pallas import tpu as pltpu

F32 = jnp.float32
BF16 = jnp.bfloat16
HIGHEST = lax.Precision.HIGHEST

N_DEV = 8
D_MODEL = 2048
PLE_DIM = 256
HEAD_DIM = 128
N_HEADS = 8
BR_WIDTH = N_HEADS * HEAD_DIM
CHUNK = 64
SUB = 16
CONV_W = 4
NORM_EPS = 1e-6
L2_EPS = 1e-6
IN_WIDTH = 8208
SHARD_IN = IN_WIDTH // N_DEV
EXP_CLAMP = 80.0

C_QKV, C_Z, C_HQ, C_HF, C_HI, C_HZ, C_B, C_A, N_PROJ = 0, 3072, 4096, 5120, 6144, 7168, 8192, 8320, 8448

ADAM_LR, ADAM_B1, ADAM_B2, ADAM_EPS, ADAM_WD, ADAM_STEP = 0.001, 0.9, 0.999, 1e-08, 0.01, 10

VMEM_LIMIT = 48 * 1024 * 1024


def _cp(*sem):
    return pltpu.CompilerParams(dimension_semantics=sem, vmem_limit_bytes=VMEM_LIMIT)


class _Heads:
    def __init__(self, vals):
        self.v = tuple(vals)

    def __add__(self, o):
        return _hmap(lambda a, b: a + b, self, o)

    def __radd__(self, o):
        return _hmap(lambda a, b: b + a, self, o)

    def __sub__(self, o):
        return _hmap(lambda a, b: a - b, self, o)

    def __rsub__(self, o):
        return _hmap(lambda a, b: b - a, self, o)

    def __mul__(self, o):
        return _hmap(lambda a, b: a * b, self, o)

    def __rmul__(self, o):
        return _hmap(lambda a, b: b * a, self, o)

    def __neg__(self):
        return _hmap(lambda a: -a, self)

    def __getitem__(self, idx):
        return _hmap(lambda a: a[idx], self)


def _hmap(fn, *args):
    n = next((len(a.v) for a in args if isinstance(a, _Heads)), None)
    if n is None:
        return fn(*args)
    return _Heads(fn(*[a.v[i] if isinstance(a, _Heads) else a for a in args]) for i in range(n))


def _dot(a, b, ca, cb):
    return _hmap(lambda x, y: lax.dot_general(x.astype(BF16), y.astype(BF16), (((ca,), (cb,)), ((), ())),
                                              preferred_element_type=F32), a, b)


def _nn(a, b):
    return _dot(a, b, 1, 0)


def _nt(a, b):
    return _dot(a, b, 1, 1)


def _tn(a, b):
    return _dot(a, b, 0, 0)


def _split(a):
    hi = _hmap(lambda x: x.astype(BF16), a)
    return hi, _hmap(lambda x, h: (x - h.astype(F32)).astype(BF16), a, hi)


def _dot3(a, b, ca, cb):
    ah, al = _split(a)
    bh, bl = _split(b)
    return _dot(ah, bh, ca, cb) + (_dot(ah, bl, ca, cb) + _dot(al, bh, ca, cb))


def _nn_exact(a, b):
    return _hmap(lambda y: lax.dot_general(a, y, (((1,), (0,)), ((), ())), precision=HIGHEST,
                                           preferred_element_type=F32), b)


def _exp(x):
    return _hmap(jnp.exp, x)


def _sum(x, axis):
    return _hmap(lambda a: jnp.sum(a, axis=axis, keepdims=True), x)


def _stack_rows(parts):
    return _hmap(lambda *xs: jnp.concatenate(xs, axis=0), *parts)


def _sigmoid(x):
    return jax.nn.sigmoid(x)


def _silu(x):
    return x * _sigmoid(x)


def _dsilu(x):
    s = _sigmoid(x)
    return s * (1.0 + x * (1.0 - s))


def _softplus(x):
    return jnp.maximum(x, 0.0) + jnp.log(1.0 + jnp.exp(-jnp.abs(x)))


def _iota2(n, m, axis):
    return lax.broadcasted_iota(jnp.int32, (n, m), axis)


def _col2row(col, eye):
    return _hmap(lambda c: jnp.sum(eye * c, axis=0, keepdims=True), col)


def _row2col(row, eye):
    return _hmap(lambda r: jnp.sum(eye * r, axis=1, keepdims=True), row)


def _pick_lane(block, lane_idx):
    lane = _iota2(block.shape[0], block.shape[1], 1)
    return jnp.sum(jnp.where(lane == lane_idx, block, 0.0), axis=1, keepdims=True)


MM_TILE_M, MM_TILE_N, MM_TILE_K = 1024, 1408, 2048


def _tile(dim, cap):
    if dim <= cap:
        return dim
    t = cap - cap % 128
    while dim % t:
        t -= 128
    return t


def _mm(a, b, *, mode, out_dtype, res=None, after=(), name):
    if mode == "nn":
        (m, kd), (_, n) = a.shape, b.shape
    elif mode == "nt":
        (m, kd), (n, _) = a.shape, b.shape
    else:
        (kd, m), (_, n) = a.shape, b.shape
    tm, tn, tk = _tile(m, MM_TILE_M), _tile(n, MM_TILE_N), _tile(kd, MM_TILE_K)
    assert m % tm == 0 and n % tn == 0 and kd % tk == 0, (m, n, kd, tm, tn, tk)
    nk = kd // tk
    ca, cb = {"nn": (1, 0), "nt": (1, 1), "tn": (0, 0)}[mode]

    def body(*refs):
        a_ref, b_ref = refs[:2]
        r_ref = None if res is None else refs[2]
        o_ref, acc_ref = refs[-2:]
        k = pl.program_id(2)

        @pl.when(k == 0)
        def _():
            acc_ref[...] = jnp.zeros_like(acc_ref)

        acc_ref[...] += _dot(a_ref[...], b_ref[...], ca, cb)

        @pl.when(k == nk - 1)
        def _():
            out = acc_ref[...]
            if r_ref is not None:
                out = out + r_ref[...].astype(F32)
            o_ref[...] = out.astype(o_ref.dtype)

    a_spec = pl.BlockSpec((tk, tm), lambda i, j, k: (k, i)) if mode == "tn" else pl.BlockSpec((tm, tk), lambda i, j, k: (i, k))
    b_spec = pl.BlockSpec((tn, tk), lambda i, j, k: (j, k)) if mode == "nt" else pl.BlockSpec((tk, tn), lambda i, j, k: (k, j))
    o_spec = pl.BlockSpec((tm, tn), lambda i, j, k: (i, j))
    in_specs = [a_spec, b_spec] + ([o_spec] if res is not None else []) + [pl.BlockSpec(memory_space=pl.ANY)] * len(after)
    args = (a, b) + ((res,) if res is not None else ()) + tuple(after)
    return pl.pallas_call(
        body, name=name, grid=(m // tm, n // tn, nk), in_specs=in_specs, out_specs=o_spec,
        out_shape=jax.ShapeDtypeStruct((m, n), out_dtype),
        scratch_shapes=[pltpu.VMEM((tm, tn), F32)],
        compiler_params=_cp("parallel", "parallel", "arbitrary"),
    )(*args)


ROW_TILE = 256


def _rms_fwd(h, w, *, name):
    s, d = h.shape
    tr = min(ROW_TILE, s)

    def body(h_ref, w_ref, o_ref):
        x = h_ref[...]
        r = lax.rsqrt(jnp.mean(x * x, axis=-1, keepdims=True) + NORM_EPS)
        o_ref[...] = (x * r * w_ref[...]).astype(o_ref.dtype)

    return pl.pallas_call(
        body, name=name, grid=(s // tr,),
        in_specs=[pl.BlockSpec((tr, d), lambda i: (i, 0)), pl.BlockSpec((1, d), lambda i: (0, 0))],
        out_specs=pl.BlockSpec((tr, d), lambda i: (i, 0)),
        out_shape=jax.ShapeDtypeStruct((s, d), BF16), compiler_params=_cp("parallel"),
    )(h, w.reshape(1, d))


def _rms_bwd_math(x, w, dy):
    d = x.shape[-1]
    r = lax.rsqrt(jnp.mean(x * x, axis=-1, keepdims=True) + NORM_EPS)
    gw = dy * w
    dx = r * gw - x * ((r * r * r) * (jnp.sum(gw * x, axis=-1, keepdims=True) / d))
    return dx, dy * x * r


def _rms_bwd(h, w, dhn, res, *, name):
    s, d = h.shape
    tr = min(ROW_TILE, s)

    def body(h_ref, w_ref, g_ref, r_ref, dh_ref, dw_ref):
        @pl.when(pl.program_id(0) == 0)
        def _():
            dw_ref[...] = jnp.zeros_like(dw_ref)

        dx, dwt = _rms_bwd_math(h_ref[...], w_ref[...], g_ref[...])
        dh_ref[...] = r_ref[...] + dx
        dw_ref[...] += jnp.sum(dwt, axis=0, keepdims=True)

    row = pl.BlockSpec((tr, d), lambda i: (i, 0))
    vec = pl.BlockSpec((1, d), lambda i: (0, 0))
    return pl.pallas_call(
        body, name=name, grid=(s // tr,), in_specs=[row, vec, row, row], out_specs=[row, vec],
        out_shape=[jax.ShapeDtypeStruct((s, d), F32), jax.ShapeDtypeStruct((1, d), F32)],
        compiler_params=_cp("arbitrary"),
    )(h, w.reshape(1, d), dhn, res)


def _final_fwd_bwd(h, w, tgt, *, name):
    s, d = h.shape
    tr = min(ROW_TILE, s)

    def body(h_ref, w_ref, t_ref, loss_ref, dh_ref, dw_ref):
        @pl.when(pl.program_id(0) == 0)
        def _():
            loss_ref[...] = jnp.zeros_like(loss_ref)
            dw_ref[...] = jnp.zeros_like(dw_ref)

        x = h_ref[...]
        wv = w_ref[...]
        r = lax.rsqrt(jnp.mean(x * x, axis=-1, keepdims=True) + NORM_EPS)
        err = x * r * wv - t_ref[...]
        row_loss = jnp.mean(err * err, axis=-1, keepdims=True)
        loss_ref[...] += 0.5 * jnp.sum(row_loss, axis=0, keepdims=True)
        dx, dwt = _rms_bwd_math(x, wv, err / d)
        dh_ref[...] = dx
        dw_ref[...] += jnp.sum(dwt, axis=0, keepdims=True)

    row = pl.BlockSpec((tr, d), lambda i: (i, 0))
    vec = pl.BlockSpec((1, d), lambda i: (0, 0))
    return pl.pallas_call(
        body, name=name, grid=(s // tr,), in_specs=[row, vec, row],
        out_specs=[pl.BlockSpec((1, 128), lambda i: (0, 0)), row, vec],
        out_shape=[jax.ShapeDtypeStruct((1, 128), F32), jax.ShapeDtypeStruct((s, d), F32),
                   jax.ShapeDtypeStruct((1, d), F32)],
        compiler_params=_cp("arbitrary"),
    )(h, w.reshape(1, d), tgt)


def _ple_fwd(h1, gate_pre, up, *, name):
    s, d = h1.shape
    tr = min(ROW_TILE, s)

    def body(h_ref, g_ref, u_ref, o_ref):
        o_ref[...] = h_ref[...] + u_ref[...] * _sigmoid(g_ref[...])

    row = pl.BlockSpec((tr, d), lambda i: (i, 0))
    return pl.pallas_call(body, name=name, grid=(s // tr,), in_specs=[row, row, row], out_specs=row,
                          out_shape=jax.ShapeDtypeStruct((s, d), F32), compiler_params=_cp("parallel"))(h1, gate_pre, up)


def _ple_bwd(dh2, gate_pre, up, *, name):
    s, d = dh2.shape
    tr = min(ROW_TILE, s)

    def body(d_ref, g_ref, u_ref, dup_ref, dgp_ref):
        dh = d_ref[...]
        gate = _sigmoid(g_ref[...])
        dup_ref[...] = (dh * gate).astype(BF16)
        dgp_ref[...] = (dh * u_ref[...] * gate * (1.0 - gate)).astype(BF16)

    row = pl.BlockSpec((tr, d), lambda i: (i, 0))
    return pl.pallas_call(body, name=name, grid=(s // tr,), in_specs=[row, row, row], out_specs=[row, row],
                          out_shape=[jax.ShapeDtypeStruct((s, d), BF16)] * 2, compiler_params=_cp("parallel"))(dh2, gate_pre, up)


HN_TILE = 512


def _hnorm_fwd(o, proj, z_col, w, *, name):
    s = o.shape[0]
    tr = min(HN_TILE, s)

    def body(o_ref, z_ref, w_ref, y_ref):
        wv = w_ref[...]
        for h in range(N_HEADS):
            cols = slice(h * HEAD_DIM, (h + 1) * HEAD_DIM)
            x = o_ref[:, cols]
            r = lax.rsqrt(jnp.mean(x * x, axis=-1, keepdims=True) + NORM_EPS)
            y_ref[:, cols] = (x * r * wv * _silu(z_ref[:, cols])).astype(BF16)

    blk = pl.BlockSpec((tr, BR_WIDTH), lambda i: (i, 0))
    return pl.pallas_call(
        body, name=name, grid=(s // tr,),
        in_specs=[blk, pl.BlockSpec((tr, BR_WIDTH), lambda i: (i, z_col // BR_WIDTH)), pl.BlockSpec((1, HEAD_DIM), lambda i: (0, 0))],
        out_specs=blk, out_shape=jax.ShapeDtypeStruct((s, BR_WIDTH), BF16), compiler_params=_cp("parallel"),
    )(o, proj, w.reshape(1, HEAD_DIM))


def _hnorm_bwd(o, proj, z_col, w, dy, dy_col, *, name):
    s = o.shape[0]
    tr = min(HN_TILE, s)

    def body(o_ref, z_ref, w_ref, dy_ref, do_ref, dz_ref, dw_ref):
        @pl.when(pl.program_id(0) == 0)
        def _():
            dw_ref[...] = jnp.zeros_like(dw_ref)

        wv = w_ref[...]
        dw = jnp.zeros((1, HEAD_DIM), F32)
        for h in range(N_HEADS):
            cols = slice(h * HEAD_DIM, (h + 1) * HEAD_DIM)
            x, z, g = o_ref[:, cols], z_ref[:, cols], dy_ref[:, cols]
            r = lax.rsqrt(jnp.mean(x * x, axis=-1, keepdims=True) + NORM_EPS)
            on = x * r * wv
            don = g * _silu(z)
            dz_ref[:, cols] = (g * on * _dsilu(z)).astype(BF16)
            gw = don * wv
            do_ref[:, cols] = r * gw - x * ((r * r * r) * (jnp.sum(gw * x, axis=-1, keepdims=True) / HEAD_DIM))
            dw = dw + jnp.sum(don * x * r, axis=0, keepdims=True)
        dw_ref[...] += dw

    blk = pl.BlockSpec((tr, BR_WIDTH), lambda i: (i, 0))
    vec = pl.BlockSpec((1, HEAD_DIM), lambda i: (0, 0))
    return pl.pallas_call(
        body, name=name, grid=(s // tr,),
        in_specs=[blk, pl.BlockSpec((tr, BR_WIDTH), lambda i: (i, z_col // BR_WIDTH)), vec,
                  pl.BlockSpec((tr, BR_WIDTH), lambda i: (i, dy_col // BR_WIDTH))],
        out_specs=[blk, blk, vec],
        out_shape=[jax.ShapeDtypeStruct((s, BR_WIDTH), F32), jax.ShapeDtypeStruct((s, BR_WIDTH), BF16),
                   jax.ShapeDtypeStruct((1, HEAD_DIM), F32)],
        compiler_params=_cp("arbitrary"),
    )(o, proj, w.reshape(1, HEAD_DIM), dy)


def _conv_silu(x, w, s):
    row = _iota2(s, x.shape[1], 0)
    c = w[CONV_W - 1:CONV_W, :] * x
    for k in range(1, CONV_W):
        c = c + w[CONV_W - 1 - k:CONV_W - k, :] * jnp.where(row >= k, pltpu.roll(x, k, 0), 0.0)
    return c


def _dn_qkv_fwd(proj, conv_w, *, name):
    s = proj.shape[0]
    nb = 3 * N_HEADS

    def body(x_ref, w_ref, o_ref):
        j = pl.program_id(0)
        sv = _silu(_conv_silu(x_ref[...], w_ref[...], s))
        r = lax.rsqrt(jnp.sum(sv * sv, axis=-1, keepdims=True) + L2_EPS)
        scale = jnp.where(j < N_HEADS, HEAD_DIM ** -0.5, 1.0).astype(F32)
        o_ref[...] = jnp.where(j < 2 * N_HEADS, sv * r * scale, sv)

    return pl.pallas_call(
        body, name=name, grid=(nb,),
        in_specs=[pl.BlockSpec((s, HEAD_DIM), lambda j: (0, j)), pl.BlockSpec((CONV_W, HEAD_DIM), lambda j: (0, j))],
        out_specs=pl.BlockSpec((s, HEAD_DIM), lambda j: (0, j)),
        out_shape=jax.ShapeDtypeStruct((s, 3 * BR_WIDTH), F32), compiler_params=_cp("parallel"),
    )(proj, conv_w)


def _dn_qkv_bwd(proj, conv_w, dqkv, *, name):
    s = proj.shape[0]
    nb = 3 * N_HEADS

    def body(x_ref, w_ref, g_ref, dx_ref, dw_ref):
        j = pl.program_id(0)
        x, w, g = x_ref[...], w_ref[...], g_ref[...]
        c = _conv_silu(x, w, s)
        sv = _silu(c)
        r = lax.rsqrt(jnp.sum(sv * sv, axis=-1, keepdims=True) + L2_EPS)
        scale = jnp.where(j < N_HEADS, HEAD_DIM ** -0.5, 1.0).astype(F32)
        ds_n = scale * (r * g - sv * ((r * r * r) * jnp.sum(g * sv, axis=-1, keepdims=True)))
        dc = jnp.where(j < 2 * N_HEADS, ds_n, g) * _dsilu(c)
        row = _iota2(s, HEAD_DIM, 0)
        dx = w[CONV_W - 1:CONV_W, :] * dc
        dws = [jnp.sum(dc * x, axis=0, keepdims=True)]
        for k in range(1, CONV_W):
            dx = dx + w[CONV_W - 1 - k:CONV_W - k, :] * jnp.where(row < s - k, pltpu.roll(dc, s - k, 0), 0.0)
            dws.append(jnp.sum(dc * jnp.where(row >= k, pltpu.roll(x, k, 0), 0.0), axis=0, keepdims=True))
        dx_ref[...] = dx.astype(BF16)
        for k in range(CONV_W):
            dw_ref[CONV_W - 1 - k:CONV_W - k, :] = dws[k]

    blk = pl.BlockSpec((s, HEAD_DIM), lambda j: (0, j))
    wblk = pl.BlockSpec((CONV_W, HEAD_DIM), lambda j: (0, j))
    return pl.pallas_call(
        body, name=name, grid=(nb,), in_specs=[blk, wblk, blk], out_specs=[blk, wblk],
        out_shape=[jax.ShapeDtypeStruct((s, 3 * BR_WIDTH), BF16), jax.ShapeDtypeStruct((CONV_W, 3 * BR_WIDTH), F32)],
        compiler_params=_cp("parallel"),
    )(proj, conv_w, dqkv)


def _tri(n, kind):
    r, c = _iota2(n, n, 0), _iota2(n, n, 1)
    if kind == "lower":
        return (r >= c).astype(F32)
    if kind == "upper":
        return (r <= c).astype(F32)
    return (r == c).astype(F32)


GATE_TILE = 512


def _dn_gate_fwd(proj, a_log, dt_bias, *, name):
    s = proj.shape[0]
    tr = min(GATE_TILE, s)

    def body(b_ref, a_ref, al_ref, dt_ref, beta_ref, g_ref):
        beta_ref[...] = _sigmoid(b_ref[...])
        g = -jnp.exp(al_ref[...]) * _softplus(a_ref[...] + dt_ref[...])
        low = _tri(CHUNK, "lower")
        for c in range(tr // CHUNK):
            rows = slice(c * CHUNK, (c + 1) * CHUNK)
            g_ref[rows, :] = _nn_exact(low, g[rows, :])

    blk = lambda cb: pl.BlockSpec((tr, HEAD_DIM), lambda i: (i, cb))
    vec = pl.BlockSpec((1, HEAD_DIM), lambda i: (0, 0))
    out = pl.BlockSpec((tr, HEAD_DIM), lambda i: (i, 0))
    return pl.pallas_call(
        body, name=name, grid=(s // tr,), in_specs=[blk(C_B // HEAD_DIM), blk(C_A // HEAD_DIM), vec, vec],
        out_specs=[out, out], out_shape=[jax.ShapeDtypeStruct((s, HEAD_DIM), F32)] * 2, compiler_params=_cp("parallel"),
    )(proj, proj, a_log, dt_bias)


def _dn_gate_bwd(proj, a_log, dt_bias, dbeta, d_g, *, name):
    s = proj.shape[0]
    tr = min(GATE_TILE, s)

    def body(b_ref, a_ref, al_ref, dt_ref, dbeta_ref, dG_ref, db_ref, da_ref, dal_ref, ddt_ref):
        @pl.when(pl.program_id(0) == 0)
        def _():
            dal_ref[...] = jnp.zeros_like(dal_ref)
            ddt_ref[...] = jnp.zeros_like(ddt_ref)

        beta = _sigmoid(b_ref[...])
        db_ref[...] = (dbeta_ref[...] * beta * (1.0 - beta)).astype(BF16)
        pre = a_ref[...] + dt_ref[...]
        neg_ea = -jnp.exp(al_ref[...])
        up = _tri(CHUNK, "upper")
        d_g = dG_ref[...]
        dg = jnp.concatenate([_nn_exact(up, d_g[c * CHUNK:(c + 1) * CHUNK, :]) for c in range(tr // CHUNK)], axis=0)
        da = dg * neg_ea * _sigmoid(pre)
        da_ref[...] = da.astype(BF16)
        ddt_ref[...] += jnp.sum(da, axis=0, keepdims=True)
        dal_ref[...] += jnp.sum(dg * neg_ea * _softplus(pre), axis=0, keepdims=True)

    blk = lambda cb: pl.BlockSpec((tr, HEAD_DIM), lambda i: (i, cb))
    vec = pl.BlockSpec((1, HEAD_DIM), lambda i: (0, 0))
    io = pl.BlockSpec((tr, HEAD_DIM), lambda i: (i, 0))
    return pl.pallas_call(
        body, name=name, grid=(s // tr,),
        in_specs=[blk(C_B // HEAD_DIM), blk(C_A // HEAD_DIM), vec, vec, io, io], out_specs=[io, io, vec, vec],
        out_shape=[jax.ShapeDtypeStruct((s, HEAD_DIM), BF16)] * 2 + [jax.ShapeDtypeStruct((1, HEAD_DIM), F32)] * 2,
        compiler_params=_cp("arbitrary"),
    )(proj, proj, a_log, dt_bias, dbeta, d_g)


def _unit_lower_inverse(a_strict, eye):
    x = -a_strict
    t = x + eye
    p = x
    n = 2
    while n < CHUNK:
        p = _nn(p, p)
        t = t + _nn(t, p)
        n *= 2
    return t


def _dn_chunk_common(q, k, v, gc, beta, st):
    c = CHUNK
    eye = _tri(c, "eye")
    low = _tri(c, "lower")
    strict = low - eye
    grow = _col2row(gc, eye)
    dec = _hmap(lambda g_, gr: low * jnp.exp(low * (g_ - gr)), gc, grow)
    kb = k * beta
    a_mat = _nt(kb, k) * dec * strict
    t_inv = _unit_lower_inverse(a_mat, eye)
    e_g = _exp(gc)
    u = _nn(t_inv, v * beta)
    w = _nn(t_inv, kb * e_g)
    p_qk = _nt(q, k)
    qk = p_qk * dec
    qd = q * e_g
    last = (_iota2(c, 1, 0) == c - 1).astype(F32)
    g_last = _sum(gc * last, 0)
    e_t = _exp(g_last - gc)
    kt = k * e_t
    tail = _exp(g_last)
    vn = u - _nn(w, st)
    return dict(eye=eye, low=low, strict=strict, dec=dec, kb=kb, a_mat=a_mat, t_inv=t_inv, e_g=e_g, u=u, w=w,
                qk=qk, qd=qd, last=last, e_t=e_t, kt=kt, tail=tail, vn=vn)


def _dn_chunk_fwd_math(q, k, v, gc, beta, st):
    m = _dn_chunk_common(q, k, v, gc, beta, st)
    o = _nn(m["qd"], st) + _nn(m["qk"], m["vn"])
    st2 = st * m["tail"] + _tn(m["kt"], m["vn"])
    return o, st2


def _dn_chunk_bwd_math(q, k, v, gc, beta, st, do, dst2):
    m = _dn_chunk_common(q, k, v, gc, beta, st)
    eye, low, strict = m["eye"], m["low"], m["strict"]
    dvn = _tn(m["qk"], do) + _nn(m["kt"], dst2)
    dqk = _nt(do, m["vn"]) * low
    dqd = _nt(do, st)
    dst = _tn(m["qd"], do) + dst2 * m["tail"] - _tn(m["w"], dvn)
    dkt = _nt(m["vn"], dst2)
    dtail = _sum(_sum(st * dst2, 1), 0)
    dw = -_nt(dvn, st)
    dvb = _tn(m["t_inv"], dvn)
    dkg = _tn(m["t_inv"], dw)
    d_a = (_nt(dvb, m["u"]) + _nt(dkg, m["w"])) * (-strict)
    dkk = d_a * m["dec"]
    dp = dqk * m["dec"]
    dq = _nn(dp, k) + dqd * m["e_g"]
    dkb = _nn(dkk, k) + dkg * m["e_g"]
    dk = _tn(dp, q) + _tn(dkk, m["kb"]) + dkb * beta + dkt * m["e_t"]
    dv = dvb * beta
    dbeta = _sum(dvb * v + dkb * k, 1)
    de_g = _sum(dkg * m["kb"] + dqd * q, 1)
    de_t = _sum(dkt * k, 1)
    mm = d_a * m["a_mat"] + dqk * m["qk"]
    dgc = (_sum(mm, 1) - _row2col(_sum(mm, 0), eye) + de_g * m["e_g"] - de_t * m["e_t"]
           + (_sum(de_t * m["e_t"], 0) + dtail * m["tail"]) * m["last"])
    return dq, dk, dv, dgc, dbeta, dst


def _heads_of(ref):
    return _Heads(ref[:, h * HEAD_DIM:(h + 1) * HEAD_DIM] for h in range(N_HEADS))


def _lanes_of(block):
    return _Heads(_pick_lane(block, h) for h in range(N_HEADS))


def _dn_chunk_fwd(qkv, gcs, beta, *, name):
    s = qkv.shape[0]
    n = s // CHUNK

    def body(q_ref, k_ref, v_ref, g_ref, b_ref, o_ref, st_out_ref, st_ref):
        @pl.when(pl.program_id(0) == 0)
        def _():
            st_ref[...] = jnp.zeros_like(st_ref)

        gblk, bblk = g_ref[...], b_ref[...]
        st = _Heads(st_ref[h] for h in range(N_HEADS))
        o, st2 = _dn_chunk_fwd_math(_heads_of(q_ref), _heads_of(k_ref), _heads_of(v_ref), _lanes_of(gblk),
                                    _lanes_of(bblk), st)
        for h in range(N_HEADS):
            st_out_ref[0, h] = st.v[h]
            o_ref[:, h * HEAD_DIM:(h + 1) * HEAD_DIM] = o.v[h]
            st_ref[h] = st2.v[h]

    blk = lambda off: pl.BlockSpec((CHUNK, BR_WIDTH), lambda c: (c, off))
    sc = pl.BlockSpec((CHUNK, HEAD_DIM), lambda c: (c, 0))
    return pl.pallas_call(
        body, name=name, grid=(n,),
        in_specs=[blk(0), blk(1), blk(2), sc, sc],
        out_specs=[blk(0), pl.BlockSpec((1, N_HEADS, HEAD_DIM, HEAD_DIM), lambda c: (c, 0, 0, 0))],
        out_shape=[jax.ShapeDtypeStruct((s, BR_WIDTH), F32), jax.ShapeDtypeStruct((n, N_HEADS, HEAD_DIM, HEAD_DIM), F32)],
        scratch_shapes=[pltpu.VMEM((N_HEADS, HEAD_DIM, HEAD_DIM), F32)],
        compiler_params=_cp("arbitrary"),
    )(qkv, qkv, qkv, gcs, beta)


def _dn_chunk_bwd(qkv, gcs, beta, states, do, *, name):
    s = qkv.shape[0]
    n = s // CHUNK

    def body(q_ref, k_ref, v_ref, g_ref, b_ref, st_in_ref, do_ref, dqkv_ref, dg_ref, dbeta_ref, dst_ref):
        @pl.when(pl.program_id(0) == 0)
        def _():
            dst_ref[...] = jnp.zeros_like(dst_ref)

        gblk, bblk = g_ref[...], b_ref[...]
        lane = _iota2(CHUNK, HEAD_DIM, 1)
        dg_all = jnp.zeros((CHUNK, HEAD_DIM), F32)
        dbeta_all = jnp.zeros((CHUNK, HEAD_DIM), F32)
        dq, dk, dv, dgc, dbeta, dst = _dn_chunk_bwd_math(
            _heads_of(q_ref), _heads_of(k_ref), _heads_of(v_ref), _lanes_of(gblk), _lanes_of(bblk),
            _Heads(st_in_ref[0, h] for h in range(N_HEADS)), _heads_of(do_ref),
            _Heads(dst_ref[h] for h in range(N_HEADS)))
        for h in range(N_HEADS):
            for part, val in enumerate((dq, dk, dv)):
                c0 = part * BR_WIDTH + h * HEAD_DIM
                dqkv_ref[:, c0:c0 + HEAD_DIM] = val.v[h]
            dg_all = jnp.where(lane == h, dgc.v[h], dg_all)
            dbeta_all = jnp.where(lane == h, dbeta.v[h], dbeta_all)
            dst_ref[h] = dst.v[h]
        dg_ref[...] = dg_all
        dbeta_ref[...] = dbeta_all

    blk = lambda off: pl.BlockSpec((CHUNK, BR_WIDTH), lambda c: (n - 1 - c, off))
    sc = pl.BlockSpec((CHUNK, HEAD_DIM), lambda c: (n - 1 - c, 0))
    outs = pl.pallas_call(
        body, name=name, grid=(n,),
        in_specs=[blk(0), blk(1), blk(2), sc, sc,
                  pl.BlockSpec((1, N_HEADS, HEAD_DIM, HEAD_DIM), lambda c: (n - 1 - c, 0, 0, 0)), blk(0)],
        out_specs=[pl.BlockSpec((CHUNK, 3 * BR_WIDTH), lambda c: (n - 1 - c, 0)), sc, sc],
        out_shape=[jax.ShapeDtypeStruct((s, 3 * BR_WIDTH), F32)] + [jax.ShapeDtypeStruct((s, HEAD_DIM), F32)] * 2,
        scratch_shapes=[pltpu.VMEM((N_HEADS, HEAD_DIM, HEAD_DIM), F32)],
        compiler_params=_cp("arbitrary"),
    )(qkv, qkv, qkv, gcs, beta, states, do)
    return outs


def _hg_prep_fwd(proj, lb, *, name):
    s = proj.shape[0]
    tr = min(ROW_TILE, s)

    def body(q_ref, f_ref, lb_ref, qo_ref, ko_ref, lf_ref):
        f, lbv = f_ref[...], lb_ref[...]
        qo_ref[...] = _silu(q_ref[...])
        ko_ref[...] = (1.0 - lbv) * _sigmoid(-f)
        lf_ref[...] = jnp.log(lbv + (1.0 - lbv) * _sigmoid(f))

    blk = lambda cb: pl.BlockSpec((tr, BR_WIDTH), lambda i: (i, cb))
    out = pl.BlockSpec((tr, BR_WIDTH), lambda i: (i, 0))
    return pl.pallas_call(
        body, name=name, grid=(s // tr,),
        in_specs=[blk(C_HQ // BR_WIDTH), blk(C_HF // BR_WIDTH), pl.BlockSpec((1, BR_WIDTH), lambda i: (0, 0))],
        out_specs=[out, out, out], out_shape=[jax.ShapeDtypeStruct((s, BR_WIDTH), F32)] * 3, compiler_params=_cp("parallel"),
    )(proj, proj, lb)


def _hg_prep_bwd(proj, lb, dq, dk, dlf, *, name):
    s = proj.shape[0]
    tr = min(ROW_TILE, s)

    def body(q_ref, f_ref, lb_ref, dq_ref, dk_ref, dlf_ref, dhq_ref, dhf_ref, dlb_ref):
        @pl.when(pl.program_id(0) == 0)
        def _():
            dlb_ref[...] = jnp.zeros_like(dlb_ref)

        f, lbv = f_ref[...], lb_ref[...]
        dhq_ref[...] = (dq_ref[...] * _dsilu(q_ref[...])).astype(BF16)
        sp, sn = _sigmoid(f), _sigmoid(-f)
        inner = lbv + (1.0 - lbv) * sp
        dlf_over = dlf_ref[...] / inner
        dkv = dk_ref[...]
        dhf_ref[...] = (dlf_over * (1.0 - lbv) * sp * sn - dkv * (1.0 - lbv) * sn * (1.0 - sn)).astype(BF16)
        dlb_ref[...] += jnp.sum(dlf_over * (1.0 - sp) - dkv * sn, axis=0, keepdims=True)

    blk = lambda cb: pl.BlockSpec((tr, BR_WIDTH), lambda i: (i, cb))
    io = pl.BlockSpec((tr, BR_WIDTH), lambda i: (i, 0))
    vec = pl.BlockSpec((1, BR_WIDTH), lambda i: (0, 0))
    return pl.pallas_call(
        body, name=name, grid=(s // tr,),
        in_specs=[blk(C_HQ // BR_WIDTH), blk(C_HF // BR_WIDTH), vec, io, io, io], out_specs=[io, io, vec],
        out_shape=[jax.ShapeDtypeStruct((s, BR_WIDTH), BF16)] * 2 + [jax.ShapeDtypeStruct((1, BR_WIDTH), F32)],
        compiler_params=_cp("arbitrary"),
    )(proj, proj, lb, dq, dk, dlf)


def _hg_chunk_common(q, k, lf):
    c = CHUNK
    g = _nn_exact(_tri(c, "lower"), lf)
    e_g = _exp(g)
    qd = q * e_g
    g_last = g[c - 1:c, :]
    e_t = _exp(g_last - g)
    kt = k * e_t
    tail = _exp(g_last)
    q_sc, k_sc, e_q, e_k = [], [], [], []
    for i in range(c // SUB):
        g_ref = g[i * SUB:i * SUB + 1, :]
        eq = _exp(g[i * SUB:(i + 1) * SUB, :] - g_ref)
        ek = _hmap(lambda gr, g_: jnp.exp(jnp.minimum(gr - g_, EXP_CLAMP)), g_ref, g)
        e_q.append(eq)
        e_k.append(ek)
        q_sc.append(q[i * SUB:(i + 1) * SUB, :] * eq)
        k_sc.append(k * ek)
    a_mat = _stack_rows([_nt(qi, ki) for qi, ki in zip(q_sc, k_sc)]) * _tri(c, "lower")
    return dict(e_g=e_g, qd=qd, e_t=e_t, kt=kt, tail=tail, q_sc=q_sc, k_sc=k_sc, e_q=e_q, e_k=e_k, a_mat=a_mat)


def _hg_chunk_fwd_math(q, k, v, lf, stt):
    m = _hg_chunk_common(q, k, lf)
    o = _nt(m["qd"], stt) + _nn(m["a_mat"], v)
    stt2 = stt * m["tail"] + _tn(v, m["kt"])
    return o, stt2


def _hg_chunk_bwd_math(q, k, v, lf, stt, do, dstt2):
    c = CHUNK
    m = _hg_chunk_common(q, k, lf)
    stt2 = stt * m["tail"] + _tn(v, m["kt"])
    later = _sum(stt2 * dstt2, 0)
    dqd = _dot3(do, stt, 1, 0)
    dstt = _tn(do, m["qd"]) + dstt2 * m["tail"]
    d_a = _dot3(do, v, 1, 1) * _tri(c, "lower")
    dv = _tn(m["a_mat"], do) + _nt(m["kt"], dstt2)
    dkt = _dot3(v, dstt2, 1, 0)
    dq_parts = []
    dk = dkt * m["e_t"]
    for i in range(c // SUB):
        d_ai = d_a[i * SUB:(i + 1) * SUB, :]
        dq_parts.append(_dot3(d_ai, m["k_sc"][i], 1, 0) * m["e_q"][i])
        dk = dk + _dot3(d_ai, m["q_sc"][i], 0, 0) * m["e_k"][i]
    dq = dqd * m["e_g"] + _stack_rows(dq_parts)
    db = q * dq - k * dk
    dlf = _nn_exact(_tri(c, "upper"), db) + later
    return dq, dk, dv, dlf, dstt


def _hg_chunk_fwd(qh, kh, proj, lf, *, name):
    s = qh.shape[0]
    n = s // CHUNK
    vb = C_HI // BR_WIDTH

    def body(q_ref, k_ref, v_ref, lf_ref, o_ref, st_out_ref, st_ref):
        @pl.when(pl.program_id(0) == 0)
        def _():
            st_ref[...] = jnp.zeros_like(st_ref)

        st = _Heads(st_ref[h] for h in range(N_HEADS))
        o, st2 = _hg_chunk_fwd_math(_heads_of(q_ref), _heads_of(k_ref), _heads_of(v_ref), _heads_of(lf_ref), st)
        for h in range(N_HEADS):
            st_out_ref[0, h] = st.v[h]
            o_ref[:, h * HEAD_DIM:(h + 1) * HEAD_DIM] = o.v[h]
            st_ref[h] = st2.v[h]

    blk = lambda off: pl.BlockSpec((CHUNK, BR_WIDTH), lambda c: (c, off))
    return pl.pallas_call(
        body, name=name, grid=(n,), in_specs=[blk(0), blk(0), blk(vb), blk(0)],
        out_specs=[blk(0), pl.BlockSpec((1, N_HEADS, HEAD_DIM, HEAD_DIM), lambda c: (c, 0, 0, 0))],
        out_shape=[jax.ShapeDtypeStruct((s, BR_WIDTH), F32), jax.ShapeDtypeStruct((n, N_HEADS, HEAD_DIM, HEAD_DIM), F32)],
        scratch_shapes=[pltpu.VMEM((N_HEADS, HEAD_DIM, HEAD_DIM), F32)],
        compiler_params=_cp("arbitrary"),
    )(qh, kh, proj, lf)


def _hg_chunk_bwd(qh, kh, proj, lf, states, do, *, name):
    s = qh.shape[0]
    n = s // CHUNK
    vb = C_HI // BR_WIDTH

    def body(q_ref, k_ref, v_ref, lf_ref, st_in_ref, do_ref, dq_ref, dk_ref, dv_ref, dlf_ref, dst_ref):
        @pl.when(pl.program_id(0) == 0)
        def _():
            dst_ref[...] = jnp.zeros_like(dst_ref)

        dq, dk, dv, dlf, dst = _hg_chunk_bwd_math(
            _heads_of(q_ref), _heads_of(k_ref), _heads_of(v_ref), _heads_of(lf_ref),
            _Heads(st_in_ref[0, h] for h in range(N_HEADS)), _heads_of(do_ref),
            _Heads(dst_ref[h] for h in range(N_HEADS)))
        for h in range(N_HEADS):
            cols = slice(h * HEAD_DIM, (h + 1) * HEAD_DIM)
            dq_ref[:, cols] = dq.v[h]
            dk_ref[:, cols] = dk.v[h]
            dv_ref[:, cols] = dv.v[h].astype(BF16)
            dlf_ref[:, cols] = dlf.v[h]
            dst_ref[h] = dst.v[h]

    blk = lambda off: pl.BlockSpec((CHUNK, BR_WIDTH), lambda c: (n - 1 - c, off))
    return pl.pallas_call(
        body, name=name, grid=(n,),
        in_specs=[blk(0), blk(0), blk(vb), blk(0),
                  pl.BlockSpec((1, N_HEADS, HEAD_DIM, HEAD_DIM), lambda c: (n - 1 - c, 0, 0, 0)), blk(0)],
        out_specs=[blk(0), blk(0), blk(0), blk(0)],
        out_shape=[jax.ShapeDtypeStruct((s, BR_WIDTH), F32)] * 2 + [jax.ShapeDtypeStruct((s, BR_WIDTH), BF16),
                                                                    jax.ShapeDtypeStruct((s, BR_WIDTH), F32)],
        scratch_shapes=[pltpu.VMEM((N_HEADS, HEAD_DIM, HEAD_DIM), F32)],
        compiler_params=_cp("arbitrary"),
    )(qh, kh, proj, lf, states, do)


_ANY = pl.BlockSpec(memory_space=pl.ANY)
_MESH = pl.DeviceIdType.MESH


def _all_gather(x_local, *, name, after=()):
    n_after = len(after)

    def body(x_ref, *refs):
        out_ref, send_sems, recv_sems, local_sem = refs[n_after:]
        x, y, c = lax.axis_index("x"), lax.axis_index("y"), lax.axis_index("c")
        me, sibling = (x, y, c), (x, y, 1 - c)
        chips = [(1 - x, y), (x, 1 - y), (1 - x, 1 - y)]

        def slot(px, py, pc):
            return out_ref.at[4 * px + 2 * py + pc]

        def copy(k, block, to, src=None):
            return pltpu.make_async_remote_copy(
                src_ref=slot(*block) if src is None else src, dst_ref=slot(*block),
                send_sem=send_sems.at[k], recv_sem=recv_sems.at[k], device_id=to, device_id_type=_MESH)

        mine = pltpu.make_async_copy(x_ref, slot(*me), local_sem)
        mine.start()
        first = [copy(0, me, sibling, src=x_ref)]
        first += [copy(1 + j, me, (*chip, c), src=x_ref) for j, chip in enumerate(chips)]
        for cp in first:
            cp.start()
        passed = [copy(4 + j, (*chip, c), sibling) for j, chip in enumerate(chips)]
        for j, chip in enumerate(chips):
            copy(1 + j, (*chip, c), me).wait_recv()
            passed[j].start()
        copy(0, sibling, me).wait_recv()
        for j, chip in enumerate(chips):
            copy(4 + j, (*chip, 1 - c), me).wait_recv()
        for cp in first + passed:
            cp.wait_send()
        mine.wait()

    return pl.pallas_call(
        body, name=name, out_shape=jax.ShapeDtypeStruct((N_DEV,) + x_local.shape, x_local.dtype),
        in_specs=[_ANY] * (1 + n_after), out_specs=_ANY,
        scratch_shapes=[pltpu.SemaphoreType.DMA((7,)), pltpu.SemaphoreType.DMA((7,)), pltpu.SemaphoreType.DMA],
    )(x_local, *after)


_HBM = pl.BlockSpec(memory_space=pltpu.HBM)
_SEM = pl.BlockSpec(memory_space=pltpu.SEMAPHORE)
_EFFECT = pltpu.SideEffectType.DATAFLOW_SIDE_EFFECTING


def _peers():
    x, y, c = lax.axis_index("x"), lax.axis_index("y"), lax.axis_index("c")
    out = []
    for k in range(1, N_DEV):
        px, py, pc = x ^ ((k >> 2) & 1), y ^ ((k >> 1) & 1), c ^ (k & 1)
        out.append(((px, py, pc), 4 * px + 2 * py + pc))
    return 4 * x + 2 * y + c, out


def _push_copies(src_ref, land_ref, send_sems, recv_sems, broadcast):
    my, peers = _peers()
    pairs = []
    for k, (pos, idx) in enumerate(peers):
        src = src_ref if broadcast else src_ref.at[idx]
        send = pltpu.make_async_remote_copy(src_ref=src, dst_ref=land_ref.at[my], send_sem=send_sems.at[k],
                                            recv_sem=recv_sems.at[k], device_id=pos, device_id_type=_MESH)
        recv = pltpu.make_async_remote_copy(src_ref=src, dst_ref=land_ref.at[idx], send_sem=send_sems.at[k],
                                            recv_sem=recv_sems.at[k], device_id=pos, device_id_type=_MESH)
        pairs.append((send, recv))
    return pairs


def _push_start(src, land, *, broadcast, name, after=()):
    n_after = len(after)

    def body(src_ref, land_ref, *refs):
        send_sems, recv_sems, _, _, token = refs[n_after:]
        for send, _ in _push_copies(src_ref, land_ref, send_sems, recv_sems, broadcast):
            send.start()
        token[...] = jnp.zeros_like(token)

    return pl.pallas_call(
        body, name=name,
        out_shape=(pltpu.SemaphoreType.DMA((N_DEV - 1,)), pltpu.SemaphoreType.DMA((N_DEV - 1,)),
                   pltpu.HBM(src.shape, src.dtype), pltpu.HBM(land.shape, land.dtype), jax.ShapeDtypeStruct((8, 128), F32)),
        in_specs=(_HBM, _HBM) + (_ANY,) * n_after, out_specs=(_SEM, _SEM, _HBM, _HBM, pl.BlockSpec(memory_space=pltpu.VMEM)),
        input_output_aliases={0: 2, 1: 3}, compiler_params=pltpu.CompilerParams(has_side_effects=_EFFECT),
    )(pltpu.with_memory_space_constraint(src, pltpu.HBM), pltpu.with_memory_space_constraint(land, pltpu.HBM), *after)


def _push_wait(handle, after, *, broadcast, name):
    send_sems, recv_sems, src_thru, land_thru, _ = handle

    def body(src_ref, land_ref, send_sems, recv_sems, *rest):
        for send, recv in _push_copies(src_ref, land_ref, send_sems, recv_sems, broadcast):
            send.wait_send()
            recv.wait_recv()

    return pl.pallas_call(
        body, name=name,
        out_shape=(pltpu.HBM(src_thru.shape, src_thru.dtype), pltpu.HBM(land_thru.shape, land_thru.dtype)),
        in_specs=(_HBM, _HBM, _SEM, _SEM) + (_ANY,) * len(after), out_specs=(_HBM, _HBM),
        input_output_aliases={0: 0, 1: 1}, compiler_params=pltpu.CompilerParams(has_side_effects=_EFFECT),
    )(src_thru, land_thru, send_sems, recv_sems, *after)[1]


def _relay_copies(src_ref, land_ref, sems_a, sems_b):
    x, y, c = lax.axis_index("x"), lax.axis_index("y"), lax.axis_index("c")
    slot = lambda px, py, pc: land_ref.at[4 * px + 2 * py + pc]
    chips = [(1 - x, y), (x, 1 - y), (1 - x, 1 - y)]
    (send_a, recv_a), (send_b, recv_b) = sems_a, sems_b

    def copy(sems, k, src, dst_slot, to):
        return pltpu.make_async_remote_copy(src_ref=src, dst_ref=dst_slot, send_sem=sems[0].at[k], recv_sem=sems[1].at[k],
                                            device_id=to, device_id_type=_MESH)

    first = [copy((send_a, recv_a), 0, src_ref, slot(x, y, c), (x, y, 1 - c))]
    first += [copy((send_a, recv_a), 1 + j, src_ref, slot(x, y, c), (*chip, c)) for j, chip in enumerate(chips)]
    first_in = [copy((send_a, recv_a), 0, src_ref, slot(x, y, 1 - c), (x, y, 1 - c))]
    first_in += [copy((send_a, recv_a), 1 + j, src_ref, slot(*chip, c), (*chip, c)) for j, chip in enumerate(chips)]
    relay = [copy((send_b, recv_b), j, slot(*chip, c), slot(*chip, c), (x, y, 1 - c)) for j, chip in enumerate(chips)]
    relay_in = [copy((send_b, recv_b), j, slot(*chip, 1 - c), slot(*chip, 1 - c), (x, y, 1 - c)) for j, chip in enumerate(chips)]
    return first, first_in, relay, relay_in


def _relay_start(src, land, *, name, after=()):
    n_after = len(after)

    def body(src_ref, land_ref, *refs):
        send_a, recv_a, _, _, token = refs[n_after:]
        for cp in _relay_copies(src_ref, land_ref, (send_a, recv_a), (send_a, recv_a))[0]:
            cp.start()
        token[...] = jnp.zeros_like(token)

    send_a, recv_a, src_thru, land_thru, token = pl.pallas_call(
        body, name=name,
        out_shape=(pltpu.SemaphoreType.DMA((4,)), pltpu.SemaphoreType.DMA((4,)), pltpu.HBM(src.shape, src.dtype),
                   pltpu.HBM(land.shape, land.dtype), jax.ShapeDtypeStruct((8, 128), F32)),
        in_specs=(_HBM, _HBM) + (_ANY,) * n_after, out_specs=(_SEM, _SEM, _HBM, _HBM, pl.BlockSpec(memory_space=pltpu.VMEM)),
        input_output_aliases={0: 2, 1: 3}, compiler_params=pltpu.CompilerParams(has_side_effects=_EFFECT),
    )(pltpu.with_memory_space_constraint(src, pltpu.HBM), pltpu.with_memory_space_constraint(land, pltpu.HBM), *after)
    return (send_a, recv_a), src_thru, land_thru, token


def _relay_mid(handle, after, *, name):
    sems_a, src_thru, land_thru, _ = handle
    n_after = len(after)

    def body(src_ref, land_ref, send_a, recv_a, *refs):
        send_b, recv_b, _, _, token = refs[n_after:]
        _, first_in, relay, _ = _relay_copies(src_ref, land_ref, (send_a, recv_a), (send_b, recv_b))
        for j in range(3):
            first_in[1 + j].wait_recv()
            relay[j].start()
        token[...] = jnp.zeros_like(token)

    send_b, recv_b, src2, land2, token = pl.pallas_call(
        body, name=name,
        out_shape=(pltpu.SemaphoreType.DMA((3,)), pltpu.SemaphoreType.DMA((3,)), pltpu.HBM(src_thru.shape, src_thru.dtype),
                   pltpu.HBM(land_thru.shape, land_thru.dtype), jax.ShapeDtypeStruct((8, 128), F32)),
        in_specs=(_HBM, _HBM, _SEM, _SEM) + (_ANY,) * n_after,
        out_specs=(_SEM, _SEM, _HBM, _HBM, pl.BlockSpec(memory_space=pltpu.VMEM)),
        input_output_aliases={0: 2, 1: 3}, compiler_params=pltpu.CompilerParams(has_side_effects=_EFFECT),
    )(src_thru, land_thru, *sems_a, *after)
    return sems_a, (send_b, recv_b), src2, land2, token


def _relay_wait(handle, after, *, name):
    sems_a, sems_b, src_thru, land_thru, _ = handle

    def body(src_ref, land_ref, send_a, recv_a, send_b, recv_b, *rest):
        first, first_in, relay, relay_in = _relay_copies(src_ref, land_ref, (send_a, recv_a), (send_b, recv_b))
        first_in[0].wait_recv()
        for cp in relay_in:
            cp.wait_recv()
        for cp in first + relay:
            cp.wait_send()

    return pl.pallas_call(
        body, name=name,
        out_shape=(pltpu.HBM(src_thru.shape, src_thru.dtype), pltpu.HBM(land_thru.shape, land_thru.dtype)),
        in_specs=(_HBM, _HBM, _SEM, _SEM, _SEM, _SEM) + (_ANY,) * len(after), out_specs=(_HBM, _HBM),
        input_output_aliases={0: 0, 1: 1}, compiler_params=pltpu.CompilerParams(has_side_effects=_EFFECT),
    )(src_thru, land_thru, *sems_a, *sems_b, *after)[1]


def _adamw(parts, row_off, w, m, v, *, layer=0, n_layers=1, prev=None, name, tr):
    rows, c = w.shape
    r = rows // n_layers
    np_ = parts.shape[0]
    tr = min(tr, r)
    assert r % tr == 0 and row_off % tr == 0
    ob, lb = row_off // tr, layer * (r // tr)
    c1 = 1.0 - ADAM_B1 ** ADAM_STEP
    c2 = 1.0 - ADAM_B2 ** ADAM_STEP
    n_prev = 0 if prev is None else 4

    def body(p_ref, w_ref, m_ref, v_ref, *refs):
        g_ref, d_ref, nm_ref, nv_ref = refs[n_prev:]
        g = p_ref[0].astype(F32)
        for s in range(1, np_):
            g = g + p_ref[s].astype(F32)
        wv = w_ref[...]
        m2 = ADAM_B1 * m_ref[...] + (1.0 - ADAM_B1) * g
        v2 = ADAM_B2 * v_ref[...] + (1.0 - ADAM_B2) * jnp.square(g)
        m_hat = m2 / c1
        v_hat = v2 / c2
        g_ref[...] = g
        d_ref[...] = -ADAM_LR * (m_hat / (jnp.sqrt(v_hat) + ADAM_EPS) + ADAM_WD * wv)
        nm_ref[...] = m2
        nv_ref[...] = v2

    blk = pl.BlockSpec((tr, c), lambda i: (lb + i, 0))
    return pl.pallas_call(
        body, name=name, grid=(r // tr,),
        in_specs=[pl.BlockSpec((np_, tr, c), lambda i: (0, ob + i, 0)), blk, blk, blk] + [_ANY] * n_prev,
        out_specs=[blk] * 4, out_shape=[jax.ShapeDtypeStruct((rows, c), F32)] * 4,
        input_output_aliases={4 + i: i for i in range(n_prev)}, compiler_params=_cp("parallel"),
    )(parts, w, m, v, *(prev or ()))


def _sum_parts(parts, *, name, after=()):
    np_, r, c = parts.shape

    def body(p_ref, *refs):
        o_ref = refs[-1]
        g = p_ref[0]
        for s in range(1, np_):
            g = g + p_ref[s]
        o_ref[...] = g

    vmem = pl.BlockSpec(memory_space=pltpu.VMEM)
    return pl.pallas_call(body, name=name, in_specs=[vmem] + [_ANY] * len(after), out_specs=vmem,
                          out_shape=jax.ShapeDtypeStruct((r, c), F32))(parts, *after)


def _pack(arrs):
    rows = []
    for a in arrs:
        f = a.reshape(-1).astype(F32)
        pad = (-f.shape[0]) % 128
        rows.append(jnp.pad(f, (0, pad)).reshape(-1, 128))
    out = jnp.concatenate(rows, axis=0)
    return jnp.pad(out, ((0, (-out.shape[0]) % 8), (0, 0)))


def _unpack(packed, shapes):
    outs, r0 = [], 0
    for shp in shapes:
        n = 1
        for d in shp:
            n *= d
        nr = -(-n // 128)
        outs.append(packed[r0:r0 + nr].reshape(-1)[:n].reshape(shp))
        r0 += nr
    return outs


_WIN_PIECES = ((0, 4096, 0), (4112, 8208, 0), (4096, 4104, HEAD_DIM - N_HEADS), (4104, 4112, HEAD_DIM - N_HEADS))


RELAYOUT_TILE = 256


def _win_from_shards(shards, *, name):
    k = shards.shape[1]
    tr = min(RELAYOUT_TILE, k)

    def body(x_ref, o_ref):
        cols = []
        for lo, hi, pad in _WIN_PIECES:
            for j in range(N_DEV):
                a, b = max(lo, j * SHARD_IN), min(hi, (j + 1) * SHARD_IN)
                if a < b:
                    cols.append(x_ref[j, :, a - j * SHARD_IN:b - j * SHARD_IN])
            if pad:
                cols.append(jnp.zeros((tr, pad), x_ref.dtype))
        o_ref[...] = jnp.concatenate(cols, axis=1)

    return pl.pallas_call(
        body, name=name, grid=(k // tr,), in_specs=[pl.BlockSpec((N_DEV, tr, SHARD_IN), lambda i: (0, i, 0))],
        out_specs=pl.BlockSpec((tr, N_PROJ), lambda i: (i, 0)), out_shape=jax.ShapeDtypeStruct((k, N_PROJ), shards.dtype),
        compiler_params=_cp("parallel"),
    )(shards)


def _win_to_shards(g, *, name):
    k = g.shape[0]
    tr = min(RELAYOUT_TILE, k)
    starts, off = [], 0
    for lo, hi, pad in _WIN_PIECES:
        starts.append((lo, hi, off))
        off += hi - lo + pad

    def body(g_ref, o_ref):
        for j in range(N_DEV):
            cols = []
            for lo, hi, off in sorted(starts):
                a, b = max(lo, j * SHARD_IN), min(hi, (j + 1) * SHARD_IN)
                if a < b:
                    cols.append(g_ref[:, off + a - lo:off + b - lo])
            o_ref[j] = jnp.concatenate(cols, axis=1)

    return pl.pallas_call(
        body, name=name, grid=(k // tr,), in_specs=[pl.BlockSpec((tr, N_PROJ), lambda i: (i, 0))],
        out_specs=pl.BlockSpec((N_DEV, tr, SHARD_IN), lambda i: (0, i, 0)),
        out_shape=jax.ShapeDtypeStruct((N_DEV, k, SHARD_IN), g.dtype), compiler_params=_cp("parallel"),
    )(g)


def _lower_bounds(logits):
    probs = jax.nn.softmax(logits.astype(F32), axis=0)
    return jnp.cumsum(probs, axis=0) - probs[0]


def _pad_lanes(vec8):
    return jnp.pad(vec8.reshape(1, N_HEADS), ((0, 0), (0, HEAD_DIM - N_HEADS)))


def kernel(x, p, norm_w, w_in, dn_conv_w, dn_A_log, dn_dt_bias, dn_norm_w, hg_lb_logits, hg_norm_w, w_out, w_ple_up, w_ple_gate, final_norm_w, loss_target, m_norm_w, m_w_in, m_dn_conv_w, m_dn_A_log, m_dn_dt_bias, m_dn_norm_w, m_hg_lb_logits, m_hg_norm_w, m_w_out, m_w_ple_up, m_w_ple_gate, m_final_norm_w, v_norm_w, v_w_in, v_dn_conv_w, v_dn_A_log, v_dn_dt_bias, v_dn_norm_w, v_hg_lb_logits, v_hg_norm_w, v_w_out, v_w_ple_up, v_w_ple_gate, v_final_norm_w):
    depth = norm_w.shape[0]
    my = 4 * lax.axis_index("x") + 2 * lax.axis_index("y") + lax.axis_index("c")
    h = x[0]
    tgt = loss_target[0]
    rows_out = D_MODEL // N_DEV
    up_rows = PLE_DIM * (D_MODEL // N_DEV) // D_MODEL
    g_off, u_off = rows_out, 2 * rows_out

    def own_slot(block):
        return lax.dynamic_update_index_in_dim(lax.empty((N_DEV,) + block.shape, block.dtype), block, my, 0)

    win_bf = w_in.astype(BF16)
    rest_bf = [jnp.concatenate([w_out[l], w_ple_gate[l], w_ple_up[l].reshape(up_rows, D_MODEL)], axis=0).astype(BF16)
               for l in range(depth)]
    conv_all = _all_gather(dn_conv_w, name="gather_conv_w")
    conv_full = conv_all.transpose(1, 2, 0, 3).reshape(depth, CONV_W, 3 * BR_WIDTH)
    win_all = {0: _all_gather(win_bf[0], name="gather_w_in_l0", after=[conv_all])}
    pending, relayed = {}, {}
    last = win_all[0]
    for l in range(depth):
        if l > 0:
            relayed["win", l] = _relay_start(win_bf[l], own_slot(win_bf[l]), after=[last], name=f"gather_w_in_l{l}_first")
            last = relayed["win", l][3]
        if l == 0:
            relayed["rest", l] = _relay_start(rest_bf[l], own_slot(rest_bf[l]), after=[last], name=f"gather_rest_l{l}_first")
            last = relayed["rest", l][3]
        else:
            pending["rest", l] = _push_start(rest_bf[l], own_slot(rest_bf[l]), broadcast=True, after=[last],
                                             name=f"gather_rest_l{l}_start")
            last = pending["rest", l][4]
    order_tok = last[0, 0]
    lbs = _lower_bounds(hg_lb_logits)

    saved = []
    weights = []
    for l in range(depth):
        tag = f"l{l}"
        if l > 0:
            win_all[l] = _relay_wait(relayed["win", l], [h], name=f"gather_w_in_{tag}_wait")
        wi = _win_from_shards(win_all[l], name=f"w_in_layout_{tag}")
        nw = norm_w[l] + order_tok if l == 0 else norm_w[l]
        hn = _rms_fwd(h, nw, name=f"rms_fwd_{tag}")
        proj = _mm(hn, wi, mode="nn", out_dtype=F32, name=f"mm_proj_{tag}")
        al, dt = _pad_lanes(dn_A_log[l]), _pad_lanes(dn_dt_bias[l])
        qkv = _dn_qkv_fwd(proj, conv_full[l], name=f"dn_qkv_fwd_{tag}")
        if ("rest", l) in relayed:
            relayed["rest", l] = _relay_mid(relayed["rest", l], [qkv], name=f"gather_rest_{tag}_relay")
            al = al + relayed["rest", l][4][0, 0]
        beta, gcs = _dn_gate_fwd(proj, al, dt, name=f"dn_gate_fwd_{tag}")
        o_dn, st_dn = _dn_chunk_fwd(qkv, gcs, beta, name=f"dn_chunk_fwd_{tag}")
        lb = lbs[l].reshape(1, BR_WIDTH)
        qh, kh, lf = _hg_prep_fwd(proj, lb, name=f"hg_prep_fwd_{tag}")
        o_hg, st_hg = _hg_chunk_fwd(qh, kh, proj, lf, name=f"hg_chunk_fwd_{tag}")
        y_dn = _hnorm_fwd(o_dn, proj, C_Z, dn_norm_w[l], name=f"hnorm_dn_fwd_{tag}")
        y_hg = _hnorm_fwd(o_hg, proj, C_HZ, hg_norm_w[l], name=f"hnorm_hg_fwd_{tag}")
        y = jnp.concatenate([y_dn, y_hg], axis=1)
        if ("rest", l) in relayed:
            rest_all = _relay_wait(relayed["rest", l], [y], name=f"gather_rest_{tag}_wait")
        else:
            rest_all = _push_wait(pending["rest", l], [y], broadcast=True, name=f"gather_rest_{tag}_wait")
        wo = rest_all[:, 0:rows_out].reshape(D_MODEL, D_MODEL)
        wg = rest_all[:, g_off:g_off + rows_out].reshape(D_MODEL, D_MODEL)
        wu = rest_all[:, u_off:u_off + up_rows].reshape(N_DEV, PLE_DIM, D_MODEL // N_DEV).transpose(1, 0, 2).reshape(PLE_DIM, D_MODEL)
        weights.append((wi, wo, wg, wu))
        h1 = _mm(y, wo, mode="nn", out_dtype=F32, res=h, name=f"mm_out_{tag}")
        pin = []
        if ("win", l + 1) in relayed:
            relayed["win", l + 1] = _relay_mid(relayed["win", l + 1], [h1], name=f"gather_w_in_l{l + 1}_relay")
            pin = [relayed["win", l + 1][4]]
        gp = _mm(h1, wg, mode="nn", out_dtype=F32, after=pin, name=f"mm_gate_{tag}")
        up = _mm(p[l, 0], wu, mode="nn", out_dtype=F32, name=f"mm_up_{tag}")
        h2 = _ple_fwd(h1, gp, up, name=f"ple_fwd_{tag}")
        saved.append(dict(h=h, hn=hn, proj=proj, qkv=qkv, beta=beta, gcs=gcs, st_dn=st_dn, qh=qh, kh=kh, lf=lf,
                          st_hg=st_hg, o_dn=o_dn, o_hg=o_hg, y=y, h1=h1, gp=gp, up=up, al=al, dt=dt, lb=lb))
        h = h2

    loss_row, dh, d_final_w = _final_fwd_bwd(h, final_norm_w, tgt, name="final_norm_loss")

    d_norm_w, d_alog, d_dt, d_dn_nw, d_hg_nw, d_lb, d_conv = ([None] * depth for _ in range(7))
    sent = {}
    for l in reversed(range(depth)):
        wi, wo, wg, wu = weights[l]
        sv = saved[l]
        tag = f"l{l}"
        dup, dgp = _ple_bwd(dh, sv["gp"], sv["up"], name=f"ple_bwd_{tag}")
        d_wu = _mm(p[l, 0], dup, mode="tn", out_dtype=BF16, name=f"mm_dwup_{tag}")
        d_wg = _mm(sv["h1"], dgp, mode="tn", out_dtype=BF16, name=f"mm_dwgate_{tag}")
        dh1 = _mm(dgp, wg, mode="nt", out_dtype=F32, res=dh, name=f"mm_dh1_{tag}")
        d_wo = _mm(sv["y"], dh1, mode="tn", out_dtype=BF16, name=f"mm_dwout_{tag}")
        parts_rest = jnp.concatenate(
            [d_wo.reshape(N_DEV, rows_out, D_MODEL), d_wg.reshape(N_DEV, rows_out, D_MODEL),
             d_wu.reshape(PLE_DIM, N_DEV, D_MODEL // N_DEV).transpose(1, 0, 2).reshape(N_DEV, up_rows, D_MODEL)], axis=1)
        sent["rest", l] = _push_start(parts_rest, own_slot(parts_rest[my]), broadcast=False, name=f"exchange_rest_{tag}_start")
        dy = _mm(dh1, wo, mode="nt", out_dtype=F32, name=f"mm_dy_{tag}")
        dn_nw = dn_norm_w[l] + sent["rest", l][4][0, 0]
        do_dn, dz_dn, d_dn_nw[l] = _hnorm_bwd(sv["o_dn"], sv["proj"], C_Z, dn_nw, dy, 0, name=f"hnorm_dn_bwd_{tag}")
        do_hg, dz_hg, d_hg_nw[l] = _hnorm_bwd(sv["o_hg"], sv["proj"], C_HZ, hg_norm_w[l], dy, BR_WIDTH, name=f"hnorm_hg_bwd_{tag}")
        dqkv, d_gc, dbeta = _dn_chunk_bwd(sv["qkv"], sv["gcs"], sv["beta"], sv["st_dn"], do_dn, name=f"dn_chunk_bwd_{tag}")
        dqkv_pre, d_conv[l] = _dn_qkv_bwd(sv["proj"], conv_full[l], dqkv, name=f"dn_qkv_bwd_{tag}")
        db, da, d_alog[l], d_dt[l] = _dn_gate_bwd(sv["proj"], sv["al"], sv["dt"], dbeta, d_gc, name=f"dn_gate_bwd_{tag}")
        dqh, dkh, dhi, dlf = _hg_chunk_bwd(sv["qh"], sv["kh"], sv["proj"], sv["lf"], sv["st_hg"], do_hg, name=f"hg_chunk_bwd_{tag}")
        dhq, dhf, d_lb[l] = _hg_prep_bwd(sv["proj"], sv["lb"], dqh, dkh, dlf, name=f"hg_prep_bwd_{tag}")
        dproj = jnp.concatenate([dqkv_pre, dz_dn, dhq, dhf, dhi, dz_hg, db, da], axis=1)
        def push_d_win(after):
            d_win = _mm(sv["hn"], dproj, mode="tn", out_dtype=BF16, after=after, name=f"mm_dwin_{tag}")
            parts_in = _win_to_shards(d_win, name=f"dw_in_shards_{tag}")
            return _push_start(parts_in, own_slot(parts_in[my]), broadcast=False, name=f"exchange_w_in_{tag}_start")

        if l > 0:
            sent["win", l] = push_d_win([])
            dhn = _mm(dproj, wi, mode="nt", out_dtype=F32, after=[sent["win", l][4]], name=f"mm_dhn_{tag}")
            dh, d_norm_w[l] = _rms_bwd(sv["h"], norm_w[l], dhn, dh1, name=f"rms_bwd_{tag}")
        else:
            dhn = _mm(dproj, wi, mode="nt", out_dtype=F32, name=f"mm_dhn_{tag}")
            dh, d_norm_w[l] = _rms_bwd(sv["h"], norm_w[l], dhn, dh1, name=f"rms_bwd_{tag}")
            small = _pack([loss_row, jnp.concatenate(d_norm_w, axis=0), d_final_w,
                           jnp.stack([a[0, :N_HEADS] for a in d_alog]), jnp.stack([a[0, :N_HEADS] for a in d_dt]),
                           jnp.concatenate(d_dn_nw, axis=0), jnp.concatenate(d_hg_nw, axis=0), jnp.concatenate(d_lb, axis=0),
                           jnp.stack(d_conv)])
            small_all = _all_gather(small, name="gather_small")
            sent["win", l] = push_d_win([small_all])
    grad_x = dh[None]

    small_shapes = [(1, 128), norm_w.shape, final_norm_w.shape, dn_A_log.shape, dn_dt_bias.shape, dn_norm_w.shape,
                    hg_norm_w.shape, hg_lb_logits.shape, (depth, CONV_W, 3 * BR_WIDTH)]
    tot = _unpack(_sum_parts(small_all, after=[sent["win", 0][4]], name="sum_small"), small_shapes)
    loss = tot[0][0, 0]
    g_lb = tot[7]
    g_logits = jax.vjp(_lower_bounds, hg_lb_logits)[1](g_lb)[0]
    g_conv = lax.dynamic_slice_in_dim(tot[8], my * (3 * BR_WIDTH // N_DEV), 3 * BR_WIDTH // N_DEV, axis=2)
    small_g = [tot[1], g_conv, tot[3], tot[4], tot[5], g_logits, tot[6], tot[2]]
    small_w = [norm_w, dn_conv_w, dn_A_log, dn_dt_bias, dn_norm_w, hg_lb_logits, hg_norm_w, final_norm_w]
    small_m = [m_norm_w, m_dn_conv_w, m_dn_A_log, m_dn_dt_bias, m_dn_norm_w, m_hg_lb_logits, m_hg_norm_w, m_final_norm_w]
    small_v = [v_norm_w, v_dn_conv_w, v_dn_A_log, v_dn_dt_bias, v_dn_norm_w, v_hg_lb_logits, v_hg_norm_w, v_final_norm_w]
    pk_w = _pack(small_w)
    res_small = _adamw(_pack(small_g)[None], 0, pk_w, _pack(small_m), _pack(small_v), name="adamw_small", tr=pk_w.shape[0])
    shapes_w = [a.shape for a in small_w]
    sg, sd, sm, sv_ = (_unpack(r, shapes_w) for r in res_small)

    r_win = r_wo = r_wg = r_wu = None
    done = [grad_x, res_small[0]]

    def flat(a, cols):
        return a.reshape(-1, cols)

    for l in reversed(range(depth)):
        tag = f"l{l}"
        land_rest = _push_wait(sent["rest", l], done, broadcast=False, name=f"exchange_rest_{tag}_wait")
        r_wo = _adamw(land_rest, 0, flat(w_out, D_MODEL), flat(m_w_out, D_MODEL), flat(v_w_out, D_MODEL), layer=l,
                      n_layers=depth, prev=r_wo, name=f"adamw_w_out_{tag}", tr=rows_out)
        r_wg = _adamw(land_rest, g_off, flat(w_ple_gate, D_MODEL), flat(m_w_ple_gate, D_MODEL), flat(v_w_ple_gate, D_MODEL),
                      layer=l, n_layers=depth, prev=r_wg, name=f"adamw_w_gate_{tag}", tr=rows_out)
        r_wu = _adamw(land_rest, u_off, flat(w_ple_up, D_MODEL), flat(m_w_ple_up, D_MODEL), flat(v_w_ple_up, D_MODEL),
                      layer=l, n_layers=depth, prev=r_wu, name=f"adamw_w_up_{tag}", tr=up_rows)
        done = [r_wo[0], r_wg[0], r_wu[0]]
    for l in reversed(range(depth)):
        tag = f"l{l}"
        land_in = _push_wait(sent["win", l], done, broadcast=False, name=f"exchange_w_in_{tag}_wait")
        r_win = _adamw(land_in, 0, flat(w_in, SHARD_IN), flat(m_w_in, SHARD_IN), flat(v_w_in, SHARD_IN), layer=l,
                       n_layers=depth, prev=r_win, name=f"adamw_w_in_{tag}", tr=256)
        done = [r_win[0]]
    r_win = [o.reshape(w_in.shape) for o in r_win]
    r_wo = [o.reshape(w_out.shape) for o in r_wo]
    r_wg = [o.reshape(w_ple_gate.shape) for o in r_wg]
    r_wu = [o.reshape(w_ple_up.shape) for o in r_wu]

    def order(small_list, big_in, big_out, big_up, big_gate):
        nw, cw, al_, dt_, dnw, lbl, hnw, fw = small_list
        return [nw, big_in, cw, al_, dt_, dnw, lbl, hnw, big_out, big_up, big_gate, fw]

    outs = [loss, grad_x]
    for i, sl in enumerate((sg, sd, sm, sv_)):
        outs += order(sl, r_win[i], r_wo[i], r_wu[i], r_wg[i])
    return tuple(outs)
```

```python
import functools

import jax
import jax.numpy as jnp
from jax import lax
from jax.experimental import pallas as pl
from jax.experimental.pallas import tpu as pltpu

F32 = jnp.float32
BF16 = jnp.bfloat16
HIGHEST = lax.Precision.HIGHEST

N_DEV = 8
D_MODEL = 2048
PLE_DIM = 256
HEAD_DIM = 128
N_HEADS = 8
BR_WIDTH = N_HEADS * HEAD_DIM
CHUNK = 64
SUB = 16
CONV_W = 4
NORM_EPS = 1e-6
L2_EPS = 1e-6
IN_WIDTH = 8208
SHARD_IN = IN_WIDTH // N_DEV
EXP_CLAMP = 80.0

C_QKV, C_Z, C_HQ, C_HF, C_HI, C_HZ, C_B, C_A, N_PROJ = 0, 3072, 4096, 5120, 6144, 7168, 8192, 8320, 8448

ADAM_LR, ADAM_B1, ADAM_B2, ADAM_EPS, ADAM_WD, ADAM_STEP = 0.001, 0.9, 0.999, 1e-08, 0.01, 10

VMEM_LIMIT = 48 * 1024 * 1024


def _cp(*sem):
    return pltpu.CompilerParams(dimension_semantics=sem, vmem_limit_bytes=VMEM_LIMIT)


class _Heads:
    def __init__(self, vals):
        self.v = tuple(vals)

    def __add__(self, o):
        return _hmap(lambda a, b: a + b, self, o)

    def __radd__(self, o):
        return _hmap(lambda a, b: b + a, self, o)

    def __sub__(self, o):
        return _hmap(lambda a, b: a - b, self, o)

    def __rsub__(self, o):
        return _hmap(lambda a, b: b - a, self, o)

    def __mul__(self, o):
        return _hmap(lambda a, b: a * b, self, o)

    def __rmul__(self, o):
        return _hmap(lambda a, b: b * a, self, o)

    def __neg__(self):
        return _hmap(lambda a: -a, self)

    def __getitem__(self, idx):
        return _hmap(lambda a: a[idx], self)


def _hmap(fn, *args):
    n = next((len(a.v) for a in args if isinstance(a, _Heads)), None)
    if n is None:
        return fn(*args)
    return _Heads(fn(*[a.v[i] if isinstance(a, _Heads) else a for a in args]) for i in range(n))


def _dot(a, b, ca, cb):
    return _hmap(lambda x, y: lax.dot_general(x.astype(BF16), y.astype(BF16), (((ca,), (cb,)), ((), ())),
                                              preferred_element_type=F32), a, b)


def _nn(a, b):
    return _dot(a, b, 1, 0)


def _nt(a, b):
    return _dot(a, b, 1, 1)


def _tn(a, b):
    return _dot(a, b, 0, 0)


def _split(a):
    hi = _hmap(lambda x: x.astype(BF16), a)
    return hi, _hmap(lambda x, h: (x - h.astype(F32)).astype(BF16), a, hi)


def _dot3(a, b, ca, cb):
    ah, al = _split(a)
    bh, bl = _split(b)
    return _dot(ah, bh, ca, cb) + (_dot(ah, bl, ca, cb) + _dot(al, bh, ca, cb))


def _nn_exact(a, b):
    return _hmap(lambda y: lax.dot_general(a, y, (((1,), (0,)), ((), ())), precision=HIGHEST,
                                           preferred_element_type=F32), b)


def _exp(x):
    return _hmap(jnp.exp, x)


def _sum(x, axis):
    return _hmap(lambda a: jnp.sum(a, axis=axis, keepdims=True), x)


def _sigmoid(x):
    return jax.nn.sigmoid(x)


def _silu(x):
    return x * _sigmoid(x)


def _dsilu(x):
    s = _sigmoid(x)
    return s * (1.0 + x * (1.0 - s))


def _silu_and_grad(x):
    s = _sigmoid(x)
    return x * s, s * (1.0 + x * (1.0 - s))


def _softplus(x):
    return jnp.maximum(x, 0.0) + jnp.log(1.0 + jnp.exp(-jnp.abs(x)))


def _iota2(n, m, axis):
    return lax.broadcasted_iota(jnp.int32, (n, m), axis)


def _col2row(col, eye):
    return _hmap(lambda c: jnp.sum(eye * c, axis=0, keepdims=True), col)


def _row2col(row, eye):
    return _hmap(lambda r: jnp.sum(eye * r, axis=1, keepdims=True), row)


def _pick_lane(block, lane_idx):
    lane = _iota2(block.shape[0], block.shape[1], 1)
    return jnp.sum(jnp.where(lane == lane_idx, block, 0.0), axis=1, keepdims=True)


MM_TILE_M, MM_TILE_N, MM_TILE_K = 1024, 1408, 2048


def _tile(dim, cap):
    if dim <= cap:
        return dim
    t = cap - cap % 128
    while dim % t:
        t -= 128
    return t


def _mm(a, b, *, mode, out_dtype, res=None, after=(), name):
    if mode == "nn":
        (m, kd), (_, n) = a.shape, b.shape
    elif mode == "nt":
        (m, kd), (n, _) = a.shape, b.shape
    else:
        (kd, m), (_, n) = a.shape, b.shape
    tm, tn, tk = _tile(m, MM_TILE_M), _tile(n, MM_TILE_N), _tile(kd, MM_TILE_K)
    assert m % tm == 0 and n % tn == 0 and kd % tk == 0, (m, n, kd, tm, tn, tk)
    nk = kd // tk
    ca, cb = {"nn": (1, 0), "nt": (1, 1), "tn": (0, 0)}[mode]

    def body(*refs):
        a_ref, b_ref = refs[:2]
        r_ref = None if res is None else refs[2]
        o_ref, acc_ref = refs[-2:]
        k = pl.program_id(2)

        @pl.when(k == 0)
        def _():
            acc_ref[...] = jnp.zeros_like(acc_ref)

        acc_ref[...] += _dot(a_ref[...], b_ref[...], ca, cb)

        @pl.when(k == nk - 1)
        def _():
            out = acc_ref[...]
            if r_ref is not None:
                out = out + r_ref[...].astype(F32)
            o_ref[...] = out.astype(o_ref.dtype)

    a_spec = pl.BlockSpec((tk, tm), lambda i, j, k: (k, i)) if mode == "tn" else pl.BlockSpec((tm, tk), lambda i, j, k: (i, k))
    b_spec = pl.BlockSpec((tn, tk), lambda i, j, k: (j, k)) if mode == "nt" else pl.BlockSpec((tk, tn), lambda i, j, k: (k, j))
    o_spec = pl.BlockSpec((tm, tn), lambda i, j, k: (i, j))
    in_specs = [a_spec, b_spec] + ([o_spec] if res is not None else []) + [pl.BlockSpec(memory_space=pl.ANY)] * len(after)
    args = (a, b) + ((res,) if res is not None else ()) + tuple(after)
    return pl.pallas_call(
        body, name=name, grid=(m // tm, n // tn, nk), in_specs=in_specs, out_specs=o_spec,
        out_shape=jax.ShapeDtypeStruct((m, n), out_dtype),
        scratch_shapes=[pltpu.VMEM((tm, tn), F32)],
        compiler_params=_cp("parallel", "parallel", "arbitrary"),
    )(*args)


ROW_TILE = 256


def _rms_fwd(h, w, *, name):
    s, d = h.shape
    tr = min(ROW_TILE, s)

    def body(h_ref, w_ref, o_ref):
        x = h_ref[...]
        r = lax.rsqrt(jnp.mean(x * x, axis=-1, keepdims=True) + NORM_EPS)
        o_ref[...] = (x * r * w_ref[...]).astype(o_ref.dtype)

    return pl.pallas_call(
        body, name=name, grid=(s // tr,),
        in_specs=[pl.BlockSpec((tr, d), lambda i: (i, 0)), pl.BlockSpec((1, d), lambda i: (0, 0))],
        out_specs=pl.BlockSpec((tr, d), lambda i: (i, 0)),
        out_shape=jax.ShapeDtypeStruct((s, d), BF16), compiler_params=_cp("parallel"),
    )(h, w.reshape(1, d))


def _rms_bwd_math(x, w, dy):
    d = x.shape[-1]
    r = lax.rsqrt(jnp.mean(x * x, axis=-1, keepdims=True) + NORM_EPS)
    gw = dy * w
    dx = r * gw - x * ((r * r * r) * (jnp.sum(gw * x, axis=-1, keepdims=True) / d))
    return dx, dy * x * r


def _rms_bwd(h, w, dhn, res, *, name):
    s, d = h.shape
    tr = min(ROW_TILE, s)

    def body(h_ref, w_ref, g_ref, r_ref, dh_ref, dw_ref):
        @pl.when(pl.program_id(0) == 0)
        def _():
            dw_ref[...] = jnp.zeros_like(dw_ref)

        dx, dwt = _rms_bwd_math(h_ref[...], w_ref[...], g_ref[...])
        dh_ref[...] = r_ref[...] + dx
        dw_ref[...] += jnp.sum(dwt, axis=0, keepdims=True)

    row = pl.BlockSpec((tr, d), lambda i: (i, 0))
    vec = pl.BlockSpec((1, d), lambda i: (0, 0))
    return pl.pallas_call(
        body, name=name, grid=(s // tr,), in_specs=[row, vec, row, row], out_specs=[row, vec],
        out_shape=[jax.ShapeDtypeStruct((s, d), F32), jax.ShapeDtypeStruct((1, d), F32)],
        compiler_params=_cp("arbitrary"),
    )(h, w.reshape(1, d), dhn, res)


def _final_fwd_bwd(h, w, tgt, *, name):
    s, d = h.shape
    tr = min(ROW_TILE, s)

    def body(h_ref, w_ref, t_ref, loss_ref, dh_ref, dw_ref):
        @pl.when(pl.program_id(0) == 0)
        def _():
            loss_ref[...] = jnp.zeros_like(loss_ref)
            dw_ref[...] = jnp.zeros_like(dw_ref)

        x = h_ref[...]
        wv = w_ref[...]
        r = lax.rsqrt(jnp.mean(x * x, axis=-1, keepdims=True) + NORM_EPS)
        err = x * r * wv - t_ref[...]
        row_loss = jnp.mean(err * err, axis=-1, keepdims=True)
        loss_ref[...] += 0.5 * jnp.sum(row_loss, axis=0, keepdims=True)
        dx, dwt = _rms_bwd_math(x, wv, err / d)
        dh_ref[...] = dx
        dw_ref[...] += jnp.sum(dwt, axis=0, keepdims=True)

    row = pl.BlockSpec((tr, d), lambda i: (i, 0))
    vec = pl.BlockSpec((1, d), lambda i: (0, 0))
    return pl.pallas_call(
        body, name=name, grid=(s // tr,), in_specs=[row, vec, row],
        out_specs=[pl.BlockSpec((1, 128), lambda i: (0, 0)), row, vec],
        out_shape=[jax.ShapeDtypeStruct((1, 128), F32), jax.ShapeDtypeStruct((s, d), F32),
                   jax.ShapeDtypeStruct((1, d), F32)],
        compiler_params=_cp("arbitrary"),
    )(h, w.reshape(1, d), tgt)


def _ple_fwd(h1, gate_pre, up, *, name):
    s, d = h1.shape
    tr = min(ROW_TILE, s)

    def body(h_ref, g_ref, u_ref, o_ref):
        o_ref[...] = h_ref[...] + u_ref[...] * _sigmoid(g_ref[...])

    row = pl.BlockSpec((tr, d), lambda i: (i, 0))
    return pl.pallas_call(body, name=name, grid=(s // tr,), in_specs=[row, row, row], out_specs=row,
                          out_shape=jax.ShapeDtypeStruct((s, d), F32), compiler_params=_cp("parallel"))(h1, gate_pre, up)


def _ple_bwd(dh2, gate_pre, up, *, name):
    s, d = dh2.shape
    tr = min(ROW_TILE, s)

    def body(d_ref, g_ref, u_ref, dup_ref, dgp_ref):
        dh = d_ref[...]
        gate = _sigmoid(g_ref[...])
        dup_ref[...] = (dh * gate).astype(BF16)
        dgp_ref[...] = (dh * u_ref[...] * gate * (1.0 - gate)).astype(BF16)

    row = pl.BlockSpec((tr, d), lambda i: (i, 0))
    return pl.pallas_call(body, name=name, grid=(s // tr,), in_specs=[row, row, row], out_specs=[row, row],
                          out_shape=[jax.ShapeDtypeStruct((s, d), BF16)] * 2, compiler_params=_cp("parallel"))(dh2, gate_pre, up)


HN_TILE = 512


def _hnorm_fwd(o, proj, z_col, w, *, name):
    s = o.shape[0]
    tr = min(HN_TILE, s)

    def body(o_ref, z_ref, w_ref, y_ref):
        wv = w_ref[...]
        for h in range(N_HEADS):
            cols = slice(h * HEAD_DIM, (h + 1) * HEAD_DIM)
            x = o_ref[:, cols]
            r = lax.rsqrt(jnp.mean(x * x, axis=-1, keepdims=True) + NORM_EPS)
            y_ref[:, cols] = (x * r * wv * _silu(z_ref[:, cols])).astype(BF16)

    blk = pl.BlockSpec((tr, BR_WIDTH), lambda i: (i, 0))
    return pl.pallas_call(
        body, name=name, grid=(s // tr,),
        in_specs=[blk, pl.BlockSpec((tr, BR_WIDTH), lambda i: (i, z_col // BR_WIDTH)), pl.BlockSpec((1, HEAD_DIM), lambda i: (0, 0))],
        out_specs=blk, out_shape=jax.ShapeDtypeStruct((s, BR_WIDTH), BF16), compiler_params=_cp("parallel"),
    )(o, proj, w.reshape(1, HEAD_DIM))


def _hnorm_bwd(o, proj, z_col, w, dy, dy_col, *, name):
    s = o.shape[0]
    tr = min(HN_TILE, s)

    def body(o_ref, z_ref, w_ref, dy_ref, do_ref, dz_ref, dw_ref):
        @pl.when(pl.program_id(0) == 0)
        def _():
            dw_ref[...] = jnp.zeros_like(dw_ref)

        wv = w_ref[...]
        dw = jnp.zeros((1, HEAD_DIM), F32)
        for h in range(N_HEADS):
            cols = slice(h * HEAD_DIM, (h + 1) * HEAD_DIM)
            x, z, g = o_ref[:, cols], z_ref[:, cols], dy_ref[:, cols]
            r = lax.rsqrt(jnp.mean(x * x, axis=-1, keepdims=True) + NORM_EPS)
            on = x * r * wv
            silu_z, dsilu_z = _silu_and_grad(z)
            don = g * silu_z
            dz_ref[:, cols] = (g * on * dsilu_z).astype(BF16)
            gw = don * wv
            do_ref[:, cols] = r * gw - x * ((r * r * r) * (jnp.sum(gw * x, axis=-1, keepdims=True) / HEAD_DIM))
            dw = dw + jnp.sum(don * x * r, axis=0, keepdims=True)
        dw_ref[...] += dw

    blk = pl.BlockSpec((tr, BR_WIDTH), lambda i: (i, 0))
    vec = pl.BlockSpec((1, HEAD_DIM), lambda i: (0, 0))
    return pl.pallas_call(
        body, name=name, grid=(s // tr,),
        in_specs=[blk, pl.BlockSpec((tr, BR_WIDTH), lambda i: (i, z_col // BR_WIDTH)), vec,
                  pl.BlockSpec((tr, BR_WIDTH), lambda i: (i, dy_col // BR_WIDTH))],
        out_specs=[blk, blk, vec],
        out_shape=[jax.ShapeDtypeStruct((s, BR_WIDTH), F32), jax.ShapeDtypeStruct((s, BR_WIDTH), BF16),
                   jax.ShapeDtypeStruct((1, HEAD_DIM), F32)],
        compiler_params=_cp("arbitrary"),
    )(o, proj, w.reshape(1, HEAD_DIM), dy)


def _conv_silu(x, w, s):
    row = _iota2(s, x.shape[1], 0)
    c = w[CONV_W - 1:CONV_W, :] * x
    for k in range(1, CONV_W):
        c = c + w[CONV_W - 1 - k:CONV_W - k, :] * jnp.where(row >= k, pltpu.roll(x, k, 0), 0.0)
    return c


def _dn_qkv_fwd(proj, conv_w, *, name):
    s = proj.shape[0]
    nb = 3 * N_HEADS

    def body(x_ref, w_ref, o_ref):
        j = pl.program_id(0)
        sv = _silu(_conv_silu(x_ref[...], w_ref[...], s))
        r = lax.rsqrt(jnp.sum(sv * sv, axis=-1, keepdims=True) + L2_EPS)
        scale = jnp.where(j < N_HEADS, HEAD_DIM ** -0.5, 1.0).astype(F32)
        o_ref[...] = jnp.where(j < 2 * N_HEADS, sv * r * scale, sv)

    return pl.pallas_call(
        body, name=name, grid=(nb,),
        in_specs=[pl.BlockSpec((s, HEAD_DIM), lambda j: (0, j)), pl.BlockSpec((CONV_W, HEAD_DIM), lambda j: (0, j))],
        out_specs=pl.BlockSpec((s, HEAD_DIM), lambda j: (0, j)),
        out_shape=jax.ShapeDtypeStruct((s, 3 * BR_WIDTH), F32), compiler_params=_cp("parallel"),
    )(proj, conv_w)


def _dn_qkv_bwd(proj, conv_w, dqkv, *, name):
    s = proj.shape[0]
    nb = 3 * N_HEADS

    def body(x_ref, w_ref, g_ref, dx_ref, dw_ref):
        j = pl.program_id(0)
        x, w, g = x_ref[...], w_ref[...], g_ref[...]
        c = _conv_silu(x, w, s)
        sv, dsv = _silu_and_grad(c)
        r = lax.rsqrt(jnp.sum(sv * sv, axis=-1, keepdims=True) + L2_EPS)
        scale = jnp.where(j < N_HEADS, HEAD_DIM ** -0.5, 1.0).astype(F32)
        ds_n = scale * (r * g - sv * ((r * r * r) * jnp.sum(g * sv, axis=-1, keepdims=True)))
        dc = jnp.where(j < 2 * N_HEADS, ds_n, g) * dsv
        row = _iota2(s, HEAD_DIM, 0)
        dx = w[CONV_W - 1:CONV_W, :] * dc
        dws = [jnp.sum(dc * x, axis=0, keepdims=True)]
        for k in range(1, CONV_W):
            dx = dx + w[CONV_W - 1 - k:CONV_W - k, :] * jnp.where(row < s - k, pltpu.roll(dc, s - k, 0), 0.0)
            dws.append(jnp.sum(dc * jnp.where(row >= k, pltpu.roll(x, k, 0), 0.0), axis=0, keepdims=True))
        dx_ref[...] = dx.astype(BF16)
        for k in range(CONV_W):
            dw_ref[CONV_W - 1 - k:CONV_W - k, :] = dws[k]

    blk = pl.BlockSpec((s, HEAD_DIM), lambda j: (0, j))
    wblk = pl.BlockSpec((CONV_W, HEAD_DIM), lambda j: (0, j))
    return pl.pallas_call(
        body, name=name, grid=(nb,), in_specs=[blk, wblk, blk], out_specs=[blk, wblk],
        out_shape=[jax.ShapeDtypeStruct((s, 3 * BR_WIDTH), BF16), jax.ShapeDtypeStruct((CONV_W, 3 * BR_WIDTH), F32)],
        compiler_params=_cp("parallel"),
    )(proj, conv_w, dqkv)


def _tri(n, kind):
    r, c = _iota2(n, n, 0), _iota2(n, n, 1)
    if kind == "lower":
        return (r >= c).astype(F32)
    if kind == "upper":
        return (r <= c).astype(F32)
    return (r == c).astype(F32)


GATE_TILE = 512


def _dn_gate_fwd(proj, a_log, dt_bias, *, name):
    s = proj.shape[0]
    tr = min(GATE_TILE, s)

    def body(b_ref, a_ref, al_ref, dt_ref, beta_ref, g_ref):
        beta_ref[...] = _sigmoid(b_ref[...])
        g = -jnp.exp(al_ref[...]) * _softplus(a_ref[...] + dt_ref[...])
        low = _tri(CHUNK, "lower")
        for c in range(tr // CHUNK):
            rows = slice(c * CHUNK, (c + 1) * CHUNK)
            g_ref[rows, :] = _nn_exact(low, g[rows, :])

    blk = lambda cb: pl.BlockSpec((tr, HEAD_DIM), lambda i: (i, cb))
    vec = pl.BlockSpec((1, HEAD_DIM), lambda i: (0, 0))
    out = pl.BlockSpec((tr, HEAD_DIM), lambda i: (i, 0))
    return pl.pallas_call(
        body, name=name, grid=(s // tr,), in_specs=[blk(C_B // HEAD_DIM), blk(C_A // HEAD_DIM), vec, vec],
        out_specs=[out, out], out_shape=[jax.ShapeDtypeStruct((s, HEAD_DIM), F32)] * 2, compiler_params=_cp("parallel"),
    )(proj, proj, a_log, dt_bias)


def _dn_gate_bwd(proj, a_log, dt_bias, dbeta, d_g, *, name):
    s = proj.shape[0]
    tr = min(GATE_TILE, s)

    def body(b_ref, a_ref, al_ref, dt_ref, dbeta_ref, dG_ref, db_ref, da_ref, dal_ref, ddt_ref):
        @pl.when(pl.program_id(0) == 0)
        def _():
            dal_ref[...] = jnp.zeros_like(dal_ref)
            ddt_ref[...] = jnp.zeros_like(ddt_ref)

        beta = _sigmoid(b_ref[...])
        db_ref[...] = (dbeta_ref[...] * beta * (1.0 - beta)).astype(BF16)
        pre = a_ref[...] + dt_ref[...]
        neg_ea = -jnp.exp(al_ref[...])
        up = _tri(CHUNK, "upper")
        d_g = dG_ref[...]
        dg = jnp.concatenate([_nn_exact(up, d_g[c * CHUNK:(c + 1) * CHUNK, :]) for c in range(tr // CHUNK)], axis=0)
        da = dg * neg_ea * _sigmoid(pre)
        da_ref[...] = da.astype(BF16)
        ddt_ref[...] += jnp.sum(da, axis=0, keepdims=True)
        dal_ref[...] += jnp.sum(dg * neg_ea * _softplus(pre), axis=0, keepdims=True)

    blk = lambda cb: pl.BlockSpec((tr, HEAD_DIM), lambda i: (i, cb))
    vec = pl.BlockSpec((1, HEAD_DIM), lambda i: (0, 0))
    io = pl.BlockSpec((tr, HEAD_DIM), lambda i: (i, 0))
    return pl.pallas_call(
        body, name=name, grid=(s // tr,),
        in_specs=[blk(C_B // HEAD_DIM), blk(C_A // HEAD_DIM), vec, vec, io, io], out_specs=[io, io, vec, vec],
        out_shape=[jax.ShapeDtypeStruct((s, HEAD_DIM), BF16)] * 2 + [jax.ShapeDtypeStruct((1, HEAD_DIM), F32)] * 2,
        compiler_params=_cp("arbitrary"),
    )(proj, proj, a_log, dt_bias, dbeta, d_g)


def _unit_lower_inverse(a_strict, eye):
    x = -a_strict
    t = x + eye
    p = x
    n = 2
    while n < CHUNK:
        p = _nn(p, p)
        t = t + _nn(t, p)
        n *= 2
    return t


def _rows(*xs):
    return _hmap(lambda *a: jnp.concatenate(a, axis=0), *xs)


def _lanes(*xs):
    return _hmap(lambda *a: jnp.concatenate(a, axis=1), *xs)


def _dn_chunk_common(q, k, v, gc, beta, st, with_qd_state):
    c, d = CHUNK, HEAD_DIM
    eye = _tri(c, "eye")
    low = _tri(c, "lower")
    strict = low - eye
    grow = _col2row(gc, eye)
    dec = _hmap(lambda g_, gr: low * jnp.exp(low * (g_ - gr)), gc, grow)
    kb = k * beta
    kq = _nt(_rows(kb, q), k)
    a_mat = kq[0:c, :] * dec * strict
    qk = kq[c:2 * c, :] * dec
    t_inv = _unit_lower_inverse(a_mat, eye)
    e_g = _exp(gc)
    qd = q * e_g
    uw = _nn(t_inv, _lanes(v * beta, kb * e_g))
    u, w = uw[:, 0:d], uw[:, d:2 * d]
    last = (_iota2(c, 1, 0) == c - 1).astype(F32)
    g_last = _sum(gc * last, 0)
    e_t = _exp(g_last - gc)
    kt = k * e_t
    tail = _exp(g_last)
    if with_qd_state:
        ws = _nn(_rows(w, qd), st)
        vn, qds = u - ws[0:c, :], ws[c:2 * c, :]
    else:
        vn, qds = u - _nn(w, st), None
    return dict(eye=eye, low=low, strict=strict, dec=dec, kb=kb, a_mat=a_mat, t_inv=t_inv, e_g=e_g, u=u, w=w, uw=uw,
                qk=qk, qd=qd, qds=qds, last=last, e_t=e_t, kt=kt, tail=tail, vn=vn)


def _dn_chunk_fwd_math(q, k, v, gc, beta, st):
    m = _dn_chunk_common(q, k, v, gc, beta, st, True)
    o = m["qds"] + _nn(m["qk"], m["vn"])
    st2 = st * m["tail"] + _tn(m["kt"], m["vn"])
    return o, st2


def _dn_chunk_bwd_math(q, k, v, gc, beta, st, do, dst2):
    c, d = CHUNK, HEAD_DIM
    m = _dn_chunk_common(q, k, v, gc, beta, st, False)
    eye, low, strict = m["eye"], m["low"], m["strict"]
    dvn = _tn(m["qk"], do) + _nn(m["kt"], dst2)
    dqk = _nt(do, m["vn"]) * low
    both = _rows(do, dvn)
    ds_both = _nt(both, st)
    dqd, dw = ds_both[0:c, :], -ds_both[c:2 * c, :]
    dst = _tn(_rows(m["qd"], -m["w"]), both) + dst2 * m["tail"]
    dkt = _nt(m["vn"], dst2)
    dtail = _sum(_sum(st * dst2, 1), 0)
    dvb_dkg = _tn(m["t_inv"], _lanes(dvn, dw))
    dvb, dkg = dvb_dkg[:, 0:d], dvb_dkg[:, d:2 * d]
    d_a = _nt(dvb_dkg, m["uw"]) * (-strict)
    dkk = d_a * m["dec"]
    dp = dqk * m["dec"]
    dpk = _rows(dp, dkk)
    dq_dkb = _nn(dpk, k)
    dq = dq_dkb[0:c, :] + dqd * m["e_g"]
    dkb = dq_dkb[c:2 * c, :] + dkg * m["e_g"]
    dk = _tn(dpk, _rows(q, m["kb"])) + dkb * beta + dkt * m["e_t"]
    dv = dvb * beta
    dbeta = _sum(dvb * v + dkb * k, 1)
    de_g = _sum(dkg * m["kb"] + dqd * q, 1)
    de_t = _sum(dkt * k, 1)
    mm = d_a * m["a_mat"] + dqk * m["qk"]
    dgc = (_sum(mm, 1) - _row2col(_sum(mm, 0), eye) + de_g * m["e_g"] - de_t * m["e_t"]
           + (_sum(de_t * m["e_t"], 0) + dtail * m["tail"]) * m["last"])
    return dq, dk, dv, dgc, dbeta, dst


def _heads_of(ref):
    return _Heads(ref[:, h * HEAD_DIM:(h + 1) * HEAD_DIM] for h in range(N_HEADS))


def _lanes_of(block):
    return _Heads(_pick_lane(block, h) for h in range(N_HEADS))


def _dn_chunk_fwd(qkv, gcs, beta, *, name):
    s = qkv.shape[0]
    n = s // CHUNK

    def body(q_ref, k_ref, v_ref, g_ref, b_ref, o_ref, st_out_ref, st_ref):
        @pl.when(pl.program_id(0) == 0)
        def _():
            st_ref[...] = jnp.zeros_like(st_ref)

        gblk, bblk = g_ref[...], b_ref[...]
        st = _Heads(st_ref[h] for h in range(N_HEADS))
        o, st2 = _dn_chunk_fwd_math(_heads_of(q_ref), _heads_of(k_ref), _heads_of(v_ref), _lanes_of(gblk),
                                    _lanes_of(bblk), st)
        for h in range(N_HEADS):
            st_out_ref[0, h] = st.v[h]
            o_ref[:, h * HEAD_DIM:(h + 1) * HEAD_DIM] = o.v[h]
            st_ref[h] = st2.v[h]

    blk = lambda off: pl.BlockSpec((CHUNK, BR_WIDTH), lambda c: (c, off))
    sc = pl.BlockSpec((CHUNK, HEAD_DIM), lambda c: (c, 0))
    return pl.pallas_call(
        body, name=name, grid=(n,),
        in_specs=[blk(0), blk(1), blk(2), sc, sc],
        out_specs=[blk(0), pl.BlockSpec((1, N_HEADS, HEAD_DIM, HEAD_DIM), lambda c: (c, 0, 0, 0))],
        out_shape=[jax.ShapeDtypeStruct((s, BR_WIDTH), F32), jax.ShapeDtypeStruct((n, N_HEADS, HEAD_DIM, HEAD_DIM), F32)],
        scratch_shapes=[pltpu.VMEM((N_HEADS, HEAD_DIM, HEAD_DIM), F32)],
        compiler_params=_cp("arbitrary"),
    )(qkv, qkv, qkv, gcs, beta)


def _dn_chunk_bwd(qkv, gcs, beta, states, do, *, name):
    s = qkv.shape[0]
    n = s // CHUNK

    def body(q_ref, k_ref, v_ref, g_ref, b_ref, st_in_ref, do_ref, dqkv_ref, dg_ref, dbeta_ref, dst_ref):
        @pl.when(pl.program_id(0) == 0)
        def _():
            dst_ref[...] = jnp.zeros_like(dst_ref)

        gblk, bblk = g_ref[...], b_ref[...]
        lane = _iota2(CHUNK, HEAD_DIM, 1)
        dg_all = jnp.zeros((CHUNK, HEAD_DIM), F32)
        dbeta_all = jnp.zeros((CHUNK, HEAD_DIM), F32)
        dq, dk, dv, dgc, dbeta, dst = _dn_chunk_bwd_math(
            _heads_of(q_ref), _heads_of(k_ref), _heads_of(v_ref), _lanes_of(gblk), _lanes_of(bblk),
            _Heads(st_in_ref[0, h] for h in range(N_HEADS)), _heads_of(do_ref),
            _Heads(dst_ref[h] for h in range(N_HEADS)))
        for h in range(N_HEADS):
            for part, val in enumerate((dq, dk, dv)):
                c0 = part * BR_WIDTH + h * HEAD_DIM
                dqkv_ref[:, c0:c0 + HEAD_DIM] = val.v[h]
            dg_all = jnp.where(lane == h, dgc.v[h], dg_all)
            dbeta_all = jnp.where(lane == h, dbeta.v[h], dbeta_all)
            dst_ref[h] = dst.v[h]
        dg_ref[...] = dg_all
        dbeta_ref[...] = dbeta_all

    blk = lambda off: pl.BlockSpec((CHUNK, BR_WIDTH), lambda c: (n - 1 - c, off))
    sc = pl.BlockSpec((CHUNK, HEAD_DIM), lambda c: (n - 1 - c, 0))
    outs = pl.pallas_call(
        body, name=name, grid=(n,),
        in_specs=[blk(0), blk(1), blk(2), sc, sc,
                  pl.BlockSpec((1, N_HEADS, HEAD_DIM, HEAD_DIM), lambda c: (n - 1 - c, 0, 0, 0)), blk(0)],
        out_specs=[pl.BlockSpec((CHUNK, 3 * BR_WIDTH), lambda c: (n - 1 - c, 0)), sc, sc],
        out_shape=[jax.ShapeDtypeStruct((s, 3 * BR_WIDTH), F32)] + [jax.ShapeDtypeStruct((s, HEAD_DIM), F32)] * 2,
        scratch_shapes=[pltpu.VMEM((N_HEADS, HEAD_DIM, HEAD_DIM), F32)],
        compiler_params=_cp("arbitrary"),
    )(qkv, qkv, qkv, gcs, beta, states, do)
    return outs


def _hg_prep_fwd(proj, lb, *, name):
    s = proj.shape[0]
    tr = min(ROW_TILE, s)

    def body(q_ref, f_ref, lb_ref, qo_ref, ko_ref, lf_ref):
        f, lbv = f_ref[...], lb_ref[...]
        qo_ref[...] = _silu(q_ref[...])
        ko_ref[...] = (1.0 - lbv) * _sigmoid(-f)
        lf_ref[...] = jnp.log(lbv + (1.0 - lbv) * _sigmoid(f))

    blk = lambda cb: pl.BlockSpec((tr, BR_WIDTH), lambda i: (i, cb))
    out = pl.BlockSpec((tr, BR_WIDTH), lambda i: (i, 0))
    return pl.pallas_call(
        body, name=name, grid=(s // tr,),
        in_specs=[blk(C_HQ // BR_WIDTH), blk(C_HF // BR_WIDTH), pl.BlockSpec((1, BR_WIDTH), lambda i: (0, 0))],
        out_specs=[out, out, out], out_shape=[jax.ShapeDtypeStruct((s, BR_WIDTH), F32)] * 3, compiler_params=_cp("parallel"),
    )(proj, proj, lb)


def _hg_prep_bwd(proj, lb, dq, dk, dlf, *, name):
    s = proj.shape[0]
    tr = min(ROW_TILE, s)

    def body(q_ref, f_ref, lb_ref, dq_ref, dk_ref, dlf_ref, dhq_ref, dhf_ref, dlb_ref):
        @pl.when(pl.program_id(0) == 0)
        def _():
            dlb_ref[...] = jnp.zeros_like(dlb_ref)

        f, lbv = f_ref[...], lb_ref[...]
        dhq_ref[...] = (dq_ref[...] * _dsilu(q_ref[...])).astype(BF16)
        sp, sn = _sigmoid(f), _sigmoid(-f)
        inner = lbv + (1.0 - lbv) * sp
        dlf_over = dlf_ref[...] / inner
        dkv = dk_ref[...]
        dhf_ref[...] = (dlf_over * (1.0 - lbv) * sp * sn - dkv * (1.0 - lbv) * sn * (1.0 - sn)).astype(BF16)
        dlb_ref[...] += jnp.sum(dlf_over * (1.0 - sp) - dkv * sn, axis=0, keepdims=True)

    blk = lambda cb: pl.BlockSpec((tr, BR_WIDTH), lambda i: (i, cb))
    io = pl.BlockSpec((tr, BR_WIDTH), lambda i: (i, 0))
    vec = pl.BlockSpec((1, BR_WIDTH), lambda i: (0, 0))
    return pl.pallas_call(
        body, name=name, grid=(s // tr,),
        in_specs=[blk(C_HQ // BR_WIDTH), blk(C_HF // BR_WIDTH), vec, io, io, io], out_specs=[io, io, vec],
        out_shape=[jax.ShapeDtypeStruct((s, BR_WIDTH), BF16)] * 2 + [jax.ShapeDtypeStruct((1, BR_WIDTH), F32)],
        compiler_params=_cp("arbitrary"),
    )(proj, proj, lb, dq, dk, dlf)


def _hg_chunk_common(q, k, g):
    c, nb = CHUNK, CHUNK // SUB
    e_g = _exp(g)
    qd = q * e_g
    g_last = g[c - 1:c, :]
    e_t = _exp(g_last - g)
    kt = k * e_t
    tail = _exp(g_last)
    g_refs = [g[i * SUB:i * SUB + 1, :] for i in range(nb)]
    g_ref_rows = _hmap(lambda *rows: jnp.concatenate([jnp.broadcast_to(r, (SUB, r.shape[1])) for r in rows], axis=0), *g_refs)
    e_q = _exp(g - g_ref_rows)
    q_sc = q * e_q
    e_k = [_hmap(lambda gr, g_: jnp.exp(jnp.minimum(gr - g_, EXP_CLAMP)), g_refs[i], g) for i in range(nb)]
    k_sc_all = _rows(*[k * e_k[i] for i in range(nb)])
    row_blk = _iota2(c, 1, 0) // SUB
    masks = [(row_blk == i).astype(F32) for i in range(nb)]
    r_all = _nt(q_sc, k_sc_all)
    a_mat = r_all[:, 0:c] * masks[0]
    for i in range(1, nb):
        a_mat = a_mat + r_all[:, i * c:(i + 1) * c] * masks[i]
    a_mat = a_mat * _tri(c, "lower")
    return dict(e_g=e_g, qd=qd, e_t=e_t, kt=kt, tail=tail, q_sc=q_sc, k_sc_all=k_sc_all, e_q=e_q, e_k=e_k, masks=masks,
                a_mat=a_mat)


def _hg_chunk_fwd_math(q, k, v, g, stt):
    m = _hg_chunk_common(q, k, g)
    o = _nt(m["qd"], stt) + _nn(m["a_mat"], v)
    stt2 = stt * m["tail"] + _tn(v, m["kt"])
    return o, stt2


def _hg_chunk_bwd_math(q, k, v, g, stt, do, dstt2):
    c, nb = CHUNK, CHUNK // SUB
    m = _hg_chunk_common(q, k, g)
    stt2 = stt * m["tail"] + _tn(v, m["kt"])
    later = _sum(stt2 * dstt2, 0)
    dqd = _dot3(do, stt, 1, 0)
    dstt = _tn(do, m["qd"]) + dstt2 * m["tail"]
    d_a = _dot3(do, v, 1, 1) * _tri(c, "lower")
    dv = _tn(m["a_mat"], do) + _nt(m["kt"], dstt2)
    dkt = _dot3(v, dstt2, 1, 0)
    d_blk = _lanes(*[d_a * m["masks"][i] for i in range(nb)])
    dq = dqd * m["e_g"] + _dot3(d_blk, m["k_sc_all"], 1, 0) * m["e_q"]
    dks = _dot3(d_blk, m["q_sc"], 0, 0)
    dk = dkt * m["e_t"]
    for i in range(nb):
        dk = dk + dks[i * c:(i + 1) * c, :] * m["e_k"][i]
    db = q * dq - k * dk
    return dq, dk, dv, db, later, dstt


def _hg_chunk_fwd(qh, kh, proj, lf, *, name):
    s = qh.shape[0]
    n = s // CHUNK
    vb = C_HI // BR_WIDTH

    def body(q_ref, k_ref, v_ref, lf_ref, o_ref, st_out_ref, st_ref):
        @pl.when(pl.program_id(0) == 0)
        def _():
            st_ref[...] = jnp.zeros_like(st_ref)

        st = _Heads(st_ref[h] for h in range(N_HEADS))
        g_all = _nn_exact(_tri(CHUNK, "lower"), lf_ref[...])
        g = _Heads(g_all[:, h * HEAD_DIM:(h + 1) * HEAD_DIM] for h in range(N_HEADS))
        o, st2 = _hg_chunk_fwd_math(_heads_of(q_ref), _heads_of(k_ref), _heads_of(v_ref), g, st)
        for h in range(N_HEADS):
            st_out_ref[0, h] = st.v[h]
            o_ref[:, h * HEAD_DIM:(h + 1) * HEAD_DIM] = o.v[h]
            st_ref[h] = st2.v[h]

    blk = lambda off: pl.BlockSpec((CHUNK, BR_WIDTH), lambda c: (c, off))
    return pl.pallas_call(
        body, name=name, grid=(n,), in_specs=[blk(0), blk(0), blk(vb), blk(0)],
        out_specs=[blk(0), pl.BlockSpec((1, N_HEADS, HEAD_DIM, HEAD_DIM), lambda c: (c, 0, 0, 0))],
        out_shape=[jax.ShapeDtypeStruct((s, BR_WIDTH), F32), jax.ShapeDtypeStruct((n, N_HEADS, HEAD_DIM, HEAD_DIM), F32)],
        scratch_shapes=[pltpu.VMEM((N_HEADS, HEAD_DIM, HEAD_DIM), F32)],
        compiler_params=_cp("arbitrary"),
    )(qh, kh, proj, lf)


def _hg_chunk_bwd(qh, kh, proj, lf, states, do, *, name):
    s = qh.shape[0]
    n = s // CHUNK
    vb = C_HI // BR_WIDTH

    def body(q_ref, k_ref, v_ref, lf_ref, st_in_ref, do_ref, dq_ref, dk_ref, dv_ref, dlf_ref, dst_ref):
        @pl.when(pl.program_id(0) == 0)
        def _():
            dst_ref[...] = jnp.zeros_like(dst_ref)

        g_all = _nn_exact(_tri(CHUNK, "lower"), lf_ref[...])
        g = _Heads(g_all[:, h * HEAD_DIM:(h + 1) * HEAD_DIM] for h in range(N_HEADS))
        dq, dk, dv, db, later, dst = _hg_chunk_bwd_math(
            _heads_of(q_ref), _heads_of(k_ref), _heads_of(v_ref), g,
            _Heads(st_in_ref[0, h] for h in range(N_HEADS)), _heads_of(do_ref),
            _Heads(dst_ref[h] for h in range(N_HEADS)))
        dlf_ref[...] = (_nn_exact(_tri(CHUNK, "upper"), jnp.concatenate(db.v, axis=1))
                        + jnp.concatenate(later.v, axis=1))
        for h in range(N_HEADS):
            cols = slice(h * HEAD_DIM, (h + 1) * HEAD_DIM)
            dq_ref[:, cols] = dq.v[h]
            dk_ref[:, cols] = dk.v[h]
            dv_ref[:, cols] = dv.v[h].astype(BF16)
            dst_ref[h] = dst.v[h]

    blk = lambda off: pl.BlockSpec((CHUNK, BR_WIDTH), lambda c: (n - 1 - c, off))
    return pl.pallas_call(
        body, name=name, grid=(n,),
        in_specs=[blk(0), blk(0), blk(vb), blk(0),
                  pl.BlockSpec((1, N_HEADS, HEAD_DIM, HEAD_DIM), lambda c: (n - 1 - c, 0, 0, 0)), blk(0)],
        out_specs=[blk(0), blk(0), blk(0), blk(0)],
        out_shape=[jax.ShapeDtypeStruct((s, BR_WIDTH), F32)] * 2 + [jax.ShapeDtypeStruct((s, BR_WIDTH), BF16),
                                                                    jax.ShapeDtypeStruct((s, BR_WIDTH), F32)],
        scratch_shapes=[pltpu.VMEM((N_HEADS, HEAD_DIM, HEAD_DIM), F32)],
        compiler_params=_cp("arbitrary"),
    )(qh, kh, proj, lf, states, do)


_ANY = pl.BlockSpec(memory_space=pl.ANY)
_MESH = pl.DeviceIdType.MESH


def _all_gather(x_local, *, name, after=()):
    n_after = len(after)

    def body(x_ref, *refs):
        out_ref, send_sems, recv_sems, local_sem = refs[n_after:]
        x, y, c = lax.axis_index("x"), lax.axis_index("y"), lax.axis_index("c")
        me, sibling = (x, y, c), (x, y, 1 - c)
        chips = [(1 - x, y), (x, 1 - y), (1 - x, 1 - y)]

        def slot(px, py, pc):
            return out_ref.at[4 * px + 2 * py + pc]

        def copy(k, block, to, src=None):
            return pltpu.make_async_remote_copy(
                src_ref=slot(*block) if src is None else src, dst_ref=slot(*block),
                send_sem=send_sems.at[k], recv_sem=recv_sems.at[k], device_id=to, device_id_type=_MESH)

        mine = pltpu.make_async_copy(x_ref, slot(*me), local_sem)
        mine.start()
        first = [copy(0, me, sibling, src=x_ref)]
        first += [copy(1 + j, me, (*chip, c), src=x_ref) for j, chip in enumerate(chips)]
        for cp in first:
            cp.start()
        passed = [copy(4 + j, (*chip, c), sibling) for j, chip in enumerate(chips)]
        for j, chip in enumerate(chips):
            copy(1 + j, (*chip, c), me).wait_recv()
            passed[j].start()
        copy(0, sibling, me).wait_recv()
        for j, chip in enumerate(chips):
            copy(4 + j, (*chip, 1 - c), me).wait_recv()
        for cp in first + passed:
            cp.wait_send()
        mine.wait()

    return pl.pallas_call(
        body, name=name, out_shape=jax.ShapeDtypeStruct((N_DEV,) + x_local.shape, x_local.dtype),
        in_specs=[_ANY] * (1 + n_after), out_specs=_ANY,
        scratch_shapes=[pltpu.SemaphoreType.DMA((7,)), pltpu.SemaphoreType.DMA((7,)), pltpu.SemaphoreType.DMA],
    )(x_local, *after)


_HBM = pl.BlockSpec(memory_space=pltpu.HBM)
_SEM = pl.BlockSpec(memory_space=pltpu.SEMAPHORE)
_EFFECT = pltpu.SideEffectType.DATAFLOW_SIDE_EFFECTING


def _peers():
    x, y, c = lax.axis_index("x"), lax.axis_index("y"), lax.axis_index("c")
    out = []
    for k in range(1, N_DEV):
        px, py, pc = x ^ ((k >> 2) & 1), y ^ ((k >> 1) & 1), c ^ (k & 1)
        out.append(((px, py, pc), 4 * px + 2 * py + pc))
    return 4 * x + 2 * y + c, out


def _push_copies(src_ref, land_ref, send_sems, recv_sems, broadcast):
    my, peers = _peers()
    pairs = []
    for k, (pos, idx) in enumerate(peers):
        src = src_ref if broadcast else src_ref.at[idx]
        send = pltpu.make_async_remote_copy(src_ref=src, dst_ref=land_ref.at[my], send_sem=send_sems.at[k],
                                            recv_sem=recv_sems.at[k], device_id=pos, device_id_type=_MESH)
        recv = pltpu.make_async_remote_copy(src_ref=src, dst_ref=land_ref.at[idx], send_sem=send_sems.at[k],
                                            recv_sem=recv_sems.at[k], device_id=pos, device_id_type=_MESH)
        pairs.append((send, recv))
    return pairs


def _push_start(src, land, *, broadcast, name, after=()):
    n_after = len(after)

    def body(src_ref, land_ref, *refs):
        send_sems, recv_sems, _, _, token = refs[n_after:]
        for send, _ in _push_copies(src_ref, land_ref, send_sems, recv_sems, broadcast):
            send.start()
        token[...] = jnp.zeros_like(token)

    return pl.pallas_call(
        body, name=name,
        out_shape=(pltpu.SemaphoreType.DMA((N_DEV - 1,)), pltpu.SemaphoreType.DMA((N_DEV - 1,)),
                   pltpu.HBM(src.shape, src.dtype), pltpu.HBM(land.shape, land.dtype), jax.ShapeDtypeStruct((8, 128), F32)),
        in_specs=(_HBM, _HBM) + (_ANY,) * n_after, out_specs=(_SEM, _SEM, _HBM, _HBM, pl.BlockSpec(memory_space=pltpu.VMEM)),
        input_output_aliases={0: 2, 1: 3}, compiler_params=pltpu.CompilerParams(has_side_effects=_EFFECT),
    )(pltpu.with_memory_space_constraint(src, pltpu.HBM), pltpu.with_memory_space_constraint(land, pltpu.HBM), *after)


def _push_wait(handle, after, *, broadcast, name):
    send_sems, recv_sems, src_thru, land_thru, _ = handle

    def body(src_ref, land_ref, send_sems, recv_sems, *rest):
        for send, recv in _push_copies(src_ref, land_ref, send_sems, recv_sems, broadcast):
            send.wait_send()
            recv.wait_recv()

    return pl.pallas_call(
        body, name=name,
        out_shape=(pltpu.HBM(src_thru.shape, src_thru.dtype), pltpu.HBM(land_thru.shape, land_thru.dtype)),
        in_specs=(_HBM, _HBM, _SEM, _SEM) + (_ANY,) * len(after), out_specs=(_HBM, _HBM),
        input_output_aliases={0: 0, 1: 1}, compiler_params=pltpu.CompilerParams(has_side_effects=_EFFECT),
    )(src_thru, land_thru, send_sems, recv_sems, *after)[1]


def _relay_copies(src_ref, land_ref, sems_a, sems_b):
    x, y, c = lax.axis_index("x"), lax.axis_index("y"), lax.axis_index("c")
    slot = lambda px, py, pc: land_ref.at[4 * px + 2 * py + pc]
    chips = [(1 - x, y), (x, 1 - y), (1 - x, 1 - y)]
    (send_a, recv_a), (send_b, recv_b) = sems_a, sems_b

    def copy(sems, k, src, dst_slot, to):
        return pltpu.make_async_remote_copy(src_ref=src, dst_ref=dst_slot, send_sem=sems[0].at[k], recv_sem=sems[1].at[k],
                                            device_id=to, device_id_type=_MESH)

    first = [copy((send_a, recv_a), 0, src_ref, slot(x, y, c), (x, y, 1 - c))]
    first += [copy((send_a, recv_a), 1 + j, src_ref, slot(x, y, c), (*chip, c)) for j, chip in enumerate(chips)]
    first_in = [copy((send_a, recv_a), 0, src_ref, slot(x, y, 1 - c), (x, y, 1 - c))]
    first_in += [copy((send_a, recv_a), 1 + j, src_ref, slot(*chip, c), (*chip, c)) for j, chip in enumerate(chips)]
    relay = [copy((send_b, recv_b), j, slot(*chip, c), slot(*chip, c), (x, y, 1 - c)) for j, chip in enumerate(chips)]
    relay_in = [copy((send_b, recv_b), j, slot(*chip, 1 - c), slot(*chip, 1 - c), (x, y, 1 - c)) for j, chip in enumerate(chips)]
    return first, first_in, relay, relay_in


def _relay_start(src, land, *, name, after=()):
    n_after = len(after)

    def body(src_ref, land_ref, *refs):
        send_a, recv_a, _, _, token = refs[n_after:]
        for cp in _relay_copies(src_ref, land_ref, (send_a, recv_a), (send_a, recv_a))[0]:
            cp.start()
        token[...] = jnp.zeros_like(token)

    send_a, recv_a, src_thru, land_thru, token = pl.pallas_call(
        body, name=name,
        out_shape=(pltpu.SemaphoreType.DMA((4,)), pltpu.SemaphoreType.DMA((4,)), pltpu.HBM(src.shape, src.dtype),
                   pltpu.HBM(land.shape, land.dtype), jax.ShapeDtypeStruct((8, 128), F32)),
        in_specs=(_HBM, _HBM) + (_ANY,) * n_after, out_specs=(_SEM, _SEM, _HBM, _HBM, pl.BlockSpec(memory_space=pltpu.VMEM)),
        input_output_aliases={0: 2, 1: 3}, compiler_params=pltpu.CompilerParams(has_side_effects=_EFFECT),
    )(pltpu.with_memory_space_constraint(src, pltpu.HBM), pltpu.with_memory_space_constraint(land, pltpu.HBM), *after)
    return (send_a, recv_a), src_thru, land_thru, token


def _relay_mid(handle, after, *, name):
    sems_a, src_thru, land_thru, _ = handle
    n_after = len(after)

    def body(src_ref, land_ref, send_a, recv_a, *refs):
        send_b, recv_b, _, _, token = refs[n_after:]
        _, first_in, relay, _ = _relay_copies(src_ref, land_ref, (send_a, recv_a), (send_b, recv_b))
        for j in range(3):
            first_in[1 + j].wait_recv()
            relay[j].start()
        token[...] = jnp.zeros_like(token)

    send_b, recv_b, src2, land2, token = pl.pallas_call(
        body, name=name,
        out_shape=(pltpu.SemaphoreType.DMA((3,)), pltpu.SemaphoreType.DMA((3,)), pltpu.HBM(src_thru.shape, src_thru.dtype),
                   pltpu.HBM(land_thru.shape, land_thru.dtype), jax.ShapeDtypeStruct((8, 128), F32)),
        in_specs=(_HBM, _HBM, _SEM, _SEM) + (_ANY,) * n_after,
        out_specs=(_SEM, _SEM, _HBM, _HBM, pl.BlockSpec(memory_space=pltpu.VMEM)),
        input_output_aliases={0: 2, 1: 3}, compiler_params=pltpu.CompilerParams(has_side_effects=_EFFECT),
    )(src_thru, land_thru, *sems_a, *after)
    return sems_a, (send_b, recv_b), src2, land2, token


def _relay_wait(handle, after, *, name):
    sems_a, sems_b, src_thru, land_thru, _ = handle

    def body(src_ref, land_ref, send_a, recv_a, send_b, recv_b, *rest):
        first, first_in, relay, relay_in = _relay_copies(src_ref, land_ref, (send_a, recv_a), (send_b, recv_b))
        first_in[0].wait_recv()
        for cp in relay_in:
            cp.wait_recv()
        for cp in first + relay:
            cp.wait_send()

    return pl.pallas_call(
        body, name=name,
        out_shape=(pltpu.HBM(src_thru.shape, src_thru.dtype), pltpu.HBM(land_thru.shape, land_thru.dtype)),
        in_specs=(_HBM, _HBM, _SEM, _SEM, _SEM, _SEM) + (_ANY,) * len(after), out_specs=(_HBM, _HBM),
        input_output_aliases={0: 0, 1: 1}, compiler_params=pltpu.CompilerParams(has_side_effects=_EFFECT),
    )(src_thru, land_thru, *sems_a, *sems_b, *after)[1]


def _adamw(parts, row_off, w, m, v, *, layer=0, n_layers=1, prev=None, name, tr):
    rows, c = w.shape
    r = rows // n_layers
    np_ = parts.shape[0]
    tr = min(tr, r)
    assert r % tr == 0 and row_off % tr == 0
    ob, lb = row_off // tr, layer * (r // tr)
    c1 = 1.0 - ADAM_B1 ** ADAM_STEP
    c2 = 1.0 - ADAM_B2 ** ADAM_STEP
    n_prev = 0 if prev is None else 4

    def body(p_ref, w_ref, m_ref, v_ref, *refs):
        g_ref, d_ref, nm_ref, nv_ref = refs[n_prev:]
        g = p_ref[0].astype(F32)
        for s in range(1, np_):
            g = g + p_ref[s].astype(F32)
        wv = w_ref[...]
        m2 = ADAM_B1 * m_ref[...] + (1.0 - ADAM_B1) * g
        v2 = ADAM_B2 * v_ref[...] + (1.0 - ADAM_B2) * jnp.square(g)
        m_hat = m2 / c1
        v_hat = v2 / c2
        g_ref[...] = g
        d_ref[...] = -ADAM_LR * (m_hat / (jnp.sqrt(v_hat) + ADAM_EPS) + ADAM_WD * wv)
        nm_ref[...] = m2
        nv_ref[...] = v2

    blk = pl.BlockSpec((tr, c), lambda i: (lb + i, 0))
    return pl.pallas_call(
        body, name=name, grid=(r // tr,),
        in_specs=[pl.BlockSpec((np_, tr, c), lambda i: (0, ob + i, 0)), blk, blk, blk] + [_ANY] * n_prev,
        out_specs=[blk] * 4, out_shape=[jax.ShapeDtypeStruct((rows, c), F32)] * 4,
        input_output_aliases={4 + i: i for i in range(n_prev)}, compiler_params=_cp("parallel"),
    )(parts, w, m, v, *(prev or ()))


def _sum_parts(parts, *, name, after=()):
    np_, r, c = parts.shape

    def body(p_ref, *refs):
        o_ref = refs[-1]
        g = p_ref[0]
        for s in range(1, np_):
            g = g + p_ref[s]
        o_ref[...] = g

    vmem = pl.BlockSpec(memory_space=pltpu.VMEM)
    return pl.pallas_call(body, name=name, in_specs=[vmem] + [_ANY] * len(after), out_specs=vmem,
                          out_shape=jax.ShapeDtypeStruct((r, c), F32))(parts, *after)


def _pack(arrs):
    rows = []
    for a in arrs:
        f = a.reshape(-1).astype(F32)
        pad = (-f.shape[0]) % 128
        rows.append(jnp.pad(f, (0, pad)).reshape(-1, 128))
    out = jnp.concatenate(rows, axis=0)
    return jnp.pad(out, ((0, (-out.shape[0]) % 8), (0, 0)))


def _unpack(packed, shapes):
    outs, r0 = [], 0
    for shp in shapes:
        n = 1
        for d in shp:
            n *= d
        nr = -(-n // 128)
        outs.append(packed[r0:r0 + nr].reshape(-1)[:n].reshape(shp))
        r0 += nr
    return outs


_WIN_PIECES = ((0, 4096, 0), (4112, 8208, 0), (4096, 4104, HEAD_DIM - N_HEADS), (4104, 4112, HEAD_DIM - N_HEADS))


RELAYOUT_TILE = 256


def _win_from_shards(shards, *, name):
    k = shards.shape[1]
    tr = min(RELAYOUT_TILE, k)

    def body(x_ref, o_ref):
        cols = []
        for lo, hi, pad in _WIN_PIECES:
            for j in range(N_DEV):
                a, b = max(lo, j * SHARD_IN), min(hi, (j + 1) * SHARD_IN)
                if a < b:
                    cols.append(x_ref[j, :, a - j * SHARD_IN:b - j * SHARD_IN])
            if pad:
                cols.append(jnp.zeros((tr, pad), x_ref.dtype))
        o_ref[...] = jnp.concatenate(cols, axis=1)

    return pl.pallas_call(
        body, name=name, grid=(k // tr,), in_specs=[pl.BlockSpec((N_DEV, tr, SHARD_IN), lambda i: (0, i, 0))],
        out_specs=pl.BlockSpec((tr, N_PROJ), lambda i: (i, 0)), out_shape=jax.ShapeDtypeStruct((k, N_PROJ), shards.dtype),
        compiler_params=_cp("parallel"),
    )(shards)


def _win_to_shards(g, *, name):
    k = g.shape[0]
    tr = min(RELAYOUT_TILE, k)
    starts, off = [], 0
    for lo, hi, pad in _WIN_PIECES:
        starts.append((lo, hi, off))
        off += hi - lo + pad

    def body(g_ref, o_ref):
        for j in range(N_DEV):
            cols = []
            for lo, hi, off in sorted(starts):
                a, b = max(lo, j * SHARD_IN), min(hi, (j + 1) * SHARD_IN)
                if a < b:
                    cols.append(g_ref[:, off + a - lo:off + b - lo])
            o_ref[j] = jnp.concatenate(cols, axis=1)

    return pl.pallas_call(
        body, name=name, grid=(k // tr,), in_specs=[pl.BlockSpec((tr, N_PROJ), lambda i: (i, 0))],
        out_specs=pl.BlockSpec((N_DEV, tr, SHARD_IN), lambda i: (0, i, 0)),
        out_shape=jax.ShapeDtypeStruct((N_DEV, k, SHARD_IN), g.dtype), compiler_params=_cp("parallel"),
    )(g)


def _lower_bounds(logits):
    probs = jax.nn.softmax(logits.astype(F32), axis=0)
    return jnp.cumsum(probs, axis=0) - probs[0]


def _pad_lanes(vec8):
    return jnp.pad(vec8.reshape(1, N_HEADS), ((0, 0), (0, HEAD_DIM - N_HEADS)))


def kernel(x, p, norm_w, w_in, dn_conv_w, dn_A_log, dn_dt_bias, dn_norm_w, hg_lb_logits, hg_norm_w, w_out, w_ple_up, w_ple_gate, final_norm_w, loss_target, m_norm_w, m_w_in, m_dn_conv_w, m_dn_A_log, m_dn_dt_bias, m_dn_norm_w, m_hg_lb_logits, m_hg_norm_w, m_w_out, m_w_ple_up, m_w_ple_gate, m_final_norm_w, v_norm_w, v_w_in, v_dn_conv_w, v_dn_A_log, v_dn_dt_bias, v_dn_norm_w, v_hg_lb_logits, v_hg_norm_w, v_w_out, v_w_ple_up, v_w_ple_gate, v_final_norm_w):
    depth = norm_w.shape[0]
    my = 4 * lax.axis_index("x") + 2 * lax.axis_index("y") + lax.axis_index("c")
    h = x[0]
    tgt = loss_target[0]
    rows_out = D_MODEL // N_DEV
    up_rows = PLE_DIM * (D_MODEL // N_DEV) // D_MODEL
    g_off, u_off = rows_out, 2 * rows_out

    def own_slot(block):
        return lax.dynamic_update_index_in_dim(lax.empty((N_DEV,) + block.shape, block.dtype), block, my, 0)

    win_bf = w_in.astype(BF16)
    rest_bf = [jnp.concatenate([w_out[l], w_ple_gate[l], w_ple_up[l].reshape(up_rows, D_MODEL)], axis=0).astype(BF16)
               for l in range(depth)]
    conv_all = _all_gather(dn_conv_w, name="gather_conv_w")
    conv_full = conv_all.transpose(1, 2, 0, 3).reshape(depth, CONV_W, 3 * BR_WIDTH)
    win_all = {0: _all_gather(win_bf[0], name="gather_w_in_l0", after=[conv_all])}
    pending, relayed = {}, {}
    last = win_all[0]
    for l in range(depth):
        if l > 0:
            relayed["win", l] = _relay_start(win_bf[l], own_slot(win_bf[l]), after=[last], name=f"gather_w_in_l{l}_first")
            last = relayed["win", l][3]
        if l == 0:
            relayed["rest", l] = _relay_start(rest_bf[l], own_slot(rest_bf[l]), after=[last], name=f"gather_rest_l{l}_first")
            last = relayed["rest", l][3]
        else:
            pending["rest", l] = _push_start(rest_bf[l], own_slot(rest_bf[l]), broadcast=True, after=[last],
                                             name=f"gather_rest_l{l}_start")
            last = pending["rest", l][4]
    order_tok = last[0, 0]
    lbs = _lower_bounds(hg_lb_logits)

    saved = []
    weights = []
    for l in range(depth):
        tag = f"l{l}"
        if l > 0:
            win_all[l] = _relay_wait(relayed["win", l], [h], name=f"gather_w_in_{tag}_wait")
        wi = _win_from_shards(win_all[l], name=f"w_in_layout_{tag}")
        nw = norm_w[l] + order_tok if l == 0 else norm_w[l]
        hn = _rms_fwd(h, nw, name=f"rms_fwd_{tag}")
        proj = _mm(hn, wi, mode="nn", out_dtype=F32, name=f"mm_proj_{tag}")
        al, dt = _pad_lanes(dn_A_log[l]), _pad_lanes(dn_dt_bias[l])
        qkv = _dn_qkv_fwd(proj, conv_full[l], name=f"dn_qkv_fwd_{tag}")
        if ("rest", l) in relayed:
            relayed["rest", l] = _relay_mid(relayed["rest", l], [qkv], name=f"gather_rest_{tag}_relay")
            al = al + relayed["rest", l][4][0, 0]
        beta, gcs = _dn_gate_fwd(proj, al, dt, name=f"dn_gate_fwd_{tag}")
        o_dn, st_dn = _dn_chunk_fwd(qkv, gcs, beta, name=f"dn_chunk_fwd_{tag}")
        lb = lbs[l].reshape(1, BR_WIDTH)
        qh, kh, lf = _hg_prep_fwd(proj, lb, name=f"hg_prep_fwd_{tag}")
        o_hg, st_hg = _hg_chunk_fwd(qh, kh, proj, lf, name=f"hg_chunk_fwd_{tag}")
        y_dn = _hnorm_fwd(o_dn, proj, C_Z, dn_norm_w[l], name=f"hnorm_dn_fwd_{tag}")
        y_hg = _hnorm_fwd(o_hg, proj, C_HZ, hg_norm_w[l], name=f"hnorm_hg_fwd_{tag}")
        y = jnp.concatenate([y_dn, y_hg], axis=1)
        if ("rest", l) in relayed:
            rest_all = _relay_wait(relayed["rest", l], [y], name=f"gather_rest_{tag}_wait")
        else:
            rest_all = _push_wait(pending["rest", l], [y], broadcast=True, name=f"gather_rest_{tag}_wait")
        wo = rest_all[:, 0:rows_out].reshape(D_MODEL, D_MODEL)
        wg = rest_all[:, g_off:g_off + rows_out].reshape(D_MODEL, D_MODEL)
        wu = rest_all[:, u_off:u_off + up_rows].reshape(N_DEV, PLE_DIM, D_MODEL // N_DEV).transpose(1, 0, 2).reshape(PLE_DIM, D_MODEL)
        weights.append((wi, wo, wg, wu))
        h1 = _mm(y, wo, mode="nn", out_dtype=F32, res=h, name=f"mm_out_{tag}")
        pin = []
        if ("win", l + 1) in relayed:
            relayed["win", l + 1] = _relay_mid(relayed["win", l + 1], [h1], name=f"gather_w_in_l{l + 1}_relay")
            pin = [relayed["win", l + 1][4]]
        gp = _mm(h1, wg, mode="nn", out_dtype=F32, after=pin, name=f"mm_gate_{tag}")
        up = _mm(p[l, 0], wu, mode="nn", out_dtype=F32, name=f"mm_up_{tag}")
        h2 = _ple_fwd(h1, gp, up, name=f"ple_fwd_{tag}")
        saved.append(dict(h=h, hn=hn, proj=proj, qkv=qkv, beta=beta, gcs=gcs, st_dn=st_dn, qh=qh, kh=kh, lf=lf,
                          st_hg=st_hg, o_dn=o_dn, o_hg=o_hg, y=y, h1=h1, gp=gp, up=up, al=al, dt=dt, lb=lb))
        h = h2

    loss_row, dh, d_final_w = _final_fwd_bwd(h, final_norm_w, tgt, name="final_norm_loss")

    d_norm_w, d_alog, d_dt, d_dn_nw, d_hg_nw, d_lb, d_conv = ([None] * depth for _ in range(7))
    sent = {}
    for l in reversed(range(depth)):
        wi, wo, wg, wu = weights[l]
        sv = saved[l]
        tag = f"l{l}"
        dup, dgp = _ple_bwd(dh, sv["gp"], sv["up"], name=f"ple_bwd_{tag}")
        d_wu = _mm(p[l, 0], dup, mode="tn", out_dtype=BF16, name=f"mm_dwup_{tag}")
        d_wg = _mm(sv["h1"], dgp, mode="tn", out_dtype=BF16, name=f"mm_dwgate_{tag}")
        dh1 = _mm(dgp, wg, mode="nt", out_dtype=F32, res=dh, name=f"mm_dh1_{tag}")
        d_wo = _mm(sv["y"], dh1, mode="tn", out_dtype=BF16, name=f"mm_dwout_{tag}")
        parts_rest = jnp.concatenate(
            [d_wo.reshape(N_DEV, rows_out, D_MODEL), d_wg.reshape(N_DEV, rows_out, D_MODEL),
             d_wu.reshape(PLE_DIM, N_DEV, D_MODEL // N_DEV).transpose(1, 0, 2).reshape(N_DEV, up_rows, D_MODEL)], axis=1)
        sent["rest", l] = _push_start(parts_rest, own_slot(parts_rest[my]), broadcast=False, name=f"exchange_rest_{tag}_start")
        dy = _mm(dh1, wo, mode="nt", out_dtype=F32, name=f"mm_dy_{tag}")
        dn_nw = dn_norm_w[l] + sent["rest", l][4][0, 0]
        do_dn, dz_dn, d_dn_nw[l] = _hnorm_bwd(sv["o_dn"], sv["proj"], C_Z, dn_nw, dy, 0, name=f"hnorm_dn_bwd_{tag}")
        do_hg, dz_hg, d_hg_nw[l] = _hnorm_bwd(sv["o_hg"], sv["proj"], C_HZ, hg_norm_w[l], dy, BR_WIDTH, name=f"hnorm_hg_bwd_{tag}")
        dqkv, d_gc, dbeta = _dn_chunk_bwd(sv["qkv"], sv["gcs"], sv["beta"], sv["st_dn"], do_dn, name=f"dn_chunk_bwd_{tag}")
        dqkv_pre, d_conv[l] = _dn_qkv_bwd(sv["proj"], conv_full[l], dqkv, name=f"dn_qkv_bwd_{tag}")
        db, da, d_alog[l], d_dt[l] = _dn_gate_bwd(sv["proj"], sv["al"], sv["dt"], dbeta, d_gc, name=f"dn_gate_bwd_{tag}")
        dqh, dkh, dhi, dlf = _hg_chunk_bwd(sv["qh"], sv["kh"], sv["proj"], sv["lf"], sv["st_hg"], do_hg, name=f"hg_chunk_bwd_{tag}")
        dhq, dhf, d_lb[l] = _hg_prep_bwd(sv["proj"], sv["lb"], dqh, dkh, dlf, name=f"hg_prep_bwd_{tag}")
        dproj = jnp.concatenate([dqkv_pre, dz_dn, dhq, dhf, dhi, dz_hg, db, da], axis=1)
        def push_d_win(after):
            d_win = _mm(sv["hn"], dproj, mode="tn", out_dtype=BF16, after=after, name=f"mm_dwin_{tag}")
            parts_in = _win_to_shards(d_win, name=f"dw_in_shards_{tag}")
            return _push_start(parts_in, own_slot(parts_in[my]), broadcast=False, name=f"exchange_w_in_{tag}_start")

        if l > 0:
            sent["win", l] = push_d_win([])
            dhn = _mm(dproj, wi, mode="nt", out_dtype=F32, after=[sent["win", l][4]], name=f"mm_dhn_{tag}")
            dh, d_norm_w[l] = _rms_bwd(sv["h"], norm_w[l], dhn, dh1, name=f"rms_bwd_{tag}")
        else:
            dhn = _mm(dproj, wi, mode="nt", out_dtype=F32, name=f"mm_dhn_{tag}")
            dh, d_norm_w[l] = _rms_bwd(sv["h"], norm_w[l], dhn, dh1, name=f"rms_bwd_{tag}")
            small = _pack([loss_row, jnp.concatenate(d_norm_w, axis=0), d_final_w,
                           jnp.stack([a[0, :N_HEADS] for a in d_alog]), jnp.stack([a[0, :N_HEADS] for a in d_dt]),
                           jnp.concatenate(d_dn_nw, axis=0), jnp.concatenate(d_hg_nw, axis=0), jnp.concatenate(d_lb, axis=0),
                           jnp.stack(d_conv)])
            small_all = _all_gather(small, name="gather_small")
            sent["win", l] = push_d_win([small_all])
    grad_x = dh[None]

    small_shapes = [(1, 128), norm_w.shape, final_norm_w.shape, dn_A_log.shape, dn_dt_bias.shape, dn_norm_w.shape,
                    hg_norm_w.shape, hg_lb_logits.shape, (depth, CONV_W, 3 * BR_WIDTH)]
    tot = _unpack(_sum_parts(small_all, after=[sent["win", 0][4]], name="sum_small"), small_shapes)
    loss = tot[0][0, 0]
    g_lb = tot[7]
    g_logits = jax.vjp(_lower_bounds, hg_lb_logits)[1](g_lb)[0]
    g_conv = lax.dynamic_slice_in_dim(tot[8], my * (3 * BR_WIDTH // N_DEV), 3 * BR_WIDTH // N_DEV, axis=2)
    small_g = [tot[1], g_conv, tot[3], tot[4], tot[5], g_logits, tot[6], tot[2]]
    small_w = [norm_w, dn_conv_w, dn_A_log, dn_dt_bias, dn_norm_w, hg_lb_logits, hg_norm_w, final_norm_w]
    small_m = [m_norm_w, m_dn_conv_w, m_dn_A_log, m_dn_dt_bias, m_dn_norm_w, m_hg_lb_logits, m_hg_norm_w, m_final_norm_w]
    small_v = [v_norm_w, v_dn_conv_w, v_dn_A_log, v_dn_dt_bias, v_dn_norm_w, v_hg_lb_logits, v_hg_norm_w, v_final_norm_w]
    pk_w = _pack(small_w)
    res_small = _adamw(_pack(small_g)[None], 0, pk_w, _pack(small_m), _pack(small_v), name="adamw_small", tr=pk_w.shape[0])
    shapes_w = [a.shape for a in small_w]
    sg, sd, sm, sv_ = (_unpack(r, shapes_w) for r in res_small)

    r_win = r_wo = r_wg = r_wu = None
    done = [grad_x, res_small[0]]

    def flat(a, cols):
        return a.reshape(-1, cols)

    for l in reversed(range(depth)):
        tag = f"l{l}"
        land_rest = _push_wait(sent["rest", l], done, broadcast=False, name=f"exchange_rest_{tag}_wait")
        r_wo = _adamw(land_rest, 0, flat(w_out, D_MODEL), flat(m_w_out, D_MODEL), flat(v_w_out, D_MODEL), layer=l,
                      n_layers=depth, prev=r_wo, name=f"adamw_w_out_{tag}", tr=rows_out)
        r_wg = _adamw(land_rest, g_off, flat(w_ple_gate, D_MODEL), flat(m_w_ple_gate, D_MODEL), flat(v_w_ple_gate, D_MODEL),
                      layer=l, n_layers=depth, prev=r_wg, name=f"adamw_w_gate_{tag}", tr=rows_out)
        r_wu = _adamw(land_rest, u_off, flat(w_ple_up, D_MODEL), flat(m_w_ple_up, D_MODEL), flat(v_w_ple_up, D_MODEL),
                      layer=l, n_layers=depth, prev=r_wu, name=f"adamw_w_up_{tag}", tr=up_rows)
        done = [r_wo[0], r_wg[0], r_wu[0]]
    for l in reversed(range(depth)):
        tag = f"l{l}"
        land_in = _push_wait(sent["win", l], done, broadcast=False, name=f"exchange_w_in_{tag}_wait")
        r_win = _adamw(land_in, 0, flat(w_in, SHARD_IN), flat(m_w_in, SHARD_IN), flat(v_w_in, SHARD_IN), layer=l,
                       n_layers=depth, prev=r_win, name=f"adamw_w_in_{tag}", tr=256)
        done = [r_win[0]]
    r_win = [o.reshape(w_in.shape) for o in r_win]
    r_wo = [o.reshape(w_out.shape) for o in r_wo]
    r_wg = [o.reshape(w_ple_gate.shape) for o in r_wg]
    r_wu = [o.reshape(w_ple_up.shape) for o in r_wu]

    def order(small_list, big_in, big_out, big_up, big_gate):
        nw, cw, al_, dt_, dnw, lbl, hnw, fw = small_list
        return [nw, big_in, cw, al_, dt_, dnw, lbl, hnw, big_out, big_up, big_gate, fw]

    outs = [loss, grad_x]
    for i, sl in enumerate((sg, sd, sm, sv_)):
        outs += order(sl, r_win[i], r_wo[i], r_wu[i], r_wg[i])
    return tuple(outs)
```

```python
import functools

import jax
import jax.numpy as jnp
from jax import lax
from jax.experimental import pallas as pl
from jax.experimental.pallas import tpu as pltpu

F32 = jnp.float32
BF16 = jnp.bfloat16
HIGHEST = lax.Precision.HIGHEST

N_DEV = 8
D_MODEL = 2048
PLE_DIM = 256
HEAD_DIM = 128
N_HEADS = 8
BR_WIDTH = N_HEADS * HEAD_DIM
CHUNK = 64
SUB = 16
CONV_W = 4
NORM_EPS = 1e-6
L2_EPS = 1e-6
IN_WIDTH = 8208
SHARD_IN = IN_WIDTH // N_DEV
EXP_CLAMP = 80.0

C_QKV, C_Z, C_HQ, C_HF, C_HI, C_HZ, C_B, C_A, N_PROJ = 0, 3072, 4096, 5120, 6144, 7168, 8192, 8320, 8448

ADAM_LR, ADAM_B1, ADAM_B2, ADAM_EPS, ADAM_WD, ADAM_STEP = 0.001, 0.9, 0.999, 1e-08, 0.01, 10

VMEM_LIMIT = 48 * 1024 * 1024


def _cp(*sem):
    return pltpu.CompilerParams(dimension_semantics=sem, vmem_limit_bytes=VMEM_LIMIT)


class _Heads:
    def __init__(self, vals):
        self.v = tuple(vals)

    def __add__(self, o):
        return _hmap(lambda a, b: a + b, self, o)

    def __radd__(self, o):
        return _hmap(lambda a, b: b + a, self, o)

    def __sub__(self, o):
        return _hmap(lambda a, b: a - b, self, o)

    def __rsub__(self, o):
        return _hmap(lambda a, b: b - a, self, o)

    def __mul__(self, o):
        return _hmap(lambda a, b: a * b, self, o)

    def __rmul__(self, o):
        return _hmap(lambda a, b: b * a, self, o)

    def __neg__(self):
        return _hmap(lambda a: -a, self)

    def __getitem__(self, idx):
        return _hmap(lambda a: a[idx], self)


def _hmap(fn, *args):
    n = next((len(a.v) for a in args if isinstance(a, _Heads)), None)
    if n is None:
        return fn(*args)
    return _Heads(fn(*[a.v[i] if isinstance(a, _Heads) else a for a in args]) for i in range(n))


def _dot(a, b, ca, cb):
    return _hmap(lambda x, y: lax.dot_general(x.astype(BF16), y.astype(BF16), (((ca,), (cb,)), ((), ())),
                                              preferred_element_type=F32), a, b)


def _nn(a, b):
    return _dot(a, b, 1, 0)


def _nt(a, b):
    return _dot(a, b, 1, 1)


def _tn(a, b):
    return _dot(a, b, 0, 0)


def _split(a):
    hi = _hmap(lambda x: x.astype(BF16), a)
    return hi, _hmap(lambda x, h: (x - h.astype(F32)).astype(BF16), a, hi)


def _dot3(a, b, ca, cb):
    ah, al = _split(a)
    bh, bl = _split(b)
    return _dot(ah, bh, ca, cb) + (_dot(ah, bl, ca, cb) + _dot(al, bh, ca, cb))


def _nn_exact(a, b):
    return _hmap(lambda y: lax.dot_general(a, y, (((1,), (0,)), ((), ())), precision=HIGHEST,
                                           preferred_element_type=F32), b)


def _exp(x):
    return _hmap(jnp.exp, x)


def _sum(x, axis):
    return _hmap(lambda a: jnp.sum(a, axis=axis, keepdims=True), x)


def _sigmoid(x):
    return jax.nn.sigmoid(x)


def _silu(x):
    return x * _sigmoid(x)


def _dsilu(x):
    s = _sigmoid(x)
    return s * (1.0 + x * (1.0 - s))


def _silu_and_grad(x):
    s = _sigmoid(x)
    return x * s, s * (1.0 + x * (1.0 - s))


def _softplus(x):
    return jnp.maximum(x, 0.0) + jnp.log(1.0 + jnp.exp(-jnp.abs(x)))


def _iota2(n, m, axis):
    return lax.broadcasted_iota(jnp.int32, (n, m), axis)


def _col2row(col, eye):
    return _hmap(lambda c: jnp.sum(eye * c, axis=0, keepdims=True), col)


def _row2col(row, eye):
    return _hmap(lambda r: jnp.sum(eye * r, axis=1, keepdims=True), row)


def _pick_lane(block, lane_idx):
    lane = _iota2(block.shape[0], block.shape[1], 1)
    return jnp.sum(jnp.where(lane == lane_idx, block, 0.0), axis=1, keepdims=True)


MM_TILE_M, MM_TILE_N, MM_TILE_K = 1024, 1408, 2048


def _tile(dim, cap):
    if dim <= cap:
        return dim
    t = cap - cap % 128
    while dim % t:
        t -= 128
    return t


def _mm(a, b, *, mode, out_dtype, res=None, after=(), name):
    if mode == "nn":
        (m, kd), (_, n) = a.shape, b.shape
    elif mode == "nt":
        (m, kd), (n, _) = a.shape, b.shape
    else:
        (kd, m), (_, n) = a.shape, b.shape
    tm, tn, tk = _tile(m, MM_TILE_M), _tile(n, MM_TILE_N), _tile(kd, MM_TILE_K)
    assert m % tm == 0 and n % tn == 0 and kd % tk == 0, (m, n, kd, tm, tn, tk)
    nk = kd // tk
    ca, cb = {"nn": (1, 0), "nt": (1, 1), "tn": (0, 0)}[mode]

    def body(*refs):
        a_ref, b_ref = refs[:2]
        r_ref = None if res is None else refs[2]
        o_ref, acc_ref = refs[-2:]
        k = pl.program_id(2)

        @pl.when(k == 0)
        def _():
            acc_ref[...] = jnp.zeros_like(acc_ref)

        acc_ref[...] += _dot(a_ref[...], b_ref[...], ca, cb)

        @pl.when(k == nk - 1)
        def _():
            out = acc_ref[...]
            if r_ref is not None:
                out = out + r_ref[...].astype(F32)
            o_ref[...] = out.astype(o_ref.dtype)

    a_spec = pl.BlockSpec((tk, tm), lambda i, j, k: (k, i)) if mode == "tn" else pl.BlockSpec((tm, tk), lambda i, j, k: (i, k))
    b_spec = pl.BlockSpec((tn, tk), lambda i, j, k: (j, k)) if mode == "nt" else pl.BlockSpec((tk, tn), lambda i, j, k: (k, j))
    o_spec = pl.BlockSpec((tm, tn), lambda i, j, k: (i, j))
    in_specs = [a_spec, b_spec] + ([o_spec] if res is not None else []) + [pl.BlockSpec(memory_space=pl.ANY)] * len(after)
    args = (a, b) + ((res,) if res is not None else ()) + tuple(after)
    return pl.pallas_call(
        body, name=name, grid=(m // tm, n // tn, nk), in_specs=in_specs, out_specs=o_spec,
        out_shape=jax.ShapeDtypeStruct((m, n), out_dtype),
        scratch_shapes=[pltpu.VMEM((tm, tn), F32)],
        compiler_params=_cp("parallel", "parallel", "arbitrary"),
    )(*args)


ROW_TILE = 256


def _rms_fwd(h, w, *, name):
    s, d = h.shape
    tr = min(ROW_TILE, s)

    def body(h_ref, w_ref, o_ref):
        x = h_ref[...]
        r = lax.rsqrt(jnp.mean(x * x, axis=-1, keepdims=True) + NORM_EPS)
        o_ref[...] = (x * r * w_ref[...]).astype(o_ref.dtype)

    return pl.pallas_call(
        body, name=name, grid=(s // tr,),
        in_specs=[pl.BlockSpec((tr, d), lambda i: (i, 0)), pl.BlockSpec((1, d), lambda i: (0, 0))],
        out_specs=pl.BlockSpec((tr, d), lambda i: (i, 0)),
        out_shape=jax.ShapeDtypeStruct((s, d), BF16), compiler_params=_cp("parallel"),
    )(h, w.reshape(1, d))


def _rms_bwd_math(x, w, dy):
    d = x.shape[-1]
    r = lax.rsqrt(jnp.mean(x * x, axis=-1, keepdims=True) + NORM_EPS)
    gw = dy * w
    dx = r * gw - x * ((r * r * r) * (jnp.sum(gw * x, axis=-1, keepdims=True) / d))
    return dx, dy * x * r


def _rms_bwd(h, w, dhn, res, *, name):
    s, d = h.shape
    tr = min(ROW_TILE, s)

    def body(h_ref, w_ref, g_ref, r_ref, dh_ref, dw_ref):
        @pl.when(pl.program_id(0) == 0)
        def _():
            dw_ref[...] = jnp.zeros_like(dw_ref)

        dx, dwt = _rms_bwd_math(h_ref[...], w_ref[...], g_ref[...])
        dh_ref[...] = r_ref[...] + dx
        dw_ref[...] += jnp.sum(dwt, axis=0, keepdims=True)

    row = pl.BlockSpec((tr, d), lambda i: (i, 0))
    vec = pl.BlockSpec((1, d), lambda i: (0, 0))
    return pl.pallas_call(
        body, name=name, grid=(s // tr,), in_specs=[row, vec, row, row], out_specs=[row, vec],
        out_shape=[jax.ShapeDtypeStruct((s, d), F32), jax.ShapeDtypeStruct((1, d), F32)],
        compiler_params=_cp("arbitrary"),
    )(h, w.reshape(1, d), dhn, res)


def _final_fwd_bwd(h, w, tgt, *, name):
    s, d = h.shape
    tr = min(ROW_TILE, s)

    def body(h_ref, w_ref, t_ref, loss_ref, dh_ref, dw_ref):
        @pl.when(pl.program_id(0) == 0)
        def _():
            loss_ref[...] = jnp.zeros_like(loss_ref)
            dw_ref[...] = jnp.zeros_like(dw_ref)

        x = h_ref[...]
        wv = w_ref[...]
        r = lax.rsqrt(jnp.mean(x * x, axis=-1, keepdims=True) + NORM_EPS)
        err = x * r * wv - t_ref[...]
        row_loss = jnp.mean(err * err, axis=-1, keepdims=True)
        loss_ref[...] += 0.5 * jnp.sum(row_loss, axis=0, keepdims=True)
        dx, dwt = _rms_bwd_math(x, wv, err / d)
        dh_ref[...] = dx
        dw_ref[...] += jnp.sum(dwt, axis=0, keepdims=True)

    row = pl.BlockSpec((tr, d), lambda i: (i, 0))
    vec = pl.BlockSpec((1, d), lambda i: (0, 0))
    return pl.pallas_call(
        body, name=name, grid=(s // tr,), in_specs=[row, vec, row],
        out_specs=[pl.BlockSpec((1, 128), lambda i: (0, 0)), row, vec],
        out_shape=[jax.ShapeDtypeStruct((1, 128), F32), jax.ShapeDtypeStruct((s, d), F32),
                   jax.ShapeDtypeStruct((1, d), F32)],
        compiler_params=_cp("arbitrary"),
    )(h, w.reshape(1, d), tgt)


def _ple_fwd(h1, gate_pre, up, *, name):
    s, d = h1.shape
    tr = min(ROW_TILE, s)

    def body(h_ref, g_ref, u_ref, o_ref):
        o_ref[...] = h_ref[...] + u_ref[...] * _sigmoid(g_ref[...])

    row = pl.BlockSpec((tr, d), lambda i: (i, 0))
    return pl.pallas_call(body, name=name, grid=(s // tr,), in_specs=[row, row, row], out_specs=row,
                          out_shape=jax.ShapeDtypeStruct((s, d), F32), compiler_params=_cp("parallel"))(h1, gate_pre, up)


def _ple_bwd(dh2, gate_pre, up, *, name):
    s, d = dh2.shape
    tr = min(ROW_TILE, s)

    def body(d_ref, g_ref, u_ref, dup_ref, dgp_ref):
        dh = d_ref[...]
        gate = _sigmoid(g_ref[...])
        dup_ref[...] = (dh * gate).astype(BF16)
        dgp_ref[...] = (dh * u_ref[...] * gate * (1.0 - gate)).astype(BF16)

    row = pl.BlockSpec((tr, d), lambda i: (i, 0))
    return pl.pallas_call(body, name=name, grid=(s // tr,), in_specs=[row, row, row], out_specs=[row, row],
                          out_shape=[jax.ShapeDtypeStruct((s, d), BF16)] * 2, compiler_params=_cp("parallel"))(dh2, gate_pre, up)


HN_TILE = 512


def _hnorm_fwd(o, proj, z_col, w, *, name):
    s = o.shape[0]
    tr = min(HN_TILE, s)

    def body(o_ref, z_ref, w_ref, y_ref):
        wv = w_ref[...]
        for h in range(N_HEADS):
            cols = slice(h * HEAD_DIM, (h + 1) * HEAD_DIM)
            x = o_ref[:, cols]
            r = lax.rsqrt(jnp.mean(x * x, axis=-1, keepdims=True) + NORM_EPS)
            y_ref[:, cols] = (x * r * wv * _silu(z_ref[:, cols])).astype(BF16)

    blk = pl.BlockSpec((tr, BR_WIDTH), lambda i: (i, 0))
    return pl.pallas_call(
        body, name=name, grid=(s // tr,),
        in_specs=[blk, pl.BlockSpec((tr, BR_WIDTH), lambda i: (i, z_col // BR_WIDTH)), pl.BlockSpec((1, HEAD_DIM), lambda i: (0, 0))],
        out_specs=blk, out_shape=jax.ShapeDtypeStruct((s, BR_WIDTH), BF16), compiler_params=_cp("parallel"),
    )(o, proj, w.reshape(1, HEAD_DIM))


def _hnorm_bwd(o, proj, z_col, w, dy, dy_col, *, name):
    s = o.shape[0]
    tr = min(HN_TILE, s)

    def body(o_ref, z_ref, w_ref, dy_ref, do_ref, dz_ref, dw_ref):
        @pl.when(pl.program_id(0) == 0)
        def _():
            dw_ref[...] = jnp.zeros_like(dw_ref)

        wv = w_ref[...]
        dw = jnp.zeros((1, HEAD_DIM), F32)
        for h in range(N_HEADS):
            cols = slice(h * HEAD_DIM, (h + 1) * HEAD_DIM)
            x, z, g = o_ref[:, cols], z_ref[:, cols], dy_ref[:, cols]
            r = lax.rsqrt(jnp.mean(x * x, axis=-1, keepdims=True) + NORM_EPS)
            on = x * r * wv
            silu_z, dsilu_z = _silu_and_grad(z)
            don = g * silu_z
            dz_ref[:, cols] = (g * on * dsilu_z).astype(BF16)
            gw = don * wv
            do_ref[:, cols] = r * gw - x * ((r * r * r) * (jnp.sum(gw * x, axis=-1, keepdims=True) / HEAD_DIM))
            dw = dw + jnp.sum(don * x * r, axis=0, keepdims=True)
        dw_ref[...] += dw

    blk = pl.BlockSpec((tr, BR_WIDTH), lambda i: (i, 0))
    vec = pl.BlockSpec((1, HEAD_DIM), lambda i: (0, 0))
    return pl.pallas_call(
        body, name=name, grid=(s // tr,),
        in_specs=[blk, pl.BlockSpec((tr, BR_WIDTH), lambda i: (i, z_col // BR_WIDTH)), vec,
                  pl.BlockSpec((tr, BR_WIDTH), lambda i: (i, dy_col // BR_WIDTH))],
        out_specs=[blk, blk, vec],
        out_shape=[jax.ShapeDtypeStruct((s, BR_WIDTH), F32), jax.ShapeDtypeStruct((s, BR_WIDTH), BF16),
                   jax.ShapeDtypeStruct((1, HEAD_DIM), F32)],
        compiler_params=_cp("arbitrary"),
    )(o, proj, w.reshape(1, HEAD_DIM), dy)


def _conv_silu(x, w, s):
    row = _iota2(s, x.shape[1], 0)
    c = w[CONV_W - 1:CONV_W, :] * x
    for k in range(1, CONV_W):
        c = c + w[CONV_W - 1 - k:CONV_W - k, :] * jnp.where(row >= k, pltpu.roll(x, k, 0), 0.0)
    return c


def _dn_qkv_fwd(proj, conv_w, *, name):
    s = proj.shape[0]
    nb = 3 * N_HEADS

    def body(x_ref, w_ref, o_ref):
        j = pl.program_id(0)
        sv = _silu(_conv_silu(x_ref[...], w_ref[...], s))
        r = lax.rsqrt(jnp.sum(sv * sv, axis=-1, keepdims=True) + L2_EPS)
        scale = jnp.where(j < N_HEADS, HEAD_DIM ** -0.5, 1.0).astype(F32)
        o_ref[...] = jnp.where(j < 2 * N_HEADS, sv * r * scale, sv)

    return pl.pallas_call(
        body, name=name, grid=(nb,),
        in_specs=[pl.BlockSpec((s, HEAD_DIM), lambda j: (0, j)), pl.BlockSpec((CONV_W, HEAD_DIM), lambda j: (0, j))],
        out_specs=pl.BlockSpec((s, HEAD_DIM), lambda j: (0, j)),
        out_shape=jax.ShapeDtypeStruct((s, 3 * BR_WIDTH), F32), compiler_params=_cp("parallel"),
    )(proj, conv_w)


def _dn_qkv_bwd(proj, conv_w, dqkv, *, name):
    s = proj.shape[0]
    nb = 3 * N_HEADS

    def body(x_ref, w_ref, g_ref, dx_ref, dw_ref):
        j = pl.program_id(0)
        x, w, g = x_ref[...], w_ref[...], g_ref[...]
        c = _conv_silu(x, w, s)
        sv, dsv = _silu_and_grad(c)
        r = lax.rsqrt(jnp.sum(sv * sv, axis=-1, keepdims=True) + L2_EPS)
        scale = jnp.where(j < N_HEADS, HEAD_DIM ** -0.5, 1.0).astype(F32)
        ds_n = scale * (r * g - sv * ((r * r * r) * jnp.sum(g * sv, axis=-1, keepdims=True)))
        dc = jnp.where(j < 2 * N_HEADS, ds_n, g) * dsv
        row = _iota2(s, HEAD_DIM, 0)
        dx = w[CONV_W - 1:CONV_W, :] * dc
        dws = [jnp.sum(dc * x, axis=0, keepdims=True)]
        for k in range(1, CONV_W):
            dx = dx + w[CONV_W - 1 - k:CONV_W - k, :] * jnp.where(row < s - k, pltpu.roll(dc, s - k, 0), 0.0)
            dws.append(jnp.sum(dc * jnp.where(row >= k, pltpu.roll(x, k, 0), 0.0), axis=0, keepdims=True))
        dx_ref[...] = dx.astype(BF16)
        for k in range(CONV_W):
            dw_ref[CONV_W - 1 - k:CONV_W - k, :] = dws[k]

    blk = pl.BlockSpec((s, HEAD_DIM), lambda j: (0, j))
    wblk = pl.BlockSpec((CONV_W, HEAD_DIM), lambda j: (0, j))
    return pl.pallas_call(
        body, name=name, grid=(nb,), in_specs=[blk, wblk, blk], out_specs=[blk, wblk],
        out_shape=[jax.ShapeDtypeStruct((s, 3 * BR_WIDTH), BF16), jax.ShapeDtypeStruct((CONV_W, 3 * BR_WIDTH), F32)],
        compiler_params=_cp("parallel"),
    )(proj, conv_w, dqkv)


def _tri(n, kind):
    r, c = _iota2(n, n, 0), _iota2(n, n, 1)
    if kind == "lower":
        return (r >= c).astype(F32)
    if kind == "upper":
        return (r <= c).astype(F32)
    return (r == c).astype(F32)


GATE_TILE = 512


def _dn_gate_fwd(proj, a_log, dt_bias, *, name):
    s = proj.shape[0]
    tr = min(GATE_TILE, s)

    def body(b_ref, a_ref, al_ref, dt_ref, beta_ref, g_ref):
        beta_ref[...] = _sigmoid(b_ref[...])
        g = -jnp.exp(al_ref[...]) * _softplus(a_ref[...] + dt_ref[...])
        low = _tri(CHUNK, "lower")
        for c in range(tr // CHUNK):
            rows = slice(c * CHUNK, (c + 1) * CHUNK)
            g_ref[rows, :] = _nn_exact(low, g[rows, :])

    blk = lambda cb: pl.BlockSpec((tr, HEAD_DIM), lambda i: (i, cb))
    vec = pl.BlockSpec((1, HEAD_DIM), lambda i: (0, 0))
    out = pl.BlockSpec((tr, HEAD_DIM), lambda i: (i, 0))
    return pl.pallas_call(
        body, name=name, grid=(s // tr,), in_specs=[blk(C_B // HEAD_DIM), blk(C_A // HEAD_DIM), vec, vec],
        out_specs=[out, out], out_shape=[jax.ShapeDtypeStruct((s, HEAD_DIM), F32)] * 2, compiler_params=_cp("parallel"),
    )(proj, proj, a_log, dt_bias)


def _dn_gate_bwd(proj, a_log, dt_bias, dbeta, d_g, *, name):
    s = proj.shape[0]
    tr = min(GATE_TILE, s)

    def body(b_ref, a_ref, al_ref, dt_ref, dbeta_ref, dG_ref, db_ref, da_ref, dal_ref, ddt_ref):
        @pl.when(pl.program_id(0) == 0)
        def _():
            dal_ref[...] = jnp.zeros_like(dal_ref)
            ddt_ref[...] = jnp.zeros_like(ddt_ref)

        beta = _sigmoid(b_ref[...])
        db_ref[...] = (dbeta_ref[...] * beta * (1.0 - beta)).astype(BF16)
        pre = a_ref[...] + dt_ref[...]
        neg_ea = -jnp.exp(al_ref[...])
        up = _tri(CHUNK, "upper")
        d_g = dG_ref[...]
        dg = jnp.concatenate([_nn_exact(up, d_g[c * CHUNK:(c + 1) * CHUNK, :]) for c in range(tr // CHUNK)], axis=0)
        da = dg * neg_ea * _sigmoid(pre)
        da_ref[...] = da.astype(BF16)
        ddt_ref[...] += jnp.sum(da, axis=0, keepdims=True)
        dal_ref[...] += jnp.sum(dg * neg_ea * _softplus(pre), axis=0, keepdims=True)

    blk = lambda cb: pl.BlockSpec((tr, HEAD_DIM), lambda i: (i, cb))
    vec = pl.BlockSpec((1, HEAD_DIM), lambda i: (0, 0))
    io = pl.BlockSpec((tr, HEAD_DIM), lambda i: (i, 0))
    return pl.pallas_call(
        body, name=name, grid=(s // tr,),
        in_specs=[blk(C_B // HEAD_DIM), blk(C_A // HEAD_DIM), vec, vec, io, io], out_specs=[io, io, vec, vec],
        out_shape=[jax.ShapeDtypeStruct((s, HEAD_DIM), BF16)] * 2 + [jax.ShapeDtypeStruct((1, HEAD_DIM), F32)] * 2,
        compiler_params=_cp("arbitrary"),
    )(proj, proj, a_log, dt_bias, dbeta, d_g)


def _unit_lower_inverse(a_strict, eye):
    x = -a_strict
    t = x + eye
    p = x
    n = 2
    while n < CHUNK:
        p = _nn(p, p)
        t = t + _nn(t, p)
        n *= 2
    return t


def _rows(*xs):
    return _hmap(lambda *a: jnp.concatenate(a, axis=0), *xs)


def _lanes(*xs):
    return _hmap(lambda *a: jnp.concatenate(a, axis=1), *xs)


def _dn_chunk_common(q, k, v, gc, beta, st, with_qd_state):
    c, d = CHUNK, HEAD_DIM
    eye = _tri(c, "eye")
    low = _tri(c, "lower")
    strict = low - eye
    grow = _col2row(gc, eye)
    dec = _hmap(lambda g_, gr: low * jnp.exp(low * (g_ - gr)), gc, grow)
    kb = k * beta
    kq = _nt(_rows(kb, q), k)
    a_mat = kq[0:c, :] * dec * strict
    qk = kq[c:2 * c, :] * dec
    t_inv = _unit_lower_inverse(a_mat, eye)
    e_g = _exp(gc)
    qd = q * e_g
    uw = _nn(t_inv, _lanes(v * beta, kb * e_g))
    u, w = uw[:, 0:d], uw[:, d:2 * d]
    last = (_iota2(c, 1, 0) == c - 1).astype(F32)
    g_last = _sum(gc * last, 0)
    e_t = _exp(g_last - gc)
    kt = k * e_t
    tail = _exp(g_last)
    if with_qd_state:
        ws = _nn(_rows(w, qd), st)
        vn, qds = u - ws[0:c, :], ws[c:2 * c, :]
    else:
        vn, qds = u - _nn(w, st), None
    return dict(eye=eye, low=low, strict=strict, dec=dec, kb=kb, a_mat=a_mat, t_inv=t_inv, e_g=e_g, u=u, w=w, uw=uw,
                qk=qk, qd=qd, qds=qds, last=last, e_t=e_t, kt=kt, tail=tail, vn=vn)


def _dn_chunk_fwd_math(q, k, v, gc, beta, st):
    m = _dn_chunk_common(q, k, v, gc, beta, st, True)
    o = m["qds"] + _nn(m["qk"], m["vn"])
    st2 = st * m["tail"] + _tn(m["kt"], m["vn"])
    return o, st2


def _dn_chunk_bwd_math(q, k, v, gc, beta, st, do, dst2):
    c, d = CHUNK, HEAD_DIM
    m = _dn_chunk_common(q, k, v, gc, beta, st, False)
    eye, low, strict = m["eye"], m["low"], m["strict"]
    dvn = _tn(m["qk"], do) + _nn(m["kt"], dst2)
    dqk = _nt(do, m["vn"]) * low
    both = _rows(do, dvn)
    ds_both = _nt(both, st)
    dqd, dw = ds_both[0:c, :], -ds_both[c:2 * c, :]
    dst = _tn(_rows(m["qd"], -m["w"]), both) + dst2 * m["tail"]
    dkt = _nt(m["vn"], dst2)
    dtail = _sum(_sum(st * dst2, 1), 0)
    dvb_dkg = _tn(m["t_inv"], _lanes(dvn, dw))
    dvb, dkg = dvb_dkg[:, 0:d], dvb_dkg[:, d:2 * d]
    d_a = _nt(dvb_dkg, m["uw"]) * (-strict)
    dkk = d_a * m["dec"]
    dp = dqk * m["dec"]
    dpk = _rows(dp, dkk)
    dq_dkb = _nn(dpk, k)
    dq = dq_dkb[0:c, :] + dqd * m["e_g"]
    dkb = dq_dkb[c:2 * c, :] + dkg * m["e_g"]
    dk = _tn(dpk, _rows(q, m["kb"])) + dkb * beta + dkt * m["e_t"]
    dv = dvb * beta
    dbeta = _sum(dvb * v + dkb * k, 1)
    de_g = _sum(dkg * m["kb"] + dqd * q, 1)
    de_t = _sum(dkt * k, 1)
    mm = d_a * m["a_mat"] + dqk * m["qk"]
    dgc = (_sum(mm, 1) - _row2col(_sum(mm, 0), eye) + de_g * m["e_g"] - de_t * m["e_t"]
           + (_sum(de_t * m["e_t"], 0) + dtail * m["tail"]) * m["last"])
    return dq, dk, dv, dgc, dbeta, dst


def _heads_of(ref):
    return _Heads(ref[:, h * HEAD_DIM:(h + 1) * HEAD_DIM] for h in range(N_HEADS))


def _lanes_of(block):
    return _Heads(_pick_lane(block, h) for h in range(N_HEADS))


def _dn_chunk_fwd(qkv, gcs, beta, *, name):
    s = qkv.shape[0]
    n = s // CHUNK

    def body(q_ref, k_ref, v_ref, g_ref, b_ref, o_ref, st_out_ref, st_ref):
        @pl.when(pl.program_id(0) == 0)
        def _():
            st_ref[...] = jnp.zeros_like(st_ref)

        gblk, bblk = g_ref[...], b_ref[...]
        st = _Heads(st_ref[h] for h in range(N_HEADS))
        o, st2 = _dn_chunk_fwd_math(_heads_of(q_ref), _heads_of(k_ref), _heads_of(v_ref), _lanes_of(gblk),
                                    _lanes_of(bblk), st)
        for h in range(N_HEADS):
            st_out_ref[0, h] = st.v[h]
            o_ref[:, h * HEAD_DIM:(h + 1) * HEAD_DIM] = o.v[h]
            st_ref[h] = st2.v[h]

    blk = lambda off: pl.BlockSpec((CHUNK, BR_WIDTH), lambda c: (c, off))
    sc = pl.BlockSpec((CHUNK, HEAD_DIM), lambda c: (c, 0))
    return pl.pallas_call(
        body, name=name, grid=(n,),
        in_specs=[blk(0), blk(1), blk(2), sc, sc],
        out_specs=[blk(0), pl.BlockSpec((1, N_HEADS, HEAD_DIM, HEAD_DIM), lambda c: (c, 0, 0, 0))],
        out_shape=[jax.ShapeDtypeStruct((s, BR_WIDTH), F32), jax.ShapeDtypeStruct((n, N_HEADS, HEAD_DIM, HEAD_DIM), F32)],
        scratch_shapes=[pltpu.VMEM((N_HEADS, HEAD_DIM, HEAD_DIM), F32)],
        compiler_params=_cp("arbitrary"),
    )(qkv, qkv, qkv, gcs, beta)


def _dn_chunk_bwd(qkv, gcs, beta, states, do, *, name):
    s = qkv.shape[0]
    n = s // CHUNK

    def body(q_ref, k_ref, v_ref, g_ref, b_ref, st_in_ref, do_ref, dqkv_ref, dg_ref, dbeta_ref, dst_ref):
        @pl.when(pl.program_id(0) == 0)
        def _():
            dst_ref[...] = jnp.zeros_like(dst_ref)

        gblk, bblk = g_ref[...], b_ref[...]
        lane = _iota2(CHUNK, HEAD_DIM, 1)
        dg_all = jnp.zeros((CHUNK, HEAD_DIM), F32)
        dbeta_all = jnp.zeros((CHUNK, HEAD_DIM), F32)
        dq, dk, dv, dgc, dbeta, dst = _dn_chunk_bwd_math(
            _heads_of(q_ref), _heads_of(k_ref), _heads_of(v_ref), _lanes_of(gblk), _lanes_of(bblk),
            _Heads(st_in_ref[0, h] for h in range(N_HEADS)), _heads_of(do_ref),
            _Heads(dst_ref[h] for h in range(N_HEADS)))
        for h in range(N_HEADS):
            for part, val in enumerate((dq, dk, dv)):
                c0 = part * BR_WIDTH + h * HEAD_DIM
                dqkv_ref[:, c0:c0 + HEAD_DIM] = val.v[h]
            dg_all = jnp.where(lane == h, dgc.v[h], dg_all)
            dbeta_all = jnp.where(lane == h, dbeta.v[h], dbeta_all)
            dst_ref[h] = dst.v[h]
        dg_ref[...] = dg_all
        dbeta_ref[...] = dbeta_all

    blk = lambda off: pl.BlockSpec((CHUNK, BR_WIDTH), lambda c: (n - 1 - c, off))
    sc = pl.BlockSpec((CHUNK, HEAD_DIM), lambda c: (n - 1 - c, 0))
    outs = pl.pallas_call(
        body, name=name, grid=(n,),
        in_specs=[blk(0), blk(1), blk(2), sc, sc,
                  pl.BlockSpec((1, N_HEADS, HEAD_DIM, HEAD_DIM), lambda c: (n - 1 - c, 0, 0, 0)), blk(0)],
        out_specs=[pl.BlockSpec((CHUNK, 3 * BR_WIDTH), lambda c: (n - 1 - c, 0)), sc, sc],
        out_shape=[jax.ShapeDtypeStruct((s, 3 * BR_WIDTH), F32)] + [jax.ShapeDtypeStruct((s, HEAD_DIM), F32)] * 2,
        scratch_shapes=[pltpu.VMEM((N_HEADS, HEAD_DIM, HEAD_DIM), F32)],
        compiler_params=_cp("arbitrary"),
    )(qkv, qkv, qkv, gcs, beta, states, do)
    return outs


def _hg_prep_fwd(proj, lb, *, name):
    s = proj.shape[0]
    tr = min(ROW_TILE, s)

    def body(q_ref, f_ref, lb_ref, qo_ref, ko_ref, lf_ref):
        f, lbv = f_ref[...], lb_ref[...]
        qo_ref[...] = _silu(q_ref[...])
        ko_ref[...] = (1.0 - lbv) * _sigmoid(-f)
        lf_ref[...] = jnp.log(lbv + (1.0 - lbv) * _sigmoid(f))

    blk = lambda cb: pl.BlockSpec((tr, BR_WIDTH), lambda i: (i, cb))
    out = pl.BlockSpec((tr, BR_WIDTH), lambda i: (i, 0))
    return pl.pallas_call(
        body, name=name, grid=(s // tr,),
        in_specs=[blk(C_HQ // BR_WIDTH), blk(C_HF // BR_WIDTH), pl.BlockSpec((1, BR_WIDTH), lambda i: (0, 0))],
        out_specs=[out, out, out], out_shape=[jax.ShapeDtypeStruct((s, BR_WIDTH), F32)] * 3, compiler_params=_cp("parallel"),
    )(proj, proj, lb)


def _hg_prep_bwd(proj, lb, dq, dk, dlf, *, name):
    s = proj.shape[0]
    tr = min(ROW_TILE, s)

    def body(q_ref, f_ref, lb_ref, dq_ref, dk_ref, dlf_ref, dhq_ref, dhf_ref, dlb_ref):
        @pl.when(pl.program_id(0) == 0)
        def _():
            dlb_ref[...] = jnp.zeros_like(dlb_ref)

        f, lbv = f_ref[...], lb_ref[...]
        dhq_ref[...] = (dq_ref[...] * _dsilu(q_ref[...])).astype(BF16)
        sp, sn = _sigmoid(f), _sigmoid(-f)
        inner = lbv + (1.0 - lbv) * sp
        dlf_over = dlf_ref[...] / inner
        dkv = dk_ref[...]
        dhf_ref[...] = (dlf_over * (1.0 - lbv) * sp * sn - dkv * (1.0 - lbv) * sn * (1.0 - sn)).astype(BF16)
        dlb_ref[...] += jnp.sum(dlf_over * (1.0 - sp) - dkv * sn, axis=0, keepdims=True)

    blk = lambda cb: pl.BlockSpec((tr, BR_WIDTH), lambda i: (i, cb))
    io = pl.BlockSpec((tr, BR_WIDTH), lambda i: (i, 0))
    vec = pl.BlockSpec((1, BR_WIDTH), lambda i: (0, 0))
    return pl.pallas_call(
        body, name=name, grid=(s // tr,),
        in_specs=[blk(C_HQ // BR_WIDTH), blk(C_HF // BR_WIDTH), vec, io, io, io], out_specs=[io, io, vec],
        out_shape=[jax.ShapeDtypeStruct((s, BR_WIDTH), BF16)] * 2 + [jax.ShapeDtypeStruct((1, BR_WIDTH), F32)],
        compiler_params=_cp("arbitrary"),
    )(proj, proj, lb, dq, dk, dlf)


def _hg_chunk_common(q, k, g):
    c, nb = CHUNK, CHUNK // SUB
    e_g = _exp(g)
    qd = q * e_g
    g_last = g[c - 1:c, :]
    e_t = _exp(g_last - g)
    kt = k * e_t
    tail = _exp(g_last)
    g_refs = [g[i * SUB:i * SUB + 1, :] for i in range(nb)]
    g_ref_rows = _hmap(lambda *rows: jnp.concatenate([jnp.broadcast_to(r, (SUB, r.shape[1])) for r in rows], axis=0), *g_refs)
    e_q = _exp(g - g_ref_rows)
    q_sc = q * e_q
    e_k = [_hmap(lambda gr, g_: jnp.exp(jnp.minimum(gr - g_, EXP_CLAMP)), g_refs[i], g) for i in range(nb)]
    k_sc_all = _rows(*[k * e_k[i] for i in range(nb)])
    row_blk = _iota2(c, 1, 0) // SUB
    masks = [(row_blk == i).astype(F32) for i in range(nb)]
    r_all = _nt(q_sc, k_sc_all)
    a_mat = r_all[:, 0:c] * masks[0]
    for i in range(1, nb):
        a_mat = a_mat + r_all[:, i * c:(i + 1) * c] * masks[i]
    a_mat = a_mat * _tri(c, "lower")
    return dict(e_g=e_g, qd=qd, e_t=e_t, kt=kt, tail=tail, q_sc=q_sc, k_sc_all=k_sc_all, e_q=e_q, e_k=e_k, masks=masks,
                a_mat=a_mat)


def _hg_chunk_fwd_math(q, k, v, g, stt):
    m = _hg_chunk_common(q, k, g)
    o = _nt(m["qd"], stt) + _nn(m["a_mat"], v)
    stt2 = stt * m["tail"] + _tn(v, m["kt"])
    return o, stt2


def _hg_chunk_bwd_math(q, k, v, g, stt, do, dstt2):
    c, nb = CHUNK, CHUNK // SUB
    m = _hg_chunk_common(q, k, g)
    stt2 = stt * m["tail"] + _tn(v, m["kt"])
    later = _sum(stt2 * dstt2, 0)
    dqd = _dot3(do, stt, 1, 0)
    dstt = _tn(do, m["qd"]) + dstt2 * m["tail"]
    d_a = _dot3(do, v, 1, 1) * _tri(c, "lower")
    dv = _tn(m["a_mat"], do) + _nt(m["kt"], dstt2)
    dkt = _dot3(v, dstt2, 1, 0)
    d_blk = _lanes(*[d_a * m["masks"][i] for i in range(nb)])
    dq = dqd * m["e_g"] + _dot3(d_blk, m["k_sc_all"], 1, 0) * m["e_q"]
    dks = _dot3(d_blk, m["q_sc"], 0, 0)
    dk = dkt * m["e_t"]
    for i in range(nb):
        dk = dk + dks[i * c:(i + 1) * c, :] * m["e_k"][i]
    db = q * dq - k * dk
    return dq, dk, dv, db, later, dstt


def _hg_chunk_fwd(qh, kh, proj, lf, *, name):
    s = qh.shape[0]
    n = s // CHUNK
    vb = C_HI // BR_WIDTH

    def body(q_ref, k_ref, v_ref, lf_ref, o_ref, st_out_ref, st_ref):
        @pl.when(pl.program_id(0) == 0)
        def _():
            st_ref[...] = jnp.zeros_like(st_ref)

        st = _Heads(st_ref[h] for h in range(N_HEADS))
        g_all = _nn_exact(_tri(CHUNK, "lower"), lf_ref[...])
        g = _Heads(g_all[:, h * HEAD_DIM:(h + 1) * HEAD_DIM] for h in range(N_HEADS))
        o, st2 = _hg_chunk_fwd_math(_heads_of(q_ref), _heads_of(k_ref), _heads_of(v_ref), g, st)
        for h in range(N_HEADS):
            st_out_ref[0, h] = st.v[h]
            o_ref[:, h * HEAD_DIM:(h + 1) * HEAD_DIM] = o.v[h]
            st_ref[h] = st2.v[h]

    blk = lambda off: pl.BlockSpec((CHUNK, BR_WIDTH), lambda c: (c, off))
    return pl.pallas_call(
        body, name=name, grid=(n,), in_specs=[blk(0), blk(0), blk(vb), blk(0)],
        out_specs=[blk(0), pl.BlockSpec((1, N_HEADS, HEAD_DIM, HEAD_DIM), lambda c: (c, 0, 0, 0))],
        out_shape=[jax.ShapeDtypeStruct((s, BR_WIDTH), F32), jax.ShapeDtypeStruct((n, N_HEADS, HEAD_DIM, HEAD_DIM), F32)],
        scratch_shapes=[pltpu.VMEM((N_HEADS, HEAD_DIM, HEAD_DIM), F32)],
        compiler_params=_cp("arbitrary"),
    )(qh, kh, proj, lf)


def _hg_chunk_bwd(qh, kh, proj, lf, states, do, *, name):
    s = qh.shape[0]
    n = s // CHUNK
    vb = C_HI // BR_WIDTH

    def body(q_ref, k_ref, v_ref, lf_ref, st_in_ref, do_ref, dq_ref, dk_ref, dv_ref, dlf_ref, dst_ref):
        @pl.when(pl.program_id(0) == 0)
        def _():
            dst_ref[...] = jnp.zeros_like(dst_ref)

        g_all = _nn_exact(_tri(CHUNK, "lower"), lf_ref[...])
        g = _Heads(g_all[:, h * HEAD_DIM:(h + 1) * HEAD_DIM] for h in range(N_HEADS))
        dq, dk, dv, db, later, dst = _hg_chunk_bwd_math(
            _heads_of(q_ref), _heads_of(k_ref), _heads_of(v_ref), g,
            _Heads(st_in_ref[0, h] for h in range(N_HEADS)), _heads_of(do_ref),
            _Heads(dst_ref[h] for h in range(N_HEADS)))
        dlf_ref[...] = (_nn_exact(_tri(CHUNK, "upper"), jnp.concatenate(db.v, axis=1))
                        + jnp.concatenate(later.v, axis=1))
        for h in range(N_HEADS):
            cols = slice(h * HEAD_DIM, (h + 1) * HEAD_DIM)
            dq_ref[:, cols] = dq.v[h]
            dk_ref[:, cols] = dk.v[h]
            dv_ref[:, cols] = dv.v[h].astype(BF16)
            dst_ref[h] = dst.v[h]

    blk = lambda off: pl.BlockSpec((CHUNK, BR_WIDTH), lambda c: (n - 1 - c, off))
    return pl.pallas_call(
        body, name=name, grid=(n,),
        in_specs=[blk(0), blk(0), blk(vb), blk(0),
                  pl.BlockSpec((1, N_HEADS, HEAD_DIM, HEAD_DIM), lambda c: (n - 1 - c, 0, 0, 0)), blk(0)],
        out_specs=[blk(0), blk(0), blk(0), blk(0)],
        out_shape=[jax.ShapeDtypeStruct((s, BR_WIDTH), F32)] * 2 + [jax.ShapeDtypeStruct((s, BR_WIDTH), BF16),
                                                                    jax.ShapeDtypeStruct((s, BR_WIDTH), F32)],
        scratch_shapes=[pltpu.VMEM((N_HEADS, HEAD_DIM, HEAD_DIM), F32)],
        compiler_params=_cp("arbitrary"),
    )(qh, kh, proj, lf, states, do)


_ANY = pl.BlockSpec(memory_space=pl.ANY)
_MESH = pl.DeviceIdType.MESH


def _all_gather(x_local, *, name, after=()):
    n_after = len(after)

    def body(x_ref, *refs):
        out_ref, send_sems, recv_sems, local_sem = refs[n_after:]
        x, y, c = lax.axis_index("x"), lax.axis_index("y"), lax.axis_index("c")
        me, sibling = (x, y, c), (x, y, 1 - c)
        chips = [(1 - x, y), (x, 1 - y), (1 - x, 1 - y)]

        def slot(px, py, pc):
            return out_ref.at[4 * px + 2 * py + pc]

        def copy(k, block, to, src=None):
            return pltpu.make_async_remote_copy(
                src_ref=slot(*block) if src is None else src, dst_ref=slot(*block),
                send_sem=send_sems.at[k], recv_sem=recv_sems.at[k], device_id=to, device_id_type=_MESH)

        mine = pltpu.make_async_copy(x_ref, slot(*me), local_sem)
        mine.start()
        first = [copy(0, me, sibling, src=x_ref)]
        first += [copy(1 + j, me, (*chip, c), src=x_ref) for j, chip in enumerate(chips)]
        for cp in first:
            cp.start()
        passed = [copy(4 + j, (*chip, c), sibling) for j, chip in enumerate(chips)]
        for j, chip in enumerate(chips):
            copy(1 + j, (*chip, c), me).wait_recv()
            passed[j].start()
        copy(0, sibling, me).wait_recv()
        for j, chip in enumerate(chips):
            copy(4 + j, (*chip, 1 - c), me).wait_recv()
        for cp in first + passed:
            cp.wait_send()
        mine.wait()

    return pl.pallas_call(
        body, name=name, out_shape=jax.ShapeDtypeStruct((N_DEV,) + x_local.shape, x_local.dtype),
        in_specs=[_ANY] * (1 + n_after), out_specs=_ANY,
        scratch_shapes=[pltpu.SemaphoreType.DMA((7,)), pltpu.SemaphoreType.DMA((7,)), pltpu.SemaphoreType.DMA],
    )(x_local, *after)


_HBM = pl.BlockSpec(memory_space=pltpu.HBM)
_SEM = pl.BlockSpec(memory_space=pltpu.SEMAPHORE)
_EFFECT = pltpu.SideEffectType.DATAFLOW_SIDE_EFFECTING


def _peers():
    x, y, c = lax.axis_index("x"), lax.axis_index("y"), lax.axis_index("c")
    out = []
    for k in range(1, N_DEV):
        px, py, pc = x ^ ((k >> 2) & 1), y ^ ((k >> 1) & 1), c ^ (k & 1)
        out.append(((px, py, pc), 4 * px + 2 * py + pc))
    return 4 * x + 2 * y + c, out


def _push_copies(src_ref, land_ref, send_sems, recv_sems, broadcast):
    my, peers = _peers()
    pairs = []
    for k, (pos, idx) in enumerate(peers):
        src = src_ref if broadcast else src_ref.at[idx]
        send = pltpu.make_async_remote_copy(src_ref=src, dst_ref=land_ref.at[my], send_sem=send_sems.at[k],
                                            recv_sem=recv_sems.at[k], device_id=pos, device_id_type=_MESH)
        recv = pltpu.make_async_remote_copy(src_ref=src, dst_ref=land_ref.at[idx], send_sem=send_sems.at[k],
                                            recv_sem=recv_sems.at[k], device_id=pos, device_id_type=_MESH)
        pairs.append((send, recv))
    return pairs


def _push_start(src, land, *, broadcast, name, after=()):
    n_after = len(after)

    def body(src_ref, land_ref, *refs):
        send_sems, recv_sems, _, _, token = refs[n_after:]
        for send, _ in _push_copies(src_ref, land_ref, send_sems, recv_sems, broadcast):
            send.start()
        token[...] = jnp.zeros_like(token)

    return pl.pallas_call(
        body, name=name,
        out_shape=(pltpu.SemaphoreType.DMA((N_DEV - 1,)), pltpu.SemaphoreType.DMA((N_DEV - 1,)),
                   pltpu.HBM(src.shape, src.dtype), pltpu.HBM(land.shape, land.dtype), jax.ShapeDtypeStruct((8, 128), F32)),
        in_specs=(_HBM, _HBM) + (_ANY,) * n_after, out_specs=(_SEM, _SEM, _HBM, _HBM, pl.BlockSpec(memory_space=pltpu.VMEM)),
        input_output_aliases={0: 2, 1: 3}, compiler_params=pltpu.CompilerParams(has_side_effects=_EFFECT),
    )(pltpu.with_memory_space_constraint(src, pltpu.HBM), pltpu.with_memory_space_constraint(land, pltpu.HBM), *after)


def _push_wait(handle, after, *, broadcast, name):
    send_sems, recv_sems, src_thru, land_thru, _ = handle

    def body(src_ref, land_ref, send_sems, recv_sems, *rest):
        for send, recv in _push_copies(src_ref, land_ref, send_sems, recv_sems, broadcast):
            send.wait_send()
            recv.wait_recv()

    return pl.pallas_call(
        body, name=name,
        out_shape=(pltpu.HBM(src_thru.shape, src_thru.dtype), pltpu.HBM(land_thru.shape, land_thru.dtype)),
        in_specs=(_HBM, _HBM, _SEM, _SEM) + (_ANY,) * len(after), out_specs=(_HBM, _HBM),
        input_output_aliases={0: 0, 1: 1}, compiler_params=pltpu.CompilerParams(has_side_effects=_EFFECT),
    )(src_thru, land_thru, send_sems, recv_sems, *after)[1]


def _relay_copies(src_ref, land_ref, sems_a, sems_b):
    x, y, c = lax.axis_index("x"), lax.axis_index("y"), lax.axis_index("c")
    slot = lambda px, py, pc: land_ref.at[4 * px + 2 * py + pc]
    chips = [(1 - x, y), (x, 1 - y), (1 - x, 1 - y)]
    (send_a, recv_a), (send_b, recv_b) = sems_a, sems_b

    def copy(sems, k, src, dst_slot, to):
        return pltpu.make_async_remote_copy(src_ref=src, dst_ref=dst_slot, send_sem=sems[0].at[k], recv_sem=sems[1].at[k],
                                            device_id=to, device_id_type=_MESH)

    first = [copy((send_a, recv_a), 0, src_ref, slot(x, y, c), (x, y, 1 - c))]
    first += [copy((send_a, recv_a), 1 + j, src_ref, slot(x, y, c), (*chip, c)) for j, chip in enumerate(chips)]
    first_in = [copy((send_a, recv_a), 0, src_ref, slot(x, y, 1 - c), (x, y, 1 - c))]
    first_in += [copy((send_a, recv_a), 1 + j, src_ref, slot(*chip, c), (*chip, c)) for j, chip in enumerate(chips)]
    relay = [copy((send_b, recv_b), j, slot(*chip, c), slot(*chip, c), (x, y, 1 - c)) for j, chip in enumerate(chips)]
    relay_in = [copy((send_b, recv_b), j, slot(*chip, 1 - c), slot(*chip, 1 - c), (x, y, 1 - c)) for j, chip in enumerate(chips)]
    return first, first_in, relay, relay_in


def _relay_start(src, land, *, name, after=()):
    n_after = len(after)

    def body(src_ref, land_ref, *refs):
        send_a, recv_a, _, _, token = refs[n_after:]
        for cp in _relay_copies(src_ref, land_ref, (send_a, recv_a), (send_a, recv_a))[0]:
            cp.start()
        token[...] = jnp.zeros_like(token)

    send_a, recv_a, src_thru, land_thru, token = pl.pallas_call(
        body, name=name,
        out_shape=(pltpu.SemaphoreType.DMA((4,)), pltpu.SemaphoreType.DMA((4,)), pltpu.HBM(src.shape, src.dtype),
                   pltpu.HBM(land.shape, land.dtype), jax.ShapeDtypeStruct((8, 128), F32)),
        in_specs=(_HBM, _HBM) + (_ANY,) * n_after, out_specs=(_SEM, _SEM, _HBM, _HBM, pl.BlockSpec(memory_space=pltpu.VMEM)),
        input_output_aliases={0: 2, 1: 3}, compiler_params=pltpu.CompilerParams(has_side_effects=_EFFECT),
    )(pltpu.with_memory_space_constraint(src, pltpu.HBM), pltpu.with_memory_space_constraint(land, pltpu.HBM), *after)
    return (send_a, recv_a), src_thru, land_thru, token


def _relay_mid(handle, after, *, name):
    sems_a, src_thru, land_thru, _ = handle
    n_after = len(after)

    def body(src_ref, land_ref, send_a, recv_a, *refs):
        send_b, recv_b, _, _, token = refs[n_after:]
        _, first_in, relay, _ = _relay_copies(src_ref, land_ref, (send_a, recv_a), (send_b, recv_b))
        for j in range(3):
            first_in[1 + j].wait_recv()
            relay[j].start()
        token[...] = jnp.zeros_like(token)

    send_b, recv_b, src2, land2, token = pl.pallas_call(
        body, name=name,
        out_shape=(pltpu.SemaphoreType.DMA((3,)), pltpu.SemaphoreType.DMA((3,)), pltpu.HBM(src_thru.shape, src_thru.dtype),
                   pltpu.HBM(land_thru.shape, land_thru.dtype), jax.ShapeDtypeStruct((8, 128), F32)),
        in_specs=(_HBM, _HBM, _SEM, _SEM) + (_ANY,) * n_after,
        out_specs=(_SEM, _SEM, _HBM, _HBM, pl.BlockSpec(memory_space=pltpu.VMEM)),
        input_output_aliases={0: 2, 1: 3}, compiler_params=pltpu.CompilerParams(has_side_effects=_EFFECT),
    )(src_thru, land_thru, *sems_a, *after)
    return sems_a, (send_b, recv_b), src2, land2, token


def _relay_wait(handle, after, *, name):
    sems_a, sems_b, src_thru, land_thru, _ = handle

    def body(src_ref, land_ref, send_a, recv_a, send_b, recv_b, *rest):
        first, first_in, relay, relay_in = _relay_copies(src_ref, land_ref, (send_a, recv_a), (send_b, recv_b))
        first_in[0].wait_recv()
        for cp in relay_in:
            cp.wait_recv()
        for cp in first + relay:
            cp.wait_send()

    return pl.pallas_call(
        body, name=name,
        out_shape=(pltpu.HBM(src_thru.shape, src_thru.dtype), pltpu.HBM(land_thru.shape, land_thru.dtype)),
        in_specs=(_HBM, _HBM, _SEM, _SEM, _SEM, _SEM) + (_ANY,) * len(after), out_specs=(_HBM, _HBM),
        input_output_aliases={0: 0, 1: 1}, compiler_params=pltpu.CompilerParams(has_side_effects=_EFFECT),
    )(src_thru, land_thru, *sems_a, *sems_b, *after)[1]


def _adamw(parts, row_off, w, m, v, *, layer=0, n_layers=1, prev=None, name, tr):
    rows, c = w.shape
    r = rows // n_layers
    np_ = parts.shape[0]
    tr = min(tr, r)
    assert r % tr == 0 and row_off % tr == 0
    ob, lb = row_off // tr, layer * (r // tr)
    c1 = 1.0 - ADAM_B1 ** ADAM_STEP
    c2 = 1.0 - ADAM_B2 ** ADAM_STEP
    n_prev = 0 if prev is None else 4

    def body(p_ref, w_ref, m_ref, v_ref, *refs):
        g_ref, d_ref, nm_ref, nv_ref = refs[n_prev:]
        g = p_ref[0].astype(F32)
        for s in range(1, np_):
            g = g + p_ref[s].astype(F32)
        wv = w_ref[...]
        m2 = ADAM_B1 * m_ref[...] + (1.0 - ADAM_B1) * g
        v2 = ADAM_B2 * v_ref[...] + (1.0 - ADAM_B2) * jnp.square(g)
        m_hat = m2 / c1
        v_hat = v2 / c2
        g_ref[...] = g
        d_ref[...] = -ADAM_LR * (m_hat / (jnp.sqrt(v_hat) + ADAM_EPS) + ADAM_WD * wv)
        nm_ref[...] = m2
        nv_ref[...] = v2

    blk = pl.BlockSpec((tr, c), lambda i: (lb + i, 0))
    return pl.pallas_call(
        body, name=name, grid=(r // tr,),
        in_specs=[pl.BlockSpec((np_, tr, c), lambda i: (0, ob + i, 0)), blk, blk, blk] + [_ANY] * n_prev,
        out_specs=[blk] * 4, out_shape=[jax.ShapeDtypeStruct((rows, c), F32)] * 4,
        input_output_aliases={4 + i: i for i in range(n_prev)}, compiler_params=_cp("parallel"),
    )(parts, w, m, v, *(prev or ()))


def _sum_parts(parts, *, name, after=()):
    np_, r, c = parts.shape

    def body(p_ref, *refs):
        o_ref = refs[-1]
        g = p_ref[0]
        for s in range(1, np_):
            g = g + p_ref[s]
        o_ref[...] = g

    vmem = pl.BlockSpec(memory_space=pltpu.VMEM)
    return pl.pallas_call(body, name=name, in_specs=[vmem] + [_ANY] * len(after), out_specs=vmem,
                          out_shape=jax.ShapeDtypeStruct((r, c), F32))(parts, *after)


def _pack(arrs):
    rows = []
    for a in arrs:
        f = a.reshape(-1).astype(F32)
        pad = (-f.shape[0]) % 128
        rows.append(jnp.pad(f, (0, pad)).reshape(-1, 128))
    out = jnp.concatenate(rows, axis=0)
    return jnp.pad(out, ((0, (-out.shape[0]) % 8), (0, 0)))


def _unpack(packed, shapes):
    outs, r0 = [], 0
    for shp in shapes:
        n = 1
        for d in shp:
            n *= d
        nr = -(-n // 128)
        outs.append(packed[r0:r0 + nr].reshape(-1)[:n].reshape(shp))
        r0 += nr
    return outs


_WIN_PIECES = ((0, 4096, 0), (4112, 8208, 0), (4096, 4104, HEAD_DIM - N_HEADS), (4104, 4112, HEAD_DIM - N_HEADS))


RELAYOUT_TILE = 256


def _win_from_shards(shards, *, name):
    k = shards.shape[1]
    tr = min(RELAYOUT_TILE, k)

    def body(x_ref, o_ref):
        cols = []
        for lo, hi, pad in _WIN_PIECES:
            for j in range(N_DEV):
                a, b = max(lo, j * SHARD_IN), min(hi, (j + 1) * SHARD_IN)
                if a < b:
                    cols.append(x_ref[j, :, a - j * SHARD_IN:b - j * SHARD_IN])
            if pad:
                cols.append(jnp.zeros((tr, pad), x_ref.dtype))
        o_ref[...] = jnp.concatenate(cols, axis=1)

    return pl.pallas_call(
        body, name=name, grid=(k // tr,), in_specs=[pl.BlockSpec((N_DEV, tr, SHARD_IN), lambda i: (0, i, 0))],
        out_specs=pl.BlockSpec((tr, N_PROJ), lambda i: (i, 0)), out_shape=jax.ShapeDtypeStruct((k, N_PROJ), shards.dtype),
        compiler_params=_cp("parallel"),
    )(shards)


def _win_to_shards(g, *, name):
    k = g.shape[0]
    tr = min(RELAYOUT_TILE, k)
    starts, off = [], 0
    for lo, hi, pad in _WIN_PIECES:
        starts.append((lo, hi, off))
        off += hi - lo + pad

    def body(g_ref, o_ref):
        for j in range(N_DEV):
            cols = []
            for lo, hi, off in sorted(starts):
                a, b = max(lo, j * SHARD_IN), min(hi, (j + 1) * SHARD_IN)
                if a < b:
                    cols.append(g_ref[:, off + a - lo:off + b - lo])
            o_ref[j] = jnp.concatenate(cols, axis=1)

    return pl.pallas_call(
        body, name=name, grid=(k // tr,), in_specs=[pl.BlockSpec((tr, N_PROJ), lambda i: (i, 0))],
        out_specs=pl.BlockSpec((N_DEV, tr, SHARD_IN), lambda i: (0, i, 0)),
        out_shape=jax.ShapeDtypeStruct((N_DEV, k, SHARD_IN), g.dtype), compiler_params=_cp("parallel"),
    )(g)


def _lower_bounds(logits):
    probs = jax.nn.softmax(logits.astype(F32), axis=0)
    return jnp.cumsum(probs, axis=0) - probs[0]


def _pad_lanes(vec8):
    return jnp.pad(vec8.reshape(1, N_HEADS), ((0, 0), (0, HEAD_DIM - N_HEADS)))


def kernel(x, p, norm_w, w_in, dn_conv_w, dn_A_log, dn_dt_bias, dn_norm_w, hg_lb_logits, hg_norm_w, w_out, w_ple_up, w_ple_gate, final_norm_w, loss_target, m_norm_w, m_w_in, m_dn_conv_w, m_dn_A_log, m_dn_dt_bias, m_dn_norm_w, m_hg_lb_logits, m_hg_norm_w, m_w_out, m_w_ple_up, m_w_ple_gate, m_final_norm_w, v_norm_w, v_w_in, v_dn_conv_w, v_dn_A_log, v_dn_dt_bias, v_dn_norm_w, v_hg_lb_logits, v_hg_norm_w, v_w_out, v_w_ple_up, v_w_ple_gate, v_final_norm_w):
    depth = norm_w.shape[0]
    my = 4 * lax.axis_index("x") + 2 * lax.axis_index("y") + lax.axis_index("c")
    h = x[0]
    tgt = loss_target[0]
    rows_out = D_MODEL // N_DEV
    up_rows = PLE_DIM * (D_MODEL // N_DEV) // D_MODEL
    g_off, u_off = rows_out, 2 * rows_out

    def own_slot(block):
        return lax.dynamic_update_index_in_dim(lax.empty((N_DEV,) + block.shape, block.dtype), block, my, 0)

    win_bf = w_in.astype(BF16)
    rest_bf = [jnp.concatenate([w_out[l], w_ple_gate[l], w_ple_up[l].reshape(up_rows, D_MODEL)], axis=0).astype(BF16)
               for l in range(depth)]
    conv_all = _all_gather(dn_conv_w, name="gather_conv_w")
    conv_full = conv_all.transpose(1, 2, 0, 3).reshape(depth, CONV_W, 3 * BR_WIDTH)
    win_all = {0: _all_gather(win_bf[0], name="gather_w_in_l0", after=[conv_all])}
    pending, relayed = {}, {}
    last = win_all[0]
    for l in range(depth):
        if l > 0:
            relayed["win", l] = _relay_start(win_bf[l], own_slot(win_bf[l]), after=[last], name=f"gather_w_in_l{l}_first")
            last = relayed["win", l][3]
        if l == 0:
            relayed["rest", l] = _relay_start(rest_bf[l], own_slot(rest_bf[l]), after=[last], name=f"gather_rest_l{l}_first")
            last = relayed["rest", l][3]
        else:
            pending["rest", l] = _push_start(rest_bf[l], own_slot(rest_bf[l]), broadcast=True, after=[last],
                                             name=f"gather_rest_l{l}_start")
            last = pending["rest", l][4]
    order_tok = last[0, 0]
    lbs = _lower_bounds(hg_lb_logits)

    saved = []
    weights = []
    for l in range(depth):
        tag = f"l{l}"
        if l > 0:
            win_all[l] = _relay_wait(relayed["win", l], [h], name=f"gather_w_in_{tag}_wait")
        wi = _win_from_shards(win_all[l], name=f"w_in_layout_{tag}")
        nw = norm_w[l] + order_tok if l == 0 else norm_w[l]
        hn = _rms_fwd(h, nw, name=f"rms_fwd_{tag}")
        proj = _mm(hn, wi, mode="nn", out_dtype=F32, name=f"mm_proj_{tag}")
        al, dt = _pad_lanes(dn_A_log[l]), _pad_lanes(dn_dt_bias[l])
        qkv = _dn_qkv_fwd(proj, conv_full[l], name=f"dn_qkv_fwd_{tag}")
        if ("rest", l) in relayed:
            relayed["rest", l] = _relay_mid(relayed["rest", l], [qkv], name=f"gather_rest_{tag}_relay")
            al = al + relayed["rest", l][4][0, 0]
        beta, gcs = _dn_gate_fwd(proj, al, dt, name=f"dn_gate_fwd_{tag}")
        o_dn, st_dn = _dn_chunk_fwd(qkv, gcs, beta, name=f"dn_chunk_fwd_{tag}")
        lb = lbs[l].reshape(1, BR_WIDTH)
        qh, kh, lf = _hg_prep_fwd(proj, lb, name=f"hg_prep_fwd_{tag}")
        o_hg, st_hg = _hg_chunk_fwd(qh, kh, proj, lf, name=f"hg_chunk_fwd_{tag}")
        y_dn = _hnorm_fwd(o_dn, proj, C_Z, dn_norm_w[l], name=f"hnorm_dn_fwd_{tag}")
        y_hg = _hnorm_fwd(o_hg, proj, C_HZ, hg_norm_w[l], name=f"hnorm_hg_fwd_{tag}")
        y = jnp.concatenate([y_dn, y_hg], axis=1)
        if ("rest", l) in relayed:
            rest_all = _relay_wait(relayed["rest", l], [y], name=f"gather_rest_{tag}_wait")
        else:
            rest_all = _push_wait(pending["rest", l], [y], broadcast=True, name=f"gather_rest_{tag}_wait")
        wo = rest_all[:, 0:rows_out].reshape(D_MODEL, D_MODEL)
        wg = rest_all[:, g_off:g_off + rows_out].reshape(D_MODEL, D_MODEL)
        wu = rest_all[:, u_off:u_off + up_rows].reshape(N_DEV, PLE_DIM, D_MODEL // N_DEV).transpose(1, 0, 2).reshape(PLE_DIM, D_MODEL)
        weights.append((wi, wo, wg, wu))
        h1 = _mm(y, wo, mode="nn", out_dtype=F32, res=h, name=f"mm_out_{tag}")
        pin = []
        if ("win", l + 1) in relayed:
            relayed["win", l + 1] = _relay_mid(relayed["win", l + 1], [h1], name=f"gather_w_in_l{l + 1}_relay")
            pin = [relayed["win", l + 1][4]]
        gp = _mm(h1, wg, mode="nn", out_dtype=F32, after=pin, name=f"mm_gate_{tag}")
        up = _mm(p[l, 0], wu, mode="nn", out_dtype=F32, name=f"mm_up_{tag}")
        h2 = _ple_fwd(h1, gp, up, name=f"ple_fwd_{tag}")
        saved.append(dict(h=h, hn=hn, proj=proj, qkv=qkv, beta=beta, gcs=gcs, st_dn=st_dn, qh=qh, kh=kh, lf=lf,
                          st_hg=st_hg, o_dn=o_dn, o_hg=o_hg, y=y, h1=h1, gp=gp, up=up, al=al, dt=dt, lb=lb))
        h = h2

    loss_row, dh, d_final_w = _final_fwd_bwd(h, final_norm_w, tgt, name="final_norm_loss")

    d_norm_w, d_alog, d_dt, d_dn_nw, d_hg_nw, d_lb, d_conv = ([None] * depth for _ in range(7))
    sent = {}
    for l in reversed(range(depth)):
        wi, wo, wg, wu = weights[l]
        sv = saved[l]
        tag = f"l{l}"
        dup, dgp = _ple_bwd(dh, sv["gp"], sv["up"], name=f"ple_bwd_{tag}")
        d_wu = _mm(p[l, 0], dup, mode="tn", out_dtype=BF16, name=f"mm_dwup_{tag}")
        d_wg = _mm(sv["h1"], dgp, mode="tn", out_dtype=BF16, name=f"mm_dwgate_{tag}")
        dh1 = _mm(dgp, wg, mode="nt", out_dtype=F32, res=dh, name=f"mm_dh1_{tag}")
        d_wo = _mm(sv["y"], dh1, mode="tn", out_dtype=BF16, name=f"mm_dwout_{tag}")
        parts_rest = jnp.concatenate(
            [d_wo.reshape(N_DEV, rows_out, D_MODEL), d_wg.reshape(N_DEV, rows_out, D_MODEL),
             d_wu.reshape(PLE_DIM, N_DEV, D_MODEL // N_DEV).transpose(1, 0, 2).reshape(N_DEV, up_rows, D_MODEL)], axis=1)
        sent["rest", l] = _push_start(parts_rest, own_slot(parts_rest[my]), broadcast=False, name=f"exchange_rest_{tag}_start")
        dy = _mm(dh1, wo, mode="nt", out_dtype=F32, name=f"mm_dy_{tag}")
        dn_nw = dn_norm_w[l] + sent["rest", l][4][0, 0]
        do_dn, dz_dn, d_dn_nw[l] = _hnorm_bwd(sv["o_dn"], sv["proj"], C_Z, dn_nw, dy, 0, name=f"hnorm_dn_bwd_{tag}")
        do_hg, dz_hg, d_hg_nw[l] = _hnorm_bwd(sv["o_hg"], sv["proj"], C_HZ, hg_norm_w[l], dy, BR_WIDTH, name=f"hnorm_hg_bwd_{tag}")
        dqkv, d_gc, dbeta = _dn_chunk_bwd(sv["qkv"], sv["gcs"], sv["beta"], sv["st_dn"], do_dn, name=f"dn_chunk_bwd_{tag}")
        dqkv_pre, d_conv[l] = _dn_qkv_bwd(sv["proj"], conv_full[l], dqkv, name=f"dn_qkv_bwd_{tag}")
        db, da, d_alog[l], d_dt[l] = _dn_gate_bwd(sv["proj"], sv["al"], sv["dt"], dbeta, d_gc, name=f"dn_gate_bwd_{tag}")
        dqh, dkh, dhi, dlf = _hg_chunk_bwd(sv["qh"], sv["kh"], sv["proj"], sv["lf"], sv["st_hg"], do_hg, name=f"hg_chunk_bwd_{tag}")
        dhq, dhf, d_lb[l] = _hg_prep_bwd(sv["proj"], sv["lb"], dqh, dkh, dlf, name=f"hg_prep_bwd_{tag}")
        dproj = jnp.concatenate([dqkv_pre, dz_dn, dhq, dhf, dhi, dz_hg, db, da], axis=1)
        def push_d_win(after):
            d_win = _mm(sv["hn"], dproj, mode="tn", out_dtype=BF16, after=after, name=f"mm_dwin_{tag}")
            parts_in = _win_to_shards(d_win, name=f"dw_in_shards_{tag}")
            return _push_start(parts_in, own_slot(parts_in[my]), broadcast=False, name=f"exchange_w_in_{tag}_start")

        if l == 0:
            small = _pack([loss_row, jnp.concatenate(d_norm_w[1:], axis=0), d_final_w,
                           jnp.stack([a[0, :N_HEADS] for a in d_alog]), jnp.stack([a[0, :N_HEADS] for a in d_dt]),
                           jnp.concatenate(d_dn_nw, axis=0), jnp.concatenate(d_hg_nw, axis=0), jnp.concatenate(d_lb, axis=0),
                           jnp.stack(d_conv)])
            small_all = _all_gather(small, name="gather_small")
        sent["win", l] = push_d_win([small_all] if l == 0 else [])
        dhn = _mm(dproj, wi, mode="nt", out_dtype=F32, after=[sent["win", l][4]], name=f"mm_dhn_{tag}")
        dh, d_norm_w[l] = _rms_bwd(sv["h"], norm_w[l], dhn, dh1, name=f"rms_bwd_{tag}")
    grad_x = dh[None]

    small_shapes = [(1, 128), (depth - 1, D_MODEL), final_norm_w.shape, dn_A_log.shape, dn_dt_bias.shape, dn_norm_w.shape,
                    hg_norm_w.shape, hg_lb_logits.shape, (depth, CONV_W, 3 * BR_WIDTH)]
    tot = _unpack(_sum_parts(small_all, after=[grad_x], name="sum_small"), small_shapes)
    loss = tot[0][0, 0]
    g_lb = tot[7]
    g_logits = jax.vjp(_lower_bounds, hg_lb_logits)[1](g_lb)[0]
    g_conv = lax.dynamic_slice_in_dim(tot[8], my * (3 * BR_WIDTH // N_DEV), 3 * BR_WIDTH // N_DEV, axis=2)
    small_g = [g_conv, tot[3], tot[4], tot[5], g_logits, tot[6], tot[2]]
    small_w = [dn_conv_w, dn_A_log, dn_dt_bias, dn_norm_w, hg_lb_logits, hg_norm_w, final_norm_w]
    small_m = [m_dn_conv_w, m_dn_A_log, m_dn_dt_bias, m_dn_norm_w, m_hg_lb_logits, m_hg_norm_w, m_final_norm_w]
    small_v = [v_dn_conv_w, v_dn_A_log, v_dn_dt_bias, v_dn_norm_w, v_hg_lb_logits, v_hg_norm_w, v_final_norm_w]
    pk_w = _pack(small_w)
    res_small = _adamw(_pack(small_g)[None], 0, pk_w, _pack(small_m), _pack(small_v), name="adamw_small", tr=pk_w.shape[0])
    shapes_w = [a.shape for a in small_w]
    sg, sd, sm, sv_ = (_unpack(r, shapes_w) for r in res_small)

    r_win = r_wo = r_wg = r_wu = None
    done = [grad_x, res_small[0]]

    def flat(a, cols):
        return a.reshape(-1, cols)

    for l in reversed(range(depth)):
        tag = f"l{l}"
        land_rest = _push_wait(sent["rest", l], done, broadcast=False, name=f"exchange_rest_{tag}_wait")
        r_wo = _adamw(land_rest, 0, flat(w_out, D_MODEL), flat(m_w_out, D_MODEL), flat(v_w_out, D_MODEL), layer=l,
                      n_layers=depth, prev=r_wo, name=f"adamw_w_out_{tag}", tr=rows_out)
        r_wg = _adamw(land_rest, g_off, flat(w_ple_gate, D_MODEL), flat(m_w_ple_gate, D_MODEL), flat(v_w_ple_gate, D_MODEL),
                      layer=l, n_layers=depth, prev=r_wg, name=f"adamw_w_gate_{tag}", tr=rows_out)
        r_wu = _adamw(land_rest, u_off, flat(w_ple_up, D_MODEL), flat(m_w_ple_up, D_MODEL), flat(v_w_ple_up, D_MODEL),
                      layer=l, n_layers=depth, prev=r_wu, name=f"adamw_w_up_{tag}", tr=up_rows)
        done = [r_wo[0], r_wg[0], r_wu[0]]
    for l in reversed(range(depth)):
        tag = f"l{l}"
        if l == 0:
            nw0 = _sum_parts(_all_gather(_pack([d_norm_w[0]]), after=done, name="gather_norm_w"), name="sum_norm_w")
            g_norm_w = jnp.concatenate([_unpack(nw0, [(1, D_MODEL)])[0], tot[1]], axis=0)
            pk_nw = _pack([norm_w])
            r_nw = _adamw(_pack([g_norm_w])[None], 0, pk_nw, _pack([m_norm_w]), _pack([v_norm_w]), name="adamw_norm_w",
                          tr=pk_nw.shape[0])
            r_nw = [_unpack(r, [norm_w.shape])[0] for r in r_nw]
            done = [r_nw[0]]
        land_in = _push_wait(sent["win", l], done, broadcast=False, name=f"exchange_w_in_{tag}_wait")
        r_win = _adamw(land_in, 0, flat(w_in, SHARD_IN), flat(m_w_in, SHARD_IN), flat(v_w_in, SHARD_IN), layer=l,
                       n_layers=depth, prev=r_win, name=f"adamw_w_in_{tag}", tr=256)
        done = [r_win[0]]
    r_win = [o.reshape(w_in.shape) for o in r_win]
    r_wo = [o.reshape(w_out.shape) for o in r_wo]
    r_wg = [o.reshape(w_ple_gate.shape) for o in r_wg]
    r_wu = [o.reshape(w_ple_up.shape) for o in r_wu]

    def order(nw, small_list, big_in, big_out, big_up, big_gate):
        cw, al_, dt_, dnw, lbl, hnw, fw = small_list
        return [nw, big_in, cw, al_, dt_, dnw, lbl, hnw, big_out, big_up, big_gate, fw]

    outs = [loss, grad_x]
    for i, sl in enumerate((sg, sd, sm, sv_)):
        outs += order(r_nw[i], sl, r_win[i], r_wo[i], r_wu[i], r_wg[i])
    return tuple(outs)
```

```python
import functools

import jax
import jax.numpy as jnp
from jax import lax
from jax.experimental import pallas as pl
from jax.experimental.pallas import tpu as pltpu

F32 = jnp.float32
BF16 = jnp.bfloat16
HIGHEST = lax.Precision.HIGHEST

N_DEV = 8
D_MODEL = 2048
PLE_DIM = 256
HEAD_DIM = 128
N_HEADS = 8
BR_WIDTH = N_HEADS * HEAD_DIM
CHUNK = 64
SUB = 16
CONV_W = 4
NORM_EPS = 1e-6
L2_EPS = 1e-6
IN_WIDTH = 8208
SHARD_IN = IN_WIDTH // N_DEV
EXP_CLAMP = 80.0

C_QKV, C_Z, C_HQ, C_HF, C_HI, C_HZ, C_B, C_A, N_PROJ = 0, 3072, 4096, 5120, 6144, 7168, 8192, 8320, 8448

ADAM_LR, ADAM_B1, ADAM_B2, ADAM_EPS, ADAM_WD, ADAM_STEP = 0.001, 0.9, 0.999, 1e-08, 0.01, 10

VMEM_LIMIT = 48 * 1024 * 1024


def _cp(*sem):
    return pltpu.CompilerParams(dimension_semantics=sem, vmem_limit_bytes=VMEM_LIMIT)


class _Heads:
    def __init__(self, vals):
        self.v = tuple(vals)

    def __add__(self, o):
        return _hmap(lambda a, b: a + b, self, o)

    def __radd__(self, o):
        return _hmap(lambda a, b: b + a, self, o)

    def __sub__(self, o):
        return _hmap(lambda a, b: a - b, self, o)

    def __rsub__(self, o):
        return _hmap(lambda a, b: b - a, self, o)

    def __mul__(self, o):
        return _hmap(lambda a, b: a * b, self, o)

    def __rmul__(self, o):
        return _hmap(lambda a, b: b * a, self, o)

    def __neg__(self):
        return _hmap(lambda a: -a, self)

    def __getitem__(self, idx):
        return _hmap(lambda a: a[idx], self)


def _hmap(fn, *args):
    n = next((len(a.v) for a in args if isinstance(a, _Heads)), None)
    if n is None:
        return fn(*args)
    return _Heads(fn(*[a.v[i] if isinstance(a, _Heads) else a for a in args]) for i in range(n))


def _dot(a, b, ca, cb):
    return _hmap(lambda x, y: lax.dot_general(x.astype(BF16), y.astype(BF16), (((ca,), (cb,)), ((), ())),
                                              preferred_element_type=F32), a, b)


def _nn(a, b):
    return _dot(a, b, 1, 0)


def _nt(a, b):
    return _dot(a, b, 1, 1)


def _tn(a, b):
    return _dot(a, b, 0, 0)


def _split(a):
    hi = _hmap(lambda x: x.astype(BF16), a)
    return hi, _hmap(lambda x, h: (x - h.astype(F32)).astype(BF16), a, hi)


def _dot3(a, b, ca, cb):
    ah, al = _split(a)
    bh, bl = _split(b)
    return _dot(ah, bh, ca, cb) + (_dot(ah, bl, ca, cb) + _dot(al, bh, ca, cb))


def _nn_exact(a, b):
    return _hmap(lambda y: lax.dot_general(a, y, (((1,), (0,)), ((), ())), precision=HIGHEST,
                                           preferred_element_type=F32), b)


def _exp(x):
    return _hmap(jnp.exp, x)


def _sum(x, axis):
    return _hmap(lambda a: jnp.sum(a, axis=axis, keepdims=True), x)


def _sigmoid(x):
    return jax.nn.sigmoid(x)


def _silu(x):
    return x * _sigmoid(x)


def _dsilu(x):
    s = _sigmoid(x)
    return s * (1.0 + x * (1.0 - s))


def _silu_and_grad(x):
    s = _sigmoid(x)
    return x * s, s * (1.0 + x * (1.0 - s))


def _softplus(x):
    return jnp.maximum(x, 0.0) + jnp.log(1.0 + jnp.exp(-jnp.abs(x)))


def _iota2(n, m, axis):
    return lax.broadcasted_iota(jnp.int32, (n, m), axis)


def _col2row(col, eye):
    return _hmap(lambda c: jnp.sum(eye * c, axis=0, keepdims=True), col)


def _row2col(row, eye):
    return _hmap(lambda r: jnp.sum(eye * r, axis=1, keepdims=True), row)


def _pick_lane(block, lane_idx):
    lane = _iota2(block.shape[0], block.shape[1], 1)
    return jnp.sum(jnp.where(lane == lane_idx, block, 0.0), axis=1, keepdims=True)


MM_TILE_M, MM_TILE_N, MM_TILE_K = 1024, 1408, 2048


def _tile(dim, cap):
    if dim <= cap:
        return dim
    t = cap - cap % 128
    while dim % t:
        t -= 128
    return t


def _mm(a, b, *, mode, out_dtype, res=None, after=(), name):
    if mode == "nn":
        (m, kd), (_, n) = a.shape, b.shape
    elif mode == "nt":
        (m, kd), (n, _) = a.shape, b.shape
    else:
        (kd, m), (_, n) = a.shape, b.shape
    tm, tn, tk = _tile(m, MM_TILE_M), _tile(n, MM_TILE_N), _tile(kd, MM_TILE_K)
    assert m % tm == 0 and n % tn == 0 and kd % tk == 0, (m, n, kd, tm, tn, tk)
    nk = kd // tk
    ca, cb = {"nn": (1, 0), "nt": (1, 1), "tn": (0, 0)}[mode]

    def body(*refs):
        a_ref, b_ref = refs[:2]
        r_ref = None if res is None else refs[2]
        o_ref, acc_ref = refs[-2:]
        k = pl.program_id(2)

        @pl.when(k == 0)
        def _():
            acc_ref[...] = jnp.zeros_like(acc_ref)

        acc_ref[...] += _dot(a_ref[...], b_ref[...], ca, cb)

        @pl.when(k == nk - 1)
        def _():
            out = acc_ref[...]
            if r_ref is not None:
                out = out + r_ref[...].astype(F32)
            o_ref[...] = out.astype(o_ref.dtype)

    a_spec = pl.BlockSpec((tk, tm), lambda i, j, k: (k, i)) if mode == "tn" else pl.BlockSpec((tm, tk), lambda i, j, k: (i, k))
    b_spec = pl.BlockSpec((tn, tk), lambda i, j, k: (j, k)) if mode == "nt" else pl.BlockSpec((tk, tn), lambda i, j, k: (k, j))
    o_spec = pl.BlockSpec((tm, tn), lambda i, j, k: (i, j))
    in_specs = [a_spec, b_spec] + ([o_spec] if res is not None else []) + [pl.BlockSpec(memory_space=pl.ANY)] * len(after)
    args = (a, b) + ((res,) if res is not None else ()) + tuple(after)
    return pl.pallas_call(
        body, name=name, grid=(m // tm, n // tn, nk), in_specs=in_specs, out_specs=o_spec,
        out_shape=jax.ShapeDtypeStruct((m, n), out_dtype),
        scratch_shapes=[pltpu.VMEM((tm, tn), F32)],
        compiler_params=_cp("parallel", "parallel", "arbitrary"),
    )(*args)


ROW_TILE = 256


def _rms_fwd(h, w, *, name):
    s, d = h.shape
    tr = min(ROW_TILE, s)

    def body(h_ref, w_ref, o_ref):
        x = h_ref[...]
        r = lax.rsqrt(jnp.mean(x * x, axis=-1, keepdims=True) + NORM_EPS)
        o_ref[...] = (x * r * w_ref[...]).astype(o_ref.dtype)

    return pl.pallas_call(
        body, name=name, grid=(s // tr,),
        in_specs=[pl.BlockSpec((tr, d), lambda i: (i, 0)), pl.BlockSpec((1, d), lambda i: (0, 0))],
        out_specs=pl.BlockSpec((tr, d), lambda i: (i, 0)),
        out_shape=jax.ShapeDtypeStruct((s, d), BF16), compiler_params=_cp("parallel"),
    )(h, w.reshape(1, d))


def _rms_bwd_math(x, w, dy):
    d = x.shape[-1]
    r = lax.rsqrt(jnp.mean(x * x, axis=-1, keepdims=True) + NORM_EPS)
    gw = dy * w
    dx = r * gw - x * ((r * r * r) * (jnp.sum(gw * x, axis=-1, keepdims=True) / d))
    return dx, dy * x * r


def _rms_bwd(h, w, dhn, res, *, name):
    s, d = h.shape
    tr = min(ROW_TILE, s)

    def body(h_ref, w_ref, g_ref, r_ref, dh_ref, dw_ref):
        @pl.when(pl.program_id(0) == 0)
        def _():
            dw_ref[...] = jnp.zeros_like(dw_ref)

        dx, dwt = _rms_bwd_math(h_ref[...], w_ref[...], g_ref[...])
        dh_ref[...] = r_ref[...] + dx
        dw_ref[...] += jnp.sum(dwt, axis=0, keepdims=True)

    row = pl.BlockSpec((tr, d), lambda i: (i, 0))
    vec = pl.BlockSpec((1, d), lambda i: (0, 0))
    return pl.pallas_call(
        body, name=name, grid=(s // tr,), in_specs=[row, vec, row, row], out_specs=[row, vec],
        out_shape=[jax.ShapeDtypeStruct((s, d), F32), jax.ShapeDtypeStruct((1, d), F32)],
        compiler_params=_cp("arbitrary"),
    )(h, w.reshape(1, d), dhn, res)


def _final_fwd_bwd(h, w, tgt, *, name):
    s, d = h.shape
    tr = min(ROW_TILE, s)

    def body(h_ref, w_ref, t_ref, loss_ref, dh_ref, dw_ref):
        @pl.when(pl.program_id(0) == 0)
        def _():
            loss_ref[...] = jnp.zeros_like(loss_ref)
            dw_ref[...] = jnp.zeros_like(dw_ref)

        x = h_ref[...]
        wv = w_ref[...]
        r = lax.rsqrt(jnp.mean(x * x, axis=-1, keepdims=True) + NORM_EPS)
        err = x * r * wv - t_ref[...]
        row_loss = jnp.mean(err * err, axis=-1, keepdims=True)
        loss_ref[...] += 0.5 * jnp.sum(row_loss, axis=0, keepdims=True)
        dx, dwt = _rms_bwd_math(x, wv, err / d)
        dh_ref[...] = dx
        dw_ref[...] += jnp.sum(dwt, axis=0, keepdims=True)

    row = pl.BlockSpec((tr, d), lambda i: (i, 0))
    vec = pl.BlockSpec((1, d), lambda i: (0, 0))
    return pl.pallas_call(
        body, name=name, grid=(s // tr,), in_specs=[row, vec, row],
        out_specs=[pl.BlockSpec((1, 128), lambda i: (0, 0)), row, vec],
        out_shape=[jax.ShapeDtypeStruct((1, 128), F32), jax.ShapeDtypeStruct((s, d), F32),
                   jax.ShapeDtypeStruct((1, d), F32)],
        compiler_params=_cp("arbitrary"),
    )(h, w.reshape(1, d), tgt)


def _ple_fwd(h1, gate_pre, up, *, name):
    s, d = h1.shape
    tr = min(ROW_TILE, s)

    def body(h_ref, g_ref, u_ref, o_ref):
        o_ref[...] = h_ref[...] + u_ref[...] * _sigmoid(g_ref[...])

    row = pl.BlockSpec((tr, d), lambda i: (i, 0))
    return pl.pallas_call(body, name=name, grid=(s // tr,), in_specs=[row, row, row], out_specs=row,
                          out_shape=jax.ShapeDtypeStruct((s, d), F32), compiler_params=_cp("parallel"))(h1, gate_pre, up)


def _ple_bwd(dh2, gate_pre, up, *, name):
    s, d = dh2.shape
    tr = min(ROW_TILE, s)

    def body(d_ref, g_ref, u_ref, dup_ref, dgp_ref):
        dh = d_ref[...]
        gate = _sigmoid(g_ref[...])
        dup_ref[...] = (dh * gate).astype(BF16)
        dgp_ref[...] = (dh * u_ref[...] * gate * (1.0 - gate)).astype(BF16)

    row = pl.BlockSpec((tr, d), lambda i: (i, 0))
    return pl.pallas_call(body, name=name, grid=(s // tr,), in_specs=[row, row, row], out_specs=[row, row],
                          out_shape=[jax.ShapeDtypeStruct((s, d), BF16)] * 2, compiler_params=_cp("parallel"))(dh2, gate_pre, up)


HN_TILE = 512


def _hnorm_fwd(o, proj, z_col, w, *, name):
    s = o.shape[0]
    tr = min(HN_TILE, s)

    def body(o_ref, z_ref, w_ref, y_ref):
        wv = w_ref[...]
        for h in range(N_HEADS):
            cols = slice(h * HEAD_DIM, (h + 1) * HEAD_DIM)
            x = o_ref[:, cols]
            r = lax.rsqrt(jnp.mean(x * x, axis=-1, keepdims=True) + NORM_EPS)
            y_ref[:, cols] = (x * r * wv * _silu(z_ref[:, cols])).astype(BF16)

    blk = pl.BlockSpec((tr, BR_WIDTH), lambda i: (i, 0))
    return pl.pallas_call(
        body, name=name, grid=(s // tr,),
        in_specs=[blk, pl.BlockSpec((tr, BR_WIDTH), lambda i: (i, z_col // BR_WIDTH)), pl.BlockSpec((1, HEAD_DIM), lambda i: (0, 0))],
        out_specs=blk, out_shape=jax.ShapeDtypeStruct((s, BR_WIDTH), BF16), compiler_params=_cp("parallel"),
    )(o, proj, w.reshape(1, HEAD_DIM))


def _hnorm_bwd(o, proj, z_col, w, dy, dy_col, *, name):
    s = o.shape[0]
    tr = min(HN_TILE, s)

    def body(o_ref, z_ref, w_ref, dy_ref, do_ref, dz_ref, dw_ref):
        @pl.when(pl.program_id(0) == 0)
        def _():
            dw_ref[...] = jnp.zeros_like(dw_ref)

        wv = w_ref[...]
        dw = jnp.zeros((1, HEAD_DIM), F32)
        for h in range(N_HEADS):
            cols = slice(h * HEAD_DIM, (h + 1) * HEAD_DIM)
            x, z, g = o_ref[:, cols], z_ref[:, cols], dy_ref[:, cols]
            r = lax.rsqrt(jnp.mean(x * x, axis=-1, keepdims=True) + NORM_EPS)
            on = x * r * wv
            silu_z, dsilu_z = _silu_and_grad(z)
            don = g * silu_z
            dz_ref[:, cols] = (g * on * dsilu_z).astype(BF16)
            gw = don * wv
            do_ref[:, cols] = r * gw - x * ((r * r * r) * (jnp.sum(gw * x, axis=-1, keepdims=True) / HEAD_DIM))
            dw = dw + jnp.sum(don * x * r, axis=0, keepdims=True)
        dw_ref[...] += dw

    blk = pl.BlockSpec((tr, BR_WIDTH), lambda i: (i, 0))
    vec = pl.BlockSpec((1, HEAD_DIM), lambda i: (0, 0))
    return pl.pallas_call(
        body, name=name, grid=(s // tr,),
        in_specs=[blk, pl.BlockSpec((tr, BR_WIDTH), lambda i: (i, z_col // BR_WIDTH)), vec,
                  pl.BlockSpec((tr, BR_WIDTH), lambda i: (i, dy_col // BR_WIDTH))],
        out_specs=[blk, blk, vec],
        out_shape=[jax.ShapeDtypeStruct((s, BR_WIDTH), F32), jax.ShapeDtypeStruct((s, BR_WIDTH), BF16),
                   jax.ShapeDtypeStruct((1, HEAD_DIM), F32)],
        compiler_params=_cp("arbitrary"),
    )(o, proj, w.reshape(1, HEAD_DIM), dy)


def _conv_silu(x, w, s):
    row = _iota2(s, x.shape[1], 0)
    c = w[CONV_W - 1:CONV_W, :] * x
    for k in range(1, CONV_W):
        c = c + w[CONV_W - 1 - k:CONV_W - k, :] * jnp.where(row >= k, pltpu.roll(x, k, 0), 0.0)
    return c


def _dn_qkv_fwd(proj, conv_w, *, name):
    s = proj.shape[0]
    nb = 3 * N_HEADS

    def body(x_ref, w_ref, o_ref):
        j = pl.program_id(0)
        sv = _silu(_conv_silu(x_ref[...], w_ref[...], s))
        r = lax.rsqrt(jnp.sum(sv * sv, axis=-1, keepdims=True) + L2_EPS)
        scale = jnp.where(j < N_HEADS, HEAD_DIM ** -0.5, 1.0).astype(F32)
        o_ref[...] = jnp.where(j < 2 * N_HEADS, sv * r * scale, sv)

    return pl.pallas_call(
        body, name=name, grid=(nb,),
        in_specs=[pl.BlockSpec((s, HEAD_DIM), lambda j: (0, j)), pl.BlockSpec((CONV_W, HEAD_DIM), lambda j: (0, j))],
        out_specs=pl.BlockSpec((s, HEAD_DIM), lambda j: (0, j)),
        out_shape=jax.ShapeDtypeStruct((s, 3 * BR_WIDTH), F32), compiler_params=_cp("parallel"),
    )(proj, conv_w)


def _dn_qkv_bwd(proj, conv_w, dqkv, *, name):
    s = proj.shape[0]
    nb = 3 * N_HEADS

    def body(x_ref, w_ref, g_ref, dx_ref, dw_ref):
        j = pl.program_id(0)
        x, w, g = x_ref[...], w_ref[...], g_ref[...]
        c = _conv_silu(x, w, s)
        sv, dsv = _silu_and_grad(c)
        r = lax.rsqrt(jnp.sum(sv * sv, axis=-1, keepdims=True) + L2_EPS)
        scale = jnp.where(j < N_HEADS, HEAD_DIM ** -0.5, 1.0).astype(F32)
        ds_n = scale * (r * g - sv * ((r * r * r) * jnp.sum(g * sv, axis=-1, keepdims=True)))
        dc = jnp.where(j < 2 * N_HEADS, ds_n, g) * dsv
        row = _iota2(s, HEAD_DIM, 0)
        dx = w[CONV_W - 1:CONV_W, :] * dc
        dws = [jnp.sum(dc * x, axis=0, keepdims=True)]
        for k in range(1, CONV_W):
            dx = dx + w[CONV_W - 1 - k:CONV_W - k, :] * jnp.where(row < s - k, pltpu.roll(dc, s - k, 0), 0.0)
            dws.append(jnp.sum(dc * jnp.where(row >= k, pltpu.roll(x, k, 0), 0.0), axis=0, keepdims=True))
        dx_ref[...] = dx.astype(BF16)
        for k in range(CONV_W):
            dw_ref[CONV_W - 1 - k:CONV_W - k, :] = dws[k]

    blk = pl.BlockSpec((s, HEAD_DIM), lambda j: (0, j))
    wblk = pl.BlockSpec((CONV_W, HEAD_DIM), lambda j: (0, j))
    return pl.pallas_call(
        body, name=name, grid=(nb,), in_specs=[blk, wblk, blk], out_specs=[blk, wblk],
        out_shape=[jax.ShapeDtypeStruct((s, 3 * BR_WIDTH), BF16), jax.ShapeDtypeStruct((CONV_W, 3 * BR_WIDTH), F32)],
        compiler_params=_cp("parallel"),
    )(proj, conv_w, dqkv)


def _tri(n, kind):
    r, c = _iota2(n, n, 0), _iota2(n, n, 1)
    if kind == "lower":
        return (r >= c).astype(F32)
    if kind == "upper":
        return (r <= c).astype(F32)
    return (r == c).astype(F32)


GATE_TILE = 512


def _dn_gate_fwd(proj, a_log, dt_bias, *, name):
    s = proj.shape[0]
    tr = min(GATE_TILE, s)

    def body(b_ref, a_ref, al_ref, dt_ref, beta_ref, g_ref):
        beta_ref[...] = _sigmoid(b_ref[...])
        g = -jnp.exp(al_ref[...]) * _softplus(a_ref[...] + dt_ref[...])
        low = _tri(CHUNK, "lower")
        for c in range(tr // CHUNK):
            rows = slice(c * CHUNK, (c + 1) * CHUNK)
            g_ref[rows, :] = _nn_exact(low, g[rows, :])

    blk = lambda cb: pl.BlockSpec((tr, HEAD_DIM), lambda i: (i, cb))
    vec = pl.BlockSpec((1, HEAD_DIM), lambda i: (0, 0))
    out = pl.BlockSpec((tr, HEAD_DIM), lambda i: (i, 0))
    return pl.pallas_call(
        body, name=name, grid=(s // tr,), in_specs=[blk(C_B // HEAD_DIM), blk(C_A // HEAD_DIM), vec, vec],
        out_specs=[out, out], out_shape=[jax.ShapeDtypeStruct((s, HEAD_DIM), F32)] * 2, compiler_params=_cp("parallel"),
    )(proj, proj, a_log, dt_bias)


def _dn_gate_bwd(proj, a_log, dt_bias, dbeta, d_g, *, name):
    s = proj.shape[0]
    tr = min(GATE_TILE, s)

    def body(b_ref, a_ref, al_ref, dt_ref, dbeta_ref, dG_ref, db_ref, da_ref, dal_ref, ddt_ref):
        @pl.when(pl.program_id(0) == 0)
        def _():
            dal_ref[...] = jnp.zeros_like(dal_ref)
            ddt_ref[...] = jnp.zeros_like(ddt_ref)

        beta = _sigmoid(b_ref[...])
        db_ref[...] = (dbeta_ref[...] * beta * (1.0 - beta)).astype(BF16)
        pre = a_ref[...] + dt_ref[...]
        neg_ea = -jnp.exp(al_ref[...])
        up = _tri(CHUNK, "upper")
        d_g = dG_ref[...]
        dg = jnp.concatenate([_nn_exact(up, d_g[c * CHUNK:(c + 1) * CHUNK, :]) for c in range(tr // CHUNK)], axis=0)
        da = dg * neg_ea * _sigmoid(pre)
        da_ref[...] = da.astype(BF16)
        ddt_ref[...] += jnp.sum(da, axis=0, keepdims=True)
        dal_ref[...] += jnp.sum(dg * neg_ea * _softplus(pre), axis=0, keepdims=True)

    blk = lambda cb: pl.BlockSpec((tr, HEAD_DIM), lambda i: (i, cb))
    vec = pl.BlockSpec((1, HEAD_DIM), lambda i: (0, 0))
    io = pl.BlockSpec((tr, HEAD_DIM), lambda i: (i, 0))
    return pl.pallas_call(
        body, name=name, grid=(s // tr,),
        in_specs=[blk(C_B // HEAD_DIM), blk(C_A // HEAD_DIM), vec, vec, io, io], out_specs=[io, io, vec, vec],
        out_shape=[jax.ShapeDtypeStruct((s, HEAD_DIM), BF16)] * 2 + [jax.ShapeDtypeStruct((1, HEAD_DIM), F32)] * 2,
        compiler_params=_cp("arbitrary"),
    )(proj, proj, a_log, dt_bias, dbeta, d_g)


def _unit_lower_inverse(a_strict, eye):
    x = -a_strict
    t = x + eye
    p = x
    n = 2
    while n < CHUNK:
        p = _nn(p, p)
        t = t + _nn(t, p)
        n *= 2
    return t


def _rows(*xs):
    return _hmap(lambda *a: jnp.concatenate(a, axis=0), *xs)


def _lanes(*xs):
    return _hmap(lambda *a: jnp.concatenate(a, axis=1), *xs)


def _dn_chunk_common(q, k, v, gc, beta, st, with_qd_state, t_inv=None):
    c, d = CHUNK, HEAD_DIM
    eye = _tri(c, "eye")
    low = _tri(c, "lower")
    strict = low - eye
    grow = _col2row(gc, eye)
    dec = _hmap(lambda g_, gr: low * jnp.exp(low * (g_ - gr)), gc, grow)
    kb = k * beta
    kq = _nt(_rows(kb, q), k)
    a_mat = kq[0:c, :] * dec * strict
    qk = kq[c:2 * c, :] * dec
    if t_inv is None:
        t_inv = _unit_lower_inverse(a_mat, eye)
    e_g = _exp(gc)
    qd = q * e_g
    uw = _nn(t_inv, _lanes(v * beta, kb * e_g))
    u, w = uw[:, 0:d], uw[:, d:2 * d]
    last = (_iota2(c, 1, 0) == c - 1).astype(F32)
    g_last = _sum(gc * last, 0)
    e_t = _exp(g_last - gc)
    kt = k * e_t
    tail = _exp(g_last)
    if with_qd_state:
        ws = _nn(_rows(w, qd), st)
        vn, qds = u - ws[0:c, :], ws[c:2 * c, :]
    else:
        vn, qds = u - _nn(w, st), None
    return dict(eye=eye, low=low, strict=strict, dec=dec, kb=kb, a_mat=a_mat, t_inv=t_inv, e_g=e_g, u=u, w=w, uw=uw,
                qk=qk, qd=qd, qds=qds, last=last, e_t=e_t, kt=kt, tail=tail, vn=vn)


def _dn_chunk_fwd_math(q, k, v, gc, beta, st):
    m = _dn_chunk_common(q, k, v, gc, beta, st, True)
    o = m["qds"] + _nn(m["qk"], m["vn"])
    st2 = st * m["tail"] + _tn(m["kt"], m["vn"])
    return o, st2, m["t_inv"]


def _dn_chunk_bwd_math(q, k, v, gc, beta, st, do, dst2, t_inv=None):
    c, d = CHUNK, HEAD_DIM
    m = _dn_chunk_common(q, k, v, gc, beta, st, False, t_inv)
    eye, low, strict = m["eye"], m["low"], m["strict"]
    dvn = _tn(m["qk"], do) + _nn(m["kt"], dst2)
    dqk = _nt(do, m["vn"]) * low
    both = _rows(do, dvn)
    ds_both = _nt(both, st)
    dqd, dw = ds_both[0:c, :], -ds_both[c:2 * c, :]
    dst = _tn(_rows(m["qd"], -m["w"]), both) + dst2 * m["tail"]
    dkt = _nt(m["vn"], dst2)
    dtail = _sum(_sum(st * dst2, 1), 0)
    dvb_dkg = _tn(m["t_inv"], _lanes(dvn, dw))
    dvb, dkg = dvb_dkg[:, 0:d], dvb_dkg[:, d:2 * d]
    d_a = _nt(dvb_dkg, m["uw"]) * (-strict)
    dkk = d_a * m["dec"]
    dp = dqk * m["dec"]
    dpk = _rows(dp, dkk)
    dq_dkb = _nn(dpk, k)
    dq = dq_dkb[0:c, :] + dqd * m["e_g"]
    dkb = dq_dkb[c:2 * c, :] + dkg * m["e_g"]
    dk = _tn(dpk, _rows(q, m["kb"])) + dkb * beta + dkt * m["e_t"]
    dv = dvb * beta
    dbeta = _sum(dvb * v + dkb * k, 1)
    de_g = _sum(dkg * m["kb"] + dqd * q, 1)
    de_t = _sum(dkt * k, 1)
    mm = d_a * m["a_mat"] + dqk * m["qk"]
    dgc = (_sum(mm, 1) - _row2col(_sum(mm, 0), eye) + de_g * m["e_g"] - de_t * m["e_t"]
           + (_sum(de_t * m["e_t"], 0) + dtail * m["tail"]) * m["last"])
    return dq, dk, dv, dgc, dbeta, dst


def _heads_of(ref):
    return _Heads(ref[:, h * HEAD_DIM:(h + 1) * HEAD_DIM] for h in range(N_HEADS))


def _lanes_of(block):
    return _Heads(_pick_lane(block, h) for h in range(N_HEADS))


def _dn_chunk_fwd(qkv, gcs, beta, *, name):
    s = qkv.shape[0]
    n = s // CHUNK

    def body(q_ref, k_ref, v_ref, g_ref, b_ref, o_ref, st_out_ref, tinv_ref, st_ref):
        @pl.when(pl.program_id(0) == 0)
        def _():
            st_ref[...] = jnp.zeros_like(st_ref)

        gblk, bblk = g_ref[...], b_ref[...]
        st = _Heads(st_ref[h] for h in range(N_HEADS))
        o, st2, t_inv = _dn_chunk_fwd_math(_heads_of(q_ref), _heads_of(k_ref), _heads_of(v_ref), _lanes_of(gblk),
                                           _lanes_of(bblk), st)
        for h in range(N_HEADS):
            st_out_ref[0, h] = st.v[h]
            tinv_ref[0, h] = t_inv.v[h].astype(BF16)
            o_ref[:, h * HEAD_DIM:(h + 1) * HEAD_DIM] = o.v[h]
            st_ref[h] = st2.v[h]

    blk = lambda off: pl.BlockSpec((CHUNK, BR_WIDTH), lambda c: (c, off))
    sc = pl.BlockSpec((CHUNK, HEAD_DIM), lambda c: (c, 0))
    return pl.pallas_call(
        body, name=name, grid=(n,),
        in_specs=[blk(0), blk(1), blk(2), sc, sc],
        out_specs=[blk(0), pl.BlockSpec((1, N_HEADS, HEAD_DIM, HEAD_DIM), lambda c: (c, 0, 0, 0)),
                   pl.BlockSpec((1, N_HEADS, CHUNK, CHUNK), lambda c: (c, 0, 0, 0))],
        out_shape=[jax.ShapeDtypeStruct((s, BR_WIDTH), F32), jax.ShapeDtypeStruct((n, N_HEADS, HEAD_DIM, HEAD_DIM), F32),
                   jax.ShapeDtypeStruct((n, N_HEADS, CHUNK, CHUNK), BF16)],
        scratch_shapes=[pltpu.VMEM((N_HEADS, HEAD_DIM, HEAD_DIM), F32)],
        compiler_params=_cp("arbitrary"),
    )(qkv, qkv, qkv, gcs, beta)


def _dn_chunk_bwd(qkv, gcs, beta, states, tinvs, do, *, name):
    s = qkv.shape[0]
    n = s // CHUNK

    def body(q_ref, k_ref, v_ref, g_ref, b_ref, st_in_ref, tinv_ref, do_ref, dqkv_ref, dg_ref, dbeta_ref, dst_ref):
        @pl.when(pl.program_id(0) == 0)
        def _():
            dst_ref[...] = jnp.zeros_like(dst_ref)

        gblk, bblk = g_ref[...], b_ref[...]
        lane = _iota2(CHUNK, HEAD_DIM, 1)
        dg_all = jnp.zeros((CHUNK, HEAD_DIM), F32)
        dbeta_all = jnp.zeros((CHUNK, HEAD_DIM), F32)
        dq, dk, dv, dgc, dbeta, dst = _dn_chunk_bwd_math(
            _heads_of(q_ref), _heads_of(k_ref), _heads_of(v_ref), _lanes_of(gblk), _lanes_of(bblk),
            _Heads(st_in_ref[0, h] for h in range(N_HEADS)), _heads_of(do_ref),
            _Heads(dst_ref[h] for h in range(N_HEADS)), _Heads(tinv_ref[0, h] for h in range(N_HEADS)))
        for h in range(N_HEADS):
            for part, val in enumerate((dq, dk, dv)):
                c0 = part * BR_WIDTH + h * HEAD_DIM
                dqkv_ref[:, c0:c0 + HEAD_DIM] = val.v[h]
            dg_all = jnp.where(lane == h, dgc.v[h], dg_all)
            dbeta_all = jnp.where(lane == h, dbeta.v[h], dbeta_all)
            dst_ref[h] = dst.v[h]
        dg_ref[...] = dg_all
        dbeta_ref[...] = dbeta_all

    blk = lambda off: pl.BlockSpec((CHUNK, BR_WIDTH), lambda c: (n - 1 - c, off))
    sc = pl.BlockSpec((CHUNK, HEAD_DIM), lambda c: (n - 1 - c, 0))
    outs = pl.pallas_call(
        body, name=name, grid=(n,),
        in_specs=[blk(0), blk(1), blk(2), sc, sc,
                  pl.BlockSpec((1, N_HEADS, HEAD_DIM, HEAD_DIM), lambda c: (n - 1 - c, 0, 0, 0)),
                  pl.BlockSpec((1, N_HEADS, CHUNK, CHUNK), lambda c: (n - 1 - c, 0, 0, 0)), blk(0)],
        out_specs=[pl.BlockSpec((CHUNK, 3 * BR_WIDTH), lambda c: (n - 1 - c, 0)), sc, sc],
        out_shape=[jax.ShapeDtypeStruct((s, 3 * BR_WIDTH), F32)] + [jax.ShapeDtypeStruct((s, HEAD_DIM), F32)] * 2,
        scratch_shapes=[pltpu.VMEM((N_HEADS, HEAD_DIM, HEAD_DIM), F32)],
        compiler_params=_cp("arbitrary"),
    )(qkv, qkv, qkv, gcs, beta, states, tinvs, do)
    return outs


def _hg_prep_fwd(proj, lb, *, name):
    s = proj.shape[0]
    tr = min(ROW_TILE, s)

    def body(q_ref, f_ref, lb_ref, qo_ref, ko_ref, lf_ref):
        f, lbv = f_ref[...], lb_ref[...]
        qo_ref[...] = _silu(q_ref[...])
        ko_ref[...] = (1.0 - lbv) * _sigmoid(-f)
        lf_ref[...] = jnp.log(lbv + (1.0 - lbv) * _sigmoid(f))

    blk = lambda cb: pl.BlockSpec((tr, BR_WIDTH), lambda i: (i, cb))
    out = pl.BlockSpec((tr, BR_WIDTH), lambda i: (i, 0))
    return pl.pallas_call(
        body, name=name, grid=(s // tr,),
        in_specs=[blk(C_HQ // BR_WIDTH), blk(C_HF // BR_WIDTH), pl.BlockSpec((1, BR_WIDTH), lambda i: (0, 0))],
        out_specs=[out, out, out], out_shape=[jax.ShapeDtypeStruct((s, BR_WIDTH), F32)] * 3, compiler_params=_cp("parallel"),
    )(proj, proj, lb)


def _hg_prep_bwd(proj, lb, dq, dk, dlf, *, name):
    s = proj.shape[0]
    tr = min(ROW_TILE, s)

    def body(q_ref, f_ref, lb_ref, dq_ref, dk_ref, dlf_ref, dhq_ref, dhf_ref, dlb_ref):
        @pl.when(pl.program_id(0) == 0)
        def _():
            dlb_ref[...] = jnp.zeros_like(dlb_ref)

        f, lbv = f_ref[...], lb_ref[...]
        dhq_ref[...] = (dq_ref[...] * _dsilu(q_ref[...])).astype(BF16)
        sp, sn = _sigmoid(f), _sigmoid(-f)
        inner = lbv + (1.0 - lbv) * sp
        dlf_over = dlf_ref[...] / inner
        dkv = dk_ref[...]
        dhf_ref[...] = (dlf_over * (1.0 - lbv) * sp * sn - dkv * (1.0 - lbv) * sn * (1.0 - sn)).astype(BF16)
        dlb_ref[...] += jnp.sum(dlf_over * (1.0 - sp) - dkv * sn, axis=0, keepdims=True)

    blk = lambda cb: pl.BlockSpec((tr, BR_WIDTH), lambda i: (i, cb))
    io = pl.BlockSpec((tr, BR_WIDTH), lambda i: (i, 0))
    vec = pl.BlockSpec((1, BR_WIDTH), lambda i: (0, 0))
    return pl.pallas_call(
        body, name=name, grid=(s // tr,),
        in_specs=[blk(C_HQ // BR_WIDTH), blk(C_HF // BR_WIDTH), vec, io, io, io], out_specs=[io, io, vec],
        out_shape=[jax.ShapeDtypeStruct((s, BR_WIDTH), BF16)] * 2 + [jax.ShapeDtypeStruct((1, BR_WIDTH), F32)],
        compiler_params=_cp("arbitrary"),
    )(proj, proj, lb, dq, dk, dlf)


def _hg_chunk_common(q, k, g):
    c, nb = CHUNK, CHUNK // SUB
    e_g = _exp(g)
    qd = q * e_g
    g_last = g[c - 1:c, :]
    e_t = _exp(g_last - g)
    kt = k * e_t
    tail = _exp(g_last)
    g_refs = [g[i * SUB:i * SUB + 1, :] for i in range(nb)]
    g_ref_rows = _hmap(lambda *rows: jnp.concatenate([jnp.broadcast_to(r, (SUB, r.shape[1])) for r in rows], axis=0), *g_refs)
    e_q = _exp(g - g_ref_rows)
    q_sc = q * e_q
    e_k = [_hmap(lambda gr, g_: jnp.exp(jnp.minimum(gr - g_, EXP_CLAMP)), g_refs[i], g) for i in range(nb)]
    k_sc_all = _rows(*[k * e_k[i] for i in range(nb)])
    row_blk = _iota2(c, 1, 0) // SUB
    masks = [(row_blk == i).astype(F32) for i in range(nb)]
    r_all = _nt(q_sc, k_sc_all)
    a_mat = r_all[:, 0:c] * masks[0]
    for i in range(1, nb):
        a_mat = a_mat + r_all[:, i * c:(i + 1) * c] * masks[i]
    a_mat = a_mat * _tri(c, "lower")
    return dict(e_g=e_g, qd=qd, e_t=e_t, kt=kt, tail=tail, q_sc=q_sc, k_sc_all=k_sc_all, e_q=e_q, e_k=e_k, masks=masks,
                a_mat=a_mat)


def _hg_chunk_fwd_math(q, k, v, g, stt):
    m = _hg_chunk_common(q, k, g)
    o = _nt(m["qd"], stt) + _nn(m["a_mat"], v)
    stt2 = stt * m["tail"] + _tn(v, m["kt"])
    return o, stt2


def _hg_chunk_bwd_math(q, k, v, g, stt, do, dstt2):
    c, nb = CHUNK, CHUNK // SUB
    m = _hg_chunk_common(q, k, g)
    stt2 = stt * m["tail"] + _tn(v, m["kt"])
    later = _sum(stt2 * dstt2, 0)
    dqd = _dot3(do, stt, 1, 0)
    dstt = _tn(do, m["qd"]) + dstt2 * m["tail"]
    d_a = _dot3(do, v, 1, 1) * _tri(c, "lower")
    dv = _tn(m["a_mat"], do) + _nt(m["kt"], dstt2)
    dkt = _dot3(v, dstt2, 1, 0)
    d_blk = _lanes(*[d_a * m["masks"][i] for i in range(nb)])
    dq = dqd * m["e_g"] + _dot3(d_blk, m["k_sc_all"], 1, 0) * m["e_q"]
    dks = _dot3(d_blk, m["q_sc"], 0, 0)
    dk = dkt * m["e_t"]
    for i in range(nb):
        dk = dk + dks[i * c:(i + 1) * c, :] * m["e_k"][i]
    db = q * dq - k * dk
    return dq, dk, dv, db, later, dstt


def _hg_chunk_fwd(qh, kh, proj, lf, *, name):
    s = qh.shape[0]
    n = s // CHUNK
    vb = C_HI // BR_WIDTH

    def body(q_ref, k_ref, v_ref, lf_ref, o_ref, st_out_ref, st_ref):
        @pl.when(pl.program_id(0) == 0)
        def _():
            st_ref[...] = jnp.zeros_like(st_ref)

        st = _Heads(st_ref[h] for h in range(N_HEADS))
        g_all = _nn_exact(_tri(CHUNK, "lower"), lf_ref[...])
        g = _Heads(g_all[:, h * HEAD_DIM:(h + 1) * HEAD_DIM] for h in range(N_HEADS))
        o, st2 = _hg_chunk_fwd_math(_heads_of(q_ref), _heads_of(k_ref), _heads_of(v_ref), g, st)
        for h in range(N_HEADS):
            st_out_ref[0, h] = st.v[h]
            o_ref[:, h * HEAD_DIM:(h + 1) * HEAD_DIM] = o.v[h]
            st_ref[h] = st2.v[h]

    blk = lambda off: pl.BlockSpec((CHUNK, BR_WIDTH), lambda c: (c, off))
    return pl.pallas_call(
        body, name=name, grid=(n,), in_specs=[blk(0), blk(0), blk(vb), blk(0)],
        out_specs=[blk(0), pl.BlockSpec((1, N_HEADS, HEAD_DIM, HEAD_DIM), lambda c: (c, 0, 0, 0))],
        out_shape=[jax.ShapeDtypeStruct((s, BR_WIDTH), F32), jax.ShapeDtypeStruct((n, N_HEADS, HEAD_DIM, HEAD_DIM), F32)],
        scratch_shapes=[pltpu.VMEM((N_HEADS, HEAD_DIM, HEAD_DIM), F32)],
        compiler_params=_cp("arbitrary"),
    )(qh, kh, proj, lf)


def _hg_chunk_bwd(qh, kh, proj, lf, states, do, *, name):
    s = qh.shape[0]
    n = s // CHUNK
    vb = C_HI // BR_WIDTH

    def body(q_ref, k_ref, v_ref, lf_ref, st_in_ref, do_ref, dq_ref, dk_ref, dv_ref, dlf_ref, dst_ref):
        @pl.when(pl.program_id(0) == 0)
        def _():
            dst_ref[...] = jnp.zeros_like(dst_ref)

        g_all = _nn_exact(_tri(CHUNK, "lower"), lf_ref[...])
        g = _Heads(g_all[:, h * HEAD_DIM:(h + 1) * HEAD_DIM] for h in range(N_HEADS))
        dq, dk, dv, db, later, dst = _hg_chunk_bwd_math(
            _heads_of(q_ref), _heads_of(k_ref), _heads_of(v_ref), g,
            _Heads(st_in_ref[0, h] for h in range(N_HEADS)), _heads_of(do_ref),
            _Heads(dst_ref[h] for h in range(N_HEADS)))
        dlf_ref[...] = (_nn_exact(_tri(CHUNK, "upper"), jnp.concatenate(db.v, axis=1))
                        + jnp.concatenate(later.v, axis=1))
        for h in range(N_HEADS):
            cols = slice(h * HEAD_DIM, (h + 1) * HEAD_DIM)
            dq_ref[:, cols] = dq.v[h]
            dk_ref[:, cols] = dk.v[h]
            dv_ref[:, cols] = dv.v[h].astype(BF16)
            dst_ref[h] = dst.v[h]

    blk = lambda off: pl.BlockSpec((CHUNK, BR_WIDTH), lambda c: (n - 1 - c, off))
    return pl.pallas_call(
        body, name=name, grid=(n,),
        in_specs=[blk(0), blk(0), blk(vb), blk(0),
                  pl.BlockSpec((1, N_HEADS, HEAD_DIM, HEAD_DIM), lambda c: (n - 1 - c, 0, 0, 0)), blk(0)],
        out_specs=[blk(0), blk(0), blk(0), blk(0)],
        out_shape=[jax.ShapeDtypeStruct((s, BR_WIDTH), F32)] * 2 + [jax.ShapeDtypeStruct((s, BR_WIDTH), BF16),
                                                                    jax.ShapeDtypeStruct((s, BR_WIDTH), F32)],
        scratch_shapes=[pltpu.VMEM((N_HEADS, HEAD_DIM, HEAD_DIM), F32)],
        compiler_params=_cp("arbitrary"),
    )(qh, kh, proj, lf, states, do)


_ANY = pl.BlockSpec(memory_space=pl.ANY)
_MESH = pl.DeviceIdType.MESH


def _all_gather(x_local, *, name, after=()):
    n_after = len(after)

    def body(x_ref, *refs):
        out_ref, send_sems, recv_sems, local_sem = refs[n_after:]
        x, y, c = lax.axis_index("x"), lax.axis_index("y"), lax.axis_index("c")
        me, sibling = (x, y, c), (x, y, 1 - c)
        chips = [(1 - x, y), (x, 1 - y), (1 - x, 1 - y)]

        def slot(px, py, pc):
            return out_ref.at[4 * px + 2 * py + pc]

        def copy(k, block, to, src=None):
            return pltpu.make_async_remote_copy(
                src_ref=slot(*block) if src is None else src, dst_ref=slot(*block),
                send_sem=send_sems.at[k], recv_sem=recv_sems.at[k], device_id=to, device_id_type=_MESH)

        mine = pltpu.make_async_copy(x_ref, slot(*me), local_sem)
        mine.start()
        first = [copy(0, me, sibling, src=x_ref)]
        first += [copy(1 + j, me, (*chip, c), src=x_ref) for j, chip in enumerate(chips)]
        for cp in first:
            cp.start()
        passed = [copy(4 + j, (*chip, c), sibling) for j, chip in enumerate(chips)]
        for j, chip in enumerate(chips):
            copy(1 + j, (*chip, c), me).wait_recv()
            passed[j].start()
        copy(0, sibling, me).wait_recv()
        for j, chip in enumerate(chips):
            copy(4 + j, (*chip, 1 - c), me).wait_recv()
        for cp in first + passed:
            cp.wait_send()
        mine.wait()

    return pl.pallas_call(
        body, name=name, out_shape=jax.ShapeDtypeStruct((N_DEV,) + x_local.shape, x_local.dtype),
        in_specs=[_ANY] * (1 + n_after), out_specs=_ANY,
        scratch_shapes=[pltpu.SemaphoreType.DMA((7,)), pltpu.SemaphoreType.DMA((7,)), pltpu.SemaphoreType.DMA],
    )(x_local, *after)


_HBM = pl.BlockSpec(memory_space=pltpu.HBM)
_SEM = pl.BlockSpec(memory_space=pltpu.SEMAPHORE)
_EFFECT = pltpu.SideEffectType.DATAFLOW_SIDE_EFFECTING


def _peers():
    x, y, c = lax.axis_index("x"), lax.axis_index("y"), lax.axis_index("c")
    out = []
    for k in range(1, N_DEV):
        px, py, pc = x ^ ((k >> 2) & 1), y ^ ((k >> 1) & 1), c ^ (k & 1)
        out.append(((px, py, pc), 4 * px + 2 * py + pc))
    return 4 * x + 2 * y + c, out


def _push_copies(src_ref, land_ref, send_sems, recv_sems, broadcast):
    my, peers = _peers()
    pairs = []
    for k, (pos, idx) in enumerate(peers):
        src = src_ref if broadcast else src_ref.at[idx]
        send = pltpu.make_async_remote_copy(src_ref=src, dst_ref=land_ref.at[my], send_sem=send_sems.at[k],
                                            recv_sem=recv_sems.at[k], device_id=pos, device_id_type=_MESH)
        recv = pltpu.make_async_remote_copy(src_ref=src, dst_ref=land_ref.at[idx], send_sem=send_sems.at[k],
                                            recv_sem=recv_sems.at[k], device_id=pos, device_id_type=_MESH)
        pairs.append((send, recv))
    return pairs


def _push_start(src, land, *, broadcast, name, after=()):
    n_after = len(after)

    def body(src_ref, land_ref, *refs):
        send_sems, recv_sems, _, _, token = refs[n_after:]
        for send, _ in _push_copies(src_ref, land_ref, send_sems, recv_sems, broadcast):
            send.start()
        token[...] = jnp.zeros_like(token)

    return pl.pallas_call(
        body, name=name,
        out_shape=(pltpu.SemaphoreType.DMA((N_DEV - 1,)), pltpu.SemaphoreType.DMA((N_DEV - 1,)),
                   pltpu.HBM(src.shape, src.dtype), pltpu.HBM(land.shape, land.dtype), jax.ShapeDtypeStruct((8, 128), F32)),
        in_specs=(_HBM, _HBM) + (_ANY,) * n_after, out_specs=(_SEM, _SEM, _HBM, _HBM, pl.BlockSpec(memory_space=pltpu.VMEM)),
        input_output_aliases={0: 2, 1: 3}, compiler_params=pltpu.CompilerParams(has_side_effects=_EFFECT),
    )(pltpu.with_memory_space_constraint(src, pltpu.HBM), pltpu.with_memory_space_constraint(land, pltpu.HBM), *after)


def _push_wait(handle, after, *, broadcast, name):
    send_sems, recv_sems, src_thru, land_thru, _ = handle

    def body(src_ref, land_ref, send_sems, recv_sems, *rest):
        for send, recv in _push_copies(src_ref, land_ref, send_sems, recv_sems, broadcast):
            send.wait_send()
            recv.wait_recv()

    return pl.pallas_call(
        body, name=name,
        out_shape=(pltpu.HBM(src_thru.shape, src_thru.dtype), pltpu.HBM(land_thru.shape, land_thru.dtype)),
        in_specs=(_HBM, _HBM, _SEM, _SEM) + (_ANY,) * len(after), out_specs=(_HBM, _HBM),
        input_output_aliases={0: 0, 1: 1}, compiler_params=pltpu.CompilerParams(has_side_effects=_EFFECT),
    )(src_thru, land_thru, send_sems, recv_sems, *after)[1]


def _relay_copies(src_ref, land_ref, sems_a, sems_b):
    x, y, c = lax.axis_index("x"), lax.axis_index("y"), lax.axis_index("c")
    slot = lambda px, py, pc: land_ref.at[4 * px + 2 * py + pc]
    chips = [(1 - x, y), (x, 1 - y), (1 - x, 1 - y)]
    (send_a, recv_a), (send_b, recv_b) = sems_a, sems_b

    def copy(sems, k, src, dst_slot, to):
        return pltpu.make_async_remote_copy(src_ref=src, dst_ref=dst_slot, send_sem=sems[0].at[k], recv_sem=sems[1].at[k],
                                            device_id=to, device_id_type=_MESH)

    first = [copy((send_a, recv_a), 0, src_ref, slot(x, y, c), (x, y, 1 - c))]
    first += [copy((send_a, recv_a), 1 + j, src_ref, slot(x, y, c), (*chip, c)) for j, chip in enumerate(chips)]
    first_in = [copy((send_a, recv_a), 0, src_ref, slot(x, y, 1 - c), (x, y, 1 - c))]
    first_in += [copy((send_a, recv_a), 1 + j, src_ref, slot(*chip, c), (*chip, c)) for j, chip in enumerate(chips)]
    relay = [copy((send_b, recv_b), j, slot(*chip, c), slot(*chip, c), (x, y, 1 - c)) for j, chip in enumerate(chips)]
    relay_in = [copy((send_b, recv_b), j, slot(*chip, 1 - c), slot(*chip, 1 - c), (x, y, 1 - c)) for j, chip in enumerate(chips)]
    return first, first_in, relay, relay_in


def _relay_start(src, land, *, name, after=()):
    n_after = len(after)

    def body(src_ref, land_ref, *refs):
        send_a, recv_a, _, _, token = refs[n_after:]
        for cp in _relay_copies(src_ref, land_ref, (send_a, recv_a), (send_a, recv_a))[0]:
            cp.start()
        token[...] = jnp.zeros_like(token)

    send_a, recv_a, src_thru, land_thru, token = pl.pallas_call(
        body, name=name,
        out_shape=(pltpu.SemaphoreType.DMA((4,)), pltpu.SemaphoreType.DMA((4,)), pltpu.HBM(src.shape, src.dtype),
                   pltpu.HBM(land.shape, land.dtype), jax.ShapeDtypeStruct((8, 128), F32)),
        in_specs=(_HBM, _HBM) + (_ANY,) * n_after, out_specs=(_SEM, _SEM, _HBM, _HBM, pl.BlockSpec(memory_space=pltpu.VMEM)),
        input_output_aliases={0: 2, 1: 3}, compiler_params=pltpu.CompilerParams(has_side_effects=_EFFECT),
    )(pltpu.with_memory_space_constraint(src, pltpu.HBM), pltpu.with_memory_space_constraint(land, pltpu.HBM), *after)
    return (send_a, recv_a), src_thru, land_thru, token


def _relay_mid(handle, after, *, name):
    sems_a, src_thru, land_thru, _ = handle
    n_after = len(after)

    def body(src_ref, land_ref, send_a, recv_a, *refs):
        send_b, recv_b, _, _, token = refs[n_after:]
        _, first_in, relay, _ = _relay_copies(src_ref, land_ref, (send_a, recv_a), (send_b, recv_b))
        for j in range(3):
            first_in[1 + j].wait_recv()
            relay[j].start()
        token[...] = jnp.zeros_like(token)

    send_b, recv_b, src2, land2, token = pl.pallas_call(
        body, name=name,
        out_shape=(pltpu.SemaphoreType.DMA((3,)), pltpu.SemaphoreType.DMA((3,)), pltpu.HBM(src_thru.shape, src_thru.dtype),
                   pltpu.HBM(land_thru.shape, land_thru.dtype), jax.ShapeDtypeStruct((8, 128), F32)),
        in_specs=(_HBM, _HBM, _SEM, _SEM) + (_ANY,) * n_after,
        out_specs=(_SEM, _SEM, _HBM, _HBM, pl.BlockSpec(memory_space=pltpu.VMEM)),
        input_output_aliases={0: 2, 1: 3}, compiler_params=pltpu.CompilerParams(has_side_effects=_EFFECT),
    )(src_thru, land_thru, *sems_a, *after)
    return sems_a, (send_b, recv_b), src2, land2, token


def _relay_wait(handle, after, *, name):
    sems_a, sems_b, src_thru, land_thru, _ = handle

    def body(src_ref, land_ref, send_a, recv_a, send_b, recv_b, *rest):
        first, first_in, relay, relay_in = _relay_copies(src_ref, land_ref, (send_a, recv_a), (send_b, recv_b))
        first_in[0].wait_recv()
        for cp in relay_in:
            cp.wait_recv()
        for cp in first + relay:
            cp.wait_send()

    return pl.pallas_call(
        body, name=name,
        out_shape=(pltpu.HBM(src_thru.shape, src_thru.dtype), pltpu.HBM(land_thru.shape, land_thru.dtype)),
        in_specs=(_HBM, _HBM, _SEM, _SEM, _SEM, _SEM) + (_ANY,) * len(after), out_specs=(_HBM, _HBM),
        input_output_aliases={0: 0, 1: 1}, compiler_params=pltpu.CompilerParams(has_side_effects=_EFFECT),
    )(src_thru, land_thru, *sems_a, *sems_b, *after)[1]


def _adamw(parts, row_off, w, m, v, *, layer=0, n_layers=1, prev=None, name, tr):
    rows, c = w.shape
    r = rows // n_layers
    np_ = parts.shape[0]
    tr = min(tr, r)
    assert r % tr == 0 and row_off % tr == 0
    ob, lb = row_off // tr, layer * (r // tr)
    c1 = 1.0 - ADAM_B1 ** ADAM_STEP
    c2 = 1.0 - ADAM_B2 ** ADAM_STEP
    n_prev = 0 if prev is None else 4

    def body(p_ref, w_ref, m_ref, v_ref, *refs):
        g_ref, d_ref, nm_ref, nv_ref = refs[n_prev:]
        g = p_ref[0].astype(F32)
        for s in range(1, np_):
            g = g + p_ref[s].astype(F32)
        wv = w_ref[...]
        m2 = ADAM_B1 * m_ref[...] + (1.0 - ADAM_B1) * g
        v2 = ADAM_B2 * v_ref[...] + (1.0 - ADAM_B2) * jnp.square(g)
        m_hat = m2 / c1
        v_hat = v2 / c2
        g_ref[...] = g
        d_ref[...] = -ADAM_LR * (m_hat / (jnp.sqrt(v_hat) + ADAM_EPS) + ADAM_WD * wv)
        nm_ref[...] = m2
        nv_ref[...] = v2

    blk = pl.BlockSpec((tr, c), lambda i: (lb + i, 0))
    return pl.pallas_call(
        body, name=name, grid=(r // tr,),
        in_specs=[pl.BlockSpec((np_, tr, c), lambda i: (0, ob + i, 0)), blk, blk, blk] + [_ANY] * n_prev,
        out_specs=[blk] * 4, out_shape=[jax.ShapeDtypeStruct((rows, c), F32)] * 4,
        input_output_aliases={4 + i: i for i in range(n_prev)}, compiler_params=_cp("parallel"),
    )(parts, w, m, v, *(prev or ()))


def _sum_parts(parts, *, name, after=()):
    np_, r, c = parts.shape

    def body(p_ref, *refs):
        o_ref = refs[-1]
        g = p_ref[0]
        for s in range(1, np_):
            g = g + p_ref[s]
        o_ref[...] = g

    vmem = pl.BlockSpec(memory_space=pltpu.VMEM)
    return pl.pallas_call(body, name=name, in_specs=[vmem] + [_ANY] * len(after), out_specs=vmem,
                          out_shape=jax.ShapeDtypeStruct((r, c), F32))(parts, *after)


def _pack(arrs):
    rows = []
    for a in arrs:
        f = a.reshape(-1).astype(F32)
        pad = (-f.shape[0]) % 128
        rows.append(jnp.pad(f, (0, pad)).reshape(-1, 128))
    out = jnp.concatenate(rows, axis=0)
    return jnp.pad(out, ((0, (-out.shape[0]) % 8), (0, 0)))


def _unpack(packed, shapes):
    outs, r0 = [], 0
    for shp in shapes:
        n = 1
        for d in shp:
            n *= d
        nr = -(-n // 128)
        outs.append(packed[r0:r0 + nr].reshape(-1)[:n].reshape(shp))
        r0 += nr
    return outs


_WIN_PIECES = ((0, 4096, 0), (4112, 8208, 0), (4096, 4104, HEAD_DIM - N_HEADS), (4104, 4112, HEAD_DIM - N_HEADS))


RELAYOUT_TILE = 256


def _win_from_shards(shards, *, name):
    k = shards.shape[1]
    tr = min(RELAYOUT_TILE, k)

    def body(x_ref, o_ref):
        cols = []
        for lo, hi, pad in _WIN_PIECES:
            for j in range(N_DEV):
                a, b = max(lo, j * SHARD_IN), min(hi, (j + 1) * SHARD_IN)
                if a < b:
                    cols.append(x_ref[j, :, a - j * SHARD_IN:b - j * SHARD_IN])
            if pad:
                cols.append(jnp.zeros((tr, pad), x_ref.dtype))
        o_ref[...] = jnp.concatenate(cols, axis=1)

    return pl.pallas_call(
        body, name=name, grid=(k // tr,), in_specs=[pl.BlockSpec((N_DEV, tr, SHARD_IN), lambda i: (0, i, 0))],
        out_specs=pl.BlockSpec((tr, N_PROJ), lambda i: (i, 0)), out_shape=jax.ShapeDtypeStruct((k, N_PROJ), shards.dtype),
        compiler_params=_cp("parallel"),
    )(shards)


def _win_to_shards(g, *, name):
    k = g.shape[0]
    tr = min(RELAYOUT_TILE, k)
    starts, off = [], 0
    for lo, hi, pad in _WIN_PIECES:
        starts.append((lo, hi, off))
        off += hi - lo + pad

    def body(g_ref, o_ref):
        for j in range(N_DEV):
            cols = []
            for lo, hi, off in sorted(starts):
                a, b = max(lo, j * SHARD_IN), min(hi, (j + 1) * SHARD_IN)
                if a < b:
                    cols.append(g_ref[:, off + a - lo:off + b - lo])
            o_ref[j] = jnp.concatenate(cols, axis=1)

    return pl.pallas_call(
        body, name=name, grid=(k // tr,), in_specs=[pl.BlockSpec((tr, N_PROJ), lambda i: (i, 0))],
        out_specs=pl.BlockSpec((N_DEV, tr, SHARD_IN), lambda i: (0, i, 0)),
        out_shape=jax.ShapeDtypeStruct((N_DEV, k, SHARD_IN), g.dtype), compiler_params=_cp("parallel"),
    )(g)


def _lower_bounds(logits):
    probs = jax.nn.softmax(logits.astype(F32), axis=0)
    return jnp.cumsum(probs, axis=0) - probs[0]


def _pad_lanes(vec8):
    return jnp.pad(vec8.reshape(1, N_HEADS), ((0, 0), (0, HEAD_DIM - N_HEADS)))


def kernel(x, p, norm_w, w_in, dn_conv_w, dn_A_log, dn_dt_bias, dn_norm_w, hg_lb_logits, hg_norm_w, w_out, w_ple_up, w_ple_gate, final_norm_w, loss_target, m_norm_w, m_w_in, m_dn_conv_w, m_dn_A_log, m_dn_dt_bias, m_dn_norm_w, m_hg_lb_logits, m_hg_norm_w, m_w_out, m_w_ple_up, m_w_ple_gate, m_final_norm_w, v_norm_w, v_w_in, v_dn_conv_w, v_dn_A_log, v_dn_dt_bias, v_dn_norm_w, v_hg_lb_logits, v_hg_norm_w, v_w_out, v_w_ple_up, v_w_ple_gate, v_final_norm_w):
    depth = norm_w.shape[0]
    my = 4 * lax.axis_index("x") + 2 * lax.axis_index("y") + lax.axis_index("c")
    h = x[0]
    tgt = loss_target[0]
    rows_out = D_MODEL // N_DEV
    up_rows = PLE_DIM * (D_MODEL // N_DEV) // D_MODEL
    g_off, u_off = rows_out, 2 * rows_out

    def own_slot(block):
        return lax.dynamic_update_index_in_dim(lax.empty((N_DEV,) + block.shape, block.dtype), block, my, 0)

    win_bf = w_in.astype(BF16)
    rest_bf = [jnp.concatenate([w_out[l], w_ple_gate[l], w_ple_up[l].reshape(up_rows, D_MODEL)], axis=0).astype(BF16)
               for l in range(depth)]
    conv_all = _all_gather(dn_conv_w, name="gather_conv_w")
    conv_full = conv_all.transpose(1, 2, 0, 3).reshape(depth, CONV_W, 3 * BR_WIDTH)
    win_all = {0: _all_gather(win_bf[0], name="gather_w_in_l0", after=[conv_all])}
    pending, relayed = {}, {}
    last = win_all[0]
    for l in range(depth):
        if l > 0:
            relayed["win", l] = _relay_start(win_bf[l], own_slot(win_bf[l]), after=[last], name=f"gather_w_in_l{l}_first")
            last = relayed["win", l][3]
        if l == 0:
            relayed["rest", l] = _relay_start(rest_bf[l], own_slot(rest_bf[l]), after=[last], name=f"gather_rest_l{l}_first")
            last = relayed["rest", l][3]
        else:
            pending["rest", l] = _push_start(rest_bf[l], own_slot(rest_bf[l]), broadcast=True, after=[last],
                                             name=f"gather_rest_l{l}_start")
            last = pending["rest", l][4]
    order_tok = last[0, 0]
    lbs = _lower_bounds(hg_lb_logits)

    saved = []
    weights = []
    for l in range(depth):
        tag = f"l{l}"
        if l > 0:
            win_all[l] = _relay_wait(relayed["win", l], [h], name=f"gather_w_in_{tag}_wait")
        wi = _win_from_shards(win_all[l], name=f"w_in_layout_{tag}")
        nw = norm_w[l] + order_tok if l == 0 else norm_w[l]
        hn = _rms_fwd(h, nw, name=f"rms_fwd_{tag}")
        proj = _mm(hn, wi, mode="nn", out_dtype=F32, name=f"mm_proj_{tag}")
        al, dt = _pad_lanes(dn_A_log[l]), _pad_lanes(dn_dt_bias[l])
        qkv = _dn_qkv_fwd(proj, conv_full[l], name=f"dn_qkv_fwd_{tag}")
        if ("rest", l) in relayed:
            relayed["rest", l] = _relay_mid(relayed["rest", l], [qkv], name=f"gather_rest_{tag}_relay")
            al = al + relayed["rest", l][4][0, 0]
        beta, gcs = _dn_gate_fwd(proj, al, dt, name=f"dn_gate_fwd_{tag}")
        o_dn, st_dn, tinv_dn = _dn_chunk_fwd(qkv, gcs, beta, name=f"dn_chunk_fwd_{tag}")
        lb = lbs[l].reshape(1, BR_WIDTH)
        qh, kh, lf = _hg_prep_fwd(proj, lb, name=f"hg_prep_fwd_{tag}")
        o_hg, st_hg = _hg_chunk_fwd(qh, kh, proj, lf, name=f"hg_chunk_fwd_{tag}")
        y_dn = _hnorm_fwd(o_dn, proj, C_Z, dn_norm_w[l], name=f"hnorm_dn_fwd_{tag}")
        y_hg = _hnorm_fwd(o_hg, proj, C_HZ, hg_norm_w[l], name=f"hnorm_hg_fwd_{tag}")
        y = jnp.concatenate([y_dn, y_hg], axis=1)
        if ("rest", l) in relayed:
            rest_all = _relay_wait(relayed["rest", l], [y], name=f"gather_rest_{tag}_wait")
        else:
            rest_all = _push_wait(pending["rest", l], [y], broadcast=True, name=f"gather_rest_{tag}_wait")
        wo = rest_all[:, 0:rows_out].reshape(D_MODEL, D_MODEL)
        wg = rest_all[:, g_off:g_off + rows_out].reshape(D_MODEL, D_MODEL)
        wu = rest_all[:, u_off:u_off + up_rows].reshape(N_DEV, PLE_DIM, D_MODEL // N_DEV).transpose(1, 0, 2).reshape(PLE_DIM, D_MODEL)
        weights.append((wi, wo, wg, wu))
        h1 = _mm(y, wo, mode="nn", out_dtype=F32, res=h, name=f"mm_out_{tag}")
        pin = []
        if ("win", l + 1) in relayed:
            relayed["win", l + 1] = _relay_mid(relayed["win", l + 1], [h1], name=f"gather_w_in_l{l + 1}_relay")
            pin = [relayed["win", l + 1][4]]
        gp = _mm(h1, wg, mode="nn", out_dtype=F32, after=pin, name=f"mm_gate_{tag}")
        up = _mm(p[l, 0], wu, mode="nn", out_dtype=F32, name=f"mm_up_{tag}")
        h2 = _ple_fwd(h1, gp, up, name=f"ple_fwd_{tag}")
        saved.append(dict(h=h, hn=hn, proj=proj, qkv=qkv, beta=beta, gcs=gcs, st_dn=st_dn, tinv_dn=tinv_dn, qh=qh, kh=kh, lf=lf,
                          st_hg=st_hg, o_dn=o_dn, o_hg=o_hg, y=y, h1=h1, gp=gp, up=up, al=al, dt=dt, lb=lb))
        h = h2

    loss_row, dh, d_final_w = _final_fwd_bwd(h, final_norm_w, tgt, name="final_norm_loss")

    d_norm_w, d_alog, d_dt, d_dn_nw, d_hg_nw, d_lb, d_conv = ([None] * depth for _ in range(7))
    sent = {}
    for l in reversed(range(depth)):
        wi, wo, wg, wu = weights[l]
        sv = saved[l]
        tag = f"l{l}"
        dup, dgp = _ple_bwd(dh, sv["gp"], sv["up"], name=f"ple_bwd_{tag}")
        d_wu = _mm(p[l, 0], dup, mode="tn", out_dtype=BF16, name=f"mm_dwup_{tag}")
        d_wg = _mm(sv["h1"], dgp, mode="tn", out_dtype=BF16, name=f"mm_dwgate_{tag}")
        dh1 = _mm(dgp, wg, mode="nt", out_dtype=F32, res=dh, name=f"mm_dh1_{tag}")
        d_wo = _mm(sv["y"], dh1, mode="tn", out_dtype=BF16, name=f"mm_dwout_{tag}")
        parts_rest = jnp.concatenate(
            [d_wo.reshape(N_DEV, rows_out, D_MODEL), d_wg.reshape(N_DEV, rows_out, D_MODEL),
             d_wu.reshape(PLE_DIM, N_DEV, D_MODEL // N_DEV).transpose(1, 0, 2).reshape(N_DEV, up_rows, D_MODEL)], axis=1)
        sent["rest", l] = _push_start(parts_rest, own_slot(parts_rest[my]), broadcast=False, name=f"exchange_rest_{tag}_start")
        dy = _mm(dh1, wo, mode="nt", out_dtype=F32, name=f"mm_dy_{tag}")
        dn_nw = dn_norm_w[l] + sent["rest", l][4][0, 0]
        do_dn, dz_dn, d_dn_nw[l] = _hnorm_bwd(sv["o_dn"], sv["proj"], C_Z, dn_nw, dy, 0, name=f"hnorm_dn_bwd_{tag}")
        do_hg, dz_hg, d_hg_nw[l] = _hnorm_bwd(sv["o_hg"], sv["proj"], C_HZ, hg_norm_w[l], dy, BR_WIDTH, name=f"hnorm_hg_bwd_{tag}")
        dqkv, d_gc, dbeta = _dn_chunk_bwd(sv["qkv"], sv["gcs"], sv["beta"], sv["st_dn"], sv["tinv_dn"], do_dn, name=f"dn_chunk_bwd_{tag}")
        dqkv_pre, d_conv[l] = _dn_qkv_bwd(sv["proj"], conv_full[l], dqkv, name=f"dn_qkv_bwd_{tag}")
        db, da, d_alog[l], d_dt[l] = _dn_gate_bwd(sv["proj"], sv["al"], sv["dt"], dbeta, d_gc, name=f"dn_gate_bwd_{tag}")
        dqh, dkh, dhi, dlf = _hg_chunk_bwd(sv["qh"], sv["kh"], sv["proj"], sv["lf"], sv["st_hg"], do_hg, name=f"hg_chunk_bwd_{tag}")
        dhq, dhf, d_lb[l] = _hg_prep_bwd(sv["proj"], sv["lb"], dqh, dkh, dlf, name=f"hg_prep_bwd_{tag}")
        dproj = jnp.concatenate([dqkv_pre, dz_dn, dhq, dhf, dhi, dz_hg, db, da], axis=1)
        def push_d_win(after):
            d_win = _mm(sv["hn"], dproj, mode="tn", out_dtype=BF16, after=after, name=f"mm_dwin_{tag}")
            parts_in = _win_to_shards(d_win, name=f"dw_in_shards_{tag}")
            return _push_start(parts_in, own_slot(parts_in[my]), broadcast=False, name=f"exchange_w_in_{tag}_start")

        if l == 0:
            small = _pack([loss_row, jnp.concatenate(d_norm_w[1:], axis=0), d_final_w,
                           jnp.stack([a[0, :N_HEADS] for a in d_alog]), jnp.stack([a[0, :N_HEADS] for a in d_dt]),
                           jnp.concatenate(d_dn_nw, axis=0), jnp.concatenate(d_hg_nw, axis=0), jnp.concatenate(d_lb, axis=0),
                           jnp.stack(d_conv)])
            small_all = _all_gather(small, name="gather_small")
        sent["win", l] = push_d_win([small_all] if l == 0 else [])
        dhn = _mm(dproj, wi, mode="nt", out_dtype=F32, after=[sent["win", l][4]], name=f"mm_dhn_{tag}")
        dh, d_norm_w[l] = _rms_bwd(sv["h"], norm_w[l], dhn, dh1, name=f"rms_bwd_{tag}")
    grad_x = dh[None]

    small_shapes = [(1, 128), (depth - 1, D_MODEL), final_norm_w.shape, dn_A_log.shape, dn_dt_bias.shape, dn_norm_w.shape,
                    hg_norm_w.shape, hg_lb_logits.shape, (depth, CONV_W, 3 * BR_WIDTH)]
    tot = _unpack(_sum_parts(small_all, after=[grad_x], name="sum_small"), small_shapes)
    loss = tot[0][0, 0]
    g_lb = tot[7]
    g_logits = jax.vjp(_lower_bounds, hg_lb_logits)[1](g_lb)[0]
    g_conv = lax.dynamic_slice_in_dim(tot[8], my * (3 * BR_WIDTH // N_DEV), 3 * BR_WIDTH // N_DEV, axis=2)
    small_g = [g_conv, tot[3], tot[4], tot[5], g_logits, tot[6], tot[2]]
    small_w = [dn_conv_w, dn_A_log, dn_dt_bias, dn_norm_w, hg_lb_logits, hg_norm_w, final_norm_w]
    small_m = [m_dn_conv_w, m_dn_A_log, m_dn_dt_bias, m_dn_norm_w, m_hg_lb_logits, m_hg_norm_w, m_final_norm_w]
    small_v = [v_dn_conv_w, v_dn_A_log, v_dn_dt_bias, v_dn_norm_w, v_hg_lb_logits, v_hg_norm_w, v_final_norm_w]
    pk_w = _pack(small_w)
    res_small = _adamw(_pack(small_g)[None], 0, pk_w, _pack(small_m), _pack(small_v), name="adamw_small", tr=pk_w.shape[0])
    shapes_w = [a.shape for a in small_w]
    sg, sd, sm, sv_ = (_unpack(r, shapes_w) for r in res_small)

    r_win = r_wo = r_wg = r_wu = None
    done = [grad_x, res_small[0]]

    def flat(a, cols):
        return a.reshape(-1, cols)

    for l in reversed(range(depth)):
        tag = f"l{l}"
        land_rest = _push_wait(sent["rest", l], done, broadcast=False, name=f"exchange_rest_{tag}_wait")
        r_wo = _adamw(land_rest, 0, flat(w_out, D_MODEL), flat(m_w_out, D_MODEL), flat(v_w_out, D_MODEL), layer=l,
                      n_layers=depth, prev=r_wo, name=f"adamw_w_out_{tag}", tr=rows_out)
        r_wg = _adamw(land_rest, g_off, flat(w_ple_gate, D_MODEL), flat(m_w_ple_gate, D_MODEL), flat(v_w_ple_gate, D_MODEL),
                      layer=l, n_layers=depth, prev=r_wg, name=f"adamw_w_gate_{tag}", tr=rows_out)
        r_wu = _adamw(land_rest, u_off, flat(w_ple_up, D_MODEL), flat(m_w_ple_up, D_MODEL), flat(v_w_ple_up, D_MODEL),
                      layer=l, n_layers=depth, prev=r_wu, name=f"adamw_w_up_{tag}", tr=up_rows)
        done = [r_wo[0], r_wg[0], r_wu[0]]
    for l in reversed(range(depth)):
        tag = f"l{l}"
        if l == 0:
            nw0 = _sum_parts(_all_gather(_pack([d_norm_w[0]]), after=done, name="gather_norm_w"), name="sum_norm_w")
            g_norm_w = jnp.concatenate([_unpack(nw0, [(1, D_MODEL)])[0], tot[1]], axis=0)
            pk_nw = _pack([norm_w])
            r_nw = _adamw(_pack([g_norm_w])[None], 0, pk_nw, _pack([m_norm_w]), _pack([v_norm_w]), name="adamw_norm_w",
                          tr=pk_nw.shape[0])
            r_nw = [_unpack(r, [norm_w.shape])[0] for r in r_nw]
            done = [r_nw[0]]
        land_in = _push_wait(sent["win", l], done, broadcast=False, name=f"exchange_w_in_{tag}_wait")
        r_win = _adamw(land_in, 0, flat(w_in, SHARD_IN), flat(m_w_in, SHARD_IN), flat(v_w_in, SHARD_IN), layer=l,
                       n_layers=depth, prev=r_win, name=f"adamw_w_in_{tag}", tr=256)
        done = [r_win[0]]
    r_win = [o.reshape(w_in.shape) for o in r_win]
    r_wo = [o.reshape(w_out.shape) for o in r_wo]
    r_wg = [o.reshape(w_ple_gate.shape) for o in r_wg]
    r_wu = [o.reshape(w_ple_up.shape) for o in r_wu]

    def order(nw, small_list, big_in, big_out, big_up, big_gate):
        cw, al_, dt_, dnw, lbl, hnw, fw = small_list
        return [nw, big_in, cw, al_, dt_, dnw, lbl, hnw, big_out, big_up, big_gate, fw]

    outs = [loss, grad_x]
    for i, sl in enumerate((sg, sd, sm, sv_)):
        outs += order(r_nw[i], sl, r_win[i], r_wo[i], r_wu[i], r_wg[i])
    return tuple(outs)
```

```python
import functools

import jax
import jax.numpy as jnp
from jax import lax
from jax.experimental import pallas as pl
from jax.experimental.pallas import tpu as pltpu

F32 = jnp.float32
BF16 = jnp.bfloat16
HIGHEST = lax.Precision.HIGHEST

N_DEV = 8
D_MODEL = 2048
PLE_DIM = 256
HEAD_DIM = 128
N_HEADS = 8
BR_WIDTH = N_HEADS * HEAD_DIM
CHUNK = 64
SUB = 16
CONV_W = 4
NORM_EPS = 1e-6
L2_EPS = 1e-6
IN_WIDTH = 8208
SHARD_IN = IN_WIDTH // N_DEV
EXP_CLAMP = 80.0

C_QKV, C_Z, C_HQ, C_HF, C_HI, C_HZ, C_B, C_A, N_PROJ = 0, 3072, 4096, 5120, 6144, 7168, 8192, 8320, 8448

ADAM_LR, ADAM_B1, ADAM_B2, ADAM_EPS, ADAM_WD, ADAM_STEP = 0.001, 0.9, 0.999, 1e-08, 0.01, 10

VMEM_LIMIT = 48 * 1024 * 1024


def _cp(*sem):
    return pltpu.CompilerParams(dimension_semantics=sem, vmem_limit_bytes=VMEM_LIMIT)


class _Heads:
    def __init__(self, vals):
        self.v = tuple(vals)

    def __add__(self, o):
        return _hmap(lambda a, b: a + b, self, o)

    def __radd__(self, o):
        return _hmap(lambda a, b: b + a, self, o)

    def __sub__(self, o):
        return _hmap(lambda a, b: a - b, self, o)

    def __rsub__(self, o):
        return _hmap(lambda a, b: b - a, self, o)

    def __mul__(self, o):
        return _hmap(lambda a, b: a * b, self, o)

    def __rmul__(self, o):
        return _hmap(lambda a, b: b * a, self, o)

    def __neg__(self):
        return _hmap(lambda a: -a, self)

    def __getitem__(self, idx):
        return _hmap(lambda a: a[idx], self)


def _hmap(fn, *args):
    n = next((len(a.v) for a in args if isinstance(a, _Heads)), None)
    if n is None:
        return fn(*args)
    return _Heads(fn(*[a.v[i] if isinstance(a, _Heads) else a for a in args]) for i in range(n))


def _dot(a, b, ca, cb):
    return _hmap(lambda x, y: lax.dot_general(x.astype(BF16), y.astype(BF16), (((ca,), (cb,)), ((), ())),
                                              preferred_element_type=F32), a, b)


def _nn(a, b):
    return _dot(a, b, 1, 0)


def _nt(a, b):
    return _dot(a, b, 1, 1)


def _tn(a, b):
    return _dot(a, b, 0, 0)


def _split(a):
    hi = _hmap(lambda x: x.astype(BF16), a)
    return hi, _hmap(lambda x, h: (x - h.astype(F32)).astype(BF16), a, hi)


def _dot3(a, b, ca, cb):
    ah, al = _split(a)
    bh, bl = _split(b)
    return _dot(ah, bh, ca, cb) + (_dot(ah, bl, ca, cb) + _dot(al, bh, ca, cb))


def _nn_exact(a, b):
    return _hmap(lambda y: lax.dot_general(a, y, (((1,), (0,)), ((), ())), precision=HIGHEST,
                                           preferred_element_type=F32), b)


def _exp(x):
    return _hmap(jnp.exp, x)


def _sum(x, axis):
    return _hmap(lambda a: jnp.sum(a, axis=axis, keepdims=True), x)


def _sigmoid(x):
    return jax.nn.sigmoid(x)


def _silu(x):
    return x * _sigmoid(x)


def _dsilu(x):
    s = _sigmoid(x)
    return s * (1.0 + x * (1.0 - s))


def _silu_and_grad(x):
    s = _sigmoid(x)
    return x * s, s * (1.0 + x * (1.0 - s))


def _softplus(x):
    return jnp.maximum(x, 0.0) + jnp.log(1.0 + jnp.exp(-jnp.abs(x)))


def _iota2(n, m, axis):
    return lax.broadcasted_iota(jnp.int32, (n, m), axis)


def _col2row(col, eye):
    return _hmap(lambda c: jnp.sum(eye * c, axis=0, keepdims=True), col)


def _row2col(row, eye):
    return _hmap(lambda r: jnp.sum(eye * r, axis=1, keepdims=True), row)


def _pick_lane(block, lane_idx):
    lane = _iota2(block.shape[0], block.shape[1], 1)
    return jnp.sum(jnp.where(lane == lane_idx, block, 0.0), axis=1, keepdims=True)


MM_TILE_M, MM_TILE_N, MM_TILE_K = 1024, 1408, 2048


def _tile(dim, cap):
    if dim <= cap:
        return dim
    t = cap - cap % 128
    while dim % t:
        t -= 128
    return t


def _mm(a, b, *, mode, out_dtype, res=None, after=(), name):
    if mode == "nn":
        (m, kd), (_, n) = a.shape, b.shape
    elif mode == "nt":
        (m, kd), (n, _) = a.shape, b.shape
    else:
        (kd, m), (_, n) = a.shape, b.shape
    tm, tn, tk = _tile(m, MM_TILE_M), _tile(n, MM_TILE_N), _tile(kd, MM_TILE_K)
    assert m % tm == 0 and n % tn == 0 and kd % tk == 0, (m, n, kd, tm, tn, tk)
    nk = kd // tk
    ca, cb = {"nn": (1, 0), "nt": (1, 1), "tn": (0, 0)}[mode]

    def body(*refs):
        a_ref, b_ref = refs[:2]
        r_ref = None if res is None else refs[2]
        o_ref, acc_ref = refs[-2:]
        k = pl.program_id(2)

        @pl.when(k == 0)
        def _():
            acc_ref[...] = jnp.zeros_like(acc_ref)

        acc_ref[...] += _dot(a_ref[...], b_ref[...], ca, cb)

        @pl.when(k == nk - 1)
        def _():
            out = acc_ref[...]
            if r_ref is not None:
                out = out + r_ref[...].astype(F32)
            o_ref[...] = out.astype(o_ref.dtype)

    a_spec = pl.BlockSpec((tk, tm), lambda i, j, k: (k, i)) if mode == "tn" else pl.BlockSpec((tm, tk), lambda i, j, k: (i, k))
    b_spec = pl.BlockSpec((tn, tk), lambda i, j, k: (j, k)) if mode == "nt" else pl.BlockSpec((tk, tn), lambda i, j, k: (k, j))
    o_spec = pl.BlockSpec((tm, tn), lambda i, j, k: (i, j))
    in_specs = [a_spec, b_spec] + ([o_spec] if res is not None else []) + [pl.BlockSpec(memory_space=pl.ANY)] * len(after)
    args = (a, b) + ((res,) if res is not None else ()) + tuple(after)
    return pl.pallas_call(
        body, name=name, grid=(m // tm, n // tn, nk), in_specs=in_specs, out_specs=o_spec,
        out_shape=jax.ShapeDtypeStruct((m, n), out_dtype),
        scratch_shapes=[pltpu.VMEM((tm, tn), F32)],
        compiler_params=_cp("parallel", "parallel", "arbitrary"),
    )(*args)


ROW_TILE = 256


def _rms_fwd(h, w, *, name):
    s, d = h.shape
    tr = min(ROW_TILE, s)

    def body(h_ref, w_ref, o_ref):
        x = h_ref[...]
        r = lax.rsqrt(jnp.mean(x * x, axis=-1, keepdims=True) + NORM_EPS)
        o_ref[...] = (x * r * w_ref[...]).astype(o_ref.dtype)

    return pl.pallas_call(
        body, name=name, grid=(s // tr,),
        in_specs=[pl.BlockSpec((tr, d), lambda i: (i, 0)), pl.BlockSpec((1, d), lambda i: (0, 0))],
        out_specs=pl.BlockSpec((tr, d), lambda i: (i, 0)),
        out_shape=jax.ShapeDtypeStruct((s, d), BF16), compiler_params=_cp("parallel"),
    )(h, w.reshape(1, d))


def _rms_bwd_math(x, w, dy):
    d = x.shape[-1]
    r = lax.rsqrt(jnp.mean(x * x, axis=-1, keepdims=True) + NORM_EPS)
    gw = dy * w
    dx = r * gw - x * ((r * r * r) * (jnp.sum(gw * x, axis=-1, keepdims=True) / d))
    return dx, dy * x * r


def _rms_bwd(h, w, dhn, res, *, name):
    s, d = h.shape
    tr = min(ROW_TILE, s)

    def body(h_ref, w_ref, g_ref, r_ref, dh_ref, dw_ref):
        @pl.when(pl.program_id(0) == 0)
        def _():
            dw_ref[...] = jnp.zeros_like(dw_ref)

        dx, dwt = _rms_bwd_math(h_ref[...], w_ref[...], g_ref[...])
        dh_ref[...] = r_ref[...] + dx
        dw_ref[...] += jnp.sum(dwt, axis=0, keepdims=True)

    row = pl.BlockSpec((tr, d), lambda i: (i, 0))
    vec = pl.BlockSpec((1, d), lambda i: (0, 0))
    return pl.pallas_call(
        body, name=name, grid=(s // tr,), in_specs=[row, vec, row, row], out_specs=[row, vec],
        out_shape=[jax.ShapeDtypeStruct((s, d), F32), jax.ShapeDtypeStruct((1, d), F32)],
        compiler_params=_cp("arbitrary"),
    )(h, w.reshape(1, d), dhn, res)


def _final_fwd_bwd(h, w, tgt, *, name):
    s, d = h.shape
    tr = min(ROW_TILE, s)

    def body(h_ref, w_ref, t_ref, loss_ref, dh_ref, dw_ref):
        @pl.when(pl.program_id(0) == 0)
        def _():
            loss_ref[...] = jnp.zeros_like(loss_ref)
            dw_ref[...] = jnp.zeros_like(dw_ref)

        x = h_ref[...]
        wv = w_ref[...]
        r = lax.rsqrt(jnp.mean(x * x, axis=-1, keepdims=True) + NORM_EPS)
        err = x * r * wv - t_ref[...]
        row_loss = jnp.mean(err * err, axis=-1, keepdims=True)
        loss_ref[...] += 0.5 * jnp.sum(row_loss, axis=0, keepdims=True)
        dx, dwt = _rms_bwd_math(x, wv, err / d)
        dh_ref[...] = dx
        dw_ref[...] += jnp.sum(dwt, axis=0, keepdims=True)

    row = pl.BlockSpec((tr, d), lambda i: (i, 0))
    vec = pl.BlockSpec((1, d), lambda i: (0, 0))
    return pl.pallas_call(
        body, name=name, grid=(s // tr,), in_specs=[row, vec, row],
        out_specs=[pl.BlockSpec((1, 128), lambda i: (0, 0)), row, vec],
        out_shape=[jax.ShapeDtypeStruct((1, 128), F32), jax.ShapeDtypeStruct((s, d), F32),
                   jax.ShapeDtypeStruct((1, d), F32)],
        compiler_params=_cp("arbitrary"),
    )(h, w.reshape(1, d), tgt)


def _ple_fwd(h1, gate_pre, up, *, name):
    s, d = h1.shape
    tr = min(ROW_TILE, s)

    def body(h_ref, g_ref, u_ref, o_ref):
        o_ref[...] = h_ref[...] + u_ref[...] * _sigmoid(g_ref[...])

    row = pl.BlockSpec((tr, d), lambda i: (i, 0))
    return pl.pallas_call(body, name=name, grid=(s // tr,), in_specs=[row, row, row], out_specs=row,
                          out_shape=jax.ShapeDtypeStruct((s, d), F32), compiler_params=_cp("parallel"))(h1, gate_pre, up)


def _ple_bwd(dh2, gate_pre, up, *, name):
    s, d = dh2.shape
    tr = min(ROW_TILE, s)

    def body(d_ref, g_ref, u_ref, dup_ref, dgp_ref):
        dh = d_ref[...]
        gate = _sigmoid(g_ref[...])
        dup_ref[...] = (dh * gate).astype(BF16)
        dgp_ref[...] = (dh * u_ref[...] * gate * (1.0 - gate)).astype(BF16)

    row = pl.BlockSpec((tr, d), lambda i: (i, 0))
    return pl.pallas_call(body, name=name, grid=(s // tr,), in_specs=[row, row, row], out_specs=[row, row],
                          out_shape=[jax.ShapeDtypeStruct((s, d), BF16)] * 2, compiler_params=_cp("parallel"))(dh2, gate_pre, up)


HN_TILE = 512


def _hnorm_fwd(o, proj, z_col, w, *, name):
    s = o.shape[0]
    tr = min(HN_TILE, s)

    def body(o_ref, z_ref, w_ref, y_ref):
        wv = w_ref[...]
        for h in range(N_HEADS):
            cols = slice(h * HEAD_DIM, (h + 1) * HEAD_DIM)
            x = o_ref[:, cols]
            r = lax.rsqrt(jnp.mean(x * x, axis=-1, keepdims=True) + NORM_EPS)
            y_ref[:, cols] = (x * r * wv * _silu(z_ref[:, cols])).astype(BF16)

    blk = pl.BlockSpec((tr, BR_WIDTH), lambda i: (i, 0))
    return pl.pallas_call(
        body, name=name, grid=(s // tr,),
        in_specs=[blk, pl.BlockSpec((tr, BR_WIDTH), lambda i: (i, z_col // BR_WIDTH)), pl.BlockSpec((1, HEAD_DIM), lambda i: (0, 0))],
        out_specs=blk, out_shape=jax.ShapeDtypeStruct((s, BR_WIDTH), BF16), compiler_params=_cp("parallel"),
    )(o, proj, w.reshape(1, HEAD_DIM))


def _hnorm_bwd(o, proj, z_col, w, dy, dy_col, *, name):
    s = o.shape[0]
    tr = min(HN_TILE, s)

    def body(o_ref, z_ref, w_ref, dy_ref, do_ref, dz_ref, dw_ref):
        @pl.when(pl.program_id(0) == 0)
        def _():
            dw_ref[...] = jnp.zeros_like(dw_ref)

        wv = w_ref[...]
        dw = jnp.zeros((1, HEAD_DIM), F32)
        for h in range(N_HEADS):
            cols = slice(h * HEAD_DIM, (h + 1) * HEAD_DIM)
            x, z, g = o_ref[:, cols], z_ref[:, cols], dy_ref[:, cols]
            r = lax.rsqrt(jnp.mean(x * x, axis=-1, keepdims=True) + NORM_EPS)
            on = x * r * wv
            silu_z, dsilu_z = _silu_and_grad(z)
            don = g * silu_z
            dz_ref[:, cols] = (g * on * dsilu_z).astype(BF16)
            gw = don * wv
            do_ref[:, cols] = r * gw - x * ((r * r * r) * (jnp.sum(gw * x, axis=-1, keepdims=True) / HEAD_DIM))
            dw = dw + jnp.sum(don * x * r, axis=0, keepdims=True)
        dw_ref[...] += dw

    blk = pl.BlockSpec((tr, BR_WIDTH), lambda i: (i, 0))
    vec = pl.BlockSpec((1, HEAD_DIM), lambda i: (0, 0))
    return pl.pallas_call(
        body, name=name, grid=(s // tr,),
        in_specs=[blk, pl.BlockSpec((tr, BR_WIDTH), lambda i: (i, z_col // BR_WIDTH)), vec,
                  pl.BlockSpec((tr, BR_WIDTH), lambda i: (i, dy_col // BR_WIDTH))],
        out_specs=[blk, blk, vec],
        out_shape=[jax.ShapeDtypeStruct((s, BR_WIDTH), F32), jax.ShapeDtypeStruct((s, BR_WIDTH), BF16),
                   jax.ShapeDtypeStruct((1, HEAD_DIM), F32)],
        compiler_params=_cp("arbitrary"),
    )(o, proj, w.reshape(1, HEAD_DIM), dy)


def _conv_silu(x, w, s):
    row = _iota2(s, x.shape[1], 0)
    c = w[CONV_W - 1:CONV_W, :] * x
    for k in range(1, CONV_W):
        c = c + w[CONV_W - 1 - k:CONV_W - k, :] * jnp.where(row >= k, pltpu.roll(x, k, 0), 0.0)
    return c


def _dn_qkv_fwd(proj, conv_w, *, name):
    s = proj.shape[0]
    nb = 3 * N_HEADS

    def body(x_ref, w_ref, o_ref):
        j = pl.program_id(0)
        sv = _silu(_conv_silu(x_ref[...], w_ref[...], s))
        r = lax.rsqrt(jnp.sum(sv * sv, axis=-1, keepdims=True) + L2_EPS)
        scale = jnp.where(j < N_HEADS, HEAD_DIM ** -0.5, 1.0).astype(F32)
        o_ref[...] = jnp.where(j < 2 * N_HEADS, sv * r * scale, sv)

    return pl.pallas_call(
        body, name=name, grid=(nb,),
        in_specs=[pl.BlockSpec((s, HEAD_DIM), lambda j: (0, j)), pl.BlockSpec((CONV_W, HEAD_DIM), lambda j: (0, j))],
        out_specs=pl.BlockSpec((s, HEAD_DIM), lambda j: (0, j)),
        out_shape=jax.ShapeDtypeStruct((s, 3 * BR_WIDTH), F32), compiler_params=_cp("parallel"),
    )(proj, conv_w)


def _dn_qkv_bwd(proj, conv_w, dqkv, *, name):
    s = proj.shape[0]
    nb = 3 * N_HEADS

    def body(x_ref, w_ref, g_ref, dx_ref, dw_ref):
        j = pl.program_id(0)
        x, w, g = x_ref[...], w_ref[...], g_ref[...]
        c = _conv_silu(x, w, s)
        sv, dsv = _silu_and_grad(c)
        r = lax.rsqrt(jnp.sum(sv * sv, axis=-1, keepdims=True) + L2_EPS)
        scale = jnp.where(j < N_HEADS, HEAD_DIM ** -0.5, 1.0).astype(F32)
        ds_n = scale * (r * g - sv * ((r * r * r) * jnp.sum(g * sv, axis=-1, keepdims=True)))
        dc = jnp.where(j < 2 * N_HEADS, ds_n, g) * dsv
        row = _iota2(s, HEAD_DIM, 0)
        dx = w[CONV_W - 1:CONV_W, :] * dc
        dws = [jnp.sum(dc * x, axis=0, keepdims=True)]
        for k in range(1, CONV_W):
            dx = dx + w[CONV_W - 1 - k:CONV_W - k, :] * jnp.where(row < s - k, pltpu.roll(dc, s - k, 0), 0.0)
            dws.append(jnp.sum(dc * jnp.where(row >= k, pltpu.roll(x, k, 0), 0.0), axis=0, keepdims=True))
        dx_ref[...] = dx.astype(BF16)
        for k in range(CONV_W):
            dw_ref[CONV_W - 1 - k:CONV_W - k, :] = dws[k]

    blk = pl.BlockSpec((s, HEAD_DIM), lambda j: (0, j))
    wblk = pl.BlockSpec((CONV_W, HEAD_DIM), lambda j: (0, j))
    return pl.pallas_call(
        body, name=name, grid=(nb,), in_specs=[blk, wblk, blk], out_specs=[blk, wblk],
        out_shape=[jax.ShapeDtypeStruct((s, 3 * BR_WIDTH), BF16), jax.ShapeDtypeStruct((CONV_W, 3 * BR_WIDTH), F32)],
        compiler_params=_cp("parallel"),
    )(proj, conv_w, dqkv)


def _tri(n, kind):
    r, c = _iota2(n, n, 0), _iota2(n, n, 1)
    if kind == "lower":
        return (r >= c).astype(F32)
    if kind == "upper":
        return (r <= c).astype(F32)
    return (r == c).astype(F32)


GATE_TILE = 512


def _dn_gate_fwd(proj, a_log, dt_bias, *, name):
    s = proj.shape[0]
    tr = min(GATE_TILE, s)

    def body(b_ref, a_ref, al_ref, dt_ref, beta_ref, g_ref):
        beta_ref[...] = _sigmoid(b_ref[...])
        g = -jnp.exp(al_ref[...]) * _softplus(a_ref[...] + dt_ref[...])
        low = _tri(CHUNK, "lower")
        for c in range(tr // CHUNK):
            rows = slice(c * CHUNK, (c + 1) * CHUNK)
            g_ref[rows, :] = _nn_exact(low, g[rows, :])

    blk = lambda cb: pl.BlockSpec((tr, HEAD_DIM), lambda i: (i, cb))
    vec = pl.BlockSpec((1, HEAD_DIM), lambda i: (0, 0))
    out = pl.BlockSpec((tr, HEAD_DIM), lambda i: (i, 0))
    return pl.pallas_call(
        body, name=name, grid=(s // tr,), in_specs=[blk(C_B // HEAD_DIM), blk(C_A // HEAD_DIM), vec, vec],
        out_specs=[out, out], out_shape=[jax.ShapeDtypeStruct((s, HEAD_DIM), F32)] * 2, compiler_params=_cp("parallel"),
    )(proj, proj, a_log, dt_bias)


def _dn_gate_bwd(proj, a_log, dt_bias, dbeta, d_g, *, name):
    s = proj.shape[0]
    tr = min(GATE_TILE, s)

    def body(b_ref, a_ref, al_ref, dt_ref, dbeta_ref, dG_ref, db_ref, da_ref, dal_ref, ddt_ref):
        @pl.when(pl.program_id(0) == 0)
        def _():
            dal_ref[...] = jnp.zeros_like(dal_ref)
            ddt_ref[...] = jnp.zeros_like(ddt_ref)

        beta = _sigmoid(b_ref[...])
        db_ref[...] = (dbeta_ref[...] * beta * (1.0 - beta)).astype(BF16)
        pre = a_ref[...] + dt_ref[...]
        neg_ea = -jnp.exp(al_ref[...])
        up = _tri(CHUNK, "upper")
        d_g = dG_ref[...]
        dg = jnp.concatenate([_nn_exact(up, d_g[c * CHUNK:(c + 1) * CHUNK, :]) for c in range(tr // CHUNK)], axis=0)
        da = dg * neg_ea * _sigmoid(pre)
        da_ref[...] = da.astype(BF16)
        ddt_ref[...] += jnp.sum(da, axis=0, keepdims=True)
        dal_ref[...] += jnp.sum(dg * neg_ea * _softplus(pre), axis=0, keepdims=True)

    blk = lambda cb: pl.BlockSpec((tr, HEAD_DIM), lambda i: (i, cb))
    vec = pl.BlockSpec((1, HEAD_DIM), lambda i: (0, 0))
    io = pl.BlockSpec((tr, HEAD_DIM), lambda i: (i, 0))
    return pl.pallas_call(
        body, name=name, grid=(s // tr,),
        in_specs=[blk(C_B // HEAD_DIM), blk(C_A // HEAD_DIM), vec, vec, io, io], out_specs=[io, io, vec, vec],
        out_shape=[jax.ShapeDtypeStruct((s, HEAD_DIM), BF16)] * 2 + [jax.ShapeDtypeStruct((1, HEAD_DIM), F32)] * 2,
        compiler_params=_cp("arbitrary"),
    )(proj, proj, a_log, dt_bias, dbeta, d_g)


def _unit_lower_inverse(a_strict, eye):
    x = -a_strict
    t = x + eye
    p = x
    n = 2
    while n < CHUNK:
        p = _nn(p, p)
        t = t + _nn(t, p)
        n *= 2
    return t


def _rows(*xs):
    return _hmap(lambda *a: jnp.concatenate(a, axis=0), *xs)


def _lanes(*xs):
    return _hmap(lambda *a: jnp.concatenate(a, axis=1), *xs)


def _dn_chunk_common(q, k, v, gc, beta, st, with_qd_state, t_inv=None):
    c, d = CHUNK, HEAD_DIM
    eye = _tri(c, "eye")
    low = _tri(c, "lower")
    strict = low - eye
    grow = _col2row(gc, eye)
    dec = _hmap(lambda g_, gr: low * jnp.exp(low * (g_ - gr)), gc, grow)
    kb = k * beta
    kq = _nt(_rows(kb, q), k)
    a_mat = kq[0:c, :] * dec * strict
    qk = kq[c:2 * c, :] * dec
    if t_inv is None:
        t_inv = _unit_lower_inverse(a_mat, eye)
    e_g = _exp(gc)
    qd = q * e_g
    uw = _nn(t_inv, _lanes(v * beta, kb * e_g))
    u, w = uw[:, 0:d], uw[:, d:2 * d]
    last = (_iota2(c, 1, 0) == c - 1).astype(F32)
    g_last = _sum(gc * last, 0)
    e_t = _exp(g_last - gc)
    kt = k * e_t
    tail = _exp(g_last)
    if with_qd_state:
        ws = _nn(_rows(w, qd), st)
        vn, qds = u - ws[0:c, :], ws[c:2 * c, :]
    else:
        vn, qds = u - _nn(w, st), None
    return dict(eye=eye, low=low, strict=strict, dec=dec, kb=kb, a_mat=a_mat, t_inv=t_inv, e_g=e_g, u=u, w=w, uw=uw,
                qk=qk, qd=qd, qds=qds, last=last, e_t=e_t, kt=kt, tail=tail, vn=vn)


def _dn_chunk_fwd_math(q, k, v, gc, beta, st):
    m = _dn_chunk_common(q, k, v, gc, beta, st, True)
    o = m["qds"] + _nn(m["qk"], m["vn"])
    st2 = st * m["tail"] + _tn(m["kt"], m["vn"])
    return o, st2, m["t_inv"]


def _dn_chunk_bwd_math(q, k, v, gc, beta, st, do, dst2, t_inv=None):
    c, d = CHUNK, HEAD_DIM
    m = _dn_chunk_common(q, k, v, gc, beta, st, False, t_inv)
    eye, low, strict = m["eye"], m["low"], m["strict"]
    dvn = _tn(m["qk"], do) + _nn(m["kt"], dst2)
    dqk = _nt(do, m["vn"]) * low
    both = _rows(do, dvn)
    ds_both = _nt(both, st)
    dqd, dw = ds_both[0:c, :], -ds_both[c:2 * c, :]
    dst = _tn(_rows(m["qd"], -m["w"]), both) + dst2 * m["tail"]
    dkt = _nt(m["vn"], dst2)
    dtail = _sum(_sum(st * dst2, 1), 0)
    dvb_dkg = _tn(m["t_inv"], _lanes(dvn, dw))
    dvb, dkg = dvb_dkg[:, 0:d], dvb_dkg[:, d:2 * d]
    d_a = _nt(dvb_dkg, m["uw"]) * (-strict)
    dkk = d_a * m["dec"]
    dp = dqk * m["dec"]
    dpk = _rows(dp, dkk)
    dq_dkb = _nn(dpk, k)
    dq = dq_dkb[0:c, :] + dqd * m["e_g"]
    dkb = dq_dkb[c:2 * c, :] + dkg * m["e_g"]
    dk = _tn(dpk, _rows(q, m["kb"])) + dkb * beta + dkt * m["e_t"]
    dv = dvb * beta
    dbeta = _sum(dvb * v + dkb * k, 1)
    de_g = _sum(dkg * m["kb"] + dqd * q, 1)
    de_t = _sum(dkt * k, 1)
    mm = d_a * m["a_mat"] + dqk * m["qk"]
    dgc = (_sum(mm, 1) - _row2col(_sum(mm, 0), eye) + de_g * m["e_g"] - de_t * m["e_t"]
           + (_sum(de_t * m["e_t"], 0) + dtail * m["tail"]) * m["last"])
    return dq, dk, dv, dgc, dbeta, dst


def _heads_of(ref):
    return _Heads(ref[:, h * HEAD_DIM:(h + 1) * HEAD_DIM] for h in range(N_HEADS))


def _lanes_of(block):
    return _Heads(_pick_lane(block, h) for h in range(N_HEADS))


def _dn_chunk_fwd(qkv, gcs, beta, *, name):
    s = qkv.shape[0]
    n = s // CHUNK

    def body(q_ref, k_ref, v_ref, g_ref, b_ref, o_ref, st_out_ref, tinv_ref, st_ref):
        @pl.when(pl.program_id(0) == 0)
        def _():
            st_ref[...] = jnp.zeros_like(st_ref)

        gblk, bblk = g_ref[...], b_ref[...]
        st = _Heads(st_ref[h] for h in range(N_HEADS))
        o, st2, t_inv = _dn_chunk_fwd_math(_heads_of(q_ref), _heads_of(k_ref), _heads_of(v_ref), _lanes_of(gblk),
                                           _lanes_of(bblk), st)
        for h in range(N_HEADS):
            st_out_ref[0, h] = st.v[h]
            tinv_ref[0, h] = t_inv.v[h].astype(BF16)
            o_ref[:, h * HEAD_DIM:(h + 1) * HEAD_DIM] = o.v[h]
            st_ref[h] = st2.v[h]

    blk = lambda off: pl.BlockSpec((CHUNK, BR_WIDTH), lambda c: (c, off))
    sc = pl.BlockSpec((CHUNK, HEAD_DIM), lambda c: (c, 0))
    return pl.pallas_call(
        body, name=name, grid=(n,),
        in_specs=[blk(0), blk(1), blk(2), sc, sc],
        out_specs=[blk(0), pl.BlockSpec((1, N_HEADS, HEAD_DIM, HEAD_DIM), lambda c: (c, 0, 0, 0)),
                   pl.BlockSpec((1, N_HEADS, CHUNK, CHUNK), lambda c: (c, 0, 0, 0))],
        out_shape=[jax.ShapeDtypeStruct((s, BR_WIDTH), F32), jax.ShapeDtypeStruct((n, N_HEADS, HEAD_DIM, HEAD_DIM), F32),
                   jax.ShapeDtypeStruct((n, N_HEADS, CHUNK, CHUNK), BF16)],
        scratch_shapes=[pltpu.VMEM((N_HEADS, HEAD_DIM, HEAD_DIM), F32)],
        compiler_params=_cp("arbitrary"),
    )(qkv, qkv, qkv, gcs, beta)


def _dn_chunk_bwd(qkv, gcs, beta, states, tinvs, do, *, name):
    s = qkv.shape[0]
    n = s // CHUNK

    def body(q_ref, k_ref, v_ref, g_ref, b_ref, st_in_ref, tinv_ref, do_ref, dqkv_ref, dg_ref, dbeta_ref, dst_ref):
        @pl.when(pl.program_id(0) == 0)
        def _():
            dst_ref[...] = jnp.zeros_like(dst_ref)

        gblk, bblk = g_ref[...], b_ref[...]
        lane = _iota2(CHUNK, HEAD_DIM, 1)
        dg_all = jnp.zeros((CHUNK, HEAD_DIM), F32)
        dbeta_all = jnp.zeros((CHUNK, HEAD_DIM), F32)
        dq, dk, dv, dgc, dbeta, dst = _dn_chunk_bwd_math(
            _heads_of(q_ref), _heads_of(k_ref), _heads_of(v_ref), _lanes_of(gblk), _lanes_of(bblk),
            _Heads(st_in_ref[0, h] for h in range(N_HEADS)), _heads_of(do_ref),
            _Heads(dst_ref[h] for h in range(N_HEADS)), _Heads(tinv_ref[0, h] for h in range(N_HEADS)))
        for h in range(N_HEADS):
            for part, val in enumerate((dq, dk, dv)):
                c0 = part * BR_WIDTH + h * HEAD_DIM
                dqkv_ref[:, c0:c0 + HEAD_DIM] = val.v[h]
            dg_all = jnp.where(lane == h, dgc.v[h], dg_all)
            dbeta_all = jnp.where(lane == h, dbeta.v[h], dbeta_all)
            dst_ref[h] = dst.v[h]
        dg_ref[...] = dg_all
        dbeta_ref[...] = dbeta_all

    blk = lambda off: pl.BlockSpec((CHUNK, BR_WIDTH), lambda c: (n - 1 - c, off))
    sc = pl.BlockSpec((CHUNK, HEAD_DIM), lambda c: (n - 1 - c, 0))
    outs = pl.pallas_call(
        body, name=name, grid=(n,),
        in_specs=[blk(0), blk(1), blk(2), sc, sc,
                  pl.BlockSpec((1, N_HEADS, HEAD_DIM, HEAD_DIM), lambda c: (n - 1 - c, 0, 0, 0)),
                  pl.BlockSpec((1, N_HEADS, CHUNK, CHUNK), lambda c: (n - 1 - c, 0, 0, 0)), blk(0)],
        out_specs=[pl.BlockSpec((CHUNK, 3 * BR_WIDTH), lambda c: (n - 1 - c, 0)), sc, sc],
        out_shape=[jax.ShapeDtypeStruct((s, 3 * BR_WIDTH), F32)] + [jax.ShapeDtypeStruct((s, HEAD_DIM), F32)] * 2,
        scratch_shapes=[pltpu.VMEM((N_HEADS, HEAD_DIM, HEAD_DIM), F32)],
        compiler_params=_cp("arbitrary"),
    )(qkv, qkv, qkv, gcs, beta, states, tinvs, do)
    return outs


def _hg_prep_fwd(proj, lb, *, name):
    s = proj.shape[0]
    tr = min(ROW_TILE, s)

    def body(q_ref, f_ref, lb_ref, qo_ref, ko_ref, lf_ref):
        f, lbv = f_ref[...], lb_ref[...]
        qo_ref[...] = _silu(q_ref[...])
        ko_ref[...] = (1.0 - lbv) * _sigmoid(-f)
        lf_ref[...] = jnp.log(lbv + (1.0 - lbv) * _sigmoid(f))

    blk = lambda cb: pl.BlockSpec((tr, BR_WIDTH), lambda i: (i, cb))
    out = pl.BlockSpec((tr, BR_WIDTH), lambda i: (i, 0))
    return pl.pallas_call(
        body, name=name, grid=(s // tr,),
        in_specs=[blk(C_HQ // BR_WIDTH), blk(C_HF // BR_WIDTH), pl.BlockSpec((1, BR_WIDTH), lambda i: (0, 0))],
        out_specs=[out, out, out], out_shape=[jax.ShapeDtypeStruct((s, BR_WIDTH), F32)] * 3, compiler_params=_cp("parallel"),
    )(proj, proj, lb)


def _hg_prep_bwd(proj, lb, dq, dk, dlf, *, name):
    s = proj.shape[0]
    tr = min(ROW_TILE, s)

    def body(q_ref, f_ref, lb_ref, dq_ref, dk_ref, dlf_ref, dhq_ref, dhf_ref, dlb_ref):
        @pl.when(pl.program_id(0) == 0)
        def _():
            dlb_ref[...] = jnp.zeros_like(dlb_ref)

        f, lbv = f_ref[...], lb_ref[...]
        dhq_ref[...] = (dq_ref[...] * _dsilu(q_ref[...])).astype(BF16)
        sp, sn = _sigmoid(f), _sigmoid(-f)
        inner = lbv + (1.0 - lbv) * sp
        dlf_over = dlf_ref[...] / inner
        dkv = dk_ref[...]
        dhf_ref[...] = (dlf_over * (1.0 - lbv) * sp * sn - dkv * (1.0 - lbv) * sn * (1.0 - sn)).astype(BF16)
        dlb_ref[...] += jnp.sum(dlf_over * (1.0 - sp) - dkv * sn, axis=0, keepdims=True)

    blk = lambda cb: pl.BlockSpec((tr, BR_WIDTH), lambda i: (i, cb))
    io = pl.BlockSpec((tr, BR_WIDTH), lambda i: (i, 0))
    vec = pl.BlockSpec((1, BR_WIDTH), lambda i: (0, 0))
    return pl.pallas_call(
        body, name=name, grid=(s // tr,),
        in_specs=[blk(C_HQ // BR_WIDTH), blk(C_HF // BR_WIDTH), vec, io, io, io], out_specs=[io, io, vec],
        out_shape=[jax.ShapeDtypeStruct((s, BR_WIDTH), BF16)] * 2 + [jax.ShapeDtypeStruct((1, BR_WIDTH), F32)],
        compiler_params=_cp("arbitrary"),
    )(proj, proj, lb, dq, dk, dlf)


def _hg_chunk_common(q, k, g):
    c, nb = CHUNK, CHUNK // SUB
    e_g = _exp(g)
    qd = q * e_g
    g_last = g[c - 1:c, :]
    e_t = _exp(g_last - g)
    kt = k * e_t
    tail = _exp(g_last)
    g_refs = [g[i * SUB:i * SUB + 1, :] for i in range(nb)]
    g_ref_rows = _hmap(lambda *rows: jnp.concatenate([jnp.broadcast_to(r, (SUB, r.shape[1])) for r in rows], axis=0), *g_refs)
    e_q = _exp(g - g_ref_rows)
    q_sc = q * e_q
    e_k = [_hmap(lambda gr, g_: jnp.exp(jnp.minimum(gr - g_, EXP_CLAMP)), g_refs[i], g) for i in range(nb)]
    k_sc_all = _rows(*[k * e_k[i] for i in range(nb)])
    row_blk = _iota2(c, 1, 0) // SUB
    masks = [(row_blk == i).astype(F32) for i in range(nb)]
    r_all = _nt(q_sc, k_sc_all)
    a_mat = r_all[:, 0:c] * masks[0]
    for i in range(1, nb):
        a_mat = a_mat + r_all[:, i * c:(i + 1) * c] * masks[i]
    a_mat = a_mat * _tri(c, "lower")
    return dict(e_g=e_g, qd=qd, e_t=e_t, kt=kt, tail=tail, q_sc=q_sc, k_sc_all=k_sc_all, e_q=e_q, e_k=e_k, masks=masks,
                a_mat=a_mat)


def _hg_chunk_fwd_math(q, k, v, g, stt):
    m = _hg_chunk_common(q, k, g)
    o = _nt(m["qd"], stt) + _nn(m["a_mat"], v)
    stt2 = stt * m["tail"] + _tn(v, m["kt"])
    return o, stt2


def _hg_chunk_bwd_math(q, k, v, g, stt, do, dstt2):
    c, nb = CHUNK, CHUNK // SUB
    m = _hg_chunk_common(q, k, g)
    stt2 = stt * m["tail"] + _tn(v, m["kt"])
    later = _sum(stt2 * dstt2, 0)
    dqd = _dot3(do, stt, 1, 0)
    dstt = _tn(do, m["qd"]) + dstt2 * m["tail"]
    d_a = _dot3(do, v, 1, 1) * _tri(c, "lower")
    dv = _tn(m["a_mat"], do) + _nt(m["kt"], dstt2)
    dkt = _dot3(v, dstt2, 1, 0)
    d_blk = _lanes(*[d_a * m["masks"][i] for i in range(nb)])
    dq = dqd * m["e_g"] + _dot3(d_blk, m["k_sc_all"], 1, 0) * m["e_q"]
    dks = _dot3(d_blk, m["q_sc"], 0, 0)
    dk = dkt * m["e_t"]
    for i in range(nb):
        dk = dk + dks[i * c:(i + 1) * c, :] * m["e_k"][i]
    db = q * dq - k * dk
    return dq, dk, dv, db, later, dstt


def _hg_chunk_fwd(qh, kh, proj, lf, *, name):
    s = qh.shape[0]
    n = s // CHUNK
    vb = C_HI // BR_WIDTH

    def body(q_ref, k_ref, v_ref, lf_ref, o_ref, st_out_ref, st_ref):
        @pl.when(pl.program_id(0) == 0)
        def _():
            st_ref[...] = jnp.zeros_like(st_ref)

        st = _Heads(st_ref[h] for h in range(N_HEADS))
        g_all = _nn_exact(_tri(CHUNK, "lower"), lf_ref[...])
        g = _Heads(g_all[:, h * HEAD_DIM:(h + 1) * HEAD_DIM] for h in range(N_HEADS))
        o, st2 = _hg_chunk_fwd_math(_heads_of(q_ref), _heads_of(k_ref), _heads_of(v_ref), g, st)
        for h in range(N_HEADS):
            st_out_ref[0, h] = st.v[h]
            o_ref[:, h * HEAD_DIM:(h + 1) * HEAD_DIM] = o.v[h]
            st_ref[h] = st2.v[h]

    blk = lambda off: pl.BlockSpec((CHUNK, BR_WIDTH), lambda c: (c, off))
    return pl.pallas_call(
        body, name=name, grid=(n,), in_specs=[blk(0), blk(0), blk(vb), blk(0)],
        out_specs=[blk(0), pl.BlockSpec((1, N_HEADS, HEAD_DIM, HEAD_DIM), lambda c: (c, 0, 0, 0))],
        out_shape=[jax.ShapeDtypeStruct((s, BR_WIDTH), F32), jax.ShapeDtypeStruct((n, N_HEADS, HEAD_DIM, HEAD_DIM), F32)],
        scratch_shapes=[pltpu.VMEM((N_HEADS, HEAD_DIM, HEAD_DIM), F32)],
        compiler_params=_cp("arbitrary"),
    )(qh, kh, proj, lf)


def _hg_chunk_bwd(qh, kh, proj, lf, states, do, *, name):
    s = qh.shape[0]
    n = s // CHUNK
    vb = C_HI // BR_WIDTH

    def body(q_ref, k_ref, v_ref, lf_ref, st_in_ref, do_ref, dq_ref, dk_ref, dv_ref, dlf_ref, dst_ref):
        @pl.when(pl.program_id(0) == 0)
        def _():
            dst_ref[...] = jnp.zeros_like(dst_ref)

        g_all = _nn_exact(_tri(CHUNK, "lower"), lf_ref[...])
        g = _Heads(g_all[:, h * HEAD_DIM:(h + 1) * HEAD_DIM] for h in range(N_HEADS))
        dq, dk, dv, db, later, dst = _hg_chunk_bwd_math(
            _heads_of(q_ref), _heads_of(k_ref), _heads_of(v_ref), g,
            _Heads(st_in_ref[0, h] for h in range(N_HEADS)), _heads_of(do_ref),
            _Heads(dst_ref[h] for h in range(N_HEADS)))
        dlf_ref[...] = (_nn_exact(_tri(CHUNK, "upper"), jnp.concatenate(db.v, axis=1))
                        + jnp.concatenate(later.v, axis=1))
        for h in range(N_HEADS):
            cols = slice(h * HEAD_DIM, (h + 1) * HEAD_DIM)
            dq_ref[:, cols] = dq.v[h]
            dk_ref[:, cols] = dk.v[h]
            dv_ref[:, cols] = dv.v[h].astype(BF16)
            dst_ref[h] = dst.v[h]

    blk = lambda off: pl.BlockSpec((CHUNK, BR_WIDTH), lambda c: (n - 1 - c, off))
    return pl.pallas_call(
        body, name=name, grid=(n,),
        in_specs=[blk(0), blk(0), blk(vb), blk(0),
                  pl.BlockSpec((1, N_HEADS, HEAD_DIM, HEAD_DIM), lambda c: (n - 1 - c, 0, 0, 0)), blk(0)],
        out_specs=[blk(0), blk(0), blk(0), blk(0)],
        out_shape=[jax.ShapeDtypeStruct((s, BR_WIDTH), F32)] * 2 + [jax.ShapeDtypeStruct((s, BR_WIDTH), BF16),
                                                                    jax.ShapeDtypeStruct((s, BR_WIDTH), F32)],
        scratch_shapes=[pltpu.VMEM((N_HEADS, HEAD_DIM, HEAD_DIM), F32)],
        compiler_params=_cp("arbitrary"),
    )(qh, kh, proj, lf, states, do)


_ANY = pl.BlockSpec(memory_space=pl.ANY)
_MESH = pl.DeviceIdType.MESH


def _all_gather(x_local, *, name, after=()):
    n_after = len(after)

    def body(x_ref, *refs):
        out_ref, send_sems, recv_sems, local_sem = refs[n_after:]
        x, y, c = lax.axis_index("x"), lax.axis_index("y"), lax.axis_index("c")
        me, sibling = (x, y, c), (x, y, 1 - c)
        chips = [(1 - x, y), (x, 1 - y), (1 - x, 1 - y)]

        def slot(px, py, pc):
            return out_ref.at[4 * px + 2 * py + pc]

        def copy(k, block, to, src=None):
            return pltpu.make_async_remote_copy(
                src_ref=slot(*block) if src is None else src, dst_ref=slot(*block),
                send_sem=send_sems.at[k], recv_sem=recv_sems.at[k], device_id=to, device_id_type=_MESH)

        mine = pltpu.make_async_copy(x_ref, slot(*me), local_sem)
        mine.start()
        first = [copy(0, me, sibling, src=x_ref)]
        first += [copy(1 + j, me, (*chip, c), src=x_ref) for j, chip in enumerate(chips)]
        for cp in first:
            cp.start()
        passed = [copy(4 + j, (*chip, c), sibling) for j, chip in enumerate(chips)]
        for j, chip in enumerate(chips):
            copy(1 + j, (*chip, c), me).wait_recv()
            passed[j].start()
        copy(0, sibling, me).wait_recv()
        for j, chip in enumerate(chips):
            copy(4 + j, (*chip, 1 - c), me).wait_recv()
        for cp in first + passed:
            cp.wait_send()
        mine.wait()

    return pl.pallas_call(
        body, name=name, out_shape=jax.ShapeDtypeStruct((N_DEV,) + x_local.shape, x_local.dtype),
        in_specs=[_ANY] * (1 + n_after), out_specs=_ANY,
        scratch_shapes=[pltpu.SemaphoreType.DMA((7,)), pltpu.SemaphoreType.DMA((7,)), pltpu.SemaphoreType.DMA],
    )(x_local, *after)


_HBM = pl.BlockSpec(memory_space=pltpu.HBM)
_SEM = pl.BlockSpec(memory_space=pltpu.SEMAPHORE)
_EFFECT = pltpu.SideEffectType.DATAFLOW_SIDE_EFFECTING


def _peers():
    x, y, c = lax.axis_index("x"), lax.axis_index("y"), lax.axis_index("c")
    out = []
    for k in range(1, N_DEV):
        px, py, pc = x ^ ((k >> 2) & 1), y ^ ((k >> 1) & 1), c ^ (k & 1)
        out.append(((px, py, pc), 4 * px + 2 * py + pc))
    return 4 * x + 2 * y + c, out


def _push_copies(src_ref, land_ref, send_sems, recv_sems, broadcast):
    my, peers = _peers()
    pairs = []
    for k, (pos, idx) in enumerate(peers):
        src = src_ref if broadcast else src_ref.at[idx]
        send = pltpu.make_async_remote_copy(src_ref=src, dst_ref=land_ref.at[my], send_sem=send_sems.at[k],
                                            recv_sem=recv_sems.at[k], device_id=pos, device_id_type=_MESH)
        recv = pltpu.make_async_remote_copy(src_ref=src, dst_ref=land_ref.at[idx], send_sem=send_sems.at[k],
                                            recv_sem=recv_sems.at[k], device_id=pos, device_id_type=_MESH)
        pairs.append((send, recv))
    return pairs


def _push_start(src, land, *, broadcast, name, after=()):
    n_after = len(after)

    def body(src_ref, land_ref, *refs):
        send_sems, recv_sems, _, _, token = refs[n_after:]
        for send, _ in _push_copies(src_ref, land_ref, send_sems, recv_sems, broadcast):
            send.start()
        token[...] = jnp.zeros_like(token)

    return pl.pallas_call(
        body, name=name,
        out_shape=(pltpu.SemaphoreType.DMA((N_DEV - 1,)), pltpu.SemaphoreType.DMA((N_DEV - 1,)),
                   pltpu.HBM(src.shape, src.dtype), pltpu.HBM(land.shape, land.dtype), jax.ShapeDtypeStruct((8, 128), F32)),
        in_specs=(_HBM, _HBM) + (_ANY,) * n_after, out_specs=(_SEM, _SEM, _HBM, _HBM, pl.BlockSpec(memory_space=pltpu.VMEM)),
        input_output_aliases={0: 2, 1: 3}, compiler_params=pltpu.CompilerParams(has_side_effects=_EFFECT),
    )(pltpu.with_memory_space_constraint(src, pltpu.HBM), pltpu.with_memory_space_constraint(land, pltpu.HBM), *after)


def _push_wait(handle, after, *, broadcast, name):
    send_sems, recv_sems, src_thru, land_thru, _ = handle

    def body(src_ref, land_ref, send_sems, recv_sems, *rest):
        for send, recv in _push_copies(src_ref, land_ref, send_sems, recv_sems, broadcast):
            send.wait_send()
            recv.wait_recv()

    return pl.pallas_call(
        body, name=name,
        out_shape=(pltpu.HBM(src_thru.shape, src_thru.dtype), pltpu.HBM(land_thru.shape, land_thru.dtype)),
        in_specs=(_HBM, _HBM, _SEM, _SEM) + (_ANY,) * len(after), out_specs=(_HBM, _HBM),
        input_output_aliases={0: 0, 1: 1}, compiler_params=pltpu.CompilerParams(has_side_effects=_EFFECT),
    )(src_thru, land_thru, send_sems, recv_sems, *after)[1]


def _relay_copies(src_ref, land_ref, sems_a, sems_b):
    x, y, c = lax.axis_index("x"), lax.axis_index("y"), lax.axis_index("c")
    slot = lambda px, py, pc: land_ref.at[4 * px + 2 * py + pc]
    chips = [(1 - x, y), (x, 1 - y), (1 - x, 1 - y)]
    (send_a, recv_a), (send_b, recv_b) = sems_a, sems_b

    def copy(sems, k, src, dst_slot, to):
        return pltpu.make_async_remote_copy(src_ref=src, dst_ref=dst_slot, send_sem=sems[0].at[k], recv_sem=sems[1].at[k],
                                            device_id=to, device_id_type=_MESH)

    first = [copy((send_a, recv_a), 0, src_ref, slot(x, y, c), (x, y, 1 - c))]
    first += [copy((send_a, recv_a), 1 + j, src_ref, slot(x, y, c), (*chip, c)) for j, chip in enumerate(chips)]
    first_in = [copy((send_a, recv_a), 0, src_ref, slot(x, y, 1 - c), (x, y, 1 - c))]
    first_in += [copy((send_a, recv_a), 1 + j, src_ref, slot(*chip, c), (*chip, c)) for j, chip in enumerate(chips)]
    relay = [copy((send_b, recv_b), j, slot(*chip, c), slot(*chip, c), (x, y, 1 - c)) for j, chip in enumerate(chips)]
    relay_in = [copy((send_b, recv_b), j, slot(*chip, 1 - c), slot(*chip, 1 - c), (x, y, 1 - c)) for j, chip in enumerate(chips)]
    return first, first_in, relay, relay_in


def _relay_start(src, land, *, name, after=()):
    n_after = len(after)

    def body(src_ref, land_ref, *refs):
        send_a, recv_a, _, _, token = refs[n_after:]
        for cp in _relay_copies(src_ref, land_ref, (send_a, recv_a), (send_a, recv_a))[0]:
            cp.start()
        token[...] = jnp.zeros_like(token)

    send_a, recv_a, src_thru, land_thru, token = pl.pallas_call(
        body, name=name,
        out_shape=(pltpu.SemaphoreType.DMA((4,)), pltpu.SemaphoreType.DMA((4,)), pltpu.HBM(src.shape, src.dtype),
                   pltpu.HBM(land.shape, land.dtype), jax.ShapeDtypeStruct((8, 128), F32)),
        in_specs=(_HBM, _HBM) + (_ANY,) * n_after, out_specs=(_SEM, _SEM, _HBM, _HBM, pl.BlockSpec(memory_space=pltpu.VMEM)),
        input_output_aliases={0: 2, 1: 3}, compiler_params=pltpu.CompilerParams(has_side_effects=_EFFECT),
    )(pltpu.with_memory_space_constraint(src, pltpu.HBM), pltpu.with_memory_space_constraint(land, pltpu.HBM), *after)
    return (send_a, recv_a), src_thru, land_thru, token


def _relay_mid(handle, after, *, name):
    sems_a, src_thru, land_thru, _ = handle
    n_after = len(after)

    def body(src_ref, land_ref, send_a, recv_a, *refs):
        send_b, recv_b, _, _, token = refs[n_after:]
        _, first_in, relay, _ = _relay_copies(src_ref, land_ref, (send_a, recv_a), (send_b, recv_b))
        for j in range(3):
            first_in[1 + j].wait_recv()
            relay[j].start()
        token[...] = jnp.zeros_like(token)

    send_b, recv_b, src2, land2, token = pl.pallas_call(
        body, name=name,
        out_shape=(pltpu.SemaphoreType.DMA((3,)), pltpu.SemaphoreType.DMA((3,)), pltpu.HBM(src_thru.shape, src_thru.dtype),
                   pltpu.HBM(land_thru.shape, land_thru.dtype), jax.ShapeDtypeStruct((8, 128), F32)),
        in_specs=(_HBM, _HBM, _SEM, _SEM) + (_ANY,) * n_after,
        out_specs=(_SEM, _SEM, _HBM, _HBM, pl.BlockSpec(memory_space=pltpu.VMEM)),
        input_output_aliases={0: 2, 1: 3}, compiler_params=pltpu.CompilerParams(has_side_effects=_EFFECT),
    )(src_thru, land_thru, *sems_a, *after)
    return sems_a, (send_b, recv_b), src2, land2, token


def _relay_wait(handle, after, *, name):
    sems_a, sems_b, src_thru, land_thru, _ = handle

    def body(src_ref, land_ref, send_a, recv_a, send_b, recv_b, *rest):
        first, first_in, relay, relay_in = _relay_copies(src_ref, land_ref, (send_a, recv_a), (send_b, recv_b))
        first_in[0].wait_recv()
        for cp in relay_in:
            cp.wait_recv()
        for cp in first + relay:
            cp.wait_send()

    return pl.pallas_call(
        body, name=name,
        out_shape=(pltpu.HBM(src_thru.shape, src_thru.dtype), pltpu.HBM(land_thru.shape, land_thru.dtype)),
        in_specs=(_HBM, _HBM, _SEM, _SEM, _SEM, _SEM) + (_ANY,) * len(after), out_specs=(_HBM, _HBM),
        input_output_aliases={0: 0, 1: 1}, compiler_params=pltpu.CompilerParams(has_side_effects=_EFFECT),
    )(src_thru, land_thru, *sems_a, *sems_b, *after)[1]


def _adamw(parts, row_off, w, m, v, *, layer=0, n_layers=1, prev=None, name, tr):
    rows, c = w.shape
    r = rows // n_layers
    np_ = parts.shape[0]
    tr = min(tr, r)
    assert r % tr == 0 and row_off % tr == 0
    ob, lb = row_off // tr, layer * (r // tr)
    c1 = 1.0 - ADAM_B1 ** ADAM_STEP
    c2 = 1.0 - ADAM_B2 ** ADAM_STEP
    n_prev = 0 if prev is None else 4

    def body(p_ref, w_ref, m_ref, v_ref, *refs):
        g_ref, d_ref, nm_ref, nv_ref = refs[n_prev:]
        g = p_ref[0].astype(F32)
        for s in range(1, np_):
            g = g + p_ref[s].astype(F32)
        wv = w_ref[...]
        m2 = ADAM_B1 * m_ref[...] + (1.0 - ADAM_B1) * g
        v2 = ADAM_B2 * v_ref[...] + (1.0 - ADAM_B2) * jnp.square(g)
        m_hat = m2 / c1
        v_hat = v2 / c2
        g_ref[...] = g
        d_ref[...] = -ADAM_LR * (m_hat / (jnp.sqrt(v_hat) + ADAM_EPS) + ADAM_WD * wv)
        nm_ref[...] = m2
        nv_ref[...] = v2

    blk = pl.BlockSpec((tr, c), lambda i: (lb + i, 0))
    return pl.pallas_call(
        body, name=name, grid=(r // tr,),
        in_specs=[pl.BlockSpec((np_, tr, c), lambda i: (0, ob + i, 0)), blk, blk, blk] + [_ANY] * n_prev,
        out_specs=[blk] * 4, out_shape=[jax.ShapeDtypeStruct((rows, c), F32)] * 4,
        input_output_aliases={4 + i: i for i in range(n_prev)}, compiler_params=_cp("parallel"),
    )(parts, w, m, v, *(prev or ()))


def _sum_parts(parts, *, name, after=()):
    np_, r, c = parts.shape

    def body(p_ref, *refs):
        o_ref = refs[-1]
        g = p_ref[0]
        for s in range(1, np_):
            g = g + p_ref[s]
        o_ref[...] = g

    vmem = pl.BlockSpec(memory_space=pltpu.VMEM)
    return pl.pallas_call(body, name=name, in_specs=[vmem] + [_ANY] * len(after), out_specs=vmem,
                          out_shape=jax.ShapeDtypeStruct((r, c), F32))(parts, *after)


def _pack(arrs):
    rows = []
    for a in arrs:
        f = a.reshape(-1).astype(F32)
        pad = (-f.shape[0]) % 128
        rows.append(jnp.pad(f, (0, pad)).reshape(-1, 128))
    out = jnp.concatenate(rows, axis=0)
    return jnp.pad(out, ((0, (-out.shape[0]) % 8), (0, 0)))


def _unpack(packed, shapes):
    outs, r0 = [], 0
    for shp in shapes:
        n = 1
        for d in shp:
            n *= d
        nr = -(-n // 128)
        outs.append(packed[r0:r0 + nr].reshape(-1)[:n].reshape(shp))
        r0 += nr
    return outs


_WIN_PIECES = ((0, 4096, 0), (4112, 8208, 0), (4096, 4104, HEAD_DIM - N_HEADS), (4104, 4112, HEAD_DIM - N_HEADS))


RELAYOUT_TILE = 256
LAST_SPLIT = 4


def _win_from_shards(shards, *, name):
    k = shards.shape[1]
    tr = min(RELAYOUT_TILE, k)

    def body(x_ref, o_ref):
        cols = []
        for lo, hi, pad in _WIN_PIECES:
            for j in range(N_DEV):
                a, b = max(lo, j * SHARD_IN), min(hi, (j + 1) * SHARD_IN)
                if a < b:
                    cols.append(x_ref[j, :, a - j * SHARD_IN:b - j * SHARD_IN])
            if pad:
                cols.append(jnp.zeros((tr, pad), x_ref.dtype))
        o_ref[...] = jnp.concatenate(cols, axis=1)

    return pl.pallas_call(
        body, name=name, grid=(k // tr,), in_specs=[pl.BlockSpec((N_DEV, tr, SHARD_IN), lambda i: (0, i, 0))],
        out_specs=pl.BlockSpec((tr, N_PROJ), lambda i: (i, 0)), out_shape=jax.ShapeDtypeStruct((k, N_PROJ), shards.dtype),
        compiler_params=_cp("parallel"),
    )(shards)


def _win_to_shards(g, *, name):
    k = g.shape[0]
    tr = min(RELAYOUT_TILE, k)
    starts, off = [], 0
    for lo, hi, pad in _WIN_PIECES:
        starts.append((lo, hi, off))
        off += hi - lo + pad

    def body(g_ref, o_ref):
        for j in range(N_DEV):
            cols = []
            for lo, hi, off in sorted(starts):
                a, b = max(lo, j * SHARD_IN), min(hi, (j + 1) * SHARD_IN)
                if a < b:
                    cols.append(g_ref[:, off + a - lo:off + b - lo])
            o_ref[j] = jnp.concatenate(cols, axis=1)

    return pl.pallas_call(
        body, name=name, grid=(k // tr,), in_specs=[pl.BlockSpec((tr, N_PROJ), lambda i: (i, 0))],
        out_specs=pl.BlockSpec((N_DEV, tr, SHARD_IN), lambda i: (0, i, 0)),
        out_shape=jax.ShapeDtypeStruct((N_DEV, k, SHARD_IN), g.dtype), compiler_params=_cp("parallel"),
    )(g)


def _lower_bounds(logits):
    probs = jax.nn.softmax(logits.astype(F32), axis=0)
    return jnp.cumsum(probs, axis=0) - probs[0]


def _pad_lanes(vec8):
    return jnp.pad(vec8.reshape(1, N_HEADS), ((0, 0), (0, HEAD_DIM - N_HEADS)))


def kernel(x, p, norm_w, w_in, dn_conv_w, dn_A_log, dn_dt_bias, dn_norm_w, hg_lb_logits, hg_norm_w, w_out, w_ple_up, w_ple_gate, final_norm_w, loss_target, m_norm_w, m_w_in, m_dn_conv_w, m_dn_A_log, m_dn_dt_bias, m_dn_norm_w, m_hg_lb_logits, m_hg_norm_w, m_w_out, m_w_ple_up, m_w_ple_gate, m_final_norm_w, v_norm_w, v_w_in, v_dn_conv_w, v_dn_A_log, v_dn_dt_bias, v_dn_norm_w, v_hg_lb_logits, v_hg_norm_w, v_w_out, v_w_ple_up, v_w_ple_gate, v_final_norm_w):
    depth = norm_w.shape[0]
    my = 4 * lax.axis_index("x") + 2 * lax.axis_index("y") + lax.axis_index("c")
    h = x[0]
    tgt = loss_target[0]
    rows_out = D_MODEL // N_DEV
    up_rows = PLE_DIM * (D_MODEL // N_DEV) // D_MODEL
    g_off, u_off = rows_out, 2 * rows_out

    def own_slot(block):
        return lax.dynamic_update_index_in_dim(lax.empty((N_DEV,) + block.shape, block.dtype), block, my, 0)

    win_bf = w_in.astype(BF16)
    rest_bf = [jnp.concatenate([w_out[l], w_ple_gate[l], w_ple_up[l].reshape(up_rows, D_MODEL)], axis=0).astype(BF16)
               for l in range(depth)]
    conv_all = _all_gather(dn_conv_w, name="gather_conv_w")
    conv_full = conv_all.transpose(1, 2, 0, 3).reshape(depth, CONV_W, 3 * BR_WIDTH)
    win_all = {0: _all_gather(win_bf[0], name="gather_w_in_l0", after=[conv_all])}
    pending, relayed = {}, {}
    last = win_all[0]
    for l in range(depth):
        if l > 0:
            relayed["win", l] = _relay_start(win_bf[l], own_slot(win_bf[l]), after=[last], name=f"gather_w_in_l{l}_first")
            last = relayed["win", l][3]
        if l == 0:
            relayed["rest", l] = _relay_start(rest_bf[l], own_slot(rest_bf[l]), after=[last], name=f"gather_rest_l{l}_first")
            last = relayed["rest", l][3]
        else:
            pending["rest", l] = _push_start(rest_bf[l], own_slot(rest_bf[l]), broadcast=True, after=[last],
                                             name=f"gather_rest_l{l}_start")
            last = pending["rest", l][4]
    order_tok = last[0, 0]
    lbs = _lower_bounds(hg_lb_logits)

    saved = []
    weights = []
    for l in range(depth):
        tag = f"l{l}"
        if l > 0:
            win_all[l] = _relay_wait(relayed["win", l], [h], name=f"gather_w_in_{tag}_wait")
        wi = _win_from_shards(win_all[l], name=f"w_in_layout_{tag}")
        nw = norm_w[l] + order_tok if l == 0 else norm_w[l]
        hn = _rms_fwd(h, nw, name=f"rms_fwd_{tag}")
        proj = _mm(hn, wi, mode="nn", out_dtype=F32, name=f"mm_proj_{tag}")
        al, dt = _pad_lanes(dn_A_log[l]), _pad_lanes(dn_dt_bias[l])
        qkv = _dn_qkv_fwd(proj, conv_full[l], name=f"dn_qkv_fwd_{tag}")
        if ("rest", l) in relayed:
            relayed["rest", l] = _relay_mid(relayed["rest", l], [qkv], name=f"gather_rest_{tag}_relay")
            al = al + relayed["rest", l][4][0, 0]
        beta, gcs = _dn_gate_fwd(proj, al, dt, name=f"dn_gate_fwd_{tag}")
        o_dn, st_dn, tinv_dn = _dn_chunk_fwd(qkv, gcs, beta, name=f"dn_chunk_fwd_{tag}")
        lb = lbs[l].reshape(1, BR_WIDTH)
        qh, kh, lf = _hg_prep_fwd(proj, lb, name=f"hg_prep_fwd_{tag}")
        o_hg, st_hg = _hg_chunk_fwd(qh, kh, proj, lf, name=f"hg_chunk_fwd_{tag}")
        y_dn = _hnorm_fwd(o_dn, proj, C_Z, dn_norm_w[l], name=f"hnorm_dn_fwd_{tag}")
        y_hg = _hnorm_fwd(o_hg, proj, C_HZ, hg_norm_w[l], name=f"hnorm_hg_fwd_{tag}")
        y = jnp.concatenate([y_dn, y_hg], axis=1)
        if ("rest", l) in relayed:
            rest_all = _relay_wait(relayed["rest", l], [y], name=f"gather_rest_{tag}_wait")
        else:
            rest_all = _push_wait(pending["rest", l], [y], broadcast=True, name=f"gather_rest_{tag}_wait")
        wo = rest_all[:, 0:rows_out].reshape(D_MODEL, D_MODEL)
        wg = rest_all[:, g_off:g_off + rows_out].reshape(D_MODEL, D_MODEL)
        wu = rest_all[:, u_off:u_off + up_rows].reshape(N_DEV, PLE_DIM, D_MODEL // N_DEV).transpose(1, 0, 2).reshape(PLE_DIM, D_MODEL)
        weights.append((wi, wo, wg, wu))
        h1 = _mm(y, wo, mode="nn", out_dtype=F32, res=h, name=f"mm_out_{tag}")
        pin = []
        if ("win", l + 1) in relayed:
            relayed["win", l + 1] = _relay_mid(relayed["win", l + 1], [h1], name=f"gather_w_in_l{l + 1}_relay")
            pin = [relayed["win", l + 1][4]]
        gp = _mm(h1, wg, mode="nn", out_dtype=F32, after=pin, name=f"mm_gate_{tag}")
        up = _mm(p[l, 0], wu, mode="nn", out_dtype=F32, name=f"mm_up_{tag}")
        h2 = _ple_fwd(h1, gp, up, name=f"ple_fwd_{tag}")
        saved.append(dict(h=h, hn=hn, proj=proj, qkv=qkv, beta=beta, gcs=gcs, st_dn=st_dn, tinv_dn=tinv_dn, qh=qh, kh=kh, lf=lf,
                          st_hg=st_hg, o_dn=o_dn, o_hg=o_hg, y=y, h1=h1, gp=gp, up=up, al=al, dt=dt, lb=lb))
        h = h2

    loss_row, dh, d_final_w = _final_fwd_bwd(h, final_norm_w, tgt, name="final_norm_loss")

    d_norm_w, d_alog, d_dt, d_dn_nw, d_hg_nw, d_lb, d_conv = ([None] * depth for _ in range(7))
    sent = {}
    for l in reversed(range(depth)):
        wi, wo, wg, wu = weights[l]
        sv = saved[l]
        tag = f"l{l}"
        dup, dgp = _ple_bwd(dh, sv["gp"], sv["up"], name=f"ple_bwd_{tag}")
        d_wu = _mm(p[l, 0], dup, mode="tn", out_dtype=BF16, name=f"mm_dwup_{tag}")
        d_wg = _mm(sv["h1"], dgp, mode="tn", out_dtype=BF16, name=f"mm_dwgate_{tag}")
        dh1 = _mm(dgp, wg, mode="nt", out_dtype=F32, res=dh, name=f"mm_dh1_{tag}")
        d_wo = _mm(sv["y"], dh1, mode="tn", out_dtype=BF16, name=f"mm_dwout_{tag}")
        parts_rest = jnp.concatenate(
            [d_wo.reshape(N_DEV, rows_out, D_MODEL), d_wg.reshape(N_DEV, rows_out, D_MODEL),
             d_wu.reshape(PLE_DIM, N_DEV, D_MODEL // N_DEV).transpose(1, 0, 2).reshape(N_DEV, up_rows, D_MODEL)], axis=1)
        sent["rest", l] = _push_start(parts_rest, own_slot(parts_rest[my]), broadcast=False, name=f"exchange_rest_{tag}_start")
        dy = _mm(dh1, wo, mode="nt", out_dtype=F32, name=f"mm_dy_{tag}")
        dn_nw = dn_norm_w[l] + sent["rest", l][4][0, 0]
        do_dn, dz_dn, d_dn_nw[l] = _hnorm_bwd(sv["o_dn"], sv["proj"], C_Z, dn_nw, dy, 0, name=f"hnorm_dn_bwd_{tag}")
        do_hg, dz_hg, d_hg_nw[l] = _hnorm_bwd(sv["o_hg"], sv["proj"], C_HZ, hg_norm_w[l], dy, BR_WIDTH, name=f"hnorm_hg_bwd_{tag}")
        dqkv, d_gc, dbeta = _dn_chunk_bwd(sv["qkv"], sv["gcs"], sv["beta"], sv["st_dn"], sv["tinv_dn"], do_dn, name=f"dn_chunk_bwd_{tag}")
        dqkv_pre, d_conv[l] = _dn_qkv_bwd(sv["proj"], conv_full[l], dqkv, name=f"dn_qkv_bwd_{tag}")
        db, da, d_alog[l], d_dt[l] = _dn_gate_bwd(sv["proj"], sv["al"], sv["dt"], dbeta, d_gc, name=f"dn_gate_bwd_{tag}")
        dqh, dkh, dhi, dlf = _hg_chunk_bwd(sv["qh"], sv["kh"], sv["proj"], sv["lf"], sv["st_hg"], do_hg, name=f"hg_chunk_bwd_{tag}")
        dhq, dhf, d_lb[l] = _hg_prep_bwd(sv["proj"], sv["lb"], dqh, dkh, dlf, name=f"hg_prep_bwd_{tag}")
        dproj = jnp.concatenate([dqkv_pre, dz_dn, dhq, dhf, dhi, dz_hg, db, da], axis=1)
        def push_d_win(after):
            n_split = LAST_SPLIT if l == 0 else 1
            rows = D_MODEL // n_split
            handles = []
            for q in range(n_split):
                hn_q = sv["hn"] if n_split == 1 else sv["hn"][:, q * rows:(q + 1) * rows]
                sfx = tag if n_split == 1 else f"{tag}_{q}"
                d_win = _mm(hn_q, dproj, mode="tn", out_dtype=BF16, after=after, name=f"mm_dwin_{sfx}")
                parts_in = _win_to_shards(d_win, name=f"dw_in_shards_{sfx}")
                handles.append(_push_start(parts_in, own_slot(parts_in[my]), broadcast=False, after=after,
                                           name=f"exchange_w_in_{sfx}_start"))
                after = [handles[-1][4]]
            return handles

        if l == 0:
            small = _pack([loss_row, jnp.concatenate(d_norm_w[1:], axis=0), d_final_w,
                           jnp.stack([a[0, :N_HEADS] for a in d_alog]), jnp.stack([a[0, :N_HEADS] for a in d_dt]),
                           jnp.concatenate(d_dn_nw, axis=0), jnp.concatenate(d_hg_nw, axis=0), jnp.concatenate(d_lb, axis=0),
                           jnp.stack(d_conv)])
            small_all = _all_gather(small, name="gather_small")
        sent["win", l] = push_d_win([small_all] if l == 0 else [])
        dhn = _mm(dproj, wi, mode="nt", out_dtype=F32, after=[sent["win", l][-1][4]], name=f"mm_dhn_{tag}")
        dh, d_norm_w[l] = _rms_bwd(sv["h"], norm_w[l], dhn, dh1, name=f"rms_bwd_{tag}")
    grad_x = dh[None]

    small_shapes = [(1, 128), (depth - 1, D_MODEL), final_norm_w.shape, dn_A_log.shape, dn_dt_bias.shape, dn_norm_w.shape,
                    hg_norm_w.shape, hg_lb_logits.shape, (depth, CONV_W, 3 * BR_WIDTH)]
    tot = _unpack(_sum_parts(small_all, after=[grad_x], name="sum_small"), small_shapes)
    loss = tot[0][0, 0]
    g_lb = tot[7]
    g_logits = jax.vjp(_lower_bounds, hg_lb_logits)[1](g_lb)[0]
    g_conv = lax.dynamic_slice_in_dim(tot[8], my * (3 * BR_WIDTH // N_DEV), 3 * BR_WIDTH // N_DEV, axis=2)
    small_g = [g_conv, tot[3], tot[4], tot[5], g_logits, tot[6], tot[2]]
    small_w = [dn_conv_w, dn_A_log, dn_dt_bias, dn_norm_w, hg_lb_logits, hg_norm_w, final_norm_w]
    small_m = [m_dn_conv_w, m_dn_A_log, m_dn_dt_bias, m_dn_norm_w, m_hg_lb_logits, m_hg_norm_w, m_final_norm_w]
    small_v = [v_dn_conv_w, v_dn_A_log, v_dn_dt_bias, v_dn_norm_w, v_hg_lb_logits, v_hg_norm_w, v_final_norm_w]
    pk_w = _pack(small_w)
    res_small = _adamw(_pack(small_g)[None], 0, pk_w, _pack(small_m), _pack(small_v), name="adamw_small", tr=pk_w.shape[0])
    shapes_w = [a.shape for a in small_w]
    sg, sd, sm, sv_ = (_unpack(r, shapes_w) for r in res_small)

    r_win = r_wo = r_wg = r_wu = None
    done = [grad_x, res_small[0]]

    def flat(a, cols):
        return a.reshape(-1, cols)

    for l in reversed(range(depth)):
        tag = f"l{l}"
        land_rest = _push_wait(sent["rest", l], done, broadcast=False, name=f"exchange_rest_{tag}_wait")
        r_wo = _adamw(land_rest, 0, flat(w_out, D_MODEL), flat(m_w_out, D_MODEL), flat(v_w_out, D_MODEL), layer=l,
                      n_layers=depth, prev=r_wo, name=f"adamw_w_out_{tag}", tr=rows_out)
        r_wg = _adamw(land_rest, g_off, flat(w_ple_gate, D_MODEL), flat(m_w_ple_gate, D_MODEL), flat(v_w_ple_gate, D_MODEL),
                      layer=l, n_layers=depth, prev=r_wg, name=f"adamw_w_gate_{tag}", tr=rows_out)
        r_wu = _adamw(land_rest, u_off, flat(w_ple_up, D_MODEL), flat(m_w_ple_up, D_MODEL), flat(v_w_ple_up, D_MODEL),
                      layer=l, n_layers=depth, prev=r_wu, name=f"adamw_w_up_{tag}", tr=up_rows)
        done = [r_wo[0], r_wg[0], r_wu[0]]
    for l in reversed(range(depth)):
        tag = f"l{l}"
        if l == 0:
            nw0 = _sum_parts(_all_gather(_pack([d_norm_w[0]]), after=done, name="gather_norm_w"), name="sum_norm_w")
            g_norm_w = jnp.concatenate([_unpack(nw0, [(1, D_MODEL)])[0], tot[1]], axis=0)
            pk_nw = _pack([norm_w])
            r_nw = _adamw(_pack([g_norm_w])[None], 0, pk_nw, _pack([m_norm_w]), _pack([v_norm_w]), name="adamw_norm_w",
                          tr=pk_nw.shape[0])
            r_nw = [_unpack(r, [norm_w.shape])[0] for r in r_nw]
            done = [r_nw[0]]
        n_split = len(sent["win", l])
        for q, handle in enumerate(sent["win", l]):
            sfx = tag if n_split == 1 else f"{tag}_{q}"
            land_in = _push_wait(handle, done, broadcast=False, name=f"exchange_w_in_{sfx}_wait")
            r_win = _adamw(land_in, 0, flat(w_in, SHARD_IN), flat(m_w_in, SHARD_IN), flat(v_w_in, SHARD_IN),
                           layer=l * n_split + q, n_layers=depth * n_split, prev=r_win, name=f"adamw_w_in_{sfx}", tr=256)
            done = [r_win[0]]
    r_win = [o.reshape(w_in.shape) for o in r_win]
    r_wo = [o.reshape(w_out.shape) for o in r_wo]
    r_wg = [o.reshape(w_ple_gate.shape) for o in r_wg]
    r_wu = [o.reshape(w_ple_up.shape) for o in r_wu]

    def order(nw, small_list, big_in, big_out, big_up, big_gate):
        cw, al_, dt_, dnw, lbl, hnw, fw = small_list
        return [nw, big_in, cw, al_, dt_, dnw, lbl, hnw, big_out, big_up, big_gate, fw]

    outs = [loss, grad_x]
    for i, sl in enumerate((sg, sd, sm, sv_)):
        outs += order(r_nw[i], sl, r_win[i], r_wo[i], r_wu[i], r_wg[i])
    return tuple(outs)
```

```python
import functools

import jax
import jax.numpy as jnp
from jax import lax
from jax.experimental import pallas as pl
from jax.experimental.pallas import tpu as pltpu

F32 = jnp.float32
BF16 = jnp.bfloat16
HIGHEST = lax.Precision.HIGHEST

N_DEV = 8
D_MODEL = 2048
PLE_DIM = 256
HEAD_DIM = 128
N_HEADS = 8
BR_WIDTH = N_HEADS * HEAD_DIM
CHUNK = 64
SUB = 16
CONV_W = 4
NORM_EPS = 1e-6
L2_EPS = 1e-6
IN_WIDTH = 8208
SHARD_IN = IN_WIDTH // N_DEV
EXP_CLAMP = 80.0

C_QKV, C_Z, C_HQ, C_HF, C_HI, C_HZ, C_B, C_A, N_PROJ = 0, 3072, 4096, 5120, 6144, 7168, 8192, 8320, 8448

ADAM_LR, ADAM_B1, ADAM_B2, ADAM_EPS, ADAM_WD, ADAM_STEP = 0.001, 0.9, 0.999, 1e-08, 0.01, 10

VMEM_LIMIT = 48 * 1024 * 1024


def _cp(*sem):
    return pltpu.CompilerParams(dimension_semantics=sem, vmem_limit_bytes=VMEM_LIMIT)


class _Heads:
    def __init__(self, vals):
        self.v = tuple(vals)

    def __add__(self, o):
        return _hmap(lambda a, b: a + b, self, o)

    def __radd__(self, o):
        return _hmap(lambda a, b: b + a, self, o)

    def __sub__(self, o):
        return _hmap(lambda a, b: a - b, self, o)

    def __rsub__(self, o):
        return _hmap(lambda a, b: b - a, self, o)

    def __mul__(self, o):
        return _hmap(lambda a, b: a * b, self, o)

    def __rmul__(self, o):
        return _hmap(lambda a, b: b * a, self, o)

    def __neg__(self):
        return _hmap(lambda a: -a, self)

    def __getitem__(self, idx):
        return _hmap(lambda a: a[idx], self)


def _hmap(fn, *args):
    n = next((len(a.v) for a in args if isinstance(a, _Heads)), None)
    if n is None:
        return fn(*args)
    return _Heads(fn(*[a.v[i] if isinstance(a, _Heads) else a for a in args]) for i in range(n))


def _dot(a, b, ca, cb):
    return _hmap(lambda x, y: lax.dot_general(x.astype(BF16), y.astype(BF16), (((ca,), (cb,)), ((), ())),
                                              preferred_element_type=F32), a, b)


def _nn(a, b):
    return _dot(a, b, 1, 0)


def _nt(a, b):
    return _dot(a, b, 1, 1)


def _tn(a, b):
    return _dot(a, b, 0, 0)


def _split(a):
    hi = _hmap(lambda x: x.astype(BF16), a)
    return hi, _hmap(lambda x, h: (x - h.astype(F32)).astype(BF16), a, hi)


def _dot3(a, b, ca, cb):
    ah, al = _split(a)
    bh, bl = _split(b)
    return _dot(ah, bh, ca, cb) + (_dot(ah, bl, ca, cb) + _dot(al, bh, ca, cb))


def _nn_exact(a, b):
    return _hmap(lambda y: lax.dot_general(a, y, (((1,), (0,)), ((), ())), precision=HIGHEST,
                                           preferred_element_type=F32), b)


def _exp(x):
    return _hmap(jnp.exp, x)


def _sum(x, axis):
    return _hmap(lambda a: jnp.sum(a, axis=axis, keepdims=True), x)


def _sigmoid(x):
    return jax.nn.sigmoid(x)


def _silu(x):
    return x * _sigmoid(x)


def _dsilu(x):
    s = _sigmoid(x)
    return s * (1.0 + x * (1.0 - s))


def _silu_and_grad(x):
    s = _sigmoid(x)
    return x * s, s * (1.0 + x * (1.0 - s))


def _softplus(x):
    return jnp.maximum(x, 0.0) + jnp.log(1.0 + jnp.exp(-jnp.abs(x)))


def _iota2(n, m, axis):
    return lax.broadcasted_iota(jnp.int32, (n, m), axis)


def _col2row(col, eye):
    return _hmap(lambda c: jnp.sum(eye * c, axis=0, keepdims=True), col)


def _row2col(row, eye):
    return _hmap(lambda r: jnp.sum(eye * r, axis=1, keepdims=True), row)


def _pick_lane(block, lane_idx):
    lane = _iota2(block.shape[0], block.shape[1], 1)
    return jnp.sum(jnp.where(lane == lane_idx, block, 0.0), axis=1, keepdims=True)


MM_TILE_M, MM_TILE_N, MM_TILE_K = 1024, 1408, 2048


def _tile(dim, cap):
    if dim <= cap:
        return dim
    t = cap - cap % 128
    while dim % t:
        t -= 128
    return t


def _mm(a, b, *, mode, out_dtype, res=None, after=(), name):
    if mode == "nn":
        (m, kd), (_, n) = a.shape, b.shape
    elif mode == "nt":
        (m, kd), (n, _) = a.shape, b.shape
    else:
        (kd, m), (_, n) = a.shape, b.shape
    tm, tn, tk = _tile(m, MM_TILE_M), _tile(n, MM_TILE_N), _tile(kd, MM_TILE_K)
    assert m % tm == 0 and n % tn == 0 and kd % tk == 0, (m, n, kd, tm, tn, tk)
    nk = kd // tk
    ca, cb = {"nn": (1, 0), "nt": (1, 1), "tn": (0, 0)}[mode]

    def body(*refs):
        a_ref, b_ref = refs[:2]
        r_ref = None if res is None else refs[2]
        o_ref, acc_ref = refs[-2:]
        k = pl.program_id(2)

        @pl.when(k == 0)
        def _():
            acc_ref[...] = jnp.zeros_like(acc_ref)

        acc_ref[...] += _dot(a_ref[...], b_ref[...], ca, cb)

        @pl.when(k == nk - 1)
        def _():
            out = acc_ref[...]
            if r_ref is not None:
                out = out + r_ref[...].astype(F32)
            o_ref[...] = out.astype(o_ref.dtype)

    a_spec = pl.BlockSpec((tk, tm), lambda i, j, k: (k, i)) if mode == "tn" else pl.BlockSpec((tm, tk), lambda i, j, k: (i, k))
    b_spec = pl.BlockSpec((tn, tk), lambda i, j, k: (j, k)) if mode == "nt" else pl.BlockSpec((tk, tn), lambda i, j, k: (k, j))
    o_spec = pl.BlockSpec((tm, tn), lambda i, j, k: (i, j))
    in_specs = [a_spec, b_spec] + ([o_spec] if res is not None else []) + [pl.BlockSpec(memory_space=pl.ANY)] * len(after)
    args = (a, b) + ((res,) if res is not None else ()) + tuple(after)
    return pl.pallas_call(
        body, name=name, grid=(m // tm, n // tn, nk), in_specs=in_specs, out_specs=o_spec,
        out_shape=jax.ShapeDtypeStruct((m, n), out_dtype),
        scratch_shapes=[pltpu.VMEM((tm, tn), F32)],
        compiler_params=_cp("parallel", "parallel", "arbitrary"),
    )(*args)


ROW_TILE = 256


def _rms_fwd(h, w, *, name):
    s, d = h.shape
    tr = min(ROW_TILE, s)

    def body(h_ref, w_ref, o_ref):
        x = h_ref[...]
        r = lax.rsqrt(jnp.mean(x * x, axis=-1, keepdims=True) + NORM_EPS)
        o_ref[...] = (x * r * w_ref[...]).astype(o_ref.dtype)

    return pl.pallas_call(
        body, name=name, grid=(s // tr,),
        in_specs=[pl.BlockSpec((tr, d), lambda i: (i, 0)), pl.BlockSpec((1, d), lambda i: (0, 0))],
        out_specs=pl.BlockSpec((tr, d), lambda i: (i, 0)),
        out_shape=jax.ShapeDtypeStruct((s, d), BF16), compiler_params=_cp("parallel"),
    )(h, w.reshape(1, d))


def _rms_bwd_math(x, w, dy):
    d = x.shape[-1]
    r = lax.rsqrt(jnp.mean(x * x, axis=-1, keepdims=True) + NORM_EPS)
    gw = dy * w
    dx = r * gw - x * ((r * r * r) * (jnp.sum(gw * x, axis=-1, keepdims=True) / d))
    return dx, dy * x * r


def _rms_bwd(h, w, dhn, res, *, name):
    s, d = h.shape
    tr = min(ROW_TILE, s)

    def body(h_ref, w_ref, g_ref, r_ref, dh_ref, dw_ref):
        @pl.when(pl.program_id(0) == 0)
        def _():
            dw_ref[...] = jnp.zeros_like(dw_ref)

        dx, dwt = _rms_bwd_math(h_ref[...], w_ref[...], g_ref[...])
        dh_ref[...] = r_ref[...] + dx
        dw_ref[...] += jnp.sum(dwt, axis=0, keepdims=True)

    row = pl.BlockSpec((tr, d), lambda i: (i, 0))
    vec = pl.BlockSpec((1, d), lambda i: (0, 0))
    return pl.pallas_call(
        body, name=name, grid=(s // tr,), in_specs=[row, vec, row, row], out_specs=[row, vec],
        out_shape=[jax.ShapeDtypeStruct((s, d), F32), jax.ShapeDtypeStruct((1, d), F32)],
        compiler_params=_cp("arbitrary"),
    )(h, w.reshape(1, d), dhn, res)


def _final_fwd_bwd(h, w, tgt, *, name):
    s, d = h.shape
    tr = min(ROW_TILE, s)

    def body(h_ref, w_ref, t_ref, loss_ref, dh_ref, dw_ref):
        @pl.when(pl.program_id(0) == 0)
        def _():
            loss_ref[...] = jnp.zeros_like(loss_ref)
            dw_ref[...] = jnp.zeros_like(dw_ref)

        x = h_ref[...]
        wv = w_ref[...]
        r = lax.rsqrt(jnp.mean(x * x, axis=-1, keepdims=True) + NORM_EPS)
        err = x * r * wv - t_ref[...]
        row_loss = jnp.mean(err * err, axis=-1, keepdims=True)
        loss_ref[...] += 0.5 * jnp.sum(row_loss, axis=0, keepdims=True)
        dx, dwt = _rms_bwd_math(x, wv, err / d)
        dh_ref[...] = dx
        dw_ref[...] += jnp.sum(dwt, axis=0, keepdims=True)

    row = pl.BlockSpec((tr, d), lambda i: (i, 0))
    vec = pl.BlockSpec((1, d), lambda i: (0, 0))
    return pl.pallas_call(
        body, name=name, grid=(s // tr,), in_specs=[row, vec, row],
        out_specs=[pl.BlockSpec((1, 128), lambda i: (0, 0)), row, vec],
        out_shape=[jax.ShapeDtypeStruct((1, 128), F32), jax.ShapeDtypeStruct((s, d), F32),
                   jax.ShapeDtypeStruct((1, d), F32)],
        compiler_params=_cp("arbitrary"),
    )(h, w.reshape(1, d), tgt)


def _ple_fwd(h1, gate_pre, up, *, name):
    s, d = h1.shape
    tr = min(ROW_TILE, s)

    def body(h_ref, g_ref, u_ref, o_ref):
        o_ref[...] = h_ref[...] + u_ref[...] * _sigmoid(g_ref[...])

    row = pl.BlockSpec((tr, d), lambda i: (i, 0))
    return pl.pallas_call(body, name=name, grid=(s // tr,), in_specs=[row, row, row], out_specs=row,
                          out_shape=jax.ShapeDtypeStruct((s, d), F32), compiler_params=_cp("parallel"))(h1, gate_pre, up)


def _ple_bwd(dh2, gate_pre, up, *, name):
    s, d = dh2.shape
    tr = min(ROW_TILE, s)

    def body(d_ref, g_ref, u_ref, dup_ref, dgp_ref):
        dh = d_ref[...]
        gate = _sigmoid(g_ref[...])
        dup_ref[...] = (dh * gate).astype(BF16)
        dgp_ref[...] = (dh * u_ref[...] * gate * (1.0 - gate)).astype(BF16)

    row = pl.BlockSpec((tr, d), lambda i: (i, 0))
    return pl.pallas_call(body, name=name, grid=(s // tr,), in_specs=[row, row, row], out_specs=[row, row],
                          out_shape=[jax.ShapeDtypeStruct((s, d), BF16)] * 2, compiler_params=_cp("parallel"))(dh2, gate_pre, up)


HN_TILE = 512


def _hnorm_fwd(o, proj, z_col, w, *, name):
    s = o.shape[0]
    tr = min(HN_TILE, s)

    def body(o_ref, z_ref, w_ref, y_ref):
        wv = w_ref[...]
        for h in range(N_HEADS):
            cols = slice(h * HEAD_DIM, (h + 1) * HEAD_DIM)
            x = o_ref[:, cols]
            r = lax.rsqrt(jnp.mean(x * x, axis=-1, keepdims=True) + NORM_EPS)
            y_ref[:, cols] = (x * r * wv * _silu(z_ref[:, cols])).astype(BF16)

    blk = pl.BlockSpec((tr, BR_WIDTH), lambda i: (i, 0))
    return pl.pallas_call(
        body, name=name, grid=(s // tr,),
        in_specs=[blk, pl.BlockSpec((tr, BR_WIDTH), lambda i: (i, z_col // BR_WIDTH)), pl.BlockSpec((1, HEAD_DIM), lambda i: (0, 0))],
        out_specs=blk, out_shape=jax.ShapeDtypeStruct((s, BR_WIDTH), BF16), compiler_params=_cp("parallel"),
    )(o, proj, w.reshape(1, HEAD_DIM))


def _hnorm_bwd(o, proj, z_col, w, dy, dy_col, *, name):
    s = o.shape[0]
    tr = min(HN_TILE, s)

    def body(o_ref, z_ref, w_ref, dy_ref, do_ref, dz_ref, dw_ref):
        @pl.when(pl.program_id(0) == 0)
        def _():
            dw_ref[...] = jnp.zeros_like(dw_ref)

        wv = w_ref[...]
        dw = jnp.zeros((1, HEAD_DIM), F32)
        for h in range(N_HEADS):
            cols = slice(h * HEAD_DIM, (h + 1) * HEAD_DIM)
            x, z, g = o_ref[:, cols], z_ref[:, cols], dy_ref[:, cols]
            r = lax.rsqrt(jnp.mean(x * x, axis=-1, keepdims=True) + NORM_EPS)
            on = x * r * wv
            silu_z, dsilu_z = _silu_and_grad(z)
            don = g * silu_z
            dz_ref[:, cols] = (g * on * dsilu_z).astype(BF16)
            gw = don * wv
            do_ref[:, cols] = r * gw - x * ((r * r * r) * (jnp.sum(gw * x, axis=-1, keepdims=True) / HEAD_DIM))
            dw = dw + jnp.sum(don * x * r, axis=0, keepdims=True)
        dw_ref[...] += dw

    blk = pl.BlockSpec((tr, BR_WIDTH), lambda i: (i, 0))
    vec = pl.BlockSpec((1, HEAD_DIM), lambda i: (0, 0))
    return pl.pallas_call(
        body, name=name, grid=(s // tr,),
        in_specs=[blk, pl.BlockSpec((tr, BR_WIDTH), lambda i: (i, z_col // BR_WIDTH)), vec,
                  pl.BlockSpec((tr, BR_WIDTH), lambda i: (i, dy_col // BR_WIDTH))],
        out_specs=[blk, blk, vec],
        out_shape=[jax.ShapeDtypeStruct((s, BR_WIDTH), F32), jax.ShapeDtypeStruct((s, BR_WIDTH), BF16),
                   jax.ShapeDtypeStruct((1, HEAD_DIM), F32)],
        compiler_params=_cp("arbitrary"),
    )(o, proj, w.reshape(1, HEAD_DIM), dy)


def _conv_silu(x, w, s):
    row = _iota2(s, x.shape[1], 0)
    c = w[CONV_W - 1:CONV_W, :] * x
    for k in range(1, CONV_W):
        c = c + w[CONV_W - 1 - k:CONV_W - k, :] * jnp.where(row >= k, pltpu.roll(x, k, 0), 0.0)
    return c


def _dn_qkv_fwd(proj, conv_w, *, name):
    s = proj.shape[0]
    nb = 3 * N_HEADS

    def body(x_ref, w_ref, o_ref):
        j = pl.program_id(0)
        sv = _silu(_conv_silu(x_ref[...], w_ref[...], s))
        r = lax.rsqrt(jnp.sum(sv * sv, axis=-1, keepdims=True) + L2_EPS)
        scale = jnp.where(j < N_HEADS, HEAD_DIM ** -0.5, 1.0).astype(F32)
        o_ref[...] = jnp.where(j < 2 * N_HEADS, sv * r * scale, sv)

    return pl.pallas_call(
        body, name=name, grid=(nb,),
        in_specs=[pl.BlockSpec((s, HEAD_DIM), lambda j: (0, j)), pl.BlockSpec((CONV_W, HEAD_DIM), lambda j: (0, j))],
        out_specs=pl.BlockSpec((s, HEAD_DIM), lambda j: (0, j)),
        out_shape=jax.ShapeDtypeStruct((s, 3 * BR_WIDTH), F32), compiler_params=_cp("parallel"),
    )(proj, conv_w)


def _dn_qkv_bwd(proj, conv_w, dqkv, *, name):
    s = proj.shape[0]
    nb = 3 * N_HEADS

    def body(x_ref, w_ref, g_ref, dx_ref, dw_ref):
        j = pl.program_id(0)
        x, w, g = x_ref[...], w_ref[...], g_ref[...]
        c = _conv_silu(x, w, s)
        sv, dsv = _silu_and_grad(c)
        r = lax.rsqrt(jnp.sum(sv * sv, axis=-1, keepdims=True) + L2_EPS)
        scale = jnp.where(j < N_HEADS, HEAD_DIM ** -0.5, 1.0).astype(F32)
        ds_n = scale * (r * g - sv * ((r * r * r) * jnp.sum(g * sv, axis=-1, keepdims=True)))
        dc = jnp.where(j < 2 * N_HEADS, ds_n, g) * dsv
        row = _iota2(s, HEAD_DIM, 0)
        dx = w[CONV_W - 1:CONV_W, :] * dc
        dws = [jnp.sum(dc * x, axis=0, keepdims=True)]
        for k in range(1, CONV_W):
            dx = dx + w[CONV_W - 1 - k:CONV_W - k, :] * jnp.where(row < s - k, pltpu.roll(dc, s - k, 0), 0.0)
            dws.append(jnp.sum(dc * jnp.where(row >= k, pltpu.roll(x, k, 0), 0.0), axis=0, keepdims=True))
        dx_ref[...] = dx.astype(BF16)
        for k in range(CONV_W):
            dw_ref[CONV_W - 1 - k:CONV_W - k, :] = dws[k]

    blk = pl.BlockSpec((s, HEAD_DIM), lambda j: (0, j))
    wblk = pl.BlockSpec((CONV_W, HEAD_DIM), lambda j: (0, j))
    return pl.pallas_call(
        body, name=name, grid=(nb,), in_specs=[blk, wblk, blk], out_specs=[blk, wblk],
        out_shape=[jax.ShapeDtypeStruct((s, 3 * BR_WIDTH), BF16), jax.ShapeDtypeStruct((CONV_W, 3 * BR_WIDTH), F32)],
        compiler_params=_cp("parallel"),
    )(proj, conv_w, dqkv)


def _tri(n, kind):
    r, c = _iota2(n, n, 0), _iota2(n, n, 1)
    if kind == "lower":
        return (r >= c).astype(F32)
    if kind == "upper":
        return (r <= c).astype(F32)
    return (r == c).astype(F32)


GATE_TILE = 512


def _dn_gate_fwd(proj, a_log, dt_bias, *, name):
    s = proj.shape[0]
    tr = min(GATE_TILE, s)

    def body(b_ref, a_ref, al_ref, dt_ref, beta_ref, g_ref):
        beta_ref[...] = _sigmoid(b_ref[...])
        g = -jnp.exp(al_ref[...]) * _softplus(a_ref[...] + dt_ref[...])
        low = _tri(CHUNK, "lower")
        for c in range(tr // CHUNK):
            rows = slice(c * CHUNK, (c + 1) * CHUNK)
            g_ref[rows, :] = _nn_exact(low, g[rows, :])

    blk = lambda cb: pl.BlockSpec((tr, HEAD_DIM), lambda i: (i, cb))
    vec = pl.BlockSpec((1, HEAD_DIM), lambda i: (0, 0))
    out = pl.BlockSpec((tr, HEAD_DIM), lambda i: (i, 0))
    return pl.pallas_call(
        body, name=name, grid=(s // tr,), in_specs=[blk(C_B // HEAD_DIM), blk(C_A // HEAD_DIM), vec, vec],
        out_specs=[out, out], out_shape=[jax.ShapeDtypeStruct((s, HEAD_DIM), F32)] * 2, compiler_params=_cp("parallel"),
    )(proj, proj, a_log, dt_bias)


def _dn_gate_bwd(proj, a_log, dt_bias, dbeta, d_g, *, name):
    s = proj.shape[0]
    tr = min(GATE_TILE, s)

    def body(b_ref, a_ref, al_ref, dt_ref, dbeta_ref, dG_ref, db_ref, da_ref, dal_ref, ddt_ref):
        @pl.when(pl.program_id(0) == 0)
        def _():
            dal_ref[...] = jnp.zeros_like(dal_ref)
            ddt_ref[...] = jnp.zeros_like(ddt_ref)

        beta = _sigmoid(b_ref[...])
        db_ref[...] = (dbeta_ref[...] * beta * (1.0 - beta)).astype(BF16)
        pre = a_ref[...] + dt_ref[...]
        neg_ea = -jnp.exp(al_ref[...])
        up = _tri(CHUNK, "upper")
        d_g = dG_ref[...]
        dg = jnp.concatenate([_nn_exact(up, d_g[c * CHUNK:(c + 1) * CHUNK, :]) for c in range(tr // CHUNK)], axis=0)
        da = dg * neg_ea * _sigmoid(pre)
        da_ref[...] = da.astype(BF16)
        ddt_ref[...] += jnp.sum(da, axis=0, keepdims=True)
        dal_ref[...] += jnp.sum(dg * neg_ea * _softplus(pre), axis=0, keepdims=True)

    blk = lambda cb: pl.BlockSpec((tr, HEAD_DIM), lambda i: (i, cb))
    vec = pl.BlockSpec((1, HEAD_DIM), lambda i: (0, 0))
    io = pl.BlockSpec((tr, HEAD_DIM), lambda i: (i, 0))
    return pl.pallas_call(
        body, name=name, grid=(s // tr,),
        in_specs=[blk(C_B // HEAD_DIM), blk(C_A // HEAD_DIM), vec, vec, io, io], out_specs=[io, io, vec, vec],
        out_shape=[jax.ShapeDtypeStruct((s, HEAD_DIM), BF16)] * 2 + [jax.ShapeDtypeStruct((1, HEAD_DIM), F32)] * 2,
        compiler_params=_cp("arbitrary"),
    )(proj, proj, a_log, dt_bias, dbeta, d_g)


def _unit_lower_inverse(a_strict, eye):
    x = -a_strict
    t = x + eye
    p = x
    n = 2
    while n < CHUNK:
        p = _nn(p, p)
        t = t + _nn(t, p)
        n *= 2
    return t


def _rows(*xs):
    return _hmap(lambda *a: jnp.concatenate(a, axis=0), *xs)


def _lanes(*xs):
    return _hmap(lambda *a: jnp.concatenate(a, axis=1), *xs)


def _dn_chunk_common(q, k, v, gc, beta, st, with_qd_state, t_inv=None):
    c, d = CHUNK, HEAD_DIM
    eye = _tri(c, "eye")
    low = _tri(c, "lower")
    strict = low - eye
    grow = _col2row(gc, eye)
    dec = _hmap(lambda g_, gr: low * jnp.exp(low * (g_ - gr)), gc, grow)
    kb = k * beta
    kq = _nt(_rows(kb, q), k)
    a_mat = kq[0:c, :] * dec * strict
    qk = kq[c:2 * c, :] * dec
    if t_inv is None:
        t_inv = _unit_lower_inverse(a_mat, eye)
    e_g = _exp(gc)
    qd = q * e_g
    uw = _nn(t_inv, _lanes(v * beta, kb * e_g))
    u, w = uw[:, 0:d], uw[:, d:2 * d]
    last = (_iota2(c, 1, 0) == c - 1).astype(F32)
    g_last = _sum(gc * last, 0)
    e_t = _exp(g_last - gc)
    kt = k * e_t
    tail = _exp(g_last)
    if with_qd_state:
        ws = _nn(_rows(w, qd), st)
        vn, qds = u - ws[0:c, :], ws[c:2 * c, :]
    else:
        vn, qds = u - _nn(w, st), None
    return dict(eye=eye, low=low, strict=strict, dec=dec, kb=kb, a_mat=a_mat, t_inv=t_inv, e_g=e_g, u=u, w=w, uw=uw,
                qk=qk, qd=qd, qds=qds, last=last, e_t=e_t, kt=kt, tail=tail, vn=vn)


def _dn_chunk_fwd_math(q, k, v, gc, beta, st):
    m = _dn_chunk_common(q, k, v, gc, beta, st, True)
    o = m["qds"] + _nn(m["qk"], m["vn"])
    st2 = st * m["tail"] + _tn(m["kt"], m["vn"])
    return o, st2, m["t_inv"]


def _dn_chunk_bwd_math(q, k, v, gc, beta, st, do, dst2, t_inv=None):
    c, d = CHUNK, HEAD_DIM
    m = _dn_chunk_common(q, k, v, gc, beta, st, False, t_inv)
    eye, low, strict = m["eye"], m["low"], m["strict"]
    dvn = _tn(m["qk"], do) + _nn(m["kt"], dst2)
    dqk = _nt(do, m["vn"]) * low
    both = _rows(do, dvn)
    ds_both = _nt(both, st)
    dqd, dw = ds_both[0:c, :], -ds_both[c:2 * c, :]
    dst = _tn(_rows(m["qd"], -m["w"]), both) + dst2 * m["tail"]
    dkt = _nt(m["vn"], dst2)
    dtail = _sum(_sum(st * dst2, 1), 0)
    dvb_dkg = _tn(m["t_inv"], _lanes(dvn, dw))
    dvb, dkg = dvb_dkg[:, 0:d], dvb_dkg[:, d:2 * d]
    d_a = _nt(dvb_dkg, m["uw"]) * (-strict)
    dkk = d_a * m["dec"]
    dp = dqk * m["dec"]
    dpk = _rows(dp, dkk)
    dq_dkb = _nn(dpk, k)
    dq = dq_dkb[0:c, :] + dqd * m["e_g"]
    dkb = dq_dkb[c:2 * c, :] + dkg * m["e_g"]
    dk = _tn(dpk, _rows(q, m["kb"])) + dkb * beta + dkt * m["e_t"]
    dv = dvb * beta
    dbeta = _sum(dvb * v + dkb * k, 1)
    de_g = _sum(dkg * m["kb"] + dqd * q, 1)
    de_t = _sum(dkt * k, 1)
    mm = d_a * m["a_mat"] + dqk * m["qk"]
    dgc = (_sum(mm, 1) - _row2col(_sum(mm, 0), eye) + de_g * m["e_g"] - de_t * m["e_t"]
           + (_sum(de_t * m["e_t"], 0) + dtail * m["tail"]) * m["last"])
    return dq, dk, dv, dgc, dbeta, dst


def _heads_of(ref):
    return _Heads(ref[:, h * HEAD_DIM:(h + 1) * HEAD_DIM] for h in range(N_HEADS))


def _lanes_of(block):
    return _Heads(_pick_lane(block, h) for h in range(N_HEADS))


def _dn_chunk_fwd(qkv, gcs, beta, *, name):
    s = qkv.shape[0]
    n = s // CHUNK

    def body(q_ref, k_ref, v_ref, g_ref, b_ref, o_ref, st_out_ref, tinv_ref, st_ref):
        @pl.when(pl.program_id(0) == 0)
        def _():
            st_ref[...] = jnp.zeros_like(st_ref)

        gblk, bblk = g_ref[...], b_ref[...]
        st = _Heads(st_ref[h] for h in range(N_HEADS))
        o, st2, t_inv = _dn_chunk_fwd_math(_heads_of(q_ref), _heads_of(k_ref), _heads_of(v_ref), _lanes_of(gblk),
                                           _lanes_of(bblk), st)
        for h in range(N_HEADS):
            st_out_ref[0, h] = st.v[h]
            tinv_ref[0, h] = t_inv.v[h].astype(BF16)
            o_ref[:, h * HEAD_DIM:(h + 1) * HEAD_DIM] = o.v[h]
            st_ref[h] = st2.v[h]

    blk = lambda off: pl.BlockSpec((CHUNK, BR_WIDTH), lambda c: (c, off))
    sc = pl.BlockSpec((CHUNK, HEAD_DIM), lambda c: (c, 0))
    return pl.pallas_call(
        body, name=name, grid=(n,),
        in_specs=[blk(0), blk(1), blk(2), sc, sc],
        out_specs=[blk(0), pl.BlockSpec((1, N_HEADS, HEAD_DIM, HEAD_DIM), lambda c: (c, 0, 0, 0)),
                   pl.BlockSpec((1, N_HEADS, CHUNK, CHUNK), lambda c: (c, 0, 0, 0))],
        out_shape=[jax.ShapeDtypeStruct((s, BR_WIDTH), F32), jax.ShapeDtypeStruct((n, N_HEADS, HEAD_DIM, HEAD_DIM), F32),
                   jax.ShapeDtypeStruct((n, N_HEADS, CHUNK, CHUNK), BF16)],
        scratch_shapes=[pltpu.VMEM((N_HEADS, HEAD_DIM, HEAD_DIM), F32)],
        compiler_params=_cp("arbitrary"),
    )(qkv, qkv, qkv, gcs, beta)


def _dn_chunk_bwd(qkv, gcs, beta, states, tinvs, do, *, name):
    s = qkv.shape[0]
    n = s // CHUNK

    def body(q_ref, k_ref, v_ref, g_ref, b_ref, st_in_ref, tinv_ref, do_ref, dqkv_ref, dg_ref, dbeta_ref, dst_ref):
        @pl.when(pl.program_id(0) == 0)
        def _():
            dst_ref[...] = jnp.zeros_like(dst_ref)

        gblk, bblk = g_ref[...], b_ref[...]
        lane = _iota2(CHUNK, HEAD_DIM, 1)
        dg_all = jnp.zeros((CHUNK, HEAD_DIM), F32)
        dbeta_all = jnp.zeros((CHUNK, HEAD_DIM), F32)
        dq, dk, dv, dgc, dbeta, dst = _dn_chunk_bwd_math(
            _heads_of(q_ref), _heads_of(k_ref), _heads_of(v_ref), _lanes_of(gblk), _lanes_of(bblk),
            _Heads(st_in_ref[0, h] for h in range(N_HEADS)), _heads_of(do_ref),
            _Heads(dst_ref[h] for h in range(N_HEADS)), _Heads(tinv_ref[0, h] for h in range(N_HEADS)))
        for h in range(N_HEADS):
            for part, val in enumerate((dq, dk, dv)):
                c0 = part * BR_WIDTH + h * HEAD_DIM
                dqkv_ref[:, c0:c0 + HEAD_DIM] = val.v[h]
            dg_all = jnp.where(lane == h, dgc.v[h], dg_all)
            dbeta_all = jnp.where(lane == h, dbeta.v[h], dbeta_all)
            dst_ref[h] = dst.v[h]
        dg_ref[...] = dg_all
        dbeta_ref[...] = dbeta_all

    blk = lambda off: pl.BlockSpec((CHUNK, BR_WIDTH), lambda c: (n - 1 - c, off))
    sc = pl.BlockSpec((CHUNK, HEAD_DIM), lambda c: (n - 1 - c, 0))
    outs = pl.pallas_call(
        body, name=name, grid=(n,),
        in_specs=[blk(0), blk(1), blk(2), sc, sc,
                  pl.BlockSpec((1, N_HEADS, HEAD_DIM, HEAD_DIM), lambda c: (n - 1 - c, 0, 0, 0)),
                  pl.BlockSpec((1, N_HEADS, CHUNK, CHUNK), lambda c: (n - 1 - c, 0, 0, 0)), blk(0)],
        out_specs=[pl.BlockSpec((CHUNK, 3 * BR_WIDTH), lambda c: (n - 1 - c, 0)), sc, sc],
        out_shape=[jax.ShapeDtypeStruct((s, 3 * BR_WIDTH), F32)] + [jax.ShapeDtypeStruct((s, HEAD_DIM), F32)] * 2,
        scratch_shapes=[pltpu.VMEM((N_HEADS, HEAD_DIM, HEAD_DIM), F32)],
        compiler_params=_cp("arbitrary"),
    )(qkv, qkv, qkv, gcs, beta, states, tinvs, do)
    return outs


def _hg_prep_fwd(proj, lb, *, name):
    s = proj.shape[0]
    tr = min(ROW_TILE, s)

    def body(q_ref, f_ref, lb_ref, qo_ref, ko_ref, lf_ref):
        f, lbv = f_ref[...], lb_ref[...]
        qo_ref[...] = _silu(q_ref[...])
        ko_ref[...] = (1.0 - lbv) * _sigmoid(-f)
        lf_ref[...] = jnp.log(lbv + (1.0 - lbv) * _sigmoid(f))

    blk = lambda cb: pl.BlockSpec((tr, BR_WIDTH), lambda i: (i, cb))
    out = pl.BlockSpec((tr, BR_WIDTH), lambda i: (i, 0))
    return pl.pallas_call(
        body, name=name, grid=(s // tr,),
        in_specs=[blk(C_HQ // BR_WIDTH), blk(C_HF // BR_WIDTH), pl.BlockSpec((1, BR_WIDTH), lambda i: (0, 0))],
        out_specs=[out, out, out], out_shape=[jax.ShapeDtypeStruct((s, BR_WIDTH), F32)] * 3, compiler_params=_cp("parallel"),
    )(proj, proj, lb)


def _hg_prep_bwd(proj, lb, dq, dk, dlf, *, name):
    s = proj.shape[0]
    tr = min(ROW_TILE, s)

    def body(q_ref, f_ref, lb_ref, dq_ref, dk_ref, dlf_ref, dhq_ref, dhf_ref, dlb_ref):
        @pl.when(pl.program_id(0) == 0)
        def _():
            dlb_ref[...] = jnp.zeros_like(dlb_ref)

        f, lbv = f_ref[...], lb_ref[...]
        dhq_ref[...] = (dq_ref[...] * _dsilu(q_ref[...])).astype(BF16)
        sp, sn = _sigmoid(f), _sigmoid(-f)
        inner = lbv + (1.0 - lbv) * sp
        dlf_over = dlf_ref[...] / inner
        dkv = dk_ref[...]
        dhf_ref[...] = (dlf_over * (1.0 - lbv) * sp * sn - dkv * (1.0 - lbv) * sn * (1.0 - sn)).astype(BF16)
        dlb_ref[...] += jnp.sum(dlf_over * (1.0 - sp) - dkv * sn, axis=0, keepdims=True)

    blk = lambda cb: pl.BlockSpec((tr, BR_WIDTH), lambda i: (i, cb))
    io = pl.BlockSpec((tr, BR_WIDTH), lambda i: (i, 0))
    vec = pl.BlockSpec((1, BR_WIDTH), lambda i: (0, 0))
    return pl.pallas_call(
        body, name=name, grid=(s // tr,),
        in_specs=[blk(C_HQ // BR_WIDTH), blk(C_HF // BR_WIDTH), vec, io, io, io], out_specs=[io, io, vec],
        out_shape=[jax.ShapeDtypeStruct((s, BR_WIDTH), BF16)] * 2 + [jax.ShapeDtypeStruct((1, BR_WIDTH), F32)],
        compiler_params=_cp("arbitrary"),
    )(proj, proj, lb, dq, dk, dlf)


def _hg_chunk_common(q, k, g):
    c, nb = CHUNK, CHUNK // SUB
    e_g = _exp(g)
    qd = q * e_g
    g_last = g[c - 1:c, :]
    e_t = _exp(g_last - g)
    kt = k * e_t
    tail = _exp(g_last)
    g_refs = [g[i * SUB:i * SUB + 1, :] for i in range(nb)]
    g_ref_rows = _hmap(lambda *rows: jnp.concatenate([jnp.broadcast_to(r, (SUB, r.shape[1])) for r in rows], axis=0), *g_refs)
    e_q = _exp(g - g_ref_rows)
    q_sc = q * e_q
    e_k = [_hmap(lambda gr, g_: jnp.exp(jnp.minimum(gr - g_, EXP_CLAMP)), g_refs[i], g) for i in range(nb)]
    k_sc_all = _rows(*[k * e_k[i] for i in range(nb)])
    row_blk = _iota2(c, 1, 0) // SUB
    masks = [(row_blk == i).astype(F32) for i in range(nb)]
    r_all = _nt(q_sc, k_sc_all)
    a_mat = r_all[:, 0:c] * masks[0]
    for i in range(1, nb):
        a_mat = a_mat + r_all[:, i * c:(i + 1) * c] * masks[i]
    a_mat = a_mat * _tri(c, "lower")
    return dict(e_g=e_g, qd=qd, e_t=e_t, kt=kt, tail=tail, q_sc=q_sc, k_sc_all=k_sc_all, e_q=e_q, e_k=e_k, masks=masks,
                a_mat=a_mat)


def _hg_chunk_fwd_math(q, k, v, g, stt):
    m = _hg_chunk_common(q, k, g)
    o = _nt(m["qd"], stt) + _nn(m["a_mat"], v)
    stt2 = stt * m["tail"] + _tn(v, m["kt"])
    return o, stt2


def _hg_chunk_bwd_math(q, k, v, g, stt, do, dstt2):
    c, nb = CHUNK, CHUNK // SUB
    m = _hg_chunk_common(q, k, g)
    stt2 = stt * m["tail"] + _tn(v, m["kt"])
    later = _sum(stt2 * dstt2, 0)
    dqd = _dot3(do, stt, 1, 0)
    dstt = _tn(do, m["qd"]) + dstt2 * m["tail"]
    d_a = _dot3(do, v, 1, 1) * _tri(c, "lower")
    dv = _tn(m["a_mat"], do) + _nt(m["kt"], dstt2)
    dkt = _dot3(v, dstt2, 1, 0)
    d_blk = _lanes(*[d_a * m["masks"][i] for i in range(nb)])
    dq = dqd * m["e_g"] + _dot3(d_blk, m["k_sc_all"], 1, 0) * m["e_q"]
    dks = _dot3(d_blk, m["q_sc"], 0, 0)
    dk = dkt * m["e_t"]
    for i in range(nb):
        dk = dk + dks[i * c:(i + 1) * c, :] * m["e_k"][i]
    db = q * dq - k * dk
    return dq, dk, dv, db, later, dstt


def _hg_chunk_fwd(qh, kh, proj, lf, *, name):
    s = qh.shape[0]
    n = s // CHUNK
    vb = C_HI // BR_WIDTH

    def body(q_ref, k_ref, v_ref, lf_ref, o_ref, st_out_ref, st_ref):
        @pl.when(pl.program_id(0) == 0)
        def _():
            st_ref[...] = jnp.zeros_like(st_ref)

        st = _Heads(st_ref[h] for h in range(N_HEADS))
        g_all = _nn_exact(_tri(CHUNK, "lower"), lf_ref[...])
        g = _Heads(g_all[:, h * HEAD_DIM:(h + 1) * HEAD_DIM] for h in range(N_HEADS))
        o, st2 = _hg_chunk_fwd_math(_heads_of(q_ref), _heads_of(k_ref), _heads_of(v_ref), g, st)
        for h in range(N_HEADS):
            st_out_ref[0, h] = st.v[h]
            o_ref[:, h * HEAD_DIM:(h + 1) * HEAD_DIM] = o.v[h]
            st_ref[h] = st2.v[h]

    blk = lambda off: pl.BlockSpec((CHUNK, BR_WIDTH), lambda c: (c, off))
    return pl.pallas_call(
        body, name=name, grid=(n,), in_specs=[blk(0), blk(0), blk(vb), blk(0)],
        out_specs=[blk(0), pl.BlockSpec((1, N_HEADS, HEAD_DIM, HEAD_DIM), lambda c: (c, 0, 0, 0))],
        out_shape=[jax.ShapeDtypeStruct((s, BR_WIDTH), F32), jax.ShapeDtypeStruct((n, N_HEADS, HEAD_DIM, HEAD_DIM), F32)],
        scratch_shapes=[pltpu.VMEM((N_HEADS, HEAD_DIM, HEAD_DIM), F32)],
        compiler_params=_cp("arbitrary"),
    )(qh, kh, proj, lf)


def _hg_chunk_bwd(qh, kh, proj, lf, states, do, *, name):
    s = qh.shape[0]
    n = s // CHUNK
    vb = C_HI // BR_WIDTH

    def body(q_ref, k_ref, v_ref, lf_ref, st_in_ref, do_ref, dq_ref, dk_ref, dv_ref, dlf_ref, dst_ref):
        @pl.when(pl.program_id(0) == 0)
        def _():
            dst_ref[...] = jnp.zeros_like(dst_ref)

        g_all = _nn_exact(_tri(CHUNK, "lower"), lf_ref[...])
        g = _Heads(g_all[:, h * HEAD_DIM:(h + 1) * HEAD_DIM] for h in range(N_HEADS))
        dq, dk, dv, db, later, dst = _hg_chunk_bwd_math(
            _heads_of(q_ref), _heads_of(k_ref), _heads_of(v_ref), g,
            _Heads(st_in_ref[0, h] for h in range(N_HEADS)), _heads_of(do_ref),
            _Heads(dst_ref[h] for h in range(N_HEADS)))
        dlf_ref[...] = (_nn_exact(_tri(CHUNK, "upper"), jnp.concatenate(db.v, axis=1))
                        + jnp.concatenate(later.v, axis=1))
        for h in range(N_HEADS):
            cols = slice(h * HEAD_DIM, (h + 1) * HEAD_DIM)
            dq_ref[:, cols] = dq.v[h]
            dk_ref[:, cols] = dk.v[h]
            dv_ref[:, cols] = dv.v[h].astype(BF16)
            dst_ref[h] = dst.v[h]

    blk = lambda off: pl.BlockSpec((CHUNK, BR_WIDTH), lambda c: (n - 1 - c, off))
    return pl.pallas_call(
        body, name=name, grid=(n,),
        in_specs=[blk(0), blk(0), blk(vb), blk(0),
                  pl.BlockSpec((1, N_HEADS, HEAD_DIM, HEAD_DIM), lambda c: (n - 1 - c, 0, 0, 0)), blk(0)],
        out_specs=[blk(0), blk(0), blk(0), blk(0)],
        out_shape=[jax.ShapeDtypeStruct((s, BR_WIDTH), F32)] * 2 + [jax.ShapeDtypeStruct((s, BR_WIDTH), BF16),
                                                                    jax.ShapeDtypeStruct((s, BR_WIDTH), F32)],
        scratch_shapes=[pltpu.VMEM((N_HEADS, HEAD_DIM, HEAD_DIM), F32)],
        compiler_params=_cp("arbitrary"),
    )(qh, kh, proj, lf, states, do)


_ANY = pl.BlockSpec(memory_space=pl.ANY)
_MESH = pl.DeviceIdType.MESH


def _all_gather(x_local, *, name, after=()):
    n_after = len(after)

    def body(x_ref, *refs):
        out_ref, send_sems, recv_sems, local_sem = refs[n_after:]
        x, y, c = lax.axis_index("x"), lax.axis_index("y"), lax.axis_index("c")
        me, sibling = (x, y, c), (x, y, 1 - c)
        n1 = (x ^ (1 - c), y ^ c)
        n2 = (x ^ c, y ^ (1 - c))
        dg = (1 - x, 1 - y)

        def slot(px, py, pc):
            return out_ref.at[4 * px + 2 * py + pc]

        def copy(k, block, to, src=None):
            return pltpu.make_async_remote_copy(
                src_ref=slot(*block) if src is None else src, dst_ref=slot(*block),
                send_sem=send_sems.at[k], recv_sem=recv_sems.at[k], device_id=to, device_id_type=_MESH)

        mine = pltpu.make_async_copy(x_ref, slot(*me), local_sem)
        mine.start()
        first = [copy(0, me, sibling, src=x_ref), copy(1, me, (*n1, c), src=x_ref), copy(2, me, (*n2, c), src=x_ref)]
        for cp in first:
            cp.start()
        copy(2, (*n2, c), me).wait_recv()
        forward = copy(3, (*n2, c), (*n1, c))
        forward.start()
        passed = [copy(5, (*n2, c), sibling)]
        passed[0].start()
        copy(1, (*n1, c), me).wait_recv()
        passed.append(copy(4, (*n1, c), sibling))
        passed[1].start()
        copy(3, (*dg, c), me).wait_recv()
        passed.append(copy(6, (*dg, c), sibling))
        passed[2].start()
        copy(0, sibling, me).wait_recv()
        copy(4, (*n2, 1 - c), me).wait_recv()
        copy(5, (*n1, 1 - c), me).wait_recv()
        copy(6, (*dg, 1 - c), me).wait_recv()
        for cp in first + [forward] + passed:
            cp.wait_send()
        mine.wait()

    return pl.pallas_call(
        body, name=name, out_shape=jax.ShapeDtypeStruct((N_DEV,) + x_local.shape, x_local.dtype),
        in_specs=[_ANY] * (1 + n_after), out_specs=_ANY,
        scratch_shapes=[pltpu.SemaphoreType.DMA((7,)), pltpu.SemaphoreType.DMA((7,)), pltpu.SemaphoreType.DMA],
    )(x_local, *after)


_HBM = pl.BlockSpec(memory_space=pltpu.HBM)
_SEM = pl.BlockSpec(memory_space=pltpu.SEMAPHORE)
_EFFECT = pltpu.SideEffectType.DATAFLOW_SIDE_EFFECTING


def _peers():
    x, y, c = lax.axis_index("x"), lax.axis_index("y"), lax.axis_index("c")
    out = []
    for k in range(1, N_DEV):
        px, py, pc = x ^ ((k >> 2) & 1), y ^ ((k >> 1) & 1), c ^ (k & 1)
        out.append(((px, py, pc), 4 * px + 2 * py + pc))
    return 4 * x + 2 * y + c, out


def _push_copies(src_ref, land_ref, send_sems, recv_sems, broadcast):
    my, peers = _peers()
    pairs = []
    for k, (pos, idx) in enumerate(peers):
        src = src_ref if broadcast else src_ref.at[idx]
        send = pltpu.make_async_remote_copy(src_ref=src, dst_ref=land_ref.at[my], send_sem=send_sems.at[k],
                                            recv_sem=recv_sems.at[k], device_id=pos, device_id_type=_MESH)
        recv = pltpu.make_async_remote_copy(src_ref=src, dst_ref=land_ref.at[idx], send_sem=send_sems.at[k],
                                            recv_sem=recv_sems.at[k], device_id=pos, device_id_type=_MESH)
        pairs.append((send, recv))
    return pairs


def _push_start(src, land, *, broadcast, name, after=()):
    n_after = len(after)

    def body(src_ref, land_ref, *refs):
        send_sems, recv_sems, _, _, token = refs[n_after:]
        for send, _ in _push_copies(src_ref, land_ref, send_sems, recv_sems, broadcast):
            send.start()
        token[...] = jnp.zeros_like(token)

    return pl.pallas_call(
        body, name=name,
        out_shape=(pltpu.SemaphoreType.DMA((N_DEV - 1,)), pltpu.SemaphoreType.DMA((N_DEV - 1,)),
                   pltpu.HBM(src.shape, src.dtype), pltpu.HBM(land.shape, land.dtype), jax.ShapeDtypeStruct((8, 128), F32)),
        in_specs=(_HBM, _HBM) + (_ANY,) * n_after, out_specs=(_SEM, _SEM, _HBM, _HBM, pl.BlockSpec(memory_space=pltpu.VMEM)),
        input_output_aliases={0: 2, 1: 3}, compiler_params=pltpu.CompilerParams(has_side_effects=_EFFECT),
    )(pltpu.with_memory_space_constraint(src, pltpu.HBM), pltpu.with_memory_space_constraint(land, pltpu.HBM), *after)


def _push_wait(handle, after, *, broadcast, name):
    send_sems, recv_sems, src_thru, land_thru, _ = handle

    def body(src_ref, land_ref, send_sems, recv_sems, *rest):
        for send, recv in _push_copies(src_ref, land_ref, send_sems, recv_sems, broadcast):
            send.wait_send()
            recv.wait_recv()

    return pl.pallas_call(
        body, name=name,
        out_shape=(pltpu.HBM(src_thru.shape, src_thru.dtype), pltpu.HBM(land_thru.shape, land_thru.dtype)),
        in_specs=(_HBM, _HBM, _SEM, _SEM) + (_ANY,) * len(after), out_specs=(_HBM, _HBM),
        input_output_aliases={0: 0, 1: 1}, compiler_params=pltpu.CompilerParams(has_side_effects=_EFFECT),
    )(src_thru, land_thru, send_sems, recv_sems, *after)[1]


def _relay_copies(src_ref, land_ref, sems_a, sems_b):
    x, y, c = lax.axis_index("x"), lax.axis_index("y"), lax.axis_index("c")
    slot = lambda px, py, pc: land_ref.at[4 * px + 2 * py + pc]
    chips = [(1 - x, y), (x, 1 - y), (1 - x, 1 - y)]
    (send_a, recv_a), (send_b, recv_b) = sems_a, sems_b

    def copy(sems, k, src, dst_slot, to):
        return pltpu.make_async_remote_copy(src_ref=src, dst_ref=dst_slot, send_sem=sems[0].at[k], recv_sem=sems[1].at[k],
                                            device_id=to, device_id_type=_MESH)

    first = [copy((send_a, recv_a), 0, src_ref, slot(x, y, c), (x, y, 1 - c))]
    first += [copy((send_a, recv_a), 1 + j, src_ref, slot(x, y, c), (*chip, c)) for j, chip in enumerate(chips)]
    first_in = [copy((send_a, recv_a), 0, src_ref, slot(x, y, 1 - c), (x, y, 1 - c))]
    first_in += [copy((send_a, recv_a), 1 + j, src_ref, slot(*chip, c), (*chip, c)) for j, chip in enumerate(chips)]
    relay = [copy((send_b, recv_b), j, slot(*chip, c), slot(*chip, c), (x, y, 1 - c)) for j, chip in enumerate(chips)]
    relay_in = [copy((send_b, recv_b), j, slot(*chip, 1 - c), slot(*chip, 1 - c), (x, y, 1 - c)) for j, chip in enumerate(chips)]
    return first, first_in, relay, relay_in


def _relay_start(src, land, *, name, after=()):
    n_after = len(after)

    def body(src_ref, land_ref, *refs):
        send_a, recv_a, _, _, token = refs[n_after:]
        for cp in _relay_copies(src_ref, land_ref, (send_a, recv_a), (send_a, recv_a))[0]:
            cp.start()
        token[...] = jnp.zeros_like(token)

    send_a, recv_a, src_thru, land_thru, token = pl.pallas_call(
        body, name=name,
        out_shape=(pltpu.SemaphoreType.DMA((4,)), pltpu.SemaphoreType.DMA((4,)), pltpu.HBM(src.shape, src.dtype),
                   pltpu.HBM(land.shape, land.dtype), jax.ShapeDtypeStruct((8, 128), F32)),
        in_specs=(_HBM, _HBM) + (_ANY,) * n_after, out_specs=(_SEM, _SEM, _HBM, _HBM, pl.BlockSpec(memory_space=pltpu.VMEM)),
        input_output_aliases={0: 2, 1: 3}, compiler_params=pltpu.CompilerParams(has_side_effects=_EFFECT),
    )(pltpu.with_memory_space_constraint(src, pltpu.HBM), pltpu.with_memory_space_constraint(land, pltpu.HBM), *after)
    return (send_a, recv_a), src_thru, land_thru, token


def _relay_mid(handle, after, *, name):
    sems_a, src_thru, land_thru, _ = handle
    n_after = len(after)

    def body(src_ref, land_ref, send_a, recv_a, *refs):
        send_b, recv_b, _, _, token = refs[n_after:]
        _, first_in, relay, _ = _relay_copies(src_ref, land_ref, (send_a, recv_a), (send_b, recv_b))
        for j in range(3):
            first_in[1 + j].wait_recv()
            relay[j].start()
        token[...] = jnp.zeros_like(token)

    send_b, recv_b, src2, land2, token = pl.pallas_call(
        body, name=name,
        out_shape=(pltpu.SemaphoreType.DMA((3,)), pltpu.SemaphoreType.DMA((3,)), pltpu.HBM(src_thru.shape, src_thru.dtype),
                   pltpu.HBM(land_thru.shape, land_thru.dtype), jax.ShapeDtypeStruct((8, 128), F32)),
        in_specs=(_HBM, _HBM, _SEM, _SEM) + (_ANY,) * n_after,
        out_specs=(_SEM, _SEM, _HBM, _HBM, pl.BlockSpec(memory_space=pltpu.VMEM)),
        input_output_aliases={0: 2, 1: 3}, compiler_params=pltpu.CompilerParams(has_side_effects=_EFFECT),
    )(src_thru, land_thru, *sems_a, *after)
    return sems_a, (send_b, recv_b), src2, land2, token


def _relay_wait(handle, after, *, name):
    sems_a, sems_b, src_thru, land_thru, _ = handle

    def body(src_ref, land_ref, send_a, recv_a, send_b, recv_b, *rest):
        first, first_in, relay, relay_in = _relay_copies(src_ref, land_ref, (send_a, recv_a), (send_b, recv_b))
        first_in[0].wait_recv()
        for cp in relay_in:
            cp.wait_recv()
        for cp in first + relay:
            cp.wait_send()

    return pl.pallas_call(
        body, name=name,
        out_shape=(pltpu.HBM(src_thru.shape, src_thru.dtype), pltpu.HBM(land_thru.shape, land_thru.dtype)),
        in_specs=(_HBM, _HBM, _SEM, _SEM, _SEM, _SEM) + (_ANY,) * len(after), out_specs=(_HBM, _HBM),
        input_output_aliases={0: 0, 1: 1}, compiler_params=pltpu.CompilerParams(has_side_effects=_EFFECT),
    )(src_thru, land_thru, *sems_a, *sems_b, *after)[1]


def _adamw(parts, row_off, w, m, v, *, layer=0, n_layers=1, prev=None, name, tr):
    rows, c = w.shape
    r = rows // n_layers
    np_ = parts.shape[0]
    tr = min(tr, r)
    assert r % tr == 0 and row_off % tr == 0
    ob, lb = row_off // tr, layer * (r // tr)
    c1 = 1.0 - ADAM_B1 ** ADAM_STEP
    c2 = 1.0 - ADAM_B2 ** ADAM_STEP
    n_prev = 0 if prev is None else 4

    def body(p_ref, w_ref, m_ref, v_ref, *refs):
        g_ref, d_ref, nm_ref, nv_ref = refs[n_prev:]
        g = p_ref[0].astype(F32)
        for s in range(1, np_):
            g = g + p_ref[s].astype(F32)
        wv = w_ref[...]
        m2 = ADAM_B1 * m_ref[...] + (1.0 - ADAM_B1) * g
        v2 = ADAM_B2 * v_ref[...] + (1.0 - ADAM_B2) * jnp.square(g)
        m_hat = m2 / c1
        v_hat = v2 / c2
        g_ref[...] = g
        d_ref[...] = -ADAM_LR * (m_hat / (jnp.sqrt(v_hat) + ADAM_EPS) + ADAM_WD * wv)
        nm_ref[...] = m2
        nv_ref[...] = v2

    blk = pl.BlockSpec((tr, c), lambda i: (lb + i, 0))
    return pl.pallas_call(
        body, name=name, grid=(r // tr,),
        in_specs=[pl.BlockSpec((np_, tr, c), lambda i: (0, ob + i, 0)), blk, blk, blk] + [_ANY] * n_prev,
        out_specs=[blk] * 4, out_shape=[jax.ShapeDtypeStruct((rows, c), F32)] * 4,
        input_output_aliases={4 + i: i for i in range(n_prev)}, compiler_params=_cp("parallel"),
    )(parts, w, m, v, *(prev or ()))


def _sum_parts(parts, *, name, after=()):
    np_, r, c = parts.shape

    def body(p_ref, *refs):
        o_ref = refs[-1]
        g = p_ref[0]
        for s in range(1, np_):
            g = g + p_ref[s]
        o_ref[...] = g

    vmem = pl.BlockSpec(memory_space=pltpu.VMEM)
    return pl.pallas_call(body, name=name, in_specs=[vmem] + [_ANY] * len(after), out_specs=vmem,
                          out_shape=jax.ShapeDtypeStruct((r, c), F32))(parts, *after)


def _pack(arrs):
    rows = []
    for a in arrs:
        f = a.reshape(-1).astype(F32)
        pad = (-f.shape[0]) % 128
        rows.append(jnp.pad(f, (0, pad)).reshape(-1, 128))
    out = jnp.concatenate(rows, axis=0)
    return jnp.pad(out, ((0, (-out.shape[0]) % 8), (0, 0)))


def _unpack(packed, shapes):
    outs, r0 = [], 0
    for shp in shapes:
        n = 1
        for d in shp:
            n *= d
        nr = -(-n // 128)
        outs.append(packed[r0:r0 + nr].reshape(-1)[:n].reshape(shp))
        r0 += nr
    return outs


_WIN_PIECES = ((0, 4096, 0), (4112, 8208, 0), (4096, 4104, HEAD_DIM - N_HEADS), (4104, 4112, HEAD_DIM - N_HEADS))


RELAYOUT_TILE = 256
LAST_SPLIT = 4


def _win_from_shards(shards, *, name):
    k = shards.shape[1]
    tr = min(RELAYOUT_TILE, k)

    def body(x_ref, o_ref):
        cols = []
        for lo, hi, pad in _WIN_PIECES:
            for j in range(N_DEV):
                a, b = max(lo, j * SHARD_IN), min(hi, (j + 1) * SHARD_IN)
                if a < b:
                    cols.append(x_ref[j, :, a - j * SHARD_IN:b - j * SHARD_IN])
            if pad:
                cols.append(jnp.zeros((tr, pad), x_ref.dtype))
        o_ref[...] = jnp.concatenate(cols, axis=1)

    return pl.pallas_call(
        body, name=name, grid=(k // tr,), in_specs=[pl.BlockSpec((N_DEV, tr, SHARD_IN), lambda i: (0, i, 0))],
        out_specs=pl.BlockSpec((tr, N_PROJ), lambda i: (i, 0)), out_shape=jax.ShapeDtypeStruct((k, N_PROJ), shards.dtype),
        compiler_params=_cp("parallel"),
    )(shards)


def _win_to_shards(g, *, name):
    k = g.shape[0]
    tr = min(RELAYOUT_TILE, k)
    starts, off = [], 0
    for lo, hi, pad in _WIN_PIECES:
        starts.append((lo, hi, off))
        off += hi - lo + pad

    def body(g_ref, o_ref):
        for j in range(N_DEV):
            cols = []
            for lo, hi, off in sorted(starts):
                a, b = max(lo, j * SHARD_IN), min(hi, (j + 1) * SHARD_IN)
                if a < b:
                    cols.append(g_ref[:, off + a - lo:off + b - lo])
            o_ref[j] = jnp.concatenate(cols, axis=1)

    return pl.pallas_call(
        body, name=name, grid=(k // tr,), in_specs=[pl.BlockSpec((tr, N_PROJ), lambda i: (i, 0))],
        out_specs=pl.BlockSpec((N_DEV, tr, SHARD_IN), lambda i: (0, i, 0)),
        out_shape=jax.ShapeDtypeStruct((N_DEV, k, SHARD_IN), g.dtype), compiler_params=_cp("parallel"),
    )(g)


def _lower_bounds(logits):
    probs = jax.nn.softmax(logits.astype(F32), axis=0)
    return jnp.cumsum(probs, axis=0) - probs[0]


def _pad_lanes(vec8):
    return jnp.pad(vec8.reshape(1, N_HEADS), ((0, 0), (0, HEAD_DIM - N_HEADS)))


def kernel(x, p, norm_w, w_in, dn_conv_w, dn_A_log, dn_dt_bias, dn_norm_w, hg_lb_logits, hg_norm_w, w_out, w_ple_up, w_ple_gate, final_norm_w, loss_target, m_norm_w, m_w_in, m_dn_conv_w, m_dn_A_log, m_dn_dt_bias, m_dn_norm_w, m_hg_lb_logits, m_hg_norm_w, m_w_out, m_w_ple_up, m_w_ple_gate, m_final_norm_w, v_norm_w, v_w_in, v_dn_conv_w, v_dn_A_log, v_dn_dt_bias, v_dn_norm_w, v_hg_lb_logits, v_hg_norm_w, v_w_out, v_w_ple_up, v_w_ple_gate, v_final_norm_w):
    depth = norm_w.shape[0]
    my = 4 * lax.axis_index("x") + 2 * lax.axis_index("y") + lax.axis_index("c")
    h = x[0]
    tgt = loss_target[0]
    rows_out = D_MODEL // N_DEV
    up_rows = PLE_DIM * (D_MODEL // N_DEV) // D_MODEL
    g_off, u_off = rows_out, 2 * rows_out

    def own_slot(block):
        return lax.dynamic_update_index_in_dim(lax.empty((N_DEV,) + block.shape, block.dtype), block, my, 0)

    win_bf = w_in.astype(BF16)
    rest_bf = [jnp.concatenate([w_out[l], w_ple_gate[l], w_ple_up[l].reshape(up_rows, D_MODEL)], axis=0).astype(BF16)
               for l in range(depth)]
    conv_all = _all_gather(dn_conv_w, name="gather_conv_w")
    conv_full = conv_all.transpose(1, 2, 0, 3).reshape(depth, CONV_W, 3 * BR_WIDTH)
    win_all = {0: _all_gather(win_bf[0], name="gather_w_in_l0", after=[conv_all])}
    pending, relayed = {}, {}
    last = win_all[0]
    for l in range(depth):
        if l > 0:
            relayed["win", l] = _relay_start(win_bf[l], own_slot(win_bf[l]), after=[last], name=f"gather_w_in_l{l}_first")
            last = relayed["win", l][3]
        if l == 0:
            relayed["rest", l] = _relay_start(rest_bf[l], own_slot(rest_bf[l]), after=[last], name=f"gather_rest_l{l}_first")
            last = relayed["rest", l][3]
        else:
            pending["rest", l] = _push_start(rest_bf[l], own_slot(rest_bf[l]), broadcast=True, after=[last],
                                             name=f"gather_rest_l{l}_start")
            last = pending["rest", l][4]
    order_tok = last[0, 0]
    lbs = _lower_bounds(hg_lb_logits)

    saved = []
    weights = []
    for l in range(depth):
        tag = f"l{l}"
        if l > 0:
            win_all[l] = _relay_wait(relayed["win", l], [h], name=f"gather_w_in_{tag}_wait")
        wi = _win_from_shards(win_all[l], name=f"w_in_layout_{tag}")
        nw = norm_w[l] + order_tok if l == 0 else norm_w[l]
        hn = _rms_fwd(h, nw, name=f"rms_fwd_{tag}")
        proj = _mm(hn, wi, mode="nn", out_dtype=F32, name=f"mm_proj_{tag}")
        al, dt = _pad_lanes(dn_A_log[l]), _pad_lanes(dn_dt_bias[l])
        qkv = _dn_qkv_fwd(proj, conv_full[l], name=f"dn_qkv_fwd_{tag}")
        if ("rest", l) in relayed:
            relayed["rest", l] = _relay_mid(relayed["rest", l], [qkv], name=f"gather_rest_{tag}_relay")
            al = al + relayed["rest", l][4][0, 0]
        beta, gcs = _dn_gate_fwd(proj, al, dt, name=f"dn_gate_fwd_{tag}")
        o_dn, st_dn, tinv_dn = _dn_chunk_fwd(qkv, gcs, beta, name=f"dn_chunk_fwd_{tag}")
        lb = lbs[l].reshape(1, BR_WIDTH)
        qh, kh, lf = _hg_prep_fwd(proj, lb, name=f"hg_prep_fwd_{tag}")
        o_hg, st_hg = _hg_chunk_fwd(qh, kh, proj, lf, name=f"hg_chunk_fwd_{tag}")
        y_dn = _hnorm_fwd(o_dn, proj, C_Z, dn_norm_w[l], name=f"hnorm_dn_fwd_{tag}")
        y_hg = _hnorm_fwd(o_hg, proj, C_HZ, hg_norm_w[l], name=f"hnorm_hg_fwd_{tag}")
        y = jnp.concatenate([y_dn, y_hg], axis=1)
        if ("rest", l) in relayed:
            rest_all = _relay_wait(relayed["rest", l], [y], name=f"gather_rest_{tag}_wait")
        else:
            rest_all = _push_wait(pending["rest", l], [y], broadcast=True, name=f"gather_rest_{tag}_wait")
        wo = rest_all[:, 0:rows_out].reshape(D_MODEL, D_MODEL)
        wg = rest_all[:, g_off:g_off + rows_out].reshape(D_MODEL, D_MODEL)
        wu = rest_all[:, u_off:u_off + up_rows].reshape(N_DEV, PLE_DIM, D_MODEL // N_DEV).transpose(1, 0, 2).reshape(PLE_DIM, D_MODEL)
        weights.append((wi, wo, wg, wu))
        h1 = _mm(y, wo, mode="nn", out_dtype=F32, res=h, name=f"mm_out_{tag}")
        pin = []
        if ("win", l + 1) in relayed:
            relayed["win", l + 1] = _relay_mid(relayed["win", l + 1], [h1], name=f"gather_w_in_l{l + 1}_relay")
            pin = [relayed["win", l + 1][4]]
        gp = _mm(h1, wg, mode="nn", out_dtype=F32, after=pin, name=f"mm_gate_{tag}")
        up = _mm(p[l, 0], wu, mode="nn", out_dtype=F32, name=f"mm_up_{tag}")
        h2 = _ple_fwd(h1, gp, up, name=f"ple_fwd_{tag}")
        saved.append(dict(h=h, hn=hn, proj=proj, qkv=qkv, beta=beta, gcs=gcs, st_dn=st_dn, tinv_dn=tinv_dn, qh=qh, kh=kh, lf=lf,
                          st_hg=st_hg, o_dn=o_dn, o_hg=o_hg, y=y, h1=h1, gp=gp, up=up, al=al, dt=dt, lb=lb))
        h = h2

    loss_row, dh, d_final_w = _final_fwd_bwd(h, final_norm_w, tgt, name="final_norm_loss")

    d_norm_w, d_alog, d_dt, d_dn_nw, d_hg_nw, d_lb, d_conv = ([None] * depth for _ in range(7))
    sent = {}
    for l in reversed(range(depth)):
        wi, wo, wg, wu = weights[l]
        sv = saved[l]
        tag = f"l{l}"
        dup, dgp = _ple_bwd(dh, sv["gp"], sv["up"], name=f"ple_bwd_{tag}")
        d_wu = _mm(p[l, 0], dup, mode="tn", out_dtype=BF16, name=f"mm_dwup_{tag}")
        d_wg = _mm(sv["h1"], dgp, mode="tn", out_dtype=BF16, name=f"mm_dwgate_{tag}")
        dh1 = _mm(dgp, wg, mode="nt", out_dtype=F32, res=dh, name=f"mm_dh1_{tag}")
        d_wo = _mm(sv["y"], dh1, mode="tn", out_dtype=BF16, name=f"mm_dwout_{tag}")
        parts_rest = jnp.concatenate(
            [d_wo.reshape(N_DEV, rows_out, D_MODEL), d_wg.reshape(N_DEV, rows_out, D_MODEL),
             d_wu.reshape(PLE_DIM, N_DEV, D_MODEL // N_DEV).transpose(1, 0, 2).reshape(N_DEV, up_rows, D_MODEL)], axis=1)
        sent["rest", l] = _push_start(parts_rest, own_slot(parts_rest[my]), broadcast=False, name=f"exchange_rest_{tag}_start")
        dy = _mm(dh1, wo, mode="nt", out_dtype=F32, name=f"mm_dy_{tag}")
        dn_nw = dn_norm_w[l] + sent["rest", l][4][0, 0]
        do_dn, dz_dn, d_dn_nw[l] = _hnorm_bwd(sv["o_dn"], sv["proj"], C_Z, dn_nw, dy, 0, name=f"hnorm_dn_bwd_{tag}")
        do_hg, dz_hg, d_hg_nw[l] = _hnorm_bwd(sv["o_hg"], sv["proj"], C_HZ, hg_norm_w[l], dy, BR_WIDTH, name=f"hnorm_hg_bwd_{tag}")
        dqkv, d_gc, dbeta = _dn_chunk_bwd(sv["qkv"], sv["gcs"], sv["beta"], sv["st_dn"], sv["tinv_dn"], do_dn, name=f"dn_chunk_bwd_{tag}")
        dqkv_pre, d_conv[l] = _dn_qkv_bwd(sv["proj"], conv_full[l], dqkv, name=f"dn_qkv_bwd_{tag}")
        db, da, d_alog[l], d_dt[l] = _dn_gate_bwd(sv["proj"], sv["al"], sv["dt"], dbeta, d_gc, name=f"dn_gate_bwd_{tag}")
        dqh, dkh, dhi, dlf = _hg_chunk_bwd(sv["qh"], sv["kh"], sv["proj"], sv["lf"], sv["st_hg"], do_hg, name=f"hg_chunk_bwd_{tag}")
        dhq, dhf, d_lb[l] = _hg_prep_bwd(sv["proj"], sv["lb"], dqh, dkh, dlf, name=f"hg_prep_bwd_{tag}")
        dproj = jnp.concatenate([dqkv_pre, dz_dn, dhq, dhf, dhi, dz_hg, db, da], axis=1)
        def push_d_win(after):
            n_split = LAST_SPLIT if l == 0 else 1
            rows = D_MODEL // n_split
            handles = []
            for q in range(n_split):
                hn_q = sv["hn"] if n_split == 1 else sv["hn"][:, q * rows:(q + 1) * rows]
                sfx = tag if n_split == 1 else f"{tag}_{q}"
                d_win = _mm(hn_q, dproj, mode="tn", out_dtype=BF16, after=after, name=f"mm_dwin_{sfx}")
                parts_in = _win_to_shards(d_win, name=f"dw_in_shards_{sfx}")
                handles.append(_push_start(parts_in, own_slot(parts_in[my]), broadcast=False, after=after,
                                           name=f"exchange_w_in_{sfx}_start"))
                after = [handles[-1][4]]
            return handles

        if l == 0:
            small = _pack([loss_row, jnp.concatenate(d_norm_w[1:], axis=0), d_final_w,
                           jnp.stack([a[0, :N_HEADS] for a in d_alog]), jnp.stack([a[0, :N_HEADS] for a in d_dt]),
                           jnp.concatenate(d_dn_nw, axis=0), jnp.concatenate(d_hg_nw, axis=0), jnp.concatenate(d_lb, axis=0),
                           jnp.stack(d_conv)])
            small_all = _all_gather(small, name="gather_small")
        sent["win", l] = push_d_win([small_all] if l == 0 else [])
        dhn = _mm(dproj, wi, mode="nt", out_dtype=F32, after=[sent["win", l][-1][4]], name=f"mm_dhn_{tag}")
        dh, d_norm_w[l] = _rms_bwd(sv["h"], norm_w[l], dhn, dh1, name=f"rms_bwd_{tag}")
    grad_x = dh[None]

    small_shapes = [(1, 128), (depth - 1, D_MODEL), final_norm_w.shape, dn_A_log.shape, dn_dt_bias.shape, dn_norm_w.shape,
                    hg_norm_w.shape, hg_lb_logits.shape, (depth, CONV_W, 3 * BR_WIDTH)]
    tot = _unpack(_sum_parts(small_all, after=[grad_x], name="sum_small"), small_shapes)
    loss = tot[0][0, 0]
    g_lb = tot[7]
    g_logits = jax.vjp(_lower_bounds, hg_lb_logits)[1](g_lb)[0]
    g_conv = lax.dynamic_slice_in_dim(tot[8], my * (3 * BR_WIDTH // N_DEV), 3 * BR_WIDTH // N_DEV, axis=2)
    small_g = [g_conv, tot[3], tot[4], tot[5], g_logits, tot[6], tot[2]]
    small_w = [dn_conv_w, dn_A_log, dn_dt_bias, dn_norm_w, hg_lb_logits, hg_norm_w, final_norm_w]
    small_m = [m_dn_conv_w, m_dn_A_log, m_dn_dt_bias, m_dn_norm_w, m_hg_lb_logits, m_hg_norm_w, m_final_norm_w]
    small_v = [v_dn_conv_w, v_dn_A_log, v_dn_dt_bias, v_dn_norm_w, v_hg_lb_logits, v_hg_norm_w, v_final_norm_w]
    pk_w = _pack(small_w)
    res_small = _adamw(_pack(small_g)[None], 0, pk_w, _pack(small_m), _pack(small_v), name="adamw_small", tr=pk_w.shape[0])
    shapes_w = [a.shape for a in small_w]
    sg, sd, sm, sv_ = (_unpack(r, shapes_w) for r in res_small)

    r_win = r_wo = r_wg = r_wu = None
    done = [grad_x, res_small[0]]

    def flat(a, cols):
        return a.reshape(-1, cols)

    for l in reversed(range(depth)):
        tag = f"l{l}"
        land_rest = _push_wait(sent["rest", l], done, broadcast=False, name=f"exchange_rest_{tag}_wait")
        r_wo = _adamw(land_rest, 0, flat(w_out, D_MODEL), flat(m_w_out, D_MODEL), flat(v_w_out, D_MODEL), layer=l,
                      n_layers=depth, prev=r_wo, name=f"adamw_w_out_{tag}", tr=rows_out)
        r_wg = _adamw(land_rest, g_off, flat(w_ple_gate, D_MODEL), flat(m_w_ple_gate, D_MODEL), flat(v_w_ple_gate, D_MODEL),
                      layer=l, n_layers=depth, prev=r_wg, name=f"adamw_w_gate_{tag}", tr=rows_out)
        r_wu = _adamw(land_rest, u_off, flat(w_ple_up, D_MODEL), flat(m_w_ple_up, D_MODEL), flat(v_w_ple_up, D_MODEL),
                      layer=l, n_layers=depth, prev=r_wu, name=f"adamw_w_up_{tag}", tr=up_rows)
        done = [r_wo[0], r_wg[0], r_wu[0]]
    for l in reversed(range(depth)):
        tag = f"l{l}"
        if l == 0:
            nw0 = _sum_parts(_all_gather(_pack([d_norm_w[0]]), after=done, name="gather_norm_w"), name="sum_norm_w")
            g_norm_w = jnp.concatenate([_unpack(nw0, [(1, D_MODEL)])[0], tot[1]], axis=0)
            pk_nw = _pack([norm_w])
            r_nw = _adamw(_pack([g_norm_w])[None], 0, pk_nw, _pack([m_norm_w]), _pack([v_norm_w]), name="adamw_norm_w",
                          tr=pk_nw.shape[0])
            r_nw = [_unpack(r, [norm_w.shape])[0] for r in r_nw]
            done = [r_nw[0]]
        n_split = len(sent["win", l])
        for q, handle in enumerate(sent["win", l]):
            sfx = tag if n_split == 1 else f"{tag}_{q}"
            land_in = _push_wait(handle, done, broadcast=False, name=f"exchange_w_in_{sfx}_wait")
            r_win = _adamw(land_in, 0, flat(w_in, SHARD_IN), flat(m_w_in, SHARD_IN), flat(v_w_in, SHARD_IN),
                           layer=l * n_split + q, n_layers=depth * n_split, prev=r_win, name=f"adamw_w_in_{sfx}", tr=256)
            done = [r_win[0]]
    r_win = [o.reshape(w_in.shape) for o in r_win]
    r_wo = [o.reshape(w_out.shape) for o in r_wo]
    r_wg = [o.reshape(w_ple_gate.shape) for o in r_wg]
    r_wu = [o.reshape(w_ple_up.shape) for o in r_wu]

    def order(nw, small_list, big_in, big_out, big_up, big_gate):
        cw, al_, dt_, dnw, lbl, hnw, fw = small_list
        return [nw, big_in, cw, al_, dt_, dnw, lbl, hnw, big_out, big_up, big_gate, fw]

    outs = [loss, grad_x]
    for i, sl in enumerate((sg, sd, sm, sv_)):
        outs += order(r_nw[i], sl, r_win[i], r_wo[i], r_wu[i], r_wg[i])
    return tuple(outs)
```

```python
import functools

import jax
import jax.numpy as jnp
from jax import lax
from jax.experimental import pallas as pl
from jax.experimental.pallas import tpu as pltpu

F32 = jnp.float32
BF16 = jnp.bfloat16
HIGHEST = lax.Precision.HIGHEST

N_DEV = 8
D_MODEL = 2048
PLE_DIM = 256
HEAD_DIM = 128
N_HEADS = 8
BR_WIDTH = N_HEADS * HEAD_DIM
CHUNK = 64
SUB = 16
CONV_W = 4
NORM_EPS = 1e-6
L2_EPS = 1e-6
IN_WIDTH = 8208
SHARD_IN = IN_WIDTH // N_DEV
EXP_CLAMP = 80.0

C_QKV, C_Z, C_HQ, C_HF, C_HI, C_HZ, C_B, C_A, N_PROJ = 0, 3072, 4096, 5120, 6144, 7168, 8192, 8320, 8448

ADAM_LR, ADAM_B1, ADAM_B2, ADAM_EPS, ADAM_WD, ADAM_STEP = 0.001, 0.9, 0.999, 1e-08, 0.01, 10

VMEM_LIMIT = 48 * 1024 * 1024


def _cp(*sem):
    return pltpu.CompilerParams(dimension_semantics=sem, vmem_limit_bytes=VMEM_LIMIT)


class _Heads:
    def __init__(self, vals):
        self.v = tuple(vals)

    def __add__(self, o):
        return _hmap(lambda a, b: a + b, self, o)

    def __radd__(self, o):
        return _hmap(lambda a, b: b + a, self, o)

    def __sub__(self, o):
        return _hmap(lambda a, b: a - b, self, o)

    def __rsub__(self, o):
        return _hmap(lambda a, b: b - a, self, o)

    def __mul__(self, o):
        return _hmap(lambda a, b: a * b, self, o)

    def __rmul__(self, o):
        return _hmap(lambda a, b: b * a, self, o)

    def __neg__(self):
        return _hmap(lambda a: -a, self)

    def __getitem__(self, idx):
        return _hmap(lambda a: a[idx], self)


def _hmap(fn, *args):
    n = next((len(a.v) for a in args if isinstance(a, _Heads)), None)
    if n is None:
        return fn(*args)
    return _Heads(fn(*[a.v[i] if isinstance(a, _Heads) else a for a in args]) for i in range(n))


def _dot(a, b, ca, cb):
    return _hmap(lambda x, y: lax.dot_general(x.astype(BF16), y.astype(BF16), (((ca,), (cb,)), ((), ())),
                                              preferred_element_type=F32), a, b)


def _nn(a, b):
    return _dot(a, b, 1, 0)


def _nt(a, b):
    return _dot(a, b, 1, 1)


def _tn(a, b):
    return _dot(a, b, 0, 0)


def _split(a):
    hi = _hmap(lambda x: x.astype(BF16), a)
    return hi, _hmap(lambda x, h: (x - h.astype(F32)).astype(BF16), a, hi)


def _dot3(a, b, ca, cb):
    ah, al = _split(a)
    bh, bl = _split(b)
    return _dot(ah, bh, ca, cb) + (_dot(ah, bl, ca, cb) + _dot(al, bh, ca, cb))


def _nn_exact(a, b):
    return _hmap(lambda y: lax.dot_general(a, y, (((1,), (0,)), ((), ())), precision=HIGHEST,
                                           preferred_element_type=F32), b)


def _exp(x):
    return _hmap(jnp.exp, x)


def _sum(x, axis):
    return _hmap(lambda a: jnp.sum(a, axis=axis, keepdims=True), x)


def _sigmoid(x):
    return jax.nn.sigmoid(x)


def _silu(x):
    return x * _sigmoid(x)


def _dsilu(x):
    s = _sigmoid(x)
    return s * (1.0 + x * (1.0 - s))


def _silu_and_grad(x):
    s = _sigmoid(x)
    return x * s, s * (1.0 + x * (1.0 - s))


def _softplus(x):
    return jnp.maximum(x, 0.0) + jnp.log(1.0 + jnp.exp(-jnp.abs(x)))


def _iota2(n, m, axis):
    return lax.broadcasted_iota(jnp.int32, (n, m), axis)


def _col2row(col, eye):
    return _hmap(lambda c: jnp.sum(eye * c, axis=0, keepdims=True), col)


def _row2col(row, eye):
    return _hmap(lambda r: jnp.sum(eye * r, axis=1, keepdims=True), row)


def _pick_lane(block, lane_idx):
    lane = _iota2(block.shape[0], block.shape[1], 1)
    return jnp.sum(jnp.where(lane == lane_idx, block, 0.0), axis=1, keepdims=True)


MM_TILE_M, MM_TILE_N, MM_TILE_K = 1024, 1408, 2048


def _tile(dim, cap):
    if dim <= cap:
        return dim
    t = cap - cap % 128
    while dim % t:
        t -= 128
    return t


def _mm(a, b, *, mode, out_dtype, res=None, after=(), b_rows=None, tile_n=MM_TILE_N, name):
    b_mat_rows = b.shape[0] if b_rows is None else N_DEV * b_rows[1]
    if mode == "nn":
        (m, kd), n = a.shape, b.shape[-1]
        assert kd == b_mat_rows
    elif mode == "nt":
        (m, kd), n = a.shape, b_mat_rows
    else:
        (kd, m), n = a.shape, b.shape[-1]
    tm, tn, tk = _tile(m, MM_TILE_M), _tile(n, tile_n), _tile(kd, MM_TILE_K)
    assert m % tm == 0 and n % tn == 0 and kd % tk == 0, (m, n, kd, tm, tn, tk)
    nk = kd // tk
    ca, cb = {"nn": (1, 0), "nt": (1, 1), "tn": (0, 0)}[mode]

    def body(*refs):
        a_ref, b_ref = refs[:2]
        r_ref = None if res is None else refs[2]
        o_ref, acc_ref = refs[-2:]
        k = pl.program_id(2)

        @pl.when(k == 0)
        def _():
            acc_ref[...] = jnp.zeros_like(acc_ref)

        b_tile = b_ref[...]
        if b_rows is not None:
            b_tile = b_tile.reshape(-1, b_tile.shape[-1])
        acc_ref[...] += _dot(a_ref[...], b_tile, ca, cb)

        @pl.when(k == nk - 1)
        def _():
            out = acc_ref[...]
            if r_ref is not None:
                out = out + r_ref[...].astype(F32)
            o_ref[...] = out.astype(o_ref.dtype)

    a_spec = pl.BlockSpec((tk, tm), lambda i, j, k: (k, i)) if mode == "tn" else pl.BlockSpec((tm, tk), lambda i, j, k: (i, k))
    if b_rows is None:
        b_spec = pl.BlockSpec((tn, tk), lambda i, j, k: (j, k)) if mode == "nt" else pl.BlockSpec((tk, tn), lambda i, j, k: (k, j))
    else:
        first, count = b_rows
        assert first % count == 0 and mode in ("nn", "nt")
        rb = first // count
        if mode == "nn":
            assert tk == kd
            b_spec = pl.BlockSpec((N_DEV, count, tn), lambda i, j, k: (0, rb, j))
        else:
            assert tn % count == 0
            b_spec = pl.BlockSpec((tn // count, count, tk), lambda i, j, k: (j, rb, k))
    o_spec = pl.BlockSpec((tm, tn), lambda i, j, k: (i, j))
    in_specs = [a_spec, b_spec] + ([o_spec] if res is not None else []) + [pl.BlockSpec(memory_space=pl.ANY)] * len(after)
    args = (a, b) + ((res,) if res is not None else ()) + tuple(after)
    return pl.pallas_call(
        body, name=name, grid=(m // tm, n // tn, nk), in_specs=in_specs, out_specs=o_spec,
        out_shape=jax.ShapeDtypeStruct((m, n), out_dtype),
        scratch_shapes=[pltpu.VMEM((tm, tn), F32)],
        compiler_params=_cp("parallel", "parallel", "arbitrary"),
    )(*args)


ROW_TILE = 256


def _rms_fwd(h, w, *, name):
    s, d = h.shape
    tr = min(ROW_TILE, s)

    def body(h_ref, w_ref, o_ref):
        x = h_ref[...]
        r = lax.rsqrt(jnp.mean(x * x, axis=-1, keepdims=True) + NORM_EPS)
        o_ref[...] = (x * r * w_ref[...]).astype(o_ref.dtype)

    return pl.pallas_call(
        body, name=name, grid=(s // tr,),
        in_specs=[pl.BlockSpec((tr, d), lambda i: (i, 0)), pl.BlockSpec((1, d), lambda i: (0, 0))],
        out_specs=pl.BlockSpec((tr, d), lambda i: (i, 0)),
        out_shape=jax.ShapeDtypeStruct((s, d), BF16), compiler_params=_cp("parallel"),
    )(h, w.reshape(1, d))


def _rms_bwd_math(x, w, dy):
    d = x.shape[-1]
    r = lax.rsqrt(jnp.mean(x * x, axis=-1, keepdims=True) + NORM_EPS)
    gw = dy * w
    dx = r * gw - x * ((r * r * r) * (jnp.sum(gw * x, axis=-1, keepdims=True) / d))
    return dx, dy * x * r


def _rms_bwd(h, w, dhn, res, *, name):
    s, d = h.shape
    tr = min(ROW_TILE, s)

    def body(h_ref, w_ref, g_ref, r_ref, dh_ref, dw_ref):
        @pl.when(pl.program_id(0) == 0)
        def _():
            dw_ref[...] = jnp.zeros_like(dw_ref)

        dx, dwt = _rms_bwd_math(h_ref[...], w_ref[...], g_ref[...])
        dh_ref[...] = r_ref[...] + dx
        dw_ref[...] += jnp.sum(dwt, axis=0, keepdims=True)

    row = pl.BlockSpec((tr, d), lambda i: (i, 0))
    vec = pl.BlockSpec((1, d), lambda i: (0, 0))
    return pl.pallas_call(
        body, name=name, grid=(s // tr,), in_specs=[row, vec, row, row], out_specs=[row, vec],
        out_shape=[jax.ShapeDtypeStruct((s, d), F32), jax.ShapeDtypeStruct((1, d), F32)],
        compiler_params=_cp("arbitrary"),
    )(h, w.reshape(1, d), dhn, res)


def _final_fwd_bwd(h, w, tgt, *, name):
    s, d = h.shape
    tr = min(ROW_TILE, s)

    def body(h_ref, w_ref, t_ref, loss_ref, dh_ref, dw_ref):
        @pl.when(pl.program_id(0) == 0)
        def _():
            loss_ref[...] = jnp.zeros_like(loss_ref)
            dw_ref[...] = jnp.zeros_like(dw_ref)

        x = h_ref[...]
        wv = w_ref[...]
        r = lax.rsqrt(jnp.mean(x * x, axis=-1, keepdims=True) + NORM_EPS)
        err = x * r * wv - t_ref[...]
        row_loss = jnp.mean(err * err, axis=-1, keepdims=True)
        loss_ref[...] += 0.5 * jnp.sum(row_loss, axis=0, keepdims=True)
        dx, dwt = _rms_bwd_math(x, wv, err / d)
        dh_ref[...] = dx
        dw_ref[...] += jnp.sum(dwt, axis=0, keepdims=True)

    row = pl.BlockSpec((tr, d), lambda i: (i, 0))
    vec = pl.BlockSpec((1, d), lambda i: (0, 0))
    return pl.pallas_call(
        body, name=name, grid=(s // tr,), in_specs=[row, vec, row],
        out_specs=[pl.BlockSpec((1, 128), lambda i: (0, 0)), row, vec],
        out_shape=[jax.ShapeDtypeStruct((1, 128), F32), jax.ShapeDtypeStruct((s, d), F32),
                   jax.ShapeDtypeStruct((1, d), F32)],
        compiler_params=_cp("arbitrary"),
    )(h, w.reshape(1, d), tgt)


def _ple_fwd(h1, gate_pre, up, *, name):
    s, d = h1.shape
    tr = min(ROW_TILE, s)

    def body(h_ref, g_ref, u_ref, o_ref):
        o_ref[...] = h_ref[...] + u_ref[...] * _sigmoid(g_ref[...])

    row = pl.BlockSpec((tr, d), lambda i: (i, 0))
    return pl.pallas_call(body, name=name, grid=(s // tr,), in_specs=[row, row, row], out_specs=row,
                          out_shape=jax.ShapeDtypeStruct((s, d), F32), compiler_params=_cp("parallel"))(h1, gate_pre, up)


def _ple_bwd(dh2, gate_pre, up, *, name):
    s, d = dh2.shape
    tr = min(ROW_TILE, s)

    def body(d_ref, g_ref, u_ref, dup_ref, dgp_ref):
        dh = d_ref[...]
        gate = _sigmoid(g_ref[...])
        dup_ref[...] = (dh * gate).astype(BF16)
        dgp_ref[...] = (dh * u_ref[...] * gate * (1.0 - gate)).astype(BF16)

    row = pl.BlockSpec((tr, d), lambda i: (i, 0))
    return pl.pallas_call(body, name=name, grid=(s // tr,), in_specs=[row, row, row], out_specs=[row, row],
                          out_shape=[jax.ShapeDtypeStruct((s, d), BF16)] * 2, compiler_params=_cp("parallel"))(dh2, gate_pre, up)


HN_TILE = 512


def _hnorm_fwd(o, proj, z_col, w, *, name):
    s = o.shape[0]
    tr = min(HN_TILE, s)

    def body(o_ref, z_ref, w_ref, y_ref):
        wv = w_ref[...]
        for h in range(N_HEADS):
            cols = slice(h * HEAD_DIM, (h + 1) * HEAD_DIM)
            x = o_ref[:, cols]
            r = lax.rsqrt(jnp.mean(x * x, axis=-1, keepdims=True) + NORM_EPS)
            y_ref[:, cols] = (x * r * wv * _silu(z_ref[:, cols])).astype(BF16)

    blk = pl.BlockSpec((tr, BR_WIDTH), lambda i: (i, 0))
    return pl.pallas_call(
        body, name=name, grid=(s // tr,),
        in_specs=[blk, pl.BlockSpec((tr, BR_WIDTH), lambda i: (i, z_col // BR_WIDTH)), pl.BlockSpec((1, HEAD_DIM), lambda i: (0, 0))],
        out_specs=blk, out_shape=jax.ShapeDtypeStruct((s, BR_WIDTH), BF16), compiler_params=_cp("parallel"),
    )(o, proj, w.reshape(1, HEAD_DIM))


def _hnorm_bwd(o, proj, z_col, w, dy, dy_col, *, name):
    s = o.shape[0]
    tr = min(HN_TILE, s)

    def body(o_ref, z_ref, w_ref, dy_ref, do_ref, dz_ref, dw_ref):
        @pl.when(pl.program_id(0) == 0)
        def _():
            dw_ref[...] = jnp.zeros_like(dw_ref)

        wv = w_ref[...]
        dw = jnp.zeros((1, HEAD_DIM), F32)
        for h in range(N_HEADS):
            cols = slice(h * HEAD_DIM, (h + 1) * HEAD_DIM)
            x, z, g = o_ref[:, cols], z_ref[:, cols], dy_ref[:, cols]
            r = lax.rsqrt(jnp.mean(x * x, axis=-1, keepdims=True) + NORM_EPS)
            on = x * r * wv
            silu_z, dsilu_z = _silu_and_grad(z)
            don = g * silu_z
            dz_ref[:, cols] = (g * on * dsilu_z).astype(BF16)
            gw = don * wv
            do_ref[:, cols] = r * gw - x * ((r * r * r) * (jnp.sum(gw * x, axis=-1, keepdims=True) / HEAD_DIM))
            dw = dw + jnp.sum(don * x * r, axis=0, keepdims=True)
        dw_ref[...] += dw

    blk = pl.BlockSpec((tr, BR_WIDTH), lambda i: (i, 0))
    vec = pl.BlockSpec((1, HEAD_DIM), lambda i: (0, 0))
    return pl.pallas_call(
        body, name=name, grid=(s // tr,),
        in_specs=[blk, pl.BlockSpec((tr, BR_WIDTH), lambda i: (i, z_col // BR_WIDTH)), vec,
                  pl.BlockSpec((tr, BR_WIDTH), lambda i: (i, dy_col // BR_WIDTH))],
        out_specs=[blk, blk, vec],
        out_shape=[jax.ShapeDtypeStruct((s, BR_WIDTH), F32), jax.ShapeDtypeStruct((s, BR_WIDTH), BF16),
                   jax.ShapeDtypeStruct((1, HEAD_DIM), F32)],
        compiler_params=_cp("arbitrary"),
    )(o, proj, w.reshape(1, HEAD_DIM), dy)


def _conv_silu(x, w, s):
    row = _iota2(s, x.shape[1], 0)
    c = w[CONV_W - 1:CONV_W, :] * x
    for k in range(1, CONV_W):
        c = c + w[CONV_W - 1 - k:CONV_W - k, :] * jnp.where(row >= k, pltpu.roll(x, k, 0), 0.0)
    return c


def _dn_qkv_fwd(proj, conv_w, *, name):
    s = proj.shape[0]
    nb = 3 * N_HEADS

    def body(x_ref, w_ref, o_ref):
        j = pl.program_id(0)
        sv = _silu(_conv_silu(x_ref[...], w_ref[...], s))
        r = lax.rsqrt(jnp.sum(sv * sv, axis=-1, keepdims=True) + L2_EPS)
        scale = jnp.where(j < N_HEADS, HEAD_DIM ** -0.5, 1.0).astype(F32)
        o_ref[...] = jnp.where(j < 2 * N_HEADS, sv * r * scale, sv)

    return pl.pallas_call(
        body, name=name, grid=(nb,),
        in_specs=[pl.BlockSpec((s, HEAD_DIM), lambda j: (0, j)), pl.BlockSpec((CONV_W, HEAD_DIM), lambda j: (0, j))],
        out_specs=pl.BlockSpec((s, HEAD_DIM), lambda j: (0, j)),
        out_shape=jax.ShapeDtypeStruct((s, 3 * BR_WIDTH), F32), compiler_params=_cp("parallel"),
    )(proj, conv_w)


def _dn_qkv_bwd(proj, conv_w, dqkv, *, name):
    s = proj.shape[0]
    nb = 3 * N_HEADS

    def body(x_ref, w_ref, g_ref, dx_ref, dw_ref):
        j = pl.program_id(0)
        x, w, g = x_ref[...], w_ref[...], g_ref[...]
        c = _conv_silu(x, w, s)
        sv, dsv = _silu_and_grad(c)
        r = lax.rsqrt(jnp.sum(sv * sv, axis=-1, keepdims=True) + L2_EPS)
        scale = jnp.where(j < N_HEADS, HEAD_DIM ** -0.5, 1.0).astype(F32)
        ds_n = scale * (r * g - sv * ((r * r * r) * jnp.sum(g * sv, axis=-1, keepdims=True)))
        dc = jnp.where(j < 2 * N_HEADS, ds_n, g) * dsv
        row = _iota2(s, HEAD_DIM, 0)
        dx = w[CONV_W - 1:CONV_W, :] * dc
        dws = [jnp.sum(dc * x, axis=0, keepdims=True)]
        for k in range(1, CONV_W):
            dc_ahead = jnp.where(row < s - k, pltpu.roll(dc, s - k, 0), 0.0)
            dx = dx + w[CONV_W - 1 - k:CONV_W - k, :] * dc_ahead
            dws.append(jnp.sum(dc_ahead * x, axis=0, keepdims=True))
        dx_ref[...] = dx.astype(BF16)
        for k in range(CONV_W):
            dw_ref[CONV_W - 1 - k:CONV_W - k, :] = dws[k]

    blk = pl.BlockSpec((s, HEAD_DIM), lambda j: (0, j))
    wblk = pl.BlockSpec((CONV_W, HEAD_DIM), lambda j: (0, j))
    return pl.pallas_call(
        body, name=name, grid=(nb,), in_specs=[blk, wblk, blk], out_specs=[blk, wblk],
        out_shape=[jax.ShapeDtypeStruct((s, 3 * BR_WIDTH), BF16), jax.ShapeDtypeStruct((CONV_W, 3 * BR_WIDTH), F32)],
        compiler_params=_cp("parallel"),
    )(proj, conv_w, dqkv)


def _tri(n, kind):
    r, c = _iota2(n, n, 0), _iota2(n, n, 1)
    if kind == "lower":
        return (r >= c).astype(F32)
    if kind == "upper":
        return (r <= c).astype(F32)
    return (r == c).astype(F32)


GATE_TILE = 512


def _dn_gate_fwd(proj, a_log, dt_bias, *, name):
    s = proj.shape[0]
    tr = min(GATE_TILE, s)

    def body(b_ref, a_ref, al_ref, dt_ref, beta_ref, g_ref):
        beta_ref[...] = _sigmoid(b_ref[...])
        g = -jnp.exp(al_ref[...]) * _softplus(a_ref[...] + dt_ref[...])
        low = _tri(CHUNK, "lower")
        for c in range(tr // CHUNK):
            rows = slice(c * CHUNK, (c + 1) * CHUNK)
            g_ref[rows, :] = _nn_exact(low, g[rows, :])

    blk = lambda cb: pl.BlockSpec((tr, HEAD_DIM), lambda i: (i, cb))
    vec = pl.BlockSpec((1, HEAD_DIM), lambda i: (0, 0))
    out = pl.BlockSpec((tr, HEAD_DIM), lambda i: (i, 0))
    return pl.pallas_call(
        body, name=name, grid=(s // tr,), in_specs=[blk(C_B // HEAD_DIM), blk(C_A // HEAD_DIM), vec, vec],
        out_specs=[out, out], out_shape=[jax.ShapeDtypeStruct((s, HEAD_DIM), F32)] * 2, compiler_params=_cp("parallel"),
    )(proj, proj, a_log, dt_bias)


def _dn_gate_bwd(proj, a_log, dt_bias, dbeta, d_g, *, name):
    s = proj.shape[0]
    tr = min(GATE_TILE, s)

    def body(b_ref, a_ref, al_ref, dt_ref, dbeta_ref, dG_ref, db_ref, da_ref, dal_ref, ddt_ref):
        @pl.when(pl.program_id(0) == 0)
        def _():
            dal_ref[...] = jnp.zeros_like(dal_ref)
            ddt_ref[...] = jnp.zeros_like(ddt_ref)

        beta = _sigmoid(b_ref[...])
        db_ref[...] = (dbeta_ref[...] * beta * (1.0 - beta)).astype(BF16)
        pre = a_ref[...] + dt_ref[...]
        neg_ea = -jnp.exp(al_ref[...])
        up = _tri(CHUNK, "upper")
        d_g = dG_ref[...]
        dg = jnp.concatenate([_nn_exact(up, d_g[c * CHUNK:(c + 1) * CHUNK, :]) for c in range(tr // CHUNK)], axis=0)
        da = dg * neg_ea * _sigmoid(pre)
        da_ref[...] = da.astype(BF16)
        ddt_ref[...] += jnp.sum(da, axis=0, keepdims=True)
        dal_ref[...] += jnp.sum(dg * neg_ea * _softplus(pre), axis=0, keepdims=True)

    blk = lambda cb: pl.BlockSpec((tr, HEAD_DIM), lambda i: (i, cb))
    vec = pl.BlockSpec((1, HEAD_DIM), lambda i: (0, 0))
    io = pl.BlockSpec((tr, HEAD_DIM), lambda i: (i, 0))
    return pl.pallas_call(
        body, name=name, grid=(s // tr,),
        in_specs=[blk(C_B // HEAD_DIM), blk(C_A // HEAD_DIM), vec, vec, io, io], out_specs=[io, io, vec, vec],
        out_shape=[jax.ShapeDtypeStruct((s, HEAD_DIM), BF16)] * 2 + [jax.ShapeDtypeStruct((1, HEAD_DIM), F32)] * 2,
        compiler_params=_cp("arbitrary"),
    )(proj, proj, a_log, dt_bias, dbeta, d_g)


def _unit_lower_inverse(a_strict, eye):
    x = -a_strict
    t = x + eye
    p = x
    n = 2
    while n < CHUNK:
        p = _nn(p, p)
        t = t + _nn(t, p)
        n *= 2
    return t


def _rows(*xs):
    return _hmap(lambda *a: jnp.concatenate(a, axis=0), *xs)


def _lanes(*xs):
    return _hmap(lambda *a: jnp.concatenate(a, axis=1), *xs)


def _dn_chunk_common(q, k, v, gc, beta, st, with_qd_state, t_inv=None):
    c, d = CHUNK, HEAD_DIM
    eye = _tri(c, "eye")
    low = _tri(c, "lower")
    strict = low - eye
    grow = _col2row(gc, eye)
    dec = _hmap(lambda g_, gr: low * jnp.exp(low * (g_ - gr)), gc, grow)
    kb = k * beta
    kq = _nt(_rows(kb, q), k)
    a_mat = kq[0:c, :] * dec * strict
    qk = kq[c:2 * c, :] * dec
    if t_inv is None:
        t_inv = _unit_lower_inverse(a_mat, eye)
    e_g = _exp(gc)
    qd = q * e_g
    uw = _nn(t_inv, _lanes(v * beta, kb * e_g))
    u, w = uw[:, 0:d], uw[:, d:2 * d]
    last = (_iota2(c, 1, 0) == c - 1).astype(F32)
    g_last = _sum(gc * last, 0)
    e_t = _exp(g_last - gc)
    kt = k * e_t
    tail = _exp(g_last)
    if with_qd_state:
        ws = _nn(_rows(w, qd), st)
        vn, qds = u - ws[0:c, :], ws[c:2 * c, :]
    else:
        vn, qds = u - _nn(w, st), None
    return dict(eye=eye, low=low, strict=strict, dec=dec, kb=kb, a_mat=a_mat, t_inv=t_inv, e_g=e_g, u=u, w=w, uw=uw,
                qk=qk, qd=qd, qds=qds, last=last, e_t=e_t, kt=kt, tail=tail, vn=vn)


def _dn_chunk_fwd_math(q, k, v, gc, beta, st):
    m = _dn_chunk_common(q, k, v, gc, beta, st, True)
    o = m["qds"] + _nn(m["qk"], m["vn"])
    st2 = st * m["tail"] + _tn(m["kt"], m["vn"])
    return o, st2, m["t_inv"]


def _dn_chunk_bwd_math(q, k, v, gc, beta, st, do, dst2, t_inv=None):
    c, d = CHUNK, HEAD_DIM
    m = _dn_chunk_common(q, k, v, gc, beta, st, False, t_inv)
    eye, low, strict = m["eye"], m["low"], m["strict"]
    dvn = _tn(m["qk"], do) + _nn(m["kt"], dst2)
    dqk = _nt(do, m["vn"]) * low
    both = _rows(do, dvn)
    ds_both = _nt(both, st)
    dqd, dw = ds_both[0:c, :], -ds_both[c:2 * c, :]
    dst = _tn(_rows(m["qd"], -m["w"]), both) + dst2 * m["tail"]
    dkt = _nt(m["vn"], dst2)
    dtail = _sum(_sum(st * dst2, 1), 0)
    dvb_dkg = _tn(m["t_inv"], _lanes(dvn, dw))
    dvb, dkg = dvb_dkg[:, 0:d], dvb_dkg[:, d:2 * d]
    d_a = _nt(dvb_dkg, m["uw"]) * (-strict)
    dkk = d_a * m["dec"]
    dp = dqk * m["dec"]
    dpk = _rows(dp, dkk)
    dq_dkb = _nn(dpk, k)
    dq = dq_dkb[0:c, :] + dqd * m["e_g"]
    dkb = dq_dkb[c:2 * c, :] + dkg * m["e_g"]
    dk = _tn(dpk, _rows(q, m["kb"])) + dkb * beta + dkt * m["e_t"]
    dv = dvb * beta
    dbeta = _sum(dvb * v + dkb * k, 1)
    de_g = _sum(dkg * m["kb"] + dqd * q, 1)
    de_t = _sum(dkt * k, 1)
    mm = d_a * m["a_mat"] + dqk * m["qk"]
    dgc = (_sum(mm, 1) - _row2col(_sum(mm, 0), eye) + de_g * m["e_g"] - de_t * m["e_t"]
           + (_sum(de_t * m["e_t"], 0) + dtail * m["tail"]) * m["last"])
    return dq, dk, dv, dgc, dbeta, dst


def _heads_of(ref):
    return _Heads(ref[:, h * HEAD_DIM:(h + 1) * HEAD_DIM] for h in range(N_HEADS))


def _lanes_of(block):
    return _Heads(_pick_lane(block, h) for h in range(N_HEADS))


def _dn_chunk_fwd(qkv, gcs, beta, *, name):
    s = qkv.shape[0]
    n = s // CHUNK

    def body(q_ref, k_ref, v_ref, g_ref, b_ref, o_ref, st_out_ref, tinv_ref, st_ref):
        @pl.when(pl.program_id(0) == 0)
        def _():
            st_ref[...] = jnp.zeros_like(st_ref)

        gblk, bblk = g_ref[...], b_ref[...]
        st = _Heads(st_ref[h] for h in range(N_HEADS))
        o, st2, t_inv = _dn_chunk_fwd_math(_heads_of(q_ref), _heads_of(k_ref), _heads_of(v_ref), _lanes_of(gblk),
                                           _lanes_of(bblk), st)
        for h in range(N_HEADS):
            st_out_ref[0, h] = st.v[h]
            tinv_ref[0, h] = t_inv.v[h].astype(BF16)
            o_ref[:, h * HEAD_DIM:(h + 1) * HEAD_DIM] = o.v[h]
            st_ref[h] = st2.v[h]

    blk = lambda off: pl.BlockSpec((CHUNK, BR_WIDTH), lambda c: (c, off))
    sc = pl.BlockSpec((CHUNK, HEAD_DIM), lambda c: (c, 0))
    return pl.pallas_call(
        body, name=name, grid=(n,),
        in_specs=[blk(0), blk(1), blk(2), sc, sc],
        out_specs=[blk(0), pl.BlockSpec((1, N_HEADS, HEAD_DIM, HEAD_DIM), lambda c: (c, 0, 0, 0)),
                   pl.BlockSpec((1, N_HEADS, CHUNK, CHUNK), lambda c: (c, 0, 0, 0))],
        out_shape=[jax.ShapeDtypeStruct((s, BR_WIDTH), F32), jax.ShapeDtypeStruct((n, N_HEADS, HEAD_DIM, HEAD_DIM), F32),
                   jax.ShapeDtypeStruct((n, N_HEADS, CHUNK, CHUNK), BF16)],
        scratch_shapes=[pltpu.VMEM((N_HEADS, HEAD_DIM, HEAD_DIM), F32)],
        compiler_params=_cp("arbitrary"),
    )(qkv, qkv, qkv, gcs, beta)


def _dn_chunk_bwd(qkv, gcs, beta, states, tinvs, do, *, name):
    s = qkv.shape[0]
    n = s // CHUNK

    def body(q_ref, k_ref, v_ref, g_ref, b_ref, st_in_ref, tinv_ref, do_ref, dqkv_ref, dg_ref, dbeta_ref, dst_ref):
        @pl.when(pl.program_id(0) == 0)
        def _():
            dst_ref[...] = jnp.zeros_like(dst_ref)

        gblk, bblk = g_ref[...], b_ref[...]
        lane = _iota2(CHUNK, HEAD_DIM, 1)
        dg_all = jnp.zeros((CHUNK, HEAD_DIM), F32)
        dbeta_all = jnp.zeros((CHUNK, HEAD_DIM), F32)
        dq, dk, dv, dgc, dbeta, dst = _dn_chunk_bwd_math(
            _heads_of(q_ref), _heads_of(k_ref), _heads_of(v_ref), _lanes_of(gblk), _lanes_of(bblk),
            _Heads(st_in_ref[0, h] for h in range(N_HEADS)), _heads_of(do_ref),
            _Heads(dst_ref[h] for h in range(N_HEADS)), _Heads(tinv_ref[0, h] for h in range(N_HEADS)))
        for h in range(N_HEADS):
            for part, val in enumerate((dq, dk, dv)):
                c0 = part * BR_WIDTH + h * HEAD_DIM
                dqkv_ref[:, c0:c0 + HEAD_DIM] = val.v[h]
            dg_all = jnp.where(lane == h, dgc.v[h], dg_all)
            dbeta_all = jnp.where(lane == h, dbeta.v[h], dbeta_all)
            dst_ref[h] = dst.v[h]
        dg_ref[...] = dg_all
        dbeta_ref[...] = dbeta_all

    blk = lambda off: pl.BlockSpec((CHUNK, BR_WIDTH), lambda c: (n - 1 - c, off))
    sc = pl.BlockSpec((CHUNK, HEAD_DIM), lambda c: (n - 1 - c, 0))
    outs = pl.pallas_call(
        body, name=name, grid=(n,),
        in_specs=[blk(0), blk(1), blk(2), sc, sc,
                  pl.BlockSpec((1, N_HEADS, HEAD_DIM, HEAD_DIM), lambda c: (n - 1 - c, 0, 0, 0)),
                  pl.BlockSpec((1, N_HEADS, CHUNK, CHUNK), lambda c: (n - 1 - c, 0, 0, 0)), blk(0)],
        out_specs=[pl.BlockSpec((CHUNK, 3 * BR_WIDTH), lambda c: (n - 1 - c, 0)), sc, sc],
        out_shape=[jax.ShapeDtypeStruct((s, 3 * BR_WIDTH), F32)] + [jax.ShapeDtypeStruct((s, HEAD_DIM), F32)] * 2,
        scratch_shapes=[pltpu.VMEM((N_HEADS, HEAD_DIM, HEAD_DIM), F32)],
        compiler_params=_cp("arbitrary"),
    )(qkv, qkv, qkv, gcs, beta, states, tinvs, do)
    return outs


def _hg_prep_fwd(proj, lb, *, name):
    s = proj.shape[0]
    tr = min(ROW_TILE, s)

    def body(q_ref, f_ref, lb_ref, qo_ref, ko_ref, lf_ref):
        f, lbv = f_ref[...], lb_ref[...]
        qo_ref[...] = _silu(q_ref[...])
        ko_ref[...] = (1.0 - lbv) * _sigmoid(-f)
        lf_ref[...] = jnp.log(lbv + (1.0 - lbv) * _sigmoid(f))

    blk = lambda cb: pl.BlockSpec((tr, BR_WIDTH), lambda i: (i, cb))
    out = pl.BlockSpec((tr, BR_WIDTH), lambda i: (i, 0))
    return pl.pallas_call(
        body, name=name, grid=(s // tr,),
        in_specs=[blk(C_HQ // BR_WIDTH), blk(C_HF // BR_WIDTH), pl.BlockSpec((1, BR_WIDTH), lambda i: (0, 0))],
        out_specs=[out, out, out], out_shape=[jax.ShapeDtypeStruct((s, BR_WIDTH), F32)] * 3, compiler_params=_cp("parallel"),
    )(proj, proj, lb)


def _hg_prep_bwd(proj, lb, dq, dk, dlf, *, name):
    s = proj.shape[0]
    tr = min(ROW_TILE, s)

    def body(q_ref, f_ref, lb_ref, dq_ref, dk_ref, dlf_ref, dhq_ref, dhf_ref, dlb_ref):
        @pl.when(pl.program_id(0) == 0)
        def _():
            dlb_ref[...] = jnp.zeros_like(dlb_ref)

        f, lbv = f_ref[...], lb_ref[...]
        dhq_ref[...] = (dq_ref[...] * _dsilu(q_ref[...])).astype(BF16)
        sp, sn = _sigmoid(f), _sigmoid(-f)
        inner = lbv + (1.0 - lbv) * sp
        dlf_over = dlf_ref[...] / inner
        dkv = dk_ref[...]
        dhf_ref[...] = (dlf_over * (1.0 - lbv) * sp * sn - dkv * (1.0 - lbv) * sn * (1.0 - sn)).astype(BF16)
        dlb_ref[...] += jnp.sum(dlf_over * (1.0 - sp) - dkv * sn, axis=0, keepdims=True)

    blk = lambda cb: pl.BlockSpec((tr, BR_WIDTH), lambda i: (i, cb))
    io = pl.BlockSpec((tr, BR_WIDTH), lambda i: (i, 0))
    vec = pl.BlockSpec((1, BR_WIDTH), lambda i: (0, 0))
    return pl.pallas_call(
        body, name=name, grid=(s // tr,),
        in_specs=[blk(C_HQ // BR_WIDTH), blk(C_HF // BR_WIDTH), vec, io, io, io], out_specs=[io, io, vec],
        out_shape=[jax.ShapeDtypeStruct((s, BR_WIDTH), BF16)] * 2 + [jax.ShapeDtypeStruct((1, BR_WIDTH), F32)],
        compiler_params=_cp("arbitrary"),
    )(proj, proj, lb, dq, dk, dlf)


def _hg_chunk_common(q, k, g):
    c, nb = CHUNK, CHUNK // SUB
    e_g = _exp(g)
    qd = q * e_g
    g_last = g[c - 1:c, :]
    e_t = _exp(g_last - g)
    kt = k * e_t
    tail = _exp(g_last)
    g_refs = [g[i * SUB:i * SUB + 1, :] for i in range(nb)]
    g_ref_rows = _hmap(lambda *rows: jnp.concatenate([jnp.broadcast_to(r, (SUB, r.shape[1])) for r in rows], axis=0), *g_refs)
    e_q = _exp(g - g_ref_rows)
    q_sc = q * e_q
    e_k = [_hmap(lambda gr, g_: jnp.exp(jnp.minimum(gr - g_, EXP_CLAMP)), g_refs[i], g) for i in range(nb)]
    k_sc_all = _rows(*[k * e_k[i] for i in range(nb)])
    row_blk = _iota2(c, 1, 0) // SUB
    masks = [(row_blk == i).astype(F32) for i in range(nb)]
    r_all = _nt(q_sc, k_sc_all)
    a_mat = r_all[:, 0:c] * masks[0]
    for i in range(1, nb):
        a_mat = a_mat + r_all[:, i * c:(i + 1) * c] * masks[i]
    a_mat = a_mat * _tri(c, "lower")
    return dict(e_g=e_g, qd=qd, e_t=e_t, kt=kt, tail=tail, q_sc=q_sc, k_sc_all=k_sc_all, e_q=e_q, e_k=e_k, masks=masks,
                a_mat=a_mat)


def _hg_chunk_fwd_math(q, k, v, g, stt):
    m = _hg_chunk_common(q, k, g)
    o = _nt(m["qd"], stt) + _nn(m["a_mat"], v)
    stt2 = stt * m["tail"] + _tn(v, m["kt"])
    return o, stt2


def _hg_chunk_bwd_math(q, k, v, g, stt, do, dstt2):
    c, nb = CHUNK, CHUNK // SUB
    m = _hg_chunk_common(q, k, g)
    stt2 = stt * m["tail"] + _tn(v, m["kt"])
    later = _sum(stt2 * dstt2, 0)
    dqd = _dot3(do, stt, 1, 0)
    dstt = _tn(do, m["qd"]) + dstt2 * m["tail"]
    d_a = _dot3(do, v, 1, 1) * _tri(c, "lower")
    dv = _tn(m["a_mat"], do) + _nt(m["kt"], dstt2)
    dkt = _dot3(v, dstt2, 1, 0)
    d_blk = _lanes(*[d_a * m["masks"][i] for i in range(nb)])
    dq = dqd * m["e_g"] + _dot3(d_blk, m["k_sc_all"], 1, 0) * m["e_q"]
    dks = _dot3(d_blk, m["q_sc"], 0, 0)
    dk = dkt * m["e_t"]
    for i in range(nb):
        dk = dk + dks[i * c:(i + 1) * c, :] * m["e_k"][i]
    db = q * dq - k * dk
    return dq, dk, dv, db, later, dstt


def _hg_chunk_fwd(qh, kh, proj, lf, *, name):
    s = qh.shape[0]
    n = s // CHUNK
    vb = C_HI // BR_WIDTH

    def body(q_ref, k_ref, v_ref, lf_ref, o_ref, st_out_ref, st_ref):
        @pl.when(pl.program_id(0) == 0)
        def _():
            st_ref[...] = jnp.zeros_like(st_ref)

        st = _Heads(st_ref[h] for h in range(N_HEADS))
        g_all = _nn_exact(_tri(CHUNK, "lower"), lf_ref[...])
        g = _Heads(g_all[:, h * HEAD_DIM:(h + 1) * HEAD_DIM] for h in range(N_HEADS))
        o, st2 = _hg_chunk_fwd_math(_heads_of(q_ref), _heads_of(k_ref), _heads_of(v_ref), g, st)
        for h in range(N_HEADS):
            st_out_ref[0, h] = st.v[h]
            o_ref[:, h * HEAD_DIM:(h + 1) * HEAD_DIM] = o.v[h]
            st_ref[h] = st2.v[h]

    blk = lambda off: pl.BlockSpec((CHUNK, BR_WIDTH), lambda c: (c, off))
    return pl.pallas_call(
        body, name=name, grid=(n,), in_specs=[blk(0), blk(0), blk(vb), blk(0)],
        out_specs=[blk(0), pl.BlockSpec((1, N_HEADS, HEAD_DIM, HEAD_DIM), lambda c: (c, 0, 0, 0))],
        out_shape=[jax.ShapeDtypeStruct((s, BR_WIDTH), F32), jax.ShapeDtypeStruct((n, N_HEADS, HEAD_DIM, HEAD_DIM), F32)],
        scratch_shapes=[pltpu.VMEM((N_HEADS, HEAD_DIM, HEAD_DIM), F32)],
        compiler_params=_cp("arbitrary"),
    )(qh, kh, proj, lf)


def _hg_chunk_bwd(qh, kh, proj, lf, states, do, *, name):
    s = qh.shape[0]
    n = s // CHUNK
    vb = C_HI // BR_WIDTH

    def body(q_ref, k_ref, v_ref, lf_ref, st_in_ref, do_ref, dq_ref, dk_ref, dv_ref, dlf_ref, dst_ref):
        @pl.when(pl.program_id(0) == 0)
        def _():
            dst_ref[...] = jnp.zeros_like(dst_ref)

        g_all = _nn_exact(_tri(CHUNK, "lower"), lf_ref[...])
        g = _Heads(g_all[:, h * HEAD_DIM:(h + 1) * HEAD_DIM] for h in range(N_HEADS))
        dq, dk, dv, db, later, dst = _hg_chunk_bwd_math(
            _heads_of(q_ref), _heads_of(k_ref), _heads_of(v_ref), g,
            _Heads(st_in_ref[0, h] for h in range(N_HEADS)), _heads_of(do_ref),
            _Heads(dst_ref[h] for h in range(N_HEADS)))
        dlf_ref[...] = (_nn_exact(_tri(CHUNK, "upper"), jnp.concatenate(db.v, axis=1))
                        + jnp.concatenate(later.v, axis=1))
        for h in range(N_HEADS):
            cols = slice(h * HEAD_DIM, (h + 1) * HEAD_DIM)
            dq_ref[:, cols] = dq.v[h]
            dk_ref[:, cols] = dk.v[h]
            dv_ref[:, cols] = dv.v[h].astype(BF16)
            dst_ref[h] = dst.v[h]

    blk = lambda off: pl.BlockSpec((CHUNK, BR_WIDTH), lambda c: (n - 1 - c, off))
    return pl.pallas_call(
        body, name=name, grid=(n,),
        in_specs=[blk(0), blk(0), blk(vb), blk(0),
                  pl.BlockSpec((1, N_HEADS, HEAD_DIM, HEAD_DIM), lambda c: (n - 1 - c, 0, 0, 0)), blk(0)],
        out_specs=[blk(0), blk(0), blk(0), blk(0)],
        out_shape=[jax.ShapeDtypeStruct((s, BR_WIDTH), F32)] * 2 + [jax.ShapeDtypeStruct((s, BR_WIDTH), BF16),
                                                                    jax.ShapeDtypeStruct((s, BR_WIDTH), F32)],
        scratch_shapes=[pltpu.VMEM((N_HEADS, HEAD_DIM, HEAD_DIM), F32)],
        compiler_params=_cp("arbitrary"),
    )(qh, kh, proj, lf, states, do)


_ANY = pl.BlockSpec(memory_space=pl.ANY)
_MESH = pl.DeviceIdType.MESH


def _all_gather(x_local, *, name, after=()):
    n_after = len(after)

    def body(x_ref, *refs):
        out_ref, send_sems, recv_sems, local_sem = refs[n_after:]
        x, y, c = lax.axis_index("x"), lax.axis_index("y"), lax.axis_index("c")
        me, sibling = (x, y, c), (x, y, 1 - c)
        n1 = (x ^ (1 - c), y ^ c)
        n2 = (x ^ c, y ^ (1 - c))
        dg = (1 - x, 1 - y)

        def slot(px, py, pc):
            return out_ref.at[4 * px + 2 * py + pc]

        def copy(k, block, to, src=None):
            return pltpu.make_async_remote_copy(
                src_ref=slot(*block) if src is None else src, dst_ref=slot(*block),
                send_sem=send_sems.at[k], recv_sem=recv_sems.at[k], device_id=to, device_id_type=_MESH)

        mine = pltpu.make_async_copy(x_ref, slot(*me), local_sem)
        mine.start()
        first = [copy(0, me, sibling, src=x_ref), copy(1, me, (*n1, c), src=x_ref), copy(2, me, (*n2, c), src=x_ref)]
        for cp in first:
            cp.start()
        copy(2, (*n2, c), me).wait_recv()
        forward = copy(3, (*n2, c), (*n1, c))
        forward.start()
        passed = [copy(5, (*n2, c), sibling)]
        passed[0].start()
        copy(1, (*n1, c), me).wait_recv()
        passed.append(copy(4, (*n1, c), sibling))
        passed[1].start()
        copy(3, (*dg, c), me).wait_recv()
        passed.append(copy(6, (*dg, c), sibling))
        passed[2].start()
        copy(0, sibling, me).wait_recv()
        copy(4, (*n2, 1 - c), me).wait_recv()
        copy(5, (*n1, 1 - c), me).wait_recv()
        copy(6, (*dg, 1 - c), me).wait_recv()
        for cp in first + [forward] + passed:
            cp.wait_send()
        mine.wait()

    return pl.pallas_call(
        body, name=name, out_shape=jax.ShapeDtypeStruct((N_DEV,) + x_local.shape, x_local.dtype),
        in_specs=[_ANY] * (1 + n_after), out_specs=_ANY,
        scratch_shapes=[pltpu.SemaphoreType.DMA((7,)), pltpu.SemaphoreType.DMA((7,)), pltpu.SemaphoreType.DMA],
    )(x_local, *after)


_HBM = pl.BlockSpec(memory_space=pltpu.HBM)
_SEM = pl.BlockSpec(memory_space=pltpu.SEMAPHORE)
_EFFECT = pltpu.SideEffectType.DATAFLOW_SIDE_EFFECTING


def _peers():
    x, y, c = lax.axis_index("x"), lax.axis_index("y"), lax.axis_index("c")
    out = []
    for k in range(1, N_DEV):
        px, py, pc = x ^ ((k >> 2) & 1), y ^ ((k >> 1) & 1), c ^ (k & 1)
        out.append(((px, py, pc), 4 * px + 2 * py + pc))
    return 4 * x + 2 * y + c, out


def _push_copies(src_ref, land_ref, send_sems, recv_sems, broadcast):
    my, peers = _peers()
    pairs = []
    for k, (pos, idx) in enumerate(peers):
        src = src_ref if broadcast else src_ref.at[idx]
        send = pltpu.make_async_remote_copy(src_ref=src, dst_ref=land_ref.at[my], send_sem=send_sems.at[k],
                                            recv_sem=recv_sems.at[k], device_id=pos, device_id_type=_MESH)
        recv = pltpu.make_async_remote_copy(src_ref=src, dst_ref=land_ref.at[idx], send_sem=send_sems.at[k],
                                            recv_sem=recv_sems.at[k], device_id=pos, device_id_type=_MESH)
        pairs.append((send, recv))
    return pairs


def _push_start(src, land, *, broadcast, name, after=()):
    n_after = len(after)

    def body(src_ref, land_ref, *refs):
        send_sems, recv_sems, _, _, token = refs[n_after:]
        for send, _ in _push_copies(src_ref, land_ref, send_sems, recv_sems, broadcast):
            send.start()
        token[...] = jnp.zeros_like(token)

    return pl.pallas_call(
        body, name=name,
        out_shape=(pltpu.SemaphoreType.DMA((N_DEV - 1,)), pltpu.SemaphoreType.DMA((N_DEV - 1,)),
                   pltpu.HBM(src.shape, src.dtype), pltpu.HBM(land.shape, land.dtype), jax.ShapeDtypeStruct((8, 128), F32)),
        in_specs=(_HBM, _HBM) + (_ANY,) * n_after, out_specs=(_SEM, _SEM, _HBM, _HBM, pl.BlockSpec(memory_space=pltpu.VMEM)),
        input_output_aliases={0: 2, 1: 3}, compiler_params=pltpu.CompilerParams(has_side_effects=_EFFECT),
    )(pltpu.with_memory_space_constraint(src, pltpu.HBM), pltpu.with_memory_space_constraint(land, pltpu.HBM), *after)


def _push_wait(handle, after, *, broadcast, name):
    send_sems, recv_sems, src_thru, land_thru, _ = handle

    def body(src_ref, land_ref, send_sems, recv_sems, *rest):
        for send, recv in _push_copies(src_ref, land_ref, send_sems, recv_sems, broadcast):
            send.wait_send()
            recv.wait_recv()

    return pl.pallas_call(
        body, name=name,
        out_shape=(pltpu.HBM(src_thru.shape, src_thru.dtype), pltpu.HBM(land_thru.shape, land_thru.dtype)),
        in_specs=(_HBM, _HBM, _SEM, _SEM) + (_ANY,) * len(after), out_specs=(_HBM, _HBM),
        input_output_aliases={0: 0, 1: 1}, compiler_params=pltpu.CompilerParams(has_side_effects=_EFFECT),
    )(src_thru, land_thru, send_sems, recv_sems, *after)[1]


def _relay_copies(src_ref, land_ref, sems_a, sems_b):
    x, y, c = lax.axis_index("x"), lax.axis_index("y"), lax.axis_index("c")
    slot = lambda px, py, pc: land_ref.at[4 * px + 2 * py + pc]
    chips = [(1 - x, y), (x, 1 - y), (1 - x, 1 - y)]
    (send_a, recv_a), (send_b, recv_b) = sems_a, sems_b

    def copy(sems, k, src, dst_slot, to):
        return pltpu.make_async_remote_copy(src_ref=src, dst_ref=dst_slot, send_sem=sems[0].at[k], recv_sem=sems[1].at[k],
                                            device_id=to, device_id_type=_MESH)

    first = [copy((send_a, recv_a), 0, src_ref, slot(x, y, c), (x, y, 1 - c))]
    first += [copy((send_a, recv_a), 1 + j, src_ref, slot(x, y, c), (*chip, c)) for j, chip in enumerate(chips)]
    first_in = [copy((send_a, recv_a), 0, src_ref, slot(x, y, 1 - c), (x, y, 1 - c))]
    first_in += [copy((send_a, recv_a), 1 + j, src_ref, slot(*chip, c), (*chip, c)) for j, chip in enumerate(chips)]
    relay = [copy((send_b, recv_b), j, slot(*chip, c), slot(*chip, c), (x, y, 1 - c)) for j, chip in enumerate(chips)]
    relay_in = [copy((send_b, recv_b), j, slot(*chip, 1 - c), slot(*chip, 1 - c), (x, y, 1 - c)) for j, chip in enumerate(chips)]
    return first, first_in, relay, relay_in


def _relay_start(src, land, *, name, after=()):
    n_after = len(after)

    def body(src_ref, land_ref, *refs):
        send_a, recv_a, _, _, token = refs[n_after:]
        for cp in _relay_copies(src_ref, land_ref, (send_a, recv_a), (send_a, recv_a))[0]:
            cp.start()
        token[...] = jnp.zeros_like(token)

    send_a, recv_a, src_thru, land_thru, token = pl.pallas_call(
        body, name=name,
        out_shape=(pltpu.SemaphoreType.DMA((4,)), pltpu.SemaphoreType.DMA((4,)), pltpu.HBM(src.shape, src.dtype),
                   pltpu.HBM(land.shape, land.dtype), jax.ShapeDtypeStruct((8, 128), F32)),
        in_specs=(_HBM, _HBM) + (_ANY,) * n_after, out_specs=(_SEM, _SEM, _HBM, _HBM, pl.BlockSpec(memory_space=pltpu.VMEM)),
        input_output_aliases={0: 2, 1: 3}, compiler_params=pltpu.CompilerParams(has_side_effects=_EFFECT),
    )(pltpu.with_memory_space_constraint(src, pltpu.HBM), pltpu.with_memory_space_constraint(land, pltpu.HBM), *after)
    return (send_a, recv_a), src_thru, land_thru, token


def _relay_mid(handle, after, *, name):
    sems_a, src_thru, land_thru, _ = handle
    n_after = len(after)

    def body(src_ref, land_ref, send_a, recv_a, *refs):
        send_b, recv_b, _, _, token = refs[n_after:]
        _, first_in, relay, _ = _relay_copies(src_ref, land_ref, (send_a, recv_a), (send_b, recv_b))
        for j in range(3):
            first_in[1 + j].wait_recv()
            relay[j].start()
        token[...] = jnp.zeros_like(token)

    send_b, recv_b, src2, land2, token = pl.pallas_call(
        body, name=name,
        out_shape=(pltpu.SemaphoreType.DMA((3,)), pltpu.SemaphoreType.DMA((3,)), pltpu.HBM(src_thru.shape, src_thru.dtype),
                   pltpu.HBM(land_thru.shape, land_thru.dtype), jax.ShapeDtypeStruct((8, 128), F32)),
        in_specs=(_HBM, _HBM, _SEM, _SEM) + (_ANY,) * n_after,
        out_specs=(_SEM, _SEM, _HBM, _HBM, pl.BlockSpec(memory_space=pltpu.VMEM)),
        input_output_aliases={0: 2, 1: 3}, compiler_params=pltpu.CompilerParams(has_side_effects=_EFFECT),
    )(src_thru, land_thru, *sems_a, *after)
    return sems_a, (send_b, recv_b), src2, land2, token


def _relay_wait(handle, after, *, name):
    sems_a, sems_b, src_thru, land_thru, _ = handle

    def body(src_ref, land_ref, send_a, recv_a, send_b, recv_b, *rest):
        first, first_in, relay, relay_in = _relay_copies(src_ref, land_ref, (send_a, recv_a), (send_b, recv_b))
        first_in[0].wait_recv()
        for cp in relay_in:
            cp.wait_recv()
        for cp in first + relay:
            cp.wait_send()

    return pl.pallas_call(
        body, name=name,
        out_shape=(pltpu.HBM(src_thru.shape, src_thru.dtype), pltpu.HBM(land_thru.shape, land_thru.dtype)),
        in_specs=(_HBM, _HBM, _SEM, _SEM, _SEM, _SEM) + (_ANY,) * len(after), out_specs=(_HBM, _HBM),
        input_output_aliases={0: 0, 1: 1}, compiler_params=pltpu.CompilerParams(has_side_effects=_EFFECT),
    )(src_thru, land_thru, *sems_a, *sems_b, *after)[1]


def _adamw(parts, row_off, w, m, v, *, layer=0, n_layers=1, prev=None, name, tr):
    rows, c = w.shape
    r = rows // n_layers
    np_ = parts.shape[0]
    tr = min(tr, r)
    assert r % tr == 0 and row_off % tr == 0
    ob, lb = row_off // tr, layer * (r // tr)
    c1 = 1.0 - ADAM_B1 ** ADAM_STEP
    c2 = 1.0 - ADAM_B2 ** ADAM_STEP
    n_prev = 0 if prev is None else 4

    def body(p_ref, w_ref, m_ref, v_ref, *refs):
        g_ref, d_ref, nm_ref, nv_ref = refs[n_prev:]
        g = p_ref[0].astype(F32)
        for s in range(1, np_):
            g = g + p_ref[s].astype(F32)
        wv = w_ref[...]
        m2 = ADAM_B1 * m_ref[...] + (1.0 - ADAM_B1) * g
        v2 = ADAM_B2 * v_ref[...] + (1.0 - ADAM_B2) * jnp.square(g)
        m_hat = m2 / c1
        v_hat = v2 / c2
        g_ref[...] = g
        d_ref[...] = -ADAM_LR * (m_hat / (jnp.sqrt(v_hat) + ADAM_EPS) + ADAM_WD * wv)
        nm_ref[...] = m2
        nv_ref[...] = v2

    blk = pl.BlockSpec((tr, c), lambda i: (lb + i, 0))
    return pl.pallas_call(
        body, name=name, grid=(r // tr,),
        in_specs=[pl.BlockSpec((np_, tr, c), lambda i: (0, ob + i, 0)), blk, blk, blk] + [_ANY] * n_prev,
        out_specs=[blk] * 4, out_shape=[jax.ShapeDtypeStruct((rows, c), F32)] * 4,
        input_output_aliases={4 + i: i for i in range(n_prev)}, compiler_params=_cp("parallel"),
    )(parts, w, m, v, *(prev or ()))


def _sum_parts(parts, *, name, after=()):
    np_, r, c = parts.shape

    def body(p_ref, *refs):
        o_ref = refs[-1]
        g = p_ref[0]
        for s in range(1, np_):
            g = g + p_ref[s]
        o_ref[...] = g

    vmem = pl.BlockSpec(memory_space=pltpu.VMEM)
    return pl.pallas_call(body, name=name, in_specs=[vmem] + [_ANY] * len(after), out_specs=vmem,
                          out_shape=jax.ShapeDtypeStruct((r, c), F32))(parts, *after)


def _pack(arrs):
    rows = []
    for a in arrs:
        f = a.reshape(-1).astype(F32)
        pad = (-f.shape[0]) % 128
        rows.append(jnp.pad(f, (0, pad)).reshape(-1, 128))
    out = jnp.concatenate(rows, axis=0)
    return jnp.pad(out, ((0, (-out.shape[0]) % 8), (0, 0)))


def _unpack(packed, shapes):
    outs, r0 = [], 0
    for shp in shapes:
        n = 1
        for d in shp:
            n *= d
        nr = -(-n // 128)
        outs.append(packed[r0:r0 + nr].reshape(-1)[:n].reshape(shp))
        r0 += nr
    return outs


_WIN_PIECES = ((0, 4096, 0), (4112, 8208, 0), (4096, 4104, HEAD_DIM - N_HEADS), (4104, 4112, HEAD_DIM - N_HEADS))


RELAYOUT_TILE = 256
LAST_SPLIT = 4


def _win_from_shards(shards, *, name):
    k = shards.shape[1]
    tr = min(RELAYOUT_TILE, k)

    def body(x_ref, o_ref):
        cols = []
        for lo, hi, pad in _WIN_PIECES:
            for j in range(N_DEV):
                a, b = max(lo, j * SHARD_IN), min(hi, (j + 1) * SHARD_IN)
                if a < b:
                    cols.append(x_ref[j, :, a - j * SHARD_IN:b - j * SHARD_IN])
            if pad:
                cols.append(jnp.zeros((tr, pad), x_ref.dtype))
        o_ref[...] = jnp.concatenate(cols, axis=1)

    return pl.pallas_call(
        body, name=name, grid=(k // tr,), in_specs=[pl.BlockSpec((N_DEV, tr, SHARD_IN), lambda i: (0, i, 0))],
        out_specs=pl.BlockSpec((tr, N_PROJ), lambda i: (i, 0)), out_shape=jax.ShapeDtypeStruct((k, N_PROJ), shards.dtype),
        compiler_params=_cp("parallel"),
    )(shards)


def _win_to_shards(g, *, name):
    k = g.shape[0]
    tr = min(RELAYOUT_TILE, k)
    starts, off = [], 0
    for lo, hi, pad in _WIN_PIECES:
        starts.append((lo, hi, off))
        off += hi - lo + pad

    def body(g_ref, o_ref):
        for j in range(N_DEV):
            cols = []
            for lo, hi, off in sorted(starts):
                a, b = max(lo, j * SHARD_IN), min(hi, (j + 1) * SHARD_IN)
                if a < b:
                    cols.append(g_ref[:, off + a - lo:off + b - lo])
            o_ref[j] = jnp.concatenate(cols, axis=1)

    return pl.pallas_call(
        body, name=name, grid=(k // tr,), in_specs=[pl.BlockSpec((tr, N_PROJ), lambda i: (i, 0))],
        out_specs=pl.BlockSpec((N_DEV, tr, SHARD_IN), lambda i: (0, i, 0)),
        out_shape=jax.ShapeDtypeStruct((N_DEV, k, SHARD_IN), g.dtype), compiler_params=_cp("parallel"),
    )(g)


def _lower_bounds(logits):
    probs = jax.nn.softmax(logits.astype(F32), axis=0)
    return jnp.cumsum(probs, axis=0) - probs[0]


def _pad_lanes(vec8):
    return jnp.pad(vec8.reshape(1, N_HEADS), ((0, 0), (0, HEAD_DIM - N_HEADS)))


def kernel(x, p, norm_w, w_in, dn_conv_w, dn_A_log, dn_dt_bias, dn_norm_w, hg_lb_logits, hg_norm_w, w_out, w_ple_up, w_ple_gate, final_norm_w, loss_target, m_norm_w, m_w_in, m_dn_conv_w, m_dn_A_log, m_dn_dt_bias, m_dn_norm_w, m_hg_lb_logits, m_hg_norm_w, m_w_out, m_w_ple_up, m_w_ple_gate, m_final_norm_w, v_norm_w, v_w_in, v_dn_conv_w, v_dn_A_log, v_dn_dt_bias, v_dn_norm_w, v_hg_lb_logits, v_hg_norm_w, v_w_out, v_w_ple_up, v_w_ple_gate, v_final_norm_w):
    depth = norm_w.shape[0]
    my = 4 * lax.axis_index("x") + 2 * lax.axis_index("y") + lax.axis_index("c")
    h = x[0]
    tgt = loss_target[0]
    rows_out = D_MODEL // N_DEV
    up_rows = PLE_DIM * (D_MODEL // N_DEV) // D_MODEL
    g_off, u_off = rows_out, 2 * rows_out

    def own_slot(block):
        return lax.dynamic_update_index_in_dim(lax.empty((N_DEV,) + block.shape, block.dtype), block, my, 0)

    win_bf = w_in.astype(BF16)
    rest_bf = [jnp.concatenate([w_out[l], w_ple_gate[l], w_ple_up[l].reshape(up_rows, D_MODEL)], axis=0).astype(BF16)
               for l in range(depth)]
    conv_all = _all_gather(dn_conv_w, name="gather_conv_w")
    conv_full = conv_all.transpose(1, 2, 0, 3).reshape(depth, CONV_W, 3 * BR_WIDTH)
    win_all = {0: _all_gather(win_bf[0], name="gather_w_in_l0", after=[conv_all])}
    pending, relayed = {}, {}
    last = win_all[0]
    for l in range(depth):
        if l > 0:
            relayed["win", l] = _relay_start(win_bf[l], own_slot(win_bf[l]), after=[last], name=f"gather_w_in_l{l}_first")
            last = relayed["win", l][3]
        if l == 0:
            relayed["rest", l] = _relay_start(rest_bf[l], own_slot(rest_bf[l]), after=[last], name=f"gather_rest_l{l}_first")
            last = relayed["rest", l][3]
        else:
            pending["rest", l] = _push_start(rest_bf[l], own_slot(rest_bf[l]), broadcast=True, after=[last],
                                             name=f"gather_rest_l{l}_start")
            last = pending["rest", l][4]
    order_tok = last[0, 0]
    lbs = _lower_bounds(hg_lb_logits)

    saved = []
    weights = []
    for l in range(depth):
        tag = f"l{l}"
        if l > 0:
            win_all[l] = _relay_wait(relayed["win", l], [h], name=f"gather_w_in_{tag}_wait")
        wi = _win_from_shards(win_all[l], name=f"w_in_layout_{tag}")
        nw = norm_w[l] + order_tok if l == 0 else norm_w[l]
        hn = _rms_fwd(h, nw, name=f"rms_fwd_{tag}")
        proj = _mm(hn, wi, mode="nn", out_dtype=F32, name=f"mm_proj_{tag}")
        al, dt = _pad_lanes(dn_A_log[l]), _pad_lanes(dn_dt_bias[l])
        qkv = _dn_qkv_fwd(proj, conv_full[l], name=f"dn_qkv_fwd_{tag}")
        if ("rest", l) in relayed:
            relayed["rest", l] = _relay_mid(relayed["rest", l], [qkv], name=f"gather_rest_{tag}_relay")
            al = al + relayed["rest", l][4][0, 0]
        beta, gcs = _dn_gate_fwd(proj, al, dt, name=f"dn_gate_fwd_{tag}")
        o_dn, st_dn, tinv_dn = _dn_chunk_fwd(qkv, gcs, beta, name=f"dn_chunk_fwd_{tag}")
        lb = lbs[l].reshape(1, BR_WIDTH)
        qh, kh, lf = _hg_prep_fwd(proj, lb, name=f"hg_prep_fwd_{tag}")
        o_hg, st_hg = _hg_chunk_fwd(qh, kh, proj, lf, name=f"hg_chunk_fwd_{tag}")
        y_dn = _hnorm_fwd(o_dn, proj, C_Z, dn_norm_w[l], name=f"hnorm_dn_fwd_{tag}")
        y_hg = _hnorm_fwd(o_hg, proj, C_HZ, hg_norm_w[l], name=f"hnorm_hg_fwd_{tag}")
        y = jnp.concatenate([y_dn, y_hg], axis=1)
        if ("rest", l) in relayed:
            rest_all = _relay_wait(relayed["rest", l], [y], name=f"gather_rest_{tag}_wait")
        else:
            rest_all = _push_wait(pending["rest", l], [y], broadcast=True, name=f"gather_rest_{tag}_wait")
        w_out_rows, w_gate_rows = (0, rows_out), (g_off, rows_out)
        wu = rest_all[:, u_off:u_off + up_rows].reshape(N_DEV, PLE_DIM, D_MODEL // N_DEV).transpose(1, 0, 2).reshape(PLE_DIM, D_MODEL)
        weights.append((wi, rest_all, wu))
        h1 = _mm(y, rest_all, mode="nn", b_rows=w_out_rows, out_dtype=F32, res=h, name=f"mm_out_{tag}")
        pin = []
        if ("win", l + 1) in relayed:
            relayed["win", l + 1] = _relay_mid(relayed["win", l + 1], [h1], name=f"gather_w_in_l{l + 1}_relay")
            pin = [relayed["win", l + 1][4]]
        gp = _mm(h1, rest_all, mode="nn", b_rows=w_gate_rows, out_dtype=F32, after=pin, name=f"mm_gate_{tag}")
        up = _mm(p[l, 0], wu, mode="nn", out_dtype=F32, name=f"mm_up_{tag}")
        h2 = _ple_fwd(h1, gp, up, name=f"ple_fwd_{tag}")
        saved.append(dict(h=h, hn=hn, proj=proj, qkv=qkv, beta=beta, gcs=gcs, st_dn=st_dn, tinv_dn=tinv_dn, qh=qh, kh=kh, lf=lf,
                          st_hg=st_hg, o_dn=o_dn, o_hg=o_hg, y=y, h1=h1, gp=gp, up=up, al=al, dt=dt, lb=lb))
        h = h2

    loss_row, dh, d_final_w = _final_fwd_bwd(h, final_norm_w, tgt, name="final_norm_loss")

    d_norm_w, d_alog, d_dt, d_dn_nw, d_hg_nw, d_lb, d_conv = ([None] * depth for _ in range(7))
    sent = {}
    for l in reversed(range(depth)):
        wi, rest_all, wu = weights[l]
        sv = saved[l]
        tag = f"l{l}"
        dup, dgp = _ple_bwd(dh, sv["gp"], sv["up"], name=f"ple_bwd_{tag}")
        d_wu = _mm(p[l, 0], dup, mode="tn", out_dtype=BF16, name=f"mm_dwup_{tag}")
        d_wg = _mm(sv["h1"], dgp, mode="tn", out_dtype=BF16, name=f"mm_dwgate_{tag}")
        dh1 = _mm(dgp, rest_all, mode="nt", b_rows=(g_off, rows_out), out_dtype=F32, res=dh, name=f"mm_dh1_{tag}")
        d_wo = _mm(sv["y"], dh1, mode="tn", out_dtype=BF16, name=f"mm_dwout_{tag}")
        parts_rest = jnp.concatenate(
            [d_wo.reshape(N_DEV, rows_out, D_MODEL), d_wg.reshape(N_DEV, rows_out, D_MODEL),
             d_wu.reshape(PLE_DIM, N_DEV, D_MODEL // N_DEV).transpose(1, 0, 2).reshape(N_DEV, up_rows, D_MODEL)], axis=1)
        sent["rest", l] = _push_start(parts_rest, own_slot(parts_rest[my]), broadcast=False, name=f"exchange_rest_{tag}_start")
        dy = _mm(dh1, rest_all, mode="nt", b_rows=(0, rows_out), out_dtype=F32, name=f"mm_dy_{tag}")
        dn_nw = dn_norm_w[l] + sent["rest", l][4][0, 0]
        do_dn, dz_dn, d_dn_nw[l] = _hnorm_bwd(sv["o_dn"], sv["proj"], C_Z, dn_nw, dy, 0, name=f"hnorm_dn_bwd_{tag}")
        do_hg, dz_hg, d_hg_nw[l] = _hnorm_bwd(sv["o_hg"], sv["proj"], C_HZ, hg_norm_w[l], dy, BR_WIDTH, name=f"hnorm_hg_bwd_{tag}")
        dqkv, d_gc, dbeta = _dn_chunk_bwd(sv["qkv"], sv["gcs"], sv["beta"], sv["st_dn"], sv["tinv_dn"], do_dn, name=f"dn_chunk_bwd_{tag}")
        dqkv_pre, d_conv[l] = _dn_qkv_bwd(sv["proj"], conv_full[l], dqkv, name=f"dn_qkv_bwd_{tag}")
        db, da, d_alog[l], d_dt[l] = _dn_gate_bwd(sv["proj"], sv["al"], sv["dt"], dbeta, d_gc, name=f"dn_gate_bwd_{tag}")
        dqh, dkh, dhi, dlf = _hg_chunk_bwd(sv["qh"], sv["kh"], sv["proj"], sv["lf"], sv["st_hg"], do_hg, name=f"hg_chunk_bwd_{tag}")
        dhq, dhf, d_lb[l] = _hg_prep_bwd(sv["proj"], sv["lb"], dqh, dkh, dlf, name=f"hg_prep_bwd_{tag}")
        dproj = jnp.concatenate([dqkv_pre, dz_dn, dhq, dhf, dhi, dz_hg, db, da], axis=1)
        def push_d_win(after):
            n_split = LAST_SPLIT if l == 0 else 1
            rows = D_MODEL // n_split
            handles = []
            for q in range(n_split):
                hn_q = sv["hn"] if n_split == 1 else sv["hn"][:, q * rows:(q + 1) * rows]
                sfx = tag if n_split == 1 else f"{tag}_{q}"
                d_win = _mm(hn_q, dproj, mode="tn", out_dtype=BF16, after=after, name=f"mm_dwin_{sfx}")
                parts_in = _win_to_shards(d_win, name=f"dw_in_shards_{sfx}")
                handles.append(_push_start(parts_in, own_slot(parts_in[my]), broadcast=False, after=after,
                                           name=f"exchange_w_in_{sfx}_start"))
                after = [handles[-1][4]]
            return handles

        if l == 0:
            small = _pack([loss_row, jnp.concatenate(d_norm_w[1:], axis=0), d_final_w,
                           jnp.stack([a[0, :N_HEADS] for a in d_alog]), jnp.stack([a[0, :N_HEADS] for a in d_dt]),
                           jnp.concatenate(d_dn_nw, axis=0), jnp.concatenate(d_hg_nw, axis=0), jnp.concatenate(d_lb, axis=0),
                           jnp.stack(d_conv)])
            small_all = _all_gather(small, name="gather_small")
        sent["win", l] = push_d_win([small_all] if l == 0 else [])
        dhn = _mm(dproj, wi, mode="nt", out_dtype=F32, tile_n=D_MODEL, after=[sent["win", l][-1][4]], name=f"mm_dhn_{tag}")
        dh, d_norm_w[l] = _rms_bwd(sv["h"], norm_w[l], dhn, dh1, name=f"rms_bwd_{tag}")
    grad_x = dh[None]

    small_shapes = [(1, 128), (depth - 1, D_MODEL), final_norm_w.shape, dn_A_log.shape, dn_dt_bias.shape, dn_norm_w.shape,
                    hg_norm_w.shape, hg_lb_logits.shape, (depth, CONV_W, 3 * BR_WIDTH)]
    tot = _unpack(_sum_parts(small_all, after=[grad_x], name="sum_small"), small_shapes)
    loss = tot[0][0, 0]
    g_lb = tot[7]
    g_logits = jax.vjp(_lower_bounds, hg_lb_logits)[1](g_lb)[0]
    g_conv = lax.dynamic_slice_in_dim(tot[8], my * (3 * BR_WIDTH // N_DEV), 3 * BR_WIDTH // N_DEV, axis=2)
    small_g = [g_conv, tot[3], tot[4], tot[5], g_logits, tot[6], tot[2]]
    small_w = [dn_conv_w, dn_A_log, dn_dt_bias, dn_norm_w, hg_lb_logits, hg_norm_w, final_norm_w]
    small_m = [m_dn_conv_w, m_dn_A_log, m_dn_dt_bias, m_dn_norm_w, m_hg_lb_logits, m_hg_norm_w, m_final_norm_w]
    small_v = [v_dn_conv_w, v_dn_A_log, v_dn_dt_bias, v_dn_norm_w, v_hg_lb_logits, v_hg_norm_w, v_final_norm_w]
    pk_w = _pack(small_w)
    res_small = _adamw(_pack(small_g)[None], 0, pk_w, _pack(small_m), _pack(small_v), name="adamw_small", tr=pk_w.shape[0])
    shapes_w = [a.shape for a in small_w]
    sg, sd, sm, sv_ = (_unpack(r, shapes_w) for r in res_small)

    r_win = r_wo = r_wg = r_wu = None
    done = [grad_x, res_small[0]]

    def flat(a, cols):
        return a.reshape(-1, cols)

    for l in reversed(range(depth)):
        tag = f"l{l}"
        land_rest = _push_wait(sent["rest", l], done, broadcast=False, name=f"exchange_rest_{tag}_wait")
        r_wo = _adamw(land_rest, 0, flat(w_out, D_MODEL), flat(m_w_out, D_MODEL), flat(v_w_out, D_MODEL), layer=l,
                      n_layers=depth, prev=r_wo, name=f"adamw_w_out_{tag}", tr=rows_out)
        r_wg = _adamw(land_rest, g_off, flat(w_ple_gate, D_MODEL), flat(m_w_ple_gate, D_MODEL), flat(v_w_ple_gate, D_MODEL),
                      layer=l, n_layers=depth, prev=r_wg, name=f"adamw_w_gate_{tag}", tr=rows_out)
        r_wu = _adamw(land_rest, u_off, flat(w_ple_up, D_MODEL), flat(m_w_ple_up, D_MODEL), flat(v_w_ple_up, D_MODEL),
                      layer=l, n_layers=depth, prev=r_wu, name=f"adamw_w_up_{tag}", tr=up_rows)
        done = [r_wo[0], r_wg[0], r_wu[0]]
    for l in reversed(range(depth)):
        tag = f"l{l}"
        if l == 0:
            nw0 = _sum_parts(_all_gather(_pack([d_norm_w[0]]), after=done, name="gather_norm_w"), name="sum_norm_w")
            g_norm_w = jnp.concatenate([_unpack(nw0, [(1, D_MODEL)])[0], tot[1]], axis=0)
            pk_nw = _pack([norm_w])
            r_nw = _adamw(_pack([g_norm_w])[None], 0, pk_nw, _pack([m_norm_w]), _pack([v_norm_w]), name="adamw_norm_w",
                          tr=pk_nw.shape[0])
            r_nw = [_unpack(r, [norm_w.shape])[0] for r in r_nw]
            done = [r_nw[0]]
        n_split = len(sent["win", l])
        for q, handle in enumerate(sent["win", l]):
            sfx = tag if n_split == 1 else f"{tag}_{q}"
            land_in = _push_wait(handle, done, broadcast=False, name=f"exchange_w_in_{sfx}_wait")
            r_win = _adamw(land_in, 0, flat(w_in, SHARD_IN), flat(m_w_in, SHARD_IN), flat(v_w_in, SHARD_IN),
                           layer=l * n_split + q, n_layers=depth * n_split, prev=r_win, name=f"adamw_w_in_{sfx}", tr=256)
            done = [r_win[0]]
    r_win = [o.reshape(w_in.shape) for o in r_win]
    r_wo = [o.reshape(w_out.shape) for o in r_wo]
    r_wg = [o.reshape(w_ple_gate.shape) for o in r_wg]
    r_wu = [o.reshape(w_ple_up.shape) for o in r_wu]

    def order(nw, small_list, big_in, big_out, big_up, big_gate):
        cw, al_, dt_, dnw, lbl, hnw, fw = small_list
        return [nw, big_in, cw, al_, dt_, dnw, lbl, hnw, big_out, big_up, big_gate, fw]

    outs = [loss, grad_x]
    for i, sl in enumerate((sg, sd, sm, sv_)):
        outs += order(r_nw[i], sl, r_win[i], r_wo[i], r_wu[i], r_wg[i])
    return tuple(outs)
```

```python
import functools

import jax
import jax.numpy as jnp
from jax import lax
from jax.experimental import pallas as pl
from jax.experimental.pallas import tpu as pltpu

F32 = jnp.float32
BF16 = jnp.bfloat16
HIGHEST = lax.Precision.HIGHEST

N_DEV = 8
D_MODEL = 2048
PLE_DIM = 256
HEAD_DIM = 128
N_HEADS = 8
BR_WIDTH = N_HEADS * HEAD_DIM
CHUNK = 64
SUB = 16
CONV_W = 4
NORM_EPS = 1e-6
L2_EPS = 1e-6
IN_WIDTH = 8208
SHARD_IN = IN_WIDTH // N_DEV
EXP_CLAMP = 80.0

C_QKV, C_Z, C_HQ, C_HF, C_HI, C_HZ, C_B, C_A, N_PROJ = 0, 3072, 4096, 5120, 6144, 7168, 8192, 8320, 8448

ADAM_LR, ADAM_B1, ADAM_B2, ADAM_EPS, ADAM_WD, ADAM_STEP = 0.001, 0.9, 0.999, 1e-08, 0.01, 10

VMEM_LIMIT = 48 * 1024 * 1024


def _cp(*sem):
    return pltpu.CompilerParams(dimension_semantics=sem, vmem_limit_bytes=VMEM_LIMIT)


class _Heads:
    def __init__(self, vals):
        self.v = tuple(vals)

    def __add__(self, o):
        return _hmap(lambda a, b: a + b, self, o)

    def __radd__(self, o):
        return _hmap(lambda a, b: b + a, self, o)

    def __sub__(self, o):
        return _hmap(lambda a, b: a - b, self, o)

    def __rsub__(self, o):
        return _hmap(lambda a, b: b - a, self, o)

    def __mul__(self, o):
        return _hmap(lambda a, b: a * b, self, o)

    def __rmul__(self, o):
        return _hmap(lambda a, b: b * a, self, o)

    def __neg__(self):
        return _hmap(lambda a: -a, self)

    def __getitem__(self, idx):
        return _hmap(lambda a: a[idx], self)


def _hmap(fn, *args):
    n = next((len(a.v) for a in args if isinstance(a, _Heads)), None)
    if n is None:
        return fn(*args)
    return _Heads(fn(*[a.v[i] if isinstance(a, _Heads) else a for a in args]) for i in range(n))


def _dot(a, b, ca, cb):
    return _hmap(lambda x, y: lax.dot_general(x.astype(BF16), y.astype(BF16), (((ca,), (cb,)), ((), ())),
                                              preferred_element_type=F32), a, b)


def _nn(a, b):
    return _dot(a, b, 1, 0)


def _nt(a, b):
    return _dot(a, b, 1, 1)


def _tn(a, b):
    return _dot(a, b, 0, 0)


def _split(a):
    hi = _hmap(lambda x: x.astype(BF16), a)
    return hi, _hmap(lambda x, h: (x - h.astype(F32)).astype(BF16), a, hi)


def _dot3(a, b, ca, cb):
    ah, al = _split(a)
    bh, bl = _split(b)
    return _dot(ah, bh, ca, cb) + (_dot(ah, bl, ca, cb) + _dot(al, bh, ca, cb))


def _nn_exact(a, b):
    return _hmap(lambda y: lax.dot_general(a, y, (((1,), (0,)), ((), ())), precision=HIGHEST,
                                           preferred_element_type=F32), b)


def _exp(x):
    return _hmap(jnp.exp, x)


def _sum(x, axis):
    return _hmap(lambda a: jnp.sum(a, axis=axis, keepdims=True), x)


def _sigmoid(x):
    return jax.nn.sigmoid(x)


def _silu(x):
    return x * _sigmoid(x)


def _dsilu(x):
    s = _sigmoid(x)
    return s * (1.0 + x * (1.0 - s))


def _silu_and_grad(x):
    s = _sigmoid(x)
    return x * s, s * (1.0 + x * (1.0 - s))


def _softplus(x):
    return jnp.maximum(x, 0.0) + jnp.log(1.0 + jnp.exp(-jnp.abs(x)))


def _iota2(n, m, axis):
    return lax.broadcasted_iota(jnp.int32, (n, m), axis)


def _col2row(col, eye):
    return _hmap(lambda c: jnp.sum(eye * c, axis=0, keepdims=True), col)


def _row2col(row, eye):
    return _hmap(lambda r: jnp.sum(eye * r, axis=1, keepdims=True), row)


def _pick_lane(block, lane_idx):
    lane = _iota2(block.shape[0], block.shape[1], 1)
    return jnp.sum(jnp.where(lane == lane_idx, block, 0.0), axis=1, keepdims=True)


MM_TILE_M, MM_TILE_N, MM_TILE_K = 1024, 1408, 2048


def _tile(dim, cap):
    if dim <= cap:
        return dim
    t = cap - cap % 128
    while dim % t:
        t -= 128
    return t


def _mm(a, b, *, mode, out_dtype, res=None, after=(), b_rows=None, tile_n=MM_TILE_N, name):
    b_mat_rows = b.shape[0] if b_rows is None else N_DEV * b_rows[1]
    if mode == "nn":
        (m, kd), n = a.shape, b.shape[-1]
        assert kd == b_mat_rows
    elif mode == "nt":
        (m, kd), n = a.shape, b_mat_rows
    else:
        (kd, m), n = a.shape, b.shape[-1]
    tm, tn, tk = _tile(m, MM_TILE_M), _tile(n, tile_n), _tile(kd, MM_TILE_K)
    assert m % tm == 0 and n % tn == 0 and kd % tk == 0, (m, n, kd, tm, tn, tk)
    nk = kd // tk
    ca, cb = {"nn": (1, 0), "nt": (1, 1), "tn": (0, 0)}[mode]

    def body(*refs):
        a_ref, b_ref = refs[:2]
        r_ref = None if res is None else refs[2]
        o_ref, acc_ref = refs[-2:]
        k = pl.program_id(2)

        @pl.when(k == 0)
        def _():
            acc_ref[...] = jnp.zeros_like(acc_ref)

        b_tile = b_ref[...]
        if b_rows is not None:
            b_tile = b_tile.reshape(-1, b_tile.shape[-1])
        acc_ref[...] += _dot(a_ref[...], b_tile, ca, cb)

        @pl.when(k == nk - 1)
        def _():
            out = acc_ref[...]
            if r_ref is not None:
                out = out + r_ref[...].astype(F32)
            o_ref[...] = out.astype(o_ref.dtype)

    a_spec = pl.BlockSpec((tk, tm), lambda i, j, k: (k, i)) if mode == "tn" else pl.BlockSpec((tm, tk), lambda i, j, k: (i, k))
    if b_rows is None:
        b_spec = pl.BlockSpec((tn, tk), lambda i, j, k: (j, k)) if mode == "nt" else pl.BlockSpec((tk, tn), lambda i, j, k: (k, j))
    else:
        first, count = b_rows
        assert first % count == 0 and mode in ("nn", "nt")
        rb = first // count
        if mode == "nn":
            assert tk == kd
            b_spec = pl.BlockSpec((N_DEV, count, tn), lambda i, j, k: (0, rb, j))
        else:
            assert tn % count == 0
            b_spec = pl.BlockSpec((tn // count, count, tk), lambda i, j, k: (j, rb, k))
    o_spec = pl.BlockSpec((tm, tn), lambda i, j, k: (i, j))
    in_specs = [a_spec, b_spec] + ([o_spec] if res is not None else []) + [pl.BlockSpec(memory_space=pl.ANY)] * len(after)
    args = (a, b) + ((res,) if res is not None else ()) + tuple(after)
    return pl.pallas_call(
        body, name=name, grid=(m // tm, n // tn, nk), in_specs=in_specs, out_specs=o_spec,
        out_shape=jax.ShapeDtypeStruct((m, n), out_dtype),
        scratch_shapes=[pltpu.VMEM((tm, tn), F32)],
        compiler_params=_cp("parallel", "parallel", "arbitrary"),
    )(*args)


ROW_TILE = 256


def _rms_fwd(h, w, *, name):
    s, d = h.shape
    tr = min(ROW_TILE, s)

    def body(h_ref, w_ref, o_ref):
        x = h_ref[...]
        r = lax.rsqrt(jnp.mean(x * x, axis=-1, keepdims=True) + NORM_EPS)
        o_ref[...] = (x * r * w_ref[...]).astype(o_ref.dtype)

    return pl.pallas_call(
        body, name=name, grid=(s // tr,),
        in_specs=[pl.BlockSpec((tr, d), lambda i: (i, 0)), pl.BlockSpec((1, d), lambda i: (0, 0))],
        out_specs=pl.BlockSpec((tr, d), lambda i: (i, 0)),
        out_shape=jax.ShapeDtypeStruct((s, d), BF16), compiler_params=_cp("parallel"),
    )(h, w.reshape(1, d))


def _rms_bwd_math(x, w, dy):
    d = x.shape[-1]
    r = lax.rsqrt(jnp.mean(x * x, axis=-1, keepdims=True) + NORM_EPS)
    gw = dy * w
    dx = r * gw - x * ((r * r * r) * (jnp.sum(gw * x, axis=-1, keepdims=True) / d))
    return dx, dy * x * r


def _rms_bwd(h, w, dhn, res, *, name):
    s, d = h.shape
    tr = min(ROW_TILE, s)

    def body(h_ref, w_ref, g_ref, r_ref, dh_ref, dw_ref):
        @pl.when(pl.program_id(0) == 0)
        def _():
            dw_ref[...] = jnp.zeros_like(dw_ref)

        dx, dwt = _rms_bwd_math(h_ref[...], w_ref[...], g_ref[...])
        dh_ref[...] = r_ref[...] + dx
        dw_ref[...] += jnp.sum(dwt, axis=0, keepdims=True)

    row = pl.BlockSpec((tr, d), lambda i: (i, 0))
    vec = pl.BlockSpec((1, d), lambda i: (0, 0))
    return pl.pallas_call(
        body, name=name, grid=(s // tr,), in_specs=[row, vec, row, row], out_specs=[row, vec],
        out_shape=[jax.ShapeDtypeStruct((s, d), F32), jax.ShapeDtypeStruct((1, d), F32)],
        compiler_params=_cp("arbitrary"),
    )(h, w.reshape(1, d), dhn, res)


def _final_fwd_bwd(h, w, tgt, *, name):
    s, d = h.shape
    tr = min(ROW_TILE, s)

    def body(h_ref, w_ref, t_ref, loss_ref, dh_ref, dw_ref):
        @pl.when(pl.program_id(0) == 0)
        def _():
            loss_ref[...] = jnp.zeros_like(loss_ref)
            dw_ref[...] = jnp.zeros_like(dw_ref)

        x = h_ref[...]
        wv = w_ref[...]
        r = lax.rsqrt(jnp.mean(x * x, axis=-1, keepdims=True) + NORM_EPS)
        err = x * r * wv - t_ref[...]
        row_loss = jnp.mean(err * err, axis=-1, keepdims=True)
        loss_ref[...] += 0.5 * jnp.sum(row_loss, axis=0, keepdims=True)
        dx, dwt = _rms_bwd_math(x, wv, err / d)
        dh_ref[...] = dx
        dw_ref[...] += jnp.sum(dwt, axis=0, keepdims=True)

    row = pl.BlockSpec((tr, d), lambda i: (i, 0))
    vec = pl.BlockSpec((1, d), lambda i: (0, 0))
    return pl.pallas_call(
        body, name=name, grid=(s // tr,), in_specs=[row, vec, row],
        out_specs=[pl.BlockSpec((1, 128), lambda i: (0, 0)), row, vec],
        out_shape=[jax.ShapeDtypeStruct((1, 128), F32), jax.ShapeDtypeStruct((s, d), F32),
                   jax.ShapeDtypeStruct((1, d), F32)],
        compiler_params=_cp("arbitrary"),
    )(h, w.reshape(1, d), tgt)


def _ple_fwd(h1, gate_pre, up, *, name):
    s, d = h1.shape
    tr = min(ROW_TILE, s)

    def body(h_ref, g_ref, u_ref, o_ref):
        o_ref[...] = h_ref[...] + u_ref[...] * _sigmoid(g_ref[...])

    row = pl.BlockSpec((tr, d), lambda i: (i, 0))
    return pl.pallas_call(body, name=name, grid=(s // tr,), in_specs=[row, row, row], out_specs=row,
                          out_shape=jax.ShapeDtypeStruct((s, d), F32), compiler_params=_cp("parallel"))(h1, gate_pre, up)


def _ple_bwd(dh2, gate_pre, up, *, name):
    s, d = dh2.shape
    tr = min(ROW_TILE, s)

    def body(d_ref, g_ref, u_ref, dup_ref, dgp_ref):
        dh = d_ref[...]
        gate = _sigmoid(g_ref[...])
        dup_ref[...] = (dh * gate).astype(BF16)
        dgp_ref[...] = (dh * u_ref[...] * gate * (1.0 - gate)).astype(BF16)

    row = pl.BlockSpec((tr, d), lambda i: (i, 0))
    return pl.pallas_call(body, name=name, grid=(s // tr,), in_specs=[row, row, row], out_specs=[row, row],
                          out_shape=[jax.ShapeDtypeStruct((s, d), BF16)] * 2, compiler_params=_cp("parallel"))(dh2, gate_pre, up)


HN_TILE = 512


def _hnorm_fwd(o, proj, z_col, w, *, name):
    s = o.shape[0]
    tr = min(HN_TILE, s)

    def body(o_ref, z_ref, w_ref, y_ref):
        wv = w_ref[...]
        for h in range(N_HEADS):
            cols = slice(h * HEAD_DIM, (h + 1) * HEAD_DIM)
            x = o_ref[:, cols]
            r = lax.rsqrt(jnp.mean(x * x, axis=-1, keepdims=True) + NORM_EPS)
            y_ref[:, cols] = (x * r * wv * _silu(z_ref[:, cols])).astype(BF16)

    blk = pl.BlockSpec((tr, BR_WIDTH), lambda i: (i, 0))
    return pl.pallas_call(
        body, name=name, grid=(s // tr,),
        in_specs=[blk, pl.BlockSpec((tr, BR_WIDTH), lambda i: (i, z_col // BR_WIDTH)), pl.BlockSpec((1, HEAD_DIM), lambda i: (0, 0))],
        out_specs=blk, out_shape=jax.ShapeDtypeStruct((s, BR_WIDTH), BF16), compiler_params=_cp("parallel"),
    )(o, proj, w.reshape(1, HEAD_DIM))


def _hnorm_bwd(o, proj, z_col, w, dy, dy_col, *, name):
    s = o.shape[0]
    tr = min(HN_TILE, s)

    def body(o_ref, z_ref, w_ref, dy_ref, do_ref, dz_ref, dw_ref):
        @pl.when(pl.program_id(0) == 0)
        def _():
            dw_ref[...] = jnp.zeros_like(dw_ref)

        wv = w_ref[...]
        dw = jnp.zeros((1, HEAD_DIM), F32)
        for h in range(N_HEADS):
            cols = slice(h * HEAD_DIM, (h + 1) * HEAD_DIM)
            x, z, g = o_ref[:, cols], z_ref[:, cols], dy_ref[:, cols]
            r = lax.rsqrt(jnp.mean(x * x, axis=-1, keepdims=True) + NORM_EPS)
            on = x * r * wv
            silu_z, dsilu_z = _silu_and_grad(z)
            don = g * silu_z
            dz_ref[:, cols] = (g * on * dsilu_z).astype(BF16)
            gw = don * wv
            do_ref[:, cols] = r * gw - x * ((r * r * r) * (jnp.sum(gw * x, axis=-1, keepdims=True) / HEAD_DIM))
            dw = dw + jnp.sum(don * x * r, axis=0, keepdims=True)
        dw_ref[...] += dw

    blk = pl.BlockSpec((tr, BR_WIDTH), lambda i: (i, 0))
    vec = pl.BlockSpec((1, HEAD_DIM), lambda i: (0, 0))
    return pl.pallas_call(
        body, name=name, grid=(s // tr,),
        in_specs=[blk, pl.BlockSpec((tr, BR_WIDTH), lambda i: (i, z_col // BR_WIDTH)), vec,
                  pl.BlockSpec((tr, BR_WIDTH), lambda i: (i, dy_col // BR_WIDTH))],
        out_specs=[blk, blk, vec],
        out_shape=[jax.ShapeDtypeStruct((s, BR_WIDTH), F32), jax.ShapeDtypeStruct((s, BR_WIDTH), BF16),
                   jax.ShapeDtypeStruct((1, HEAD_DIM), F32)],
        compiler_params=_cp("arbitrary"),
    )(o, proj, w.reshape(1, HEAD_DIM), dy)


def _conv_silu(x, w, s):
    row = _iota2(s, x.shape[1], 0)
    c = w[CONV_W - 1:CONV_W, :] * x
    for k in range(1, CONV_W):
        c = c + w[CONV_W - 1 - k:CONV_W - k, :] * jnp.where(row >= k, pltpu.roll(x, k, 0), 0.0)
    return c


def _dn_qkv_fwd(proj, conv_w, *, name):
    s = proj.shape[0]
    nb = 3 * N_HEADS

    def body(x_ref, w_ref, o_ref):
        j = pl.program_id(0)
        sv = _silu(_conv_silu(x_ref[...], w_ref[...], s))
        r = lax.rsqrt(jnp.sum(sv * sv, axis=-1, keepdims=True) + L2_EPS)
        scale = jnp.where(j < N_HEADS, HEAD_DIM ** -0.5, 1.0).astype(F32)
        o_ref[...] = jnp.where(j < 2 * N_HEADS, sv * r * scale, sv)

    return pl.pallas_call(
        body, name=name, grid=(nb,),
        in_specs=[pl.BlockSpec((s, HEAD_DIM), lambda j: (0, j)), pl.BlockSpec((CONV_W, HEAD_DIM), lambda j: (0, j))],
        out_specs=pl.BlockSpec((s, HEAD_DIM), lambda j: (0, j)),
        out_shape=jax.ShapeDtypeStruct((s, 3 * BR_WIDTH), F32), compiler_params=_cp("parallel"),
    )(proj, conv_w)


def _dn_qkv_bwd(proj, conv_w, dqkv, *, name):
    s = proj.shape[0]
    nb = 3 * N_HEADS

    def body(x_ref, w_ref, g_ref, dx_ref, dw_ref):
        j = pl.program_id(0)
        x, w, g = x_ref[...], w_ref[...], g_ref[...]
        c = _conv_silu(x, w, s)
        sv, dsv = _silu_and_grad(c)
        r = lax.rsqrt(jnp.sum(sv * sv, axis=-1, keepdims=True) + L2_EPS)
        scale = jnp.where(j < N_HEADS, HEAD_DIM ** -0.5, 1.0).astype(F32)
        ds_n = scale * (r * g - sv * ((r * r * r) * jnp.sum(g * sv, axis=-1, keepdims=True)))
        dc = jnp.where(j < 2 * N_HEADS, ds_n, g) * dsv
        row = _iota2(s, HEAD_DIM, 0)
        dx = w[CONV_W - 1:CONV_W, :] * dc
        dws = [jnp.sum(dc * x, axis=0, keepdims=True)]
        for k in range(1, CONV_W):
            dc_ahead = jnp.where(row < s - k, pltpu.roll(dc, s - k, 0), 0.0)
            dx = dx + w[CONV_W - 1 - k:CONV_W - k, :] * dc_ahead
            dws.append(jnp.sum(dc_ahead * x, axis=0, keepdims=True))
        dx_ref[...] = dx.astype(BF16)
        for k in range(CONV_W):
            dw_ref[CONV_W - 1 - k:CONV_W - k, :] = dws[k]

    blk = pl.BlockSpec((s, HEAD_DIM), lambda j: (0, j))
    wblk = pl.BlockSpec((CONV_W, HEAD_DIM), lambda j: (0, j))
    return pl.pallas_call(
        body, name=name, grid=(nb,), in_specs=[blk, wblk, blk], out_specs=[blk, wblk],
        out_shape=[jax.ShapeDtypeStruct((s, 3 * BR_WIDTH), BF16), jax.ShapeDtypeStruct((CONV_W, 3 * BR_WIDTH), F32)],
        compiler_params=_cp("parallel"),
    )(proj, conv_w, dqkv)


def _tri(n, kind):
    r, c = _iota2(n, n, 0), _iota2(n, n, 1)
    if kind == "lower":
        return (r >= c).astype(F32)
    if kind == "upper":
        return (r <= c).astype(F32)
    return (r == c).astype(F32)


GATE_TILE = 512


def _dn_gate_fwd(proj, a_log, dt_bias, *, name):
    s = proj.shape[0]
    tr = min(GATE_TILE, s)

    def body(b_ref, a_ref, al_ref, dt_ref, beta_ref, g_ref):
        beta_ref[...] = _sigmoid(b_ref[...])
        g = -jnp.exp(al_ref[...]) * _softplus(a_ref[...] + dt_ref[...])
        low = _tri(CHUNK, "lower")
        for c in range(tr // CHUNK):
            rows = slice(c * CHUNK, (c + 1) * CHUNK)
            g_ref[rows, :] = _nn_exact(low, g[rows, :])

    blk = lambda cb: pl.BlockSpec((tr, HEAD_DIM), lambda i: (i, cb))
    vec = pl.BlockSpec((1, HEAD_DIM), lambda i: (0, 0))
    out = pl.BlockSpec((tr, HEAD_DIM), lambda i: (i, 0))
    return pl.pallas_call(
        body, name=name, grid=(s // tr,), in_specs=[blk(C_B // HEAD_DIM), blk(C_A // HEAD_DIM), vec, vec],
        out_specs=[out, out], out_shape=[jax.ShapeDtypeStruct((s, HEAD_DIM), F32)] * 2, compiler_params=_cp("parallel"),
    )(proj, proj, a_log, dt_bias)


def _dn_gate_bwd(proj, a_log, dt_bias, dbeta, d_g, *, name):
    s = proj.shape[0]
    tr = min(GATE_TILE, s)

    def body(b_ref, a_ref, al_ref, dt_ref, dbeta_ref, dG_ref, db_ref, da_ref, dal_ref, ddt_ref):
        @pl.when(pl.program_id(0) == 0)
        def _():
            dal_ref[...] = jnp.zeros_like(dal_ref)
            ddt_ref[...] = jnp.zeros_like(ddt_ref)

        beta = _sigmoid(b_ref[...])
        db_ref[...] = (dbeta_ref[...] * beta * (1.0 - beta)).astype(BF16)
        pre = a_ref[...] + dt_ref[...]
        neg_ea = -jnp.exp(al_ref[...])
        up = _tri(CHUNK, "upper")
        d_g = dG_ref[...]
        dg = jnp.concatenate([_nn_exact(up, d_g[c * CHUNK:(c + 1) * CHUNK, :]) for c in range(tr // CHUNK)], axis=0)
        da = dg * neg_ea * _sigmoid(pre)
        da_ref[...] = da.astype(BF16)
        ddt_ref[...] += jnp.sum(da, axis=0, keepdims=True)
        dal_ref[...] += jnp.sum(dg * neg_ea * _softplus(pre), axis=0, keepdims=True)

    blk = lambda cb: pl.BlockSpec((tr, HEAD_DIM), lambda i: (i, cb))
    vec = pl.BlockSpec((1, HEAD_DIM), lambda i: (0, 0))
    io = pl.BlockSpec((tr, HEAD_DIM), lambda i: (i, 0))
    return pl.pallas_call(
        body, name=name, grid=(s // tr,),
        in_specs=[blk(C_B // HEAD_DIM), blk(C_A // HEAD_DIM), vec, vec, io, io], out_specs=[io, io, vec, vec],
        out_shape=[jax.ShapeDtypeStruct((s, HEAD_DIM), BF16)] * 2 + [jax.ShapeDtypeStruct((1, HEAD_DIM), F32)] * 2,
        compiler_params=_cp("arbitrary"),
    )(proj, proj, a_log, dt_bias, dbeta, d_g)


def _unit_lower_inverse(a_strict, eye):
    x = -a_strict
    t = x + eye
    p = x
    n = 2
    while n < CHUNK:
        p = _nn(p, p)
        t = t + _nn(t, p)
        n *= 2
    return t


def _rows(*xs):
    return _hmap(lambda *a: jnp.concatenate(a, axis=0), *xs)


def _lanes(*xs):
    return _hmap(lambda *a: jnp.concatenate(a, axis=1), *xs)


def _dn_chunk_common(q, k, v, gc, beta, st, with_qd_state, t_inv=None):
    c, d = CHUNK, HEAD_DIM
    eye = _tri(c, "eye")
    low = _tri(c, "lower")
    strict = low - eye
    grow = _col2row(gc, eye)
    dec = _hmap(lambda g_, gr: low * jnp.exp(low * (g_ - gr)), gc, grow)
    kb = k * beta
    kq = _nt(_rows(kb, q), k)
    a_mat = kq[0:c, :] * dec * strict
    qk = kq[c:2 * c, :] * dec
    if t_inv is None:
        t_inv = _unit_lower_inverse(a_mat, eye)
    e_g = _exp(gc)
    qd = q * e_g
    uw = _nn(t_inv, _lanes(v * beta, kb * e_g))
    u, w = uw[:, 0:d], uw[:, d:2 * d]
    last = (_iota2(c, 1, 0) == c - 1).astype(F32)
    g_last = _sum(gc * last, 0)
    e_t = _exp(g_last - gc)
    kt = k * e_t
    tail = _exp(g_last)
    if with_qd_state:
        ws = _nn(_rows(w, qd), st)
        vn, qds = u - ws[0:c, :], ws[c:2 * c, :]
    else:
        vn, qds = u - _nn(w, st), None
    return dict(eye=eye, low=low, strict=strict, dec=dec, kb=kb, a_mat=a_mat, t_inv=t_inv, e_g=e_g, u=u, w=w, uw=uw,
                qk=qk, qd=qd, qds=qds, last=last, e_t=e_t, kt=kt, tail=tail, vn=vn)


def _dn_chunk_fwd_math(q, k, v, gc, beta, st):
    m = _dn_chunk_common(q, k, v, gc, beta, st, True)
    o = m["qds"] + _nn(m["qk"], m["vn"])
    st2 = st * m["tail"] + _tn(m["kt"], m["vn"])
    return o, st2, m["t_inv"]


def _dn_chunk_bwd_math(q, k, v, gc, beta, st, do, dst2, t_inv=None):
    c, d = CHUNK, HEAD_DIM
    m = _dn_chunk_common(q, k, v, gc, beta, st, False, t_inv)
    eye, low, strict = m["eye"], m["low"], m["strict"]
    dvn = _tn(m["qk"], do) + _nn(m["kt"], dst2)
    dqk = _nt(do, m["vn"]) * low
    both = _rows(do, dvn)
    ds_both = _nt(both, st)
    dqd, dw = ds_both[0:c, :], -ds_both[c:2 * c, :]
    dst = _tn(_rows(m["qd"], -m["w"]), both) + dst2 * m["tail"]
    dkt = _nt(m["vn"], dst2)
    dtail = _sum(_sum(st * dst2, 1), 0)
    dvb_dkg = _tn(m["t_inv"], _lanes(dvn, dw))
    dvb, dkg = dvb_dkg[:, 0:d], dvb_dkg[:, d:2 * d]
    d_a = _nt(dvb_dkg, m["uw"]) * (-strict)
    dkk = d_a * m["dec"]
    dp = dqk * m["dec"]
    dpk = _rows(dp, dkk)
    dq_dkb = _nn(dpk, k)
    dq = dq_dkb[0:c, :] + dqd * m["e_g"]
    dkb = dq_dkb[c:2 * c, :] + dkg * m["e_g"]
    dk = _tn(dpk, _rows(q, m["kb"])) + dkb * beta + dkt * m["e_t"]
    dv = dvb * beta
    dbeta = _sum(dvb * v + dkb * k, 1)
    de_g = _sum(dkg * m["kb"] + dqd * q, 1)
    de_t = _sum(dkt * k, 1)
    mm = d_a * m["a_mat"] + dqk * m["qk"]
    dgc = (_sum(mm, 1) - _row2col(_sum(mm, 0), eye) + de_g * m["e_g"] - de_t * m["e_t"]
           + (_sum(de_t * m["e_t"], 0) + dtail * m["tail"]) * m["last"])
    return dq, dk, dv, dgc, dbeta, dst


def _heads_of(ref):
    return _Heads(ref[:, h * HEAD_DIM:(h + 1) * HEAD_DIM] for h in range(N_HEADS))


def _lanes_of(block):
    return _Heads(_pick_lane(block, h) for h in range(N_HEADS))


def _dn_chunk_fwd(qkv, gcs, beta, *, name):
    s = qkv.shape[0]
    n = s // CHUNK

    def body(q_ref, k_ref, v_ref, g_ref, b_ref, o_ref, st_out_ref, tinv_ref, st_ref):
        @pl.when(pl.program_id(0) == 0)
        def _():
            st_ref[...] = jnp.zeros_like(st_ref)

        gblk, bblk = g_ref[...], b_ref[...]
        st = _Heads(st_ref[h] for h in range(N_HEADS))
        o, st2, t_inv = _dn_chunk_fwd_math(_heads_of(q_ref), _heads_of(k_ref), _heads_of(v_ref), _lanes_of(gblk),
                                           _lanes_of(bblk), st)
        for h in range(N_HEADS):
            st_out_ref[0, h] = st.v[h]
            tinv_ref[0, h] = t_inv.v[h].astype(BF16)
            o_ref[:, h * HEAD_DIM:(h + 1) * HEAD_DIM] = o.v[h]
            st_ref[h] = st2.v[h]

    blk = lambda off: pl.BlockSpec((CHUNK, BR_WIDTH), lambda c: (c, off))
    sc = pl.BlockSpec((CHUNK, HEAD_DIM), lambda c: (c, 0))
    return pl.pallas_call(
        body, name=name, grid=(n,),
        in_specs=[blk(0), blk(1), blk(2), sc, sc],
        out_specs=[blk(0), pl.BlockSpec((1, N_HEADS, HEAD_DIM, HEAD_DIM), lambda c: (c, 0, 0, 0)),
                   pl.BlockSpec((1, N_HEADS, CHUNK, CHUNK), lambda c: (c, 0, 0, 0))],
        out_shape=[jax.ShapeDtypeStruct((s, BR_WIDTH), F32), jax.ShapeDtypeStruct((n, N_HEADS, HEAD_DIM, HEAD_DIM), F32),
                   jax.ShapeDtypeStruct((n, N_HEADS, CHUNK, CHUNK), BF16)],
        scratch_shapes=[pltpu.VMEM((N_HEADS, HEAD_DIM, HEAD_DIM), F32)],
        compiler_params=_cp("arbitrary"),
    )(qkv, qkv, qkv, gcs, beta)


def _dn_chunk_bwd(qkv, gcs, beta, states, tinvs, do, *, name):
    s = qkv.shape[0]
    n = s // CHUNK

    def body(q_ref, k_ref, v_ref, g_ref, b_ref, st_in_ref, tinv_ref, do_ref, dqkv_ref, dg_ref, dbeta_ref, dst_ref):
        @pl.when(pl.program_id(0) == 0)
        def _():
            dst_ref[...] = jnp.zeros_like(dst_ref)

        gblk, bblk = g_ref[...], b_ref[...]
        lane = _iota2(CHUNK, HEAD_DIM, 1)
        dg_all = jnp.zeros((CHUNK, HEAD_DIM), F32)
        dbeta_all = jnp.zeros((CHUNK, HEAD_DIM), F32)
        dq, dk, dv, dgc, dbeta, dst = _dn_chunk_bwd_math(
            _heads_of(q_ref), _heads_of(k_ref), _heads_of(v_ref), _lanes_of(gblk), _lanes_of(bblk),
            _Heads(st_in_ref[0, h] for h in range(N_HEADS)), _heads_of(do_ref),
            _Heads(dst_ref[h] for h in range(N_HEADS)), _Heads(tinv_ref[0, h] for h in range(N_HEADS)))
        for h in range(N_HEADS):
            for part, val in enumerate((dq, dk, dv)):
                c0 = part * BR_WIDTH + h * HEAD_DIM
                dqkv_ref[:, c0:c0 + HEAD_DIM] = val.v[h]
            dg_all = jnp.where(lane == h, dgc.v[h], dg_all)
            dbeta_all = jnp.where(lane == h, dbeta.v[h], dbeta_all)
            dst_ref[h] = dst.v[h]
        dg_ref[...] = dg_all
        dbeta_ref[...] = dbeta_all

    blk = lambda off: pl.BlockSpec((CHUNK, BR_WIDTH), lambda c: (n - 1 - c, off))
    sc = pl.BlockSpec((CHUNK, HEAD_DIM), lambda c: (n - 1 - c, 0))
    outs = pl.pallas_call(
        body, name=name, grid=(n,),
        in_specs=[blk(0), blk(1), blk(2), sc, sc,
                  pl.BlockSpec((1, N_HEADS, HEAD_DIM, HEAD_DIM), lambda c: (n - 1 - c, 0, 0, 0)),
                  pl.BlockSpec((1, N_HEADS, CHUNK, CHUNK), lambda c: (n - 1 - c, 0, 0, 0)), blk(0)],
        out_specs=[pl.BlockSpec((CHUNK, 3 * BR_WIDTH), lambda c: (n - 1 - c, 0)), sc, sc],
        out_shape=[jax.ShapeDtypeStruct((s, 3 * BR_WIDTH), F32)] + [jax.ShapeDtypeStruct((s, HEAD_DIM), F32)] * 2,
        scratch_shapes=[pltpu.VMEM((N_HEADS, HEAD_DIM, HEAD_DIM), F32)],
        compiler_params=_cp("arbitrary"),
    )(qkv, qkv, qkv, gcs, beta, states, tinvs, do)
    return outs


def _hg_chunk_common(q, k, g):
    c, nb = CHUNK, CHUNK // SUB
    e_g = _exp(g)
    qd = q * e_g
    g_last = g[c - 1:c, :]
    e_t = _exp(g_last - g)
    kt = k * e_t
    tail = _exp(g_last)
    g_refs = [g[i * SUB:i * SUB + 1, :] for i in range(nb)]
    g_ref_rows = _hmap(lambda *rows: jnp.concatenate([jnp.broadcast_to(r, (SUB, r.shape[1])) for r in rows], axis=0), *g_refs)
    e_q = _exp(g - g_ref_rows)
    q_sc = q * e_q
    e_k = [_hmap(lambda gr, g_: jnp.exp(jnp.minimum(gr - g_, EXP_CLAMP)), g_refs[i], g) for i in range(nb)]
    k_sc_all = _rows(*[k * e_k[i] for i in range(nb)])
    row_blk = _iota2(c, 1, 0) // SUB
    masks = [(row_blk == i).astype(F32) for i in range(nb)]
    r_all = _nt(q_sc, k_sc_all)
    a_mat = r_all[:, 0:c] * masks[0]
    for i in range(1, nb):
        a_mat = a_mat + r_all[:, i * c:(i + 1) * c] * masks[i]
    a_mat = a_mat * _tri(c, "lower")
    return dict(e_g=e_g, qd=qd, e_t=e_t, kt=kt, tail=tail, q_sc=q_sc, k_sc_all=k_sc_all, e_q=e_q, e_k=e_k, masks=masks,
                a_mat=a_mat)


def _hg_chunk_fwd_math(q, k, v, g, stt):
    m = _hg_chunk_common(q, k, g)
    o = _nt(m["qd"], stt) + _nn(m["a_mat"], v)
    stt2 = stt * m["tail"] + _tn(v, m["kt"])
    return o, stt2


def _hg_chunk_bwd_math(q, k, v, g, stt, do, dstt2):
    c, nb = CHUNK, CHUNK // SUB
    m = _hg_chunk_common(q, k, g)
    stt2 = stt * m["tail"] + _tn(v, m["kt"])
    later = _sum(stt2 * dstt2, 0)
    dqd = _dot3(do, stt, 1, 0)
    dstt = _tn(do, m["qd"]) + dstt2 * m["tail"]
    d_a = _dot3(do, v, 1, 1) * _tri(c, "lower")
    dv = _tn(m["a_mat"], do) + _nt(m["kt"], dstt2)
    dkt = _dot3(v, dstt2, 1, 0)
    d_blk = _lanes(*[d_a * m["masks"][i] for i in range(nb)])
    dq = dqd * m["e_g"] + _dot3(d_blk, m["k_sc_all"], 1, 0) * m["e_q"]
    dks = _dot3(d_blk, m["q_sc"], 0, 0)
    dk = dkt * m["e_t"]
    for i in range(nb):
        dk = dk + dks[i * c:(i + 1) * c, :] * m["e_k"][i]
    db = q * dq - k * dk
    return dq, dk, dv, db, later, dstt


def _hg_chunk_fwd(proj, lb, *, name):
    s = proj.shape[0]
    n = s // CHUNK

    def body(hq_ref, hf_ref, v_ref, lb_ref, o_ref, st_out_ref, q_out, k_out, lf_out, st_ref):
        @pl.when(pl.program_id(0) == 0)
        def _():
            st_ref[...] = jnp.zeros_like(st_ref)

        f, lbv = hf_ref[...], lb_ref[...]
        q_all = _silu(hq_ref[...])
        k_all = (1.0 - lbv) * _sigmoid(-f)
        lf_all = jnp.log(lbv + (1.0 - lbv) * _sigmoid(f))
        q_out[...], k_out[...], lf_out[...] = q_all, k_all, lf_all
        st = _Heads(st_ref[h] for h in range(N_HEADS))
        g_all = _nn_exact(_tri(CHUNK, "lower"), lf_all)
        o, st2 = _hg_chunk_fwd_math(_heads_of(q_all), _heads_of(k_all), _heads_of(v_ref), _heads_of(g_all), st)
        for h in range(N_HEADS):
            st_out_ref[0, h] = st.v[h]
            o_ref[:, h * HEAD_DIM:(h + 1) * HEAD_DIM] = o.v[h]
            st_ref[h] = st2.v[h]

    blk = lambda off: pl.BlockSpec((CHUNK, BR_WIDTH), lambda c: (c, off))
    return pl.pallas_call(
        body, name=name, grid=(n,),
        in_specs=[blk(C_HQ // BR_WIDTH), blk(C_HF // BR_WIDTH), blk(C_HI // BR_WIDTH), pl.BlockSpec((1, BR_WIDTH), lambda c: (0, 0))],
        out_specs=[blk(0), pl.BlockSpec((1, N_HEADS, HEAD_DIM, HEAD_DIM), lambda c: (c, 0, 0, 0)), blk(0), blk(0), blk(0)],
        out_shape=[jax.ShapeDtypeStruct((s, BR_WIDTH), F32), jax.ShapeDtypeStruct((n, N_HEADS, HEAD_DIM, HEAD_DIM), F32)]
        + [jax.ShapeDtypeStruct((s, BR_WIDTH), F32)] * 3,
        scratch_shapes=[pltpu.VMEM((N_HEADS, HEAD_DIM, HEAD_DIM), F32)],
        compiler_params=_cp("arbitrary"),
    )(proj, proj, proj, lb)


def _hg_chunk_bwd(proj, lb, qh, kh, lf, states, do, *, name):
    s = proj.shape[0]
    n = s // CHUNK

    def body(hq_ref, hf_ref, v_ref, lb_ref, q_ref, k_ref, lf_ref, st_in_ref, do_ref, dhq_ref, dhf_ref, dhi_ref, dlb_ref, dst_ref):
        @pl.when(pl.program_id(0) == 0)
        def _():
            dst_ref[...] = jnp.zeros_like(dst_ref)
            dlb_ref[...] = jnp.zeros_like(dlb_ref)

        g_all = _nn_exact(_tri(CHUNK, "lower"), lf_ref[...])
        dq, dk, dv, db, later, dst = _hg_chunk_bwd_math(
            _heads_of(q_ref), _heads_of(k_ref), _heads_of(v_ref), _heads_of(g_all),
            _Heads(st_in_ref[0, h] for h in range(N_HEADS)), _heads_of(do_ref),
            _Heads(dst_ref[h] for h in range(N_HEADS)))
        dlf = _nn_exact(_tri(CHUNK, "upper"), jnp.concatenate(db.v, axis=1)) + jnp.concatenate(later.v, axis=1)
        dq_all, dk_all = jnp.concatenate(dq.v, axis=1), jnp.concatenate(dk.v, axis=1)
        f, lbv = hf_ref[...], lb_ref[...]
        dhq_ref[...] = (dq_all * _dsilu(hq_ref[...])).astype(BF16)
        sp, sn = _sigmoid(f), _sigmoid(-f)
        dlf_over = dlf / (lbv + (1.0 - lbv) * sp)
        dhf_ref[...] = (dlf_over * (1.0 - lbv) * sp * sn - dk_all * (1.0 - lbv) * sn * (1.0 - sn)).astype(BF16)
        dlb_ref[...] += jnp.sum(dlf_over * (1.0 - sp) - dk_all * sn, axis=0, keepdims=True)
        for h in range(N_HEADS):
            dhi_ref[:, h * HEAD_DIM:(h + 1) * HEAD_DIM] = dv.v[h].astype(BF16)
            dst_ref[h] = dst.v[h]

    blk = lambda off: pl.BlockSpec((CHUNK, BR_WIDTH), lambda c: (n - 1 - c, off))
    vec = pl.BlockSpec((1, BR_WIDTH), lambda c: (0, 0))
    return pl.pallas_call(
        body, name=name, grid=(n,),
        in_specs=[blk(C_HQ // BR_WIDTH), blk(C_HF // BR_WIDTH), blk(C_HI // BR_WIDTH), vec, blk(0), blk(0), blk(0),
                  pl.BlockSpec((1, N_HEADS, HEAD_DIM, HEAD_DIM), lambda c: (n - 1 - c, 0, 0, 0)), blk(0)],
        out_specs=[blk(0), blk(0), blk(0), vec],
        out_shape=[jax.ShapeDtypeStruct((s, BR_WIDTH), BF16)] * 3 + [jax.ShapeDtypeStruct((1, BR_WIDTH), F32)],
        scratch_shapes=[pltpu.VMEM((N_HEADS, HEAD_DIM, HEAD_DIM), F32)],
        compiler_params=_cp("arbitrary"),
    )(proj, proj, proj, lb, qh, kh, lf, states, do)


_ANY = pl.BlockSpec(memory_space=pl.ANY)
_MESH = pl.DeviceIdType.MESH


def _all_gather(x_local, *, name, after=()):
    n_after = len(after)

    def body(x_ref, *refs):
        out_ref, send_sems, recv_sems, local_sem = refs[n_after:]
        x, y, c = lax.axis_index("x"), lax.axis_index("y"), lax.axis_index("c")
        me, sibling = (x, y, c), (x, y, 1 - c)
        n1 = (x ^ (1 - c), y ^ c)
        n2 = (x ^ c, y ^ (1 - c))
        dg = (1 - x, 1 - y)

        def slot(px, py, pc):
            return out_ref.at[4 * px + 2 * py + pc]

        def copy(k, block, to, src=None):
            return pltpu.make_async_remote_copy(
                src_ref=slot(*block) if src is None else src, dst_ref=slot(*block),
                send_sem=send_sems.at[k], recv_sem=recv_sems.at[k], device_id=to, device_id_type=_MESH)

        mine = pltpu.make_async_copy(x_ref, slot(*me), local_sem)
        mine.start()
        first = [copy(0, me, sibling, src=x_ref), copy(1, me, (*n1, c), src=x_ref), copy(2, me, (*n2, c), src=x_ref)]
        for cp in first:
            cp.start()
        copy(2, (*n2, c), me).wait_recv()
        forward = copy(3, (*n2, c), (*n1, c))
        forward.start()
        passed = [copy(5, (*n2, c), sibling)]
        passed[0].start()
        copy(1, (*n1, c), me).wait_recv()
        passed.append(copy(4, (*n1, c), sibling))
        passed[1].start()
        copy(3, (*dg, c), me).wait_recv()
        passed.append(copy(6, (*dg, c), sibling))
        passed[2].start()
        copy(0, sibling, me).wait_recv()
        copy(4, (*n2, 1 - c), me).wait_recv()
        copy(5, (*n1, 1 - c), me).wait_recv()
        copy(6, (*dg, 1 - c), me).wait_recv()
        for cp in first + [forward] + passed:
            cp.wait_send()
        mine.wait()

    return pl.pallas_call(
        body, name=name, out_shape=jax.ShapeDtypeStruct((N_DEV,) + x_local.shape, x_local.dtype),
        in_specs=[_ANY] * (1 + n_after), out_specs=_ANY,
        scratch_shapes=[pltpu.SemaphoreType.DMA((7,)), pltpu.SemaphoreType.DMA((7,)), pltpu.SemaphoreType.DMA],
    )(x_local, *after)


_HBM = pl.BlockSpec(memory_space=pltpu.HBM)
_SEM = pl.BlockSpec(memory_space=pltpu.SEMAPHORE)
_EFFECT = pltpu.SideEffectType.DATAFLOW_SIDE_EFFECTING


def _peers():
    x, y, c = lax.axis_index("x"), lax.axis_index("y"), lax.axis_index("c")
    out = []
    for k in range(1, N_DEV):
        px, py, pc = x ^ ((k >> 2) & 1), y ^ ((k >> 1) & 1), c ^ (k & 1)
        out.append(((px, py, pc), 4 * px + 2 * py + pc))
    return 4 * x + 2 * y + c, out


def _push_copies(src_ref, land_ref, send_sems, recv_sems, broadcast):
    my, peers = _peers()
    pairs = []
    for k, (pos, idx) in enumerate(peers):
        src = src_ref if broadcast else src_ref.at[idx]
        send = pltpu.make_async_remote_copy(src_ref=src, dst_ref=land_ref.at[my], send_sem=send_sems.at[k],
                                            recv_sem=recv_sems.at[k], device_id=pos, device_id_type=_MESH)
        recv = pltpu.make_async_remote_copy(src_ref=src, dst_ref=land_ref.at[idx], send_sem=send_sems.at[k],
                                            recv_sem=recv_sems.at[k], device_id=pos, device_id_type=_MESH)
        pairs.append((send, recv))
    return pairs


def _push_start(src, land, *, broadcast, name, after=()):
    n_after = len(after)

    def body(src_ref, land_ref, *refs):
        send_sems, recv_sems, _, _, token = refs[n_after:]
        for send, _ in _push_copies(src_ref, land_ref, send_sems, recv_sems, broadcast):
            send.start()
        token[...] = jnp.zeros_like(token)

    return pl.pallas_call(
        body, name=name,
        out_shape=(pltpu.SemaphoreType.DMA((N_DEV - 1,)), pltpu.SemaphoreType.DMA((N_DEV - 1,)),
                   pltpu.HBM(src.shape, src.dtype), pltpu.HBM(land.shape, land.dtype), jax.ShapeDtypeStruct((8, 128), F32)),
        in_specs=(_HBM, _HBM) + (_ANY,) * n_after, out_specs=(_SEM, _SEM, _HBM, _HBM, pl.BlockSpec(memory_space=pltpu.VMEM)),
        input_output_aliases={0: 2, 1: 3}, compiler_params=pltpu.CompilerParams(has_side_effects=_EFFECT),
    )(pltpu.with_memory_space_constraint(src, pltpu.HBM), pltpu.with_memory_space_constraint(land, pltpu.HBM), *after)


def _push_wait(handle, after, *, broadcast, name):
    send_sems, recv_sems, src_thru, land_thru, _ = handle

    def body(src_ref, land_ref, send_sems, recv_sems, *rest):
        for send, recv in _push_copies(src_ref, land_ref, send_sems, recv_sems, broadcast):
            send.wait_send()
            recv.wait_recv()

    return pl.pallas_call(
        body, name=name,
        out_shape=(pltpu.HBM(src_thru.shape, src_thru.dtype), pltpu.HBM(land_thru.shape, land_thru.dtype)),
        in_specs=(_HBM, _HBM, _SEM, _SEM) + (_ANY,) * len(after), out_specs=(_HBM, _HBM),
        input_output_aliases={0: 0, 1: 1}, compiler_params=pltpu.CompilerParams(has_side_effects=_EFFECT),
    )(src_thru, land_thru, send_sems, recv_sems, *after)[1]


def _relay_copies(src_ref, land_ref, sems_a, sems_b):
    x, y, c = lax.axis_index("x"), lax.axis_index("y"), lax.axis_index("c")
    slot = lambda px, py, pc: land_ref.at[4 * px + 2 * py + pc]
    chips = [(1 - x, y), (x, 1 - y), (1 - x, 1 - y)]
    (send_a, recv_a), (send_b, recv_b) = sems_a, sems_b

    def copy(sems, k, src, dst_slot, to):
        return pltpu.make_async_remote_copy(src_ref=src, dst_ref=dst_slot, send_sem=sems[0].at[k], recv_sem=sems[1].at[k],
                                            device_id=to, device_id_type=_MESH)

    first = [copy((send_a, recv_a), 0, src_ref, slot(x, y, c), (x, y, 1 - c))]
    first += [copy((send_a, recv_a), 1 + j, src_ref, slot(x, y, c), (*chip, c)) for j, chip in enumerate(chips)]
    first_in = [copy((send_a, recv_a), 0, src_ref, slot(x, y, 1 - c), (x, y, 1 - c))]
    first_in += [copy((send_a, recv_a), 1 + j, src_ref, slot(*chip, c), (*chip, c)) for j, chip in enumerate(chips)]
    relay = [copy((send_b, recv_b), j, slot(*chip, c), slot(*chip, c), (x, y, 1 - c)) for j, chip in enumerate(chips)]
    relay_in = [copy((send_b, recv_b), j, slot(*chip, 1 - c), slot(*chip, 1 - c), (x, y, 1 - c)) for j, chip in enumerate(chips)]
    return first, first_in, relay, relay_in


def _relay_start(src, land, *, name, after=()):
    n_after = len(after)

    def body(src_ref, land_ref, *refs):
        send_a, recv_a, _, _, token = refs[n_after:]
        for cp in _relay_copies(src_ref, land_ref, (send_a, recv_a), (send_a, recv_a))[0]:
            cp.start()
        token[...] = jnp.zeros_like(token)

    send_a, recv_a, src_thru, land_thru, token = pl.pallas_call(
        body, name=name,
        out_shape=(pltpu.SemaphoreType.DMA((4,)), pltpu.SemaphoreType.DMA((4,)), pltpu.HBM(src.shape, src.dtype),
                   pltpu.HBM(land.shape, land.dtype), jax.ShapeDtypeStruct((8, 128), F32)),
        in_specs=(_HBM, _HBM) + (_ANY,) * n_after, out_specs=(_SEM, _SEM, _HBM, _HBM, pl.BlockSpec(memory_space=pltpu.VMEM)),
        input_output_aliases={0: 2, 1: 3}, compiler_params=pltpu.CompilerParams(has_side_effects=_EFFECT),
    )(pltpu.with_memory_space_constraint(src, pltpu.HBM), pltpu.with_memory_space_constraint(land, pltpu.HBM), *after)
    return (send_a, recv_a), src_thru, land_thru, token


def _relay_mid(handle, after, *, name):
    sems_a, src_thru, land_thru, _ = handle
    n_after = len(after)

    def body(src_ref, land_ref, send_a, recv_a, *refs):
        send_b, recv_b, _, _, token = refs[n_after:]
        _, first_in, relay, _ = _relay_copies(src_ref, land_ref, (send_a, recv_a), (send_b, recv_b))
        for j in range(3):
            first_in[1 + j].wait_recv()
            relay[j].start()
        token[...] = jnp.zeros_like(token)

    send_b, recv_b, src2, land2, token = pl.pallas_call(
        body, name=name,
        out_shape=(pltpu.SemaphoreType.DMA((3,)), pltpu.SemaphoreType.DMA((3,)), pltpu.HBM(src_thru.shape, src_thru.dtype),
                   pltpu.HBM(land_thru.shape, land_thru.dtype), jax.ShapeDtypeStruct((8, 128), F32)),
        in_specs=(_HBM, _HBM, _SEM, _SEM) + (_ANY,) * n_after,
        out_specs=(_SEM, _SEM, _HBM, _HBM, pl.BlockSpec(memory_space=pltpu.VMEM)),
        input_output_aliases={0: 2, 1: 3}, compiler_params=pltpu.CompilerParams(has_side_effects=_EFFECT),
    )(src_thru, land_thru, *sems_a, *after)
    return sems_a, (send_b, recv_b), src2, land2, token


def _relay_wait(handle, after, *, name):
    sems_a, sems_b, src_thru, land_thru, _ = handle

    def body(src_ref, land_ref, send_a, recv_a, send_b, recv_b, *rest):
        first, first_in, relay, relay_in = _relay_copies(src_ref, land_ref, (send_a, recv_a), (send_b, recv_b))
        first_in[0].wait_recv()
        for cp in relay_in:
            cp.wait_recv()
        for cp in first + relay:
            cp.wait_send()

    return pl.pallas_call(
        body, name=name,
        out_shape=(pltpu.HBM(src_thru.shape, src_thru.dtype), pltpu.HBM(land_thru.shape, land_thru.dtype)),
        in_specs=(_HBM, _HBM, _SEM, _SEM, _SEM, _SEM) + (_ANY,) * len(after), out_specs=(_HBM, _HBM),
        input_output_aliases={0: 0, 1: 1}, compiler_params=pltpu.CompilerParams(has_side_effects=_EFFECT),
    )(src_thru, land_thru, *sems_a, *sems_b, *after)[1]


def _adamw(parts, row_off, w, m, v, *, layer=0, n_layers=1, prev=None, name, tr):
    rows, c = w.shape
    r = rows // n_layers
    np_ = parts.shape[0]
    tr = min(tr, r)
    assert r % tr == 0 and row_off % tr == 0
    ob, lb = row_off // tr, layer * (r // tr)
    c1 = 1.0 - ADAM_B1 ** ADAM_STEP
    c2 = 1.0 - ADAM_B2 ** ADAM_STEP
    n_prev = 0 if prev is None else 4

    def body(p_ref, w_ref, m_ref, v_ref, *refs):
        g_ref, d_ref, nm_ref, nv_ref = refs[n_prev:]
        g = p_ref[0].astype(F32)
        for s in range(1, np_):
            g = g + p_ref[s].astype(F32)
        wv = w_ref[...]
        m2 = ADAM_B1 * m_ref[...] + (1.0 - ADAM_B1) * g
        v2 = ADAM_B2 * v_ref[...] + (1.0 - ADAM_B2) * jnp.square(g)
        m_hat = m2 / c1
        v_hat = v2 / c2
        g_ref[...] = g
        d_ref[...] = -ADAM_LR * (m_hat / (jnp.sqrt(v_hat) + ADAM_EPS) + ADAM_WD * wv)
        nm_ref[...] = m2
        nv_ref[...] = v2

    blk = pl.BlockSpec((tr, c), lambda i: (lb + i, 0))
    return pl.pallas_call(
        body, name=name, grid=(r // tr,),
        in_specs=[pl.BlockSpec((np_, tr, c), lambda i: (0, ob + i, 0)), blk, blk, blk] + [_ANY] * n_prev,
        out_specs=[blk] * 4, out_shape=[jax.ShapeDtypeStruct((rows, c), F32)] * 4,
        input_output_aliases={4 + i: i for i in range(n_prev)}, compiler_params=_cp("parallel"),
    )(parts, w, m, v, *(prev or ()))


def _sum_parts(parts, *, name, after=()):
    np_, r, c = parts.shape

    def body(p_ref, *refs):
        o_ref = refs[-1]
        g = p_ref[0]
        for s in range(1, np_):
            g = g + p_ref[s]
        o_ref[...] = g

    vmem = pl.BlockSpec(memory_space=pltpu.VMEM)
    return pl.pallas_call(body, name=name, in_specs=[vmem] + [_ANY] * len(after), out_specs=vmem,
                          out_shape=jax.ShapeDtypeStruct((r, c), F32))(parts, *after)


def _pack(arrs):
    rows = []
    for a in arrs:
        f = a.reshape(-1).astype(F32)
        pad = (-f.shape[0]) % 128
        rows.append(jnp.pad(f, (0, pad)).reshape(-1, 128))
    out = jnp.concatenate(rows, axis=0)
    return jnp.pad(out, ((0, (-out.shape[0]) % 8), (0, 0)))


def _unpack(packed, shapes):
    outs, r0 = [], 0
    for shp in shapes:
        n = 1
        for d in shp:
            n *= d
        nr = -(-n // 128)
        outs.append(packed[r0:r0 + nr].reshape(-1)[:n].reshape(shp))
        r0 += nr
    return outs


_WIN_PIECES = ((0, 4096, 0), (4112, 8208, 0), (4096, 4104, HEAD_DIM - N_HEADS), (4104, 4112, HEAD_DIM - N_HEADS))


RELAYOUT_TILE = 256
LAST_SPLIT = 4


def _win_from_shards(shards, *, name):
    k = shards.shape[1]
    tr = min(RELAYOUT_TILE, k)

    def body(x_ref, o_ref):
        cols = []
        for lo, hi, pad in _WIN_PIECES:
            for j in range(N_DEV):
                a, b = max(lo, j * SHARD_IN), min(hi, (j + 1) * SHARD_IN)
                if a < b:
                    cols.append(x_ref[j, :, a - j * SHARD_IN:b - j * SHARD_IN])
            if pad:
                cols.append(jnp.zeros((tr, pad), x_ref.dtype))
        o_ref[...] = jnp.concatenate(cols, axis=1)

    return pl.pallas_call(
        body, name=name, grid=(k // tr,), in_specs=[pl.BlockSpec((N_DEV, tr, SHARD_IN), lambda i: (0, i, 0))],
        out_specs=pl.BlockSpec((tr, N_PROJ), lambda i: (i, 0)), out_shape=jax.ShapeDtypeStruct((k, N_PROJ), shards.dtype),
        compiler_params=_cp("parallel"),
    )(shards)


def _win_to_shards(g, *, name):
    k = g.shape[0]
    tr = min(RELAYOUT_TILE, k)
    starts, off = [], 0
    for lo, hi, pad in _WIN_PIECES:
        starts.append((lo, hi, off))
        off += hi - lo + pad

    def body(g_ref, o_ref):
        for j in range(N_DEV):
            cols = []
            for lo, hi, off in sorted(starts):
                a, b = max(lo, j * SHARD_IN), min(hi, (j + 1) * SHARD_IN)
                if a < b:
                    cols.append(g_ref[:, off + a - lo:off + b - lo])
            o_ref[j] = jnp.concatenate(cols, axis=1)

    return pl.pallas_call(
        body, name=name, grid=(k // tr,), in_specs=[pl.BlockSpec((tr, N_PROJ), lambda i: (i, 0))],
        out_specs=pl.BlockSpec((N_DEV, tr, SHARD_IN), lambda i: (0, i, 0)),
        out_shape=jax.ShapeDtypeStruct((N_DEV, k, SHARD_IN), g.dtype), compiler_params=_cp("parallel"),
    )(g)


def _lower_bounds(logits):
    probs = jax.nn.softmax(logits.astype(F32), axis=0)
    return jnp.cumsum(probs, axis=0) - probs[0]


def _pad_lanes(vec8):
    return jnp.pad(vec8.reshape(1, N_HEADS), ((0, 0), (0, HEAD_DIM - N_HEADS)))


def kernel(x, p, norm_w, w_in, dn_conv_w, dn_A_log, dn_dt_bias, dn_norm_w, hg_lb_logits, hg_norm_w, w_out, w_ple_up, w_ple_gate, final_norm_w, loss_target, m_norm_w, m_w_in, m_dn_conv_w, m_dn_A_log, m_dn_dt_bias, m_dn_norm_w, m_hg_lb_logits, m_hg_norm_w, m_w_out, m_w_ple_up, m_w_ple_gate, m_final_norm_w, v_norm_w, v_w_in, v_dn_conv_w, v_dn_A_log, v_dn_dt_bias, v_dn_norm_w, v_hg_lb_logits, v_hg_norm_w, v_w_out, v_w_ple_up, v_w_ple_gate, v_final_norm_w):
    depth = norm_w.shape[0]
    my = 4 * lax.axis_index("x") + 2 * lax.axis_index("y") + lax.axis_index("c")
    h = x[0]
    tgt = loss_target[0]
    rows_out = D_MODEL // N_DEV
    up_rows = PLE_DIM * (D_MODEL // N_DEV) // D_MODEL
    g_off, u_off = rows_out, 2 * rows_out

    def own_slot(block):
        return lax.dynamic_update_index_in_dim(lax.empty((N_DEV,) + block.shape, block.dtype), block, my, 0)

    win_bf = w_in.astype(BF16)
    rest_bf = [jnp.concatenate([w_out[l], w_ple_gate[l], w_ple_up[l].reshape(up_rows, D_MODEL)], axis=0).astype(BF16)
               for l in range(depth)]
    conv_all = _all_gather(dn_conv_w, name="gather_conv_w")
    conv_full = conv_all.transpose(1, 2, 0, 3).reshape(depth, CONV_W, 3 * BR_WIDTH)
    win_all = {0: _all_gather(win_bf[0], name="gather_w_in_l0", after=[conv_all])}
    pending, relayed = {}, {}
    last = win_all[0]
    for l in range(depth):
        if l > 0:
            relayed["win", l] = _relay_start(win_bf[l], own_slot(win_bf[l]), after=[last], name=f"gather_w_in_l{l}_first")
            last = relayed["win", l][3]
        if l == 0:
            relayed["rest", l] = _relay_start(rest_bf[l], own_slot(rest_bf[l]), after=[last], name=f"gather_rest_l{l}_first")
            last = relayed["rest", l][3]
        else:
            pending["rest", l] = _push_start(rest_bf[l], own_slot(rest_bf[l]), broadcast=True, after=[last],
                                             name=f"gather_rest_l{l}_start")
            last = pending["rest", l][4]
    order_tok = last[0, 0]
    lbs = _lower_bounds(hg_lb_logits)

    saved = []
    weights = []
    for l in range(depth):
        tag = f"l{l}"
        if l > 0:
            win_all[l] = _relay_wait(relayed["win", l], [h], name=f"gather_w_in_{tag}_wait")
        wi = _win_from_shards(win_all[l], name=f"w_in_layout_{tag}")
        nw = norm_w[l] + order_tok if l == 0 else norm_w[l]
        hn = _rms_fwd(h, nw, name=f"rms_fwd_{tag}")
        proj = _mm(hn, wi, mode="nn", out_dtype=F32, name=f"mm_proj_{tag}")
        al, dt = _pad_lanes(dn_A_log[l]), _pad_lanes(dn_dt_bias[l])
        qkv = _dn_qkv_fwd(proj, conv_full[l], name=f"dn_qkv_fwd_{tag}")
        if ("rest", l) in relayed:
            relayed["rest", l] = _relay_mid(relayed["rest", l], [qkv], name=f"gather_rest_{tag}_relay")
            al = al + relayed["rest", l][4][0, 0]
        beta, gcs = _dn_gate_fwd(proj, al, dt, name=f"dn_gate_fwd_{tag}")
        o_dn, st_dn, tinv_dn = _dn_chunk_fwd(qkv, gcs, beta, name=f"dn_chunk_fwd_{tag}")
        lb = lbs[l].reshape(1, BR_WIDTH)
        o_hg, st_hg, qh, kh, lf = _hg_chunk_fwd(proj, lb, name=f"hg_chunk_fwd_{tag}")
        y_dn = _hnorm_fwd(o_dn, proj, C_Z, dn_norm_w[l], name=f"hnorm_dn_fwd_{tag}")
        y_hg = _hnorm_fwd(o_hg, proj, C_HZ, hg_norm_w[l], name=f"hnorm_hg_fwd_{tag}")
        y = jnp.concatenate([y_dn, y_hg], axis=1)
        if ("rest", l) in relayed:
            rest_all = _relay_wait(relayed["rest", l], [y], name=f"gather_rest_{tag}_wait")
        else:
            rest_all = _push_wait(pending["rest", l], [y], broadcast=True, name=f"gather_rest_{tag}_wait")
        w_out_rows, w_gate_rows = (0, rows_out), (g_off, rows_out)
        wu = rest_all[:, u_off:u_off + up_rows].reshape(N_DEV, PLE_DIM, D_MODEL // N_DEV).transpose(1, 0, 2).reshape(PLE_DIM, D_MODEL)
        weights.append((wi, rest_all, wu))
        h1 = _mm(y, rest_all, mode="nn", b_rows=w_out_rows, out_dtype=F32, res=h, name=f"mm_out_{tag}")
        pin = []
        if ("win", l + 1) in relayed:
            relayed["win", l + 1] = _relay_mid(relayed["win", l + 1], [h1], name=f"gather_w_in_l{l + 1}_relay")
            pin = [relayed["win", l + 1][4]]
        gp = _mm(h1, rest_all, mode="nn", b_rows=w_gate_rows, out_dtype=F32, after=pin, name=f"mm_gate_{tag}")
        up = _mm(p[l, 0], wu, mode="nn", out_dtype=F32, name=f"mm_up_{tag}")
        h2 = _ple_fwd(h1, gp, up, name=f"ple_fwd_{tag}")
        saved.append(dict(h=h, hn=hn, proj=proj, qkv=qkv, beta=beta, gcs=gcs, st_dn=st_dn, tinv_dn=tinv_dn, qh=qh, kh=kh, lf=lf,
                          st_hg=st_hg, o_dn=o_dn, o_hg=o_hg, y=y, h1=h1, gp=gp, up=up, al=al, dt=dt, lb=lb))
        h = h2

    loss_row, dh, d_final_w = _final_fwd_bwd(h, final_norm_w, tgt, name="final_norm_loss")

    d_norm_w, d_alog, d_dt, d_dn_nw, d_hg_nw, d_lb, d_conv = ([None] * depth for _ in range(7))
    sent = {}
    for l in reversed(range(depth)):
        wi, rest_all, wu = weights[l]
        sv = saved[l]
        tag = f"l{l}"
        dup, dgp = _ple_bwd(dh, sv["gp"], sv["up"], name=f"ple_bwd_{tag}")
        d_wu = _mm(p[l, 0], dup, mode="tn", out_dtype=BF16, name=f"mm_dwup_{tag}")
        d_wg = _mm(sv["h1"], dgp, mode="tn", out_dtype=BF16, name=f"mm_dwgate_{tag}")
        dh1 = _mm(dgp, rest_all, mode="nt", b_rows=(g_off, rows_out), out_dtype=F32, res=dh, name=f"mm_dh1_{tag}")
        d_wo = _mm(sv["y"], dh1, mode="tn", out_dtype=BF16, name=f"mm_dwout_{tag}")
        parts_rest = jnp.concatenate(
            [d_wo.reshape(N_DEV, rows_out, D_MODEL), d_wg.reshape(N_DEV, rows_out, D_MODEL),
             d_wu.reshape(PLE_DIM, N_DEV, D_MODEL // N_DEV).transpose(1, 0, 2).reshape(N_DEV, up_rows, D_MODEL)], axis=1)
        sent["rest", l] = _push_start(parts_rest, own_slot(parts_rest[my]), broadcast=False, name=f"exchange_rest_{tag}_start")
        dy = _mm(dh1, rest_all, mode="nt", b_rows=(0, rows_out), out_dtype=F32, name=f"mm_dy_{tag}")
        dn_nw = dn_norm_w[l] + sent["rest", l][4][0, 0]
        do_dn, dz_dn, d_dn_nw[l] = _hnorm_bwd(sv["o_dn"], sv["proj"], C_Z, dn_nw, dy, 0, name=f"hnorm_dn_bwd_{tag}")
        do_hg, dz_hg, d_hg_nw[l] = _hnorm_bwd(sv["o_hg"], sv["proj"], C_HZ, hg_norm_w[l], dy, BR_WIDTH, name=f"hnorm_hg_bwd_{tag}")
        dqkv, d_gc, dbeta = _dn_chunk_bwd(sv["qkv"], sv["gcs"], sv["beta"], sv["st_dn"], sv["tinv_dn"], do_dn, name=f"dn_chunk_bwd_{tag}")
        dqkv_pre, d_conv[l] = _dn_qkv_bwd(sv["proj"], conv_full[l], dqkv, name=f"dn_qkv_bwd_{tag}")
        db, da, d_alog[l], d_dt[l] = _dn_gate_bwd(sv["proj"], sv["al"], sv["dt"], dbeta, d_gc, name=f"dn_gate_bwd_{tag}")
        dhq, dhf, dhi, d_lb[l] = _hg_chunk_bwd(sv["proj"], sv["lb"], sv["qh"], sv["kh"], sv["lf"], sv["st_hg"], do_hg,
                                               name=f"hg_chunk_bwd_{tag}")
        dproj = jnp.concatenate([dqkv_pre, dz_dn, dhq, dhf, dhi, dz_hg, db, da], axis=1)
        def push_d_win(after):
            n_split = LAST_SPLIT if l == 0 else 1
            rows = D_MODEL // n_split
            handles = []
            for q in range(n_split):
                hn_q = sv["hn"] if n_split == 1 else sv["hn"][:, q * rows:(q + 1) * rows]
                sfx = tag if n_split == 1 else f"{tag}_{q}"
                d_win = _mm(hn_q, dproj, mode="tn", out_dtype=BF16, after=after, name=f"mm_dwin_{sfx}")
                parts_in = _win_to_shards(d_win, name=f"dw_in_shards_{sfx}")
                handles.append(_push_start(parts_in, own_slot(parts_in[my]), broadcast=False, after=after,
                                           name=f"exchange_w_in_{sfx}_start"))
                after = [handles[-1][4]]
            return handles

        if l == 0:
            small = _pack([loss_row, jnp.concatenate(d_norm_w[1:], axis=0), d_final_w,
                           jnp.stack([a[0, :N_HEADS] for a in d_alog]), jnp.stack([a[0, :N_HEADS] for a in d_dt]),
                           jnp.concatenate(d_dn_nw, axis=0), jnp.concatenate(d_hg_nw, axis=0), jnp.concatenate(d_lb, axis=0),
                           jnp.stack(d_conv)])
            small_all = _all_gather(small, name="gather_small")
        sent["win", l] = push_d_win([small_all] if l == 0 else [])
        dhn = _mm(dproj, wi, mode="nt", out_dtype=F32, tile_n=D_MODEL, after=[sent["win", l][-1][4]], name=f"mm_dhn_{tag}")
        dh, d_norm_w[l] = _rms_bwd(sv["h"], norm_w[l], dhn, dh1, name=f"rms_bwd_{tag}")
    grad_x = dh[None]

    small_shapes = [(1, 128), (depth - 1, D_MODEL), final_norm_w.shape, dn_A_log.shape, dn_dt_bias.shape, dn_norm_w.shape,
                    hg_norm_w.shape, hg_lb_logits.shape, (depth, CONV_W, 3 * BR_WIDTH)]
    tot = _unpack(_sum_parts(small_all, after=[grad_x], name="sum_small"), small_shapes)
    loss = tot[0][0, 0]
    g_lb = tot[7]
    g_logits = jax.vjp(_lower_bounds, hg_lb_logits)[1](g_lb)[0]
    g_conv = lax.dynamic_slice_in_dim(tot[8], my * (3 * BR_WIDTH // N_DEV), 3 * BR_WIDTH // N_DEV, axis=2)
    small_g = [g_conv, tot[3], tot[4], tot[5], g_logits, tot[6], tot[2]]
    small_w = [dn_conv_w, dn_A_log, dn_dt_bias, dn_norm_w, hg_lb_logits, hg_norm_w, final_norm_w]
    small_m = [m_dn_conv_w, m_dn_A_log, m_dn_dt_bias, m_dn_norm_w, m_hg_lb_logits, m_hg_norm_w, m_final_norm_w]
    small_v = [v_dn_conv_w, v_dn_A_log, v_dn_dt_bias, v_dn_norm_w, v_hg_lb_logits, v_hg_norm_w, v_final_norm_w]
    pk_w = _pack(small_w)
    res_small = _adamw(_pack(small_g)[None], 0, pk_w, _pack(small_m), _pack(small_v), name="adamw_small", tr=pk_w.shape[0])
    shapes_w = [a.shape for a in small_w]
    sg, sd, sm, sv_ = (_unpack(r, shapes_w) for r in res_small)

    r_win = r_wo = r_wg = r_wu = None
    done = [grad_x, res_small[0]]

    def flat(a, cols):
        return a.reshape(-1, cols)

    for l in reversed(range(depth)):
        tag = f"l{l}"
        land_rest = _push_wait(sent["rest", l], done, broadcast=False, name=f"exchange_rest_{tag}_wait")
        r_wo = _adamw(land_rest, 0, flat(w_out, D_MODEL), flat(m_w_out, D_MODEL), flat(v_w_out, D_MODEL), layer=l,
                      n_layers=depth, prev=r_wo, name=f"adamw_w_out_{tag}", tr=rows_out)
        r_wg = _adamw(land_rest, g_off, flat(w_ple_gate, D_MODEL), flat(m_w_ple_gate, D_MODEL), flat(v_w_ple_gate, D_MODEL),
                      layer=l, n_layers=depth, prev=r_wg, name=f"adamw_w_gate_{tag}", tr=rows_out)
        r_wu = _adamw(land_rest, u_off, flat(w_ple_up, D_MODEL), flat(m_w_ple_up, D_MODEL), flat(v_w_ple_up, D_MODEL),
                      layer=l, n_layers=depth, prev=r_wu, name=f"adamw_w_up_{tag}", tr=up_rows)
        done = [r_wo[0], r_wg[0], r_wu[0]]
    for l in reversed(range(depth)):
        tag = f"l{l}"
        if l == 0:
            nw0 = _sum_parts(_all_gather(_pack([d_norm_w[0]]), after=done, name="gather_norm_w"), name="sum_norm_w")
            g_norm_w = jnp.concatenate([_unpack(nw0, [(1, D_MODEL)])[0], tot[1]], axis=0)
            pk_nw = _pack([norm_w])
            r_nw = _adamw(_pack([g_norm_w])[None], 0, pk_nw, _pack([m_norm_w]), _pack([v_norm_w]), name="adamw_norm_w",
                          tr=pk_nw.shape[0])
            r_nw = [_unpack(r, [norm_w.shape])[0] for r in r_nw]
            done = [r_nw[0]]
        n_split = len(sent["win", l])
        for q, handle in enumerate(sent["win", l]):
            sfx = tag if n_split == 1 else f"{tag}_{q}"
            land_in = _push_wait(handle, done, broadcast=False, name=f"exchange_w_in_{sfx}_wait")
            r_win = _adamw(land_in, 0, flat(w_in, SHARD_IN), flat(m_w_in, SHARD_IN), flat(v_w_in, SHARD_IN),
                           layer=l * n_split + q, n_layers=depth * n_split, prev=r_win, name=f"adamw_w_in_{sfx}", tr=256)
            done = [r_win[0]]
    r_win = [o.reshape(w_in.shape) for o in r_win]
    r_wo = [o.reshape(w_out.shape) for o in r_wo]
    r_wg = [o.reshape(w_ple_gate.shape) for o in r_wg]
    r_wu = [o.reshape(w_ple_up.shape) for o in r_wu]

    def order(nw, small_list, big_in, big_out, big_up, big_gate):
        cw, al_, dt_, dnw, lbl, hnw, fw = small_list
        return [nw, big_in, cw, al_, dt_, dnw, lbl, hnw, big_out, big_up, big_gate, fw]

    outs = [loss, grad_x]
    for i, sl in enumerate((sg, sd, sm, sv_)):
        outs += order(r_nw[i], sl, r_win[i], r_wo[i], r_wu[i], r_wg[i])
    return tuple(outs)
```

```python
import functools

import jax
import jax.numpy as jnp
from jax import lax
from jax.experimental import pallas as pl
from jax.experimental.pallas import tpu as pltpu

F32 = jnp.float32
BF16 = jnp.bfloat16
HIGHEST = lax.Precision.HIGHEST

N_DEV = 8
D_MODEL = 2048
PLE_DIM = 256
HEAD_DIM = 128
N_HEADS = 8
BR_WIDTH = N_HEADS * HEAD_DIM
CHUNK = 64
SUB = 16
CONV_W = 4
NORM_EPS = 1e-6
L2_EPS = 1e-6
IN_WIDTH = 8208
SHARD_IN = IN_WIDTH // N_DEV
EXP_CLAMP = 80.0

C_QKV, C_Z, C_HQ, C_HF, C_HI, C_HZ, C_B, C_A, N_PROJ = 0, 3072, 4096, 5120, 6144, 7168, 8192, 8320, 8448

ADAM_LR, ADAM_B1, ADAM_B2, ADAM_EPS, ADAM_WD, ADAM_STEP = 0.001, 0.9, 0.999, 1e-08, 0.01, 10

VMEM_LIMIT = 48 * 1024 * 1024


def _cp(*sem):
    return pltpu.CompilerParams(dimension_semantics=sem, vmem_limit_bytes=VMEM_LIMIT)


class _Heads:
    def __init__(self, vals):
        self.v = tuple(vals)

    def __add__(self, o):
        return _hmap(lambda a, b: a + b, self, o)

    def __radd__(self, o):
        return _hmap(lambda a, b: b + a, self, o)

    def __sub__(self, o):
        return _hmap(lambda a, b: a - b, self, o)

    def __rsub__(self, o):
        return _hmap(lambda a, b: b - a, self, o)

    def __mul__(self, o):
        return _hmap(lambda a, b: a * b, self, o)

    def __rmul__(self, o):
        return _hmap(lambda a, b: b * a, self, o)

    def __neg__(self):
        return _hmap(lambda a: -a, self)

    def __getitem__(self, idx):
        return _hmap(lambda a: a[idx], self)


def _hmap(fn, *args):
    n = next((len(a.v) for a in args if isinstance(a, _Heads)), None)
    if n is None:
        return fn(*args)
    return _Heads(fn(*[a.v[i] if isinstance(a, _Heads) else a for a in args]) for i in range(n))


def _dot(a, b, ca, cb):
    return _hmap(lambda x, y: lax.dot_general(x.astype(BF16), y.astype(BF16), (((ca,), (cb,)), ((), ())),
                                              preferred_element_type=F32), a, b)


def _nn(a, b):
    return _dot(a, b, 1, 0)


def _nt(a, b):
    return _dot(a, b, 1, 1)


def _tn(a, b):
    return _dot(a, b, 0, 0)


def _split(a):
    hi = _hmap(lambda x: x.astype(BF16), a)
    return hi, _hmap(lambda x, h: (x - h.astype(F32)).astype(BF16), a, hi)


def _dot3(a, b, ca, cb):
    ah, al = _split(a)
    bh, bl = _split(b)
    return _dot(ah, bh, ca, cb) + (_dot(ah, bl, ca, cb) + _dot(al, bh, ca, cb))


def _nn_exact(a, b):
    return _hmap(lambda y: lax.dot_general(a, y, (((1,), (0,)), ((), ())), precision=HIGHEST,
                                           preferred_element_type=F32), b)


def _exp(x):
    return _hmap(jnp.exp, x)


def _sum(x, axis):
    return _hmap(lambda a: jnp.sum(a, axis=axis, keepdims=True), x)


def _sigmoid(x):
    return jax.nn.sigmoid(x)


def _silu(x):
    return x * _sigmoid(x)


def _dsilu(x):
    s = _sigmoid(x)
    return s * (1.0 + x * (1.0 - s))


def _silu_and_grad(x):
    s = _sigmoid(x)
    return x * s, s * (1.0 + x * (1.0 - s))


def _softplus(x):
    return jnp.maximum(x, 0.0) + jnp.log(1.0 + jnp.exp(-jnp.abs(x)))


def _iota2(n, m, axis):
    return lax.broadcasted_iota(jnp.int32, (n, m), axis)


def _col2row(col, eye):
    return _hmap(lambda c: jnp.sum(eye * c, axis=0, keepdims=True), col)


def _row2col(row, eye):
    return _hmap(lambda r: jnp.sum(eye * r, axis=1, keepdims=True), row)


def _pick_lane(block, lane_idx):
    lane = _iota2(block.shape[0], block.shape[1], 1)
    return jnp.sum(jnp.where(lane == lane_idx, block, 0.0), axis=1, keepdims=True)


MM_TILE_M, MM_TILE_N, MM_TILE_K = 1024, 1408, 2048


def _tile(dim, cap):
    if dim <= cap:
        return dim
    t = cap - cap % 128
    while dim % t:
        t -= 128
    return t


def _mm(a, b, *, mode, out_dtype, res=None, after=(), b_rows=None, tile_n=MM_TILE_N, name):
    b_mat_rows = b.shape[0] if b_rows is None else N_DEV * b_rows[1]
    if mode == "nn":
        (m, kd), n = a.shape, b.shape[-1]
        assert kd == b_mat_rows
    elif mode == "nt":
        (m, kd), n = a.shape, b_mat_rows
    else:
        (kd, m), n = a.shape, b.shape[-1]
    tm, tn, tk = _tile(m, MM_TILE_M), _tile(n, tile_n), _tile(kd, MM_TILE_K)
    assert m % tm == 0 and n % tn == 0 and kd % tk == 0, (m, n, kd, tm, tn, tk)
    nk = kd // tk
    ca, cb = {"nn": (1, 0), "nt": (1, 1), "tn": (0, 0)}[mode]

    def body(*refs):
        a_ref, b_ref = refs[:2]
        r_ref = None if res is None else refs[2]
        o_ref, acc_ref = refs[-2:]
        k = pl.program_id(2)

        @pl.when(k == 0)
        def _():
            acc_ref[...] = jnp.zeros_like(acc_ref)

        b_tile = b_ref[...]
        if b_rows is not None:
            b_tile = b_tile.reshape(-1, b_tile.shape[-1])
        acc_ref[...] += _dot(a_ref[...], b_tile, ca, cb)

        @pl.when(k == nk - 1)
        def _():
            out = acc_ref[...]
            if r_ref is not None:
                out = out + r_ref[...].astype(F32)
            o_ref[...] = out.astype(o_ref.dtype)

    a_spec = pl.BlockSpec((tk, tm), lambda i, j, k: (k, i)) if mode == "tn" else pl.BlockSpec((tm, tk), lambda i, j, k: (i, k))
    if b_rows is None:
        b_spec = pl.BlockSpec((tn, tk), lambda i, j, k: (j, k)) if mode == "nt" else pl.BlockSpec((tk, tn), lambda i, j, k: (k, j))
    else:
        first, count = b_rows
        assert first % count == 0 and mode in ("nn", "nt")
        rb = first // count
        if mode == "nn":
            assert tk == kd
            b_spec = pl.BlockSpec((N_DEV, count, tn), lambda i, j, k: (0, rb, j))
        else:
            assert tn % count == 0
            b_spec = pl.BlockSpec((tn // count, count, tk), lambda i, j, k: (j, rb, k))
    o_spec = pl.BlockSpec((tm, tn), lambda i, j, k: (i, j))
    in_specs = [a_spec, b_spec] + ([o_spec] if res is not None else []) + [pl.BlockSpec(memory_space=pl.ANY)] * len(after)
    args = (a, b) + ((res,) if res is not None else ()) + tuple(after)
    return pl.pallas_call(
        body, name=name, grid=(m // tm, n // tn, nk), in_specs=in_specs, out_specs=o_spec,
        out_shape=jax.ShapeDtypeStruct((m, n), out_dtype),
        scratch_shapes=[pltpu.VMEM((tm, tn), F32)],
        compiler_params=_cp("parallel", "parallel", "arbitrary"),
    )(*args)


ROW_TILE = 256


def _rms_fwd(h, w, *, name):
    s, d = h.shape
    tr = min(ROW_TILE, s)

    def body(h_ref, w_ref, o_ref):
        x = h_ref[...]
        r = lax.rsqrt(jnp.mean(x * x, axis=-1, keepdims=True) + NORM_EPS)
        o_ref[...] = (x * r * w_ref[...]).astype(o_ref.dtype)

    return pl.pallas_call(
        body, name=name, grid=(s // tr,),
        in_specs=[pl.BlockSpec((tr, d), lambda i: (i, 0)), pl.BlockSpec((1, d), lambda i: (0, 0))],
        out_specs=pl.BlockSpec((tr, d), lambda i: (i, 0)),
        out_shape=jax.ShapeDtypeStruct((s, d), BF16), compiler_params=_cp("parallel"),
    )(h, w.reshape(1, d))


def _rms_bwd_math(x, w, dy):
    d = x.shape[-1]
    r = lax.rsqrt(jnp.mean(x * x, axis=-1, keepdims=True) + NORM_EPS)
    gw = dy * w
    dx = r * gw - x * ((r * r * r) * (jnp.sum(gw * x, axis=-1, keepdims=True) / d))
    return dx, dy * x * r


def _rms_bwd(h, w, dhn, res, *, name):
    s, d = h.shape
    tr = min(ROW_TILE, s)

    def body(h_ref, w_ref, g_ref, r_ref, dh_ref, dw_ref):
        @pl.when(pl.program_id(0) == 0)
        def _():
            dw_ref[...] = jnp.zeros_like(dw_ref)

        dx, dwt = _rms_bwd_math(h_ref[...], w_ref[...], g_ref[...])
        dh_ref[...] = r_ref[...] + dx
        dw_ref[...] += jnp.sum(dwt, axis=0, keepdims=True)

    row = pl.BlockSpec((tr, d), lambda i: (i, 0))
    vec = pl.BlockSpec((1, d), lambda i: (0, 0))
    return pl.pallas_call(
        body, name=name, grid=(s // tr,), in_specs=[row, vec, row, row], out_specs=[row, vec],
        out_shape=[jax.ShapeDtypeStruct((s, d), F32), jax.ShapeDtypeStruct((1, d), F32)],
        compiler_params=_cp("arbitrary"),
    )(h, w.reshape(1, d), dhn, res)


def _final_fwd_bwd(h, w, tgt, *, name):
    s, d = h.shape
    tr = min(ROW_TILE, s)

    def body(h_ref, w_ref, t_ref, loss_ref, dh_ref, dw_ref):
        @pl.when(pl.program_id(0) == 0)
        def _():
            loss_ref[...] = jnp.zeros_like(loss_ref)
            dw_ref[...] = jnp.zeros_like(dw_ref)

        x = h_ref[...]
        wv = w_ref[...]
        r = lax.rsqrt(jnp.mean(x * x, axis=-1, keepdims=True) + NORM_EPS)
        err = x * r * wv - t_ref[...]
        row_loss = jnp.mean(err * err, axis=-1, keepdims=True)
        loss_ref[...] += 0.5 * jnp.sum(row_loss, axis=0, keepdims=True)
        dx, dwt = _rms_bwd_math(x, wv, err / d)
        dh_ref[...] = dx
        dw_ref[...] += jnp.sum(dwt, axis=0, keepdims=True)

    row = pl.BlockSpec((tr, d), lambda i: (i, 0))
    vec = pl.BlockSpec((1, d), lambda i: (0, 0))
    return pl.pallas_call(
        body, name=name, grid=(s // tr,), in_specs=[row, vec, row],
        out_specs=[pl.BlockSpec((1, 128), lambda i: (0, 0)), row, vec],
        out_shape=[jax.ShapeDtypeStruct((1, 128), F32), jax.ShapeDtypeStruct((s, d), F32),
                   jax.ShapeDtypeStruct((1, d), F32)],
        compiler_params=_cp("arbitrary"),
    )(h, w.reshape(1, d), tgt)


def _ple_fwd(h1, gate_pre, up, *, name):
    s, d = h1.shape
    tr = min(ROW_TILE, s)

    def body(h_ref, g_ref, u_ref, o_ref):
        o_ref[...] = h_ref[...] + u_ref[...] * _sigmoid(g_ref[...])

    row = pl.BlockSpec((tr, d), lambda i: (i, 0))
    return pl.pallas_call(body, name=name, grid=(s // tr,), in_specs=[row, row, row], out_specs=row,
                          out_shape=jax.ShapeDtypeStruct((s, d), F32), compiler_params=_cp("parallel"))(h1, gate_pre, up)


def _ple_bwd(dh2, gate_pre, up, *, name):
    s, d = dh2.shape
    tr = min(ROW_TILE, s)

    def body(d_ref, g_ref, u_ref, dup_ref, dgp_ref):
        dh = d_ref[...]
        gate = _sigmoid(g_ref[...])
        dup_ref[...] = (dh * gate).astype(BF16)
        dgp_ref[...] = (dh * u_ref[...] * gate * (1.0 - gate)).astype(BF16)

    row = pl.BlockSpec((tr, d), lambda i: (i, 0))
    return pl.pallas_call(body, name=name, grid=(s // tr,), in_specs=[row, row, row], out_specs=[row, row],
                          out_shape=[jax.ShapeDtypeStruct((s, d), BF16)] * 2, compiler_params=_cp("parallel"))(dh2, gate_pre, up)


def _head_norm_fwd(o, z, w):
    return _hmap(lambda x, zz: (x * lax.rsqrt(jnp.mean(x * x, axis=-1, keepdims=True) + NORM_EPS) * w * _silu(zz)).astype(BF16),
                 o, z)


def _head_norm_bwd(o, z, w, dy):
    dos, dzs, dw = [], [], jnp.zeros((1, HEAD_DIM), F32)
    for x, zz, g in zip(o.v, z.v, dy.v):
        r = lax.rsqrt(jnp.mean(x * x, axis=-1, keepdims=True) + NORM_EPS)
        silu_z, dsilu_z = _silu_and_grad(zz)
        don = g * silu_z
        dzs.append((g * (x * r * w) * dsilu_z).astype(BF16))
        gw = don * w
        dos.append(r * gw - x * ((r * r * r) * (jnp.sum(gw * x, axis=-1, keepdims=True) / HEAD_DIM)))
        dw = dw + jnp.sum(don * x * r, axis=0, keepdims=True)
    return _Heads(dos), _Heads(dzs), dw


def _conv_silu(x, w, s):
    row = _iota2(s, x.shape[1], 0)
    c = w[CONV_W - 1:CONV_W, :] * x
    for k in range(1, CONV_W):
        c = c + w[CONV_W - 1 - k:CONV_W - k, :] * jnp.where(row >= k, pltpu.roll(x, k, 0), 0.0)
    return c


def _dn_qkv_fwd(proj, conv_w, *, name):
    s = proj.shape[0]
    nb = 3 * N_HEADS

    def body(x_ref, w_ref, o_ref):
        j = pl.program_id(0)
        sv = _silu(_conv_silu(x_ref[...], w_ref[...], s))
        r = lax.rsqrt(jnp.sum(sv * sv, axis=-1, keepdims=True) + L2_EPS)
        scale = jnp.where(j < N_HEADS, HEAD_DIM ** -0.5, 1.0).astype(F32)
        o_ref[...] = jnp.where(j < 2 * N_HEADS, sv * r * scale, sv)

    return pl.pallas_call(
        body, name=name, grid=(nb,),
        in_specs=[pl.BlockSpec((s, HEAD_DIM), lambda j: (0, j)), pl.BlockSpec((CONV_W, HEAD_DIM), lambda j: (0, j))],
        out_specs=pl.BlockSpec((s, HEAD_DIM), lambda j: (0, j)),
        out_shape=jax.ShapeDtypeStruct((s, 3 * BR_WIDTH), F32), compiler_params=_cp("parallel"),
    )(proj, conv_w)


def _dn_qkv_bwd(proj, conv_w, dqkv, *, name):
    s = proj.shape[0]
    nb = 3 * N_HEADS

    def body(x_ref, w_ref, g_ref, dx_ref, dw_ref):
        j = pl.program_id(0)
        x, w, g = x_ref[...], w_ref[...], g_ref[...]
        c = _conv_silu(x, w, s)
        sv, dsv = _silu_and_grad(c)
        r = lax.rsqrt(jnp.sum(sv * sv, axis=-1, keepdims=True) + L2_EPS)
        scale = jnp.where(j < N_HEADS, HEAD_DIM ** -0.5, 1.0).astype(F32)
        ds_n = scale * (r * g - sv * ((r * r * r) * jnp.sum(g * sv, axis=-1, keepdims=True)))
        dc = jnp.where(j < 2 * N_HEADS, ds_n, g) * dsv
        row = _iota2(s, HEAD_DIM, 0)
        dx = w[CONV_W - 1:CONV_W, :] * dc
        dws = [jnp.sum(dc * x, axis=0, keepdims=True)]
        for k in range(1, CONV_W):
            dc_ahead = jnp.where(row < s - k, pltpu.roll(dc, s - k, 0), 0.0)
            dx = dx + w[CONV_W - 1 - k:CONV_W - k, :] * dc_ahead
            dws.append(jnp.sum(dc_ahead * x, axis=0, keepdims=True))
        dx_ref[...] = dx.astype(BF16)
        for k in range(CONV_W):
            dw_ref[CONV_W - 1 - k:CONV_W - k, :] = dws[k]

    blk = pl.BlockSpec((s, HEAD_DIM), lambda j: (0, j))
    wblk = pl.BlockSpec((CONV_W, HEAD_DIM), lambda j: (0, j))
    return pl.pallas_call(
        body, name=name, grid=(nb,), in_specs=[blk, wblk, blk], out_specs=[blk, wblk],
        out_shape=[jax.ShapeDtypeStruct((s, 3 * BR_WIDTH), BF16), jax.ShapeDtypeStruct((CONV_W, 3 * BR_WIDTH), F32)],
        compiler_params=_cp("parallel"),
    )(proj, conv_w, dqkv)


def _tri(n, kind):
    r, c = _iota2(n, n, 0), _iota2(n, n, 1)
    if kind == "lower":
        return (r >= c).astype(F32)
    if kind == "upper":
        return (r <= c).astype(F32)
    return (r == c).astype(F32)


GATE_TILE = 512


def _dn_gate_fwd(proj, a_log, dt_bias, *, name):
    s = proj.shape[0]
    tr = min(GATE_TILE, s)

    def body(b_ref, a_ref, al_ref, dt_ref, beta_ref, g_ref):
        beta_ref[...] = _sigmoid(b_ref[...])
        g = -jnp.exp(al_ref[...]) * _softplus(a_ref[...] + dt_ref[...])
        low = _tri(CHUNK, "lower")
        for c in range(tr // CHUNK):
            rows = slice(c * CHUNK, (c + 1) * CHUNK)
            g_ref[rows, :] = _nn_exact(low, g[rows, :])

    blk = lambda cb: pl.BlockSpec((tr, HEAD_DIM), lambda i: (i, cb))
    vec = pl.BlockSpec((1, HEAD_DIM), lambda i: (0, 0))
    out = pl.BlockSpec((tr, HEAD_DIM), lambda i: (i, 0))
    return pl.pallas_call(
        body, name=name, grid=(s // tr,), in_specs=[blk(C_B // HEAD_DIM), blk(C_A // HEAD_DIM), vec, vec],
        out_specs=[out, out], out_shape=[jax.ShapeDtypeStruct((s, HEAD_DIM), F32)] * 2, compiler_params=_cp("parallel"),
    )(proj, proj, a_log, dt_bias)


def _dn_gate_bwd(proj, a_log, dt_bias, dbeta, d_g, *, name):
    s = proj.shape[0]
    tr = min(GATE_TILE, s)

    def body(b_ref, a_ref, al_ref, dt_ref, dbeta_ref, dG_ref, db_ref, da_ref, dal_ref, ddt_ref):
        @pl.when(pl.program_id(0) == 0)
        def _():
            dal_ref[...] = jnp.zeros_like(dal_ref)
            ddt_ref[...] = jnp.zeros_like(ddt_ref)

        beta = _sigmoid(b_ref[...])
        db_ref[...] = (dbeta_ref[...] * beta * (1.0 - beta)).astype(BF16)
        pre = a_ref[...] + dt_ref[...]
        neg_ea = -jnp.exp(al_ref[...])
        up = _tri(CHUNK, "upper")
        d_g = dG_ref[...]
        dg = jnp.concatenate([_nn_exact(up, d_g[c * CHUNK:(c + 1) * CHUNK, :]) for c in range(tr // CHUNK)], axis=0)
        da = dg * neg_ea * _sigmoid(pre)
        da_ref[...] = da.astype(BF16)
        ddt_ref[...] += jnp.sum(da, axis=0, keepdims=True)
        dal_ref[...] += jnp.sum(dg * neg_ea * _softplus(pre), axis=0, keepdims=True)

    blk = lambda cb: pl.BlockSpec((tr, HEAD_DIM), lambda i: (i, cb))
    vec = pl.BlockSpec((1, HEAD_DIM), lambda i: (0, 0))
    io = pl.BlockSpec((tr, HEAD_DIM), lambda i: (i, 0))
    return pl.pallas_call(
        body, name=name, grid=(s // tr,),
        in_specs=[blk(C_B // HEAD_DIM), blk(C_A // HEAD_DIM), vec, vec, io, io], out_specs=[io, io, vec, vec],
        out_shape=[jax.ShapeDtypeStruct((s, HEAD_DIM), BF16)] * 2 + [jax.ShapeDtypeStruct((1, HEAD_DIM), F32)] * 2,
        compiler_params=_cp("arbitrary"),
    )(proj, proj, a_log, dt_bias, dbeta, d_g)


def _unit_lower_inverse(a_strict, eye):
    x = -a_strict
    t = x + eye
    p = x
    n = 2
    while n < CHUNK:
        p = _nn(p, p)
        t = t + _nn(t, p)
        n *= 2
    return t


def _rows(*xs):
    return _hmap(lambda *a: jnp.concatenate(a, axis=0), *xs)


def _lanes(*xs):
    return _hmap(lambda *a: jnp.concatenate(a, axis=1), *xs)


def _dn_chunk_common(q, k, v, gc, beta, st, with_qd_state, t_inv=None):
    c, d = CHUNK, HEAD_DIM
    eye = _tri(c, "eye")
    low = _tri(c, "lower")
    strict = low - eye
    grow = _col2row(gc, eye)
    dec = _hmap(lambda g_, gr: low * jnp.exp(low * (g_ - gr)), gc, grow)
    kb = k * beta
    kq = _nt(_rows(kb, q), k)
    a_mat = kq[0:c, :] * dec * strict
    qk = kq[c:2 * c, :] * dec
    if t_inv is None:
        t_inv = _unit_lower_inverse(a_mat, eye)
    e_g = _exp(gc)
    qd = q * e_g
    uw = _nn(t_inv, _lanes(v * beta, kb * e_g))
    u, w = uw[:, 0:d], uw[:, d:2 * d]
    last = (_iota2(c, 1, 0) == c - 1).astype(F32)
    g_last = _sum(gc * last, 0)
    e_t = _exp(g_last - gc)
    kt = k * e_t
    tail = _exp(g_last)
    if with_qd_state:
        ws = _nn(_rows(w, qd), st)
        vn, qds = u - ws[0:c, :], ws[c:2 * c, :]
    else:
        vn, qds = u - _nn(w, st), None
    return dict(eye=eye, low=low, strict=strict, dec=dec, kb=kb, a_mat=a_mat, t_inv=t_inv, e_g=e_g, u=u, w=w, uw=uw,
                qk=qk, qd=qd, qds=qds, last=last, e_t=e_t, kt=kt, tail=tail, vn=vn)


def _dn_chunk_fwd_math(q, k, v, gc, beta, st):
    m = _dn_chunk_common(q, k, v, gc, beta, st, True)
    o = m["qds"] + _nn(m["qk"], m["vn"])
    st2 = st * m["tail"] + _tn(m["kt"], m["vn"])
    return o, st2, m["t_inv"]


def _dn_chunk_bwd_math(q, k, v, gc, beta, st, do, dst2, t_inv=None):
    c, d = CHUNK, HEAD_DIM
    m = _dn_chunk_common(q, k, v, gc, beta, st, False, t_inv)
    eye, low, strict = m["eye"], m["low"], m["strict"]
    dvn = _tn(m["qk"], do) + _nn(m["kt"], dst2)
    dqk = _nt(do, m["vn"]) * low
    both = _rows(do, dvn)
    ds_both = _nt(both, st)
    dqd, dw = ds_both[0:c, :], -ds_both[c:2 * c, :]
    dst = _tn(_rows(m["qd"], -m["w"]), both) + dst2 * m["tail"]
    dkt = _nt(m["vn"], dst2)
    dtail = _sum(_sum(st * dst2, 1), 0)
    dvb_dkg = _tn(m["t_inv"], _lanes(dvn, dw))
    dvb, dkg = dvb_dkg[:, 0:d], dvb_dkg[:, d:2 * d]
    d_a = _nt(dvb_dkg, m["uw"]) * (-strict)
    dkk = d_a * m["dec"]
    dp = dqk * m["dec"]
    dpk = _rows(dp, dkk)
    dq_dkb = _nn(dpk, k)
    dq = dq_dkb[0:c, :] + dqd * m["e_g"]
    dkb = dq_dkb[c:2 * c, :] + dkg * m["e_g"]
    dk = _tn(dpk, _rows(q, m["kb"])) + dkb * beta + dkt * m["e_t"]
    dv = dvb * beta
    dbeta = _sum(dvb * v + dkb * k, 1)
    de_g = _sum(dkg * m["kb"] + dqd * q, 1)
    de_t = _sum(dkt * k, 1)
    mm = d_a * m["a_mat"] + dqk * m["qk"]
    dgc = (_sum(mm, 1) - _row2col(_sum(mm, 0), eye) + de_g * m["e_g"] - de_t * m["e_t"]
           + (_sum(de_t * m["e_t"], 0) + dtail * m["tail"]) * m["last"])
    return dq, dk, dv, dgc, dbeta, dst


def _heads_of(ref):
    return _Heads(ref[:, h * HEAD_DIM:(h + 1) * HEAD_DIM] for h in range(N_HEADS))


def _lanes_of(block):
    return _Heads(_pick_lane(block, h) for h in range(N_HEADS))


def _dn_chunk_fwd(qkv, gcs, beta, proj, norm_w, *, name):
    s = qkv.shape[0]
    n = s // CHUNK

    def body(q_ref, k_ref, v_ref, g_ref, b_ref, z_ref, w_ref, o_ref, st_out_ref, tinv_ref, y_ref, st_ref):
        @pl.when(pl.program_id(0) == 0)
        def _():
            st_ref[...] = jnp.zeros_like(st_ref)

        gblk, bblk = g_ref[...], b_ref[...]
        st = _Heads(st_ref[h] for h in range(N_HEADS))
        o, st2, t_inv = _dn_chunk_fwd_math(_heads_of(q_ref), _heads_of(k_ref), _heads_of(v_ref), _lanes_of(gblk),
                                           _lanes_of(bblk), st)
        y = _head_norm_fwd(o, _heads_of(z_ref), w_ref[...])
        for h in range(N_HEADS):
            st_out_ref[0, h] = st.v[h]
            tinv_ref[0, h] = t_inv.v[h].astype(BF16)
            o_ref[:, h * HEAD_DIM:(h + 1) * HEAD_DIM] = o.v[h]
            y_ref[:, h * HEAD_DIM:(h + 1) * HEAD_DIM] = y.v[h]
            st_ref[h] = st2.v[h]

    blk = lambda off: pl.BlockSpec((CHUNK, BR_WIDTH), lambda c: (c, off))
    sc = pl.BlockSpec((CHUNK, HEAD_DIM), lambda c: (c, 0))
    return pl.pallas_call(
        body, name=name, grid=(n,),
        in_specs=[blk(0), blk(1), blk(2), sc, sc, blk(C_Z // BR_WIDTH), pl.BlockSpec((1, HEAD_DIM), lambda c: (0, 0))],
        out_specs=[blk(0), pl.BlockSpec((1, N_HEADS, HEAD_DIM, HEAD_DIM), lambda c: (c, 0, 0, 0)),
                   pl.BlockSpec((1, N_HEADS, CHUNK, CHUNK), lambda c: (c, 0, 0, 0)), blk(0)],
        out_shape=[jax.ShapeDtypeStruct((s, BR_WIDTH), F32), jax.ShapeDtypeStruct((n, N_HEADS, HEAD_DIM, HEAD_DIM), F32),
                   jax.ShapeDtypeStruct((n, N_HEADS, CHUNK, CHUNK), BF16), jax.ShapeDtypeStruct((s, BR_WIDTH), BF16)],
        scratch_shapes=[pltpu.VMEM((N_HEADS, HEAD_DIM, HEAD_DIM), F32)],
        compiler_params=_cp("arbitrary"),
    )(qkv, qkv, qkv, gcs, beta, proj, norm_w.reshape(1, HEAD_DIM))


def _dn_chunk_bwd(qkv, gcs, beta, states, tinvs, o, proj, norm_w, dy, *, name):
    s = qkv.shape[0]
    n = s // CHUNK

    def body(q_ref, k_ref, v_ref, g_ref, b_ref, st_in_ref, tinv_ref, o_ref, z_ref, w_ref, dy_ref,
             dqkv_ref, dg_ref, dbeta_ref, dz_ref, dw_ref, dst_ref):
        @pl.when(pl.program_id(0) == 0)
        def _():
            dst_ref[...] = jnp.zeros_like(dst_ref)
            dw_ref[...] = jnp.zeros_like(dw_ref)

        do, dz, dw = _head_norm_bwd(_heads_of(o_ref), _heads_of(z_ref), w_ref[...], _heads_of(dy_ref))
        dw_ref[...] += dw

        gblk, bblk = g_ref[...], b_ref[...]
        lane = _iota2(CHUNK, HEAD_DIM, 1)
        dg_all = jnp.zeros((CHUNK, HEAD_DIM), F32)
        dbeta_all = jnp.zeros((CHUNK, HEAD_DIM), F32)
        dq, dk, dv, dgc, dbeta, dst = _dn_chunk_bwd_math(
            _heads_of(q_ref), _heads_of(k_ref), _heads_of(v_ref), _lanes_of(gblk), _lanes_of(bblk),
            _Heads(st_in_ref[0, h] for h in range(N_HEADS)), do,
            _Heads(dst_ref[h] for h in range(N_HEADS)), _Heads(tinv_ref[0, h] for h in range(N_HEADS)))
        for h in range(N_HEADS):
            dz_ref[:, h * HEAD_DIM:(h + 1) * HEAD_DIM] = dz.v[h]
            for part, val in enumerate((dq, dk, dv)):
                c0 = part * BR_WIDTH + h * HEAD_DIM
                dqkv_ref[:, c0:c0 + HEAD_DIM] = val.v[h]
            dg_all = jnp.where(lane == h, dgc.v[h], dg_all)
            dbeta_all = jnp.where(lane == h, dbeta.v[h], dbeta_all)
            dst_ref[h] = dst.v[h]
        dg_ref[...] = dg_all
        dbeta_ref[...] = dbeta_all

    blk = lambda off: pl.BlockSpec((CHUNK, BR_WIDTH), lambda c: (n - 1 - c, off))
    sc = pl.BlockSpec((CHUNK, HEAD_DIM), lambda c: (n - 1 - c, 0))
    vec = pl.BlockSpec((1, HEAD_DIM), lambda c: (0, 0))
    outs = pl.pallas_call(
        body, name=name, grid=(n,),
        in_specs=[blk(0), blk(1), blk(2), sc, sc,
                  pl.BlockSpec((1, N_HEADS, HEAD_DIM, HEAD_DIM), lambda c: (n - 1 - c, 0, 0, 0)),
                  pl.BlockSpec((1, N_HEADS, CHUNK, CHUNK), lambda c: (n - 1 - c, 0, 0, 0)), blk(0), blk(C_Z // BR_WIDTH),
                  vec, blk(0)],
        out_specs=[pl.BlockSpec((CHUNK, 3 * BR_WIDTH), lambda c: (n - 1 - c, 0)), sc, sc, blk(0), vec],
        out_shape=[jax.ShapeDtypeStruct((s, 3 * BR_WIDTH), F32)] + [jax.ShapeDtypeStruct((s, HEAD_DIM), F32)] * 2
        + [jax.ShapeDtypeStruct((s, BR_WIDTH), BF16), jax.ShapeDtypeStruct((1, HEAD_DIM), F32)],
        scratch_shapes=[pltpu.VMEM((N_HEADS, HEAD_DIM, HEAD_DIM), F32)],
        compiler_params=_cp("arbitrary"),
    )(qkv, qkv, qkv, gcs, beta, states, tinvs, o, proj, norm_w.reshape(1, HEAD_DIM), dy)
    return outs


def _hg_chunk_common(q, k, g):
    c, nb = CHUNK, CHUNK // SUB
    e_g = _exp(g)
    qd = q * e_g
    g_last = g[c - 1:c, :]
    e_t = _exp(g_last - g)
    kt = k * e_t
    tail = _exp(g_last)
    g_refs = [g[i * SUB:i * SUB + 1, :] for i in range(nb)]
    g_ref_rows = _hmap(lambda *rows: jnp.concatenate([jnp.broadcast_to(r, (SUB, r.shape[1])) for r in rows], axis=0), *g_refs)
    e_q = _exp(g - g_ref_rows)
    q_sc = q * e_q
    e_k = [_hmap(lambda gr, g_: jnp.exp(jnp.minimum(gr - g_, EXP_CLAMP)), g_refs[i], g) for i in range(nb)]
    k_sc_all = _rows(*[k * e_k[i] for i in range(nb)])
    row_blk = _iota2(c, 1, 0) // SUB
    masks = [(row_blk == i).astype(F32) for i in range(nb)]
    r_all = _nt(q_sc, k_sc_all)
    a_mat = r_all[:, 0:c] * masks[0]
    for i in range(1, nb):
        a_mat = a_mat + r_all[:, i * c:(i + 1) * c] * masks[i]
    a_mat = a_mat * _tri(c, "lower")
    return dict(e_g=e_g, qd=qd, e_t=e_t, kt=kt, tail=tail, q_sc=q_sc, k_sc_all=k_sc_all, e_q=e_q, e_k=e_k, masks=masks,
                a_mat=a_mat)


def _hg_chunk_fwd_math(q, k, v, g, stt):
    m = _hg_chunk_common(q, k, g)
    o = _nt(m["qd"], stt) + _nn(m["a_mat"], v)
    stt2 = stt * m["tail"] + _tn(v, m["kt"])
    return o, stt2


def _hg_chunk_bwd_math(q, k, v, g, stt, do, dstt2):
    c, nb = CHUNK, CHUNK // SUB
    m = _hg_chunk_common(q, k, g)
    stt2 = stt * m["tail"] + _tn(v, m["kt"])
    later = _sum(stt2 * dstt2, 0)
    dqd = _dot3(do, stt, 1, 0)
    dstt = _tn(do, m["qd"]) + dstt2 * m["tail"]
    d_a = _dot3(do, v, 1, 1) * _tri(c, "lower")
    dv = _tn(m["a_mat"], do) + _nt(m["kt"], dstt2)
    dkt = _dot3(v, dstt2, 1, 0)
    d_blk = _lanes(*[d_a * m["masks"][i] for i in range(nb)])
    dq = dqd * m["e_g"] + _dot3(d_blk, m["k_sc_all"], 1, 0) * m["e_q"]
    dks = _dot3(d_blk, m["q_sc"], 0, 0)
    dk = dkt * m["e_t"]
    for i in range(nb):
        dk = dk + dks[i * c:(i + 1) * c, :] * m["e_k"][i]
    db = q * dq - k * dk
    return dq, dk, dv, db, later, dstt


def _hg_chunk_fwd(proj, lb, norm_w, *, name):
    s = proj.shape[0]
    n = s // CHUNK

    def body(hq_ref, hf_ref, v_ref, lb_ref, z_ref, w_ref, o_ref, st_out_ref, q_out, k_out, lf_out, y_ref, st_ref):
        @pl.when(pl.program_id(0) == 0)
        def _():
            st_ref[...] = jnp.zeros_like(st_ref)

        f, lbv = hf_ref[...], lb_ref[...]
        q_all = _silu(hq_ref[...])
        k_all = (1.0 - lbv) * _sigmoid(-f)
        lf_all = jnp.log(lbv + (1.0 - lbv) * _sigmoid(f))
        q_out[...], k_out[...], lf_out[...] = q_all, k_all, lf_all
        st = _Heads(st_ref[h] for h in range(N_HEADS))
        g_all = _nn_exact(_tri(CHUNK, "lower"), lf_all)
        o, st2 = _hg_chunk_fwd_math(_heads_of(q_all), _heads_of(k_all), _heads_of(v_ref), _heads_of(g_all), st)
        y = _head_norm_fwd(o, _heads_of(z_ref), w_ref[...])
        for h in range(N_HEADS):
            st_out_ref[0, h] = st.v[h]
            o_ref[:, h * HEAD_DIM:(h + 1) * HEAD_DIM] = o.v[h]
            y_ref[:, h * HEAD_DIM:(h + 1) * HEAD_DIM] = y.v[h]
            st_ref[h] = st2.v[h]

    blk = lambda off: pl.BlockSpec((CHUNK, BR_WIDTH), lambda c: (c, off))
    return pl.pallas_call(
        body, name=name, grid=(n,),
        in_specs=[blk(C_HQ // BR_WIDTH), blk(C_HF // BR_WIDTH), blk(C_HI // BR_WIDTH), pl.BlockSpec((1, BR_WIDTH), lambda c: (0, 0)),
                  blk(C_HZ // BR_WIDTH), pl.BlockSpec((1, HEAD_DIM), lambda c: (0, 0))],
        out_specs=[blk(0), pl.BlockSpec((1, N_HEADS, HEAD_DIM, HEAD_DIM), lambda c: (c, 0, 0, 0)), blk(0), blk(0), blk(0), blk(0)],
        out_shape=[jax.ShapeDtypeStruct((s, BR_WIDTH), F32), jax.ShapeDtypeStruct((n, N_HEADS, HEAD_DIM, HEAD_DIM), F32)]
        + [jax.ShapeDtypeStruct((s, BR_WIDTH), F32)] * 3 + [jax.ShapeDtypeStruct((s, BR_WIDTH), BF16)],
        scratch_shapes=[pltpu.VMEM((N_HEADS, HEAD_DIM, HEAD_DIM), F32)],
        compiler_params=_cp("arbitrary"),
    )(proj, proj, proj, lb, proj, norm_w.reshape(1, HEAD_DIM))


def _hg_chunk_bwd(proj, lb, qh, kh, lf, states, o, norm_w, dy, *, name):
    s = proj.shape[0]
    n = s // CHUNK

    def body(hq_ref, hf_ref, v_ref, lb_ref, q_ref, k_ref, lf_ref, st_in_ref, o_ref, z_ref, w_ref, dy_ref,
             dhq_ref, dhf_ref, dhi_ref, dz_ref, dlb_ref, dw_ref, dst_ref):
        @pl.when(pl.program_id(0) == 0)
        def _():
            dst_ref[...] = jnp.zeros_like(dst_ref)
            dlb_ref[...] = jnp.zeros_like(dlb_ref)
            dw_ref[...] = jnp.zeros_like(dw_ref)

        do, dz, dw = _head_norm_bwd(_heads_of(o_ref), _heads_of(z_ref), w_ref[...], _heads_of(dy_ref))
        dw_ref[...] += dw

        g_all = _nn_exact(_tri(CHUNK, "lower"), lf_ref[...])
        dq, dk, dv, db, later, dst = _hg_chunk_bwd_math(
            _heads_of(q_ref), _heads_of(k_ref), _heads_of(v_ref), _heads_of(g_all),
            _Heads(st_in_ref[0, h] for h in range(N_HEADS)), do,
            _Heads(dst_ref[h] for h in range(N_HEADS)))
        dlf = _nn_exact(_tri(CHUNK, "upper"), jnp.concatenate(db.v, axis=1)) + jnp.concatenate(later.v, axis=1)
        dq_all, dk_all = jnp.concatenate(dq.v, axis=1), jnp.concatenate(dk.v, axis=1)
        f, lbv = hf_ref[...], lb_ref[...]
        dhq_ref[...] = (dq_all * _dsilu(hq_ref[...])).astype(BF16)
        sp, sn = _sigmoid(f), _sigmoid(-f)
        dlf_over = dlf / (lbv + (1.0 - lbv) * sp)
        dhf_ref[...] = (dlf_over * (1.0 - lbv) * sp * sn - dk_all * (1.0 - lbv) * sn * (1.0 - sn)).astype(BF16)
        dlb_ref[...] += jnp.sum(dlf_over * (1.0 - sp) - dk_all * sn, axis=0, keepdims=True)
        for h in range(N_HEADS):
            dhi_ref[:, h * HEAD_DIM:(h + 1) * HEAD_DIM] = dv.v[h].astype(BF16)
            dz_ref[:, h * HEAD_DIM:(h + 1) * HEAD_DIM] = dz.v[h]
            dst_ref[h] = dst.v[h]

    blk = lambda off: pl.BlockSpec((CHUNK, BR_WIDTH), lambda c: (n - 1 - c, off))
    vec = pl.BlockSpec((1, BR_WIDTH), lambda c: (0, 0))
    wvec = pl.BlockSpec((1, HEAD_DIM), lambda c: (0, 0))
    return pl.pallas_call(
        body, name=name, grid=(n,),
        in_specs=[blk(C_HQ // BR_WIDTH), blk(C_HF // BR_WIDTH), blk(C_HI // BR_WIDTH), vec, blk(0), blk(0), blk(0),
                  pl.BlockSpec((1, N_HEADS, HEAD_DIM, HEAD_DIM), lambda c: (n - 1 - c, 0, 0, 0)), blk(0), blk(C_HZ // BR_WIDTH),
                  wvec, blk(1)],
        out_specs=[blk(0), blk(0), blk(0), blk(0), vec, wvec],
        out_shape=[jax.ShapeDtypeStruct((s, BR_WIDTH), BF16)] * 4 + [jax.ShapeDtypeStruct((1, BR_WIDTH), F32),
                                                                    jax.ShapeDtypeStruct((1, HEAD_DIM), F32)],
        scratch_shapes=[pltpu.VMEM((N_HEADS, HEAD_DIM, HEAD_DIM), F32)],
        compiler_params=_cp("arbitrary"),
    )(proj, proj, proj, lb, qh, kh, lf, states, o, proj, norm_w.reshape(1, HEAD_DIM), dy)


_ANY = pl.BlockSpec(memory_space=pl.ANY)
_MESH = pl.DeviceIdType.MESH


def _all_gather(x_local, *, name, after=()):
    n_after = len(after)

    def body(x_ref, *refs):
        out_ref, send_sems, recv_sems, local_sem = refs[n_after:]
        x, y, c = lax.axis_index("x"), lax.axis_index("y"), lax.axis_index("c")
        me, sibling = (x, y, c), (x, y, 1 - c)
        n1 = (x ^ (1 - c), y ^ c)
        n2 = (x ^ c, y ^ (1 - c))
        dg = (1 - x, 1 - y)

        def slot(px, py, pc):
            return out_ref.at[4 * px + 2 * py + pc]

        def copy(k, block, to, src=None):
            return pltpu.make_async_remote_copy(
                src_ref=slot(*block) if src is None else src, dst_ref=slot(*block),
                send_sem=send_sems.at[k], recv_sem=recv_sems.at[k], device_id=to, device_id_type=_MESH)

        mine = pltpu.make_async_copy(x_ref, slot(*me), local_sem)
        mine.start()
        first = [copy(0, me, sibling, src=x_ref), copy(1, me, (*n1, c), src=x_ref), copy(2, me, (*n2, c), src=x_ref)]
        for cp in first:
            cp.start()
        copy(2, (*n2, c), me).wait_recv()
        forward = copy(3, (*n2, c), (*n1, c))
        forward.start()
        passed = [copy(5, (*n2, c), sibling)]
        passed[0].start()
        copy(1, (*n1, c), me).wait_recv()
        passed.append(copy(4, (*n1, c), sibling))
        passed[1].start()
        copy(3, (*dg, c), me).wait_recv()
        passed.append(copy(6, (*dg, c), sibling))
        passed[2].start()
        copy(0, sibling, me).wait_recv()
        copy(4, (*n2, 1 - c), me).wait_recv()
        copy(5, (*n1, 1 - c), me).wait_recv()
        copy(6, (*dg, 1 - c), me).wait_recv()
        for cp in first + [forward] + passed:
            cp.wait_send()
        mine.wait()

    return pl.pallas_call(
        body, name=name, out_shape=jax.ShapeDtypeStruct((N_DEV,) + x_local.shape, x_local.dtype),
        in_specs=[_ANY] * (1 + n_after), out_specs=_ANY,
        scratch_shapes=[pltpu.SemaphoreType.DMA((7,)), pltpu.SemaphoreType.DMA((7,)), pltpu.SemaphoreType.DMA],
    )(x_local, *after)


_HBM = pl.BlockSpec(memory_space=pltpu.HBM)
_SEM = pl.BlockSpec(memory_space=pltpu.SEMAPHORE)
_EFFECT = pltpu.SideEffectType.DATAFLOW_SIDE_EFFECTING


def _peers():
    x, y, c = lax.axis_index("x"), lax.axis_index("y"), lax.axis_index("c")
    out = []
    for k in range(1, N_DEV):
        px, py, pc = x ^ ((k >> 2) & 1), y ^ ((k >> 1) & 1), c ^ (k & 1)
        out.append(((px, py, pc), 4 * px + 2 * py + pc))
    return 4 * x + 2 * y + c, out


def _push_copies(src_ref, land_ref, send_sems, recv_sems, broadcast):
    my, peers = _peers()
    pairs = []
    for k, (pos, idx) in enumerate(peers):
        src = src_ref if broadcast else src_ref.at[idx]
        send = pltpu.make_async_remote_copy(src_ref=src, dst_ref=land_ref.at[my], send_sem=send_sems.at[k],
                                            recv_sem=recv_sems.at[k], device_id=pos, device_id_type=_MESH)
        recv = pltpu.make_async_remote_copy(src_ref=src, dst_ref=land_ref.at[idx], send_sem=send_sems.at[k],
                                            recv_sem=recv_sems.at[k], device_id=pos, device_id_type=_MESH)
        pairs.append((send, recv))
    return pairs


def _push_start(src, land, *, broadcast, name, after=()):
    n_after = len(after)

    def body(src_ref, land_ref, *refs):
        send_sems, recv_sems, _, _, token = refs[n_after:]
        for send, _ in _push_copies(src_ref, land_ref, send_sems, recv_sems, broadcast):
            send.start()
        token[...] = jnp.zeros_like(token)

    return pl.pallas_call(
        body, name=name,
        out_shape=(pltpu.SemaphoreType.DMA((N_DEV - 1,)), pltpu.SemaphoreType.DMA((N_DEV - 1,)),
                   pltpu.HBM(src.shape, src.dtype), pltpu.HBM(land.shape, land.dtype), jax.ShapeDtypeStruct((8, 128), F32)),
        in_specs=(_HBM, _HBM) + (_ANY,) * n_after, out_specs=(_SEM, _SEM, _HBM, _HBM, pl.BlockSpec(memory_space=pltpu.VMEM)),
        input_output_aliases={0: 2, 1: 3}, compiler_params=pltpu.CompilerParams(has_side_effects=_EFFECT),
    )(pltpu.with_memory_space_constraint(src, pltpu.HBM), pltpu.with_memory_space_constraint(land, pltpu.HBM), *after)


def _push_wait(handle, after, *, broadcast, name):
    send_sems, recv_sems, src_thru, land_thru, _ = handle

    def body(src_ref, land_ref, send_sems, recv_sems, *rest):
        for send, recv in _push_copies(src_ref, land_ref, send_sems, recv_sems, broadcast):
            send.wait_send()
            recv.wait_recv()

    return pl.pallas_call(
        body, name=name,
        out_shape=(pltpu.HBM(src_thru.shape, src_thru.dtype), pltpu.HBM(land_thru.shape, land_thru.dtype)),
        in_specs=(_HBM, _HBM, _SEM, _SEM) + (_ANY,) * len(after), out_specs=(_HBM, _HBM),
        input_output_aliases={0: 0, 1: 1}, compiler_params=pltpu.CompilerParams(has_side_effects=_EFFECT),
    )(src_thru, land_thru, send_sems, recv_sems, *after)[1]


def _relay_copies(src_ref, land_ref, sems_a, sems_b):
    x, y, c = lax.axis_index("x"), lax.axis_index("y"), lax.axis_index("c")
    slot = lambda px, py, pc: land_ref.at[4 * px + 2 * py + pc]
    chips = [(1 - x, y), (x, 1 - y), (1 - x, 1 - y)]
    (send_a, recv_a), (send_b, recv_b) = sems_a, sems_b

    def copy(sems, k, src, dst_slot, to):
        return pltpu.make_async_remote_copy(src_ref=src, dst_ref=dst_slot, send_sem=sems[0].at[k], recv_sem=sems[1].at[k],
                                            device_id=to, device_id_type=_MESH)

    first = [copy((send_a, recv_a), 0, src_ref, slot(x, y, c), (x, y, 1 - c))]
    first += [copy((send_a, recv_a), 1 + j, src_ref, slot(x, y, c), (*chip, c)) for j, chip in enumerate(chips)]
    first_in = [copy((send_a, recv_a), 0, src_ref, slot(x, y, 1 - c), (x, y, 1 - c))]
    first_in += [copy((send_a, recv_a), 1 + j, src_ref, slot(*chip, c), (*chip, c)) for j, chip in enumerate(chips)]
    relay = [copy((send_b, recv_b), j, slot(*chip, c), slot(*chip, c), (x, y, 1 - c)) for j, chip in enumerate(chips)]
    relay_in = [copy((send_b, recv_b), j, slot(*chip, 1 - c), slot(*chip, 1 - c), (x, y, 1 - c)) for j, chip in enumerate(chips)]
    return first, first_in, relay, relay_in


def _relay_start(src, land, *, name, after=()):
    n_after = len(after)

    def body(src_ref, land_ref, *refs):
        send_a, recv_a, _, _, token = refs[n_after:]
        for cp in _relay_copies(src_ref, land_ref, (send_a, recv_a), (send_a, recv_a))[0]:
            cp.start()
        token[...] = jnp.zeros_like(token)

    send_a, recv_a, src_thru, land_thru, token = pl.pallas_call(
        body, name=name,
        out_shape=(pltpu.SemaphoreType.DMA((4,)), pltpu.SemaphoreType.DMA((4,)), pltpu.HBM(src.shape, src.dtype),
                   pltpu.HBM(land.shape, land.dtype), jax.ShapeDtypeStruct((8, 128), F32)),
        in_specs=(_HBM, _HBM) + (_ANY,) * n_after, out_specs=(_SEM, _SEM, _HBM, _HBM, pl.BlockSpec(memory_space=pltpu.VMEM)),
        input_output_aliases={0: 2, 1: 3}, compiler_params=pltpu.CompilerParams(has_side_effects=_EFFECT),
    )(pltpu.with_memory_space_constraint(src, pltpu.HBM), pltpu.with_memory_space_constraint(land, pltpu.HBM), *after)
    return (send_a, recv_a), src_thru, land_thru, token


def _relay_mid(handle, after, *, name):
    sems_a, src_thru, land_thru, _ = handle
    n_after = len(after)

    def body(src_ref, land_ref, send_a, recv_a, *refs):
        send_b, recv_b, _, _, token = refs[n_after:]
        _, first_in, relay, _ = _relay_copies(src_ref, land_ref, (send_a, recv_a), (send_b, recv_b))
        for j in range(3):
            first_in[1 + j].wait_recv()
            relay[j].start()
        token[...] = jnp.zeros_like(token)

    send_b, recv_b, src2, land2, token = pl.pallas_call(
        body, name=name,
        out_shape=(pltpu.SemaphoreType.DMA((3,)), pltpu.SemaphoreType.DMA((3,)), pltpu.HBM(src_thru.shape, src_thru.dtype),
                   pltpu.HBM(land_thru.shape, land_thru.dtype), jax.ShapeDtypeStruct((8, 128), F32)),
        in_specs=(_HBM, _HBM, _SEM, _SEM) + (_ANY,) * n_after,
        out_specs=(_SEM, _SEM, _HBM, _HBM, pl.BlockSpec(memory_space=pltpu.VMEM)),
        input_output_aliases={0: 2, 1: 3}, compiler_params=pltpu.CompilerParams(has_side_effects=_EFFECT),
    )(src_thru, land_thru, *sems_a, *after)
    return sems_a, (send_b, recv_b), src2, land2, token


def _relay_wait(handle, after, *, name):
    sems_a, sems_b, src_thru, land_thru, _ = handle

    def body(src_ref, land_ref, send_a, recv_a, send_b, recv_b, *rest):
        first, first_in, relay, relay_in = _relay_copies(src_ref, land_ref, (send_a, recv_a), (send_b, recv_b))
        first_in[0].wait_recv()
        for cp in relay_in:
            cp.wait_recv()
        for cp in first + relay:
            cp.wait_send()

    return pl.pallas_call(
        body, name=name,
        out_shape=(pltpu.HBM(src_thru.shape, src_thru.dtype), pltpu.HBM(land_thru.shape, land_thru.dtype)),
        in_specs=(_HBM, _HBM, _SEM, _SEM, _SEM, _SEM) + (_ANY,) * len(after), out_specs=(_HBM, _HBM),
        input_output_aliases={0: 0, 1: 1}, compiler_params=pltpu.CompilerParams(has_side_effects=_EFFECT),
    )(src_thru, land_thru, *sems_a, *sems_b, *after)[1]


def _adamw(parts, row_off, w, m, v, *, layer=0, n_layers=1, prev=None, name, tr):
    rows, c = w.shape
    r = rows // n_layers
    np_ = parts.shape[0]
    tr = min(tr, r)
    assert r % tr == 0 and row_off % tr == 0
    ob, lb = row_off // tr, layer * (r // tr)
    c1 = 1.0 - ADAM_B1 ** ADAM_STEP
    c2 = 1.0 - ADAM_B2 ** ADAM_STEP
    n_prev = 0 if prev is None else 4

    def body(p_ref, w_ref, m_ref, v_ref, *refs):
        g_ref, d_ref, nm_ref, nv_ref = refs[n_prev:]
        g = p_ref[0].astype(F32)
        for s in range(1, np_):
            g = g + p_ref[s].astype(F32)
        wv = w_ref[...]
        m2 = ADAM_B1 * m_ref[...] + (1.0 - ADAM_B1) * g
        v2 = ADAM_B2 * v_ref[...] + (1.0 - ADAM_B2) * jnp.square(g)
        m_hat = m2 / c1
        v_hat = v2 / c2
        g_ref[...] = g
        d_ref[...] = -ADAM_LR * (m_hat / (jnp.sqrt(v_hat) + ADAM_EPS) + ADAM_WD * wv)
        nm_ref[...] = m2
        nv_ref[...] = v2

    blk = pl.BlockSpec((tr, c), lambda i: (lb + i, 0))
    return pl.pallas_call(
        body, name=name, grid=(r // tr,),
        in_specs=[pl.BlockSpec((np_, tr, c), lambda i: (0, ob + i, 0)), blk, blk, blk] + [_ANY] * n_prev,
        out_specs=[blk] * 4, out_shape=[jax.ShapeDtypeStruct((rows, c), F32)] * 4,
        input_output_aliases={4 + i: i for i in range(n_prev)}, compiler_params=_cp("parallel"),
    )(parts, w, m, v, *(prev or ()))


def _sum_parts(parts, *, name, after=()):
    np_, r, c = parts.shape

    def body(p_ref, *refs):
        o_ref = refs[-1]
        g = p_ref[0]
        for s in range(1, np_):
            g = g + p_ref[s]
        o_ref[...] = g

    vmem = pl.BlockSpec(memory_space=pltpu.VMEM)
    return pl.pallas_call(body, name=name, in_specs=[vmem] + [_ANY] * len(after), out_specs=vmem,
                          out_shape=jax.ShapeDtypeStruct((r, c), F32))(parts, *after)


def _pack(arrs):
    rows = []
    for a in arrs:
        f = a.reshape(-1).astype(F32)
        pad = (-f.shape[0]) % 128
        rows.append(jnp.pad(f, (0, pad)).reshape(-1, 128))
    out = jnp.concatenate(rows, axis=0)
    return jnp.pad(out, ((0, (-out.shape[0]) % 8), (0, 0)))


def _unpack(packed, shapes):
    outs, r0 = [], 0
    for shp in shapes:
        n = 1
        for d in shp:
            n *= d
        nr = -(-n // 128)
        outs.append(packed[r0:r0 + nr].reshape(-1)[:n].reshape(shp))
        r0 += nr
    return outs


_WIN_PIECES = ((0, 4096, 0), (4112, 8208, 0), (4096, 4104, HEAD_DIM - N_HEADS), (4104, 4112, HEAD_DIM - N_HEADS))


RELAYOUT_TILE = 256
LAST_SPLIT = 4
OTHER_SPLIT = 2


def _win_from_shards(shards, *, name):
    k = shards.shape[1]
    tr = min(RELAYOUT_TILE, k)

    def body(x_ref, o_ref):
        cols = []
        for lo, hi, pad in _WIN_PIECES:
            for j in range(N_DEV):
                a, b = max(lo, j * SHARD_IN), min(hi, (j + 1) * SHARD_IN)
                if a < b:
                    cols.append(x_ref[j, :, a - j * SHARD_IN:b - j * SHARD_IN])
            if pad:
                cols.append(jnp.zeros((tr, pad), x_ref.dtype))
        o_ref[...] = jnp.concatenate(cols, axis=1)

    return pl.pallas_call(
        body, name=name, grid=(k // tr,), in_specs=[pl.BlockSpec((N_DEV, tr, SHARD_IN), lambda i: (0, i, 0))],
        out_specs=pl.BlockSpec((tr, N_PROJ), lambda i: (i, 0)), out_shape=jax.ShapeDtypeStruct((k, N_PROJ), shards.dtype),
        compiler_params=_cp("parallel"),
    )(shards)


def _win_to_shards(g, *, name):
    k = g.shape[0]
    tr = min(RELAYOUT_TILE, k)
    starts, off = [], 0
    for lo, hi, pad in _WIN_PIECES:
        starts.append((lo, hi, off))
        off += hi - lo + pad

    def body(g_ref, o_ref):
        for j in range(N_DEV):
            cols = []
            for lo, hi, off in sorted(starts):
                a, b = max(lo, j * SHARD_IN), min(hi, (j + 1) * SHARD_IN)
                if a < b:
                    cols.append(g_ref[:, off + a - lo:off + b - lo])
            o_ref[j] = jnp.concatenate(cols, axis=1)

    return pl.pallas_call(
        body, name=name, grid=(k // tr,), in_specs=[pl.BlockSpec((tr, N_PROJ), lambda i: (i, 0))],
        out_specs=pl.BlockSpec((N_DEV, tr, SHARD_IN), lambda i: (0, i, 0)),
        out_shape=jax.ShapeDtypeStruct((N_DEV, k, SHARD_IN), g.dtype), compiler_params=_cp("parallel"),
    )(g)


def _lower_bounds(logits):
    probs = jax.nn.softmax(logits.astype(F32), axis=0)
    return jnp.cumsum(probs, axis=0) - probs[0]


def _pad_lanes(vec8):
    return jnp.pad(vec8.reshape(1, N_HEADS), ((0, 0), (0, HEAD_DIM - N_HEADS)))


def kernel(x, p, norm_w, w_in, dn_conv_w, dn_A_log, dn_dt_bias, dn_norm_w, hg_lb_logits, hg_norm_w, w_out, w_ple_up, w_ple_gate, final_norm_w, loss_target, m_norm_w, m_w_in, m_dn_conv_w, m_dn_A_log, m_dn_dt_bias, m_dn_norm_w, m_hg_lb_logits, m_hg_norm_w, m_w_out, m_w_ple_up, m_w_ple_gate, m_final_norm_w, v_norm_w, v_w_in, v_dn_conv_w, v_dn_A_log, v_dn_dt_bias, v_dn_norm_w, v_hg_lb_logits, v_hg_norm_w, v_w_out, v_w_ple_up, v_w_ple_gate, v_final_norm_w):
    depth = norm_w.shape[0]
    my = 4 * lax.axis_index("x") + 2 * lax.axis_index("y") + lax.axis_index("c")
    h = x[0]
    tgt = loss_target[0]
    rows_out = D_MODEL // N_DEV
    up_rows = PLE_DIM * (D_MODEL // N_DEV) // D_MODEL
    g_off, u_off = rows_out, 2 * rows_out

    def own_slot(block):
        return lax.dynamic_update_index_in_dim(lax.empty((N_DEV,) + block.shape, block.dtype), block, my, 0)

    win_bf = w_in.astype(BF16)
    rest_bf = [jnp.concatenate([w_out[l], w_ple_gate[l], w_ple_up[l].reshape(up_rows, D_MODEL)], axis=0).astype(BF16)
               for l in range(depth)]
    conv_all = _all_gather(dn_conv_w, name="gather_conv_w")
    conv_full = conv_all.transpose(1, 2, 0, 3).reshape(depth, CONV_W, 3 * BR_WIDTH)
    win_all = {0: _all_gather(win_bf[0], name="gather_w_in_l0", after=[conv_all])}
    pending, relayed = {}, {}
    last = win_all[0]
    for l in range(depth):
        if l > 0:
            relayed["win", l] = _relay_start(win_bf[l], own_slot(win_bf[l]), after=[last], name=f"gather_w_in_l{l}_first")
            last = relayed["win", l][3]
        if l == 0:
            relayed["rest", l] = _relay_start(rest_bf[l], own_slot(rest_bf[l]), after=[last], name=f"gather_rest_l{l}_first")
            last = relayed["rest", l][3]
        else:
            pending["rest", l] = _push_start(rest_bf[l], own_slot(rest_bf[l]), broadcast=True, after=[last],
                                             name=f"gather_rest_l{l}_start")
            last = pending["rest", l][4]
    order_tok = last[0, 0]
    lbs = _lower_bounds(hg_lb_logits)

    saved = []
    weights = []
    for l in range(depth):
        tag = f"l{l}"
        if l > 0:
            win_all[l] = _relay_wait(relayed["win", l], [h], name=f"gather_w_in_{tag}_wait")
        wi = _win_from_shards(win_all[l], name=f"w_in_layout_{tag}")
        nw = norm_w[l] + order_tok if l == 0 else norm_w[l]
        hn = _rms_fwd(h, nw, name=f"rms_fwd_{tag}")
        proj = _mm(hn, wi, mode="nn", out_dtype=F32, name=f"mm_proj_{tag}")
        al, dt = _pad_lanes(dn_A_log[l]), _pad_lanes(dn_dt_bias[l])
        qkv = _dn_qkv_fwd(proj, conv_full[l], name=f"dn_qkv_fwd_{tag}")
        if ("rest", l) in relayed:
            relayed["rest", l] = _relay_mid(relayed["rest", l], [qkv], name=f"gather_rest_{tag}_relay")
            al = al + relayed["rest", l][4][0, 0]
        beta, gcs = _dn_gate_fwd(proj, al, dt, name=f"dn_gate_fwd_{tag}")
        o_dn, st_dn, tinv_dn, y_dn = _dn_chunk_fwd(qkv, gcs, beta, proj, dn_norm_w[l], name=f"dn_chunk_fwd_{tag}")
        lb = lbs[l].reshape(1, BR_WIDTH)
        o_hg, st_hg, qh, kh, lf, y_hg = _hg_chunk_fwd(proj, lb, hg_norm_w[l], name=f"hg_chunk_fwd_{tag}")
        y = jnp.concatenate([y_dn, y_hg], axis=1)
        if ("rest", l) in relayed:
            rest_all = _relay_wait(relayed["rest", l], [y], name=f"gather_rest_{tag}_wait")
        else:
            rest_all = _push_wait(pending["rest", l], [y], broadcast=True, name=f"gather_rest_{tag}_wait")
        w_out_rows, w_gate_rows = (0, rows_out), (g_off, rows_out)
        wu = rest_all[:, u_off:u_off + up_rows].reshape(N_DEV, PLE_DIM, D_MODEL // N_DEV).transpose(1, 0, 2).reshape(PLE_DIM, D_MODEL)
        weights.append((wi, rest_all, wu))
        h1 = _mm(y, rest_all, mode="nn", b_rows=w_out_rows, out_dtype=F32, res=h, name=f"mm_out_{tag}")
        pin = []
        if ("win", l + 1) in relayed:
            relayed["win", l + 1] = _relay_mid(relayed["win", l + 1], [h1], name=f"gather_w_in_l{l + 1}_relay")
            pin = [relayed["win", l + 1][4]]
        gp = _mm(h1, rest_all, mode="nn", b_rows=w_gate_rows, out_dtype=F32, after=pin, name=f"mm_gate_{tag}")
        up = _mm(p[l, 0], wu, mode="nn", out_dtype=F32, name=f"mm_up_{tag}")
        h2 = _ple_fwd(h1, gp, up, name=f"ple_fwd_{tag}")
        saved.append(dict(h=h, hn=hn, proj=proj, qkv=qkv, beta=beta, gcs=gcs, st_dn=st_dn, tinv_dn=tinv_dn, qh=qh, kh=kh, lf=lf,
                          st_hg=st_hg, o_dn=o_dn, o_hg=o_hg, y=y, h1=h1, gp=gp, up=up, al=al, dt=dt, lb=lb))
        h = h2

    loss_row, dh, d_final_w = _final_fwd_bwd(h, final_norm_w, tgt, name="final_norm_loss")

    d_norm_w, d_alog, d_dt, d_dn_nw, d_hg_nw, d_lb, d_conv = ([None] * depth for _ in range(7))
    sent = {}
    for l in reversed(range(depth)):
        wi, rest_all, wu = weights[l]
        sv = saved[l]
        tag = f"l{l}"
        dup, dgp = _ple_bwd(dh, sv["gp"], sv["up"], name=f"ple_bwd_{tag}")
        d_wu = _mm(p[l, 0], dup, mode="tn", out_dtype=BF16, name=f"mm_dwup_{tag}")
        d_wg = _mm(sv["h1"], dgp, mode="tn", out_dtype=BF16, name=f"mm_dwgate_{tag}")
        dh1 = _mm(dgp, rest_all, mode="nt", b_rows=(g_off, rows_out), out_dtype=F32, res=dh, name=f"mm_dh1_{tag}")
        d_wo = _mm(sv["y"], dh1, mode="tn", out_dtype=BF16, name=f"mm_dwout_{tag}")
        parts_rest = jnp.concatenate(
            [d_wo.reshape(N_DEV, rows_out, D_MODEL), d_wg.reshape(N_DEV, rows_out, D_MODEL),
             d_wu.reshape(PLE_DIM, N_DEV, D_MODEL // N_DEV).transpose(1, 0, 2).reshape(N_DEV, up_rows, D_MODEL)], axis=1)
        sent["rest", l] = _push_start(parts_rest, own_slot(parts_rest[my]), broadcast=False, name=f"exchange_rest_{tag}_start")
        dy = _mm(dh1, rest_all, mode="nt", b_rows=(0, rows_out), out_dtype=F32, name=f"mm_dy_{tag}")
        dn_nw = dn_norm_w[l] + sent["rest", l][4][0, 0]
        dqkv, d_gc, dbeta, dz_dn, d_dn_nw[l] = _dn_chunk_bwd(sv["qkv"], sv["gcs"], sv["beta"], sv["st_dn"], sv["tinv_dn"],
                                                             sv["o_dn"], sv["proj"], dn_nw, dy, name=f"dn_chunk_bwd_{tag}")
        dqkv_pre, d_conv[l] = _dn_qkv_bwd(sv["proj"], conv_full[l], dqkv, name=f"dn_qkv_bwd_{tag}")
        db, da, d_alog[l], d_dt[l] = _dn_gate_bwd(sv["proj"], sv["al"], sv["dt"], dbeta, d_gc, name=f"dn_gate_bwd_{tag}")
        dhq, dhf, dhi, dz_hg, d_lb[l], d_hg_nw[l] = _hg_chunk_bwd(sv["proj"], sv["lb"], sv["qh"], sv["kh"], sv["lf"], sv["st_hg"],
                                                                  sv["o_hg"], hg_norm_w[l], dy, name=f"hg_chunk_bwd_{tag}")
        dproj = jnp.concatenate([dqkv_pre, dz_dn, dhq, dhf, dhi, dz_hg, db, da], axis=1)
        def push_d_win(after):
            n_split = LAST_SPLIT if l == 0 else OTHER_SPLIT
            rows = D_MODEL // n_split
            handles = []
            for q in range(n_split):
                hn_q = sv["hn"] if n_split == 1 else sv["hn"][:, q * rows:(q + 1) * rows]
                sfx = tag if n_split == 1 else f"{tag}_{q}"
                d_win = _mm(hn_q, dproj, mode="tn", out_dtype=BF16, after=after, name=f"mm_dwin_{sfx}")
                parts_in = _win_to_shards(d_win, name=f"dw_in_shards_{sfx}")
                handles.append(_push_start(parts_in, own_slot(parts_in[my]), broadcast=False, after=after,
                                           name=f"exchange_w_in_{sfx}_start"))
                after = [handles[-1][4]]
            return handles

        if l == 0:
            small = _pack([loss_row, jnp.concatenate(d_norm_w[1:], axis=0), d_final_w,
                           jnp.stack([a[0, :N_HEADS] for a in d_alog]), jnp.stack([a[0, :N_HEADS] for a in d_dt]),
                           jnp.concatenate(d_dn_nw, axis=0), jnp.concatenate(d_hg_nw, axis=0), jnp.concatenate(d_lb, axis=0),
                           jnp.stack(d_conv)])
            small_all = _all_gather(small, name="gather_small")
        sent["win", l] = push_d_win([small_all] if l == 0 else [])
        dhn = _mm(dproj, wi, mode="nt", out_dtype=F32, tile_n=D_MODEL, after=[sent["win", l][-1][4]], name=f"mm_dhn_{tag}")
        dh, d_norm_w[l] = _rms_bwd(sv["h"], norm_w[l], dhn, dh1, name=f"rms_bwd_{tag}")
    grad_x = dh[None]

    small_shapes = [(1, 128), (depth - 1, D_MODEL), final_norm_w.shape, dn_A_log.shape, dn_dt_bias.shape, dn_norm_w.shape,
                    hg_norm_w.shape, hg_lb_logits.shape, (depth, CONV_W, 3 * BR_WIDTH)]
    tot = _unpack(_sum_parts(small_all, after=[grad_x], name="sum_small"), small_shapes)
    loss = tot[0][0, 0]
    g_lb = tot[7]
    g_logits = jax.vjp(_lower_bounds, hg_lb_logits)[1](g_lb)[0]
    g_conv = lax.dynamic_slice_in_dim(tot[8], my * (3 * BR_WIDTH // N_DEV), 3 * BR_WIDTH // N_DEV, axis=2)
    small_g = [g_conv, tot[3], tot[4], tot[5], g_logits, tot[6], tot[2]]
    small_w = [dn_conv_w, dn_A_log, dn_dt_bias, dn_norm_w, hg_lb_logits, hg_norm_w, final_norm_w]
    small_m = [m_dn_conv_w, m_dn_A_log, m_dn_dt_bias, m_dn_norm_w, m_hg_lb_logits, m_hg_norm_w, m_final_norm_w]
    small_v = [v_dn_conv_w, v_dn_A_log, v_dn_dt_bias, v_dn_norm_w, v_hg_lb_logits, v_hg_norm_w, v_final_norm_w]
    pk_w = _pack(small_w)
    res_small = _adamw(_pack(small_g)[None], 0, pk_w, _pack(small_m), _pack(small_v), name="adamw_small", tr=pk_w.shape[0])
    shapes_w = [a.shape for a in small_w]
    sg, sd, sm, sv_ = (_unpack(r, shapes_w) for r in res_small)

    r_win = r_wo = r_wg = r_wu = None
    done = [grad_x, res_small[0]]

    def flat(a, cols):
        return a.reshape(-1, cols)

    for l in reversed(range(depth)):
        tag = f"l{l}"
        land_rest = _push_wait(sent["rest", l], done, broadcast=False, name=f"exchange_rest_{tag}_wait")
        r_wo = _adamw(land_rest, 0, flat(w_out, D_MODEL), flat(m_w_out, D_MODEL), flat(v_w_out, D_MODEL), layer=l,
                      n_layers=depth, prev=r_wo, name=f"adamw_w_out_{tag}", tr=rows_out)
        r_wg = _adamw(land_rest, g_off, flat(w_ple_gate, D_MODEL), flat(m_w_ple_gate, D_MODEL), flat(v_w_ple_gate, D_MODEL),
                      layer=l, n_layers=depth, prev=r_wg, name=f"adamw_w_gate_{tag}", tr=rows_out)
        r_wu = _adamw(land_rest, u_off, flat(w_ple_up, D_MODEL), flat(m_w_ple_up, D_MODEL), flat(v_w_ple_up, D_MODEL),
                      layer=l, n_layers=depth, prev=r_wu, name=f"adamw_w_up_{tag}", tr=up_rows)
        done = [r_wo[0], r_wg[0], r_wu[0]]
    for l in reversed(range(depth)):
        tag = f"l{l}"
        if l == 0:
            nw0 = _sum_parts(_all_gather(_pack([d_norm_w[0]]), after=done, name="gather_norm_w"), name="sum_norm_w")
            g_norm_w = jnp.concatenate([_unpack(nw0, [(1, D_MODEL)])[0], tot[1]], axis=0)
            pk_nw = _pack([norm_w])
            r_nw = _adamw(_pack([g_norm_w])[None], 0, pk_nw, _pack([m_norm_w]), _pack([v_norm_w]), name="adamw_norm_w",
                          tr=pk_nw.shape[0])
            r_nw = [_unpack(r, [norm_w.shape])[0] for r in r_nw]
            done = [r_nw[0]]
        n_split = len(sent["win", l])
        for q, handle in enumerate(sent["win", l]):
            sfx = tag if n_split == 1 else f"{tag}_{q}"
            land_in = _push_wait(handle, done, broadcast=False, name=f"exchange_w_in_{sfx}_wait")
            r_win = _adamw(land_in, 0, flat(w_in, SHARD_IN), flat(m_w_in, SHARD_IN), flat(v_w_in, SHARD_IN),
                           layer=l * n_split + q, n_layers=depth * n_split, prev=r_win, name=f"adamw_w_in_{sfx}", tr=256)
            done = [r_win[0]]
    r_win = [o.reshape(w_in.shape) for o in r_win]
    r_wo = [o.reshape(w_out.shape) for o in r_wo]
    r_wg = [o.reshape(w_ple_gate.shape) for o in r_wg]
    r_wu = [o.reshape(w_ple_up.shape) for o in r_wu]

    def order(nw, small_list, big_in, big_out, big_up, big_gate):
        cw, al_, dt_, dnw, lbl, hnw, fw = small_list
        return [nw, big_in, cw, al_, dt_, dnw, lbl, hnw, big_out, big_up, big_gate, fw]

    outs = [loss, grad_x]
    for i, sl in enumerate((sg, sd, sm, sv_)):
        outs += order(r_nw[i], sl, r_win[i], r_wo[i], r_wu[i], r_wg[i])
    return tuple(outs)
```

```python
import functools

import jax
import jax.numpy as jnp
from jax import lax
from jax.experimental import pallas as pl
from jax.experimental.pallas import tpu as pltpu

F32 = jnp.float32
BF16 = jnp.bfloat16
HIGHEST = lax.Precision.HIGHEST

N_DEV = 8
D_MODEL = 2048
PLE_DIM = 256
HEAD_DIM = 128
N_HEADS = 8
BR_WIDTH = N_HEADS * HEAD_DIM
CHUNK = 64
SUB = 16
CONV_W = 4
NORM_EPS = 1e-6
L2_EPS = 1e-6
IN_WIDTH = 8208
SHARD_IN = IN_WIDTH // N_DEV
EXP_CLAMP = 80.0

C_QKV, C_Z, C_HQ, C_HF, C_HI, C_HZ, C_B, C_A, N_PROJ = 0, 3072, 4096, 5120, 6144, 7168, 8192, 8320, 8448

ADAM_LR, ADAM_B1, ADAM_B2, ADAM_EPS, ADAM_WD, ADAM_STEP = 0.001, 0.9, 0.999, 1e-08, 0.01, 10

VMEM_LIMIT = 48 * 1024 * 1024


def _cp(*sem):
    return pltpu.CompilerParams(dimension_semantics=sem, vmem_limit_bytes=VMEM_LIMIT)


class _Heads:
    def __init__(self, vals):
        self.v = tuple(vals)

    def __add__(self, o):
        return _hmap(lambda a, b: a + b, self, o)

    def __radd__(self, o):
        return _hmap(lambda a, b: b + a, self, o)

    def __sub__(self, o):
        return _hmap(lambda a, b: a - b, self, o)

    def __rsub__(self, o):
        return _hmap(lambda a, b: b - a, self, o)

    def __mul__(self, o):
        return _hmap(lambda a, b: a * b, self, o)

    def __rmul__(self, o):
        return _hmap(lambda a, b: b * a, self, o)

    def __neg__(self):
        return _hmap(lambda a: -a, self)

    def __getitem__(self, idx):
        return _hmap(lambda a: a[idx], self)


def _hmap(fn, *args):
    n = next((len(a.v) for a in args if isinstance(a, _Heads)), None)
    if n is None:
        return fn(*args)
    return _Heads(fn(*[a.v[i] if isinstance(a, _Heads) else a for a in args]) for i in range(n))


def _dot(a, b, ca, cb):
    return _hmap(lambda x, y: lax.dot_general(x.astype(BF16), y.astype(BF16), (((ca,), (cb,)), ((), ())),
                                              preferred_element_type=F32), a, b)


def _nn(a, b):
    return _dot(a, b, 1, 0)


def _nt(a, b):
    return _dot(a, b, 1, 1)


def _tn(a, b):
    return _dot(a, b, 0, 0)


def _split(a):
    hi = _hmap(lambda x: x.astype(BF16), a)
    return hi, _hmap(lambda x, h: (x - h.astype(F32)).astype(BF16), a, hi)


def _dot3(a, b, ca, cb):
    ah, al = _split(a)
    bh, bl = _split(b)
    return _dot(ah, bh, ca, cb) + (_dot(ah, bl, ca, cb) + _dot(al, bh, ca, cb))


def _nn_exact(a, b):
    return _hmap(lambda y: lax.dot_general(a, y, (((1,), (0,)), ((), ())), precision=HIGHEST,
                                           preferred_element_type=F32), b)


def _exp(x):
    return _hmap(jnp.exp, x)


def _sum(x, axis):
    return _hmap(lambda a: jnp.sum(a, axis=axis, keepdims=True), x)


def _sigmoid(x):
    return jax.nn.sigmoid(x)


def _silu(x):
    return x * _sigmoid(x)


def _dsilu(x):
    s = _sigmoid(x)
    return s * (1.0 + x * (1.0 - s))


def _silu_and_grad(x):
    s = _sigmoid(x)
    return x * s, s * (1.0 + x * (1.0 - s))


def _softplus(x):
    return jnp.maximum(x, 0.0) + jnp.log(1.0 + jnp.exp(-jnp.abs(x)))


def _iota2(n, m, axis):
    return lax.broadcasted_iota(jnp.int32, (n, m), axis)


def _col2row(col, eye):
    return _hmap(lambda c: jnp.sum(eye * c, axis=0, keepdims=True), col)


def _row2col(row, eye):
    return _hmap(lambda r: jnp.sum(eye * r, axis=1, keepdims=True), row)


def _pick_lane(block, lane_idx):
    lane = _iota2(block.shape[0], block.shape[1], 1)
    return jnp.sum(jnp.where(lane == lane_idx, block, 0.0), axis=1, keepdims=True)


MM_TILE_M, MM_TILE_N, MM_TILE_K = 1024, 1408, 2048


def _tile(dim, cap):
    if dim <= cap:
        return dim
    t = cap - cap % 128
    while dim % t:
        t -= 128
    return t


def _mm(a, b, *, mode, out_dtype, res=None, after=(), b_rows=None, tile_m=MM_TILE_M, tile_n=MM_TILE_N, tile_k=MM_TILE_K,
        fused=None, name):
    b_mat_rows = b.shape[0] if b_rows is None else N_DEV * b_rows[1]
    if mode == "nn":
        (m, kd), n = a.shape, b.shape[-1]
        assert kd == b_mat_rows
    elif mode == "nt":
        (m, kd), n = a.shape, b_mat_rows
    else:
        (kd, m), n = a.shape, b.shape[-1]
    tm, tn, tk = _tile(m, tile_m), _tile(n, tile_n), _tile(kd, tile_k)
    assert m % tm == 0 and n % tn == 0 and kd % tk == 0, (m, n, kd, tm, tn, tk)
    nk = kd // tk
    ca, cb = {"nn": (1, 0), "nt": (1, 1), "tn": (0, 0)}[mode]

    kind = None if fused is None else fused[0]
    n_in = 2 + (res is not None) + (0 if fused is None else len(fused) - 1)
    n_out = 1 if fused is None else 2

    def body(*refs):
        a_ref, b_ref = refs[:2]
        r_ref = None if res is None else refs[2]
        extra = refs[2 + (res is not None):n_in]
        outs = refs[-1 - n_out:-1]
        o_ref, acc_ref = outs[0], refs[-1]
        k = pl.program_id(2)

        @pl.when(k == 0)
        def _():
            acc_ref[...] = jnp.zeros_like(acc_ref)

        b_tile = b_ref[...]
        if b_rows is not None:
            b_tile = b_tile.reshape(-1, b_tile.shape[-1])
        acc_ref[...] += _dot(a_ref[...], b_tile, ca, cb)

        @pl.when(k == nk - 1)
        def _():
            out = acc_ref[...]
            if r_ref is not None:
                out = out + r_ref[...].astype(F32)
            if kind == "ple":
                h1_ref, up_ref = extra
                o_ref[...] = out
                outs[1][...] = h1_ref[...] + up_ref[...] * _sigmoid(out)
            elif kind == "rms_bwd":
                h_ref, w_ref, res_ref = extra
                dx, dwt = _rms_bwd_math(h_ref[...], w_ref[...], out)
                o_ref[...] = res_ref[...] + dx

                @pl.when(pl.program_id(0) == 0)
                def _():
                    outs[1][...] = jnp.zeros_like(outs[1])

                outs[1][...] += jnp.sum(dwt, axis=0, keepdims=True)
            else:
                o_ref[...] = out.astype(o_ref.dtype)

    a_spec = pl.BlockSpec((tk, tm), lambda i, j, k: (k, i)) if mode == "tn" else pl.BlockSpec((tm, tk), lambda i, j, k: (i, k))
    if b_rows is None:
        b_spec = pl.BlockSpec((tn, tk), lambda i, j, k: (j, k)) if mode == "nt" else pl.BlockSpec((tk, tn), lambda i, j, k: (k, j))
    else:
        first, count = b_rows
        assert first % count == 0 and mode in ("nn", "nt")
        rb = first // count
        if mode == "nn":
            assert tk == kd
            b_spec = pl.BlockSpec((N_DEV, count, tn), lambda i, j, k: (0, rb, j))
        else:
            assert tn % count == 0
            b_spec = pl.BlockSpec((tn // count, count, tk), lambda i, j, k: (j, rb, k))
    o_spec = pl.BlockSpec((tm, tn), lambda i, j, k: (i, j))
    row_spec = pl.BlockSpec((1, tn), lambda i, j, k: (0, j))
    extra_specs, extra_args, out_specs, out_shape = [], (), o_spec, jax.ShapeDtypeStruct((m, n), out_dtype)
    sem = ("parallel", "parallel", "arbitrary")
    if kind == "ple":
        extra_specs, extra_args = [o_spec, o_spec], tuple(fused[1:])
        out_specs, out_shape = [o_spec, o_spec], [jax.ShapeDtypeStruct((m, n), F32)] * 2
    elif kind == "rms_bwd":
        assert tn == n
        extra_specs, extra_args = [o_spec, row_spec, o_spec], (fused[1], fused[2].reshape(1, n), fused[3])
        out_specs, out_shape = [o_spec, row_spec], [jax.ShapeDtypeStruct((m, n), F32), jax.ShapeDtypeStruct((1, n), F32)]
        sem = ("arbitrary", "arbitrary", "arbitrary")
    in_specs = ([a_spec, b_spec] + ([o_spec] if res is not None else []) + extra_specs
                + [pl.BlockSpec(memory_space=pl.ANY)] * len(after))
    args = (a, b) + ((res,) if res is not None else ()) + extra_args + tuple(after)
    return pl.pallas_call(
        body, name=name, grid=(m // tm, n // tn, nk), in_specs=in_specs, out_specs=out_specs, out_shape=out_shape,
        scratch_shapes=[pltpu.VMEM((tm, tn), F32)], compiler_params=_cp(*sem),
    )(*args)


ROW_TILE = 256


def _rms_fwd(h, w, *, name):
    s, d = h.shape
    tr = min(ROW_TILE, s)

    def body(h_ref, w_ref, o_ref):
        x = h_ref[...]
        r = lax.rsqrt(jnp.mean(x * x, axis=-1, keepdims=True) + NORM_EPS)
        o_ref[...] = (x * r * w_ref[...]).astype(o_ref.dtype)

    return pl.pallas_call(
        body, name=name, grid=(s // tr,),
        in_specs=[pl.BlockSpec((tr, d), lambda i: (i, 0)), pl.BlockSpec((1, d), lambda i: (0, 0))],
        out_specs=pl.BlockSpec((tr, d), lambda i: (i, 0)),
        out_shape=jax.ShapeDtypeStruct((s, d), BF16), compiler_params=_cp("parallel"),
    )(h, w.reshape(1, d))


def _rms_bwd_math(x, w, dy):
    d = x.shape[-1]
    r = lax.rsqrt(jnp.mean(x * x, axis=-1, keepdims=True) + NORM_EPS)
    gw = dy * w
    dx = r * gw - x * ((r * r * r) * (jnp.sum(gw * x, axis=-1, keepdims=True) / d))
    return dx, dy * x * r


def _final_fwd_bwd(h, w, tgt, *, name):
    s, d = h.shape
    tr = min(ROW_TILE, s)

    def body(h_ref, w_ref, t_ref, loss_ref, dh_ref, dw_ref):
        @pl.when(pl.program_id(0) == 0)
        def _():
            loss_ref[...] = jnp.zeros_like(loss_ref)
            dw_ref[...] = jnp.zeros_like(dw_ref)

        x = h_ref[...]
        wv = w_ref[...]
        r = lax.rsqrt(jnp.mean(x * x, axis=-1, keepdims=True) + NORM_EPS)
        err = x * r * wv - t_ref[...]
        row_loss = jnp.mean(err * err, axis=-1, keepdims=True)
        loss_ref[...] += 0.5 * jnp.sum(row_loss, axis=0, keepdims=True)
        dx, dwt = _rms_bwd_math(x, wv, err / d)
        dh_ref[...] = dx
        dw_ref[...] += jnp.sum(dwt, axis=0, keepdims=True)

    row = pl.BlockSpec((tr, d), lambda i: (i, 0))
    vec = pl.BlockSpec((1, d), lambda i: (0, 0))
    return pl.pallas_call(
        body, name=name, grid=(s // tr,), in_specs=[row, vec, row],
        out_specs=[pl.BlockSpec((1, 128), lambda i: (0, 0)), row, vec],
        out_shape=[jax.ShapeDtypeStruct((1, 128), F32), jax.ShapeDtypeStruct((s, d), F32),
                   jax.ShapeDtypeStruct((1, d), F32)],
        compiler_params=_cp("arbitrary"),
    )(h, w.reshape(1, d), tgt)


def _ple_bwd(dh2, gate_pre, up, *, name):
    s, d = dh2.shape
    tr = min(ROW_TILE, s)

    def body(d_ref, g_ref, u_ref, dup_ref, dgp_ref):
        dh = d_ref[...]
        gate = _sigmoid(g_ref[...])
        dup_ref[...] = (dh * gate).astype(BF16)
        dgp_ref[...] = (dh * u_ref[...] * gate * (1.0 - gate)).astype(BF16)

    row = pl.BlockSpec((tr, d), lambda i: (i, 0))
    return pl.pallas_call(body, name=name, grid=(s // tr,), in_specs=[row, row, row], out_specs=[row, row],
                          out_shape=[jax.ShapeDtypeStruct((s, d), BF16)] * 2, compiler_params=_cp("parallel"))(dh2, gate_pre, up)


def _head_norm_fwd(o, z, w):
    return _hmap(lambda x, zz: (x * lax.rsqrt(jnp.mean(x * x, axis=-1, keepdims=True) + NORM_EPS) * w * _silu(zz)).astype(BF16),
                 o, z)


def _head_norm_bwd(o, z, w, dy):
    dos, dzs, dw = [], [], jnp.zeros((1, HEAD_DIM), F32)
    for x, zz, g in zip(o.v, z.v, dy.v):
        r = lax.rsqrt(jnp.mean(x * x, axis=-1, keepdims=True) + NORM_EPS)
        silu_z, dsilu_z = _silu_and_grad(zz)
        don = g * silu_z
        dzs.append((g * (x * r * w) * dsilu_z).astype(BF16))
        gw = don * w
        dos.append(r * gw - x * ((r * r * r) * (jnp.sum(gw * x, axis=-1, keepdims=True) / HEAD_DIM)))
        dw = dw + jnp.sum(don * x * r, axis=0, keepdims=True)
    return _Heads(dos), _Heads(dzs), dw


def _conv_silu(x, w, s):
    row = _iota2(s, x.shape[1], 0)
    c = w[CONV_W - 1:CONV_W, :] * x
    for k in range(1, CONV_W):
        c = c + w[CONV_W - 1 - k:CONV_W - k, :] * jnp.where(row >= k, pltpu.roll(x, k, 0), 0.0)
    return c


def _dn_qkv_fwd(proj, conv_w, *, name):
    s = proj.shape[0]
    nb = 3 * N_HEADS

    def body(x_ref, w_ref, o_ref):
        j = pl.program_id(0)
        sv = _silu(_conv_silu(x_ref[...], w_ref[...], s))
        r = lax.rsqrt(jnp.sum(sv * sv, axis=-1, keepdims=True) + L2_EPS)
        scale = jnp.where(j < N_HEADS, HEAD_DIM ** -0.5, 1.0).astype(F32)
        o_ref[...] = jnp.where(j < 2 * N_HEADS, sv * r * scale, sv)

    return pl.pallas_call(
        body, name=name, grid=(nb,),
        in_specs=[pl.BlockSpec((s, HEAD_DIM), lambda j: (0, j)), pl.BlockSpec((CONV_W, HEAD_DIM), lambda j: (0, j))],
        out_specs=pl.BlockSpec((s, HEAD_DIM), lambda j: (0, j)),
        out_shape=jax.ShapeDtypeStruct((s, 3 * BR_WIDTH), F32), compiler_params=_cp("parallel"),
    )(proj, conv_w)


def _dn_qkv_bwd(proj, conv_w, dqkv, *, name):
    s = proj.shape[0]
    nb = 3 * N_HEADS

    def body(x_ref, w_ref, g_ref, dx_ref, dw_ref):
        j = pl.program_id(0)
        x, w, g = x_ref[...], w_ref[...], g_ref[...]
        c = _conv_silu(x, w, s)
        sv, dsv = _silu_and_grad(c)
        r = lax.rsqrt(jnp.sum(sv * sv, axis=-1, keepdims=True) + L2_EPS)
        scale = jnp.where(j < N_HEADS, HEAD_DIM ** -0.5, 1.0).astype(F32)
        ds_n = scale * (r * g - sv * ((r * r * r) * jnp.sum(g * sv, axis=-1, keepdims=True)))
        dc = jnp.where(j < 2 * N_HEADS, ds_n, g) * dsv
        row = _iota2(s, HEAD_DIM, 0)
        dx = w[CONV_W - 1:CONV_W, :] * dc
        dws = [jnp.sum(dc * x, axis=0, keepdims=True)]
        for k in range(1, CONV_W):
            dc_ahead = jnp.where(row < s - k, pltpu.roll(dc, s - k, 0), 0.0)
            dx = dx + w[CONV_W - 1 - k:CONV_W - k, :] * dc_ahead
            dws.append(jnp.sum(dc_ahead * x, axis=0, keepdims=True))
        dx_ref[...] = dx.astype(BF16)
        for k in range(CONV_W):
            dw_ref[CONV_W - 1 - k:CONV_W - k, :] = dws[k]

    blk = pl.BlockSpec((s, HEAD_DIM), lambda j: (0, j))
    wblk = pl.BlockSpec((CONV_W, HEAD_DIM), lambda j: (0, j))
    return pl.pallas_call(
        body, name=name, grid=(nb,), in_specs=[blk, wblk, blk], out_specs=[blk, wblk],
        out_shape=[jax.ShapeDtypeStruct((s, 3 * BR_WIDTH), BF16), jax.ShapeDtypeStruct((CONV_W, 3 * BR_WIDTH), F32)],
        compiler_params=_cp("parallel"),
    )(proj, conv_w, dqkv)


def _tri(n, kind):
    r, c = _iota2(n, n, 0), _iota2(n, n, 1)
    if kind == "lower":
        return (r >= c).astype(F32)
    if kind == "upper":
        return (r <= c).astype(F32)
    return (r == c).astype(F32)


GATE_TILE = 512


def _dn_gate_fwd(proj, a_log, dt_bias, *, name):
    s = proj.shape[0]
    tr = min(GATE_TILE, s)

    def body(b_ref, a_ref, al_ref, dt_ref, beta_ref, g_ref):
        beta_ref[...] = _sigmoid(b_ref[...])
        g = -jnp.exp(al_ref[...]) * _softplus(a_ref[...] + dt_ref[...])
        low = _tri(CHUNK, "lower")
        for c in range(tr // CHUNK):
            rows = slice(c * CHUNK, (c + 1) * CHUNK)
            g_ref[rows, :] = _nn_exact(low, g[rows, :])

    blk = lambda cb: pl.BlockSpec((tr, HEAD_DIM), lambda i: (i, cb))
    vec = pl.BlockSpec((1, HEAD_DIM), lambda i: (0, 0))
    out = pl.BlockSpec((tr, HEAD_DIM), lambda i: (i, 0))
    return pl.pallas_call(
        body, name=name, grid=(s // tr,), in_specs=[blk(C_B // HEAD_DIM), blk(C_A // HEAD_DIM), vec, vec],
        out_specs=[out, out], out_shape=[jax.ShapeDtypeStruct((s, HEAD_DIM), F32)] * 2, compiler_params=_cp("parallel"),
    )(proj, proj, a_log, dt_bias)


def _dn_gate_bwd(proj, a_log, dt_bias, dbeta, d_g, *, name):
    s = proj.shape[0]
    tr = min(GATE_TILE, s)

    def body(b_ref, a_ref, al_ref, dt_ref, dbeta_ref, dG_ref, db_ref, da_ref, dal_ref, ddt_ref):
        @pl.when(pl.program_id(0) == 0)
        def _():
            dal_ref[...] = jnp.zeros_like(dal_ref)
            ddt_ref[...] = jnp.zeros_like(ddt_ref)

        beta = _sigmoid(b_ref[...])
        db_ref[...] = (dbeta_ref[...] * beta * (1.0 - beta)).astype(BF16)
        pre = a_ref[...] + dt_ref[...]
        neg_ea = -jnp.exp(al_ref[...])
        up = _tri(CHUNK, "upper")
        d_g = dG_ref[...]
        dg = jnp.concatenate([_nn_exact(up, d_g[c * CHUNK:(c + 1) * CHUNK, :]) for c in range(tr // CHUNK)], axis=0)
        da = dg * neg_ea * _sigmoid(pre)
        da_ref[...] = da.astype(BF16)
        ddt_ref[...] += jnp.sum(da, axis=0, keepdims=True)
        dal_ref[...] += jnp.sum(dg * neg_ea * _softplus(pre), axis=0, keepdims=True)

    blk = lambda cb: pl.BlockSpec((tr, HEAD_DIM), lambda i: (i, cb))
    vec = pl.BlockSpec((1, HEAD_DIM), lambda i: (0, 0))
    io = pl.BlockSpec((tr, HEAD_DIM), lambda i: (i, 0))
    return pl.pallas_call(
        body, name=name, grid=(s // tr,),
        in_specs=[blk(C_B // HEAD_DIM), blk(C_A // HEAD_DIM), vec, vec, io, io], out_specs=[io, io, vec, vec],
        out_shape=[jax.ShapeDtypeStruct((s, HEAD_DIM), BF16)] * 2 + [jax.ShapeDtypeStruct((1, HEAD_DIM), F32)] * 2,
        compiler_params=_cp("arbitrary"),
    )(proj, proj, a_log, dt_bias, dbeta, d_g)


def _unit_lower_inverse(a_strict, eye):
    x = -a_strict
    t = x + eye
    p = x
    n = 2
    while n < CHUNK:
        p = _nn(p, p)
        t = t + _nn(t, p)
        n *= 2
    return t


def _rows(*xs):
    return _hmap(lambda *a: jnp.concatenate(a, axis=0), *xs)


def _lanes(*xs):
    return _hmap(lambda *a: jnp.concatenate(a, axis=1), *xs)


def _dn_chunk_common(q, k, v, gc, beta, st, with_qd_state, t_inv=None):
    c, d = CHUNK, HEAD_DIM
    eye = _tri(c, "eye")
    low = _tri(c, "lower")
    strict = low - eye
    grow = _col2row(gc, eye)
    dec = _hmap(lambda g_, gr: low * jnp.exp(low * (g_ - gr)), gc, grow)
    kb = k * beta
    kq = _nt(_rows(kb, q), k)
    a_mat = kq[0:c, :] * dec * strict
    qk = kq[c:2 * c, :] * dec
    if t_inv is None:
        t_inv = _unit_lower_inverse(a_mat, eye)
    e_g = _exp(gc)
    qd = q * e_g
    uw = _nn(t_inv, _lanes(v * beta, kb * e_g))
    u, w = uw[:, 0:d], uw[:, d:2 * d]
    last = (_iota2(c, 1, 0) == c - 1).astype(F32)
    g_last = _sum(gc * last, 0)
    e_t = _exp(g_last - gc)
    kt = k * e_t
    tail = _exp(g_last)
    if with_qd_state:
        ws = _nn(_rows(w, qd), st)
        vn, qds = u - ws[0:c, :], ws[c:2 * c, :]
    else:
        vn, qds = u - _nn(w, st), None
    return dict(eye=eye, low=low, strict=strict, dec=dec, kb=kb, a_mat=a_mat, t_inv=t_inv, e_g=e_g, u=u, w=w, uw=uw,
                qk=qk, qd=qd, qds=qds, last=last, e_t=e_t, kt=kt, tail=tail, vn=vn)


def _dn_chunk_fwd_math(q, k, v, gc, beta, st):
    m = _dn_chunk_common(q, k, v, gc, beta, st, True)
    o = m["qds"] + _nn(m["qk"], m["vn"])
    st2 = st * m["tail"] + _tn(m["kt"], m["vn"])
    return o, st2, m["t_inv"]


def _dn_chunk_bwd_math(q, k, v, gc, beta, st, do, dst2, t_inv=None):
    c, d = CHUNK, HEAD_DIM
    m = _dn_chunk_common(q, k, v, gc, beta, st, False, t_inv)
    eye, low, strict = m["eye"], m["low"], m["strict"]
    dvn = _tn(m["qk"], do) + _nn(m["kt"], dst2)
    dqk = _nt(do, m["vn"]) * low
    both = _rows(do, dvn)
    ds_both = _nt(both, st)
    dqd, dw = ds_both[0:c, :], -ds_both[c:2 * c, :]
    dst = _tn(_rows(m["qd"], -m["w"]), both) + dst2 * m["tail"]
    dkt = _nt(m["vn"], dst2)
    dtail = _sum(_sum(st * dst2, 1), 0)
    dvb_dkg = _tn(m["t_inv"], _lanes(dvn, dw))
    dvb, dkg = dvb_dkg[:, 0:d], dvb_dkg[:, d:2 * d]
    d_a = _nt(dvb_dkg, m["uw"]) * (-strict)
    dkk = d_a * m["dec"]
    dp = dqk * m["dec"]
    dpk = _rows(dp, dkk)
    dq_dkb = _nn(dpk, k)
    dq = dq_dkb[0:c, :] + dqd * m["e_g"]
    dkb = dq_dkb[c:2 * c, :] + dkg * m["e_g"]
    dk = _tn(dpk, _rows(q, m["kb"])) + dkb * beta + dkt * m["e_t"]
    dv = dvb * beta
    dbeta = _sum(dvb * v + dkb * k, 1)
    de_g = _sum(dkg * m["kb"] + dqd * q, 1)
    de_t = _sum(dkt * k, 1)
    mm = d_a * m["a_mat"] + dqk * m["qk"]
    dgc = (_sum(mm, 1) - _row2col(_sum(mm, 0), eye) + de_g * m["e_g"] - de_t * m["e_t"]
           + (_sum(de_t * m["e_t"], 0) + dtail * m["tail"]) * m["last"])
    return dq, dk, dv, dgc, dbeta, dst


def _heads_of(ref):
    return _Heads(ref[:, h * HEAD_DIM:(h + 1) * HEAD_DIM] for h in range(N_HEADS))


def _lanes_of(block):
    return _Heads(_pick_lane(block, h) for h in range(N_HEADS))


def _dn_chunk_fwd(qkv, gcs, beta, proj, norm_w, *, name):
    s = qkv.shape[0]
    n = s // CHUNK

    def body(q_ref, k_ref, v_ref, g_ref, b_ref, z_ref, w_ref, o_ref, st_out_ref, tinv_ref, y_ref, st_ref):
        @pl.when(pl.program_id(0) == 0)
        def _():
            st_ref[...] = jnp.zeros_like(st_ref)

        gblk, bblk = g_ref[...], b_ref[...]
        st = _Heads(st_ref[h] for h in range(N_HEADS))
        o, st2, t_inv = _dn_chunk_fwd_math(_heads_of(q_ref), _heads_of(k_ref), _heads_of(v_ref), _lanes_of(gblk),
                                           _lanes_of(bblk), st)
        y = _head_norm_fwd(o, _heads_of(z_ref), w_ref[...])
        for h in range(N_HEADS):
            st_out_ref[0, h] = st.v[h]
            tinv_ref[0, h] = t_inv.v[h].astype(BF16)
            o_ref[:, h * HEAD_DIM:(h + 1) * HEAD_DIM] = o.v[h]
            y_ref[:, h * HEAD_DIM:(h + 1) * HEAD_DIM] = y.v[h]
            st_ref[h] = st2.v[h]

    blk = lambda off: pl.BlockSpec((CHUNK, BR_WIDTH), lambda c: (c, off))
    sc = pl.BlockSpec((CHUNK, HEAD_DIM), lambda c: (c, 0))
    return pl.pallas_call(
        body, name=name, grid=(n,),
        in_specs=[blk(0), blk(1), blk(2), sc, sc, blk(C_Z // BR_WIDTH), pl.BlockSpec((1, HEAD_DIM), lambda c: (0, 0))],
        out_specs=[blk(0), pl.BlockSpec((1, N_HEADS, HEAD_DIM, HEAD_DIM), lambda c: (c, 0, 0, 0)),
                   pl.BlockSpec((1, N_HEADS, CHUNK, CHUNK), lambda c: (c, 0, 0, 0)), blk(0)],
        out_shape=[jax.ShapeDtypeStruct((s, BR_WIDTH), F32), jax.ShapeDtypeStruct((n, N_HEADS, HEAD_DIM, HEAD_DIM), F32),
                   jax.ShapeDtypeStruct((n, N_HEADS, CHUNK, CHUNK), BF16), jax.ShapeDtypeStruct((s, BR_WIDTH), BF16)],
        scratch_shapes=[pltpu.VMEM((N_HEADS, HEAD_DIM, HEAD_DIM), F32)],
        compiler_params=_cp("arbitrary"),
    )(qkv, qkv, qkv, gcs, beta, proj, norm_w.reshape(1, HEAD_DIM))


def _dn_chunk_bwd(qkv, gcs, beta, states, tinvs, o, proj, norm_w, dy, *, name):
    s = qkv.shape[0]
    n = s // CHUNK

    def body(q_ref, k_ref, v_ref, g_ref, b_ref, st_in_ref, tinv_ref, o_ref, z_ref, w_ref, dy_ref,
             dqkv_ref, dg_ref, dbeta_ref, dz_ref, dw_ref, dst_ref):
        @pl.when(pl.program_id(0) == 0)
        def _():
            dst_ref[...] = jnp.zeros_like(dst_ref)
            dw_ref[...] = jnp.zeros_like(dw_ref)

        do, dz, dw = _head_norm_bwd(_heads_of(o_ref), _heads_of(z_ref), w_ref[...], _heads_of(dy_ref))
        dw_ref[...] += dw

        gblk, bblk = g_ref[...], b_ref[...]
        lane = _iota2(CHUNK, HEAD_DIM, 1)
        dg_all = jnp.zeros((CHUNK, HEAD_DIM), F32)
        dbeta_all = jnp.zeros((CHUNK, HEAD_DIM), F32)
        dq, dk, dv, dgc, dbeta, dst = _dn_chunk_bwd_math(
            _heads_of(q_ref), _heads_of(k_ref), _heads_of(v_ref), _lanes_of(gblk), _lanes_of(bblk),
            _Heads(st_in_ref[0, h] for h in range(N_HEADS)), do,
            _Heads(dst_ref[h] for h in range(N_HEADS)), _Heads(tinv_ref[0, h] for h in range(N_HEADS)))
        for h in range(N_HEADS):
            dz_ref[:, h * HEAD_DIM:(h + 1) * HEAD_DIM] = dz.v[h]
            for part, val in enumerate((dq, dk, dv)):
                c0 = part * BR_WIDTH + h * HEAD_DIM
                dqkv_ref[:, c0:c0 + HEAD_DIM] = val.v[h]
            dg_all = jnp.where(lane == h, dgc.v[h], dg_all)
            dbeta_all = jnp.where(lane == h, dbeta.v[h], dbeta_all)
            dst_ref[h] = dst.v[h]
        dg_ref[...] = dg_all
        dbeta_ref[...] = dbeta_all

    blk = lambda off: pl.BlockSpec((CHUNK, BR_WIDTH), lambda c: (n - 1 - c, off))
    sc = pl.BlockSpec((CHUNK, HEAD_DIM), lambda c: (n - 1 - c, 0))
    vec = pl.BlockSpec((1, HEAD_DIM), lambda c: (0, 0))
    outs = pl.pallas_call(
        body, name=name, grid=(n,),
        in_specs=[blk(0), blk(1), blk(2), sc, sc,
                  pl.BlockSpec((1, N_HEADS, HEAD_DIM, HEAD_DIM), lambda c: (n - 1 - c, 0, 0, 0)),
                  pl.BlockSpec((1, N_HEADS, CHUNK, CHUNK), lambda c: (n - 1 - c, 0, 0, 0)), blk(0), blk(C_Z // BR_WIDTH),
                  vec, blk(0)],
        out_specs=[pl.BlockSpec((CHUNK, 3 * BR_WIDTH), lambda c: (n - 1 - c, 0)), sc, sc, blk(0), vec],
        out_shape=[jax.ShapeDtypeStruct((s, 3 * BR_WIDTH), F32)] + [jax.ShapeDtypeStruct((s, HEAD_DIM), F32)] * 2
        + [jax.ShapeDtypeStruct((s, BR_WIDTH), BF16), jax.ShapeDtypeStruct((1, HEAD_DIM), F32)],
        scratch_shapes=[pltpu.VMEM((N_HEADS, HEAD_DIM, HEAD_DIM), F32)],
        compiler_params=_cp("arbitrary"),
    )(qkv, qkv, qkv, gcs, beta, states, tinvs, o, proj, norm_w.reshape(1, HEAD_DIM), dy)
    return outs


def _hg_chunk_common(q, k, g):
    c, nb = CHUNK, CHUNK // SUB
    e_g = _exp(g)
    qd = q * e_g
    g_last = g[c - 1:c, :]
    e_t = _exp(g_last - g)
    kt = k * e_t
    tail = _exp(g_last)
    g_refs = [g[i * SUB:i * SUB + 1, :] for i in range(nb)]
    g_ref_rows = _hmap(lambda *rows: jnp.concatenate([jnp.broadcast_to(r, (SUB, r.shape[1])) for r in rows], axis=0), *g_refs)
    e_q = _exp(g - g_ref_rows)
    q_sc = q * e_q
    e_k = [_hmap(lambda gr, g_: jnp.exp(jnp.minimum(gr - g_, EXP_CLAMP)), g_refs[i], g) for i in range(nb)]
    k_sc_all = _rows(*[k * e_k[i] for i in range(nb)])
    row_blk = _iota2(c, 1, 0) // SUB
    masks = [(row_blk == i).astype(F32) for i in range(nb)]
    r_all = _nt(q_sc, k_sc_all)
    a_mat = r_all[:, 0:c] * masks[0]
    for i in range(1, nb):
        a_mat = a_mat + r_all[:, i * c:(i + 1) * c] * masks[i]
    a_mat = a_mat * _tri(c, "lower")
    return dict(e_g=e_g, qd=qd, e_t=e_t, kt=kt, tail=tail, q_sc=q_sc, k_sc_all=k_sc_all, e_q=e_q, e_k=e_k, masks=masks,
                a_mat=a_mat)


def _hg_chunk_fwd_math(q, k, v, g, stt):
    m = _hg_chunk_common(q, k, g)
    o = _nt(m["qd"], stt) + _nn(m["a_mat"], v)
    stt2 = stt * m["tail"] + _tn(v, m["kt"])
    return o, stt2


def _hg_chunk_bwd_math(q, k, v, g, stt, do, dstt2):
    c, nb = CHUNK, CHUNK // SUB
    m = _hg_chunk_common(q, k, g)
    stt2 = stt * m["tail"] + _tn(v, m["kt"])
    later = _sum(stt2 * dstt2, 0)
    dqd = _dot3(do, stt, 1, 0)
    dstt = _tn(do, m["qd"]) + dstt2 * m["tail"]
    d_a = _dot3(do, v, 1, 1) * _tri(c, "lower")
    dv = _tn(m["a_mat"], do) + _nt(m["kt"], dstt2)
    dkt = _dot3(v, dstt2, 1, 0)
    d_blk = _lanes(*[d_a * m["masks"][i] for i in range(nb)])
    dq = dqd * m["e_g"] + _dot3(d_blk, m["k_sc_all"], 1, 0) * m["e_q"]
    dks = _dot3(d_blk, m["q_sc"], 0, 0)
    dk = dkt * m["e_t"]
    for i in range(nb):
        dk = dk + dks[i * c:(i + 1) * c, :] * m["e_k"][i]
    db = q * dq - k * dk
    return dq, dk, dv, db, later, dstt


def _hg_chunk_fwd(proj, lb, norm_w, *, name):
    s = proj.shape[0]
    n = s // CHUNK

    def body(hq_ref, hf_ref, v_ref, lb_ref, z_ref, w_ref, o_ref, st_out_ref, q_out, k_out, lf_out, y_ref, st_ref):
        @pl.when(pl.program_id(0) == 0)
        def _():
            st_ref[...] = jnp.zeros_like(st_ref)

        f, lbv = hf_ref[...], lb_ref[...]
        q_all = _silu(hq_ref[...])
        k_all = (1.0 - lbv) * _sigmoid(-f)
        lf_all = jnp.log(lbv + (1.0 - lbv) * _sigmoid(f))
        q_out[...], k_out[...], lf_out[...] = q_all, k_all, lf_all
        st = _Heads(st_ref[h] for h in range(N_HEADS))
        g_all = _nn_exact(_tri(CHUNK, "lower"), lf_all)
        o, st2 = _hg_chunk_fwd_math(_heads_of(q_all), _heads_of(k_all), _heads_of(v_ref), _heads_of(g_all), st)
        y = _head_norm_fwd(o, _heads_of(z_ref), w_ref[...])
        for h in range(N_HEADS):
            st_out_ref[0, h] = st.v[h]
            o_ref[:, h * HEAD_DIM:(h + 1) * HEAD_DIM] = o.v[h]
            y_ref[:, h * HEAD_DIM:(h + 1) * HEAD_DIM] = y.v[h]
            st_ref[h] = st2.v[h]

    blk = lambda off: pl.BlockSpec((CHUNK, BR_WIDTH), lambda c: (c, off))
    return pl.pallas_call(
        body, name=name, grid=(n,),
        in_specs=[blk(C_HQ // BR_WIDTH), blk(C_HF // BR_WIDTH), blk(C_HI // BR_WIDTH), pl.BlockSpec((1, BR_WIDTH), lambda c: (0, 0)),
                  blk(C_HZ // BR_WIDTH), pl.BlockSpec((1, HEAD_DIM), lambda c: (0, 0))],
        out_specs=[blk(0), pl.BlockSpec((1, N_HEADS, HEAD_DIM, HEAD_DIM), lambda c: (c, 0, 0, 0)), blk(0), blk(0), blk(0), blk(0)],
        out_shape=[jax.ShapeDtypeStruct((s, BR_WIDTH), F32), jax.ShapeDtypeStruct((n, N_HEADS, HEAD_DIM, HEAD_DIM), F32)]
        + [jax.ShapeDtypeStruct((s, BR_WIDTH), F32)] * 3 + [jax.ShapeDtypeStruct((s, BR_WIDTH), BF16)],
        scratch_shapes=[pltpu.VMEM((N_HEADS, HEAD_DIM, HEAD_DIM), F32)],
        compiler_params=_cp("arbitrary"),
    )(proj, proj, proj, lb, proj, norm_w.reshape(1, HEAD_DIM))


def _hg_chunk_bwd(proj, lb, qh, kh, lf, states, o, norm_w, dy, *, name):
    s = proj.shape[0]
    n = s // CHUNK

    def body(hq_ref, hf_ref, v_ref, lb_ref, q_ref, k_ref, lf_ref, st_in_ref, o_ref, z_ref, w_ref, dy_ref,
             dhq_ref, dhf_ref, dhi_ref, dz_ref, dlb_ref, dw_ref, dst_ref):
        @pl.when(pl.program_id(0) == 0)
        def _():
            dst_ref[...] = jnp.zeros_like(dst_ref)
            dlb_ref[...] = jnp.zeros_like(dlb_ref)
            dw_ref[...] = jnp.zeros_like(dw_ref)

        do, dz, dw = _head_norm_bwd(_heads_of(o_ref), _heads_of(z_ref), w_ref[...], _heads_of(dy_ref))
        dw_ref[...] += dw

        g_all = _nn_exact(_tri(CHUNK, "lower"), lf_ref[...])
        dq, dk, dv, db, later, dst = _hg_chunk_bwd_math(
            _heads_of(q_ref), _heads_of(k_ref), _heads_of(v_ref), _heads_of(g_all),
            _Heads(st_in_ref[0, h] for h in range(N_HEADS)), do,
            _Heads(dst_ref[h] for h in range(N_HEADS)))
        dlf = _nn_exact(_tri(CHUNK, "upper"), jnp.concatenate(db.v, axis=1)) + jnp.concatenate(later.v, axis=1)
        dq_all, dk_all = jnp.concatenate(dq.v, axis=1), jnp.concatenate(dk.v, axis=1)
        f, lbv = hf_ref[...], lb_ref[...]
        dhq_ref[...] = (dq_all * _dsilu(hq_ref[...])).astype(BF16)
        sp, sn = _sigmoid(f), _sigmoid(-f)
        dlf_over = dlf / (lbv + (1.0 - lbv) * sp)
        dhf_ref[...] = (dlf_over * (1.0 - lbv) * sp * sn - dk_all * (1.0 - lbv) * sn * (1.0 - sn)).astype(BF16)
        dlb_ref[...] += jnp.sum(dlf_over * (1.0 - sp) - dk_all * sn, axis=0, keepdims=True)
        for h in range(N_HEADS):
            dhi_ref[:, h * HEAD_DIM:(h + 1) * HEAD_DIM] = dv.v[h].astype(BF16)
            dz_ref[:, h * HEAD_DIM:(h + 1) * HEAD_DIM] = dz.v[h]
            dst_ref[h] = dst.v[h]

    blk = lambda off: pl.BlockSpec((CHUNK, BR_WIDTH), lambda c: (n - 1 - c, off))
    vec = pl.BlockSpec((1, BR_WIDTH), lambda c: (0, 0))
    wvec = pl.BlockSpec((1, HEAD_DIM), lambda c: (0, 0))
    return pl.pallas_call(
        body, name=name, grid=(n,),
        in_specs=[blk(C_HQ // BR_WIDTH), blk(C_HF // BR_WIDTH), blk(C_HI // BR_WIDTH), vec, blk(0), blk(0), blk(0),
                  pl.BlockSpec((1, N_HEADS, HEAD_DIM, HEAD_DIM), lambda c: (n - 1 - c, 0, 0, 0)), blk(0), blk(C_HZ // BR_WIDTH),
                  wvec, blk(1)],
        out_specs=[blk(0), blk(0), blk(0), blk(0), vec, wvec],
        out_shape=[jax.ShapeDtypeStruct((s, BR_WIDTH), BF16)] * 4 + [jax.ShapeDtypeStruct((1, BR_WIDTH), F32),
                                                                    jax.ShapeDtypeStruct((1, HEAD_DIM), F32)],
        scratch_shapes=[pltpu.VMEM((N_HEADS, HEAD_DIM, HEAD_DIM), F32)],
        compiler_params=_cp("arbitrary"),
    )(proj, proj, proj, lb, qh, kh, lf, states, o, proj, norm_w.reshape(1, HEAD_DIM), dy)


_ANY = pl.BlockSpec(memory_space=pl.ANY)
_MESH = pl.DeviceIdType.MESH


def _all_gather(x_local, *, name, after=()):
    n_after = len(after)

    def body(x_ref, *refs):
        out_ref, send_sems, recv_sems, local_sem = refs[n_after:]
        x, y, c = lax.axis_index("x"), lax.axis_index("y"), lax.axis_index("c")
        me, sibling = (x, y, c), (x, y, 1 - c)
        n1 = (x ^ (1 - c), y ^ c)
        n2 = (x ^ c, y ^ (1 - c))
        dg = (1 - x, 1 - y)

        def slot(px, py, pc):
            return out_ref.at[4 * px + 2 * py + pc]

        def copy(k, block, to, src=None):
            return pltpu.make_async_remote_copy(
                src_ref=slot(*block) if src is None else src, dst_ref=slot(*block),
                send_sem=send_sems.at[k], recv_sem=recv_sems.at[k], device_id=to, device_id_type=_MESH)

        mine = pltpu.make_async_copy(x_ref, slot(*me), local_sem)
        mine.start()
        first = [copy(0, me, sibling, src=x_ref), copy(1, me, (*n1, c), src=x_ref), copy(2, me, (*n2, c), src=x_ref)]
        for cp in first:
            cp.start()
        copy(2, (*n2, c), me).wait_recv()
        forward = copy(3, (*n2, c), (*n1, c))
        forward.start()
        passed = [copy(5, (*n2, c), sibling)]
        passed[0].start()
        copy(1, (*n1, c), me).wait_recv()
        passed.append(copy(4, (*n1, c), sibling))
        passed[1].start()
        copy(3, (*dg, c), me).wait_recv()
        passed.append(copy(6, (*dg, c), sibling))
        passed[2].start()
        copy(0, sibling, me).wait_recv()
        copy(4, (*n2, 1 - c), me).wait_recv()
        copy(5, (*n1, 1 - c), me).wait_recv()
        copy(6, (*dg, 1 - c), me).wait_recv()
        for cp in first + [forward] + passed:
            cp.wait_send()
        mine.wait()

    return pl.pallas_call(
        body, name=name, out_shape=jax.ShapeDtypeStruct((N_DEV,) + x_local.shape, x_local.dtype),
        in_specs=[_ANY] * (1 + n_after), out_specs=_ANY,
        scratch_shapes=[pltpu.SemaphoreType.DMA((7,)), pltpu.SemaphoreType.DMA((7,)), pltpu.SemaphoreType.DMA],
    )(x_local, *after)


_HBM = pl.BlockSpec(memory_space=pltpu.HBM)
_SEM = pl.BlockSpec(memory_space=pltpu.SEMAPHORE)
_EFFECT = pltpu.SideEffectType.DATAFLOW_SIDE_EFFECTING


def _peers():
    x, y, c = lax.axis_index("x"), lax.axis_index("y"), lax.axis_index("c")
    out = []
    for k in range(1, N_DEV):
        px, py, pc = x ^ ((k >> 2) & 1), y ^ ((k >> 1) & 1), c ^ (k & 1)
        out.append(((px, py, pc), 4 * px + 2 * py + pc))
    return 4 * x + 2 * y + c, out


def _push_copies(src_ref, land_ref, send_sems, recv_sems, broadcast):
    my, peers = _peers()
    pairs = []
    for k, (pos, idx) in enumerate(peers):
        src = src_ref if broadcast else src_ref.at[idx]
        send = pltpu.make_async_remote_copy(src_ref=src, dst_ref=land_ref.at[my], send_sem=send_sems.at[k],
                                            recv_sem=recv_sems.at[k], device_id=pos, device_id_type=_MESH)
        recv = pltpu.make_async_remote_copy(src_ref=src, dst_ref=land_ref.at[idx], send_sem=send_sems.at[k],
                                            recv_sem=recv_sems.at[k], device_id=pos, device_id_type=_MESH)
        pairs.append((send, recv))
    return pairs


def _push_start(src, land, *, broadcast, name, after=()):
    n_after = len(after)

    def body(src_ref, land_ref, *refs):
        send_sems, recv_sems, _, _, token = refs[n_after:]
        for send, _ in _push_copies(src_ref, land_ref, send_sems, recv_sems, broadcast):
            send.start()
        token[...] = jnp.zeros_like(token)

    return pl.pallas_call(
        body, name=name,
        out_shape=(pltpu.SemaphoreType.DMA((N_DEV - 1,)), pltpu.SemaphoreType.DMA((N_DEV - 1,)),
                   pltpu.HBM(src.shape, src.dtype), pltpu.HBM(land.shape, land.dtype), jax.ShapeDtypeStruct((8, 128), F32)),
        in_specs=(_HBM, _HBM) + (_ANY,) * n_after, out_specs=(_SEM, _SEM, _HBM, _HBM, pl.BlockSpec(memory_space=pltpu.VMEM)),
        input_output_aliases={0: 2, 1: 3}, compiler_params=pltpu.CompilerParams(has_side_effects=_EFFECT),
    )(pltpu.with_memory_space_constraint(src, pltpu.HBM), pltpu.with_memory_space_constraint(land, pltpu.HBM), *after)


def _push_wait(handle, after, *, broadcast, name):
    send_sems, recv_sems, src_thru, land_thru, _ = handle

    def body(src_ref, land_ref, send_sems, recv_sems, *rest):
        for send, recv in _push_copies(src_ref, land_ref, send_sems, recv_sems, broadcast):
            send.wait_send()
            recv.wait_recv()

    return pl.pallas_call(
        body, name=name,
        out_shape=(pltpu.HBM(src_thru.shape, src_thru.dtype), pltpu.HBM(land_thru.shape, land_thru.dtype)),
        in_specs=(_HBM, _HBM, _SEM, _SEM) + (_ANY,) * len(after), out_specs=(_HBM, _HBM),
        input_output_aliases={0: 0, 1: 1}, compiler_params=pltpu.CompilerParams(has_side_effects=_EFFECT),
    )(src_thru, land_thru, send_sems, recv_sems, *after)[1]


def _relay_copies(src_ref, land_ref, sems_a, sems_b):
    x, y, c = lax.axis_index("x"), lax.axis_index("y"), lax.axis_index("c")
    slot = lambda px, py, pc: land_ref.at[4 * px + 2 * py + pc]
    chips = [(1 - x, y), (x, 1 - y), (1 - x, 1 - y)]
    (send_a, recv_a), (send_b, recv_b) = sems_a, sems_b

    def copy(sems, k, src, dst_slot, to):
        return pltpu.make_async_remote_copy(src_ref=src, dst_ref=dst_slot, send_sem=sems[0].at[k], recv_sem=sems[1].at[k],
                                            device_id=to, device_id_type=_MESH)

    first = [copy((send_a, recv_a), 0, src_ref, slot(x, y, c), (x, y, 1 - c))]
    first += [copy((send_a, recv_a), 1 + j, src_ref, slot(x, y, c), (*chip, c)) for j, chip in enumerate(chips)]
    first_in = [copy((send_a, recv_a), 0, src_ref, slot(x, y, 1 - c), (x, y, 1 - c))]
    first_in += [copy((send_a, recv_a), 1 + j, src_ref, slot(*chip, c), (*chip, c)) for j, chip in enumerate(chips)]
    relay = [copy((send_b, recv_b), j, slot(*chip, c), slot(*chip, c), (x, y, 1 - c)) for j, chip in enumerate(chips)]
    relay_in = [copy((send_b, recv_b), j, slot(*chip, 1 - c), slot(*chip, 1 - c), (x, y, 1 - c)) for j, chip in enumerate(chips)]
    return first, first_in, relay, relay_in


def _relay_start(src, land, *, name, after=()):
    n_after = len(after)

    def body(src_ref, land_ref, *refs):
        send_a, recv_a, _, _, token = refs[n_after:]
        for cp in _relay_copies(src_ref, land_ref, (send_a, recv_a), (send_a, recv_a))[0]:
            cp.start()
        token[...] = jnp.zeros_like(token)

    send_a, recv_a, src_thru, land_thru, token = pl.pallas_call(
        body, name=name,
        out_shape=(pltpu.SemaphoreType.DMA((4,)), pltpu.SemaphoreType.DMA((4,)), pltpu.HBM(src.shape, src.dtype),
                   pltpu.HBM(land.shape, land.dtype), jax.ShapeDtypeStruct((8, 128), F32)),
        in_specs=(_HBM, _HBM) + (_ANY,) * n_after, out_specs=(_SEM, _SEM, _HBM, _HBM, pl.BlockSpec(memory_space=pltpu.VMEM)),
        input_output_aliases={0: 2, 1: 3}, compiler_params=pltpu.CompilerParams(has_side_effects=_EFFECT),
    )(pltpu.with_memory_space_constraint(src, pltpu.HBM), pltpu.with_memory_space_constraint(land, pltpu.HBM), *after)
    return (send_a, recv_a), src_thru, land_thru, token


def _relay_mid(handle, after, *, name):
    sems_a, src_thru, land_thru, _ = handle
    n_after = len(after)

    def body(src_ref, land_ref, send_a, recv_a, *refs):
        send_b, recv_b, _, _, token = refs[n_after:]
        _, first_in, relay, _ = _relay_copies(src_ref, land_ref, (send_a, recv_a), (send_b, recv_b))
        for j in range(3):
            first_in[1 + j].wait_recv()
            relay[j].start()
        token[...] = jnp.zeros_like(token)

    send_b, recv_b, src2, land2, token = pl.pallas_call(
        body, name=name,
        out_shape=(pltpu.SemaphoreType.DMA((3,)), pltpu.SemaphoreType.DMA((3,)), pltpu.HBM(src_thru.shape, src_thru.dtype),
                   pltpu.HBM(land_thru.shape, land_thru.dtype), jax.ShapeDtypeStruct((8, 128), F32)),
        in_specs=(_HBM, _HBM, _SEM, _SEM) + (_ANY,) * n_after,
        out_specs=(_SEM, _SEM, _HBM, _HBM, pl.BlockSpec(memory_space=pltpu.VMEM)),
        input_output_aliases={0: 2, 1: 3}, compiler_params=pltpu.CompilerParams(has_side_effects=_EFFECT),
    )(src_thru, land_thru, *sems_a, *after)
    return sems_a, (send_b, recv_b), src2, land2, token


def _relay_wait(handle, after, *, name):
    sems_a, sems_b, src_thru, land_thru, _ = handle

    def body(src_ref, land_ref, send_a, recv_a, send_b, recv_b, *rest):
        first, first_in, relay, relay_in = _relay_copies(src_ref, land_ref, (send_a, recv_a), (send_b, recv_b))
        first_in[0].wait_recv()
        for cp in relay_in:
            cp.wait_recv()
        for cp in first + relay:
            cp.wait_send()

    return pl.pallas_call(
        body, name=name,
        out_shape=(pltpu.HBM(src_thru.shape, src_thru.dtype), pltpu.HBM(land_thru.shape, land_thru.dtype)),
        in_specs=(_HBM, _HBM, _SEM, _SEM, _SEM, _SEM) + (_ANY,) * len(after), out_specs=(_HBM, _HBM),
        input_output_aliases={0: 0, 1: 1}, compiler_params=pltpu.CompilerParams(has_side_effects=_EFFECT),
    )(src_thru, land_thru, *sems_a, *sems_b, *after)[1]


def _adamw(parts, row_off, w, m, v, *, layer=0, n_layers=1, prev=None, name, tr):
    rows, c = w.shape
    r = rows // n_layers
    np_ = parts.shape[0]
    tr = min(tr, r)
    assert r % tr == 0 and row_off % tr == 0
    ob, lb = row_off // tr, layer * (r // tr)
    c1 = 1.0 - ADAM_B1 ** ADAM_STEP
    c2 = 1.0 - ADAM_B2 ** ADAM_STEP
    n_prev = 0 if prev is None else 4

    def body(p_ref, w_ref, m_ref, v_ref, *refs):
        g_ref, d_ref, nm_ref, nv_ref = refs[n_prev:]
        g = p_ref[0].astype(F32)
        for s in range(1, np_):
            g = g + p_ref[s].astype(F32)
        wv = w_ref[...]
        m2 = ADAM_B1 * m_ref[...] + (1.0 - ADAM_B1) * g
        v2 = ADAM_B2 * v_ref[...] + (1.0 - ADAM_B2) * jnp.square(g)
        m_hat = m2 / c1
        v_hat = v2 / c2
        g_ref[...] = g
        d_ref[...] = -ADAM_LR * (m_hat / (jnp.sqrt(v_hat) + ADAM_EPS) + ADAM_WD * wv)
        nm_ref[...] = m2
        nv_ref[...] = v2

    blk = pl.BlockSpec((tr, c), lambda i: (lb + i, 0))
    return pl.pallas_call(
        body, name=name, grid=(r // tr,),
        in_specs=[pl.BlockSpec((np_, tr, c), lambda i: (0, ob + i, 0)), blk, blk, blk] + [_ANY] * n_prev,
        out_specs=[blk] * 4, out_shape=[jax.ShapeDtypeStruct((rows, c), F32)] * 4,
        input_output_aliases={4 + i: i for i in range(n_prev)}, compiler_params=_cp("parallel"),
    )(parts, w, m, v, *(prev or ()))


def _sum_parts(parts, *, name, after=()):
    np_, r, c = parts.shape

    def body(p_ref, *refs):
        o_ref = refs[-1]
        g = p_ref[0]
        for s in range(1, np_):
            g = g + p_ref[s]
        o_ref[...] = g

    vmem = pl.BlockSpec(memory_space=pltpu.VMEM)
    return pl.pallas_call(body, name=name, in_specs=[vmem] + [_ANY] * len(after), out_specs=vmem,
                          out_shape=jax.ShapeDtypeStruct((r, c), F32))(parts, *after)


def _pack(arrs):
    rows = []
    for a in arrs:
        f = a.reshape(-1).astype(F32)
        pad = (-f.shape[0]) % 128
        rows.append(jnp.pad(f, (0, pad)).reshape(-1, 128))
    out = jnp.concatenate(rows, axis=0)
    return jnp.pad(out, ((0, (-out.shape[0]) % 8), (0, 0)))


def _unpack(packed, shapes):
    outs, r0 = [], 0
    for shp in shapes:
        n = 1
        for d in shp:
            n *= d
        nr = -(-n // 128)
        outs.append(packed[r0:r0 + nr].reshape(-1)[:n].reshape(shp))
        r0 += nr
    return outs


_WIN_PIECES = ((0, 4096, 0), (4112, 8208, 0), (4096, 4104, HEAD_DIM - N_HEADS), (4104, 4112, HEAD_DIM - N_HEADS))


RELAYOUT_TILE = 256
LAST_SPLIT = 4
OTHER_SPLIT = 2


def _win_from_shards(shards, *, name):
    k = shards.shape[1]
    tr = min(RELAYOUT_TILE, k)

    def body(x_ref, o_ref):
        cols = []
        for lo, hi, pad in _WIN_PIECES:
            for j in range(N_DEV):
                a, b = max(lo, j * SHARD_IN), min(hi, (j + 1) * SHARD_IN)
                if a < b:
                    cols.append(x_ref[j, :, a - j * SHARD_IN:b - j * SHARD_IN])
            if pad:
                cols.append(jnp.zeros((tr, pad), x_ref.dtype))
        o_ref[...] = jnp.concatenate(cols, axis=1)

    return pl.pallas_call(
        body, name=name, grid=(k // tr,), in_specs=[pl.BlockSpec((N_DEV, tr, SHARD_IN), lambda i: (0, i, 0))],
        out_specs=pl.BlockSpec((tr, N_PROJ), lambda i: (i, 0)), out_shape=jax.ShapeDtypeStruct((k, N_PROJ), shards.dtype),
        compiler_params=_cp("parallel"),
    )(shards)


def _win_to_shards(g, *, name):
    k = g.shape[0]
    tr = min(RELAYOUT_TILE, k)
    starts, off = [], 0
    for lo, hi, pad in _WIN_PIECES:
        starts.append((lo, hi, off))
        off += hi - lo + pad

    def body(g_ref, o_ref):
        for j in range(N_DEV):
            cols = []
            for lo, hi, off in sorted(starts):
                a, b = max(lo, j * SHARD_IN), min(hi, (j + 1) * SHARD_IN)
                if a < b:
                    cols.append(g_ref[:, off + a - lo:off + b - lo])
            o_ref[j] = jnp.concatenate(cols, axis=1)

    return pl.pallas_call(
        body, name=name, grid=(k // tr,), in_specs=[pl.BlockSpec((tr, N_PROJ), lambda i: (i, 0))],
        out_specs=pl.BlockSpec((N_DEV, tr, SHARD_IN), lambda i: (0, i, 0)),
        out_shape=jax.ShapeDtypeStruct((N_DEV, k, SHARD_IN), g.dtype), compiler_params=_cp("parallel"),
    )(g)


def _lower_bounds(logits):
    probs = jax.nn.softmax(logits.astype(F32), axis=0)
    return jnp.cumsum(probs, axis=0) - probs[0]


def _pad_lanes(vec8):
    return jnp.pad(vec8.reshape(1, N_HEADS), ((0, 0), (0, HEAD_DIM - N_HEADS)))


def kernel(x, p, norm_w, w_in, dn_conv_w, dn_A_log, dn_dt_bias, dn_norm_w, hg_lb_logits, hg_norm_w, w_out, w_ple_up, w_ple_gate, final_norm_w, loss_target, m_norm_w, m_w_in, m_dn_conv_w, m_dn_A_log, m_dn_dt_bias, m_dn_norm_w, m_hg_lb_logits, m_hg_norm_w, m_w_out, m_w_ple_up, m_w_ple_gate, m_final_norm_w, v_norm_w, v_w_in, v_dn_conv_w, v_dn_A_log, v_dn_dt_bias, v_dn_norm_w, v_hg_lb_logits, v_hg_norm_w, v_w_out, v_w_ple_up, v_w_ple_gate, v_final_norm_w):
    depth = norm_w.shape[0]
    my = 4 * lax.axis_index("x") + 2 * lax.axis_index("y") + lax.axis_index("c")
    h = x[0]
    tgt = loss_target[0]
    rows_out = D_MODEL // N_DEV
    up_rows = PLE_DIM * (D_MODEL // N_DEV) // D_MODEL
    g_off, u_off = rows_out, 2 * rows_out

    def own_slot(block):
        return lax.dynamic_update_index_in_dim(lax.empty((N_DEV,) + block.shape, block.dtype), block, my, 0)

    win_bf = w_in.astype(BF16)
    rest_bf = [jnp.concatenate([w_out[l], w_ple_gate[l], w_ple_up[l].reshape(up_rows, D_MODEL)], axis=0).astype(BF16)
               for l in range(depth)]
    conv_all = _all_gather(dn_conv_w, name="gather_conv_w")
    conv_full = conv_all.transpose(1, 2, 0, 3).reshape(depth, CONV_W, 3 * BR_WIDTH)
    win_all = {0: _all_gather(win_bf[0], name="gather_w_in_l0", after=[conv_all])}
    pending, relayed = {}, {}
    last = win_all[0]
    for l in range(depth):
        if l > 0:
            relayed["win", l] = _relay_start(win_bf[l], own_slot(win_bf[l]), after=[last], name=f"gather_w_in_l{l}_first")
            last = relayed["win", l][3]
        if l == 0:
            relayed["rest", l] = _relay_start(rest_bf[l], own_slot(rest_bf[l]), after=[last], name=f"gather_rest_l{l}_first")
            last = relayed["rest", l][3]
        else:
            pending["rest", l] = _push_start(rest_bf[l], own_slot(rest_bf[l]), broadcast=True, after=[last],
                                             name=f"gather_rest_l{l}_start")
            last = pending["rest", l][4]
    order_tok = last[0, 0]
    lbs = _lower_bounds(hg_lb_logits)

    saved = []
    weights = []
    for l in range(depth):
        tag = f"l{l}"
        if l > 0:
            win_all[l] = _relay_wait(relayed["win", l], [h], name=f"gather_w_in_{tag}_wait")
        wi = _win_from_shards(win_all[l], name=f"w_in_layout_{tag}")
        nw = norm_w[l] + order_tok if l == 0 else norm_w[l]
        hn = _rms_fwd(h, nw, name=f"rms_fwd_{tag}")
        proj = _mm(hn, wi, mode="nn", out_dtype=F32, name=f"mm_proj_{tag}")
        al, dt = _pad_lanes(dn_A_log[l]), _pad_lanes(dn_dt_bias[l])
        qkv = _dn_qkv_fwd(proj, conv_full[l], name=f"dn_qkv_fwd_{tag}")
        if ("rest", l) in relayed:
            relayed["rest", l] = _relay_mid(relayed["rest", l], [qkv], name=f"gather_rest_{tag}_relay")
            al = al + relayed["rest", l][4][0, 0]
        beta, gcs = _dn_gate_fwd(proj, al, dt, name=f"dn_gate_fwd_{tag}")
        o_dn, st_dn, tinv_dn, y_dn = _dn_chunk_fwd(qkv, gcs, beta, proj, dn_norm_w[l], name=f"dn_chunk_fwd_{tag}")
        lb = lbs[l].reshape(1, BR_WIDTH)
        o_hg, st_hg, qh, kh, lf, y_hg = _hg_chunk_fwd(proj, lb, hg_norm_w[l], name=f"hg_chunk_fwd_{tag}")
        y = jnp.concatenate([y_dn, y_hg], axis=1)
        if ("rest", l) in relayed:
            rest_all = _relay_wait(relayed["rest", l], [y], name=f"gather_rest_{tag}_wait")
        else:
            rest_all = _push_wait(pending["rest", l], [y], broadcast=True, name=f"gather_rest_{tag}_wait")
        w_out_rows, w_gate_rows = (0, rows_out), (g_off, rows_out)
        wu = rest_all[:, u_off:u_off + up_rows].reshape(N_DEV, PLE_DIM, D_MODEL // N_DEV).transpose(1, 0, 2).reshape(PLE_DIM, D_MODEL)
        weights.append((wi, rest_all, wu))
        h1 = _mm(y, rest_all, mode="nn", b_rows=w_out_rows, out_dtype=F32, res=h, name=f"mm_out_{tag}")
        pin = []
        if ("win", l + 1) in relayed:
            relayed["win", l + 1] = _relay_mid(relayed["win", l + 1], [h1], name=f"gather_w_in_l{l + 1}_relay")
            pin = [relayed["win", l + 1][4]]
        up = _mm(p[l, 0], wu, mode="nn", out_dtype=F32, name=f"mm_up_{tag}")
        gp, h2 = _mm(h1, rest_all, mode="nn", b_rows=w_gate_rows, out_dtype=F32, after=pin, tile_m=512, fused=("ple", h1, up),
                     name=f"mm_gate_{tag}")
        saved.append(dict(h=h, hn=hn, proj=proj, qkv=qkv, beta=beta, gcs=gcs, st_dn=st_dn, tinv_dn=tinv_dn, qh=qh, kh=kh, lf=lf,
                          st_hg=st_hg, o_dn=o_dn, o_hg=o_hg, y=y, h1=h1, gp=gp, up=up, al=al, dt=dt, lb=lb))
        h = h2

    loss_row, dh, d_final_w = _final_fwd_bwd(h, final_norm_w, tgt, name="final_norm_loss")

    d_norm_w, d_alog, d_dt, d_dn_nw, d_hg_nw, d_lb, d_conv = ([None] * depth for _ in range(7))
    sent = {}
    for l in reversed(range(depth)):
        wi, rest_all, wu = weights[l]
        sv = saved[l]
        tag = f"l{l}"
        dup, dgp = _ple_bwd(dh, sv["gp"], sv["up"], name=f"ple_bwd_{tag}")
        d_wu = _mm(p[l, 0], dup, mode="tn", out_dtype=BF16, name=f"mm_dwup_{tag}")
        d_wg = _mm(sv["h1"], dgp, mode="tn", out_dtype=BF16, name=f"mm_dwgate_{tag}")
        dh1 = _mm(dgp, rest_all, mode="nt", b_rows=(g_off, rows_out), out_dtype=F32, res=dh, name=f"mm_dh1_{tag}")
        d_wo = _mm(sv["y"], dh1, mode="tn", out_dtype=BF16, name=f"mm_dwout_{tag}")
        parts_rest = jnp.concatenate(
            [d_wo.reshape(N_DEV, rows_out, D_MODEL), d_wg.reshape(N_DEV, rows_out, D_MODEL),
             d_wu.reshape(PLE_DIM, N_DEV, D_MODEL // N_DEV).transpose(1, 0, 2).reshape(N_DEV, up_rows, D_MODEL)], axis=1)
        sent["rest", l] = _push_start(parts_rest, own_slot(parts_rest[my]), broadcast=False, name=f"exchange_rest_{tag}_start")
        dy = _mm(dh1, rest_all, mode="nt", b_rows=(0, rows_out), out_dtype=F32, name=f"mm_dy_{tag}")
        dn_nw = dn_norm_w[l] + sent["rest", l][4][0, 0]
        dqkv, d_gc, dbeta, dz_dn, d_dn_nw[l] = _dn_chunk_bwd(sv["qkv"], sv["gcs"], sv["beta"], sv["st_dn"], sv["tinv_dn"],
                                                             sv["o_dn"], sv["proj"], dn_nw, dy, name=f"dn_chunk_bwd_{tag}")
        dqkv_pre, d_conv[l] = _dn_qkv_bwd(sv["proj"], conv_full[l], dqkv, name=f"dn_qkv_bwd_{tag}")
        db, da, d_alog[l], d_dt[l] = _dn_gate_bwd(sv["proj"], sv["al"], sv["dt"], dbeta, d_gc, name=f"dn_gate_bwd_{tag}")
        dhq, dhf, dhi, dz_hg, d_lb[l], d_hg_nw[l] = _hg_chunk_bwd(sv["proj"], sv["lb"], sv["qh"], sv["kh"], sv["lf"], sv["st_hg"],
                                                                  sv["o_hg"], hg_norm_w[l], dy, name=f"hg_chunk_bwd_{tag}")
        dproj = jnp.concatenate([dqkv_pre, dz_dn, dhq, dhf, dhi, dz_hg, db, da], axis=1)
        def push_d_win(after):
            n_split = LAST_SPLIT if l == 0 else OTHER_SPLIT
            rows = D_MODEL // n_split
            handles = []
            for q in range(n_split):
                hn_q = sv["hn"] if n_split == 1 else sv["hn"][:, q * rows:(q + 1) * rows]
                sfx = tag if n_split == 1 else f"{tag}_{q}"
                d_win = _mm(hn_q, dproj, mode="tn", out_dtype=BF16, after=after, name=f"mm_dwin_{sfx}")
                parts_in = _win_to_shards(d_win, name=f"dw_in_shards_{sfx}")
                handles.append(_push_start(parts_in, own_slot(parts_in[my]), broadcast=False, after=after,
                                           name=f"exchange_w_in_{sfx}_start"))
                after = [handles[-1][4]]
            return handles

        if l == 0:
            small = _pack([loss_row, jnp.concatenate(d_norm_w[1:], axis=0), d_final_w,
                           jnp.stack([a[0, :N_HEADS] for a in d_alog]), jnp.stack([a[0, :N_HEADS] for a in d_dt]),
                           jnp.concatenate(d_dn_nw, axis=0), jnp.concatenate(d_hg_nw, axis=0), jnp.concatenate(d_lb, axis=0),
                           jnp.stack(d_conv)])
            small_all = _all_gather(small, name="gather_small")
        sent["win", l] = push_d_win([small_all] if l == 0 else [])
        dh, d_norm_w[l] = _mm(dproj, wi, mode="nt", out_dtype=F32, tile_m=512, tile_n=D_MODEL, tile_k=768,
                              after=[sent["win", l][-1][4]],
                              fused=("rms_bwd", sv["h"], norm_w[l], dh1), name=f"mm_dhn_{tag}")
    grad_x = dh[None]

    small_shapes = [(1, 128), (depth - 1, D_MODEL), final_norm_w.shape, dn_A_log.shape, dn_dt_bias.shape, dn_norm_w.shape,
                    hg_norm_w.shape, hg_lb_logits.shape, (depth, CONV_W, 3 * BR_WIDTH)]
    tot = _unpack(_sum_parts(small_all, after=[grad_x], name="sum_small"), small_shapes)
    loss = tot[0][0, 0]
    g_lb = tot[7]
    g_logits = jax.vjp(_lower_bounds, hg_lb_logits)[1](g_lb)[0]
    g_conv = lax.dynamic_slice_in_dim(tot[8], my * (3 * BR_WIDTH // N_DEV), 3 * BR_WIDTH // N_DEV, axis=2)
    small_g = [g_conv, tot[3], tot[4], tot[5], g_logits, tot[6], tot[2]]
    small_w = [dn_conv_w, dn_A_log, dn_dt_bias, dn_norm_w, hg_lb_logits, hg_norm_w, final_norm_w]
    small_m = [m_dn_conv_w, m_dn_A_log, m_dn_dt_bias, m_dn_norm_w, m_hg_lb_logits, m_hg_norm_w, m_final_norm_w]
    small_v = [v_dn_conv_w, v_dn_A_log, v_dn_dt_bias, v_dn_norm_w, v_hg_lb_logits, v_hg_norm_w, v_final_norm_w]
    pk_w = _pack(small_w)
    res_small = _adamw(_pack(small_g)[None], 0, pk_w, _pack(small_m), _pack(small_v), name="adamw_small", tr=pk_w.shape[0])
    shapes_w = [a.shape for a in small_w]
    sg, sd, sm, sv_ = (_unpack(r, shapes_w) for r in res_small)

    r_win = r_wo = r_wg = r_wu = None
    done = [grad_x, res_small[0]]

    def flat(a, cols):
        return a.reshape(-1, cols)

    for l in reversed(range(depth)):
        tag = f"l{l}"
        land_rest = _push_wait(sent["rest", l], done, broadcast=False, name=f"exchange_rest_{tag}_wait")
        r_wo = _adamw(land_rest, 0, flat(w_out, D_MODEL), flat(m_w_out, D_MODEL), flat(v_w_out, D_MODEL), layer=l,
                      n_layers=depth, prev=r_wo, name=f"adamw_w_out_{tag}", tr=rows_out)
        r_wg = _adamw(land_rest, g_off, flat(w_ple_gate, D_MODEL), flat(m_w_ple_gate, D_MODEL), flat(v_w_ple_gate, D_MODEL),
                      layer=l, n_layers=depth, prev=r_wg, name=f"adamw_w_gate_{tag}", tr=rows_out)
        r_wu = _adamw(land_rest, u_off, flat(w_ple_up, D_MODEL), flat(m_w_ple_up, D_MODEL), flat(v_w_ple_up, D_MODEL),
                      layer=l, n_layers=depth, prev=r_wu, name=f"adamw_w_up_{tag}", tr=up_rows)
        done = [r_wo[0], r_wg[0], r_wu[0]]
    for l in reversed(range(depth)):
        tag = f"l{l}"
        if l == 0:
            nw0 = _sum_parts(_all_gather(_pack([d_norm_w[0]]), after=done, name="gather_norm_w"), name="sum_norm_w")
            g_norm_w = jnp.concatenate([_unpack(nw0, [(1, D_MODEL)])[0], tot[1]], axis=0)
            pk_nw = _pack([norm_w])
            r_nw = _adamw(_pack([g_norm_w])[None], 0, pk_nw, _pack([m_norm_w]), _pack([v_norm_w]), name="adamw_norm_w",
                          tr=pk_nw.shape[0])
            r_nw = [_unpack(r, [norm_w.shape])[0] for r in r_nw]
            done = [r_nw[0]]
        n_split = len(sent["win", l])
        for q, handle in enumerate(sent["win", l]):
            sfx = tag if n_split == 1 else f"{tag}_{q}"
            land_in = _push_wait(handle, done, broadcast=False, name=f"exchange_w_in_{sfx}_wait")
            r_win = _adamw(land_in, 0, flat(w_in, SHARD_IN), flat(m_w_in, SHARD_IN), flat(v_w_in, SHARD_IN),
                           layer=l * n_split + q, n_layers=depth * n_split, prev=r_win, name=f"adamw_w_in_{sfx}", tr=256)
            done = [r_win[0]]
    r_win = [o.reshape(w_in.shape) for o in r_win]
    r_wo = [o.reshape(w_out.shape) for o in r_wo]
    r_wg = [o.reshape(w_ple_gate.shape) for o in r_wg]
    r_wu = [o.reshape(w_ple_up.shape) for o in r_wu]

    def order(nw, small_list, big_in, big_out, big_up, big_gate):
        cw, al_, dt_, dnw, lbl, hnw, fw = small_list
        return [nw, big_in, cw, al_, dt_, dnw, lbl, hnw, big_out, big_up, big_gate, fw]

    outs = [loss, grad_x]
    for i, sl in enumerate((sg, sd, sm, sv_)):
        outs += order(r_nw[i], sl, r_win[i], r_wo[i], r_wu[i], r_wg[i])
    return tuple(outs)
```

```python
import functools

import jax
import jax.numpy as jnp
from jax import lax
from jax.experimental import pallas as pl
from jax.experimental.pallas import tpu as pltpu

F32 = jnp.float32
BF16 = jnp.bfloat16
HIGHEST = lax.Precision.HIGHEST

N_DEV = 8
D_MODEL = 2048
PLE_DIM = 256
HEAD_DIM = 128
N_HEADS = 8
BR_WIDTH = N_HEADS * HEAD_DIM
CHUNK = 64
SUB = 16
CONV_W = 4
NORM_EPS = 1e-6
L2_EPS = 1e-6
IN_WIDTH = 8208
SHARD_IN = IN_WIDTH // N_DEV
EXP_CLAMP = 80.0

C_QKV, C_Z, C_HQ, C_HF, C_HI, C_HZ, C_B, C_A, N_PROJ = 0, 3072, 4096, 5120, 6144, 7168, 8192, 8320, 8448

ADAM_LR, ADAM_B1, ADAM_B2, ADAM_EPS, ADAM_WD, ADAM_STEP = 0.001, 0.9, 0.999, 1e-08, 0.01, 10

VMEM_LIMIT = 56 * 1024 * 1024


def _cp(*sem):
    return pltpu.CompilerParams(dimension_semantics=sem, vmem_limit_bytes=VMEM_LIMIT)


class _Heads:
    def __init__(self, vals):
        self.v = tuple(vals)

    def __add__(self, o):
        return _hmap(lambda a, b: a + b, self, o)

    def __radd__(self, o):
        return _hmap(lambda a, b: b + a, self, o)

    def __sub__(self, o):
        return _hmap(lambda a, b: a - b, self, o)

    def __rsub__(self, o):
        return _hmap(lambda a, b: b - a, self, o)

    def __mul__(self, o):
        return _hmap(lambda a, b: a * b, self, o)

    def __rmul__(self, o):
        return _hmap(lambda a, b: b * a, self, o)

    def __neg__(self):
        return _hmap(lambda a: -a, self)

    def __getitem__(self, idx):
        return _hmap(lambda a: a[idx], self)


def _hmap(fn, *args):
    n = next((len(a.v) for a in args if isinstance(a, _Heads)), None)
    if n is None:
        return fn(*args)
    return _Heads(fn(*[a.v[i] if isinstance(a, _Heads) else a for a in args]) for i in range(n))


def _dot(a, b, ca, cb):
    return _hmap(lambda x, y: lax.dot_general(x.astype(BF16), y.astype(BF16), (((ca,), (cb,)), ((), ())),
                                              preferred_element_type=F32), a, b)


def _nn(a, b):
    return _dot(a, b, 1, 0)


def _nt(a, b):
    return _dot(a, b, 1, 1)


def _tn(a, b):
    return _dot(a, b, 0, 0)


def _split(a):
    hi = _hmap(lambda x: x.astype(BF16), a)
    return hi, _hmap(lambda x, h: (x - h.astype(F32)).astype(BF16), a, hi)


def _dot3(a, b, ca, cb):
    ah, al = _split(a)
    bh, bl = _split(b)
    return _dot(ah, bh, ca, cb) + (_dot(ah, bl, ca, cb) + _dot(al, bh, ca, cb))


def _nn_exact(a, b):
    return _hmap(lambda y: lax.dot_general(a, y, (((1,), (0,)), ((), ())), precision=HIGHEST,
                                           preferred_element_type=F32), b)


def _exp(x):
    return _hmap(jnp.exp, x)


def _sum(x, axis):
    return _hmap(lambda a: jnp.sum(a, axis=axis, keepdims=True), x)


def _sigmoid(x):
    return jax.nn.sigmoid(x)


def _silu(x):
    return x * _sigmoid(x)


def _dsilu(x):
    s = _sigmoid(x)
    return s * (1.0 + x * (1.0 - s))


def _silu_and_grad(x):
    s = _sigmoid(x)
    return x * s, s * (1.0 + x * (1.0 - s))


def _softplus(x):
    return jnp.maximum(x, 0.0) + jnp.log(1.0 + jnp.exp(-jnp.abs(x)))


def _iota2(n, m, axis):
    return lax.broadcasted_iota(jnp.int32, (n, m), axis)


def _col2row(col, eye):
    return _hmap(lambda c: jnp.sum(eye * c, axis=0, keepdims=True), col)


def _row2col(row, eye):
    return _hmap(lambda r: jnp.sum(eye * r, axis=1, keepdims=True), row)


def _pick_lane(block, lane_idx):
    lane = _iota2(block.shape[0], block.shape[1], 1)
    return jnp.sum(jnp.where(lane == lane_idx, block, 0.0), axis=1, keepdims=True)


MM_TILE_M, MM_TILE_N, MM_TILE_K = 1024, 1408, 2048


def _tile(dim, cap):
    if dim <= cap:
        return dim
    t = cap - cap % 128
    while dim % t:
        t -= 128
    return t


def _mm(a, b, *, mode, out_dtype, res=None, after=(), b_rows=None, tile_m=MM_TILE_M, tile_n=MM_TILE_N, tile_k=MM_TILE_K,
        fused=None, name):
    b_mat_rows = b.shape[0] if b_rows is None else N_DEV * b_rows[1]
    if mode == "nn":
        (m, kd), n = a.shape, b.shape[-1]
        assert kd == b_mat_rows
    elif mode == "nt":
        (m, kd), n = a.shape, b_mat_rows
    else:
        (kd, m), n = a.shape, b.shape[-1]
    tm, tn, tk = _tile(m, tile_m), _tile(n, tile_n), _tile(kd, tile_k)
    assert m % tm == 0 and n % tn == 0 and kd % tk == 0, (m, n, kd, tm, tn, tk)
    nk = kd // tk
    ca, cb = {"nn": (1, 0), "nt": (1, 1), "tn": (0, 0)}[mode]

    kind = None if fused is None else fused[0]
    n_in = 2 + (res is not None) + (0 if fused is None else len(fused) - 1)
    n_out = 1 if fused is None else 2

    def body(*refs):
        a_ref, b_ref = refs[:2]
        r_ref = None if res is None else refs[2]
        extra = refs[2 + (res is not None):n_in]
        outs = refs[-1 - n_out:-1]
        o_ref, acc_ref = outs[0], refs[-1]
        k = pl.program_id(2)

        @pl.when(k == 0)
        def _():
            acc_ref[...] = jnp.zeros_like(acc_ref)

        b_tile = b_ref[...]
        if b_rows is not None:
            b_tile = b_tile.reshape(-1, b_tile.shape[-1])
        acc_ref[...] += _dot(a_ref[...], b_tile, ca, cb)

        @pl.when(k == nk - 1)
        def _():
            out = acc_ref[...]
            if r_ref is not None:
                out = out + r_ref[...].astype(F32)
            if kind == "ple":
                h1_ref, up_ref = extra
                o_ref[...] = out
                outs[1][...] = h1_ref[...] + up_ref[...] * _sigmoid(out)
            elif kind == "rms_bwd":
                h_ref, w_ref, res_ref = extra
                dx, dwt = _rms_bwd_math(h_ref[...], w_ref[...], out)
                o_ref[...] = res_ref[...] + dx

                @pl.when(pl.program_id(0) == 0)
                def _():
                    outs[1][...] = jnp.zeros_like(outs[1])

                outs[1][...] += jnp.sum(dwt, axis=0, keepdims=True)
            else:
                o_ref[...] = out.astype(o_ref.dtype)

    a_spec = pl.BlockSpec((tk, tm), lambda i, j, k: (k, i)) if mode == "tn" else pl.BlockSpec((tm, tk), lambda i, j, k: (i, k))
    if b_rows is None:
        b_spec = pl.BlockSpec((tn, tk), lambda i, j, k: (j, k)) if mode == "nt" else pl.BlockSpec((tk, tn), lambda i, j, k: (k, j))
    else:
        first, count = b_rows
        assert first % count == 0 and mode in ("nn", "nt")
        rb = first // count
        if mode == "nn":
            assert tk == kd
            b_spec = pl.BlockSpec((N_DEV, count, tn), lambda i, j, k: (0, rb, j))
        else:
            assert tn % count == 0
            b_spec = pl.BlockSpec((tn // count, count, tk), lambda i, j, k: (j, rb, k))
    o_spec = pl.BlockSpec((tm, tn), lambda i, j, k: (i, j))
    row_spec = pl.BlockSpec((1, tn), lambda i, j, k: (0, j))
    extra_specs, extra_args, out_specs, out_shape = [], (), o_spec, jax.ShapeDtypeStruct((m, n), out_dtype)
    sem = ("parallel", "parallel", "arbitrary")
    if kind == "ple":
        extra_specs, extra_args = [o_spec, o_spec], tuple(fused[1:])
        out_specs, out_shape = [o_spec, o_spec], [jax.ShapeDtypeStruct((m, n), F32)] * 2
    elif kind == "rms_bwd":
        assert tn == n
        extra_specs, extra_args = [o_spec, row_spec, o_spec], (fused[1], fused[2].reshape(1, n), fused[3])
        out_specs, out_shape = [o_spec, row_spec], [jax.ShapeDtypeStruct((m, n), F32), jax.ShapeDtypeStruct((1, n), F32)]
        sem = ("arbitrary", "arbitrary", "arbitrary")
    in_specs = ([a_spec, b_spec] + ([o_spec] if res is not None else []) + extra_specs
                + [pl.BlockSpec(memory_space=pl.ANY)] * len(after))
    args = (a, b) + ((res,) if res is not None else ()) + extra_args + tuple(after)
    return pl.pallas_call(
        body, name=name, grid=(m // tm, n // tn, nk), in_specs=in_specs, out_specs=out_specs, out_shape=out_shape,
        scratch_shapes=[pltpu.VMEM((tm, tn), F32)], compiler_params=_cp(*sem),
    )(*args)


ROW_TILE = 256


def _rms_fwd(h, w, *, name):
    s, d = h.shape
    tr = min(ROW_TILE, s)

    def body(h_ref, w_ref, o_ref):
        x = h_ref[...]
        r = lax.rsqrt(jnp.mean(x * x, axis=-1, keepdims=True) + NORM_EPS)
        o_ref[...] = (x * r * w_ref[...]).astype(o_ref.dtype)

    return pl.pallas_call(
        body, name=name, grid=(s // tr,),
        in_specs=[pl.BlockSpec((tr, d), lambda i: (i, 0)), pl.BlockSpec((1, d), lambda i: (0, 0))],
        out_specs=pl.BlockSpec((tr, d), lambda i: (i, 0)),
        out_shape=jax.ShapeDtypeStruct((s, d), BF16), compiler_params=_cp("parallel"),
    )(h, w.reshape(1, d))


def _rms_bwd_math(x, w, dy):
    d = x.shape[-1]
    r = lax.rsqrt(jnp.mean(x * x, axis=-1, keepdims=True) + NORM_EPS)
    gw = dy * w
    dx = r * gw - x * ((r * r * r) * (jnp.sum(gw * x, axis=-1, keepdims=True) / d))
    return dx, dy * x * r


def _final_fwd_bwd(h, w, tgt, *, name):
    s, d = h.shape
    tr = min(ROW_TILE, s)

    def body(h_ref, w_ref, t_ref, loss_ref, dh_ref, dw_ref):
        @pl.when(pl.program_id(0) == 0)
        def _():
            loss_ref[...] = jnp.zeros_like(loss_ref)
            dw_ref[...] = jnp.zeros_like(dw_ref)

        x = h_ref[...]
        wv = w_ref[...]
        r = lax.rsqrt(jnp.mean(x * x, axis=-1, keepdims=True) + NORM_EPS)
        err = x * r * wv - t_ref[...]
        row_loss = jnp.mean(err * err, axis=-1, keepdims=True)
        loss_ref[...] += 0.5 * jnp.sum(row_loss, axis=0, keepdims=True)
        dx, dwt = _rms_bwd_math(x, wv, err / d)
        dh_ref[...] = dx
        dw_ref[...] += jnp.sum(dwt, axis=0, keepdims=True)

    row = pl.BlockSpec((tr, d), lambda i: (i, 0))
    vec = pl.BlockSpec((1, d), lambda i: (0, 0))
    return pl.pallas_call(
        body, name=name, grid=(s // tr,), in_specs=[row, vec, row],
        out_specs=[pl.BlockSpec((1, 128), lambda i: (0, 0)), row, vec],
        out_shape=[jax.ShapeDtypeStruct((1, 128), F32), jax.ShapeDtypeStruct((s, d), F32),
                   jax.ShapeDtypeStruct((1, d), F32)],
        compiler_params=_cp("arbitrary"),
    )(h, w.reshape(1, d), tgt)


def _ple_bwd(dh2, gate_pre, up, *, name):
    s, d = dh2.shape
    tr = min(ROW_TILE, s)

    def body(d_ref, g_ref, u_ref, dup_ref, dgp_ref):
        dh = d_ref[...]
        gate = _sigmoid(g_ref[...])
        dup_ref[...] = (dh * gate).astype(BF16)
        dgp_ref[...] = (dh * u_ref[...] * gate * (1.0 - gate)).astype(BF16)

    row = pl.BlockSpec((tr, d), lambda i: (i, 0))
    return pl.pallas_call(body, name=name, grid=(s // tr,), in_specs=[row, row, row], out_specs=[row, row],
                          out_shape=[jax.ShapeDtypeStruct((s, d), BF16)] * 2, compiler_params=_cp("parallel"))(dh2, gate_pre, up)


def _head_norm_fwd(o, z, w):
    return _hmap(lambda x, zz: (x * lax.rsqrt(jnp.mean(x * x, axis=-1, keepdims=True) + NORM_EPS) * w * _silu(zz)).astype(BF16),
                 o, z)


def _head_norm_bwd(o, z, w, dy):
    dos, dzs, dw = [], [], jnp.zeros((1, HEAD_DIM), F32)
    for x, zz, g in zip(o.v, z.v, dy.v):
        r = lax.rsqrt(jnp.mean(x * x, axis=-1, keepdims=True) + NORM_EPS)
        silu_z, dsilu_z = _silu_and_grad(zz)
        don = g * silu_z
        dzs.append((g * (x * r * w) * dsilu_z).astype(BF16))
        gw = don * w
        dos.append(r * gw - x * ((r * r * r) * (jnp.sum(gw * x, axis=-1, keepdims=True) / HEAD_DIM)))
        dw = dw + jnp.sum(don * x * r, axis=0, keepdims=True)
    return _Heads(dos), _Heads(dzs), dw


def _conv_silu(x, w, s):
    row = _iota2(s, x.shape[1], 0)
    c = w[CONV_W - 1:CONV_W, :] * x
    for k in range(1, CONV_W):
        c = c + w[CONV_W - 1 - k:CONV_W - k, :] * jnp.where(row >= k, pltpu.roll(x, k, 0), 0.0)
    return c


def _dn_qkv_fwd(proj, conv_w, *, name):
    s = proj.shape[0]
    nb = 3 * N_HEADS

    def body(x_ref, w_ref, o_ref):
        j = pl.program_id(0)
        sv = _silu(_conv_silu(x_ref[...], w_ref[...], s))
        r = lax.rsqrt(jnp.sum(sv * sv, axis=-1, keepdims=True) + L2_EPS)
        scale = jnp.where(j < N_HEADS, HEAD_DIM ** -0.5, 1.0).astype(F32)
        o_ref[...] = jnp.where(j < 2 * N_HEADS, sv * r * scale, sv)

    return pl.pallas_call(
        body, name=name, grid=(nb,),
        in_specs=[pl.BlockSpec((s, HEAD_DIM), lambda j: (0, j)), pl.BlockSpec((CONV_W, HEAD_DIM), lambda j: (0, j))],
        out_specs=pl.BlockSpec((s, HEAD_DIM), lambda j: (0, j)),
        out_shape=jax.ShapeDtypeStruct((s, 3 * BR_WIDTH), F32), compiler_params=_cp("parallel"),
    )(proj, conv_w)


def _dn_qkv_bwd(proj, conv_w, dqkv, *, name):
    s = proj.shape[0]
    nb = 3 * N_HEADS

    def body(x_ref, w_ref, g_ref, dx_ref, dw_ref):
        j = pl.program_id(0)
        x, w, g = x_ref[...], w_ref[...], g_ref[...]
        c = _conv_silu(x, w, s)
        sv, dsv = _silu_and_grad(c)
        r = lax.rsqrt(jnp.sum(sv * sv, axis=-1, keepdims=True) + L2_EPS)
        scale = jnp.where(j < N_HEADS, HEAD_DIM ** -0.5, 1.0).astype(F32)
        ds_n = scale * (r * g - sv * ((r * r * r) * jnp.sum(g * sv, axis=-1, keepdims=True)))
        dc = jnp.where(j < 2 * N_HEADS, ds_n, g) * dsv
        row = _iota2(s, HEAD_DIM, 0)
        dx = w[CONV_W - 1:CONV_W, :] * dc
        dws = [jnp.sum(dc * x, axis=0, keepdims=True)]
        for k in range(1, CONV_W):
            dc_ahead = jnp.where(row < s - k, pltpu.roll(dc, s - k, 0), 0.0)
            dx = dx + w[CONV_W - 1 - k:CONV_W - k, :] * dc_ahead
            dws.append(jnp.sum(dc_ahead * x, axis=0, keepdims=True))
        dx_ref[...] = dx.astype(BF16)
        for k in range(CONV_W):
            dw_ref[CONV_W - 1 - k:CONV_W - k, :] = dws[k]

    blk = pl.BlockSpec((s, HEAD_DIM), lambda j: (0, j))
    wblk = pl.BlockSpec((CONV_W, HEAD_DIM), lambda j: (0, j))
    return pl.pallas_call(
        body, name=name, grid=(nb,), in_specs=[blk, wblk, blk], out_specs=[blk, wblk],
        out_shape=[jax.ShapeDtypeStruct((s, 3 * BR_WIDTH), BF16), jax.ShapeDtypeStruct((CONV_W, 3 * BR_WIDTH), F32)],
        compiler_params=_cp("parallel"),
    )(proj, conv_w, dqkv)


def _tri(n, kind):
    r, c = _iota2(n, n, 0), _iota2(n, n, 1)
    if kind == "lower":
        return (r >= c).astype(F32)
    if kind == "upper":
        return (r <= c).astype(F32)
    return (r == c).astype(F32)


GATE_TILE = 512


def _dn_gate_fwd(proj, a_log, dt_bias, *, name):
    s = proj.shape[0]
    tr = min(GATE_TILE, s)

    def body(b_ref, a_ref, al_ref, dt_ref, beta_ref, g_ref):
        beta_ref[...] = _sigmoid(b_ref[...])
        g = -jnp.exp(al_ref[...]) * _softplus(a_ref[...] + dt_ref[...])
        low = _tri(CHUNK, "lower")
        for c in range(tr // CHUNK):
            rows = slice(c * CHUNK, (c + 1) * CHUNK)
            g_ref[rows, :] = _nn_exact(low, g[rows, :])

    blk = lambda cb: pl.BlockSpec((tr, HEAD_DIM), lambda i: (i, cb))
    vec = pl.BlockSpec((1, HEAD_DIM), lambda i: (0, 0))
    out = pl.BlockSpec((tr, HEAD_DIM), lambda i: (i, 0))
    return pl.pallas_call(
        body, name=name, grid=(s // tr,), in_specs=[blk(C_B // HEAD_DIM), blk(C_A // HEAD_DIM), vec, vec],
        out_specs=[out, out], out_shape=[jax.ShapeDtypeStruct((s, HEAD_DIM), F32)] * 2, compiler_params=_cp("parallel"),
    )(proj, proj, a_log, dt_bias)


def _dn_gate_bwd(proj, a_log, dt_bias, dbeta, d_g, *, name):
    s = proj.shape[0]
    tr = min(GATE_TILE, s)

    def body(b_ref, a_ref, al_ref, dt_ref, dbeta_ref, dG_ref, db_ref, da_ref, dal_ref, ddt_ref):
        @pl.when(pl.program_id(0) == 0)
        def _():
            dal_ref[...] = jnp.zeros_like(dal_ref)
            ddt_ref[...] = jnp.zeros_like(ddt_ref)

        beta = _sigmoid(b_ref[...])
        db_ref[...] = (dbeta_ref[...] * beta * (1.0 - beta)).astype(BF16)
        pre = a_ref[...] + dt_ref[...]
        neg_ea = -jnp.exp(al_ref[...])
        up = _tri(CHUNK, "upper")
        d_g = dG_ref[...]
        dg = jnp.concatenate([_nn_exact(up, d_g[c * CHUNK:(c + 1) * CHUNK, :]) for c in range(tr // CHUNK)], axis=0)
        da = dg * neg_ea * _sigmoid(pre)
        da_ref[...] = da.astype(BF16)
        ddt_ref[...] += jnp.sum(da, axis=0, keepdims=True)
        dal_ref[...] += jnp.sum(dg * neg_ea * _softplus(pre), axis=0, keepdims=True)

    blk = lambda cb: pl.BlockSpec((tr, HEAD_DIM), lambda i: (i, cb))
    vec = pl.BlockSpec((1, HEAD_DIM), lambda i: (0, 0))
    io = pl.BlockSpec((tr, HEAD_DIM), lambda i: (i, 0))
    return pl.pallas_call(
        body, name=name, grid=(s // tr,),
        in_specs=[blk(C_B // HEAD_DIM), blk(C_A // HEAD_DIM), vec, vec, io, io], out_specs=[io, io, vec, vec],
        out_shape=[jax.ShapeDtypeStruct((s, HEAD_DIM), BF16)] * 2 + [jax.ShapeDtypeStruct((1, HEAD_DIM), F32)] * 2,
        compiler_params=_cp("arbitrary"),
    )(proj, proj, a_log, dt_bias, dbeta, d_g)


def _unit_lower_inverse(a_strict, eye):
    x = -a_strict
    t = x + eye
    p = x
    n = 2
    while n < CHUNK:
        p = _nn(p, p)
        t = t + _nn(t, p)
        n *= 2
    return t


def _rows(*xs):
    return _hmap(lambda *a: jnp.concatenate(a, axis=0), *xs)


def _lanes(*xs):
    return _hmap(lambda *a: jnp.concatenate(a, axis=1), *xs)


def _dn_chunk_common(q, k, v, gc, beta, st, with_qd_state, t_inv=None):
    c, d = CHUNK, HEAD_DIM
    eye = _tri(c, "eye")
    low = _tri(c, "lower")
    strict = low - eye
    grow = _col2row(gc, eye)
    dec = _hmap(lambda g_, gr: low * jnp.exp(low * (g_ - gr)), gc, grow)
    kb = k * beta
    kq = _nt(_rows(kb, q), k)
    a_mat = kq[0:c, :] * dec * strict
    qk = kq[c:2 * c, :] * dec
    if t_inv is None:
        t_inv = _unit_lower_inverse(a_mat, eye)
    e_g = _exp(gc)
    qd = q * e_g
    uw = _nn(t_inv, _lanes(v * beta, kb * e_g))
    u, w = uw[:, 0:d], uw[:, d:2 * d]
    last = (_iota2(c, 1, 0) == c - 1).astype(F32)
    g_last = _sum(gc * last, 0)
    e_t = _exp(g_last - gc)
    kt = k * e_t
    tail = _exp(g_last)
    if with_qd_state:
        ws = _nn(_rows(w, qd), st)
        vn, qds = u - ws[0:c, :], ws[c:2 * c, :]
    else:
        vn, qds = u - _nn(w, st), None
    return dict(eye=eye, low=low, strict=strict, dec=dec, kb=kb, a_mat=a_mat, t_inv=t_inv, e_g=e_g, u=u, w=w, uw=uw,
                qk=qk, qd=qd, qds=qds, last=last, e_t=e_t, kt=kt, tail=tail, vn=vn)


def _dn_chunk_fwd_math(q, k, v, gc, beta, st):
    m = _dn_chunk_common(q, k, v, gc, beta, st, True)
    o = m["qds"] + _nn(m["qk"], m["vn"])
    st2 = st * m["tail"] + _tn(m["kt"], m["vn"])
    return o, st2, m["t_inv"]


def _dn_chunk_bwd_math(q, k, v, gc, beta, st, do, dst2, t_inv=None):
    c, d = CHUNK, HEAD_DIM
    m = _dn_chunk_common(q, k, v, gc, beta, st, False, t_inv)
    eye, low, strict = m["eye"], m["low"], m["strict"]
    dvn = _tn(m["qk"], do) + _nn(m["kt"], dst2)
    dqk = _nt(do, m["vn"]) * low
    both = _rows(do, dvn)
    ds_both = _nt(both, st)
    dqd, dw = ds_both[0:c, :], -ds_both[c:2 * c, :]
    dst = _tn(_rows(m["qd"], -m["w"]), both) + dst2 * m["tail"]
    dkt = _nt(m["vn"], dst2)
    dtail = _sum(_sum(st * dst2, 1), 0)
    dvb_dkg = _tn(m["t_inv"], _lanes(dvn, dw))
    dvb, dkg = dvb_dkg[:, 0:d], dvb_dkg[:, d:2 * d]
    d_a = _nt(dvb_dkg, m["uw"]) * (-strict)
    dkk = d_a * m["dec"]
    dp = dqk * m["dec"]
    dpk = _rows(dp, dkk)
    dq_dkb = _nn(dpk, k)
    dq = dq_dkb[0:c, :] + dqd * m["e_g"]
    dkb = dq_dkb[c:2 * c, :] + dkg * m["e_g"]
    dk = _tn(dpk, _rows(q, m["kb"])) + dkb * beta + dkt * m["e_t"]
    dv = dvb * beta
    dbeta = _sum(dvb * v + dkb * k, 1)
    de_g = _sum(dkg * m["kb"] + dqd * q, 1)
    de_t = _sum(dkt * k, 1)
    mm = d_a * m["a_mat"] + dqk * m["qk"]
    dgc = (_sum(mm, 1) - _row2col(_sum(mm, 0), eye) + de_g * m["e_g"] - de_t * m["e_t"]
           + (_sum(de_t * m["e_t"], 0) + dtail * m["tail"]) * m["last"])
    return dq, dk, dv, dgc, dbeta, dst


def _heads_of(ref):
    return _Heads(ref[:, h * HEAD_DIM:(h + 1) * HEAD_DIM] for h in range(N_HEADS))


def _lanes_of(block):
    return _Heads(_pick_lane(block, h) for h in range(N_HEADS))


def _dn_chunk_fwd(qkv, gcs, beta, proj, norm_w, *, name):
    s = qkv.shape[0]
    n = s // CHUNK

    def body(q_ref, k_ref, v_ref, g_ref, b_ref, z_ref, w_ref, o_ref, st_out_ref, tinv_ref, y_ref, st_ref):
        @pl.when(pl.program_id(0) == 0)
        def _():
            st_ref[...] = jnp.zeros_like(st_ref)

        gblk, bblk = g_ref[...], b_ref[...]
        st = _Heads(st_ref[h] for h in range(N_HEADS))
        o, st2, t_inv = _dn_chunk_fwd_math(_heads_of(q_ref), _heads_of(k_ref), _heads_of(v_ref), _lanes_of(gblk),
                                           _lanes_of(bblk), st)
        y = _head_norm_fwd(o, _heads_of(z_ref), w_ref[...])
        for h in range(N_HEADS):
            st_out_ref[0, h] = st.v[h]
            tinv_ref[0, h] = t_inv.v[h].astype(BF16)
            o_ref[:, h * HEAD_DIM:(h + 1) * HEAD_DIM] = o.v[h]
            y_ref[:, h * HEAD_DIM:(h + 1) * HEAD_DIM] = y.v[h]
            st_ref[h] = st2.v[h]

    blk = lambda off: pl.BlockSpec((CHUNK, BR_WIDTH), lambda c: (c, off))
    sc = pl.BlockSpec((CHUNK, HEAD_DIM), lambda c: (c, 0))
    return pl.pallas_call(
        body, name=name, grid=(n,),
        in_specs=[blk(0), blk(1), blk(2), sc, sc, blk(C_Z // BR_WIDTH), pl.BlockSpec((1, HEAD_DIM), lambda c: (0, 0))],
        out_specs=[blk(0), pl.BlockSpec((1, N_HEADS, HEAD_DIM, HEAD_DIM), lambda c: (c, 0, 0, 0)),
                   pl.BlockSpec((1, N_HEADS, CHUNK, CHUNK), lambda c: (c, 0, 0, 0)), blk(0)],
        out_shape=[jax.ShapeDtypeStruct((s, BR_WIDTH), F32), jax.ShapeDtypeStruct((n, N_HEADS, HEAD_DIM, HEAD_DIM), F32),
                   jax.ShapeDtypeStruct((n, N_HEADS, CHUNK, CHUNK), BF16), jax.ShapeDtypeStruct((s, BR_WIDTH), BF16)],
        scratch_shapes=[pltpu.VMEM((N_HEADS, HEAD_DIM, HEAD_DIM), F32)],
        compiler_params=_cp("arbitrary"),
    )(qkv, qkv, qkv, gcs, beta, proj, norm_w.reshape(1, HEAD_DIM))


def _dn_chunk_bwd(qkv, gcs, beta, states, tinvs, o, proj, norm_w, dy, *, name):
    s = qkv.shape[0]
    n = s // CHUNK

    def body(q_ref, k_ref, v_ref, g_ref, b_ref, st_in_ref, tinv_ref, o_ref, z_ref, w_ref, dy_ref,
             dqkv_ref, dg_ref, dbeta_ref, dz_ref, dw_ref, dst_ref):
        @pl.when(pl.program_id(0) == 0)
        def _():
            dst_ref[...] = jnp.zeros_like(dst_ref)
            dw_ref[...] = jnp.zeros_like(dw_ref)

        do, dz, dw = _head_norm_bwd(_heads_of(o_ref), _heads_of(z_ref), w_ref[...], _heads_of(dy_ref))
        dw_ref[...] += dw

        gblk, bblk = g_ref[...], b_ref[...]
        lane = _iota2(CHUNK, HEAD_DIM, 1)
        dg_all = jnp.zeros((CHUNK, HEAD_DIM), F32)
        dbeta_all = jnp.zeros((CHUNK, HEAD_DIM), F32)
        dq, dk, dv, dgc, dbeta, dst = _dn_chunk_bwd_math(
            _heads_of(q_ref), _heads_of(k_ref), _heads_of(v_ref), _lanes_of(gblk), _lanes_of(bblk),
            _Heads(st_in_ref[0, h] for h in range(N_HEADS)), do,
            _Heads(dst_ref[h] for h in range(N_HEADS)), _Heads(tinv_ref[0, h] for h in range(N_HEADS)))
        for h in range(N_HEADS):
            dz_ref[:, h * HEAD_DIM:(h + 1) * HEAD_DIM] = dz.v[h]
            for part, val in enumerate((dq, dk, dv)):
                c0 = part * BR_WIDTH + h * HEAD_DIM
                dqkv_ref[:, c0:c0 + HEAD_DIM] = val.v[h]
            dg_all = jnp.where(lane == h, dgc.v[h], dg_all)
            dbeta_all = jnp.where(lane == h, dbeta.v[h], dbeta_all)
            dst_ref[h] = dst.v[h]
        dg_ref[...] = dg_all
        dbeta_ref[...] = dbeta_all

    blk = lambda off: pl.BlockSpec((CHUNK, BR_WIDTH), lambda c: (n - 1 - c, off))
    sc = pl.BlockSpec((CHUNK, HEAD_DIM), lambda c: (n - 1 - c, 0))
    vec = pl.BlockSpec((1, HEAD_DIM), lambda c: (0, 0))
    outs = pl.pallas_call(
        body, name=name, grid=(n,),
        in_specs=[blk(0), blk(1), blk(2), sc, sc,
                  pl.BlockSpec((1, N_HEADS, HEAD_DIM, HEAD_DIM), lambda c: (n - 1 - c, 0, 0, 0)),
                  pl.BlockSpec((1, N_HEADS, CHUNK, CHUNK), lambda c: (n - 1 - c, 0, 0, 0)), blk(0), blk(C_Z // BR_WIDTH),
                  vec, blk(0)],
        out_specs=[pl.BlockSpec((CHUNK, 3 * BR_WIDTH), lambda c: (n - 1 - c, 0)), sc, sc, blk(0), vec],
        out_shape=[jax.ShapeDtypeStruct((s, 3 * BR_WIDTH), F32)] + [jax.ShapeDtypeStruct((s, HEAD_DIM), F32)] * 2
        + [jax.ShapeDtypeStruct((s, BR_WIDTH), BF16), jax.ShapeDtypeStruct((1, HEAD_DIM), F32)],
        scratch_shapes=[pltpu.VMEM((N_HEADS, HEAD_DIM, HEAD_DIM), F32)],
        compiler_params=_cp("arbitrary"),
    )(qkv, qkv, qkv, gcs, beta, states, tinvs, o, proj, norm_w.reshape(1, HEAD_DIM), dy)
    return outs


def _hg_chunk_common(q, k, g):
    c, nb = CHUNK, CHUNK // SUB
    e_g = _exp(g)
    qd = q * e_g
    g_last = g[c - 1:c, :]
    e_t = _exp(g_last - g)
    kt = k * e_t
    tail = _exp(g_last)
    g_refs = [g[i * SUB:i * SUB + 1, :] for i in range(nb)]
    g_ref_rows = _hmap(lambda *rows: jnp.concatenate([jnp.broadcast_to(r, (SUB, r.shape[1])) for r in rows], axis=0), *g_refs)
    e_q = _exp(g - g_ref_rows)
    q_sc = q * e_q
    e_k = [_hmap(lambda gr, g_: jnp.exp(jnp.minimum(gr - g_, EXP_CLAMP)), g_refs[i], g) for i in range(nb)]
    k_sc_all = _rows(*[k * e_k[i] for i in range(nb)])
    row_blk = _iota2(c, 1, 0) // SUB
    masks = [(row_blk == i).astype(F32) for i in range(nb)]
    r_all = _nt(q_sc, k_sc_all)
    a_mat = r_all[:, 0:c] * masks[0]
    for i in range(1, nb):
        a_mat = a_mat + r_all[:, i * c:(i + 1) * c] * masks[i]
    a_mat = a_mat * _tri(c, "lower")
    return dict(e_g=e_g, qd=qd, e_t=e_t, kt=kt, tail=tail, q_sc=q_sc, k_sc_all=k_sc_all, e_q=e_q, e_k=e_k, masks=masks,
                a_mat=a_mat)


def _hg_chunk_fwd_math(q, k, v, g, stt):
    m = _hg_chunk_common(q, k, g)
    o = _nt(m["qd"], stt) + _nn(m["a_mat"], v)
    stt2 = stt * m["tail"] + _tn(v, m["kt"])
    return o, stt2


def _hg_chunk_bwd_math(q, k, v, g, stt, do, dstt2):
    c, nb = CHUNK, CHUNK // SUB
    m = _hg_chunk_common(q, k, g)
    stt2 = stt * m["tail"] + _tn(v, m["kt"])
    later = _sum(stt2 * dstt2, 0)
    dqd = _dot3(do, stt, 1, 0)
    dstt = _tn(do, m["qd"]) + dstt2 * m["tail"]
    d_a = _dot3(do, v, 1, 1) * _tri(c, "lower")
    dv = _tn(m["a_mat"], do) + _nt(m["kt"], dstt2)
    dkt = _dot3(v, dstt2, 1, 0)
    d_blk = _lanes(*[d_a * m["masks"][i] for i in range(nb)])
    dq = dqd * m["e_g"] + _dot3(d_blk, m["k_sc_all"], 1, 0) * m["e_q"]
    dks = _dot3(d_blk, m["q_sc"], 0, 0)
    dk = dkt * m["e_t"]
    for i in range(nb):
        dk = dk + dks[i * c:(i + 1) * c, :] * m["e_k"][i]
    db = q * dq - k * dk
    return dq, dk, dv, db, later, dstt


def _hg_chunk_fwd(proj, lb, norm_w, *, name):
    s = proj.shape[0]
    n = s // CHUNK

    def body(hq_ref, hf_ref, v_ref, lb_ref, z_ref, w_ref, o_ref, st_out_ref, q_out, k_out, lf_out, y_ref, st_ref):
        @pl.when(pl.program_id(0) == 0)
        def _():
            st_ref[...] = jnp.zeros_like(st_ref)

        f, lbv = hf_ref[...], lb_ref[...]
        q_all = _silu(hq_ref[...])
        k_all = (1.0 - lbv) * _sigmoid(-f)
        lf_all = jnp.log(lbv + (1.0 - lbv) * _sigmoid(f))
        q_out[...], k_out[...], lf_out[...] = q_all, k_all, lf_all
        st = _Heads(st_ref[h] for h in range(N_HEADS))
        g_all = _nn_exact(_tri(CHUNK, "lower"), lf_all)
        o, st2 = _hg_chunk_fwd_math(_heads_of(q_all), _heads_of(k_all), _heads_of(v_ref), _heads_of(g_all), st)
        y = _head_norm_fwd(o, _heads_of(z_ref), w_ref[...])
        for h in range(N_HEADS):
            st_out_ref[0, h] = st.v[h]
            o_ref[:, h * HEAD_DIM:(h + 1) * HEAD_DIM] = o.v[h]
            y_ref[:, h * HEAD_DIM:(h + 1) * HEAD_DIM] = y.v[h]
            st_ref[h] = st2.v[h]

    blk = lambda off: pl.BlockSpec((CHUNK, BR_WIDTH), lambda c: (c, off))
    return pl.pallas_call(
        body, name=name, grid=(n,),
        in_specs=[blk(C_HQ // BR_WIDTH), blk(C_HF // BR_WIDTH), blk(C_HI // BR_WIDTH), pl.BlockSpec((1, BR_WIDTH), lambda c: (0, 0)),
                  blk(C_HZ // BR_WIDTH), pl.BlockSpec((1, HEAD_DIM), lambda c: (0, 0))],
        out_specs=[blk(0), pl.BlockSpec((1, N_HEADS, HEAD_DIM, HEAD_DIM), lambda c: (c, 0, 0, 0)), blk(0), blk(0), blk(0), blk(0)],
        out_shape=[jax.ShapeDtypeStruct((s, BR_WIDTH), F32), jax.ShapeDtypeStruct((n, N_HEADS, HEAD_DIM, HEAD_DIM), F32)]
        + [jax.ShapeDtypeStruct((s, BR_WIDTH), F32)] * 3 + [jax.ShapeDtypeStruct((s, BR_WIDTH), BF16)],
        scratch_shapes=[pltpu.VMEM((N_HEADS, HEAD_DIM, HEAD_DIM), F32)],
        compiler_params=_cp("arbitrary"),
    )(proj, proj, proj, lb, proj, norm_w.reshape(1, HEAD_DIM))


def _hg_chunk_bwd(proj, lb, qh, kh, lf, states, o, norm_w, dy, *, name):
    s = proj.shape[0]
    n = s // CHUNK

    def body(hq_ref, hf_ref, v_ref, lb_ref, q_ref, k_ref, lf_ref, st_in_ref, o_ref, z_ref, w_ref, dy_ref,
             dhq_ref, dhf_ref, dhi_ref, dz_ref, dlb_ref, dw_ref, dst_ref):
        @pl.when(pl.program_id(0) == 0)
        def _():
            dst_ref[...] = jnp.zeros_like(dst_ref)
            dlb_ref[...] = jnp.zeros_like(dlb_ref)
            dw_ref[...] = jnp.zeros_like(dw_ref)

        do, dz, dw = _head_norm_bwd(_heads_of(o_ref), _heads_of(z_ref), w_ref[...], _heads_of(dy_ref))
        dw_ref[...] += dw

        g_all = _nn_exact(_tri(CHUNK, "lower"), lf_ref[...])
        dq, dk, dv, db, later, dst = _hg_chunk_bwd_math(
            _heads_of(q_ref), _heads_of(k_ref), _heads_of(v_ref), _heads_of(g_all),
            _Heads(st_in_ref[0, h] for h in range(N_HEADS)), do,
            _Heads(dst_ref[h] for h in range(N_HEADS)))
        dlf = _nn_exact(_tri(CHUNK, "upper"), jnp.concatenate(db.v, axis=1)) + jnp.concatenate(later.v, axis=1)
        dq_all, dk_all = jnp.concatenate(dq.v, axis=1), jnp.concatenate(dk.v, axis=1)
        f, lbv = hf_ref[...], lb_ref[...]
        dhq_ref[...] = (dq_all * _dsilu(hq_ref[...])).astype(BF16)
        sp, sn = _sigmoid(f), _sigmoid(-f)
        dlf_over = dlf / (lbv + (1.0 - lbv) * sp)
        dhf_ref[...] = (dlf_over * (1.0 - lbv) * sp * sn - dk_all * (1.0 - lbv) * sn * (1.0 - sn)).astype(BF16)
        dlb_ref[...] += jnp.sum(dlf_over * (1.0 - sp) - dk_all * sn, axis=0, keepdims=True)
        for h in range(N_HEADS):
            dhi_ref[:, h * HEAD_DIM:(h + 1) * HEAD_DIM] = dv.v[h].astype(BF16)
            dz_ref[:, h * HEAD_DIM:(h + 1) * HEAD_DIM] = dz.v[h]
            dst_ref[h] = dst.v[h]

    blk = lambda off: pl.BlockSpec((CHUNK, BR_WIDTH), lambda c: (n - 1 - c, off))
    vec = pl.BlockSpec((1, BR_WIDTH), lambda c: (0, 0))
    wvec = pl.BlockSpec((1, HEAD_DIM), lambda c: (0, 0))
    return pl.pallas_call(
        body, name=name, grid=(n,),
        in_specs=[blk(C_HQ // BR_WIDTH), blk(C_HF // BR_WIDTH), blk(C_HI // BR_WIDTH), vec, blk(0), blk(0), blk(0),
                  pl.BlockSpec((1, N_HEADS, HEAD_DIM, HEAD_DIM), lambda c: (n - 1 - c, 0, 0, 0)), blk(0), blk(C_HZ // BR_WIDTH),
                  wvec, blk(1)],
        out_specs=[blk(0), blk(0), blk(0), blk(0), vec, wvec],
        out_shape=[jax.ShapeDtypeStruct((s, BR_WIDTH), BF16)] * 4 + [jax.ShapeDtypeStruct((1, BR_WIDTH), F32),
                                                                    jax.ShapeDtypeStruct((1, HEAD_DIM), F32)],
        scratch_shapes=[pltpu.VMEM((N_HEADS, HEAD_DIM, HEAD_DIM), F32)],
        compiler_params=_cp("arbitrary"),
    )(proj, proj, proj, lb, qh, kh, lf, states, o, proj, norm_w.reshape(1, HEAD_DIM), dy)


_ANY = pl.BlockSpec(memory_space=pl.ANY)
_MESH = pl.DeviceIdType.MESH


def _all_gather(x_local, *, name, after=()):
    n_after = len(after)

    def body(x_ref, *refs):
        out_ref, send_sems, recv_sems, local_sem = refs[n_after:]
        x, y, c = lax.axis_index("x"), lax.axis_index("y"), lax.axis_index("c")
        me, sibling = (x, y, c), (x, y, 1 - c)
        n1 = (x ^ (1 - c), y ^ c)
        n2 = (x ^ c, y ^ (1 - c))
        dg = (1 - x, 1 - y)

        def slot(px, py, pc):
            return out_ref.at[4 * px + 2 * py + pc]

        def copy(k, block, to, src=None):
            return pltpu.make_async_remote_copy(
                src_ref=slot(*block) if src is None else src, dst_ref=slot(*block),
                send_sem=send_sems.at[k], recv_sem=recv_sems.at[k], device_id=to, device_id_type=_MESH)

        mine = pltpu.make_async_copy(x_ref, slot(*me), local_sem)
        mine.start()
        first = [copy(0, me, sibling, src=x_ref), copy(1, me, (*n1, c), src=x_ref), copy(2, me, (*n2, c), src=x_ref)]
        for cp in first:
            cp.start()
        copy(2, (*n2, c), me).wait_recv()
        forward = copy(3, (*n2, c), (*n1, c))
        forward.start()
        passed = [copy(5, (*n2, c), sibling)]
        passed[0].start()
        copy(1, (*n1, c), me).wait_recv()
        passed.append(copy(4, (*n1, c), sibling))
        passed[1].start()
        copy(3, (*dg, c), me).wait_recv()
        passed.append(copy(6, (*dg, c), sibling))
        passed[2].start()
        copy(0, sibling, me).wait_recv()
        copy(4, (*n2, 1 - c), me).wait_recv()
        copy(5, (*n1, 1 - c), me).wait_recv()
        copy(6, (*dg, 1 - c), me).wait_recv()
        for cp in first + [forward] + passed:
            cp.wait_send()
        mine.wait()

    return pl.pallas_call(
        body, name=name, out_shape=jax.ShapeDtypeStruct((N_DEV,) + x_local.shape, x_local.dtype),
        in_specs=[_ANY] * (1 + n_after), out_specs=_ANY,
        scratch_shapes=[pltpu.SemaphoreType.DMA((7,)), pltpu.SemaphoreType.DMA((7,)), pltpu.SemaphoreType.DMA],
    )(x_local, *after)


_HBM = pl.BlockSpec(memory_space=pltpu.HBM)
_SEM = pl.BlockSpec(memory_space=pltpu.SEMAPHORE)
_EFFECT = pltpu.SideEffectType.DATAFLOW_SIDE_EFFECTING


def _peers():
    x, y, c = lax.axis_index("x"), lax.axis_index("y"), lax.axis_index("c")
    out = []
    for k in range(1, N_DEV):
        px, py, pc = x ^ ((k >> 2) & 1), y ^ ((k >> 1) & 1), c ^ (k & 1)
        out.append(((px, py, pc), 4 * px + 2 * py + pc))
    return 4 * x + 2 * y + c, out


def _push_copies(src_ref, land_ref, send_sems, recv_sems, broadcast):
    my, peers = _peers()
    pairs = []
    for k, (pos, idx) in enumerate(peers):
        src = src_ref if broadcast else src_ref.at[idx]
        send = pltpu.make_async_remote_copy(src_ref=src, dst_ref=land_ref.at[my], send_sem=send_sems.at[k],
                                            recv_sem=recv_sems.at[k], device_id=pos, device_id_type=_MESH)
        recv = pltpu.make_async_remote_copy(src_ref=src, dst_ref=land_ref.at[idx], send_sem=send_sems.at[k],
                                            recv_sem=recv_sems.at[k], device_id=pos, device_id_type=_MESH)
        pairs.append((send, recv))
    return pairs


def _push_start(src, land, *, broadcast, name, after=()):
    n_after = len(after)

    def body(src_ref, land_ref, *refs):
        send_sems, recv_sems, _, _, token = refs[n_after:]
        for send, _ in _push_copies(src_ref, land_ref, send_sems, recv_sems, broadcast):
            send.start()
        token[...] = jnp.zeros_like(token)

    return pl.pallas_call(
        body, name=name,
        out_shape=(pltpu.SemaphoreType.DMA((N_DEV - 1,)), pltpu.SemaphoreType.DMA((N_DEV - 1,)),
                   pltpu.HBM(src.shape, src.dtype), pltpu.HBM(land.shape, land.dtype), jax.ShapeDtypeStruct((8, 128), F32)),
        in_specs=(_HBM, _HBM) + (_ANY,) * n_after, out_specs=(_SEM, _SEM, _HBM, _HBM, pl.BlockSpec(memory_space=pltpu.VMEM)),
        input_output_aliases={0: 2, 1: 3}, compiler_params=pltpu.CompilerParams(has_side_effects=_EFFECT),
    )(pltpu.with_memory_space_constraint(src, pltpu.HBM), pltpu.with_memory_space_constraint(land, pltpu.HBM), *after)


def _push_wait(handle, after, *, broadcast, name):
    send_sems, recv_sems, src_thru, land_thru, _ = handle

    def body(src_ref, land_ref, send_sems, recv_sems, *rest):
        for send, recv in _push_copies(src_ref, land_ref, send_sems, recv_sems, broadcast):
            send.wait_send()
            recv.wait_recv()

    return pl.pallas_call(
        body, name=name,
        out_shape=(pltpu.HBM(src_thru.shape, src_thru.dtype), pltpu.HBM(land_thru.shape, land_thru.dtype)),
        in_specs=(_HBM, _HBM, _SEM, _SEM) + (_ANY,) * len(after), out_specs=(_HBM, _HBM),
        input_output_aliases={0: 0, 1: 1}, compiler_params=pltpu.CompilerParams(has_side_effects=_EFFECT),
    )(src_thru, land_thru, send_sems, recv_sems, *after)[1]


def _relay_copies(src_ref, land_ref, sems_a, sems_b):
    x, y, c = lax.axis_index("x"), lax.axis_index("y"), lax.axis_index("c")
    slot = lambda px, py, pc: land_ref.at[4 * px + 2 * py + pc]
    chips = [(1 - x, y), (x, 1 - y), (1 - x, 1 - y)]
    (send_a, recv_a), (send_b, recv_b) = sems_a, sems_b

    def copy(sems, k, src, dst_slot, to):
        return pltpu.make_async_remote_copy(src_ref=src, dst_ref=dst_slot, send_sem=sems[0].at[k], recv_sem=sems[1].at[k],
                                            device_id=to, device_id_type=_MESH)

    first = [copy((send_a, recv_a), 0, src_ref, slot(x, y, c), (x, y, 1 - c))]
    first += [copy((send_a, recv_a), 1 + j, src_ref, slot(x, y, c), (*chip, c)) for j, chip in enumerate(chips)]
    first_in = [copy((send_a, recv_a), 0, src_ref, slot(x, y, 1 - c), (x, y, 1 - c))]
    first_in += [copy((send_a, recv_a), 1 + j, src_ref, slot(*chip, c), (*chip, c)) for j, chip in enumerate(chips)]
    relay = [copy((send_b, recv_b), j, slot(*chip, c), slot(*chip, c), (x, y, 1 - c)) for j, chip in enumerate(chips)]
    relay_in = [copy((send_b, recv_b), j, slot(*chip, 1 - c), slot(*chip, 1 - c), (x, y, 1 - c)) for j, chip in enumerate(chips)]
    return first, first_in, relay, relay_in


def _relay_start(src, land, *, name, after=()):
    n_after = len(after)

    def body(src_ref, land_ref, *refs):
        send_a, recv_a, _, _, token = refs[n_after:]
        for cp in _relay_copies(src_ref, land_ref, (send_a, recv_a), (send_a, recv_a))[0]:
            cp.start()
        token[...] = jnp.zeros_like(token)

    send_a, recv_a, src_thru, land_thru, token = pl.pallas_call(
        body, name=name,
        out_shape=(pltpu.SemaphoreType.DMA((4,)), pltpu.SemaphoreType.DMA((4,)), pltpu.HBM(src.shape, src.dtype),
                   pltpu.HBM(land.shape, land.dtype), jax.ShapeDtypeStruct((8, 128), F32)),
        in_specs=(_HBM, _HBM) + (_ANY,) * n_after, out_specs=(_SEM, _SEM, _HBM, _HBM, pl.BlockSpec(memory_space=pltpu.VMEM)),
        input_output_aliases={0: 2, 1: 3}, compiler_params=pltpu.CompilerParams(has_side_effects=_EFFECT),
    )(pltpu.with_memory_space_constraint(src, pltpu.HBM), pltpu.with_memory_space_constraint(land, pltpu.HBM), *after)
    return (send_a, recv_a), src_thru, land_thru, token


def _relay_mid(handle, after, *, name):
    sems_a, src_thru, land_thru, _ = handle
    n_after = len(after)

    def body(src_ref, land_ref, send_a, recv_a, *refs):
        send_b, recv_b, _, _, token = refs[n_after:]
        _, first_in, relay, _ = _relay_copies(src_ref, land_ref, (send_a, recv_a), (send_b, recv_b))
        for j in range(3):
            first_in[1 + j].wait_recv()
            relay[j].start()
        token[...] = jnp.zeros_like(token)

    send_b, recv_b, src2, land2, token = pl.pallas_call(
        body, name=name,
        out_shape=(pltpu.SemaphoreType.DMA((3,)), pltpu.SemaphoreType.DMA((3,)), pltpu.HBM(src_thru.shape, src_thru.dtype),
                   pltpu.HBM(land_thru.shape, land_thru.dtype), jax.ShapeDtypeStruct((8, 128), F32)),
        in_specs=(_HBM, _HBM, _SEM, _SEM) + (_ANY,) * n_after,
        out_specs=(_SEM, _SEM, _HBM, _HBM, pl.BlockSpec(memory_space=pltpu.VMEM)),
        input_output_aliases={0: 2, 1: 3}, compiler_params=pltpu.CompilerParams(has_side_effects=_EFFECT),
    )(src_thru, land_thru, *sems_a, *after)
    return sems_a, (send_b, recv_b), src2, land2, token


def _relay_wait(handle, after, *, name):
    sems_a, sems_b, src_thru, land_thru, _ = handle

    def body(src_ref, land_ref, send_a, recv_a, send_b, recv_b, *rest):
        first, first_in, relay, relay_in = _relay_copies(src_ref, land_ref, (send_a, recv_a), (send_b, recv_b))
        first_in[0].wait_recv()
        for cp in relay_in:
            cp.wait_recv()
        for cp in first + relay:
            cp.wait_send()

    return pl.pallas_call(
        body, name=name,
        out_shape=(pltpu.HBM(src_thru.shape, src_thru.dtype), pltpu.HBM(land_thru.shape, land_thru.dtype)),
        in_specs=(_HBM, _HBM, _SEM, _SEM, _SEM, _SEM) + (_ANY,) * len(after), out_specs=(_HBM, _HBM),
        input_output_aliases={0: 0, 1: 1}, compiler_params=pltpu.CompilerParams(has_side_effects=_EFFECT),
    )(src_thru, land_thru, *sems_a, *sems_b, *after)[1]


def _adamw(parts, row_off, w, m, v, *, layer=0, n_layers=1, prev=None, name, tr):
    rows, c = w.shape
    r = rows // n_layers
    np_ = parts.shape[0]
    tr = min(tr, r)
    assert r % tr == 0 and row_off % tr == 0
    ob, lb = row_off // tr, layer * (r // tr)
    c1 = 1.0 - ADAM_B1 ** ADAM_STEP
    c2 = 1.0 - ADAM_B2 ** ADAM_STEP
    n_prev = 0 if prev is None else 4

    def body(p_ref, w_ref, m_ref, v_ref, *refs):
        g_ref, d_ref, nm_ref, nv_ref = refs[n_prev:]
        g = p_ref[0].astype(F32)
        for s in range(1, np_):
            g = g + p_ref[s].astype(F32)
        wv = w_ref[...]
        m2 = ADAM_B1 * m_ref[...] + (1.0 - ADAM_B1) * g
        v2 = ADAM_B2 * v_ref[...] + (1.0 - ADAM_B2) * jnp.square(g)
        m_hat = m2 / c1
        v_hat = v2 / c2
        g_ref[...] = g
        d_ref[...] = -ADAM_LR * (m_hat / (jnp.sqrt(v_hat) + ADAM_EPS) + ADAM_WD * wv)
        nm_ref[...] = m2
        nv_ref[...] = v2

    blk = pl.BlockSpec((tr, c), lambda i: (lb + i, 0))
    return pl.pallas_call(
        body, name=name, grid=(r // tr,),
        in_specs=[pl.BlockSpec((np_, tr, c), lambda i: (0, ob + i, 0)), blk, blk, blk] + [_ANY] * n_prev,
        out_specs=[blk] * 4, out_shape=[jax.ShapeDtypeStruct((rows, c), F32)] * 4,
        input_output_aliases={4 + i: i for i in range(n_prev)}, compiler_params=_cp("parallel"),
    )(parts, w, m, v, *(prev or ()))


def _sum_parts(parts, *, name, after=()):
    np_, r, c = parts.shape

    def body(p_ref, *refs):
        o_ref = refs[-1]
        g = p_ref[0]
        for s in range(1, np_):
            g = g + p_ref[s]
        o_ref[...] = g

    vmem = pl.BlockSpec(memory_space=pltpu.VMEM)
    return pl.pallas_call(body, name=name, in_specs=[vmem] + [_ANY] * len(after), out_specs=vmem,
                          out_shape=jax.ShapeDtypeStruct((r, c), F32))(parts, *after)


def _pack(arrs):
    rows = []
    for a in arrs:
        f = a.reshape(-1).astype(F32)
        pad = (-f.shape[0]) % 128
        rows.append(jnp.pad(f, (0, pad)).reshape(-1, 128))
    out = jnp.concatenate(rows, axis=0)
    return jnp.pad(out, ((0, (-out.shape[0]) % 8), (0, 0)))


def _unpack(packed, shapes):
    outs, r0 = [], 0
    for shp in shapes:
        n = 1
        for d in shp:
            n *= d
        nr = -(-n // 128)
        outs.append(packed[r0:r0 + nr].reshape(-1)[:n].reshape(shp))
        r0 += nr
    return outs


_WIN_PIECES = ((0, 4096, 0), (4112, 8208, 0), (4096, 4104, HEAD_DIM - N_HEADS), (4104, 4112, HEAD_DIM - N_HEADS))


RELAYOUT_TILE = 256
LAST_SPLIT = 4
OTHER_SPLIT = 2


def _win_from_shards(shards, *, name):
    k = shards.shape[1]
    tr = min(RELAYOUT_TILE, k)

    def body(x_ref, o_ref):
        cols = []
        for lo, hi, pad in _WIN_PIECES:
            for j in range(N_DEV):
                a, b = max(lo, j * SHARD_IN), min(hi, (j + 1) * SHARD_IN)
                if a < b:
                    cols.append(x_ref[j, :, a - j * SHARD_IN:b - j * SHARD_IN])
            if pad:
                cols.append(jnp.zeros((tr, pad), x_ref.dtype))
        o_ref[...] = jnp.concatenate(cols, axis=1)

    return pl.pallas_call(
        body, name=name, grid=(k // tr,), in_specs=[pl.BlockSpec((N_DEV, tr, SHARD_IN), lambda i: (0, i, 0))],
        out_specs=pl.BlockSpec((tr, N_PROJ), lambda i: (i, 0)), out_shape=jax.ShapeDtypeStruct((k, N_PROJ), shards.dtype),
        compiler_params=_cp("parallel"),
    )(shards)


def _win_to_shards(g, *, name):
    k = g.shape[0]
    tr = min(RELAYOUT_TILE, k)
    starts, off = [], 0
    for lo, hi, pad in _WIN_PIECES:
        starts.append((lo, hi, off))
        off += hi - lo + pad

    def body(g_ref, o_ref):
        for j in range(N_DEV):
            cols = []
            for lo, hi, off in sorted(starts):
                a, b = max(lo, j * SHARD_IN), min(hi, (j + 1) * SHARD_IN)
                if a < b:
                    cols.append(g_ref[:, off + a - lo:off + b - lo])
            o_ref[j] = jnp.concatenate(cols, axis=1)

    return pl.pallas_call(
        body, name=name, grid=(k // tr,), in_specs=[pl.BlockSpec((tr, N_PROJ), lambda i: (i, 0))],
        out_specs=pl.BlockSpec((N_DEV, tr, SHARD_IN), lambda i: (0, i, 0)),
        out_shape=jax.ShapeDtypeStruct((N_DEV, k, SHARD_IN), g.dtype), compiler_params=_cp("parallel"),
    )(g)


def _lower_bounds(logits):
    probs = jax.nn.softmax(logits.astype(F32), axis=0)
    return jnp.cumsum(probs, axis=0) - probs[0]


def _pad_lanes(vec8):
    return jnp.pad(vec8.reshape(1, N_HEADS), ((0, 0), (0, HEAD_DIM - N_HEADS)))


def kernel(x, p, norm_w, w_in, dn_conv_w, dn_A_log, dn_dt_bias, dn_norm_w, hg_lb_logits, hg_norm_w, w_out, w_ple_up, w_ple_gate, final_norm_w, loss_target, m_norm_w, m_w_in, m_dn_conv_w, m_dn_A_log, m_dn_dt_bias, m_dn_norm_w, m_hg_lb_logits, m_hg_norm_w, m_w_out, m_w_ple_up, m_w_ple_gate, m_final_norm_w, v_norm_w, v_w_in, v_dn_conv_w, v_dn_A_log, v_dn_dt_bias, v_dn_norm_w, v_hg_lb_logits, v_hg_norm_w, v_w_out, v_w_ple_up, v_w_ple_gate, v_final_norm_w):
    depth = norm_w.shape[0]
    my = 4 * lax.axis_index("x") + 2 * lax.axis_index("y") + lax.axis_index("c")
    h = x[0]
    tgt = loss_target[0]
    rows_out = D_MODEL // N_DEV
    up_rows = PLE_DIM * (D_MODEL // N_DEV) // D_MODEL
    g_off, u_off = rows_out, 2 * rows_out

    def own_slot(block):
        return lax.dynamic_update_index_in_dim(lax.empty((N_DEV,) + block.shape, block.dtype), block, my, 0)

    win_bf = w_in.astype(BF16)
    rest_bf = [jnp.concatenate([w_out[l], w_ple_gate[l], w_ple_up[l].reshape(up_rows, D_MODEL)], axis=0).astype(BF16)
               for l in range(depth)]
    conv_all = _all_gather(dn_conv_w, name="gather_conv_w")
    conv_full = conv_all.transpose(1, 2, 0, 3).reshape(depth, CONV_W, 3 * BR_WIDTH)
    win_all = {0: _all_gather(win_bf[0], name="gather_w_in_l0", after=[conv_all])}
    pending, relayed = {}, {}
    last = win_all[0]
    for l in range(depth):
        if l > 0:
            relayed["win", l] = _relay_start(win_bf[l], own_slot(win_bf[l]), after=[last], name=f"gather_w_in_l{l}_first")
            last = relayed["win", l][3]
        if l == 0:
            relayed["rest", l] = _relay_start(rest_bf[l], own_slot(rest_bf[l]), after=[last], name=f"gather_rest_l{l}_first")
            last = relayed["rest", l][3]
        else:
            pending["rest", l] = _push_start(rest_bf[l], own_slot(rest_bf[l]), broadcast=True, after=[last],
                                             name=f"gather_rest_l{l}_start")
            last = pending["rest", l][4]
    order_tok = last[0, 0]
    lbs = _lower_bounds(hg_lb_logits)

    saved = []
    weights = []
    for l in range(depth):
        tag = f"l{l}"
        if l > 0:
            win_all[l] = _relay_wait(relayed["win", l], [h], name=f"gather_w_in_{tag}_wait")
        wi = _win_from_shards(win_all[l], name=f"w_in_layout_{tag}")
        nw = norm_w[l] + order_tok if l == 0 else norm_w[l]
        hn = _rms_fwd(h, nw, name=f"rms_fwd_{tag}")
        proj = _mm(hn, wi, mode="nn", out_dtype=F32, name=f"mm_proj_{tag}")
        al, dt = _pad_lanes(dn_A_log[l]), _pad_lanes(dn_dt_bias[l])
        qkv = _dn_qkv_fwd(proj, conv_full[l], name=f"dn_qkv_fwd_{tag}")
        if ("rest", l) in relayed:
            relayed["rest", l] = _relay_mid(relayed["rest", l], [qkv], name=f"gather_rest_{tag}_relay")
            al = al + relayed["rest", l][4][0, 0]
        beta, gcs = _dn_gate_fwd(proj, al, dt, name=f"dn_gate_fwd_{tag}")
        o_dn, st_dn, tinv_dn, y_dn = _dn_chunk_fwd(qkv, gcs, beta, proj, dn_norm_w[l], name=f"dn_chunk_fwd_{tag}")
        lb = lbs[l].reshape(1, BR_WIDTH)
        o_hg, st_hg, qh, kh, lf, y_hg = _hg_chunk_fwd(proj, lb, hg_norm_w[l], name=f"hg_chunk_fwd_{tag}")
        y = jnp.concatenate([y_dn, y_hg], axis=1)
        if ("rest", l) in relayed:
            rest_all = _relay_wait(relayed["rest", l], [y], name=f"gather_rest_{tag}_wait")
        else:
            rest_all = _push_wait(pending["rest", l], [y], broadcast=True, name=f"gather_rest_{tag}_wait")
        w_out_rows, w_gate_rows = (0, rows_out), (g_off, rows_out)
        wu = rest_all[:, u_off:u_off + up_rows].reshape(N_DEV, PLE_DIM, D_MODEL // N_DEV).transpose(1, 0, 2).reshape(PLE_DIM, D_MODEL)
        weights.append((wi, rest_all, wu))
        h1 = _mm(y, rest_all, mode="nn", b_rows=w_out_rows, out_dtype=F32, res=h, name=f"mm_out_{tag}")
        pin = []
        if ("win", l + 1) in relayed:
            relayed["win", l + 1] = _relay_mid(relayed["win", l + 1], [h1], name=f"gather_w_in_l{l + 1}_relay")
            pin = [relayed["win", l + 1][4]]
        up = _mm(p[l, 0], wu, mode="nn", out_dtype=F32, name=f"mm_up_{tag}")
        gp, h2 = _mm(h1, rest_all, mode="nn", b_rows=w_gate_rows, out_dtype=F32, after=pin, tile_m=512, fused=("ple", h1, up),
                     name=f"mm_gate_{tag}")
        saved.append(dict(h=h, hn=hn, proj=proj, qkv=qkv, beta=beta, gcs=gcs, st_dn=st_dn, tinv_dn=tinv_dn, qh=qh, kh=kh, lf=lf,
                          st_hg=st_hg, o_dn=o_dn, o_hg=o_hg, y=y, h1=h1, gp=gp, up=up, al=al, dt=dt, lb=lb))
        h = h2

    loss_row, dh, d_final_w = _final_fwd_bwd(h, final_norm_w, tgt, name="final_norm_loss")

    d_norm_w, d_alog, d_dt, d_dn_nw, d_hg_nw, d_lb, d_conv = ([None] * depth for _ in range(7))
    sent = {}
    for l in reversed(range(depth)):
        wi, rest_all, wu = weights[l]
        sv = saved[l]
        tag = f"l{l}"
        dup, dgp = _ple_bwd(dh, sv["gp"], sv["up"], name=f"ple_bwd_{tag}")
        d_wu = _mm(p[l, 0], dup, mode="tn", out_dtype=BF16, name=f"mm_dwup_{tag}")
        d_wg = _mm(sv["h1"], dgp, mode="tn", out_dtype=BF16, name=f"mm_dwgate_{tag}")
        dh1 = _mm(dgp, rest_all, mode="nt", b_rows=(g_off, rows_out), out_dtype=F32, res=dh, name=f"mm_dh1_{tag}")
        d_wo = _mm(sv["y"], dh1, mode="tn", out_dtype=BF16, name=f"mm_dwout_{tag}")
        parts_rest = jnp.concatenate(
            [d_wo.reshape(N_DEV, rows_out, D_MODEL), d_wg.reshape(N_DEV, rows_out, D_MODEL),
             d_wu.reshape(PLE_DIM, N_DEV, D_MODEL // N_DEV).transpose(1, 0, 2).reshape(N_DEV, up_rows, D_MODEL)], axis=1)
        sent["rest", l] = _push_start(parts_rest, own_slot(parts_rest[my]), broadcast=False, name=f"exchange_rest_{tag}_start")
        dy = _mm(dh1, rest_all, mode="nt", b_rows=(0, rows_out), out_dtype=F32, name=f"mm_dy_{tag}")
        dn_nw = dn_norm_w[l] + sent["rest", l][4][0, 0]
        dqkv, d_gc, dbeta, dz_dn, d_dn_nw[l] = _dn_chunk_bwd(sv["qkv"], sv["gcs"], sv["beta"], sv["st_dn"], sv["tinv_dn"],
                                                             sv["o_dn"], sv["proj"], dn_nw, dy, name=f"dn_chunk_bwd_{tag}")
        dqkv_pre, d_conv[l] = _dn_qkv_bwd(sv["proj"], conv_full[l], dqkv, name=f"dn_qkv_bwd_{tag}")
        db, da, d_alog[l], d_dt[l] = _dn_gate_bwd(sv["proj"], sv["al"], sv["dt"], dbeta, d_gc, name=f"dn_gate_bwd_{tag}")
        dhq, dhf, dhi, dz_hg, d_lb[l], d_hg_nw[l] = _hg_chunk_bwd(sv["proj"], sv["lb"], sv["qh"], sv["kh"], sv["lf"], sv["st_hg"],
                                                                  sv["o_hg"], hg_norm_w[l], dy, name=f"hg_chunk_bwd_{tag}")
        dproj = jnp.concatenate([dqkv_pre, dz_dn, dhq, dhf, dhi, dz_hg, db, da], axis=1)
        def push_d_win(after):
            n_split = LAST_SPLIT if l == 0 else OTHER_SPLIT
            rows = D_MODEL // n_split
            handles = []
            for q in range(n_split):
                hn_q = sv["hn"] if n_split == 1 else sv["hn"][:, q * rows:(q + 1) * rows]
                sfx = tag if n_split == 1 else f"{tag}_{q}"
                d_win = _mm(hn_q, dproj, mode="tn", out_dtype=BF16, after=after, name=f"mm_dwin_{sfx}")
                parts_in = _win_to_shards(d_win, name=f"dw_in_shards_{sfx}")
                handles.append(_push_start(parts_in, own_slot(parts_in[my]), broadcast=False, after=after,
                                           name=f"exchange_w_in_{sfx}_start"))
                after = [handles[-1][4]]
            return handles

        if l == 0:
            small = _pack([loss_row, jnp.concatenate(d_norm_w[1:], axis=0), d_final_w,
                           jnp.stack([a[0, :N_HEADS] for a in d_alog]), jnp.stack([a[0, :N_HEADS] for a in d_dt]),
                           jnp.concatenate(d_dn_nw, axis=0), jnp.concatenate(d_hg_nw, axis=0), jnp.concatenate(d_lb, axis=0),
                           jnp.stack(d_conv)])
            small_all = _all_gather(small, name="gather_small")
        sent["win", l] = push_d_win([small_all] if l == 0 else [])
        dh, d_norm_w[l] = _mm(dproj, wi, mode="nt", out_dtype=F32, tile_m=512, tile_n=D_MODEL, tile_k=1408,
                              after=[sent["win", l][-1][4]],
                              fused=("rms_bwd", sv["h"], norm_w[l], dh1), name=f"mm_dhn_{tag}")
    grad_x = dh[None]

    small_shapes = [(1, 128), (depth - 1, D_MODEL), final_norm_w.shape, dn_A_log.shape, dn_dt_bias.shape, dn_norm_w.shape,
                    hg_norm_w.shape, hg_lb_logits.shape, (depth, CONV_W, 3 * BR_WIDTH)]
    tot = _unpack(_sum_parts(small_all, after=[grad_x], name="sum_small"), small_shapes)
    loss = tot[0][0, 0]
    g_lb = tot[7]
    g_logits = jax.vjp(_lower_bounds, hg_lb_logits)[1](g_lb)[0]
    g_conv = lax.dynamic_slice_in_dim(tot[8], my * (3 * BR_WIDTH // N_DEV), 3 * BR_WIDTH // N_DEV, axis=2)
    small_g = [g_conv, tot[3], tot[4], tot[5], g_logits, tot[6], tot[2]]
    small_w = [dn_conv_w, dn_A_log, dn_dt_bias, dn_norm_w, hg_lb_logits, hg_norm_w, final_norm_w]
    small_m = [m_dn_conv_w, m_dn_A_log, m_dn_dt_bias, m_dn_norm_w, m_hg_lb_logits, m_hg_norm_w, m_final_norm_w]
    small_v = [v_dn_conv_w, v_dn_A_log, v_dn_dt_bias, v_dn_norm_w, v_hg_lb_logits, v_hg_norm_w, v_final_norm_w]
    pk_w = _pack(small_w)
    res_small = _adamw(_pack(small_g)[None], 0, pk_w, _pack(small_m), _pack(small_v), name="adamw_small", tr=pk_w.shape[0])
    shapes_w = [a.shape for a in small_w]
    sg, sd, sm, sv_ = (_unpack(r, shapes_w) for r in res_small)

    r_win = r_wo = r_wg = r_wu = None
    done = [grad_x, res_small[0]]

    def flat(a, cols):
        return a.reshape(-1, cols)

    for l in reversed(range(depth)):
        tag = f"l{l}"
        land_rest = _push_wait(sent["rest", l], done, broadcast=False, name=f"exchange_rest_{tag}_wait")
        r_wo = _adamw(land_rest, 0, flat(w_out, D_MODEL), flat(m_w_out, D_MODEL), flat(v_w_out, D_MODEL), layer=l,
                      n_layers=depth, prev=r_wo, name=f"adamw_w_out_{tag}", tr=rows_out)
        r_wg = _adamw(land_rest, g_off, flat(w_ple_gate, D_MODEL), flat(m_w_ple_gate, D_MODEL), flat(v_w_ple_gate, D_MODEL),
                      layer=l, n_layers=depth, prev=r_wg, name=f"adamw_w_gate_{tag}", tr=rows_out)
        r_wu = _adamw(land_rest, u_off, flat(w_ple_up, D_MODEL), flat(m_w_ple_up, D_MODEL), flat(v_w_ple_up, D_MODEL),
                      layer=l, n_layers=depth, prev=r_wu, name=f"adamw_w_up_{tag}", tr=up_rows)
        done = [r_wo[0], r_wg[0], r_wu[0]]
    for l in reversed(range(depth)):
        tag = f"l{l}"
        if l == 0:
            nw0 = _sum_parts(_all_gather(_pack([d_norm_w[0]]), after=done, name="gather_norm_w"), name="sum_norm_w")
            g_norm_w = jnp.concatenate([_unpack(nw0, [(1, D_MODEL)])[0], tot[1]], axis=0)
            pk_nw = _pack([norm_w])
            r_nw = _adamw(_pack([g_norm_w])[None], 0, pk_nw, _pack([m_norm_w]), _pack([v_norm_w]), name="adamw_norm_w",
                          tr=pk_nw.shape[0])
            r_nw = [_unpack(r, [norm_w.shape])[0] for r in r_nw]
            done = [r_nw[0]]
        n_split = len(sent["win", l])
        for q, handle in enumerate(sent["win", l]):
            sfx = tag if n_split == 1 else f"{tag}_{q}"
            land_in = _push_wait(handle, done, broadcast=False, name=f"exchange_w_in_{sfx}_wait")
            r_win = _adamw(land_in, 0, flat(w_in, SHARD_IN), flat(m_w_in, SHARD_IN), flat(v_w_in, SHARD_IN),
                           layer=l * n_split + q, n_layers=depth * n_split, prev=r_win, name=f"adamw_w_in_{sfx}", tr=256)
            done = [r_win[0]]
    r_win = [o.reshape(w_in.shape) for o in r_win]
    r_wo = [o.reshape(w_out.shape) for o in r_wo]
    r_wg = [o.reshape(w_ple_gate.shape) for o in r_wg]
    r_wu = [o.reshape(w_ple_up.shape) for o in r_wu]

    def order(nw, small_list, big_in, big_out, big_up, big_gate):
        cw, al_, dt_, dnw, lbl, hnw, fw = small_list
        return [nw, big_in, cw, al_, dt_, dnw, lbl, hnw, big_out, big_up, big_gate, fw]

    outs = [loss, grad_x]
    for i, sl in enumerate((sg, sd, sm, sv_)):
        outs += order(r_nw[i], sl, r_win[i], r_wo[i], r_wu[i], r_wg[i])
    return tuple(outs)
```

```python
import functools

import jax
import jax.numpy as jnp
from jax import lax
from jax.experimental import pallas as pl
from jax.experimental.pallas import tpu as pltpu

F32 = jnp.float32
BF16 = jnp.bfloat16
HIGHEST = lax.Precision.HIGHEST

N_DEV = 8
D_MODEL = 2048
PLE_DIM = 256
HEAD_DIM = 128
N_HEADS = 8
BR_WIDTH = N_HEADS * HEAD_DIM
CHUNK = 64
SUB = 16
CONV_W = 4
NORM_EPS = 1e-6
L2_EPS = 1e-6
IN_WIDTH = 8208
SHARD_IN = IN_WIDTH // N_DEV
EXP_CLAMP = 80.0

C_QKV, C_Z, C_HQ, C_HF, C_HI, C_HZ, C_B, C_A, N_PROJ = 0, 3072, 4096, 5120, 6144, 7168, 8192, 8320, 8448

ADAM_LR, ADAM_B1, ADAM_B2, ADAM_EPS, ADAM_WD, ADAM_STEP = 0.001, 0.9, 0.999, 1e-08, 0.01, 10

VMEM_LIMIT = 48 * 1024 * 1024


def _cp(*sem):
    return pltpu.CompilerParams(dimension_semantics=sem, vmem_limit_bytes=VMEM_LIMIT)


class _Heads:
    def __init__(self, vals):
        self.v = tuple(vals)

    def __add__(self, o):
        return _hmap(lambda a, b: a + b, self, o)

    def __radd__(self, o):
        return _hmap(lambda a, b: b + a, self, o)

    def __sub__(self, o):
        return _hmap(lambda a, b: a - b, self, o)

    def __rsub__(self, o):
        return _hmap(lambda a, b: b - a, self, o)

    def __mul__(self, o):
        return _hmap(lambda a, b: a * b, self, o)

    def __rmul__(self, o):
        return _hmap(lambda a, b: b * a, self, o)

    def __neg__(self):
        return _hmap(lambda a: -a, self)

    def __getitem__(self, idx):
        return _hmap(lambda a: a[idx], self)


def _hmap(fn, *args):
    n = next((len(a.v) for a in args if isinstance(a, _Heads)), None)
    if n is None:
        return fn(*args)
    return _Heads(fn(*[a.v[i] if isinstance(a, _Heads) else a for a in args]) for i in range(n))


def _dot(a, b, ca, cb):
    return _hmap(lambda x, y: lax.dot_general(x.astype(BF16), y.astype(BF16), (((ca,), (cb,)), ((), ())),
                                              preferred_element_type=F32), a, b)


def _nn(a, b):
    return _dot(a, b, 1, 0)


def _nt(a, b):
    return _dot(a, b, 1, 1)


def _tn(a, b):
    return _dot(a, b, 0, 0)


def _split(a):
    hi = _hmap(lambda x: x.astype(BF16), a)
    return hi, _hmap(lambda x, h: (x - h.astype(F32)).astype(BF16), a, hi)


def _dot3(a, b, ca, cb):
    ah, al = _split(a)
    bh, bl = _split(b)
    return _dot(ah, bh, ca, cb) + (_dot(ah, bl, ca, cb) + _dot(al, bh, ca, cb))


def _nn_exact(a, b):
    return _hmap(lambda y: lax.dot_general(a, y, (((1,), (0,)), ((), ())), precision=HIGHEST,
                                           preferred_element_type=F32), b)


def _exp(x):
    return _hmap(jnp.exp, x)


def _sum(x, axis):
    return _hmap(lambda a: jnp.sum(a, axis=axis, keepdims=True), x)


def _sigmoid(x):
    return jax.nn.sigmoid(x)


def _silu(x):
    return x * _sigmoid(x)


def _dsilu(x):
    s = _sigmoid(x)
    return s * (1.0 + x * (1.0 - s))


def _silu_and_grad(x):
    s = _sigmoid(x)
    return x * s, s * (1.0 + x * (1.0 - s))


def _softplus(x):
    return jnp.maximum(x, 0.0) + jnp.log(1.0 + jnp.exp(-jnp.abs(x)))


def _iota2(n, m, axis):
    return lax.broadcasted_iota(jnp.int32, (n, m), axis)


def _col2row(col, eye):
    return _hmap(lambda c: jnp.sum(eye * c, axis=0, keepdims=True), col)


def _row2col(row, eye):
    return _hmap(lambda r: jnp.sum(eye * r, axis=1, keepdims=True), row)


def _pick_lane(block, lane_idx):
    lane = _iota2(block.shape[0], block.shape[1], 1)
    return jnp.sum(jnp.where(lane == lane_idx, block, 0.0), axis=1, keepdims=True)


MM_TILE_M, MM_TILE_N, MM_TILE_K = 1024, 1408, 2048


def _tile(dim, cap):
    if dim <= cap:
        return dim
    t = cap - cap % 128
    while dim % t:
        t -= 128
    return t


def _mm(a, b, *, mode, out_dtype, res=None, after=(), b_rows=None, tile_m=MM_TILE_M, tile_n=MM_TILE_N, tile_k=MM_TILE_K,
        fused=None, name):
    b_mat_rows = b.shape[0] if b_rows is None else N_DEV * b_rows[1]
    if mode == "nn":
        (m, kd), n = a.shape, b.shape[-1]
        assert kd == b_mat_rows
    elif mode == "nt":
        (m, kd), n = a.shape, b_mat_rows
    else:
        (kd, m), n = a.shape, b.shape[-1]
    tm, tn, tk = _tile(m, tile_m), _tile(n, tile_n), _tile(kd, tile_k)
    assert m % tm == 0 and n % tn == 0 and kd % tk == 0, (m, n, kd, tm, tn, tk)
    nk = kd // tk
    ca, cb = {"nn": (1, 0), "nt": (1, 1), "tn": (0, 0)}[mode]

    kind = None if fused is None else fused[0]
    n_in = 2 + (res is not None) + (0 if fused is None else len(fused) - 1)
    n_out = 1 if fused is None else 2

    def body(*refs):
        a_ref, b_ref = refs[:2]
        r_ref = None if res is None else refs[2]
        extra = refs[2 + (res is not None):n_in]
        outs = refs[-1 - n_out:-1]
        o_ref, acc_ref = outs[0], refs[-1]
        k = pl.program_id(2)

        @pl.when(k == 0)
        def _():
            acc_ref[...] = jnp.zeros_like(acc_ref)

        b_tile = b_ref[...]
        if b_rows is not None:
            b_tile = b_tile.reshape(-1, b_tile.shape[-1])
        acc_ref[...] += _dot(a_ref[...], b_tile, ca, cb)

        @pl.when(k == nk - 1)
        def _():
            out = acc_ref[...]
            if r_ref is not None:
                out = out + r_ref[...].astype(F32)
            if kind == "ple":
                h1_ref, up_ref = extra
                o_ref[...] = out
                outs[1][...] = h1_ref[...] + up_ref[...] * _sigmoid(out)
            elif kind == "rms_bwd":
                h_ref, w_ref, res_ref = extra
                dx, dwt = _rms_bwd_math(h_ref[...], w_ref[...], out)
                o_ref[...] = res_ref[...] + dx

                @pl.when(pl.program_id(0) == 0)
                def _():
                    outs[1][...] = jnp.zeros_like(outs[1])

                outs[1][...] += jnp.sum(dwt, axis=0, keepdims=True)
            else:
                o_ref[...] = out.astype(o_ref.dtype)

    a_spec = pl.BlockSpec((tk, tm), lambda i, j, k: (k, i)) if mode == "tn" else pl.BlockSpec((tm, tk), lambda i, j, k: (i, k))
    if b_rows is None:
        b_spec = pl.BlockSpec((tn, tk), lambda i, j, k: (j, k)) if mode == "nt" else pl.BlockSpec((tk, tn), lambda i, j, k: (k, j))
    else:
        first, count = b_rows
        assert first % count == 0 and mode in ("nn", "nt")
        rb = first // count
        if mode == "nn":
            assert tk == kd
            b_spec = pl.BlockSpec((N_DEV, count, tn), lambda i, j, k: (0, rb, j))
        else:
            assert tn % count == 0
            b_spec = pl.BlockSpec((tn // count, count, tk), lambda i, j, k: (j, rb, k))
    o_spec = pl.BlockSpec((tm, tn), lambda i, j, k: (i, j))
    row_spec = pl.BlockSpec((1, tn), lambda i, j, k: (0, j))
    extra_specs, extra_args, out_specs, out_shape = [], (), o_spec, jax.ShapeDtypeStruct((m, n), out_dtype)
    sem = ("parallel", "parallel", "arbitrary")
    if kind == "ple":
        extra_specs, extra_args = [o_spec, o_spec], tuple(fused[1:])
        out_specs, out_shape = [o_spec, o_spec], [jax.ShapeDtypeStruct((m, n), F32)] * 2
    elif kind == "rms_bwd":
        assert tn == n
        extra_specs, extra_args = [o_spec, row_spec, o_spec], (fused[1], fused[2].reshape(1, n), fused[3])
        out_specs, out_shape = [o_spec, row_spec], [jax.ShapeDtypeStruct((m, n), F32), jax.ShapeDtypeStruct((1, n), F32)]
        sem = ("arbitrary", "arbitrary", "arbitrary")
    in_specs = ([a_spec, b_spec] + ([o_spec] if res is not None else []) + extra_specs
                + [pl.BlockSpec(memory_space=pl.ANY)] * len(after))
    args = (a, b) + ((res,) if res is not None else ()) + extra_args + tuple(after)
    return pl.pallas_call(
        body, name=name, grid=(m // tm, n // tn, nk), in_specs=in_specs, out_specs=out_specs, out_shape=out_shape,
        scratch_shapes=[pltpu.VMEM((tm, tn), F32)], compiler_params=_cp(*sem),
    )(*args)


ROW_TILE = 256


def _rms_fwd(h, w, *, name):
    s, d = h.shape
    tr = min(ROW_TILE, s)

    def body(h_ref, w_ref, o_ref):
        x = h_ref[...]
        r = lax.rsqrt(jnp.mean(x * x, axis=-1, keepdims=True) + NORM_EPS)
        o_ref[...] = (x * r * w_ref[...]).astype(o_ref.dtype)

    return pl.pallas_call(
        body, name=name, grid=(s // tr,),
        in_specs=[pl.BlockSpec((tr, d), lambda i: (i, 0)), pl.BlockSpec((1, d), lambda i: (0, 0))],
        out_specs=pl.BlockSpec((tr, d), lambda i: (i, 0)),
        out_shape=jax.ShapeDtypeStruct((s, d), BF16), compiler_params=_cp("parallel"),
    )(h, w.reshape(1, d))


def _rms_bwd_math(x, w, dy):
    d = x.shape[-1]
    r = lax.rsqrt(jnp.mean(x * x, axis=-1, keepdims=True) + NORM_EPS)
    gw = dy * w
    dx = r * gw - x * ((r * r * r) * (jnp.sum(gw * x, axis=-1, keepdims=True) / d))
    return dx, dy * x * r


def _final_fwd_bwd(h, w, tgt, *, name):
    s, d = h.shape
    tr = min(ROW_TILE, s)

    def body(h_ref, w_ref, t_ref, loss_ref, dh_ref, dw_ref):
        @pl.when(pl.program_id(0) == 0)
        def _():
            loss_ref[...] = jnp.zeros_like(loss_ref)
            dw_ref[...] = jnp.zeros_like(dw_ref)

        x = h_ref[...]
        wv = w_ref[...]
        r = lax.rsqrt(jnp.mean(x * x, axis=-1, keepdims=True) + NORM_EPS)
        err = x * r * wv - t_ref[...]
        row_loss = jnp.mean(err * err, axis=-1, keepdims=True)
        loss_ref[...] += 0.5 * jnp.sum(row_loss, axis=0, keepdims=True)
        dx, dwt = _rms_bwd_math(x, wv, err / d)
        dh_ref[...] = dx
        dw_ref[...] += jnp.sum(dwt, axis=0, keepdims=True)

    row = pl.BlockSpec((tr, d), lambda i: (i, 0))
    vec = pl.BlockSpec((1, d), lambda i: (0, 0))
    return pl.pallas_call(
        body, name=name, grid=(s // tr,), in_specs=[row, vec, row],
        out_specs=[pl.BlockSpec((1, 128), lambda i: (0, 0)), row, vec],
        out_shape=[jax.ShapeDtypeStruct((1, 128), F32), jax.ShapeDtypeStruct((s, d), F32),
                   jax.ShapeDtypeStruct((1, d), F32)],
        compiler_params=_cp("arbitrary"),
    )(h, w.reshape(1, d), tgt)


def _ple_bwd(dh2, gate_pre, up, *, name):
    s, d = dh2.shape
    tr = min(ROW_TILE, s)

    def body(d_ref, g_ref, u_ref, dup_ref, dgp_ref):
        dh = d_ref[...]
        gate = _sigmoid(g_ref[...])
        dup_ref[...] = (dh * gate).astype(BF16)
        dgp_ref[...] = (dh * u_ref[...] * gate * (1.0 - gate)).astype(BF16)

    row = pl.BlockSpec((tr, d), lambda i: (i, 0))
    return pl.pallas_call(body, name=name, grid=(s // tr,), in_specs=[row, row, row], out_specs=[row, row],
                          out_shape=[jax.ShapeDtypeStruct((s, d), BF16)] * 2, compiler_params=_cp("parallel"))(dh2, gate_pre, up)


def _head_norm_fwd(o, z, w):
    return _hmap(lambda x, zz: (x * lax.rsqrt(jnp.mean(x * x, axis=-1, keepdims=True) + NORM_EPS) * w * _silu(zz)).astype(BF16),
                 o, z)


def _head_norm_bwd(o, z, w, dy):
    dos, dzs, dw = [], [], jnp.zeros((1, HEAD_DIM), F32)
    for x, zz, g in zip(o.v, z.v, dy.v):
        r = lax.rsqrt(jnp.mean(x * x, axis=-1, keepdims=True) + NORM_EPS)
        silu_z, dsilu_z = _silu_and_grad(zz)
        don = g * silu_z
        dzs.append((g * (x * r * w) * dsilu_z).astype(BF16))
        gw = don * w
        dos.append(r * gw - x * ((r * r * r) * (jnp.sum(gw * x, axis=-1, keepdims=True) / HEAD_DIM)))
        dw = dw + jnp.sum(don * x * r, axis=0, keepdims=True)
    return _Heads(dos), _Heads(dzs), dw


def _conv_silu(x, w, s):
    row = _iota2(s, x.shape[1], 0)
    c = w[CONV_W - 1:CONV_W, :] * x
    for k in range(1, CONV_W):
        c = c + w[CONV_W - 1 - k:CONV_W - k, :] * jnp.where(row >= k, pltpu.roll(x, k, 0), 0.0)
    return c


def _dn_qkv_fwd(proj, conv_w, *, name):
    s = proj.shape[0]
    nb = 3 * N_HEADS

    def body(x_ref, w_ref, o_ref):
        j = pl.program_id(0)
        sv = _silu(_conv_silu(x_ref[...], w_ref[...], s))
        r = lax.rsqrt(jnp.sum(sv * sv, axis=-1, keepdims=True) + L2_EPS)
        scale = jnp.where(j < N_HEADS, HEAD_DIM ** -0.5, 1.0).astype(F32)
        o_ref[...] = jnp.where(j < 2 * N_HEADS, sv * r * scale, sv)

    return pl.pallas_call(
        body, name=name, grid=(nb,),
        in_specs=[pl.BlockSpec((s, HEAD_DIM), lambda j: (0, j)), pl.BlockSpec((CONV_W, HEAD_DIM), lambda j: (0, j))],
        out_specs=pl.BlockSpec((s, HEAD_DIM), lambda j: (0, j)),
        out_shape=jax.ShapeDtypeStruct((s, 3 * BR_WIDTH), F32), compiler_params=_cp("parallel"),
    )(proj, conv_w)


def _dn_qkv_bwd(proj, conv_w, dqkv, *, name):
    s = proj.shape[0]
    nb = 3 * N_HEADS

    def body(x_ref, w_ref, g_ref, dx_ref, dw_ref):
        j = pl.program_id(0)
        x, w, g = x_ref[...], w_ref[...], g_ref[...]
        c = _conv_silu(x, w, s)
        sv, dsv = _silu_and_grad(c)
        r = lax.rsqrt(jnp.sum(sv * sv, axis=-1, keepdims=True) + L2_EPS)
        scale = jnp.where(j < N_HEADS, HEAD_DIM ** -0.5, 1.0).astype(F32)
        ds_n = scale * (r * g - sv * ((r * r * r) * jnp.sum(g * sv, axis=-1, keepdims=True)))
        dc = jnp.where(j < 2 * N_HEADS, ds_n, g) * dsv
        row = _iota2(s, HEAD_DIM, 0)
        dx = w[CONV_W - 1:CONV_W, :] * dc
        dws = [jnp.sum(dc * x, axis=0, keepdims=True)]
        for k in range(1, CONV_W):
            dc_ahead = jnp.where(row < s - k, pltpu.roll(dc, s - k, 0), 0.0)
            dx = dx + w[CONV_W - 1 - k:CONV_W - k, :] * dc_ahead
            dws.append(jnp.sum(dc_ahead * x, axis=0, keepdims=True))
        dx_ref[...] = dx.astype(BF16)
        for k in range(CONV_W):
            dw_ref[CONV_W - 1 - k:CONV_W - k, :] = dws[k]

    blk = pl.BlockSpec((s, HEAD_DIM), lambda j: (0, j))
    wblk = pl.BlockSpec((CONV_W, HEAD_DIM), lambda j: (0, j))
    return pl.pallas_call(
        body, name=name, grid=(nb,), in_specs=[blk, wblk, blk], out_specs=[blk, wblk],
        out_shape=[jax.ShapeDtypeStruct((s, 3 * BR_WIDTH), BF16), jax.ShapeDtypeStruct((CONV_W, 3 * BR_WIDTH), F32)],
        compiler_params=_cp("parallel"),
    )(proj, conv_w, dqkv)


def _tri(n, kind):
    r, c = _iota2(n, n, 0), _iota2(n, n, 1)
    if kind == "lower":
        return (r >= c).astype(F32)
    if kind == "upper":
        return (r <= c).astype(F32)
    return (r == c).astype(F32)


GATE_TILE = 512


def _dn_gate_fwd(proj, a_log, dt_bias, *, name):
    s = proj.shape[0]
    tr = min(GATE_TILE, s)

    def body(b_ref, a_ref, al_ref, dt_ref, beta_ref, g_ref):
        beta_ref[...] = _sigmoid(b_ref[...])
        g = -jnp.exp(al_ref[...]) * _softplus(a_ref[...] + dt_ref[...])
        low = _tri(CHUNK, "lower")
        for c in range(tr // CHUNK):
            rows = slice(c * CHUNK, (c + 1) * CHUNK)
            g_ref[rows, :] = _nn_exact(low, g[rows, :])

    blk = lambda cb: pl.BlockSpec((tr, HEAD_DIM), lambda i: (i, cb))
    vec = pl.BlockSpec((1, HEAD_DIM), lambda i: (0, 0))
    out = pl.BlockSpec((tr, HEAD_DIM), lambda i: (i, 0))
    return pl.pallas_call(
        body, name=name, grid=(s // tr,), in_specs=[blk(C_B // HEAD_DIM), blk(C_A // HEAD_DIM), vec, vec],
        out_specs=[out, out], out_shape=[jax.ShapeDtypeStruct((s, HEAD_DIM), F32)] * 2, compiler_params=_cp("parallel"),
    )(proj, proj, a_log, dt_bias)


def _dn_gate_bwd(proj, a_log, dt_bias, dbeta, d_g, *, name):
    s = proj.shape[0]
    tr = min(GATE_TILE, s)

    def body(b_ref, a_ref, al_ref, dt_ref, dbeta_ref, dG_ref, db_ref, da_ref, dal_ref, ddt_ref):
        @pl.when(pl.program_id(0) == 0)
        def _():
            dal_ref[...] = jnp.zeros_like(dal_ref)
            ddt_ref[...] = jnp.zeros_like(ddt_ref)

        beta = _sigmoid(b_ref[...])
        db_ref[...] = (dbeta_ref[...] * beta * (1.0 - beta)).astype(BF16)
        pre = a_ref[...] + dt_ref[...]
        neg_ea = -jnp.exp(al_ref[...])
        up = _tri(CHUNK, "upper")
        d_g = dG_ref[...]
        dg = jnp.concatenate([_nn_exact(up, d_g[c * CHUNK:(c + 1) * CHUNK, :]) for c in range(tr // CHUNK)], axis=0)
        da = dg * neg_ea * _sigmoid(pre)
        da_ref[...] = da.astype(BF16)
        ddt_ref[...] += jnp.sum(da, axis=0, keepdims=True)
        dal_ref[...] += jnp.sum(dg * neg_ea * _softplus(pre), axis=0, keepdims=True)

    blk = lambda cb: pl.BlockSpec((tr, HEAD_DIM), lambda i: (i, cb))
    vec = pl.BlockSpec((1, HEAD_DIM), lambda i: (0, 0))
    io = pl.BlockSpec((tr, HEAD_DIM), lambda i: (i, 0))
    return pl.pallas_call(
        body, name=name, grid=(s // tr,),
        in_specs=[blk(C_B // HEAD_DIM), blk(C_A // HEAD_DIM), vec, vec, io, io], out_specs=[io, io, vec, vec],
        out_shape=[jax.ShapeDtypeStruct((s, HEAD_DIM), BF16)] * 2 + [jax.ShapeDtypeStruct((1, HEAD_DIM), F32)] * 2,
        compiler_params=_cp("arbitrary"),
    )(proj, proj, a_log, dt_bias, dbeta, d_g)


def _unit_lower_inverse(a_strict, eye):
    x = -a_strict
    t = x + eye
    p = x
    n = 2
    while n < CHUNK:
        p = _nn(p, p)
        t = t + _nn(t, p)
        n *= 2
    return t


def _rows(*xs):
    return _hmap(lambda *a: jnp.concatenate(a, axis=0), *xs)


def _lanes(*xs):
    return _hmap(lambda *a: jnp.concatenate(a, axis=1), *xs)


def _dn_chunk_common(q, k, v, gc, beta, st, with_qd_state, t_inv=None):
    c, d = CHUNK, HEAD_DIM
    eye = _tri(c, "eye")
    low = _tri(c, "lower")
    strict = low - eye
    grow = _col2row(gc, eye)
    dec = _hmap(lambda g_, gr: low * jnp.exp(low * (g_ - gr)), gc, grow)
    kb = k * beta
    kq = _nt(_rows(kb, q), k)
    a_mat = kq[0:c, :] * dec * strict
    qk = kq[c:2 * c, :] * dec
    if t_inv is None:
        t_inv = _unit_lower_inverse(a_mat, eye)
    e_g = _exp(gc)
    qd = q * e_g
    uw = _nn(t_inv, _lanes(v * beta, kb * e_g))
    u, w = uw[:, 0:d], uw[:, d:2 * d]
    last = (_iota2(c, 1, 0) == c - 1).astype(F32)
    g_last = _sum(gc * last, 0)
    e_t = _exp(g_last - gc)
    kt = k * e_t
    tail = _exp(g_last)
    if with_qd_state:
        ws = _nn(_rows(w, qd), st)
        vn, qds = u - ws[0:c, :], ws[c:2 * c, :]
    else:
        vn, qds = u - _nn(w, st), None
    return dict(eye=eye, low=low, strict=strict, dec=dec, kb=kb, a_mat=a_mat, t_inv=t_inv, e_g=e_g, u=u, w=w, uw=uw,
                qk=qk, qd=qd, qds=qds, last=last, e_t=e_t, kt=kt, tail=tail, vn=vn)


def _dn_chunk_fwd_math(q, k, v, gc, beta, st):
    m = _dn_chunk_common(q, k, v, gc, beta, st, True)
    o = m["qds"] + _nn(m["qk"], m["vn"])
    st2 = st * m["tail"] + _tn(m["kt"], m["vn"])
    return o, st2, m["t_inv"]


def _dn_chunk_bwd_math(q, k, v, gc, beta, st, do, dst2, t_inv=None):
    c, d = CHUNK, HEAD_DIM
    m = _dn_chunk_common(q, k, v, gc, beta, st, False, t_inv)
    eye, low, strict = m["eye"], m["low"], m["strict"]
    dvn = _tn(m["qk"], do) + _nn(m["kt"], dst2)
    dqk = _nt(do, m["vn"]) * low
    both = _rows(do, dvn)
    ds_both = _nt(both, st)
    dqd, dw = ds_both[0:c, :], -ds_both[c:2 * c, :]
    dst = _tn(_rows(m["qd"], -m["w"]), both) + dst2 * m["tail"]
    dkt = _nt(m["vn"], dst2)
    dtail = _sum(_sum(st * dst2, 1), 0)
    dvb_dkg = _tn(m["t_inv"], _lanes(dvn, dw))
    dvb, dkg = dvb_dkg[:, 0:d], dvb_dkg[:, d:2 * d]
    d_a = _nt(dvb_dkg, m["uw"]) * (-strict)
    dkk = d_a * m["dec"]
    dp = dqk * m["dec"]
    dpk = _rows(dp, dkk)
    dq_dkb = _nn(dpk, k)
    dq = dq_dkb[0:c, :] + dqd * m["e_g"]
    dkb = dq_dkb[c:2 * c, :] + dkg * m["e_g"]
    dk = _tn(dpk, _rows(q, m["kb"])) + dkb * beta + dkt * m["e_t"]
    dv = dvb * beta
    dbeta = _sum(dvb * v + dkb * k, 1)
    de_g = _sum(dkg * m["kb"] + dqd * q, 1)
    de_t = _sum(dkt * k, 1)
    mm = d_a * m["a_mat"] + dqk * m["qk"]
    dgc = (_sum(mm, 1) - _row2col(_sum(mm, 0), eye) + de_g * m["e_g"] - de_t * m["e_t"]
           + (_sum(de_t * m["e_t"], 0) + dtail * m["tail"]) * m["last"])
    return dq, dk, dv, dgc, dbeta, dst


def _heads_of(ref):
    return _Heads(ref[:, h * HEAD_DIM:(h + 1) * HEAD_DIM] for h in range(N_HEADS))


def _lanes_of(block):
    return _Heads(_pick_lane(block, h) for h in range(N_HEADS))


def _dn_chunk_fwd(qkv, gcs, beta, proj, norm_w, *, name):
    s = qkv.shape[0]
    n = s // CHUNK

    def body(q_ref, k_ref, v_ref, g_ref, b_ref, z_ref, w_ref, o_ref, st_out_ref, tinv_ref, y_ref, st_ref):
        @pl.when(pl.program_id(0) == 0)
        def _():
            st_ref[...] = jnp.zeros_like(st_ref)

        gblk, bblk = g_ref[...], b_ref[...]
        st = _Heads(st_ref[h] for h in range(N_HEADS))
        o, st2, t_inv = _dn_chunk_fwd_math(_heads_of(q_ref), _heads_of(k_ref), _heads_of(v_ref), _lanes_of(gblk),
                                           _lanes_of(bblk), st)
        y = _head_norm_fwd(o, _heads_of(z_ref), w_ref[...])
        for h in range(N_HEADS):
            st_out_ref[0, h] = st.v[h]
            tinv_ref[0, h] = t_inv.v[h].astype(BF16)
            o_ref[:, h * HEAD_DIM:(h + 1) * HEAD_DIM] = o.v[h]
            y_ref[:, h * HEAD_DIM:(h + 1) * HEAD_DIM] = y.v[h]
            st_ref[h] = st2.v[h]

    blk = lambda off: pl.BlockSpec((CHUNK, BR_WIDTH), lambda c: (c, off))
    sc = pl.BlockSpec((CHUNK, HEAD_DIM), lambda c: (c, 0))
    return pl.pallas_call(
        body, name=name, grid=(n,),
        in_specs=[blk(0), blk(1), blk(2), sc, sc, blk(C_Z // BR_WIDTH), pl.BlockSpec((1, HEAD_DIM), lambda c: (0, 0))],
        out_specs=[blk(0), pl.BlockSpec((1, N_HEADS, HEAD_DIM, HEAD_DIM), lambda c: (c, 0, 0, 0)),
                   pl.BlockSpec((1, N_HEADS, CHUNK, CHUNK), lambda c: (c, 0, 0, 0)), blk(0)],
        out_shape=[jax.ShapeDtypeStruct((s, BR_WIDTH), F32), jax.ShapeDtypeStruct((n, N_HEADS, HEAD_DIM, HEAD_DIM), F32),
                   jax.ShapeDtypeStruct((n, N_HEADS, CHUNK, CHUNK), BF16), jax.ShapeDtypeStruct((s, BR_WIDTH), BF16)],
        scratch_shapes=[pltpu.VMEM((N_HEADS, HEAD_DIM, HEAD_DIM), F32)],
        compiler_params=_cp("arbitrary"),
    )(qkv, qkv, qkv, gcs, beta, proj, norm_w.reshape(1, HEAD_DIM))


def _dn_chunk_bwd(qkv, gcs, beta, states, tinvs, o, proj, norm_w, dy, *, name):
    s = qkv.shape[0]
    n = s // CHUNK

    def body(q_ref, k_ref, v_ref, g_ref, b_ref, st_in_ref, tinv_ref, o_ref, z_ref, w_ref, dy_ref,
             dqkv_ref, dg_ref, dbeta_ref, dz_ref, dw_ref, dst_ref):
        @pl.when(pl.program_id(0) == 0)
        def _():
            dst_ref[...] = jnp.zeros_like(dst_ref)
            dw_ref[...] = jnp.zeros_like(dw_ref)

        do, dz, dw = _head_norm_bwd(_heads_of(o_ref), _heads_of(z_ref), w_ref[...], _heads_of(dy_ref))
        dw_ref[...] += dw

        gblk, bblk = g_ref[...], b_ref[...]
        lane = _iota2(CHUNK, HEAD_DIM, 1)
        dg_all = jnp.zeros((CHUNK, HEAD_DIM), F32)
        dbeta_all = jnp.zeros((CHUNK, HEAD_DIM), F32)
        dq, dk, dv, dgc, dbeta, dst = _dn_chunk_bwd_math(
            _heads_of(q_ref), _heads_of(k_ref), _heads_of(v_ref), _lanes_of(gblk), _lanes_of(bblk),
            _Heads(st_in_ref[0, h] for h in range(N_HEADS)), do,
            _Heads(dst_ref[h] for h in range(N_HEADS)), _Heads(tinv_ref[0, h] for h in range(N_HEADS)))
        for h in range(N_HEADS):
            dz_ref[:, h * HEAD_DIM:(h + 1) * HEAD_DIM] = dz.v[h]
            for part, val in enumerate((dq, dk, dv)):
                c0 = part * BR_WIDTH + h * HEAD_DIM
                dqkv_ref[:, c0:c0 + HEAD_DIM] = val.v[h]
            dg_all = jnp.where(lane == h, dgc.v[h], dg_all)
            dbeta_all = jnp.where(lane == h, dbeta.v[h], dbeta_all)
            dst_ref[h] = dst.v[h]
        dg_ref[...] = dg_all
        dbeta_ref[...] = dbeta_all

    blk = lambda off: pl.BlockSpec((CHUNK, BR_WIDTH), lambda c: (n - 1 - c, off))
    sc = pl.BlockSpec((CHUNK, HEAD_DIM), lambda c: (n - 1 - c, 0))
    vec = pl.BlockSpec((1, HEAD_DIM), lambda c: (0, 0))
    outs = pl.pallas_call(
        body, name=name, grid=(n,),
        in_specs=[blk(0), blk(1), blk(2), sc, sc,
                  pl.BlockSpec((1, N_HEADS, HEAD_DIM, HEAD_DIM), lambda c: (n - 1 - c, 0, 0, 0)),
                  pl.BlockSpec((1, N_HEADS, CHUNK, CHUNK), lambda c: (n - 1 - c, 0, 0, 0)), blk(0), blk(C_Z // BR_WIDTH),
                  vec, blk(0)],
        out_specs=[pl.BlockSpec((CHUNK, 3 * BR_WIDTH), lambda c: (n - 1 - c, 0)), sc, sc, blk(0), vec],
        out_shape=[jax.ShapeDtypeStruct((s, 3 * BR_WIDTH), F32)] + [jax.ShapeDtypeStruct((s, HEAD_DIM), F32)] * 2
        + [jax.ShapeDtypeStruct((s, BR_WIDTH), BF16), jax.ShapeDtypeStruct((1, HEAD_DIM), F32)],
        scratch_shapes=[pltpu.VMEM((N_HEADS, HEAD_DIM, HEAD_DIM), F32)],
        compiler_params=_cp("arbitrary"),
    )(qkv, qkv, qkv, gcs, beta, states, tinvs, o, proj, norm_w.reshape(1, HEAD_DIM), dy)
    return outs


def _hg_chunk_common(q, k, g):
    c, nb = CHUNK, CHUNK // SUB
    e_g = _exp(g)
    qd = q * e_g
    g_last = g[c - 1:c, :]
    e_t = _exp(g_last - g)
    kt = k * e_t
    tail = _exp(g_last)
    g_refs = [g[i * SUB:i * SUB + 1, :] for i in range(nb)]
    g_ref_rows = _hmap(lambda *rows: jnp.concatenate([jnp.broadcast_to(r, (SUB, r.shape[1])) for r in rows], axis=0), *g_refs)
    e_q = _exp(g - g_ref_rows)
    q_sc = q * e_q
    e_k = [_hmap(lambda gr, g_: jnp.exp(jnp.minimum(gr - g_, EXP_CLAMP)), g_refs[i], g) for i in range(nb)]
    k_sc_all = _rows(*[k * e_k[i] for i in range(nb)])
    row_blk = _iota2(c, 1, 0) // SUB
    masks = [(row_blk == i).astype(F32) for i in range(nb)]
    r_all = _nt(q_sc, k_sc_all)
    a_mat = r_all[:, 0:c] * masks[0]
    for i in range(1, nb):
        a_mat = a_mat + r_all[:, i * c:(i + 1) * c] * masks[i]
    a_mat = a_mat * _tri(c, "lower")
    return dict(e_g=e_g, qd=qd, e_t=e_t, kt=kt, tail=tail, q_sc=q_sc, k_sc_all=k_sc_all, e_q=e_q, e_k=e_k, masks=masks,
                a_mat=a_mat)


def _hg_chunk_fwd_math(q, k, v, g, stt):
    m = _hg_chunk_common(q, k, g)
    o = _nt(m["qd"], stt) + _nn(m["a_mat"], v)
    stt2 = stt * m["tail"] + _tn(v, m["kt"])
    return o, stt2


def _hg_chunk_bwd_math(q, k, v, g, stt, do, dstt2):
    c, nb = CHUNK, CHUNK // SUB
    m = _hg_chunk_common(q, k, g)
    stt2 = stt * m["tail"] + _tn(v, m["kt"])
    later = _sum(stt2 * dstt2, 0)
    dqd = _dot3(do, stt, 1, 0)
    dstt = _tn(do, m["qd"]) + dstt2 * m["tail"]
    d_a = _dot3(do, v, 1, 1) * _tri(c, "lower")
    dv = _tn(m["a_mat"], do) + _nt(m["kt"], dstt2)
    dkt = _dot3(v, dstt2, 1, 0)
    d_blk = _lanes(*[d_a * m["masks"][i] for i in range(nb)])
    dq = dqd * m["e_g"] + _dot3(d_blk, m["k_sc_all"], 1, 0) * m["e_q"]
    dks = _dot3(d_blk, m["q_sc"], 0, 0)
    dk = dkt * m["e_t"]
    for i in range(nb):
        dk = dk + dks[i * c:(i + 1) * c, :] * m["e_k"][i]
    db = q * dq - k * dk
    return dq, dk, dv, db, later, dstt


def _hg_chunk_fwd(proj, lb, norm_w, *, name):
    s = proj.shape[0]
    n = s // CHUNK

    def body(hq_ref, hf_ref, v_ref, lb_ref, z_ref, w_ref, o_ref, st_out_ref, q_out, k_out, lf_out, y_ref, st_ref):
        @pl.when(pl.program_id(0) == 0)
        def _():
            st_ref[...] = jnp.zeros_like(st_ref)

        f, lbv = hf_ref[...], lb_ref[...]
        q_all = _silu(hq_ref[...])
        k_all = (1.0 - lbv) * _sigmoid(-f)
        lf_all = jnp.log(lbv + (1.0 - lbv) * _sigmoid(f))
        q_out[...], k_out[...], lf_out[...] = q_all, k_all, lf_all
        st = _Heads(st_ref[h] for h in range(N_HEADS))
        g_all = _nn_exact(_tri(CHUNK, "lower"), lf_all)
        o, st2 = _hg_chunk_fwd_math(_heads_of(q_all), _heads_of(k_all), _heads_of(v_ref), _heads_of(g_all), st)
        y = _head_norm_fwd(o, _heads_of(z_ref), w_ref[...])
        for h in range(N_HEADS):
            st_out_ref[0, h] = st.v[h]
            o_ref[:, h * HEAD_DIM:(h + 1) * HEAD_DIM] = o.v[h]
            y_ref[:, h * HEAD_DIM:(h + 1) * HEAD_DIM] = y.v[h]
            st_ref[h] = st2.v[h]

    blk = lambda off: pl.BlockSpec((CHUNK, BR_WIDTH), lambda c: (c, off))
    return pl.pallas_call(
        body, name=name, grid=(n,),
        in_specs=[blk(C_HQ // BR_WIDTH), blk(C_HF // BR_WIDTH), blk(C_HI // BR_WIDTH), pl.BlockSpec((1, BR_WIDTH), lambda c: (0, 0)),
                  blk(C_HZ // BR_WIDTH), pl.BlockSpec((1, HEAD_DIM), lambda c: (0, 0))],
        out_specs=[blk(0), pl.BlockSpec((1, N_HEADS, HEAD_DIM, HEAD_DIM), lambda c: (c, 0, 0, 0)), blk(0), blk(0), blk(0), blk(0)],
        out_shape=[jax.ShapeDtypeStruct((s, BR_WIDTH), F32), jax.ShapeDtypeStruct((n, N_HEADS, HEAD_DIM, HEAD_DIM), F32)]
        + [jax.ShapeDtypeStruct((s, BR_WIDTH), F32)] * 3 + [jax.ShapeDtypeStruct((s, BR_WIDTH), BF16)],
        scratch_shapes=[pltpu.VMEM((N_HEADS, HEAD_DIM, HEAD_DIM), F32)],
        compiler_params=_cp("arbitrary"),
    )(proj, proj, proj, lb, proj, norm_w.reshape(1, HEAD_DIM))


def _hg_chunk_bwd(proj, lb, qh, kh, lf, states, o, norm_w, dy, *, name):
    s = proj.shape[0]
    n = s // CHUNK

    def body(hq_ref, hf_ref, v_ref, lb_ref, q_ref, k_ref, lf_ref, st_in_ref, o_ref, z_ref, w_ref, dy_ref,
             dhq_ref, dhf_ref, dhi_ref, dz_ref, dlb_ref, dw_ref, dst_ref):
        @pl.when(pl.program_id(0) == 0)
        def _():
            dst_ref[...] = jnp.zeros_like(dst_ref)
            dlb_ref[...] = jnp.zeros_like(dlb_ref)
            dw_ref[...] = jnp.zeros_like(dw_ref)

        do, dz, dw = _head_norm_bwd(_heads_of(o_ref), _heads_of(z_ref), w_ref[...], _heads_of(dy_ref))
        dw_ref[...] += dw

        g_all = _nn_exact(_tri(CHUNK, "lower"), lf_ref[...])
        dq, dk, dv, db, later, dst = _hg_chunk_bwd_math(
            _heads_of(q_ref), _heads_of(k_ref), _heads_of(v_ref), _heads_of(g_all),
            _Heads(st_in_ref[0, h] for h in range(N_HEADS)), do,
            _Heads(dst_ref[h] for h in range(N_HEADS)))
        dlf = _nn_exact(_tri(CHUNK, "upper"), jnp.concatenate(db.v, axis=1)) + jnp.concatenate(later.v, axis=1)
        dq_all, dk_all = jnp.concatenate(dq.v, axis=1), jnp.concatenate(dk.v, axis=1)
        f, lbv = hf_ref[...], lb_ref[...]
        dhq_ref[...] = (dq_all * _dsilu(hq_ref[...])).astype(BF16)
        sp, sn = _sigmoid(f), _sigmoid(-f)
        dlf_over = dlf / (lbv + (1.0 - lbv) * sp)
        dhf_ref[...] = (dlf_over * (1.0 - lbv) * sp * sn - dk_all * (1.0 - lbv) * sn * (1.0 - sn)).astype(BF16)
        dlb_ref[...] += jnp.sum(dlf_over * (1.0 - sp) - dk_all * sn, axis=0, keepdims=True)
        for h in range(N_HEADS):
            dhi_ref[:, h * HEAD_DIM:(h + 1) * HEAD_DIM] = dv.v[h].astype(BF16)
            dz_ref[:, h * HEAD_DIM:(h + 1) * HEAD_DIM] = dz.v[h]
            dst_ref[h] = dst.v[h]

    blk = lambda off: pl.BlockSpec((CHUNK, BR_WIDTH), lambda c: (n - 1 - c, off))
    vec = pl.BlockSpec((1, BR_WIDTH), lambda c: (0, 0))
    wvec = pl.BlockSpec((1, HEAD_DIM), lambda c: (0, 0))
    return pl.pallas_call(
        body, name=name, grid=(n,),
        in_specs=[blk(C_HQ // BR_WIDTH), blk(C_HF // BR_WIDTH), blk(C_HI // BR_WIDTH), vec, blk(0), blk(0), blk(0),
                  pl.BlockSpec((1, N_HEADS, HEAD_DIM, HEAD_DIM), lambda c: (n - 1 - c, 0, 0, 0)), blk(0), blk(C_HZ // BR_WIDTH),
                  wvec, blk(1)],
        out_specs=[blk(0), blk(0), blk(0), blk(0), vec, wvec],
        out_shape=[jax.ShapeDtypeStruct((s, BR_WIDTH), BF16)] * 4 + [jax.ShapeDtypeStruct((1, BR_WIDTH), F32),
                                                                    jax.ShapeDtypeStruct((1, HEAD_DIM), F32)],
        scratch_shapes=[pltpu.VMEM((N_HEADS, HEAD_DIM, HEAD_DIM), F32)],
        compiler_params=_cp("arbitrary"),
    )(proj, proj, proj, lb, qh, kh, lf, states, o, proj, norm_w.reshape(1, HEAD_DIM), dy)


_ANY = pl.BlockSpec(memory_space=pl.ANY)
_MESH = pl.DeviceIdType.MESH


def _all_gather(x_local, *, name, after=()):
    n_after = len(after)

    def body(x_ref, *refs):
        out_ref, send_sems, recv_sems, local_sem = refs[n_after:]
        x, y, c = lax.axis_index("x"), lax.axis_index("y"), lax.axis_index("c")
        me, sibling = (x, y, c), (x, y, 1 - c)
        n1 = (x ^ (1 - c), y ^ c)
        n2 = (x ^ c, y ^ (1 - c))
        dg = (1 - x, 1 - y)

        def piece(ref, p):
            return ref if n_pieces == 1 else ref.at[pl.ds(p * rows, rows)]

        def slot(px, py, pc, p):
            return piece(out_ref.at[4 * px + 2 * py + pc], p)

        def copy(k, p, block, to, src=None):
            return pltpu.make_async_remote_copy(
                src_ref=slot(*block, p) if src is None else piece(src, p), dst_ref=slot(*block, p),
                send_sem=send_sems.at[n_pieces * k + p], recv_sem=recv_sems.at[n_pieces * k + p], device_id=to,
                device_id_type=_MESH)

        mine = pltpu.make_async_copy(x_ref, out_ref.at[4 * x + 2 * y + c], local_sem)
        mine.start()
        sends = []
        for p in range(n_pieces):
            sends += [copy(0, p, me, sibling, src=x_ref), copy(1, p, me, (*n1, c), src=x_ref), copy(2, p, me, (*n2, c), src=x_ref)]
        for cp in sends:
            cp.start()
        for p in range(n_pieces):
            copy(2, p, (*n2, c), me).wait_recv()
            sends += [copy(3, p, (*n2, c), (*n1, c)), copy(5, p, (*n2, c), sibling)]
            sends[-2].start()
            sends[-1].start()
        for p in range(n_pieces):
            copy(1, p, (*n1, c), me).wait_recv()
            sends.append(copy(4, p, (*n1, c), sibling))
            sends[-1].start()
        for p in range(n_pieces):
            copy(3, p, (*dg, c), me).wait_recv()
            sends.append(copy(6, p, (*dg, c), sibling))
            sends[-1].start()
        for p in range(n_pieces):
            copy(0, p, sibling, me).wait_recv()
            copy(4, p, (*n2, 1 - c), me).wait_recv()
            copy(5, p, (*n1, 1 - c), me).wait_recv()
            copy(6, p, (*dg, 1 - c), me).wait_recv()
        for cp in sends:
            cp.wait_send()
        mine.wait()

    n_pieces = 2 if x_local.shape[0] % 16 == 0 else 1
    rows = x_local.shape[0] // n_pieces
    return pl.pallas_call(
        body, name=name, out_shape=jax.ShapeDtypeStruct((N_DEV,) + x_local.shape, x_local.dtype),
        in_specs=[_ANY] * (1 + n_after), out_specs=_ANY,
        scratch_shapes=[pltpu.SemaphoreType.DMA((7 * n_pieces,)), pltpu.SemaphoreType.DMA((7 * n_pieces,)),
                        pltpu.SemaphoreType.DMA],
    )(x_local, *after)


_HBM = pl.BlockSpec(memory_space=pltpu.HBM)
_SEM = pl.BlockSpec(memory_space=pltpu.SEMAPHORE)
_EFFECT = pltpu.SideEffectType.DATAFLOW_SIDE_EFFECTING


def _peers():
    x, y, c = lax.axis_index("x"), lax.axis_index("y"), lax.axis_index("c")
    out = []
    for k in range(1, N_DEV):
        px, py, pc = x ^ ((k >> 2) & 1), y ^ ((k >> 1) & 1), c ^ (k & 1)
        out.append(((px, py, pc), 4 * px + 2 * py + pc))
    return 4 * x + 2 * y + c, out


def _push_copies(src_ref, land_ref, send_sems, recv_sems, broadcast):
    my, peers = _peers()
    pairs = []
    for k, (pos, idx) in enumerate(peers):
        src = src_ref if broadcast else src_ref.at[idx]
        send = pltpu.make_async_remote_copy(src_ref=src, dst_ref=land_ref.at[my], send_sem=send_sems.at[k],
                                            recv_sem=recv_sems.at[k], device_id=pos, device_id_type=_MESH)
        recv = pltpu.make_async_remote_copy(src_ref=src, dst_ref=land_ref.at[idx], send_sem=send_sems.at[k],
                                            recv_sem=recv_sems.at[k], device_id=pos, device_id_type=_MESH)
        pairs.append((send, recv))
    return pairs


def _push_start(src, land, *, broadcast, name, after=()):
    n_after = len(after)

    def body(src_ref, land_ref, *refs):
        send_sems, recv_sems, _, _, token = refs[n_after:]
        for send, _ in _push_copies(src_ref, land_ref, send_sems, recv_sems, broadcast):
            send.start()
        token[...] = jnp.zeros_like(token)

    return pl.pallas_call(
        body, name=name,
        out_shape=(pltpu.SemaphoreType.DMA((N_DEV - 1,)), pltpu.SemaphoreType.DMA((N_DEV - 1,)),
                   pltpu.HBM(src.shape, src.dtype), pltpu.HBM(land.shape, land.dtype), jax.ShapeDtypeStruct((8, 128), F32)),
        in_specs=(_HBM, _HBM) + (_ANY,) * n_after, out_specs=(_SEM, _SEM, _HBM, _HBM, pl.BlockSpec(memory_space=pltpu.VMEM)),
        input_output_aliases={0: 2, 1: 3}, compiler_params=pltpu.CompilerParams(has_side_effects=_EFFECT),
    )(pltpu.with_memory_space_constraint(src, pltpu.HBM), pltpu.with_memory_space_constraint(land, pltpu.HBM), *after)


def _push_wait(handle, after, *, broadcast, name):
    send_sems, recv_sems, src_thru, land_thru, _ = handle

    def body(src_ref, land_ref, send_sems, recv_sems, *rest):
        for send, recv in _push_copies(src_ref, land_ref, send_sems, recv_sems, broadcast):
            send.wait_send()
            recv.wait_recv()

    return pl.pallas_call(
        body, name=name,
        out_shape=(pltpu.HBM(src_thru.shape, src_thru.dtype), pltpu.HBM(land_thru.shape, land_thru.dtype)),
        in_specs=(_HBM, _HBM, _SEM, _SEM) + (_ANY,) * len(after), out_specs=(_HBM, _HBM),
        input_output_aliases={0: 0, 1: 1}, compiler_params=pltpu.CompilerParams(has_side_effects=_EFFECT),
    )(src_thru, land_thru, send_sems, recv_sems, *after)[1]


def _relay_copies(src_ref, land_ref, sems_a, sems_b):
    x, y, c = lax.axis_index("x"), lax.axis_index("y"), lax.axis_index("c")
    slot = lambda px, py, pc: land_ref.at[4 * px + 2 * py + pc]
    chips = [(1 - x, y), (x, 1 - y), (1 - x, 1 - y)]
    (send_a, recv_a), (send_b, recv_b) = sems_a, sems_b

    def copy(sems, k, src, dst_slot, to):
        return pltpu.make_async_remote_copy(src_ref=src, dst_ref=dst_slot, send_sem=sems[0].at[k], recv_sem=sems[1].at[k],
                                            device_id=to, device_id_type=_MESH)

    first = [copy((send_a, recv_a), 0, src_ref, slot(x, y, c), (x, y, 1 - c))]
    first += [copy((send_a, recv_a), 1 + j, src_ref, slot(x, y, c), (*chip, c)) for j, chip in enumerate(chips)]
    first_in = [copy((send_a, recv_a), 0, src_ref, slot(x, y, 1 - c), (x, y, 1 - c))]
    first_in += [copy((send_a, recv_a), 1 + j, src_ref, slot(*chip, c), (*chip, c)) for j, chip in enumerate(chips)]
    relay = [copy((send_b, recv_b), j, slot(*chip, c), slot(*chip, c), (x, y, 1 - c)) for j, chip in enumerate(chips)]
    relay_in = [copy((send_b, recv_b), j, slot(*chip, 1 - c), slot(*chip, 1 - c), (x, y, 1 - c)) for j, chip in enumerate(chips)]
    return first, first_in, relay, relay_in


def _relay_start(src, land, *, name, after=()):
    n_after = len(after)

    def body(src_ref, land_ref, *refs):
        send_a, recv_a, _, _, token = refs[n_after:]
        for cp in _relay_copies(src_ref, land_ref, (send_a, recv_a), (send_a, recv_a))[0]:
            cp.start()
        token[...] = jnp.zeros_like(token)

    send_a, recv_a, src_thru, land_thru, token = pl.pallas_call(
        body, name=name,
        out_shape=(pltpu.SemaphoreType.DMA((4,)), pltpu.SemaphoreType.DMA((4,)), pltpu.HBM(src.shape, src.dtype),
                   pltpu.HBM(land.shape, land.dtype), jax.ShapeDtypeStruct((8, 128), F32)),
        in_specs=(_HBM, _HBM) + (_ANY,) * n_after, out_specs=(_SEM, _SEM, _HBM, _HBM, pl.BlockSpec(memory_space=pltpu.VMEM)),
        input_output_aliases={0: 2, 1: 3}, compiler_params=pltpu.CompilerParams(has_side_effects=_EFFECT),
    )(pltpu.with_memory_space_constraint(src, pltpu.HBM), pltpu.with_memory_space_constraint(land, pltpu.HBM), *after)
    return (send_a, recv_a), src_thru, land_thru, token


def _relay_mid(handle, after, *, name):
    sems_a, src_thru, land_thru, _ = handle
    n_after = len(after)

    def body(src_ref, land_ref, send_a, recv_a, *refs):
        send_b, recv_b, _, _, token = refs[n_after:]
        _, first_in, relay, _ = _relay_copies(src_ref, land_ref, (send_a, recv_a), (send_b, recv_b))
        for j in range(3):
            first_in[1 + j].wait_recv()
            relay[j].start()
        token[...] = jnp.zeros_like(token)

    send_b, recv_b, src2, land2, token = pl.pallas_call(
        body, name=name,
        out_shape=(pltpu.SemaphoreType.DMA((3,)), pltpu.SemaphoreType.DMA((3,)), pltpu.HBM(src_thru.shape, src_thru.dtype),
                   pltpu.HBM(land_thru.shape, land_thru.dtype), jax.ShapeDtypeStruct((8, 128), F32)),
        in_specs=(_HBM, _HBM, _SEM, _SEM) + (_ANY,) * n_after,
        out_specs=(_SEM, _SEM, _HBM, _HBM, pl.BlockSpec(memory_space=pltpu.VMEM)),
        input_output_aliases={0: 2, 1: 3}, compiler_params=pltpu.CompilerParams(has_side_effects=_EFFECT),
    )(src_thru, land_thru, *sems_a, *after)
    return sems_a, (send_b, recv_b), src2, land2, token


def _relay_wait(handle, after, *, name):
    sems_a, sems_b, src_thru, land_thru, _ = handle

    def body(src_ref, land_ref, send_a, recv_a, send_b, recv_b, *rest):
        first, first_in, relay, relay_in = _relay_copies(src_ref, land_ref, (send_a, recv_a), (send_b, recv_b))
        first_in[0].wait_recv()
        for cp in relay_in:
            cp.wait_recv()
        for cp in first + relay:
            cp.wait_send()

    return pl.pallas_call(
        body, name=name,
        out_shape=(pltpu.HBM(src_thru.shape, src_thru.dtype), pltpu.HBM(land_thru.shape, land_thru.dtype)),
        in_specs=(_HBM, _HBM, _SEM, _SEM, _SEM, _SEM) + (_ANY,) * len(after), out_specs=(_HBM, _HBM),
        input_output_aliases={0: 0, 1: 1}, compiler_params=pltpu.CompilerParams(has_side_effects=_EFFECT),
    )(src_thru, land_thru, *sems_a, *sems_b, *after)[1]


def _adamw(parts, row_off, w, m, v, *, layer=0, n_layers=1, prev=None, name, tr):
    rows, c = w.shape
    r = rows // n_layers
    np_ = parts.shape[0]
    tr = min(tr, r)
    assert r % tr == 0 and row_off % tr == 0
    ob, lb = row_off // tr, layer * (r // tr)
    c1 = 1.0 - ADAM_B1 ** ADAM_STEP
    c2 = 1.0 - ADAM_B2 ** ADAM_STEP
    n_prev = 0 if prev is None else 4

    def body(p_ref, w_ref, m_ref, v_ref, *refs):
        g_ref, d_ref, nm_ref, nv_ref = refs[n_prev:]
        g = p_ref[0].astype(F32)
        for s in range(1, np_):
            g = g + p_ref[s].astype(F32)
        wv = w_ref[...]
        m2 = ADAM_B1 * m_ref[...] + (1.0 - ADAM_B1) * g
        v2 = ADAM_B2 * v_ref[...] + (1.0 - ADAM_B2) * jnp.square(g)
        m_hat = m2 / c1
        v_hat = v2 / c2
        g_ref[...] = g
        d_ref[...] = -ADAM_LR * (m_hat / (jnp.sqrt(v_hat) + ADAM_EPS) + ADAM_WD * wv)
        nm_ref[...] = m2
        nv_ref[...] = v2

    blk = pl.BlockSpec((tr, c), lambda i: (lb + i, 0))
    return pl.pallas_call(
        body, name=name, grid=(r // tr,),
        in_specs=[pl.BlockSpec((np_, tr, c), lambda i: (0, ob + i, 0)), blk, blk, blk] + [_ANY] * n_prev,
        out_specs=[blk] * 4, out_shape=[jax.ShapeDtypeStruct((rows, c), F32)] * 4,
        input_output_aliases={4 + i: i for i in range(n_prev)}, compiler_params=_cp("parallel"),
    )(parts, w, m, v, *(prev or ()))


def _sum_parts(parts, *, name, after=()):
    np_, r, c = parts.shape

    def body(p_ref, *refs):
        o_ref = refs[-1]
        g = p_ref[0]
        for s in range(1, np_):
            g = g + p_ref[s]
        o_ref[...] = g

    vmem = pl.BlockSpec(memory_space=pltpu.VMEM)
    return pl.pallas_call(body, name=name, in_specs=[vmem] + [_ANY] * len(after), out_specs=vmem,
                          out_shape=jax.ShapeDtypeStruct((r, c), F32))(parts, *after)


def _pack(arrs):
    rows = []
    for a in arrs:
        f = a.reshape(-1).astype(F32)
        pad = (-f.shape[0]) % 128
        rows.append(jnp.pad(f, (0, pad)).reshape(-1, 128))
    out = jnp.concatenate(rows, axis=0)
    return jnp.pad(out, ((0, (-out.shape[0]) % 8), (0, 0)))


def _unpack(packed, shapes):
    outs, r0 = [], 0
    for shp in shapes:
        n = 1
        for d in shp:
            n *= d
        nr = -(-n // 128)
        outs.append(packed[r0:r0 + nr].reshape(-1)[:n].reshape(shp))
        r0 += nr
    return outs


_WIN_PIECES = ((0, 4096, 0), (4112, 8208, 0), (4096, 4104, HEAD_DIM - N_HEADS), (4104, 4112, HEAD_DIM - N_HEADS))


RELAYOUT_TILE = 256
LAST_SPLIT = 4
OTHER_SPLIT = 2


def _win_from_shards(shards, *, name):
    k = shards.shape[1]
    tr = min(RELAYOUT_TILE, k)

    def body(x_ref, o_ref):
        cols = []
        for lo, hi, pad in _WIN_PIECES:
            for j in range(N_DEV):
                a, b = max(lo, j * SHARD_IN), min(hi, (j + 1) * SHARD_IN)
                if a < b:
                    cols.append(x_ref[j, :, a - j * SHARD_IN:b - j * SHARD_IN])
            if pad:
                cols.append(jnp.zeros((tr, pad), x_ref.dtype))
        o_ref[...] = jnp.concatenate(cols, axis=1)

    return pl.pallas_call(
        body, name=name, grid=(k // tr,), in_specs=[pl.BlockSpec((N_DEV, tr, SHARD_IN), lambda i: (0, i, 0))],
        out_specs=pl.BlockSpec((tr, N_PROJ), lambda i: (i, 0)), out_shape=jax.ShapeDtypeStruct((k, N_PROJ), shards.dtype),
        compiler_params=_cp("parallel"),
    )(shards)


def _win_to_shards(g, *, name):
    k = g.shape[0]
    tr = min(RELAYOUT_TILE, k)
    starts, off = [], 0
    for lo, hi, pad in _WIN_PIECES:
        starts.append((lo, hi, off))
        off += hi - lo + pad

    def body(g_ref, o_ref):
        for j in range(N_DEV):
            cols = []
            for lo, hi, off in sorted(starts):
                a, b = max(lo, j * SHARD_IN), min(hi, (j + 1) * SHARD_IN)
                if a < b:
                    cols.append(g_ref[:, off + a - lo:off + b - lo])
            o_ref[j] = jnp.concatenate(cols, axis=1)

    return pl.pallas_call(
        body, name=name, grid=(k // tr,), in_specs=[pl.BlockSpec((tr, N_PROJ), lambda i: (i, 0))],
        out_specs=pl.BlockSpec((N_DEV, tr, SHARD_IN), lambda i: (0, i, 0)),
        out_shape=jax.ShapeDtypeStruct((N_DEV, k, SHARD_IN), g.dtype), compiler_params=_cp("parallel"),
    )(g)


def _lower_bounds(logits):
    probs = jax.nn.softmax(logits.astype(F32), axis=0)
    return jnp.cumsum(probs, axis=0) - probs[0]


def _pad_lanes(vec8):
    return jnp.pad(vec8.reshape(1, N_HEADS), ((0, 0), (0, HEAD_DIM - N_HEADS)))


def kernel(x, p, norm_w, w_in, dn_conv_w, dn_A_log, dn_dt_bias, dn_norm_w, hg_lb_logits, hg_norm_w, w_out, w_ple_up, w_ple_gate, final_norm_w, loss_target, m_norm_w, m_w_in, m_dn_conv_w, m_dn_A_log, m_dn_dt_bias, m_dn_norm_w, m_hg_lb_logits, m_hg_norm_w, m_w_out, m_w_ple_up, m_w_ple_gate, m_final_norm_w, v_norm_w, v_w_in, v_dn_conv_w, v_dn_A_log, v_dn_dt_bias, v_dn_norm_w, v_hg_lb_logits, v_hg_norm_w, v_w_out, v_w_ple_up, v_w_ple_gate, v_final_norm_w):
    depth = norm_w.shape[0]
    my = 4 * lax.axis_index("x") + 2 * lax.axis_index("y") + lax.axis_index("c")
    h = x[0]
    tgt = loss_target[0]
    rows_out = D_MODEL // N_DEV
    up_rows = PLE_DIM * (D_MODEL // N_DEV) // D_MODEL
    g_off, u_off = rows_out, 2 * rows_out

    def own_slot(block):
        return lax.dynamic_update_index_in_dim(lax.empty((N_DEV,) + block.shape, block.dtype), block, my, 0)

    win_bf = w_in.astype(BF16)
    rest_bf = [jnp.concatenate([w_out[l], w_ple_gate[l], w_ple_up[l].reshape(up_rows, D_MODEL)], axis=0).astype(BF16)
               for l in range(depth)]
    conv_all = _all_gather(dn_conv_w, name="gather_conv_w")
    conv_full = conv_all.transpose(1, 2, 0, 3).reshape(depth, CONV_W, 3 * BR_WIDTH)
    win_all = {0: _all_gather(win_bf[0], name="gather_w_in_l0", after=[conv_all])}
    pending, relayed = {}, {}
    last = win_all[0]
    for l in range(depth):
        if l > 0:
            relayed["win", l] = _relay_start(win_bf[l], own_slot(win_bf[l]), after=[last], name=f"gather_w_in_l{l}_first")
            last = relayed["win", l][3]
        if l == 0:
            relayed["rest", l] = _relay_start(rest_bf[l], own_slot(rest_bf[l]), after=[last], name=f"gather_rest_l{l}_first")
            last = relayed["rest", l][3]
        else:
            pending["rest", l] = _push_start(rest_bf[l], own_slot(rest_bf[l]), broadcast=True, after=[last],
                                             name=f"gather_rest_l{l}_start")
            last = pending["rest", l][4]
    order_tok = last[0, 0]
    lbs = _lower_bounds(hg_lb_logits)

    saved = []
    weights = []
    for l in range(depth):
        tag = f"l{l}"
        if l > 0:
            win_all[l] = _relay_wait(relayed["win", l], [h], name=f"gather_w_in_{tag}_wait")
        wi = _win_from_shards(win_all[l], name=f"w_in_layout_{tag}")
        nw = norm_w[l] + order_tok if l == 0 else norm_w[l]
        hn = _rms_fwd(h, nw, name=f"rms_fwd_{tag}")
        proj = _mm(hn, wi, mode="nn", out_dtype=F32, name=f"mm_proj_{tag}")
        al, dt = _pad_lanes(dn_A_log[l]), _pad_lanes(dn_dt_bias[l])
        qkv = _dn_qkv_fwd(proj, conv_full[l], name=f"dn_qkv_fwd_{tag}")
        if ("rest", l) in relayed:
            relayed["rest", l] = _relay_mid(relayed["rest", l], [qkv], name=f"gather_rest_{tag}_relay")
            al = al + relayed["rest", l][4][0, 0]
        beta, gcs = _dn_gate_fwd(proj, al, dt, name=f"dn_gate_fwd_{tag}")
        o_dn, st_dn, tinv_dn, y_dn = _dn_chunk_fwd(qkv, gcs, beta, proj, dn_norm_w[l], name=f"dn_chunk_fwd_{tag}")
        lb = lbs[l].reshape(1, BR_WIDTH)
        o_hg, st_hg, qh, kh, lf, y_hg = _hg_chunk_fwd(proj, lb, hg_norm_w[l], name=f"hg_chunk_fwd_{tag}")
        y = jnp.concatenate([y_dn, y_hg], axis=1)
        if ("rest", l) in relayed:
            rest_all = _relay_wait(relayed["rest", l], [y], name=f"gather_rest_{tag}_wait")
        else:
            rest_all = _push_wait(pending["rest", l], [y], broadcast=True, name=f"gather_rest_{tag}_wait")
        w_out_rows, w_gate_rows = (0, rows_out), (g_off, rows_out)
        wu = rest_all[:, u_off:u_off + up_rows].reshape(N_DEV, PLE_DIM, D_MODEL // N_DEV).transpose(1, 0, 2).reshape(PLE_DIM, D_MODEL)
        weights.append((wi, rest_all, wu))
        h1 = _mm(y, rest_all, mode="nn", b_rows=w_out_rows, out_dtype=F32, res=h, name=f"mm_out_{tag}")
        pin = []
        if ("win", l + 1) in relayed:
            relayed["win", l + 1] = _relay_mid(relayed["win", l + 1], [h1], name=f"gather_w_in_l{l + 1}_relay")
            pin = [relayed["win", l + 1][4]]
        up = _mm(p[l, 0], wu, mode="nn", out_dtype=F32, name=f"mm_up_{tag}")
        gp, h2 = _mm(h1, rest_all, mode="nn", b_rows=w_gate_rows, out_dtype=F32, after=pin, tile_m=512, fused=("ple", h1, up),
                     name=f"mm_gate_{tag}")
        saved.append(dict(h=h, hn=hn, proj=proj, qkv=qkv, beta=beta, gcs=gcs, st_dn=st_dn, tinv_dn=tinv_dn, qh=qh, kh=kh, lf=lf,
                          st_hg=st_hg, o_dn=o_dn, o_hg=o_hg, y=y, h1=h1, gp=gp, up=up, al=al, dt=dt, lb=lb))
        h = h2

    loss_row, dh, d_final_w = _final_fwd_bwd(h, final_norm_w, tgt, name="final_norm_loss")

    d_norm_w, d_alog, d_dt, d_dn_nw, d_hg_nw, d_lb, d_conv = ([None] * depth for _ in range(7))
    sent = {}
    for l in reversed(range(depth)):
        wi, rest_all, wu = weights[l]
        sv = saved[l]
        tag = f"l{l}"
        dup, dgp = _ple_bwd(dh, sv["gp"], sv["up"], name=f"ple_bwd_{tag}")
        d_wu = _mm(p[l, 0], dup, mode="tn", out_dtype=BF16, name=f"mm_dwup_{tag}")
        d_wg = _mm(sv["h1"], dgp, mode="tn", out_dtype=BF16, name=f"mm_dwgate_{tag}")
        dh1 = _mm(dgp, rest_all, mode="nt", b_rows=(g_off, rows_out), out_dtype=F32, res=dh, name=f"mm_dh1_{tag}")
        d_wo = _mm(sv["y"], dh1, mode="tn", out_dtype=BF16, name=f"mm_dwout_{tag}")
        parts_rest = jnp.concatenate(
            [d_wo.reshape(N_DEV, rows_out, D_MODEL), d_wg.reshape(N_DEV, rows_out, D_MODEL),
             d_wu.reshape(PLE_DIM, N_DEV, D_MODEL // N_DEV).transpose(1, 0, 2).reshape(N_DEV, up_rows, D_MODEL)], axis=1)
        sent["rest", l] = _push_start(parts_rest, own_slot(parts_rest[my]), broadcast=False, name=f"exchange_rest_{tag}_start")
        dy = _mm(dh1, rest_all, mode="nt", b_rows=(0, rows_out), out_dtype=F32, name=f"mm_dy_{tag}")
        dn_nw = dn_norm_w[l] + sent["rest", l][4][0, 0]
        dqkv, d_gc, dbeta, dz_dn, d_dn_nw[l] = _dn_chunk_bwd(sv["qkv"], sv["gcs"], sv["beta"], sv["st_dn"], sv["tinv_dn"],
                                                             sv["o_dn"], sv["proj"], dn_nw, dy, name=f"dn_chunk_bwd_{tag}")
        dqkv_pre, d_conv[l] = _dn_qkv_bwd(sv["proj"], conv_full[l], dqkv, name=f"dn_qkv_bwd_{tag}")
        db, da, d_alog[l], d_dt[l] = _dn_gate_bwd(sv["proj"], sv["al"], sv["dt"], dbeta, d_gc, name=f"dn_gate_bwd_{tag}")
        dhq, dhf, dhi, dz_hg, d_lb[l], d_hg_nw[l] = _hg_chunk_bwd(sv["proj"], sv["lb"], sv["qh"], sv["kh"], sv["lf"], sv["st_hg"],
                                                                  sv["o_hg"], hg_norm_w[l], dy, name=f"hg_chunk_bwd_{tag}")
        dproj = jnp.concatenate([dqkv_pre, dz_dn, dhq, dhf, dhi, dz_hg, db, da], axis=1)
        def push_d_win(after):
            n_split = LAST_SPLIT if l == 0 else OTHER_SPLIT
            rows = D_MODEL // n_split
            handles = []
            for q in range(n_split):
                hn_q = sv["hn"] if n_split == 1 else sv["hn"][:, q * rows:(q + 1) * rows]
                sfx = tag if n_split == 1 else f"{tag}_{q}"
                d_win = _mm(hn_q, dproj, mode="tn", out_dtype=BF16, after=after, name=f"mm_dwin_{sfx}")
                parts_in = _win_to_shards(d_win, name=f"dw_in_shards_{sfx}")
                handles.append(_push_start(parts_in, own_slot(parts_in[my]), broadcast=False, after=after,
                                           name=f"exchange_w_in_{sfx}_start"))
                after = [handles[-1][4]]
            return handles

        if l == 0:
            small = _pack([loss_row, jnp.concatenate(d_norm_w[1:], axis=0), d_final_w,
                           jnp.stack([a[0, :N_HEADS] for a in d_alog]), jnp.stack([a[0, :N_HEADS] for a in d_dt]),
                           jnp.concatenate(d_dn_nw, axis=0), jnp.concatenate(d_hg_nw, axis=0), jnp.concatenate(d_lb, axis=0),
                           jnp.stack(d_conv)])
            small_all = _all_gather(small, name="gather_small")
        sent["win", l] = push_d_win([small_all] if l == 0 else [])
        dh, d_norm_w[l] = _mm(dproj, wi, mode="nt", out_dtype=F32, tile_m=512, tile_n=D_MODEL, tile_k=768,
                              after=[sent["win", l][-1][4]],
                              fused=("rms_bwd", sv["h"], norm_w[l], dh1), name=f"mm_dhn_{tag}")
    grad_x = dh[None]

    small_shapes = [(1, 128), (depth - 1, D_MODEL), final_norm_w.shape, dn_A_log.shape, dn_dt_bias.shape, dn_norm_w.shape,
                    hg_norm_w.shape, hg_lb_logits.shape, (depth, CONV_W, 3 * BR_WIDTH)]
    tot = _unpack(_sum_parts(small_all, after=[grad_x], name="sum_small"), small_shapes)
    loss = tot[0][0, 0]
    g_lb = tot[7]
    g_logits = jax.vjp(_lower_bounds, hg_lb_logits)[1](g_lb)[0]
    g_conv = lax.dynamic_slice_in_dim(tot[8], my * (3 * BR_WIDTH // N_DEV), 3 * BR_WIDTH // N_DEV, axis=2)
    small_g = [g_conv, tot[3], tot[4], tot[5], g_logits, tot[6], tot[2]]
    small_w = [dn_conv_w, dn_A_log, dn_dt_bias, dn_norm_w, hg_lb_logits, hg_norm_w, final_norm_w]
    small_m = [m_dn_conv_w, m_dn_A_log, m_dn_dt_bias, m_dn_norm_w, m_hg_lb_logits, m_hg_norm_w, m_final_norm_w]
    small_v = [v_dn_conv_w, v_dn_A_log, v_dn_dt_bias, v_dn_norm_w, v_hg_lb_logits, v_hg_norm_w, v_final_norm_w]
    pk_w = _pack(small_w)
    res_small = _adamw(_pack(small_g)[None], 0, pk_w, _pack(small_m), _pack(small_v), name="adamw_small", tr=pk_w.shape[0])
    shapes_w = [a.shape for a in small_w]
    sg, sd, sm, sv_ = (_unpack(r, shapes_w) for r in res_small)

    r_win = r_wo = r_wg = r_wu = None
    done = [grad_x, res_small[0]]

    def flat(a, cols):
        return a.reshape(-1, cols)

    for l in reversed(range(depth)):
        tag = f"l{l}"
        land_rest = _push_wait(sent["rest", l], done, broadcast=False, name=f"exchange_rest_{tag}_wait")
        r_wo = _adamw(land_rest, 0, flat(w_out, D_MODEL), flat(m_w_out, D_MODEL), flat(v_w_out, D_MODEL), layer=l,
                      n_layers=depth, prev=r_wo, name=f"adamw_w_out_{tag}", tr=rows_out)
        r_wg = _adamw(land_rest, g_off, flat(w_ple_gate, D_MODEL), flat(m_w_ple_gate, D_MODEL), flat(v_w_ple_gate, D_MODEL),
                      layer=l, n_layers=depth, prev=r_wg, name=f"adamw_w_gate_{tag}", tr=rows_out)
        r_wu = _adamw(land_rest, u_off, flat(w_ple_up, D_MODEL), flat(m_w_ple_up, D_MODEL), flat(v_w_ple_up, D_MODEL),
                      layer=l, n_layers=depth, prev=r_wu, name=f"adamw_w_up_{tag}", tr=up_rows)
        done = [r_wo[0], r_wg[0], r_wu[0]]
    for l in reversed(range(depth)):
        tag = f"l{l}"
        if l == 0:
            nw0 = _sum_parts(_all_gather(_pack([d_norm_w[0]]), after=done, name="gather_norm_w"), name="sum_norm_w")
            g_norm_w = jnp.concatenate([_unpack(nw0, [(1, D_MODEL)])[0], tot[1]], axis=0)
            pk_nw = _pack([norm_w])
            r_nw = _adamw(_pack([g_norm_w])[None], 0, pk_nw, _pack([m_norm_w]), _pack([v_norm_w]), name="adamw_norm_w",
                          tr=pk_nw.shape[0])
            r_nw = [_unpack(r, [norm_w.shape])[0] for r in r_nw]
            done = [r_nw[0]]
        n_split = len(sent["win", l])
        for q, handle in enumerate(sent["win", l]):
            sfx = tag if n_split == 1 else f"{tag}_{q}"
            land_in = _push_wait(handle, done, broadcast=False, name=f"exchange_w_in_{sfx}_wait")
            r_win = _adamw(land_in, 0, flat(w_in, SHARD_IN), flat(m_w_in, SHARD_IN), flat(v_w_in, SHARD_IN),
                           layer=l * n_split + q, n_layers=depth * n_split, prev=r_win, name=f"adamw_w_in_{sfx}", tr=256)
            done = [r_win[0]]
    r_win = [o.reshape(w_in.shape) for o in r_win]
    r_wo = [o.reshape(w_out.shape) for o in r_wo]
    r_wg = [o.reshape(w_ple_gate.shape) for o in r_wg]
    r_wu = [o.reshape(w_ple_up.shape) for o in r_wu]

    def order(nw, small_list, big_in, big_out, big_up, big_gate):
        cw, al_, dt_, dnw, lbl, hnw, fw = small_list
        return [nw, big_in, cw, al_, dt_, dnw, lbl, hnw, big_out, big_up, big_gate, fw]

    outs = [loss, grad_x]
    for i, sl in enumerate((sg, sd, sm, sv_)):
        outs += order(r_nw[i], sl, r_win[i], r_wo[i], r_wu[i], r_wg[i])
    return tuple(outs)
```

```python
import functools

import jax
import jax.numpy as jnp
from jax import lax
from jax.experimental import pallas as pl
from jax.experimental.pallas import tpu as pltpu

F32 = jnp.float32
BF16 = jnp.bfloat16
HIGHEST = lax.Precision.HIGHEST

N_DEV = 8
D_MODEL = 2048
PLE_DIM = 256
HEAD_DIM = 128
N_HEADS = 8
BR_WIDTH = N_HEADS * HEAD_DIM
CHUNK = 64
SUB = 16
CONV_W = 4
NORM_EPS = 1e-6
L2_EPS = 1e-6
IN_WIDTH = 8208
SHARD_IN = IN_WIDTH // N_DEV
EXP_CLAMP = 80.0

C_QKV, C_Z, C_HQ, C_HF, C_HI, C_HZ, C_B, C_A, N_PROJ = 0, 3072, 4096, 5120, 6144, 7168, 8192, 8320, 8448

ADAM_LR, ADAM_B1, ADAM_B2, ADAM_EPS, ADAM_WD, ADAM_STEP = 0.001, 0.9, 0.999, 1e-08, 0.01, 10

VMEM_LIMIT = 48 * 1024 * 1024


def _cp(*sem):
    return pltpu.CompilerParams(dimension_semantics=sem, vmem_limit_bytes=VMEM_LIMIT)


class _Heads:
    def __init__(self, vals):
        self.v = tuple(vals)

    def __add__(self, o):
        return _hmap(lambda a, b: a + b, self, o)

    def __radd__(self, o):
        return _hmap(lambda a, b: b + a, self, o)

    def __sub__(self, o):
        return _hmap(lambda a, b: a - b, self, o)

    def __rsub__(self, o):
        return _hmap(lambda a, b: b - a, self, o)

    def __mul__(self, o):
        return _hmap(lambda a, b: a * b, self, o)

    def __rmul__(self, o):
        return _hmap(lambda a, b: b * a, self, o)

    def __neg__(self):
        return _hmap(lambda a: -a, self)

    def __getitem__(self, idx):
        return _hmap(lambda a: a[idx], self)


def _hmap(fn, *args):
    n = next((len(a.v) for a in args if isinstance(a, _Heads)), None)
    if n is None:
        return fn(*args)
    return _Heads(fn(*[a.v[i] if isinstance(a, _Heads) else a for a in args]) for i in range(n))


def _dot(a, b, ca, cb):
    return _hmap(lambda x, y: lax.dot_general(x.astype(BF16), y.astype(BF16), (((ca,), (cb,)), ((), ())),
                                              preferred_element_type=F32), a, b)


def _nn(a, b):
    return _dot(a, b, 1, 0)


def _nt(a, b):
    return _dot(a, b, 1, 1)


def _tn(a, b):
    return _dot(a, b, 0, 0)


def _split(a):
    hi = _hmap(lambda x: x.astype(BF16), a)
    return hi, _hmap(lambda x, h: (x - h.astype(F32)).astype(BF16), a, hi)


def _dot3(a, b, ca, cb):
    ah, al = _split(a)
    bh, bl = _split(b)
    return _dot(ah, bh, ca, cb) + (_dot(ah, bl, ca, cb) + _dot(al, bh, ca, cb))


def _nn_exact(a, b):
    return _hmap(lambda y: lax.dot_general(a, y, (((1,), (0,)), ((), ())), precision=HIGHEST,
                                           preferred_element_type=F32), b)


def _exp(x):
    return _hmap(jnp.exp, x)


def _sum(x, axis):
    return _hmap(lambda a: jnp.sum(a, axis=axis, keepdims=True), x)


def _sigmoid(x):
    return jax.nn.sigmoid(x)


def _silu(x):
    return x * _sigmoid(x)


def _dsilu(x):
    s = _sigmoid(x)
    return s * (1.0 + x * (1.0 - s))


def _silu_and_grad(x):
    s = _sigmoid(x)
    return x * s, s * (1.0 + x * (1.0 - s))


def _softplus(x):
    return jnp.maximum(x, 0.0) + jnp.log(1.0 + jnp.exp(-jnp.abs(x)))


def _iota2(n, m, axis):
    return lax.broadcasted_iota(jnp.int32, (n, m), axis)


def _col2row(col, eye):
    return _hmap(lambda c: jnp.sum(eye * c, axis=0, keepdims=True), col)


def _row2col(row, eye):
    return _hmap(lambda r: jnp.sum(eye * r, axis=1, keepdims=True), row)


def _pick_lane(block, lane_idx):
    lane = _iota2(block.shape[0], block.shape[1], 1)
    return jnp.sum(jnp.where(lane == lane_idx, block, 0.0), axis=1, keepdims=True)


MM_TILE_M, MM_TILE_N, MM_TILE_K = 1024, 1408, 2048


def _tile(dim, cap):
    if dim <= cap:
        return dim
    t = cap - cap % 128
    while dim % t:
        t -= 128
    return t


def _mm(a, b, *, mode, out_dtype, res=None, after=(), b_rows=None, tile_m=MM_TILE_M, tile_n=MM_TILE_N, tile_k=MM_TILE_K,
        fused=None, name):
    b_mat_rows = b.shape[0] if b_rows is None else N_DEV * b_rows[1]
    if mode == "nn":
        (m, kd), n = a.shape, b.shape[-1]
        assert kd == b_mat_rows
    elif mode == "nt":
        (m, kd), n = a.shape, b_mat_rows
    else:
        (kd, m), n = a.shape, b.shape[-1]
    tm, tn, tk = _tile(m, tile_m), _tile(n, tile_n), _tile(kd, tile_k)
    assert m % tm == 0 and n % tn == 0 and kd % tk == 0, (m, n, kd, tm, tn, tk)
    nk = kd // tk
    ca, cb = {"nn": (1, 0), "nt": (1, 1), "tn": (0, 0)}[mode]

    kind = None if fused is None else fused[0]
    n_in = 2 + (res is not None) + (0 if fused is None else len(fused) - 1)
    n_out = 1 if fused is None else 2

    def body(*refs):
        a_ref, b_ref = refs[:2]
        r_ref = None if res is None else refs[2]
        extra = refs[2 + (res is not None):n_in]
        outs = refs[-1 - n_out:-1]
        o_ref, acc_ref = outs[0], refs[-1]
        k = pl.program_id(2)

        @pl.when(k == 0)
        def _():
            acc_ref[...] = jnp.zeros_like(acc_ref)

        b_tile = b_ref[...]
        if b_rows is not None:
            b_tile = b_tile.reshape(-1, b_tile.shape[-1])
        acc_ref[...] += _dot(a_ref[...], b_tile, ca, cb)

        @pl.when(k == nk - 1)
        def _():
            out = acc_ref[...]
            if r_ref is not None:
                out = out + r_ref[...].astype(F32)
            if kind == "ple":
                h1_ref, up_ref = extra
                o_ref[...] = out
                outs[1][...] = h1_ref[...] + up_ref[...] * _sigmoid(out)
            elif kind == "rms_bwd":
                h_ref, w_ref, res_ref = extra
                dx, dwt = _rms_bwd_math(h_ref[...], w_ref[...], out)
                o_ref[...] = res_ref[...] + dx

                @pl.when(pl.program_id(0) == 0)
                def _():
                    outs[1][...] = jnp.zeros_like(outs[1])

                outs[1][...] += jnp.sum(dwt, axis=0, keepdims=True)
            else:
                o_ref[...] = out.astype(o_ref.dtype)

    a_spec = pl.BlockSpec((tk, tm), lambda i, j, k: (k, i)) if mode == "tn" else pl.BlockSpec((tm, tk), lambda i, j, k: (i, k))
    if b_rows is None:
        b_spec = pl.BlockSpec((tn, tk), lambda i, j, k: (j, k)) if mode == "nt" else pl.BlockSpec((tk, tn), lambda i, j, k: (k, j))
    else:
        first, count = b_rows
        assert first % count == 0 and mode in ("nn", "nt")
        rb = first // count
        if mode == "nn":
            assert tk == kd
            b_spec = pl.BlockSpec((N_DEV, count, tn), lambda i, j, k: (0, rb, j))
        else:
            assert tn % count == 0
            b_spec = pl.BlockSpec((tn // count, count, tk), lambda i, j, k: (j, rb, k))
    o_spec = pl.BlockSpec((tm, tn), lambda i, j, k: (i, j))
    row_spec = pl.BlockSpec((1, tn), lambda i, j, k: (0, j))
    extra_specs, extra_args, out_specs, out_shape = [], (), o_spec, jax.ShapeDtypeStruct((m, n), out_dtype)
    sem = ("parallel", "parallel", "arbitrary")
    if kind == "ple":
        extra_specs, extra_args = [o_spec, o_spec], tuple(fused[1:])
        out_specs, out_shape = [o_spec, o_spec], [jax.ShapeDtypeStruct((m, n), F32)] * 2
    elif kind == "rms_bwd":
        assert tn == n
        extra_specs, extra_args = [o_spec, row_spec, o_spec], (fused[1], fused[2].reshape(1, n), fused[3])
        out_specs, out_shape = [o_spec, row_spec], [jax.ShapeDtypeStruct((m, n), F32), jax.ShapeDtypeStruct((1, n), F32)]
        sem = ("arbitrary", "arbitrary", "arbitrary")
    in_specs = ([a_spec, b_spec] + ([o_spec] if res is not None else []) + extra_specs
                + [pl.BlockSpec(memory_space=pl.ANY)] * len(after))
    args = (a, b) + ((res,) if res is not None else ()) + extra_args + tuple(after)
    return pl.pallas_call(
        body, name=name, grid=(m // tm, n // tn, nk), in_specs=in_specs, out_specs=out_specs, out_shape=out_shape,
        scratch_shapes=[pltpu.VMEM((tm, tn), F32)], compiler_params=_cp(*sem),
    )(*args)


ROW_TILE = 256


def _rms_fwd(h, w, *, name):
    s, d = h.shape
    tr = min(ROW_TILE, s)

    def body(h_ref, w_ref, o_ref):
        x = h_ref[...]
        r = lax.rsqrt(jnp.mean(x * x, axis=-1, keepdims=True) + NORM_EPS)
        o_ref[...] = (x * r * w_ref[...]).astype(o_ref.dtype)

    return pl.pallas_call(
        body, name=name, grid=(s // tr,),
        in_specs=[pl.BlockSpec((tr, d), lambda i: (i, 0)), pl.BlockSpec((1, d), lambda i: (0, 0))],
        out_specs=pl.BlockSpec((tr, d), lambda i: (i, 0)),
        out_shape=jax.ShapeDtypeStruct((s, d), BF16), compiler_params=_cp("parallel"),
    )(h, w.reshape(1, d))


def _rms_bwd_math(x, w, dy):
    d = x.shape[-1]
    r = lax.rsqrt(jnp.mean(x * x, axis=-1, keepdims=True) + NORM_EPS)
    gw = dy * w
    dx = r * gw - x * ((r * r * r) * (jnp.sum(gw * x, axis=-1, keepdims=True) / d))
    return dx, dy * x * r


def _final_fwd_bwd(h, w, tgt, *, name):
    s, d = h.shape
    tr = min(ROW_TILE, s)

    def body(h_ref, w_ref, t_ref, loss_ref, dh_ref, dw_ref):
        @pl.when(pl.program_id(0) == 0)
        def _():
            loss_ref[...] = jnp.zeros_like(loss_ref)
            dw_ref[...] = jnp.zeros_like(dw_ref)

        x = h_ref[...]
        wv = w_ref[...]
        r = lax.rsqrt(jnp.mean(x * x, axis=-1, keepdims=True) + NORM_EPS)
        err = x * r * wv - t_ref[...]
        row_loss = jnp.mean(err * err, axis=-1, keepdims=True)
        loss_ref[...] += 0.5 * jnp.sum(row_loss, axis=0, keepdims=True)
        dx, dwt = _rms_bwd_math(x, wv, err / d)
        dh_ref[...] = dx
        dw_ref[...] += jnp.sum(dwt, axis=0, keepdims=True)

    row = pl.BlockSpec((tr, d), lambda i: (i, 0))
    vec = pl.BlockSpec((1, d), lambda i: (0, 0))
    return pl.pallas_call(
        body, name=name, grid=(s // tr,), in_specs=[row, vec, row],
        out_specs=[pl.BlockSpec((1, 128), lambda i: (0, 0)), row, vec],
        out_shape=[jax.ShapeDtypeStruct((1, 128), F32), jax.ShapeDtypeStruct((s, d), F32),
                   jax.ShapeDtypeStruct((1, d), F32)],
        compiler_params=_cp("arbitrary"),
    )(h, w.reshape(1, d), tgt)


def _ple_bwd(dh2, gate_pre, up, *, name):
    s, d = dh2.shape
    tr = min(ROW_TILE, s)

    def body(d_ref, g_ref, u_ref, dup_ref, dgp_ref):
        dh = d_ref[...]
        gate = _sigmoid(g_ref[...])
        dup_ref[...] = (dh * gate).astype(BF16)
        dgp_ref[...] = (dh * u_ref[...] * gate * (1.0 - gate)).astype(BF16)

    row = pl.BlockSpec((tr, d), lambda i: (i, 0))
    return pl.pallas_call(body, name=name, grid=(s // tr,), in_specs=[row, row, row], out_specs=[row, row],
                          out_shape=[jax.ShapeDtypeStruct((s, d), BF16)] * 2, compiler_params=_cp("parallel"))(dh2, gate_pre, up)


def _head_norm_fwd(o, z, w):
    return _hmap(lambda x, zz: (x * lax.rsqrt(jnp.mean(x * x, axis=-1, keepdims=True) + NORM_EPS) * w * _silu(zz)).astype(BF16),
                 o, z)


def _head_norm_bwd(o, z, w, dy):
    dos, dzs, dw = [], [], jnp.zeros((1, HEAD_DIM), F32)
    for x, zz, g in zip(o.v, z.v, dy.v):
        r = lax.rsqrt(jnp.mean(x * x, axis=-1, keepdims=True) + NORM_EPS)
        silu_z, dsilu_z = _silu_and_grad(zz)
        don = g * silu_z
        dzs.append((g * (x * r * w) * dsilu_z).astype(BF16))
        gw = don * w
        dos.append(r * gw - x * ((r * r * r) * (jnp.sum(gw * x, axis=-1, keepdims=True) / HEAD_DIM)))
        dw = dw + jnp.sum(don * x * r, axis=0, keepdims=True)
    return _Heads(dos), _Heads(dzs), dw


def _conv_silu(x, w, s):
    row = _iota2(s, x.shape[1], 0)
    c = w[CONV_W - 1:CONV_W, :] * x
    for k in range(1, CONV_W):
        c = c + w[CONV_W - 1 - k:CONV_W - k, :] * jnp.where(row >= k, pltpu.roll(x, k, 0), 0.0)
    return c


def _dn_qkv_fwd(proj, conv_w, *, name):
    s = proj.shape[0]
    nb = 3 * N_HEADS

    def body(x_ref, w_ref, o_ref):
        j = pl.program_id(0)
        sv = _silu(_conv_silu(x_ref[...], w_ref[...], s))
        r = lax.rsqrt(jnp.sum(sv * sv, axis=-1, keepdims=True) + L2_EPS)
        scale = jnp.where(j < N_HEADS, HEAD_DIM ** -0.5, 1.0).astype(F32)
        o_ref[...] = jnp.where(j < 2 * N_HEADS, sv * r * scale, sv)

    return pl.pallas_call(
        body, name=name, grid=(nb,),
        in_specs=[pl.BlockSpec((s, HEAD_DIM), lambda j: (0, j)), pl.BlockSpec((CONV_W, HEAD_DIM), lambda j: (0, j))],
        out_specs=pl.BlockSpec((s, HEAD_DIM), lambda j: (0, j)),
        out_shape=jax.ShapeDtypeStruct((s, 3 * BR_WIDTH), F32), compiler_params=_cp("parallel"),
    )(proj, conv_w)


def _dn_qkv_bwd(proj, conv_w, dqkv, *, name):
    s = proj.shape[0]
    nb = 3 * N_HEADS

    def body(x_ref, w_ref, g_ref, dx_ref, dw_ref):
        j = pl.program_id(0)
        x, w, g = x_ref[...], w_ref[...], g_ref[...]
        c = _conv_silu(x, w, s)
        sv, dsv = _silu_and_grad(c)
        r = lax.rsqrt(jnp.sum(sv * sv, axis=-1, keepdims=True) + L2_EPS)
        scale = jnp.where(j < N_HEADS, HEAD_DIM ** -0.5, 1.0).astype(F32)
        ds_n = scale * (r * g - sv * ((r * r * r) * jnp.sum(g * sv, axis=-1, keepdims=True)))
        dc = jnp.where(j < 2 * N_HEADS, ds_n, g) * dsv
        row = _iota2(s, HEAD_DIM, 0)
        dx = w[CONV_W - 1:CONV_W, :] * dc
        dws = [jnp.sum(dc * x, axis=0, keepdims=True)]
        for k in range(1, CONV_W):
            dc_ahead = jnp.where(row < s - k, pltpu.roll(dc, s - k, 0), 0.0)
            dx = dx + w[CONV_W - 1 - k:CONV_W - k, :] * dc_ahead
            dws.append(jnp.sum(dc_ahead * x, axis=0, keepdims=True))
        dx_ref[...] = dx.astype(BF16)
        for k in range(CONV_W):
            dw_ref[CONV_W - 1 - k:CONV_W - k, :] = dws[k]

    blk = pl.BlockSpec((s, HEAD_DIM), lambda j: (0, j))
    wblk = pl.BlockSpec((CONV_W, HEAD_DIM), lambda j: (0, j))
    return pl.pallas_call(
        body, name=name, grid=(nb,), in_specs=[blk, wblk, blk], out_specs=[blk, wblk],
        out_shape=[jax.ShapeDtypeStruct((s, 3 * BR_WIDTH), BF16), jax.ShapeDtypeStruct((CONV_W, 3 * BR_WIDTH), F32)],
        compiler_params=_cp("parallel"),
    )(proj, conv_w, dqkv)


def _tri(n, kind):
    r, c = _iota2(n, n, 0), _iota2(n, n, 1)
    if kind == "lower":
        return (r >= c).astype(F32)
    if kind == "upper":
        return (r <= c).astype(F32)
    return (r == c).astype(F32)


GATE_TILE = 512


def _dn_gate_fwd(proj, a_log, dt_bias, *, name):
    s = proj.shape[0]
    tr = min(GATE_TILE, s)

    def body(b_ref, a_ref, al_ref, dt_ref, beta_ref, g_ref):
        beta_ref[...] = _sigmoid(b_ref[...])
        g = -jnp.exp(al_ref[...]) * _softplus(a_ref[...] + dt_ref[...])
        low = _tri(CHUNK, "lower")
        for c in range(tr // CHUNK):
            rows = slice(c * CHUNK, (c + 1) * CHUNK)
            g_ref[rows, :] = _nn_exact(low, g[rows, :])

    blk = lambda cb: pl.BlockSpec((tr, HEAD_DIM), lambda i: (i, cb))
    vec = pl.BlockSpec((1, HEAD_DIM), lambda i: (0, 0))
    out = pl.BlockSpec((tr, HEAD_DIM), lambda i: (i, 0))
    return pl.pallas_call(
        body, name=name, grid=(s // tr,), in_specs=[blk(C_B // HEAD_DIM), blk(C_A // HEAD_DIM), vec, vec],
        out_specs=[out, out], out_shape=[jax.ShapeDtypeStruct((s, HEAD_DIM), F32)] * 2, compiler_params=_cp("parallel"),
    )(proj, proj, a_log, dt_bias)


def _dn_gate_bwd(proj, a_log, dt_bias, dbeta, d_g, *, name):
    s = proj.shape[0]
    tr = min(GATE_TILE, s)

    def body(b_ref, a_ref, al_ref, dt_ref, dbeta_ref, dG_ref, db_ref, da_ref, dal_ref, ddt_ref):
        @pl.when(pl.program_id(0) == 0)
        def _():
            dal_ref[...] = jnp.zeros_like(dal_ref)
            ddt_ref[...] = jnp.zeros_like(ddt_ref)

        beta = _sigmoid(b_ref[...])
        db_ref[...] = (dbeta_ref[...] * beta * (1.0 - beta)).astype(BF16)
        pre = a_ref[...] + dt_ref[...]
        neg_ea = -jnp.exp(al_ref[...])
        up = _tri(CHUNK, "upper")
        d_g = dG_ref[...]
        dg = jnp.concatenate([_nn_exact(up, d_g[c * CHUNK:(c + 1) * CHUNK, :]) for c in range(tr // CHUNK)], axis=0)
        da = dg * neg_ea * _sigmoid(pre)
        da_ref[...] = da.astype(BF16)
        ddt_ref[...] += jnp.sum(da, axis=0, keepdims=True)
        dal_ref[...] += jnp.sum(dg * neg_ea * _softplus(pre), axis=0, keepdims=True)

    blk = lambda cb: pl.BlockSpec((tr, HEAD_DIM), lambda i: (i, cb))
    vec = pl.BlockSpec((1, HEAD_DIM), lambda i: (0, 0))
    io = pl.BlockSpec((tr, HEAD_DIM), lambda i: (i, 0))
    return pl.pallas_call(
        body, name=name, grid=(s // tr,),
        in_specs=[blk(C_B // HEAD_DIM), blk(C_A // HEAD_DIM), vec, vec, io, io], out_specs=[io, io, vec, vec],
        out_shape=[jax.ShapeDtypeStruct((s, HEAD_DIM), BF16)] * 2 + [jax.ShapeDtypeStruct((1, HEAD_DIM), F32)] * 2,
        compiler_params=_cp("arbitrary"),
    )(proj, proj, a_log, dt_bias, dbeta, d_g)


def _unit_lower_inverse(a_strict, eye):
    x = -a_strict
    t = x + eye
    p = x
    n = 2
    while n < CHUNK:
        p = _nn(p, p)
        t = t + _nn(t, p)
        n *= 2
    return t


def _rows(*xs):
    return _hmap(lambda *a: jnp.concatenate(a, axis=0), *xs)


def _lanes(*xs):
    return _hmap(lambda *a: jnp.concatenate(a, axis=1), *xs)


def _dn_chunk_common(q, k, v, gc, beta, st, with_qd_state, t_inv=None):
    c, d = CHUNK, HEAD_DIM
    eye = _tri(c, "eye")
    low = _tri(c, "lower")
    strict = low - eye
    grow = _col2row(gc, eye)
    dec = _hmap(lambda g_, gr: low * jnp.exp(low * (g_ - gr)), gc, grow)
    kb = k * beta
    kq = _nt(_rows(kb, q), k)
    a_mat = kq[0:c, :] * dec * strict
    qk = kq[c:2 * c, :] * dec
    if t_inv is None:
        t_inv = _unit_lower_inverse(a_mat, eye)
    e_g = _exp(gc)
    qd = q * e_g
    uw = _nn(t_inv, _lanes(v * beta, kb * e_g))
    u, w = uw[:, 0:d], uw[:, d:2 * d]
    last = (_iota2(c, 1, 0) == c - 1).astype(F32)
    g_last = _sum(gc * last, 0)
    e_t = _exp(g_last - gc)
    kt = k * e_t
    tail = _exp(g_last)
    if with_qd_state:
        ws = _nn(_rows(w, qd), st)
        vn, qds = u - ws[0:c, :], ws[c:2 * c, :]
    else:
        vn, qds = u - _nn(w, st), None
    return dict(eye=eye, low=low, strict=strict, dec=dec, kb=kb, a_mat=a_mat, t_inv=t_inv, e_g=e_g, u=u, w=w, uw=uw,
                qk=qk, qd=qd, qds=qds, last=last, e_t=e_t, kt=kt, tail=tail, vn=vn)


def _dn_chunk_fwd_math(q, k, v, gc, beta, st):
    m = _dn_chunk_common(q, k, v, gc, beta, st, True)
    o = m["qds"] + _nn(m["qk"], m["vn"])
    st2 = st * m["tail"] + _tn(m["kt"], m["vn"])
    return o, st2, m["t_inv"]


def _dn_chunk_bwd_math(q, k, v, gc, beta, st, do, dst2, t_inv=None):
    c, d = CHUNK, HEAD_DIM
    m = _dn_chunk_common(q, k, v, gc, beta, st, False, t_inv)
    eye, low, strict = m["eye"], m["low"], m["strict"]
    dvn = _tn(m["qk"], do) + _nn(m["kt"], dst2)
    dqk = _nt(do, m["vn"]) * low
    both = _rows(do, dvn)
    ds_both = _nt(both, st)
    dqd, dw = ds_both[0:c, :], -ds_both[c:2 * c, :]
    dst = _tn(_rows(m["qd"], -m["w"]), both) + dst2 * m["tail"]
    dkt = _nt(m["vn"], dst2)
    dtail = _sum(_sum(st * dst2, 1), 0)
    dvb_dkg = _tn(m["t_inv"], _lanes(dvn, dw))
    dvb, dkg = dvb_dkg[:, 0:d], dvb_dkg[:, d:2 * d]
    d_a = _nt(dvb_dkg, m["uw"]) * (-strict)
    dkk = d_a * m["dec"]
    dp = dqk * m["dec"]
    dpk = _rows(dp, dkk)
    dq_dkb = _nn(dpk, k)
    dq = dq_dkb[0:c, :] + dqd * m["e_g"]
    dkb = dq_dkb[c:2 * c, :] + dkg * m["e_g"]
    dk = _tn(dpk, _rows(q, m["kb"])) + dkb * beta + dkt * m["e_t"]
    dv = dvb * beta
    dbeta = _sum(dvb * v + dkb * k, 1)
    de_g = _sum(dkg * m["kb"] + dqd * q, 1)
    de_t = _sum(dkt * k, 1)
    mm = d_a * m["a_mat"] + dqk * m["qk"]
    dgc = (_sum(mm, 1) - _row2col(_sum(mm, 0), eye) + de_g * m["e_g"] - de_t * m["e_t"]
           + (_sum(de_t * m["e_t"], 0) + dtail * m["tail"]) * m["last"])
    return dq, dk, dv, dgc, dbeta, dst


def _heads_of(ref):
    return _Heads(ref[:, h * HEAD_DIM:(h + 1) * HEAD_DIM] for h in range(N_HEADS))


def _lanes_of(block):
    return _Heads(_pick_lane(block, h) for h in range(N_HEADS))


def _dn_chunk_fwd(qkv, gcs, beta, proj, norm_w, *, name):
    s = qkv.shape[0]
    n = s // CHUNK

    def body(q_ref, k_ref, v_ref, g_ref, b_ref, z_ref, w_ref, o_ref, st_out_ref, tinv_ref, y_ref, st_ref):
        @pl.when(pl.program_id(0) == 0)
        def _():
            st_ref[...] = jnp.zeros_like(st_ref)

        gblk, bblk = g_ref[...], b_ref[...]
        st = _Heads(st_ref[h] for h in range(N_HEADS))
        o, st2, t_inv = _dn_chunk_fwd_math(_heads_of(q_ref), _heads_of(k_ref), _heads_of(v_ref), _lanes_of(gblk),
                                           _lanes_of(bblk), st)
        y = _head_norm_fwd(o, _heads_of(z_ref), w_ref[...])
        for h in range(N_HEADS):
            st_out_ref[0, h] = st.v[h]
            tinv_ref[0, h] = t_inv.v[h].astype(BF16)
            o_ref[:, h * HEAD_DIM:(h + 1) * HEAD_DIM] = o.v[h]
            y_ref[:, h * HEAD_DIM:(h + 1) * HEAD_DIM] = y.v[h]
            st_ref[h] = st2.v[h]

    blk = lambda off: pl.BlockSpec((CHUNK, BR_WIDTH), lambda c: (c, off))
    sc = pl.BlockSpec((CHUNK, HEAD_DIM), lambda c: (c, 0))
    return pl.pallas_call(
        body, name=name, grid=(n,),
        in_specs=[blk(0), blk(1), blk(2), sc, sc, blk(C_Z // BR_WIDTH), pl.BlockSpec((1, HEAD_DIM), lambda c: (0, 0))],
        out_specs=[blk(0), pl.BlockSpec((1, N_HEADS, HEAD_DIM, HEAD_DIM), lambda c: (c, 0, 0, 0)),
                   pl.BlockSpec((1, N_HEADS, CHUNK, CHUNK), lambda c: (c, 0, 0, 0)), blk(0)],
        out_shape=[jax.ShapeDtypeStruct((s, BR_WIDTH), F32), jax.ShapeDtypeStruct((n, N_HEADS, HEAD_DIM, HEAD_DIM), F32),
                   jax.ShapeDtypeStruct((n, N_HEADS, CHUNK, CHUNK), BF16), jax.ShapeDtypeStruct((s, BR_WIDTH), BF16)],
        scratch_shapes=[pltpu.VMEM((N_HEADS, HEAD_DIM, HEAD_DIM), F32)],
        compiler_params=_cp("arbitrary"),
    )(qkv, qkv, qkv, gcs, beta, proj, norm_w.reshape(1, HEAD_DIM))


def _dn_chunk_bwd(qkv, gcs, beta, states, tinvs, o, proj, norm_w, dy, *, name):
    s = qkv.shape[0]
    n = s // CHUNK

    def body(q_ref, k_ref, v_ref, g_ref, b_ref, st_in_ref, tinv_ref, o_ref, z_ref, w_ref, dy_ref,
             dqkv_ref, dg_ref, dbeta_ref, dz_ref, dw_ref, dst_ref):
        @pl.when(pl.program_id(0) == 0)
        def _():
            dst_ref[...] = jnp.zeros_like(dst_ref)
            dw_ref[...] = jnp.zeros_like(dw_ref)

        do, dz, dw = _head_norm_bwd(_heads_of(o_ref), _heads_of(z_ref), w_ref[...], _heads_of(dy_ref))
        dw_ref[...] += dw

        gblk, bblk = g_ref[...], b_ref[...]
        lane = _iota2(CHUNK, HEAD_DIM, 1)
        dg_all = jnp.zeros((CHUNK, HEAD_DIM), F32)
        dbeta_all = jnp.zeros((CHUNK, HEAD_DIM), F32)
        dq, dk, dv, dgc, dbeta, dst = _dn_chunk_bwd_math(
            _heads_of(q_ref), _heads_of(k_ref), _heads_of(v_ref), _lanes_of(gblk), _lanes_of(bblk),
            _Heads(st_in_ref[0, h] for h in range(N_HEADS)), do,
            _Heads(dst_ref[h] for h in range(N_HEADS)), _Heads(tinv_ref[0, h] for h in range(N_HEADS)))
        for h in range(N_HEADS):
            dz_ref[:, h * HEAD_DIM:(h + 1) * HEAD_DIM] = dz.v[h]
            for part, val in enumerate((dq, dk, dv)):
                c0 = part * BR_WIDTH + h * HEAD_DIM
                dqkv_ref[:, c0:c0 + HEAD_DIM] = val.v[h]
            dg_all = jnp.where(lane == h, dgc.v[h], dg_all)
            dbeta_all = jnp.where(lane == h, dbeta.v[h], dbeta_all)
            dst_ref[h] = dst.v[h]
        dg_ref[...] = dg_all
        dbeta_ref[...] = dbeta_all

    blk = lambda off: pl.BlockSpec((CHUNK, BR_WIDTH), lambda c: (n - 1 - c, off))
    sc = pl.BlockSpec((CHUNK, HEAD_DIM), lambda c: (n - 1 - c, 0))
    vec = pl.BlockSpec((1, HEAD_DIM), lambda c: (0, 0))
    outs = pl.pallas_call(
        body, name=name, grid=(n,),
        in_specs=[blk(0), blk(1), blk(2), sc, sc,
                  pl.BlockSpec((1, N_HEADS, HEAD_DIM, HEAD_DIM), lambda c: (n - 1 - c, 0, 0, 0)),
                  pl.BlockSpec((1, N_HEADS, CHUNK, CHUNK), lambda c: (n - 1 - c, 0, 0, 0)), blk(0), blk(C_Z // BR_WIDTH),
                  vec, blk(0)],
        out_specs=[pl.BlockSpec((CHUNK, 3 * BR_WIDTH), lambda c: (n - 1 - c, 0)), sc, sc, blk(0), vec],
        out_shape=[jax.ShapeDtypeStruct((s, 3 * BR_WIDTH), F32)] + [jax.ShapeDtypeStruct((s, HEAD_DIM), F32)] * 2
        + [jax.ShapeDtypeStruct((s, BR_WIDTH), BF16), jax.ShapeDtypeStruct((1, HEAD_DIM), F32)],
        scratch_shapes=[pltpu.VMEM((N_HEADS, HEAD_DIM, HEAD_DIM), F32)],
        compiler_params=_cp("arbitrary"),
    )(qkv, qkv, qkv, gcs, beta, states, tinvs, o, proj, norm_w.reshape(1, HEAD_DIM), dy)
    return outs


def _hg_chunk_common(q, k, g):
    c, nb = CHUNK, CHUNK // SUB
    e_g = _exp(g)
    qd = q * e_g
    g_last = g[c - 1:c, :]
    e_t = _exp(g_last - g)
    kt = k * e_t
    tail = _exp(g_last)
    g_refs = [g[i * SUB:i * SUB + 1, :] for i in range(nb)]
    g_ref_rows = _hmap(lambda *rows: jnp.concatenate([jnp.broadcast_to(r, (SUB, r.shape[1])) for r in rows], axis=0), *g_refs)
    e_q = _exp(g - g_ref_rows)
    q_sc = q * e_q
    e_k = [_hmap(lambda gr, g_: jnp.exp(jnp.minimum(gr - g_, EXP_CLAMP)), g_refs[i], g) for i in range(nb)]
    k_sc_all = _rows(*[k * e_k[i] for i in range(nb)])
    row_blk = _iota2(c, 1, 0) // SUB
    masks = [(row_blk == i).astype(F32) for i in range(nb)]
    r_all = _nt(q_sc, k_sc_all)
    a_mat = r_all[:, 0:c] * masks[0]
    for i in range(1, nb):
        a_mat = a_mat + r_all[:, i * c:(i + 1) * c] * masks[i]
    a_mat = a_mat * _tri(c, "lower")
    return dict(e_g=e_g, qd=qd, e_t=e_t, kt=kt, tail=tail, q_sc=q_sc, k_sc_all=k_sc_all, e_q=e_q, e_k=e_k, masks=masks,
                a_mat=a_mat)


def _hg_chunk_fwd_math(q, k, v, g, stt):
    m = _hg_chunk_common(q, k, g)
    o = _nt(m["qd"], stt) + _nn(m["a_mat"], v)
    stt2 = stt * m["tail"] + _tn(v, m["kt"])
    return o, stt2


def _hg_chunk_bwd_math(q, k, v, g, stt, do, dstt2):
    c, nb = CHUNK, CHUNK // SUB
    m = _hg_chunk_common(q, k, g)
    stt2 = stt * m["tail"] + _tn(v, m["kt"])
    later = _sum(stt2 * dstt2, 0)
    dqd = _dot3(do, stt, 1, 0)
    dstt = _tn(do, m["qd"]) + dstt2 * m["tail"]
    d_a = _dot3(do, v, 1, 1) * _tri(c, "lower")
    dv = _tn(m["a_mat"], do) + _nt(m["kt"], dstt2)
    dkt = _dot3(v, dstt2, 1, 0)
    d_blk = _lanes(*[d_a * m["masks"][i] for i in range(nb)])
    dq = dqd * m["e_g"] + _dot3(d_blk, m["k_sc_all"], 1, 0) * m["e_q"]
    dks = _dot3(d_blk, m["q_sc"], 0, 0)
    dk = dkt * m["e_t"]
    for i in range(nb):
        dk = dk + dks[i * c:(i + 1) * c, :] * m["e_k"][i]
    db = q * dq - k * dk
    return dq, dk, dv, db, later, dstt


def _hg_chunk_fwd(proj, lb, norm_w, *, name):
    s = proj.shape[0]
    n = s // CHUNK

    def body(hq_ref, hf_ref, v_ref, lb_ref, z_ref, w_ref, o_ref, st_out_ref, q_out, k_out, lf_out, y_ref, st_ref):
        @pl.when(pl.program_id(0) == 0)
        def _():
            st_ref[...] = jnp.zeros_like(st_ref)

        f, lbv = hf_ref[...], lb_ref[...]
        q_all = _silu(hq_ref[...])
        k_all = (1.0 - lbv) * _sigmoid(-f)
        lf_all = jnp.log(lbv + (1.0 - lbv) * _sigmoid(f))
        q_out[...], k_out[...], lf_out[...] = q_all, k_all, lf_all
        st = _Heads(st_ref[h] for h in range(N_HEADS))
        g_all = _nn_exact(_tri(CHUNK, "lower"), lf_all)
        o, st2 = _hg_chunk_fwd_math(_heads_of(q_all), _heads_of(k_all), _heads_of(v_ref), _heads_of(g_all), st)
        y = _head_norm_fwd(o, _heads_of(z_ref), w_ref[...])
        for h in range(N_HEADS):
            st_out_ref[0, h] = st.v[h]
            o_ref[:, h * HEAD_DIM:(h + 1) * HEAD_DIM] = o.v[h]
            y_ref[:, h * HEAD_DIM:(h + 1) * HEAD_DIM] = y.v[h]
            st_ref[h] = st2.v[h]

    blk = lambda off: pl.BlockSpec((CHUNK, BR_WIDTH), lambda c: (c, off))
    return pl.pallas_call(
        body, name=name, grid=(n,),
        in_specs=[blk(C_HQ // BR_WIDTH), blk(C_HF // BR_WIDTH), blk(C_HI // BR_WIDTH), pl.BlockSpec((1, BR_WIDTH), lambda c: (0, 0)),
                  blk(C_HZ // BR_WIDTH), pl.BlockSpec((1, HEAD_DIM), lambda c: (0, 0))],
        out_specs=[blk(0), pl.BlockSpec((1, N_HEADS, HEAD_DIM, HEAD_DIM), lambda c: (c, 0, 0, 0)), blk(0), blk(0), blk(0), blk(0)],
        out_shape=[jax.ShapeDtypeStruct((s, BR_WIDTH), F32), jax.ShapeDtypeStruct((n, N_HEADS, HEAD_DIM, HEAD_DIM), F32)]
        + [jax.ShapeDtypeStruct((s, BR_WIDTH), F32)] * 3 + [jax.ShapeDtypeStruct((s, BR_WIDTH), BF16)],
        scratch_shapes=[pltpu.VMEM((N_HEADS, HEAD_DIM, HEAD_DIM), F32)],
        compiler_params=_cp("arbitrary"),
    )(proj, proj, proj, lb, proj, norm_w.reshape(1, HEAD_DIM))


def _hg_chunk_bwd(proj, lb, qh, kh, lf, states, o, norm_w, dy, *, name):
    s = proj.shape[0]
    n = s // CHUNK

    def body(hq_ref, hf_ref, v_ref, lb_ref, q_ref, k_ref, lf_ref, st_in_ref, o_ref, z_ref, w_ref, dy_ref,
             dhq_ref, dhf_ref, dhi_ref, dz_ref, dlb_ref, dw_ref, dst_ref):
        @pl.when(pl.program_id(0) == 0)
        def _():
            dst_ref[...] = jnp.zeros_like(dst_ref)
            dlb_ref[...] = jnp.zeros_like(dlb_ref)
            dw_ref[...] = jnp.zeros_like(dw_ref)

        do, dz, dw = _head_norm_bwd(_heads_of(o_ref), _heads_of(z_ref), w_ref[...], _heads_of(dy_ref))
        dw_ref[...] += dw

        g_all = _nn_exact(_tri(CHUNK, "lower"), lf_ref[...])
        dq, dk, dv, db, later, dst = _hg_chunk_bwd_math(
            _heads_of(q_ref), _heads_of(k_ref), _heads_of(v_ref), _heads_of(g_all),
            _Heads(st_in_ref[0, h] for h in range(N_HEADS)), do,
            _Heads(dst_ref[h] for h in range(N_HEADS)))
        dlf = _nn_exact(_tri(CHUNK, "upper"), jnp.concatenate(db.v, axis=1)) + jnp.concatenate(later.v, axis=1)
        dq_all, dk_all = jnp.concatenate(dq.v, axis=1), jnp.concatenate(dk.v, axis=1)
        f, lbv = hf_ref[...], lb_ref[...]
        dhq_ref[...] = (dq_all * _dsilu(hq_ref[...])).astype(BF16)
        sp, sn = _sigmoid(f), _sigmoid(-f)
        dlf_over = dlf / (lbv + (1.0 - lbv) * sp)
        dhf_ref[...] = (dlf_over * (1.0 - lbv) * sp * sn - dk_all * (1.0 - lbv) * sn * (1.0 - sn)).astype(BF16)
        dlb_ref[...] += jnp.sum(dlf_over * (1.0 - sp) - dk_all * sn, axis=0, keepdims=True)
        for h in range(N_HEADS):
            dhi_ref[:, h * HEAD_DIM:(h + 1) * HEAD_DIM] = dv.v[h].astype(BF16)
            dz_ref[:, h * HEAD_DIM:(h + 1) * HEAD_DIM] = dz.v[h]
            dst_ref[h] = dst.v[h]

    blk = lambda off: pl.BlockSpec((CHUNK, BR_WIDTH), lambda c: (n - 1 - c, off))
    vec = pl.BlockSpec((1, BR_WIDTH), lambda c: (0, 0))
    wvec = pl.BlockSpec((1, HEAD_DIM), lambda c: (0, 0))
    return pl.pallas_call(
        body, name=name, grid=(n,),
        in_specs=[blk(C_HQ // BR_WIDTH), blk(C_HF // BR_WIDTH), blk(C_HI // BR_WIDTH), vec, blk(0), blk(0), blk(0),
                  pl.BlockSpec((1, N_HEADS, HEAD_DIM, HEAD_DIM), lambda c: (n - 1 - c, 0, 0, 0)), blk(0), blk(C_HZ // BR_WIDTH),
                  wvec, blk(1)],
        out_specs=[blk(0), blk(0), blk(0), blk(0), vec, wvec],
        out_shape=[jax.ShapeDtypeStruct((s, BR_WIDTH), BF16)] * 4 + [jax.ShapeDtypeStruct((1, BR_WIDTH), F32),
                                                                    jax.ShapeDtypeStruct((1, HEAD_DIM), F32)],
        scratch_shapes=[pltpu.VMEM((N_HEADS, HEAD_DIM, HEAD_DIM), F32)],
        compiler_params=_cp("arbitrary"),
    )(proj, proj, proj, lb, qh, kh, lf, states, o, proj, norm_w.reshape(1, HEAD_DIM), dy)


_ANY = pl.BlockSpec(memory_space=pl.ANY)
_MESH = pl.DeviceIdType.MESH


def _all_gather(x_local, *, name, after=()):
    n_after = len(after)

    def body(x_ref, *refs):
        out_ref, send_sems, recv_sems, local_sem = refs[n_after:]
        x, y, c = lax.axis_index("x"), lax.axis_index("y"), lax.axis_index("c")
        me, sibling = (x, y, c), (x, y, 1 - c)
        n1 = (x ^ (1 - c), y ^ c)
        n2 = (x ^ c, y ^ (1 - c))
        dg = (1 - x, 1 - y)

        def slot(px, py, pc):
            return out_ref.at[4 * px + 2 * py + pc]

        def copy(k, block, to, src=None):
            return pltpu.make_async_remote_copy(
                src_ref=slot(*block) if src is None else src, dst_ref=slot(*block),
                send_sem=send_sems.at[k], recv_sem=recv_sems.at[k], device_id=to, device_id_type=_MESH)

        mine = pltpu.make_async_copy(x_ref, slot(*me), local_sem)
        mine.start()
        first = [copy(0, me, sibling, src=x_ref), copy(1, me, (*n1, c), src=x_ref), copy(2, me, (*n2, c), src=x_ref)]
        for cp in first:
            cp.start()
        copy(2, (*n2, c), me).wait_recv()
        forward = copy(3, (*n2, c), (*n1, c))
        forward.start()
        passed = [copy(5, (*n2, c), sibling)]
        passed[0].start()
        copy(1, (*n1, c), me).wait_recv()
        passed.append(copy(4, (*n1, c), sibling))
        passed[1].start()
        copy(3, (*dg, c), me).wait_recv()
        passed.append(copy(6, (*dg, c), sibling))
        passed[2].start()
        copy(0, sibling, me).wait_recv()
        copy(4, (*n2, 1 - c), me).wait_recv()
        copy(5, (*n1, 1 - c), me).wait_recv()
        copy(6, (*dg, 1 - c), me).wait_recv()
        for cp in first + [forward] + passed:
            cp.wait_send()
        mine.wait()

    return pl.pallas_call(
        body, name=name, out_shape=jax.ShapeDtypeStruct((N_DEV,) + x_local.shape, x_local.dtype),
        in_specs=[_ANY] * (1 + n_after), out_specs=_ANY,
        scratch_shapes=[pltpu.SemaphoreType.DMA((7,)), pltpu.SemaphoreType.DMA((7,)), pltpu.SemaphoreType.DMA],
    )(x_local, *after)


_HBM = pl.BlockSpec(memory_space=pltpu.HBM)
_SEM = pl.BlockSpec(memory_space=pltpu.SEMAPHORE)
_EFFECT = pltpu.SideEffectType.DATAFLOW_SIDE_EFFECTING


def _peers():
    x, y, c = lax.axis_index("x"), lax.axis_index("y"), lax.axis_index("c")
    out = []
    for k in range(1, N_DEV):
        px, py, pc = x ^ ((k >> 2) & 1), y ^ ((k >> 1) & 1), c ^ (k & 1)
        out.append(((px, py, pc), 4 * px + 2 * py + pc))
    return 4 * x + 2 * y + c, out


def _push_copies(src_ref, land_ref, send_sems, recv_sems, broadcast):
    my, peers = _peers()
    pairs = []
    for k, (pos, idx) in enumerate(peers):
        src = src_ref if broadcast else src_ref.at[idx]
        send = pltpu.make_async_remote_copy(src_ref=src, dst_ref=land_ref.at[my], send_sem=send_sems.at[k],
                                            recv_sem=recv_sems.at[k], device_id=pos, device_id_type=_MESH)
        recv = pltpu.make_async_remote_copy(src_ref=src, dst_ref=land_ref.at[idx], send_sem=send_sems.at[k],
                                            recv_sem=recv_sems.at[k], device_id=pos, device_id_type=_MESH)
        pairs.append((send, recv))
    return pairs


def _push_start(src, land, *, broadcast, name, after=()):
    n_after = len(after)

    def body(src_ref, land_ref, *refs):
        send_sems, recv_sems, _, _, token = refs[n_after:]
        for send, _ in _push_copies(src_ref, land_ref, send_sems, recv_sems, broadcast):
            send.start()
        token[...] = jnp.zeros_like(token)

    return pl.pallas_call(
        body, name=name,
        out_shape=(pltpu.SemaphoreType.DMA((N_DEV - 1,)), pltpu.SemaphoreType.DMA((N_DEV - 1,)),
                   pltpu.HBM(src.shape, src.dtype), pltpu.HBM(land.shape, land.dtype), jax.ShapeDtypeStruct((8, 128), F32)),
        in_specs=(_HBM, _HBM) + (_ANY,) * n_after, out_specs=(_SEM, _SEM, _HBM, _HBM, pl.BlockSpec(memory_space=pltpu.VMEM)),
        input_output_aliases={0: 2, 1: 3}, compiler_params=pltpu.CompilerParams(has_side_effects=_EFFECT),
    )(pltpu.with_memory_space_constraint(src, pltpu.HBM), pltpu.with_memory_space_constraint(land, pltpu.HBM), *after)


def _push_wait(handle, after, *, broadcast, name):
    send_sems, recv_sems, src_thru, land_thru, _ = handle

    def body(src_ref, land_ref, send_sems, recv_sems, *rest):
        for send, recv in _push_copies(src_ref, land_ref, send_sems, recv_sems, broadcast):
            send.wait_send()
            recv.wait_recv()

    return pl.pallas_call(
        body, name=name,
        out_shape=(pltpu.HBM(src_thru.shape, src_thru.dtype), pltpu.HBM(land_thru.shape, land_thru.dtype)),
        in_specs=(_HBM, _HBM, _SEM, _SEM) + (_ANY,) * len(after), out_specs=(_HBM, _HBM),
        input_output_aliases={0: 0, 1: 1}, compiler_params=pltpu.CompilerParams(has_side_effects=_EFFECT),
    )(src_thru, land_thru, send_sems, recv_sems, *after)[1]


def _relay_copies(src_ref, land_ref, sems_a, sems_b):
    x, y, c = lax.axis_index("x"), lax.axis_index("y"), lax.axis_index("c")
    slot = lambda px, py, pc: land_ref.at[4 * px + 2 * py + pc]
    chips = [(1 - x, y), (x, 1 - y), (1 - x, 1 - y)]
    (send_a, recv_a), (send_b, recv_b) = sems_a, sems_b

    def copy(sems, k, src, dst_slot, to):
        return pltpu.make_async_remote_copy(src_ref=src, dst_ref=dst_slot, send_sem=sems[0].at[k], recv_sem=sems[1].at[k],
                                            device_id=to, device_id_type=_MESH)

    first = [copy((send_a, recv_a), 0, src_ref, slot(x, y, c), (x, y, 1 - c))]
    first += [copy((send_a, recv_a), 1 + j, src_ref, slot(x, y, c), (*chip, c)) for j, chip in enumerate(chips)]
    first_in = [copy((send_a, recv_a), 0, src_ref, slot(x, y, 1 - c), (x, y, 1 - c))]
    first_in += [copy((send_a, recv_a), 1 + j, src_ref, slot(*chip, c), (*chip, c)) for j, chip in enumerate(chips)]
    relay = [copy((send_b, recv_b), j, slot(*chip, c), slot(*chip, c), (x, y, 1 - c)) for j, chip in enumerate(chips)]
    relay_in = [copy((send_b, recv_b), j, slot(*chip, 1 - c), slot(*chip, 1 - c), (x, y, 1 - c)) for j, chip in enumerate(chips)]
    return first, first_in, relay, relay_in


def _relay_start(src, land, *, name, after=()):
    n_after = len(after)

    def body(src_ref, land_ref, *refs):
        send_a, recv_a, _, _, token = refs[n_after:]
        for cp in _relay_copies(src_ref, land_ref, (send_a, recv_a), (send_a, recv_a))[0]:
            cp.start()
        token[...] = jnp.zeros_like(token)

    send_a, recv_a, src_thru, land_thru, token = pl.pallas_call(
        body, name=name,
        out_shape=(pltpu.SemaphoreType.DMA((4,)), pltpu.SemaphoreType.DMA((4,)), pltpu.HBM(src.shape, src.dtype),
                   pltpu.HBM(land.shape, land.dtype), jax.ShapeDtypeStruct((8, 128), F32)),
        in_specs=(_HBM, _HBM) + (_ANY,) * n_after, out_specs=(_SEM, _SEM, _HBM, _HBM, pl.BlockSpec(memory_space=pltpu.VMEM)),
        input_output_aliases={0: 2, 1: 3}, compiler_params=pltpu.CompilerParams(has_side_effects=_EFFECT),
    )(pltpu.with_memory_space_constraint(src, pltpu.HBM), pltpu.with_memory_space_constraint(land, pltpu.HBM), *after)
    return (send_a, recv_a), src_thru, land_thru, token


def _relay_mid(handle, after, *, name):
    sems_a, src_thru, land_thru, _ = handle
    n_after = len(after)

    def body(src_ref, land_ref, send_a, recv_a, *refs):
        send_b, recv_b, _, _, token = refs[n_after:]
        _, first_in, relay, _ = _relay_copies(src_ref, land_ref, (send_a, recv_a), (send_b, recv_b))
        for j in range(3):
            first_in[1 + j].wait_recv()
            relay[j].start()
        token[...] = jnp.zeros_like(token)

    send_b, recv_b, src2, land2, token = pl.pallas_call(
        body, name=name,
        out_shape=(pltpu.SemaphoreType.DMA((3,)), pltpu.SemaphoreType.DMA((3,)), pltpu.HBM(src_thru.shape, src_thru.dtype),
                   pltpu.HBM(land_thru.shape, land_thru.dtype), jax.ShapeDtypeStruct((8, 128), F32)),
        in_specs=(_HBM, _HBM, _SEM, _SEM) + (_ANY,) * n_after,
        out_specs=(_SEM, _SEM, _HBM, _HBM, pl.BlockSpec(memory_space=pltpu.VMEM)),
        input_output_aliases={0: 2, 1: 3}, compiler_params=pltpu.CompilerParams(has_side_effects=_EFFECT),
    )(src_thru, land_thru, *sems_a, *after)
    return sems_a, (send_b, recv_b), src2, land2, token


def _relay_wait(handle, after, *, name):
    sems_a, sems_b, src_thru, land_thru, _ = handle

    def body(src_ref, land_ref, send_a, recv_a, send_b, recv_b, *rest):
        first, first_in, relay, relay_in = _relay_copies(src_ref, land_ref, (send_a, recv_a), (send_b, recv_b))
        first_in[0].wait_recv()
        for cp in relay_in:
            cp.wait_recv()
        for cp in first + relay:
            cp.wait_send()

    return pl.pallas_call(
        body, name=name,
        out_shape=(pltpu.HBM(src_thru.shape, src_thru.dtype), pltpu.HBM(land_thru.shape, land_thru.dtype)),
        in_specs=(_HBM, _HBM, _SEM, _SEM, _SEM, _SEM) + (_ANY,) * len(after), out_specs=(_HBM, _HBM),
        input_output_aliases={0: 0, 1: 1}, compiler_params=pltpu.CompilerParams(has_side_effects=_EFFECT),
    )(src_thru, land_thru, *sems_a, *sems_b, *after)[1]


def _adamw(parts, row_off, w, m, v, *, layer=0, n_layers=1, prev=None, name, tr):
    rows, c = w.shape
    r = rows // n_layers
    np_ = parts.shape[0]
    tr = min(tr, r)
    assert r % tr == 0 and row_off % tr == 0
    ob, lb = row_off // tr, layer * (r // tr)
    c1 = 1.0 - ADAM_B1 ** ADAM_STEP
    c2 = 1.0 - ADAM_B2 ** ADAM_STEP
    n_prev = 0 if prev is None else 4

    def body(p_ref, w_ref, m_ref, v_ref, *refs):
        g_ref, d_ref, nm_ref, nv_ref = refs[n_prev:]
        g = p_ref[0].astype(F32)
        for s in range(1, np_):
            g = g + p_ref[s].astype(F32)
        wv = w_ref[...]
        m2 = ADAM_B1 * m_ref[...] + (1.0 - ADAM_B1) * g
        v2 = ADAM_B2 * v_ref[...] + (1.0 - ADAM_B2) * jnp.square(g)
        m_hat = m2 / c1
        v_hat = v2 / c2
        g_ref[...] = g
        d_ref[...] = -ADAM_LR * (m_hat / (jnp.sqrt(v_hat) + ADAM_EPS) + ADAM_WD * wv)
        nm_ref[...] = m2
        nv_ref[...] = v2

    blk = pl.BlockSpec((tr, c), lambda i: (lb + i, 0))
    return pl.pallas_call(
        body, name=name, grid=(r // tr,),
        in_specs=[pl.BlockSpec((np_, tr, c), lambda i: (0, ob + i, 0)), blk, blk, blk] + [_ANY] * n_prev,
        out_specs=[blk] * 4, out_shape=[jax.ShapeDtypeStruct((rows, c), F32)] * 4,
        input_output_aliases={4 + i: i for i in range(n_prev)}, compiler_params=_cp("parallel"),
    )(parts, w, m, v, *(prev or ()))


def _sum_parts(parts, *, name, after=()):
    np_, r, c = parts.shape

    def body(p_ref, *refs):
        o_ref = refs[-1]
        g = p_ref[0]
        for s in range(1, np_):
            g = g + p_ref[s]
        o_ref[...] = g

    vmem = pl.BlockSpec(memory_space=pltpu.VMEM)
    return pl.pallas_call(body, name=name, in_specs=[vmem] + [_ANY] * len(after), out_specs=vmem,
                          out_shape=jax.ShapeDtypeStruct((r, c), F32))(parts, *after)


def _pack(arrs):
    rows = []
    for a in arrs:
        f = a.reshape(-1).astype(F32)
        pad = (-f.shape[0]) % 128
        rows.append(jnp.pad(f, (0, pad)).reshape(-1, 128))
    out = jnp.concatenate(rows, axis=0)
    return jnp.pad(out, ((0, (-out.shape[0]) % 8), (0, 0)))


def _unpack(packed, shapes):
    outs, r0 = [], 0
    for shp in shapes:
        n = 1
        for d in shp:
            n *= d
        nr = -(-n // 128)
        outs.append(packed[r0:r0 + nr].reshape(-1)[:n].reshape(shp))
        r0 += nr
    return outs


_WIN_PIECES = ((0, 4096, 0), (4112, 8208, 0), (4096, 4104, HEAD_DIM - N_HEADS), (4104, 4112, HEAD_DIM - N_HEADS))


RELAYOUT_TILE = 256
LAST_SPLIT = 4
OTHER_SPLIT = 2


def _win_from_shards(shards, *, name):
    k = shards.shape[1]
    tr = min(RELAYOUT_TILE, k)

    def body(x_ref, o_ref):
        cols = []
        for lo, hi, pad in _WIN_PIECES:
            for j in range(N_DEV):
                a, b = max(lo, j * SHARD_IN), min(hi, (j + 1) * SHARD_IN)
                if a < b:
                    cols.append(x_ref[j, :, a - j * SHARD_IN:b - j * SHARD_IN])
            if pad:
                cols.append(jnp.zeros((tr, pad), x_ref.dtype))
        o_ref[...] = jnp.concatenate(cols, axis=1)

    return pl.pallas_call(
        body, name=name, grid=(k // tr,), in_specs=[pl.BlockSpec((N_DEV, tr, SHARD_IN), lambda i: (0, i, 0))],
        out_specs=pl.BlockSpec((tr, N_PROJ), lambda i: (i, 0)), out_shape=jax.ShapeDtypeStruct((k, N_PROJ), shards.dtype),
        compiler_params=_cp("parallel"),
    )(shards)


def _win_to_shards(g, *, name):
    k = g.shape[0]
    tr = min(RELAYOUT_TILE, k)
    starts, off = [], 0
    for lo, hi, pad in _WIN_PIECES:
        starts.append((lo, hi, off))
        off += hi - lo + pad

    def body(g_ref, o_ref):
        for j in range(N_DEV):
            cols = []
            for lo, hi, off in sorted(starts):
                a, b = max(lo, j * SHARD_IN), min(hi, (j + 1) * SHARD_IN)
                if a < b:
                    cols.append(g_ref[:, off + a - lo:off + b - lo])
            o_ref[j] = jnp.concatenate(cols, axis=1)

    return pl.pallas_call(
        body, name=name, grid=(k // tr,), in_specs=[pl.BlockSpec((tr, N_PROJ), lambda i: (i, 0))],
        out_specs=pl.BlockSpec((N_DEV, tr, SHARD_IN), lambda i: (0, i, 0)),
        out_shape=jax.ShapeDtypeStruct((N_DEV, k, SHARD_IN), g.dtype), compiler_params=_cp("parallel"),
    )(g)


def _lower_bounds(logits):
    probs = jax.nn.softmax(logits.astype(F32), axis=0)
    return jnp.cumsum(probs, axis=0) - probs[0]


def _pad_lanes(vec8):
    return jnp.pad(vec8.reshape(1, N_HEADS), ((0, 0), (0, HEAD_DIM - N_HEADS)))


def kernel(x, p, norm_w, w_in, dn_conv_w, dn_A_log, dn_dt_bias, dn_norm_w, hg_lb_logits, hg_norm_w, w_out, w_ple_up, w_ple_gate, final_norm_w, loss_target, m_norm_w, m_w_in, m_dn_conv_w, m_dn_A_log, m_dn_dt_bias, m_dn_norm_w, m_hg_lb_logits, m_hg_norm_w, m_w_out, m_w_ple_up, m_w_ple_gate, m_final_norm_w, v_norm_w, v_w_in, v_dn_conv_w, v_dn_A_log, v_dn_dt_bias, v_dn_norm_w, v_hg_lb_logits, v_hg_norm_w, v_w_out, v_w_ple_up, v_w_ple_gate, v_final_norm_w):
    depth = norm_w.shape[0]
    my = 4 * lax.axis_index("x") + 2 * lax.axis_index("y") + lax.axis_index("c")
    h = x[0]
    tgt = loss_target[0]
    rows_out = D_MODEL // N_DEV
    up_rows = PLE_DIM * (D_MODEL // N_DEV) // D_MODEL
    g_off, u_off = rows_out, 2 * rows_out

    def own_slot(block):
        return lax.dynamic_update_index_in_dim(lax.empty((N_DEV,) + block.shape, block.dtype), block, my, 0)

    win_bf = w_in.astype(BF16)
    rest_bf = [jnp.concatenate([w_out[l], w_ple_gate[l], w_ple_up[l].reshape(up_rows, D_MODEL)], axis=0).astype(BF16)
               for l in range(depth)]
    conv_push = _push_start(dn_conv_w, own_slot(dn_conv_w), broadcast=True, name="gather_conv_w_start")
    win_all = {0: _all_gather(win_bf[0], name="gather_w_in_l0", after=[conv_push[4]])}
    pending, relayed = {}, {}
    last = win_all[0]
    for l in range(depth):
        if l > 0:
            relayed["win", l] = _relay_start(win_bf[l], own_slot(win_bf[l]), after=[last], name=f"gather_w_in_l{l}_first")
            last = relayed["win", l][3]
        if l == 0:
            relayed["rest", l] = _relay_start(rest_bf[l], own_slot(rest_bf[l]), after=[last], name=f"gather_rest_l{l}_first")
            last = relayed["rest", l][3]
        else:
            pending["rest", l] = _push_start(rest_bf[l], own_slot(rest_bf[l]), broadcast=True, after=[last],
                                             name=f"gather_rest_l{l}_start")
            last = pending["rest", l][4]
    order_tok = last[0, 0]
    lbs = _lower_bounds(hg_lb_logits)

    saved = []
    weights = []
    for l in range(depth):
        tag = f"l{l}"
        if l > 0:
            win_all[l] = _relay_wait(relayed["win", l], [h], name=f"gather_w_in_{tag}_wait")
        wi = _win_from_shards(win_all[l], name=f"w_in_layout_{tag}")
        nw = norm_w[l] + order_tok if l == 0 else norm_w[l]
        hn = _rms_fwd(h, nw, name=f"rms_fwd_{tag}")
        proj = _mm(hn, wi, mode="nn", out_dtype=F32, name=f"mm_proj_{tag}")
        al, dt = _pad_lanes(dn_A_log[l]), _pad_lanes(dn_dt_bias[l])
        if l == 0:
            conv_all = _push_wait(conv_push, [proj], broadcast=True, name="gather_conv_w_wait")
            conv_full = conv_all.transpose(1, 2, 0, 3).reshape(depth, CONV_W, 3 * BR_WIDTH)
        qkv = _dn_qkv_fwd(proj, conv_full[l], name=f"dn_qkv_fwd_{tag}")
        if ("rest", l) in relayed:
            relayed["rest", l] = _relay_mid(relayed["rest", l], [qkv], name=f"gather_rest_{tag}_relay")
            al = al + relayed["rest", l][4][0, 0]
        beta, gcs = _dn_gate_fwd(proj, al, dt, name=f"dn_gate_fwd_{tag}")
        o_dn, st_dn, tinv_dn, y_dn = _dn_chunk_fwd(qkv, gcs, beta, proj, dn_norm_w[l], name=f"dn_chunk_fwd_{tag}")
        lb = lbs[l].reshape(1, BR_WIDTH)
        o_hg, st_hg, qh, kh, lf, y_hg = _hg_chunk_fwd(proj, lb, hg_norm_w[l], name=f"hg_chunk_fwd_{tag}")
        y = jnp.concatenate([y_dn, y_hg], axis=1)
        if ("rest", l) in relayed:
            rest_all = _relay_wait(relayed["rest", l], [y], name=f"gather_rest_{tag}_wait")
        else:
            rest_all = _push_wait(pending["rest", l], [y], broadcast=True, name=f"gather_rest_{tag}_wait")
        w_out_rows, w_gate_rows = (0, rows_out), (g_off, rows_out)
        wu = rest_all[:, u_off:u_off + up_rows].reshape(N_DEV, PLE_DIM, D_MODEL // N_DEV).transpose(1, 0, 2).reshape(PLE_DIM, D_MODEL)
        weights.append((wi, rest_all, wu))
        h1 = _mm(y, rest_all, mode="nn", b_rows=w_out_rows, out_dtype=F32, res=h, name=f"mm_out_{tag}")
        pin = []
        if ("win", l + 1) in relayed:
            relayed["win", l + 1] = _relay_mid(relayed["win", l + 1], [h1], name=f"gather_w_in_l{l + 1}_relay")
            pin = [relayed["win", l + 1][4]]
        up = _mm(p[l, 0], wu, mode="nn", out_dtype=F32, name=f"mm_up_{tag}")
        gp, h2 = _mm(h1, rest_all, mode="nn", b_rows=w_gate_rows, out_dtype=F32, after=pin, tile_n=512, fused=("ple", h1, up),
                     name=f"mm_gate_{tag}")
        saved.append(dict(h=h, hn=hn, proj=proj, qkv=qkv, beta=beta, gcs=gcs, st_dn=st_dn, tinv_dn=tinv_dn, qh=qh, kh=kh, lf=lf,
                          st_hg=st_hg, o_dn=o_dn, o_hg=o_hg, y=y, h1=h1, gp=gp, up=up, al=al, dt=dt, lb=lb))
        h = h2

    loss_row, dh, d_final_w = _final_fwd_bwd(h, final_norm_w, tgt, name="final_norm_loss")

    d_norm_w, d_alog, d_dt, d_dn_nw, d_hg_nw, d_lb, d_conv = ([None] * depth for _ in range(7))
    sent = {}
    for l in reversed(range(depth)):
        wi, rest_all, wu = weights[l]
        sv = saved[l]
        tag = f"l{l}"
        dup, dgp = _ple_bwd(dh, sv["gp"], sv["up"], name=f"ple_bwd_{tag}")
        d_wu = _mm(p[l, 0], dup, mode="tn", out_dtype=BF16, name=f"mm_dwup_{tag}")
        d_wg = _mm(sv["h1"], dgp, mode="tn", out_dtype=BF16, name=f"mm_dwgate_{tag}")
        dh1 = _mm(dgp, rest_all, mode="nt", b_rows=(g_off, rows_out), out_dtype=F32, res=dh, name=f"mm_dh1_{tag}")
        d_wo = _mm(sv["y"], dh1, mode="tn", out_dtype=BF16, name=f"mm_dwout_{tag}")
        parts_rest = jnp.concatenate(
            [d_wo.reshape(N_DEV, rows_out, D_MODEL), d_wg.reshape(N_DEV, rows_out, D_MODEL),
             d_wu.reshape(PLE_DIM, N_DEV, D_MODEL // N_DEV).transpose(1, 0, 2).reshape(N_DEV, up_rows, D_MODEL)], axis=1)
        sent["rest", l] = _push_start(parts_rest, own_slot(parts_rest[my]), broadcast=False, name=f"exchange_rest_{tag}_start")
        dy = _mm(dh1, rest_all, mode="nt", b_rows=(0, rows_out), out_dtype=F32, name=f"mm_dy_{tag}")
        dn_nw = dn_norm_w[l] + sent["rest", l][4][0, 0]
        dqkv, d_gc, dbeta, dz_dn, d_dn_nw[l] = _dn_chunk_bwd(sv["qkv"], sv["gcs"], sv["beta"], sv["st_dn"], sv["tinv_dn"],
                                                             sv["o_dn"], sv["proj"], dn_nw, dy, name=f"dn_chunk_bwd_{tag}")
        dqkv_pre, d_conv[l] = _dn_qkv_bwd(sv["proj"], conv_full[l], dqkv, name=f"dn_qkv_bwd_{tag}")
        db, da, d_alog[l], d_dt[l] = _dn_gate_bwd(sv["proj"], sv["al"], sv["dt"], dbeta, d_gc, name=f"dn_gate_bwd_{tag}")
        dhq, dhf, dhi, dz_hg, d_lb[l], d_hg_nw[l] = _hg_chunk_bwd(sv["proj"], sv["lb"], sv["qh"], sv["kh"], sv["lf"], sv["st_hg"],
                                                                  sv["o_hg"], hg_norm_w[l], dy, name=f"hg_chunk_bwd_{tag}")
        dproj = jnp.concatenate([dqkv_pre, dz_dn, dhq, dhf, dhi, dz_hg, db, da], axis=1)
        def push_d_win(after):
            n_split = LAST_SPLIT if l == 0 else OTHER_SPLIT
            rows = D_MODEL // n_split
            handles = []
            for q in range(n_split):
                hn_q = sv["hn"] if n_split == 1 else sv["hn"][:, q * rows:(q + 1) * rows]
                sfx = tag if n_split == 1 else f"{tag}_{q}"
                d_win = _mm(hn_q, dproj, mode="tn", out_dtype=BF16, after=after, name=f"mm_dwin_{sfx}")
                parts_in = _win_to_shards(d_win, name=f"dw_in_shards_{sfx}")
                handles.append(_push_start(parts_in, own_slot(parts_in[my]), broadcast=False, after=after,
                                           name=f"exchange_w_in_{sfx}_start"))
                after = [handles[-1][4]]
            return handles

        if l == 0:
            small = _pack([loss_row, jnp.concatenate(d_norm_w[1:], axis=0), d_final_w,
                           jnp.stack([a[0, :N_HEADS] for a in d_alog]), jnp.stack([a[0, :N_HEADS] for a in d_dt]),
                           jnp.concatenate(d_dn_nw, axis=0), jnp.concatenate(d_hg_nw, axis=0), jnp.concatenate(d_lb, axis=0),
                           jnp.stack(d_conv)])
            small_all = _all_gather(small, name="gather_small")
        sent["win", l] = push_d_win([small_all] if l == 0 else [])
        dh, d_norm_w[l] = _mm(dproj, wi, mode="nt", out_dtype=F32, tile_m=512, tile_n=D_MODEL, tile_k=768,
                              after=[sent["win", l][-1][4]],
                              fused=("rms_bwd", sv["h"], norm_w[l], dh1), name=f"mm_dhn_{tag}")
    grad_x = dh[None]

    small_shapes = [(1, 128), (depth - 1, D_MODEL), final_norm_w.shape, dn_A_log.shape, dn_dt_bias.shape, dn_norm_w.shape,
                    hg_norm_w.shape, hg_lb_logits.shape, (depth, CONV_W, 3 * BR_WIDTH)]
    tot = _unpack(_sum_parts(small_all, after=[grad_x], name="sum_small"), small_shapes)
    loss = tot[0][0, 0]
    g_lb = tot[7]
    g_logits = jax.vjp(_lower_bounds, hg_lb_logits)[1](g_lb)[0]
    g_conv = lax.dynamic_slice_in_dim(tot[8], my * (3 * BR_WIDTH // N_DEV), 3 * BR_WIDTH // N_DEV, axis=2)
    small_g = [g_conv, tot[3], tot[4], tot[5], g_logits, tot[6], tot[2]]
    small_w = [dn_conv_w, dn_A_log, dn_dt_bias, dn_norm_w, hg_lb_logits, hg_norm_w, final_norm_w]
    small_m = [m_dn_conv_w, m_dn_A_log, m_dn_dt_bias, m_dn_norm_w, m_hg_lb_logits, m_hg_norm_w, m_final_norm_w]
    small_v = [v_dn_conv_w, v_dn_A_log, v_dn_dt_bias, v_dn_norm_w, v_hg_lb_logits, v_hg_norm_w, v_final_norm_w]
    pk_w = _pack(small_w)
    res_small = _adamw(_pack(small_g)[None], 0, pk_w, _pack(small_m), _pack(small_v), name="adamw_small", tr=pk_w.shape[0])
    shapes_w = [a.shape for a in small_w]
    sg, sd, sm, sv_ = (_unpack(r, shapes_w) for r in res_small)

    r_win = r_wo = r_wg = r_wu = None
    done = [grad_x, res_small[0]]

    def flat(a, cols):
        return a.reshape(-1, cols)

    for l in reversed(range(depth)):
        tag = f"l{l}"
        land_rest = _push_wait(sent["rest", l], done, broadcast=False, name=f"exchange_rest_{tag}_wait")
        r_wo = _adamw(land_rest, 0, flat(w_out, D_MODEL), flat(m_w_out, D_MODEL), flat(v_w_out, D_MODEL), layer=l,
                      n_layers=depth, prev=r_wo, name=f"adamw_w_out_{tag}", tr=rows_out)
        r_wg = _adamw(land_rest, g_off, flat(w_ple_gate, D_MODEL), flat(m_w_ple_gate, D_MODEL), flat(v_w_ple_gate, D_MODEL),
                      layer=l, n_layers=depth, prev=r_wg, name=f"adamw_w_gate_{tag}", tr=rows_out)
        r_wu = _adamw(land_rest, u_off, flat(w_ple_up, D_MODEL), flat(m_w_ple_up, D_MODEL), flat(v_w_ple_up, D_MODEL),
                      layer=l, n_layers=depth, prev=r_wu, name=f"adamw_w_up_{tag}", tr=up_rows)
        done = [r_wo[0], r_wg[0], r_wu[0]]
    for l in reversed(range(depth)):
        tag = f"l{l}"
        if l == 0:
            nw0 = _sum_parts(_all_gather(_pack([d_norm_w[0]]), after=done, name="gather_norm_w"), name="sum_norm_w")
            g_norm_w = jnp.concatenate([_unpack(nw0, [(1, D_MODEL)])[0], tot[1]], axis=0)
            pk_nw = _pack([norm_w])
            r_nw = _adamw(_pack([g_norm_w])[None], 0, pk_nw, _pack([m_norm_w]), _pack([v_norm_w]), name="adamw_norm_w",
                          tr=pk_nw.shape[0])
            r_nw = [_unpack(r, [norm_w.shape])[0] for r in r_nw]
            done = [r_nw[0]]
        n_split = len(sent["win", l])
        for q, handle in enumerate(sent["win", l]):
            sfx = tag if n_split == 1 else f"{tag}_{q}"
            land_in = _push_wait(handle, done, broadcast=False, name=f"exchange_w_in_{sfx}_wait")
            r_win = _adamw(land_in, 0, flat(w_in, SHARD_IN), flat(m_w_in, SHARD_IN), flat(v_w_in, SHARD_IN),
                           layer=l * n_split + q, n_layers=depth * n_split, prev=r_win, name=f"adamw_w_in_{sfx}", tr=256)
            done = [r_win[0]]
    r_win = [o.reshape(w_in.shape) for o in r_win]
    r_wo = [o.reshape(w_out.shape) for o in r_wo]
    r_wg = [o.reshape(w_ple_gate.shape) for o in r_wg]
    r_wu = [o.reshape(w_ple_up.shape) for o in r_wu]

    def order(nw, small_list, big_in, big_out, big_up, big_gate):
        cw, al_, dt_, dnw, lbl, hnw, fw = small_list
        return [nw, big_in, cw, al_, dt_, dnw, lbl, hnw, big_out, big_up, big_gate, fw]

    outs = [loss, grad_x]
    for i, sl in enumerate((sg, sd, sm, sv_)):
        outs += order(r_nw[i], sl, r_win[i], r_wo[i], r_wu[i], r_wg[i])
    return tuple(outs)
```

```python
import functools

import jax
import jax.numpy as jnp
from jax import lax
from jax.experimental import pallas as pl
from jax.experimental.pallas import tpu as pltpu

F32 = jnp.float32
BF16 = jnp.bfloat16
HIGHEST = lax.Precision.HIGHEST

N_DEV = 8
D_MODEL = 2048
PLE_DIM = 256
HEAD_DIM = 128
N_HEADS = 8
BR_WIDTH = N_HEADS * HEAD_DIM
CHUNK = 64
SUB = 16
CONV_W = 4
NORM_EPS = 1e-6
L2_EPS = 1e-6
IN_WIDTH = 8208
SHARD_IN = IN_WIDTH // N_DEV
EXP_CLAMP = 80.0

C_QKV, C_Z, C_HQ, C_HF, C_HI, C_HZ, C_B, C_A, N_PROJ = 0, 3072, 4096, 5120, 6144, 7168, 8192, 8320, 8448

ADAM_LR, ADAM_B1, ADAM_B2, ADAM_EPS, ADAM_WD, ADAM_STEP = 0.001, 0.9, 0.999, 1e-08, 0.01, 10

VMEM_LIMIT = 48 * 1024 * 1024


def _cp(*sem):
    return pltpu.CompilerParams(dimension_semantics=sem, vmem_limit_bytes=VMEM_LIMIT)


class _Heads:
    def __init__(self, vals):
        self.v = tuple(vals)

    def __add__(self, o):
        return _hmap(lambda a, b: a + b, self, o)

    def __radd__(self, o):
        return _hmap(lambda a, b: b + a, self, o)

    def __sub__(self, o):
        return _hmap(lambda a, b: a - b, self, o)

    def __rsub__(self, o):
        return _hmap(lambda a, b: b - a, self, o)

    def __mul__(self, o):
        return _hmap(lambda a, b: a * b, self, o)

    def __rmul__(self, o):
        return _hmap(lambda a, b: b * a, self, o)

    def __neg__(self):
        return _hmap(lambda a: -a, self)

    def __getitem__(self, idx):
        return _hmap(lambda a: a[idx], self)


def _hmap(fn, *args):
    n = next((len(a.v) for a in args if isinstance(a, _Heads)), None)
    if n is None:
        return fn(*args)
    return _Heads(fn(*[a.v[i] if isinstance(a, _Heads) else a for a in args]) for i in range(n))


def _dot(a, b, ca, cb):
    return _hmap(lambda x, y: lax.dot_general(x.astype(BF16), y.astype(BF16), (((ca,), (cb,)), ((), ())),
                                              preferred_element_type=F32), a, b)


def _nn(a, b):
    return _dot(a, b, 1, 0)


def _nt(a, b):
    return _dot(a, b, 1, 1)


def _tn(a, b):
    return _dot(a, b, 0, 0)


def _split(a):
    hi = _hmap(lambda x: x.astype(BF16), a)
    return hi, _hmap(lambda x, h: (x - h.astype(F32)).astype(BF16), a, hi)


def _dot3(a, b, ca, cb):
    ah, al = _split(a)
    bh, bl = _split(b)
    return _dot(ah, bh, ca, cb) + (_dot(ah, bl, ca, cb) + _dot(al, bh, ca, cb))


def _nn_exact(a, b):
    return _hmap(lambda y: lax.dot_general(a, y, (((1,), (0,)), ((), ())), precision=HIGHEST,
                                           preferred_element_type=F32), b)


def _exp(x):
    return _hmap(jnp.exp, x)


def _sum(x, axis):
    return _hmap(lambda a: jnp.sum(a, axis=axis, keepdims=True), x)


def _sigmoid(x):
    return jax.nn.sigmoid(x)


def _silu(x):
    return x * _sigmoid(x)


def _dsilu(x):
    s = _sigmoid(x)
    return s * (1.0 + x * (1.0 - s))


def _silu_and_grad(x):
    s = _sigmoid(x)
    return x * s, s * (1.0 + x * (1.0 - s))


def _softplus(x):
    return jnp.maximum(x, 0.0) + jnp.log(1.0 + jnp.exp(-jnp.abs(x)))


def _iota2(n, m, axis):
    return lax.broadcasted_iota(jnp.int32, (n, m), axis)


def _col2row(col, eye):
    return _hmap(lambda c: jnp.sum(eye * c, axis=0, keepdims=True), col)


def _row2col(row, eye):
    return _hmap(lambda r: jnp.sum(eye * r, axis=1, keepdims=True), row)


def _pick_lane(block, lane_idx):
    lane = _iota2(block.shape[0], block.shape[1], 1)
    return jnp.sum(jnp.where(lane == lane_idx, block, 0.0), axis=1, keepdims=True)


MM_TILE_M, MM_TILE_N, MM_TILE_K = 1024, 1408, 2048


def _tile(dim, cap):
    if dim <= cap:
        return dim
    t = cap - cap % 128
    while dim % t:
        t -= 128
    return t


def _mm(a, b, *, mode, out_dtype, res=None, after=(), b_rows=None, tile_m=MM_TILE_M, tile_n=MM_TILE_N, tile_k=MM_TILE_K,
        fused=None, name):
    b_mat_rows = b.shape[0] if b_rows is None else N_DEV * b_rows[1]
    if mode == "nn":
        (m, kd), n = a.shape, b.shape[-1]
        assert kd == b_mat_rows
    elif mode == "nt":
        (m, kd), n = a.shape, b_mat_rows
    else:
        (kd, m), n = a.shape, b.shape[-1]
    tm, tn, tk = _tile(m, tile_m), _tile(n, tile_n), _tile(kd, tile_k)
    assert m % tm == 0 and n % tn == 0 and kd % tk == 0, (m, n, kd, tm, tn, tk)
    nk = kd // tk
    ca, cb = {"nn": (1, 0), "nt": (1, 1), "tn": (0, 0)}[mode]

    kind = None if fused is None else fused[0]
    n_in = 2 + (res is not None) + (0 if fused is None else len(fused) - 1)
    n_out = 1 if fused is None else 2

    def body(*refs):
        a_ref, b_ref = refs[:2]
        r_ref = None if res is None else refs[2]
        extra = refs[2 + (res is not None):n_in]
        outs = refs[-1 - n_out:-1]
        o_ref, acc_ref = outs[0], refs[-1]
        k = pl.program_id(2)

        @pl.when(k == 0)
        def _():
            acc_ref[...] = jnp.zeros_like(acc_ref)

        b_tile = b_ref[...]
        if b_rows is not None:
            b_tile = b_tile.reshape(-1, b_tile.shape[-1])
        acc_ref[...] += _dot(a_ref[...], b_tile, ca, cb)

        @pl.when(k == nk - 1)
        def _():
            out = acc_ref[...]
            if r_ref is not None:
                out = out + r_ref[...].astype(F32)
            if kind == "ple":
                h1_ref, up_ref = extra
                o_ref[...] = out
                outs[1][...] = h1_ref[...] + up_ref[...] * _sigmoid(out)
            elif kind == "rms_bwd":
                h_ref, w_ref, res_ref = extra
                dx, dwt = _rms_bwd_math(h_ref[...], w_ref[...], out)
                o_ref[...] = res_ref[...] + dx

                @pl.when(pl.program_id(0) == 0)
                def _():
                    outs[1][...] = jnp.zeros_like(outs[1])

                outs[1][...] += jnp.sum(dwt, axis=0, keepdims=True)
            else:
                o_ref[...] = out.astype(o_ref.dtype)

    a_spec = pl.BlockSpec((tk, tm), lambda i, j, k: (k, i)) if mode == "tn" else pl.BlockSpec((tm, tk), lambda i, j, k: (i, k))
    if b_rows is None:
        b_spec = pl.BlockSpec((tn, tk), lambda i, j, k: (j, k)) if mode == "nt" else pl.BlockSpec((tk, tn), lambda i, j, k: (k, j))
    else:
        first, count = b_rows
        assert first % count == 0 and mode in ("nn", "nt")
        rb = first // count
        if mode == "nn":
            assert tk == kd
            b_spec = pl.BlockSpec((N_DEV, count, tn), lambda i, j, k: (0, rb, j))
        else:
            assert tn % count == 0
            b_spec = pl.BlockSpec((tn // count, count, tk), lambda i, j, k: (j, rb, k))
    o_spec = pl.BlockSpec((tm, tn), lambda i, j, k: (i, j))
    row_spec = pl.BlockSpec((1, tn), lambda i, j, k: (0, j))
    extra_specs, extra_args, out_specs, out_shape = [], (), o_spec, jax.ShapeDtypeStruct((m, n), out_dtype)
    sem = ("parallel", "parallel", "arbitrary")
    if kind == "ple":
        extra_specs, extra_args = [o_spec, o_spec], tuple(fused[1:])
        out_specs, out_shape = [o_spec, o_spec], [jax.ShapeDtypeStruct((m, n), F32)] * 2
    elif kind == "rms_bwd":
        assert tn == n
        extra_specs, extra_args = [o_spec, row_spec, o_spec], (fused[1], fused[2].reshape(1, n), fused[3])
        out_specs, out_shape = [o_spec, row_spec], [jax.ShapeDtypeStruct((m, n), F32), jax.ShapeDtypeStruct((1, n), F32)]
        sem = ("arbitrary", "arbitrary", "arbitrary")
    in_specs = ([a_spec, b_spec] + ([o_spec] if res is not None else []) + extra_specs
                + [pl.BlockSpec(memory_space=pl.ANY)] * len(after))
    args = (a, b) + ((res,) if res is not None else ()) + extra_args + tuple(after)
    return pl.pallas_call(
        body, name=name, grid=(m // tm, n // tn, nk), in_specs=in_specs, out_specs=out_specs, out_shape=out_shape,
        scratch_shapes=[pltpu.VMEM((tm, tn), F32)], compiler_params=_cp(*sem),
    )(*args)


ROW_TILE = 256


def _rms_fwd(h, w, *, name):
    s, d = h.shape
    tr = min(ROW_TILE, s)

    def body(h_ref, w_ref, o_ref, ot_ref):
        x = h_ref[...]
        r = lax.rsqrt(jnp.mean(x * x, axis=-1, keepdims=True) + NORM_EPS)
        y = x * r * w_ref[...]
        o_ref[...] = y.astype(o_ref.dtype)
        ot_ref[...] = y.T.astype(ot_ref.dtype)

    return pl.pallas_call(
        body, name=name, grid=(s // tr,),
        in_specs=[pl.BlockSpec((tr, d), lambda i: (i, 0)), pl.BlockSpec((1, d), lambda i: (0, 0))],
        out_specs=[pl.BlockSpec((tr, d), lambda i: (i, 0)), pl.BlockSpec((d, tr), lambda i: (0, i))],
        out_shape=[jax.ShapeDtypeStruct((s, d), BF16), jax.ShapeDtypeStruct((d, s), BF16)], compiler_params=_cp("parallel"),
    )(h, w.reshape(1, d))


def _rms_bwd_math(x, w, dy):
    d = x.shape[-1]
    r = lax.rsqrt(jnp.mean(x * x, axis=-1, keepdims=True) + NORM_EPS)
    gw = dy * w
    dx = r * gw - x * ((r * r * r) * (jnp.sum(gw * x, axis=-1, keepdims=True) / d))
    return dx, dy * x * r


def _final_fwd_bwd(h, w, tgt, *, name):
    s, d = h.shape
    tr = min(ROW_TILE, s)

    def body(h_ref, w_ref, t_ref, loss_ref, dh_ref, dw_ref):
        @pl.when(pl.program_id(0) == 0)
        def _():
            loss_ref[...] = jnp.zeros_like(loss_ref)
            dw_ref[...] = jnp.zeros_like(dw_ref)

        x = h_ref[...]
        wv = w_ref[...]
        r = lax.rsqrt(jnp.mean(x * x, axis=-1, keepdims=True) + NORM_EPS)
        err = x * r * wv - t_ref[...]
        row_loss = jnp.mean(err * err, axis=-1, keepdims=True)
        loss_ref[...] += 0.5 * jnp.sum(row_loss, axis=0, keepdims=True)
        dx, dwt = _rms_bwd_math(x, wv, err / d)
        dh_ref[...] = dx
        dw_ref[...] += jnp.sum(dwt, axis=0, keepdims=True)

    row = pl.BlockSpec((tr, d), lambda i: (i, 0))
    vec = pl.BlockSpec((1, d), lambda i: (0, 0))
    return pl.pallas_call(
        body, name=name, grid=(s // tr,), in_specs=[row, vec, row],
        out_specs=[pl.BlockSpec((1, 128), lambda i: (0, 0)), row, vec],
        out_shape=[jax.ShapeDtypeStruct((1, 128), F32), jax.ShapeDtypeStruct((s, d), F32),
                   jax.ShapeDtypeStruct((1, d), F32)],
        compiler_params=_cp("arbitrary"),
    )(h, w.reshape(1, d), tgt)


def _ple_bwd(dh2, gate_pre, up, *, name):
    s, d = dh2.shape
    tr = min(ROW_TILE, s)

    def body(d_ref, g_ref, u_ref, dup_ref, dgp_ref):
        dh = d_ref[...]
        gate = _sigmoid(g_ref[...])
        dup_ref[...] = (dh * gate).astype(BF16)
        dgp_ref[...] = (dh * u_ref[...] * gate * (1.0 - gate)).astype(BF16)

    row = pl.BlockSpec((tr, d), lambda i: (i, 0))
    return pl.pallas_call(body, name=name, grid=(s // tr,), in_specs=[row, row, row], out_specs=[row, row],
                          out_shape=[jax.ShapeDtypeStruct((s, d), BF16)] * 2, compiler_params=_cp("parallel"))(dh2, gate_pre, up)


def _head_norm_fwd(o, z, w):
    return _hmap(lambda x, zz: (x * lax.rsqrt(jnp.mean(x * x, axis=-1, keepdims=True) + NORM_EPS) * w * _silu(zz)).astype(BF16),
                 o, z)


def _head_norm_bwd(o, z, w, dy):
    dos, dzs, dw = [], [], jnp.zeros((1, HEAD_DIM), F32)
    for x, zz, g in zip(o.v, z.v, dy.v):
        r = lax.rsqrt(jnp.mean(x * x, axis=-1, keepdims=True) + NORM_EPS)
        silu_z, dsilu_z = _silu_and_grad(zz)
        don = g * silu_z
        dzs.append((g * (x * r * w) * dsilu_z).astype(BF16))
        gw = don * w
        dos.append(r * gw - x * ((r * r * r) * (jnp.sum(gw * x, axis=-1, keepdims=True) / HEAD_DIM)))
        dw = dw + jnp.sum(don * x * r, axis=0, keepdims=True)
    return _Heads(dos), _Heads(dzs), dw


def _conv_silu(x, w, s):
    row = _iota2(s, x.shape[1], 0)
    c = w[CONV_W - 1:CONV_W, :] * x
    for k in range(1, CONV_W):
        c = c + w[CONV_W - 1 - k:CONV_W - k, :] * jnp.where(row >= k, pltpu.roll(x, k, 0), 0.0)
    return c


def _dn_qkv_fwd(proj, conv_w, *, name):
    s = proj.shape[0]
    nb = 3 * N_HEADS

    def body(x_ref, w_ref, o_ref):
        j = pl.program_id(0)
        sv = _silu(_conv_silu(x_ref[...], w_ref[...], s))
        r = lax.rsqrt(jnp.sum(sv * sv, axis=-1, keepdims=True) + L2_EPS)
        scale = jnp.where(j < N_HEADS, HEAD_DIM ** -0.5, 1.0).astype(F32)
        o_ref[...] = jnp.where(j < 2 * N_HEADS, sv * r * scale, sv)

    return pl.pallas_call(
        body, name=name, grid=(nb,),
        in_specs=[pl.BlockSpec((s, HEAD_DIM), lambda j: (0, j)), pl.BlockSpec((CONV_W, HEAD_DIM), lambda j: (0, j))],
        out_specs=pl.BlockSpec((s, HEAD_DIM), lambda j: (0, j)),
        out_shape=jax.ShapeDtypeStruct((s, 3 * BR_WIDTH), F32), compiler_params=_cp("parallel"),
    )(proj, conv_w)


def _dn_qkv_bwd(proj, conv_w, dqkv, *, name):
    s = proj.shape[0]
    nb = 3 * N_HEADS

    def body(x_ref, w_ref, g_ref, dx_ref, dw_ref):
        j = pl.program_id(0)
        x, w, g = x_ref[...], w_ref[...], g_ref[...]
        c = _conv_silu(x, w, s)
        sv, dsv = _silu_and_grad(c)
        r = lax.rsqrt(jnp.sum(sv * sv, axis=-1, keepdims=True) + L2_EPS)
        scale = jnp.where(j < N_HEADS, HEAD_DIM ** -0.5, 1.0).astype(F32)
        ds_n = scale * (r * g - sv * ((r * r * r) * jnp.sum(g * sv, axis=-1, keepdims=True)))
        dc = jnp.where(j < 2 * N_HEADS, ds_n, g) * dsv
        row = _iota2(s, HEAD_DIM, 0)
        dx = w[CONV_W - 1:CONV_W, :] * dc
        dws = [jnp.sum(dc * x, axis=0, keepdims=True)]
        for k in range(1, CONV_W):
            dc_ahead = jnp.where(row < s - k, pltpu.roll(dc, s - k, 0), 0.0)
            dx = dx + w[CONV_W - 1 - k:CONV_W - k, :] * dc_ahead
            dws.append(jnp.sum(dc_ahead * x, axis=0, keepdims=True))
        dx_ref[...] = dx.astype(BF16)
        for k in range(CONV_W):
            dw_ref[CONV_W - 1 - k:CONV_W - k, :] = dws[k]

    blk = pl.BlockSpec((s, HEAD_DIM), lambda j: (0, j))
    wblk = pl.BlockSpec((CONV_W, HEAD_DIM), lambda j: (0, j))
    return pl.pallas_call(
        body, name=name, grid=(nb,), in_specs=[blk, wblk, blk], out_specs=[blk, wblk],
        out_shape=[jax.ShapeDtypeStruct((s, 3 * BR_WIDTH), BF16), jax.ShapeDtypeStruct((CONV_W, 3 * BR_WIDTH), F32)],
        compiler_params=_cp("parallel"),
    )(proj, conv_w, dqkv)


def _tri(n, kind):
    r, c = _iota2(n, n, 0), _iota2(n, n, 1)
    if kind == "lower":
        return (r >= c).astype(F32)
    if kind == "upper":
        return (r <= c).astype(F32)
    return (r == c).astype(F32)


GATE_TILE = 512


def _dn_gate_fwd(proj, a_log, dt_bias, *, name):
    s = proj.shape[0]
    tr = min(GATE_TILE, s)

    def body(b_ref, a_ref, al_ref, dt_ref, beta_ref, g_ref):
        beta_ref[...] = _sigmoid(b_ref[...])
        g = -jnp.exp(al_ref[...]) * _softplus(a_ref[...] + dt_ref[...])
        low = _tri(CHUNK, "lower")
        for c in range(tr // CHUNK):
            rows = slice(c * CHUNK, (c + 1) * CHUNK)
            g_ref[rows, :] = _nn_exact(low, g[rows, :])

    blk = lambda cb: pl.BlockSpec((tr, HEAD_DIM), lambda i: (i, cb))
    vec = pl.BlockSpec((1, HEAD_DIM), lambda i: (0, 0))
    out = pl.BlockSpec((tr, HEAD_DIM), lambda i: (i, 0))
    return pl.pallas_call(
        body, name=name, grid=(s // tr,), in_specs=[blk(C_B // HEAD_DIM), blk(C_A // HEAD_DIM), vec, vec],
        out_specs=[out, out], out_shape=[jax.ShapeDtypeStruct((s, HEAD_DIM), F32)] * 2, compiler_params=_cp("parallel"),
    )(proj, proj, a_log, dt_bias)


def _dn_gate_bwd(proj, a_log, dt_bias, dbeta, d_g, *, name):
    s = proj.shape[0]
    tr = min(GATE_TILE, s)

    def body(b_ref, a_ref, al_ref, dt_ref, dbeta_ref, dG_ref, db_ref, da_ref, dal_ref, ddt_ref):
        @pl.when(pl.program_id(0) == 0)
        def _():
            dal_ref[...] = jnp.zeros_like(dal_ref)
            ddt_ref[...] = jnp.zeros_like(ddt_ref)

        beta = _sigmoid(b_ref[...])
        db_ref[...] = (dbeta_ref[...] * beta * (1.0 - beta)).astype(BF16)
        pre = a_ref[...] + dt_ref[...]
        neg_ea = -jnp.exp(al_ref[...])
        up = _tri(CHUNK, "upper")
        d_g = dG_ref[...]
        dg = jnp.concatenate([_nn_exact(up, d_g[c * CHUNK:(c + 1) * CHUNK, :]) for c in range(tr // CHUNK)], axis=0)
        da = dg * neg_ea * _sigmoid(pre)
        da_ref[...] = da.astype(BF16)
        ddt_ref[...] += jnp.sum(da, axis=0, keepdims=True)
        dal_ref[...] += jnp.sum(dg * neg_ea * _softplus(pre), axis=0, keepdims=True)

    blk = lambda cb: pl.BlockSpec((tr, HEAD_DIM), lambda i: (i, cb))
    vec = pl.BlockSpec((1, HEAD_DIM), lambda i: (0, 0))
    io = pl.BlockSpec((tr, HEAD_DIM), lambda i: (i, 0))
    return pl.pallas_call(
        body, name=name, grid=(s // tr,),
        in_specs=[blk(C_B // HEAD_DIM), blk(C_A // HEAD_DIM), vec, vec, io, io], out_specs=[io, io, vec, vec],
        out_shape=[jax.ShapeDtypeStruct((s, HEAD_DIM), BF16)] * 2 + [jax.ShapeDtypeStruct((1, HEAD_DIM), F32)] * 2,
        compiler_params=_cp("arbitrary"),
    )(proj, proj, a_log, dt_bias, dbeta, d_g)


def _unit_lower_inverse(a_strict, eye):
    x = -a_strict
    t = x + eye
    p = x
    n = 2
    while n < CHUNK:
        p = _nn(p, p)
        t = t + _nn(t, p)
        n *= 2
    return t


def _rows(*xs):
    return _hmap(lambda *a: jnp.concatenate(a, axis=0), *xs)


def _lanes(*xs):
    return _hmap(lambda *a: jnp.concatenate(a, axis=1), *xs)


def _dn_chunk_common(q, k, v, gc, beta, st, with_qd_state, t_inv=None):
    c, d = CHUNK, HEAD_DIM
    eye = _tri(c, "eye")
    low = _tri(c, "lower")
    strict = low - eye
    grow = _col2row(gc, eye)
    dec = _hmap(lambda g_, gr: low * jnp.exp(low * (g_ - gr)), gc, grow)
    kb = k * beta
    kq = _nt(_rows(kb, q), k)
    a_mat = kq[0:c, :] * dec * strict
    qk = kq[c:2 * c, :] * dec
    if t_inv is None:
        t_inv = _unit_lower_inverse(a_mat, eye)
    e_g = _exp(gc)
    qd = q * e_g
    uw = _nn(t_inv, _lanes(v * beta, kb * e_g))
    u, w = uw[:, 0:d], uw[:, d:2 * d]
    last = (_iota2(c, 1, 0) == c - 1).astype(F32)
    g_last = _sum(gc * last, 0)
    e_t = _exp(g_last - gc)
    kt = k * e_t
    tail = _exp(g_last)
    if with_qd_state:
        ws = _nn(_rows(w, qd), st)
        vn, qds = u - ws[0:c, :], ws[c:2 * c, :]
    else:
        vn, qds = u - _nn(w, st), None
    return dict(eye=eye, low=low, strict=strict, dec=dec, kb=kb, a_mat=a_mat, t_inv=t_inv, e_g=e_g, u=u, w=w, uw=uw,
                qk=qk, qd=qd, qds=qds, last=last, e_t=e_t, kt=kt, tail=tail, vn=vn)


def _dn_chunk_fwd_math(q, k, v, gc, beta, st):
    m = _dn_chunk_common(q, k, v, gc, beta, st, True)
    o = m["qds"] + _nn(m["qk"], m["vn"])
    st2 = st * m["tail"] + _tn(m["kt"], m["vn"])
    return o, st2, m["t_inv"]


def _dn_chunk_bwd_math(q, k, v, gc, beta, st, do, dst2, t_inv=None):
    c, d = CHUNK, HEAD_DIM
    m = _dn_chunk_common(q, k, v, gc, beta, st, False, t_inv)
    eye, low, strict = m["eye"], m["low"], m["strict"]
    dvn = _tn(m["qk"], do) + _nn(m["kt"], dst2)
    dqk = _nt(do, m["vn"]) * low
    both = _rows(do, dvn)
    ds_both = _nt(both, st)
    dqd, dw = ds_both[0:c, :], -ds_both[c:2 * c, :]
    dst = _tn(_rows(m["qd"], -m["w"]), both) + dst2 * m["tail"]
    dkt = _nt(m["vn"], dst2)
    dtail = _sum(_sum(st * dst2, 1), 0)
    dvb_dkg = _tn(m["t_inv"], _lanes(dvn, dw))
    dvb, dkg = dvb_dkg[:, 0:d], dvb_dkg[:, d:2 * d]
    d_a = _nt(dvb_dkg, m["uw"]) * (-strict)
    dkk = d_a * m["dec"]
    dp = dqk * m["dec"]
    dpk = _rows(dp, dkk)
    dq_dkb = _nn(dpk, k)
    dq = dq_dkb[0:c, :] + dqd * m["e_g"]
    dkb = dq_dkb[c:2 * c, :] + dkg * m["e_g"]
    dk = _tn(dpk, _rows(q, m["kb"])) + dkb * beta + dkt * m["e_t"]
    dv = dvb * beta
    dbeta = _sum(dvb * v + dkb * k, 1)
    de_g = _sum(dkg * m["kb"] + dqd * q, 1)
    de_t = _sum(dkt * k, 1)
    mm = d_a * m["a_mat"] + dqk * m["qk"]
    dgc = (_sum(mm, 1) - _row2col(_sum(mm, 0), eye) + de_g * m["e_g"] - de_t * m["e_t"]
           + (_sum(de_t * m["e_t"], 0) + dtail * m["tail"]) * m["last"])
    return dq, dk, dv, dgc, dbeta, dst


def _heads_of(ref):
    return _Heads(ref[:, h * HEAD_DIM:(h + 1) * HEAD_DIM] for h in range(N_HEADS))


def _lanes_of(block):
    return _Heads(_pick_lane(block, h) for h in range(N_HEADS))


def _dn_chunk_fwd(qkv, gcs, beta, proj, norm_w, *, name):
    s = qkv.shape[0]
    n = s // CHUNK

    def body(q_ref, k_ref, v_ref, g_ref, b_ref, z_ref, w_ref, o_ref, st_out_ref, tinv_ref, y_ref, st_ref):
        @pl.when(pl.program_id(0) == 0)
        def _():
            st_ref[...] = jnp.zeros_like(st_ref)

        gblk, bblk = g_ref[...], b_ref[...]
        st = _Heads(st_ref[h] for h in range(N_HEADS))
        o, st2, t_inv = _dn_chunk_fwd_math(_heads_of(q_ref), _heads_of(k_ref), _heads_of(v_ref), _lanes_of(gblk),
                                           _lanes_of(bblk), st)
        y = _head_norm_fwd(o, _heads_of(z_ref), w_ref[...])
        for h in range(N_HEADS):
            st_out_ref[0, h] = st.v[h]
            tinv_ref[0, h] = t_inv.v[h].astype(BF16)
            o_ref[:, h * HEAD_DIM:(h + 1) * HEAD_DIM] = o.v[h]
            y_ref[:, h * HEAD_DIM:(h + 1) * HEAD_DIM] = y.v[h]
            st_ref[h] = st2.v[h]

    blk = lambda off: pl.BlockSpec((CHUNK, BR_WIDTH), lambda c: (c, off))
    sc = pl.BlockSpec((CHUNK, HEAD_DIM), lambda c: (c, 0))
    return pl.pallas_call(
        body, name=name, grid=(n,),
        in_specs=[blk(0), blk(1), blk(2), sc, sc, blk(C_Z // BR_WIDTH), pl.BlockSpec((1, HEAD_DIM), lambda c: (0, 0))],
        out_specs=[blk(0), pl.BlockSpec((1, N_HEADS, HEAD_DIM, HEAD_DIM), lambda c: (c, 0, 0, 0)),
                   pl.BlockSpec((1, N_HEADS, CHUNK, CHUNK), lambda c: (c, 0, 0, 0)), blk(0)],
        out_shape=[jax.ShapeDtypeStruct((s, BR_WIDTH), F32), jax.ShapeDtypeStruct((n, N_HEADS, HEAD_DIM, HEAD_DIM), F32),
                   jax.ShapeDtypeStruct((n, N_HEADS, CHUNK, CHUNK), BF16), jax.ShapeDtypeStruct((s, BR_WIDTH), BF16)],
        scratch_shapes=[pltpu.VMEM((N_HEADS, HEAD_DIM, HEAD_DIM), F32)],
        compiler_params=_cp("arbitrary"),
    )(qkv, qkv, qkv, gcs, beta, proj, norm_w.reshape(1, HEAD_DIM))


def _dn_chunk_bwd(qkv, gcs, beta, states, tinvs, o, proj, norm_w, dy, *, name):
    s = qkv.shape[0]
    n = s // CHUNK

    def body(q_ref, k_ref, v_ref, g_ref, b_ref, st_in_ref, tinv_ref, o_ref, z_ref, w_ref, dy_ref,
             dqkv_ref, dg_ref, dbeta_ref, dz_ref, dw_ref, dst_ref):
        @pl.when(pl.program_id(0) == 0)
        def _():
            dst_ref[...] = jnp.zeros_like(dst_ref)
            dw_ref[...] = jnp.zeros_like(dw_ref)

        do, dz, dw = _head_norm_bwd(_heads_of(o_ref), _heads_of(z_ref), w_ref[...], _heads_of(dy_ref))
        dw_ref[...] += dw

        gblk, bblk = g_ref[...], b_ref[...]
        lane = _iota2(CHUNK, HEAD_DIM, 1)
        dg_all = jnp.zeros((CHUNK, HEAD_DIM), F32)
        dbeta_all = jnp.zeros((CHUNK, HEAD_DIM), F32)
        dq, dk, dv, dgc, dbeta, dst = _dn_chunk_bwd_math(
            _heads_of(q_ref), _heads_of(k_ref), _heads_of(v_ref), _lanes_of(gblk), _lanes_of(bblk),
            _Heads(st_in_ref[0, h] for h in range(N_HEADS)), do,
            _Heads(dst_ref[h] for h in range(N_HEADS)), _Heads(tinv_ref[0, h] for h in range(N_HEADS)))
        for h in range(N_HEADS):
            dz_ref[:, h * HEAD_DIM:(h + 1) * HEAD_DIM] = dz.v[h]
            for part, val in enumerate((dq, dk, dv)):
                c0 = part * BR_WIDTH + h * HEAD_DIM
                dqkv_ref[:, c0:c0 + HEAD_DIM] = val.v[h]
            dg_all = jnp.where(lane == h, dgc.v[h], dg_all)
            dbeta_all = jnp.where(lane == h, dbeta.v[h], dbeta_all)
            dst_ref[h] = dst.v[h]
        dg_ref[...] = dg_all
        dbeta_ref[...] = dbeta_all

    blk = lambda off: pl.BlockSpec((CHUNK, BR_WIDTH), lambda c: (n - 1 - c, off))
    sc = pl.BlockSpec((CHUNK, HEAD_DIM), lambda c: (n - 1 - c, 0))
    vec = pl.BlockSpec((1, HEAD_DIM), lambda c: (0, 0))
    outs = pl.pallas_call(
        body, name=name, grid=(n,),
        in_specs=[blk(0), blk(1), blk(2), sc, sc,
                  pl.BlockSpec((1, N_HEADS, HEAD_DIM, HEAD_DIM), lambda c: (n - 1 - c, 0, 0, 0)),
                  pl.BlockSpec((1, N_HEADS, CHUNK, CHUNK), lambda c: (n - 1 - c, 0, 0, 0)), blk(0), blk(C_Z // BR_WIDTH),
                  vec, blk(0)],
        out_specs=[pl.BlockSpec((CHUNK, 3 * BR_WIDTH), lambda c: (n - 1 - c, 0)), sc, sc, blk(0), vec],
        out_shape=[jax.ShapeDtypeStruct((s, 3 * BR_WIDTH), F32)] + [jax.ShapeDtypeStruct((s, HEAD_DIM), F32)] * 2
        + [jax.ShapeDtypeStruct((s, BR_WIDTH), BF16), jax.ShapeDtypeStruct((1, HEAD_DIM), F32)],
        scratch_shapes=[pltpu.VMEM((N_HEADS, HEAD_DIM, HEAD_DIM), F32)],
        compiler_params=_cp("arbitrary"),
    )(qkv, qkv, qkv, gcs, beta, states, tinvs, o, proj, norm_w.reshape(1, HEAD_DIM), dy)
    return outs


def _hg_chunk_common(q, k, g):
    c, nb = CHUNK, CHUNK // SUB
    e_g = _exp(g)
    qd = q * e_g
    g_last = g[c - 1:c, :]
    e_t = _exp(g_last - g)
    kt = k * e_t
    tail = _exp(g_last)
    g_refs = [g[i * SUB:i * SUB + 1, :] for i in range(nb)]
    g_ref_rows = _hmap(lambda *rows: jnp.concatenate([jnp.broadcast_to(r, (SUB, r.shape[1])) for r in rows], axis=0), *g_refs)
    e_q = _exp(g - g_ref_rows)
    q_sc = q * e_q
    e_k = [_hmap(lambda gr, g_: jnp.exp(jnp.minimum(gr - g_, EXP_CLAMP)), g_refs[i], g) for i in range(nb)]
    k_sc_all = _rows(*[k * e_k[i] for i in range(nb)])
    row_blk = _iota2(c, 1, 0) // SUB
    masks = [(row_blk == i).astype(F32) for i in range(nb)]
    r_all = _nt(q_sc, k_sc_all)
    a_mat = r_all[:, 0:c] * masks[0]
    for i in range(1, nb):
        a_mat = a_mat + r_all[:, i * c:(i + 1) * c] * masks[i]
    a_mat = a_mat * _tri(c, "lower")
    return dict(e_g=e_g, qd=qd, e_t=e_t, kt=kt, tail=tail, q_sc=q_sc, k_sc_all=k_sc_all, e_q=e_q, e_k=e_k, masks=masks,
                a_mat=a_mat)


def _hg_chunk_fwd_math(q, k, v, g, stt):
    m = _hg_chunk_common(q, k, g)
    o = _nt(m["qd"], stt) + _nn(m["a_mat"], v)
    stt2 = stt * m["tail"] + _tn(v, m["kt"])
    return o, stt2


def _hg_chunk_bwd_math(q, k, v, g, stt, do, dstt2):
    c, nb = CHUNK, CHUNK // SUB
    m = _hg_chunk_common(q, k, g)
    stt2 = stt * m["tail"] + _tn(v, m["kt"])
    later = _sum(stt2 * dstt2, 0)
    dqd = _dot3(do, stt, 1, 0)
    dstt = _tn(do, m["qd"]) + dstt2 * m["tail"]
    d_a = _dot3(do, v, 1, 1) * _tri(c, "lower")
    dv = _tn(m["a_mat"], do) + _nt(m["kt"], dstt2)
    dkt = _dot3(v, dstt2, 1, 0)
    d_blk = _lanes(*[d_a * m["masks"][i] for i in range(nb)])
    dq = dqd * m["e_g"] + _dot3(d_blk, m["k_sc_all"], 1, 0) * m["e_q"]
    dks = _dot3(d_blk, m["q_sc"], 0, 0)
    dk = dkt * m["e_t"]
    for i in range(nb):
        dk = dk + dks[i * c:(i + 1) * c, :] * m["e_k"][i]
    db = q * dq - k * dk
    return dq, dk, dv, db, later, dstt


def _hg_chunk_fwd(proj, lb, norm_w, *, name):
    s = proj.shape[0]
    n = s // CHUNK

    def body(hq_ref, hf_ref, v_ref, lb_ref, z_ref, w_ref, o_ref, st_out_ref, q_out, k_out, lf_out, y_ref, st_ref):
        @pl.when(pl.program_id(0) == 0)
        def _():
            st_ref[...] = jnp.zeros_like(st_ref)

        f, lbv = hf_ref[...], lb_ref[...]
        q_all = _silu(hq_ref[...])
        k_all = (1.0 - lbv) * _sigmoid(-f)
        lf_all = jnp.log(lbv + (1.0 - lbv) * _sigmoid(f))
        q_out[...], k_out[...], lf_out[...] = q_all, k_all, lf_all
        st = _Heads(st_ref[h] for h in range(N_HEADS))
        g_all = _nn_exact(_tri(CHUNK, "lower"), lf_all)
        o, st2 = _hg_chunk_fwd_math(_heads_of(q_all), _heads_of(k_all), _heads_of(v_ref), _heads_of(g_all), st)
        y = _head_norm_fwd(o, _heads_of(z_ref), w_ref[...])
        for h in range(N_HEADS):
            st_out_ref[0, h] = st.v[h]
            o_ref[:, h * HEAD_DIM:(h + 1) * HEAD_DIM] = o.v[h]
            y_ref[:, h * HEAD_DIM:(h + 1) * HEAD_DIM] = y.v[h]
            st_ref[h] = st2.v[h]

    blk = lambda off: pl.BlockSpec((CHUNK, BR_WIDTH), lambda c: (c, off))
    return pl.pallas_call(
        body, name=name, grid=(n,),
        in_specs=[blk(C_HQ // BR_WIDTH), blk(C_HF // BR_WIDTH), blk(C_HI // BR_WIDTH), pl.BlockSpec((1, BR_WIDTH), lambda c: (0, 0)),
                  blk(C_HZ // BR_WIDTH), pl.BlockSpec((1, HEAD_DIM), lambda c: (0, 0))],
        out_specs=[blk(0), pl.BlockSpec((1, N_HEADS, HEAD_DIM, HEAD_DIM), lambda c: (c, 0, 0, 0)), blk(0), blk(0), blk(0), blk(0)],
        out_shape=[jax.ShapeDtypeStruct((s, BR_WIDTH), F32), jax.ShapeDtypeStruct((n, N_HEADS, HEAD_DIM, HEAD_DIM), F32)]
        + [jax.ShapeDtypeStruct((s, BR_WIDTH), F32)] * 3 + [jax.ShapeDtypeStruct((s, BR_WIDTH), BF16)],
        scratch_shapes=[pltpu.VMEM((N_HEADS, HEAD_DIM, HEAD_DIM), F32)],
        compiler_params=_cp("arbitrary"),
    )(proj, proj, proj, lb, proj, norm_w.reshape(1, HEAD_DIM))


def _hg_chunk_bwd(proj, lb, qh, kh, lf, states, o, norm_w, dy, *, name):
    s = proj.shape[0]
    n = s // CHUNK

    def body(hq_ref, hf_ref, v_ref, lb_ref, q_ref, k_ref, lf_ref, st_in_ref, o_ref, z_ref, w_ref, dy_ref,
             dhq_ref, dhf_ref, dhi_ref, dz_ref, dlb_ref, dw_ref, dst_ref):
        @pl.when(pl.program_id(0) == 0)
        def _():
            dst_ref[...] = jnp.zeros_like(dst_ref)
            dlb_ref[...] = jnp.zeros_like(dlb_ref)
            dw_ref[...] = jnp.zeros_like(dw_ref)

        do, dz, dw = _head_norm_bwd(_heads_of(o_ref), _heads_of(z_ref), w_ref[...], _heads_of(dy_ref))
        dw_ref[...] += dw

        g_all = _nn_exact(_tri(CHUNK, "lower"), lf_ref[...])
        dq, dk, dv, db, later, dst = _hg_chunk_bwd_math(
            _heads_of(q_ref), _heads_of(k_ref), _heads_of(v_ref), _heads_of(g_all),
            _Heads(st_in_ref[0, h] for h in range(N_HEADS)), do,
            _Heads(dst_ref[h] for h in range(N_HEADS)))
        dlf = _nn_exact(_tri(CHUNK, "upper"), jnp.concatenate(db.v, axis=1)) + jnp.concatenate(later.v, axis=1)
        dq_all, dk_all = jnp.concatenate(dq.v, axis=1), jnp.concatenate(dk.v, axis=1)
        f, lbv = hf_ref[...], lb_ref[...]
        dhq_ref[...] = (dq_all * _dsilu(hq_ref[...])).astype(BF16)
        sp, sn = _sigmoid(f), _sigmoid(-f)
        dlf_over = dlf / (lbv + (1.0 - lbv) * sp)
        dhf_ref[...] = (dlf_over * (1.0 - lbv) * sp * sn - dk_all * (1.0 - lbv) * sn * (1.0 - sn)).astype(BF16)
        dlb_ref[...] += jnp.sum(dlf_over * (1.0 - sp) - dk_all * sn, axis=0, keepdims=True)
        for h in range(N_HEADS):
            dhi_ref[:, h * HEAD_DIM:(h + 1) * HEAD_DIM] = dv.v[h].astype(BF16)
            dz_ref[:, h * HEAD_DIM:(h + 1) * HEAD_DIM] = dz.v[h]
            dst_ref[h] = dst.v[h]

    blk = lambda off: pl.BlockSpec((CHUNK, BR_WIDTH), lambda c: (n - 1 - c, off))
    vec = pl.BlockSpec((1, BR_WIDTH), lambda c: (0, 0))
    wvec = pl.BlockSpec((1, HEAD_DIM), lambda c: (0, 0))
    return pl.pallas_call(
        body, name=name, grid=(n,),
        in_specs=[blk(C_HQ // BR_WIDTH), blk(C_HF // BR_WIDTH), blk(C_HI // BR_WIDTH), vec, blk(0), blk(0), blk(0),
                  pl.BlockSpec((1, N_HEADS, HEAD_DIM, HEAD_DIM), lambda c: (n - 1 - c, 0, 0, 0)), blk(0), blk(C_HZ // BR_WIDTH),
                  wvec, blk(1)],
        out_specs=[blk(0), blk(0), blk(0), blk(0), vec, wvec],
        out_shape=[jax.ShapeDtypeStruct((s, BR_WIDTH), BF16)] * 4 + [jax.ShapeDtypeStruct((1, BR_WIDTH), F32),
                                                                    jax.ShapeDtypeStruct((1, HEAD_DIM), F32)],
        scratch_shapes=[pltpu.VMEM((N_HEADS, HEAD_DIM, HEAD_DIM), F32)],
        compiler_params=_cp("arbitrary"),
    )(proj, proj, proj, lb, qh, kh, lf, states, o, proj, norm_w.reshape(1, HEAD_DIM), dy)


_ANY = pl.BlockSpec(memory_space=pl.ANY)
_MESH = pl.DeviceIdType.MESH


def _all_gather(x_local, *, name, after=()):
    n_after = len(after)

    def body(x_ref, *refs):
        out_ref, send_sems, recv_sems, local_sem = refs[n_after:]
        x, y, c = lax.axis_index("x"), lax.axis_index("y"), lax.axis_index("c")
        me, sibling = (x, y, c), (x, y, 1 - c)
        n1 = (x ^ (1 - c), y ^ c)
        n2 = (x ^ c, y ^ (1 - c))
        dg = (1 - x, 1 - y)

        def slot(px, py, pc):
            return out_ref.at[4 * px + 2 * py + pc]

        def copy(k, block, to, src=None):
            return pltpu.make_async_remote_copy(
                src_ref=slot(*block) if src is None else src, dst_ref=slot(*block),
                send_sem=send_sems.at[k], recv_sem=recv_sems.at[k], device_id=to, device_id_type=_MESH)

        mine = pltpu.make_async_copy(x_ref, slot(*me), local_sem)
        mine.start()
        first = [copy(0, me, sibling, src=x_ref), copy(1, me, (*n1, c), src=x_ref), copy(2, me, (*n2, c), src=x_ref)]
        for cp in first:
            cp.start()
        copy(2, (*n2, c), me).wait_recv()
        forward = copy(3, (*n2, c), (*n1, c))
        forward.start()
        passed = [copy(5, (*n2, c), sibling)]
        passed[0].start()
        copy(1, (*n1, c), me).wait_recv()
        passed.append(copy(4, (*n1, c), sibling))
        passed[1].start()
        copy(3, (*dg, c), me).wait_recv()
        passed.append(copy(6, (*dg, c), sibling))
        passed[2].start()
        copy(0, sibling, me).wait_recv()
        copy(4, (*n2, 1 - c), me).wait_recv()
        copy(5, (*n1, 1 - c), me).wait_recv()
        copy(6, (*dg, 1 - c), me).wait_recv()
        for cp in first + [forward] + passed:
            cp.wait_send()
        mine.wait()

    return pl.pallas_call(
        body, name=name, out_shape=jax.ShapeDtypeStruct((N_DEV,) + x_local.shape, x_local.dtype),
        in_specs=[_ANY] * (1 + n_after), out_specs=_ANY,
        scratch_shapes=[pltpu.SemaphoreType.DMA((7,)), pltpu.SemaphoreType.DMA((7,)), pltpu.SemaphoreType.DMA],
    )(x_local, *after)


_HBM = pl.BlockSpec(memory_space=pltpu.HBM)
_SEM = pl.BlockSpec(memory_space=pltpu.SEMAPHORE)
_EFFECT = pltpu.SideEffectType.DATAFLOW_SIDE_EFFECTING


def _peers():
    x, y, c = lax.axis_index("x"), lax.axis_index("y"), lax.axis_index("c")
    out = []
    for k in range(1, N_DEV):
        px, py, pc = x ^ ((k >> 2) & 1), y ^ ((k >> 1) & 1), c ^ (k & 1)
        out.append(((px, py, pc), 4 * px + 2 * py + pc))
    return 4 * x + 2 * y + c, out


def _push_copies(src_ref, land_ref, send_sems, recv_sems, broadcast):
    my, peers = _peers()
    pairs = []
    for k, (pos, idx) in enumerate(peers):
        src = src_ref if broadcast else src_ref.at[idx]
        send = pltpu.make_async_remote_copy(src_ref=src, dst_ref=land_ref.at[my], send_sem=send_sems.at[k],
                                            recv_sem=recv_sems.at[k], device_id=pos, device_id_type=_MESH)
        recv = pltpu.make_async_remote_copy(src_ref=src, dst_ref=land_ref.at[idx], send_sem=send_sems.at[k],
                                            recv_sem=recv_sems.at[k], device_id=pos, device_id_type=_MESH)
        pairs.append((send, recv))
    return pairs


def _push_start(src, land, *, broadcast, name, after=()):
    n_after = len(after)

    def body(src_ref, land_ref, *refs):
        send_sems, recv_sems, _, _, token = refs[n_after:]
        for send, _ in _push_copies(src_ref, land_ref, send_sems, recv_sems, broadcast):
            send.start()
        token[...] = jnp.zeros_like(token)

    return pl.pallas_call(
        body, name=name,
        out_shape=(pltpu.SemaphoreType.DMA((N_DEV - 1,)), pltpu.SemaphoreType.DMA((N_DEV - 1,)),
                   pltpu.HBM(src.shape, src.dtype), pltpu.HBM(land.shape, land.dtype), jax.ShapeDtypeStruct((8, 128), F32)),
        in_specs=(_HBM, _HBM) + (_ANY,) * n_after, out_specs=(_SEM, _SEM, _HBM, _HBM, pl.BlockSpec(memory_space=pltpu.VMEM)),
        input_output_aliases={0: 2, 1: 3}, compiler_params=pltpu.CompilerParams(has_side_effects=_EFFECT),
    )(pltpu.with_memory_space_constraint(src, pltpu.HBM), pltpu.with_memory_space_constraint(land, pltpu.HBM), *after)


def _push_wait(handle, after, *, broadcast, name):
    send_sems, recv_sems, src_thru, land_thru, _ = handle

    def body(src_ref, land_ref, send_sems, recv_sems, *rest):
        for send, recv in _push_copies(src_ref, land_ref, send_sems, recv_sems, broadcast):
            send.wait_send()
            recv.wait_recv()

    return pl.pallas_call(
        body, name=name,
        out_shape=(pltpu.HBM(src_thru.shape, src_thru.dtype), pltpu.HBM(land_thru.shape, land_thru.dtype)),
        in_specs=(_HBM, _HBM, _SEM, _SEM) + (_ANY,) * len(after), out_specs=(_HBM, _HBM),
        input_output_aliases={0: 0, 1: 1}, compiler_params=pltpu.CompilerParams(has_side_effects=_EFFECT),
    )(src_thru, land_thru, send_sems, recv_sems, *after)[1]


def _relay_copies(src_ref, land_ref, sems_a, sems_b):
    x, y, c = lax.axis_index("x"), lax.axis_index("y"), lax.axis_index("c")
    slot = lambda px, py, pc: land_ref.at[4 * px + 2 * py + pc]
    chips = [(1 - x, y), (x, 1 - y), (1 - x, 1 - y)]
    (send_a, recv_a), (send_b, recv_b) = sems_a, sems_b

    def copy(sems, k, src, dst_slot, to):
        return pltpu.make_async_remote_copy(src_ref=src, dst_ref=dst_slot, send_sem=sems[0].at[k], recv_sem=sems[1].at[k],
                                            device_id=to, device_id_type=_MESH)

    first = [copy((send_a, recv_a), 0, src_ref, slot(x, y, c), (x, y, 1 - c))]
    first += [copy((send_a, recv_a), 1 + j, src_ref, slot(x, y, c), (*chip, c)) for j, chip in enumerate(chips)]
    first_in = [copy((send_a, recv_a), 0, src_ref, slot(x, y, 1 - c), (x, y, 1 - c))]
    first_in += [copy((send_a, recv_a), 1 + j, src_ref, slot(*chip, c), (*chip, c)) for j, chip in enumerate(chips)]
    relay = [copy((send_b, recv_b), j, slot(*chip, c), slot(*chip, c), (x, y, 1 - c)) for j, chip in enumerate(chips)]
    relay_in = [copy((send_b, recv_b), j, slot(*chip, 1 - c), slot(*chip, 1 - c), (x, y, 1 - c)) for j, chip in enumerate(chips)]
    return first, first_in, relay, relay_in


def _relay_start(src, land, *, name, after=()):
    n_after = len(after)

    def body(src_ref, land_ref, *refs):
        send_a, recv_a, _, _, token = refs[n_after:]
        for cp in _relay_copies(src_ref, land_ref, (send_a, recv_a), (send_a, recv_a))[0]:
            cp.start()
        token[...] = jnp.zeros_like(token)

    send_a, recv_a, src_thru, land_thru, token = pl.pallas_call(
        body, name=name,
        out_shape=(pltpu.SemaphoreType.DMA((4,)), pltpu.SemaphoreType.DMA((4,)), pltpu.HBM(src.shape, src.dtype),
                   pltpu.HBM(land.shape, land.dtype), jax.ShapeDtypeStruct((8, 128), F32)),
        in_specs=(_HBM, _HBM) + (_ANY,) * n_after, out_specs=(_SEM, _SEM, _HBM, _HBM, pl.BlockSpec(memory_space=pltpu.VMEM)),
        input_output_aliases={0: 2, 1: 3}, compiler_params=pltpu.CompilerParams(has_side_effects=_EFFECT),
    )(pltpu.with_memory_space_constraint(src, pltpu.HBM), pltpu.with_memory_space_constraint(land, pltpu.HBM), *after)
    return (send_a, recv_a), src_thru, land_thru, token


def _relay_mid(handle, after, *, name):
    sems_a, src_thru, land_thru, _ = handle
    n_after = len(after)

    def body(src_ref, land_ref, send_a, recv_a, *refs):
        send_b, recv_b, _, _, token = refs[n_after:]
        _, first_in, relay, _ = _relay_copies(src_ref, land_ref, (send_a, recv_a), (send_b, recv_b))
        for j in range(3):
            first_in[1 + j].wait_recv()
            relay[j].start()
        token[...] = jnp.zeros_like(token)

    send_b, recv_b, src2, land2, token = pl.pallas_call(
        body, name=name,
        out_shape=(pltpu.SemaphoreType.DMA((3,)), pltpu.SemaphoreType.DMA((3,)), pltpu.HBM(src_thru.shape, src_thru.dtype),
                   pltpu.HBM(land_thru.shape, land_thru.dtype), jax.ShapeDtypeStruct((8, 128), F32)),
        in_specs=(_HBM, _HBM, _SEM, _SEM) + (_ANY,) * n_after,
        out_specs=(_SEM, _SEM, _HBM, _HBM, pl.BlockSpec(memory_space=pltpu.VMEM)),
        input_output_aliases={0: 2, 1: 3}, compiler_params=pltpu.CompilerParams(has_side_effects=_EFFECT),
    )(src_thru, land_thru, *sems_a, *after)
    return sems_a, (send_b, recv_b), src2, land2, token


def _relay_wait(handle, after, *, name):
    sems_a, sems_b, src_thru, land_thru, _ = handle

    def body(src_ref, land_ref, send_a, recv_a, send_b, recv_b, *rest):
        first, first_in, relay, relay_in = _relay_copies(src_ref, land_ref, (send_a, recv_a), (send_b, recv_b))
        first_in[0].wait_recv()
        for cp in relay_in:
            cp.wait_recv()
        for cp in first + relay:
            cp.wait_send()

    return pl.pallas_call(
        body, name=name,
        out_shape=(pltpu.HBM(src_thru.shape, src_thru.dtype), pltpu.HBM(land_thru.shape, land_thru.dtype)),
        in_specs=(_HBM, _HBM, _SEM, _SEM, _SEM, _SEM) + (_ANY,) * len(after), out_specs=(_HBM, _HBM),
        input_output_aliases={0: 0, 1: 1}, compiler_params=pltpu.CompilerParams(has_side_effects=_EFFECT),
    )(src_thru, land_thru, *sems_a, *sems_b, *after)[1]


def _adamw(parts, row_off, w, m, v, *, layer=0, n_layers=1, prev=None, name, tr):
    rows, c = w.shape
    r = rows // n_layers
    np_ = parts.shape[0]
    tr = min(tr, r)
    assert r % tr == 0 and row_off % tr == 0
    ob, lb = row_off // tr, layer * (r // tr)
    c1 = 1.0 - ADAM_B1 ** ADAM_STEP
    c2 = 1.0 - ADAM_B2 ** ADAM_STEP
    n_prev = 0 if prev is None else 4

    def body(p_ref, w_ref, m_ref, v_ref, *refs):
        g_ref, d_ref, nm_ref, nv_ref = refs[n_prev:]
        g = p_ref[0].astype(F32)
        for s in range(1, np_):
            g = g + p_ref[s].astype(F32)
        wv = w_ref[...]
        m2 = ADAM_B1 * m_ref[...] + (1.0 - ADAM_B1) * g
        v2 = ADAM_B2 * v_ref[...] + (1.0 - ADAM_B2) * jnp.square(g)
        m_hat = m2 / c1
        v_hat = v2 / c2
        g_ref[...] = g
        d_ref[...] = -ADAM_LR * (m_hat / (jnp.sqrt(v_hat) + ADAM_EPS) + ADAM_WD * wv)
        nm_ref[...] = m2
        nv_ref[...] = v2

    blk = pl.BlockSpec((tr, c), lambda i: (lb + i, 0))
    return pl.pallas_call(
        body, name=name, grid=(r // tr,),
        in_specs=[pl.BlockSpec((np_, tr, c), lambda i: (0, ob + i, 0)), blk, blk, blk] + [_ANY] * n_prev,
        out_specs=[blk] * 4, out_shape=[jax.ShapeDtypeStruct((rows, c), F32)] * 4,
        input_output_aliases={4 + i: i for i in range(n_prev)}, compiler_params=_cp("parallel"),
    )(parts, w, m, v, *(prev or ()))


def _sum_parts(parts, *, name, after=()):
    np_, r, c = parts.shape

    def body(p_ref, *refs):
        o_ref = refs[-1]
        g = p_ref[0]
        for s in range(1, np_):
            g = g + p_ref[s]
        o_ref[...] = g

    vmem = pl.BlockSpec(memory_space=pltpu.VMEM)
    return pl.pallas_call(body, name=name, in_specs=[vmem] + [_ANY] * len(after), out_specs=vmem,
                          out_shape=jax.ShapeDtypeStruct((r, c), F32))(parts, *after)


def _pack(arrs):
    rows = []
    for a in arrs:
        f = a.reshape(-1).astype(F32)
        pad = (-f.shape[0]) % 128
        rows.append(jnp.pad(f, (0, pad)).reshape(-1, 128))
    out = jnp.concatenate(rows, axis=0)
    return jnp.pad(out, ((0, (-out.shape[0]) % 8), (0, 0)))


def _unpack(packed, shapes):
    outs, r0 = [], 0
    for shp in shapes:
        n = 1
        for d in shp:
            n *= d
        nr = -(-n // 128)
        outs.append(packed[r0:r0 + nr].reshape(-1)[:n].reshape(shp))
        r0 += nr
    return outs


_WIN_PIECES = ((0, 4096, 0), (4112, 8208, 0), (4096, 4104, HEAD_DIM - N_HEADS), (4104, 4112, HEAD_DIM - N_HEADS))


RELAYOUT_TILE = 256
LAST_SPLIT = 4
OTHER_SPLIT = 2


def _win_from_shards(shards, *, name):
    k = shards.shape[1]
    tr = min(RELAYOUT_TILE, k)

    def body(x_ref, o_ref):
        cols = []
        for lo, hi, pad in _WIN_PIECES:
            for j in range(N_DEV):
                a, b = max(lo, j * SHARD_IN), min(hi, (j + 1) * SHARD_IN)
                if a < b:
                    cols.append(x_ref[j, :, a - j * SHARD_IN:b - j * SHARD_IN])
            if pad:
                cols.append(jnp.zeros((tr, pad), x_ref.dtype))
        o_ref[...] = jnp.concatenate(cols, axis=1)

    return pl.pallas_call(
        body, name=name, grid=(k // tr,), in_specs=[pl.BlockSpec((N_DEV, tr, SHARD_IN), lambda i: (0, i, 0))],
        out_specs=pl.BlockSpec((tr, N_PROJ), lambda i: (i, 0)), out_shape=jax.ShapeDtypeStruct((k, N_PROJ), shards.dtype),
        compiler_params=_cp("parallel"),
    )(shards)


def _win_to_shards(g, *, name):
    k = g.shape[0]
    tr = min(RELAYOUT_TILE, k)
    starts, off = [], 0
    for lo, hi, pad in _WIN_PIECES:
        starts.append((lo, hi, off))
        off += hi - lo + pad

    def body(g_ref, o_ref):
        for j in range(N_DEV):
            cols = []
            for lo, hi, off in sorted(starts):
                a, b = max(lo, j * SHARD_IN), min(hi, (j + 1) * SHARD_IN)
                if a < b:
                    cols.append(g_ref[:, off + a - lo:off + b - lo])
            o_ref[j] = jnp.concatenate(cols, axis=1)

    return pl.pallas_call(
        body, name=name, grid=(k // tr,), in_specs=[pl.BlockSpec((tr, N_PROJ), lambda i: (i, 0))],
        out_specs=pl.BlockSpec((N_DEV, tr, SHARD_IN), lambda i: (0, i, 0)),
        out_shape=jax.ShapeDtypeStruct((N_DEV, k, SHARD_IN), g.dtype), compiler_params=_cp("parallel"),
    )(g)


def _lower_bounds(logits):
    probs = jax.nn.softmax(logits.astype(F32), axis=0)
    return jnp.cumsum(probs, axis=0) - probs[0]


def _pad_lanes(vec8):
    return jnp.pad(vec8.reshape(1, N_HEADS), ((0, 0), (0, HEAD_DIM - N_HEADS)))


def kernel(x, p, norm_w, w_in, dn_conv_w, dn_A_log, dn_dt_bias, dn_norm_w, hg_lb_logits, hg_norm_w, w_out, w_ple_up, w_ple_gate, final_norm_w, loss_target, m_norm_w, m_w_in, m_dn_conv_w, m_dn_A_log, m_dn_dt_bias, m_dn_norm_w, m_hg_lb_logits, m_hg_norm_w, m_w_out, m_w_ple_up, m_w_ple_gate, m_final_norm_w, v_norm_w, v_w_in, v_dn_conv_w, v_dn_A_log, v_dn_dt_bias, v_dn_norm_w, v_hg_lb_logits, v_hg_norm_w, v_w_out, v_w_ple_up, v_w_ple_gate, v_final_norm_w):
    depth = norm_w.shape[0]
    my = 4 * lax.axis_index("x") + 2 * lax.axis_index("y") + lax.axis_index("c")
    h = x[0]
    tgt = loss_target[0]
    rows_out = D_MODEL // N_DEV
    up_rows = PLE_DIM * (D_MODEL // N_DEV) // D_MODEL
    g_off, u_off = rows_out, 2 * rows_out

    def own_slot(block):
        return lax.dynamic_update_index_in_dim(lax.empty((N_DEV,) + block.shape, block.dtype), block, my, 0)

    win_bf = w_in.astype(BF16)
    rest_bf = [jnp.concatenate([w_out[l], w_ple_gate[l], w_ple_up[l].reshape(up_rows, D_MODEL)], axis=0).astype(BF16)
               for l in range(depth)]
    conv_push = _push_start(dn_conv_w, own_slot(dn_conv_w), broadcast=True, name="gather_conv_w_start")
    win_all = {0: _all_gather(win_bf[0], name="gather_w_in_l0", after=[conv_push[4]])}
    pending, relayed = {}, {}
    last = win_all[0]
    for l in range(depth):
        if l > 0:
            relayed["win", l] = _relay_start(win_bf[l], own_slot(win_bf[l]), after=[last], name=f"gather_w_in_l{l}_first")
            last = relayed["win", l][3]
        if l == 0:
            relayed["rest", l] = _relay_start(rest_bf[l], own_slot(rest_bf[l]), after=[last], name=f"gather_rest_l{l}_first")
            last = relayed["rest", l][3]
        else:
            pending["rest", l] = _push_start(rest_bf[l], own_slot(rest_bf[l]), broadcast=True, after=[last],
                                             name=f"gather_rest_l{l}_start")
            last = pending["rest", l][4]
    order_tok = last[0, 0]
    lbs = _lower_bounds(hg_lb_logits)

    saved = []
    weights = []
    for l in range(depth):
        tag = f"l{l}"
        if l > 0:
            win_all[l] = _relay_wait(relayed["win", l], [h], name=f"gather_w_in_{tag}_wait")
        wi = _win_from_shards(win_all[l], name=f"w_in_layout_{tag}")
        nw = norm_w[l] + order_tok if l == 0 else norm_w[l]
        hn, hn_t = _rms_fwd(h, nw, name=f"rms_fwd_{tag}")
        proj = _mm(hn, wi, mode="nn", out_dtype=F32, name=f"mm_proj_{tag}")
        al, dt = _pad_lanes(dn_A_log[l]), _pad_lanes(dn_dt_bias[l])
        if l == 0:
            conv_all = _push_wait(conv_push, [proj], broadcast=True, name="gather_conv_w_wait")
            conv_full = conv_all.transpose(1, 2, 0, 3).reshape(depth, CONV_W, 3 * BR_WIDTH)
        qkv = _dn_qkv_fwd(proj, conv_full[l], name=f"dn_qkv_fwd_{tag}")
        if ("rest", l) in relayed:
            relayed["rest", l] = _relay_mid(relayed["rest", l], [qkv], name=f"gather_rest_{tag}_relay")
            al = al + relayed["rest", l][4][0, 0]
        beta, gcs = _dn_gate_fwd(proj, al, dt, name=f"dn_gate_fwd_{tag}")
        o_dn, st_dn, tinv_dn, y_dn = _dn_chunk_fwd(qkv, gcs, beta, proj, dn_norm_w[l], name=f"dn_chunk_fwd_{tag}")
        lb = lbs[l].reshape(1, BR_WIDTH)
        o_hg, st_hg, qh, kh, lf, y_hg = _hg_chunk_fwd(proj, lb, hg_norm_w[l], name=f"hg_chunk_fwd_{tag}")
        y = jnp.concatenate([y_dn, y_hg], axis=1)
        if ("rest", l) in relayed:
            rest_all = _relay_wait(relayed["rest", l], [y], name=f"gather_rest_{tag}_wait")
        else:
            rest_all = _push_wait(pending["rest", l], [y], broadcast=True, name=f"gather_rest_{tag}_wait")
        w_out_rows, w_gate_rows = (0, rows_out), (g_off, rows_out)
        wu = rest_all[:, u_off:u_off + up_rows].reshape(N_DEV, PLE_DIM, D_MODEL // N_DEV).transpose(1, 0, 2).reshape(PLE_DIM, D_MODEL)
        weights.append((wi, rest_all, wu))
        h1 = _mm(y, rest_all, mode="nn", b_rows=w_out_rows, out_dtype=F32, res=h, name=f"mm_out_{tag}")
        pin = []
        if ("win", l + 1) in relayed:
            relayed["win", l + 1] = _relay_mid(relayed["win", l + 1], [h1], name=f"gather_w_in_l{l + 1}_relay")
            pin = [relayed["win", l + 1][4]]
        up = _mm(p[l, 0], wu, mode="nn", out_dtype=F32, name=f"mm_up_{tag}")
        gp, h2 = _mm(h1, rest_all, mode="nn", b_rows=w_gate_rows, out_dtype=F32, after=pin, tile_n=512, fused=("ple", h1, up),
                     name=f"mm_gate_{tag}")
        saved.append(dict(h=h, hn_t=hn_t, proj=proj, qkv=qkv, beta=beta, gcs=gcs, st_dn=st_dn, tinv_dn=tinv_dn, qh=qh, kh=kh, lf=lf,
                          st_hg=st_hg, o_dn=o_dn, o_hg=o_hg, y=y, h1=h1, gp=gp, up=up, al=al, dt=dt, lb=lb))
        h = h2

    loss_row, dh, d_final_w = _final_fwd_bwd(h, final_norm_w, tgt, name="final_norm_loss")

    d_norm_w, d_alog, d_dt, d_dn_nw, d_hg_nw, d_lb, d_conv = ([None] * depth for _ in range(7))
    sent = {}
    for l in reversed(range(depth)):
        wi, rest_all, wu = weights[l]
        sv = saved[l]
        tag = f"l{l}"
        dup, dgp = _ple_bwd(dh, sv["gp"], sv["up"], name=f"ple_bwd_{tag}")
        d_wu = _mm(p[l, 0], dup, mode="tn", out_dtype=BF16, name=f"mm_dwup_{tag}")
        d_wg = _mm(sv["h1"], dgp, mode="tn", out_dtype=BF16, name=f"mm_dwgate_{tag}")
        dh1 = _mm(dgp, rest_all, mode="nt", b_rows=(g_off, rows_out), out_dtype=F32, res=dh, name=f"mm_dh1_{tag}")
        d_wo = _mm(sv["y"], dh1, mode="tn", out_dtype=BF16, name=f"mm_dwout_{tag}")
        parts_rest = jnp.concatenate(
            [d_wo.reshape(N_DEV, rows_out, D_MODEL), d_wg.reshape(N_DEV, rows_out, D_MODEL),
             d_wu.reshape(PLE_DIM, N_DEV, D_MODEL // N_DEV).transpose(1, 0, 2).reshape(N_DEV, up_rows, D_MODEL)], axis=1)
        sent["rest", l] = _push_start(parts_rest, own_slot(parts_rest[my]), broadcast=False, name=f"exchange_rest_{tag}_start")
        dy = _mm(dh1, rest_all, mode="nt", b_rows=(0, rows_out), out_dtype=F32, name=f"mm_dy_{tag}")
        dn_nw = dn_norm_w[l] + sent["rest", l][4][0, 0]
        dqkv, d_gc, dbeta, dz_dn, d_dn_nw[l] = _dn_chunk_bwd(sv["qkv"], sv["gcs"], sv["beta"], sv["st_dn"], sv["tinv_dn"],
                                                             sv["o_dn"], sv["proj"], dn_nw, dy, name=f"dn_chunk_bwd_{tag}")
        dqkv_pre, d_conv[l] = _dn_qkv_bwd(sv["proj"], conv_full[l], dqkv, name=f"dn_qkv_bwd_{tag}")
        db, da, d_alog[l], d_dt[l] = _dn_gate_bwd(sv["proj"], sv["al"], sv["dt"], dbeta, d_gc, name=f"dn_gate_bwd_{tag}")
        dhq, dhf, dhi, dz_hg, d_lb[l], d_hg_nw[l] = _hg_chunk_bwd(sv["proj"], sv["lb"], sv["qh"], sv["kh"], sv["lf"], sv["st_hg"],
                                                                  sv["o_hg"], hg_norm_w[l], dy, name=f"hg_chunk_bwd_{tag}")
        dproj = jnp.concatenate([dqkv_pre, dz_dn, dhq, dhf, dhi, dz_hg, db, da], axis=1)
        def push_d_win(after):
            n_split = LAST_SPLIT if l == 0 else OTHER_SPLIT
            rows = D_MODEL // n_split
            handles = []
            for q in range(n_split):
                hn_q = sv["hn_t"] if n_split == 1 else sv["hn_t"][q * rows:(q + 1) * rows]
                sfx = tag if n_split == 1 else f"{tag}_{q}"
                d_win = _mm(hn_q, dproj, mode="nn", out_dtype=BF16, after=after, name=f"mm_dwin_{sfx}")
                parts_in = _win_to_shards(d_win, name=f"dw_in_shards_{sfx}")
                handles.append(_push_start(parts_in, own_slot(parts_in[my]), broadcast=False, after=after,
                                           name=f"exchange_w_in_{sfx}_start"))
                after = [handles[-1][4]]
            return handles

        if l == 0:
            small = _pack([loss_row, jnp.concatenate(d_norm_w[1:], axis=0), d_final_w,
                           jnp.stack([a[0, :N_HEADS] for a in d_alog]), jnp.stack([a[0, :N_HEADS] for a in d_dt]),
                           jnp.concatenate(d_dn_nw, axis=0), jnp.concatenate(d_hg_nw, axis=0), jnp.concatenate(d_lb, axis=0),
                           jnp.stack(d_conv)])
            small_all = _all_gather(small, name="gather_small")
        sent["win", l] = push_d_win([small_all] if l == 0 else [])
        dh, d_norm_w[l] = _mm(dproj, wi, mode="nt", out_dtype=F32, tile_m=512, tile_n=D_MODEL, tile_k=768,
                              after=[sent["win", l][-1][4]],
                              fused=("rms_bwd", sv["h"], norm_w[l], dh1), name=f"mm_dhn_{tag}")
    grad_x = dh[None]

    small_shapes = [(1, 128), (depth - 1, D_MODEL), final_norm_w.shape, dn_A_log.shape, dn_dt_bias.shape, dn_norm_w.shape,
                    hg_norm_w.shape, hg_lb_logits.shape, (depth, CONV_W, 3 * BR_WIDTH)]
    tot = _unpack(_sum_parts(small_all, after=[grad_x], name="sum_small"), small_shapes)
    loss = tot[0][0, 0]
    g_lb = tot[7]
    g_logits = jax.vjp(_lower_bounds, hg_lb_logits)[1](g_lb)[0]
    g_conv = lax.dynamic_slice_in_dim(tot[8], my * (3 * BR_WIDTH // N_DEV), 3 * BR_WIDTH // N_DEV, axis=2)
    small_g = [g_conv, tot[3], tot[4], tot[5], g_logits, tot[6], tot[2]]
    small_w = [dn_conv_w, dn_A_log, dn_dt_bias, dn_norm_w, hg_lb_logits, hg_norm_w, final_norm_w]
    small_m = [m_dn_conv_w, m_dn_A_log, m_dn_dt_bias, m_dn_norm_w, m_hg_lb_logits, m_hg_norm_w, m_final_norm_w]
    small_v = [v_dn_conv_w, v_dn_A_log, v_dn_dt_bias, v_dn_norm_w, v_hg_lb_logits, v_hg_norm_w, v_final_norm_w]
    pk_w = _pack(small_w)
    res_small = _adamw(_pack(small_g)[None], 0, pk_w, _pack(small_m), _pack(small_v), name="adamw_small", tr=pk_w.shape[0])
    shapes_w = [a.shape for a in small_w]
    sg, sd, sm, sv_ = (_unpack(r, shapes_w) for r in res_small)

    r_win = r_wo = r_wg = r_wu = None
    done = [grad_x, res_small[0]]

    def flat(a, cols):
        return a.reshape(-1, cols)

    for l in reversed(range(depth)):
        tag = f"l{l}"
        land_rest = _push_wait(sent["rest", l], done, broadcast=False, name=f"exchange_rest_{tag}_wait")
        r_wo = _adamw(land_rest, 0, flat(w_out, D_MODEL), flat(m_w_out, D_MODEL), flat(v_w_out, D_MODEL), layer=l,
                      n_layers=depth, prev=r_wo, name=f"adamw_w_out_{tag}", tr=rows_out)
        r_wg = _adamw(land_rest, g_off, flat(w_ple_gate, D_MODEL), flat(m_w_ple_gate, D_MODEL), flat(v_w_ple_gate, D_MODEL),
                      layer=l, n_layers=depth, prev=r_wg, name=f"adamw_w_gate_{tag}", tr=rows_out)
        r_wu = _adamw(land_rest, u_off, flat(w_ple_up, D_MODEL), flat(m_w_ple_up, D_MODEL), flat(v_w_ple_up, D_MODEL),
                      layer=l, n_layers=depth, prev=r_wu, name=f"adamw_w_up_{tag}", tr=up_rows)
        done = [r_wo[0], r_wg[0], r_wu[0]]
    for l in reversed(range(depth)):
        tag = f"l{l}"
        if l == 0:
            nw0 = _sum_parts(_all_gather(_pack([d_norm_w[0]]), after=done, name="gather_norm_w"), name="sum_norm_w")
            g_norm_w = jnp.concatenate([_unpack(nw0, [(1, D_MODEL)])[0], tot[1]], axis=0)
            pk_nw = _pack([norm_w])
            r_nw = _adamw(_pack([g_norm_w])[None], 0, pk_nw, _pack([m_norm_w]), _pack([v_norm_w]), name="adamw_norm_w",
                          tr=pk_nw.shape[0])
            r_nw = [_unpack(r, [norm_w.shape])[0] for r in r_nw]
            done = [r_nw[0]]
        n_split = len(sent["win", l])
        for q, handle in enumerate(sent["win", l]):
            sfx = tag if n_split == 1 else f"{tag}_{q}"
            land_in = _push_wait(handle, done, broadcast=False, name=f"exchange_w_in_{sfx}_wait")
            r_win = _adamw(land_in, 0, flat(w_in, SHARD_IN), flat(m_w_in, SHARD_IN), flat(v_w_in, SHARD_IN),
                           layer=l * n_split + q, n_layers=depth * n_split, prev=r_win, name=f"adamw_w_in_{sfx}", tr=256)
            done = [r_win[0]]
    r_win = [o.reshape(w_in.shape) for o in r_win]
    r_wo = [o.reshape(w_out.shape) for o in r_wo]
    r_wg = [o.reshape(w_ple_gate.shape) for o in r_wg]
    r_wu = [o.reshape(w_ple_up.shape) for o in r_wu]

    def order(nw, small_list, big_in, big_out, big_up, big_gate):
        cw, al_, dt_, dnw, lbl, hnw, fw = small_list
        return [nw, big_in, cw, al_, dt_, dnw, lbl, hnw, big_out, big_up, big_gate, fw]

    outs = [loss, grad_x]
    for i, sl in enumerate((sg, sd, sm, sv_)):
        outs += order(r_nw[i], sl, r_win[i], r_wo[i], r_wu[i], r_wg[i])
    return tuple(outs)
```

```python
import functools

import jax
import jax.numpy as jnp
from jax import lax
from jax.experimental import pallas as pl
from jax.experimental.pallas import tpu as pltpu

F32 = jnp.float32
BF16 = jnp.bfloat16
HIGHEST = lax.Precision.HIGHEST

N_DEV = 8
D_MODEL = 2048
PLE_DIM = 256
HEAD_DIM = 128
N_HEADS = 8
BR_WIDTH = N_HEADS * HEAD_DIM
CHUNK = 64
SUB = 16
CONV_W = 4
NORM_EPS = 1e-6
L2_EPS = 1e-6
IN_WIDTH = 8208
SHARD_IN = IN_WIDTH // N_DEV
EXP_CLAMP = 80.0

C_QKV, C_Z, C_HQ, C_HF, C_HI, C_HZ, C_B, C_A, N_PROJ = 0, 3072, 4096, 5120, 6144, 7168, 8192, 8320, 8448

ADAM_LR, ADAM_B1, ADAM_B2, ADAM_EPS, ADAM_WD, ADAM_STEP = 0.001, 0.9, 0.999, 1e-08, 0.01, 10

VMEM_LIMIT = 48 * 1024 * 1024


def _cp(*sem):
    return pltpu.CompilerParams(dimension_semantics=sem, vmem_limit_bytes=VMEM_LIMIT)


class _Heads:
    def __init__(self, vals):
        self.v = tuple(vals)

    def __add__(self, o):
        return _hmap(lambda a, b: a + b, self, o)

    def __radd__(self, o):
        return _hmap(lambda a, b: b + a, self, o)

    def __sub__(self, o):
        return _hmap(lambda a, b: a - b, self, o)

    def __rsub__(self, o):
        return _hmap(lambda a, b: b - a, self, o)

    def __mul__(self, o):
        return _hmap(lambda a, b: a * b, self, o)

    def __rmul__(self, o):
        return _hmap(lambda a, b: b * a, self, o)

    def __neg__(self):
        return _hmap(lambda a: -a, self)

    def __getitem__(self, idx):
        return _hmap(lambda a: a[idx], self)


def _hmap(fn, *args):
    n = next((len(a.v) for a in args if isinstance(a, _Heads)), None)
    if n is None:
        return fn(*args)
    return _Heads(fn(*[a.v[i] if isinstance(a, _Heads) else a for a in args]) for i in range(n))


def _dot(a, b, ca, cb):
    return _hmap(lambda x, y: lax.dot_general(x.astype(BF16), y.astype(BF16), (((ca,), (cb,)), ((), ())),
                                              preferred_element_type=F32), a, b)


def _nn(a, b):
    return _dot(a, b, 1, 0)


def _nt(a, b):
    return _dot(a, b, 1, 1)


def _tn(a, b):
    return _dot(a, b, 0, 0)


def _split(a):
    hi = _hmap(lambda x: x.astype(BF16), a)
    return hi, _hmap(lambda x, h: (x - h.astype(F32)).astype(BF16), a, hi)


def _dot3(a, b, ca, cb):
    ah, al = _split(a)
    bh, bl = _split(b)
    return _dot(ah, bh, ca, cb) + (_dot(ah, bl, ca, cb) + _dot(al, bh, ca, cb))


def _nn_exact(a, b):
    return _hmap(lambda y: lax.dot_general(a, y, (((1,), (0,)), ((), ())), precision=HIGHEST,
                                           preferred_element_type=F32), b)


def _exp(x):
    return _hmap(jnp.exp, x)


def _sum(x, axis):
    return _hmap(lambda a: jnp.sum(a, axis=axis, keepdims=True), x)


def _sigmoid(x):
    return jax.nn.sigmoid(x)


def _silu(x):
    return x * _sigmoid(x)


def _dsilu(x):
    s = _sigmoid(x)
    return s * (1.0 + x * (1.0 - s))


def _silu_and_grad(x):
    s = _sigmoid(x)
    return x * s, s * (1.0 + x * (1.0 - s))


def _softplus(x):
    return jnp.maximum(x, 0.0) + jnp.log(1.0 + jnp.exp(-jnp.abs(x)))


def _iota2(n, m, axis):
    return lax.broadcasted_iota(jnp.int32, (n, m), axis)


def _col2row(col, eye):
    return _hmap(lambda c: jnp.sum(eye * c, axis=0, keepdims=True), col)


def _row2col(row, eye):
    return _hmap(lambda r: jnp.sum(eye * r, axis=1, keepdims=True), row)


def _pick_lane(block, lane_idx):
    lane = _iota2(block.shape[0], block.shape[1], 1)
    return jnp.sum(jnp.where(lane == lane_idx, block, 0.0), axis=1, keepdims=True)


MM_TILE_M, MM_TILE_N, MM_TILE_K = 1024, 1408, 2048


def _tile(dim, cap):
    if dim <= cap:
        return dim
    t = cap - cap % 128
    while dim % t:
        t -= 128
    return t


def _mm(a, b, *, mode, out_dtype, res=None, after=(), b_rows=None, tile_m=MM_TILE_M, tile_n=MM_TILE_N, tile_k=MM_TILE_K,
        fused=None, name):
    b_mat_rows = b.shape[0] if b_rows is None else N_DEV * b_rows[1]
    if mode == "nn":
        (m, kd), n = a.shape, b.shape[-1]
        assert kd == b_mat_rows
    elif mode == "nt":
        (m, kd), n = a.shape, b_mat_rows
    else:
        (kd, m), n = a.shape, b.shape[-1]
    tm, tn, tk = _tile(m, tile_m), _tile(n, tile_n), _tile(kd, tile_k)
    assert m % tm == 0 and n % tn == 0 and kd % tk == 0, (m, n, kd, tm, tn, tk)
    nk = kd // tk
    ca, cb = {"nn": (1, 0), "nt": (1, 1), "tn": (0, 0)}[mode]

    kind = None if fused is None else fused[0]
    n_in = 2 + (res is not None) + (0 if fused is None else len(fused) - 1)
    n_out = 1 if fused is None else 2

    def body(*refs):
        a_ref, b_ref = refs[:2]
        r_ref = None if res is None else refs[2]
        extra = refs[2 + (res is not None):n_in]
        first_out = n_in + len(after)
        outs = refs[first_out:first_out + n_out]
        o_ref = outs[0]
        k = pl.program_id(2)

        def finish(out):
            if r_ref is not None:
                out = out + r_ref[...].astype(F32)
            if kind == "ple":
                h1_ref, up_ref = extra
                o_ref[...] = out
                outs[1][...] = h1_ref[...] + up_ref[...] * _sigmoid(out)
            elif kind == "rms_bwd":
                h_ref, w_ref, res_ref = extra
                dx, dwt = _rms_bwd_math(h_ref[...], w_ref[...], out)
                o_ref[...] = res_ref[...] + dx

                @pl.when(pl.program_id(0) == 0)
                def _():
                    outs[1][...] = jnp.zeros_like(outs[1])

                outs[1][...] += jnp.sum(dwt, axis=0, keepdims=True)
            else:
                o_ref[...] = out.astype(o_ref.dtype)

        b_tile = b_ref[...]
        if b_rows is not None:
            b_tile = b_tile.reshape(-1, b_tile.shape[-1])
        prod = _dot(a_ref[...], b_tile, ca, cb)
        if nk == 1:
            finish(prod)
            return
        acc_ref = refs[-1]

        @pl.when(k == 0)
        def _():
            acc_ref[...] = jnp.zeros_like(acc_ref)

        acc_ref[...] += prod

        @pl.when(k == nk - 1)
        def _():
            finish(acc_ref[...])

    a_spec = pl.BlockSpec((tk, tm), lambda i, j, k: (k, i)) if mode == "tn" else pl.BlockSpec((tm, tk), lambda i, j, k: (i, k))
    if b_rows is None:
        b_spec = pl.BlockSpec((tn, tk), lambda i, j, k: (j, k)) if mode == "nt" else pl.BlockSpec((tk, tn), lambda i, j, k: (k, j))
    else:
        first, count = b_rows
        assert first % count == 0 and mode in ("nn", "nt")
        rb = first // count
        if mode == "nn":
            assert tk == kd
            b_spec = pl.BlockSpec((N_DEV, count, tn), lambda i, j, k: (0, rb, j))
        else:
            assert tn % count == 0
            b_spec = pl.BlockSpec((tn // count, count, tk), lambda i, j, k: (j, rb, k))
    o_spec = pl.BlockSpec((tm, tn), lambda i, j, k: (i, j))
    row_spec = pl.BlockSpec((1, tn), lambda i, j, k: (0, j))
    extra_specs, extra_args, out_specs, out_shape = [], (), o_spec, jax.ShapeDtypeStruct((m, n), out_dtype)
    sem = ("parallel", "parallel", "arbitrary")
    if kind == "ple":
        extra_specs, extra_args = [o_spec, o_spec], tuple(fused[1:])
        out_specs, out_shape = [o_spec, o_spec], [jax.ShapeDtypeStruct((m, n), F32)] * 2
    elif kind == "rms_bwd":
        assert tn == n
        extra_specs, extra_args = [o_spec, row_spec, o_spec], (fused[1], fused[2].reshape(1, n), fused[3])
        out_specs, out_shape = [o_spec, row_spec], [jax.ShapeDtypeStruct((m, n), F32), jax.ShapeDtypeStruct((1, n), F32)]
        sem = ("arbitrary", "arbitrary", "arbitrary")
    in_specs = ([a_spec, b_spec] + ([o_spec] if res is not None else []) + extra_specs
                + [pl.BlockSpec(memory_space=pl.ANY)] * len(after))
    args = (a, b) + ((res,) if res is not None else ()) + extra_args + tuple(after)
    return pl.pallas_call(
        body, name=name, grid=(m // tm, n // tn, nk), in_specs=in_specs, out_specs=out_specs, out_shape=out_shape,
        scratch_shapes=[pltpu.VMEM((tm, tn), F32)] if nk > 1 else [], compiler_params=_cp(*sem),
    )(*args)


ROW_TILE = 256


def _rms_fwd(h, w, *, name):
    s, d = h.shape
    tr = min(ROW_TILE, s)

    def body(h_ref, w_ref, o_ref):
        x = h_ref[...]
        r = lax.rsqrt(jnp.mean(x * x, axis=-1, keepdims=True) + NORM_EPS)
        o_ref[...] = (x * r * w_ref[...]).astype(o_ref.dtype)

    return pl.pallas_call(
        body, name=name, grid=(s // tr,),
        in_specs=[pl.BlockSpec((tr, d), lambda i: (i, 0)), pl.BlockSpec((1, d), lambda i: (0, 0))],
        out_specs=pl.BlockSpec((tr, d), lambda i: (i, 0)),
        out_shape=jax.ShapeDtypeStruct((s, d), BF16), compiler_params=_cp("parallel"),
    )(h, w.reshape(1, d))


def _rms_bwd_math(x, w, dy):
    d = x.shape[-1]
    r = lax.rsqrt(jnp.mean(x * x, axis=-1, keepdims=True) + NORM_EPS)
    gw = dy * w
    dx = r * gw - x * ((r * r * r) * (jnp.sum(gw * x, axis=-1, keepdims=True) / d))
    return dx, dy * x * r


def _final_fwd_bwd(h, w, tgt, *, name):
    s, d = h.shape
    tr = min(ROW_TILE, s)

    def body(h_ref, w_ref, t_ref, loss_ref, dh_ref, dw_ref):
        @pl.when(pl.program_id(0) == 0)
        def _():
            loss_ref[...] = jnp.zeros_like(loss_ref)
            dw_ref[...] = jnp.zeros_like(dw_ref)

        x = h_ref[...]
        wv = w_ref[...]
        r = lax.rsqrt(jnp.mean(x * x, axis=-1, keepdims=True) + NORM_EPS)
        err = x * r * wv - t_ref[...]
        row_loss = jnp.mean(err * err, axis=-1, keepdims=True)
        loss_ref[...] += 0.5 * jnp.sum(row_loss, axis=0, keepdims=True)
        dx, dwt = _rms_bwd_math(x, wv, err / d)
        dh_ref[...] = dx
        dw_ref[...] += jnp.sum(dwt, axis=0, keepdims=True)

    row = pl.BlockSpec((tr, d), lambda i: (i, 0))
    vec = pl.BlockSpec((1, d), lambda i: (0, 0))
    return pl.pallas_call(
        body, name=name, grid=(s // tr,), in_specs=[row, vec, row],
        out_specs=[pl.BlockSpec((1, 128), lambda i: (0, 0)), row, vec],
        out_shape=[jax.ShapeDtypeStruct((1, 128), F32), jax.ShapeDtypeStruct((s, d), F32),
                   jax.ShapeDtypeStruct((1, d), F32)],
        compiler_params=_cp("arbitrary"),
    )(h, w.reshape(1, d), tgt)


def _ple_bwd(dh2, gate_pre, up, *, name):
    s, d = dh2.shape
    tr = min(ROW_TILE, s)

    def body(d_ref, g_ref, u_ref, dup_ref, dgp_ref):
        dh = d_ref[...]
        gate = _sigmoid(g_ref[...])
        dup_ref[...] = (dh * gate).astype(BF16)
        dgp_ref[...] = (dh * u_ref[...] * gate * (1.0 - gate)).astype(BF16)

    row = pl.BlockSpec((tr, d), lambda i: (i, 0))
    return pl.pallas_call(body, name=name, grid=(s // tr,), in_specs=[row, row, row], out_specs=[row, row],
                          out_shape=[jax.ShapeDtypeStruct((s, d), BF16)] * 2, compiler_params=_cp("parallel"))(dh2, gate_pre, up)


def _head_norm_fwd(o, z, w):
    return _hmap(lambda x, zz: (x * lax.rsqrt(jnp.mean(x * x, axis=-1, keepdims=True) + NORM_EPS) * w * _silu(zz)).astype(BF16),
                 o, z)


def _head_norm_bwd(o, z, w, dy):
    dos, dzs, dw = [], [], jnp.zeros((1, HEAD_DIM), F32)
    for x, zz, g in zip(o.v, z.v, dy.v):
        r = lax.rsqrt(jnp.mean(x * x, axis=-1, keepdims=True) + NORM_EPS)
        silu_z, dsilu_z = _silu_and_grad(zz)
        don = g * silu_z
        dzs.append((g * (x * r * w) * dsilu_z).astype(BF16))
        gw = don * w
        dos.append(r * gw - x * ((r * r * r) * (jnp.sum(gw * x, axis=-1, keepdims=True) / HEAD_DIM)))
        dw = dw + jnp.sum(don * x * r, axis=0, keepdims=True)
    return _Heads(dos), _Heads(dzs), dw


def _conv_silu(x, w, s):
    row = _iota2(s, x.shape[1], 0)
    c = w[CONV_W - 1:CONV_W, :] * x
    for k in range(1, CONV_W):
        c = c + w[CONV_W - 1 - k:CONV_W - k, :] * jnp.where(row >= k, pltpu.roll(x, k, 0), 0.0)
    return c


def _dn_qkv_fwd(proj, conv_w, *, name):
    s = proj.shape[0]
    nb = 3 * N_HEADS

    def body(x_ref, w_ref, o_ref):
        j = pl.program_id(0)
        sv = _silu(_conv_silu(x_ref[...], w_ref[...], s))
        r = lax.rsqrt(jnp.sum(sv * sv, axis=-1, keepdims=True) + L2_EPS)
        scale = jnp.where(j < N_HEADS, HEAD_DIM ** -0.5, 1.0).astype(F32)
        o_ref[...] = jnp.where(j < 2 * N_HEADS, sv * r * scale, sv)

    return pl.pallas_call(
        body, name=name, grid=(nb,),
        in_specs=[pl.BlockSpec((s, HEAD_DIM), lambda j: (0, j)), pl.BlockSpec((CONV_W, HEAD_DIM), lambda j: (0, j))],
        out_specs=pl.BlockSpec((s, HEAD_DIM), lambda j: (0, j)),
        out_shape=jax.ShapeDtypeStruct((s, 3 * BR_WIDTH), F32), compiler_params=_cp("parallel"),
    )(proj, conv_w)


def _dn_qkv_bwd(proj, conv_w, dqkv, *, name):
    s = proj.shape[0]
    nb = 3 * N_HEADS

    def body(x_ref, w_ref, g_ref, dx_ref, dw_ref):
        j = pl.program_id(0)
        x, w, g = x_ref[...], w_ref[...], g_ref[...]
        c = _conv_silu(x, w, s)
        sv, dsv = _silu_and_grad(c)
        r = lax.rsqrt(jnp.sum(sv * sv, axis=-1, keepdims=True) + L2_EPS)
        scale = jnp.where(j < N_HEADS, HEAD_DIM ** -0.5, 1.0).astype(F32)
        ds_n = scale * (r * g - sv * ((r * r * r) * jnp.sum(g * sv, axis=-1, keepdims=True)))
        dc = jnp.where(j < 2 * N_HEADS, ds_n, g) * dsv
        row = _iota2(s, HEAD_DIM, 0)
        dx = w[CONV_W - 1:CONV_W, :] * dc
        dws = [jnp.sum(dc * x, axis=0, keepdims=True)]
        for k in range(1, CONV_W):
            dc_ahead = jnp.where(row < s - k, pltpu.roll(dc, s - k, 0), 0.0)
            dx = dx + w[CONV_W - 1 - k:CONV_W - k, :] * dc_ahead
            dws.append(jnp.sum(dc_ahead * x, axis=0, keepdims=True))
        dx_ref[...] = dx.astype(BF16)
        for k in range(CONV_W):
            dw_ref[CONV_W - 1 - k:CONV_W - k, :] = dws[k]

    blk = pl.BlockSpec((s, HEAD_DIM), lambda j: (0, j))
    wblk = pl.BlockSpec((CONV_W, HEAD_DIM), lambda j: (0, j))
    return pl.pallas_call(
        body, name=name, grid=(nb,), in_specs=[blk, wblk, blk], out_specs=[blk, wblk],
        out_shape=[jax.ShapeDtypeStruct((s, 3 * BR_WIDTH), BF16), jax.ShapeDtypeStruct((CONV_W, 3 * BR_WIDTH), F32)],
        compiler_params=_cp("parallel"),
    )(proj, conv_w, dqkv)


def _tri(n, kind):
    r, c = _iota2(n, n, 0), _iota2(n, n, 1)
    if kind == "lower":
        return (r >= c).astype(F32)
    if kind == "upper":
        return (r <= c).astype(F32)
    return (r == c).astype(F32)


GATE_TILE = 512


def _dn_gate_fwd(proj, a_log, dt_bias, *, name):
    s = proj.shape[0]
    tr = min(GATE_TILE, s)

    def body(b_ref, a_ref, al_ref, dt_ref, beta_ref, g_ref):
        beta_ref[...] = _sigmoid(b_ref[...])
        g = -jnp.exp(al_ref[...]) * _softplus(a_ref[...] + dt_ref[...])
        low = _tri(CHUNK, "lower")
        for c in range(tr // CHUNK):
            rows = slice(c * CHUNK, (c + 1) * CHUNK)
            g_ref[rows, :] = _nn_exact(low, g[rows, :])

    blk = lambda cb: pl.BlockSpec((tr, HEAD_DIM), lambda i: (i, cb))
    vec = pl.BlockSpec((1, HEAD_DIM), lambda i: (0, 0))
    out = pl.BlockSpec((tr, HEAD_DIM), lambda i: (i, 0))
    return pl.pallas_call(
        body, name=name, grid=(s // tr,), in_specs=[blk(C_B // HEAD_DIM), blk(C_A // HEAD_DIM), vec, vec],
        out_specs=[out, out], out_shape=[jax.ShapeDtypeStruct((s, HEAD_DIM), F32)] * 2, compiler_params=_cp("parallel"),
    )(proj, proj, a_log, dt_bias)


def _dn_gate_bwd(proj, a_log, dt_bias, dbeta, d_g, *, name):
    s = proj.shape[0]
    tr = min(GATE_TILE, s)

    def body(b_ref, a_ref, al_ref, dt_ref, dbeta_ref, dG_ref, db_ref, da_ref, dal_ref, ddt_ref):
        @pl.when(pl.program_id(0) == 0)
        def _():
            dal_ref[...] = jnp.zeros_like(dal_ref)
            ddt_ref[...] = jnp.zeros_like(ddt_ref)

        beta = _sigmoid(b_ref[...])
        db_ref[...] = (dbeta_ref[...] * beta * (1.0 - beta)).astype(BF16)
        pre = a_ref[...] + dt_ref[...]
        neg_ea = -jnp.exp(al_ref[...])
        up = _tri(CHUNK, "upper")
        d_g = dG_ref[...]
        dg = jnp.concatenate([_nn_exact(up, d_g[c * CHUNK:(c + 1) * CHUNK, :]) for c in range(tr // CHUNK)], axis=0)
        da = dg * neg_ea * _sigmoid(pre)
        da_ref[...] = da.astype(BF16)
        ddt_ref[...] += jnp.sum(da, axis=0, keepdims=True)
        dal_ref[...] += jnp.sum(dg * neg_ea * _softplus(pre), axis=0, keepdims=True)

    blk = lambda cb: pl.BlockSpec((tr, HEAD_DIM), lambda i: (i, cb))
    vec = pl.BlockSpec((1, HEAD_DIM), lambda i: (0, 0))
    io = pl.BlockSpec((tr, HEAD_DIM), lambda i: (i, 0))
    return pl.pallas_call(
        body, name=name, grid=(s // tr,),
        in_specs=[blk(C_B // HEAD_DIM), blk(C_A // HEAD_DIM), vec, vec, io, io], out_specs=[io, io, vec, vec],
        out_shape=[jax.ShapeDtypeStruct((s, HEAD_DIM), BF16)] * 2 + [jax.ShapeDtypeStruct((1, HEAD_DIM), F32)] * 2,
        compiler_params=_cp("arbitrary"),
    )(proj, proj, a_log, dt_bias, dbeta, d_g)


def _unit_lower_inverse(a_strict, eye):
    x = -a_strict
    t = x + eye
    p = x
    n = 2
    while n < CHUNK:
        p = _nn(p, p)
        t = t + _nn(t, p)
        n *= 2
    return t


def _rows(*xs):
    return _hmap(lambda *a: jnp.concatenate(a, axis=0), *xs)


def _lanes(*xs):
    return _hmap(lambda *a: jnp.concatenate(a, axis=1), *xs)


def _dn_chunk_common(q, k, v, gc, beta, st, with_qd_state, t_inv=None):
    c, d = CHUNK, HEAD_DIM
    eye = _tri(c, "eye")
    low = _tri(c, "lower")
    strict = low - eye
    grow = _col2row(gc, eye)
    dec = _hmap(lambda g_, gr: low * jnp.exp(low * (g_ - gr)), gc, grow)
    kb = k * beta
    kq = _nt(_rows(kb, q), k)
    a_mat = kq[0:c, :] * dec * strict
    qk = kq[c:2 * c, :] * dec
    if t_inv is None:
        t_inv = _unit_lower_inverse(a_mat, eye)
    e_g = _exp(gc)
    qd = q * e_g
    uw = _nn(t_inv, _lanes(v * beta, kb * e_g))
    u, w = uw[:, 0:d], uw[:, d:2 * d]
    last = (_iota2(c, 1, 0) == c - 1).astype(F32)
    g_last = _sum(gc * last, 0)
    e_t = _exp(g_last - gc)
    kt = k * e_t
    tail = _exp(g_last)
    if with_qd_state:
        ws = _nn(_rows(w, qd), st)
        vn, qds = u - ws[0:c, :], ws[c:2 * c, :]
    else:
        vn, qds = u - _nn(w, st), None
    return dict(eye=eye, low=low, strict=strict, dec=dec, kb=kb, a_mat=a_mat, t_inv=t_inv, e_g=e_g, u=u, w=w, uw=uw,
                qk=qk, qd=qd, qds=qds, last=last, e_t=e_t, kt=kt, tail=tail, vn=vn)


def _dn_chunk_fwd_math(q, k, v, gc, beta, st):
    m = _dn_chunk_common(q, k, v, gc, beta, st, True)
    o = m["qds"] + _nn(m["qk"], m["vn"])
    st2 = st * m["tail"] + _tn(m["kt"], m["vn"])
    return o, st2, m["t_inv"]


def _dn_chunk_bwd_math(q, k, v, gc, beta, st, do, dst2, t_inv=None):
    c, d = CHUNK, HEAD_DIM
    m = _dn_chunk_common(q, k, v, gc, beta, st, False, t_inv)
    eye, low, strict = m["eye"], m["low"], m["strict"]
    dvn = _tn(m["qk"], do) + _nn(m["kt"], dst2)
    dqk = _nt(do, m["vn"]) * low
    both = _rows(do, dvn)
    ds_both = _nt(both, st)
    dqd, dw = ds_both[0:c, :], -ds_both[c:2 * c, :]
    dst = _tn(_rows(m["qd"], -m["w"]), both) + dst2 * m["tail"]
    dkt = _nt(m["vn"], dst2)
    dtail = _sum(_sum(st * dst2, 1), 0)
    dvb_dkg = _tn(m["t_inv"], _lanes(dvn, dw))
    dvb, dkg = dvb_dkg[:, 0:d], dvb_dkg[:, d:2 * d]
    d_a = _nt(dvb_dkg, m["uw"]) * (-strict)
    dkk = d_a * m["dec"]
    dp = dqk * m["dec"]
    dpk = _rows(dp, dkk)
    dq_dkb = _nn(dpk, k)
    dq = dq_dkb[0:c, :] + dqd * m["e_g"]
    dkb = dq_dkb[c:2 * c, :] + dkg * m["e_g"]
    dk = _tn(dpk, _rows(q, m["kb"])) + dkb * beta + dkt * m["e_t"]
    dv = dvb * beta
    dbeta = _sum(dvb * v + dkb * k, 1)
    de_g = _sum(dkg * m["kb"] + dqd * q, 1)
    de_t = _sum(dkt * k, 1)
    mm = d_a * m["a_mat"] + dqk * m["qk"]
    dgc = (_sum(mm, 1) - _row2col(_sum(mm, 0), eye) + de_g * m["e_g"] - de_t * m["e_t"]
           + (_sum(de_t * m["e_t"], 0) + dtail * m["tail"]) * m["last"])
    return dq, dk, dv, dgc, dbeta, dst


def _heads_of(ref):
    return _Heads(ref[:, h * HEAD_DIM:(h + 1) * HEAD_DIM] for h in range(N_HEADS))


def _lanes_of(block):
    return _Heads(_pick_lane(block, h) for h in range(N_HEADS))


def _dn_chunk_fwd(qkv, gcs, beta, proj, norm_w, *, name):
    s = qkv.shape[0]
    n = s // CHUNK

    def body(q_ref, k_ref, v_ref, g_ref, b_ref, z_ref, w_ref, o_ref, st_out_ref, tinv_ref, y_ref, st_ref):
        @pl.when(pl.program_id(0) == 0)
        def _():
            st_ref[...] = jnp.zeros_like(st_ref)

        gblk, bblk = g_ref[...], b_ref[...]
        st = _Heads(st_ref[h] for h in range(N_HEADS))
        o, st2, t_inv = _dn_chunk_fwd_math(_heads_of(q_ref), _heads_of(k_ref), _heads_of(v_ref), _lanes_of(gblk),
                                           _lanes_of(bblk), st)
        y = _head_norm_fwd(o, _heads_of(z_ref), w_ref[...])
        for h in range(N_HEADS):
            st_out_ref[0, h] = st.v[h]
            tinv_ref[0, h] = t_inv.v[h].astype(BF16)
            o_ref[:, h * HEAD_DIM:(h + 1) * HEAD_DIM] = o.v[h]
            y_ref[:, h * HEAD_DIM:(h + 1) * HEAD_DIM] = y.v[h]
            st_ref[h] = st2.v[h]

    blk = lambda off: pl.BlockSpec((CHUNK, BR_WIDTH), lambda c: (c, off))
    sc = pl.BlockSpec((CHUNK, HEAD_DIM), lambda c: (c, 0))
    return pl.pallas_call(
        body, name=name, grid=(n,),
        in_specs=[blk(0), blk(1), blk(2), sc, sc, blk(C_Z // BR_WIDTH), pl.BlockSpec((1, HEAD_DIM), lambda c: (0, 0))],
        out_specs=[blk(0), pl.BlockSpec((1, N_HEADS, HEAD_DIM, HEAD_DIM), lambda c: (c, 0, 0, 0)),
                   pl.BlockSpec((1, N_HEADS, CHUNK, CHUNK), lambda c: (c, 0, 0, 0)), blk(0)],
        out_shape=[jax.ShapeDtypeStruct((s, BR_WIDTH), F32), jax.ShapeDtypeStruct((n, N_HEADS, HEAD_DIM, HEAD_DIM), F32),
                   jax.ShapeDtypeStruct((n, N_HEADS, CHUNK, CHUNK), BF16), jax.ShapeDtypeStruct((s, BR_WIDTH), BF16)],
        scratch_shapes=[pltpu.VMEM((N_HEADS, HEAD_DIM, HEAD_DIM), F32)],
        compiler_params=_cp("arbitrary"),
    )(qkv, qkv, qkv, gcs, beta, proj, norm_w.reshape(1, HEAD_DIM))


def _dn_chunk_bwd(qkv, gcs, beta, states, tinvs, o, proj, norm_w, dy, *, name):
    s = qkv.shape[0]
    n = s // CHUNK

    def body(q_ref, k_ref, v_ref, g_ref, b_ref, st_in_ref, tinv_ref, o_ref, z_ref, w_ref, dy_ref,
             dqkv_ref, dg_ref, dbeta_ref, dz_ref, dw_ref, dst_ref):
        @pl.when(pl.program_id(0) == 0)
        def _():
            dst_ref[...] = jnp.zeros_like(dst_ref)
            dw_ref[...] = jnp.zeros_like(dw_ref)

        do, dz, dw = _head_norm_bwd(_heads_of(o_ref), _heads_of(z_ref), w_ref[...], _heads_of(dy_ref))
        dw_ref[...] += dw

        gblk, bblk = g_ref[...], b_ref[...]
        lane = _iota2(CHUNK, HEAD_DIM, 1)
        dg_all = jnp.zeros((CHUNK, HEAD_DIM), F32)
        dbeta_all = jnp.zeros((CHUNK, HEAD_DIM), F32)
        dq, dk, dv, dgc, dbeta, dst = _dn_chunk_bwd_math(
            _heads_of(q_ref), _heads_of(k_ref), _heads_of(v_ref), _lanes_of(gblk), _lanes_of(bblk),
            _Heads(st_in_ref[0, h] for h in range(N_HEADS)), do,
            _Heads(dst_ref[h] for h in range(N_HEADS)), _Heads(tinv_ref[0, h] for h in range(N_HEADS)))
        for h in range(N_HEADS):
            dz_ref[:, h * HEAD_DIM:(h + 1) * HEAD_DIM] = dz.v[h]
            for part, val in enumerate((dq, dk, dv)):
                c0 = part * BR_WIDTH + h * HEAD_DIM
                dqkv_ref[:, c0:c0 + HEAD_DIM] = val.v[h]
            dg_all = jnp.where(lane == h, dgc.v[h], dg_all)
            dbeta_all = jnp.where(lane == h, dbeta.v[h], dbeta_all)
            dst_ref[h] = dst.v[h]
        dg_ref[...] = dg_all
        dbeta_ref[...] = dbeta_all

    blk = lambda off: pl.BlockSpec((CHUNK, BR_WIDTH), lambda c: (n - 1 - c, off))
    sc = pl.BlockSpec((CHUNK, HEAD_DIM), lambda c: (n - 1 - c, 0))
    vec = pl.BlockSpec((1, HEAD_DIM), lambda c: (0, 0))
    outs = pl.pallas_call(
        body, name=name, grid=(n,),
        in_specs=[blk(0), blk(1), blk(2), sc, sc,
                  pl.BlockSpec((1, N_HEADS, HEAD_DIM, HEAD_DIM), lambda c: (n - 1 - c, 0, 0, 0)),
                  pl.BlockSpec((1, N_HEADS, CHUNK, CHUNK), lambda c: (n - 1 - c, 0, 0, 0)), blk(0), blk(C_Z // BR_WIDTH),
                  vec, blk(0)],
        out_specs=[pl.BlockSpec((CHUNK, 3 * BR_WIDTH), lambda c: (n - 1 - c, 0)), sc, sc, blk(0), vec],
        out_shape=[jax.ShapeDtypeStruct((s, 3 * BR_WIDTH), F32)] + [jax.ShapeDtypeStruct((s, HEAD_DIM), F32)] * 2
        + [jax.ShapeDtypeStruct((s, BR_WIDTH), BF16), jax.ShapeDtypeStruct((1, HEAD_DIM), F32)],
        scratch_shapes=[pltpu.VMEM((N_HEADS, HEAD_DIM, HEAD_DIM), F32)],
        compiler_params=_cp("arbitrary"),
    )(qkv, qkv, qkv, gcs, beta, states, tinvs, o, proj, norm_w.reshape(1, HEAD_DIM), dy)
    return outs


def _hg_chunk_common(q, k, g):
    c, nb = CHUNK, CHUNK // SUB
    e_g = _exp(g)
    qd = q * e_g
    g_last = g[c - 1:c, :]
    e_t = _exp(g_last - g)
    kt = k * e_t
    tail = _exp(g_last)
    g_refs = [g[i * SUB:i * SUB + 1, :] for i in range(nb)]
    g_ref_rows = _hmap(lambda *rows: jnp.concatenate([jnp.broadcast_to(r, (SUB, r.shape[1])) for r in rows], axis=0), *g_refs)
    e_q = _exp(g - g_ref_rows)
    q_sc = q * e_q
    e_k = [_hmap(lambda gr, g_: jnp.exp(jnp.minimum(gr - g_, EXP_CLAMP)), g_refs[i], g) for i in range(nb)]
    k_sc_all = _rows(*[k * e_k[i] for i in range(nb)])
    row_blk = _iota2(c, 1, 0) // SUB
    masks = [(row_blk == i).astype(F32) for i in range(nb)]
    r_all = _nt(q_sc, k_sc_all)
    a_mat = r_all[:, 0:c] * masks[0]
    for i in range(1, nb):
        a_mat = a_mat + r_all[:, i * c:(i + 1) * c] * masks[i]
    a_mat = a_mat * _tri(c, "lower")
    return dict(e_g=e_g, qd=qd, e_t=e_t, kt=kt, tail=tail, q_sc=q_sc, k_sc_all=k_sc_all, e_q=e_q, e_k=e_k, masks=masks,
                a_mat=a_mat)


def _hg_chunk_fwd_math(q, k, v, g, stt):
    m = _hg_chunk_common(q, k, g)
    o = _nt(m["qd"], stt) + _nn(m["a_mat"], v)
    stt2 = stt * m["tail"] + _tn(v, m["kt"])
    return o, stt2


def _hg_chunk_bwd_math(q, k, v, g, stt, do, dstt2):
    c, nb = CHUNK, CHUNK // SUB
    m = _hg_chunk_common(q, k, g)
    stt2 = stt * m["tail"] + _tn(v, m["kt"])
    later = _sum(stt2 * dstt2, 0)
    dqd = _dot3(do, stt, 1, 0)
    dstt = _tn(do, m["qd"]) + dstt2 * m["tail"]
    d_a = _dot3(do, v, 1, 1) * _tri(c, "lower")
    dv = _tn(m["a_mat"], do) + _nt(m["kt"], dstt2)
    dkt = _dot3(v, dstt2, 1, 0)
    d_blk = _lanes(*[d_a * m["masks"][i] for i in range(nb)])
    dq = dqd * m["e_g"] + _dot3(d_blk, m["k_sc_all"], 1, 0) * m["e_q"]
    dks = _dot3(d_blk, m["q_sc"], 0, 0)
    dk = dkt * m["e_t"]
    for i in range(nb):
        dk = dk + dks[i * c:(i + 1) * c, :] * m["e_k"][i]
    db = q * dq - k * dk
    return dq, dk, dv, db, later, dstt


def _hg_chunk_fwd(proj, lb, norm_w, *, name):
    s = proj.shape[0]
    n = s // CHUNK

    def body(hq_ref, hf_ref, v_ref, lb_ref, z_ref, w_ref, o_ref, st_out_ref, q_out, k_out, lf_out, y_ref, st_ref):
        @pl.when(pl.program_id(0) == 0)
        def _():
            st_ref[...] = jnp.zeros_like(st_ref)

        f, lbv = hf_ref[...], lb_ref[...]
        q_all = _silu(hq_ref[...])
        k_all = (1.0 - lbv) * _sigmoid(-f)
        lf_all = jnp.log(lbv + (1.0 - lbv) * _sigmoid(f))
        q_out[...], k_out[...], lf_out[...] = q_all, k_all, lf_all
        st = _Heads(st_ref[h] for h in range(N_HEADS))
        g_all = _nn_exact(_tri(CHUNK, "lower"), lf_all)
        o, st2 = _hg_chunk_fwd_math(_heads_of(q_all), _heads_of(k_all), _heads_of(v_ref), _heads_of(g_all), st)
        y = _head_norm_fwd(o, _heads_of(z_ref), w_ref[...])
        for h in range(N_HEADS):
            st_out_ref[0, h] = st.v[h]
            o_ref[:, h * HEAD_DIM:(h + 1) * HEAD_DIM] = o.v[h]
            y_ref[:, h * HEAD_DIM:(h + 1) * HEAD_DIM] = y.v[h]
            st_ref[h] = st2.v[h]

    blk = lambda off: pl.BlockSpec((CHUNK, BR_WIDTH), lambda c: (c, off))
    return pl.pallas_call(
        body, name=name, grid=(n,),
        in_specs=[blk(C_HQ // BR_WIDTH), blk(C_HF // BR_WIDTH), blk(C_HI // BR_WIDTH), pl.BlockSpec((1, BR_WIDTH), lambda c: (0, 0)),
                  blk(C_HZ // BR_WIDTH), pl.BlockSpec((1, HEAD_DIM), lambda c: (0, 0))],
        out_specs=[blk(0), pl.BlockSpec((1, N_HEADS, HEAD_DIM, HEAD_DIM), lambda c: (c, 0, 0, 0)), blk(0), blk(0), blk(0), blk(0)],
        out_shape=[jax.ShapeDtypeStruct((s, BR_WIDTH), F32), jax.ShapeDtypeStruct((n, N_HEADS, HEAD_DIM, HEAD_DIM), F32)]
        + [jax.ShapeDtypeStruct((s, BR_WIDTH), F32)] * 3 + [jax.ShapeDtypeStruct((s, BR_WIDTH), BF16)],
        scratch_shapes=[pltpu.VMEM((N_HEADS, HEAD_DIM, HEAD_DIM), F32)],
        compiler_params=_cp("arbitrary"),
    )(proj, proj, proj, lb, proj, norm_w.reshape(1, HEAD_DIM))


def _hg_chunk_bwd(proj, lb, qh, kh, lf, states, o, norm_w, dy, *, name):
    s = proj.shape[0]
    n = s // CHUNK

    def body(hq_ref, hf_ref, v_ref, lb_ref, q_ref, k_ref, lf_ref, st_in_ref, o_ref, z_ref, w_ref, dy_ref,
             dhq_ref, dhf_ref, dhi_ref, dz_ref, dlb_ref, dw_ref, dst_ref):
        @pl.when(pl.program_id(0) == 0)
        def _():
            dst_ref[...] = jnp.zeros_like(dst_ref)
            dlb_ref[...] = jnp.zeros_like(dlb_ref)
            dw_ref[...] = jnp.zeros_like(dw_ref)

        do, dz, dw = _head_norm_bwd(_heads_of(o_ref), _heads_of(z_ref), w_ref[...], _heads_of(dy_ref))
        dw_ref[...] += dw

        g_all = _nn_exact(_tri(CHUNK, "lower"), lf_ref[...])
        dq, dk, dv, db, later, dst = _hg_chunk_bwd_math(
            _heads_of(q_ref), _heads_of(k_ref), _heads_of(v_ref), _heads_of(g_all),
            _Heads(st_in_ref[0, h] for h in range(N_HEADS)), do,
            _Heads(dst_ref[h] for h in range(N_HEADS)))
        dlf = _nn_exact(_tri(CHUNK, "upper"), jnp.concatenate(db.v, axis=1)) + jnp.concatenate(later.v, axis=1)
        dq_all, dk_all = jnp.concatenate(dq.v, axis=1), jnp.concatenate(dk.v, axis=1)
        f, lbv = hf_ref[...], lb_ref[...]
        dhq_ref[...] = (dq_all * _dsilu(hq_ref[...])).astype(BF16)
        sp, sn = _sigmoid(f), _sigmoid(-f)
        dlf_over = dlf / (lbv + (1.0 - lbv) * sp)
        dhf_ref[...] = (dlf_over * (1.0 - lbv) * sp * sn - dk_all * (1.0 - lbv) * sn * (1.0 - sn)).astype(BF16)
        dlb_ref[...] += jnp.sum(dlf_over * (1.0 - sp) - dk_all * sn, axis=0, keepdims=True)
        for h in range(N_HEADS):
            dhi_ref[:, h * HEAD_DIM:(h + 1) * HEAD_DIM] = dv.v[h].astype(BF16)
            dz_ref[:, h * HEAD_DIM:(h + 1) * HEAD_DIM] = dz.v[h]
            dst_ref[h] = dst.v[h]

    blk = lambda off: pl.BlockSpec((CHUNK, BR_WIDTH), lambda c: (n - 1 - c, off))
    vec = pl.BlockSpec((1, BR_WIDTH), lambda c: (0, 0))
    wvec = pl.BlockSpec((1, HEAD_DIM), lambda c: (0, 0))
    return pl.pallas_call(
        body, name=name, grid=(n,),
        in_specs=[blk(C_HQ // BR_WIDTH), blk(C_HF // BR_WIDTH), blk(C_HI // BR_WIDTH), vec, blk(0), blk(0), blk(0),
                  pl.BlockSpec((1, N_HEADS, HEAD_DIM, HEAD_DIM), lambda c: (n - 1 - c, 0, 0, 0)), blk(0), blk(C_HZ // BR_WIDTH),
                  wvec, blk(1)],
        out_specs=[blk(0), blk(0), blk(0), blk(0), vec, wvec],
        out_shape=[jax.ShapeDtypeStruct((s, BR_WIDTH), BF16)] * 4 + [jax.ShapeDtypeStruct((1, BR_WIDTH), F32),
                                                                    jax.ShapeDtypeStruct((1, HEAD_DIM), F32)],
        scratch_shapes=[pltpu.VMEM((N_HEADS, HEAD_DIM, HEAD_DIM), F32)],
        compiler_params=_cp("arbitrary"),
    )(proj, proj, proj, lb, qh, kh, lf, states, o, proj, norm_w.reshape(1, HEAD_DIM), dy)


_ANY = pl.BlockSpec(memory_space=pl.ANY)
_MESH = pl.DeviceIdType.MESH


def _all_gather(x_local, *, name, after=()):
    n_after = len(after)

    def body(x_ref, *refs):
        out_ref, send_sems, recv_sems, local_sem = refs[n_after:]
        x, y, c = lax.axis_index("x"), lax.axis_index("y"), lax.axis_index("c")
        me, sibling = (x, y, c), (x, y, 1 - c)
        n1 = (x ^ (1 - c), y ^ c)
        n2 = (x ^ c, y ^ (1 - c))
        dg = (1 - x, 1 - y)

        def slot(px, py, pc):
            return out_ref.at[4 * px + 2 * py + pc]

        def copy(k, block, to, src=None):
            return pltpu.make_async_remote_copy(
                src_ref=slot(*block) if src is None else src, dst_ref=slot(*block),
                send_sem=send_sems.at[k], recv_sem=recv_sems.at[k], device_id=to, device_id_type=_MESH)

        mine = pltpu.make_async_copy(x_ref, slot(*me), local_sem)
        mine.start()
        first = [copy(0, me, sibling, src=x_ref), copy(1, me, (*n1, c), src=x_ref), copy(2, me, (*n2, c), src=x_ref)]
        for cp in first:
            cp.start()
        copy(2, (*n2, c), me).wait_recv()
        forward = copy(3, (*n2, c), (*n1, c))
        forward.start()
        passed = [copy(5, (*n2, c), sibling)]
        passed[0].start()
        copy(1, (*n1, c), me).wait_recv()
        passed.append(copy(4, (*n1, c), sibling))
        passed[1].start()
        copy(3, (*dg, c), me).wait_recv()
        passed.append(copy(6, (*dg, c), sibling))
        passed[2].start()
        copy(0, sibling, me).wait_recv()
        copy(4, (*n2, 1 - c), me).wait_recv()
        copy(5, (*n1, 1 - c), me).wait_recv()
        copy(6, (*dg, 1 - c), me).wait_recv()
        for cp in first + [forward] + passed:
            cp.wait_send()
        mine.wait()

    return pl.pallas_call(
        body, name=name, out_shape=jax.ShapeDtypeStruct((N_DEV,) + x_local.shape, x_local.dtype),
        in_specs=[_ANY] * (1 + n_after), out_specs=_ANY,
        scratch_shapes=[pltpu.SemaphoreType.DMA((7,)), pltpu.SemaphoreType.DMA((7,)), pltpu.SemaphoreType.DMA],
    )(x_local, *after)


_HBM = pl.BlockSpec(memory_space=pltpu.HBM)
_SEM = pl.BlockSpec(memory_space=pltpu.SEMAPHORE)
_EFFECT = pltpu.SideEffectType.DATAFLOW_SIDE_EFFECTING


def _peers():
    x, y, c = lax.axis_index("x"), lax.axis_index("y"), lax.axis_index("c")
    out = []
    for k in range(1, N_DEV):
        px, py, pc = x ^ ((k >> 2) & 1), y ^ ((k >> 1) & 1), c ^ (k & 1)
        out.append(((px, py, pc), 4 * px + 2 * py + pc))
    return 4 * x + 2 * y + c, out


def _push_copies(src_ref, land_ref, send_sems, recv_sems, broadcast):
    my, peers = _peers()
    pairs = []
    for k, (pos, idx) in enumerate(peers):
        src = src_ref if broadcast else src_ref.at[idx]
        send = pltpu.make_async_remote_copy(src_ref=src, dst_ref=land_ref.at[my], send_sem=send_sems.at[k],
                                            recv_sem=recv_sems.at[k], device_id=pos, device_id_type=_MESH)
        recv = pltpu.make_async_remote_copy(src_ref=src, dst_ref=land_ref.at[idx], send_sem=send_sems.at[k],
                                            recv_sem=recv_sems.at[k], device_id=pos, device_id_type=_MESH)
        pairs.append((send, recv))
    return pairs


def _push_start(src, land, *, broadcast, name, after=()):
    n_after = len(after)

    def body(src_ref, land_ref, *refs):
        send_sems, recv_sems, _, _, token = refs[n_after:]
        for send, _ in _push_copies(src_ref, land_ref, send_sems, recv_sems, broadcast):
            send.start()
        token[...] = jnp.zeros_like(token)

    return pl.pallas_call(
        body, name=name,
        out_shape=(pltpu.SemaphoreType.DMA((N_DEV - 1,)), pltpu.SemaphoreType.DMA((N_DEV - 1,)),
                   pltpu.HBM(src.shape, src.dtype), pltpu.HBM(land.shape, land.dtype), jax.ShapeDtypeStruct((8, 128), F32)),
        in_specs=(_HBM, _HBM) + (_ANY,) * n_after, out_specs=(_SEM, _SEM, _HBM, _HBM, pl.BlockSpec(memory_space=pltpu.VMEM)),
        input_output_aliases={0: 2, 1: 3}, compiler_params=pltpu.CompilerParams(has_side_effects=_EFFECT),
    )(pltpu.with_memory_space_constraint(src, pltpu.HBM), pltpu.with_memory_space_constraint(land, pltpu.HBM), *after)


def _push_wait(handle, after, *, broadcast, name):
    send_sems, recv_sems, src_thru, land_thru, _ = handle

    def body(src_ref, land_ref, send_sems, recv_sems, *rest):
        for send, recv in _push_copies(src_ref, land_ref, send_sems, recv_sems, broadcast):
            send.wait_send()
            recv.wait_recv()

    return pl.pallas_call(
        body, name=name,
        out_shape=(pltpu.HBM(src_thru.shape, src_thru.dtype), pltpu.HBM(land_thru.shape, land_thru.dtype)),
        in_specs=(_HBM, _HBM, _SEM, _SEM) + (_ANY,) * len(after), out_specs=(_HBM, _HBM),
        input_output_aliases={0: 0, 1: 1}, compiler_params=pltpu.CompilerParams(has_side_effects=_EFFECT),
    )(src_thru, land_thru, send_sems, recv_sems, *after)[1]


def _relay_copies(src_ref, land_ref, sems_a, sems_b):
    x, y, c = lax.axis_index("x"), lax.axis_index("y"), lax.axis_index("c")
    slot = lambda px, py, pc: land_ref.at[4 * px + 2 * py + pc]
    chips = [(1 - x, y), (x, 1 - y), (1 - x, 1 - y)]
    (send_a, recv_a), (send_b, recv_b) = sems_a, sems_b

    def copy(sems, k, src, dst_slot, to):
        return pltpu.make_async_remote_copy(src_ref=src, dst_ref=dst_slot, send_sem=sems[0].at[k], recv_sem=sems[1].at[k],
                                            device_id=to, device_id_type=_MESH)

    first = [copy((send_a, recv_a), 0, src_ref, slot(x, y, c), (x, y, 1 - c))]
    first += [copy((send_a, recv_a), 1 + j, src_ref, slot(x, y, c), (*chip, c)) for j, chip in enumerate(chips)]
    first_in = [copy((send_a, recv_a), 0, src_ref, slot(x, y, 1 - c), (x, y, 1 - c))]
    first_in += [copy((send_a, recv_a), 1 + j, src_ref, slot(*chip, c), (*chip, c)) for j, chip in enumerate(chips)]
    relay = [copy((send_b, recv_b), j, slot(*chip, c), slot(*chip, c), (x, y, 1 - c)) for j, chip in enumerate(chips)]
    relay_in = [copy((send_b, recv_b), j, slot(*chip, 1 - c), slot(*chip, 1 - c), (x, y, 1 - c)) for j, chip in enumerate(chips)]
    return first, first_in, relay, relay_in


def _relay_start(src, land, *, name, after=()):
    n_after = len(after)

    def body(src_ref, land_ref, *refs):
        send_a, recv_a, _, _, token = refs[n_after:]
        for cp in _relay_copies(src_ref, land_ref, (send_a, recv_a), (send_a, recv_a))[0]:
            cp.start()
        token[...] = jnp.zeros_like(token)

    send_a, recv_a, src_thru, land_thru, token = pl.pallas_call(
        body, name=name,
        out_shape=(pltpu.SemaphoreType.DMA((4,)), pltpu.SemaphoreType.DMA((4,)), pltpu.HBM(src.shape, src.dtype),
                   pltpu.HBM(land.shape, land.dtype), jax.ShapeDtypeStruct((8, 128), F32)),
        in_specs=(_HBM, _HBM) + (_ANY,) * n_after, out_specs=(_SEM, _SEM, _HBM, _HBM, pl.BlockSpec(memory_space=pltpu.VMEM)),
        input_output_aliases={0: 2, 1: 3}, compiler_params=pltpu.CompilerParams(has_side_effects=_EFFECT),
    )(pltpu.with_memory_space_constraint(src, pltpu.HBM), pltpu.with_memory_space_constraint(land, pltpu.HBM), *after)
    return (send_a, recv_a), src_thru, land_thru, token


def _relay_mid(handle, after, *, name):
    sems_a, src_thru, land_thru, _ = handle
    n_after = len(after)

    def body(src_ref, land_ref, send_a, recv_a, *refs):
        send_b, recv_b, _, _, token = refs[n_after:]
        _, first_in, relay, _ = _relay_copies(src_ref, land_ref, (send_a, recv_a), (send_b, recv_b))
        for j in range(3):
            first_in[1 + j].wait_recv()
            relay[j].start()
        token[...] = jnp.zeros_like(token)

    send_b, recv_b, src2, land2, token = pl.pallas_call(
        body, name=name,
        out_shape=(pltpu.SemaphoreType.DMA((3,)), pltpu.SemaphoreType.DMA((3,)), pltpu.HBM(src_thru.shape, src_thru.dtype),
                   pltpu.HBM(land_thru.shape, land_thru.dtype), jax.ShapeDtypeStruct((8, 128), F32)),
        in_specs=(_HBM, _HBM, _SEM, _SEM) + (_ANY,) * n_after,
        out_specs=(_SEM, _SEM, _HBM, _HBM, pl.BlockSpec(memory_space=pltpu.VMEM)),
        input_output_aliases={0: 2, 1: 3}, compiler_params=pltpu.CompilerParams(has_side_effects=_EFFECT),
    )(src_thru, land_thru, *sems_a, *after)
    return sems_a, (send_b, recv_b), src2, land2, token


def _relay_wait(handle, after, *, name):
    sems_a, sems_b, src_thru, land_thru, _ = handle

    def body(src_ref, land_ref, send_a, recv_a, send_b, recv_b, *rest):
        first, first_in, relay, relay_in = _relay_copies(src_ref, land_ref, (send_a, recv_a), (send_b, recv_b))
        first_in[0].wait_recv()
        for cp in relay_in:
            cp.wait_recv()
        for cp in first + relay:
            cp.wait_send()

    return pl.pallas_call(
        body, name=name,
        out_shape=(pltpu.HBM(src_thru.shape, src_thru.dtype), pltpu.HBM(land_thru.shape, land_thru.dtype)),
        in_specs=(_HBM, _HBM, _SEM, _SEM, _SEM, _SEM) + (_ANY,) * len(after), out_specs=(_HBM, _HBM),
        input_output_aliases={0: 0, 1: 1}, compiler_params=pltpu.CompilerParams(has_side_effects=_EFFECT),
    )(src_thru, land_thru, *sems_a, *sems_b, *after)[1]


def _adamw(parts, row_off, w, m, v, *, layer=0, n_layers=1, prev=None, name, tr):
    rows, c = w.shape
    r = rows // n_layers
    np_ = parts.shape[0]
    tr = min(tr, r)
    assert r % tr == 0 and row_off % tr == 0
    ob, lb = row_off // tr, layer * (r // tr)
    c1 = 1.0 - ADAM_B1 ** ADAM_STEP
    c2 = 1.0 - ADAM_B2 ** ADAM_STEP
    n_prev = 0 if prev is None else 4

    def body(p_ref, w_ref, m_ref, v_ref, *refs):
        g_ref, d_ref, nm_ref, nv_ref = refs[n_prev:]
        g = p_ref[0].astype(F32)
        for s in range(1, np_):
            g = g + p_ref[s].astype(F32)
        wv = w_ref[...]
        m2 = ADAM_B1 * m_ref[...] + (1.0 - ADAM_B1) * g
        v2 = ADAM_B2 * v_ref[...] + (1.0 - ADAM_B2) * jnp.square(g)
        m_hat = m2 / c1
        v_hat = v2 / c2
        g_ref[...] = g
        d_ref[...] = -ADAM_LR * (m_hat / (jnp.sqrt(v_hat) + ADAM_EPS) + ADAM_WD * wv)
        nm_ref[...] = m2
        nv_ref[...] = v2

    blk = pl.BlockSpec((tr, c), lambda i: (lb + i, 0))
    return pl.pallas_call(
        body, name=name, grid=(r // tr,),
        in_specs=[pl.BlockSpec((np_, tr, c), lambda i: (0, ob + i, 0)), blk, blk, blk] + [_ANY] * n_prev,
        out_specs=[blk] * 4, out_shape=[jax.ShapeDtypeStruct((rows, c), F32)] * 4,
        input_output_aliases={4 + i: i for i in range(n_prev)}, compiler_params=_cp("parallel"),
    )(parts, w, m, v, *(prev or ()))


def _sum_parts(parts, *, name, after=()):
    np_, r, c = parts.shape

    def body(p_ref, *refs):
        o_ref = refs[-1]
        g = p_ref[0]
        for s in range(1, np_):
            g = g + p_ref[s]
        o_ref[...] = g

    vmem = pl.BlockSpec(memory_space=pltpu.VMEM)
    return pl.pallas_call(body, name=name, in_specs=[vmem] + [_ANY] * len(after), out_specs=vmem,
                          out_shape=jax.ShapeDtypeStruct((r, c), F32))(parts, *after)


def _pack(arrs):
    rows = []
    for a in arrs:
        f = a.reshape(-1).astype(F32)
        pad = (-f.shape[0]) % 128
        rows.append(jnp.pad(f, (0, pad)).reshape(-1, 128))
    out = jnp.concatenate(rows, axis=0)
    return jnp.pad(out, ((0, (-out.shape[0]) % 8), (0, 0)))


def _unpack(packed, shapes):
    outs, r0 = [], 0
    for shp in shapes:
        n = 1
        for d in shp:
            n *= d
        nr = -(-n // 128)
        outs.append(packed[r0:r0 + nr].reshape(-1)[:n].reshape(shp))
        r0 += nr
    return outs


_WIN_PIECES = ((0, 4096, 0), (4112, 8208, 0), (4096, 4104, HEAD_DIM - N_HEADS), (4104, 4112, HEAD_DIM - N_HEADS))


RELAYOUT_TILE = 256
LAST_SPLIT = 4
OTHER_SPLIT = 2


def _win_from_shards(shards, *, name):
    k = shards.shape[1]
    tr = min(RELAYOUT_TILE, k)

    def body(x_ref, o_ref):
        cols = []
        for lo, hi, pad in _WIN_PIECES:
            for j in range(N_DEV):
                a, b = max(lo, j * SHARD_IN), min(hi, (j + 1) * SHARD_IN)
                if a < b:
                    cols.append(x_ref[j, :, a - j * SHARD_IN:b - j * SHARD_IN])
            if pad:
                cols.append(jnp.zeros((tr, pad), x_ref.dtype))
        o_ref[...] = jnp.concatenate(cols, axis=1)

    return pl.pallas_call(
        body, name=name, grid=(k // tr,), in_specs=[pl.BlockSpec((N_DEV, tr, SHARD_IN), lambda i: (0, i, 0))],
        out_specs=pl.BlockSpec((tr, N_PROJ), lambda i: (i, 0)), out_shape=jax.ShapeDtypeStruct((k, N_PROJ), shards.dtype),
        compiler_params=_cp("parallel"),
    )(shards)


def _win_to_shards(g, *, name):
    k = g.shape[0]
    tr = min(RELAYOUT_TILE, k)
    starts, off = [], 0
    for lo, hi, pad in _WIN_PIECES:
        starts.append((lo, hi, off))
        off += hi - lo + pad

    def body(g_ref, o_ref):
        for j in range(N_DEV):
            cols = []
            for lo, hi, off in sorted(starts):
                a, b = max(lo, j * SHARD_IN), min(hi, (j + 1) * SHARD_IN)
                if a < b:
                    cols.append(g_ref[:, off + a - lo:off + b - lo])
            o_ref[j] = jnp.concatenate(cols, axis=1)

    return pl.pallas_call(
        body, name=name, grid=(k // tr,), in_specs=[pl.BlockSpec((tr, N_PROJ), lambda i: (i, 0))],
        out_specs=pl.BlockSpec((N_DEV, tr, SHARD_IN), lambda i: (0, i, 0)),
        out_shape=jax.ShapeDtypeStruct((N_DEV, k, SHARD_IN), g.dtype), compiler_params=_cp("parallel"),
    )(g)


def _lower_bounds(logits):
    probs = jax.nn.softmax(logits.astype(F32), axis=0)
    return jnp.cumsum(probs, axis=0) - probs[0]


def _pad_lanes(vec8):
    return jnp.pad(vec8.reshape(1, N_HEADS), ((0, 0), (0, HEAD_DIM - N_HEADS)))


def kernel(x, p, norm_w, w_in, dn_conv_w, dn_A_log, dn_dt_bias, dn_norm_w, hg_lb_logits, hg_norm_w, w_out, w_ple_up, w_ple_gate, final_norm_w, loss_target, m_norm_w, m_w_in, m_dn_conv_w, m_dn_A_log, m_dn_dt_bias, m_dn_norm_w, m_hg_lb_logits, m_hg_norm_w, m_w_out, m_w_ple_up, m_w_ple_gate, m_final_norm_w, v_norm_w, v_w_in, v_dn_conv_w, v_dn_A_log, v_dn_dt_bias, v_dn_norm_w, v_hg_lb_logits, v_hg_norm_w, v_w_out, v_w_ple_up, v_w_ple_gate, v_final_norm_w):
    depth = norm_w.shape[0]
    my = 4 * lax.axis_index("x") + 2 * lax.axis_index("y") + lax.axis_index("c")
    h = x[0]
    tgt = loss_target[0]
    rows_out = D_MODEL // N_DEV
    up_rows = PLE_DIM * (D_MODEL // N_DEV) // D_MODEL
    g_off, u_off = rows_out, 2 * rows_out

    def own_slot(block):
        return lax.dynamic_update_index_in_dim(lax.empty((N_DEV,) + block.shape, block.dtype), block, my, 0)

    win_bf = w_in.astype(BF16)
    rest_bf = [jnp.concatenate([w_out[l], w_ple_gate[l], w_ple_up[l].reshape(up_rows, D_MODEL)], axis=0).astype(BF16)
               for l in range(depth)]
    conv_push = _push_start(dn_conv_w, own_slot(dn_conv_w), broadcast=True, name="gather_conv_w_start")
    win_all = {0: _all_gather(win_bf[0], name="gather_w_in_l0", after=[conv_push[4]])}
    pending, relayed = {}, {}
    last = win_all[0]
    for l in range(depth):
        if l > 0:
            relayed["win", l] = _relay_start(win_bf[l], own_slot(win_bf[l]), after=[last], name=f"gather_w_in_l{l}_first")
            last = relayed["win", l][3]
        if l == 0:
            relayed["rest", l] = _relay_start(rest_bf[l], own_slot(rest_bf[l]), after=[last], name=f"gather_rest_l{l}_first")
            last = relayed["rest", l][3]
        else:
            pending["rest", l] = _push_start(rest_bf[l], own_slot(rest_bf[l]), broadcast=True, after=[last],
                                             name=f"gather_rest_l{l}_start")
            last = pending["rest", l][4]
    order_tok = last[0, 0]
    lbs = _lower_bounds(hg_lb_logits)

    saved = []
    weights = []
    for l in range(depth):
        tag = f"l{l}"
        if l > 0:
            win_all[l] = _relay_wait(relayed["win", l], [h], name=f"gather_w_in_{tag}_wait")
        wi = _win_from_shards(win_all[l], name=f"w_in_layout_{tag}")
        nw = norm_w[l] + order_tok if l == 0 else norm_w[l]
        hn = _rms_fwd(h, nw, name=f"rms_fwd_{tag}")
        proj = _mm(hn, wi, mode="nn", out_dtype=F32, name=f"mm_proj_{tag}")
        al, dt = _pad_lanes(dn_A_log[l]), _pad_lanes(dn_dt_bias[l])
        if l == 0:
            conv_all = _push_wait(conv_push, [proj], broadcast=True, name="gather_conv_w_wait")
            conv_full = conv_all.transpose(1, 2, 0, 3).reshape(depth, CONV_W, 3 * BR_WIDTH)
        qkv = _dn_qkv_fwd(proj, conv_full[l], name=f"dn_qkv_fwd_{tag}")
        if ("rest", l) in relayed:
            relayed["rest", l] = _relay_mid(relayed["rest", l], [qkv], name=f"gather_rest_{tag}_relay")
            al = al + relayed["rest", l][4][0, 0]
        beta, gcs = _dn_gate_fwd(proj, al, dt, name=f"dn_gate_fwd_{tag}")
        o_dn, st_dn, tinv_dn, y_dn = _dn_chunk_fwd(qkv, gcs, beta, proj, dn_norm_w[l], name=f"dn_chunk_fwd_{tag}")
        lb = lbs[l].reshape(1, BR_WIDTH)
        o_hg, st_hg, qh, kh, lf, y_hg = _hg_chunk_fwd(proj, lb, hg_norm_w[l], name=f"hg_chunk_fwd_{tag}")
        y = jnp.concatenate([y_dn, y_hg], axis=1)
        if ("rest", l) in relayed:
            rest_all = _relay_wait(relayed["rest", l], [y], name=f"gather_rest_{tag}_wait")
        else:
            rest_all = _push_wait(pending["rest", l], [y], broadcast=True, name=f"gather_rest_{tag}_wait")
        w_out_rows, w_gate_rows = (0, rows_out), (g_off, rows_out)
        wu = rest_all[:, u_off:u_off + up_rows].reshape(N_DEV, PLE_DIM, D_MODEL // N_DEV).transpose(1, 0, 2).reshape(PLE_DIM, D_MODEL)
        weights.append((wi, rest_all, wu))
        h1 = _mm(y, rest_all, mode="nn", b_rows=w_out_rows, out_dtype=F32, res=h, name=f"mm_out_{tag}")
        pin = []
        if ("win", l + 1) in relayed:
            relayed["win", l + 1] = _relay_mid(relayed["win", l + 1], [h1], name=f"gather_w_in_l{l + 1}_relay")
            pin = [relayed["win", l + 1][4]]
        up = _mm(p[l, 0], wu, mode="nn", out_dtype=F32, name=f"mm_up_{tag}")
        gp, h2 = _mm(h1, rest_all, mode="nn", b_rows=w_gate_rows, out_dtype=F32, after=pin, tile_n=512, fused=("ple", h1, up),
                     name=f"mm_gate_{tag}")
        saved.append(dict(h=h, hn=hn, proj=proj, qkv=qkv, beta=beta, gcs=gcs, st_dn=st_dn, tinv_dn=tinv_dn, qh=qh, kh=kh, lf=lf,
                          st_hg=st_hg, o_dn=o_dn, o_hg=o_hg, y=y, h1=h1, gp=gp, up=up, al=al, dt=dt, lb=lb))
        h = h2

    loss_row, dh, d_final_w = _final_fwd_bwd(h, final_norm_w, tgt, name="final_norm_loss")

    d_norm_w, d_alog, d_dt, d_dn_nw, d_hg_nw, d_lb, d_conv = ([None] * depth for _ in range(7))
    sent = {}
    for l in reversed(range(depth)):
        wi, rest_all, wu = weights[l]
        sv = saved[l]
        tag = f"l{l}"
        dup, dgp = _ple_bwd(dh, sv["gp"], sv["up"], name=f"ple_bwd_{tag}")
        d_wu = _mm(p[l, 0], dup, mode="tn", out_dtype=BF16, name=f"mm_dwup_{tag}")
        d_wg = _mm(sv["h1"], dgp, mode="tn", out_dtype=BF16, name=f"mm_dwgate_{tag}")
        dh1 = _mm(dgp, rest_all, mode="nt", b_rows=(g_off, rows_out), out_dtype=F32, res=dh, name=f"mm_dh1_{tag}")
        d_wo = _mm(sv["y"], dh1, mode="tn", out_dtype=BF16, name=f"mm_dwout_{tag}")
        parts_rest = jnp.concatenate(
            [d_wo.reshape(N_DEV, rows_out, D_MODEL), d_wg.reshape(N_DEV, rows_out, D_MODEL),
             d_wu.reshape(PLE_DIM, N_DEV, D_MODEL // N_DEV).transpose(1, 0, 2).reshape(N_DEV, up_rows, D_MODEL)], axis=1)
        sent["rest", l] = _push_start(parts_rest, own_slot(parts_rest[my]), broadcast=False, name=f"exchange_rest_{tag}_start")
        dy = _mm(dh1, rest_all, mode="nt", b_rows=(0, rows_out), out_dtype=F32, name=f"mm_dy_{tag}")
        dn_nw = dn_norm_w[l] + sent["rest", l][4][0, 0]
        dqkv, d_gc, dbeta, dz_dn, d_dn_nw[l] = _dn_chunk_bwd(sv["qkv"], sv["gcs"], sv["beta"], sv["st_dn"], sv["tinv_dn"],
                                                             sv["o_dn"], sv["proj"], dn_nw, dy, name=f"dn_chunk_bwd_{tag}")
        dqkv_pre, d_conv[l] = _dn_qkv_bwd(sv["proj"], conv_full[l], dqkv, name=f"dn_qkv_bwd_{tag}")
        db, da, d_alog[l], d_dt[l] = _dn_gate_bwd(sv["proj"], sv["al"], sv["dt"], dbeta, d_gc, name=f"dn_gate_bwd_{tag}")
        dhq, dhf, dhi, dz_hg, d_lb[l], d_hg_nw[l] = _hg_chunk_bwd(sv["proj"], sv["lb"], sv["qh"], sv["kh"], sv["lf"], sv["st_hg"],
                                                                  sv["o_hg"], hg_norm_w[l], dy, name=f"hg_chunk_bwd_{tag}")
        dproj = jnp.concatenate([dqkv_pre, dz_dn, dhq, dhf, dhi, dz_hg, db, da], axis=1)
        def push_d_win(after):
            n_split = LAST_SPLIT if l == 0 else OTHER_SPLIT
            rows = D_MODEL // n_split
            handles = []
            for q in range(n_split):
                hn_q = sv["hn"] if n_split == 1 else sv["hn"][:, q * rows:(q + 1) * rows]
                sfx = tag if n_split == 1 else f"{tag}_{q}"
                d_win = _mm(hn_q, dproj, mode="tn", out_dtype=BF16, after=after, name=f"mm_dwin_{sfx}")
                parts_in = _win_to_shards(d_win, name=f"dw_in_shards_{sfx}")
                handles.append(_push_start(parts_in, own_slot(parts_in[my]), broadcast=False, after=after,
                                           name=f"exchange_w_in_{sfx}_start"))
                after = [handles[-1][4]]
            return handles

        if l == 0:
            small = _pack([loss_row, jnp.concatenate(d_norm_w[1:], axis=0), d_final_w,
                           jnp.stack([a[0, :N_HEADS] for a in d_alog]), jnp.stack([a[0, :N_HEADS] for a in d_dt]),
                           jnp.concatenate(d_dn_nw, axis=0), jnp.concatenate(d_hg_nw, axis=0), jnp.concatenate(d_lb, axis=0),
                           jnp.stack(d_conv)])
            small_all = _all_gather(small, name="gather_small")
        sent["win", l] = push_d_win([small_all] if l == 0 else [])
        dh, d_norm_w[l] = _mm(dproj, wi, mode="nt", out_dtype=F32, tile_m=512, tile_n=D_MODEL, tile_k=768,
                              after=[sent["win", l][-1][4]],
                              fused=("rms_bwd", sv["h"], norm_w[l], dh1), name=f"mm_dhn_{tag}")
    grad_x = dh[None]

    small_shapes = [(1, 128), (depth - 1, D_MODEL), final_norm_w.shape, dn_A_log.shape, dn_dt_bias.shape, dn_norm_w.shape,
                    hg_norm_w.shape, hg_lb_logits.shape, (depth, CONV_W, 3 * BR_WIDTH)]
    tot = _unpack(_sum_parts(small_all, after=[grad_x], name="sum_small"), small_shapes)
    loss = tot[0][0, 0]
    g_lb = tot[7]
    g_logits = jax.vjp(_lower_bounds, hg_lb_logits)[1](g_lb)[0]
    g_conv = lax.dynamic_slice_in_dim(tot[8], my * (3 * BR_WIDTH // N_DEV), 3 * BR_WIDTH // N_DEV, axis=2)
    small_g = [g_conv, tot[3], tot[4], tot[5], g_logits, tot[6], tot[2]]
    small_w = [dn_conv_w, dn_A_log, dn_dt_bias, dn_norm_w, hg_lb_logits, hg_norm_w, final_norm_w]
    small_m = [m_dn_conv_w, m_dn_A_log, m_dn_dt_bias, m_dn_norm_w, m_hg_lb_logits, m_hg_norm_w, m_final_norm_w]
    small_v = [v_dn_conv_w, v_dn_A_log, v_dn_dt_bias, v_dn_norm_w, v_hg_lb_logits, v_hg_norm_w, v_final_norm_w]
    pk_w = _pack(small_w)
    res_small = _adamw(_pack(small_g)[None], 0, pk_w, _pack(small_m), _pack(small_v), name="adamw_small", tr=pk_w.shape[0])
    shapes_w = [a.shape for a in small_w]
    sg, sd, sm, sv_ = (_unpack(r, shapes_w) for r in res_small)

    r_win = r_wo = r_wg = r_wu = None
    done = [grad_x, res_small[0]]

    def flat(a, cols):
        return a.reshape(-1, cols)

    for l in reversed(range(depth)):
        tag = f"l{l}"
        land_rest = _push_wait(sent["rest", l], done, broadcast=False, name=f"exchange_rest_{tag}_wait")
        r_wo = _adamw(land_rest, 0, flat(w_out, D_MODEL), flat(m_w_out, D_MODEL), flat(v_w_out, D_MODEL), layer=l,
                      n_layers=depth, prev=r_wo, name=f"adamw_w_out_{tag}", tr=rows_out)
        r_wg = _adamw(land_rest, g_off, flat(w_ple_gate, D_MODEL), flat(m_w_ple_gate, D_MODEL), flat(v_w_ple_gate, D_MODEL),
                      layer=l, n_layers=depth, prev=r_wg, name=f"adamw_w_gate_{tag}", tr=rows_out)
        r_wu = _adamw(land_rest, u_off, flat(w_ple_up, D_MODEL), flat(m_w_ple_up, D_MODEL), flat(v_w_ple_up, D_MODEL),
                      layer=l, n_layers=depth, prev=r_wu, name=f"adamw_w_up_{tag}", tr=up_rows)
        done = [r_wo[0], r_wg[0], r_wu[0]]
    for l in reversed(range(depth)):
        tag = f"l{l}"
        if l == 0:
            nw0 = _sum_parts(_all_gather(_pack([d_norm_w[0]]), after=done, name="gather_norm_w"), name="sum_norm_w")
            g_norm_w = jnp.concatenate([_unpack(nw0, [(1, D_MODEL)])[0], tot[1]], axis=0)
            pk_nw = _pack([norm_w])
            r_nw = _adamw(_pack([g_norm_w])[None], 0, pk_nw, _pack([m_norm_w]), _pack([v_norm_w]), name="adamw_norm_w",
                          tr=pk_nw.shape[0])
            r_nw = [_unpack(r, [norm_w.shape])[0] for r in r_nw]
            done = [r_nw[0]]
        n_split = len(sent["win", l])
        for q, handle in enumerate(sent["win", l]):
            sfx = tag if n_split == 1 else f"{tag}_{q}"
            land_in = _push_wait(handle, done, broadcast=False, name=f"exchange_w_in_{sfx}_wait")
            r_win = _adamw(land_in, 0, flat(w_in, SHARD_IN), flat(m_w_in, SHARD_IN), flat(v_w_in, SHARD_IN),
                           layer=l * n_split + q, n_layers=depth * n_split, prev=r_win, name=f"adamw_w_in_{sfx}", tr=256)
            done = [r_win[0]]
    r_win = [o.reshape(w_in.shape) for o in r_win]
    r_wo = [o.reshape(w_out.shape) for o in r_wo]
    r_wg = [o.reshape(w_ple_gate.shape) for o in r_wg]
    r_wu = [o.reshape(w_ple_up.shape) for o in r_wu]

    def order(nw, small_list, big_in, big_out, big_up, big_gate):
        cw, al_, dt_, dnw, lbl, hnw, fw = small_list
        return [nw, big_in, cw, al_, dt_, dnw, lbl, hnw, big_out, big_up, big_gate, fw]

    outs = [loss, grad_x]
    for i, sl in enumerate((sg, sd, sm, sv_)):
        outs += order(r_nw[i], sl, r_win[i], r_wo[i], r_wu[i], r_wg[i])
    return tuple(outs)
```

```python
import functools

import jax
import jax.numpy as jnp
from jax import lax
from jax.experimental import pallas as pl
from jax.experimental.pallas import tpu as pltpu

F32 = jnp.float32
BF16 = jnp.bfloat16
HIGHEST = lax.Precision.HIGHEST

N_DEV = 8
D_MODEL = 2048
PLE_DIM = 256
HEAD_DIM = 128
N_HEADS = 8
BR_WIDTH = N_HEADS * HEAD_DIM
CHUNK = 64
SUB = 16
CONV_W = 4
NORM_EPS = 1e-6
L2_EPS = 1e-6
IN_WIDTH = 8208
SHARD_IN = IN_WIDTH // N_DEV
EXP_CLAMP = 80.0

C_QKV, C_Z, C_HQ, C_HF, C_HI, C_HZ, C_B, C_A, N_PROJ = 0, 3072, 4096, 5120, 6144, 7168, 8192, 8320, 8448

ADAM_LR, ADAM_B1, ADAM_B2, ADAM_EPS, ADAM_WD, ADAM_STEP = 0.001, 0.9, 0.999, 1e-08, 0.01, 10

VMEM_LIMIT = 48 * 1024 * 1024


def _cp(*sem):
    return pltpu.CompilerParams(dimension_semantics=sem, vmem_limit_bytes=VMEM_LIMIT)


class _Heads:
    def __init__(self, vals):
        self.v = tuple(vals)

    def __add__(self, o):
        return _hmap(lambda a, b: a + b, self, o)

    def __radd__(self, o):
        return _hmap(lambda a, b: b + a, self, o)

    def __sub__(self, o):
        return _hmap(lambda a, b: a - b, self, o)

    def __rsub__(self, o):
        return _hmap(lambda a, b: b - a, self, o)

    def __mul__(self, o):
        return _hmap(lambda a, b: a * b, self, o)

    def __rmul__(self, o):
        return _hmap(lambda a, b: b * a, self, o)

    def __neg__(self):
        return _hmap(lambda a: -a, self)

    def __getitem__(self, idx):
        return _hmap(lambda a: a[idx], self)


def _hmap(fn, *args):
    n = next((len(a.v) for a in args if isinstance(a, _Heads)), None)
    if n is None:
        return fn(*args)
    return _Heads(fn(*[a.v[i] if isinstance(a, _Heads) else a for a in args]) for i in range(n))


def _dot(a, b, ca, cb):
    return _hmap(lambda x, y: lax.dot_general(x.astype(BF16), y.astype(BF16), (((ca,), (cb,)), ((), ())),
                                              preferred_element_type=F32), a, b)


def _nn(a, b):
    return _dot(a, b, 1, 0)


def _nt(a, b):
    return _dot(a, b, 1, 1)


def _tn(a, b):
    return _dot(a, b, 0, 0)


def _split(a):
    hi = _hmap(lambda x: x.astype(BF16), a)
    return hi, _hmap(lambda x, h: (x - h.astype(F32)).astype(BF16), a, hi)


def _dot3(a, b, ca, cb):
    ah, al = _split(a)
    bh, bl = _split(b)
    return _dot(ah, bh, ca, cb) + (_dot(ah, bl, ca, cb) + _dot(al, bh, ca, cb))


def _nn_exact(a, b):
    return _hmap(lambda y: lax.dot_general(a, y, (((1,), (0,)), ((), ())), precision=HIGHEST,
                                           preferred_element_type=F32), b)


def _exp(x):
    return _hmap(jnp.exp, x)


def _sum(x, axis):
    return _hmap(lambda a: jnp.sum(a, axis=axis, keepdims=True), x)


def _sigmoid(x):
    return jax.nn.sigmoid(x)


def _silu(x):
    return x * _sigmoid(x)


def _dsilu(x):
    s = _sigmoid(x)
    return s * (1.0 + x * (1.0 - s))


def _silu_and_grad(x):
    s = _sigmoid(x)
    return x * s, s * (1.0 + x * (1.0 - s))


def _softplus(x):
    return jnp.maximum(x, 0.0) + jnp.log(1.0 + jnp.exp(-jnp.abs(x)))


def _iota2(n, m, axis):
    return lax.broadcasted_iota(jnp.int32, (n, m), axis)


def _col2row(col, eye):
    return _hmap(lambda c: jnp.sum(eye * c, axis=0, keepdims=True), col)


def _row2col(row, eye):
    return _hmap(lambda r: jnp.sum(eye * r, axis=1, keepdims=True), row)


def _pick_lane(block, lane_idx):
    lane = _iota2(block.shape[0], block.shape[1], 1)
    return jnp.sum(jnp.where(lane == lane_idx, block, 0.0), axis=1, keepdims=True)


MM_TILE_M, MM_TILE_N, MM_TILE_K = 1024, 1408, 2048


def _tile(dim, cap):
    if dim <= cap:
        return dim
    t = cap - cap % 128
    while dim % t:
        t -= 128
    return t


def _mm(a, b, *, mode, out_dtype, res=None, after=(), b_rows=None, tile_m=MM_TILE_M, tile_n=MM_TILE_N, tile_k=MM_TILE_K,
        fused=None, name):
    b_mat_rows = b.shape[0] if b_rows is None else N_DEV * b_rows[1]
    if mode == "nn":
        (m, kd), n = a.shape, b.shape[-1]
        assert kd == b_mat_rows
    elif mode == "nt":
        (m, kd), n = a.shape, b_mat_rows
    else:
        (kd, m), n = a.shape, b.shape[-1]
    tm, tn, tk = _tile(m, tile_m), _tile(n, tile_n), _tile(kd, tile_k)
    assert m % tm == 0 and n % tn == 0 and kd % tk == 0, (m, n, kd, tm, tn, tk)
    nk = kd // tk
    ca, cb = {"nn": (1, 0), "nt": (1, 1), "tn": (0, 0)}[mode]

    kind = None if fused is None else fused[0]
    n_in = 2 + (res is not None) + (0 if fused is None else len(fused) - 1)
    n_out = 1 if fused is None else 2

    def body(*refs):
        a_ref, b_ref = refs[:2]
        r_ref = None if res is None else refs[2]
        extra = refs[2 + (res is not None):n_in]
        outs = refs[-1 - n_out:-1]
        o_ref, acc_ref = outs[0], refs[-1]
        k = pl.program_id(2)

        @pl.when(k == 0)
        def _():
            acc_ref[...] = jnp.zeros_like(acc_ref)

        b_tile = b_ref[...]
        if b_rows is not None:
            b_tile = b_tile.reshape(-1, b_tile.shape[-1])
        acc_ref[...] += _dot(a_ref[...], b_tile, ca, cb)

        @pl.when(k == nk - 1)
        def _():
            out = acc_ref[...]
            if r_ref is not None:
                out = out + r_ref[...].astype(F32)
            if kind == "ple":
                h1_ref, up_ref = extra
                o_ref[...] = out
                outs[1][...] = h1_ref[...] + up_ref[...] * _sigmoid(out)
            elif kind == "rms_bwd":
                h_ref, w_ref, res_ref = extra
                dx, dwt = _rms_bwd_math(h_ref[...], w_ref[...], out)
                o_ref[...] = res_ref[...] + dx

                @pl.when(pl.program_id(0) == 0)
                def _():
                    outs[1][...] = jnp.zeros_like(outs[1])

                outs[1][...] += jnp.sum(dwt, axis=0, keepdims=True)
            else:
                o_ref[...] = out.astype(o_ref.dtype)

    a_spec = pl.BlockSpec((tk, tm), lambda i, j, k: (k, i)) if mode == "tn" else pl.BlockSpec((tm, tk), lambda i, j, k: (i, k))
    if b_rows is None:
        b_spec = pl.BlockSpec((tn, tk), lambda i, j, k: (j, k)) if mode == "nt" else pl.BlockSpec((tk, tn), lambda i, j, k: (k, j))
    else:
        first, count = b_rows
        assert first % count == 0 and mode in ("nn", "nt")
        rb = first // count
        if mode == "nn":
            assert tk == kd
            b_spec = pl.BlockSpec((N_DEV, count, tn), lambda i, j, k: (0, rb, j))
        else:
            assert tn % count == 0
            b_spec = pl.BlockSpec((tn // count, count, tk), lambda i, j, k: (j, rb, k))
    o_spec = pl.BlockSpec((tm, tn), lambda i, j, k: (i, j))
    row_spec = pl.BlockSpec((1, tn), lambda i, j, k: (0, j))
    extra_specs, extra_args, out_specs, out_shape = [], (), o_spec, jax.ShapeDtypeStruct((m, n), out_dtype)
    sem = ("parallel", "parallel", "arbitrary")
    if kind == "ple":
        extra_specs, extra_args = [o_spec, o_spec], tuple(fused[1:])
        out_specs, out_shape = [o_spec, o_spec], [jax.ShapeDtypeStruct((m, n), F32)] * 2
    elif kind == "rms_bwd":
        assert tn == n
        extra_specs, extra_args = [o_spec, row_spec, o_spec], (fused[1], fused[2].reshape(1, n), fused[3])
        out_specs, out_shape = [o_spec, row_spec], [jax.ShapeDtypeStruct((m, n), F32), jax.ShapeDtypeStruct((1, n), F32)]
        sem = ("arbitrary", "arbitrary", "arbitrary")
    in_specs = ([a_spec, b_spec] + ([o_spec] if res is not None else []) + extra_specs
                + [pl.BlockSpec(memory_space=pl.ANY)] * len(after))
    args = (a, b) + ((res,) if res is not None else ()) + extra_args + tuple(after)
    return pl.pallas_call(
        body, name=name, grid=(m // tm, n // tn, nk), in_specs=in_specs, out_specs=out_specs, out_shape=out_shape,
        scratch_shapes=[pltpu.VMEM((tm, tn), F32)], compiler_params=_cp(*sem),
    )(*args)


ROW_TILE = 256


def _rms_fwd(h, w, *, name):
    s, d = h.shape
    tr = min(ROW_TILE, s)

    def body(h_ref, w_ref, o_ref):
        x = h_ref[...]
        r = lax.rsqrt(jnp.mean(x * x, axis=-1, keepdims=True) + NORM_EPS)
        o_ref[...] = (x * r * w_ref[...]).astype(o_ref.dtype)

    return pl.pallas_call(
        body, name=name, grid=(s // tr,),
        in_specs=[pl.BlockSpec((tr, d), lambda i: (i, 0)), pl.BlockSpec((1, d), lambda i: (0, 0))],
        out_specs=pl.BlockSpec((tr, d), lambda i: (i, 0)),
        out_shape=jax.ShapeDtypeStruct((s, d), BF16), compiler_params=_cp("parallel"),
    )(h, w.reshape(1, d))


def _rms_bwd_math(x, w, dy):
    d = x.shape[-1]
    r = lax.rsqrt(jnp.mean(x * x, axis=-1, keepdims=True) + NORM_EPS)
    gw = dy * w
    dx = r * gw - x * ((r * r * r) * (jnp.sum(gw * x, axis=-1, keepdims=True) / d))
    return dx, dy * x * r


def _final_fwd_bwd(h, w, tgt, *, name):
    s, d = h.shape
    tr = min(ROW_TILE, s)

    def body(h_ref, w_ref, t_ref, loss_ref, dh_ref, dw_ref):
        @pl.when(pl.program_id(0) == 0)
        def _():
            loss_ref[...] = jnp.zeros_like(loss_ref)
            dw_ref[...] = jnp.zeros_like(dw_ref)

        x = h_ref[...]
        wv = w_ref[...]
        r = lax.rsqrt(jnp.mean(x * x, axis=-1, keepdims=True) + NORM_EPS)
        err = x * r * wv - t_ref[...]
        row_loss = jnp.mean(err * err, axis=-1, keepdims=True)
        loss_ref[...] += 0.5 * jnp.sum(row_loss, axis=0, keepdims=True)
        dx, dwt = _rms_bwd_math(x, wv, err / d)
        dh_ref[...] = dx
        dw_ref[...] += jnp.sum(dwt, axis=0, keepdims=True)

    row = pl.BlockSpec((tr, d), lambda i: (i, 0))
    vec = pl.BlockSpec((1, d), lambda i: (0, 0))
    return pl.pallas_call(
        body, name=name, grid=(s // tr,), in_specs=[row, vec, row],
        out_specs=[pl.BlockSpec((1, 128), lambda i: (0, 0)), row, vec],
        out_shape=[jax.ShapeDtypeStruct((1, 128), F32), jax.ShapeDtypeStruct((s, d), F32),
                   jax.ShapeDtypeStruct((1, d), F32)],
        compiler_params=_cp("arbitrary"),
    )(h, w.reshape(1, d), tgt)


def _ple_bwd(dh2, gate_pre, up, *, name):
    s, d = dh2.shape
    tr = min(ROW_TILE, s)

    def body(d_ref, g_ref, u_ref, dup_ref, dgp_ref):
        dh = d_ref[...]
        gate = _sigmoid(g_ref[...])
        dup_ref[...] = (dh * gate).astype(BF16)
        dgp_ref[...] = (dh * u_ref[...] * gate * (1.0 - gate)).astype(BF16)

    row = pl.BlockSpec((tr, d), lambda i: (i, 0))
    return pl.pallas_call(body, name=name, grid=(s // tr,), in_specs=[row, row, row], out_specs=[row, row],
                          out_shape=[jax.ShapeDtypeStruct((s, d), BF16)] * 2, compiler_params=_cp("parallel"))(dh2, gate_pre, up)


def _head_norm_fwd(o, z, w):
    return _hmap(lambda x, zz: (x * lax.rsqrt(jnp.mean(x * x, axis=-1, keepdims=True) + NORM_EPS) * w * _silu(zz)).astype(BF16),
                 o, z)


def _head_norm_bwd(o, z, w, dy):
    dos, dzs, dw = [], [], jnp.zeros((1, HEAD_DIM), F32)
    for x, zz, g in zip(o.v, z.v, dy.v):
        r = lax.rsqrt(jnp.mean(x * x, axis=-1, keepdims=True) + NORM_EPS)
        silu_z, dsilu_z = _silu_and_grad(zz)
        don = g * silu_z
        dzs.append((g * (x * r * w) * dsilu_z).astype(BF16))
        gw = don * w
        dos.append(r * gw - x * ((r * r * r) * (jnp.sum(gw * x, axis=-1, keepdims=True) / HEAD_DIM)))
        dw = dw + jnp.sum(don * x * r, axis=0, keepdims=True)
    return _Heads(dos), _Heads(dzs), dw


def _conv_silu(x, w, s):
    row = _iota2(s, x.shape[1], 0)
    c = w[CONV_W - 1:CONV_W, :] * x
    for k in range(1, CONV_W):
        c = c + w[CONV_W - 1 - k:CONV_W - k, :] * jnp.where(row >= k, pltpu.roll(x, k, 0), 0.0)
    return c


def _dn_qkv_fwd(proj, conv_w, *, name):
    s = proj.shape[0]
    nb = 3 * N_HEADS

    def body(x_ref, w_ref, o_ref):
        j = pl.program_id(0)
        sv = _silu(_conv_silu(x_ref[...], w_ref[...], s))
        r = lax.rsqrt(jnp.sum(sv * sv, axis=-1, keepdims=True) + L2_EPS)
        scale = jnp.where(j < N_HEADS, HEAD_DIM ** -0.5, 1.0).astype(F32)
        o_ref[...] = jnp.where(j < 2 * N_HEADS, sv * r * scale, sv)

    return pl.pallas_call(
        body, name=name, grid=(nb,),
        in_specs=[pl.BlockSpec((s, HEAD_DIM), lambda j: (0, j)), pl.BlockSpec((CONV_W, HEAD_DIM), lambda j: (0, j))],
        out_specs=pl.BlockSpec((s, HEAD_DIM), lambda j: (0, j)),
        out_shape=jax.ShapeDtypeStruct((s, 3 * BR_WIDTH), F32), compiler_params=_cp("parallel"),
    )(proj, conv_w)


def _dn_qkv_bwd(proj, conv_w, dqkv, *, name):
    s = proj.shape[0]
    nb = 3 * N_HEADS

    def body(x_ref, w_ref, g_ref, dx_ref, dw_ref):
        j = pl.program_id(0)
        x, w, g = x_ref[...], w_ref[...], g_ref[...]
        c = _conv_silu(x, w, s)
        sv, dsv = _silu_and_grad(c)
        r = lax.rsqrt(jnp.sum(sv * sv, axis=-1, keepdims=True) + L2_EPS)
        scale = jnp.where(j < N_HEADS, HEAD_DIM ** -0.5, 1.0).astype(F32)
        ds_n = scale * (r * g - sv * ((r * r * r) * jnp.sum(g * sv, axis=-1, keepdims=True)))
        dc = jnp.where(j < 2 * N_HEADS, ds_n, g) * dsv
        row = _iota2(s, HEAD_DIM, 0)
        dx = w[CONV_W - 1:CONV_W, :] * dc
        dws = [jnp.sum(dc * x, axis=0, keepdims=True)]
        for k in range(1, CONV_W):
            dc_ahead = jnp.where(row < s - k, pltpu.roll(dc, s - k, 0), 0.0)
            dx = dx + w[CONV_W - 1 - k:CONV_W - k, :] * dc_ahead
            dws.append(jnp.sum(dc_ahead * x, axis=0, keepdims=True))
        dx_ref[...] = dx.astype(BF16)
        for k in range(CONV_W):
            dw_ref[CONV_W - 1 - k:CONV_W - k, :] = dws[k]

    blk = pl.BlockSpec((s, HEAD_DIM), lambda j: (0, j))
    wblk = pl.BlockSpec((CONV_W, HEAD_DIM), lambda j: (0, j))
    return pl.pallas_call(
        body, name=name, grid=(nb,), in_specs=[blk, wblk, blk], out_specs=[blk, wblk],
        out_shape=[jax.ShapeDtypeStruct((s, 3 * BR_WIDTH), BF16), jax.ShapeDtypeStruct((CONV_W, 3 * BR_WIDTH), F32)],
        compiler_params=_cp("parallel"),
    )(proj, conv_w, dqkv)


def _tri(n, kind):
    r, c = _iota2(n, n, 0), _iota2(n, n, 1)
    if kind == "lower":
        return (r >= c).astype(F32)
    if kind == "upper":
        return (r <= c).astype(F32)
    return (r == c).astype(F32)


GATE_TILE = 512


def _dn_gate_fwd(proj, a_log, dt_bias, *, name):
    s = proj.shape[0]
    tr = min(GATE_TILE, s)

    def body(b_ref, a_ref, al_ref, dt_ref, beta_ref, g_ref):
        beta_ref[...] = _sigmoid(b_ref[...])
        g = -jnp.exp(al_ref[...]) * _softplus(a_ref[...] + dt_ref[...])
        low = _tri(CHUNK, "lower")
        for c in range(tr // CHUNK):
            rows = slice(c * CHUNK, (c + 1) * CHUNK)
            g_ref[rows, :] = _nn_exact(low, g[rows, :])

    blk = lambda cb: pl.BlockSpec((tr, HEAD_DIM), lambda i: (i, cb))
    vec = pl.BlockSpec((1, HEAD_DIM), lambda i: (0, 0))
    out = pl.BlockSpec((tr, HEAD_DIM), lambda i: (i, 0))
    return pl.pallas_call(
        body, name=name, grid=(s // tr,), in_specs=[blk(C_B // HEAD_DIM), blk(C_A // HEAD_DIM), vec, vec],
        out_specs=[out, out], out_shape=[jax.ShapeDtypeStruct((s, HEAD_DIM), F32)] * 2, compiler_params=_cp("parallel"),
    )(proj, proj, a_log, dt_bias)


def _dn_gate_bwd(proj, a_log, dt_bias, dbeta, d_g, *, name):
    s = proj.shape[0]
    tr = min(GATE_TILE, s)

    def body(b_ref, a_ref, al_ref, dt_ref, dbeta_ref, dG_ref, db_ref, da_ref, dal_ref, ddt_ref):
        @pl.when(pl.program_id(0) == 0)
        def _():
            dal_ref[...] = jnp.zeros_like(dal_ref)
            ddt_ref[...] = jnp.zeros_like(ddt_ref)

        beta = _sigmoid(b_ref[...])
        db_ref[...] = (dbeta_ref[...] * beta * (1.0 - beta)).astype(BF16)
        pre = a_ref[...] + dt_ref[...]
        neg_ea = -jnp.exp(al_ref[...])
        up = _tri(CHUNK, "upper")
        d_g = dG_ref[...]
        dg = jnp.concatenate([_nn_exact(up, d_g[c * CHUNK:(c + 1) * CHUNK, :]) for c in range(tr // CHUNK)], axis=0)
        da = dg * neg_ea * _sigmoid(pre)
        da_ref[...] = da.astype(BF16)
        ddt_ref[...] += jnp.sum(da, axis=0, keepdims=True)
        dal_ref[...] += jnp.sum(dg * neg_ea * _softplus(pre), axis=0, keepdims=True)

    blk = lambda cb: pl.BlockSpec((tr, HEAD_DIM), lambda i: (i, cb))
    vec = pl.BlockSpec((1, HEAD_DIM), lambda i: (0, 0))
    io = pl.BlockSpec((tr, HEAD_DIM), lambda i: (i, 0))
    return pl.pallas_call(
        body, name=name, grid=(s // tr,),
        in_specs=[blk(C_B // HEAD_DIM), blk(C_A // HEAD_DIM), vec, vec, io, io], out_specs=[io, io, vec, vec],
        out_shape=[jax.ShapeDtypeStruct((s, HEAD_DIM), BF16)] * 2 + [jax.ShapeDtypeStruct((1, HEAD_DIM), F32)] * 2,
        compiler_params=_cp("arbitrary"),
    )(proj, proj, a_log, dt_bias, dbeta, d_g)


def _unit_lower_inverse(a_strict, eye):
    x = -a_strict
    t = x + eye
    p = x
    n = 2
    while n < CHUNK:
        p = _nn(p, p)
        t = t + _nn(t, p)
        n *= 2
    return t


def _rows(*xs):
    return _hmap(lambda *a: jnp.concatenate(a, axis=0), *xs)


def _lanes(*xs):
    return _hmap(lambda *a: jnp.concatenate(a, axis=1), *xs)


def _dn_chunk_common(q, k, v, gc, beta, st, with_qd_state, t_inv=None):
    c, d = CHUNK, HEAD_DIM
    eye = _tri(c, "eye")
    low = _tri(c, "lower")
    strict = low - eye
    grow = _col2row(gc, eye)
    dec = _hmap(lambda g_, gr: low * jnp.exp(low * (g_ - gr)), gc, grow)
    kb = k * beta
    kq = _nt(_rows(kb, q), k)
    a_mat = kq[0:c, :] * dec * strict
    qk = kq[c:2 * c, :] * dec
    if t_inv is None:
        t_inv = _unit_lower_inverse(a_mat, eye)
    e_g = _exp(gc)
    qd = q * e_g
    uw = _nn(t_inv, _lanes(v * beta, kb * e_g))
    u, w = uw[:, 0:d], uw[:, d:2 * d]
    last = (_iota2(c, 1, 0) == c - 1).astype(F32)
    g_last = _sum(gc * last, 0)
    e_t = _exp(g_last - gc)
    kt = k * e_t
    tail = _exp(g_last)
    if with_qd_state:
        ws = _nn(_rows(w, qd), st)
        vn, qds = u - ws[0:c, :], ws[c:2 * c, :]
    else:
        vn, qds = u - _nn(w, st), None
    return dict(eye=eye, low=low, strict=strict, dec=dec, kb=kb, a_mat=a_mat, t_inv=t_inv, e_g=e_g, u=u, w=w, uw=uw,
                qk=qk, qd=qd, qds=qds, last=last, e_t=e_t, kt=kt, tail=tail, vn=vn)


def _dn_chunk_fwd_math(q, k, v, gc, beta, st):
    m = _dn_chunk_common(q, k, v, gc, beta, st, True)
    o = m["qds"] + _nn(m["qk"], m["vn"])
    st2 = st * m["tail"] + _tn(m["kt"], m["vn"])
    return o, st2, m["t_inv"]


def _dn_chunk_bwd_math(q, k, v, gc, beta, st, do, dst2, t_inv=None):
    c, d = CHUNK, HEAD_DIM
    m = _dn_chunk_common(q, k, v, gc, beta, st, False, t_inv)
    eye, low, strict = m["eye"], m["low"], m["strict"]
    dvn = _tn(m["qk"], do) + _nn(m["kt"], dst2)
    dqk = _nt(do, m["vn"]) * low
    both = _rows(do, dvn)
    ds_both = _nt(both, st)
    dqd, dw = ds_both[0:c, :], -ds_both[c:2 * c, :]
    dst = _tn(_rows(m["qd"], -m["w"]), both) + dst2 * m["tail"]
    dkt = _nt(m["vn"], dst2)
    dtail = _sum(_sum(st * dst2, 1), 0)
    dvb_dkg = _tn(m["t_inv"], _lanes(dvn, dw))
    dvb, dkg = dvb_dkg[:, 0:d], dvb_dkg[:, d:2 * d]
    d_a = _nt(dvb_dkg, m["uw"]) * (-strict)
    dkk = d_a * m["dec"]
    dp = dqk * m["dec"]
    dpk = _rows(dp, dkk)
    dq_dkb = _nn(dpk, k)
    dq = dq_dkb[0:c, :] + dqd * m["e_g"]
    dkb = dq_dkb[c:2 * c, :] + dkg * m["e_g"]
    dk = _tn(dpk, _rows(q, m["kb"])) + dkb * beta + dkt * m["e_t"]
    dv = dvb * beta
    dbeta = _sum(dvb * v + dkb * k, 1)
    de_g = _sum(dkg * m["kb"] + dqd * q, 1)
    de_t = _sum(dkt * k, 1)
    mm = d_a * m["a_mat"] + dqk * m["qk"]
    dgc = (_sum(mm, 1) - _row2col(_sum(mm, 0), eye) + de_g * m["e_g"] - de_t * m["e_t"]
           + (_sum(de_t * m["e_t"], 0) + dtail * m["tail"]) * m["last"])
    return dq, dk, dv, dgc, dbeta, dst


def _heads_of(ref):
    return _Heads(ref[:, h * HEAD_DIM:(h + 1) * HEAD_DIM] for h in range(N_HEADS))


def _lanes_of(block):
    return _Heads(_pick_lane(block, h) for h in range(N_HEADS))


def _dn_chunk_fwd(qkv, gcs, beta, proj, norm_w, *, name):
    s = qkv.shape[0]
    n = s // CHUNK

    def body(q_ref, k_ref, v_ref, g_ref, b_ref, z_ref, w_ref, o_ref, st_out_ref, tinv_ref, y_ref, st_ref):
        @pl.when(pl.program_id(0) == 0)
        def _():
            st_ref[...] = jnp.zeros_like(st_ref)

        gblk, bblk = g_ref[...], b_ref[...]
        st = _Heads(st_ref[h] for h in range(N_HEADS))
        o, st2, t_inv = _dn_chunk_fwd_math(_heads_of(q_ref), _heads_of(k_ref), _heads_of(v_ref), _lanes_of(gblk),
                                           _lanes_of(bblk), st)
        y = _head_norm_fwd(o, _heads_of(z_ref), w_ref[...])
        for h in range(N_HEADS):
            st_out_ref[0, h] = st.v[h]
            tinv_ref[0, h] = t_inv.v[h].astype(BF16)
            o_ref[:, h * HEAD_DIM:(h + 1) * HEAD_DIM] = o.v[h]
            y_ref[:, h * HEAD_DIM:(h + 1) * HEAD_DIM] = y.v[h]
            st_ref[h] = st2.v[h]

    blk = lambda off: pl.BlockSpec((CHUNK, BR_WIDTH), lambda c: (c, off))
    sc = pl.BlockSpec((CHUNK, HEAD_DIM), lambda c: (c, 0))
    return pl.pallas_call(
        body, name=name, grid=(n,),
        in_specs=[blk(0), blk(1), blk(2), sc, sc, blk(C_Z // BR_WIDTH), pl.BlockSpec((1, HEAD_DIM), lambda c: (0, 0))],
        out_specs=[blk(0), pl.BlockSpec((1, N_HEADS, HEAD_DIM, HEAD_DIM), lambda c: (c, 0, 0, 0)),
                   pl.BlockSpec((1, N_HEADS, CHUNK, CHUNK), lambda c: (c, 0, 0, 0)), blk(0)],
        out_shape=[jax.ShapeDtypeStruct((s, BR_WIDTH), F32), jax.ShapeDtypeStruct((n, N_HEADS, HEAD_DIM, HEAD_DIM), F32),
                   jax.ShapeDtypeStruct((n, N_HEADS, CHUNK, CHUNK), BF16), jax.ShapeDtypeStruct((s, BR_WIDTH), BF16)],
        scratch_shapes=[pltpu.VMEM((N_HEADS, HEAD_DIM, HEAD_DIM), F32)],
        compiler_params=_cp("arbitrary"),
    )(qkv, qkv, qkv, gcs, beta, proj, norm_w.reshape(1, HEAD_DIM))


def _dn_chunk_bwd(qkv, gcs, beta, states, tinvs, o, proj, norm_w, dy, *, name):
    s = qkv.shape[0]
    n = s // CHUNK

    def body(q_ref, k_ref, v_ref, g_ref, b_ref, st_in_ref, tinv_ref, o_ref, z_ref, w_ref, dy_ref,
             dqkv_ref, dg_ref, dbeta_ref, dz_ref, dw_ref, dst_ref):
        @pl.when(pl.program_id(0) == 0)
        def _():
            dst_ref[...] = jnp.zeros_like(dst_ref)
            dw_ref[...] = jnp.zeros_like(dw_ref)

        do, dz, dw = _head_norm_bwd(_heads_of(o_ref), _heads_of(z_ref), w_ref[...], _heads_of(dy_ref))
        dw_ref[...] += dw

        gblk, bblk = g_ref[...], b_ref[...]
        lane = _iota2(CHUNK, HEAD_DIM, 1)
        dg_all = jnp.zeros((CHUNK, HEAD_DIM), F32)
        dbeta_all = jnp.zeros((CHUNK, HEAD_DIM), F32)
        dq, dk, dv, dgc, dbeta, dst = _dn_chunk_bwd_math(
            _heads_of(q_ref), _heads_of(k_ref), _heads_of(v_ref), _lanes_of(gblk), _lanes_of(bblk),
            _Heads(st_in_ref[0, h] for h in range(N_HEADS)), do,
            _Heads(dst_ref[h] for h in range(N_HEADS)), _Heads(tinv_ref[0, h] for h in range(N_HEADS)))
        for h in range(N_HEADS):
            dz_ref[:, h * HEAD_DIM:(h + 1) * HEAD_DIM] = dz.v[h]
            for part, val in enumerate((dq, dk, dv)):
                c0 = part * BR_WIDTH + h * HEAD_DIM
                dqkv_ref[:, c0:c0 + HEAD_DIM] = val.v[h]
            dg_all = jnp.where(lane == h, dgc.v[h], dg_all)
            dbeta_all = jnp.where(lane == h, dbeta.v[h], dbeta_all)
            dst_ref[h] = dst.v[h]
        dg_ref[...] = dg_all
        dbeta_ref[...] = dbeta_all

    blk = lambda off: pl.BlockSpec((CHUNK, BR_WIDTH), lambda c: (n - 1 - c, off))
    sc = pl.BlockSpec((CHUNK, HEAD_DIM), lambda c: (n - 1 - c, 0))
    vec = pl.BlockSpec((1, HEAD_DIM), lambda c: (0, 0))
    outs = pl.pallas_call(
        body, name=name, grid=(n,),
        in_specs=[blk(0), blk(1), blk(2), sc, sc,
                  pl.BlockSpec((1, N_HEADS, HEAD_DIM, HEAD_DIM), lambda c: (n - 1 - c, 0, 0, 0)),
                  pl.BlockSpec((1, N_HEADS, CHUNK, CHUNK), lambda c: (n - 1 - c, 0, 0, 0)), blk(0), blk(C_Z // BR_WIDTH),
                  vec, blk(0)],
        out_specs=[pl.BlockSpec((CHUNK, 3 * BR_WIDTH), lambda c: (n - 1 - c, 0)), sc, sc, blk(0), vec],
        out_shape=[jax.ShapeDtypeStruct((s, 3 * BR_WIDTH), F32)] + [jax.ShapeDtypeStruct((s, HEAD_DIM), F32)] * 2
        + [jax.ShapeDtypeStruct((s, BR_WIDTH), BF16), jax.ShapeDtypeStruct((1, HEAD_DIM), F32)],
        scratch_shapes=[pltpu.VMEM((N_HEADS, HEAD_DIM, HEAD_DIM), F32)],
        compiler_params=_cp("arbitrary"),
    )(qkv, qkv, qkv, gcs, beta, states, tinvs, o, proj, norm_w.reshape(1, HEAD_DIM), dy)
    return outs


def _hg_chunk_common(q, k, g):
    c, nb = CHUNK, CHUNK // SUB
    e_g = _exp(g)
    qd = q * e_g
    g_last = g[c - 1:c, :]
    e_t = _exp(g_last - g)
    kt = k * e_t
    tail = _exp(g_last)
    g_refs = [g[i * SUB:i * SUB + 1, :] for i in range(nb)]
    g_ref_rows = _hmap(lambda *rows: jnp.concatenate([jnp.broadcast_to(r, (SUB, r.shape[1])) for r in rows], axis=0), *g_refs)
    e_q = _exp(g - g_ref_rows)
    q_sc = q * e_q
    e_k = [_hmap(lambda gr, g_: jnp.exp(jnp.minimum(gr - g_, EXP_CLAMP)), g_refs[i], g) for i in range(nb)]
    k_sc_all = _rows(*[k * e_k[i] for i in range(nb)])
    row_blk = _iota2(c, 1, 0) // SUB
    masks = [(row_blk == i).astype(F32) for i in range(nb)]
    r_all = _nt(q_sc, k_sc_all)
    a_mat = r_all[:, 0:c] * masks[0]
    for i in range(1, nb):
        a_mat = a_mat + r_all[:, i * c:(i + 1) * c] * masks[i]
    a_mat = a_mat * _tri(c, "lower")
    return dict(e_g=e_g, qd=qd, e_t=e_t, kt=kt, tail=tail, q_sc=q_sc, k_sc_all=k_sc_all, e_q=e_q, e_k=e_k, masks=masks,
                a_mat=a_mat)


def _hg_chunk_fwd_math(q, k, v, g, stt):
    m = _hg_chunk_common(q, k, g)
    o = _nt(m["qd"], stt) + _nn(m["a_mat"], v)
    stt2 = stt * m["tail"] + _tn(v, m["kt"])
    return o, stt2


def _hg_chunk_bwd_math(q, k, v, g, stt, do, dstt2):
    c, nb = CHUNK, CHUNK // SUB
    m = _hg_chunk_common(q, k, g)
    stt2 = stt * m["tail"] + _tn(v, m["kt"])
    later = _sum(stt2 * dstt2, 0)
    dqd = _dot3(do, stt, 1, 0)
    dstt = _tn(do, m["qd"]) + dstt2 * m["tail"]
    d_a = _dot3(do, v, 1, 1) * _tri(c, "lower")
    dv = _tn(m["a_mat"], do) + _nt(m["kt"], dstt2)
    dkt = _dot3(v, dstt2, 1, 0)
    d_blk = _lanes(*[d_a * m["masks"][i] for i in range(nb)])
    dq = dqd * m["e_g"] + _dot3(d_blk, m["k_sc_all"], 1, 0) * m["e_q"]
    dks = _dot3(d_blk, m["q_sc"], 0, 0)
    dk = dkt * m["e_t"]
    for i in range(nb):
        dk = dk + dks[i * c:(i + 1) * c, :] * m["e_k"][i]
    db = q * dq - k * dk
    return dq, dk, dv, db, later, dstt


def _hg_chunk_fwd(proj, lb, norm_w, *, name):
    s = proj.shape[0]
    n = s // CHUNK

    def body(hq_ref, hf_ref, v_ref, lb_ref, z_ref, w_ref, o_ref, st_out_ref, q_out, k_out, lf_out, y_ref, st_ref):
        @pl.when(pl.program_id(0) == 0)
        def _():
            st_ref[...] = jnp.zeros_like(st_ref)

        f, lbv = hf_ref[...], lb_ref[...]
        q_all = _silu(hq_ref[...])
        k_all = (1.0 - lbv) * _sigmoid(-f)
        lf_all = jnp.log(lbv + (1.0 - lbv) * _sigmoid(f))
        q_out[...], k_out[...], lf_out[...] = q_all, k_all, lf_all
        st = _Heads(st_ref[h] for h in range(N_HEADS))
        g_all = _nn_exact(_tri(CHUNK, "lower"), lf_all)
        o, st2 = _hg_chunk_fwd_math(_heads_of(q_all), _heads_of(k_all), _heads_of(v_ref), _heads_of(g_all), st)
        y = _head_norm_fwd(o, _heads_of(z_ref), w_ref[...])
        for h in range(N_HEADS):
            st_out_ref[0, h] = st.v[h]
            o_ref[:, h * HEAD_DIM:(h + 1) * HEAD_DIM] = o.v[h]
            y_ref[:, h * HEAD_DIM:(h + 1) * HEAD_DIM] = y.v[h]
            st_ref[h] = st2.v[h]

    blk = lambda off: pl.BlockSpec((CHUNK, BR_WIDTH), lambda c: (c, off))
    return pl.pallas_call(
        body, name=name, grid=(n,),
        in_specs=[blk(C_HQ // BR_WIDTH), blk(C_HF // BR_WIDTH), blk(C_HI // BR_WIDTH), pl.BlockSpec((1, BR_WIDTH), lambda c: (0, 0)),
                  blk(C_HZ // BR_WIDTH), pl.BlockSpec((1, HEAD_DIM), lambda c: (0, 0))],
        out_specs=[blk(0), pl.BlockSpec((1, N_HEADS, HEAD_DIM, HEAD_DIM), lambda c: (c, 0, 0, 0)), blk(0), blk(0), blk(0), blk(0)],
        out_shape=[jax.ShapeDtypeStruct((s, BR_WIDTH), F32), jax.ShapeDtypeStruct((n, N_HEADS, HEAD_DIM, HEAD_DIM), F32)]
        + [jax.ShapeDtypeStruct((s, BR_WIDTH), F32)] * 3 + [jax.ShapeDtypeStruct((s, BR_WIDTH), BF16)],
        scratch_shapes=[pltpu.VMEM((N_HEADS, HEAD_DIM, HEAD_DIM), F32)],
        compiler_params=_cp("arbitrary"),
    )(proj, proj, proj, lb, proj, norm_w.reshape(1, HEAD_DIM))


def _hg_chunk_bwd(proj, lb, qh, kh, lf, states, o, norm_w, dy, *, name):
    s = proj.shape[0]
    n = s // CHUNK

    def body(hq_ref, hf_ref, v_ref, lb_ref, q_ref, k_ref, lf_ref, st_in_ref, o_ref, z_ref, w_ref, dy_ref,
             dhq_ref, dhf_ref, dhi_ref, dz_ref, dlb_ref, dw_ref, dst_ref):
        @pl.when(pl.program_id(0) == 0)
        def _():
            dst_ref[...] = jnp.zeros_like(dst_ref)
            dlb_ref[...] = jnp.zeros_like(dlb_ref)
            dw_ref[...] = jnp.zeros_like(dw_ref)

        do, dz, dw = _head_norm_bwd(_heads_of(o_ref), _heads_of(z_ref), w_ref[...], _heads_of(dy_ref))
        dw_ref[...] += dw

        g_all = _nn_exact(_tri(CHUNK, "lower"), lf_ref[...])
        dq, dk, dv, db, later, dst = _hg_chunk_bwd_math(
            _heads_of(q_ref), _heads_of(k_ref), _heads_of(v_ref), _heads_of(g_all),
            _Heads(st_in_ref[0, h] for h in range(N_HEADS)), do,
            _Heads(dst_ref[h] for h in range(N_HEADS)))
        dlf = _nn_exact(_tri(CHUNK, "upper"), jnp.concatenate(db.v, axis=1)) + jnp.concatenate(later.v, axis=1)
        dq_all, dk_all = jnp.concatenate(dq.v, axis=1), jnp.concatenate(dk.v, axis=1)
        f, lbv = hf_ref[...], lb_ref[...]
        dhq_ref[...] = (dq_all * _dsilu(hq_ref[...])).astype(BF16)
        sp, sn = _sigmoid(f), _sigmoid(-f)
        dlf_over = dlf / (lbv + (1.0 - lbv) * sp)
        dhf_ref[...] = (dlf_over * (1.0 - lbv) * sp * sn - dk_all * (1.0 - lbv) * sn * (1.0 - sn)).astype(BF16)
        dlb_ref[...] += jnp.sum(dlf_over * (1.0 - sp) - dk_all * sn, axis=0, keepdims=True)
        for h in range(N_HEADS):
            dhi_ref[:, h * HEAD_DIM:(h + 1) * HEAD_DIM] = dv.v[h].astype(BF16)
            dz_ref[:, h * HEAD_DIM:(h + 1) * HEAD_DIM] = dz.v[h]
            dst_ref[h] = dst.v[h]

    blk = lambda off: pl.BlockSpec((CHUNK, BR_WIDTH), lambda c: (n - 1 - c, off))
    vec = pl.BlockSpec((1, BR_WIDTH), lambda c: (0, 0))
    wvec = pl.BlockSpec((1, HEAD_DIM), lambda c: (0, 0))
    return pl.pallas_call(
        body, name=name, grid=(n,),
        in_specs=[blk(C_HQ // BR_WIDTH), blk(C_HF // BR_WIDTH), blk(C_HI // BR_WIDTH), vec, blk(0), blk(0), blk(0),
                  pl.BlockSpec((1, N_HEADS, HEAD_DIM, HEAD_DIM), lambda c: (n - 1 - c, 0, 0, 0)), blk(0), blk(C_HZ // BR_WIDTH),
                  wvec, blk(1)],
        out_specs=[blk(0), blk(0), blk(0), blk(0), vec, wvec],
        out_shape=[jax.ShapeDtypeStruct((s, BR_WIDTH), BF16)] * 4 + [jax.ShapeDtypeStruct((1, BR_WIDTH), F32),
                                                                    jax.ShapeDtypeStruct((1, HEAD_DIM), F32)],
        scratch_shapes=[pltpu.VMEM((N_HEADS, HEAD_DIM, HEAD_DIM), F32)],
        compiler_params=_cp("arbitrary"),
    )(proj, proj, proj, lb, qh, kh, lf, states, o, proj, norm_w.reshape(1, HEAD_DIM), dy)


_ANY = pl.BlockSpec(memory_space=pl.ANY)
_MESH = pl.DeviceIdType.MESH


def _all_gather(x_local, *, name, after=(), norm=None):
    n_after = len(after)
    n_norm = 0 if norm is None else 2

    def norm_rows(h_ref, w_ref, hn_ref, hbuf, obuf, in_sems, out_sems):
        n_tiles = h_ref.shape[0] // ROW_TILE

        def load(t):
            return pltpu.make_async_copy(h_ref.at[pl.ds(t * ROW_TILE, ROW_TILE)], hbuf.at[t % 2], in_sems.at[t % 2])

        def store(t):
            return pltpu.make_async_copy(obuf.at[t % 2], hn_ref.at[pl.ds(t * ROW_TILE, ROW_TILE)], out_sems.at[t % 2])

        load(0).start()
        for t in range(n_tiles):
            if t + 1 < n_tiles:
                load(t + 1).start()
            load(t).wait()
            if t >= 2:
                store(t - 2).wait()
            xv = hbuf[t % 2]
            r = lax.rsqrt(jnp.mean(xv * xv, axis=-1, keepdims=True) + NORM_EPS)
            obuf[t % 2] = (xv * r * w_ref[...]).astype(BF16)
            store(t).start()
        for t in range(max(n_tiles - 2, 0), n_tiles):
            store(t).wait()

    def body(x_ref, *refs):
        if norm is None:
            out_ref, send_sems, recv_sems, local_sem = refs[n_after:]
        else:
            h_ref, w_ref = refs[n_after:n_after + 2]
            out_ref, hn_ref, send_sems, recv_sems, local_sem, hbuf, obuf, in_sems, out_sems = refs[n_after + 2:]
        x, y, c = lax.axis_index("x"), lax.axis_index("y"), lax.axis_index("c")
        me, sibling = (x, y, c), (x, y, 1 - c)
        n1 = (x ^ (1 - c), y ^ c)
        n2 = (x ^ c, y ^ (1 - c))
        dg = (1 - x, 1 - y)

        def slot(px, py, pc):
            return out_ref.at[4 * px + 2 * py + pc]

        def copy(k, block, to, src=None):
            return pltpu.make_async_remote_copy(
                src_ref=slot(*block) if src is None else src, dst_ref=slot(*block),
                send_sem=send_sems.at[k], recv_sem=recv_sems.at[k], device_id=to, device_id_type=_MESH)

        mine = pltpu.make_async_copy(x_ref, slot(*me), local_sem)
        mine.start()
        first = [copy(0, me, sibling, src=x_ref), copy(1, me, (*n1, c), src=x_ref), copy(2, me, (*n2, c), src=x_ref)]
        for cp in first:
            cp.start()
        if norm is not None:
            norm_rows(h_ref, w_ref, hn_ref, hbuf, obuf, in_sems, out_sems)
        copy(2, (*n2, c), me).wait_recv()
        forward = copy(3, (*n2, c), (*n1, c))
        forward.start()
        passed = [copy(5, (*n2, c), sibling)]
        passed[0].start()
        copy(1, (*n1, c), me).wait_recv()
        passed.append(copy(4, (*n1, c), sibling))
        passed[1].start()
        copy(3, (*dg, c), me).wait_recv()
        passed.append(copy(6, (*dg, c), sibling))
        passed[2].start()
        copy(0, sibling, me).wait_recv()
        copy(4, (*n2, 1 - c), me).wait_recv()
        copy(5, (*n1, 1 - c), me).wait_recv()
        copy(6, (*dg, 1 - c), me).wait_recv()
        for cp in first + [forward] + passed:
            cp.wait_send()
        mine.wait()

    gathered = jax.ShapeDtypeStruct((N_DEV,) + x_local.shape, x_local.dtype)
    sems = [pltpu.SemaphoreType.DMA((7,)), pltpu.SemaphoreType.DMA((7,)), pltpu.SemaphoreType.DMA]
    if norm is None:
        return pl.pallas_call(body, name=name, out_shape=gathered, in_specs=[_ANY] * (1 + n_after), out_specs=_ANY,
                              scratch_shapes=sems)(x_local, *after)
    h, w = norm
    d = h.shape[1]
    return pl.pallas_call(
        body, name=name, out_shape=[gathered, jax.ShapeDtypeStruct(h.shape, BF16)],
        in_specs=[_ANY] * (2 + n_after) + [pl.BlockSpec(memory_space=pltpu.VMEM)], out_specs=[_ANY, _ANY],
        scratch_shapes=sems + [pltpu.VMEM((2, ROW_TILE, d), F32), pltpu.VMEM((2, ROW_TILE, d), BF16),
                               pltpu.SemaphoreType.DMA((2,)), pltpu.SemaphoreType.DMA((2,))],
    )(x_local, *after, h, w.reshape(1, d))


_HBM = pl.BlockSpec(memory_space=pltpu.HBM)
_SEM = pl.BlockSpec(memory_space=pltpu.SEMAPHORE)
_EFFECT = pltpu.SideEffectType.DATAFLOW_SIDE_EFFECTING


def _peers():
    x, y, c = lax.axis_index("x"), lax.axis_index("y"), lax.axis_index("c")
    out = []
    for k in range(1, N_DEV):
        px, py, pc = x ^ ((k >> 2) & 1), y ^ ((k >> 1) & 1), c ^ (k & 1)
        out.append(((px, py, pc), 4 * px + 2 * py + pc))
    return 4 * x + 2 * y + c, out


def _push_copies(src_ref, land_ref, send_sems, recv_sems, broadcast):
    my, peers = _peers()
    pairs = []
    for k, (pos, idx) in enumerate(peers):
        src = src_ref if broadcast else src_ref.at[idx]
        send = pltpu.make_async_remote_copy(src_ref=src, dst_ref=land_ref.at[my], send_sem=send_sems.at[k],
                                            recv_sem=recv_sems.at[k], device_id=pos, device_id_type=_MESH)
        recv = pltpu.make_async_remote_copy(src_ref=src, dst_ref=land_ref.at[idx], send_sem=send_sems.at[k],
                                            recv_sem=recv_sems.at[k], device_id=pos, device_id_type=_MESH)
        pairs.append((send, recv))
    return pairs


def _push_start(src, land, *, broadcast, name, after=()):
    n_after = len(after)

    def body(src_ref, land_ref, *refs):
        send_sems, recv_sems, _, _, token = refs[n_after:]
        for send, _ in _push_copies(src_ref, land_ref, send_sems, recv_sems, broadcast):
            send.start()
        token[...] = jnp.zeros_like(token)

    return pl.pallas_call(
        body, name=name,
        out_shape=(pltpu.SemaphoreType.DMA((N_DEV - 1,)), pltpu.SemaphoreType.DMA((N_DEV - 1,)),
                   pltpu.HBM(src.shape, src.dtype), pltpu.HBM(land.shape, land.dtype), jax.ShapeDtypeStruct((8, 128), F32)),
        in_specs=(_HBM, _HBM) + (_ANY,) * n_after, out_specs=(_SEM, _SEM, _HBM, _HBM, pl.BlockSpec(memory_space=pltpu.VMEM)),
        input_output_aliases={0: 2, 1: 3}, compiler_params=pltpu.CompilerParams(has_side_effects=_EFFECT),
    )(pltpu.with_memory_space_constraint(src, pltpu.HBM), pltpu.with_memory_space_constraint(land, pltpu.HBM), *after)


def _push_wait(handle, after, *, broadcast, name):
    send_sems, recv_sems, src_thru, land_thru, _ = handle

    def body(src_ref, land_ref, send_sems, recv_sems, *rest):
        for send, recv in _push_copies(src_ref, land_ref, send_sems, recv_sems, broadcast):
            send.wait_send()
            recv.wait_recv()

    return pl.pallas_call(
        body, name=name,
        out_shape=(pltpu.HBM(src_thru.shape, src_thru.dtype), pltpu.HBM(land_thru.shape, land_thru.dtype)),
        in_specs=(_HBM, _HBM, _SEM, _SEM) + (_ANY,) * len(after), out_specs=(_HBM, _HBM),
        input_output_aliases={0: 0, 1: 1}, compiler_params=pltpu.CompilerParams(has_side_effects=_EFFECT),
    )(src_thru, land_thru, send_sems, recv_sems, *after)[1]


def _relay_copies(src_ref, land_ref, sems_a, sems_b):
    x, y, c = lax.axis_index("x"), lax.axis_index("y"), lax.axis_index("c")
    slot = lambda px, py, pc: land_ref.at[4 * px + 2 * py + pc]
    chips = [(1 - x, y), (x, 1 - y), (1 - x, 1 - y)]
    (send_a, recv_a), (send_b, recv_b) = sems_a, sems_b

    def copy(sems, k, src, dst_slot, to):
        return pltpu.make_async_remote_copy(src_ref=src, dst_ref=dst_slot, send_sem=sems[0].at[k], recv_sem=sems[1].at[k],
                                            device_id=to, device_id_type=_MESH)

    first = [copy((send_a, recv_a), 0, src_ref, slot(x, y, c), (x, y, 1 - c))]
    first += [copy((send_a, recv_a), 1 + j, src_ref, slot(x, y, c), (*chip, c)) for j, chip in enumerate(chips)]
    first_in = [copy((send_a, recv_a), 0, src_ref, slot(x, y, 1 - c), (x, y, 1 - c))]
    first_in += [copy((send_a, recv_a), 1 + j, src_ref, slot(*chip, c), (*chip, c)) for j, chip in enumerate(chips)]
    relay = [copy((send_b, recv_b), j, slot(*chip, c), slot(*chip, c), (x, y, 1 - c)) for j, chip in enumerate(chips)]
    relay_in = [copy((send_b, recv_b), j, slot(*chip, 1 - c), slot(*chip, 1 - c), (x, y, 1 - c)) for j, chip in enumerate(chips)]
    return first, first_in, relay, relay_in


def _relay_start(src, land, *, name, after=()):
    n_after = len(after)

    def body(src_ref, land_ref, *refs):
        send_a, recv_a, _, _, token = refs[n_after:]
        for cp in _relay_copies(src_ref, land_ref, (send_a, recv_a), (send_a, recv_a))[0]:
            cp.start()
        token[...] = jnp.zeros_like(token)

    send_a, recv_a, src_thru, land_thru, token = pl.pallas_call(
        body, name=name,
        out_shape=(pltpu.SemaphoreType.DMA((4,)), pltpu.SemaphoreType.DMA((4,)), pltpu.HBM(src.shape, src.dtype),
                   pltpu.HBM(land.shape, land.dtype), jax.ShapeDtypeStruct((8, 128), F32)),
        in_specs=(_HBM, _HBM) + (_ANY,) * n_after, out_specs=(_SEM, _SEM, _HBM, _HBM, pl.BlockSpec(memory_space=pltpu.VMEM)),
        input_output_aliases={0: 2, 1: 3}, compiler_params=pltpu.CompilerParams(has_side_effects=_EFFECT),
    )(pltpu.with_memory_space_constraint(src, pltpu.HBM), pltpu.with_memory_space_constraint(land, pltpu.HBM), *after)
    return (send_a, recv_a), src_thru, land_thru, token


def _relay_mid(handle, after, *, name):
    sems_a, src_thru, land_thru, _ = handle
    n_after = len(after)

    def body(src_ref, land_ref, send_a, recv_a, *refs):
        send_b, recv_b, _, _, token = refs[n_after:]
        _, first_in, relay, _ = _relay_copies(src_ref, land_ref, (send_a, recv_a), (send_b, recv_b))
        for j in range(3):
            first_in[1 + j].wait_recv()
            relay[j].start()
        token[...] = jnp.zeros_like(token)

    send_b, recv_b, src2, land2, token = pl.pallas_call(
        body, name=name,
        out_shape=(pltpu.SemaphoreType.DMA((3,)), pltpu.SemaphoreType.DMA((3,)), pltpu.HBM(src_thru.shape, src_thru.dtype),
                   pltpu.HBM(land_thru.shape, land_thru.dtype), jax.ShapeDtypeStruct((8, 128), F32)),
        in_specs=(_HBM, _HBM, _SEM, _SEM) + (_ANY,) * n_after,
        out_specs=(_SEM, _SEM, _HBM, _HBM, pl.BlockSpec(memory_space=pltpu.VMEM)),
        input_output_aliases={0: 2, 1: 3}, compiler_params=pltpu.CompilerParams(has_side_effects=_EFFECT),
    )(src_thru, land_thru, *sems_a, *after)
    return sems_a, (send_b, recv_b), src2, land2, token


def _relay_wait(handle, after, *, name):
    sems_a, sems_b, src_thru, land_thru, _ = handle

    def body(src_ref, land_ref, send_a, recv_a, send_b, recv_b, *rest):
        first, first_in, relay, relay_in = _relay_copies(src_ref, land_ref, (send_a, recv_a), (send_b, recv_b))
        first_in[0].wait_recv()
        for cp in relay_in:
            cp.wait_recv()
        for cp in first + relay:
            cp.wait_send()

    return pl.pallas_call(
        body, name=name,
        out_shape=(pltpu.HBM(src_thru.shape, src_thru.dtype), pltpu.HBM(land_thru.shape, land_thru.dtype)),
        in_specs=(_HBM, _HBM, _SEM, _SEM, _SEM, _SEM) + (_ANY,) * len(after), out_specs=(_HBM, _HBM),
        input_output_aliases={0: 0, 1: 1}, compiler_params=pltpu.CompilerParams(has_side_effects=_EFFECT),
    )(src_thru, land_thru, *sems_a, *sems_b, *after)[1]


def _adamw(parts, row_off, w, m, v, *, layer=0, n_layers=1, prev=None, name, tr):
    rows, c = w.shape
    r = rows // n_layers
    np_ = parts.shape[0]
    tr = min(tr, r)
    assert r % tr == 0 and row_off % tr == 0
    ob, lb = row_off // tr, layer * (r // tr)
    c1 = 1.0 - ADAM_B1 ** ADAM_STEP
    c2 = 1.0 - ADAM_B2 ** ADAM_STEP
    n_prev = 0 if prev is None else 4

    def body(p_ref, w_ref, m_ref, v_ref, *refs):
        g_ref, d_ref, nm_ref, nv_ref = refs[n_prev:]
        g = p_ref[0].astype(F32)
        for s in range(1, np_):
            g = g + p_ref[s].astype(F32)
        wv = w_ref[...]
        m2 = ADAM_B1 * m_ref[...] + (1.0 - ADAM_B1) * g
        v2 = ADAM_B2 * v_ref[...] + (1.0 - ADAM_B2) * jnp.square(g)
        m_hat = m2 / c1
        v_hat = v2 / c2
        g_ref[...] = g
        d_ref[...] = -ADAM_LR * (m_hat / (jnp.sqrt(v_hat) + ADAM_EPS) + ADAM_WD * wv)
        nm_ref[...] = m2
        nv_ref[...] = v2

    blk = pl.BlockSpec((tr, c), lambda i: (lb + i, 0))
    return pl.pallas_call(
        body, name=name, grid=(r // tr,),
        in_specs=[pl.BlockSpec((np_, tr, c), lambda i: (0, ob + i, 0)), blk, blk, blk] + [_ANY] * n_prev,
        out_specs=[blk] * 4, out_shape=[jax.ShapeDtypeStruct((rows, c), F32)] * 4,
        input_output_aliases={4 + i: i for i in range(n_prev)}, compiler_params=_cp("parallel"),
    )(parts, w, m, v, *(prev or ()))


def _sum_parts(parts, *, name, after=()):
    np_, r, c = parts.shape

    def body(p_ref, *refs):
        o_ref = refs[-1]
        g = p_ref[0]
        for s in range(1, np_):
            g = g + p_ref[s]
        o_ref[...] = g

    vmem = pl.BlockSpec(memory_space=pltpu.VMEM)
    return pl.pallas_call(body, name=name, in_specs=[vmem] + [_ANY] * len(after), out_specs=vmem,
                          out_shape=jax.ShapeDtypeStruct((r, c), F32))(parts, *after)


def _pack(arrs):
    rows = []
    for a in arrs:
        f = a.reshape(-1).astype(F32)
        pad = (-f.shape[0]) % 128
        rows.append(jnp.pad(f, (0, pad)).reshape(-1, 128))
    out = jnp.concatenate(rows, axis=0)
    return jnp.pad(out, ((0, (-out.shape[0]) % 8), (0, 0)))


def _unpack(packed, shapes):
    outs, r0 = [], 0
    for shp in shapes:
        n = 1
        for d in shp:
            n *= d
        nr = -(-n // 128)
        outs.append(packed[r0:r0 + nr].reshape(-1)[:n].reshape(shp))
        r0 += nr
    return outs


_WIN_PIECES = ((0, 4096, 0), (4112, 8208, 0), (4096, 4104, HEAD_DIM - N_HEADS), (4104, 4112, HEAD_DIM - N_HEADS))


RELAYOUT_TILE = 256
LAST_SPLIT = 4
OTHER_SPLIT = 2


def _win_from_shards(shards, *, name):
    k = shards.shape[1]
    tr = min(RELAYOUT_TILE, k)

    def body(x_ref, o_ref):
        cols = []
        for lo, hi, pad in _WIN_PIECES:
            for j in range(N_DEV):
                a, b = max(lo, j * SHARD_IN), min(hi, (j + 1) * SHARD_IN)
                if a < b:
                    cols.append(x_ref[j, :, a - j * SHARD_IN:b - j * SHARD_IN])
            if pad:
                cols.append(jnp.zeros((tr, pad), x_ref.dtype))
        o_ref[...] = jnp.concatenate(cols, axis=1)

    return pl.pallas_call(
        body, name=name, grid=(k // tr,), in_specs=[pl.BlockSpec((N_DEV, tr, SHARD_IN), lambda i: (0, i, 0))],
        out_specs=pl.BlockSpec((tr, N_PROJ), lambda i: (i, 0)), out_shape=jax.ShapeDtypeStruct((k, N_PROJ), shards.dtype),
        compiler_params=_cp("parallel"),
    )(shards)


def _win_to_shards(g, *, name):
    k = g.shape[0]
    tr = min(RELAYOUT_TILE, k)
    starts, off = [], 0
    for lo, hi, pad in _WIN_PIECES:
        starts.append((lo, hi, off))
        off += hi - lo + pad

    def body(g_ref, o_ref):
        for j in range(N_DEV):
            cols = []
            for lo, hi, off in sorted(starts):
                a, b = max(lo, j * SHARD_IN), min(hi, (j + 1) * SHARD_IN)
                if a < b:
                    cols.append(g_ref[:, off + a - lo:off + b - lo])
            o_ref[j] = jnp.concatenate(cols, axis=1)

    return pl.pallas_call(
        body, name=name, grid=(k // tr,), in_specs=[pl.BlockSpec((tr, N_PROJ), lambda i: (i, 0))],
        out_specs=pl.BlockSpec((N_DEV, tr, SHARD_IN), lambda i: (0, i, 0)),
        out_shape=jax.ShapeDtypeStruct((N_DEV, k, SHARD_IN), g.dtype), compiler_params=_cp("parallel"),
    )(g)


def _lower_bounds(logits):
    probs = jax.nn.softmax(logits.astype(F32), axis=0)
    return jnp.cumsum(probs, axis=0) - probs[0]


def _pad_lanes(vec8):
    return jnp.pad(vec8.reshape(1, N_HEADS), ((0, 0), (0, HEAD_DIM - N_HEADS)))


def kernel(x, p, norm_w, w_in, dn_conv_w, dn_A_log, dn_dt_bias, dn_norm_w, hg_lb_logits, hg_norm_w, w_out, w_ple_up, w_ple_gate, final_norm_w, loss_target, m_norm_w, m_w_in, m_dn_conv_w, m_dn_A_log, m_dn_dt_bias, m_dn_norm_w, m_hg_lb_logits, m_hg_norm_w, m_w_out, m_w_ple_up, m_w_ple_gate, m_final_norm_w, v_norm_w, v_w_in, v_dn_conv_w, v_dn_A_log, v_dn_dt_bias, v_dn_norm_w, v_hg_lb_logits, v_hg_norm_w, v_w_out, v_w_ple_up, v_w_ple_gate, v_final_norm_w):
    depth = norm_w.shape[0]
    my = 4 * lax.axis_index("x") + 2 * lax.axis_index("y") + lax.axis_index("c")
    h = x[0]
    tgt = loss_target[0]
    rows_out = D_MODEL // N_DEV
    up_rows = PLE_DIM * (D_MODEL // N_DEV) // D_MODEL
    g_off, u_off = rows_out, 2 * rows_out

    def own_slot(block):
        return lax.dynamic_update_index_in_dim(lax.empty((N_DEV,) + block.shape, block.dtype), block, my, 0)

    win_bf = w_in.astype(BF16)
    rest_bf = [jnp.concatenate([w_out[l], w_ple_gate[l], w_ple_up[l].reshape(up_rows, D_MODEL)], axis=0).astype(BF16)
               for l in range(depth)]
    conv_push = _push_start(dn_conv_w, own_slot(dn_conv_w), broadcast=True, name="gather_conv_w_start")
    win_l0, hn_l0 = _all_gather(win_bf[0], name="gather_w_in_l0", after=[conv_push[4]], norm=(h, norm_w[0]))
    win_all = {0: win_l0}
    pending, relayed = {}, {}
    last = win_all[0]
    for l in range(depth):
        if l > 0:
            relayed["win", l] = _relay_start(win_bf[l], own_slot(win_bf[l]), after=[last], name=f"gather_w_in_l{l}_first")
            last = relayed["win", l][3]
        if l == 0:
            relayed["rest", l] = _relay_start(rest_bf[l], own_slot(rest_bf[l]), after=[last], name=f"gather_rest_l{l}_first")
            last = relayed["rest", l][3]
        else:
            pending["rest", l] = _push_start(rest_bf[l], own_slot(rest_bf[l]), broadcast=True, after=[last],
                                             name=f"gather_rest_l{l}_start")
            last = pending["rest", l][4]
    lbs = _lower_bounds(hg_lb_logits)

    saved = []
    weights = []
    for l in range(depth):
        tag = f"l{l}"
        if l > 0:
            win_all[l] = _relay_wait(relayed["win", l], [h], name=f"gather_w_in_{tag}_wait")
        wi = _win_from_shards(win_all[l], name=f"w_in_layout_{tag}")
        hn = hn_l0 if l == 0 else _rms_fwd(h, norm_w[l], name=f"rms_fwd_{tag}")
        proj = _mm(hn, wi, mode="nn", out_dtype=F32, after=[last] if l == 0 else (), name=f"mm_proj_{tag}")
        al, dt = _pad_lanes(dn_A_log[l]), _pad_lanes(dn_dt_bias[l])
        if l == 0:
            conv_all = _push_wait(conv_push, [proj], broadcast=True, name="gather_conv_w_wait")
            conv_full = conv_all.transpose(1, 2, 0, 3).reshape(depth, CONV_W, 3 * BR_WIDTH)
        qkv = _dn_qkv_fwd(proj, conv_full[l], name=f"dn_qkv_fwd_{tag}")
        if ("rest", l) in relayed:
            relayed["rest", l] = _relay_mid(relayed["rest", l], [qkv], name=f"gather_rest_{tag}_relay")
            al = al + relayed["rest", l][4][0, 0]
        beta, gcs = _dn_gate_fwd(proj, al, dt, name=f"dn_gate_fwd_{tag}")
        o_dn, st_dn, tinv_dn, y_dn = _dn_chunk_fwd(qkv, gcs, beta, proj, dn_norm_w[l], name=f"dn_chunk_fwd_{tag}")
        lb = lbs[l].reshape(1, BR_WIDTH)
        o_hg, st_hg, qh, kh, lf, y_hg = _hg_chunk_fwd(proj, lb, hg_norm_w[l], name=f"hg_chunk_fwd_{tag}")
        y = jnp.concatenate([y_dn, y_hg], axis=1)
        if ("rest", l) in relayed:
            rest_all = _relay_wait(relayed["rest", l], [y], name=f"gather_rest_{tag}_wait")
        else:
            rest_all = _push_wait(pending["rest", l], [y], broadcast=True, name=f"gather_rest_{tag}_wait")
        w_out_rows, w_gate_rows = (0, rows_out), (g_off, rows_out)
        wu = rest_all[:, u_off:u_off + up_rows].reshape(N_DEV, PLE_DIM, D_MODEL // N_DEV).transpose(1, 0, 2).reshape(PLE_DIM, D_MODEL)
        weights.append((wi, rest_all, wu))
        h1 = _mm(y, rest_all, mode="nn", b_rows=w_out_rows, out_dtype=F32, res=h, name=f"mm_out_{tag}")
        pin = []
        if ("win", l + 1) in relayed:
            relayed["win", l + 1] = _relay_mid(relayed["win", l + 1], [h1], name=f"gather_w_in_l{l + 1}_relay")
            pin = [relayed["win", l + 1][4]]
        up = _mm(p[l, 0], wu, mode="nn", out_dtype=F32, name=f"mm_up_{tag}")
        gp, h2 = _mm(h1, rest_all, mode="nn", b_rows=w_gate_rows, out_dtype=F32, after=pin, tile_n=512, fused=("ple", h1, up),
                     name=f"mm_gate_{tag}")
        saved.append(dict(h=h, hn=hn, proj=proj, qkv=qkv, beta=beta, gcs=gcs, st_dn=st_dn, tinv_dn=tinv_dn, qh=qh, kh=kh, lf=lf,
                          st_hg=st_hg, o_dn=o_dn, o_hg=o_hg, y=y, h1=h1, gp=gp, up=up, al=al, dt=dt, lb=lb))
        h = h2

    loss_row, dh, d_final_w = _final_fwd_bwd(h, final_norm_w, tgt, name="final_norm_loss")

    d_norm_w, d_alog, d_dt, d_dn_nw, d_hg_nw, d_lb, d_conv = ([None] * depth for _ in range(7))
    sent = {}
    for l in reversed(range(depth)):
        wi, rest_all, wu = weights[l]
        sv = saved[l]
        tag = f"l{l}"
        dup, dgp = _ple_bwd(dh, sv["gp"], sv["up"], name=f"ple_bwd_{tag}")
        d_wu = _mm(p[l, 0], dup, mode="tn", out_dtype=BF16, name=f"mm_dwup_{tag}")
        d_wg = _mm(sv["h1"], dgp, mode="tn", out_dtype=BF16, name=f"mm_dwgate_{tag}")
        dh1 = _mm(dgp, rest_all, mode="nt", b_rows=(g_off, rows_out), out_dtype=F32, res=dh, name=f"mm_dh1_{tag}")
        d_wo = _mm(sv["y"], dh1, mode="tn", out_dtype=BF16, name=f"mm_dwout_{tag}")
        parts_rest = jnp.concatenate(
            [d_wo.reshape(N_DEV, rows_out, D_MODEL), d_wg.reshape(N_DEV, rows_out, D_MODEL),
             d_wu.reshape(PLE_DIM, N_DEV, D_MODEL // N_DEV).transpose(1, 0, 2).reshape(N_DEV, up_rows, D_MODEL)], axis=1)
        sent["rest", l] = _push_start(parts_rest, own_slot(parts_rest[my]), broadcast=False, name=f"exchange_rest_{tag}_start")
        dy = _mm(dh1, rest_all, mode="nt", b_rows=(0, rows_out), out_dtype=F32, name=f"mm_dy_{tag}")
        dn_nw = dn_norm_w[l] + sent["rest", l][4][0, 0]
        dqkv, d_gc, dbeta, dz_dn, d_dn_nw[l] = _dn_chunk_bwd(sv["qkv"], sv["gcs"], sv["beta"], sv["st_dn"], sv["tinv_dn"],
                                                             sv["o_dn"], sv["proj"], dn_nw, dy, name=f"dn_chunk_bwd_{tag}")
        dqkv_pre, d_conv[l] = _dn_qkv_bwd(sv["proj"], conv_full[l], dqkv, name=f"dn_qkv_bwd_{tag}")
        db, da, d_alog[l], d_dt[l] = _dn_gate_bwd(sv["proj"], sv["al"], sv["dt"], dbeta, d_gc, name=f"dn_gate_bwd_{tag}")
        dhq, dhf, dhi, dz_hg, d_lb[l], d_hg_nw[l] = _hg_chunk_bwd(sv["proj"], sv["lb"], sv["qh"], sv["kh"], sv["lf"], sv["st_hg"],
                                                                  sv["o_hg"], hg_norm_w[l], dy, name=f"hg_chunk_bwd_{tag}")
        dproj = jnp.concatenate([dqkv_pre, dz_dn, dhq, dhf, dhi, dz_hg, db, da], axis=1)
        def push_d_win(after):
            n_split = LAST_SPLIT if l == 0 else OTHER_SPLIT
            rows = D_MODEL // n_split
            handles = []
            for q in range(n_split):
                hn_q = sv["hn"] if n_split == 1 else sv["hn"][:, q * rows:(q + 1) * rows]
                sfx = tag if n_split == 1 else f"{tag}_{q}"
                d_win = _mm(hn_q, dproj, mode="tn", out_dtype=BF16, after=after, name=f"mm_dwin_{sfx}")
                parts_in = _win_to_shards(d_win, name=f"dw_in_shards_{sfx}")
                handles.append(_push_start(parts_in, own_slot(parts_in[my]), broadcast=False, after=after,
                                           name=f"exchange_w_in_{sfx}_start"))
                after = [handles[-1][4]]
            return handles

        if l == 0:
            small = _pack([loss_row, jnp.concatenate(d_norm_w[1:], axis=0), d_final_w,
                           jnp.stack([a[0, :N_HEADS] for a in d_alog]), jnp.stack([a[0, :N_HEADS] for a in d_dt]),
                           jnp.concatenate(d_dn_nw, axis=0), jnp.concatenate(d_hg_nw, axis=0), jnp.concatenate(d_lb, axis=0),
                           jnp.stack(d_conv)])
            small_all = _all_gather(small, name="gather_small")
        sent["win", l] = push_d_win([small_all] if l == 0 else [])
        dh, d_norm_w[l] = _mm(dproj, wi, mode="nt", out_dtype=F32, tile_m=512, tile_n=D_MODEL, tile_k=768,
                              after=[sent["win", l][-1][4]],
                              fused=("rms_bwd", sv["h"], norm_w[l], dh1), name=f"mm_dhn_{tag}")
    grad_x = dh[None]

    small_shapes = [(1, 128), (depth - 1, D_MODEL), final_norm_w.shape, dn_A_log.shape, dn_dt_bias.shape, dn_norm_w.shape,
                    hg_norm_w.shape, hg_lb_logits.shape, (depth, CONV_W, 3 * BR_WIDTH)]
    tot = _unpack(_sum_parts(small_all, after=[grad_x], name="sum_small"), small_shapes)
    loss = tot[0][0, 0]
    g_lb = tot[7]
    g_logits = jax.vjp(_lower_bounds, hg_lb_logits)[1](g_lb)[0]
    g_conv = lax.dynamic_slice_in_dim(tot[8], my * (3 * BR_WIDTH // N_DEV), 3 * BR_WIDTH // N_DEV, axis=2)
    small_g = [g_conv, tot[3], tot[4], tot[5], g_logits, tot[6], tot[2]]
    small_w = [dn_conv_w, dn_A_log, dn_dt_bias, dn_norm_w, hg_lb_logits, hg_norm_w, final_norm_w]
    small_m = [m_dn_conv_w, m_dn_A_log, m_dn_dt_bias, m_dn_norm_w, m_hg_lb_logits, m_hg_norm_w, m_final_norm_w]
    small_v = [v_dn_conv_w, v_dn_A_log, v_dn_dt_bias, v_dn_norm_w, v_hg_lb_logits, v_hg_norm_w, v_final_norm_w]
    pk_w = _pack(small_w)
    res_small = _adamw(_pack(small_g)[None], 0, pk_w, _pack(small_m), _pack(small_v), name="adamw_small", tr=pk_w.shape[0])
    shapes_w = [a.shape for a in small_w]
    sg, sd, sm, sv_ = (_unpack(r, shapes_w) for r in res_small)

    r_win = r_wo = r_wg = r_wu = None
    done = [grad_x, res_small[0]]

    def flat(a, cols):
        return a.reshape(-1, cols)

    for l in reversed(range(depth)):
        tag = f"l{l}"
        land_rest = _push_wait(sent["rest", l], done, broadcast=False, name=f"exchange_rest_{tag}_wait")
        r_wo = _adamw(land_rest, 0, flat(w_out, D_MODEL), flat(m_w_out, D_MODEL), flat(v_w_out, D_MODEL), layer=l,
                      n_layers=depth, prev=r_wo, name=f"adamw_w_out_{tag}", tr=rows_out)
        r_wg = _adamw(land_rest, g_off, flat(w_ple_gate, D_MODEL), flat(m_w_ple_gate, D_MODEL), flat(v_w_ple_gate, D_MODEL),
                      layer=l, n_layers=depth, prev=r_wg, name=f"adamw_w_gate_{tag}", tr=rows_out)
        r_wu = _adamw(land_rest, u_off, flat(w_ple_up, D_MODEL), flat(m_w_ple_up, D_MODEL), flat(v_w_ple_up, D_MODEL),
                      layer=l, n_layers=depth, prev=r_wu, name=f"adamw_w_up_{tag}", tr=up_rows)
        done = [r_wo[0], r_wg[0], r_wu[0]]
    for l in reversed(range(depth)):
        tag = f"l{l}"
        if l == 0:
            nw0 = _sum_parts(_all_gather(_pack([d_norm_w[0]]), after=done, name="gather_norm_w"), name="sum_norm_w")
            g_norm_w = jnp.concatenate([_unpack(nw0, [(1, D_MODEL)])[0], tot[1]], axis=0)
            pk_nw = _pack([norm_w])
            r_nw = _adamw(_pack([g_norm_w])[None], 0, pk_nw, _pack([m_norm_w]), _pack([v_norm_w]), name="adamw_norm_w",
                          tr=pk_nw.shape[0])
            r_nw = [_unpack(r, [norm_w.shape])[0] for r in r_nw]
            done = [r_nw[0]]
        n_split = len(sent["win", l])
        for q, handle in enumerate(sent["win", l]):
            sfx = tag if n_split == 1 else f"{tag}_{q}"
            land_in = _push_wait(handle, done, broadcast=False, name=f"exchange_w_in_{sfx}_wait")
            r_win = _adamw(land_in, 0, flat(w_in, SHARD_IN), flat(m_w_in, SHARD_IN), flat(v_w_in, SHARD_IN),
                           layer=l * n_split + q, n_layers=depth * n_split, prev=r_win, name=f"adamw_w_in_{sfx}", tr=256)
            done = [r_win[0]]
    r_win = [o.reshape(w_in.shape) for o in r_win]
    r_wo = [o.reshape(w_out.shape) for o in r_wo]
    r_wg = [o.reshape(w_ple_gate.shape) for o in r_wg]
    r_wu = [o.reshape(w_ple_up.shape) for o in r_wu]

    def order(nw, small_list, big_in, big_out, big_up, big_gate):
        cw, al_, dt_, dnw, lbl, hnw, fw = small_list
        return [nw, big_in, cw, al_, dt_, dnw, lbl, hnw, big_out, big_up, big_gate, fw]

    outs = [loss, grad_x]
    for i, sl in enumerate((sg, sd, sm, sv_)):
        outs += order(r_nw[i], sl, r_win[i], r_wo[i], r_wu[i], r_wg[i])
    return tuple(outs)
```

```python
import functools

import jax
import jax.numpy as jnp
from jax import lax
from jax.experimental import pallas as pl
from jax.experimental.pallas import tpu as pltpu

F32 = jnp.float32
BF16 = jnp.bfloat16
HIGHEST = lax.Precision.HIGHEST

N_DEV = 8
D_MODEL = 2048
PLE_DIM = 256
HEAD_DIM = 128
N_HEADS = 8
BR_WIDTH = N_HEADS * HEAD_DIM
CHUNK = 64
SUB = 16
CONV_W = 4
NORM_EPS = 1e-6
L2_EPS = 1e-6
IN_WIDTH = 8208
SHARD_IN = IN_WIDTH // N_DEV
EXP_CLAMP = 80.0

C_QKV, C_Z, C_HQ, C_HF, C_HI, C_HZ, C_B, C_A, N_PROJ = 0, 3072, 4096, 5120, 6144, 7168, 8192, 8320, 8448

ADAM_LR, ADAM_B1, ADAM_B2, ADAM_EPS, ADAM_WD, ADAM_STEP = 0.001, 0.9, 0.999, 1e-08, 0.01, 10

VMEM_LIMIT = 48 * 1024 * 1024


def _cp(*sem):
    return pltpu.CompilerParams(dimension_semantics=sem, vmem_limit_bytes=VMEM_LIMIT)


class _Heads:
    def __init__(self, vals):
        self.v = tuple(vals)

    def __add__(self, o):
        return _hmap(lambda a, b: a + b, self, o)

    def __radd__(self, o):
        return _hmap(lambda a, b: b + a, self, o)

    def __sub__(self, o):
        return _hmap(lambda a, b: a - b, self, o)

    def __rsub__(self, o):
        return _hmap(lambda a, b: b - a, self, o)

    def __mul__(self, o):
        return _hmap(lambda a, b: a * b, self, o)

    def __rmul__(self, o):
        return _hmap(lambda a, b: b * a, self, o)

    def __neg__(self):
        return _hmap(lambda a: -a, self)

    def __getitem__(self, idx):
        return _hmap(lambda a: a[idx], self)


def _hmap(fn, *args):
    n = next((len(a.v) for a in args if isinstance(a, _Heads)), None)
    if n is None:
        return fn(*args)
    return _Heads(fn(*[a.v[i] if isinstance(a, _Heads) else a for a in args]) for i in range(n))


def _dot(a, b, ca, cb):
    return _hmap(lambda x, y: lax.dot_general(x.astype(BF16), y.astype(BF16), (((ca,), (cb,)), ((), ())),
                                              preferred_element_type=F32), a, b)


def _nn(a, b):
    return _dot(a, b, 1, 0)


def _nt(a, b):
    return _dot(a, b, 1, 1)


def _tn(a, b):
    return _dot(a, b, 0, 0)


def _split(a):
    hi = _hmap(lambda x: x.astype(BF16), a)
    return hi, _hmap(lambda x, h: (x - h.astype(F32)).astype(BF16), a, hi)


def _dot3(a, b, ca, cb):
    ah, al = _split(a)
    bh, bl = _split(b)
    return _dot(ah, bh, ca, cb) + (_dot(ah, bl, ca, cb) + _dot(al, bh, ca, cb))


def _nn_exact(a, b):
    return _hmap(lambda y: lax.dot_general(a, y, (((1,), (0,)), ((), ())), precision=HIGHEST,
                                           preferred_element_type=F32), b)


def _exp(x):
    return _hmap(jnp.exp, x)


def _sum(x, axis):
    return _hmap(lambda a: jnp.sum(a, axis=axis, keepdims=True), x)


def _sigmoid(x):
    return jax.nn.sigmoid(x)


def _silu(x):
    return x * _sigmoid(x)


def _dsilu(x):
    s = _sigmoid(x)
    return s * (1.0 + x * (1.0 - s))


def _silu_and_grad(x):
    s = _sigmoid(x)
    return x * s, s * (1.0 + x * (1.0 - s))


def _softplus(x):
    return jnp.maximum(x, 0.0) + jnp.log(1.0 + jnp.exp(-jnp.abs(x)))


def _iota2(n, m, axis):
    return lax.broadcasted_iota(jnp.int32, (n, m), axis)


def _col2row(col, eye):
    return _hmap(lambda c: jnp.sum(eye * c, axis=0, keepdims=True), col)


def _row2col(row, eye):
    return _hmap(lambda r: jnp.sum(eye * r, axis=1, keepdims=True), row)


def _pick_lane(block, lane_idx):
    lane = _iota2(block.shape[0], block.shape[1], 1)
    return jnp.sum(jnp.where(lane == lane_idx, block, 0.0), axis=1, keepdims=True)


MM_TILE_M, MM_TILE_N, MM_TILE_K = 1024, 1408, 2048


def _tile(dim, cap):
    if dim <= cap:
        return dim
    t = cap - cap % 128
    while dim % t:
        t -= 128
    return t


def _mm(a, b, *, mode, out_dtype, res=None, after=(), b_rows=None, tile_m=MM_TILE_M, tile_n=MM_TILE_N, tile_k=MM_TILE_K,
        fused=None, name):
    b_mat_rows = b.shape[0] if b_rows is None else N_DEV * b_rows[1]
    if mode == "nn":
        (m, kd), n = a.shape, b.shape[-1]
        assert kd == b_mat_rows
    elif mode == "nt":
        (m, kd), n = a.shape, b_mat_rows
    else:
        (kd, m), n = a.shape, b.shape[-1]
    tm, tn, tk = _tile(m, tile_m), _tile(n, tile_n), _tile(kd, tile_k)
    assert m % tm == 0 and n % tn == 0 and kd % tk == 0, (m, n, kd, tm, tn, tk)
    nk = kd // tk
    ca, cb = {"nn": (1, 0), "nt": (1, 1), "tn": (0, 0)}[mode]

    kind = None if fused is None else fused[0]
    n_in = 2 + (res is not None) + (0 if fused is None else len(fused) - 1)
    n_out = 1 if fused is None else 2

    def body(*refs):
        a_ref, b_ref = refs[:2]
        r_ref = None if res is None else refs[2]
        extra = refs[2 + (res is not None):n_in]
        outs = refs[-1 - n_out:-1]
        o_ref, acc_ref = outs[0], refs[-1]
        k = pl.program_id(2)

        @pl.when(k == 0)
        def _():
            acc_ref[...] = jnp.zeros_like(acc_ref)

        b_tile = b_ref[...]
        if b_rows is not None:
            b_tile = b_tile.reshape(-1, b_tile.shape[-1])
        acc_ref[...] += _dot(a_ref[...], b_tile, ca, cb)

        @pl.when(k == nk - 1)
        def _():
            out = acc_ref[...]
            if r_ref is not None:
                out = out + r_ref[...].astype(F32)
            if kind == "ple":
                h1_ref, up_ref = extra
                o_ref[...] = out
                outs[1][...] = h1_ref[...] + up_ref[...] * _sigmoid(out)
            elif kind == "rms_bwd":
                h_ref, w_ref, res_ref = extra
                dx, dwt = _rms_bwd_math(h_ref[...], w_ref[...], out)
                o_ref[...] = res_ref[...] + dx

                @pl.when(pl.program_id(0) == 0)
                def _():
                    outs[1][...] = jnp.zeros_like(outs[1])

                outs[1][...] += jnp.sum(dwt, axis=0, keepdims=True)
            else:
                o_ref[...] = out.astype(o_ref.dtype)

    a_spec = pl.BlockSpec((tk, tm), lambda i, j, k: (k, i)) if mode == "tn" else pl.BlockSpec((tm, tk), lambda i, j, k: (i, k))
    if b_rows is None:
        b_spec = pl.BlockSpec((tn, tk), lambda i, j, k: (j, k)) if mode == "nt" else pl.BlockSpec((tk, tn), lambda i, j, k: (k, j))
    else:
        first, count = b_rows
        assert first % count == 0 and mode in ("nn", "nt")
        rb = first // count
        if mode == "nn":
            assert tk == kd
            b_spec = pl.BlockSpec((N_DEV, count, tn), lambda i, j, k: (0, rb, j))
        else:
            assert tn % count == 0
            b_spec = pl.BlockSpec((tn // count, count, tk), lambda i, j, k: (j, rb, k))
    o_spec = pl.BlockSpec((tm, tn), lambda i, j, k: (i, j))
    row_spec = pl.BlockSpec((1, tn), lambda i, j, k: (0, j))
    extra_specs, extra_args, out_specs, out_shape = [], (), o_spec, jax.ShapeDtypeStruct((m, n), out_dtype)
    sem = ("parallel", "parallel", "arbitrary")
    if kind == "ple":
        extra_specs, extra_args = [o_spec, o_spec], tuple(fused[1:])
        out_specs, out_shape = [o_spec, o_spec], [jax.ShapeDtypeStruct((m, n), F32)] * 2
    elif kind == "rms_bwd":
        assert tn == n
        extra_specs, extra_args = [o_spec, row_spec, o_spec], (fused[1], fused[2].reshape(1, n), fused[3])
        out_specs, out_shape = [o_spec, row_spec], [jax.ShapeDtypeStruct((m, n), F32), jax.ShapeDtypeStruct((1, n), F32)]
        sem = ("arbitrary", "arbitrary", "arbitrary")
    in_specs = ([a_spec, b_spec] + ([o_spec] if res is not None else []) + extra_specs
                + [pl.BlockSpec(memory_space=pl.ANY)] * len(after))
    args = (a, b) + ((res,) if res is not None else ()) + extra_args + tuple(after)
    return pl.pallas_call(
        body, name=name, grid=(m // tm, n // tn, nk), in_specs=in_specs, out_specs=out_specs, out_shape=out_shape,
        scratch_shapes=[pltpu.VMEM((tm, tn), F32)], compiler_params=_cp(*sem),
    )(*args)


ROW_TILE = 256


def _rms_fwd(h, w, *, name):
    s, d = h.shape
    tr = min(ROW_TILE, s)

    def body(h_ref, w_ref, o_ref):
        x = h_ref[...]
        r = lax.rsqrt(jnp.mean(x * x, axis=-1, keepdims=True) + NORM_EPS)
        o_ref[...] = (x * r * w_ref[...]).astype(o_ref.dtype)

    return pl.pallas_call(
        body, name=name, grid=(s // tr,),
        in_specs=[pl.BlockSpec((tr, d), lambda i: (i, 0)), pl.BlockSpec((1, d), lambda i: (0, 0))],
        out_specs=pl.BlockSpec((tr, d), lambda i: (i, 0)),
        out_shape=jax.ShapeDtypeStruct((s, d), BF16), compiler_params=_cp("parallel"),
    )(h, w.reshape(1, d))


def _rms_bwd_math(x, w, dy):
    d = x.shape[-1]
    r = lax.rsqrt(jnp.mean(x * x, axis=-1, keepdims=True) + NORM_EPS)
    gw = dy * w
    dx = r * gw - x * ((r * r * r) * (jnp.sum(gw * x, axis=-1, keepdims=True) / d))
    return dx, dy * x * r


def _final_fwd_bwd(h, w, tgt, *, name):
    s, d = h.shape
    tr = min(ROW_TILE, s)

    def body(h_ref, w_ref, t_ref, loss_ref, dh_ref, dw_ref):
        @pl.when(pl.program_id(0) == 0)
        def _():
            loss_ref[...] = jnp.zeros_like(loss_ref)
            dw_ref[...] = jnp.zeros_like(dw_ref)

        x = h_ref[...]
        wv = w_ref[...]
        r = lax.rsqrt(jnp.mean(x * x, axis=-1, keepdims=True) + NORM_EPS)
        err = x * r * wv - t_ref[...]
        row_loss = jnp.mean(err * err, axis=-1, keepdims=True)
        loss_ref[...] += 0.5 * jnp.sum(row_loss, axis=0, keepdims=True)
        dx, dwt = _rms_bwd_math(x, wv, err / d)
        dh_ref[...] = dx
        dw_ref[...] += jnp.sum(dwt, axis=0, keepdims=True)

    row = pl.BlockSpec((tr, d), lambda i: (i, 0))
    vec = pl.BlockSpec((1, d), lambda i: (0, 0))
    return pl.pallas_call(
        body, name=name, grid=(s // tr,), in_specs=[row, vec, row],
        out_specs=[pl.BlockSpec((1, 128), lambda i: (0, 0)), row, vec],
        out_shape=[jax.ShapeDtypeStruct((1, 128), F32), jax.ShapeDtypeStruct((s, d), F32),
                   jax.ShapeDtypeStruct((1, d), F32)],
        compiler_params=_cp("arbitrary"),
    )(h, w.reshape(1, d), tgt)


def _ple_bwd(dh2, gate_pre, up, *, name):
    s, d = dh2.shape
    tr = min(ROW_TILE, s)

    def body(d_ref, g_ref, u_ref, dup_ref, dgp_ref):
        dh = d_ref[...]
        gate = _sigmoid(g_ref[...])
        dup_ref[...] = (dh * gate).astype(BF16)
        dgp_ref[...] = (dh * u_ref[...] * gate * (1.0 - gate)).astype(BF16)

    row = pl.BlockSpec((tr, d), lambda i: (i, 0))
    return pl.pallas_call(body, name=name, grid=(s // tr,), in_specs=[row, row, row], out_specs=[row, row],
                          out_shape=[jax.ShapeDtypeStruct((s, d), BF16)] * 2, compiler_params=_cp("parallel"))(dh2, gate_pre, up)


def _head_norm_fwd(o, z, w):
    return _hmap(lambda x, zz: (x * lax.rsqrt(jnp.mean(x * x, axis=-1, keepdims=True) + NORM_EPS) * w * _silu(zz)).astype(BF16),
                 o, z)


def _head_norm_bwd(o, z, w, dy):
    dos, dzs, dw = [], [], jnp.zeros((1, HEAD_DIM), F32)
    for x, zz, g in zip(o.v, z.v, dy.v):
        r = lax.rsqrt(jnp.mean(x * x, axis=-1, keepdims=True) + NORM_EPS)
        silu_z, dsilu_z = _silu_and_grad(zz)
        don = g * silu_z
        dzs.append((g * (x * r * w) * dsilu_z).astype(BF16))
        gw = don * w
        dos.append(r * gw - x * ((r * r * r) * (jnp.sum(gw * x, axis=-1, keepdims=True) / HEAD_DIM)))
        dw = dw + jnp.sum(don * x * r, axis=0, keepdims=True)
    return _Heads(dos), _Heads(dzs), dw


def _conv_silu(x, w, s):
    row = _iota2(s, x.shape[1], 0)
    c = w[CONV_W - 1:CONV_W, :] * x
    for k in range(1, CONV_W):
        c = c + w[CONV_W - 1 - k:CONV_W - k, :] * jnp.where(row >= k, pltpu.roll(x, k, 0), 0.0)
    return c


def _dn_qkv_fwd(proj, conv_w, *, name):
    s = proj.shape[0]
    nb = 3 * N_HEADS

    def body(x_ref, w_ref, o_ref):
        j = pl.program_id(0)
        sv = _silu(_conv_silu(x_ref[...], w_ref[...], s))
        r = lax.rsqrt(jnp.sum(sv * sv, axis=-1, keepdims=True) + L2_EPS)
        scale = jnp.where(j < N_HEADS, HEAD_DIM ** -0.5, 1.0).astype(F32)
        o_ref[...] = jnp.where(j < 2 * N_HEADS, sv * r * scale, sv)

    return pl.pallas_call(
        body, name=name, grid=(nb,),
        in_specs=[pl.BlockSpec((s, HEAD_DIM), lambda j: (0, j)), pl.BlockSpec((CONV_W, HEAD_DIM), lambda j: (0, j))],
        out_specs=pl.BlockSpec((s, HEAD_DIM), lambda j: (0, j)),
        out_shape=jax.ShapeDtypeStruct((s, 3 * BR_WIDTH), F32), compiler_params=_cp("parallel"),
    )(proj, conv_w)


def _dn_qkv_bwd(proj, conv_w, dqkv, *, name):
    s = proj.shape[0]
    nb = 3 * N_HEADS

    def body(x_ref, w_ref, g_ref, dx_ref, dw_ref):
        j = pl.program_id(0)
        x, w, g = x_ref[...], w_ref[...], g_ref[...]
        c = _conv_silu(x, w, s)
        sv, dsv = _silu_and_grad(c)
        r = lax.rsqrt(jnp.sum(sv * sv, axis=-1, keepdims=True) + L2_EPS)
        scale = jnp.where(j < N_HEADS, HEAD_DIM ** -0.5, 1.0).astype(F32)
        ds_n = scale * (r * g - sv * ((r * r * r) * jnp.sum(g * sv, axis=-1, keepdims=True)))
        dc = jnp.where(j < 2 * N_HEADS, ds_n, g) * dsv
        row = _iota2(s, HEAD_DIM, 0)
        dx = w[CONV_W - 1:CONV_W, :] * dc
        dws = [jnp.sum(dc * x, axis=0, keepdims=True)]
        for k in range(1, CONV_W):
            dc_ahead = jnp.where(row < s - k, pltpu.roll(dc, s - k, 0), 0.0)
            dx = dx + w[CONV_W - 1 - k:CONV_W - k, :] * dc_ahead
            dws.append(jnp.sum(dc_ahead * x, axis=0, keepdims=True))
        dx_ref[...] = dx.astype(BF16)
        for k in range(CONV_W):
            dw_ref[CONV_W - 1 - k:CONV_W - k, :] = dws[k]

    blk = pl.BlockSpec((s, HEAD_DIM), lambda j: (0, j))
    wblk = pl.BlockSpec((CONV_W, HEAD_DIM), lambda j: (0, j))
    return pl.pallas_call(
        body, name=name, grid=(nb,), in_specs=[blk, wblk, blk], out_specs=[blk, wblk],
        out_shape=[jax.ShapeDtypeStruct((s, 3 * BR_WIDTH), BF16), jax.ShapeDtypeStruct((CONV_W, 3 * BR_WIDTH), F32)],
        compiler_params=_cp("parallel"),
    )(proj, conv_w, dqkv)


def _tri(n, kind):
    r, c = _iota2(n, n, 0), _iota2(n, n, 1)
    if kind == "lower":
        return (r >= c).astype(F32)
    if kind == "upper":
        return (r <= c).astype(F32)
    return (r == c).astype(F32)


GATE_TILE = 512


def _dn_gate_fwd(proj, a_log, dt_bias, *, name):
    s = proj.shape[0]
    tr = min(GATE_TILE, s)

    def body(b_ref, a_ref, al_ref, dt_ref, beta_ref, g_ref):
        beta_ref[...] = _sigmoid(b_ref[...])
        g = -jnp.exp(al_ref[...]) * _softplus(a_ref[...] + dt_ref[...])
        low = _tri(CHUNK, "lower")
        for c in range(tr // CHUNK):
            rows = slice(c * CHUNK, (c + 1) * CHUNK)
            g_ref[rows, :] = _nn_exact(low, g[rows, :])

    blk = lambda cb: pl.BlockSpec((tr, HEAD_DIM), lambda i: (i, cb))
    vec = pl.BlockSpec((1, HEAD_DIM), lambda i: (0, 0))
    out = pl.BlockSpec((tr, HEAD_DIM), lambda i: (i, 0))
    return pl.pallas_call(
        body, name=name, grid=(s // tr,), in_specs=[blk(C_B // HEAD_DIM), blk(C_A // HEAD_DIM), vec, vec],
        out_specs=[out, out], out_shape=[jax.ShapeDtypeStruct((s, HEAD_DIM), F32)] * 2, compiler_params=_cp("parallel"),
    )(proj, proj, a_log, dt_bias)


def _dn_gate_bwd(proj, a_log, dt_bias, dbeta, d_g, *, name):
    s = proj.shape[0]
    tr = min(GATE_TILE, s)

    def body(b_ref, a_ref, al_ref, dt_ref, dbeta_ref, dG_ref, db_ref, da_ref, dal_ref, ddt_ref):
        @pl.when(pl.program_id(0) == 0)
        def _():
            dal_ref[...] = jnp.zeros_like(dal_ref)
            ddt_ref[...] = jnp.zeros_like(ddt_ref)

        beta = _sigmoid(b_ref[...])
        db_ref[...] = (dbeta_ref[...] * beta * (1.0 - beta)).astype(BF16)
        pre = a_ref[...] + dt_ref[...]
        neg_ea = -jnp.exp(al_ref[...])
        up = _tri(CHUNK, "upper")
        d_g = dG_ref[...]
        dg = jnp.concatenate([_nn_exact(up, d_g[c * CHUNK:(c + 1) * CHUNK, :]) for c in range(tr // CHUNK)], axis=0)
        da = dg * neg_ea * _sigmoid(pre)
        da_ref[...] = da.astype(BF16)
        ddt_ref[...] += jnp.sum(da, axis=0, keepdims=True)
        dal_ref[...] += jnp.sum(dg * neg_ea * _softplus(pre), axis=0, keepdims=True)

    blk = lambda cb: pl.BlockSpec((tr, HEAD_DIM), lambda i: (i, cb))
    vec = pl.BlockSpec((1, HEAD_DIM), lambda i: (0, 0))
    io = pl.BlockSpec((tr, HEAD_DIM), lambda i: (i, 0))
    return pl.pallas_call(
        body, name=name, grid=(s // tr,),
        in_specs=[blk(C_B // HEAD_DIM), blk(C_A // HEAD_DIM), vec, vec, io, io], out_specs=[io, io, vec, vec],
        out_shape=[jax.ShapeDtypeStruct((s, HEAD_DIM), BF16)] * 2 + [jax.ShapeDtypeStruct((1, HEAD_DIM), F32)] * 2,
        compiler_params=_cp("arbitrary"),
    )(proj, proj, a_log, dt_bias, dbeta, d_g)


def _unit_lower_inverse(a_strict, eye):
    x = -a_strict
    t = x + eye
    p = x
    n = 2
    while n < CHUNK:
        p = _nn(p, p)
        t = t + _nn(t, p)
        n *= 2
    return t


def _rows(*xs):
    return _hmap(lambda *a: jnp.concatenate(a, axis=0), *xs)


def _lanes(*xs):
    return _hmap(lambda *a: jnp.concatenate(a, axis=1), *xs)


def _dn_chunk_common(q, k, v, gc, beta, st, with_qd_state, t_inv=None):
    c, d = CHUNK, HEAD_DIM
    eye = _tri(c, "eye")
    low = _tri(c, "lower")
    strict = low - eye
    grow = _col2row(gc, eye)
    dec = _hmap(lambda g_, gr: low * jnp.exp(low * (g_ - gr)), gc, grow)
    kb = k * beta
    kq = _nt(_rows(kb, q), k)
    a_mat = kq[0:c, :] * dec * strict
    qk = kq[c:2 * c, :] * dec
    if t_inv is None:
        t_inv = _unit_lower_inverse(a_mat, eye)
    e_g = _exp(gc)
    qd = q * e_g
    uw = _nn(t_inv, _lanes(v * beta, kb * e_g))
    u, w = uw[:, 0:d], uw[:, d:2 * d]
    last = (_iota2(c, 1, 0) == c - 1).astype(F32)
    g_last = _sum(gc * last, 0)
    e_t = _exp(g_last - gc)
    kt = k * e_t
    tail = _exp(g_last)
    if with_qd_state:
        ws = _nn(_rows(w, qd), st)
        vn, qds = u - ws[0:c, :], ws[c:2 * c, :]
    else:
        vn, qds = u - _nn(w, st), None
    return dict(eye=eye, low=low, strict=strict, dec=dec, kb=kb, a_mat=a_mat, t_inv=t_inv, e_g=e_g, u=u, w=w, uw=uw,
                qk=qk, qd=qd, qds=qds, last=last, e_t=e_t, kt=kt, tail=tail, vn=vn)


def _dn_chunk_fwd_math(q, k, v, gc, beta, st):
    m = _dn_chunk_common(q, k, v, gc, beta, st, True)
    o = m["qds"] + _nn(m["qk"], m["vn"])
    st2 = st * m["tail"] + _tn(m["kt"], m["vn"])
    return o, st2, m["t_inv"]


def _dn_chunk_bwd_math(q, k, v, gc, beta, st, do, dst2, t_inv=None):
    c, d = CHUNK, HEAD_DIM
    m = _dn_chunk_common(q, k, v, gc, beta, st, False, t_inv)
    eye, low, strict = m["eye"], m["low"], m["strict"]
    dvn = _tn(m["qk"], do) + _nn(m["kt"], dst2)
    dqk = _nt(do, m["vn"]) * low
    both = _rows(do, dvn)
    ds_both = _nt(both, st)
    dqd, dw = ds_both[0:c, :], -ds_both[c:2 * c, :]
    dst = _tn(_rows(m["qd"], -m["w"]), both) + dst2 * m["tail"]
    dkt = _nt(m["vn"], dst2)
    dtail = _sum(_sum(st * dst2, 1), 0)
    dvb_dkg = _tn(m["t_inv"], _lanes(dvn, dw))
    dvb, dkg = dvb_dkg[:, 0:d], dvb_dkg[:, d:2 * d]
    d_a = _nt(dvb_dkg, m["uw"]) * (-strict)
    dkk = d_a * m["dec"]
    dp = dqk * m["dec"]
    dpk = _rows(dp, dkk)
    dq_dkb = _nn(dpk, k)
    dq = dq_dkb[0:c, :] + dqd * m["e_g"]
    dkb = dq_dkb[c:2 * c, :] + dkg * m["e_g"]
    dk = _tn(dpk, _rows(q, m["kb"])) + dkb * beta + dkt * m["e_t"]
    dv = dvb * beta
    dbeta = _sum(dvb * v + dkb * k, 1)
    de_g = _sum(dkg * m["kb"] + dqd * q, 1)
    de_t = _sum(dkt * k, 1)
    mm = d_a * m["a_mat"] + dqk * m["qk"]
    dgc = (_sum(mm, 1) - _row2col(_sum(mm, 0), eye) + de_g * m["e_g"] - de_t * m["e_t"]
           + (_sum(de_t * m["e_t"], 0) + dtail * m["tail"]) * m["last"])
    return dq, dk, dv, dgc, dbeta, dst


def _heads_of(ref):
    return _Heads(ref[:, h * HEAD_DIM:(h + 1) * HEAD_DIM] for h in range(N_HEADS))


def _lanes_of(block):
    return _Heads(_pick_lane(block, h) for h in range(N_HEADS))


def _dn_chunk_fwd(qkv, gcs, beta, proj, norm_w, *, name):
    s = qkv.shape[0]
    n = s // CHUNK

    def body(q_ref, k_ref, v_ref, g_ref, b_ref, z_ref, w_ref, o_ref, st_out_ref, tinv_ref, y_ref, st_ref):
        @pl.when(pl.program_id(0) == 0)
        def _():
            st_ref[...] = jnp.zeros_like(st_ref)

        gblk, bblk = g_ref[...], b_ref[...]
        st = _Heads(st_ref[h] for h in range(N_HEADS))
        o, st2, t_inv = _dn_chunk_fwd_math(_heads_of(q_ref), _heads_of(k_ref), _heads_of(v_ref), _lanes_of(gblk),
                                           _lanes_of(bblk), st)
        y = _head_norm_fwd(o, _heads_of(z_ref), w_ref[...])
        for h in range(N_HEADS):
            st_out_ref[0, h] = st.v[h]
            tinv_ref[0, h] = t_inv.v[h].astype(BF16)
            o_ref[:, h * HEAD_DIM:(h + 1) * HEAD_DIM] = o.v[h]
            y_ref[:, h * HEAD_DIM:(h + 1) * HEAD_DIM] = y.v[h]
            st_ref[h] = st2.v[h]

    blk = lambda off: pl.BlockSpec((CHUNK, BR_WIDTH), lambda c: (c, off))
    sc = pl.BlockSpec((CHUNK, HEAD_DIM), lambda c: (c, 0))
    return pl.pallas_call(
        body, name=name, grid=(n,),
        in_specs=[blk(0), blk(1), blk(2), sc, sc, blk(C_Z // BR_WIDTH), pl.BlockSpec((1, HEAD_DIM), lambda c: (0, 0))],
        out_specs=[blk(0), pl.BlockSpec((1, N_HEADS, HEAD_DIM, HEAD_DIM), lambda c: (c, 0, 0, 0)),
                   pl.BlockSpec((1, N_HEADS, CHUNK, CHUNK), lambda c: (c, 0, 0, 0)), blk(0)],
        out_shape=[jax.ShapeDtypeStruct((s, BR_WIDTH), F32), jax.ShapeDtypeStruct((n, N_HEADS, HEAD_DIM, HEAD_DIM), F32),
                   jax.ShapeDtypeStruct((n, N_HEADS, CHUNK, CHUNK), BF16), jax.ShapeDtypeStruct((s, BR_WIDTH), BF16)],
        scratch_shapes=[pltpu.VMEM((N_HEADS, HEAD_DIM, HEAD_DIM), F32)],
        compiler_params=_cp("arbitrary"),
    )(qkv, qkv, qkv, gcs, beta, proj, norm_w.reshape(1, HEAD_DIM))


def _dn_chunk_bwd(qkv, gcs, beta, states, tinvs, o, proj, norm_w, dy, *, name):
    s = qkv.shape[0]
    n = s // CHUNK

    def body(q_ref, k_ref, v_ref, g_ref, b_ref, st_in_ref, tinv_ref, o_ref, z_ref, w_ref, dy_ref,
             dqkv_ref, dg_ref, dbeta_ref, dz_ref, dw_ref, dst_ref):
        @pl.when(pl.program_id(0) == 0)
        def _():
            dst_ref[...] = jnp.zeros_like(dst_ref)
            dw_ref[...] = jnp.zeros_like(dw_ref)

        do, dz, dw = _head_norm_bwd(_heads_of(o_ref), _heads_of(z_ref), w_ref[...], _heads_of(dy_ref))
        dw_ref[...] += dw

        gblk, bblk = g_ref[...], b_ref[...]
        lane = _iota2(CHUNK, HEAD_DIM, 1)
        dg_all = jnp.zeros((CHUNK, HEAD_DIM), F32)
        dbeta_all = jnp.zeros((CHUNK, HEAD_DIM), F32)
        dq, dk, dv, dgc, dbeta, dst = _dn_chunk_bwd_math(
            _heads_of(q_ref), _heads_of(k_ref), _heads_of(v_ref), _lanes_of(gblk), _lanes_of(bblk),
            _Heads(st_in_ref[0, h] for h in range(N_HEADS)), do,
            _Heads(dst_ref[h] for h in range(N_HEADS)), _Heads(tinv_ref[0, h] for h in range(N_HEADS)))
        for h in range(N_HEADS):
            dz_ref[:, h * HEAD_DIM:(h + 1) * HEAD_DIM] = dz.v[h]
            for part, val in enumerate((dq, dk, dv)):
                c0 = part * BR_WIDTH + h * HEAD_DIM
                dqkv_ref[:, c0:c0 + HEAD_DIM] = val.v[h]
            dg_all = jnp.where(lane == h, dgc.v[h], dg_all)
            dbeta_all = jnp.where(lane == h, dbeta.v[h], dbeta_all)
            dst_ref[h] = dst.v[h]
        dg_ref[...] = dg_all
        dbeta_ref[...] = dbeta_all

    blk = lambda off: pl.BlockSpec((CHUNK, BR_WIDTH), lambda c: (n - 1 - c, off))
    sc = pl.BlockSpec((CHUNK, HEAD_DIM), lambda c: (n - 1 - c, 0))
    vec = pl.BlockSpec((1, HEAD_DIM), lambda c: (0, 0))
    outs = pl.pallas_call(
        body, name=name, grid=(n,),
        in_specs=[blk(0), blk(1), blk(2), sc, sc,
                  pl.BlockSpec((1, N_HEADS, HEAD_DIM, HEAD_DIM), lambda c: (n - 1 - c, 0, 0, 0)),
                  pl.BlockSpec((1, N_HEADS, CHUNK, CHUNK), lambda c: (n - 1 - c, 0, 0, 0)), blk(0), blk(C_Z // BR_WIDTH),
                  vec, blk(0)],
        out_specs=[pl.BlockSpec((CHUNK, 3 * BR_WIDTH), lambda c: (n - 1 - c, 0)), sc, sc, blk(0), vec],
        out_shape=[jax.ShapeDtypeStruct((s, 3 * BR_WIDTH), F32)] + [jax.ShapeDtypeStruct((s, HEAD_DIM), F32)] * 2
        + [jax.ShapeDtypeStruct((s, BR_WIDTH), BF16), jax.ShapeDtypeStruct((1, HEAD_DIM), F32)],
        scratch_shapes=[pltpu.VMEM((N_HEADS, HEAD_DIM, HEAD_DIM), F32)],
        compiler_params=_cp("arbitrary"),
    )(qkv, qkv, qkv, gcs, beta, states, tinvs, o, proj, norm_w.reshape(1, HEAD_DIM), dy)
    return outs


def _hg_chunk_common(q, k, g):
    c, nb = CHUNK, CHUNK // SUB
    e_g = _exp(g)
    qd = q * e_g
    g_last = g[c - 1:c, :]
    e_t = _exp(g_last - g)
    kt = k * e_t
    tail = _exp(g_last)
    g_refs = [g[i * SUB:i * SUB + 1, :] for i in range(nb)]
    g_ref_rows = _hmap(lambda *rows: jnp.concatenate([jnp.broadcast_to(r, (SUB, r.shape[1])) for r in rows], axis=0), *g_refs)
    e_q = _exp(g - g_ref_rows)
    q_sc = q * e_q
    e_k = [_hmap(lambda gr, g_: jnp.exp(jnp.minimum(gr - g_, EXP_CLAMP)), g_refs[i], g) for i in range(nb)]
    k_sc_all = _rows(*[k * e_k[i] for i in range(nb)])
    row_blk = _iota2(c, 1, 0) // SUB
    masks = [(row_blk == i).astype(F32) for i in range(nb)]
    r_all = _nt(q_sc, k_sc_all)
    a_mat = r_all[:, 0:c] * masks[0]
    for i in range(1, nb):
        a_mat = a_mat + r_all[:, i * c:(i + 1) * c] * masks[i]
    a_mat = a_mat * _tri(c, "lower")
    return dict(e_g=e_g, qd=qd, e_t=e_t, kt=kt, tail=tail, q_sc=q_sc, k_sc_all=k_sc_all, e_q=e_q, e_k=e_k, masks=masks,
                a_mat=a_mat)


def _hg_chunk_fwd_math(q, k, v, g, stt):
    m = _hg_chunk_common(q, k, g)
    o = _nt(m["qd"], stt) + _nn(m["a_mat"], v)
    stt2 = stt * m["tail"] + _tn(v, m["kt"])
    return o, stt2


def _hg_chunk_bwd_math(q, k, v, g, stt, do, dstt2):
    c, nb = CHUNK, CHUNK // SUB
    m = _hg_chunk_common(q, k, g)
    stt2 = stt * m["tail"] + _tn(v, m["kt"])
    later = _sum(stt2 * dstt2, 0)
    dqd = _dot3(do, stt, 1, 0)
    dstt = _tn(do, m["qd"]) + dstt2 * m["tail"]
    d_a = _dot3(do, v, 1, 1) * _tri(c, "lower")
    dv = _tn(m["a_mat"], do) + _nt(m["kt"], dstt2)
    dkt = _dot3(v, dstt2, 1, 0)
    d_blk = _lanes(*[d_a * m["masks"][i] for i in range(nb)])
    dq = dqd * m["e_g"] + _dot3(d_blk, m["k_sc_all"], 1, 0) * m["e_q"]
    dks = _dot3(d_blk, m["q_sc"], 0, 0)
    dk = dkt * m["e_t"]
    for i in range(nb):
        dk = dk + dks[i * c:(i + 1) * c, :] * m["e_k"][i]
    db = q * dq - k * dk
    return dq, dk, dv, db, later, dstt


def _hg_chunk_fwd(proj, lb, norm_w, *, name):
    s = proj.shape[0]
    n = s // CHUNK

    def body(hq_ref, hf_ref, v_ref, lb_ref, z_ref, w_ref, o_ref, st_out_ref, q_out, k_out, lf_out, y_ref, st_ref):
        @pl.when(pl.program_id(0) == 0)
        def _():
            st_ref[...] = jnp.zeros_like(st_ref)

        f, lbv = hf_ref[...], lb_ref[...]
        q_all = _silu(hq_ref[...])
        k_all = (1.0 - lbv) * _sigmoid(-f)
        lf_all = jnp.log(lbv + (1.0 - lbv) * _sigmoid(f))
        q_out[...], k_out[...], lf_out[...] = q_all, k_all, lf_all
        st = _Heads(st_ref[h] for h in range(N_HEADS))
        g_all = _nn_exact(_tri(CHUNK, "lower"), lf_all)
        o, st2 = _hg_chunk_fwd_math(_heads_of(q_all), _heads_of(k_all), _heads_of(v_ref), _heads_of(g_all), st)
        y = _head_norm_fwd(o, _heads_of(z_ref), w_ref[...])
        for h in range(N_HEADS):
            st_out_ref[0, h] = st.v[h]
            o_ref[:, h * HEAD_DIM:(h + 1) * HEAD_DIM] = o.v[h]
            y_ref[:, h * HEAD_DIM:(h + 1) * HEAD_DIM] = y.v[h]
            st_ref[h] = st2.v[h]

    blk = lambda off: pl.BlockSpec((CHUNK, BR_WIDTH), lambda c: (c, off))
    return pl.pallas_call(
        body, name=name, grid=(n,),
        in_specs=[blk(C_HQ // BR_WIDTH), blk(C_HF // BR_WIDTH), blk(C_HI // BR_WIDTH), pl.BlockSpec((1, BR_WIDTH), lambda c: (0, 0)),
                  blk(C_HZ // BR_WIDTH), pl.BlockSpec((1, HEAD_DIM), lambda c: (0, 0))],
        out_specs=[blk(0), pl.BlockSpec((1, N_HEADS, HEAD_DIM, HEAD_DIM), lambda c: (c, 0, 0, 0)), blk(0), blk(0), blk(0), blk(0)],
        out_shape=[jax.ShapeDtypeStruct((s, BR_WIDTH), F32), jax.ShapeDtypeStruct((n, N_HEADS, HEAD_DIM, HEAD_DIM), F32)]
        + [jax.ShapeDtypeStruct((s, BR_WIDTH), F32)] * 3 + [jax.ShapeDtypeStruct((s, BR_WIDTH), BF16)],
        scratch_shapes=[pltpu.VMEM((N_HEADS, HEAD_DIM, HEAD_DIM), F32)],
        compiler_params=_cp("arbitrary"),
    )(proj, proj, proj, lb, proj, norm_w.reshape(1, HEAD_DIM))


def _hg_chunk_bwd(proj, lb, qh, kh, lf, states, o, norm_w, dy, *, name):
    s = proj.shape[0]
    n = s // CHUNK

    def body(hq_ref, hf_ref, v_ref, lb_ref, q_ref, k_ref, lf_ref, st_in_ref, o_ref, z_ref, w_ref, dy_ref,
             dhq_ref, dhf_ref, dhi_ref, dz_ref, dlb_ref, dw_ref, dst_ref):
        @pl.when(pl.program_id(0) == 0)
        def _():
            dst_ref[...] = jnp.zeros_like(dst_ref)
            dlb_ref[...] = jnp.zeros_like(dlb_ref)
            dw_ref[...] = jnp.zeros_like(dw_ref)

        do, dz, dw = _head_norm_bwd(_heads_of(o_ref), _heads_of(z_ref), w_ref[...], _heads_of(dy_ref))
        dw_ref[...] += dw

        g_all = _nn_exact(_tri(CHUNK, "lower"), lf_ref[...])
        dq, dk, dv, db, later, dst = _hg_chunk_bwd_math(
            _heads_of(q_ref), _heads_of(k_ref), _heads_of(v_ref), _heads_of(g_all),
            _Heads(st_in_ref[0, h] for h in range(N_HEADS)), do,
            _Heads(dst_ref[h] for h in range(N_HEADS)))
        dlf = _nn_exact(_tri(CHUNK, "upper"), jnp.concatenate(db.v, axis=1)) + jnp.concatenate(later.v, axis=1)
        dq_all, dk_all = jnp.concatenate(dq.v, axis=1), jnp.concatenate(dk.v, axis=1)
        f, lbv = hf_ref[...], lb_ref[...]
        dhq_ref[...] = (dq_all * _dsilu(hq_ref[...])).astype(BF16)
        sp, sn = _sigmoid(f), _sigmoid(-f)
        dlf_over = dlf / (lbv + (1.0 - lbv) * sp)
        dhf_ref[...] = (dlf_over * (1.0 - lbv) * sp * sn - dk_all * (1.0 - lbv) * sn * (1.0 - sn)).astype(BF16)
        dlb_ref[...] += jnp.sum(dlf_over * (1.0 - sp) - dk_all * sn, axis=0, keepdims=True)
        for h in range(N_HEADS):
            dhi_ref[:, h * HEAD_DIM:(h + 1) * HEAD_DIM] = dv.v[h].astype(BF16)
            dz_ref[:, h * HEAD_DIM:(h + 1) * HEAD_DIM] = dz.v[h]
            dst_ref[h] = dst.v[h]

    blk = lambda off: pl.BlockSpec((CHUNK, BR_WIDTH), lambda c: (n - 1 - c, off))
    vec = pl.BlockSpec((1, BR_WIDTH), lambda c: (0, 0))
    wvec = pl.BlockSpec((1, HEAD_DIM), lambda c: (0, 0))
    return pl.pallas_call(
        body, name=name, grid=(n,),
        in_specs=[blk(C_HQ // BR_WIDTH), blk(C_HF // BR_WIDTH), blk(C_HI // BR_WIDTH), vec, blk(0), blk(0), blk(0),
                  pl.BlockSpec((1, N_HEADS, HEAD_DIM, HEAD_DIM), lambda c: (n - 1 - c, 0, 0, 0)), blk(0), blk(C_HZ // BR_WIDTH),
                  wvec, blk(1)],
        out_specs=[blk(0), blk(0), blk(0), blk(0), vec, wvec],
        out_shape=[jax.ShapeDtypeStruct((s, BR_WIDTH), BF16)] * 4 + [jax.ShapeDtypeStruct((1, BR_WIDTH), F32),
                                                                    jax.ShapeDtypeStruct((1, HEAD_DIM), F32)],
        scratch_shapes=[pltpu.VMEM((N_HEADS, HEAD_DIM, HEAD_DIM), F32)],
        compiler_params=_cp("arbitrary"),
    )(proj, proj, proj, lb, qh, kh, lf, states, o, proj, norm_w.reshape(1, HEAD_DIM), dy)


_ANY = pl.BlockSpec(memory_space=pl.ANY)
_MESH = pl.DeviceIdType.MESH


def _all_gather(x_local, *, name, after=(), norm=None):
    n_after = len(after)
    n_norm = 0 if norm is None else 2

    def norm_rows(h_ref, w_ref, hn_ref, hbuf, obuf, in_sems, out_sems):
        n_tiles = h_ref.shape[0] // ROW_TILE

        def load(t):
            return pltpu.make_async_copy(h_ref.at[pl.ds(t * ROW_TILE, ROW_TILE)], hbuf.at[t % 2], in_sems.at[t % 2])

        def store(t):
            return pltpu.make_async_copy(obuf.at[t % 2], hn_ref.at[pl.ds(t * ROW_TILE, ROW_TILE)], out_sems.at[t % 2])

        load(0).start()
        for t in range(n_tiles):
            if t + 1 < n_tiles:
                load(t + 1).start()
            load(t).wait()
            if t >= 2:
                store(t - 2).wait()
            xv = hbuf[t % 2]
            r = lax.rsqrt(jnp.mean(xv * xv, axis=-1, keepdims=True) + NORM_EPS)
            obuf[t % 2] = (xv * r * w_ref[...]).astype(BF16)
            store(t).start()
        for t in range(max(n_tiles - 2, 0), n_tiles):
            store(t).wait()

    def body(x_ref, *refs):
        if norm is None:
            out_ref, send_sems, recv_sems, local_sem = refs[n_after:]
        else:
            h_ref, w_ref = refs[n_after:n_after + 2]
            out_ref, hn_ref, send_sems, recv_sems, local_sem, hbuf, obuf, in_sems, out_sems = refs[n_after + 2:]
        x, y, c = lax.axis_index("x"), lax.axis_index("y"), lax.axis_index("c")
        me, sibling = (x, y, c), (x, y, 1 - c)
        n1 = (x ^ (1 - c), y ^ c)
        n2 = (x ^ c, y ^ (1 - c))
        dg = (1 - x, 1 - y)

        def slot(px, py, pc):
            return out_ref.at[4 * px + 2 * py + pc]

        def copy(k, block, to, src=None):
            return pltpu.make_async_remote_copy(
                src_ref=slot(*block) if src is None else src, dst_ref=slot(*block),
                send_sem=send_sems.at[k], recv_sem=recv_sems.at[k], device_id=to, device_id_type=_MESH)

        mine = pltpu.make_async_copy(x_ref, slot(*me), local_sem)
        mine.start()
        first = [copy(0, me, sibling, src=x_ref), copy(1, me, (*n1, c), src=x_ref), copy(2, me, (*n2, c), src=x_ref)]
        for cp in first:
            cp.start()
        if norm is not None:
            norm_rows(h_ref, w_ref, hn_ref, hbuf, obuf, in_sems, out_sems)
        copy(2, (*n2, c), me).wait_recv()
        forward = copy(3, (*n2, c), (*n1, c))
        forward.start()
        passed = [copy(5, (*n2, c), sibling)]
        passed[0].start()
        copy(1, (*n1, c), me).wait_recv()
        passed.append(copy(4, (*n1, c), sibling))
        passed[1].start()
        copy(3, (*dg, c), me).wait_recv()
        passed.append(copy(6, (*dg, c), sibling))
        passed[2].start()
        copy(0, sibling, me).wait_recv()
        copy(4, (*n2, 1 - c), me).wait_recv()
        copy(5, (*n1, 1 - c), me).wait_recv()
        copy(6, (*dg, 1 - c), me).wait_recv()
        for cp in first + [forward] + passed:
            cp.wait_send()
        mine.wait()

    gathered = jax.ShapeDtypeStruct((N_DEV,) + x_local.shape, x_local.dtype)
    sems = [pltpu.SemaphoreType.DMA((7,)), pltpu.SemaphoreType.DMA((7,)), pltpu.SemaphoreType.DMA]
    if norm is None:
        return pl.pallas_call(body, name=name, out_shape=gathered, in_specs=[_ANY] * (1 + n_after), out_specs=_ANY,
                              scratch_shapes=sems)(x_local, *after)
    h, w = norm
    d = h.shape[1]
    return pl.pallas_call(
        body, name=name, out_shape=[gathered, jax.ShapeDtypeStruct(h.shape, BF16)],
        in_specs=[_ANY] * (2 + n_after) + [pl.BlockSpec(memory_space=pltpu.VMEM)], out_specs=[_ANY, _ANY],
        scratch_shapes=sems + [pltpu.VMEM((2, ROW_TILE, d), F32), pltpu.VMEM((2, ROW_TILE, d), BF16),
                               pltpu.SemaphoreType.DMA((2,)), pltpu.SemaphoreType.DMA((2,))],
    )(x_local, *after, h, w.reshape(1, d))


_HBM = pl.BlockSpec(memory_space=pltpu.HBM)
_SEM = pl.BlockSpec(memory_space=pltpu.SEMAPHORE)
_EFFECT = pltpu.SideEffectType.DATAFLOW_SIDE_EFFECTING


def _peers():
    x, y, c = lax.axis_index("x"), lax.axis_index("y"), lax.axis_index("c")
    out = []
    for k in range(1, N_DEV):
        px, py, pc = x ^ ((k >> 2) & 1), y ^ ((k >> 1) & 1), c ^ (k & 1)
        out.append(((px, py, pc), 4 * px + 2 * py + pc))
    return 4 * x + 2 * y + c, out


def _push_copies(src_ref, land_ref, send_sems, recv_sems, broadcast):
    my, peers = _peers()
    pairs = []
    for k, (pos, idx) in enumerate(peers):
        src = src_ref if broadcast else src_ref.at[idx]
        send = pltpu.make_async_remote_copy(src_ref=src, dst_ref=land_ref.at[my], send_sem=send_sems.at[k],
                                            recv_sem=recv_sems.at[k], device_id=pos, device_id_type=_MESH)
        recv = pltpu.make_async_remote_copy(src_ref=src, dst_ref=land_ref.at[idx], send_sem=send_sems.at[k],
                                            recv_sem=recv_sems.at[k], device_id=pos, device_id_type=_MESH)
        pairs.append((send, recv))
    return pairs


def _push_start(src, land, *, broadcast, name, after=()):
    n_after = len(after)

    def body(src_ref, land_ref, *refs):
        send_sems, recv_sems, _, _, token = refs[n_after:]
        for send, _ in _push_copies(src_ref, land_ref, send_sems, recv_sems, broadcast):
            send.start()
        token[...] = jnp.zeros_like(token)

    return pl.pallas_call(
        body, name=name,
        out_shape=(pltpu.SemaphoreType.DMA((N_DEV - 1,)), pltpu.SemaphoreType.DMA((N_DEV - 1,)),
                   pltpu.HBM(src.shape, src.dtype), pltpu.HBM(land.shape, land.dtype), jax.ShapeDtypeStruct((8, 128), F32)),
        in_specs=(_HBM, _HBM) + (_ANY,) * n_after, out_specs=(_SEM, _SEM, _HBM, _HBM, pl.BlockSpec(memory_space=pltpu.VMEM)),
        input_output_aliases={0: 2, 1: 3}, compiler_params=pltpu.CompilerParams(has_side_effects=_EFFECT),
    )(pltpu.with_memory_space_constraint(src, pltpu.HBM), pltpu.with_memory_space_constraint(land, pltpu.HBM), *after)


def _push_wait(handle, after, *, broadcast, name):
    send_sems, recv_sems, src_thru, land_thru, _ = handle

    def body(src_ref, land_ref, send_sems, recv_sems, *rest):
        for send, recv in _push_copies(src_ref, land_ref, send_sems, recv_sems, broadcast):
            send.wait_send()
            recv.wait_recv()

    return pl.pallas_call(
        body, name=name,
        out_shape=(pltpu.HBM(src_thru.shape, src_thru.dtype), pltpu.HBM(land_thru.shape, land_thru.dtype)),
        in_specs=(_HBM, _HBM, _SEM, _SEM) + (_ANY,) * len(after), out_specs=(_HBM, _HBM),
        input_output_aliases={0: 0, 1: 1}, compiler_params=pltpu.CompilerParams(has_side_effects=_EFFECT),
    )(src_thru, land_thru, send_sems, recv_sems, *after)[1]


def _relay_copies(src_ref, land_ref, sems_a, sems_b):
    x, y, c = lax.axis_index("x"), lax.axis_index("y"), lax.axis_index("c")
    slot = lambda px, py, pc: land_ref.at[4 * px + 2 * py + pc]
    chips = [(1 - x, y), (x, 1 - y), (1 - x, 1 - y)]
    (send_a, recv_a), (send_b, recv_b) = sems_a, sems_b

    def copy(sems, k, src, dst_slot, to):
        return pltpu.make_async_remote_copy(src_ref=src, dst_ref=dst_slot, send_sem=sems[0].at[k], recv_sem=sems[1].at[k],
                                            device_id=to, device_id_type=_MESH)

    first = [copy((send_a, recv_a), 0, src_ref, slot(x, y, c), (x, y, 1 - c))]
    first += [copy((send_a, recv_a), 1 + j, src_ref, slot(x, y, c), (*chip, c)) for j, chip in enumerate(chips)]
    first_in = [copy((send_a, recv_a), 0, src_ref, slot(x, y, 1 - c), (x, y, 1 - c))]
    first_in += [copy((send_a, recv_a), 1 + j, src_ref, slot(*chip, c), (*chip, c)) for j, chip in enumerate(chips)]
    relay = [copy((send_b, recv_b), j, slot(*chip, c), slot(*chip, c), (x, y, 1 - c)) for j, chip in enumerate(chips)]
    relay_in = [copy((send_b, recv_b), j, slot(*chip, 1 - c), slot(*chip, 1 - c), (x, y, 1 - c)) for j, chip in enumerate(chips)]
    return first, first_in, relay, relay_in


def _relay_start(src, land, *, name, after=()):
    n_after = len(after)

    def body(src_ref, land_ref, *refs):
        send_a, recv_a, _, _, token = refs[n_after:]
        for cp in _relay_copies(src_ref, land_ref, (send_a, recv_a), (send_a, recv_a))[0]:
            cp.start()
        token[...] = jnp.zeros_like(token)

    send_a, recv_a, src_thru, land_thru, token = pl.pallas_call(
        body, name=name,
        out_shape=(pltpu.SemaphoreType.DMA((4,)), pltpu.SemaphoreType.DMA((4,)), pltpu.HBM(src.shape, src.dtype),
                   pltpu.HBM(land.shape, land.dtype), jax.ShapeDtypeStruct((8, 128), F32)),
        in_specs=(_HBM, _HBM) + (_ANY,) * n_after, out_specs=(_SEM, _SEM, _HBM, _HBM, pl.BlockSpec(memory_space=pltpu.VMEM)),
        input_output_aliases={0: 2, 1: 3}, compiler_params=pltpu.CompilerParams(has_side_effects=_EFFECT),
    )(pltpu.with_memory_space_constraint(src, pltpu.HBM), pltpu.with_memory_space_constraint(land, pltpu.HBM), *after)
    return (send_a, recv_a), src_thru, land_thru, token


def _relay_mid(handle, after, *, name):
    sems_a, src_thru, land_thru, _ = handle
    n_after = len(after)

    def body(src_ref, land_ref, send_a, recv_a, *refs):
        send_b, recv_b, _, _, token = refs[n_after:]
        _, first_in, relay, _ = _relay_copies(src_ref, land_ref, (send_a, recv_a), (send_b, recv_b))
        for j in range(3):
            first_in[1 + j].wait_recv()
            relay[j].start()
        token[...] = jnp.zeros_like(token)

    send_b, recv_b, src2, land2, token = pl.pallas_call(
        body, name=name,
        out_shape=(pltpu.SemaphoreType.DMA((3,)), pltpu.SemaphoreType.DMA((3,)), pltpu.HBM(src_thru.shape, src_thru.dtype),
                   pltpu.HBM(land_thru.shape, land_thru.dtype), jax.ShapeDtypeStruct((8, 128), F32)),
        in_specs=(_HBM, _HBM, _SEM, _SEM) + (_ANY,) * n_after,
        out_specs=(_SEM, _SEM, _HBM, _HBM, pl.BlockSpec(memory_space=pltpu.VMEM)),
        input_output_aliases={0: 2, 1: 3}, compiler_params=pltpu.CompilerParams(has_side_effects=_EFFECT),
    )(src_thru, land_thru, *sems_a, *after)
    return sems_a, (send_b, recv_b), src2, land2, token


def _relay_wait(handle, after, *, name):
    sems_a, sems_b, src_thru, land_thru, _ = handle

    def body(src_ref, land_ref, send_a, recv_a, send_b, recv_b, *rest):
        first, first_in, relay, relay_in = _relay_copies(src_ref, land_ref, (send_a, recv_a), (send_b, recv_b))
        first_in[0].wait_recv()
        for cp in relay_in:
            cp.wait_recv()
        for cp in first + relay:
            cp.wait_send()

    return pl.pallas_call(
        body, name=name,
        out_shape=(pltpu.HBM(src_thru.shape, src_thru.dtype), pltpu.HBM(land_thru.shape, land_thru.dtype)),
        in_specs=(_HBM, _HBM, _SEM, _SEM, _SEM, _SEM) + (_ANY,) * len(after), out_specs=(_HBM, _HBM),
        input_output_aliases={0: 0, 1: 1}, compiler_params=pltpu.CompilerParams(has_side_effects=_EFFECT),
    )(src_thru, land_thru, *sems_a, *sems_b, *after)[1]


def _adamw(parts, row_off, w, m, v, *, layer=0, n_layers=1, prev=None, name, tr):
    rows, c = w.shape
    r = rows // n_layers
    np_ = parts.shape[0]
    tr = min(tr, r)
    assert r % tr == 0 and row_off % tr == 0
    ob, lb = row_off // tr, layer * (r // tr)
    c1 = 1.0 - ADAM_B1 ** ADAM_STEP
    c2 = 1.0 - ADAM_B2 ** ADAM_STEP
    n_prev = 0 if prev is None else 4

    def body(p_ref, w_ref, m_ref, v_ref, *refs):
        g_ref, d_ref, nm_ref, nv_ref = refs[n_prev:]
        g = p_ref[0].astype(F32)
        for s in range(1, np_):
            g = g + p_ref[s].astype(F32)
        wv = w_ref[...]
        m2 = ADAM_B1 * m_ref[...] + (1.0 - ADAM_B1) * g
        v2 = ADAM_B2 * v_ref[...] + (1.0 - ADAM_B2) * jnp.square(g)
        m_hat = m2 / c1
        v_hat = v2 / c2
        g_ref[...] = g
        d_ref[...] = -ADAM_LR * (m_hat / (jnp.sqrt(v_hat) + ADAM_EPS) + ADAM_WD * wv)
        nm_ref[...] = m2
        nv_ref[...] = v2

    blk = pl.BlockSpec((tr, c), lambda i: (lb + i, 0))
    return pl.pallas_call(
        body, name=name, grid=(r // tr,),
        in_specs=[pl.BlockSpec((np_, tr, c), lambda i: (0, ob + i, 0)), blk, blk, blk] + [_ANY] * n_prev,
        out_specs=[blk] * 4, out_shape=[jax.ShapeDtypeStruct((rows, c), F32)] * 4,
        input_output_aliases={4 + i: i for i in range(n_prev)}, compiler_params=_cp("parallel"),
    )(parts, w, m, v, *(prev or ()))


def _sum_parts(parts, *, name, after=()):
    np_, r, c = parts.shape

    def body(p_ref, *refs):
        o_ref = refs[-1]
        g = p_ref[0]
        for s in range(1, np_):
            g = g + p_ref[s]
        o_ref[...] = g

    vmem = pl.BlockSpec(memory_space=pltpu.VMEM)
    return pl.pallas_call(body, name=name, in_specs=[vmem] + [_ANY] * len(after), out_specs=vmem,
                          out_shape=jax.ShapeDtypeStruct((r, c), F32))(parts, *after)


def _pack(arrs):
    rows = []
    for a in arrs:
        f = a.reshape(-1).astype(F32)
        pad = (-f.shape[0]) % 128
        rows.append(jnp.pad(f, (0, pad)).reshape(-1, 128))
    out = jnp.concatenate(rows, axis=0)
    return jnp.pad(out, ((0, (-out.shape[0]) % 8), (0, 0)))


def _unpack(packed, shapes):
    outs, r0 = [], 0
    for shp in shapes:
        n = 1
        for d in shp:
            n *= d
        nr = -(-n // 128)
        outs.append(packed[r0:r0 + nr].reshape(-1)[:n].reshape(shp))
        r0 += nr
    return outs


_WIN_PIECES = ((0, 4096, 0), (4112, 8208, 0), (4096, 4104, HEAD_DIM - N_HEADS), (4104, 4112, HEAD_DIM - N_HEADS))


RELAYOUT_TILE = 256
LAST_SPLIT = 4
OTHER_SPLIT = 2


def _win_from_shards(shards, *, name):
    k = shards.shape[1]
    tr = min(RELAYOUT_TILE, k)

    def body(x_ref, o_ref):
        cols = []
        for lo, hi, pad in _WIN_PIECES:
            for j in range(N_DEV):
                a, b = max(lo, j * SHARD_IN), min(hi, (j + 1) * SHARD_IN)
                if a < b:
                    cols.append(x_ref[j, :, a - j * SHARD_IN:b - j * SHARD_IN])
            if pad:
                cols.append(jnp.zeros((tr, pad), x_ref.dtype))
        o_ref[...] = jnp.concatenate(cols, axis=1)

    return pl.pallas_call(
        body, name=name, grid=(k // tr,), in_specs=[pl.BlockSpec((N_DEV, tr, SHARD_IN), lambda i: (0, i, 0))],
        out_specs=pl.BlockSpec((tr, N_PROJ), lambda i: (i, 0)), out_shape=jax.ShapeDtypeStruct((k, N_PROJ), shards.dtype),
        compiler_params=_cp("parallel"),
    )(shards)


def _win_to_shards(g, *, name):
    k = g.shape[0]
    tr = min(RELAYOUT_TILE, k)
    starts, off = [], 0
    for lo, hi, pad in _WIN_PIECES:
        starts.append((lo, hi, off))
        off += hi - lo + pad

    def body(g_ref, o_ref):
        for j in range(N_DEV):
            cols = []
            for lo, hi, off in sorted(starts):
                a, b = max(lo, j * SHARD_IN), min(hi, (j + 1) * SHARD_IN)
                if a < b:
                    cols.append(g_ref[:, off + a - lo:off + b - lo])
            o_ref[j] = jnp.concatenate(cols, axis=1)

    return pl.pallas_call(
        body, name=name, grid=(k // tr,), in_specs=[pl.BlockSpec((tr, N_PROJ), lambda i: (i, 0))],
        out_specs=pl.BlockSpec((N_DEV, tr, SHARD_IN), lambda i: (0, i, 0)),
        out_shape=jax.ShapeDtypeStruct((N_DEV, k, SHARD_IN), g.dtype), compiler_params=_cp("parallel"),
    )(g)


def _lower_bounds(logits):
    probs = jax.nn.softmax(logits.astype(F32), axis=0)
    return jnp.cumsum(probs, axis=0) - probs[0]


def _pad_lanes(vec8):
    return jnp.pad(vec8.reshape(1, N_HEADS), ((0, 0), (0, HEAD_DIM - N_HEADS)))


def kernel(x, p, norm_w, w_in, dn_conv_w, dn_A_log, dn_dt_bias, dn_norm_w, hg_lb_logits, hg_norm_w, w_out, w_ple_up, w_ple_gate, final_norm_w, loss_target, m_norm_w, m_w_in, m_dn_conv_w, m_dn_A_log, m_dn_dt_bias, m_dn_norm_w, m_hg_lb_logits, m_hg_norm_w, m_w_out, m_w_ple_up, m_w_ple_gate, m_final_norm_w, v_norm_w, v_w_in, v_dn_conv_w, v_dn_A_log, v_dn_dt_bias, v_dn_norm_w, v_hg_lb_logits, v_hg_norm_w, v_w_out, v_w_ple_up, v_w_ple_gate, v_final_norm_w):
    depth = norm_w.shape[0]
    my = 4 * lax.axis_index("x") + 2 * lax.axis_index("y") + lax.axis_index("c")
    h = x[0]
    tgt = loss_target[0]
    rows_out = D_MODEL // N_DEV
    up_rows = PLE_DIM * (D_MODEL // N_DEV) // D_MODEL
    g_off, u_off = rows_out, 2 * rows_out

    def own_slot(block):
        return lax.dynamic_update_index_in_dim(lax.empty((N_DEV,) + block.shape, block.dtype), block, my, 0)

    win_bf = w_in.astype(BF16)
    rest_bf = [jnp.concatenate([w_out[l], w_ple_gate[l], w_ple_up[l].reshape(up_rows, D_MODEL)], axis=0).astype(BF16)
               for l in range(depth)]
    conv_push = _push_start(dn_conv_w, own_slot(dn_conv_w), broadcast=True, name="gather_conv_w_start")
    win_l0, hn_l0 = _all_gather(win_bf[0], name="gather_w_in_l0", after=[conv_push[4]], norm=(h, norm_w[0]))
    win_all = {0: win_l0}
    pending, relayed = {}, {}
    last = win_all[0]
    for l in range(depth):
        if l > 0:
            relayed["win", l] = _relay_start(win_bf[l], own_slot(win_bf[l]), after=[last], name=f"gather_w_in_l{l}_first")
            last = relayed["win", l][3]
        if l == 0:
            relayed["rest", l] = _relay_start(rest_bf[l], own_slot(rest_bf[l]), after=[last], name=f"gather_rest_l{l}_first")
            last = relayed["rest", l][3]
        else:
            pending["rest", l] = _push_start(rest_bf[l], own_slot(rest_bf[l]), broadcast=True, after=[last],
                                             name=f"gather_rest_l{l}_start")
            last = pending["rest", l][4]
    lbs = _lower_bounds(hg_lb_logits)

    saved = []
    weights = []
    for l in range(depth):
        tag = f"l{l}"
        if l > 0:
            win_all[l] = _relay_wait(relayed["win", l], [h], name=f"gather_w_in_{tag}_wait")
        wi = _win_from_shards(win_all[l], name=f"w_in_layout_{tag}")
        hn = hn_l0 if l == 0 else _rms_fwd(h, norm_w[l], name=f"rms_fwd_{tag}")
        proj = _mm(hn, wi, mode="nn", out_dtype=F32, after=[last] if l == 0 else (), name=f"mm_proj_{tag}")
        al, dt = _pad_lanes(dn_A_log[l]), _pad_lanes(dn_dt_bias[l])
        if l == 0:
            conv_all = _push_wait(conv_push, [proj], broadcast=True, name="gather_conv_w_wait")
            conv_full = conv_all.transpose(1, 2, 0, 3).reshape(depth, CONV_W, 3 * BR_WIDTH)
        qkv = _dn_qkv_fwd(proj, conv_full[l], name=f"dn_qkv_fwd_{tag}")
        if ("rest", l) in relayed:
            relayed["rest", l] = _relay_mid(relayed["rest", l], [qkv], name=f"gather_rest_{tag}_relay")
            al = al + relayed["rest", l][4][0, 0]
        beta, gcs = _dn_gate_fwd(proj, al, dt, name=f"dn_gate_fwd_{tag}")
        o_dn, st_dn, tinv_dn, y_dn = _dn_chunk_fwd(qkv, gcs, beta, proj, dn_norm_w[l], name=f"dn_chunk_fwd_{tag}")
        lb = lbs[l].reshape(1, BR_WIDTH)
        o_hg, st_hg, qh, kh, lf, y_hg = _hg_chunk_fwd(proj, lb, hg_norm_w[l], name=f"hg_chunk_fwd_{tag}")
        y = jnp.concatenate([y_dn, y_hg], axis=1)
        if ("rest", l) in relayed:
            rest_all = _relay_wait(relayed["rest", l], [y], name=f"gather_rest_{tag}_wait")
        else:
            rest_all = _push_wait(pending["rest", l], [y], broadcast=True, name=f"gather_rest_{tag}_wait")
        w_out_rows, w_gate_rows = (0, rows_out), (g_off, rows_out)
        wu = rest_all[:, u_off:u_off + up_rows].reshape(N_DEV, PLE_DIM, D_MODEL // N_DEV).transpose(1, 0, 2).reshape(PLE_DIM, D_MODEL)
        weights.append((wi, rest_all, wu))
        h1 = _mm(y, rest_all, mode="nn", b_rows=w_out_rows, out_dtype=F32, res=h, name=f"mm_out_{tag}")
        pin = []
        if ("win", l + 1) in relayed:
            relayed["win", l + 1] = _relay_mid(relayed["win", l + 1], [h1], name=f"gather_w_in_l{l + 1}_relay")
            pin = [relayed["win", l + 1][4]]
        up = _mm(p[l, 0], wu, mode="nn", out_dtype=F32, name=f"mm_up_{tag}")
        gp, h2 = _mm(h1, rest_all, mode="nn", b_rows=w_gate_rows, out_dtype=F32, after=pin, tile_n=512, fused=("ple", h1, up),
                     name=f"mm_gate_{tag}")
        saved.append(dict(h=h, hn=hn, proj=proj, qkv=qkv, beta=beta, gcs=gcs, st_dn=st_dn, tinv_dn=tinv_dn, qh=qh, kh=kh, lf=lf,
                          st_hg=st_hg, o_dn=o_dn, o_hg=o_hg, y=y, h1=h1, gp=gp, up=up, al=al, dt=dt, lb=lb))
        h = h2

    loss_row, dh, d_final_w = _final_fwd_bwd(h, final_norm_w, tgt, name="final_norm_loss")

    d_norm_w, d_alog, d_dt, d_dn_nw, d_hg_nw, d_lb, d_conv = ([None] * depth for _ in range(7))
    sent = {}
    for l in reversed(range(depth)):
        wi, rest_all, wu = weights[l]
        sv = saved[l]
        tag = f"l{l}"
        dup, dgp = _ple_bwd(dh, sv["gp"], sv["up"], name=f"ple_bwd_{tag}")
        d_wu = _mm(p[l, 0], dup, mode="tn", out_dtype=BF16, name=f"mm_dwup_{tag}")
        d_wg = _mm(sv["h1"], dgp, mode="tn", out_dtype=BF16, name=f"mm_dwgate_{tag}")
        dh1 = _mm(dgp, rest_all, mode="nt", b_rows=(g_off, rows_out), out_dtype=F32, res=dh, name=f"mm_dh1_{tag}")
        d_wo = _mm(sv["y"], dh1, mode="tn", out_dtype=BF16, name=f"mm_dwout_{tag}")
        parts_rest = jnp.concatenate(
            [d_wo.reshape(N_DEV, rows_out, D_MODEL), d_wg.reshape(N_DEV, rows_out, D_MODEL),
             d_wu.reshape(PLE_DIM, N_DEV, D_MODEL // N_DEV).transpose(1, 0, 2).reshape(N_DEV, up_rows, D_MODEL)], axis=1)
        sent["rest", l] = _push_start(parts_rest, own_slot(parts_rest[my]), broadcast=False, name=f"exchange_rest_{tag}_start")
        dy = _mm(dh1, rest_all, mode="nt", b_rows=(0, rows_out), out_dtype=F32, name=f"mm_dy_{tag}")
        dn_nw = dn_norm_w[l] + sent["rest", l][4][0, 0]
        dqkv, d_gc, dbeta, dz_dn, d_dn_nw[l] = _dn_chunk_bwd(sv["qkv"], sv["gcs"], sv["beta"], sv["st_dn"], sv["tinv_dn"],
                                                             sv["o_dn"], sv["proj"], dn_nw, dy, name=f"dn_chunk_bwd_{tag}")
        dqkv_pre, d_conv[l] = _dn_qkv_bwd(sv["proj"], conv_full[l], dqkv, name=f"dn_qkv_bwd_{tag}")
        db, da, d_alog[l], d_dt[l] = _dn_gate_bwd(sv["proj"], sv["al"], sv["dt"], dbeta, d_gc, name=f"dn_gate_bwd_{tag}")
        dhq, dhf, dhi, dz_hg, d_lb[l], d_hg_nw[l] = _hg_chunk_bwd(sv["proj"], sv["lb"], sv["qh"], sv["kh"], sv["lf"], sv["st_hg"],
                                                                  sv["o_hg"], hg_norm_w[l], dy, name=f"hg_chunk_bwd_{tag}")
        dproj = jnp.concatenate([dqkv_pre, dz_dn, dhq, dhf, dhi, dz_hg, db, da], axis=1)
        def push_d_win(after):
            n_split = LAST_SPLIT if l == 0 else OTHER_SPLIT
            rows = D_MODEL // n_split
            handles = []
            for q in range(n_split):
                hn_q = sv["hn"] if n_split == 1 else sv["hn"][:, q * rows:(q + 1) * rows]
                sfx = tag if n_split == 1 else f"{tag}_{q}"
                d_win = _mm(hn_q, dproj, mode="tn", out_dtype=BF16, after=after, name=f"mm_dwin_{sfx}")
                parts_in = _win_to_shards(d_win, name=f"dw_in_shards_{sfx}")
                handles.append(_push_start(parts_in, own_slot(parts_in[my]), broadcast=False, after=after,
                                           name=f"exchange_w_in_{sfx}_start"))
                after = [handles[-1][4]]
            return handles

        if l == 0:
            small = _pack([loss_row, jnp.concatenate(d_norm_w[1:], axis=0), d_final_w,
                           jnp.stack([a[0, :N_HEADS] for a in d_alog]), jnp.stack([a[0, :N_HEADS] for a in d_dt]),
                           jnp.concatenate(d_dn_nw, axis=0), jnp.concatenate(d_hg_nw, axis=0), jnp.concatenate(d_lb, axis=0),
                           jnp.stack(d_conv)])
            small_push = _push_start(small, own_slot(small), broadcast=True, name="gather_small_start")
        sent["win", l] = push_d_win([small_push[4]] if l == 0 else [])
        dh, d_norm_w[l] = _mm(dproj, wi, mode="nt", out_dtype=F32, tile_m=512, tile_n=D_MODEL, tile_k=768,
                              after=[sent["win", l][-1][4]],
                              fused=("rms_bwd", sv["h"], norm_w[l], dh1), name=f"mm_dhn_{tag}")
    grad_x = dh[None]

    small_shapes = [(1, 128), (depth - 1, D_MODEL), final_norm_w.shape, dn_A_log.shape, dn_dt_bias.shape, dn_norm_w.shape,
                    hg_norm_w.shape, hg_lb_logits.shape, (depth, CONV_W, 3 * BR_WIDTH)]
    small_all = _push_wait(small_push, [grad_x], broadcast=True, name="gather_small_wait")
    tot = _unpack(_sum_parts(small_all, after=[grad_x], name="sum_small"), small_shapes)
    loss = tot[0][0, 0]
    g_lb = tot[7]
    g_logits = jax.vjp(_lower_bounds, hg_lb_logits)[1](g_lb)[0]
    g_conv = lax.dynamic_slice_in_dim(tot[8], my * (3 * BR_WIDTH // N_DEV), 3 * BR_WIDTH // N_DEV, axis=2)
    small_g = [g_conv, tot[3], tot[4], tot[5], g_logits, tot[6], tot[2]]
    small_w = [dn_conv_w, dn_A_log, dn_dt_bias, dn_norm_w, hg_lb_logits, hg_norm_w, final_norm_w]
    small_m = [m_dn_conv_w, m_dn_A_log, m_dn_dt_bias, m_dn_norm_w, m_hg_lb_logits, m_hg_norm_w, m_final_norm_w]
    small_v = [v_dn_conv_w, v_dn_A_log, v_dn_dt_bias, v_dn_norm_w, v_hg_lb_logits, v_hg_norm_w, v_final_norm_w]
    pk_w = _pack(small_w)
    res_small = _adamw(_pack(small_g)[None], 0, pk_w, _pack(small_m), _pack(small_v), name="adamw_small", tr=pk_w.shape[0])
    shapes_w = [a.shape for a in small_w]
    sg, sd, sm, sv_ = (_unpack(r, shapes_w) for r in res_small)

    r_win = r_wo = r_wg = r_wu = None
    done = [grad_x, res_small[0]]

    def flat(a, cols):
        return a.reshape(-1, cols)

    for l in reversed(range(depth)):
        tag = f"l{l}"
        land_rest = _push_wait(sent["rest", l], done, broadcast=False, name=f"exchange_rest_{tag}_wait")
        r_wo = _adamw(land_rest, 0, flat(w_out, D_MODEL), flat(m_w_out, D_MODEL), flat(v_w_out, D_MODEL), layer=l,
                      n_layers=depth, prev=r_wo, name=f"adamw_w_out_{tag}", tr=rows_out)
        r_wg = _adamw(land_rest, g_off, flat(w_ple_gate, D_MODEL), flat(m_w_ple_gate, D_MODEL), flat(v_w_ple_gate, D_MODEL),
                      layer=l, n_layers=depth, prev=r_wg, name=f"adamw_w_gate_{tag}", tr=rows_out)
        r_wu = _adamw(land_rest, u_off, flat(w_ple_up, D_MODEL), flat(m_w_ple_up, D_MODEL), flat(v_w_ple_up, D_MODEL),
                      layer=l, n_layers=depth, prev=r_wu, name=f"adamw_w_up_{tag}", tr=up_rows)
        done = [r_wo[0], r_wg[0], r_wu[0]]
    for l in reversed(range(depth)):
        tag = f"l{l}"
        if l == 0:
            nw0 = _sum_parts(_all_gather(_pack([d_norm_w[0]]), after=done, name="gather_norm_w"), name="sum_norm_w")
            g_norm_w = jnp.concatenate([_unpack(nw0, [(1, D_MODEL)])[0], tot[1]], axis=0)
            pk_nw = _pack([norm_w])
            r_nw = _adamw(_pack([g_norm_w])[None], 0, pk_nw, _pack([m_norm_w]), _pack([v_norm_w]), name="adamw_norm_w",
                          tr=pk_nw.shape[0])
            r_nw = [_unpack(r, [norm_w.shape])[0] for r in r_nw]
            done = [r_nw[0]]
        n_split = len(sent["win", l])
        for q, handle in enumerate(sent["win", l]):
            sfx = tag if n_split == 1 else f"{tag}_{q}"
            land_in = _push_wait(handle, done, broadcast=False, name=f"exchange_w_in_{sfx}_wait")
            r_win = _adamw(land_in, 0, flat(w_in, SHARD_IN), flat(m_w_in, SHARD_IN), flat(v_w_in, SHARD_IN),
                           layer=l * n_split + q, n_layers=depth * n_split, prev=r_win, name=f"adamw_w_in_{sfx}", tr=256)
            done = [r_win[0]]
    r_win = [o.reshape(w_in.shape) for o in r_win]
    r_wo = [o.reshape(w_out.shape) for o in r_wo]
    r_wg = [o.reshape(w_ple_gate.shape) for o in r_wg]
    r_wu = [o.reshape(w_ple_up.shape) for o in r_wu]

    def order(nw, small_list, big_in, big_out, big_up, big_gate):
        cw, al_, dt_, dnw, lbl, hnw, fw = small_list
        return [nw, big_in, cw, al_, dt_, dnw, lbl, hnw, big_out, big_up, big_gate, fw]

    outs = [loss, grad_x]
    for i, sl in enumerate((sg, sd, sm, sv_)):
        outs += order(r_nw[i], sl, r_win[i], r_wo[i], r_wu[i], r_wg[i])
    return tuple(outs)
```

```python
import functools

import jax
import jax.numpy as jnp
from jax import lax
from jax.experimental import pallas as pl
from jax.experimental.pallas import tpu as pltpu

F32 = jnp.float32
BF16 = jnp.bfloat16
HIGHEST = lax.Precision.HIGHEST

N_DEV = 8
D_MODEL = 2048
PLE_DIM = 256
HEAD_DIM = 128
N_HEADS = 8
BR_WIDTH = N_HEADS * HEAD_DIM
CHUNK = 64
SUB = 16
CONV_W = 4
NORM_EPS = 1e-6
L2_EPS = 1e-6
IN_WIDTH = 8208
SHARD_IN = IN_WIDTH // N_DEV
EXP_CLAMP = 80.0

C_QKV, C_Z, C_HQ, C_HF, C_HI, C_HZ, C_B, C_A, N_PROJ = 0, 3072, 4096, 5120, 6144, 7168, 8192, 8320, 8448

ADAM_LR, ADAM_B1, ADAM_B2, ADAM_EPS, ADAM_WD, ADAM_STEP = 0.001, 0.9, 0.999, 1e-08, 0.01, 10

VMEM_LIMIT = 48 * 1024 * 1024


def _cp(*sem):
    return pltpu.CompilerParams(dimension_semantics=sem, vmem_limit_bytes=VMEM_LIMIT)


class _Heads:
    def __init__(self, vals):
        self.v = tuple(vals)

    def __add__(self, o):
        return _hmap(lambda a, b: a + b, self, o)

    def __radd__(self, o):
        return _hmap(lambda a, b: b + a, self, o)

    def __sub__(self, o):
        return _hmap(lambda a, b: a - b, self, o)

    def __rsub__(self, o):
        return _hmap(lambda a, b: b - a, self, o)

    def __mul__(self, o):
        return _hmap(lambda a, b: a * b, self, o)

    def __rmul__(self, o):
        return _hmap(lambda a, b: b * a, self, o)

    def __neg__(self):
        return _hmap(lambda a: -a, self)

    def __getitem__(self, idx):
        return _hmap(lambda a: a[idx], self)


def _hmap(fn, *args):
    n = next((len(a.v) for a in args if isinstance(a, _Heads)), None)
    if n is None:
        return fn(*args)
    return _Heads(fn(*[a.v[i] if isinstance(a, _Heads) else a for a in args]) for i in range(n))


def _dot(a, b, ca, cb):
    return _hmap(lambda x, y: lax.dot_general(x.astype(BF16), y.astype(BF16), (((ca,), (cb,)), ((), ())),
                                              preferred_element_type=F32), a, b)


def _nn(a, b):
    return _dot(a, b, 1, 0)


def _nt(a, b):
    return _dot(a, b, 1, 1)


def _tn(a, b):
    return _dot(a, b, 0, 0)


def _split(a):
    hi = _hmap(lambda x: x.astype(BF16), a)
    return hi, _hmap(lambda x, h: (x - h.astype(F32)).astype(BF16), a, hi)


def _dot3(a, b, ca, cb):
    ah, al = _split(a)
    bh, bl = _split(b)
    return _dot(ah, bh, ca, cb) + (_dot(ah, bl, ca, cb) + _dot(al, bh, ca, cb))


def _nn_exact(a, b):
    return _hmap(lambda y: lax.dot_general(a, y, (((1,), (0,)), ((), ())), precision=HIGHEST,
                                           preferred_element_type=F32), b)


def _exp(x):
    return _hmap(jnp.exp, x)


def _sum(x, axis):
    return _hmap(lambda a: jnp.sum(a, axis=axis, keepdims=True), x)


def _sigmoid(x):
    return jax.nn.sigmoid(x)


def _silu(x):
    return x * _sigmoid(x)


def _dsilu(x):
    s = _sigmoid(x)
    return s * (1.0 + x * (1.0 - s))


def _silu_and_grad(x):
    s = _sigmoid(x)
    return x * s, s * (1.0 + x * (1.0 - s))


def _softplus(x):
    return jnp.maximum(x, 0.0) + jnp.log(1.0 + jnp.exp(-jnp.abs(x)))


def _iota2(n, m, axis):
    return lax.broadcasted_iota(jnp.int32, (n, m), axis)


def _col2row(col, eye):
    return _hmap(lambda c: jnp.sum(eye * c, axis=0, keepdims=True), col)


def _row2col(row, eye):
    return _hmap(lambda r: jnp.sum(eye * r, axis=1, keepdims=True), row)


def _pick_lane(block, lane_idx):
    lane = _iota2(block.shape[0], block.shape[1], 1)
    return jnp.sum(jnp.where(lane == lane_idx, block, 0.0), axis=1, keepdims=True)


MM_TILE_M, MM_TILE_N, MM_TILE_K = 1024, 1408, 2048


def _tile(dim, cap):
    if dim <= cap:
        return dim
    t = cap - cap % 128
    while dim % t:
        t -= 128
    return t


def _mm(a, b, *, mode, out_dtype, res=None, after=(), b_rows=None, tile_m=MM_TILE_M, tile_n=MM_TILE_N, tile_k=MM_TILE_K,
        fused=None, name):
    b_mat_rows = b.shape[0] if b_rows is None else N_DEV * b_rows[1]
    if mode == "nn":
        (m, kd), n = a.shape, b.shape[-1]
        assert kd == b_mat_rows
    elif mode == "nt":
        (m, kd), n = a.shape, b_mat_rows
    else:
        (kd, m), n = a.shape, b.shape[-1]
    tm, tn, tk = _tile(m, tile_m), _tile(n, tile_n), _tile(kd, tile_k)
    assert m % tm == 0 and n % tn == 0 and kd % tk == 0, (m, n, kd, tm, tn, tk)
    nk = kd // tk
    ca, cb = {"nn": (1, 0), "nt": (1, 1), "tn": (0, 0)}[mode]

    kind = None if fused is None else fused[0]
    n_in = 2 + (res is not None) + (0 if fused is None else len(fused) - 1)
    n_out = 1 if fused is None else 2

    def body(*refs):
        a_ref, b_ref = refs[:2]
        r_ref = None if res is None else refs[2]
        extra = refs[2 + (res is not None):n_in]
        outs = refs[-1 - n_out:-1]
        o_ref, acc_ref = outs[0], refs[-1]
        k = pl.program_id(2)

        @pl.when(k == 0)
        def _():
            acc_ref[...] = jnp.zeros_like(acc_ref)

        b_tile = b_ref[...]
        if b_rows is not None:
            b_tile = b_tile.reshape(-1, b_tile.shape[-1])
        acc_ref[...] += _dot(a_ref[...], b_tile, ca, cb)

        @pl.when(k == nk - 1)
        def _():
            out = acc_ref[...]
            if r_ref is not None:
                out = out + r_ref[...].astype(F32)
            if kind == "ple":
                h1_ref, up_ref = extra
                o_ref[...] = out
                outs[1][...] = h1_ref[...] + up_ref[...] * _sigmoid(out)
            elif kind == "rms_bwd":
                h_ref, w_ref, res_ref = extra
                dx, dwt = _rms_bwd_math(h_ref[...], w_ref[...], out)
                o_ref[...] = res_ref[...] + dx

                @pl.when(pl.program_id(0) == 0)
                def _():
                    outs[1][...] = jnp.zeros_like(outs[1])

                outs[1][...] += jnp.sum(dwt, axis=0, keepdims=True)
            else:
                o_ref[...] = out.astype(o_ref.dtype)

    a_spec = pl.BlockSpec((tk, tm), lambda i, j, k: (k, i)) if mode == "tn" else pl.BlockSpec((tm, tk), lambda i, j, k: (i, k))
    if b_rows is None:
        b_spec = pl.BlockSpec((tn, tk), lambda i, j, k: (j, k)) if mode == "nt" else pl.BlockSpec((tk, tn), lambda i, j, k: (k, j))
    else:
        first, count = b_rows
        assert first % count == 0 and mode in ("nn", "nt")
        rb = first // count
        if mode == "nn":
            assert tk == kd
            b_spec = pl.BlockSpec((N_DEV, count, tn), lambda i, j, k: (0, rb, j))
        else:
            assert tn % count == 0
            b_spec = pl.BlockSpec((tn // count, count, tk), lambda i, j, k: (j, rb, k))
    o_spec = pl.BlockSpec((tm, tn), lambda i, j, k: (i, j))
    row_spec = pl.BlockSpec((1, tn), lambda i, j, k: (0, j))
    extra_specs, extra_args, out_specs, out_shape = [], (), o_spec, jax.ShapeDtypeStruct((m, n), out_dtype)
    sem = ("parallel", "parallel", "arbitrary")
    if kind == "ple":
        extra_specs, extra_args = [o_spec, o_spec], tuple(fused[1:])
        out_specs, out_shape = [o_spec, o_spec], [jax.ShapeDtypeStruct((m, n), F32)] * 2
    elif kind == "rms_bwd":
        assert tn == n
        extra_specs, extra_args = [o_spec, row_spec, o_spec], (fused[1], fused[2].reshape(1, n), fused[3])
        out_specs, out_shape = [o_spec, row_spec], [jax.ShapeDtypeStruct((m, n), F32), jax.ShapeDtypeStruct((1, n), F32)]
        sem = ("arbitrary", "arbitrary", "arbitrary")
    in_specs = ([a_spec, b_spec] + ([o_spec] if res is not None else []) + extra_specs
                + [pl.BlockSpec(memory_space=pl.ANY)] * len(after))
    args = (a, b) + ((res,) if res is not None else ()) + extra_args + tuple(after)
    return pl.pallas_call(
        body, name=name, grid=(m // tm, n // tn, nk), in_specs=in_specs, out_specs=out_specs, out_shape=out_shape,
        scratch_shapes=[pltpu.VMEM((tm, tn), F32)], compiler_params=_cp(*sem),
    )(*args)


ROW_TILE = 256


def _rms_fwd(h, w, *, name):
    s, d = h.shape
    tr = min(ROW_TILE, s)

    def body(h_ref, w_ref, o_ref):
        x = h_ref[...]
        r = lax.rsqrt(jnp.mean(x * x, axis=-1, keepdims=True) + NORM_EPS)
        o_ref[...] = (x * r * w_ref[...]).astype(o_ref.dtype)

    return pl.pallas_call(
        body, name=name, grid=(s // tr,),
        in_specs=[pl.BlockSpec((tr, d), lambda i: (i, 0)), pl.BlockSpec((1, d), lambda i: (0, 0))],
        out_specs=pl.BlockSpec((tr, d), lambda i: (i, 0)),
        out_shape=jax.ShapeDtypeStruct((s, d), BF16), compiler_params=_cp("parallel"),
    )(h, w.reshape(1, d))


def _rms_bwd_math(x, w, dy):
    d = x.shape[-1]
    r = lax.rsqrt(jnp.mean(x * x, axis=-1, keepdims=True) + NORM_EPS)
    gw = dy * w
    dx = r * gw - x * ((r * r * r) * (jnp.sum(gw * x, axis=-1, keepdims=True) / d))
    return dx, dy * x * r


def _final_fwd_bwd(h, w, tgt, *, name):
    s, d = h.shape
    tr = min(ROW_TILE, s)

    def body(h_ref, w_ref, t_ref, loss_ref, dh_ref, dw_ref):
        @pl.when(pl.program_id(0) == 0)
        def _():
            loss_ref[...] = jnp.zeros_like(loss_ref)
            dw_ref[...] = jnp.zeros_like(dw_ref)

        x = h_ref[...]
        wv = w_ref[...]
        r = lax.rsqrt(jnp.mean(x * x, axis=-1, keepdims=True) + NORM_EPS)
        err = x * r * wv - t_ref[...]
        row_loss = jnp.mean(err * err, axis=-1, keepdims=True)
        loss_ref[...] += 0.5 * jnp.sum(row_loss, axis=0, keepdims=True)
        dx, dwt = _rms_bwd_math(x, wv, err / d)
        dh_ref[...] = dx
        dw_ref[...] += jnp.sum(dwt, axis=0, keepdims=True)

    row = pl.BlockSpec((tr, d), lambda i: (i, 0))
    vec = pl.BlockSpec((1, d), lambda i: (0, 0))
    return pl.pallas_call(
        body, name=name, grid=(s // tr,), in_specs=[row, vec, row],
        out_specs=[pl.BlockSpec((1, 128), lambda i: (0, 0)), row, vec],
        out_shape=[jax.ShapeDtypeStruct((1, 128), F32), jax.ShapeDtypeStruct((s, d), F32),
                   jax.ShapeDtypeStruct((1, d), F32)],
        compiler_params=_cp("arbitrary"),
    )(h, w.reshape(1, d), tgt)


def _ple_bwd(dh2, gate_pre, up, *, name):
    s, d = dh2.shape
    tr = min(ROW_TILE, s)

    def body(d_ref, g_ref, u_ref, dup_ref, dgp_ref):
        dh = d_ref[...]
        gate = _sigmoid(g_ref[...])
        dup_ref[...] = (dh * gate).astype(BF16)
        dgp_ref[...] = (dh * u_ref[...] * gate * (1.0 - gate)).astype(BF16)

    row = pl.BlockSpec((tr, d), lambda i: (i, 0))
    return pl.pallas_call(body, name=name, grid=(s // tr,), in_specs=[row, row, row], out_specs=[row, row],
                          out_shape=[jax.ShapeDtypeStruct((s, d), BF16)] * 2, compiler_params=_cp("parallel"))(dh2, gate_pre, up)


def _head_norm_fwd(o, z, w):
    return _hmap(lambda x, zz: (x * lax.rsqrt(jnp.mean(x * x, axis=-1, keepdims=True) + NORM_EPS) * w * _silu(zz)).astype(BF16),
                 o, z)


def _head_norm_bwd(o, z, w, dy):
    dos, dzs, dw = [], [], jnp.zeros((1, HEAD_DIM), F32)
    for x, zz, g in zip(o.v, z.v, dy.v):
        r = lax.rsqrt(jnp.mean(x * x, axis=-1, keepdims=True) + NORM_EPS)
        silu_z, dsilu_z = _silu_and_grad(zz)
        don = g * silu_z
        dzs.append((g * (x * r * w) * dsilu_z).astype(BF16))
        gw = don * w
        dos.append(r * gw - x * ((r * r * r) * (jnp.sum(gw * x, axis=-1, keepdims=True) / HEAD_DIM)))
        dw = dw + jnp.sum(don * x * r, axis=0, keepdims=True)
    return _Heads(dos), _Heads(dzs), dw


def _conv_silu(x, w, s):
    row = _iota2(s, x.shape[1], 0)
    c = w[CONV_W - 1:CONV_W, :] * x
    for k in range(1, CONV_W):
        c = c + w[CONV_W - 1 - k:CONV_W - k, :] * jnp.where(row >= k, pltpu.roll(x, k, 0), 0.0)
    return c


def _dn_qkv_fwd(proj, conv_w, *, name):
    s = proj.shape[0]
    nb = 3 * N_HEADS

    def body(x_ref, w_ref, o_ref):
        j = pl.program_id(0)
        sv = _silu(_conv_silu(x_ref[...], w_ref[...], s))
        r = lax.rsqrt(jnp.sum(sv * sv, axis=-1, keepdims=True) + L2_EPS)
        scale = jnp.where(j < N_HEADS, HEAD_DIM ** -0.5, 1.0).astype(F32)
        o_ref[...] = jnp.where(j < 2 * N_HEADS, sv * r * scale, sv)

    return pl.pallas_call(
        body, name=name, grid=(nb,),
        in_specs=[pl.BlockSpec((s, HEAD_DIM), lambda j: (0, j)), pl.BlockSpec((CONV_W, HEAD_DIM), lambda j: (0, j))],
        out_specs=pl.BlockSpec((s, HEAD_DIM), lambda j: (0, j)),
        out_shape=jax.ShapeDtypeStruct((s, 3 * BR_WIDTH), F32), compiler_params=_cp("parallel"),
    )(proj, conv_w)


def _dn_qkv_bwd(proj, conv_w, dqkv, *, name):
    s = proj.shape[0]
    nb = 3 * N_HEADS

    def body(x_ref, w_ref, g_ref, dx_ref, dw_ref):
        j = pl.program_id(0)
        x, w, g = x_ref[...], w_ref[...], g_ref[...]
        c = _conv_silu(x, w, s)
        sv, dsv = _silu_and_grad(c)
        r = lax.rsqrt(jnp.sum(sv * sv, axis=-1, keepdims=True) + L2_EPS)
        scale = jnp.where(j < N_HEADS, HEAD_DIM ** -0.5, 1.0).astype(F32)
        ds_n = scale * (r * g - sv * ((r * r * r) * jnp.sum(g * sv, axis=-1, keepdims=True)))
        dc = jnp.where(j < 2 * N_HEADS, ds_n, g) * dsv
        row = _iota2(s, HEAD_DIM, 0)
        dx = w[CONV_W - 1:CONV_W, :] * dc
        dws = [jnp.sum(dc * x, axis=0, keepdims=True)]
        for k in range(1, CONV_W):
            dc_ahead = jnp.where(row < s - k, pltpu.roll(dc, s - k, 0), 0.0)
            dx = dx + w[CONV_W - 1 - k:CONV_W - k, :] * dc_ahead
            dws.append(jnp.sum(dc_ahead * x, axis=0, keepdims=True))
        dx_ref[...] = dx.astype(BF16)
        for k in range(CONV_W):
            dw_ref[CONV_W - 1 - k:CONV_W - k, :] = dws[k]

    blk = pl.BlockSpec((s, HEAD_DIM), lambda j: (0, j))
    wblk = pl.BlockSpec((CONV_W, HEAD_DIM), lambda j: (0, j))
    return pl.pallas_call(
        body, name=name, grid=(nb,), in_specs=[blk, wblk, blk], out_specs=[blk, wblk],
        out_shape=[jax.ShapeDtypeStruct((s, 3 * BR_WIDTH), BF16), jax.ShapeDtypeStruct((CONV_W, 3 * BR_WIDTH), F32)],
        compiler_params=_cp("parallel"),
    )(proj, conv_w, dqkv)


def _tri(n, kind):
    r, c = _iota2(n, n, 0), _iota2(n, n, 1)
    if kind == "lower":
        return (r >= c).astype(F32)
    if kind == "upper":
        return (r <= c).astype(F32)
    return (r == c).astype(F32)


GATE_TILE = 512


def _dn_gate_fwd(proj, a_log, dt_bias, *, name):
    s = proj.shape[0]
    tr = min(GATE_TILE, s)

    def body(b_ref, a_ref, al_ref, dt_ref, beta_ref, g_ref):
        beta_ref[...] = _sigmoid(b_ref[...])
        g = -jnp.exp(al_ref[...]) * _softplus(a_ref[...] + dt_ref[...])
        low = _tri(CHUNK, "lower")
        for c in range(tr // CHUNK):
            rows = slice(c * CHUNK, (c + 1) * CHUNK)
            g_ref[rows, :] = _nn_exact(low, g[rows, :])

    blk = lambda cb: pl.BlockSpec((tr, HEAD_DIM), lambda i: (i, cb))
    vec = pl.BlockSpec((1, HEAD_DIM), lambda i: (0, 0))
    out = pl.BlockSpec((tr, HEAD_DIM), lambda i: (i, 0))
    return pl.pallas_call(
        body, name=name, grid=(s // tr,), in_specs=[blk(C_B // HEAD_DIM), blk(C_A // HEAD_DIM), vec, vec],
        out_specs=[out, out], out_shape=[jax.ShapeDtypeStruct((s, HEAD_DIM), F32)] * 2, compiler_params=_cp("parallel"),
    )(proj, proj, a_log, dt_bias)


def _dn_gate_bwd(proj, a_log, dt_bias, dbeta, d_g, *, name):
    s = proj.shape[0]
    tr = min(GATE_TILE, s)

    def body(b_ref, a_ref, al_ref, dt_ref, dbeta_ref, dG_ref, db_ref, da_ref, dal_ref, ddt_ref):
        @pl.when(pl.program_id(0) == 0)
        def _():
            dal_ref[...] = jnp.zeros_like(dal_ref)
            ddt_ref[...] = jnp.zeros_like(ddt_ref)

        beta = _sigmoid(b_ref[...])
        db_ref[...] = (dbeta_ref[...] * beta * (1.0 - beta)).astype(BF16)
        pre = a_ref[...] + dt_ref[...]
        neg_ea = -jnp.exp(al_ref[...])
        up = _tri(CHUNK, "upper")
        d_g = dG_ref[...]
        dg = jnp.concatenate([_nn_exact(up, d_g[c * CHUNK:(c + 1) * CHUNK, :]) for c in range(tr // CHUNK)], axis=0)
        da = dg * neg_ea * _sigmoid(pre)
        da_ref[...] = da.astype(BF16)
        ddt_ref[...] += jnp.sum(da, axis=0, keepdims=True)
        dal_ref[...] += jnp.sum(dg * neg_ea * _softplus(pre), axis=0, keepdims=True)

    blk = lambda cb: pl.BlockSpec((tr, HEAD_DIM), lambda i: (i, cb))
    vec = pl.BlockSpec((1, HEAD_DIM), lambda i: (0, 0))
    io = pl.BlockSpec((tr, HEAD_DIM), lambda i: (i, 0))
    return pl.pallas_call(
        body, name=name, grid=(s // tr,),
        in_specs=[blk(C_B // HEAD_DIM), blk(C_A // HEAD_DIM), vec, vec, io, io], out_specs=[io, io, vec, vec],
        out_shape=[jax.ShapeDtypeStruct((s, HEAD_DIM), BF16)] * 2 + [jax.ShapeDtypeStruct((1, HEAD_DIM), F32)] * 2,
        compiler_params=_cp("arbitrary"),
    )(proj, proj, a_log, dt_bias, dbeta, d_g)


def _unit_lower_inverse(a_strict, eye):
    x = -a_strict
    t = x + eye
    p = x
    n = 2
    while n < CHUNK:
        p = _nn(p, p)
        t = t + _nn(t, p)
        n *= 2
    return t


def _rows(*xs):
    return _hmap(lambda *a: jnp.concatenate(a, axis=0), *xs)


def _lanes(*xs):
    return _hmap(lambda *a: jnp.concatenate(a, axis=1), *xs)


def _dn_chunk_common(q, k, v, gc, beta, st, with_qd_state, t_inv=None):
    c, d = CHUNK, HEAD_DIM
    eye = _tri(c, "eye")
    low = _tri(c, "lower")
    strict = low - eye
    grow = _col2row(gc, eye)
    dec = _hmap(lambda g_, gr: low * jnp.exp(low * (g_ - gr)), gc, grow)
    kb = k * beta
    kq = _nt(_rows(kb, q), k)
    a_mat = kq[0:c, :] * dec * strict
    qk = kq[c:2 * c, :] * dec
    if t_inv is None:
        t_inv = _unit_lower_inverse(a_mat, eye)
    e_g = _exp(gc)
    qd = q * e_g
    uw = _nn(t_inv, _lanes(v * beta, kb * e_g))
    u, w = uw[:, 0:d], uw[:, d:2 * d]
    last = (_iota2(c, 1, 0) == c - 1).astype(F32)
    g_last = _sum(gc * last, 0)
    e_t = _exp(g_last - gc)
    kt = k * e_t
    tail = _exp(g_last)
    if with_qd_state:
        ws = _nn(_rows(w, qd), st)
        vn, qds = u - ws[0:c, :], ws[c:2 * c, :]
    else:
        vn, qds = u - _nn(w, st), None
    return dict(eye=eye, low=low, strict=strict, dec=dec, kb=kb, a_mat=a_mat, t_inv=t_inv, e_g=e_g, u=u, w=w, uw=uw,
                qk=qk, qd=qd, qds=qds, last=last, e_t=e_t, kt=kt, tail=tail, vn=vn)


def _dn_chunk_fwd_math(q, k, v, gc, beta, st):
    m = _dn_chunk_common(q, k, v, gc, beta, st, True)
    o = m["qds"] + _nn(m["qk"], m["vn"])
    st2 = st * m["tail"] + _tn(m["kt"], m["vn"])
    return o, st2, m["t_inv"]


def _dn_chunk_bwd_math(q, k, v, gc, beta, st, do, dst2, t_inv=None):
    c, d = CHUNK, HEAD_DIM
    m = _dn_chunk_common(q, k, v, gc, beta, st, False, t_inv)
    eye, low, strict = m["eye"], m["low"], m["strict"]
    dvn = _tn(m["qk"], do) + _nn(m["kt"], dst2)
    dqk = _nt(do, m["vn"]) * low
    both = _rows(do, dvn)
    ds_both = _nt(both, st)
    dqd, dw = ds_both[0:c, :], -ds_both[c:2 * c, :]
    dst = _tn(_rows(m["qd"], -m["w"]), both) + dst2 * m["tail"]
    dkt = _nt(m["vn"], dst2)
    dtail = _sum(_sum(st * dst2, 1), 0)
    dvb_dkg = _tn(m["t_inv"], _lanes(dvn, dw))
    dvb, dkg = dvb_dkg[:, 0:d], dvb_dkg[:, d:2 * d]
    d_a = _nt(dvb_dkg, m["uw"]) * (-strict)
    dkk = d_a * m["dec"]
    dp = dqk * m["dec"]
    dpk = _rows(dp, dkk)
    dq_dkb = _nn(dpk, k)
    dq = dq_dkb[0:c, :] + dqd * m["e_g"]
    dkb = dq_dkb[c:2 * c, :] + dkg * m["e_g"]
    dk = _tn(dpk, _rows(q, m["kb"])) + dkb * beta + dkt * m["e_t"]
    dv = dvb * beta
    dbeta = _sum(dvb * v + dkb * k, 1)
    de_g = _sum(dkg * m["kb"] + dqd * q, 1)
    de_t = _sum(dkt * k, 1)
    mm = d_a * m["a_mat"] + dqk * m["qk"]
    dgc = (_sum(mm, 1) - _row2col(_sum(mm, 0), eye) + de_g * m["e_g"] - de_t * m["e_t"]
           + (_sum(de_t * m["e_t"], 0) + dtail * m["tail"]) * m["last"])
    return dq, dk, dv, dgc, dbeta, dst


def _heads_of(ref):
    return _Heads(ref[:, h * HEAD_DIM:(h + 1) * HEAD_DIM] for h in range(N_HEADS))


def _lanes_of(block):
    return _Heads(_pick_lane(block, h) for h in range(N_HEADS))


def _dn_chunk_fwd(qkv, gcs, beta, proj, norm_w, *, name):
    s = qkv.shape[0]
    n = s // CHUNK

    def body(q_ref, k_ref, v_ref, g_ref, b_ref, z_ref, w_ref, o_ref, st_out_ref, tinv_ref, y_ref, st_ref):
        @pl.when(pl.program_id(0) == 0)
        def _():
            st_ref[...] = jnp.zeros_like(st_ref)

        gblk, bblk = g_ref[...], b_ref[...]
        st = _Heads(st_ref[h] for h in range(N_HEADS))
        o, st2, t_inv = _dn_chunk_fwd_math(_heads_of(q_ref), _heads_of(k_ref), _heads_of(v_ref), _lanes_of(gblk),
                                           _lanes_of(bblk), st)
        y = _head_norm_fwd(o, _heads_of(z_ref), w_ref[...])
        for h in range(N_HEADS):
            st_out_ref[0, h] = st.v[h]
            tinv_ref[0, h] = t_inv.v[h].astype(BF16)
            o_ref[:, h * HEAD_DIM:(h + 1) * HEAD_DIM] = o.v[h]
            y_ref[:, h * HEAD_DIM:(h + 1) * HEAD_DIM] = y.v[h]
            st_ref[h] = st2.v[h]

    blk = lambda off: pl.BlockSpec((CHUNK, BR_WIDTH), lambda c: (c, off))
    sc = pl.BlockSpec((CHUNK, HEAD_DIM), lambda c: (c, 0))
    return pl.pallas_call(
        body, name=name, grid=(n,),
        in_specs=[blk(0), blk(1), blk(2), sc, sc, blk(C_Z // BR_WIDTH), pl.BlockSpec((1, HEAD_DIM), lambda c: (0, 0))],
        out_specs=[blk(0), pl.BlockSpec((1, N_HEADS, HEAD_DIM, HEAD_DIM), lambda c: (c, 0, 0, 0)),
                   pl.BlockSpec((1, N_HEADS, CHUNK, CHUNK), lambda c: (c, 0, 0, 0)), blk(0)],
        out_shape=[jax.ShapeDtypeStruct((s, BR_WIDTH), F32), jax.ShapeDtypeStruct((n, N_HEADS, HEAD_DIM, HEAD_DIM), F32),
                   jax.ShapeDtypeStruct((n, N_HEADS, CHUNK, CHUNK), BF16), jax.ShapeDtypeStruct((s, BR_WIDTH), BF16)],
        scratch_shapes=[pltpu.VMEM((N_HEADS, HEAD_DIM, HEAD_DIM), F32)],
        compiler_params=_cp("arbitrary"),
    )(qkv, qkv, qkv, gcs, beta, proj, norm_w.reshape(1, HEAD_DIM))


def _dn_chunk_bwd(qkv, gcs, beta, states, tinvs, o, proj, norm_w, dy, *, name):
    s = qkv.shape[0]
    n = s // CHUNK

    def body(q_ref, k_ref, v_ref, g_ref, b_ref, st_in_ref, tinv_ref, o_ref, z_ref, w_ref, dy_ref,
             dqkv_ref, dg_ref, dbeta_ref, dz_ref, dw_ref, dst_ref):
        @pl.when(pl.program_id(0) == 0)
        def _():
            dst_ref[...] = jnp.zeros_like(dst_ref)
            dw_ref[...] = jnp.zeros_like(dw_ref)

        do, dz, dw = _head_norm_bwd(_heads_of(o_ref), _heads_of(z_ref), w_ref[...], _heads_of(dy_ref))
        dw_ref[...] += dw

        gblk, bblk = g_ref[...], b_ref[...]
        lane = _iota2(CHUNK, HEAD_DIM, 1)
        dg_all = jnp.zeros((CHUNK, HEAD_DIM), F32)
        dbeta_all = jnp.zeros((CHUNK, HEAD_DIM), F32)
        dq, dk, dv, dgc, dbeta, dst = _dn_chunk_bwd_math(
            _heads_of(q_ref), _heads_of(k_ref), _heads_of(v_ref), _lanes_of(gblk), _lanes_of(bblk),
            _Heads(st_in_ref[0, h] for h in range(N_HEADS)), do,
            _Heads(dst_ref[h] for h in range(N_HEADS)), _Heads(tinv_ref[0, h] for h in range(N_HEADS)))
        for h in range(N_HEADS):
            dz_ref[:, h * HEAD_DIM:(h + 1) * HEAD_DIM] = dz.v[h]
            for part, val in enumerate((dq, dk, dv)):
                c0 = part * BR_WIDTH + h * HEAD_DIM
                dqkv_ref[:, c0:c0 + HEAD_DIM] = val.v[h]
            dg_all = jnp.where(lane == h, dgc.v[h], dg_all)
            dbeta_all = jnp.where(lane == h, dbeta.v[h], dbeta_all)
            dst_ref[h] = dst.v[h]
        dg_ref[...] = dg_all
        dbeta_ref[...] = dbeta_all

    blk = lambda off: pl.BlockSpec((CHUNK, BR_WIDTH), lambda c: (n - 1 - c, off))
    sc = pl.BlockSpec((CHUNK, HEAD_DIM), lambda c: (n - 1 - c, 0))
    vec = pl.BlockSpec((1, HEAD_DIM), lambda c: (0, 0))
    outs = pl.pallas_call(
        body, name=name, grid=(n,),
        in_specs=[blk(0), blk(1), blk(2), sc, sc,
                  pl.BlockSpec((1, N_HEADS, HEAD_DIM, HEAD_DIM), lambda c: (n - 1 - c, 0, 0, 0)),
                  pl.BlockSpec((1, N_HEADS, CHUNK, CHUNK), lambda c: (n - 1 - c, 0, 0, 0)), blk(0), blk(C_Z // BR_WIDTH),
                  vec, blk(0)],
        out_specs=[pl.BlockSpec((CHUNK, 3 * BR_WIDTH), lambda c: (n - 1 - c, 0)), sc, sc, blk(0), vec],
        out_shape=[jax.ShapeDtypeStruct((s, 3 * BR_WIDTH), F32)] + [jax.ShapeDtypeStruct((s, HEAD_DIM), F32)] * 2
        + [jax.ShapeDtypeStruct((s, BR_WIDTH), BF16), jax.ShapeDtypeStruct((1, HEAD_DIM), F32)],
        scratch_shapes=[pltpu.VMEM((N_HEADS, HEAD_DIM, HEAD_DIM), F32)],
        compiler_params=_cp("arbitrary"),
    )(qkv, qkv, qkv, gcs, beta, states, tinvs, o, proj, norm_w.reshape(1, HEAD_DIM), dy)
    return outs


def _hg_chunk_common(q, k, g):
    c, nb = CHUNK, CHUNK // SUB
    e_g = _exp(g)
    qd = q * e_g
    g_last = g[c - 1:c, :]
    e_t = _exp(g_last - g)
    kt = k * e_t
    tail = _exp(g_last)
    g_refs = [g[i * SUB:i * SUB + 1, :] for i in range(nb)]
    g_ref_rows = _hmap(lambda *rows: jnp.concatenate([jnp.broadcast_to(r, (SUB, r.shape[1])) for r in rows], axis=0), *g_refs)
    e_q = _exp(g - g_ref_rows)
    q_sc = q * e_q
    e_k = [_hmap(lambda gr, g_: jnp.exp(jnp.minimum(gr - g_, EXP_CLAMP)), g_refs[i], g) for i in range(nb)]
    k_sc_all = _rows(*[k * e_k[i] for i in range(nb)])
    row_blk = _iota2(c, 1, 0) // SUB
    masks = [(row_blk == i).astype(F32) for i in range(nb)]
    r_all = _nt(q_sc, k_sc_all)
    a_mat = r_all[:, 0:c] * masks[0]
    for i in range(1, nb):
        a_mat = a_mat + r_all[:, i * c:(i + 1) * c] * masks[i]
    a_mat = a_mat * _tri(c, "lower")
    return dict(e_g=e_g, qd=qd, e_t=e_t, kt=kt, tail=tail, q_sc=q_sc, k_sc_all=k_sc_all, e_q=e_q, e_k=e_k, masks=masks,
                a_mat=a_mat)


def _hg_chunk_fwd_math(q, k, v, g, stt):
    m = _hg_chunk_common(q, k, g)
    o = _nt(m["qd"], stt) + _nn(m["a_mat"], v)
    stt2 = stt * m["tail"] + _tn(v, m["kt"])
    return o, stt2


def _hg_chunk_bwd_math(q, k, v, g, stt, do, dstt2):
    c, nb = CHUNK, CHUNK // SUB
    m = _hg_chunk_common(q, k, g)
    stt2 = stt * m["tail"] + _tn(v, m["kt"])
    later = _sum(stt2 * dstt2, 0)
    dqd = _dot3(do, stt, 1, 0)
    dstt = _tn(do, m["qd"]) + dstt2 * m["tail"]
    d_a = _dot3(do, v, 1, 1) * _tri(c, "lower")
    dv = _tn(m["a_mat"], do) + _nt(m["kt"], dstt2)
    dkt = _dot3(v, dstt2, 1, 0)
    d_blk = _lanes(*[d_a * m["masks"][i] for i in range(nb)])
    dq = dqd * m["e_g"] + _dot3(d_blk, m["k_sc_all"], 1, 0) * m["e_q"]
    dks = _dot3(d_blk, m["q_sc"], 0, 0)
    dk = dkt * m["e_t"]
    for i in range(nb):
        dk = dk + dks[i * c:(i + 1) * c, :] * m["e_k"][i]
    db = q * dq - k * dk
    return dq, dk, dv, db, later, dstt


def _hg_chunk_fwd(proj, lb, norm_w, *, name):
    s = proj.shape[0]
    n = s // CHUNK

    def body(hq_ref, hf_ref, v_ref, lb_ref, z_ref, w_ref, o_ref, st_out_ref, q_out, k_out, lf_out, y_ref, st_ref):
        @pl.when(pl.program_id(0) == 0)
        def _():
            st_ref[...] = jnp.zeros_like(st_ref)

        f, lbv = hf_ref[...], lb_ref[...]
        q_all = _silu(hq_ref[...])
        k_all = (1.0 - lbv) * _sigmoid(-f)
        lf_all = jnp.log(lbv + (1.0 - lbv) * _sigmoid(f))
        q_out[...], k_out[...], lf_out[...] = q_all, k_all, lf_all
        st = _Heads(st_ref[h] for h in range(N_HEADS))
        g_all = _nn_exact(_tri(CHUNK, "lower"), lf_all)
        o, st2 = _hg_chunk_fwd_math(_heads_of(q_all), _heads_of(k_all), _heads_of(v_ref), _heads_of(g_all), st)
        y = _head_norm_fwd(o, _heads_of(z_ref), w_ref[...])
        for h in range(N_HEADS):
            st_out_ref[0, h] = st.v[h]
            o_ref[:, h * HEAD_DIM:(h + 1) * HEAD_DIM] = o.v[h]
            y_ref[:, h * HEAD_DIM:(h + 1) * HEAD_DIM] = y.v[h]
            st_ref[h] = st2.v[h]

    blk = lambda off: pl.BlockSpec((CHUNK, BR_WIDTH), lambda c: (c, off))
    return pl.pallas_call(
        body, name=name, grid=(n,),
        in_specs=[blk(C_HQ // BR_WIDTH), blk(C_HF // BR_WIDTH), blk(C_HI // BR_WIDTH), pl.BlockSpec((1, BR_WIDTH), lambda c: (0, 0)),
                  blk(C_HZ // BR_WIDTH), pl.BlockSpec((1, HEAD_DIM), lambda c: (0, 0))],
        out_specs=[blk(0), pl.BlockSpec((1, N_HEADS, HEAD_DIM, HEAD_DIM), lambda c: (c, 0, 0, 0)), blk(0), blk(0), blk(0), blk(0)],
        out_shape=[jax.ShapeDtypeStruct((s, BR_WIDTH), F32), jax.ShapeDtypeStruct((n, N_HEADS, HEAD_DIM, HEAD_DIM), F32)]
        + [jax.ShapeDtypeStruct((s, BR_WIDTH), F32)] * 3 + [jax.ShapeDtypeStruct((s, BR_WIDTH), BF16)],
        scratch_shapes=[pltpu.VMEM((N_HEADS, HEAD_DIM, HEAD_DIM), F32)],
        compiler_params=_cp("arbitrary"),
    )(proj, proj, proj, lb, proj, norm_w.reshape(1, HEAD_DIM))


def _hg_chunk_bwd(proj, lb, qh, kh, lf, states, o, norm_w, dy, *, name):
    s = proj.shape[0]
    n = s // CHUNK

    def body(hq_ref, hf_ref, v_ref, lb_ref, q_ref, k_ref, lf_ref, st_in_ref, o_ref, z_ref, w_ref, dy_ref,
             dhq_ref, dhf_ref, dhi_ref, dz_ref, dlb_ref, dw_ref, dst_ref):
        @pl.when(pl.program_id(0) == 0)
        def _():
            dst_ref[...] = jnp.zeros_like(dst_ref)
            dlb_ref[...] = jnp.zeros_like(dlb_ref)
            dw_ref[...] = jnp.zeros_like(dw_ref)

        do, dz, dw = _head_norm_bwd(_heads_of(o_ref), _heads_of(z_ref), w_ref[...], _heads_of(dy_ref))
        dw_ref[...] += dw

        g_all = _nn_exact(_tri(CHUNK, "lower"), lf_ref[...])
        dq, dk, dv, db, later, dst = _hg_chunk_bwd_math(
            _heads_of(q_ref), _heads_of(k_ref), _heads_of(v_ref), _heads_of(g_all),
            _Heads(st_in_ref[0, h] for h in range(N_HEADS)), do,
            _Heads(dst_ref[h] for h in range(N_HEADS)))
        dlf = _nn_exact(_tri(CHUNK, "upper"), jnp.concatenate(db.v, axis=1)) + jnp.concatenate(later.v, axis=1)
        dq_all, dk_all = jnp.concatenate(dq.v, axis=1), jnp.concatenate(dk.v, axis=1)
        f, lbv = hf_ref[...], lb_ref[...]
        dhq_ref[...] = (dq_all * _dsilu(hq_ref[...])).astype(BF16)
        sp, sn = _sigmoid(f), _sigmoid(-f)
        dlf_over = dlf / (lbv + (1.0 - lbv) * sp)
        dhf_ref[...] = (dlf_over * (1.0 - lbv) * sp * sn - dk_all * (1.0 - lbv) * sn * (1.0 - sn)).astype(BF16)
        dlb_ref[...] += jnp.sum(dlf_over * (1.0 - sp) - dk_all * sn, axis=0, keepdims=True)
        for h in range(N_HEADS):
            dhi_ref[:, h * HEAD_DIM:(h + 1) * HEAD_DIM] = dv.v[h].astype(BF16)
            dz_ref[:, h * HEAD_DIM:(h + 1) * HEAD_DIM] = dz.v[h]
            dst_ref[h] = dst.v[h]

    blk = lambda off: pl.BlockSpec((CHUNK, BR_WIDTH), lambda c: (n - 1 - c, off))
    vec = pl.BlockSpec((1, BR_WIDTH), lambda c: (0, 0))
    wvec = pl.BlockSpec((1, HEAD_DIM), lambda c: (0, 0))
    return pl.pallas_call(
        body, name=name, grid=(n,),
        in_specs=[blk(C_HQ // BR_WIDTH), blk(C_HF // BR_WIDTH), blk(C_HI // BR_WIDTH), vec, blk(0), blk(0), blk(0),
                  pl.BlockSpec((1, N_HEADS, HEAD_DIM, HEAD_DIM), lambda c: (n - 1 - c, 0, 0, 0)), blk(0), blk(C_HZ // BR_WIDTH),
                  wvec, blk(1)],
        out_specs=[blk(0), blk(0), blk(0), blk(0), vec, wvec],
        out_shape=[jax.ShapeDtypeStruct((s, BR_WIDTH), BF16)] * 4 + [jax.ShapeDtypeStruct((1, BR_WIDTH), F32),
                                                                    jax.ShapeDtypeStruct((1, HEAD_DIM), F32)],
        scratch_shapes=[pltpu.VMEM((N_HEADS, HEAD_DIM, HEAD_DIM), F32)],
        compiler_params=_cp("arbitrary"),
    )(proj, proj, proj, lb, qh, kh, lf, states, o, proj, norm_w.reshape(1, HEAD_DIM), dy)


_ANY = pl.BlockSpec(memory_space=pl.ANY)
_MESH = pl.DeviceIdType.MESH


def _all_gather(x_local, *, name, after=(), norm=None):
    n_after = len(after)
    n_norm = 0 if norm is None else 2

    def norm_rows(h_ref, w_ref, hn_ref, hbuf, obuf, in_sems, out_sems):
        n_tiles = h_ref.shape[0] // ROW_TILE

        def load(t):
            return pltpu.make_async_copy(h_ref.at[pl.ds(t * ROW_TILE, ROW_TILE)], hbuf.at[t % 2], in_sems.at[t % 2])

        def store(t):
            return pltpu.make_async_copy(obuf.at[t % 2], hn_ref.at[pl.ds(t * ROW_TILE, ROW_TILE)], out_sems.at[t % 2])

        load(0).start()
        for t in range(n_tiles):
            if t + 1 < n_tiles:
                load(t + 1).start()
            load(t).wait()
            if t >= 2:
                store(t - 2).wait()
            xv = hbuf[t % 2]
            r = lax.rsqrt(jnp.mean(xv * xv, axis=-1, keepdims=True) + NORM_EPS)
            obuf[t % 2] = (xv * r * w_ref[...]).astype(BF16)
            store(t).start()
        for t in range(max(n_tiles - 2, 0), n_tiles):
            store(t).wait()

    def body(x_ref, *refs):
        if norm is None:
            out_ref, send_sems, recv_sems, local_sem = refs[n_after:]
        else:
            h_ref, w_ref = refs[n_after:n_after + 2]
            out_ref, hn_ref, send_sems, recv_sems, local_sem, hbuf, obuf, in_sems, out_sems = refs[n_after + 2:]
        x, y, c = lax.axis_index("x"), lax.axis_index("y"), lax.axis_index("c")
        me, sibling = (x, y, c), (x, y, 1 - c)
        n1 = (x ^ (1 - c), y ^ c)
        n2 = (x ^ c, y ^ (1 - c))
        dg = (1 - x, 1 - y)

        def slot(px, py, pc):
            return out_ref.at[4 * px + 2 * py + pc]

        def copy(k, block, to, src=None):
            return pltpu.make_async_remote_copy(
                src_ref=slot(*block) if src is None else src, dst_ref=slot(*block),
                send_sem=send_sems.at[k], recv_sem=recv_sems.at[k], device_id=to, device_id_type=_MESH)

        mine = pltpu.make_async_copy(x_ref, slot(*me), local_sem)
        mine.start()
        first = [copy(0, me, sibling, src=x_ref), copy(1, me, (*n1, c), src=x_ref), copy(2, me, (*n2, c), src=x_ref)]
        for cp in first:
            cp.start()
        if norm is not None:
            norm_rows(h_ref, w_ref, hn_ref, hbuf, obuf, in_sems, out_sems)
        copy(2, (*n2, c), me).wait_recv()
        forward = copy(3, (*n2, c), (*n1, c))
        forward.start()
        passed = [copy(5, (*n2, c), sibling)]
        passed[0].start()
        copy(1, (*n1, c), me).wait_recv()
        passed.append(copy(4, (*n1, c), sibling))
        passed[1].start()
        copy(3, (*dg, c), me).wait_recv()
        passed.append(copy(6, (*dg, c), sibling))
        passed[2].start()
        copy(0, sibling, me).wait_recv()
        copy(4, (*n2, 1 - c), me).wait_recv()
        copy(5, (*n1, 1 - c), me).wait_recv()
        copy(6, (*dg, 1 - c), me).wait_recv()
        for cp in first + [forward] + passed:
            cp.wait_send()
        mine.wait()

    gathered = jax.ShapeDtypeStruct((N_DEV,) + x_local.shape, x_local.dtype)
    sems = [pltpu.SemaphoreType.DMA((7,)), pltpu.SemaphoreType.DMA((7,)), pltpu.SemaphoreType.DMA]
    if norm is None:
        return pl.pallas_call(body, name=name, out_shape=gathered, in_specs=[_ANY] * (1 + n_after), out_specs=_ANY,
                              scratch_shapes=sems)(x_local, *after)
    h, w = norm
    d = h.shape[1]
    return pl.pallas_call(
        body, name=name, out_shape=[gathered, jax.ShapeDtypeStruct(h.shape, BF16)],
        in_specs=[_ANY] * (2 + n_after) + [pl.BlockSpec(memory_space=pltpu.VMEM)], out_specs=[_ANY, _ANY],
        scratch_shapes=sems + [pltpu.VMEM((2, ROW_TILE, d), F32), pltpu.VMEM((2, ROW_TILE, d), BF16),
                               pltpu.SemaphoreType.DMA((2,)), pltpu.SemaphoreType.DMA((2,))],
    )(x_local, *after, h, w.reshape(1, d))


_HBM = pl.BlockSpec(memory_space=pltpu.HBM)
_SEM = pl.BlockSpec(memory_space=pltpu.SEMAPHORE)
_EFFECT = pltpu.SideEffectType.DATAFLOW_SIDE_EFFECTING


def _peers():
    x, y, c = lax.axis_index("x"), lax.axis_index("y"), lax.axis_index("c")
    out = []
    for k in range(1, N_DEV):
        px, py, pc = x ^ ((k >> 2) & 1), y ^ ((k >> 1) & 1), c ^ (k & 1)
        out.append(((px, py, pc), 4 * px + 2 * py + pc))
    return 4 * x + 2 * y + c, out


def _push_copies(src_ref, land_ref, send_sems, recv_sems, broadcast):
    my, peers = _peers()
    pairs = []
    for k, (pos, idx) in enumerate(peers):
        src = src_ref if broadcast else src_ref.at[idx]
        send = pltpu.make_async_remote_copy(src_ref=src, dst_ref=land_ref.at[my], send_sem=send_sems.at[k],
                                            recv_sem=recv_sems.at[k], device_id=pos, device_id_type=_MESH)
        recv = pltpu.make_async_remote_copy(src_ref=src, dst_ref=land_ref.at[idx], send_sem=send_sems.at[k],
                                            recv_sem=recv_sems.at[k], device_id=pos, device_id_type=_MESH)
        pairs.append((send, recv))
    return pairs


def _push_start(src, land, *, broadcast, name, after=()):
    n_after = len(after)

    def body(src_ref, land_ref, *refs):
        send_sems, recv_sems, _, _, token = refs[n_after:]
        for send, _ in _push_copies(src_ref, land_ref, send_sems, recv_sems, broadcast):
            send.start()
        token[...] = jnp.zeros_like(token)

    return pl.pallas_call(
        body, name=name,
        out_shape=(pltpu.SemaphoreType.DMA((N_DEV - 1,)), pltpu.SemaphoreType.DMA((N_DEV - 1,)),
                   pltpu.HBM(src.shape, src.dtype), pltpu.HBM(land.shape, land.dtype), jax.ShapeDtypeStruct((8, 128), F32)),
        in_specs=(_HBM, _HBM) + (_ANY,) * n_after, out_specs=(_SEM, _SEM, _HBM, _HBM, pl.BlockSpec(memory_space=pltpu.VMEM)),
        input_output_aliases={0: 2, 1: 3}, compiler_params=pltpu.CompilerParams(has_side_effects=_EFFECT),
    )(pltpu.with_memory_space_constraint(src, pltpu.HBM), pltpu.with_memory_space_constraint(land, pltpu.HBM), *after)


def _push_wait(handle, after, *, broadcast, name):
    send_sems, recv_sems, src_thru, land_thru, _ = handle

    def body(src_ref, land_ref, send_sems, recv_sems, *rest):
        for send, recv in _push_copies(src_ref, land_ref, send_sems, recv_sems, broadcast):
            send.wait_send()
            recv.wait_recv()

    return pl.pallas_call(
        body, name=name,
        out_shape=(pltpu.HBM(src_thru.shape, src_thru.dtype), pltpu.HBM(land_thru.shape, land_thru.dtype)),
        in_specs=(_HBM, _HBM, _SEM, _SEM) + (_ANY,) * len(after), out_specs=(_HBM, _HBM),
        input_output_aliases={0: 0, 1: 1}, compiler_params=pltpu.CompilerParams(has_side_effects=_EFFECT),
    )(src_thru, land_thru, send_sems, recv_sems, *after)[1]


def _relay_copies(src_ref, land_ref, sems_a, sems_b):
    x, y, c = lax.axis_index("x"), lax.axis_index("y"), lax.axis_index("c")
    slot = lambda px, py, pc: land_ref.at[4 * px + 2 * py + pc]
    chips = [(1 - x, y), (x, 1 - y), (1 - x, 1 - y)]
    (send_a, recv_a), (send_b, recv_b) = sems_a, sems_b

    def copy(sems, k, src, dst_slot, to):
        return pltpu.make_async_remote_copy(src_ref=src, dst_ref=dst_slot, send_sem=sems[0].at[k], recv_sem=sems[1].at[k],
                                            device_id=to, device_id_type=_MESH)

    first = [copy((send_a, recv_a), 0, src_ref, slot(x, y, c), (x, y, 1 - c))]
    first += [copy((send_a, recv_a), 1 + j, src_ref, slot(x, y, c), (*chip, c)) for j, chip in enumerate(chips)]
    first_in = [copy((send_a, recv_a), 0, src_ref, slot(x, y, 1 - c), (x, y, 1 - c))]
    first_in += [copy((send_a, recv_a), 1 + j, src_ref, slot(*chip, c), (*chip, c)) for j, chip in enumerate(chips)]
    relay = [copy((send_b, recv_b), j, slot(*chip, c), slot(*chip, c), (x, y, 1 - c)) for j, chip in enumerate(chips)]
    relay_in = [copy((send_b, recv_b), j, slot(*chip, 1 - c), slot(*chip, 1 - c), (x, y, 1 - c)) for j, chip in enumerate(chips)]
    return first, first_in, relay, relay_in


def _relay_start(src, land, *, name, after=()):
    n_after = len(after)

    def body(src_ref, land_ref, *refs):
        send_a, recv_a, _, _, token = refs[n_after:]
        for cp in _relay_copies(src_ref, land_ref, (send_a, recv_a), (send_a, recv_a))[0]:
            cp.start()
        token[...] = jnp.zeros_like(token)

    send_a, recv_a, src_thru, land_thru, token = pl.pallas_call(
        body, name=name,
        out_shape=(pltpu.SemaphoreType.DMA((4,)), pltpu.SemaphoreType.DMA((4,)), pltpu.HBM(src.shape, src.dtype),
                   pltpu.HBM(land.shape, land.dtype), jax.ShapeDtypeStruct((8, 128), F32)),
        in_specs=(_HBM, _HBM) + (_ANY,) * n_after, out_specs=(_SEM, _SEM, _HBM, _HBM, pl.BlockSpec(memory_space=pltpu.VMEM)),
        input_output_aliases={0: 2, 1: 3}, compiler_params=pltpu.CompilerParams(has_side_effects=_EFFECT),
    )(pltpu.with_memory_space_constraint(src, pltpu.HBM), pltpu.with_memory_space_constraint(land, pltpu.HBM), *after)
    return (send_a, recv_a), src_thru, land_thru, token


def _relay_mid(handle, after, *, name):
    sems_a, src_thru, land_thru, _ = handle
    n_after = len(after)

    def body(src_ref, land_ref, send_a, recv_a, *refs):
        send_b, recv_b, _, _, token = refs[n_after:]
        _, first_in, relay, _ = _relay_copies(src_ref, land_ref, (send_a, recv_a), (send_b, recv_b))
        for j in range(3):
            first_in[1 + j].wait_recv()
            relay[j].start()
        token[...] = jnp.zeros_like(token)

    send_b, recv_b, src2, land2, token = pl.pallas_call(
        body, name=name,
        out_shape=(pltpu.SemaphoreType.DMA((3,)), pltpu.SemaphoreType.DMA((3,)), pltpu.HBM(src_thru.shape, src_thru.dtype),
                   pltpu.HBM(land_thru.shape, land_thru.dtype), jax.ShapeDtypeStruct((8, 128), F32)),
        in_specs=(_HBM, _HBM, _SEM, _SEM) + (_ANY,) * n_after,
        out_specs=(_SEM, _SEM, _HBM, _HBM, pl.BlockSpec(memory_space=pltpu.VMEM)),
        input_output_aliases={0: 2, 1: 3}, compiler_params=pltpu.CompilerParams(has_side_effects=_EFFECT),
    )(src_thru, land_thru, *sems_a, *after)
    return sems_a, (send_b, recv_b), src2, land2, token


def _relay_wait(handle, after, *, name):
    sems_a, sems_b, src_thru, land_thru, _ = handle

    def body(src_ref, land_ref, send_a, recv_a, send_b, recv_b, *rest):
        first, first_in, relay, relay_in = _relay_copies(src_ref, land_ref, (send_a, recv_a), (send_b, recv_b))
        first_in[0].wait_recv()
        for cp in relay_in:
            cp.wait_recv()
        for cp in first + relay:
            cp.wait_send()

    return pl.pallas_call(
        body, name=name,
        out_shape=(pltpu.HBM(src_thru.shape, src_thru.dtype), pltpu.HBM(land_thru.shape, land_thru.dtype)),
        in_specs=(_HBM, _HBM, _SEM, _SEM, _SEM, _SEM) + (_ANY,) * len(after), out_specs=(_HBM, _HBM),
        input_output_aliases={0: 0, 1: 1}, compiler_params=pltpu.CompilerParams(has_side_effects=_EFFECT),
    )(src_thru, land_thru, *sems_a, *sems_b, *after)[1]


def _adamw(parts, row_off, w, m, v, *, layer=0, n_layers=1, prev=None, name, tr):
    rows, c = w.shape
    r = rows // n_layers
    np_ = parts.shape[0]
    tr = min(tr, r)
    assert r % tr == 0 and row_off % tr == 0
    ob, lb = row_off // tr, layer * (r // tr)
    c1 = 1.0 - ADAM_B1 ** ADAM_STEP
    c2 = 1.0 - ADAM_B2 ** ADAM_STEP
    n_prev = 0 if prev is None else 4

    def body(p_ref, w_ref, m_ref, v_ref, *refs):
        g_ref, d_ref, nm_ref, nv_ref = refs[n_prev:]
        g = p_ref[0].astype(F32)
        for s in range(1, np_):
            g = g + p_ref[s].astype(F32)
        wv = w_ref[...]
        m2 = ADAM_B1 * m_ref[...] + (1.0 - ADAM_B1) * g
        v2 = ADAM_B2 * v_ref[...] + (1.0 - ADAM_B2) * jnp.square(g)
        m_hat = m2 / c1
        v_hat = v2 / c2
        g_ref[...] = g
        d_ref[...] = -ADAM_LR * (m_hat / (jnp.sqrt(v_hat) + ADAM_EPS) + ADAM_WD * wv)
        nm_ref[...] = m2
        nv_ref[...] = v2

    blk = pl.BlockSpec((tr, c), lambda i: (lb + i, 0))
    return pl.pallas_call(
        body, name=name, grid=(r // tr,),
        in_specs=[pl.BlockSpec((np_, tr, c), lambda i: (0, ob + i, 0)), blk, blk, blk] + [_ANY] * n_prev,
        out_specs=[blk] * 4, out_shape=[jax.ShapeDtypeStruct((rows, c), F32)] * 4,
        input_output_aliases={4 + i: i for i in range(n_prev)}, compiler_params=_cp("parallel"),
    )(parts, w, m, v, *(prev or ()))


def _sum_parts(parts, *, name, after=()):
    np_, r, c = parts.shape

    def body(p_ref, *refs):
        o_ref = refs[-1]
        g = p_ref[0]
        for s in range(1, np_):
            g = g + p_ref[s]
        o_ref[...] = g

    vmem = pl.BlockSpec(memory_space=pltpu.VMEM)
    return pl.pallas_call(body, name=name, in_specs=[vmem] + [_ANY] * len(after), out_specs=vmem,
                          out_shape=jax.ShapeDtypeStruct((r, c), F32))(parts, *after)


def _pack(arrs):
    rows = []
    for a in arrs:
        f = a.reshape(-1).astype(F32)
        pad = (-f.shape[0]) % 128
        rows.append(jnp.pad(f, (0, pad)).reshape(-1, 128))
    out = jnp.concatenate(rows, axis=0)
    return jnp.pad(out, ((0, (-out.shape[0]) % 8), (0, 0)))


def _unpack(packed, shapes):
    outs, r0 = [], 0
    for shp in shapes:
        n = 1
        for d in shp:
            n *= d
        nr = -(-n // 128)
        outs.append(packed[r0:r0 + nr].reshape(-1)[:n].reshape(shp))
        r0 += nr
    return outs


_WIN_PIECES = ((0, 4096, 0), (4112, 8208, 0), (4096, 4104, HEAD_DIM - N_HEADS), (4104, 4112, HEAD_DIM - N_HEADS))


RELAYOUT_TILE = 256
LAST_SPLIT = 4
OTHER_SPLIT = 2


def _win_from_shards(shards, *, name):
    k = shards.shape[1]
    tr = min(RELAYOUT_TILE, k)

    def body(x_ref, o_ref):
        cols = []
        for lo, hi, pad in _WIN_PIECES:
            for j in range(N_DEV):
                a, b = max(lo, j * SHARD_IN), min(hi, (j + 1) * SHARD_IN)
                if a < b:
                    cols.append(x_ref[j, :, a - j * SHARD_IN:b - j * SHARD_IN])
            if pad:
                cols.append(jnp.zeros((tr, pad), x_ref.dtype))
        o_ref[...] = jnp.concatenate(cols, axis=1)

    return pl.pallas_call(
        body, name=name, grid=(k // tr,), in_specs=[pl.BlockSpec((N_DEV, tr, SHARD_IN), lambda i: (0, i, 0))],
        out_specs=pl.BlockSpec((tr, N_PROJ), lambda i: (i, 0)), out_shape=jax.ShapeDtypeStruct((k, N_PROJ), shards.dtype),
        compiler_params=_cp("parallel"),
    )(shards)


def _win_to_shards(g, *, name):
    k = g.shape[0]
    tr = min(RELAYOUT_TILE, k)
    starts, off = [], 0
    for lo, hi, pad in _WIN_PIECES:
        starts.append((lo, hi, off))
        off += hi - lo + pad

    def body(g_ref, o_ref):
        for j in range(N_DEV):
            cols = []
            for lo, hi, off in sorted(starts):
                a, b = max(lo, j * SHARD_IN), min(hi, (j + 1) * SHARD_IN)
                if a < b:
                    cols.append(g_ref[:, off + a - lo:off + b - lo])
            o_ref[j] = jnp.concatenate(cols, axis=1)

    return pl.pallas_call(
        body, name=name, grid=(k // tr,), in_specs=[pl.BlockSpec((tr, N_PROJ), lambda i: (i, 0))],
        out_specs=pl.BlockSpec((N_DEV, tr, SHARD_IN), lambda i: (0, i, 0)),
        out_shape=jax.ShapeDtypeStruct((N_DEV, k, SHARD_IN), g.dtype), compiler_params=_cp("parallel"),
    )(g)


def _lower_bounds(logits):
    probs = jax.nn.softmax(logits.astype(F32), axis=0)
    return jnp.cumsum(probs, axis=0) - probs[0]


def _pad_lanes(vec8):
    return jnp.pad(vec8.reshape(1, N_HEADS), ((0, 0), (0, HEAD_DIM - N_HEADS)))


def kernel(x, p, norm_w, w_in, dn_conv_w, dn_A_log, dn_dt_bias, dn_norm_w, hg_lb_logits, hg_norm_w, w_out, w_ple_up, w_ple_gate, final_norm_w, loss_target, m_norm_w, m_w_in, m_dn_conv_w, m_dn_A_log, m_dn_dt_bias, m_dn_norm_w, m_hg_lb_logits, m_hg_norm_w, m_w_out, m_w_ple_up, m_w_ple_gate, m_final_norm_w, v_norm_w, v_w_in, v_dn_conv_w, v_dn_A_log, v_dn_dt_bias, v_dn_norm_w, v_hg_lb_logits, v_hg_norm_w, v_w_out, v_w_ple_up, v_w_ple_gate, v_final_norm_w):
    depth = norm_w.shape[0]
    my = 4 * lax.axis_index("x") + 2 * lax.axis_index("y") + lax.axis_index("c")
    h = x[0]
    tgt = loss_target[0]
    rows_out = D_MODEL // N_DEV
    up_rows = PLE_DIM * (D_MODEL // N_DEV) // D_MODEL
    g_off, u_off = rows_out, 2 * rows_out

    def own_slot(block):
        return lax.dynamic_update_index_in_dim(lax.empty((N_DEV,) + block.shape, block.dtype), block, my, 0)

    win_bf = w_in.astype(BF16)
    rest_bf = [jnp.concatenate([w_out[l], w_ple_gate[l], w_ple_up[l].reshape(up_rows, D_MODEL)], axis=0).astype(BF16)
               for l in range(depth)]
    conv_push = _push_start(dn_conv_w, own_slot(dn_conv_w), broadcast=True, name="gather_conv_w_start")
    win_l0, hn_l0 = _all_gather(win_bf[0], name="gather_w_in_l0", after=[conv_push[4]], norm=(h, norm_w[0]))
    win_all = {0: win_l0}
    pending, relayed = {}, {}
    last = win_all[0]
    for l in range(depth):
        if l > 0:
            relayed["win", l] = _relay_start(win_bf[l], own_slot(win_bf[l]), after=[last], name=f"gather_w_in_l{l}_first")
            last = relayed["win", l][3]
        if l == 0:
            relayed["rest", l] = _relay_start(rest_bf[l], own_slot(rest_bf[l]), after=[last], name=f"gather_rest_l{l}_first")
            last = relayed["rest", l][3]
        else:
            pending["rest", l] = _push_start(rest_bf[l], own_slot(rest_bf[l]), broadcast=True, after=[last],
                                             name=f"gather_rest_l{l}_start")
            last = pending["rest", l][4]
    lbs = _lower_bounds(hg_lb_logits)

    saved = []
    weights = []
    for l in range(depth):
        tag = f"l{l}"
        if l > 0:
            win_all[l] = _relay_wait(relayed["win", l], [h], name=f"gather_w_in_{tag}_wait")
        wi = _win_from_shards(win_all[l], name=f"w_in_layout_{tag}")
        hn = hn_l0 if l == 0 else _rms_fwd(h, norm_w[l], name=f"rms_fwd_{tag}")
        proj = _mm(hn, wi, mode="nn", out_dtype=F32, after=[last] if l == 0 else (), name=f"mm_proj_{tag}")
        al, dt = _pad_lanes(dn_A_log[l]), _pad_lanes(dn_dt_bias[l])
        if l == 0:
            conv_all = _push_wait(conv_push, [proj], broadcast=True, name="gather_conv_w_wait")
            conv_full = conv_all.transpose(1, 2, 0, 3).reshape(depth, CONV_W, 3 * BR_WIDTH)
        qkv = _dn_qkv_fwd(proj, conv_full[l], name=f"dn_qkv_fwd_{tag}")
        if ("rest", l) in relayed:
            relayed["rest", l] = _relay_mid(relayed["rest", l], [qkv], name=f"gather_rest_{tag}_relay")
            al = al + relayed["rest", l][4][0, 0]
        beta, gcs = _dn_gate_fwd(proj, al, dt, name=f"dn_gate_fwd_{tag}")
        o_dn, st_dn, tinv_dn, y_dn = _dn_chunk_fwd(qkv, gcs, beta, proj, dn_norm_w[l], name=f"dn_chunk_fwd_{tag}")
        lb = lbs[l].reshape(1, BR_WIDTH)
        o_hg, st_hg, qh, kh, lf, y_hg = _hg_chunk_fwd(proj, lb, hg_norm_w[l], name=f"hg_chunk_fwd_{tag}")
        y = jnp.concatenate([y_dn, y_hg], axis=1)
        if ("rest", l) in relayed:
            rest_all = _relay_wait(relayed["rest", l], [y], name=f"gather_rest_{tag}_wait")
        else:
            rest_all = _push_wait(pending["rest", l], [y], broadcast=True, name=f"gather_rest_{tag}_wait")
        w_out_rows, w_gate_rows = (0, rows_out), (g_off, rows_out)
        wu = rest_all[:, u_off:u_off + up_rows].reshape(N_DEV, PLE_DIM, D_MODEL // N_DEV).transpose(1, 0, 2).reshape(PLE_DIM, D_MODEL)
        weights.append((wi, rest_all, wu))
        h1 = _mm(y, rest_all, mode="nn", b_rows=w_out_rows, out_dtype=F32, res=h, name=f"mm_out_{tag}")
        pin = []
        if ("win", l + 1) in relayed:
            relayed["win", l + 1] = _relay_mid(relayed["win", l + 1], [h1], name=f"gather_w_in_l{l + 1}_relay")
            pin = [relayed["win", l + 1][4]]
        up = _mm(p[l, 0], wu, mode="nn", out_dtype=F32, name=f"mm_up_{tag}")
        gp, h2 = _mm(h1, rest_all, mode="nn", b_rows=w_gate_rows, out_dtype=F32, after=pin, tile_n=512, fused=("ple", h1, up),
                     name=f"mm_gate_{tag}")
        saved.append(dict(h=h, hn=hn, proj=proj, qkv=qkv, beta=beta, gcs=gcs, st_dn=st_dn, tinv_dn=tinv_dn, qh=qh, kh=kh, lf=lf,
                          st_hg=st_hg, o_dn=o_dn, o_hg=o_hg, y=y, h1=h1, gp=gp, up=up, al=al, dt=dt, lb=lb))
        h = h2

    loss_row, dh, d_final_w = _final_fwd_bwd(h, final_norm_w, tgt, name="final_norm_loss")

    d_norm_w, d_alog, d_dt, d_dn_nw, d_hg_nw, d_lb, d_conv = ([None] * depth for _ in range(7))
    sent = {}
    for l in reversed(range(depth)):
        wi, rest_all, wu = weights[l]
        sv = saved[l]
        tag = f"l{l}"
        dup, dgp = _ple_bwd(dh, sv["gp"], sv["up"], name=f"ple_bwd_{tag}")
        d_wu = _mm(p[l, 0], dup, mode="tn", out_dtype=BF16, name=f"mm_dwup_{tag}")
        d_wg = _mm(sv["h1"], dgp, mode="tn", out_dtype=BF16, name=f"mm_dwgate_{tag}")
        dh1 = _mm(dgp, rest_all, mode="nt", b_rows=(g_off, rows_out), out_dtype=F32, res=dh, name=f"mm_dh1_{tag}")
        d_wo = _mm(sv["y"], dh1, mode="tn", out_dtype=BF16, name=f"mm_dwout_{tag}")
        parts_rest = jnp.concatenate(
            [d_wo.reshape(N_DEV, rows_out, D_MODEL), d_wg.reshape(N_DEV, rows_out, D_MODEL),
             d_wu.reshape(PLE_DIM, N_DEV, D_MODEL // N_DEV).transpose(1, 0, 2).reshape(N_DEV, up_rows, D_MODEL)], axis=1)
        sent["rest", l] = _push_start(parts_rest, own_slot(parts_rest[my]), broadcast=False, name=f"exchange_rest_{tag}_start")
        dy = _mm(dh1, rest_all, mode="nt", b_rows=(0, rows_out), out_dtype=F32, name=f"mm_dy_{tag}")
        dn_nw = dn_norm_w[l] + sent["rest", l][4][0, 0]
        dqkv, d_gc, dbeta, dz_dn, d_dn_nw[l] = _dn_chunk_bwd(sv["qkv"], sv["gcs"], sv["beta"], sv["st_dn"], sv["tinv_dn"],
                                                             sv["o_dn"], sv["proj"], dn_nw, dy, name=f"dn_chunk_bwd_{tag}")
        dqkv_pre, d_conv[l] = _dn_qkv_bwd(sv["proj"], conv_full[l], dqkv, name=f"dn_qkv_bwd_{tag}")
        db, da, d_alog[l], d_dt[l] = _dn_gate_bwd(sv["proj"], sv["al"], sv["dt"], dbeta, d_gc, name=f"dn_gate_bwd_{tag}")
        dhq, dhf, dhi, dz_hg, d_lb[l], d_hg_nw[l] = _hg_chunk_bwd(sv["proj"], sv["lb"], sv["qh"], sv["kh"], sv["lf"], sv["st_hg"],
                                                                  sv["o_hg"], hg_norm_w[l], dy, name=f"hg_chunk_bwd_{tag}")
        dproj = jnp.concatenate([dqkv_pre, dz_dn, dhq, dhf, dhi, dz_hg, db, da], axis=1)
        def push_d_win(after):
            n_split = LAST_SPLIT if l == 0 else OTHER_SPLIT
            rows = D_MODEL // n_split
            handles = []
            for q in range(n_split):
                hn_q = sv["hn"] if n_split == 1 else sv["hn"][:, q * rows:(q + 1) * rows]
                sfx = tag if n_split == 1 else f"{tag}_{q}"
                d_win = _mm(hn_q, dproj, mode="tn", out_dtype=BF16, after=after, name=f"mm_dwin_{sfx}")
                parts_in = _win_to_shards(d_win, name=f"dw_in_shards_{sfx}")
                handles.append(_push_start(parts_in, own_slot(parts_in[my]), broadcast=False, after=after,
                                           name=f"exchange_w_in_{sfx}_start"))
                after = [handles[-1][4]]
            return handles

        if l == 0:
            small = _pack([loss_row, jnp.concatenate(d_norm_w[1:], axis=0), d_final_w,
                           jnp.stack([a[0, :N_HEADS] for a in d_alog]), jnp.stack([a[0, :N_HEADS] for a in d_dt]),
                           jnp.concatenate(d_dn_nw, axis=0), jnp.concatenate(d_hg_nw, axis=0), jnp.concatenate(d_lb, axis=0),
                           jnp.stack(d_conv)])
            small_push = _push_start(small, own_slot(small), broadcast=True, name="gather_small_start")
        sent["win", l] = push_d_win([small_push[4]] if l == 0 else [])
        dh, d_norm_w[l] = _mm(dproj, wi, mode="nt", out_dtype=F32, tile_m=512, tile_n=D_MODEL, tile_k=768,
                              after=[sent["win", l][-1][4]],
                              fused=("rms_bwd", sv["h"], norm_w[l], dh1), name=f"mm_dhn_{tag}")
    grad_x = dh[None]

    small_shapes = [(1, 128), (depth - 1, D_MODEL), final_norm_w.shape, dn_A_log.shape, dn_dt_bias.shape, dn_norm_w.shape,
                    hg_norm_w.shape, hg_lb_logits.shape, (depth, CONV_W, 3 * BR_WIDTH)]
    nw_part = _pack([d_norm_w[0]])
    nw_push = _push_start(nw_part, own_slot(nw_part), broadcast=True, name="gather_norm_w_start")
    small_all = _push_wait(small_push, [grad_x, nw_push[4]], broadcast=True, name="gather_small_wait")
    tot = _unpack(_sum_parts(small_all, after=[grad_x], name="sum_small"), small_shapes)
    loss = tot[0][0, 0]
    g_lb = tot[7]
    g_logits = jax.vjp(_lower_bounds, hg_lb_logits)[1](g_lb)[0]
    g_conv = lax.dynamic_slice_in_dim(tot[8], my * (3 * BR_WIDTH // N_DEV), 3 * BR_WIDTH // N_DEV, axis=2)
    small_g = [g_conv, tot[3], tot[4], tot[5], g_logits, tot[6], tot[2]]
    small_w = [dn_conv_w, dn_A_log, dn_dt_bias, dn_norm_w, hg_lb_logits, hg_norm_w, final_norm_w]
    small_m = [m_dn_conv_w, m_dn_A_log, m_dn_dt_bias, m_dn_norm_w, m_hg_lb_logits, m_hg_norm_w, m_final_norm_w]
    small_v = [v_dn_conv_w, v_dn_A_log, v_dn_dt_bias, v_dn_norm_w, v_hg_lb_logits, v_hg_norm_w, v_final_norm_w]
    pk_w = _pack(small_w)
    res_small = _adamw(_pack(small_g)[None], 0, pk_w, _pack(small_m), _pack(small_v), name="adamw_small", tr=pk_w.shape[0])
    shapes_w = [a.shape for a in small_w]
    sg, sd, sm, sv_ = (_unpack(r, shapes_w) for r in res_small)

    r_win = r_wo = r_wg = r_wu = None
    done = [grad_x, res_small[0]]

    def flat(a, cols):
        return a.reshape(-1, cols)

    for l in reversed(range(depth)):
        tag = f"l{l}"
        land_rest = _push_wait(sent["rest", l], done, broadcast=False, name=f"exchange_rest_{tag}_wait")
        r_wo = _adamw(land_rest, 0, flat(w_out, D_MODEL), flat(m_w_out, D_MODEL), flat(v_w_out, D_MODEL), layer=l,
                      n_layers=depth, prev=r_wo, name=f"adamw_w_out_{tag}", tr=rows_out)
        r_wg = _adamw(land_rest, g_off, flat(w_ple_gate, D_MODEL), flat(m_w_ple_gate, D_MODEL), flat(v_w_ple_gate, D_MODEL),
                      layer=l, n_layers=depth, prev=r_wg, name=f"adamw_w_gate_{tag}", tr=rows_out)
        r_wu = _adamw(land_rest, u_off, flat(w_ple_up, D_MODEL), flat(m_w_ple_up, D_MODEL), flat(v_w_ple_up, D_MODEL),
                      layer=l, n_layers=depth, prev=r_wu, name=f"adamw_w_up_{tag}", tr=up_rows)
        done = [r_wo[0], r_wg[0], r_wu[0]]
    for l in reversed(range(depth)):
        tag = f"l{l}"
        if l == 0:
            nw0 = _sum_parts(_push_wait(nw_push, done, broadcast=True, name="gather_norm_w_wait"), name="sum_norm_w")
            g_norm_w = jnp.concatenate([_unpack(nw0, [(1, D_MODEL)])[0], tot[1]], axis=0)
            pk_nw = _pack([norm_w])
            r_nw = _adamw(_pack([g_norm_w])[None], 0, pk_nw, _pack([m_norm_w]), _pack([v_norm_w]), name="adamw_norm_w",
                          tr=pk_nw.shape[0])
            r_nw = [_unpack(r, [norm_w.shape])[0] for r in r_nw]
            done = [r_nw[0]]
        n_split = len(sent["win", l])
        for q, handle in enumerate(sent["win", l]):
            sfx = tag if n_split == 1 else f"{tag}_{q}"
            land_in = _push_wait(handle, done, broadcast=False, name=f"exchange_w_in_{sfx}_wait")
            r_win = _adamw(land_in, 0, flat(w_in, SHARD_IN), flat(m_w_in, SHARD_IN), flat(v_w_in, SHARD_IN),
                           layer=l * n_split + q, n_layers=depth * n_split, prev=r_win, name=f"adamw_w_in_{sfx}", tr=256)
            done = [r_win[0]]
    r_win = [o.reshape(w_in.shape) for o in r_win]
    r_wo = [o.reshape(w_out.shape) for o in r_wo]
    r_wg = [o.reshape(w_ple_gate.shape) for o in r_wg]
    r_wu = [o.reshape(w_ple_up.shape) for o in r_wu]

    def order(nw, small_list, big_in, big_out, big_up, big_gate):
        cw, al_, dt_, dnw, lbl, hnw, fw = small_list
        return [nw, big_in, cw, al_, dt_, dnw, lbl, hnw, big_out, big_up, big_gate, fw]

    outs = [loss, grad_x]
    for i, sl in enumerate((sg, sd, sm, sv_)):
        outs += order(r_nw[i], sl, r_win[i], r_wo[i], r_wu[i], r_wg[i])
    return tuple(outs)
```
